```python
import jax, jax.numpy as jnp
from jax import lax
import numpy as np

D_MODEL = 1024
BATCH = 8
SEQ = 4096
DEPTH = 2

QBLOCK = 128
WINDOW = 128
EPS = 1e-6
ROPE_THETA = 10000.0
MLA_HEADS = D_MODEL // 128
MLA_Q_RANK = D_MODEL // 4
MLA_KV_RANK = D_MODEL // 8
MLA_NOPE = 64
MLA_ROPE = 32
MLA_V = 64
SWA_HEADS = D_MODEL // 128
SWA_KV_HEADS = SWA_HEADS // 4
SWA_HEAD_DIM = 64
FOX_HEADS = D_MODEL // 64
FOX_HEAD_DIM = 64
FOX_WIDTH = FOX_HEADS * FOX_HEAD_DIM
FORGET_BIAS_INIT = 2.0

EVEN_WIDTH = MLA_HEADS * MLA_V + SWA_HEADS * SWA_HEAD_DIM
EVEN_SPLITS = (MLA_Q_RANK, MLA_KV_RANK, MLA_ROPE, SWA_HEADS * SWA_HEAD_DIM,
               SWA_KV_HEADS * SWA_HEAD_DIM, SWA_KV_HEADS * SWA_HEAD_DIM, EVEN_WIDTH)
ODD_SPLITS = (FOX_WIDTH, FOX_WIDTH, FOX_WIDTH, FOX_HEADS, FOX_WIDTH)
N_EVEN = (DEPTH + 1) // 2
N_ODD = DEPTH // 2

kernel_name = "hybrid_mla_swa_fox_gated"


def rms_norm(x, g):
    xf = x.astype(jnp.float32)
    y = xf * lax.rsqrt(jnp.mean(xf * xf, axis=-1, keepdims=True) + EPS)
    return (y * g.astype(jnp.float32)).astype(x.dtype)


def split_cols(z, sizes):
    idx = [int(v) for v in np.cumsum(sizes)[:-1]]
    return jnp.split(z, idx, axis=-1)


def rope_angles(positions, dim):
    inv_freq = 1.0 / (ROPE_THETA ** (jnp.arange(0, dim, 2, dtype=jnp.float32) / dim))
    return positions.astype(jnp.float32)[..., None] * inv_freq


def apply_rope(x, ang):
    cos, sin = jnp.cos(ang), jnp.sin(ang)
    xf = x.astype(jnp.float32)
    x1, x2 = jnp.split(xf, 2, axis=-1)
    return jnp.concatenate([x1 * cos - x2 * sin, x2 * cos + x1 * sin], axis=-1).astype(x.dtype)


def alibi_slopes(n):
    return 2.0 ** (-8.0 * (jnp.arange(n, dtype=jnp.float32) + 1.0) / n)


def causal_block_attention(q, k, v, log_cum=None):
    B, S, H, Dk = q.shape
    Dv = v.shape[-1]
    nb = S // QBLOCK
    scale = Dk ** -0.5
    kpos = jnp.arange(S)
    lc_t = None if log_cum is None else jnp.transpose(log_cum, (0, 2, 1))

    def one_block(i):
        start = i * QBLOCK
        qi = lax.dynamic_slice_in_dim(q, start, QBLOCK, axis=1)
        s = jnp.einsum('bqhd,bkhd->bhqk', qi, k, preferred_element_type=jnp.float32) * scale
        if lc_t is not None:
            ci = lax.dynamic_slice_in_dim(lc_t, start, QBLOCK, axis=2)
            s = s + ci[..., :, None] - lc_t[..., None, :]
        qpos = start + jnp.arange(QBLOCK)
        mask = kpos[None, :] <= qpos[:, None]
        s = jnp.where(mask, s, -jnp.inf)
        p = jax.nn.softmax(s, axis=-1)
        return jnp.einsum('bhqk,bkhd->bqhd', p.astype(v.dtype), v)

    out = lax.map(one_block, jnp.arange(nb))
    return jnp.transpose(out, (1, 0, 2, 3, 4)).reshape(B, S, H, Dv)


def sliding_window_sink_attention(q, k, v, sinks, slopes):
    B, S, H, D = q.shape
    KV = k.shape[2]
    G = H // KV
    W = WINDOW
    nb = S // W
    qb = q.reshape(B, nb, W, KV, G, D)
    pad = ((0, 0), (W, 0), (0, 0), (0, 0))
    kp = jnp.pad(k, pad).reshape(B, nb + 1, W, KV, D)
    vp = jnp.pad(v, pad).reshape(B, nb + 1, W, KV, D)
    kb = jnp.concatenate([kp[:, :-1], kp[:, 1:]], axis=2)
    vb = jnp.concatenate([vp[:, :-1], vp[:, 1:]], axis=2)
    s = jnp.einsum('bnqkgd,bnckd->bnkgqc', qb, kb, preferred_element_type=jnp.float32) * (D ** -0.5)
    a = jnp.arange(W)[:, None]
    c = jnp.arange(2 * W)[None, :]
    dist = (W + a - c).astype(jnp.float32)
    blk = jnp.arange(nb)[:, None, None]
    valid = (dist >= 0) & (dist < W) & (blk * W + c[None] - W >= 0)
    s = s - slopes.reshape(KV, G)[:, :, None, None] * dist
    s = jnp.where(valid[None, :, None, None], s, -jnp.inf)
    sink = sinks.astype(jnp.float32).reshape(KV, G)[:, :, None, None]
    m = jnp.maximum(jnp.max(s, axis=-1, keepdims=True), sink)
    p = jnp.exp(s - m)
    p = p / (jnp.sum(p, axis=-1, keepdims=True) + jnp.exp(sink - m))
    o = jnp.einsum('bnkgqc,bnckd->bnqkgd', p.astype(v.dtype), vb)
    return o.reshape(B, S, H, D)


def mla_swa_layer(x, positions, g_in, w_in, g_q_a, w_q_up, g_kv_a, w_kv_up, sinks, w_out):
    B, S, _ = x.shape
    h = rms_norm(x, g_in)
    z = h @ w_in
    cq, ckv, kpe, q_s, k_s, v_s, gate = split_cols(z, EVEN_SPLITS)
    q = (rms_norm(cq, g_q_a) @ w_q_up).reshape(B, S, MLA_HEADS, MLA_NOPE + MLA_ROPE)
    q_nope, q_pe = q[..., :MLA_NOPE], q[..., MLA_NOPE:]
    kv = (rms_norm(ckv, g_kv_a) @ w_kv_up).reshape(B, S, MLA_HEADS, MLA_NOPE + MLA_V)
    k_nope, v_m = kv[..., :MLA_NOPE], kv[..., MLA_NOPE:]
    ang = rope_angles(positions, MLA_ROPE)
    q_pe = apply_rope(q_pe, ang[:, :, None, :])
    k_pe = apply_rope(kpe, ang)[:, :, None, :]
    qm = jnp.concatenate([q_nope, q_pe], axis=-1)
    km = jnp.concatenate([k_nope, jnp.broadcast_to(k_pe, (B, S, MLA_HEADS, MLA_ROPE))], axis=-1)
    o_mla = causal_block_attention(qm, km, v_m)
    o_swa = sliding_window_sink_attention(
        q_s.reshape(B, S, SWA_HEADS, SWA_HEAD_DIM),
        k_s.reshape(B, S, SWA_KV_HEADS, SWA_HEAD_DIM),
        v_s.reshape(B, S, SWA_KV_HEADS, SWA_HEAD_DIM),
        sinks, alibi_slopes(SWA_HEADS))
    o = jnp.concatenate([o_mla.reshape(B, S, -1), o_swa.reshape(B, S, -1)], axis=-1)
    return x + (o * jax.nn.silu(gate)) @ w_out


def fox_layer(x, g_in, w_in, b_f, w_out):
    B, S, _ = x.shape
    h = rms_norm(x, g_in)
    z = h @ w_in
    q, k, v, f_logit, gate = split_cols(z, ODD_SPLITS)
    log_f = jax.nn.log_sigmoid(f_logit.astype(jnp.float32) + b_f.astype(jnp.float32))
    log_cum = jnp.cumsum(log_f, axis=1)
    o = causal_block_attention(q.reshape(B, S, FOX_HEADS, FOX_HEAD_DIM),
                               k.reshape(B, S, FOX_HEADS, FOX_HEAD_DIM),
                               v.reshape(B, S, FOX_HEADS, FOX_HEAD_DIM), log_cum=log_cum)
    return x + (o.reshape(B, S, FOX_WIDTH) * jax.nn.silu(gate)) @ w_out


def _fwd_setup_inputs(seed: int = 0) -> dict:
    key = jax.random.key(seed)
    ks = jax.random.split(key, 16)
    f32 = jnp.float32

    def w(k, shape, fan_in):
        return jax.random.normal(k, shape, f32) * (fan_in ** -0.5)

    def gain(k, shape):
        return 1.0 + 0.05 * jax.random.normal(k, shape, f32)

    x = jax.random.normal(ks[0], (BATCH, SEQ, D_MODEL), f32)
    positions = jnp.broadcast_to(jnp.arange(SEQ, dtype=jnp.int32), (BATCH, SEQ))
    return {
        "x": x,
        "positions": positions,
        "e_g_in": gain(ks[1], (N_EVEN, D_MODEL)),
        "e_w_in": w(ks[2], (N_EVEN, D_MODEL, sum(EVEN_SPLITS)), D_MODEL),
        "e_g_q_a": gain(ks[3], (N_EVEN, MLA_Q_RANK)),
        "e_w_q_up": w(ks[4], (N_EVEN, MLA_Q_RANK, MLA_HEADS * (MLA_NOPE + MLA_ROPE)), MLA_Q_RANK),
        "e_g_kv_a": gain(ks[5], (N_EVEN, MLA_KV_RANK)),
        "e_w_kv_up": w(ks[6], (N_EVEN, MLA_KV_RANK, MLA_HEADS * (MLA_NOPE + MLA_V)), MLA_KV_RANK),
        "e_sinks": jax.random.normal(ks[7], (N_EVEN, SWA_HEADS), f32),
        "e_w_out": w(ks[8], (N_EVEN, EVEN_WIDTH, D_MODEL), EVEN_WIDTH),
        "o_g_in": gain(ks[9], (N_ODD, D_MODEL)),
        "o_w_in": w(ks[10], (N_ODD, D_MODEL, sum(ODD_SPLITS)), D_MODEL),
        "o_b_f": FORGET_BIAS_INIT + 0.5 * jax.random.normal(ks[11], (N_ODD, FOX_HEADS), f32),
        "o_w_out": w(ks[12], (N_ODD, FOX_WIDTH, D_MODEL), FOX_WIDTH),
        "g_final": gain(ks[13], (D_MODEL,)),
    }


def _fwd_reference(x, positions, e_g_in, e_w_in, e_g_q_a, e_w_q_up, e_g_kv_a, e_w_kv_up, e_sinks,
              e_w_out, o_g_in, o_w_in, o_b_f, o_w_out, g_final):
    for layer in range(DEPTH):
        j = layer // 2
        if layer % 2 == 0:
            x = mla_swa_layer(x, positions, e_g_in[j], e_w_in[j], e_g_q_a[j], e_w_q_up[j],
                              e_g_kv_a[j], e_w_kv_up[j], e_sinks[j], e_w_out[j])
        else:
            x = fox_layer(x, o_g_in[j], o_w_in[j], o_b_f[j], o_w_out[j])
    return rms_norm(x, g_final)


import jax as _jax
import jax.numpy as _jnp

TWIN_FORMAT = 'train_step'
FWD_PARAMS = ['x', 'positions', 'e_g_in', 'e_w_in', 'e_g_q_a', 'e_w_q_up', 'e_g_kv_a', 'e_w_kv_up', 'e_sinks', 'e_w_out', 'o_g_in', 'o_w_in', 'o_b_f', 'o_w_out', 'g_final']
TWIN_WEIGHTS = ['e_g_in', 'e_w_in', 'e_g_q_a', 'e_w_q_up', 'e_g_kv_a', 'e_w_kv_up', 'e_sinks', 'e_w_out', 'o_g_in', 'o_w_in', 'o_b_f', 'o_w_out', 'g_final']
TWIN_DIFF_INPUT = 'x'
TWIN_INPUTS = ['x', 'positions', 'e_g_in', 'e_w_in', 'e_g_q_a', 'e_w_q_up', 'e_g_kv_a', 'e_w_kv_up', 'e_sinks', 'e_w_out', 'o_g_in', 'o_w_in', 'o_b_f', 'o_w_out', 'g_final', 'loss_target', 'm_e_g_in', 'm_e_w_in', 'm_e_g_q_a', 'm_e_w_q_up', 'm_e_g_kv_a', 'm_e_w_kv_up', 'm_e_sinks', 'm_e_w_out', 'm_o_g_in', 'm_o_w_in', 'm_o_b_f', 'm_o_w_out', 'm_g_final', 'v_e_g_in', 'v_e_w_in', 'v_e_g_q_a', 'v_e_w_q_up', 'v_e_g_kv_a', 'v_e_w_kv_up', 'v_e_sinks', 'v_e_w_out', 'v_o_g_in', 'v_o_w_in', 'v_o_b_f', 'v_o_w_out', 'v_g_final']
TWIN_OUTPUTS = ['loss', 'grad_x', 'grad_e_g_in', 'grad_e_w_in', 'grad_e_g_q_a', 'grad_e_w_q_up', 'grad_e_g_kv_a', 'grad_e_w_kv_up', 'grad_e_sinks', 'grad_e_w_out', 'grad_o_g_in', 'grad_o_w_in', 'grad_o_b_f', 'grad_o_w_out', 'grad_g_final', 'delta_e_g_in', 'delta_e_w_in', 'delta_e_g_q_a', 'delta_e_w_q_up', 'delta_e_g_kv_a', 'delta_e_w_kv_up', 'delta_e_sinks', 'delta_e_w_out', 'delta_o_g_in', 'delta_o_w_in', 'delta_o_b_f', 'delta_o_w_out', 'delta_g_final', 'new_m_e_g_in', 'new_m_e_w_in', 'new_m_e_g_q_a', 'new_m_e_w_q_up', 'new_m_e_g_kv_a', 'new_m_e_w_kv_up', 'new_m_e_sinks', 'new_m_e_w_out', 'new_m_o_g_in', 'new_m_o_w_in', 'new_m_o_b_f', 'new_m_o_w_out', 'new_m_g_final', 'new_v_e_g_in', 'new_v_e_w_in', 'new_v_e_g_q_a', 'new_v_e_w_q_up', 'new_v_e_g_kv_a', 'new_v_e_w_kv_up', 'new_v_e_sinks', 'new_v_e_w_out', 'new_v_o_g_in', 'new_v_o_w_in', 'new_v_o_b_f', 'new_v_o_w_out', 'new_v_g_final']
TWIN_LEAF_KINDS = {'loss': 'loss', 'grad_x': 'grad_x', 'grad_e_g_in': 'grad_w', 'grad_e_w_in': 'grad_w', 'grad_e_g_q_a': 'grad_w', 'grad_e_w_q_up': 'grad_w', 'grad_e_g_kv_a': 'grad_w', 'grad_e_w_kv_up': 'grad_w', 'grad_e_sinks': 'grad_w', 'grad_e_w_out': 'grad_w', 'grad_o_g_in': 'grad_w', 'grad_o_w_in': 'grad_w', 'grad_o_b_f': 'grad_w', 'grad_o_w_out': 'grad_w', 'grad_g_final': 'grad_w', 'delta_e_g_in': 'delta_w', 'delta_e_w_in': 'delta_w', 'delta_e_g_q_a': 'delta_w', 'delta_e_w_q_up': 'delta_w', 'delta_e_g_kv_a': 'delta_w', 'delta_e_w_kv_up': 'delta_w', 'delta_e_sinks': 'delta_w', 'delta_e_w_out': 'delta_w', 'delta_o_g_in': 'delta_w', 'delta_o_w_in': 'delta_w', 'delta_o_b_f': 'delta_w', 'delta_o_w_out': 'delta_w', 'delta_g_final': 'delta_w', 'new_m_e_g_in': 'new_m', 'new_m_e_w_in': 'new_m', 'new_m_e_g_q_a': 'new_m', 'new_m_e_w_q_up': 'new_m', 'new_m_e_g_kv_a': 'new_m', 'new_m_e_w_kv_up': 'new_m', 'new_m_e_sinks': 'new_m', 'new_m_e_w_out': 'new_m', 'new_m_o_g_in': 'new_m', 'new_m_o_w_in': 'new_m', 'new_m_o_b_f': 'new_m', 'new_m_o_w_out': 'new_m', 'new_m_g_final': 'new_m', 'new_v_e_g_in': 'new_v', 'new_v_e_w_in': 'new_v', 'new_v_e_g_q_a': 'new_v', 'new_v_e_w_q_up': 'new_v', 'new_v_e_g_kv_a': 'new_v', 'new_v_e_w_kv_up': 'new_v', 'new_v_e_sinks': 'new_v', 'new_v_e_w_out': 'new_v', 'new_v_o_g_in': 'new_v', 'new_v_o_w_in': 'new_v', 'new_v_o_b_f': 'new_v', 'new_v_o_w_out': 'new_v', 'new_v_g_final': 'new_v'}


def _forward(args):
    return _fwd_reference(*[args[k] for k in FWD_PARAMS])


def _output_shape():
    out = _jax.eval_shape(lambda: _forward(_fwd_setup_inputs(0)))
    return out.shape, out.dtype

N_MICROBATCH = 1
ADAM_LR = 0.001
ADAM_B1 = 0.9
ADAM_B2 = 0.999
ADAM_EPS = 1e-08
ADAM_WD = 0.01
ADAM_STEP = 10
PER_EXAMPLE_BATCH_AXIS = {'x': 0, 'positions': 0, 'loss_target': 0}
SHARED_INPUTS = []
_WEIGHT_DTYPES = {'e_g_in': _jnp.float32, 'e_w_in': _jnp.float32, 'e_g_q_a': _jnp.float32, 'e_w_q_up': _jnp.float32, 'e_g_kv_a': _jnp.float32, 'e_w_kv_up': _jnp.float32, 'e_sinks': _jnp.float32, 'e_w_out': _jnp.float32, 'o_g_in': _jnp.float32, 'o_w_in': _jnp.float32, 'o_b_f': _jnp.float32, 'o_w_out': _jnp.float32, 'g_final': _jnp.float32}
MOMENT_SCALE = {'e_g_in': 5.666114e-02, 'e_w_in': 3.766790e-02, 'e_g_q_a': 3.059433e-02, 'e_w_q_up': 1.707080e-02, 'e_g_kv_a': 6.573533e-02, 'e_w_kv_up': 2.167790e-02, 'e_sinks': 7.064196e-02, 'e_w_out': 3.083066e-02, 'o_g_in': 7.571011e-02, 'o_w_in': 3.812472e-02, 'o_b_f': 3.221238e-01, 'o_w_out': 4.247714e-02, 'g_final': 3.202406e+01}


def _to_microbatches(a, axis):
    t = _jnp.moveaxis(a, axis, 0)
    t = t.reshape((N_MICROBATCH, t.shape[0] // N_MICROBATCH) + t.shape[1:])
    return _jnp.moveaxis(t, 1, axis + 1)


def setup_inputs(seed: int = 0) -> dict:
    inp = _fwd_setup_inputs(seed)
    key = _jax.random.fold_in(_jax.random.key(seed), 7919)
    shape, _ = _output_shape()
    out = dict(inp)
    out["loss_target"] = _jax.random.normal(_jax.random.fold_in(key, 0), shape, _jnp.float32)
    for i, name in enumerate(TWIN_WEIGHTS):
        w = inp[name].astype(_jnp.float32)
        if MOMENT_SCALE is None:
            s = _jnp.sqrt(_jnp.mean(_jnp.square(w)) + 1e-30)
        else:
            s = MOMENT_SCALE[name]
        km, kv = _jax.random.split(_jax.random.fold_in(key, i + 1))
        out[name] = w
        out["m_" + name] = s * _jax.random.normal(km, w.shape, _jnp.float32)
        out["v_" + name] = (s * s) * _jax.random.uniform(kv, w.shape, _jnp.float32, 0.5, 1.5)
    if N_MICROBATCH > 1:
        for name, axis in PER_EXAMPLE_BATCH_AXIS.items():
            out[name] = _to_microbatches(out[name], axis)
    return {'x': out['x'], 'positions': out['positions'], 'e_g_in': out['e_g_in'], 'e_w_in': out['e_w_in'], 'e_g_q_a': out['e_g_q_a'], 'e_w_q_up': out['e_w_q_up'], 'e_g_kv_a': out['e_g_kv_a'], 'e_w_kv_up': out['e_w_kv_up'], 'e_sinks': out['e_sinks'], 'e_w_out': out['e_w_out'], 'o_g_in': out['o_g_in'], 'o_w_in': out['o_w_in'], 'o_b_f': out['o_b_f'], 'o_w_out': out['o_w_out'], 'g_final': out['g_final'], 'loss_target': out['loss_target'], 'm_e_g_in': out['m_e_g_in'], 'm_e_w_in': out['m_e_w_in'], 'm_e_g_q_a': out['m_e_g_q_a'], 'm_e_w_q_up': out['m_e_w_q_up'], 'm_e_g_kv_a': out['m_e_g_kv_a'], 'm_e_w_kv_up': out['m_e_w_kv_up'], 'm_e_sinks': out['m_e_sinks'], 'm_e_w_out': out['m_e_w_out'], 'm_o_g_in': out['m_o_g_in'], 'm_o_w_in': out['m_o_w_in'], 'm_o_b_f': out['m_o_b_f'], 'm_o_w_out': out['m_o_w_out'], 'm_g_final': out['m_g_final'], 'v_e_g_in': out['v_e_g_in'], 'v_e_w_in': out['v_e_w_in'], 'v_e_g_q_a': out['v_e_g_q_a'], 'v_e_w_q_up': out['v_e_w_q_up'], 'v_e_g_kv_a': out['v_e_g_kv_a'], 'v_e_w_kv_up': out['v_e_w_kv_up'], 'v_e_sinks': out['v_e_sinks'], 'v_e_w_out': out['v_e_w_out'], 'v_o_g_in': out['v_o_g_in'], 'v_o_w_in': out['v_o_w_in'], 'v_o_b_f': out['v_o_b_f'], 'v_o_w_out': out['v_o_w_out'], 'v_g_final': out['v_g_final']}


def _loss(weights, diff, rest, loss_target):
    with _jax.named_scope("forward"):
        args = {**rest, TWIN_DIFF_INPUT: diff, **{k: w.astype(_WEIGHT_DTYPES[k]) for k, w in weights.items()}}
        y = _forward(args)
    with _jax.named_scope("loss_head"):
        err = _jnp.square(y.astype(_jnp.float32) - loss_target)
        return 0.5 * _jnp.sum(_jnp.mean(err, axis=-1)) if err.ndim else 0.5 * err


def _adamw(w, g, m, v):
    m = ADAM_B1 * m + (1.0 - ADAM_B1) * g
    v = ADAM_B2 * v + (1.0 - ADAM_B2) * _jnp.square(g)
    m_hat = m / (1.0 - ADAM_B1 ** ADAM_STEP)
    v_hat = v / (1.0 - ADAM_B2 ** ADAM_STEP)
    delta = -ADAM_LR * (m_hat / (_jnp.sqrt(v_hat) + ADAM_EPS) + ADAM_WD * w)
    return delta, m, v


def reference(x, positions, e_g_in, e_w_in, e_g_q_a, e_w_q_up, e_g_kv_a, e_w_kv_up, e_sinks, e_w_out, o_g_in, o_w_in, o_b_f, o_w_out, g_final, loss_target, m_e_g_in, m_e_w_in, m_e_g_q_a, m_e_w_q_up, m_e_g_kv_a, m_e_w_kv_up, m_e_sinks, m_e_w_out, m_o_g_in, m_o_w_in, m_o_b_f, m_o_w_out, m_g_final, v_e_g_in, v_e_w_in, v_e_g_q_a, v_e_w_q_up, v_e_g_kv_a, v_e_w_kv_up, v_e_sinks, v_e_w_out, v_o_g_in, v_o_w_in, v_o_b_f, v_o_w_out, v_g_final):
    given = dict(x=x, positions=positions, e_g_in=e_g_in, e_w_in=e_w_in, e_g_q_a=e_g_q_a, e_w_q_up=e_w_q_up, e_g_kv_a=e_g_kv_a, e_w_kv_up=e_w_kv_up, e_sinks=e_sinks, e_w_out=e_w_out, o_g_in=o_g_in, o_w_in=o_w_in, o_b_f=o_b_f, o_w_out=o_w_out, g_final=g_final, loss_target=loss_target, m_e_g_in=m_e_g_in, m_e_w_in=m_e_w_in, m_e_g_q_a=m_e_g_q_a, m_e_w_q_up=m_e_w_q_up, m_e_g_kv_a=m_e_g_kv_a, m_e_w_kv_up=m_e_w_kv_up, m_e_sinks=m_e_sinks, m_e_w_out=m_e_w_out, m_o_g_in=m_o_g_in, m_o_w_in=m_o_w_in, m_o_b_f=m_o_b_f, m_o_w_out=m_o_w_out, m_g_final=m_g_final, v_e_g_in=v_e_g_in, v_e_w_in=v_e_w_in, v_e_g_q_a=v_e_g_q_a, v_e_w_q_up=v_e_w_q_up, v_e_g_kv_a=v_e_g_kv_a, v_e_w_kv_up=v_e_w_kv_up, v_e_sinks=v_e_sinks, v_e_w_out=v_e_w_out, v_o_g_in=v_o_g_in, v_o_w_in=v_o_w_in, v_o_b_f=v_o_b_f, v_o_w_out=v_o_w_out, v_g_final=v_g_final)
    weights = {n: given[n] for n in TWIN_WEIGHTS}
    shared = {n: given[n] for n in SHARED_INPUTS}
    per_example = {n: given[n] for n in ['x', 'positions']}
    grad_fn = _jax.value_and_grad(_loss, argnums=(0, 1))

    def one_microbatch(ex, loss_target):
        ex = dict(ex)
        diff = ex.pop(TWIN_DIFF_INPUT)
        return grad_fn(weights, diff, {**shared, **ex}, loss_target)

    if N_MICROBATCH == 1:
        loss, (grad_w, grad_x) = one_microbatch(per_example, given["loss_target"])
    else:
        def body(carry, xs):
            loss_sum, grad_sum = carry
            l_k, (gw_k, gx_k) = one_microbatch(xs[0], xs[1])
            with _jax.named_scope("update"):
                return (loss_sum + l_k, _jax.tree.map(_jnp.add, grad_sum, gw_k)), gx_k

        init = (_jnp.zeros((), _jnp.float32), _jax.tree.map(_jnp.zeros_like, weights))
        (loss, grad_w), grad_x = _jax.lax.scan(body, init, (per_example, given["loss_target"]))
    with _jax.named_scope("update"):
        delta_w, new_m, new_v = {}, {}, {}
        for n in TWIN_WEIGHTS:
            delta_w[n], new_m[n], new_v[n] = _adamw(weights[n], grad_w[n], given["m_" + n], given["v_" + n])
    return (loss, grad_x, *[grad_w[n] for n in TWIN_WEIGHTS], *[delta_w[n] for n in TWIN_WEIGHTS],
            *[new_m[n] for n in TWIN_WEIGHTS], *[new_v[n] for n in TWIN_WEIGHTS])
```

```python
import functools

import jax
import jax.numpy as jnp
from jax import lax
from jax.experimental import pallas as pl
from jax.experimental.pallas import tpu as pltpu

F32 = jnp.float32
BF16 = jnp.bfloat16
NEG_INF = float("-inf")

N_DEV = 8
LANES = 128
D_MODEL = 1024
EPS = 1e-6
ROPE_THETA = 10000.0
MLA_HEADS = 8
MLA_Q_RANK = 256
MLA_KV_RANK = 128
MLA_NOPE = 64
MLA_ROPE = 32
MLA_V = 64
SWA_HEADS = 8
SWA_KV_HEADS = 2
SWA_DIM = 64
WINDOW = 128
FOX_HEADS = 16
FOX_DIM = 64

ADAM_LR = 0.001
ADAM_B1 = 0.9
ADAM_B2 = 0.999
ADAM_EPS = 1e-08
ADAM_WD = 0.01
ADAM_STEP = 10

ATT_T = 256
VMEM_LIMIT = 56 * 1024 * 1024

Z0A_UNITS = 12
Z0B_UNITS = 6

R_E_W_IN = 1024 * 276 // LANES
R_E_W_Q = 256 * 96 // LANES
R_E_W_KV = 128 * 128 // LANES
R_E_W_OUT = 128 * 1024 // LANES
R_O_W_IN = 1024 * 514 // LANES
R_O_W_OUT = 128 * 1024 // LANES
R_O_G_IN = 16
OFF_E_W_IN = 0
OFF_E_W_Q = OFF_E_W_IN + R_E_W_IN
OFF_E_W_KV = OFF_E_W_Q + R_E_W_Q
OFF_E_W_OUT = OFF_E_W_KV + R_E_W_KV
OFF_O_W_IN = OFF_E_W_OUT + R_E_W_OUT
OFF_O_W_OUT = OFF_O_W_IN + R_O_W_IN
OFF_O_G_IN = OFF_O_W_OUT + R_O_W_OUT
R_FLAT = OFF_O_G_IN + R_O_G_IN
SMALL_ROWS = 24


def _tile(n, cands):
    for c in cands:
        if n % c == 0:
            return c
    raise ValueError(f"no tile for {n}")


def _params(sem, vmem=None):
    return pltpu.CompilerParams(dimension_semantics=sem, vmem_limit_bytes=vmem)


def _matmul(a, b, *, name, ta=False, tb=False, add=None, out_dtype=F32):
    if ta:
        kdim, m = a.shape
    else:
        m, kdim = a.shape
    if tb:
        n, kb = b.shape
    else:
        kb, n = b.shape
    assert kdim == kb, (a.shape, b.shape)
    tm = _tile(m, (512, 256, 128))
    tn = _tile(n, (768, 512, 384, 256, 128))
    tk = _tile(kdim, (1024, 512, 256, 128))
    nk = kdim // tk
    dims = (((0 if ta else 1,), (1 if tb else 0,)), ((), ()))

    def body(*refs):
        if add is None:
            a_ref, b_ref, o_ref, acc_ref = refs
            add_ref = None
        else:
            a_ref, b_ref, add_ref, o_ref, acc_ref = refs
        k = pl.program_id(2)

        @pl.when(k == 0)
        def _():
            acc_ref[...] = jnp.zeros_like(acc_ref)

        acc_ref[...] += lax.dot_general(a_ref[...].astype(BF16), b_ref[...].astype(BF16), dims,
                                        preferred_element_type=F32)

        @pl.when(k == nk - 1)
        def _():
            r = acc_ref[...]
            if add_ref is not None:
                r = r + add_ref[...]
            o_ref[...] = r.astype(out_dtype)

    a_spec = pl.BlockSpec((tk, tm), lambda i, j, k: (k, i)) if ta else pl.BlockSpec((tm, tk), lambda i, j, k: (i, k))
    b_spec = pl.BlockSpec((tn, tk), lambda i, j, k: (j, k)) if tb else pl.BlockSpec((tk, tn), lambda i, j, k: (k, j))
    in_specs = [a_spec, b_spec]
    args = [a, b]
    if add is not None:
        in_specs.append(pl.BlockSpec((tm, tn), lambda i, j, k: (i, j)))
        args.append(add)
    return pl.pallas_call(
        body, name=name, grid=(m // tm, n // tn, nk),
        in_specs=in_specs, out_specs=pl.BlockSpec((tm, tn), lambda i, j, k: (i, j)),
        out_shape=jax.ShapeDtypeStruct((m, n), out_dtype),
        scratch_shapes=[pltpu.VMEM((tm, tn), F32)],
        compiler_params=_params(("parallel", "parallel", "arbitrary")),
    )(*args)


def _rmsnorm_fwd(x, g, *, width, col_blk, name):
    s = x.shape[0]
    tm = _tile(s, (256, 128))

    def body(x_ref, g_ref, y_ref):
        xf = x_ref[...].astype(F32)
        r = lax.rsqrt(jnp.mean(xf * xf, axis=-1, keepdims=True) + EPS)
        y_ref[...] = ((xf * r) * g_ref[...]).astype(BF16)

    return pl.pallas_call(
        body, name=name, grid=(s // tm,),
        in_specs=[pl.BlockSpec((tm, width), lambda i: (i, col_blk)), pl.BlockSpec((1, width), lambda i: (0, 0))],
        out_specs=pl.BlockSpec((tm, width), lambda i: (i, 0)),
        out_shape=jax.ShapeDtypeStruct((s, width), BF16),
        compiler_params=_params(("parallel",)),
    )(x, g)


def _rmsnorm_bwd(x, g, dy, *, width, col_blk, name, add=None, out_dtype=F32):
    s = x.shape[0]
    tm = _tile(s, (256, 128))

    def body(*refs):
        if add is None:
            x_ref, g_ref, dy_ref, dx_ref, dg_ref = refs
            add_ref = None
        else:
            x_ref, g_ref, dy_ref, add_ref, dx_ref, dg_ref = refs
        i = pl.program_id(0)
        xf = x_ref[...].astype(F32)
        r = lax.rsqrt(jnp.mean(xf * xf, axis=-1, keepdims=True) + EPS)
        xh = xf * r
        dyf = dy_ref[...].astype(F32)

        @pl.when(i == 0)
        def _():
            dg_ref[...] = jnp.zeros_like(dg_ref)

        dg_ref[...] += jnp.sum(dyf * xh, axis=0, keepdims=True)
        dxh = dyf * g_ref[...]
        dx = r * (dxh - xh * jnp.mean(dxh * xh, axis=-1, keepdims=True))
        if add_ref is not None:
            dx = dx + add_ref[...]
        dx_ref[...] = dx.astype(out_dtype)

    in_specs = [pl.BlockSpec((tm, width), lambda i: (i, col_blk)), pl.BlockSpec((1, width), lambda i: (0, 0)),
                pl.BlockSpec((tm, width), lambda i: (i, 0))]
    args = [x, g, dy]
    if add is not None:
        in_specs.append(pl.BlockSpec((tm, width), lambda i: (i, 0)))
        args.append(add)
    return pl.pallas_call(
        body, name=name, grid=(s // tm,),
        in_specs=in_specs,
        out_specs=[pl.BlockSpec((tm, width), lambda i: (i, 0)), pl.BlockSpec((1, width), lambda i: (0, 0))],
        out_shape=[jax.ShapeDtypeStruct((s, width), out_dtype), jax.ShapeDtypeStruct((1, width), F32)],
        compiler_params=_params(("arbitrary",)),
    )(*args)


def _sigmoid(x):
    return 1.0 / (1.0 + jnp.exp(-x))


def _gate_fwd(o_parts, gate, *, name):
    s = gate.shape[0]
    tm = _tile(s, (256, 128))
    n_o = len(o_parts)

    def body(*refs):
        o_refs, g_ref, y_ref = refs[:n_o], refs[n_o], refs[n_o + 1]
        o = o_refs[0][...] if n_o == 1 else jnp.concatenate([r[...] for r in o_refs], axis=1)
        gt = g_ref[...]
        y_ref[...] = (o * (gt * _sigmoid(gt))).astype(BF16)

    in_specs = [pl.BlockSpec((tm, o.shape[1]), lambda i: (i, 0)) for o in o_parts]
    in_specs.append(pl.BlockSpec((tm, D_MODEL), lambda i: (i, 0)))
    return pl.pallas_call(
        body, name=name, grid=(s // tm,), in_specs=in_specs,
        out_specs=pl.BlockSpec((tm, D_MODEL), lambda i: (i, 0)),
        out_shape=jax.ShapeDtypeStruct((s, D_MODEL), BF16),
        compiler_params=_params(("parallel",)),
    )(*o_parts, gate)


def _gate_bwd(d_og, o_parts, gate, *, name):
    s = gate.shape[0]
    tm = _tile(s, (256, 128))
    n_o = len(o_parts)
    widths = [o.shape[1] for o in o_parts]

    def body(*refs):
        d_ref, o_refs, g_ref = refs[0], refs[1:1 + n_o], refs[1 + n_o]
        do_refs, dg_ref = refs[2 + n_o:2 + 2 * n_o], refs[2 + 2 * n_o]
        d = d_ref[...]
        gt = g_ref[...]
        sg = _sigmoid(gt)
        silu = gt * sg
        dsilu = sg * (1.0 + gt * (1.0 - sg))
        o = o_refs[0][...] if n_o == 1 else jnp.concatenate([r[...] for r in o_refs], axis=1)
        dg_ref[...] = (d * o * dsilu).astype(BF16)
        do = d * silu
        off = 0
        for r, w in zip(do_refs, widths):
            r[...] = do[:, off:off + w]
            off += w

    in_specs = [pl.BlockSpec((tm, D_MODEL), lambda i: (i, 0))]
    in_specs += [pl.BlockSpec((tm, w), lambda i: (i, 0)) for w in widths]
    in_specs.append(pl.BlockSpec((tm, D_MODEL), lambda i: (i, 0)))
    out_specs = [pl.BlockSpec((tm, w), lambda i: (i, 0)) for w in widths]
    out_specs.append(pl.BlockSpec((tm, D_MODEL), lambda i: (i, 0)))
    out_shape = [jax.ShapeDtypeStruct((s, w), F32) for w in widths]
    out_shape.append(jax.ShapeDtypeStruct((s, D_MODEL), BF16))
    return pl.pallas_call(
        body, name=name, grid=(s // tm,), in_specs=in_specs, out_specs=out_specs, out_shape=out_shape,
        compiler_params=_params(("parallel",)),
    )(d_og, *o_parts, gate)


def _rot_half(x):
    lane = lax.broadcasted_iota(jnp.int32, x.shape, 1)
    return jnp.where(lane < 80, pltpu.roll(x, LANES - 16, axis=1), pltpu.roll(x, 16, axis=1))


def _rot_half_t(g):
    lane = lax.broadcasted_iota(jnp.int32, g.shape, 1)
    lo = (lane >= MLA_NOPE) & (lane < MLA_NOPE + MLA_ROPE // 2)
    hi = (lane >= MLA_NOPE + MLA_ROPE // 2) & (lane < MLA_NOPE + MLA_ROPE)
    return jnp.where(lo, pltpu.roll(g, LANES - 16, axis=1), jnp.where(hi, pltpu.roll(g, 16, axis=1), 0.0))


def _rope_fwd(qp, kvp, z0a, cos_t, sin_t, *, name):
    s = qp.shape[0]
    tm = _tile(s, (256, 128))
    hw = MLA_HEADS * LANES

    def body(q_ref, k_ref, kpe_ref, c_ref, s_ref, qm_ref, km_ref):
        c = c_ref[...]
        sn = s_ref[...]
        kpe = kpe_ref[...]
        kpe_r = (kpe * c + _rot_half(kpe) * sn).astype(BF16)
        lane = lax.broadcasted_iota(jnp.int32, kpe.shape, 1)
        for h in range(MLA_HEADS):
            sl = slice(h * LANES, (h + 1) * LANES)
            qh = q_ref[:, sl]
            qm_ref[:, sl] = (qh * c + _rot_half(qh) * sn).astype(BF16)
            km_ref[:, sl] = jnp.where(lane < MLA_NOPE, k_ref[:, sl], kpe_r)

    return pl.pallas_call(
        body, name=name, grid=(s // tm,),
        in_specs=[pl.BlockSpec((tm, hw), lambda i: (i, 0)), pl.BlockSpec((tm, hw), lambda i: (i, 0)),
                  pl.BlockSpec((tm, LANES), lambda i: (i, 11)),
                  pl.BlockSpec((tm, LANES), lambda i: (i, 0)), pl.BlockSpec((tm, LANES), lambda i: (i, 0))],
        out_specs=[pl.BlockSpec((tm, hw), lambda i: (i, 0)), pl.BlockSpec((tm, hw), lambda i: (i, 0))],
        out_shape=[jax.ShapeDtypeStruct((s, hw), BF16), jax.ShapeDtypeStruct((s, hw), BF16)],
        compiler_params=_params(("parallel",)),
    )(qp, kvp, z0a, cos_t, sin_t)


def _rope_bwd(dqm, dkm, cos_t, sin_t, *, name):
    s = dqm.shape[0]
    tm = _tile(s, (256, 128))
    hw = MLA_HEADS * LANES

    def body(dq_ref, dk_ref, c_ref, s_ref, dqp_ref, dkpe_ref):
        c = c_ref[...]
        sn = s_ref[...]
        ksum = jnp.zeros((tm, LANES), F32)
        for h in range(MLA_HEADS):
            sl = slice(h * LANES, (h + 1) * LANES)
            dq = dq_ref[:, sl]
            dqp_ref[:, sl] = (dq * c + _rot_half_t(dq * sn)).astype(BF16)
            ksum = ksum + dk_ref[:, sl]
        lane = lax.broadcasted_iota(jnp.int32, ksum.shape, 1)
        dkpe = ksum * c + _rot_half_t(ksum * sn)
        dkpe_ref[...] = jnp.where((lane >= MLA_NOPE) & (lane < MLA_NOPE + MLA_ROPE), dkpe, 0.0).astype(BF16)

    return pl.pallas_call(
        body, name=name, grid=(s // tm,),
        in_specs=[pl.BlockSpec((tm, hw), lambda i: (i, 0)), pl.BlockSpec((tm, hw), lambda i: (i, 0)),
                  pl.BlockSpec((tm, LANES), lambda i: (i, 0)), pl.BlockSpec((tm, LANES), lambda i: (i, 0))],
        out_specs=[pl.BlockSpec((tm, hw), lambda i: (i, 0)), pl.BlockSpec((tm, LANES), lambda i: (i, 0))],
        out_shape=[jax.ShapeDtypeStruct((s, hw), BF16), jax.ShapeDtypeStruct((s, LANES), BF16)],
        compiler_params=_params(("parallel",)),
    )(dqm, dkm, cos_t, sin_t)


def _head_mask(shape, a):
    lane = lax.broadcasted_iota(jnp.int32, shape, 1)
    return (lane >= 64 * a) & (lane < 64 * (a + 1))


def _causal_mask(t):
    row = lax.broadcasted_iota(jnp.int32, (t, t), 0)
    col = lax.broadcasted_iota(jnp.int32, (t, t), 1)
    return col <= row


_NT = (((1,), (1,)), ((), ()))


def _flash_fwd(q, k, v, bias, *, n_pairs, hw, q_off, k_off, v_off, scale, name):
    s = q.shape[0]
    t = ATT_T
    nb = s // t
    qw = 2 * hw
    has_bias = bias is not None

    def body(*refs):
        if has_bias:
            q_ref, k_ref, v_ref, b_ref, o_ref, lse_ref = refs
        else:
            q_ref, k_ref, v_ref, o_ref, lse_ref = refs
            b_ref = None
        cmask = _causal_mask(t)
        lane_lt64 = lax.broadcasted_iota(jnp.int32, (t, LANES), 1) < 64

        def q_block(i, _):
            r0 = pl.multiple_of(i * t, t)
            qt = q_ref[pl.ds(r0, t), :]
            outs = []
            for a in range(2):
                if hw == LANES:
                    qa = qt[:, a * LANES:(a + 1) * LANES]
                else:
                    qa = jnp.where(_head_mask((t, LANES), a), qt, jnp.zeros_like(qt))

                def kv_step(j, carry, masked, a=a, qa=qa):
                    m, l, acc = carry
                    c0 = pl.multiple_of(j * t, t)
                    kt = k_ref[pl.ds(c0, t), :]
                    ka = kt[:, a * LANES:(a + 1) * LANES] if hw == LANES else kt
                    sc = lax.dot_general(qa, ka, _NT, preferred_element_type=F32) * scale
                    if has_bias:
                        sc = sc + b_ref[0, a, j]
                    if masked:
                        sc = jnp.where(cmask, sc, NEG_INF)
                    m_new = jnp.maximum(m, jnp.max(sc, axis=-1, keepdims=True))
                    alpha = jnp.exp(m - m_new)
                    p = jnp.exp(sc - m_new)
                    l_new = alpha * l + jnp.sum(p, axis=-1, keepdims=True)
                    pv = jnp.dot(p.astype(BF16), v_ref[pl.ds(c0, t), :], preferred_element_type=F32)
                    return m_new, l_new, alpha * acc + pv

                init = (jnp.full((t, 1), NEG_INF, F32), jnp.zeros((t, 1), F32), jnp.zeros((t, LANES), F32))
                carry = lax.fori_loop(0, i, functools.partial(kv_step, masked=False), init)
                m, l, acc = kv_step(i, carry, True)
                outs.append(acc / l)
                lse_ref[0, a, pl.ds(r0, t), :] = m + jnp.log(l)
            o_ref[pl.ds(r0, t), :] = jnp.where(lane_lt64, outs[0], outs[1])
            return 0

        lax.fori_loop(0, nb, q_block, 0)

    in_specs = [pl.BlockSpec((s, qw), lambda p: (0, q_off + p)), pl.BlockSpec((s, qw), lambda p: (0, k_off + p)),
                pl.BlockSpec((s, LANES), lambda p: (0, v_off + p))]
    args = [q, k, v]
    if has_bias:
        in_specs.append(pl.BlockSpec((1, 2, nb, 1, t), lambda p: (p, 0, 0, 0, 0)))
        args.append(bias)
    return pl.pallas_call(
        body, name=name, grid=(n_pairs,), in_specs=in_specs,
        out_specs=[pl.BlockSpec((s, LANES), lambda p: (0, p)), pl.BlockSpec((1, 2, s, 1), lambda p: (p, 0, 0, 0))],
        out_shape=[jax.ShapeDtypeStruct((s, n_pairs * LANES), F32), jax.ShapeDtypeStruct((n_pairs, 2, s, 1), F32)],
        compiler_params=_params(("parallel",), VMEM_LIMIT),
    )(*args)


def _flash_bwd(q, k, v, do, o, lse, bias, *, n_pairs, hw, q_off, k_off, v_off, scale, name):
    s = q.shape[0]
    t = ATT_T
    nb = s // t
    qw = 2 * hw
    has_bias = bias is not None

    def body(*refs):
        if has_bias:
            q_ref, k_ref, v_ref, do_ref, o_ref, lse_ref, b_ref, dq_ref, dkt_ref, dvt_ref, db_ref, dr_ref = refs
            db_ref[...] = jnp.zeros_like(db_ref)
        else:
            q_ref, k_ref, v_ref, do_ref, o_ref, lse_ref, dq_ref, dkt_ref, dvt_ref = refs
            b_ref = db_ref = dr_ref = None
        dkt_ref[...] = jnp.zeros_like(dkt_ref)
        dvt_ref[...] = jnp.zeros_like(dvt_ref)
        cmask = _causal_mask(t)

        def q_block(i, _):
            r0 = pl.multiple_of(i * t, t)
            qt = q_ref[pl.ds(r0, t), :]
            dot = do_ref[pl.ds(r0, t), :]
            prod = dot * o_ref[pl.ds(r0, t), :]
            dq_parts = []
            for a in range(2):
                hm = _head_mask((t, LANES), a)
                delta = jnp.sum(jnp.where(hm, prod, 0.0), axis=-1, keepdims=True)
                lse = lse_ref[0, a, pl.ds(r0, t), :]
                do_a = jnp.where(hm, dot, 0.0)
                do_ab = do_a.astype(BF16)
                do_at = do_a.T.astype(BF16)
                if hw == LANES:
                    qa = qt[:, a * LANES:(a + 1) * LANES]
                else:
                    qa = jnp.where(hm, qt, jnp.zeros_like(qt))
                qa_t = qa.astype(F32).T.astype(BF16)

                def kv_step(j, carry, masked, a=a, qa=qa, qa_t=qa_t, do_ab=do_ab, do_at=do_at, lse=lse, delta=delta):
                    dq, rsum = carry
                    c0 = pl.multiple_of(j * t, t)
                    kt = k_ref[pl.ds(c0, t), :]
                    ka = kt[:, a * LANES:(a + 1) * LANES] if hw == LANES else kt
                    vt = v_ref[pl.ds(c0, t), :]
                    sc = lax.dot_general(qa, ka, _NT, preferred_element_type=F32) * scale
                    if has_bias:
                        sc = sc + b_ref[0, a, j]
                    if masked:
                        sc = jnp.where(cmask, sc, NEG_INF)
                    p = jnp.exp(sc - lse)
                    dp = lax.dot_general(do_ab, vt, _NT, preferred_element_type=F32)
                    ds = p * (dp - delta)
                    dsb = ds.astype(BF16)
                    dvt_ref[0, j] += jnp.dot(do_at, p.astype(BF16), preferred_element_type=F32)
                    dk_add = jnp.dot(qa_t, dsb, preferred_element_type=F32) * scale
                    if hw == LANES:
                        dkt_ref[0, j, a * LANES:(a + 1) * LANES, :] += dk_add
                    else:
                        dkt_ref[0, j] += dk_add
                    if has_bias:
                        db_ref[0, a, j] += jnp.sum(ds, axis=0, keepdims=True)
                        rsum = rsum + jnp.sum(ds, axis=-1, keepdims=True)
                    return dq + jnp.dot(dsb, ka, preferred_element_type=F32), rsum

                init = (jnp.zeros((t, LANES), F32), jnp.zeros((t, 1), F32))
                carry = lax.fori_loop(0, i, functools.partial(kv_step, masked=False), init)
                dq, rsum = kv_step(i, carry, True)
                dq = dq * scale
                if has_bias:
                    dr_ref[0, a, pl.ds(r0, t), :] = rsum
                dq_parts.append(dq if hw == LANES else jnp.where(hm, dq, 0.0))
            if hw == LANES:
                dq_ref[pl.ds(r0, t), :] = jnp.concatenate(dq_parts, axis=1)
            else:
                dq_ref[pl.ds(r0, t), :] = dq_parts[0] + dq_parts[1]
            return 0

        lax.fori_loop(0, nb, q_block, 0)

    in_specs = [pl.BlockSpec((s, qw), lambda p: (0, q_off + p)), pl.BlockSpec((s, qw), lambda p: (0, k_off + p)),
                pl.BlockSpec((s, LANES), lambda p: (0, v_off + p)),
                pl.BlockSpec((s, LANES), lambda p: (0, p)), pl.BlockSpec((s, LANES), lambda p: (0, p)),
                pl.BlockSpec((1, 2, s, 1), lambda p: (p, 0, 0, 0))]
    args = [q, k, v, do, o, lse]
    out_specs = [pl.BlockSpec((s, qw), lambda p: (0, p)),
                 pl.BlockSpec((1, nb, qw, t), lambda p: (p, 0, 0, 0)),
                 pl.BlockSpec((1, nb, LANES, t), lambda p: (p, 0, 0, 0))]
    out_shape = [jax.ShapeDtypeStruct((s, n_pairs * qw), F32),
                 jax.ShapeDtypeStruct((n_pairs, nb, qw, t), F32),
                 jax.ShapeDtypeStruct((n_pairs, nb, LANES, t), F32)]
    if has_bias:
        in_specs.append(pl.BlockSpec((1, 2, nb, 1, t), lambda p: (p, 0, 0, 0, 0)))
        args.append(bias)
        out_specs.append(pl.BlockSpec((1, 2, nb, 1, t), lambda p: (p, 0, 0, 0, 0)))
        out_shape.append(jax.ShapeDtypeStruct((n_pairs, 2, nb, 1, t), F32))
        out_specs.append(pl.BlockSpec((1, 2, s, 1), lambda p: (p, 0, 0, 0)))
        out_shape.append(jax.ShapeDtypeStruct((n_pairs, 2, s, 1), F32))
    return pl.pallas_call(
        body, name=name, grid=(n_pairs,), in_specs=in_specs, out_specs=out_specs, out_shape=out_shape,
        compiler_params=_params(("parallel",), VMEM_LIMIT),
    )(*args)


def _untranspose(xt):
    p, nb, w, t = xt.shape
    return xt.transpose(1, 3, 0, 2).reshape(nb * t, p * w)


def _alibi_slope(h):
    return 2.0 ** (-8.0 * (h + 1.0) / SWA_HEADS)


SWA_ROWS = 512
SWA_SCALE = SWA_DIM ** -0.5


def _swa_geometry(i):
    w = WINDOW
    r0 = pl.multiple_of(i * w, w)
    b0 = pl.multiple_of(jnp.maximum(i - 1, 0) * w, w)
    row = lax.broadcasted_iota(jnp.int32, (w, 2 * w), 0)
    col = lax.broadcasted_iota(jnp.int32, (w, 2 * w), 1)
    dist = row - col + (r0 - b0)
    valid = (dist >= 0) & (dist < w)
    return r0, b0, dist.astype(F32), valid


def _swa_q_head(qblk, h):
    kv = h // (SWA_HEADS // SWA_KV_HEADS)
    if h % 2 != kv:
        qblk = pltpu.roll(qblk, 64, axis=1)
    return jnp.where(_head_mask(qblk.shape, kv), qblk, 0.0)


def _swa_fwd(z0b, sinks, *, name):
    s = z0b.shape[0]
    w = WINDOW
    rows = min(SWA_ROWS, s)
    per_step = rows // w
    qcols = SWA_HEADS * SWA_DIM

    def body(sink_ref, q_ref, k_ref, v_ref, o_ref, lse_ref):
        g = pl.program_id(0)
        for ii in range(per_step):
            r0, b0, dist, valid = _swa_geometry(g * per_step + ii)
            kb = k_ref[pl.ds(b0, 2 * w), :]
            vb = v_ref[pl.ds(b0, 2 * w), :]
            o_heads = []
            for h in range(SWA_HEADS):
                kv = h // (SWA_HEADS // SWA_KV_HEADS)
                blk = h // 2
                qh = _swa_q_head(q_ref[ii * w:(ii + 1) * w, blk * LANES:(blk + 1) * LANES].astype(F32), h).astype(BF16)
                sc = lax.dot_general(qh, kb, _NT, preferred_element_type=F32) * SWA_SCALE - _alibi_slope(h) * dist
                sc = jnp.where(valid, sc, NEG_INF)
                sink = sink_ref[0, h]
                m = jnp.maximum(jnp.max(sc, axis=-1, keepdims=True), sink)
                p = jnp.exp(sc - m)
                l = jnp.sum(p, axis=-1, keepdims=True) + jnp.exp(sink - m)
                oh = jnp.dot(p.astype(BF16), vb, preferred_element_type=F32) / l
                if h % 2 != kv:
                    oh = pltpu.roll(oh, 64, axis=1)
                o_heads.append(oh)
                lse_ref[h, ii * w:(ii + 1) * w, :] = m + jnp.log(l)
            lt64 = lax.broadcasted_iota(jnp.int32, (w, LANES), 1) < 64
            o_ref[ii * w:(ii + 1) * w, :] = jnp.concatenate(
                [jnp.where(lt64, o_heads[2 * b], o_heads[2 * b + 1]) for b in range(SWA_HEADS // 2)], axis=1)

    return pl.pallas_call(
        body, name=name, grid=(s // rows,),
        in_specs=[pl.BlockSpec(memory_space=pltpu.SMEM),
                  pl.BlockSpec((rows, qcols), lambda g: (g, 0)),
                  pl.BlockSpec((s, LANES), lambda g: (0, 4)), pl.BlockSpec((s, LANES), lambda g: (0, 5))],
        out_specs=[pl.BlockSpec((rows, qcols), lambda g: (g, 0)), pl.BlockSpec((SWA_HEADS, rows, 1), lambda g: (0, g, 0))],
        out_shape=[jax.ShapeDtypeStruct((s, qcols), F32), jax.ShapeDtypeStruct((SWA_HEADS, s, 1), F32)],
        compiler_params=_params(("parallel",), VMEM_LIMIT),
    )(sinks, z0b, z0b, z0b)


def _swa_bwd(z0b, sinks, do, o, lse, *, name):
    s = z0b.shape[0]
    w = WINDOW
    rows = min(SWA_ROWS, s)
    per_step = rows // w
    qcols = SWA_HEADS * SWA_DIM
    nblk = s // w

    def body(sink_ref, q_ref, k_ref, v_ref, do_ref, o_ref, lse_ref, dq_ref, dkt_ref, dvt_ref, dsink_ref):
        g = pl.program_id(0)

        @pl.when(g == 0)
        def _():
            dkt_ref[...] = jnp.zeros_like(dkt_ref)
            dvt_ref[...] = jnp.zeros_like(dvt_ref)
            dsink_ref[...] = jnp.zeros_like(dsink_ref)

        for ii in range(per_step):
            i = g * per_step + ii
            r0, b0, dist, valid = _swa_geometry(i)
            j0 = jnp.maximum(i - 1, 0)
            kb = k_ref[pl.ds(b0, 2 * w), :]
            vb = v_ref[pl.ds(b0, 2 * w), :]
            dq_heads = []
            for h in range(SWA_HEADS):
                kv = h // (SWA_HEADS // SWA_KV_HEADS)
                blk = h // 2
                cs = slice(blk * LANES, (blk + 1) * LANES)
                rs = slice(ii * w, (ii + 1) * w)
                qh32 = _swa_q_head(q_ref[rs, cs].astype(F32), h)
                qh = qh32.astype(BF16)
                doh32 = _swa_q_head(do_ref[rs, cs], h)
                oh32 = _swa_q_head(o_ref[rs, cs], h)
                delta = jnp.sum(doh32 * oh32, axis=-1, keepdims=True)
                lse = lse_ref[h, rs, :]
                sink = sink_ref[0, h]
                sc = lax.dot_general(qh, kb, _NT, preferred_element_type=F32) * SWA_SCALE - _alibi_slope(h) * dist
                sc = jnp.where(valid, sc, NEG_INF)
                p = jnp.exp(sc - lse)
                dp = lax.dot_general(doh32.astype(BF16), vb, _NT, preferred_element_type=F32)
                ds = p * (dp - delta)
                dsb = ds.astype(BF16)
                pb = p.astype(BF16)
                dsink_ref[h:h + 1, :] += jnp.broadcast_to(-jnp.sum(jnp.exp(sink - lse) * delta), (1, LANES))
                do_t = doh32.T.astype(BF16)
                q_t = qh32.T.astype(BF16)
                dvt = jnp.dot(do_t, pb, preferred_element_type=F32)
                dkt = jnp.dot(q_t, dsb, preferred_element_type=F32) * SWA_SCALE
                dvt_ref[j0] += dvt[:, :w]
                dvt_ref[j0 + 1] += dvt[:, w:]
                dkt_ref[j0] += dkt[:, :w]
                dkt_ref[j0 + 1] += dkt[:, w:]
                dq = jnp.dot(dsb, kb, preferred_element_type=F32) * SWA_SCALE
                if h % 2 != kv:
                    dq = pltpu.roll(dq, 64, axis=1)
                dq_heads.append(dq)
            lt64 = lax.broadcasted_iota(jnp.int32, (w, LANES), 1) < 64
            dq_ref[ii * w:(ii + 1) * w, :] = jnp.concatenate(
                [jnp.where(lt64, dq_heads[2 * b], dq_heads[2 * b + 1]) for b in range(SWA_HEADS // 2)], axis=1)

    return pl.pallas_call(
        body, name=name, grid=(s // rows,),
        in_specs=[pl.BlockSpec(memory_space=pltpu.SMEM),
                  pl.BlockSpec((rows, qcols), lambda g: (g, 0)),
                  pl.BlockSpec((s, LANES), lambda g: (0, 4)), pl.BlockSpec((s, LANES), lambda g: (0, 5)),
                  pl.BlockSpec((rows, qcols), lambda g: (g, 0)), pl.BlockSpec((rows, qcols), lambda g: (g, 0)),
                  pl.BlockSpec((SWA_HEADS, rows, 1), lambda g: (0, g, 0))],
        out_specs=[pl.BlockSpec((rows, qcols), lambda g: (g, 0)),
                   pl.BlockSpec((nblk, LANES, w), lambda g: (0, 0, 0)),
                   pl.BlockSpec((nblk, LANES, w), lambda g: (0, 0, 0)),
                   pl.BlockSpec((SWA_HEADS, LANES), lambda g: (0, 0))],
        out_shape=[jax.ShapeDtypeStruct((s, qcols), F32),
                   jax.ShapeDtypeStruct((nblk, LANES, w), F32), jax.ShapeDtypeStruct((nblk, LANES, w), F32),
                   jax.ShapeDtypeStruct((SWA_HEADS, LANES), F32)],
        compiler_params=_params(("arbitrary",), VMEM_LIMIT),
    )(sinks, z0b, z0b, z0b, do, o, lse)


CUM_T = 256


def _split3(x):
    hi = x.astype(BF16)
    r1 = x - hi.astype(F32)
    mid = r1.astype(BF16)
    lo = (r1 - mid.astype(F32)).astype(BF16)
    return hi, mid, lo


def _tri_dot(tri, x):
    hi, mid, lo = _split3(x)
    out = jnp.dot(tri, hi, preferred_element_type=F32)
    out = out + jnp.dot(tri, mid, preferred_element_type=F32)
    return out + jnp.dot(tri, lo, preferred_element_type=F32)


def _logf_fwd(zf, bf, *, name):
    s = zf.shape[0]
    t = CUM_T
    nb = s // t

    def body(z_ref, b_ref, c_ref, carry_ref):
        i = pl.program_id(0)

        @pl.when(i == 0)
        def _():
            carry_ref[...] = jnp.zeros_like(carry_ref)

        x = z_ref[...] + b_ref[...]
        lf = jnp.minimum(x, 0.0) - jnp.log(1.0 + jnp.exp(-jnp.abs(x)))
        row = lax.broadcasted_iota(jnp.int32, (t, t), 0)
        col = lax.broadcasted_iota(jnp.int32, (t, t), 1)
        tri = jnp.where(col <= row, 1.0, 0.0).astype(BF16)
        c = _tri_dot(tri, lf) + carry_ref[...]
        c_ref[...] = c
        carry_ref[...] = c[t - 1:t, :]

    return pl.pallas_call(
        body, name=name, grid=(nb,),
        in_specs=[pl.BlockSpec((t, LANES), lambda i: (i, 0)), pl.BlockSpec((1, LANES), lambda i: (0, 0))],
        out_specs=pl.BlockSpec((t, LANES), lambda i: (i, 0)),
        out_shape=jax.ShapeDtypeStruct((s, LANES), F32),
        scratch_shapes=[pltpu.VMEM((1, LANES), F32)],
        compiler_params=_params(("arbitrary",)),
    )(zf, bf)


def _logf_bwd(dc, zf, bf, *, name):
    s = zf.shape[0]
    t = CUM_T
    nb = s // t

    def body(dc_ref, z_ref, b_ref, dz_ref, db_ref, carry_ref):
        i = pl.program_id(0)

        @pl.when(i == 0)
        def _():
            carry_ref[...] = jnp.zeros_like(carry_ref)
            db_ref[...] = jnp.zeros_like(db_ref)

        row = lax.broadcasted_iota(jnp.int32, (t, t), 0)
        col = lax.broadcasted_iota(jnp.int32, (t, t), 1)
        tri = jnp.where(col >= row, 1.0, 0.0).astype(BF16)
        dlf = _tri_dot(tri, dc_ref[...]) + carry_ref[...]
        carry_ref[...] = dlf[0:1, :]
        x = z_ref[...] + b_ref[...]
        dz = dlf * _sigmoid(-x)
        dz_ref[...] = dz.astype(BF16)
        db_ref[...] += jnp.sum(dz, axis=0, keepdims=True)

    return pl.pallas_call(
        body, name=name, grid=(nb,),
        in_specs=[pl.BlockSpec((t, LANES), lambda i: (nb - 1 - i, 0)), pl.BlockSpec((t, LANES), lambda i: (nb - 1 - i, 0)),
                  pl.BlockSpec((1, LANES), lambda i: (0, 0))],
        out_specs=[pl.BlockSpec((t, LANES), lambda i: (nb - 1 - i, 0)), pl.BlockSpec((1, LANES), lambda i: (0, 0))],
        out_shape=[jax.ShapeDtypeStruct((s, LANES), BF16), jax.ShapeDtypeStruct((1, LANES), F32)],
        scratch_shapes=[pltpu.VMEM((1, LANES), F32)],
        compiler_params=_params(("arbitrary",)),
    )(dc, zf, bf)


def _loss_head(x2, g, target, *, name):
    s = x2.shape[0]
    tm = _tile(s, (256, 128))

    def body(x_ref, g_ref, t_ref, dx_ref, loss_ref, dg_ref):
        i = pl.program_id(0)

        @pl.when(i == 0)
        def _():
            loss_ref[...] = jnp.zeros_like(loss_ref)
            dg_ref[...] = jnp.zeros_like(dg_ref)

        xf = x_ref[...]
        r = lax.rsqrt(jnp.mean(xf * xf, axis=-1, keepdims=True) + EPS)
        xh = xf * r
        gv = g_ref[...]
        err = xh * gv - t_ref[...]
        loss_ref[...] += jnp.broadcast_to(0.5 * jnp.sum(jnp.mean(err * err, axis=-1, keepdims=True)), loss_ref.shape)
        dy = err * (1.0 / D_MODEL)
        dg_ref[...] += jnp.sum(dy * xh, axis=0, keepdims=True)
        dxh = dy * gv
        dx_ref[...] = r * (dxh - xh * jnp.mean(dxh * xh, axis=-1, keepdims=True))

    return pl.pallas_call(
        body, name=name, grid=(s // tm,),
        in_specs=[pl.BlockSpec((tm, D_MODEL), lambda i: (i, 0)), pl.BlockSpec((1, D_MODEL), lambda i: (0, 0)),
                  pl.BlockSpec((tm, D_MODEL), lambda i: (i, 0))],
        out_specs=[pl.BlockSpec((tm, D_MODEL), lambda i: (i, 0)), pl.BlockSpec((8, LANES), lambda i: (0, 0)),
                   pl.BlockSpec((1, D_MODEL), lambda i: (0, 0))],
        out_shape=[jax.ShapeDtypeStruct((s, D_MODEL), F32), jax.ShapeDtypeStruct((8, LANES), F32),
                   jax.ShapeDtypeStruct((1, D_MODEL), F32)],
        compiler_params=_params(("arbitrary",)),
    )(x2, g, target)


def _adamw(pieces, w, m, v, *, name):
    rows = w.shape[0]
    tr = _tile(rows, (544, 256, 24))
    bc1 = 1.0 - ADAM_B1 ** ADAM_STEP
    bc2 = 1.0 - ADAM_B2 ** ADAM_STEP

    def body(p_ref, w_ref, m_ref, v_ref, g_ref, d_ref, nm_ref, nv_ref):
        g = p_ref[0].astype(F32)
        for k in range(1, N_DEV):
            g = g + p_ref[k].astype(F32)
        nm = ADAM_B1 * m_ref[...] + (1.0 - ADAM_B1) * g
        nv = ADAM_B2 * v_ref[...] + (1.0 - ADAM_B2) * (g * g)
        m_hat = nm / bc1
        v_hat = nv / bc2
        g_ref[...] = g
        d_ref[...] = -ADAM_LR * (m_hat / (jnp.sqrt(v_hat) + ADAM_EPS) + ADAM_WD * w_ref[...])
        nm_ref[...] = nm
        nv_ref[...] = nv

    spec = pl.BlockSpec((tr, LANES), lambda i: (i, 0))
    shape = jax.ShapeDtypeStruct((rows, LANES), F32)
    return pl.pallas_call(
        body, name=name, grid=(rows // tr,),
        in_specs=[pl.BlockSpec((N_DEV, tr, LANES), lambda i: (0, i, 0)), spec, spec, spec],
        out_specs=[spec, spec, spec, spec], out_shape=[shape, shape, shape, shape],
        compiler_params=_params(("parallel",)),
    )(pieces, w, m, v)


MESH = pl.DeviceIdType.MESH
ANY = pl.BlockSpec(memory_space=pl.ANY)


def _all_gather(shard, *, name):
    rows, lanes = shard.shape

    def body(x_ref, out_ref, send_sems, recv_sems, local_sem):
        x, y, c = lax.axis_index("x"), lax.axis_index("y"), lax.axis_index("c")
        me, sibling = (x, y, c), (x, y, 1 - c)
        chips = [(1 - x, y), (x, 1 - y), (1 - x, 1 - y)]

        def block(px, py, pc):
            return out_ref.at[4 * px + 2 * py + pc]

        def copy(k, blk, to, src=None):
            return pltpu.make_async_remote_copy(
                src_ref=block(*blk) if src is None else src, dst_ref=block(*blk),
                send_sem=send_sems.at[k], recv_sem=recv_sems.at[k], device_id=to, device_id_type=MESH)

        mine = pltpu.make_async_copy(x_ref, block(*me), local_sem)
        mine.start()
        first = [copy(0, me, sibling, src=x_ref)]
        first += [copy(1 + j, me, (*chip, c), src=x_ref) for j, chip in enumerate(chips)]
        for cp in first:
            cp.start()
        passed = [copy(4 + j, (*chip, c), sibling) for j, chip in enumerate(chips)]
        for j, chip in enumerate(chips):
            copy(1 + j, (*chip, c), me).wait_recv()
            passed[j].start()
        copy(0, sibling, me).wait_recv()
        for j, chip in enumerate(chips):
            copy(4 + j, (*chip, 1 - c), me).wait_recv()
        for cp in first + passed:
            cp.wait_send()
        mine.wait()

    return pl.pallas_call(
        body, name=name, out_shape=jax.ShapeDtypeStruct((N_DEV, rows, lanes), shard.dtype),
        in_specs=[ANY], out_specs=ANY,
        scratch_shapes=[pltpu.SemaphoreType.DMA((7,)), pltpu.SemaphoreType.DMA((7,)), pltpu.SemaphoreType.DMA(())],
    )(shard)


def _exchange(pieces, *, name):
    def body(g_ref, out_ref, send_sems, recv_sems, local_sem):
        x, y, c = lax.axis_index("x"), lax.axis_index("y"), lax.axis_index("c")
        me = 4 * x + 2 * y + c
        mine = pltpu.make_async_copy(g_ref.at[me], out_ref.at[me], local_sem)
        mine.start()
        copies = []
        for r in range(1, N_DEV):
            px = 1 - x if r & 4 else x
            py = 1 - y if r & 2 else y
            pc = 1 - c if r & 1 else c
            cp = pltpu.make_async_remote_copy(
                src_ref=g_ref.at[4 * px + 2 * py + pc], dst_ref=out_ref.at[me],
                send_sem=send_sems.at[r - 1], recv_sem=recv_sems.at[r - 1],
                device_id=(px, py, pc), device_id_type=MESH)
            cp.start()
            copies.append(cp)
        for cp in copies:
            cp.wait()
        mine.wait()

    return pl.pallas_call(
        body, name=name, out_shape=jax.ShapeDtypeStruct(pieces.shape, pieces.dtype),
        in_specs=[ANY], out_specs=ANY,
        scratch_shapes=[pltpu.SemaphoreType.DMA((7,)), pltpu.SemaphoreType.DMA((7,)), pltpu.SemaphoreType.DMA(())],
    )(pieces)


def _cols_to_rows(w_shard):
    return w_shard.reshape(-1, LANES)


def _gathered_cols(gath, off, nrows, kdim):
    n = nrows * LANES // kdim
    return gath[:, off:off + nrows, :].reshape(N_DEV, kdim, n).transpose(1, 0, 2).reshape(kdim, N_DEV * n)


def _gathered_rows(gath, off, nrows, ncols):
    return gath[:, off:off + nrows, :].reshape(-1, ncols)


def _scatter_cols(dw):
    kdim, n8 = dw.shape
    n = n8 // N_DEV
    return dw.reshape(kdim, N_DEV, n).transpose(1, 0, 2).reshape(N_DEV, kdim * n // LANES, LANES)


def _scatter_rows(dw):
    return dw.reshape(N_DEV, -1, LANES)


def _layer0_in_weight(w_in):
    cq, ckv, kpe = w_in[:, 0:256], w_in[:, 256:384], w_in[:, 384:416]
    q_s, k_s, v_s, gate = w_in[:, 416:928], w_in[:, 928:1056], w_in[:, 1056:1184], w_in[:, 1184:2208]
    z = jnp.zeros((w_in.shape[0], 64), w_in.dtype)
    return jnp.concatenate([gate, cq, ckv, z, kpe, z[:, :32], q_s, k_s, v_s], axis=1)


def _layer0_in_grad(dwp):
    gate, cq, ckv, kpe = dwp[:, 0:1024], dwp[:, 1024:1280], dwp[:, 1280:1408], dwp[:, 1472:1504]
    q_s, k_s, v_s = dwp[:, 1536:2048], dwp[:, 2048:2176], dwp[:, 2176:2304]
    return jnp.concatenate([cq, ckv, kpe, q_s, k_s, v_s, gate], axis=1)


def _q_up_weight(w):
    return jnp.pad(w.reshape(MLA_Q_RANK, MLA_HEADS, 96), ((0, 0), (0, 0), (0, 32))).reshape(MLA_Q_RANK, MLA_HEADS * LANES)


def _q_up_grad(dwp):
    return dwp.reshape(MLA_Q_RANK, MLA_HEADS, LANES)[:, :, :96].reshape(MLA_Q_RANK, MLA_HEADS * 96)


def _kv_up_weight(w):
    w4 = w.reshape(MLA_KV_RANK, MLA_HEADS, 2, 64)
    kp = jnp.pad(w4[:, :, 0, :], ((0, 0), (0, 0), (0, 64))).reshape(MLA_KV_RANK, MLA_HEADS * LANES)
    vp = w4[:, :, 1, :].reshape(MLA_KV_RANK, MLA_HEADS * 64)
    return jnp.concatenate([kp, vp], axis=1)


def _kv_up_grad(dwp):
    dk = dwp[:, :MLA_HEADS * LANES].reshape(MLA_KV_RANK, MLA_HEADS, LANES)[:, :, :64]
    dv = dwp[:, MLA_HEADS * LANES:].reshape(MLA_KV_RANK, MLA_HEADS, 64)
    return jnp.stack([dk, dv], axis=2).reshape(MLA_KV_RANK, MLA_HEADS * LANES)


def _pad_lanes(a):
    return jnp.pad(a, ((0, 0), (0, LANES - a.shape[1])))


def _small_pack(g_in, g_final, g_q_a, g_kv_a, sinks, b_f, loss):
    rows = [g_in.reshape(8, LANES), g_final.reshape(8, LANES), g_q_a.reshape(2, LANES), g_kv_a.reshape(1, LANES),
            _pad_lanes(sinks.reshape(1, -1)), _pad_lanes(b_f.reshape(1, -1)), _pad_lanes(loss.reshape(1, 1)),
            jnp.zeros((2, LANES), F32)]
    return jnp.concatenate(rows, axis=0)


def _small_unpack(a):
    return (a[0:8].reshape(1, D_MODEL), a[8:16].reshape(D_MODEL), a[16:18].reshape(1, MLA_Q_RANK),
            a[18:19].reshape(1, MLA_KV_RANK), a[19:20, :SWA_HEADS], a[20:21, :FOX_HEADS], a[21, 0])


def _local_step(x, positions, target, e_g_in, w0, e_g_q_a, wq, e_g_kv_a, wkv, e_sinks, wo0,
                o_g_in, w1, wf, o_b_f, wo1, g_final):
    s = x.shape[0]
    nb = s // ATT_T
    mla_scale = (MLA_NOPE + MLA_ROPE) ** -0.5
    fox_scale = FOX_DIM ** -0.5
    n0a = Z0A_UNITS * LANES

    inv_freq = 1.0 / (ROPE_THETA ** (jnp.arange(0, MLA_ROPE, 2, dtype=F32) / MLA_ROPE))
    ang = positions.astype(F32)[:, None] * inv_freq
    cos, sin = jnp.cos(ang), jnp.sin(ang)
    ones, zeros = jnp.ones((s, 64), F32), jnp.zeros((s, 64), F32)
    cos_t = jnp.concatenate([ones, cos, cos, ones[:, :32]], axis=1)
    sin_t = jnp.concatenate([zeros, -sin, sin, zeros[:, :32]], axis=1)

    h0 = _rmsnorm_fwd(x, e_g_in, width=D_MODEL, col_blk=0, name="l0_norm")
    z0a = _matmul(h0, w0[:, :n0a], name="l0_in_a")
    z0b = _matmul(h0, w0[:, n0a:], name="l0_in_b", out_dtype=BF16)
    cqn = _rmsnorm_fwd(z0a, e_g_q_a, width=MLA_Q_RANK, col_blk=4, name="l0_q_norm")
    ckvn = _rmsnorm_fwd(z0a, e_g_kv_a, width=MLA_KV_RANK, col_blk=10, name="l0_kv_norm")
    qp = _matmul(cqn, wq, name="l0_q_up")
    kvp = _matmul(ckvn, wkv, name="l0_kv_up", out_dtype=BF16)
    qm, km = _rope_fwd(qp, kvp, z0a, cos_t, sin_t, name="l0_rope")
    o_mla, lse_mla = _flash_fwd(qm, km, kvp, None, n_pairs=MLA_HEADS // 2, hw=LANES, q_off=0, k_off=0,
                                v_off=MLA_HEADS, scale=mla_scale, name="l0_mla_fwd")
    o_swa, lse_swa = _swa_fwd(z0b, e_sinks, name="l0_swa_fwd")
    og0 = _gate_fwd([o_mla, o_swa], z0a, name="l0_gate")
    x1 = _matmul(og0, wo0, add=x, name="l0_out")

    h1 = _rmsnorm_fwd(x1, o_g_in, width=D_MODEL, col_blk=0, name="l1_norm")
    z1 = _matmul(h1, w1[:, :3 * D_MODEL], name="l1_in_qkv", out_dtype=BF16)
    gate1 = _matmul(h1, w1[:, 3 * D_MODEL:], name="l1_in_gate")
    zf = _matmul(h1, wf, name="l1_in_f")
    bf = _pad_lanes(o_b_f)
    log_cum = _logf_fwd(zf, bf, name="l1_logf")
    bias = (-log_cum[:, :FOX_HEADS]).T.reshape(FOX_HEADS // 2, 2, nb, 1, ATT_T)
    o_fox, lse_fox = _flash_fwd(z1, z1, z1, bias, n_pairs=FOX_HEADS // 2, hw=64, q_off=0, k_off=8, v_off=16,
                                scale=fox_scale, name="l1_fox_fwd")
    og1 = _gate_fwd([o_fox], gate1, name="l1_gate")
    x2 = _matmul(og1, wo1, add=x1, name="l1_out")

    dx2, loss_part, d_g_final = _loss_head(x2, g_final.reshape(1, D_MODEL), target, name="loss_head")

    d_wo1 = _matmul(og1, dx2, ta=True, name="l1_out_dw")
    d_og1 = _matmul(dx2, wo1, tb=True, name="l1_out_dx")
    do_fox, d_gate1 = _gate_bwd(d_og1, [o_fox], gate1, name="l1_gate_bwd")
    dq1, dkt1, dvt1, dbias, drow = _flash_bwd(z1, z1, z1, do_fox, o_fox, lse_fox, bias, n_pairs=FOX_HEADS // 2, hw=64,
                                        q_off=0, k_off=8, v_off=16, scale=fox_scale, name="l1_fox_bwd")
    d_log_cum = (drow.reshape(FOX_HEADS, s) - dbias.reshape(FOX_HEADS, s)).T
    d_log_cum = jnp.pad(d_log_cum, ((0, 0), (0, LANES - FOX_HEADS)))
    d_zf, d_bf = _logf_bwd(d_log_cum, zf, bf, name="l1_logf_bwd")
    dz1 = jnp.concatenate([dq1.astype(BF16), _untranspose(dkt1).astype(BF16), _untranspose(dvt1).astype(BF16), d_gate1],
                          axis=1)
    d_w1 = _matmul(h1, dz1, ta=True, name="l1_in_dw")
    d_wf = _matmul(h1, d_zf, ta=True, name="l1_in_f_dw")
    dh1 = _matmul(dz1, w1, tb=True, name="l1_in_dx")
    dh1 = _matmul(d_zf, wf, tb=True, add=dh1, name="l1_in_f_dx")
    dx1, d_o_g_in = _rmsnorm_bwd(x1, o_g_in, dh1, width=D_MODEL, col_blk=0, add=dx2, name="l1_norm_bwd")

    d_wo0 = _matmul(og0, dx1, ta=True, name="l0_out_dw")
    d_og0 = _matmul(dx1, wo0, tb=True, name="l0_out_dx")
    do_mla, do_swa, d_gate0 = _gate_bwd(d_og0, [o_mla, o_swa], z0a, name="l0_gate_bwd")
    dq_s, dkt_s, dvt_s, d_sinks = _swa_bwd(z0b, e_sinks, do_swa, o_swa, lse_swa, name="l0_swa_bwd")
    dk_s = dkt_s.transpose(0, 2, 1).reshape(s, LANES)
    dv_s = dvt_s.transpose(0, 2, 1).reshape(s, LANES)
    dqm, dkt_m, dvt_m = _flash_bwd(qm, km, kvp, do_mla, o_mla, lse_mla, None, n_pairs=MLA_HEADS // 2, hw=LANES,
                                   q_off=0, k_off=0, v_off=MLA_HEADS, scale=mla_scale, name="l0_mla_bwd")
    dkm = _untranspose(dkt_m)
    d_qp, d_kpe = _rope_bwd(dqm, dkm, cos_t, sin_t, name="l0_rope_bwd")
    d_kvp = jnp.concatenate([dkm.astype(BF16), _untranspose(dvt_m).astype(BF16)], axis=1)
    d_wq = _matmul(cqn, d_qp, ta=True, name="l0_q_up_dw")
    d_cqn = _matmul(d_qp, wq, tb=True, name="l0_q_up_dx")
    d_wkv = _matmul(ckvn, d_kvp, ta=True, name="l0_kv_up_dw")
    d_ckvn = _matmul(d_kvp, wkv, tb=True, name="l0_kv_up_dx")
    d_cq, d_g_q_a = _rmsnorm_bwd(z0a, e_g_q_a, d_cqn, width=MLA_Q_RANK, col_blk=4, out_dtype=BF16, name="l0_q_norm_bwd")
    d_ckv, d_g_kv_a = _rmsnorm_bwd(z0a, e_g_kv_a, d_ckvn, width=MLA_KV_RANK, col_blk=10, out_dtype=BF16,
                                   name="l0_kv_norm_bwd")
    dz0 = jnp.concatenate([d_gate0, d_cq, d_ckv, d_kpe, dq_s.astype(BF16), dk_s.astype(BF16), dv_s.astype(BF16)], axis=1)
    d_w0 = _matmul(h0, dz0, ta=True, name="l0_in_dw")
    dh0 = _matmul(dz0, w0, tb=True, name="l0_in_dx")
    grad_x, d_e_g_in = _rmsnorm_bwd(x, e_g_in, dh0, width=D_MODEL, col_blk=0, add=dx1, name="l0_norm_bwd")

    return dict(loss=loss_part[0, 0], grad_x=grad_x, e_g_in=d_e_g_in, w0=d_w0, e_g_q_a=d_g_q_a, wq=d_wq,
                e_g_kv_a=d_g_kv_a, wkv=d_wkv, e_sinks=d_sinks[:, 0].reshape(1, SWA_HEADS), wo0=d_wo0,
                o_g_in=d_o_g_in, w1=d_w1, wf=d_wf, o_b_f=d_bf[:, :FOX_HEADS], wo1=d_wo1, g_final=d_g_final.reshape(D_MODEL))


def _layer1_in_weight(w_in):
    main = jnp.concatenate([w_in[:, :3 * D_MODEL], w_in[:, 3 * D_MODEL + FOX_HEADS:]], axis=1)
    wf = jnp.pad(w_in[:, 3 * D_MODEL:3 * D_MODEL + FOX_HEADS], ((0, 0), (0, LANES - FOX_HEADS)))
    return main, wf


def _layer1_in_grad(d_main, d_wf):
    return jnp.concatenate([d_main[:, :3 * D_MODEL], d_wf[:, :FOX_HEADS], d_main[:, 3 * D_MODEL:]], axis=1)


def _flat_shard(e_w_in, e_w_q_up, e_w_kv_up, e_w_out, o_w_in, o_w_out, o_g_in_rows):
    return jnp.concatenate([_cols_to_rows(e_w_in), _cols_to_rows(e_w_q_up), _cols_to_rows(e_w_kv_up),
                            _cols_to_rows(e_w_out), _cols_to_rows(o_w_in), _cols_to_rows(o_w_out), o_g_in_rows], axis=0)


def _g_rows(a):
    return jnp.pad(a, ((0, R_O_G_IN - 1), (0, 0)))


def _unflat_shard(flat):
    return (flat[OFF_E_W_IN:OFF_E_W_Q].reshape(1, D_MODEL, 276), flat[OFF_E_W_Q:OFF_E_W_KV].reshape(1, MLA_Q_RANK, 96),
            flat[OFF_E_W_KV:OFF_E_W_OUT].reshape(1, MLA_KV_RANK, 128), flat[OFF_E_W_OUT:OFF_O_W_IN].reshape(1, 128, D_MODEL),
            flat[OFF_O_W_IN:OFF_O_W_OUT].reshape(1, D_MODEL, 514), flat[OFF_O_W_OUT:OFF_O_G_IN].reshape(1, 128, D_MODEL),
            flat[OFF_O_G_IN:OFF_O_G_IN + 1])


def kernel(x, positions, e_g_in, e_w_in, e_g_q_a, e_w_q_up, e_g_kv_a, e_w_kv_up, e_sinks, e_w_out, o_g_in, o_w_in, o_b_f, o_w_out, g_final, loss_target, m_e_g_in, m_e_w_in, m_e_g_q_a, m_e_w_q_up, m_e_g_kv_a, m_e_w_kv_up, m_e_sinks, m_e_w_out, m_o_g_in, m_o_w_in, m_o_b_f, m_o_w_out, m_g_final, v_e_g_in, v_e_w_in, v_e_g_q_a, v_e_w_q_up, v_e_g_kv_a, v_e_w_kv_up, v_e_sinks, v_e_w_out, v_o_g_in, v_o_w_in, v_o_b_f, v_o_w_out, v_g_final):
    g_bits = lax.bitcast_convert_type(o_g_in.reshape(LANES), BF16).reshape(2, LANES)
    g_bits = jnp.pad(g_bits, ((0, R_O_G_IN - 2), (0, 0)))
    shard = _flat_shard(e_w_in[0].astype(BF16), e_w_q_up[0].astype(BF16), e_w_kv_up[0].astype(BF16),
                        e_w_out[0].astype(BF16), o_w_in[0].astype(BF16), o_w_out[0].astype(BF16), g_bits)
    gath = _all_gather(shard, name="weights_all_gather")
    w0 = _layer0_in_weight(_gathered_cols(gath, OFF_E_W_IN, R_E_W_IN, D_MODEL))
    wq = _q_up_weight(_gathered_cols(gath, OFF_E_W_Q, R_E_W_Q, MLA_Q_RANK))
    wkv = _kv_up_weight(_gathered_cols(gath, OFF_E_W_KV, R_E_W_KV, MLA_KV_RANK))
    wo0 = _gathered_rows(gath, OFF_E_W_OUT, R_E_W_OUT, D_MODEL)
    w1, wf = _layer1_in_weight(_gathered_cols(gath, OFF_O_W_IN, R_O_W_IN, D_MODEL))
    wo1 = _gathered_rows(gath, OFF_O_W_OUT, R_O_W_OUT, D_MODEL)
    o_g_full = lax.bitcast_convert_type(gath[:, OFF_O_G_IN:OFF_O_G_IN + 2, :].reshape(N_DEV, LANES, 2), F32)
    o_g_full = o_g_full.reshape(1, D_MODEL)

    gr = _local_step(x[0], positions[0], loss_target[0], e_g_in, w0, e_g_q_a, wq, e_g_kv_a, wkv, e_sinks, wo0,
                     o_g_full, w1, wf, o_b_f, wo1, g_final)

    d_o_g = jnp.pad(gr["o_g_in"].reshape(N_DEV, 1, LANES), ((0, 0), (0, R_O_G_IN - 1), (0, 0)))
    pieces = jnp.concatenate([
        _scatter_cols(_layer0_in_grad(gr["w0"])), _scatter_cols(_q_up_grad(gr["wq"])),
        _scatter_cols(_kv_up_grad(gr["wkv"])), _scatter_rows(gr["wo0"]),
        _scatter_cols(_layer1_in_grad(gr["w1"], gr["wf"])), _scatter_rows(gr["wo1"]), d_o_g], axis=1)
    recv = _exchange(pieces.astype(BF16), name="grads_exchange")
    w_flat = _flat_shard(e_w_in[0], e_w_q_up[0], e_w_kv_up[0], e_w_out[0], o_w_in[0], o_w_out[0], _g_rows(o_g_in))
    m_flat = _flat_shard(m_e_w_in[0], m_e_w_q_up[0], m_e_w_kv_up[0], m_e_w_out[0], m_o_w_in[0], m_o_w_out[0],
                         _g_rows(m_o_g_in))
    v_flat = _flat_shard(v_e_w_in[0], v_e_w_q_up[0], v_e_w_kv_up[0], v_e_w_out[0], v_o_w_in[0], v_o_w_out[0],
                         _g_rows(v_o_g_in))
    flats = _adamw(recv, w_flat, m_flat, v_flat, name="adamw_sharded")
    g_sh, d_sh, m_sh, v_sh = [_unflat_shard(f) for f in flats]

    small = _small_pack(gr["e_g_in"], gr["g_final"], gr["e_g_q_a"], gr["e_g_kv_a"], gr["e_sinks"], gr["o_b_f"], gr["loss"])
    small_all = _all_gather(small, name="small_all_gather")
    zero = jnp.zeros((), F32)
    w_small = _small_pack(e_g_in, g_final, e_g_q_a, e_g_kv_a, e_sinks, o_b_f, zero)
    m_small = _small_pack(m_e_g_in, m_g_final, m_e_g_q_a, m_e_g_kv_a, m_e_sinks, m_o_b_f, zero)
    v_small = _small_pack(v_e_g_in, v_g_final, v_e_g_q_a, v_e_g_kv_a, v_e_sinks, v_o_b_f, zero)
    smalls = _adamw(small_all, w_small, m_small, v_small, name="adamw_replicated")
    g_sm, d_sm, m_sm, v_sm = [_small_unpack(a) for a in smalls]
    loss = g_sm[6]

    def leaves(sh, sm):
        return (sm[0], sh[0], sm[2], sh[1], sm[3], sh[2], sm[4], sh[3], sh[6], sh[4], sm[5], sh[5], sm[1])

    return (loss, gr["grad_x"][None], *leaves(g_sh, g_sm), *leaves(d_sh, d_sm), *leaves(m_sh, m_sm), *leaves(v_sh, v_sm))
```

```python
import functools

import jax
import jax.numpy as jnp
from jax import lax
from jax.experimental import pallas as pl
from jax.experimental.pallas import tpu as pltpu

F32 = jnp.float32
BF16 = jnp.bfloat16
NEG_INF = float("-inf")

N_DEV = 8
LANES = 128
D_MODEL = 1024
EPS = 1e-6
ROPE_THETA = 10000.0
MLA_HEADS = 8
MLA_Q_RANK = 256
MLA_KV_RANK = 128
MLA_NOPE = 64
MLA_ROPE = 32
MLA_V = 64
SWA_HEADS = 8
SWA_KV_HEADS = 2
SWA_DIM = 64
WINDOW = 128
FOX_HEADS = 16
FOX_DIM = 64

ADAM_LR = 0.001
ADAM_B1 = 0.9
ADAM_B2 = 0.999
ADAM_EPS = 1e-08
ADAM_WD = 0.01
ADAM_STEP = 10

ATT_T = 512
VMEM_LIMIT = 56 * 1024 * 1024

Z0A_UNITS = 12
Z0B_UNITS = 6

R_E_W_IN = 1024 * 276 // LANES
R_E_W_Q = 256 * 96 // LANES
R_E_W_KV = 128 * 128 // LANES
R_E_W_OUT = 128 * 1024 // LANES
R_O_W_IN = 1024 * 514 // LANES
R_O_W_OUT = 128 * 1024 // LANES
R_O_G_IN = 16
OFF_E_W_IN = 0
OFF_E_W_Q = OFF_E_W_IN + R_E_W_IN
OFF_E_W_KV = OFF_E_W_Q + R_E_W_Q
OFF_E_W_OUT = OFF_E_W_KV + R_E_W_KV
OFF_O_W_IN = OFF_E_W_OUT + R_E_W_OUT
OFF_O_W_OUT = OFF_O_W_IN + R_O_W_IN
OFF_O_G_IN = OFF_O_W_OUT + R_O_W_OUT
R_FLAT = OFF_O_G_IN + R_O_G_IN
SMALL_ROWS = 24


def _tile(n, cands):
    for c in cands:
        if n % c == 0:
            return c
    raise ValueError(f"no tile for {n}")


def _params(sem, vmem=None):
    return pltpu.CompilerParams(dimension_semantics=sem, vmem_limit_bytes=vmem)


def _matmul(a, b, *, name, ta=False, tb=False, add=None, out_dtype=F32):
    if ta:
        kdim, m = a.shape
    else:
        m, kdim = a.shape
    if tb:
        n, kb = b.shape
    else:
        kb, n = b.shape
    assert kdim == kb, (a.shape, b.shape)
    tm = _tile(m, (512, 256, 128))
    tn = _tile(n, (768, 512, 384, 256, 128))
    tk = _tile(kdim, (1024, 512, 256, 128))
    nk = kdim // tk
    dims = (((0 if ta else 1,), (1 if tb else 0,)), ((), ()))

    def body(*refs):
        if add is None:
            a_ref, b_ref, o_ref, acc_ref = refs
            add_ref = None
        else:
            a_ref, b_ref, add_ref, o_ref, acc_ref = refs
        k = pl.program_id(2)

        @pl.when(k == 0)
        def _():
            acc_ref[...] = jnp.zeros_like(acc_ref)

        acc_ref[...] += lax.dot_general(a_ref[...].astype(BF16), b_ref[...].astype(BF16), dims,
                                        preferred_element_type=F32)

        @pl.when(k == nk - 1)
        def _():
            r = acc_ref[...]
            if add_ref is not None:
                r = r + add_ref[...]
            o_ref[...] = r.astype(out_dtype)

    a_spec = pl.BlockSpec((tk, tm), lambda i, j, k: (k, i)) if ta else pl.BlockSpec((tm, tk), lambda i, j, k: (i, k))
    b_spec = pl.BlockSpec((tn, tk), lambda i, j, k: (j, k)) if tb else pl.BlockSpec((tk, tn), lambda i, j, k: (k, j))
    in_specs = [a_spec, b_spec]
    args = [a, b]
    if add is not None:
        in_specs.append(pl.BlockSpec((tm, tn), lambda i, j, k: (i, j)))
        args.append(add)
    return pl.pallas_call(
        body, name=name, grid=(m // tm, n // tn, nk),
        in_specs=in_specs, out_specs=pl.BlockSpec((tm, tn), lambda i, j, k: (i, j)),
        out_shape=jax.ShapeDtypeStruct((m, n), out_dtype),
        scratch_shapes=[pltpu.VMEM((tm, tn), F32)],
        compiler_params=_params(("parallel", "parallel", "arbitrary")),
    )(*args)


def _rmsnorm_fwd(x, g, *, width, col_blk, name):
    s = x.shape[0]
    tm = _tile(s, (256, 128))

    def body(x_ref, g_ref, y_ref):
        xf = x_ref[...].astype(F32)
        r = lax.rsqrt(jnp.mean(xf * xf, axis=-1, keepdims=True) + EPS)
        y_ref[...] = ((xf * r) * g_ref[...]).astype(BF16)

    return pl.pallas_call(
        body, name=name, grid=(s // tm,),
        in_specs=[pl.BlockSpec((tm, width), lambda i: (i, col_blk)), pl.BlockSpec((1, width), lambda i: (0, 0))],
        out_specs=pl.BlockSpec((tm, width), lambda i: (i, 0)),
        out_shape=jax.ShapeDtypeStruct((s, width), BF16),
        compiler_params=_params(("parallel",)),
    )(x, g)


def _rmsnorm_bwd(x, g, dy, *, width, col_blk, name, add=None, out_dtype=F32):
    s = x.shape[0]
    tm = _tile(s, (256, 128))

    def body(*refs):
        if add is None:
            x_ref, g_ref, dy_ref, dx_ref, dg_ref = refs
            add_ref = None
        else:
            x_ref, g_ref, dy_ref, add_ref, dx_ref, dg_ref = refs
        i = pl.program_id(0)
        xf = x_ref[...].astype(F32)
        r = lax.rsqrt(jnp.mean(xf * xf, axis=-1, keepdims=True) + EPS)
        xh = xf * r
        dyf = dy_ref[...].astype(F32)

        @pl.when(i == 0)
        def _():
            dg_ref[...] = jnp.zeros_like(dg_ref)

        dg_ref[...] += jnp.sum(dyf * xh, axis=0, keepdims=True)
        dxh = dyf * g_ref[...]
        dx = r * (dxh - xh * jnp.mean(dxh * xh, axis=-1, keepdims=True))
        if add_ref is not None:
            dx = dx + add_ref[...]
        dx_ref[...] = dx.astype(out_dtype)

    in_specs = [pl.BlockSpec((tm, width), lambda i: (i, col_blk)), pl.BlockSpec((1, width), lambda i: (0, 0)),
                pl.BlockSpec((tm, width), lambda i: (i, 0))]
    args = [x, g, dy]
    if add is not None:
        in_specs.append(pl.BlockSpec((tm, width), lambda i: (i, 0)))
        args.append(add)
    return pl.pallas_call(
        body, name=name, grid=(s // tm,),
        in_specs=in_specs,
        out_specs=[pl.BlockSpec((tm, width), lambda i: (i, 0)), pl.BlockSpec((1, width), lambda i: (0, 0))],
        out_shape=[jax.ShapeDtypeStruct((s, width), out_dtype), jax.ShapeDtypeStruct((1, width), F32)],
        compiler_params=_params(("arbitrary",)),
    )(*args)


def _sigmoid(x):
    return 1.0 / (1.0 + jnp.exp(-x))


def _gate_fwd(o_parts, gate, *, name):
    s = gate.shape[0]
    tm = _tile(s, (256, 128))
    n_o = len(o_parts)

    def body(*refs):
        o_refs, g_ref, y_ref = refs[:n_o], refs[n_o], refs[n_o + 1]
        o = o_refs[0][...] if n_o == 1 else jnp.concatenate([r[...] for r in o_refs], axis=1)
        gt = g_ref[...]
        y_ref[...] = (o * (gt * _sigmoid(gt))).astype(BF16)

    in_specs = [pl.BlockSpec((tm, o.shape[1]), lambda i: (i, 0)) for o in o_parts]
    in_specs.append(pl.BlockSpec((tm, D_MODEL), lambda i: (i, 0)))
    return pl.pallas_call(
        body, name=name, grid=(s // tm,), in_specs=in_specs,
        out_specs=pl.BlockSpec((tm, D_MODEL), lambda i: (i, 0)),
        out_shape=jax.ShapeDtypeStruct((s, D_MODEL), BF16),
        compiler_params=_params(("parallel",)),
    )(*o_parts, gate)


def _gate_bwd(d_og, o_parts, gate, *, name):
    s = gate.shape[0]
    tm = _tile(s, (256, 128))
    n_o = len(o_parts)
    widths = [o.shape[1] for o in o_parts]

    def body(*refs):
        d_ref, o_refs, g_ref = refs[0], refs[1:1 + n_o], refs[1 + n_o]
        do_refs, dg_ref = refs[2 + n_o:2 + 2 * n_o], refs[2 + 2 * n_o]
        d = d_ref[...]
        gt = g_ref[...]
        sg = _sigmoid(gt)
        silu = gt * sg
        dsilu = sg * (1.0 + gt * (1.0 - sg))
        o = o_refs[0][...] if n_o == 1 else jnp.concatenate([r[...] for r in o_refs], axis=1)
        dg_ref[...] = (d * o * dsilu).astype(BF16)
        do = d * silu
        off = 0
        for r, w in zip(do_refs, widths):
            r[...] = do[:, off:off + w]
            off += w

    in_specs = [pl.BlockSpec((tm, D_MODEL), lambda i: (i, 0))]
    in_specs += [pl.BlockSpec((tm, w), lambda i: (i, 0)) for w in widths]
    in_specs.append(pl.BlockSpec((tm, D_MODEL), lambda i: (i, 0)))
    out_specs = [pl.BlockSpec((tm, w), lambda i: (i, 0)) for w in widths]
    out_specs.append(pl.BlockSpec((tm, D_MODEL), lambda i: (i, 0)))
    out_shape = [jax.ShapeDtypeStruct((s, w), F32) for w in widths]
    out_shape.append(jax.ShapeDtypeStruct((s, D_MODEL), BF16))
    return pl.pallas_call(
        body, name=name, grid=(s // tm,), in_specs=in_specs, out_specs=out_specs, out_shape=out_shape,
        compiler_params=_params(("parallel",)),
    )(d_og, *o_parts, gate)


def _rot_half(x):
    lane = lax.broadcasted_iota(jnp.int32, x.shape, 1)
    return jnp.where(lane < 80, pltpu.roll(x, LANES - 16, axis=1), pltpu.roll(x, 16, axis=1))


def _rot_half_t(g):
    lane = lax.broadcasted_iota(jnp.int32, g.shape, 1)
    lo = (lane >= MLA_NOPE) & (lane < MLA_NOPE + MLA_ROPE // 2)
    hi = (lane >= MLA_NOPE + MLA_ROPE // 2) & (lane < MLA_NOPE + MLA_ROPE)
    return jnp.where(lo, pltpu.roll(g, LANES - 16, axis=1), jnp.where(hi, pltpu.roll(g, 16, axis=1), 0.0))


def _rope_fwd(qp, kvp, z0a, cos_t, sin_t, *, name):
    s = qp.shape[0]
    tm = _tile(s, (256, 128))
    hw = MLA_HEADS * LANES

    def body(q_ref, k_ref, kpe_ref, c_ref, s_ref, qm_ref, km_ref):
        c = c_ref[...]
        sn = s_ref[...]
        kpe = kpe_ref[...]
        kpe_r = (kpe * c + _rot_half(kpe) * sn).astype(BF16)
        lane = lax.broadcasted_iota(jnp.int32, kpe.shape, 1)
        for h in range(MLA_HEADS):
            sl = slice(h * LANES, (h + 1) * LANES)
            qh = q_ref[:, sl]
            qm_ref[:, sl] = (qh * c + _rot_half(qh) * sn).astype(BF16)
            km_ref[:, sl] = jnp.where(lane < MLA_NOPE, k_ref[:, sl], kpe_r)

    return pl.pallas_call(
        body, name=name, grid=(s // tm,),
        in_specs=[pl.BlockSpec((tm, hw), lambda i: (i, 0)), pl.BlockSpec((tm, hw), lambda i: (i, 0)),
                  pl.BlockSpec((tm, LANES), lambda i: (i, 11)),
                  pl.BlockSpec((tm, LANES), lambda i: (i, 0)), pl.BlockSpec((tm, LANES), lambda i: (i, 0))],
        out_specs=[pl.BlockSpec((tm, hw), lambda i: (i, 0)), pl.BlockSpec((tm, hw), lambda i: (i, 0))],
        out_shape=[jax.ShapeDtypeStruct((s, hw), BF16), jax.ShapeDtypeStruct((s, hw), BF16)],
        compiler_params=_params(("parallel",)),
    )(qp, kvp, z0a, cos_t, sin_t)


def _rope_bwd(dqm, dkm, cos_t, sin_t, *, name):
    s = dqm.shape[0]
    tm = _tile(s, (256, 128))
    hw = MLA_HEADS * LANES

    def body(dq_ref, dk_ref, c_ref, s_ref, dqp_ref, dkpe_ref):
        c = c_ref[...]
        sn = s_ref[...]
        ksum = jnp.zeros((tm, LANES), F32)
        for h in range(MLA_HEADS):
            sl = slice(h * LANES, (h + 1) * LANES)
            dq = dq_ref[:, sl]
            dqp_ref[:, sl] = (dq * c + _rot_half_t(dq * sn)).astype(BF16)
            ksum = ksum + dk_ref[:, sl]
        lane = lax.broadcasted_iota(jnp.int32, ksum.shape, 1)
        dkpe = ksum * c + _rot_half_t(ksum * sn)
        dkpe_ref[...] = jnp.where((lane >= MLA_NOPE) & (lane < MLA_NOPE + MLA_ROPE), dkpe, 0.0).astype(BF16)

    return pl.pallas_call(
        body, name=name, grid=(s // tm,),
        in_specs=[pl.BlockSpec((tm, hw), lambda i: (i, 0)), pl.BlockSpec((tm, hw), lambda i: (i, 0)),
                  pl.BlockSpec((tm, LANES), lambda i: (i, 0)), pl.BlockSpec((tm, LANES), lambda i: (i, 0))],
        out_specs=[pl.BlockSpec((tm, hw), lambda i: (i, 0)), pl.BlockSpec((tm, LANES), lambda i: (i, 0))],
        out_shape=[jax.ShapeDtypeStruct((s, hw), BF16), jax.ShapeDtypeStruct((s, LANES), BF16)],
        compiler_params=_params(("parallel",)),
    )(dqm, dkm, cos_t, sin_t)


def _head_mask(shape, a):
    lane = lax.broadcasted_iota(jnp.int32, shape, 1)
    return (lane >= 64 * a) & (lane < 64 * (a + 1))


def _causal_mask(t):
    row = lax.broadcasted_iota(jnp.int32, (t, t), 0)
    col = lax.broadcasted_iota(jnp.int32, (t, t), 1)
    return col <= row


_NT = (((1,), (1,)), ((), ()))
LOG2E = 1.4426950408889634


def _stack_heads(tile, hw):
    lane = lax.broadcasted_iota(jnp.int32, tile.shape, 1)
    z = jnp.zeros_like(tile)
    return jnp.concatenate([jnp.where(lane < hw, tile, z), jnp.where(lane >= hw, tile, z)], axis=0)


def _stacked_rows(r0, r1, t):
    n = r0.shape[-1]
    return jnp.concatenate([jnp.broadcast_to(r0, (t, n)), jnp.broadcast_to(r1, (t, n))], axis=0)


def _stacked_causal_mask(t):
    m = _causal_mask(t)
    return jnp.concatenate([m, m], axis=0)


def _resident(block, index_map):
    return pl.BlockSpec(block, index_map, pipeline_mode=pl.Buffered(1))


def _flash_fwd(q, k, v, bias, *, n_pairs, hw, q_off, k_off, v_off, scale, name):
    s = q.shape[0]
    t = min(ATT_T, s)
    nb = s // t
    qw = 2 * hw
    has_bias = bias is not None
    c1 = scale * LOG2E

    def body(*refs):
        if has_bias:
            q_ref, k_ref, v_ref, b_ref, o_ref, lse_ref = refs
        else:
            q_ref, k_ref, v_ref, o_ref, lse_ref = refs
            b_ref = None
        cmask = _stacked_causal_mask(t)
        lane_lt64 = lax.broadcasted_iota(jnp.int32, (t, LANES), 1) < 64

        def q_block(i, _):
            r0 = pl.multiple_of(i * t, t)
            qs = _stack_heads(q_ref[pl.ds(r0, t), :], hw)

            def kv_step(j, carry, masked):
                m, l, acc = carry
                c0 = pl.multiple_of(j * t, t)
                sc = lax.dot_general(qs, k_ref[pl.ds(c0, t), :], _NT, preferred_element_type=F32) * c1
                if has_bias:
                    sc = sc + _stacked_rows(b_ref[0, 0, j], b_ref[0, 1, j], t)
                if masked:
                    sc = jnp.where(cmask, sc, NEG_INF)
                m_new = jnp.maximum(m, jnp.max(sc, axis=-1, keepdims=True))
                alpha = jnp.exp2(m - m_new)
                p = jnp.exp2(sc - m_new)
                l_new = alpha * l + jnp.sum(p, axis=-1, keepdims=True)
                pv = jnp.dot(p.astype(BF16), v_ref[pl.ds(c0, t), :], preferred_element_type=F32)
                return m_new, l_new, alpha * acc + pv

            init = (jnp.full((2 * t, 1), NEG_INF, F32), jnp.zeros((2 * t, 1), F32), jnp.zeros((2 * t, LANES), F32))
            carry = lax.fori_loop(0, i, functools.partial(kv_step, masked=False), init)
            m, l, acc = kv_step(i, carry, True)
            out = acc / l
            lse2 = m + jnp.log2(l)
            lse_ref[0, 0, pl.ds(r0, t), :] = lse2[:t]
            lse_ref[0, 1, pl.ds(r0, t), :] = lse2[t:]
            o_ref[pl.ds(r0, t), :] = jnp.where(lane_lt64, out[:t], out[t:])
            return 0

        lax.fori_loop(0, nb, q_block, 0)

    in_specs = [_resident((s, qw), lambda p: (0, q_off + p)), _resident((s, qw), lambda p: (0, k_off + p)),
                _resident((s, LANES), lambda p: (0, v_off + p))]
    args = [q, k, v]
    if has_bias:
        in_specs.append(_resident((1, 2, nb, 1, t), lambda p: (p, 0, 0, 0, 0)))
        args.append(bias)
    return pl.pallas_call(
        body, name=name, grid=(n_pairs,), in_specs=in_specs,
        out_specs=[pl.BlockSpec((s, LANES), lambda p: (0, p)), pl.BlockSpec((1, 2, s, 1), lambda p: (p, 0, 0, 0))],
        out_shape=[jax.ShapeDtypeStruct((s, n_pairs * LANES), F32), jax.ShapeDtypeStruct((n_pairs, 2, s, 1), F32)],
        compiler_params=_params(("parallel",), VMEM_LIMIT),
    )(*args)


def _flash_bwd(q, k, v, do, o, lse, bias, *, n_pairs, hw, q_off, k_off, v_off, scale, name):
    s = q.shape[0]
    t = min(ATT_T, s)
    nb = s // t
    qw = 2 * hw
    has_bias = bias is not None
    c1 = scale * LOG2E

    def body(*refs):
        if has_bias:
            q_ref, k_ref, v_ref, do_ref, o_ref, lse_ref, b_ref, dq_ref, dkt_ref, dvt_ref, db_ref, dr_ref = refs
            db_ref[...] = jnp.zeros_like(db_ref)
        else:
            q_ref, k_ref, v_ref, do_ref, o_ref, lse_ref, dq_ref, dkt_ref, dvt_ref = refs
            b_ref = db_ref = dr_ref = None
        dkt_ref[...] = jnp.zeros_like(dkt_ref)
        dvt_ref[...] = jnp.zeros_like(dvt_ref)
        cmask = _stacked_causal_mask(t)
        lane_lt_hw = lax.broadcasted_iota(jnp.int32, (t, qw), 1) < hw

        def q_block(i, _):
            r0 = pl.multiple_of(i * t, t)
            qs = _stack_heads(q_ref[pl.ds(r0, t), :], hw)
            dos = _stack_heads(do_ref[pl.ds(r0, t), :], 64)
            ot = o_ref[pl.ds(r0, t), :]
            delta = jnp.sum(dos * jnp.concatenate([ot, ot], axis=0), axis=-1, keepdims=True)
            lse2 = jnp.concatenate([lse_ref[0, 0, pl.ds(r0, t), :], lse_ref[0, 1, pl.ds(r0, t), :]], axis=0)
            dosb = dos.astype(BF16)
            dos_t = dos.T.astype(BF16)
            qs_t = qs.astype(F32).T.astype(BF16)

            def kv_step(j, carry, masked):
                dq, rsum = carry
                c0 = pl.multiple_of(j * t, t)
                kt = k_ref[pl.ds(c0, t), :]
                vt = v_ref[pl.ds(c0, t), :]
                sc = lax.dot_general(qs, kt, _NT, preferred_element_type=F32) * c1
                if has_bias:
                    sc = sc + _stacked_rows(b_ref[0, 0, j], b_ref[0, 1, j], t)
                if masked:
                    sc = jnp.where(cmask, sc, NEG_INF)
                p = jnp.exp2(sc - lse2)
                dp = lax.dot_general(dosb, vt, _NT, preferred_element_type=F32)
                ds = p * (dp - delta)
                dsb = ds.astype(BF16)
                dvt_ref[0, j] += jnp.dot(dos_t, p.astype(BF16), preferred_element_type=F32)
                dkt_ref[0, j] += jnp.dot(qs_t, dsb, preferred_element_type=F32) * scale
                if has_bias:
                    db_ref[0, 0, j] += jnp.sum(ds[:t], axis=0, keepdims=True)
                    db_ref[0, 1, j] += jnp.sum(ds[t:], axis=0, keepdims=True)
                    rsum = rsum + jnp.sum(ds, axis=-1, keepdims=True)
                return dq + jnp.dot(dsb, kt, preferred_element_type=F32), rsum

            init = (jnp.zeros((2 * t, qw), F32), jnp.zeros((2 * t, 1), F32))
            carry = lax.fori_loop(0, i, functools.partial(kv_step, masked=False), init)
            dq, rsum = kv_step(i, carry, True)
            dq = dq * scale
            dq_ref[pl.ds(r0, t), :] = jnp.where(lane_lt_hw, dq[:t], dq[t:])
            if has_bias:
                dr_ref[0, 0, pl.ds(r0, t), :] = rsum[:t]
                dr_ref[0, 1, pl.ds(r0, t), :] = rsum[t:]
            return 0

        lax.fori_loop(0, nb, q_block, 0)

    in_specs = [_resident((s, qw), lambda p: (0, q_off + p)), _resident((s, qw), lambda p: (0, k_off + p)),
                _resident((s, LANES), lambda p: (0, v_off + p)),
                _resident((s, LANES), lambda p: (0, p)), _resident((s, LANES), lambda p: (0, p)),
                _resident((1, 2, s, 1), lambda p: (p, 0, 0, 0))]
    args = [q, k, v, do, o, lse]
    out_specs = [pl.BlockSpec((s, qw), lambda p: (0, p)),
                 pl.BlockSpec((1, nb, qw, t), lambda p: (p, 0, 0, 0)),
                 pl.BlockSpec((1, nb, LANES, t), lambda p: (p, 0, 0, 0))]
    out_shape = [jax.ShapeDtypeStruct((s, n_pairs * qw), F32),
                 jax.ShapeDtypeStruct((n_pairs, nb, qw, t), F32),
                 jax.ShapeDtypeStruct((n_pairs, nb, LANES, t), F32)]
    if has_bias:
        in_specs.append(_resident((1, 2, nb, 1, t), lambda p: (p, 0, 0, 0, 0)))
        args.append(bias)
        out_specs.append(pl.BlockSpec((1, 2, nb, 1, t), lambda p: (p, 0, 0, 0, 0)))
        out_shape.append(jax.ShapeDtypeStruct((n_pairs, 2, nb, 1, t), F32))
        out_specs.append(pl.BlockSpec((1, 2, s, 1), lambda p: (p, 0, 0, 0)))
        out_shape.append(jax.ShapeDtypeStruct((n_pairs, 2, s, 1), F32))
    return pl.pallas_call(
        body, name=name, grid=(n_pairs,), in_specs=in_specs, out_specs=out_specs, out_shape=out_shape,
        compiler_params=_params(("parallel",), VMEM_LIMIT),
    )(*args)


def _untranspose(xt):
    p, nb, w, t = xt.shape
    return xt.transpose(1, 3, 0, 2).reshape(nb * t, p * w)


def _alibi_slope(h):
    return 2.0 ** (-8.0 * (h + 1.0) / SWA_HEADS)


SWA_ROWS = 512
SWA_SCALE = SWA_DIM ** -0.5


def _swa_geometry(i):
    w = WINDOW
    r0 = pl.multiple_of(i * w, w)
    b0 = pl.multiple_of(jnp.maximum(i - 1, 0) * w, w)
    row = lax.broadcasted_iota(jnp.int32, (w, 2 * w), 0)
    col = lax.broadcasted_iota(jnp.int32, (w, 2 * w), 1)
    dist = row - col + (r0 - b0)
    valid = (dist >= 0) & (dist < w)
    return r0, b0, dist.astype(F32), valid


def _swa_q_head(qblk, h):
    kv = h // (SWA_HEADS // SWA_KV_HEADS)
    if h % 2 != kv:
        qblk = pltpu.roll(qblk, 64, axis=1)
    return jnp.where(_head_mask(qblk.shape, kv), qblk, 0.0)


def _swa_fwd(z0b, sinks, *, name):
    s = z0b.shape[0]
    w = WINDOW
    rows = min(SWA_ROWS, s)
    per_step = rows // w
    qcols = SWA_HEADS * SWA_DIM

    def body(sink_ref, q_ref, k_ref, v_ref, o_ref, lse_ref):
        g = pl.program_id(0)
        for ii in range(per_step):
            r0, b0, dist, valid = _swa_geometry(g * per_step + ii)
            kb = k_ref[pl.ds(b0, 2 * w), :]
            vb = v_ref[pl.ds(b0, 2 * w), :]
            o_heads = []
            for h in range(SWA_HEADS):
                kv = h // (SWA_HEADS // SWA_KV_HEADS)
                blk = h // 2
                qh = _swa_q_head(q_ref[ii * w:(ii + 1) * w, blk * LANES:(blk + 1) * LANES].astype(F32), h).astype(BF16)
                sc = lax.dot_general(qh, kb, _NT, preferred_element_type=F32) * SWA_SCALE - _alibi_slope(h) * dist
                sc = jnp.where(valid, sc, NEG_INF)
                sink = sink_ref[0, h]
                m = jnp.maximum(jnp.max(sc, axis=-1, keepdims=True), sink)
                p = jnp.exp(sc - m)
                l = jnp.sum(p, axis=-1, keepdims=True) + jnp.exp(sink - m)
                oh = jnp.dot(p.astype(BF16), vb, preferred_element_type=F32) / l
                if h % 2 != kv:
                    oh = pltpu.roll(oh, 64, axis=1)
                o_heads.append(oh)
                lse_ref[h, ii * w:(ii + 1) * w, :] = m + jnp.log(l)
            lt64 = lax.broadcasted_iota(jnp.int32, (w, LANES), 1) < 64
            o_ref[ii * w:(ii + 1) * w, :] = jnp.concatenate(
                [jnp.where(lt64, o_heads[2 * b], o_heads[2 * b + 1]) for b in range(SWA_HEADS // 2)], axis=1)

    return pl.pallas_call(
        body, name=name, grid=(s // rows,),
        in_specs=[pl.BlockSpec(memory_space=pltpu.SMEM),
                  pl.BlockSpec((rows, qcols), lambda g: (g, 0)),
                  pl.BlockSpec((s, LANES), lambda g: (0, 4)), pl.BlockSpec((s, LANES), lambda g: (0, 5))],
        out_specs=[pl.BlockSpec((rows, qcols), lambda g: (g, 0)), pl.BlockSpec((SWA_HEADS, rows, 1), lambda g: (0, g, 0))],
        out_shape=[jax.ShapeDtypeStruct((s, qcols), F32), jax.ShapeDtypeStruct((SWA_HEADS, s, 1), F32)],
        compiler_params=_params(("parallel",), VMEM_LIMIT),
    )(sinks, z0b, z0b, z0b)


def _swa_bwd(z0b, sinks, do, o, lse, *, name):
    s = z0b.shape[0]
    w = WINDOW
    rows = min(SWA_ROWS, s)
    per_step = rows // w
    qcols = SWA_HEADS * SWA_DIM
    nblk = s // w

    def body(sink_ref, q_ref, k_ref, v_ref, do_ref, o_ref, lse_ref, dq_ref, dkt_ref, dvt_ref, dsink_ref):
        g = pl.program_id(0)

        @pl.when(g == 0)
        def _():
            dkt_ref[...] = jnp.zeros_like(dkt_ref)
            dvt_ref[...] = jnp.zeros_like(dvt_ref)
            dsink_ref[...] = jnp.zeros_like(dsink_ref)

        for ii in range(per_step):
            i = g * per_step + ii
            r0, b0, dist, valid = _swa_geometry(i)
            j0 = jnp.maximum(i - 1, 0)
            kb = k_ref[pl.ds(b0, 2 * w), :]
            vb = v_ref[pl.ds(b0, 2 * w), :]
            dq_heads = []
            for h in range(SWA_HEADS):
                kv = h // (SWA_HEADS // SWA_KV_HEADS)
                blk = h // 2
                cs = slice(blk * LANES, (blk + 1) * LANES)
                rs = slice(ii * w, (ii + 1) * w)
                qh32 = _swa_q_head(q_ref[rs, cs].astype(F32), h)
                qh = qh32.astype(BF16)
                doh32 = _swa_q_head(do_ref[rs, cs], h)
                oh32 = _swa_q_head(o_ref[rs, cs], h)
                delta = jnp.sum(doh32 * oh32, axis=-1, keepdims=True)
                lse = lse_ref[h, rs, :]
                sink = sink_ref[0, h]
                sc = lax.dot_general(qh, kb, _NT, preferred_element_type=F32) * SWA_SCALE - _alibi_slope(h) * dist
                sc = jnp.where(valid, sc, NEG_INF)
                p = jnp.exp(sc - lse)
                dp = lax.dot_general(doh32.astype(BF16), vb, _NT, preferred_element_type=F32)
                ds = p * (dp - delta)
                dsb = ds.astype(BF16)
                pb = p.astype(BF16)
                dsink_ref[h:h + 1, :] += jnp.broadcast_to(-jnp.sum(jnp.exp(sink - lse) * delta), (1, LANES))
                do_t = doh32.T.astype(BF16)
                q_t = qh32.T.astype(BF16)
                dvt = jnp.dot(do_t, pb, preferred_element_type=F32)
                dkt = jnp.dot(q_t, dsb, preferred_element_type=F32) * SWA_SCALE
                dvt_ref[j0] += dvt[:, :w]
                dvt_ref[j0 + 1] += dvt[:, w:]
                dkt_ref[j0] += dkt[:, :w]
                dkt_ref[j0 + 1] += dkt[:, w:]
                dq = jnp.dot(dsb, kb, preferred_element_type=F32) * SWA_SCALE
                if h % 2 != kv:
                    dq = pltpu.roll(dq, 64, axis=1)
                dq_heads.append(dq)
            lt64 = lax.broadcasted_iota(jnp.int32, (w, LANES), 1) < 64
            dq_ref[ii * w:(ii + 1) * w, :] = jnp.concatenate(
                [jnp.where(lt64, dq_heads[2 * b], dq_heads[2 * b + 1]) for b in range(SWA_HEADS // 2)], axis=1)

    return pl.pallas_call(
        body, name=name, grid=(s // rows,),
        in_specs=[pl.BlockSpec(memory_space=pltpu.SMEM),
                  pl.BlockSpec((rows, qcols), lambda g: (g, 0)),
                  pl.BlockSpec((s, LANES), lambda g: (0, 4)), pl.BlockSpec((s, LANES), lambda g: (0, 5)),
                  pl.BlockSpec((rows, qcols), lambda g: (g, 0)), pl.BlockSpec((rows, qcols), lambda g: (g, 0)),
                  pl.BlockSpec((SWA_HEADS, rows, 1), lambda g: (0, g, 0))],
        out_specs=[pl.BlockSpec((rows, qcols), lambda g: (g, 0)),
                   pl.BlockSpec((nblk, LANES, w), lambda g: (0, 0, 0)),
                   pl.BlockSpec((nblk, LANES, w), lambda g: (0, 0, 0)),
                   pl.BlockSpec((SWA_HEADS, LANES), lambda g: (0, 0))],
        out_shape=[jax.ShapeDtypeStruct((s, qcols), F32),
                   jax.ShapeDtypeStruct((nblk, LANES, w), F32), jax.ShapeDtypeStruct((nblk, LANES, w), F32),
                   jax.ShapeDtypeStruct((SWA_HEADS, LANES), F32)],
        compiler_params=_params(("arbitrary",), VMEM_LIMIT),
    )(sinks, z0b, z0b, z0b, do, o, lse)


CUM_T = 256


def _split3(x):
    hi = x.astype(BF16)
    r1 = x - hi.astype(F32)
    mid = r1.astype(BF16)
    lo = (r1 - mid.astype(F32)).astype(BF16)
    return hi, mid, lo


def _tri_dot(tri, x):
    hi, mid, lo = _split3(x)
    out = jnp.dot(tri, hi, preferred_element_type=F32)
    out = out + jnp.dot(tri, mid, preferred_element_type=F32)
    return out + jnp.dot(tri, lo, preferred_element_type=F32)


def _logf_fwd(zf, bf, *, name):
    s = zf.shape[0]
    t = CUM_T
    nb = s // t

    def body(z_ref, b_ref, c_ref, carry_ref):
        i = pl.program_id(0)

        @pl.when(i == 0)
        def _():
            carry_ref[...] = jnp.zeros_like(carry_ref)

        x = z_ref[...] + b_ref[...]
        lf = jnp.minimum(x, 0.0) - jnp.log(1.0 + jnp.exp(-jnp.abs(x)))
        row = lax.broadcasted_iota(jnp.int32, (t, t), 0)
        col = lax.broadcasted_iota(jnp.int32, (t, t), 1)
        tri = jnp.where(col <= row, 1.0, 0.0).astype(BF16)
        c = _tri_dot(tri, lf) + carry_ref[...]
        c_ref[...] = c
        carry_ref[...] = c[t - 1:t, :]

    return pl.pallas_call(
        body, name=name, grid=(nb,),
        in_specs=[pl.BlockSpec((t, LANES), lambda i: (i, 0)), pl.BlockSpec((1, LANES), lambda i: (0, 0))],
        out_specs=pl.BlockSpec((t, LANES), lambda i: (i, 0)),
        out_shape=jax.ShapeDtypeStruct((s, LANES), F32),
        scratch_shapes=[pltpu.VMEM((1, LANES), F32)],
        compiler_params=_params(("arbitrary",)),
    )(zf, bf)


def _logf_bwd(dc, zf, bf, *, name):
    s = zf.shape[0]
    t = CUM_T
    nb = s // t

    def body(dc_ref, z_ref, b_ref, dz_ref, db_ref, carry_ref):
        i = pl.program_id(0)

        @pl.when(i == 0)
        def _():
            carry_ref[...] = jnp.zeros_like(carry_ref)
            db_ref[...] = jnp.zeros_like(db_ref)

        row = lax.broadcasted_iota(jnp.int32, (t, t), 0)
        col = lax.broadcasted_iota(jnp.int32, (t, t), 1)
        tri = jnp.where(col >= row, 1.0, 0.0).astype(BF16)
        dlf = _tri_dot(tri, dc_ref[...]) + carry_ref[...]
        carry_ref[...] = dlf[0:1, :]
        x = z_ref[...] + b_ref[...]
        dz = dlf * _sigmoid(-x)
        dz_ref[...] = dz.astype(BF16)
        db_ref[...] += jnp.sum(dz, axis=0, keepdims=True)

    return pl.pallas_call(
        body, name=name, grid=(nb,),
        in_specs=[pl.BlockSpec((t, LANES), lambda i: (nb - 1 - i, 0)), pl.BlockSpec((t, LANES), lambda i: (nb - 1 - i, 0)),
                  pl.BlockSpec((1, LANES), lambda i: (0, 0))],
        out_specs=[pl.BlockSpec((t, LANES), lambda i: (nb - 1 - i, 0)), pl.BlockSpec((1, LANES), lambda i: (0, 0))],
        out_shape=[jax.ShapeDtypeStruct((s, LANES), BF16), jax.ShapeDtypeStruct((1, LANES), F32)],
        scratch_shapes=[pltpu.VMEM((1, LANES), F32)],
        compiler_params=_params(("arbitrary",)),
    )(dc, zf, bf)


def _loss_head(x2, g, target, *, name):
    s = x2.shape[0]
    tm = _tile(s, (256, 128))

    def body(x_ref, g_ref, t_ref, dx_ref, loss_ref, dg_ref):
        i = pl.program_id(0)

        @pl.when(i == 0)
        def _():
            loss_ref[...] = jnp.zeros_like(loss_ref)
            dg_ref[...] = jnp.zeros_like(dg_ref)

        xf = x_ref[...]
        r = lax.rsqrt(jnp.mean(xf * xf, axis=-1, keepdims=True) + EPS)
        xh = xf * r
        gv = g_ref[...]
        err = xh * gv - t_ref[...]
        loss_ref[...] += jnp.broadcast_to(0.5 * jnp.sum(jnp.mean(err * err, axis=-1, keepdims=True)), loss_ref.shape)
        dy = err * (1.0 / D_MODEL)
        dg_ref[...] += jnp.sum(dy * xh, axis=0, keepdims=True)
        dxh = dy * gv
        dx_ref[...] = r * (dxh - xh * jnp.mean(dxh * xh, axis=-1, keepdims=True))

    return pl.pallas_call(
        body, name=name, grid=(s // tm,),
        in_specs=[pl.BlockSpec((tm, D_MODEL), lambda i: (i, 0)), pl.BlockSpec((1, D_MODEL), lambda i: (0, 0)),
                  pl.BlockSpec((tm, D_MODEL), lambda i: (i, 0))],
        out_specs=[pl.BlockSpec((tm, D_MODEL), lambda i: (i, 0)), pl.BlockSpec((8, LANES), lambda i: (0, 0)),
                   pl.BlockSpec((1, D_MODEL), lambda i: (0, 0))],
        out_shape=[jax.ShapeDtypeStruct((s, D_MODEL), F32), jax.ShapeDtypeStruct((8, LANES), F32),
                   jax.ShapeDtypeStruct((1, D_MODEL), F32)],
        compiler_params=_params(("arbitrary",)),
    )(x2, g, target)


def _adamw(pieces, w, m, v, *, name):
    rows = w.shape[0]
    tr = _tile(rows, (544, 256, 24))
    bc1 = 1.0 - ADAM_B1 ** ADAM_STEP
    bc2 = 1.0 - ADAM_B2 ** ADAM_STEP

    def body(p_ref, w_ref, m_ref, v_ref, g_ref, d_ref, nm_ref, nv_ref):
        g = p_ref[0].astype(F32)
        for k in range(1, N_DEV):
            g = g + p_ref[k].astype(F32)
        nm = ADAM_B1 * m_ref[...] + (1.0 - ADAM_B1) * g
        nv = ADAM_B2 * v_ref[...] + (1.0 - ADAM_B2) * (g * g)
        m_hat = nm / bc1
        v_hat = nv / bc2
        g_ref[...] = g
        d_ref[...] = -ADAM_LR * (m_hat / (jnp.sqrt(v_hat) + ADAM_EPS) + ADAM_WD * w_ref[...])
        nm_ref[...] = nm
        nv_ref[...] = nv

    spec = pl.BlockSpec((tr, LANES), lambda i: (i, 0))
    shape = jax.ShapeDtypeStruct((rows, LANES), F32)
    return pl.pallas_call(
        body, name=name, grid=(rows // tr,),
        in_specs=[pl.BlockSpec((N_DEV, tr, LANES), lambda i: (0, i, 0)), spec, spec, spec],
        out_specs=[spec, spec, spec, spec], out_shape=[shape, shape, shape, shape],
        compiler_params=_params(("parallel",)),
    )(pieces, w, m, v)


MESH = pl.DeviceIdType.MESH
ANY = pl.BlockSpec(memory_space=pl.ANY)


def _all_gather(shard, *, name):
    rows, lanes = shard.shape

    def body(x_ref, out_ref, send_sems, recv_sems, local_sem):
        x, y, c = lax.axis_index("x"), lax.axis_index("y"), lax.axis_index("c")
        me, sibling = (x, y, c), (x, y, 1 - c)
        chips = [(1 - x, y), (x, 1 - y), (1 - x, 1 - y)]

        def block(px, py, pc):
            return out_ref.at[4 * px + 2 * py + pc]

        def copy(k, blk, to, src=None):
            return pltpu.make_async_remote_copy(
                src_ref=block(*blk) if src is None else src, dst_ref=block(*blk),
                send_sem=send_sems.at[k], recv_sem=recv_sems.at[k], device_id=to, device_id_type=MESH)

        mine = pltpu.make_async_copy(x_ref, block(*me), local_sem)
        mine.start()
        first = [copy(0, me, sibling, src=x_ref)]
        first += [copy(1 + j, me, (*chip, c), src=x_ref) for j, chip in enumerate(chips)]
        for cp in first:
            cp.start()
        passed = [copy(4 + j, (*chip, c), sibling) for j, chip in enumerate(chips)]
        for j, chip in enumerate(chips):
            copy(1 + j, (*chip, c), me).wait_recv()
            passed[j].start()
        copy(0, sibling, me).wait_recv()
        for j, chip in enumerate(chips):
            copy(4 + j, (*chip, 1 - c), me).wait_recv()
        for cp in first + passed:
            cp.wait_send()
        mine.wait()

    return pl.pallas_call(
        body, name=name, out_shape=jax.ShapeDtypeStruct((N_DEV, rows, lanes), shard.dtype),
        in_specs=[ANY], out_specs=ANY,
        scratch_shapes=[pltpu.SemaphoreType.DMA((7,)), pltpu.SemaphoreType.DMA((7,)), pltpu.SemaphoreType.DMA(())],
    )(shard)


def _exchange(pieces, *, name):
    def body(g_ref, out_ref, send_sems, recv_sems, local_sem):
        x, y, c = lax.axis_index("x"), lax.axis_index("y"), lax.axis_index("c")
        me = 4 * x + 2 * y + c
        mine = pltpu.make_async_copy(g_ref.at[me], out_ref.at[me], local_sem)
        mine.start()
        copies = []
        for r in range(1, N_DEV):
            px = 1 - x if r & 4 else x
            py = 1 - y if r & 2 else y
            pc = 1 - c if r & 1 else c
            cp = pltpu.make_async_remote_copy(
                src_ref=g_ref.at[4 * px + 2 * py + pc], dst_ref=out_ref.at[me],
                send_sem=send_sems.at[r - 1], recv_sem=recv_sems.at[r - 1],
                device_id=(px, py, pc), device_id_type=MESH)
            cp.start()
            copies.append(cp)
        for cp in copies:
            cp.wait()
        mine.wait()

    return pl.pallas_call(
        body, name=name, out_shape=jax.ShapeDtypeStruct(pieces.shape, pieces.dtype),
        in_specs=[ANY], out_specs=ANY,
        scratch_shapes=[pltpu.SemaphoreType.DMA((7,)), pltpu.SemaphoreType.DMA((7,)), pltpu.SemaphoreType.DMA(())],
    )(pieces)


def _cols_to_rows(w_shard):
    return w_shard.reshape(-1, LANES)


def _gathered_cols(gath, off, nrows, kdim):
    n = nrows * LANES // kdim
    return gath[:, off:off + nrows, :].reshape(N_DEV, kdim, n).transpose(1, 0, 2).reshape(kdim, N_DEV * n)


def _gathered_rows(gath, off, nrows, ncols):
    return gath[:, off:off + nrows, :].reshape(-1, ncols)


def _scatter_cols(dw):
    kdim, n8 = dw.shape
    n = n8 // N_DEV
    return dw.reshape(kdim, N_DEV, n).transpose(1, 0, 2).reshape(N_DEV, kdim * n // LANES, LANES)


def _scatter_rows(dw):
    return dw.reshape(N_DEV, -1, LANES)


def _layer0_in_weight(w_in):
    cq, ckv, kpe = w_in[:, 0:256], w_in[:, 256:384], w_in[:, 384:416]
    q_s, k_s, v_s, gate = w_in[:, 416:928], w_in[:, 928:1056], w_in[:, 1056:1184], w_in[:, 1184:2208]
    z = jnp.zeros((w_in.shape[0], 64), w_in.dtype)
    return jnp.concatenate([gate, cq, ckv, z, kpe, z[:, :32], q_s, k_s, v_s], axis=1)


def _layer0_in_grad(dwp):
    gate, cq, ckv, kpe = dwp[:, 0:1024], dwp[:, 1024:1280], dwp[:, 1280:1408], dwp[:, 1472:1504]
    q_s, k_s, v_s = dwp[:, 1536:2048], dwp[:, 2048:2176], dwp[:, 2176:2304]
    return jnp.concatenate([cq, ckv, kpe, q_s, k_s, v_s, gate], axis=1)


def _q_up_weight(w):
    return jnp.pad(w.reshape(MLA_Q_RANK, MLA_HEADS, 96), ((0, 0), (0, 0), (0, 32))).reshape(MLA_Q_RANK, MLA_HEADS * LANES)


def _q_up_grad(dwp):
    return dwp.reshape(MLA_Q_RANK, MLA_HEADS, LANES)[:, :, :96].reshape(MLA_Q_RANK, MLA_HEADS * 96)


def _kv_up_weight(w):
    w4 = w.reshape(MLA_KV_RANK, MLA_HEADS, 2, 64)
    kp = jnp.pad(w4[:, :, 0, :], ((0, 0), (0, 0), (0, 64))).reshape(MLA_KV_RANK, MLA_HEADS * LANES)
    vp = w4[:, :, 1, :].reshape(MLA_KV_RANK, MLA_HEADS * 64)
    return jnp.concatenate([kp, vp], axis=1)


def _kv_up_grad(dwp):
    dk = dwp[:, :MLA_HEADS * LANES].reshape(MLA_KV_RANK, MLA_HEADS, LANES)[:, :, :64]
    dv = dwp[:, MLA_HEADS * LANES:].reshape(MLA_KV_RANK, MLA_HEADS, 64)
    return jnp.stack([dk, dv], axis=2).reshape(MLA_KV_RANK, MLA_HEADS * LANES)


def _pad_lanes(a):
    return jnp.pad(a, ((0, 0), (0, LANES - a.shape[1])))


def _small_pack(g_in, g_final, g_q_a, g_kv_a, sinks, b_f, loss):
    rows = [g_in.reshape(8, LANES), g_final.reshape(8, LANES), g_q_a.reshape(2, LANES), g_kv_a.reshape(1, LANES),
            _pad_lanes(sinks.reshape(1, -1)), _pad_lanes(b_f.reshape(1, -1)), _pad_lanes(loss.reshape(1, 1)),
            jnp.zeros((2, LANES), F32)]
    return jnp.concatenate(rows, axis=0)


def _small_unpack(a):
    return (a[0:8].reshape(1, D_MODEL), a[8:16].reshape(D_MODEL), a[16:18].reshape(1, MLA_Q_RANK),
            a[18:19].reshape(1, MLA_KV_RANK), a[19:20, :SWA_HEADS], a[20:21, :FOX_HEADS], a[21, 0])


def _local_step(x, positions, target, e_g_in, w0, e_g_q_a, wq, e_g_kv_a, wkv, e_sinks, wo0,
                o_g_in, w1, wf, o_b_f, wo1, g_final):
    s = x.shape[0]
    att_t = min(ATT_T, s)
    nb = s // att_t
    mla_scale = (MLA_NOPE + MLA_ROPE) ** -0.5
    fox_scale = FOX_DIM ** -0.5
    n0a = Z0A_UNITS * LANES

    inv_freq = 1.0 / (ROPE_THETA ** (jnp.arange(0, MLA_ROPE, 2, dtype=F32) / MLA_ROPE))
    ang = positions.astype(F32)[:, None] * inv_freq
    cos, sin = jnp.cos(ang), jnp.sin(ang)
    ones, zeros = jnp.ones((s, 64), F32), jnp.zeros((s, 64), F32)
    cos_t = jnp.concatenate([ones, cos, cos, ones[:, :32]], axis=1)
    sin_t = jnp.concatenate([zeros, -sin, sin, zeros[:, :32]], axis=1)

    h0 = _rmsnorm_fwd(x, e_g_in, width=D_MODEL, col_blk=0, name="l0_norm")
    z0a = _matmul(h0, w0[:, :n0a], name="l0_in_a")
    z0b = _matmul(h0, w0[:, n0a:], name="l0_in_b", out_dtype=BF16)
    cqn = _rmsnorm_fwd(z0a, e_g_q_a, width=MLA_Q_RANK, col_blk=4, name="l0_q_norm")
    ckvn = _rmsnorm_fwd(z0a, e_g_kv_a, width=MLA_KV_RANK, col_blk=10, name="l0_kv_norm")
    qp = _matmul(cqn, wq, name="l0_q_up")
    kvp = _matmul(ckvn, wkv, name="l0_kv_up", out_dtype=BF16)
    qm, km = _rope_fwd(qp, kvp, z0a, cos_t, sin_t, name="l0_rope")
    o_mla, lse_mla = _flash_fwd(qm, km, kvp, None, n_pairs=MLA_HEADS // 2, hw=LANES, q_off=0, k_off=0,
                                v_off=MLA_HEADS, scale=mla_scale, name="l0_mla_fwd")
    o_swa, lse_swa = _swa_fwd(z0b, e_sinks, name="l0_swa_fwd")
    og0 = _gate_fwd([o_mla, o_swa], z0a, name="l0_gate")
    x1 = _matmul(og0, wo0, add=x, name="l0_out")

    h1 = _rmsnorm_fwd(x1, o_g_in, width=D_MODEL, col_blk=0, name="l1_norm")
    z1 = _matmul(h1, w1[:, :3 * D_MODEL], name="l1_in_qkv", out_dtype=BF16)
    gate1 = _matmul(h1, w1[:, 3 * D_MODEL:], name="l1_in_gate")
    zf = _matmul(h1, wf, name="l1_in_f")
    bf = _pad_lanes(o_b_f)
    log_cum = _logf_fwd(zf, bf, name="l1_logf")
    bias = (-LOG2E * log_cum[:, :FOX_HEADS]).T.reshape(FOX_HEADS // 2, 2, nb, 1, att_t)
    o_fox, lse_fox = _flash_fwd(z1, z1, z1, bias, n_pairs=FOX_HEADS // 2, hw=64, q_off=0, k_off=8, v_off=16,
                                scale=fox_scale, name="l1_fox_fwd")
    og1 = _gate_fwd([o_fox], gate1, name="l1_gate")
    x2 = _matmul(og1, wo1, add=x1, name="l1_out")

    dx2, loss_part, d_g_final = _loss_head(x2, g_final.reshape(1, D_MODEL), target, name="loss_head")

    d_wo1 = _matmul(og1, dx2, ta=True, name="l1_out_dw")
    d_og1 = _matmul(dx2, wo1, tb=True, name="l1_out_dx")
    do_fox, d_gate1 = _gate_bwd(d_og1, [o_fox], gate1, name="l1_gate_bwd")
    dq1, dkt1, dvt1, dbias, drow = _flash_bwd(z1, z1, z1, do_fox, o_fox, lse_fox, bias, n_pairs=FOX_HEADS // 2, hw=64,
                                        q_off=0, k_off=8, v_off=16, scale=fox_scale, name="l1_fox_bwd")
    d_log_cum = (drow.reshape(FOX_HEADS, s) - dbias.reshape(FOX_HEADS, s)).T
    d_log_cum = jnp.pad(d_log_cum, ((0, 0), (0, LANES - FOX_HEADS)))
    d_zf, d_bf = _logf_bwd(d_log_cum, zf, bf, name="l1_logf_bwd")
    dz1 = jnp.concatenate([dq1.astype(BF16), _untranspose(dkt1).astype(BF16), _untranspose(dvt1).astype(BF16), d_gate1],
                          axis=1)
    d_w1 = _matmul(h1, dz1, ta=True, name="l1_in_dw")
    d_wf = _matmul(h1, d_zf, ta=True, name="l1_in_f_dw")
    dh1 = _matmul(dz1, w1, tb=True, name="l1_in_dx")
    dh1 = _matmul(d_zf, wf, tb=True, add=dh1, name="l1_in_f_dx")
    dx1, d_o_g_in = _rmsnorm_bwd(x1, o_g_in, dh1, width=D_MODEL, col_blk=0, add=dx2, name="l1_norm_bwd")

    d_wo0 = _matmul(og0, dx1, ta=True, name="l0_out_dw")
    d_og0 = _matmul(dx1, wo0, tb=True, name="l0_out_dx")
    do_mla, do_swa, d_gate0 = _gate_bwd(d_og0, [o_mla, o_swa], z0a, name="l0_gate_bwd")
    dq_s, dkt_s, dvt_s, d_sinks = _swa_bwd(z0b, e_sinks, do_swa, o_swa, lse_swa, name="l0_swa_bwd")
    dk_s = dkt_s.transpose(0, 2, 1).reshape(s, LANES)
    dv_s = dvt_s.transpose(0, 2, 1).reshape(s, LANES)
    dqm, dkt_m, dvt_m = _flash_bwd(qm, km, kvp, do_mla, o_mla, lse_mla, None, n_pairs=MLA_HEADS // 2, hw=LANES,
                                   q_off=0, k_off=0, v_off=MLA_HEADS, scale=mla_scale, name="l0_mla_bwd")
    dkm = _untranspose(dkt_m)
    d_qp, d_kpe = _rope_bwd(dqm, dkm, cos_t, sin_t, name="l0_rope_bwd")
    d_kvp = jnp.concatenate([dkm.astype(BF16), _untranspose(dvt_m).astype(BF16)], axis=1)
    d_wq = _matmul(cqn, d_qp, ta=True, name="l0_q_up_dw")
    d_cqn = _matmul(d_qp, wq, tb=True, name="l0_q_up_dx")
    d_wkv = _matmul(ckvn, d_kvp, ta=True, name="l0_kv_up_dw")
    d_ckvn = _matmul(d_kvp, wkv, tb=True, name="l0_kv_up_dx")
    d_cq, d_g_q_a = _rmsnorm_bwd(z0a, e_g_q_a, d_cqn, width=MLA_Q_RANK, col_blk=4, out_dtype=BF16, name="l0_q_norm_bwd")
    d_ckv, d_g_kv_a = _rmsnorm_bwd(z0a, e_g_kv_a, d_ckvn, width=MLA_KV_RANK, col_blk=10, out_dtype=BF16,
                                   name="l0_kv_norm_bwd")
    dz0 = jnp.concatenate([d_gate0, d_cq, d_ckv, d_kpe, dq_s.astype(BF16), dk_s.astype(BF16), dv_s.astype(BF16)], axis=1)
    d_w0 = _matmul(h0, dz0, ta=True, name="l0_in_dw")
    dh0 = _matmul(dz0, w0, tb=True, name="l0_in_dx")
    grad_x, d_e_g_in = _rmsnorm_bwd(x, e_g_in, dh0, width=D_MODEL, col_blk=0, add=dx1, name="l0_norm_bwd")

    return dict(loss=loss_part[0, 0], grad_x=grad_x, e_g_in=d_e_g_in, w0=d_w0, e_g_q_a=d_g_q_a, wq=d_wq,
                e_g_kv_a=d_g_kv_a, wkv=d_wkv, e_sinks=d_sinks[:, 0].reshape(1, SWA_HEADS), wo0=d_wo0,
                o_g_in=d_o_g_in, w1=d_w1, wf=d_wf, o_b_f=d_bf[:, :FOX_HEADS], wo1=d_wo1, g_final=d_g_final.reshape(D_MODEL))


def _layer1_in_weight(w_in):
    main = jnp.concatenate([w_in[:, :3 * D_MODEL], w_in[:, 3 * D_MODEL + FOX_HEADS:]], axis=1)
    wf = jnp.pad(w_in[:, 3 * D_MODEL:3 * D_MODEL + FOX_HEADS], ((0, 0), (0, LANES - FOX_HEADS)))
    return main, wf


def _layer1_in_grad(d_main, d_wf):
    return jnp.concatenate([d_main[:, :3 * D_MODEL], d_wf[:, :FOX_HEADS], d_main[:, 3 * D_MODEL:]], axis=1)


def _flat_shard(e_w_in, e_w_q_up, e_w_kv_up, e_w_out, o_w_in, o_w_out, o_g_in_rows):
    return jnp.concatenate([_cols_to_rows(e_w_in), _cols_to_rows(e_w_q_up), _cols_to_rows(e_w_kv_up),
                            _cols_to_rows(e_w_out), _cols_to_rows(o_w_in), _cols_to_rows(o_w_out), o_g_in_rows], axis=0)


def _g_rows(a):
    return jnp.pad(a, ((0, R_O_G_IN - 1), (0, 0)))


def _unflat_shard(flat):
    return (flat[OFF_E_W_IN:OFF_E_W_Q].reshape(1, D_MODEL, 276), flat[OFF_E_W_Q:OFF_E_W_KV].reshape(1, MLA_Q_RANK, 96),
            flat[OFF_E_W_KV:OFF_E_W_OUT].reshape(1, MLA_KV_RANK, 128), flat[OFF_E_W_OUT:OFF_O_W_IN].reshape(1, 128, D_MODEL),
            flat[OFF_O_W_IN:OFF_O_W_OUT].reshape(1, D_MODEL, 514), flat[OFF_O_W_OUT:OFF_O_G_IN].reshape(1, 128, D_MODEL),
            flat[OFF_O_G_IN:OFF_O_G_IN + 1])


def kernel(x, positions, e_g_in, e_w_in, e_g_q_a, e_w_q_up, e_g_kv_a, e_w_kv_up, e_sinks, e_w_out, o_g_in, o_w_in, o_b_f, o_w_out, g_final, loss_target, m_e_g_in, m_e_w_in, m_e_g_q_a, m_e_w_q_up, m_e_g_kv_a, m_e_w_kv_up, m_e_sinks, m_e_w_out, m_o_g_in, m_o_w_in, m_o_b_f, m_o_w_out, m_g_final, v_e_g_in, v_e_w_in, v_e_g_q_a, v_e_w_q_up, v_e_g_kv_a, v_e_w_kv_up, v_e_sinks, v_e_w_out, v_o_g_in, v_o_w_in, v_o_b_f, v_o_w_out, v_g_final):
    g_bits = lax.bitcast_convert_type(o_g_in.reshape(LANES), BF16).reshape(2, LANES)
    g_bits = jnp.pad(g_bits, ((0, R_O_G_IN - 2), (0, 0)))
    shard = _flat_shard(e_w_in[0].astype(BF16), e_w_q_up[0].astype(BF16), e_w_kv_up[0].astype(BF16),
                        e_w_out[0].astype(BF16), o_w_in[0].astype(BF16), o_w_out[0].astype(BF16), g_bits)
    gath = _all_gather(shard, name="weights_all_gather")
    w0 = _layer0_in_weight(_gathered_cols(gath, OFF_E_W_IN, R_E_W_IN, D_MODEL))
    wq = _q_up_weight(_gathered_cols(gath, OFF_E_W_Q, R_E_W_Q, MLA_Q_RANK))
    wkv = _kv_up_weight(_gathered_cols(gath, OFF_E_W_KV, R_E_W_KV, MLA_KV_RANK))
    wo0 = _gathered_rows(gath, OFF_E_W_OUT, R_E_W_OUT, D_MODEL)
    w1, wf = _layer1_in_weight(_gathered_cols(gath, OFF_O_W_IN, R_O_W_IN, D_MODEL))
    wo1 = _gathered_rows(gath, OFF_O_W_OUT, R_O_W_OUT, D_MODEL)
    o_g_full = lax.bitcast_convert_type(gath[:, OFF_O_G_IN:OFF_O_G_IN + 2, :].reshape(N_DEV, LANES, 2), F32)
    o_g_full = o_g_full.reshape(1, D_MODEL)

    gr = _local_step(x[0], positions[0], loss_target[0], e_g_in, w0, e_g_q_a, wq, e_g_kv_a, wkv, e_sinks, wo0,
                     o_g_full, w1, wf, o_b_f, wo1, g_final)

    d_o_g = jnp.pad(gr["o_g_in"].reshape(N_DEV, 1, LANES), ((0, 0), (0, R_O_G_IN - 1), (0, 0)))
    pieces = jnp.concatenate([
        _scatter_cols(_layer0_in_grad(gr["w0"])), _scatter_cols(_q_up_grad(gr["wq"])),
        _scatter_cols(_kv_up_grad(gr["wkv"])), _scatter_rows(gr["wo0"]),
        _scatter_cols(_layer1_in_grad(gr["w1"], gr["wf"])), _scatter_rows(gr["wo1"]), d_o_g], axis=1)
    recv = _exchange(pieces.astype(BF16), name="grads_exchange")
    w_flat = _flat_shard(e_w_in[0], e_w_q_up[0], e_w_kv_up[0], e_w_out[0], o_w_in[0], o_w_out[0], _g_rows(o_g_in))
    m_flat = _flat_shard(m_e_w_in[0], m_e_w_q_up[0], m_e_w_kv_up[0], m_e_w_out[0], m_o_w_in[0], m_o_w_out[0],
                         _g_rows(m_o_g_in))
    v_flat = _flat_shard(v_e_w_in[0], v_e_w_q_up[0], v_e_w_kv_up[0], v_e_w_out[0], v_o_w_in[0], v_o_w_out[0],
                         _g_rows(v_o_g_in))
    flats = _adamw(recv, w_flat, m_flat, v_flat, name="adamw_sharded")
    g_sh, d_sh, m_sh, v_sh = [_unflat_shard(f) for f in flats]

    small = _small_pack(gr["e_g_in"], gr["g_final"], gr["e_g_q_a"], gr["e_g_kv_a"], gr["e_sinks"], gr["o_b_f"], gr["loss"])
    small_all = _all_gather(small, name="small_all_gather")
    zero = jnp.zeros((), F32)
    w_small = _small_pack(e_g_in, g_final, e_g_q_a, e_g_kv_a, e_sinks, o_b_f, zero)
    m_small = _small_pack(m_e_g_in, m_g_final, m_e_g_q_a, m_e_g_kv_a, m_e_sinks, m_o_b_f, zero)
    v_small = _small_pack(v_e_g_in, v_g_final, v_e_g_q_a, v_e_g_kv_a, v_e_sinks, v_o_b_f, zero)
    smalls = _adamw(small_all, w_small, m_small, v_small, name="adamw_replicated")
    g_sm, d_sm, m_sm, v_sm = [_small_unpack(a) for a in smalls]
    loss = g_sm[6]

    def leaves(sh, sm):
        return (sm[0], sh[0], sm[2], sh[1], sm[3], sh[2], sm[4], sh[3], sh[6], sh[4], sm[5], sh[5], sm[1])

    return (loss, gr["grad_x"][None], *leaves(g_sh, g_sm), *leaves(d_sh, d_sm), *leaves(m_sh, m_sm), *leaves(v_sh, v_sm))
```

```python
import functools

import jax
import jax.numpy as jnp
from jax import lax
from jax.experimental import pallas as pl
from jax.experimental.pallas import tpu as pltpu

F32 = jnp.float32
BF16 = jnp.bfloat16
NEG_INF = float("-inf")

N_DEV = 8
LANES = 128
D_MODEL = 1024
EPS = 1e-6
ROPE_THETA = 10000.0
MLA_HEADS = 8
MLA_Q_RANK = 256
MLA_KV_RANK = 128
MLA_NOPE = 64
MLA_ROPE = 32
MLA_V = 64
SWA_HEADS = 8
SWA_KV_HEADS = 2
SWA_DIM = 64
WINDOW = 128
FOX_HEADS = 16
FOX_DIM = 64

ADAM_LR = 0.001
ADAM_B1 = 0.9
ADAM_B2 = 0.999
ADAM_EPS = 1e-08
ADAM_WD = 0.01
ADAM_STEP = 10

ATT_T = 512
VMEM_LIMIT = 56 * 1024 * 1024

Z0A_UNITS = 12
Z0B_UNITS = 6

R_E_W_IN = 1024 * 276 // LANES
R_E_W_Q = 256 * 96 // LANES
R_E_W_KV = 128 * 128 // LANES
R_E_W_OUT = 128 * 1024 // LANES
R_O_W_IN = 1024 * 514 // LANES
R_O_W_OUT = 128 * 1024 // LANES
R_O_G_IN = 16
OFF_E_W_IN = 0
OFF_E_W_Q = OFF_E_W_IN + R_E_W_IN
OFF_E_W_KV = OFF_E_W_Q + R_E_W_Q
OFF_E_W_OUT = OFF_E_W_KV + R_E_W_KV
OFF_O_W_IN = OFF_E_W_OUT + R_E_W_OUT
OFF_O_W_OUT = OFF_O_W_IN + R_O_W_IN
OFF_O_G_IN = OFF_O_W_OUT + R_O_W_OUT
R_FLAT = OFF_O_G_IN + R_O_G_IN
SMALL_ROWS = 24


def _tile(n, cands):
    for c in cands:
        if n % c == 0:
            return c
    raise ValueError(f"no tile for {n}")


def _params(sem, vmem=None):
    return pltpu.CompilerParams(dimension_semantics=sem, vmem_limit_bytes=vmem)


def _matmul(a, b, *, name, ta=False, tb=False, add=None, out_dtype=F32):
    if ta:
        kdim, m = a.shape
    else:
        m, kdim = a.shape
    if tb:
        n, kb = b.shape
    else:
        kb, n = b.shape
    assert kdim == kb, (a.shape, b.shape)
    tm = _tile(m, (512, 256, 128))
    tn = _tile(n, (768, 512, 384, 256, 128))
    dims = (((0 if ta else 1,), (1 if tb else 0,)), ((), ()))

    def body(*refs):
        if add is None:
            a_ref, b_ref, o_ref = refs
            add_ref = None
        else:
            a_ref, b_ref, add_ref, o_ref = refs
        r = lax.dot_general(a_ref[...].astype(BF16), b_ref[...].astype(BF16), dims, preferred_element_type=F32)
        if add_ref is not None:
            r = r + add_ref[...]
        o_ref[...] = r.astype(out_dtype)

    a_spec = pl.BlockSpec((kdim, tm), lambda i, j: (0, i)) if ta else pl.BlockSpec((tm, kdim), lambda i, j: (i, 0))
    b_spec = pl.BlockSpec((tn, kdim), lambda i, j: (j, 0)) if tb else pl.BlockSpec((kdim, tn), lambda i, j: (0, j))
    in_specs = [a_spec, b_spec]
    args = [a, b]
    if add is not None:
        in_specs.append(pl.BlockSpec((tm, tn), lambda i, j: (i, j)))
        args.append(add)
    return pl.pallas_call(
        body, name=name, grid=(m // tm, n // tn),
        in_specs=in_specs, out_specs=pl.BlockSpec((tm, tn), lambda i, j: (i, j)),
        out_shape=jax.ShapeDtypeStruct((m, n), out_dtype),
        compiler_params=_params(("parallel", "parallel"), VMEM_LIMIT),
    )(*args)


def _rmsnorm_fwd(x, g, *, width, col_blk, name):
    s = x.shape[0]
    tm = _tile(s, (256, 128))

    def body(x_ref, g_ref, y_ref):
        xf = x_ref[...].astype(F32)
        r = lax.rsqrt(jnp.mean(xf * xf, axis=-1, keepdims=True) + EPS)
        y_ref[...] = ((xf * r) * g_ref[...]).astype(BF16)

    return pl.pallas_call(
        body, name=name, grid=(s // tm,),
        in_specs=[pl.BlockSpec((tm, width), lambda i: (i, col_blk)), pl.BlockSpec((1, width), lambda i: (0, 0))],
        out_specs=pl.BlockSpec((tm, width), lambda i: (i, 0)),
        out_shape=jax.ShapeDtypeStruct((s, width), BF16),
        compiler_params=_params(("parallel",)),
    )(x, g)


def _rmsnorm_bwd(x, g, dy, *, width, col_blk, name, add=None, out_dtype=F32):
    s = x.shape[0]
    tm = _tile(s, (256, 128))

    def body(*refs):
        if add is None:
            x_ref, g_ref, dy_ref, dx_ref, dg_ref = refs
            add_ref = None
        else:
            x_ref, g_ref, dy_ref, add_ref, dx_ref, dg_ref = refs
        i = pl.program_id(0)
        xf = x_ref[...].astype(F32)
        r = lax.rsqrt(jnp.mean(xf * xf, axis=-1, keepdims=True) + EPS)
        xh = xf * r
        dyf = dy_ref[...].astype(F32)

        @pl.when(i == 0)
        def _():
            dg_ref[...] = jnp.zeros_like(dg_ref)

        dg_ref[...] += jnp.sum(dyf * xh, axis=0, keepdims=True)
        dxh = dyf * g_ref[...]
        dx = r * (dxh - xh * jnp.mean(dxh * xh, axis=-1, keepdims=True))
        if add_ref is not None:
            dx = dx + add_ref[...]
        dx_ref[...] = dx.astype(out_dtype)

    in_specs = [pl.BlockSpec((tm, width), lambda i: (i, col_blk)), pl.BlockSpec((1, width), lambda i: (0, 0)),
                pl.BlockSpec((tm, width), lambda i: (i, 0))]
    args = [x, g, dy]
    if add is not None:
        in_specs.append(pl.BlockSpec((tm, width), lambda i: (i, 0)))
        args.append(add)
    return pl.pallas_call(
        body, name=name, grid=(s // tm,),
        in_specs=in_specs,
        out_specs=[pl.BlockSpec((tm, width), lambda i: (i, 0)), pl.BlockSpec((1, width), lambda i: (0, 0))],
        out_shape=[jax.ShapeDtypeStruct((s, width), out_dtype), jax.ShapeDtypeStruct((1, width), F32)],
        compiler_params=_params(("arbitrary",)),
    )(*args)


def _sigmoid(x):
    return 1.0 / (1.0 + jnp.exp(-x))


def _gate_fwd(o_parts, gate, *, name):
    s = gate.shape[0]
    tm = _tile(s, (256, 128))
    n_o = len(o_parts)

    def body(*refs):
        o_refs, g_ref, y_ref = refs[:n_o], refs[n_o], refs[n_o + 1]
        o = o_refs[0][...] if n_o == 1 else jnp.concatenate([r[...] for r in o_refs], axis=1)
        gt = g_ref[...]
        y_ref[...] = (o * (gt * _sigmoid(gt))).astype(BF16)

    in_specs = [pl.BlockSpec((tm, o.shape[1]), lambda i: (i, 0)) for o in o_parts]
    in_specs.append(pl.BlockSpec((tm, D_MODEL), lambda i: (i, 0)))
    return pl.pallas_call(
        body, name=name, grid=(s // tm,), in_specs=in_specs,
        out_specs=pl.BlockSpec((tm, D_MODEL), lambda i: (i, 0)),
        out_shape=jax.ShapeDtypeStruct((s, D_MODEL), BF16),
        compiler_params=_params(("parallel",)),
    )(*o_parts, gate)


def _gate_bwd(d_og, o_parts, gate, *, name):
    s = gate.shape[0]
    tm = _tile(s, (256, 128))
    n_o = len(o_parts)
    widths = [o.shape[1] for o in o_parts]

    def body(*refs):
        d_ref, o_refs, g_ref = refs[0], refs[1:1 + n_o], refs[1 + n_o]
        do_refs, dg_ref = refs[2 + n_o:2 + 2 * n_o], refs[2 + 2 * n_o]
        d = d_ref[...]
        gt = g_ref[...]
        sg = _sigmoid(gt)
        silu = gt * sg
        dsilu = sg * (1.0 + gt * (1.0 - sg))
        o = o_refs[0][...] if n_o == 1 else jnp.concatenate([r[...] for r in o_refs], axis=1)
        dg_ref[...] = (d * o * dsilu).astype(BF16)
        do = d * silu
        off = 0
        for r, w in zip(do_refs, widths):
            r[...] = do[:, off:off + w]
            off += w

    in_specs = [pl.BlockSpec((tm, D_MODEL), lambda i: (i, 0))]
    in_specs += [pl.BlockSpec((tm, w), lambda i: (i, 0)) for w in widths]
    in_specs.append(pl.BlockSpec((tm, D_MODEL), lambda i: (i, 0)))
    out_specs = [pl.BlockSpec((tm, w), lambda i: (i, 0)) for w in widths]
    out_specs.append(pl.BlockSpec((tm, D_MODEL), lambda i: (i, 0)))
    out_shape = [jax.ShapeDtypeStruct((s, w), F32) for w in widths]
    out_shape.append(jax.ShapeDtypeStruct((s, D_MODEL), BF16))
    return pl.pallas_call(
        body, name=name, grid=(s // tm,), in_specs=in_specs, out_specs=out_specs, out_shape=out_shape,
        compiler_params=_params(("parallel",)),
    )(d_og, *o_parts, gate)


def _rot_half(x):
    lane = lax.broadcasted_iota(jnp.int32, x.shape, 1)
    return jnp.where(lane < 80, pltpu.roll(x, LANES - 16, axis=1), pltpu.roll(x, 16, axis=1))


def _rot_half_t(g):
    lane = lax.broadcasted_iota(jnp.int32, g.shape, 1)
    lo = (lane >= MLA_NOPE) & (lane < MLA_NOPE + MLA_ROPE // 2)
    hi = (lane >= MLA_NOPE + MLA_ROPE // 2) & (lane < MLA_NOPE + MLA_ROPE)
    return jnp.where(lo, pltpu.roll(g, LANES - 16, axis=1), jnp.where(hi, pltpu.roll(g, 16, axis=1), 0.0))


def _rope_fwd(qp, kvp, z0a, cos_t, sin_t, *, name):
    s = qp.shape[0]
    tm = _tile(s, (256, 128))
    hw = MLA_HEADS * LANES

    def body(q_ref, k_ref, kpe_ref, c_ref, s_ref, qm_ref, km_ref):
        c = c_ref[...]
        sn = s_ref[...]
        kpe = kpe_ref[...]
        kpe_r = (kpe * c + _rot_half(kpe) * sn).astype(BF16)
        lane = lax.broadcasted_iota(jnp.int32, kpe.shape, 1)
        for h in range(MLA_HEADS):
            sl = slice(h * LANES, (h + 1) * LANES)
            qh = q_ref[:, sl]
            qm_ref[:, sl] = (qh * c + _rot_half(qh) * sn).astype(BF16)
            km_ref[:, sl] = jnp.where(lane < MLA_NOPE, k_ref[:, sl], kpe_r)

    return pl.pallas_call(
        body, name=name, grid=(s // tm,),
        in_specs=[pl.BlockSpec((tm, hw), lambda i: (i, 0)), pl.BlockSpec((tm, hw), lambda i: (i, 0)),
                  pl.BlockSpec((tm, LANES), lambda i: (i, 11)),
                  pl.BlockSpec((tm, LANES), lambda i: (i, 0)), pl.BlockSpec((tm, LANES), lambda i: (i, 0))],
        out_specs=[pl.BlockSpec((tm, hw), lambda i: (i, 0)), pl.BlockSpec((tm, hw), lambda i: (i, 0))],
        out_shape=[jax.ShapeDtypeStruct((s, hw), BF16), jax.ShapeDtypeStruct((s, hw), BF16)],
        compiler_params=_params(("parallel",)),
    )(qp, kvp, z0a, cos_t, sin_t)


def _rope_bwd(dqm, dkm, dvm, cos_t, sin_t, *, name):
    s = dqm.shape[0]
    tm = _tile(s, (256, 128))
    hw = MLA_HEADS * LANES
    vw = MLA_HEADS * MLA_V

    def body(dq_ref, dk_ref, dv_ref, c_ref, s_ref, dqp_ref, dkv_ref, dkpe_ref):
        c = c_ref[...]
        sn = s_ref[...]
        ksum = jnp.zeros((tm, LANES), F32)
        for h in range(MLA_HEADS):
            sl = slice(h * LANES, (h + 1) * LANES)
            dq = dq_ref[:, sl]
            dqp_ref[:, sl] = (dq * c + _rot_half_t(dq * sn)).astype(BF16)
            dk = dk_ref[:, sl]
            dkv_ref[:, sl] = dk.astype(BF16)
            ksum = ksum + dk
        dkv_ref[:, hw:] = dv_ref[...]
        lane = lax.broadcasted_iota(jnp.int32, ksum.shape, 1)
        dkpe = ksum * c + _rot_half_t(ksum * sn)
        dkpe_ref[...] = jnp.where((lane >= MLA_NOPE) & (lane < MLA_NOPE + MLA_ROPE), dkpe, 0.0).astype(BF16)

    return pl.pallas_call(
        body, name=name, grid=(s // tm,),
        in_specs=[pl.BlockSpec((tm, hw), lambda i: (i, 0)), pl.BlockSpec((tm, hw), lambda i: (i, 0)),
                  pl.BlockSpec((tm, vw), lambda i: (i, 0)),
                  pl.BlockSpec((tm, LANES), lambda i: (i, 0)), pl.BlockSpec((tm, LANES), lambda i: (i, 0))],
        out_specs=[pl.BlockSpec((tm, hw), lambda i: (i, 0)), pl.BlockSpec((tm, hw + vw), lambda i: (i, 0)),
                   pl.BlockSpec((tm, LANES), lambda i: (i, 0))],
        out_shape=[jax.ShapeDtypeStruct((s, hw), BF16), jax.ShapeDtypeStruct((s, hw + vw), BF16),
                   jax.ShapeDtypeStruct((s, LANES), BF16)],
        compiler_params=_params(("parallel",)),
    )(dqm, dkm, dvm, cos_t, sin_t)


def _head_mask(shape, a):
    lane = lax.broadcasted_iota(jnp.int32, shape, 1)
    return (lane >= 64 * a) & (lane < 64 * (a + 1))


def _causal_mask(t):
    row = lax.broadcasted_iota(jnp.int32, (t, t), 0)
    col = lax.broadcasted_iota(jnp.int32, (t, t), 1)
    return col <= row


_NT = (((1,), (1,)), ((), ()))
LOG2E = 1.4426950408889634


def _stack_heads(tile, hw):
    lane = lax.broadcasted_iota(jnp.int32, tile.shape, 1)
    z = jnp.zeros_like(tile)
    return jnp.concatenate([jnp.where(lane < hw, tile, z), jnp.where(lane >= hw, tile, z)], axis=0)


def _stacked_rows(r0, r1, t):
    n = r0.shape[-1]
    return jnp.concatenate([jnp.broadcast_to(r0, (t, n)), jnp.broadcast_to(r1, (t, n))], axis=0)


def _stacked_causal_mask(t):
    m = _causal_mask(t)
    return jnp.concatenate([m, m], axis=0)


def _resident(block, index_map):
    return pl.BlockSpec(block, index_map, pipeline_mode=pl.Buffered(1))


def _flash_fwd(q, k, v, bias, *, n_pairs, hw, q_off, k_off, v_off, scale, name):
    s = q.shape[0]
    t = min(ATT_T, s)
    nb = s // t
    qw = 2 * hw
    has_bias = bias is not None
    c1 = scale * LOG2E

    def body(*refs):
        if has_bias:
            q_ref, k_ref, v_ref, b_ref, o_ref, lse_ref = refs
        else:
            q_ref, k_ref, v_ref, o_ref, lse_ref = refs
            b_ref = None
        cmask = _stacked_causal_mask(t)
        lane_lt64 = lax.broadcasted_iota(jnp.int32, (t, LANES), 1) < 64

        def q_block(i, _):
            r0 = pl.multiple_of(i * t, t)
            qs = _stack_heads(q_ref[pl.ds(r0, t), :], hw)

            def kv_step(j, carry, masked):
                m, l, acc = carry
                c0 = pl.multiple_of(j * t, t)
                sc = lax.dot_general(qs, k_ref[pl.ds(c0, t), :], _NT, preferred_element_type=F32) * c1
                if has_bias:
                    sc = sc + _stacked_rows(b_ref[0, 0, j], b_ref[0, 1, j], t)
                if masked:
                    sc = jnp.where(cmask, sc, NEG_INF)
                m_new = jnp.maximum(m, jnp.max(sc, axis=-1, keepdims=True))
                alpha = jnp.exp2(m - m_new)
                p = jnp.exp2(sc - m_new)
                l_new = alpha * l + jnp.sum(p, axis=-1, keepdims=True)
                pv = jnp.dot(p.astype(BF16), v_ref[pl.ds(c0, t), :], preferred_element_type=F32)
                return m_new, l_new, alpha * acc + pv

            init = (jnp.full((2 * t, 1), NEG_INF, F32), jnp.zeros((2 * t, 1), F32), jnp.zeros((2 * t, LANES), F32))
            carry = lax.fori_loop(0, i, functools.partial(kv_step, masked=False), init)
            m, l, acc = kv_step(i, carry, True)
            out = acc / l
            lse2 = m + jnp.log2(l)
            lse_ref[0, 0, pl.ds(r0, t), :] = lse2[:t]
            lse_ref[0, 1, pl.ds(r0, t), :] = lse2[t:]
            o_ref[pl.ds(r0, t), :] = jnp.where(lane_lt64, out[:t], out[t:])
            return 0

        lax.fori_loop(0, nb, q_block, 0)

    in_specs = [_resident((s, qw), lambda p: (0, q_off + p)), _resident((s, qw), lambda p: (0, k_off + p)),
                _resident((s, LANES), lambda p: (0, v_off + p))]
    args = [q, k, v]
    if has_bias:
        in_specs.append(_resident((1, 2, nb, 1, t), lambda p: (p, 0, 0, 0, 0)))
        args.append(bias)
    return pl.pallas_call(
        body, name=name, grid=(n_pairs,), in_specs=in_specs,
        out_specs=[pl.BlockSpec((s, LANES), lambda p: (0, p)), pl.BlockSpec((1, 2, s, 1), lambda p: (p, 0, 0, 0))],
        out_shape=[jax.ShapeDtypeStruct((s, n_pairs * LANES), F32), jax.ShapeDtypeStruct((n_pairs, 2, s, 1), F32)],
        compiler_params=_params(("parallel",), VMEM_LIMIT),
    )(*args)


def _flash_bwd(q, k, v, do, o, lse, bias, *, n_pairs, hw, q_off, k_off, v_off, scale, qk_dtype, name):
    s = q.shape[0]
    t = min(ATT_T, s)
    nb = s // t
    qw = 2 * hw
    has_bias = bias is not None
    c1 = scale * LOG2E

    def body(*refs):
        if has_bias:
            (q_ref, k_ref, v_ref, do_ref, o_ref, lse_ref, b_ref, dq_ref, dk_ref, dv_ref, db_ref, dr_ref,
             dkt_ref, dvt_ref) = refs
            db_ref[...] = jnp.zeros_like(db_ref)
        else:
            q_ref, k_ref, v_ref, do_ref, o_ref, lse_ref, dq_ref, dk_ref, dv_ref, dkt_ref, dvt_ref = refs
            b_ref = db_ref = dr_ref = None
        dkt_ref[...] = jnp.zeros_like(dkt_ref)
        dvt_ref[...] = jnp.zeros_like(dvt_ref)
        cmask = _stacked_causal_mask(t)
        lane_lt_hw = lax.broadcasted_iota(jnp.int32, (t, qw), 1) < hw

        def q_block(i, _):
            r0 = pl.multiple_of(i * t, t)
            qs = _stack_heads(q_ref[pl.ds(r0, t), :], hw)
            dos = _stack_heads(do_ref[pl.ds(r0, t), :], 64)
            ot = o_ref[pl.ds(r0, t), :]
            delta = jnp.sum(dos * jnp.concatenate([ot, ot], axis=0), axis=-1, keepdims=True)
            lse2 = jnp.concatenate([lse_ref[0, 0, pl.ds(r0, t), :], lse_ref[0, 1, pl.ds(r0, t), :]], axis=0)
            dosb = dos.astype(BF16)
            dos_t = dos.T.astype(BF16)
            qs_t = qs.astype(F32).T.astype(BF16)

            def kv_step(j, carry, masked):
                dq, rsum = carry
                c0 = pl.multiple_of(j * t, t)
                kt = k_ref[pl.ds(c0, t), :]
                vt = v_ref[pl.ds(c0, t), :]
                sc = lax.dot_general(qs, kt, _NT, preferred_element_type=F32) * c1
                if has_bias:
                    sc = sc + _stacked_rows(b_ref[0, 0, j], b_ref[0, 1, j], t)
                if masked:
                    sc = jnp.where(cmask, sc, NEG_INF)
                p = jnp.exp2(sc - lse2)
                dp = lax.dot_general(dosb, vt, _NT, preferred_element_type=F32)
                ds = p * (dp - delta)
                dsb = ds.astype(BF16)
                dvt_ref[j] += jnp.dot(dos_t, p.astype(BF16), preferred_element_type=F32)
                dkt_ref[j] += jnp.dot(qs_t, dsb, preferred_element_type=F32)
                if has_bias:
                    db_ref[0, 0, j] += jnp.sum(ds[:t], axis=0, keepdims=True)
                    db_ref[0, 1, j] += jnp.sum(ds[t:], axis=0, keepdims=True)
                    rsum = rsum + jnp.sum(ds, axis=-1, keepdims=True)
                return dq + jnp.dot(dsb, kt, preferred_element_type=F32), rsum

            init = (jnp.zeros((2 * t, qw), F32), jnp.zeros((2 * t, 1), F32))
            carry = lax.fori_loop(0, i, functools.partial(kv_step, masked=False), init)
            dq, rsum = kv_step(i, carry, True)
            dq = dq * scale
            dq_ref[pl.ds(r0, t), :] = jnp.where(lane_lt_hw, dq[:t], dq[t:]).astype(qk_dtype)
            if has_bias:
                dr_ref[0, 0, pl.ds(r0, t), :] = rsum[:t]
                dr_ref[0, 1, pl.ds(r0, t), :] = rsum[t:]
            return 0

        lax.fori_loop(0, nb, q_block, 0)

        def k_block(j, _):
            c0 = pl.multiple_of(j * t, t)
            dk_ref[pl.ds(c0, t), :] = (dkt_ref[j].T * scale).astype(qk_dtype)
            dv_ref[pl.ds(c0, t), :] = dvt_ref[j].T.astype(BF16)
            return 0

        lax.fori_loop(0, nb, k_block, 0)

    in_specs = [_resident((s, qw), lambda p: (0, q_off + p)), _resident((s, qw), lambda p: (0, k_off + p)),
                _resident((s, LANES), lambda p: (0, v_off + p)),
                _resident((s, LANES), lambda p: (0, p)), _resident((s, LANES), lambda p: (0, p)),
                _resident((1, 2, s, 1), lambda p: (p, 0, 0, 0))]
    args = [q, k, v, do, o, lse]
    out_specs = [pl.BlockSpec((s, qw), lambda p: (0, p)), pl.BlockSpec((s, qw), lambda p: (0, p)),
                 pl.BlockSpec((s, LANES), lambda p: (0, p))]
    out_shape = [jax.ShapeDtypeStruct((s, n_pairs * qw), qk_dtype), jax.ShapeDtypeStruct((s, n_pairs * qw), qk_dtype),
                 jax.ShapeDtypeStruct((s, n_pairs * LANES), BF16)]
    if has_bias:
        in_specs.append(_resident((1, 2, nb, 1, t), lambda p: (p, 0, 0, 0, 0)))
        args.append(bias)
        out_specs.append(pl.BlockSpec((1, 2, nb, 1, t), lambda p: (p, 0, 0, 0, 0)))
        out_shape.append(jax.ShapeDtypeStruct((n_pairs, 2, nb, 1, t), F32))
        out_specs.append(pl.BlockSpec((1, 2, s, 1), lambda p: (p, 0, 0, 0)))
        out_shape.append(jax.ShapeDtypeStruct((n_pairs, 2, s, 1), F32))
    return pl.pallas_call(
        body, name=name, grid=(n_pairs,), in_specs=in_specs, out_specs=out_specs, out_shape=out_shape,
        scratch_shapes=[pltpu.VMEM((nb, qw, t), F32), pltpu.VMEM((nb, LANES, t), F32)],
        compiler_params=_params(("parallel",), VMEM_LIMIT),
    )(*args)


def _alibi_slope(h):
    return 2.0 ** (-8.0 * (h + 1.0) / SWA_HEADS)


SWA_ROWS = 512
SWA_SCALE = SWA_DIM ** -0.5


def _swa_geometry(i):
    w = WINDOW
    r0 = pl.multiple_of(i * w, w)
    b0 = pl.multiple_of(jnp.maximum(i - 1, 0) * w, w)
    row = lax.broadcasted_iota(jnp.int32, (w, 2 * w), 0)
    col = lax.broadcasted_iota(jnp.int32, (w, 2 * w), 1)
    dist = row - col + (r0 - b0)
    valid = (dist >= 0) & (dist < w)
    return r0, b0, dist.astype(F32), valid


def _swa_q_head(qblk, h):
    kv = h // (SWA_HEADS // SWA_KV_HEADS)
    if h % 2 != kv:
        qblk = pltpu.roll(qblk, 64, axis=1)
    return jnp.where(_head_mask(qblk.shape, kv), qblk, 0.0)


def _swa_fwd(z0b, sinks, *, name):
    s = z0b.shape[0]
    w = WINDOW
    rows = min(SWA_ROWS, s)
    per_step = rows // w
    qcols = SWA_HEADS * SWA_DIM

    def body(sink_ref, q_ref, k_ref, v_ref, o_ref, lse_ref):
        g = pl.program_id(0)
        for ii in range(per_step):
            r0, b0, dist, valid = _swa_geometry(g * per_step + ii)
            kb = k_ref[pl.ds(b0, 2 * w), :]
            vb = v_ref[pl.ds(b0, 2 * w), :]
            o_heads = []
            for h in range(SWA_HEADS):
                kv = h // (SWA_HEADS // SWA_KV_HEADS)
                blk = h // 2
                qh = _swa_q_head(q_ref[ii * w:(ii + 1) * w, blk * LANES:(blk + 1) * LANES].astype(F32), h).astype(BF16)
                sc = lax.dot_general(qh, kb, _NT, preferred_element_type=F32) * SWA_SCALE - _alibi_slope(h) * dist
                sc = jnp.where(valid, sc, NEG_INF)
                sink = sink_ref[0, h]
                m = jnp.maximum(jnp.max(sc, axis=-1, keepdims=True), sink)
                p = jnp.exp(sc - m)
                l = jnp.sum(p, axis=-1, keepdims=True) + jnp.exp(sink - m)
                oh = jnp.dot(p.astype(BF16), vb, preferred_element_type=F32) / l
                if h % 2 != kv:
                    oh = pltpu.roll(oh, 64, axis=1)
                o_heads.append(oh)
                lse_ref[h, ii * w:(ii + 1) * w, :] = m + jnp.log(l)
            lt64 = lax.broadcasted_iota(jnp.int32, (w, LANES), 1) < 64
            o_ref[ii * w:(ii + 1) * w, :] = jnp.concatenate(
                [jnp.where(lt64, o_heads[2 * b], o_heads[2 * b + 1]) for b in range(SWA_HEADS // 2)], axis=1)

    return pl.pallas_call(
        body, name=name, grid=(s // rows,),
        in_specs=[pl.BlockSpec(memory_space=pltpu.SMEM),
                  pl.BlockSpec((rows, qcols), lambda g: (g, 0)),
                  pl.BlockSpec((s, LANES), lambda g: (0, 4)), pl.BlockSpec((s, LANES), lambda g: (0, 5))],
        out_specs=[pl.BlockSpec((rows, qcols), lambda g: (g, 0)), pl.BlockSpec((SWA_HEADS, rows, 1), lambda g: (0, g, 0))],
        out_shape=[jax.ShapeDtypeStruct((s, qcols), F32), jax.ShapeDtypeStruct((SWA_HEADS, s, 1), F32)],
        compiler_params=_params(("parallel",), VMEM_LIMIT),
    )(sinks, z0b, z0b, z0b)


def _swa_bwd(z0b, sinks, do, o, lse, *, name):
    s = z0b.shape[0]
    w = WINDOW
    rows = min(SWA_ROWS, s)
    per_step = rows // w
    qcols = SWA_HEADS * SWA_DIM
    nblk = s // w

    def body(sink_ref, q_ref, k_ref, v_ref, do_ref, o_ref, lse_ref, dq_ref, dkt_ref, dvt_ref, dsink_ref):
        g = pl.program_id(0)

        @pl.when(g == 0)
        def _():
            dkt_ref[...] = jnp.zeros_like(dkt_ref)
            dvt_ref[...] = jnp.zeros_like(dvt_ref)
            dsink_ref[...] = jnp.zeros_like(dsink_ref)

        for ii in range(per_step):
            i = g * per_step + ii
            r0, b0, dist, valid = _swa_geometry(i)
            j0 = jnp.maximum(i - 1, 0)
            kb = k_ref[pl.ds(b0, 2 * w), :]
            vb = v_ref[pl.ds(b0, 2 * w), :]
            dq_heads = []
            for h in range(SWA_HEADS):
                kv = h // (SWA_HEADS // SWA_KV_HEADS)
                blk = h // 2
                cs = slice(blk * LANES, (blk + 1) * LANES)
                rs = slice(ii * w, (ii + 1) * w)
                qh32 = _swa_q_head(q_ref[rs, cs].astype(F32), h)
                qh = qh32.astype(BF16)
                doh32 = _swa_q_head(do_ref[rs, cs], h)
                oh32 = _swa_q_head(o_ref[rs, cs], h)
                delta = jnp.sum(doh32 * oh32, axis=-1, keepdims=True)
                lse = lse_ref[h, rs, :]
                sink = sink_ref[0, h]
                sc = lax.dot_general(qh, kb, _NT, preferred_element_type=F32) * SWA_SCALE - _alibi_slope(h) * dist
                sc = jnp.where(valid, sc, NEG_INF)
                p = jnp.exp(sc - lse)
                dp = lax.dot_general(doh32.astype(BF16), vb, _NT, preferred_element_type=F32)
                ds = p * (dp - delta)
                dsb = ds.astype(BF16)
                pb = p.astype(BF16)
                dsink_ref[h:h + 1, :] += jnp.broadcast_to(-jnp.sum(jnp.exp(sink - lse) * delta), (1, LANES))
                do_t = doh32.T.astype(BF16)
                q_t = qh32.T.astype(BF16)
                dvt = jnp.dot(do_t, pb, preferred_element_type=F32)
                dkt = jnp.dot(q_t, dsb, preferred_element_type=F32) * SWA_SCALE
                dvt_ref[j0] += dvt[:, :w]
                dvt_ref[j0 + 1] += dvt[:, w:]
                dkt_ref[j0] += dkt[:, :w]
                dkt_ref[j0 + 1] += dkt[:, w:]
                dq = jnp.dot(dsb, kb, preferred_element_type=F32) * SWA_SCALE
                if h % 2 != kv:
                    dq = pltpu.roll(dq, 64, axis=1)
                dq_heads.append(dq)
            lt64 = lax.broadcasted_iota(jnp.int32, (w, LANES), 1) < 64
            dq_ref[ii * w:(ii + 1) * w, :] = jnp.concatenate(
                [jnp.where(lt64, dq_heads[2 * b], dq_heads[2 * b + 1]) for b in range(SWA_HEADS // 2)], axis=1)

    return pl.pallas_call(
        body, name=name, grid=(s // rows,),
        in_specs=[pl.BlockSpec(memory_space=pltpu.SMEM),
                  pl.BlockSpec((rows, qcols), lambda g: (g, 0)),
                  pl.BlockSpec((s, LANES), lambda g: (0, 4)), pl.BlockSpec((s, LANES), lambda g: (0, 5)),
                  pl.BlockSpec((rows, qcols), lambda g: (g, 0)), pl.BlockSpec((rows, qcols), lambda g: (g, 0)),
                  pl.BlockSpec((SWA_HEADS, rows, 1), lambda g: (0, g, 0))],
        out_specs=[pl.BlockSpec((rows, qcols), lambda g: (g, 0)),
                   pl.BlockSpec((nblk, LANES, w), lambda g: (0, 0, 0)),
                   pl.BlockSpec((nblk, LANES, w), lambda g: (0, 0, 0)),
                   pl.BlockSpec((SWA_HEADS, LANES), lambda g: (0, 0))],
        out_shape=[jax.ShapeDtypeStruct((s, qcols), F32),
                   jax.ShapeDtypeStruct((nblk, LANES, w), F32), jax.ShapeDtypeStruct((nblk, LANES, w), F32),
                   jax.ShapeDtypeStruct((SWA_HEADS, LANES), F32)],
        compiler_params=_params(("arbitrary",), VMEM_LIMIT),
    )(sinks, z0b, z0b, z0b, do, o, lse)


CUM_T = 256


def _split3(x):
    hi = x.astype(BF16)
    r1 = x - hi.astype(F32)
    mid = r1.astype(BF16)
    lo = (r1 - mid.astype(F32)).astype(BF16)
    return hi, mid, lo


def _tri_dot(tri, x):
    hi, mid, lo = _split3(x)
    out = jnp.dot(tri, hi, preferred_element_type=F32)
    out = out + jnp.dot(tri, mid, preferred_element_type=F32)
    return out + jnp.dot(tri, lo, preferred_element_type=F32)


def _logf_fwd(zf, bf, *, name):
    s = zf.shape[0]
    t = CUM_T
    nb = s // t

    def body(z_ref, b_ref, c_ref, carry_ref):
        i = pl.program_id(0)

        @pl.when(i == 0)
        def _():
            carry_ref[...] = jnp.zeros_like(carry_ref)

        x = z_ref[...] + b_ref[...]
        lf = jnp.minimum(x, 0.0) - jnp.log(1.0 + jnp.exp(-jnp.abs(x)))
        row = lax.broadcasted_iota(jnp.int32, (t, t), 0)
        col = lax.broadcasted_iota(jnp.int32, (t, t), 1)
        tri = jnp.where(col <= row, 1.0, 0.0).astype(BF16)
        c = _tri_dot(tri, lf) + carry_ref[...]
        c_ref[...] = c
        carry_ref[...] = c[t - 1:t, :]

    return pl.pallas_call(
        body, name=name, grid=(nb,),
        in_specs=[pl.BlockSpec((t, LANES), lambda i: (i, 0)), pl.BlockSpec((1, LANES), lambda i: (0, 0))],
        out_specs=pl.BlockSpec((t, LANES), lambda i: (i, 0)),
        out_shape=jax.ShapeDtypeStruct((s, LANES), F32),
        scratch_shapes=[pltpu.VMEM((1, LANES), F32)],
        compiler_params=_params(("arbitrary",)),
    )(zf, bf)


def _logf_bwd(dc, zf, bf, *, name):
    s = zf.shape[0]
    t = CUM_T
    nb = s // t

    def body(dc_ref, z_ref, b_ref, dz_ref, db_ref, carry_ref):
        i = pl.program_id(0)

        @pl.when(i == 0)
        def _():
            carry_ref[...] = jnp.zeros_like(carry_ref)
            db_ref[...] = jnp.zeros_like(db_ref)

        row = lax.broadcasted_iota(jnp.int32, (t, t), 0)
        col = lax.broadcasted_iota(jnp.int32, (t, t), 1)
        tri = jnp.where(col >= row, 1.0, 0.0).astype(BF16)
        dlf = _tri_dot(tri, dc_ref[...]) + carry_ref[...]
        carry_ref[...] = dlf[0:1, :]
        x = z_ref[...] + b_ref[...]
        dz = dlf * _sigmoid(-x)
        dz_ref[...] = dz.astype(BF16)
        db_ref[...] += jnp.sum(dz, axis=0, keepdims=True)

    return pl.pallas_call(
        body, name=name, grid=(nb,),
        in_specs=[pl.BlockSpec((t, LANES), lambda i: (nb - 1 - i, 0)), pl.BlockSpec((t, LANES), lambda i: (nb - 1 - i, 0)),
                  pl.BlockSpec((1, LANES), lambda i: (0, 0))],
        out_specs=[pl.BlockSpec((t, LANES), lambda i: (nb - 1 - i, 0)), pl.BlockSpec((1, LANES), lambda i: (0, 0))],
        out_shape=[jax.ShapeDtypeStruct((s, LANES), BF16), jax.ShapeDtypeStruct((1, LANES), F32)],
        scratch_shapes=[pltpu.VMEM((1, LANES), F32)],
        compiler_params=_params(("arbitrary",)),
    )(dc, zf, bf)


def _loss_head(x2, g, target, *, name):
    s = x2.shape[0]
    tm = _tile(s, (256, 128))

    def body(x_ref, g_ref, t_ref, dx_ref, loss_ref, dg_ref):
        i = pl.program_id(0)

        @pl.when(i == 0)
        def _():
            loss_ref[...] = jnp.zeros_like(loss_ref)
            dg_ref[...] = jnp.zeros_like(dg_ref)

        xf = x_ref[...]
        r = lax.rsqrt(jnp.mean(xf * xf, axis=-1, keepdims=True) + EPS)
        xh = xf * r
        gv = g_ref[...]
        err = xh * gv - t_ref[...]
        loss_ref[...] += jnp.broadcast_to(0.5 * jnp.sum(jnp.mean(err * err, axis=-1, keepdims=True)), loss_ref.shape)
        dy = err * (1.0 / D_MODEL)
        dg_ref[...] += jnp.sum(dy * xh, axis=0, keepdims=True)
        dxh = dy * gv
        dx_ref[...] = r * (dxh - xh * jnp.mean(dxh * xh, axis=-1, keepdims=True))

    return pl.pallas_call(
        body, name=name, grid=(s // tm,),
        in_specs=[pl.BlockSpec((tm, D_MODEL), lambda i: (i, 0)), pl.BlockSpec((1, D_MODEL), lambda i: (0, 0)),
                  pl.BlockSpec((tm, D_MODEL), lambda i: (i, 0))],
        out_specs=[pl.BlockSpec((tm, D_MODEL), lambda i: (i, 0)), pl.BlockSpec((8, LANES), lambda i: (0, 0)),
                   pl.BlockSpec((1, D_MODEL), lambda i: (0, 0))],
        out_shape=[jax.ShapeDtypeStruct((s, D_MODEL), F32), jax.ShapeDtypeStruct((8, LANES), F32),
                   jax.ShapeDtypeStruct((1, D_MODEL), F32)],
        compiler_params=_params(("arbitrary",)),
    )(x2, g, target)


def _adamw(pieces, w, m, v, *, name):
    rows = w.shape[0]
    tr = _tile(rows, (544, 256, 24))
    bc1 = 1.0 - ADAM_B1 ** ADAM_STEP
    bc2 = 1.0 - ADAM_B2 ** ADAM_STEP

    def body(p_ref, w_ref, m_ref, v_ref, g_ref, d_ref, nm_ref, nv_ref):
        g = p_ref[0].astype(F32)
        for k in range(1, N_DEV):
            g = g + p_ref[k].astype(F32)
        nm = ADAM_B1 * m_ref[...] + (1.0 - ADAM_B1) * g
        nv = ADAM_B2 * v_ref[...] + (1.0 - ADAM_B2) * (g * g)
        m_hat = nm / bc1
        v_hat = nv / bc2
        g_ref[...] = g
        d_ref[...] = -ADAM_LR * (m_hat / (jnp.sqrt(v_hat) + ADAM_EPS) + ADAM_WD * w_ref[...])
        nm_ref[...] = nm
        nv_ref[...] = nv

    spec = pl.BlockSpec((tr, LANES), lambda i: (i, 0))
    shape = jax.ShapeDtypeStruct((rows, LANES), F32)
    return pl.pallas_call(
        body, name=name, grid=(rows // tr,),
        in_specs=[pl.BlockSpec((N_DEV, tr, LANES), lambda i: (0, i, 0)), spec, spec, spec],
        out_specs=[spec, spec, spec, spec], out_shape=[shape, shape, shape, shape],
        compiler_params=_params(("parallel",)),
    )(pieces, w, m, v)


MESH = pl.DeviceIdType.MESH
ANY = pl.BlockSpec(memory_space=pl.ANY)


def _all_gather(shard, *, name):
    rows, lanes = shard.shape

    def body(x_ref, out_ref, send_sems, recv_sems, local_sem):
        x, y, c = lax.axis_index("x"), lax.axis_index("y"), lax.axis_index("c")
        me, sibling = (x, y, c), (x, y, 1 - c)
        chips = [(1 - x, y), (x, 1 - y), (1 - x, 1 - y)]

        def block(px, py, pc):
            return out_ref.at[4 * px + 2 * py + pc]

        def copy(k, blk, to, src=None):
            return pltpu.make_async_remote_copy(
                src_ref=block(*blk) if src is None else src, dst_ref=block(*blk),
                send_sem=send_sems.at[k], recv_sem=recv_sems.at[k], device_id=to, device_id_type=MESH)

        mine = pltpu.make_async_copy(x_ref, block(*me), local_sem)
        mine.start()
        first = [copy(0, me, sibling, src=x_ref)]
        first += [copy(1 + j, me, (*chip, c), src=x_ref) for j, chip in enumerate(chips)]
        for cp in first:
            cp.start()
        passed = [copy(4 + j, (*chip, c), sibling) for j, chip in enumerate(chips)]
        for j, chip in enumerate(chips):
            copy(1 + j, (*chip, c), me).wait_recv()
            passed[j].start()
        copy(0, sibling, me).wait_recv()
        for j, chip in enumerate(chips):
            copy(4 + j, (*chip, 1 - c), me).wait_recv()
        for cp in first + passed:
            cp.wait_send()
        mine.wait()

    return pl.pallas_call(
        body, name=name, out_shape=jax.ShapeDtypeStruct((N_DEV, rows, lanes), shard.dtype),
        in_specs=[ANY], out_specs=ANY,
        scratch_shapes=[pltpu.SemaphoreType.DMA((7,)), pltpu.SemaphoreType.DMA((7,)), pltpu.SemaphoreType.DMA(())],
    )(shard)


def _exchange(pieces, *, name):
    def body(g_ref, out_ref, send_sems, recv_sems, local_sem):
        x, y, c = lax.axis_index("x"), lax.axis_index("y"), lax.axis_index("c")
        me = 4 * x + 2 * y + c
        mine = pltpu.make_async_copy(g_ref.at[me], out_ref.at[me], local_sem)
        mine.start()
        copies = []
        for r in range(1, N_DEV):
            px = 1 - x if r & 4 else x
            py = 1 - y if r & 2 else y
            pc = 1 - c if r & 1 else c
            cp = pltpu.make_async_remote_copy(
                src_ref=g_ref.at[4 * px + 2 * py + pc], dst_ref=out_ref.at[me],
                send_sem=send_sems.at[r - 1], recv_sem=recv_sems.at[r - 1],
                device_id=(px, py, pc), device_id_type=MESH)
            cp.start()
            copies.append(cp)
        for cp in copies:
            cp.wait()
        mine.wait()

    return pl.pallas_call(
        body, name=name, out_shape=jax.ShapeDtypeStruct(pieces.shape, pieces.dtype),
        in_specs=[ANY], out_specs=ANY,
        scratch_shapes=[pltpu.SemaphoreType.DMA((7,)), pltpu.SemaphoreType.DMA((7,)), pltpu.SemaphoreType.DMA(())],
    )(pieces)


def _cols_to_rows(w_shard):
    return w_shard.reshape(-1, LANES)


def _gathered_cols(gath, off, nrows, kdim):
    n = nrows * LANES // kdim
    return gath[:, off:off + nrows, :].reshape(N_DEV, kdim, n).transpose(1, 0, 2).reshape(kdim, N_DEV * n)


def _gathered_rows(gath, off, nrows, ncols):
    return gath[:, off:off + nrows, :].reshape(-1, ncols)


def _scatter_cols(dw):
    kdim, n8 = dw.shape
    n = n8 // N_DEV
    return dw.reshape(kdim, N_DEV, n).transpose(1, 0, 2).reshape(N_DEV, kdim * n // LANES, LANES)


def _scatter_rows(dw):
    return dw.reshape(N_DEV, -1, LANES)


def _layer0_in_weight(w_in):
    cq, ckv, kpe = w_in[:, 0:256], w_in[:, 256:384], w_in[:, 384:416]
    q_s, k_s, v_s, gate = w_in[:, 416:928], w_in[:, 928:1056], w_in[:, 1056:1184], w_in[:, 1184:2208]
    z = jnp.zeros((w_in.shape[0], 64), w_in.dtype)
    return jnp.concatenate([gate, cq, ckv, z, kpe, z[:, :32], q_s, k_s, v_s], axis=1)


def _layer0_in_grad(dwp):
    gate, cq, ckv, kpe = dwp[:, 0:1024], dwp[:, 1024:1280], dwp[:, 1280:1408], dwp[:, 1472:1504]
    q_s, k_s, v_s = dwp[:, 1536:2048], dwp[:, 2048:2176], dwp[:, 2176:2304]
    return jnp.concatenate([cq, ckv, kpe, q_s, k_s, v_s, gate], axis=1)


def _q_up_weight(w):
    return jnp.pad(w.reshape(MLA_Q_RANK, MLA_HEADS, 96), ((0, 0), (0, 0), (0, 32))).reshape(MLA_Q_RANK, MLA_HEADS * LANES)


def _q_up_grad(dwp):
    return dwp.reshape(MLA_Q_RANK, MLA_HEADS, LANES)[:, :, :96].reshape(MLA_Q_RANK, MLA_HEADS * 96)


def _kv_up_weight(w):
    w4 = w.reshape(MLA_KV_RANK, MLA_HEADS, 2, 64)
    kp = jnp.pad(w4[:, :, 0, :], ((0, 0), (0, 0), (0, 64))).reshape(MLA_KV_RANK, MLA_HEADS * LANES)
    vp = w4[:, :, 1, :].reshape(MLA_KV_RANK, MLA_HEADS * 64)
    return jnp.concatenate([kp, vp], axis=1)


def _kv_up_grad(dwp):
    dk = dwp[:, :MLA_HEADS * LANES].reshape(MLA_KV_RANK, MLA_HEADS, LANES)[:, :, :64]
    dv = dwp[:, MLA_HEADS * LANES:].reshape(MLA_KV_RANK, MLA_HEADS, 64)
    return jnp.stack([dk, dv], axis=2).reshape(MLA_KV_RANK, MLA_HEADS * LANES)


def _pad_lanes(a):
    return jnp.pad(a, ((0, 0), (0, LANES - a.shape[1])))


def _small_pack(g_in, g_final, g_q_a, g_kv_a, sinks, b_f, loss):
    rows = [g_in.reshape(8, LANES), g_final.reshape(8, LANES), g_q_a.reshape(2, LANES), g_kv_a.reshape(1, LANES),
            _pad_lanes(sinks.reshape(1, -1)), _pad_lanes(b_f.reshape(1, -1)), _pad_lanes(loss.reshape(1, 1)),
            jnp.zeros((2, LANES), F32)]
    return jnp.concatenate(rows, axis=0)


def _small_unpack(a):
    return (a[0:8].reshape(1, D_MODEL), a[8:16].reshape(D_MODEL), a[16:18].reshape(1, MLA_Q_RANK),
            a[18:19].reshape(1, MLA_KV_RANK), a[19:20, :SWA_HEADS], a[20:21, :FOX_HEADS], a[21, 0])


def _local_step(x, positions, target, e_g_in, w0, e_g_q_a, wq, e_g_kv_a, wkv, e_sinks, wo0,
                o_g_in, w1, wf, o_b_f, wo1, g_final):
    s = x.shape[0]
    att_t = min(ATT_T, s)
    nb = s // att_t
    mla_scale = (MLA_NOPE + MLA_ROPE) ** -0.5
    fox_scale = FOX_DIM ** -0.5
    n0a = Z0A_UNITS * LANES

    inv_freq = 1.0 / (ROPE_THETA ** (jnp.arange(0, MLA_ROPE, 2, dtype=F32) / MLA_ROPE))
    ang = positions.astype(F32)[:, None] * inv_freq
    cos, sin = jnp.cos(ang), jnp.sin(ang)
    ones, zeros = jnp.ones((s, 64), F32), jnp.zeros((s, 64), F32)
    cos_t = jnp.concatenate([ones, cos, cos, ones[:, :32]], axis=1)
    sin_t = jnp.concatenate([zeros, -sin, sin, zeros[:, :32]], axis=1)

    h0 = _rmsnorm_fwd(x, e_g_in, width=D_MODEL, col_blk=0, name="l0_norm")
    z0a = _matmul(h0, w0[:, :n0a], name="l0_in_a")
    z0b = _matmul(h0, w0[:, n0a:], name="l0_in_b", out_dtype=BF16)
    cqn = _rmsnorm_fwd(z0a, e_g_q_a, width=MLA_Q_RANK, col_blk=4, name="l0_q_norm")
    ckvn = _rmsnorm_fwd(z0a, e_g_kv_a, width=MLA_KV_RANK, col_blk=10, name="l0_kv_norm")
    qp = _matmul(cqn, wq, name="l0_q_up")
    kvp = _matmul(ckvn, wkv, name="l0_kv_up", out_dtype=BF16)
    qm, km = _rope_fwd(qp, kvp, z0a, cos_t, sin_t, name="l0_rope")
    o_mla, lse_mla = _flash_fwd(qm, km, kvp, None, n_pairs=MLA_HEADS // 2, hw=LANES, q_off=0, k_off=0,
                                v_off=MLA_HEADS, scale=mla_scale, name="l0_mla_fwd")
    o_swa, lse_swa = _swa_fwd(z0b, e_sinks, name="l0_swa_fwd")
    og0 = _gate_fwd([o_mla, o_swa], z0a, name="l0_gate")
    x1 = _matmul(og0, wo0, add=x, name="l0_out")

    h1 = _rmsnorm_fwd(x1, o_g_in, width=D_MODEL, col_blk=0, name="l1_norm")
    z1 = _matmul(h1, w1[:, :3 * D_MODEL], name="l1_in_qkv", out_dtype=BF16)
    gate1 = _matmul(h1, w1[:, 3 * D_MODEL:], name="l1_in_gate")
    zf = _matmul(h1, wf, name="l1_in_f")
    bf = _pad_lanes(o_b_f)
    log_cum = _logf_fwd(zf, bf, name="l1_logf")
    bias = (-LOG2E * log_cum[:, :FOX_HEADS]).T.reshape(FOX_HEADS // 2, 2, nb, 1, att_t)
    o_fox, lse_fox = _flash_fwd(z1, z1, z1, bias, n_pairs=FOX_HEADS // 2, hw=64, q_off=0, k_off=8, v_off=16,
                                scale=fox_scale, name="l1_fox_fwd")
    og1 = _gate_fwd([o_fox], gate1, name="l1_gate")
    x2 = _matmul(og1, wo1, add=x1, name="l1_out")

    dx2, loss_part, d_g_final = _loss_head(x2, g_final.reshape(1, D_MODEL), target, name="loss_head")

    d_wo1 = _matmul(og1, dx2, ta=True, name="l1_out_dw")
    d_og1 = _matmul(dx2, wo1, tb=True, name="l1_out_dx")
    do_fox, d_gate1 = _gate_bwd(d_og1, [o_fox], gate1, name="l1_gate_bwd")
    dq1, dk1, dv1, dbias, drow = _flash_bwd(z1, z1, z1, do_fox, o_fox, lse_fox, bias, n_pairs=FOX_HEADS // 2, hw=64,
                                            q_off=0, k_off=8, v_off=16, scale=fox_scale, qk_dtype=BF16,
                                            name="l1_fox_bwd")
    d_log_cum = (drow.reshape(FOX_HEADS, s) - dbias.reshape(FOX_HEADS, s)).T
    d_log_cum = jnp.pad(d_log_cum, ((0, 0), (0, LANES - FOX_HEADS)))
    d_zf, d_bf = _logf_bwd(d_log_cum, zf, bf, name="l1_logf_bwd")
    dz1 = jnp.concatenate([dq1, dk1, dv1, d_gate1], axis=1)
    d_w1 = _matmul(h1, dz1, ta=True, name="l1_in_dw")
    d_wf = _matmul(h1, d_zf, ta=True, name="l1_in_f_dw")
    dh1 = _matmul(dz1, w1, tb=True, name="l1_in_dx")
    dh1 = _matmul(d_zf, wf, tb=True, add=dh1, name="l1_in_f_dx")
    dx1, d_o_g_in = _rmsnorm_bwd(x1, o_g_in, dh1, width=D_MODEL, col_blk=0, add=dx2, name="l1_norm_bwd")

    d_wo0 = _matmul(og0, dx1, ta=True, name="l0_out_dw")
    d_og0 = _matmul(dx1, wo0, tb=True, name="l0_out_dx")
    do_mla, do_swa, d_gate0 = _gate_bwd(d_og0, [o_mla, o_swa], z0a, name="l0_gate_bwd")
    dq_s, dkt_s, dvt_s, d_sinks = _swa_bwd(z0b, e_sinks, do_swa, o_swa, lse_swa, name="l0_swa_bwd")
    dk_s = dkt_s.transpose(0, 2, 1).reshape(s, LANES)
    dv_s = dvt_s.transpose(0, 2, 1).reshape(s, LANES)
    dqm, dkm, dvm = _flash_bwd(qm, km, kvp, do_mla, o_mla, lse_mla, None, n_pairs=MLA_HEADS // 2, hw=LANES,
                               q_off=0, k_off=0, v_off=MLA_HEADS, scale=mla_scale, qk_dtype=F32, name="l0_mla_bwd")
    d_qp, d_kvp, d_kpe = _rope_bwd(dqm, dkm, dvm, cos_t, sin_t, name="l0_rope_bwd")
    d_wq = _matmul(cqn, d_qp, ta=True, name="l0_q_up_dw")
    d_cqn = _matmul(d_qp, wq, tb=True, name="l0_q_up_dx")
    d_wkv = _matmul(ckvn, d_kvp, ta=True, name="l0_kv_up_dw")
    d_ckvn = _matmul(d_kvp, wkv, tb=True, name="l0_kv_up_dx")
    d_cq, d_g_q_a = _rmsnorm_bwd(z0a, e_g_q_a, d_cqn, width=MLA_Q_RANK, col_blk=4, out_dtype=BF16, name="l0_q_norm_bwd")
    d_ckv, d_g_kv_a = _rmsnorm_bwd(z0a, e_g_kv_a, d_ckvn, width=MLA_KV_RANK, col_blk=10, out_dtype=BF16,
                                   name="l0_kv_norm_bwd")
    dz0 = jnp.concatenate([d_gate0, d_cq, d_ckv, d_kpe, dq_s.astype(BF16), dk_s.astype(BF16), dv_s.astype(BF16)], axis=1)
    d_w0 = _matmul(h0, dz0, ta=True, name="l0_in_dw")
    dh0 = _matmul(dz0, w0, tb=True, name="l0_in_dx")
    grad_x, d_e_g_in = _rmsnorm_bwd(x, e_g_in, dh0, width=D_MODEL, col_blk=0, add=dx1, name="l0_norm_bwd")

    return dict(loss=loss_part[0, 0], grad_x=grad_x, e_g_in=d_e_g_in, w0=d_w0, e_g_q_a=d_g_q_a, wq=d_wq,
                e_g_kv_a=d_g_kv_a, wkv=d_wkv, e_sinks=d_sinks[:, 0].reshape(1, SWA_HEADS), wo0=d_wo0,
                o_g_in=d_o_g_in, w1=d_w1, wf=d_wf, o_b_f=d_bf[:, :FOX_HEADS], wo1=d_wo1, g_final=d_g_final.reshape(D_MODEL))


def _layer1_in_weight(w_in):
    main = jnp.concatenate([w_in[:, :3 * D_MODEL], w_in[:, 3 * D_MODEL + FOX_HEADS:]], axis=1)
    wf = jnp.pad(w_in[:, 3 * D_MODEL:3 * D_MODEL + FOX_HEADS], ((0, 0), (0, LANES - FOX_HEADS)))
    return main, wf


def _layer1_in_grad(d_main, d_wf):
    return jnp.concatenate([d_main[:, :3 * D_MODEL], d_wf[:, :FOX_HEADS], d_main[:, 3 * D_MODEL:]], axis=1)


def _flat_shard(e_w_in, e_w_q_up, e_w_kv_up, e_w_out, o_w_in, o_w_out, o_g_in_rows):
    return jnp.concatenate([_cols_to_rows(e_w_in), _cols_to_rows(e_w_q_up), _cols_to_rows(e_w_kv_up),
                            _cols_to_rows(e_w_out), _cols_to_rows(o_w_in), _cols_to_rows(o_w_out), o_g_in_rows], axis=0)


def _g_rows(a):
    return jnp.pad(a, ((0, R_O_G_IN - 1), (0, 0)))


def _unflat_shard(flat):
    return (flat[OFF_E_W_IN:OFF_E_W_Q].reshape(1, D_MODEL, 276), flat[OFF_E_W_Q:OFF_E_W_KV].reshape(1, MLA_Q_RANK, 96),
            flat[OFF_E_W_KV:OFF_E_W_OUT].reshape(1, MLA_KV_RANK, 128), flat[OFF_E_W_OUT:OFF_O_W_IN].reshape(1, 128, D_MODEL),
            flat[OFF_O_W_IN:OFF_O_W_OUT].reshape(1, D_MODEL, 514), flat[OFF_O_W_OUT:OFF_O_G_IN].reshape(1, 128, D_MODEL),
            flat[OFF_O_G_IN:OFF_O_G_IN + 1])


def kernel(x, positions, e_g_in, e_w_in, e_g_q_a, e_w_q_up, e_g_kv_a, e_w_kv_up, e_sinks, e_w_out, o_g_in, o_w_in, o_b_f, o_w_out, g_final, loss_target, m_e_g_in, m_e_w_in, m_e_g_q_a, m_e_w_q_up, m_e_g_kv_a, m_e_w_kv_up, m_e_sinks, m_e_w_out, m_o_g_in, m_o_w_in, m_o_b_f, m_o_w_out, m_g_final, v_e_g_in, v_e_w_in, v_e_g_q_a, v_e_w_q_up, v_e_g_kv_a, v_e_w_kv_up, v_e_sinks, v_e_w_out, v_o_g_in, v_o_w_in, v_o_b_f, v_o_w_out, v_g_final):
    g_bits = lax.bitcast_convert_type(o_g_in.reshape(LANES), BF16).reshape(2, LANES)
    g_bits = jnp.pad(g_bits, ((0, R_O_G_IN - 2), (0, 0)))
    shard = _flat_shard(e_w_in[0].astype(BF16), e_w_q_up[0].astype(BF16), e_w_kv_up[0].astype(BF16),
                        e_w_out[0].astype(BF16), o_w_in[0].astype(BF16), o_w_out[0].astype(BF16), g_bits)
    gath = _all_gather(shard, name="weights_all_gather")
    w0 = _layer0_in_weight(_gathered_cols(gath, OFF_E_W_IN, R_E_W_IN, D_MODEL))
    wq = _q_up_weight(_gathered_cols(gath, OFF_E_W_Q, R_E_W_Q, MLA_Q_RANK))
    wkv = _kv_up_weight(_gathered_cols(gath, OFF_E_W_KV, R_E_W_KV, MLA_KV_RANK))
    wo0 = _gathered_rows(gath, OFF_E_W_OUT, R_E_W_OUT, D_MODEL)
    w1, wf = _layer1_in_weight(_gathered_cols(gath, OFF_O_W_IN, R_O_W_IN, D_MODEL))
    wo1 = _gathered_rows(gath, OFF_O_W_OUT, R_O_W_OUT, D_MODEL)
    o_g_full = lax.bitcast_convert_type(gath[:, OFF_O_G_IN:OFF_O_G_IN + 2, :].reshape(N_DEV, LANES, 2), F32)
    o_g_full = o_g_full.reshape(1, D_MODEL)

    gr = _local_step(x[0], positions[0], loss_target[0], e_g_in, w0, e_g_q_a, wq, e_g_kv_a, wkv, e_sinks, wo0,
                     o_g_full, w1, wf, o_b_f, wo1, g_final)

    d_o_g = jnp.pad(gr["o_g_in"].reshape(N_DEV, 1, LANES), ((0, 0), (0, R_O_G_IN - 1), (0, 0)))
    pieces = jnp.concatenate([
        _scatter_cols(_layer0_in_grad(gr["w0"])), _scatter_cols(_q_up_grad(gr["wq"])),
        _scatter_cols(_kv_up_grad(gr["wkv"])), _scatter_rows(gr["wo0"]),
        _scatter_cols(_layer1_in_grad(gr["w1"], gr["wf"])), _scatter_rows(gr["wo1"]), d_o_g], axis=1)
    recv = _exchange(pieces.astype(BF16), name="grads_exchange")
    w_flat = _flat_shard(e_w_in[0], e_w_q_up[0], e_w_kv_up[0], e_w_out[0], o_w_in[0], o_w_out[0], _g_rows(o_g_in))
    m_flat = _flat_shard(m_e_w_in[0], m_e_w_q_up[0], m_e_w_kv_up[0], m_e_w_out[0], m_o_w_in[0], m_o_w_out[0],
                         _g_rows(m_o_g_in))
    v_flat = _flat_shard(v_e_w_in[0], v_e_w_q_up[0], v_e_w_kv_up[0], v_e_w_out[0], v_o_w_in[0], v_o_w_out[0],
                         _g_rows(v_o_g_in))
    flats = _adamw(recv, w_flat, m_flat, v_flat, name="adamw_sharded")
    g_sh, d_sh, m_sh, v_sh = [_unflat_shard(f) for f in flats]

    small = _small_pack(gr["e_g_in"], gr["g_final"], gr["e_g_q_a"], gr["e_g_kv_a"], gr["e_sinks"], gr["o_b_f"], gr["loss"])
    small_all = _all_gather(small, name="small_all_gather")
    zero = jnp.zeros((), F32)
    w_small = _small_pack(e_g_in, g_final, e_g_q_a, e_g_kv_a, e_sinks, o_b_f, zero)
    m_small = _small_pack(m_e_g_in, m_g_final, m_e_g_q_a, m_e_g_kv_a, m_e_sinks, m_o_b_f, zero)
    v_small = _small_pack(v_e_g_in, v_g_final, v_e_g_q_a, v_e_g_kv_a, v_e_sinks, v_o_b_f, zero)
    smalls = _adamw(small_all, w_small, m_small, v_small, name="adamw_replicated")
    g_sm, d_sm, m_sm, v_sm = [_small_unpack(a) for a in smalls]
    loss = g_sm[6]

    def leaves(sh, sm):
        return (sm[0], sh[0], sm[2], sh[1], sm[3], sh[2], sm[4], sh[3], sh[6], sh[4], sm[5], sh[5], sm[1])

    return (loss, gr["grad_x"][None], *leaves(g_sh, g_sm), *leaves(d_sh, d_sm), *leaves(m_sh, m_sm), *leaves(v_sh, v_sm))
```

```python
import functools

import jax
import jax.numpy as jnp
from jax import lax
from jax.experimental import pallas as pl
from jax.experimental.pallas import tpu as pltpu

F32 = jnp.float32
BF16 = jnp.bfloat16
NEG_INF = float("-inf")

N_DEV = 8
LANES = 128
D_MODEL = 1024
EPS = 1e-6
ROPE_THETA = 10000.0
MLA_HEADS = 8
MLA_Q_RANK = 256
MLA_KV_RANK = 128
MLA_NOPE = 64
MLA_ROPE = 32
MLA_V = 64
SWA_HEADS = 8
SWA_KV_HEADS = 2
SWA_DIM = 64
WINDOW = 128
FOX_HEADS = 16
FOX_DIM = 64

ADAM_LR = 0.001
ADAM_B1 = 0.9
ADAM_B2 = 0.999
ADAM_EPS = 1e-08
ADAM_WD = 0.01
ADAM_STEP = 10

ATT_T = 512
VMEM_LIMIT = 56 * 1024 * 1024

Z0A_UNITS = 12
Z0B_UNITS = 6

R_E_W_IN = 1024 * 276 // LANES
R_E_W_Q = 256 * 96 // LANES
R_E_W_KV = 128 * 128 // LANES
R_E_W_OUT = 128 * 1024 // LANES
R_O_W_IN = 1024 * 514 // LANES
R_O_W_OUT = 128 * 1024 // LANES
R_O_G_IN = 16
OFF_E_W_IN = 0
OFF_E_W_Q = OFF_E_W_IN + R_E_W_IN
OFF_E_W_KV = OFF_E_W_Q + R_E_W_Q
OFF_E_W_OUT = OFF_E_W_KV + R_E_W_KV
R_LAYER0 = OFF_E_W_OUT + R_E_W_OUT
SMALL_ROWS = 24


def _tile(n, cands):
    for c in cands:
        if n % c == 0:
            return c
    raise ValueError(f"no tile for {n}")


def _params(sem, vmem=None):
    return pltpu.CompilerParams(dimension_semantics=sem, vmem_limit_bytes=vmem)


def _matmul(a, b, *, name, ta=False, tb=False, add=None, out_dtype=F32):
    if ta:
        kdim, m = a.shape
    else:
        m, kdim = a.shape
    if tb:
        n, kb = b.shape
    else:
        kb, n = b.shape
    assert kdim == kb, (a.shape, b.shape)
    tm = _tile(m, (512, 256, 128))
    tn = _tile(n, (768, 512, 384, 256, 128))
    dims = (((0 if ta else 1,), (1 if tb else 0,)), ((), ()))

    def body(*refs):
        if add is None:
            a_ref, b_ref, o_ref = refs
            add_ref = None
        else:
            a_ref, b_ref, add_ref, o_ref = refs
        r = lax.dot_general(a_ref[...].astype(BF16), b_ref[...].astype(BF16), dims, preferred_element_type=F32)
        if add_ref is not None:
            r = r + add_ref[...]
        o_ref[...] = r.astype(out_dtype)

    a_spec = pl.BlockSpec((kdim, tm), lambda i, j: (0, i)) if ta else pl.BlockSpec((tm, kdim), lambda i, j: (i, 0))
    b_spec = pl.BlockSpec((tn, kdim), lambda i, j: (j, 0)) if tb else pl.BlockSpec((kdim, tn), lambda i, j: (0, j))
    in_specs = [a_spec, b_spec]
    args = [a, b]
    if add is not None:
        in_specs.append(pl.BlockSpec((tm, tn), lambda i, j: (i, j)))
        args.append(add)
    return pl.pallas_call(
        body, name=name, grid=(m // tm, n // tn),
        in_specs=in_specs, out_specs=pl.BlockSpec((tm, tn), lambda i, j: (i, j)),
        out_shape=jax.ShapeDtypeStruct((m, n), out_dtype),
        compiler_params=_params(("parallel", "parallel"), VMEM_LIMIT),
    )(*args)


def _rmsnorm_fwd(x, g, *, width, col_blk, name):
    s = x.shape[0]
    tm = _tile(s, (256, 128))

    def body(x_ref, g_ref, y_ref):
        xf = x_ref[...].astype(F32)
        r = lax.rsqrt(jnp.mean(xf * xf, axis=-1, keepdims=True) + EPS)
        y_ref[...] = ((xf * r) * g_ref[...]).astype(BF16)

    return pl.pallas_call(
        body, name=name, grid=(s // tm,),
        in_specs=[pl.BlockSpec((tm, width), lambda i: (i, col_blk)), pl.BlockSpec((1, width), lambda i: (0, 0))],
        out_specs=pl.BlockSpec((tm, width), lambda i: (i, 0)),
        out_shape=jax.ShapeDtypeStruct((s, width), BF16),
        compiler_params=_params(("parallel",)),
    )(x, g)


def _rmsnorm_bwd(x, g, dy, *, width, col_blk, name, add=None, out_dtype=F32):
    s = x.shape[0]
    tm = _tile(s, (256, 128))

    def body(*refs):
        if add is None:
            x_ref, g_ref, dy_ref, dx_ref, dg_ref = refs
            add_ref = None
        else:
            x_ref, g_ref, dy_ref, add_ref, dx_ref, dg_ref = refs
        i = pl.program_id(0)
        xf = x_ref[...].astype(F32)
        r = lax.rsqrt(jnp.mean(xf * xf, axis=-1, keepdims=True) + EPS)
        xh = xf * r
        dyf = dy_ref[...].astype(F32)

        @pl.when(i == 0)
        def _():
            dg_ref[...] = jnp.zeros_like(dg_ref)

        dg_ref[...] += jnp.sum(dyf * xh, axis=0, keepdims=True)
        dxh = dyf * g_ref[...]
        dx = r * (dxh - xh * jnp.mean(dxh * xh, axis=-1, keepdims=True))
        if add_ref is not None:
            dx = dx + add_ref[...]
        dx_ref[...] = dx.astype(out_dtype)

    in_specs = [pl.BlockSpec((tm, width), lambda i: (i, col_blk)), pl.BlockSpec((1, width), lambda i: (0, 0)),
                pl.BlockSpec((tm, width), lambda i: (i, 0))]
    args = [x, g, dy]
    if add is not None:
        in_specs.append(pl.BlockSpec((tm, width), lambda i: (i, 0)))
        args.append(add)
    return pl.pallas_call(
        body, name=name, grid=(s // tm,),
        in_specs=in_specs,
        out_specs=[pl.BlockSpec((tm, width), lambda i: (i, 0)), pl.BlockSpec((1, width), lambda i: (0, 0))],
        out_shape=[jax.ShapeDtypeStruct((s, width), out_dtype), jax.ShapeDtypeStruct((1, width), F32)],
        compiler_params=_params(("arbitrary",)),
    )(*args)


def _sigmoid(x):
    return 1.0 / (1.0 + jnp.exp(-x))


def _gate_fwd(o_parts, gate, *, name):
    s = gate.shape[0]
    tm = _tile(s, (256, 128))
    n_o = len(o_parts)

    def body(*refs):
        o_refs, g_ref, y_ref = refs[:n_o], refs[n_o], refs[n_o + 1]
        o = o_refs[0][...] if n_o == 1 else jnp.concatenate([r[...] for r in o_refs], axis=1)
        gt = g_ref[...]
        y_ref[...] = (o * (gt * _sigmoid(gt))).astype(BF16)

    in_specs = [pl.BlockSpec((tm, o.shape[1]), lambda i: (i, 0)) for o in o_parts]
    in_specs.append(pl.BlockSpec((tm, D_MODEL), lambda i: (i, 0)))
    return pl.pallas_call(
        body, name=name, grid=(s // tm,), in_specs=in_specs,
        out_specs=pl.BlockSpec((tm, D_MODEL), lambda i: (i, 0)),
        out_shape=jax.ShapeDtypeStruct((s, D_MODEL), BF16),
        compiler_params=_params(("parallel",)),
    )(*o_parts, gate)


def _gate_bwd(d_og, o_parts, gate, *, name):
    s = gate.shape[0]
    tm = _tile(s, (256, 128))
    n_o = len(o_parts)
    widths = [o.shape[1] for o in o_parts]

    def body(*refs):
        d_ref, o_refs, g_ref = refs[0], refs[1:1 + n_o], refs[1 + n_o]
        do_refs, dg_ref = refs[2 + n_o:2 + 2 * n_o], refs[2 + 2 * n_o]
        d = d_ref[...]
        gt = g_ref[...]
        sg = _sigmoid(gt)
        silu = gt * sg
        dsilu = sg * (1.0 + gt * (1.0 - sg))
        o = o_refs[0][...] if n_o == 1 else jnp.concatenate([r[...] for r in o_refs], axis=1)
        dg_ref[...] = (d * o * dsilu).astype(BF16)
        do = d * silu
        off = 0
        for r, w in zip(do_refs, widths):
            r[...] = do[:, off:off + w]
            off += w

    in_specs = [pl.BlockSpec((tm, D_MODEL), lambda i: (i, 0))]
    in_specs += [pl.BlockSpec((tm, w), lambda i: (i, 0)) for w in widths]
    in_specs.append(pl.BlockSpec((tm, D_MODEL), lambda i: (i, 0)))
    out_specs = [pl.BlockSpec((tm, w), lambda i: (i, 0)) for w in widths]
    out_specs.append(pl.BlockSpec((tm, D_MODEL), lambda i: (i, 0)))
    out_shape = [jax.ShapeDtypeStruct((s, w), F32) for w in widths]
    out_shape.append(jax.ShapeDtypeStruct((s, D_MODEL), BF16))
    return pl.pallas_call(
        body, name=name, grid=(s // tm,), in_specs=in_specs, out_specs=out_specs, out_shape=out_shape,
        compiler_params=_params(("parallel",)),
    )(d_og, *o_parts, gate)


def _rot_half(x):
    lane = lax.broadcasted_iota(jnp.int32, x.shape, 1)
    return jnp.where(lane < 80, pltpu.roll(x, LANES - 16, axis=1), pltpu.roll(x, 16, axis=1))


def _rot_half_t(g):
    lane = lax.broadcasted_iota(jnp.int32, g.shape, 1)
    lo = (lane >= MLA_NOPE) & (lane < MLA_NOPE + MLA_ROPE // 2)
    hi = (lane >= MLA_NOPE + MLA_ROPE // 2) & (lane < MLA_NOPE + MLA_ROPE)
    return jnp.where(lo, pltpu.roll(g, LANES - 16, axis=1), jnp.where(hi, pltpu.roll(g, 16, axis=1), 0.0))


def _rope_fwd(qp, kvp, z0a, cos_t, sin_t, *, name):
    s = qp.shape[0]
    tm = _tile(s, (256, 128))
    hw = MLA_HEADS * LANES

    def body(q_ref, k_ref, kpe_ref, c_ref, s_ref, qm_ref, km_ref):
        c = c_ref[...]
        sn = s_ref[...]
        kpe = kpe_ref[...]
        kpe_r = (kpe * c + _rot_half(kpe) * sn).astype(BF16)
        lane = lax.broadcasted_iota(jnp.int32, kpe.shape, 1)
        for h in range(MLA_HEADS):
            sl = slice(h * LANES, (h + 1) * LANES)
            qh = q_ref[:, sl]
            qm_ref[:, sl] = (qh * c + _rot_half(qh) * sn).astype(BF16)
            km_ref[:, sl] = jnp.where(lane < MLA_NOPE, k_ref[:, sl], kpe_r)

    return pl.pallas_call(
        body, name=name, grid=(s // tm,),
        in_specs=[pl.BlockSpec((tm, hw), lambda i: (i, 0)), pl.BlockSpec((tm, hw), lambda i: (i, 0)),
                  pl.BlockSpec((tm, LANES), lambda i: (i, 11)),
                  pl.BlockSpec((tm, LANES), lambda i: (i, 0)), pl.BlockSpec((tm, LANES), lambda i: (i, 0))],
        out_specs=[pl.BlockSpec((tm, hw), lambda i: (i, 0)), pl.BlockSpec((tm, hw), lambda i: (i, 0))],
        out_shape=[jax.ShapeDtypeStruct((s, hw), BF16), jax.ShapeDtypeStruct((s, hw), BF16)],
        compiler_params=_params(("parallel",)),
    )(qp, kvp, z0a, cos_t, sin_t)


def _rope_bwd(dqm, dkm, dvm, cos_t, sin_t, *, name):
    s = dqm.shape[0]
    tm = _tile(s, (256, 128))
    hw = MLA_HEADS * LANES
    vw = MLA_HEADS * MLA_V

    def body(dq_ref, dk_ref, dv_ref, c_ref, s_ref, dqp_ref, dkv_ref, dkpe_ref):
        c = c_ref[...]
        sn = s_ref[...]
        ksum = jnp.zeros((tm, LANES), F32)
        for h in range(MLA_HEADS):
            sl = slice(h * LANES, (h + 1) * LANES)
            dq = dq_ref[:, sl]
            dqp_ref[:, sl] = (dq * c + _rot_half_t(dq * sn)).astype(BF16)
            dk = dk_ref[:, sl]
            dkv_ref[:, sl] = dk.astype(BF16)
            ksum = ksum + dk
        dkv_ref[:, hw:] = dv_ref[...]
        lane = lax.broadcasted_iota(jnp.int32, ksum.shape, 1)
        dkpe = ksum * c + _rot_half_t(ksum * sn)
        dkpe_ref[...] = jnp.where((lane >= MLA_NOPE) & (lane < MLA_NOPE + MLA_ROPE), dkpe, 0.0).astype(BF16)

    return pl.pallas_call(
        body, name=name, grid=(s // tm,),
        in_specs=[pl.BlockSpec((tm, hw), lambda i: (i, 0)), pl.BlockSpec((tm, hw), lambda i: (i, 0)),
                  pl.BlockSpec((tm, vw), lambda i: (i, 0)),
                  pl.BlockSpec((tm, LANES), lambda i: (i, 0)), pl.BlockSpec((tm, LANES), lambda i: (i, 0))],
        out_specs=[pl.BlockSpec((tm, hw), lambda i: (i, 0)), pl.BlockSpec((tm, hw + vw), lambda i: (i, 0)),
                   pl.BlockSpec((tm, LANES), lambda i: (i, 0))],
        out_shape=[jax.ShapeDtypeStruct((s, hw), BF16), jax.ShapeDtypeStruct((s, hw + vw), BF16),
                   jax.ShapeDtypeStruct((s, LANES), BF16)],
        compiler_params=_params(("parallel",)),
    )(dqm, dkm, dvm, cos_t, sin_t)


def _head_mask(shape, a):
    lane = lax.broadcasted_iota(jnp.int32, shape, 1)
    return (lane >= 64 * a) & (lane < 64 * (a + 1))


def _causal_mask(t):
    row = lax.broadcasted_iota(jnp.int32, (t, t), 0)
    col = lax.broadcasted_iota(jnp.int32, (t, t), 1)
    return col <= row


_NT = (((1,), (1,)), ((), ()))
LOG2E = 1.4426950408889634


def _stack_heads(tile, hw):
    lane = lax.broadcasted_iota(jnp.int32, tile.shape, 1)
    z = jnp.zeros_like(tile)
    return jnp.concatenate([jnp.where(lane < hw, tile, z), jnp.where(lane >= hw, tile, z)], axis=0)


def _stacked_rows(r0, r1, t):
    n = r0.shape[-1]
    return jnp.concatenate([jnp.broadcast_to(r0, (t, n)), jnp.broadcast_to(r1, (t, n))], axis=0)


def _stacked_causal_mask(t):
    m = _causal_mask(t)
    return jnp.concatenate([m, m], axis=0)


def _resident(block, index_map):
    return pl.BlockSpec(block, index_map, pipeline_mode=pl.Buffered(1))


def _flash_fwd(q, k, v, bias, *, n_pairs, hw, q_off, k_off, v_off, scale, name, rider=None):
    s = q.shape[0]
    t = min(ATT_T, s)
    nb = s // t
    qw = 2 * hw
    has_bias = bias is not None
    c1 = scale * LOG2E

    def body(*refs):
        refs, ride_refs = _split_rider(refs, rider, n_in=4 if has_bias else 3, n_out=2)
        if has_bias:
            q_ref, k_ref, v_ref, b_ref, o_ref, lse_ref = refs
        else:
            q_ref, k_ref, v_ref, o_ref, lse_ref = refs
            b_ref = None
        _ride_start(rider, ride_refs, pl.program_id(0) == 0)
        cmask = _stacked_causal_mask(t)
        lane_lt64 = lax.broadcasted_iota(jnp.int32, (t, LANES), 1) < 64

        def q_block(i, _):
            r0 = pl.multiple_of(i * t, t)
            qs = _stack_heads(q_ref[pl.ds(r0, t), :], hw)

            def kv_step(j, carry, masked):
                m, l, acc = carry
                c0 = pl.multiple_of(j * t, t)
                sc = lax.dot_general(qs, k_ref[pl.ds(c0, t), :], _NT, preferred_element_type=F32) * c1
                if has_bias:
                    sc = sc + _stacked_rows(b_ref[0, 0, j], b_ref[0, 1, j], t)
                if masked:
                    sc = jnp.where(cmask, sc, NEG_INF)
                m_new = jnp.maximum(m, jnp.max(sc, axis=-1, keepdims=True))
                alpha = jnp.exp2(m - m_new)
                p = jnp.exp2(sc - m_new)
                l_new = alpha * l + jnp.sum(p, axis=-1, keepdims=True)
                pv = jnp.dot(p.astype(BF16), v_ref[pl.ds(c0, t), :], preferred_element_type=F32)
                return m_new, l_new, alpha * acc + pv

            init = (jnp.full((2 * t, 1), NEG_INF, F32), jnp.zeros((2 * t, 1), F32), jnp.zeros((2 * t, LANES), F32))
            carry = lax.fori_loop(0, i, functools.partial(kv_step, masked=False), init)
            m, l, acc = kv_step(i, carry, True)
            out = acc / l
            lse2 = m + jnp.log2(l)
            lse_ref[0, 0, pl.ds(r0, t), :] = lse2[:t]
            lse_ref[0, 1, pl.ds(r0, t), :] = lse2[t:]
            o_ref[pl.ds(r0, t), :] = jnp.where(lane_lt64, out[:t], out[t:])
            return 0

        lax.fori_loop(0, nb, q_block, 0)
        _ride_wait(rider, ride_refs, pl.program_id(0) == n_pairs - 1)

    in_specs = [_resident((s, qw), lambda p: (0, q_off + p)), _resident((s, qw), lambda p: (0, k_off + p)),
                _resident((s, LANES), lambda p: (0, v_off + p))]
    args = [q, k, v]
    if has_bias:
        in_specs.append(_resident((1, 2, nb, 1, t), lambda p: (p, 0, 0, 0, 0)))
        args.append(bias)
    out_specs = [pl.BlockSpec((s, LANES), lambda p: (0, p)), pl.BlockSpec((1, 2, s, 1), lambda p: (p, 0, 0, 0))]
    out_shape = [jax.ShapeDtypeStruct((s, n_pairs * LANES), F32), jax.ShapeDtypeStruct((n_pairs, 2, s, 1), F32)]
    scratch = _add_rider(rider, in_specs, args, out_specs, out_shape)
    return pl.pallas_call(
        body, name=name, grid=(n_pairs,), in_specs=in_specs, out_specs=out_specs, out_shape=out_shape,
        scratch_shapes=scratch,
        compiler_params=_params(("parallel",) if rider is None else ("arbitrary",), VMEM_LIMIT),
    )(*args)


def _flash_bwd(q, k, v, do, o, lse, bias, *, n_pairs, hw, q_off, k_off, v_off, scale, qk_dtype, name, rider=None):
    s = q.shape[0]
    t = min(ATT_T, s)
    nb = s // t
    qw = 2 * hw
    has_bias = bias is not None
    c1 = scale * LOG2E

    def body(*refs):
        refs, ride_refs = _split_rider(refs, rider, n_in=7 if has_bias else 6, n_out=5 if has_bias else 3)
        if has_bias:
            (q_ref, k_ref, v_ref, do_ref, o_ref, lse_ref, b_ref, dq_ref, dk_ref, dv_ref, db_ref, dr_ref,
             dkt_ref, dvt_ref) = refs
            db_ref[...] = jnp.zeros_like(db_ref)
        else:
            q_ref, k_ref, v_ref, do_ref, o_ref, lse_ref, dq_ref, dk_ref, dv_ref, dkt_ref, dvt_ref = refs
            b_ref = db_ref = dr_ref = None
        _ride_start(rider, ride_refs, pl.program_id(0) == 0)
        dkt_ref[...] = jnp.zeros_like(dkt_ref)
        dvt_ref[...] = jnp.zeros_like(dvt_ref)
        cmask = _stacked_causal_mask(t)
        lane_lt_hw = lax.broadcasted_iota(jnp.int32, (t, qw), 1) < hw

        def q_block(i, _):
            r0 = pl.multiple_of(i * t, t)
            qs = _stack_heads(q_ref[pl.ds(r0, t), :], hw)
            dos = _stack_heads(do_ref[pl.ds(r0, t), :], 64)
            ot = o_ref[pl.ds(r0, t), :]
            delta = jnp.sum(dos * jnp.concatenate([ot, ot], axis=0), axis=-1, keepdims=True)
            lse2 = jnp.concatenate([lse_ref[0, 0, pl.ds(r0, t), :], lse_ref[0, 1, pl.ds(r0, t), :]], axis=0)
            dosb = dos.astype(BF16)
            dos_t = dos.T.astype(BF16)
            qs_t = qs.astype(F32).T.astype(BF16)

            def kv_step(j, carry, masked):
                dq, rsum = carry
                c0 = pl.multiple_of(j * t, t)
                kt = k_ref[pl.ds(c0, t), :]
                vt = v_ref[pl.ds(c0, t), :]
                sc = lax.dot_general(qs, kt, _NT, preferred_element_type=F32) * c1
                if has_bias:
                    sc = sc + _stacked_rows(b_ref[0, 0, j], b_ref[0, 1, j], t)
                if masked:
                    sc = jnp.where(cmask, sc, NEG_INF)
                p = jnp.exp2(sc - lse2)
                dp = lax.dot_general(dosb, vt, _NT, preferred_element_type=F32)
                ds = p * (dp - delta)
                dsb = ds.astype(BF16)
                dvt_ref[j] += jnp.dot(dos_t, p.astype(BF16), preferred_element_type=F32)
                dkt_ref[j] += jnp.dot(qs_t, dsb, preferred_element_type=F32)
                if has_bias:
                    db_ref[0, 0, j] += jnp.sum(ds[:t], axis=0, keepdims=True)
                    db_ref[0, 1, j] += jnp.sum(ds[t:], axis=0, keepdims=True)
                    rsum = rsum + jnp.sum(ds, axis=-1, keepdims=True)
                return dq + jnp.dot(dsb, kt, preferred_element_type=F32), rsum

            init = (jnp.zeros((2 * t, qw), F32), jnp.zeros((2 * t, 1), F32))
            carry = lax.fori_loop(0, i, functools.partial(kv_step, masked=False), init)
            dq, rsum = kv_step(i, carry, True)
            dq = dq * scale
            dq_ref[pl.ds(r0, t), :] = jnp.where(lane_lt_hw, dq[:t], dq[t:]).astype(qk_dtype)
            if has_bias:
                dr_ref[0, 0, pl.ds(r0, t), :] = rsum[:t]
                dr_ref[0, 1, pl.ds(r0, t), :] = rsum[t:]
            return 0

        lax.fori_loop(0, nb, q_block, 0)

        def k_block(j, _):
            c0 = pl.multiple_of(j * t, t)
            dk_ref[pl.ds(c0, t), :] = (dkt_ref[j].T * scale).astype(qk_dtype)
            dv_ref[pl.ds(c0, t), :] = dvt_ref[j].T.astype(BF16)
            return 0

        lax.fori_loop(0, nb, k_block, 0)
        _ride_wait(rider, ride_refs, pl.program_id(0) == n_pairs - 1)

    in_specs = [_resident((s, qw), lambda p: (0, q_off + p)), _resident((s, qw), lambda p: (0, k_off + p)),
                _resident((s, LANES), lambda p: (0, v_off + p)),
                _resident((s, LANES), lambda p: (0, p)), _resident((s, LANES), lambda p: (0, p)),
                _resident((1, 2, s, 1), lambda p: (p, 0, 0, 0))]
    args = [q, k, v, do, o, lse]
    out_specs = [pl.BlockSpec((s, qw), lambda p: (0, p)), pl.BlockSpec((s, qw), lambda p: (0, p)),
                 pl.BlockSpec((s, LANES), lambda p: (0, p))]
    out_shape = [jax.ShapeDtypeStruct((s, n_pairs * qw), qk_dtype), jax.ShapeDtypeStruct((s, n_pairs * qw), qk_dtype),
                 jax.ShapeDtypeStruct((s, n_pairs * LANES), BF16)]
    if has_bias:
        in_specs.append(_resident((1, 2, nb, 1, t), lambda p: (p, 0, 0, 0, 0)))
        args.append(bias)
        out_specs.append(pl.BlockSpec((1, 2, nb, 1, t), lambda p: (p, 0, 0, 0, 0)))
        out_shape.append(jax.ShapeDtypeStruct((n_pairs, 2, nb, 1, t), F32))
        out_specs.append(pl.BlockSpec((1, 2, s, 1), lambda p: (p, 0, 0, 0)))
        out_shape.append(jax.ShapeDtypeStruct((n_pairs, 2, s, 1), F32))
    scratch = [pltpu.VMEM((nb, qw, t), F32), pltpu.VMEM((nb, LANES, t), F32)]
    scratch += _add_rider(rider, in_specs, args, out_specs, out_shape)
    return pl.pallas_call(
        body, name=name, grid=(n_pairs,), in_specs=in_specs, out_specs=out_specs, out_shape=out_shape,
        scratch_shapes=scratch,
        compiler_params=_params(("parallel",) if rider is None else ("arbitrary",), VMEM_LIMIT),
    )(*args)


def _alibi_slope(h):
    return 2.0 ** (-8.0 * (h + 1.0) / SWA_HEADS)


SWA_ROWS = 512
SWA_SCALE = SWA_DIM ** -0.5


def _swa_geometry(i):
    w = WINDOW
    r0 = pl.multiple_of(i * w, w)
    b0 = pl.multiple_of(jnp.maximum(i - 1, 0) * w, w)
    row = lax.broadcasted_iota(jnp.int32, (w, 2 * w), 0)
    col = lax.broadcasted_iota(jnp.int32, (w, 2 * w), 1)
    dist = row - col + (r0 - b0)
    valid = (dist >= 0) & (dist < w)
    return r0, b0, dist.astype(F32), valid


def _swa_q_head(qblk, h):
    kv = h // (SWA_HEADS // SWA_KV_HEADS)
    if h % 2 != kv:
        qblk = pltpu.roll(qblk, 64, axis=1)
    return jnp.where(_head_mask(qblk.shape, kv), qblk, 0.0)


def _swa_fwd(z0b, sinks, *, name):
    s = z0b.shape[0]
    w = WINDOW
    rows = min(SWA_ROWS, s)
    per_step = rows // w
    qcols = SWA_HEADS * SWA_DIM

    def body(sink_ref, q_ref, k_ref, v_ref, o_ref, lse_ref):
        g = pl.program_id(0)
        for ii in range(per_step):
            r0, b0, dist, valid = _swa_geometry(g * per_step + ii)
            kb = k_ref[pl.ds(b0, 2 * w), :]
            vb = v_ref[pl.ds(b0, 2 * w), :]
            o_heads = []
            for h in range(SWA_HEADS):
                kv = h // (SWA_HEADS // SWA_KV_HEADS)
                blk = h // 2
                qh = _swa_q_head(q_ref[ii * w:(ii + 1) * w, blk * LANES:(blk + 1) * LANES].astype(F32), h).astype(BF16)
                sc = lax.dot_general(qh, kb, _NT, preferred_element_type=F32) * SWA_SCALE - _alibi_slope(h) * dist
                sc = jnp.where(valid, sc, NEG_INF)
                sink = sink_ref[0, h]
                m = jnp.maximum(jnp.max(sc, axis=-1, keepdims=True), sink)
                p = jnp.exp(sc - m)
                l = jnp.sum(p, axis=-1, keepdims=True) + jnp.exp(sink - m)
                oh = jnp.dot(p.astype(BF16), vb, preferred_element_type=F32) / l
                if h % 2 != kv:
                    oh = pltpu.roll(oh, 64, axis=1)
                o_heads.append(oh)
                lse_ref[h, ii * w:(ii + 1) * w, :] = m + jnp.log(l)
            lt64 = lax.broadcasted_iota(jnp.int32, (w, LANES), 1) < 64
            o_ref[ii * w:(ii + 1) * w, :] = jnp.concatenate(
                [jnp.where(lt64, o_heads[2 * b], o_heads[2 * b + 1]) for b in range(SWA_HEADS // 2)], axis=1)

    return pl.pallas_call(
        body, name=name, grid=(s // rows,),
        in_specs=[pl.BlockSpec(memory_space=pltpu.SMEM),
                  pl.BlockSpec((rows, qcols), lambda g: (g, 0)),
                  pl.BlockSpec((s, LANES), lambda g: (0, 4)), pl.BlockSpec((s, LANES), lambda g: (0, 5))],
        out_specs=[pl.BlockSpec((rows, qcols), lambda g: (g, 0)), pl.BlockSpec((SWA_HEADS, rows, 1), lambda g: (0, g, 0))],
        out_shape=[jax.ShapeDtypeStruct((s, qcols), F32), jax.ShapeDtypeStruct((SWA_HEADS, s, 1), F32)],
        compiler_params=_params(("parallel",), VMEM_LIMIT),
    )(sinks, z0b, z0b, z0b)


def _swa_bwd(z0b, sinks, do, o, lse, *, name):
    s = z0b.shape[0]
    w = WINDOW
    rows = min(SWA_ROWS, s)
    per_step = rows // w
    qcols = SWA_HEADS * SWA_DIM
    nblk = s // w

    def body(sink_ref, q_ref, k_ref, v_ref, do_ref, o_ref, lse_ref, dq_ref, dkt_ref, dvt_ref, dsink_ref):
        g = pl.program_id(0)

        @pl.when(g == 0)
        def _():
            dkt_ref[...] = jnp.zeros_like(dkt_ref)
            dvt_ref[...] = jnp.zeros_like(dvt_ref)
            dsink_ref[...] = jnp.zeros_like(dsink_ref)

        for ii in range(per_step):
            i = g * per_step + ii
            r0, b0, dist, valid = _swa_geometry(i)
            j0 = jnp.maximum(i - 1, 0)
            kb = k_ref[pl.ds(b0, 2 * w), :]
            vb = v_ref[pl.ds(b0, 2 * w), :]
            dq_heads = []
            for h in range(SWA_HEADS):
                kv = h // (SWA_HEADS // SWA_KV_HEADS)
                blk = h // 2
                cs = slice(blk * LANES, (blk + 1) * LANES)
                rs = slice(ii * w, (ii + 1) * w)
                qh32 = _swa_q_head(q_ref[rs, cs].astype(F32), h)
                qh = qh32.astype(BF16)
                doh32 = _swa_q_head(do_ref[rs, cs], h)
                oh32 = _swa_q_head(o_ref[rs, cs], h)
                delta = jnp.sum(doh32 * oh32, axis=-1, keepdims=True)
                lse = lse_ref[h, rs, :]
                sink = sink_ref[0, h]
                sc = lax.dot_general(qh, kb, _NT, preferred_element_type=F32) * SWA_SCALE - _alibi_slope(h) * dist
                sc = jnp.where(valid, sc, NEG_INF)
                p = jnp.exp(sc - lse)
                dp = lax.dot_general(doh32.astype(BF16), vb, _NT, preferred_element_type=F32)
                ds = p * (dp - delta)
                dsb = ds.astype(BF16)
                pb = p.astype(BF16)
                dsink_ref[h:h + 1, :] += jnp.broadcast_to(-jnp.sum(jnp.exp(sink - lse) * delta), (1, LANES))
                do_t = doh32.T.astype(BF16)
                q_t = qh32.T.astype(BF16)
                dvt = jnp.dot(do_t, pb, preferred_element_type=F32)
                dkt = jnp.dot(q_t, dsb, preferred_element_type=F32) * SWA_SCALE
                dvt_ref[j0] += dvt[:, :w]
                dvt_ref[j0 + 1] += dvt[:, w:]
                dkt_ref[j0] += dkt[:, :w]
                dkt_ref[j0 + 1] += dkt[:, w:]
                dq = jnp.dot(dsb, kb, preferred_element_type=F32) * SWA_SCALE
                if h % 2 != kv:
                    dq = pltpu.roll(dq, 64, axis=1)
                dq_heads.append(dq)
            lt64 = lax.broadcasted_iota(jnp.int32, (w, LANES), 1) < 64
            dq_ref[ii * w:(ii + 1) * w, :] = jnp.concatenate(
                [jnp.where(lt64, dq_heads[2 * b], dq_heads[2 * b + 1]) for b in range(SWA_HEADS // 2)], axis=1)

    return pl.pallas_call(
        body, name=name, grid=(s // rows,),
        in_specs=[pl.BlockSpec(memory_space=pltpu.SMEM),
                  pl.BlockSpec((rows, qcols), lambda g: (g, 0)),
                  pl.BlockSpec((s, LANES), lambda g: (0, 4)), pl.BlockSpec((s, LANES), lambda g: (0, 5)),
                  pl.BlockSpec((rows, qcols), lambda g: (g, 0)), pl.BlockSpec((rows, qcols), lambda g: (g, 0)),
                  pl.BlockSpec((SWA_HEADS, rows, 1), lambda g: (0, g, 0))],
        out_specs=[pl.BlockSpec((rows, qcols), lambda g: (g, 0)),
                   pl.BlockSpec((nblk, LANES, w), lambda g: (0, 0, 0)),
                   pl.BlockSpec((nblk, LANES, w), lambda g: (0, 0, 0)),
                   pl.BlockSpec((SWA_HEADS, LANES), lambda g: (0, 0))],
        out_shape=[jax.ShapeDtypeStruct((s, qcols), F32),
                   jax.ShapeDtypeStruct((nblk, LANES, w), F32), jax.ShapeDtypeStruct((nblk, LANES, w), F32),
                   jax.ShapeDtypeStruct((SWA_HEADS, LANES), F32)],
        compiler_params=_params(("arbitrary",), VMEM_LIMIT),
    )(sinks, z0b, z0b, z0b, do, o, lse)


CUM_T = 256


def _split3(x):
    hi = x.astype(BF16)
    r1 = x - hi.astype(F32)
    mid = r1.astype(BF16)
    lo = (r1 - mid.astype(F32)).astype(BF16)
    return hi, mid, lo


def _tri_dot(tri, x):
    hi, mid, lo = _split3(x)
    out = jnp.dot(tri, hi, preferred_element_type=F32)
    out = out + jnp.dot(tri, mid, preferred_element_type=F32)
    return out + jnp.dot(tri, lo, preferred_element_type=F32)


def _logf_fwd(zf, bf, *, name):
    s = zf.shape[0]
    t = CUM_T
    nb = s // t

    def body(z_ref, b_ref, c_ref, carry_ref):
        i = pl.program_id(0)

        @pl.when(i == 0)
        def _():
            carry_ref[...] = jnp.zeros_like(carry_ref)

        x = z_ref[...] + b_ref[...]
        lf = jnp.minimum(x, 0.0) - jnp.log(1.0 + jnp.exp(-jnp.abs(x)))
        row = lax.broadcasted_iota(jnp.int32, (t, t), 0)
        col = lax.broadcasted_iota(jnp.int32, (t, t), 1)
        tri = jnp.where(col <= row, 1.0, 0.0).astype(BF16)
        c = _tri_dot(tri, lf) + carry_ref[...]
        c_ref[...] = c
        carry_ref[...] = c[t - 1:t, :]

    return pl.pallas_call(
        body, name=name, grid=(nb,),
        in_specs=[pl.BlockSpec((t, LANES), lambda i: (i, 0)), pl.BlockSpec((1, LANES), lambda i: (0, 0))],
        out_specs=pl.BlockSpec((t, LANES), lambda i: (i, 0)),
        out_shape=jax.ShapeDtypeStruct((s, LANES), F32),
        scratch_shapes=[pltpu.VMEM((1, LANES), F32)],
        compiler_params=_params(("arbitrary",)),
    )(zf, bf)


def _logf_bwd(dc, zf, bf, *, name):
    s = zf.shape[0]
    t = CUM_T
    nb = s // t

    def body(dc_ref, z_ref, b_ref, dz_ref, db_ref, carry_ref):
        i = pl.program_id(0)

        @pl.when(i == 0)
        def _():
            carry_ref[...] = jnp.zeros_like(carry_ref)
            db_ref[...] = jnp.zeros_like(db_ref)

        row = lax.broadcasted_iota(jnp.int32, (t, t), 0)
        col = lax.broadcasted_iota(jnp.int32, (t, t), 1)
        tri = jnp.where(col >= row, 1.0, 0.0).astype(BF16)
        dlf = _tri_dot(tri, dc_ref[...]) + carry_ref[...]
        carry_ref[...] = dlf[0:1, :]
        x = z_ref[...] + b_ref[...]
        dz = dlf * _sigmoid(-x)
        dz_ref[...] = dz.astype(BF16)
        db_ref[...] += jnp.sum(dz, axis=0, keepdims=True)

    return pl.pallas_call(
        body, name=name, grid=(nb,),
        in_specs=[pl.BlockSpec((t, LANES), lambda i: (nb - 1 - i, 0)), pl.BlockSpec((t, LANES), lambda i: (nb - 1 - i, 0)),
                  pl.BlockSpec((1, LANES), lambda i: (0, 0))],
        out_specs=[pl.BlockSpec((t, LANES), lambda i: (nb - 1 - i, 0)), pl.BlockSpec((1, LANES), lambda i: (0, 0))],
        out_shape=[jax.ShapeDtypeStruct((s, LANES), BF16), jax.ShapeDtypeStruct((1, LANES), F32)],
        scratch_shapes=[pltpu.VMEM((1, LANES), F32)],
        compiler_params=_params(("arbitrary",)),
    )(dc, zf, bf)


def _loss_head(x2, g, target, *, name):
    s = x2.shape[0]
    tm = _tile(s, (256, 128))

    def body(x_ref, g_ref, t_ref, dx_ref, loss_ref, dg_ref):
        i = pl.program_id(0)

        @pl.when(i == 0)
        def _():
            loss_ref[...] = jnp.zeros_like(loss_ref)
            dg_ref[...] = jnp.zeros_like(dg_ref)

        xf = x_ref[...]
        r = lax.rsqrt(jnp.mean(xf * xf, axis=-1, keepdims=True) + EPS)
        xh = xf * r
        gv = g_ref[...]
        err = xh * gv - t_ref[...]
        loss_ref[...] += jnp.broadcast_to(0.5 * jnp.sum(jnp.mean(err * err, axis=-1, keepdims=True)), loss_ref.shape)
        dy = err * (1.0 / D_MODEL)
        dg_ref[...] += jnp.sum(dy * xh, axis=0, keepdims=True)
        dxh = dy * gv
        dx_ref[...] = r * (dxh - xh * jnp.mean(dxh * xh, axis=-1, keepdims=True))

    return pl.pallas_call(
        body, name=name, grid=(s // tm,),
        in_specs=[pl.BlockSpec((tm, D_MODEL), lambda i: (i, 0)), pl.BlockSpec((1, D_MODEL), lambda i: (0, 0)),
                  pl.BlockSpec((tm, D_MODEL), lambda i: (i, 0))],
        out_specs=[pl.BlockSpec((tm, D_MODEL), lambda i: (i, 0)), pl.BlockSpec((8, LANES), lambda i: (0, 0)),
                   pl.BlockSpec((1, D_MODEL), lambda i: (0, 0))],
        out_shape=[jax.ShapeDtypeStruct((s, D_MODEL), F32), jax.ShapeDtypeStruct((8, LANES), F32),
                   jax.ShapeDtypeStruct((1, D_MODEL), F32)],
        compiler_params=_params(("arbitrary",)),
    )(x2, g, target)


def _adamw(pieces, w, m, v, *, name):
    rows = w.shape[0]
    tr = _tile(rows, (1184, 736, SMALL_ROWS))
    bc1 = 1.0 - ADAM_B1 ** ADAM_STEP
    bc2 = 1.0 - ADAM_B2 ** ADAM_STEP

    def body(p_ref, w_ref, m_ref, v_ref, g_ref, d_ref, nm_ref, nv_ref):
        g = p_ref[0].astype(F32)
        for k in range(1, N_DEV):
            g = g + p_ref[k].astype(F32)
        nm = ADAM_B1 * m_ref[...] + (1.0 - ADAM_B1) * g
        nv = ADAM_B2 * v_ref[...] + (1.0 - ADAM_B2) * (g * g)
        m_hat = nm / bc1
        v_hat = nv / bc2
        g_ref[...] = g
        d_ref[...] = -ADAM_LR * (m_hat / (jnp.sqrt(v_hat) + ADAM_EPS) + ADAM_WD * w_ref[...])
        nm_ref[...] = nm
        nv_ref[...] = nv

    spec = pl.BlockSpec((tr, LANES), lambda i: (i, 0))
    shape = jax.ShapeDtypeStruct((rows, LANES), F32)
    return pl.pallas_call(
        body, name=name, grid=(rows // tr,),
        in_specs=[pl.BlockSpec((N_DEV, tr, LANES), lambda i: (0, i, 0)), spec, spec, spec],
        out_specs=[spec, spec, spec, spec], out_shape=[shape, shape, shape, shape],
        compiler_params=_params(("parallel",)),
    )(pieces, w, m, v)


MESH = pl.DeviceIdType.MESH
ANY = pl.BlockSpec(memory_space=pl.ANY)


def _all_gather(shard, *, name):
    rows, lanes = shard.shape

    def body(x_ref, out_ref, send_sems, recv_sems, local_sem):
        x, y, c = lax.axis_index("x"), lax.axis_index("y"), lax.axis_index("c")
        me, sibling = (x, y, c), (x, y, 1 - c)
        chips = [(1 - x, y), (x, 1 - y), (1 - x, 1 - y)]

        def block(px, py, pc):
            return out_ref.at[4 * px + 2 * py + pc]

        def copy(k, blk, to, src=None):
            return pltpu.make_async_remote_copy(
                src_ref=block(*blk) if src is None else src, dst_ref=block(*blk),
                send_sem=send_sems.at[k], recv_sem=recv_sems.at[k], device_id=to, device_id_type=MESH)

        mine = pltpu.make_async_copy(x_ref, block(*me), local_sem)
        mine.start()
        first = [copy(0, me, sibling, src=x_ref)]
        first += [copy(1 + j, me, (*chip, c), src=x_ref) for j, chip in enumerate(chips)]
        for cp in first:
            cp.start()
        passed = [copy(4 + j, (*chip, c), sibling) for j, chip in enumerate(chips)]
        for j, chip in enumerate(chips):
            copy(1 + j, (*chip, c), me).wait_recv()
            passed[j].start()
        copy(0, sibling, me).wait_recv()
        for j, chip in enumerate(chips):
            copy(4 + j, (*chip, 1 - c), me).wait_recv()
        for cp in first + passed:
            cp.wait_send()
        mine.wait()

    return pl.pallas_call(
        body, name=name, out_shape=jax.ShapeDtypeStruct((N_DEV, rows, lanes), shard.dtype),
        in_specs=[ANY], out_specs=ANY,
        scratch_shapes=[pltpu.SemaphoreType.DMA((7,)), pltpu.SemaphoreType.DMA((7,)), pltpu.SemaphoreType.DMA(())],
    )(shard)


def _peer_copies(kind, src_ref, out_ref, send_sems, recv_sems, local_sem):
    x, y, c = lax.axis_index("x"), lax.axis_index("y"), lax.axis_index("c")
    me = 4 * x + 2 * y + c

    def src(idx):
        return src_ref.at[idx] if kind == "exchange" else src_ref

    mine = pltpu.make_async_copy(src(me), out_ref.at[me], local_sem)
    copies = []
    for r in range(1, N_DEV):
        px = 1 - x if r & 4 else x
        py = 1 - y if r & 2 else y
        pc = 1 - c if r & 1 else c
        copies.append(pltpu.make_async_remote_copy(
            src_ref=src(4 * px + 2 * py + pc), dst_ref=out_ref.at[me],
            send_sem=send_sems.at[r - 1], recv_sem=recv_sems.at[r - 1],
            device_id=(px, py, pc), device_id_type=MESH))
    return mine, copies


PEER_SEMS = [pltpu.SemaphoreType.DMA((7,)), pltpu.SemaphoreType.DMA((7,)), pltpu.SemaphoreType.DMA(())]


def _exchange(pieces, *, name):
    def body(g_ref, out_ref, send_sems, recv_sems, local_sem):
        mine, copies = _peer_copies("exchange", g_ref, out_ref, send_sems, recv_sems, local_sem)
        mine.start()
        for cp in copies:
            cp.start()
        for cp in copies:
            cp.wait()
        mine.wait()

    return pl.pallas_call(
        body, name=name, out_shape=jax.ShapeDtypeStruct(pieces.shape, pieces.dtype),
        in_specs=[ANY], out_specs=ANY, scratch_shapes=list(PEER_SEMS),
    )(pieces)


def _add_rider(rider, in_specs, args, out_specs, out_shape):
    if rider is None:
        return []
    _, arr = rider
    in_specs.append(ANY)
    args.append(arr)
    out_specs.append(ANY)
    out_shape.append(jax.ShapeDtypeStruct((N_DEV,) + arr.shape[-2:], arr.dtype))
    return list(PEER_SEMS)


def _split_rider(refs, rider, n_in, n_out):
    if rider is None:
        return refs, None
    refs = list(refs)
    rin = refs.pop(n_in)
    rout = refs.pop(n_in + n_out)
    return refs[:-3], (rin, rout, *refs[-3:])


def _ride_start(rider, ride_refs, first):
    if rider is None:
        return

    @pl.when(first)
    def _():
        mine, copies = _peer_copies(rider[0], *ride_refs)
        mine.start()
        for cp in copies:
            cp.start()


def _ride_wait(rider, ride_refs, last):
    if rider is None:
        return

    @pl.when(last)
    def _():
        mine, copies = _peer_copies(rider[0], *ride_refs)
        for cp in copies:
            cp.wait()
        mine.wait()


def _gathered_cols(gath, off, nrows, kdim):
    n = nrows * LANES // kdim
    return gath[:, off:off + nrows, :].reshape(N_DEV, kdim, n).transpose(1, 0, 2).reshape(kdim, N_DEV * n)


def _gathered_rows(gath, off, nrows, ncols):
    return gath[:, off:off + nrows, :].reshape(-1, ncols)


def _scatter_cols(dw):
    kdim, n8 = dw.shape
    n = n8 // N_DEV
    return dw.reshape(kdim, N_DEV, n).transpose(1, 0, 2).reshape(N_DEV, kdim * n // LANES, LANES)


def _scatter_rows(dw):
    return dw.reshape(N_DEV, -1, LANES)


def _layer0_in_weight(w_in):
    cq, ckv, kpe = w_in[:, 0:256], w_in[:, 256:384], w_in[:, 384:416]
    q_s, k_s, v_s, gate = w_in[:, 416:928], w_in[:, 928:1056], w_in[:, 1056:1184], w_in[:, 1184:2208]
    z = jnp.zeros((w_in.shape[0], 64), w_in.dtype)
    return jnp.concatenate([gate, cq, ckv, z, kpe, z[:, :32], q_s, k_s, v_s], axis=1)


def _layer0_in_grad(dwp):
    gate, cq, ckv, kpe = dwp[:, 0:1024], dwp[:, 1024:1280], dwp[:, 1280:1408], dwp[:, 1472:1504]
    q_s, k_s, v_s = dwp[:, 1536:2048], dwp[:, 2048:2176], dwp[:, 2176:2304]
    return jnp.concatenate([cq, ckv, kpe, q_s, k_s, v_s, gate], axis=1)


def _q_up_weight(w):
    return jnp.pad(w.reshape(MLA_Q_RANK, MLA_HEADS, 96), ((0, 0), (0, 0), (0, 32))).reshape(MLA_Q_RANK, MLA_HEADS * LANES)


def _q_up_grad(dwp):
    return dwp.reshape(MLA_Q_RANK, MLA_HEADS, LANES)[:, :, :96].reshape(MLA_Q_RANK, MLA_HEADS * 96)


def _kv_up_weight(w):
    w4 = w.reshape(MLA_KV_RANK, MLA_HEADS, 2, 64)
    kp = jnp.pad(w4[:, :, 0, :], ((0, 0), (0, 0), (0, 64))).reshape(MLA_KV_RANK, MLA_HEADS * LANES)
    vp = w4[:, :, 1, :].reshape(MLA_KV_RANK, MLA_HEADS * 64)
    return jnp.concatenate([kp, vp], axis=1)


def _kv_up_grad(dwp):
    dk = dwp[:, :MLA_HEADS * LANES].reshape(MLA_KV_RANK, MLA_HEADS, LANES)[:, :, :64]
    dv = dwp[:, MLA_HEADS * LANES:].reshape(MLA_KV_RANK, MLA_HEADS, 64)
    return jnp.stack([dk, dv], axis=2).reshape(MLA_KV_RANK, MLA_HEADS * LANES)


def _pad_lanes(a):
    return jnp.pad(a, ((0, 0), (0, LANES - a.shape[1])))


def _small_pack(g_in, g_final, g_q_a, g_kv_a, sinks, b_f, loss):
    rows = [g_in.reshape(8, LANES), g_final.reshape(8, LANES), g_q_a.reshape(2, LANES), g_kv_a.reshape(1, LANES),
            _pad_lanes(sinks.reshape(1, -1)), _pad_lanes(b_f.reshape(1, -1)), _pad_lanes(loss.reshape(1, 1)),
            jnp.zeros((2, LANES), F32)]
    return jnp.concatenate(rows, axis=0)


def _small_unpack(a):
    return (a[0:8].reshape(1, D_MODEL), a[8:16].reshape(D_MODEL), a[16:18].reshape(1, MLA_Q_RANK),
            a[18:19].reshape(1, MLA_KV_RANK), a[19:20, :SWA_HEADS], a[20:21, :FOX_HEADS], a[21, 0])


def _local_step(x, positions, target, e_g_in, w0, e_g_q_a, wq, e_g_kv_a, wkv, e_sinks, wo0,
                layer1, o_b_f, g_final, scatter1=None):
    s = x.shape[0]
    att_t = min(ATT_T, s)
    nb = s // att_t
    mla_scale = (MLA_NOPE + MLA_ROPE) ** -0.5
    fox_scale = FOX_DIM ** -0.5
    n0a = Z0A_UNITS * LANES

    inv_freq = 1.0 / (ROPE_THETA ** (jnp.arange(0, MLA_ROPE, 2, dtype=F32) / MLA_ROPE))
    ang = positions.astype(F32)[:, None] * inv_freq
    cos, sin = jnp.cos(ang), jnp.sin(ang)
    ones, zeros = jnp.ones((s, 64), F32), jnp.zeros((s, 64), F32)
    cos_t = jnp.concatenate([ones, cos, cos, ones[:, :32]], axis=1)
    sin_t = jnp.concatenate([zeros, -sin, sin, zeros[:, :32]], axis=1)

    h0 = _rmsnorm_fwd(x, e_g_in, width=D_MODEL, col_blk=0, name="l0_norm")
    z0a = _matmul(h0, w0[:, :n0a], name="l0_in_a")
    z0b = _matmul(h0, w0[:, n0a:], name="l0_in_b", out_dtype=BF16)
    cqn = _rmsnorm_fwd(z0a, e_g_q_a, width=MLA_Q_RANK, col_blk=4, name="l0_q_norm")
    ckvn = _rmsnorm_fwd(z0a, e_g_kv_a, width=MLA_KV_RANK, col_blk=10, name="l0_kv_norm")
    qp = _matmul(cqn, wq, name="l0_q_up")
    kvp = _matmul(ckvn, wkv, name="l0_kv_up", out_dtype=BF16)
    qm, km = _rope_fwd(qp, kvp, z0a, cos_t, sin_t, name="l0_rope")
    gathers = len(layer1) == 2
    res = _flash_fwd(qm, km, kvp, None, n_pairs=MLA_HEADS // 2, hw=LANES, q_off=0, k_off=0, v_off=MLA_HEADS,
                     scale=mla_scale, name="l0_mla_fwd", rider=("gather", layer1[0]) if gathers else None)
    o_mla, lse_mla = res[0], res[1]
    o_g_in, w1, wf, wo1 = layer1[1](res[2]) if gathers else layer1
    o_swa, lse_swa = _swa_fwd(z0b, e_sinks, name="l0_swa_fwd")
    og0 = _gate_fwd([o_mla, o_swa], z0a, name="l0_gate")
    x1 = _matmul(og0, wo0, add=x, name="l0_out")

    h1 = _rmsnorm_fwd(x1, o_g_in, width=D_MODEL, col_blk=0, name="l1_norm")
    z1 = _matmul(h1, w1[:, :3 * D_MODEL], name="l1_in_qkv", out_dtype=BF16)
    gate1 = _matmul(h1, w1[:, 3 * D_MODEL:], name="l1_in_gate")
    zf = _matmul(h1, wf, name="l1_in_f")
    bf = _pad_lanes(o_b_f)
    log_cum = _logf_fwd(zf, bf, name="l1_logf")
    bias = (-LOG2E * log_cum[:, :FOX_HEADS]).T.reshape(FOX_HEADS // 2, 2, nb, 1, att_t)
    o_fox, lse_fox = _flash_fwd(z1, z1, z1, bias, n_pairs=FOX_HEADS // 2, hw=64, q_off=0, k_off=8, v_off=16,
                                scale=fox_scale, name="l1_fox_fwd")
    og1 = _gate_fwd([o_fox], gate1, name="l1_gate")
    x2 = _matmul(og1, wo1, add=x1, name="l1_out")

    dx2, loss_part, d_g_final = _loss_head(x2, g_final.reshape(1, D_MODEL), target, name="loss_head")

    d_wo1 = _matmul(og1, dx2, ta=True, name="l1_out_dw")
    d_og1 = _matmul(dx2, wo1, tb=True, name="l1_out_dx")
    do_fox, d_gate1 = _gate_bwd(d_og1, [o_fox], gate1, name="l1_gate_bwd")
    dq1, dk1, dv1, dbias, drow = _flash_bwd(z1, z1, z1, do_fox, o_fox, lse_fox, bias, n_pairs=FOX_HEADS // 2, hw=64,
                                            q_off=0, k_off=8, v_off=16, scale=fox_scale, qk_dtype=BF16,
                                            name="l1_fox_bwd")
    d_log_cum = (drow.reshape(FOX_HEADS, s) - dbias.reshape(FOX_HEADS, s)).T
    d_log_cum = jnp.pad(d_log_cum, ((0, 0), (0, LANES - FOX_HEADS)))
    d_zf, d_bf = _logf_bwd(d_log_cum, zf, bf, name="l1_logf_bwd")
    dz1 = jnp.concatenate([dq1, dk1, dv1, d_gate1], axis=1)
    d_w1 = _matmul(h1, dz1, ta=True, name="l1_in_dw")
    d_wf = _matmul(h1, d_zf, ta=True, name="l1_in_f_dw")
    dh1 = _matmul(dz1, w1, tb=True, name="l1_in_dx")
    dh1 = _matmul(d_zf, wf, tb=True, add=dh1, name="l1_in_f_dx")
    dx1, d_o_g_in = _rmsnorm_bwd(x1, o_g_in, dh1, width=D_MODEL, col_blk=0, add=dx2, name="l1_norm_bwd")

    d_wo0 = _matmul(og0, dx1, ta=True, name="l0_out_dw")
    d_og0 = _matmul(dx1, wo0, tb=True, name="l0_out_dx")
    do_mla, do_swa, d_gate0 = _gate_bwd(d_og0, [o_mla, o_swa], z0a, name="l0_gate_bwd")
    dq_s, dkt_s, dvt_s, d_sinks = _swa_bwd(z0b, e_sinks, do_swa, o_swa, lse_swa, name="l0_swa_bwd")
    dk_s = dkt_s.transpose(0, 2, 1).reshape(s, LANES)
    dv_s = dvt_s.transpose(0, 2, 1).reshape(s, LANES)
    rider = None
    if scatter1 is not None:
        rider = ("exchange", scatter1(dict(w1=d_w1, wf=d_wf, wo1=d_wo1, o_g_in=d_o_g_in)))
    res = _flash_bwd(qm, km, kvp, do_mla, o_mla, lse_mla, None, n_pairs=MLA_HEADS // 2, hw=LANES, q_off=0, k_off=0,
                     v_off=MLA_HEADS, scale=mla_scale, qk_dtype=F32, name="l0_mla_bwd", rider=rider)
    dqm, dkm, dvm = res[0], res[1], res[2]
    recv1 = res[3] if rider is not None else None
    d_qp, d_kvp, d_kpe = _rope_bwd(dqm, dkm, dvm, cos_t, sin_t, name="l0_rope_bwd")
    d_wq = _matmul(cqn, d_qp, ta=True, name="l0_q_up_dw")
    d_cqn = _matmul(d_qp, wq, tb=True, name="l0_q_up_dx")
    d_wkv = _matmul(ckvn, d_kvp, ta=True, name="l0_kv_up_dw")
    d_ckvn = _matmul(d_kvp, wkv, tb=True, name="l0_kv_up_dx")
    d_cq, d_g_q_a = _rmsnorm_bwd(z0a, e_g_q_a, d_cqn, width=MLA_Q_RANK, col_blk=4, out_dtype=BF16, name="l0_q_norm_bwd")
    d_ckv, d_g_kv_a = _rmsnorm_bwd(z0a, e_g_kv_a, d_ckvn, width=MLA_KV_RANK, col_blk=10, out_dtype=BF16,
                                   name="l0_kv_norm_bwd")
    dz0 = jnp.concatenate([d_gate0, d_cq, d_ckv, d_kpe, dq_s.astype(BF16), dk_s.astype(BF16), dv_s.astype(BF16)], axis=1)
    d_w0 = _matmul(h0, dz0, ta=True, name="l0_in_dw")
    dh0 = _matmul(dz0, w0, tb=True, name="l0_in_dx")
    grad_x, d_e_g_in = _rmsnorm_bwd(x, e_g_in, dh0, width=D_MODEL, col_blk=0, add=dx1, name="l0_norm_bwd")

    return dict(recv1=recv1, loss=loss_part[0, 0], grad_x=grad_x, e_g_in=d_e_g_in, w0=d_w0, e_g_q_a=d_g_q_a, wq=d_wq,
                e_g_kv_a=d_g_kv_a, wkv=d_wkv, e_sinks=d_sinks[:, 0].reshape(1, SWA_HEADS), wo0=d_wo0,
                o_g_in=d_o_g_in, w1=d_w1, wf=d_wf, o_b_f=d_bf[:, :FOX_HEADS], wo1=d_wo1, g_final=d_g_final.reshape(D_MODEL))


def _layer1_in_weight(w_in):
    main = jnp.concatenate([w_in[:, :3 * D_MODEL], w_in[:, 3 * D_MODEL + FOX_HEADS:]], axis=1)
    wf = jnp.pad(w_in[:, 3 * D_MODEL:3 * D_MODEL + FOX_HEADS], ((0, 0), (0, LANES - FOX_HEADS)))
    return main, wf


def _layer1_in_grad(d_main, d_wf):
    return jnp.concatenate([d_main[:, :3 * D_MODEL], d_wf[:, :FOX_HEADS], d_main[:, 3 * D_MODEL:]], axis=1)


def _flat_rows(*parts):
    return jnp.concatenate([p.reshape(-1, LANES) for p in parts], axis=0)


def _g_rows(a):
    return jnp.pad(a, ((0, R_O_G_IN - 1), (0, 0)))


def _unflat0(flat):
    return (flat[OFF_E_W_IN:OFF_E_W_Q].reshape(1, D_MODEL, 276), flat[OFF_E_W_Q:OFF_E_W_KV].reshape(1, MLA_Q_RANK, 96),
            flat[OFF_E_W_KV:OFF_E_W_OUT].reshape(1, MLA_KV_RANK, 128), flat[OFF_E_W_OUT:R_LAYER0].reshape(1, 128, D_MODEL))


def _unflat1(flat):
    return (flat[:R_O_W_IN].reshape(1, D_MODEL, 514), flat[R_O_W_IN:R_O_W_IN + R_O_W_OUT].reshape(1, 128, D_MODEL),
            flat[R_O_W_IN + R_O_W_OUT:R_O_W_IN + R_O_W_OUT + 1])


def kernel(x, positions, e_g_in, e_w_in, e_g_q_a, e_w_q_up, e_g_kv_a, e_w_kv_up, e_sinks, e_w_out, o_g_in, o_w_in, o_b_f, o_w_out, g_final, loss_target, m_e_g_in, m_e_w_in, m_e_g_q_a, m_e_w_q_up, m_e_g_kv_a, m_e_w_kv_up, m_e_sinks, m_e_w_out, m_o_g_in, m_o_w_in, m_o_b_f, m_o_w_out, m_g_final, v_e_g_in, v_e_w_in, v_e_g_q_a, v_e_w_q_up, v_e_g_kv_a, v_e_w_kv_up, v_e_sinks, v_e_w_out, v_o_g_in, v_o_w_in, v_o_b_f, v_o_w_out, v_g_final):
    shard0 = _flat_rows(e_w_in[0].astype(BF16), e_w_q_up[0].astype(BF16), e_w_kv_up[0].astype(BF16),
                        e_w_out[0].astype(BF16))
    gath0 = _all_gather(shard0, name="weights0_all_gather")
    w0 = _layer0_in_weight(_gathered_cols(gath0, OFF_E_W_IN, R_E_W_IN, D_MODEL))
    wq = _q_up_weight(_gathered_cols(gath0, OFF_E_W_Q, R_E_W_Q, MLA_Q_RANK))
    wkv = _kv_up_weight(_gathered_cols(gath0, OFF_E_W_KV, R_E_W_KV, MLA_KV_RANK))
    wo0 = _gathered_rows(gath0, OFF_E_W_OUT, R_E_W_OUT, D_MODEL)

    g_bits = lax.bitcast_convert_type(o_g_in.reshape(LANES), BF16).reshape(2, LANES)
    g_bits = jnp.pad(g_bits, ((0, R_O_G_IN - 2), (0, 0)))
    shard1 = _flat_rows(o_w_in[0].astype(BF16), o_w_out[0].astype(BF16), g_bits)

    def unpack1(gath1):
        w1, wf = _layer1_in_weight(_gathered_cols(gath1, 0, R_O_W_IN, D_MODEL))
        wo1 = _gathered_rows(gath1, R_O_W_IN, R_O_W_OUT, D_MODEL)
        bits = gath1[:, R_O_W_IN + R_O_W_OUT:R_O_W_IN + R_O_W_OUT + 2, :].reshape(N_DEV, LANES, 2)
        return lax.bitcast_convert_type(bits, F32).reshape(1, D_MODEL), w1, wf, wo1

    def scatter1(g):
        d_o_g = jnp.pad(g["o_g_in"].reshape(N_DEV, 1, LANES), ((0, 0), (0, R_O_G_IN - 1), (0, 0)))
        pieces1 = jnp.concatenate([_scatter_cols(_layer1_in_grad(g["w1"], g["wf"])), _scatter_rows(g["wo1"]), d_o_g], axis=1)
        return pieces1.astype(BF16)

    gr = _local_step(x[0], positions[0], loss_target[0], e_g_in, w0, e_g_q_a, wq, e_g_kv_a, wkv, e_sinks, wo0,
                     (shard1, unpack1), o_b_f, g_final, scatter1=scatter1)

    pieces0 = jnp.concatenate([
        _scatter_cols(_layer0_in_grad(gr["w0"])), _scatter_cols(_q_up_grad(gr["wq"])),
        _scatter_cols(_kv_up_grad(gr["wkv"])), _scatter_rows(gr["wo0"])], axis=1)
    recv0 = _exchange(pieces0.astype(BF16), name="grads0_exchange")
    flats0 = _adamw(recv0, _flat_rows(e_w_in[0], e_w_q_up[0], e_w_kv_up[0], e_w_out[0]),
                    _flat_rows(m_e_w_in[0], m_e_w_q_up[0], m_e_w_kv_up[0], m_e_w_out[0]),
                    _flat_rows(v_e_w_in[0], v_e_w_q_up[0], v_e_w_kv_up[0], v_e_w_out[0]), name="adamw_layer0")
    flats1 = _adamw(gr["recv1"], _flat_rows(o_w_in[0], o_w_out[0], _g_rows(o_g_in)),
                    _flat_rows(m_o_w_in[0], m_o_w_out[0], _g_rows(m_o_g_in)),
                    _flat_rows(v_o_w_in[0], v_o_w_out[0], _g_rows(v_o_g_in)), name="adamw_layer1")
    g_sh, d_sh, m_sh, v_sh = [_unflat0(f0) + _unflat1(f1) for f0, f1 in zip(flats0, flats1)]

    small = _small_pack(gr["e_g_in"], gr["g_final"], gr["e_g_q_a"], gr["e_g_kv_a"], gr["e_sinks"], gr["o_b_f"], gr["loss"])
    small_all = _all_gather(small, name="small_all_gather")
    zero = jnp.zeros((), F32)
    w_small = _small_pack(e_g_in, g_final, e_g_q_a, e_g_kv_a, e_sinks, o_b_f, zero)
    m_small = _small_pack(m_e_g_in, m_g_final, m_e_g_q_a, m_e_g_kv_a, m_e_sinks, m_o_b_f, zero)
    v_small = _small_pack(v_e_g_in, v_g_final, v_e_g_q_a, v_e_g_kv_a, v_e_sinks, v_o_b_f, zero)
    smalls = _adamw(small_all, w_small, m_small, v_small, name="adamw_replicated")
    g_sm, d_sm, m_sm, v_sm = [_small_unpack(a) for a in smalls]
    loss = g_sm[6]

    def leaves(sh, sm):
        return (sm[0], sh[0], sm[2], sh[1], sm[3], sh[2], sm[4], sh[3], sh[6], sh[4], sm[5], sh[5], sm[1])

    return (loss, gr["grad_x"][None], *leaves(g_sh, g_sm), *leaves(d_sh, d_sm), *leaves(m_sh, m_sm), *leaves(v_sh, v_sm))
```

```python
import functools

import jax
import jax.numpy as jnp
from jax import lax
from jax.experimental import pallas as pl
from jax.experimental.pallas import tpu as pltpu

F32 = jnp.float32
BF16 = jnp.bfloat16
NEG_INF = float("-inf")

N_DEV = 8
LANES = 128
D_MODEL = 1024
EPS = 1e-6
ROPE_THETA = 10000.0
MLA_HEADS = 8
MLA_Q_RANK = 256
MLA_KV_RANK = 128
MLA_NOPE = 64
MLA_ROPE = 32
MLA_V = 64
SWA_HEADS = 8
SWA_KV_HEADS = 2
SWA_DIM = 64
WINDOW = 128
FOX_HEADS = 16
FOX_DIM = 64

ADAM_LR = 0.001
ADAM_B1 = 0.9
ADAM_B2 = 0.999
ADAM_EPS = 1e-08
ADAM_WD = 0.01
ADAM_STEP = 10

ATT_T = 512
VMEM_LIMIT = 56 * 1024 * 1024

Z0A_UNITS = 12
Z0B_UNITS = 6

WIDE = 1024
N_E_IN = 276
N_O_IN = 514
RA0 = 288
RB0 = 128 + 32 + 16
RA1 = 528
RB1 = 128 + 16
SMALL_ROWS = 24


def _tile(n, cands):
    for c in cands:
        if n % c == 0:
            return c
    raise ValueError(f"no tile for {n}")


def _params(sem, vmem=None):
    return pltpu.CompilerParams(dimension_semantics=sem, vmem_limit_bytes=vmem)


def _matmul(a, b, *, name, ta=False, tb=False, add=None, out_dtype=F32, b_rows=None):
    if ta:
        kdim, m = a.shape
    else:
        m, kdim = a.shape
    if tb:
        n, kb = b.shape
    else:
        kb, n = b.shape
    assert kdim == kb, (a.shape, b.shape)
    b_start = 0
    if b_rows is not None:
        assert tb
        b_start, n = b_rows
    tm = _tile(m, (512, 256, 128))
    tn = _tile(n, (768, 512, 384, 256, 128))
    assert b_start % tn == 0, (b_start, tn)
    b_off = b_start // tn
    dims = (((0 if ta else 1,), (1 if tb else 0,)), ((), ()))

    def body(*refs):
        if add is None:
            a_ref, b_ref, o_ref = refs
            add_ref = None
        else:
            a_ref, b_ref, add_ref, o_ref = refs
        r = lax.dot_general(a_ref[...].astype(BF16), b_ref[...].astype(BF16), dims, preferred_element_type=F32)
        if add_ref is not None:
            r = r + add_ref[...]
        o_ref[...] = r.astype(out_dtype)

    a_spec = pl.BlockSpec((kdim, tm), lambda i, j: (0, i)) if ta else pl.BlockSpec((tm, kdim), lambda i, j: (i, 0))
    b_spec = pl.BlockSpec((tn, kdim), lambda i, j: (j + b_off, 0)) if tb else pl.BlockSpec((kdim, tn), lambda i, j: (0, j))
    in_specs = [a_spec, b_spec]
    args = [a, b]
    if add is not None:
        in_specs.append(pl.BlockSpec((tm, tn), lambda i, j: (i, j)))
        args.append(add)
    return pl.pallas_call(
        body, name=name, grid=(m // tm, n // tn),
        in_specs=in_specs, out_specs=pl.BlockSpec((tm, tn), lambda i, j: (i, j)),
        out_shape=jax.ShapeDtypeStruct((m, n), out_dtype),
        compiler_params=_params(("parallel", "parallel"), VMEM_LIMIT),
    )(*args)


def _rmsnorm_fwd(x, g, *, width, col_blk, name):
    s = x.shape[0]
    tm = _tile(s, (256, 128))

    def body(x_ref, g_ref, y_ref):
        xf = x_ref[...].astype(F32)
        r = lax.rsqrt(jnp.mean(xf * xf, axis=-1, keepdims=True) + EPS)
        y_ref[...] = ((xf * r) * g_ref[...]).astype(BF16)

    return pl.pallas_call(
        body, name=name, grid=(s // tm,),
        in_specs=[pl.BlockSpec((tm, width), lambda i: (i, col_blk)), pl.BlockSpec((1, width), lambda i: (0, 0))],
        out_specs=pl.BlockSpec((tm, width), lambda i: (i, 0)),
        out_shape=jax.ShapeDtypeStruct((s, width), BF16),
        compiler_params=_params(("parallel",)),
    )(x, g)


def _rmsnorm_bwd(x, g, dy, *, width, col_blk, name, add=None, out_dtype=F32):
    s = x.shape[0]
    tm = _tile(s, (256, 128))

    def body(*refs):
        if add is None:
            x_ref, g_ref, dy_ref, dx_ref, dg_ref = refs
            add_ref = None
        else:
            x_ref, g_ref, dy_ref, add_ref, dx_ref, dg_ref = refs
        i = pl.program_id(0)
        xf = x_ref[...].astype(F32)
        r = lax.rsqrt(jnp.mean(xf * xf, axis=-1, keepdims=True) + EPS)
        xh = xf * r
        dyf = dy_ref[...].astype(F32)

        @pl.when(i == 0)
        def _():
            dg_ref[...] = jnp.zeros_like(dg_ref)

        dg_ref[...] += jnp.sum(dyf * xh, axis=0, keepdims=True)
        dxh = dyf * g_ref[...]
        dx = r * (dxh - xh * jnp.mean(dxh * xh, axis=-1, keepdims=True))
        if add_ref is not None:
            dx = dx + add_ref[...]
        dx_ref[...] = dx.astype(out_dtype)

    in_specs = [pl.BlockSpec((tm, width), lambda i: (i, col_blk)), pl.BlockSpec((1, width), lambda i: (0, 0)),
                pl.BlockSpec((tm, width), lambda i: (i, 0))]
    args = [x, g, dy]
    if add is not None:
        in_specs.append(pl.BlockSpec((tm, width), lambda i: (i, 0)))
        args.append(add)
    return pl.pallas_call(
        body, name=name, grid=(s // tm,),
        in_specs=in_specs,
        out_specs=[pl.BlockSpec((tm, width), lambda i: (i, 0)), pl.BlockSpec((1, width), lambda i: (0, 0))],
        out_shape=[jax.ShapeDtypeStruct((s, width), out_dtype), jax.ShapeDtypeStruct((1, width), F32)],
        compiler_params=_params(("arbitrary",)),
    )(*args)


def _sigmoid(x):
    return 1.0 / (1.0 + jnp.exp(-x))


def _gate_fwd(o_parts, gate, *, name):
    s = gate.shape[0]
    tm = _tile(s, (256, 128))
    n_o = len(o_parts)

    def body(*refs):
        o_refs, g_ref, y_ref = refs[:n_o], refs[n_o], refs[n_o + 1]
        o = o_refs[0][...] if n_o == 1 else jnp.concatenate([r[...] for r in o_refs], axis=1)
        gt = g_ref[...]
        y_ref[...] = (o * (gt * _sigmoid(gt))).astype(BF16)

    in_specs = [pl.BlockSpec((tm, o.shape[1]), lambda i: (i, 0)) for o in o_parts]
    in_specs.append(pl.BlockSpec((tm, D_MODEL), lambda i: (i, 0)))
    return pl.pallas_call(
        body, name=name, grid=(s // tm,), in_specs=in_specs,
        out_specs=pl.BlockSpec((tm, D_MODEL), lambda i: (i, 0)),
        out_shape=jax.ShapeDtypeStruct((s, D_MODEL), BF16),
        compiler_params=_params(("parallel",)),
    )(*o_parts, gate)


def _gate_bwd(d_og, o_parts, gate, *, name):
    s = gate.shape[0]
    tm = _tile(s, (256, 128))
    n_o = len(o_parts)
    widths = [o.shape[1] for o in o_parts]

    def body(*refs):
        d_ref, o_refs, g_ref = refs[0], refs[1:1 + n_o], refs[1 + n_o]
        do_refs, dg_ref = refs[2 + n_o:2 + 2 * n_o], refs[2 + 2 * n_o]
        d = d_ref[...]
        gt = g_ref[...]
        sg = _sigmoid(gt)
        silu = gt * sg
        dsilu = sg * (1.0 + gt * (1.0 - sg))
        o = o_refs[0][...] if n_o == 1 else jnp.concatenate([r[...] for r in o_refs], axis=1)
        dg_ref[...] = (d * o * dsilu).astype(BF16)
        do = d * silu
        off = 0
        for r, w in zip(do_refs, widths):
            r[...] = do[:, off:off + w]
            off += w

    in_specs = [pl.BlockSpec((tm, D_MODEL), lambda i: (i, 0))]
    in_specs += [pl.BlockSpec((tm, w), lambda i: (i, 0)) for w in widths]
    in_specs.append(pl.BlockSpec((tm, D_MODEL), lambda i: (i, 0)))
    out_specs = [pl.BlockSpec((tm, w), lambda i: (i, 0)) for w in widths]
    out_specs.append(pl.BlockSpec((tm, D_MODEL), lambda i: (i, 0)))
    out_shape = [jax.ShapeDtypeStruct((s, w), F32) for w in widths]
    out_shape.append(jax.ShapeDtypeStruct((s, D_MODEL), BF16))
    return pl.pallas_call(
        body, name=name, grid=(s // tm,), in_specs=in_specs, out_specs=out_specs, out_shape=out_shape,
        compiler_params=_params(("parallel",)),
    )(d_og, *o_parts, gate)


def _rot_half(x):
    lane = lax.broadcasted_iota(jnp.int32, x.shape, 1)
    return jnp.where(lane < 80, pltpu.roll(x, LANES - 16, axis=1), pltpu.roll(x, 16, axis=1))


def _rot_half_t(g):
    lane = lax.broadcasted_iota(jnp.int32, g.shape, 1)
    lo = (lane >= MLA_NOPE) & (lane < MLA_NOPE + MLA_ROPE // 2)
    hi = (lane >= MLA_NOPE + MLA_ROPE // 2) & (lane < MLA_NOPE + MLA_ROPE)
    return jnp.where(lo, pltpu.roll(g, LANES - 16, axis=1), jnp.where(hi, pltpu.roll(g, 16, axis=1), 0.0))


def _rope_fwd(qp, kvp, z0a, cos_t, sin_t, *, name):
    s = qp.shape[0]
    tm = _tile(s, (256, 128))
    hw = MLA_HEADS * LANES

    def body(q_ref, k_ref, kpe_ref, c_ref, s_ref, qm_ref, km_ref):
        c = c_ref[...]
        sn = s_ref[...]
        kpe = kpe_ref[...]
        kpe_r = (kpe * c + _rot_half(kpe) * sn).astype(BF16)
        lane = lax.broadcasted_iota(jnp.int32, kpe.shape, 1)
        for h in range(MLA_HEADS):
            sl = slice(h * LANES, (h + 1) * LANES)
            qh = q_ref[:, sl]
            qm_ref[:, sl] = (qh * c + _rot_half(qh) * sn).astype(BF16)
            km_ref[:, sl] = jnp.where(lane < MLA_NOPE, k_ref[:, sl], kpe_r)

    return pl.pallas_call(
        body, name=name, grid=(s // tm,),
        in_specs=[pl.BlockSpec((tm, hw), lambda i: (i, 0)), pl.BlockSpec((tm, hw), lambda i: (i, 0)),
                  pl.BlockSpec((tm, LANES), lambda i: (i, 11)),
                  pl.BlockSpec((tm, LANES), lambda i: (i, 0)), pl.BlockSpec((tm, LANES), lambda i: (i, 0))],
        out_specs=[pl.BlockSpec((tm, hw), lambda i: (i, 0)), pl.BlockSpec((tm, hw), lambda i: (i, 0))],
        out_shape=[jax.ShapeDtypeStruct((s, hw), BF16), jax.ShapeDtypeStruct((s, hw), BF16)],
        compiler_params=_params(("parallel",)),
    )(qp, kvp, z0a, cos_t, sin_t)


def _rope_bwd(dqm, dkm, dvm, cos_t, sin_t, *, name):
    s = dqm.shape[0]
    tm = _tile(s, (256, 128))
    hw = MLA_HEADS * LANES
    vw = MLA_HEADS * MLA_V

    def body(dq_ref, dk_ref, dv_ref, c_ref, s_ref, dqp_ref, dkv_ref, dkpe_ref):
        c = c_ref[...]
        sn = s_ref[...]
        ksum = jnp.zeros((tm, LANES), F32)
        for h in range(MLA_HEADS):
            sl = slice(h * LANES, (h + 1) * LANES)
            dq = dq_ref[:, sl]
            dqp_ref[:, sl] = (dq * c + _rot_half_t(dq * sn)).astype(BF16)
            dk = dk_ref[:, sl]
            dkv_ref[:, sl] = dk.astype(BF16)
            ksum = ksum + dk
        dkv_ref[:, hw:] = dv_ref[...]
        lane = lax.broadcasted_iota(jnp.int32, ksum.shape, 1)
        dkpe = ksum * c + _rot_half_t(ksum * sn)
        dkpe_ref[...] = jnp.where((lane >= MLA_NOPE) & (lane < MLA_NOPE + MLA_ROPE), dkpe, 0.0).astype(BF16)

    return pl.pallas_call(
        body, name=name, grid=(s // tm,),
        in_specs=[pl.BlockSpec((tm, hw), lambda i: (i, 0)), pl.BlockSpec((tm, hw), lambda i: (i, 0)),
                  pl.BlockSpec((tm, vw), lambda i: (i, 0)),
                  pl.BlockSpec((tm, LANES), lambda i: (i, 0)), pl.BlockSpec((tm, LANES), lambda i: (i, 0))],
        out_specs=[pl.BlockSpec((tm, hw), lambda i: (i, 0)), pl.BlockSpec((tm, hw + vw), lambda i: (i, 0)),
                   pl.BlockSpec((tm, LANES), lambda i: (i, 0))],
        out_shape=[jax.ShapeDtypeStruct((s, hw), BF16), jax.ShapeDtypeStruct((s, hw + vw), BF16),
                   jax.ShapeDtypeStruct((s, LANES), BF16)],
        compiler_params=_params(("parallel",)),
    )(dqm, dkm, dvm, cos_t, sin_t)


def _head_mask(shape, a):
    lane = lax.broadcasted_iota(jnp.int32, shape, 1)
    return (lane >= 64 * a) & (lane < 64 * (a + 1))


def _causal_mask(t):
    row = lax.broadcasted_iota(jnp.int32, (t, t), 0)
    col = lax.broadcasted_iota(jnp.int32, (t, t), 1)
    return col <= row


_NT = (((1,), (1,)), ((), ()))
LOG2E = 1.4426950408889634


def _stack_heads(tile, hw):
    lane = lax.broadcasted_iota(jnp.int32, tile.shape, 1)
    z = jnp.zeros_like(tile)
    return jnp.concatenate([jnp.where(lane < hw, tile, z), jnp.where(lane >= hw, tile, z)], axis=0)


def _stacked_rows(r0, r1, t):
    n = r0.shape[-1]
    return jnp.concatenate([jnp.broadcast_to(r0, (t, n)), jnp.broadcast_to(r1, (t, n))], axis=0)


def _stacked_causal_mask(t):
    m = _causal_mask(t)
    return jnp.concatenate([m, m], axis=0)


def _resident(block, index_map):
    return pl.BlockSpec(block, index_map, pipeline_mode=pl.Buffered(1))


def _flash_fwd(q, k, v, bias, *, n_pairs, hw, q_off, k_off, v_off, scale, name, rider=None):
    s = q.shape[0]
    t = min(ATT_T, s)
    nb = s // t
    qw = 2 * hw
    has_bias = bias is not None
    c1 = scale * LOG2E

    def body(*refs):
        refs, ride_refs = _split_rider(refs, rider, n_in=4 if has_bias else 3, n_out=2)
        if has_bias:
            q_ref, k_ref, v_ref, b_ref, o_ref, lse_ref = refs
        else:
            q_ref, k_ref, v_ref, o_ref, lse_ref = refs
            b_ref = None
        _ride_start(rider, ride_refs, pl.program_id(0) == 0)
        cmask = _stacked_causal_mask(t)
        lane_lt64 = lax.broadcasted_iota(jnp.int32, (t, LANES), 1) < 64

        def q_block(i, _):
            r0 = pl.multiple_of(i * t, t)
            qs = _stack_heads(q_ref[pl.ds(r0, t), :], hw)

            def kv_step(j, carry, masked):
                m, l, acc = carry
                c0 = pl.multiple_of(j * t, t)
                sc = lax.dot_general(qs, k_ref[pl.ds(c0, t), :], _NT, preferred_element_type=F32) * c1
                if has_bias:
                    sc = sc + _stacked_rows(b_ref[0, 0, j], b_ref[0, 1, j], t)
                if masked:
                    sc = jnp.where(cmask, sc, NEG_INF)
                m_new = jnp.maximum(m, jnp.max(sc, axis=-1, keepdims=True))
                alpha = jnp.exp2(m - m_new)
                p = jnp.exp2(sc - m_new)
                l_new = alpha * l + jnp.sum(p, axis=-1, keepdims=True)
                pv = jnp.dot(p.astype(BF16), v_ref[pl.ds(c0, t), :], preferred_element_type=F32)
                return m_new, l_new, alpha * acc + pv

            init = (jnp.full((2 * t, 1), NEG_INF, F32), jnp.zeros((2 * t, 1), F32), jnp.zeros((2 * t, LANES), F32))
            carry = lax.fori_loop(0, i, functools.partial(kv_step, masked=False), init)
            m, l, acc = kv_step(i, carry, True)
            out = acc / l
            lse2 = m + jnp.log2(l)
            lse_ref[0, 0, pl.ds(r0, t), :] = lse2[:t]
            lse_ref[0, 1, pl.ds(r0, t), :] = lse2[t:]
            o_ref[pl.ds(r0, t), :] = jnp.where(lane_lt64, out[:t], out[t:])
            return 0

        lax.fori_loop(0, nb, q_block, 0)
        _ride_wait(rider, ride_refs, pl.program_id(0) == n_pairs - 1)

    in_specs = [_resident((s, qw), lambda p: (0, q_off + p)), _resident((s, qw), lambda p: (0, k_off + p)),
                _resident((s, LANES), lambda p: (0, v_off + p))]
    args = [q, k, v]
    if has_bias:
        in_specs.append(_resident((1, 2, nb, 1, t), lambda p: (p, 0, 0, 0, 0)))
        args.append(bias)
    out_specs = [pl.BlockSpec((s, LANES), lambda p: (0, p)), pl.BlockSpec((1, 2, s, 1), lambda p: (p, 0, 0, 0))]
    out_shape = [jax.ShapeDtypeStruct((s, n_pairs * LANES), F32), jax.ShapeDtypeStruct((n_pairs, 2, s, 1), F32)]
    scratch = _add_rider(rider, in_specs, args, out_specs, out_shape)
    return pl.pallas_call(
        body, name=name, grid=(n_pairs,), in_specs=in_specs, out_specs=out_specs, out_shape=out_shape,
        scratch_shapes=scratch,
        compiler_params=_params(("parallel",) if rider is None else ("arbitrary",), VMEM_LIMIT),
    )(*args)


def _flash_bwd(q, k, v, do, o, lse, bias, *, n_pairs, hw, q_off, k_off, v_off, scale, qk_dtype, name, rider=None):
    s = q.shape[0]
    t = min(ATT_T, s)
    nb = s // t
    qw = 2 * hw
    has_bias = bias is not None
    c1 = scale * LOG2E

    def body(*refs):
        refs, ride_refs = _split_rider(refs, rider, n_in=7 if has_bias else 6, n_out=5 if has_bias else 3)
        if has_bias:
            (q_ref, k_ref, v_ref, do_ref, o_ref, lse_ref, b_ref, dq_ref, dk_ref, dv_ref, db_ref, dr_ref,
             dkt_ref, dvt_ref) = refs
            db_ref[...] = jnp.zeros_like(db_ref)
        else:
            q_ref, k_ref, v_ref, do_ref, o_ref, lse_ref, dq_ref, dk_ref, dv_ref, dkt_ref, dvt_ref = refs
            b_ref = db_ref = dr_ref = None
        _ride_start(rider, ride_refs, pl.program_id(0) == 0)
        dkt_ref[...] = jnp.zeros_like(dkt_ref)
        dvt_ref[...] = jnp.zeros_like(dvt_ref)
        cmask = _stacked_causal_mask(t)
        lane_lt_hw = lax.broadcasted_iota(jnp.int32, (t, qw), 1) < hw

        def q_block(i, _):
            r0 = pl.multiple_of(i * t, t)
            qs = _stack_heads(q_ref[pl.ds(r0, t), :], hw)
            dos = _stack_heads(do_ref[pl.ds(r0, t), :], 64)
            ot = o_ref[pl.ds(r0, t), :]
            delta = jnp.sum(dos * jnp.concatenate([ot, ot], axis=0), axis=-1, keepdims=True)
            lse2 = jnp.concatenate([lse_ref[0, 0, pl.ds(r0, t), :], lse_ref[0, 1, pl.ds(r0, t), :]], axis=0)
            dosb = dos.astype(BF16)
            dos_t = dos.T.astype(BF16)
            qs_t = qs.astype(F32).T.astype(BF16)

            def kv_step(j, carry, masked):
                dq, rsum = carry
                c0 = pl.multiple_of(j * t, t)
                kt = k_ref[pl.ds(c0, t), :]
                vt = v_ref[pl.ds(c0, t), :]
                sc = lax.dot_general(qs, kt, _NT, preferred_element_type=F32) * c1
                if has_bias:
                    sc = sc + _stacked_rows(b_ref[0, 0, j], b_ref[0, 1, j], t)
                if masked:
                    sc = jnp.where(cmask, sc, NEG_INF)
                p = jnp.exp2(sc - lse2)
                dp = lax.dot_general(dosb, vt, _NT, preferred_element_type=F32)
                ds = p * (dp - delta)
                dsb = ds.astype(BF16)
                dvt_ref[j] += jnp.dot(dos_t, p.astype(BF16), preferred_element_type=F32)
                dkt_ref[j] += jnp.dot(qs_t, dsb, preferred_element_type=F32)
                if has_bias:
                    db_ref[0, 0, j] += jnp.sum(ds[:t], axis=0, keepdims=True)
                    db_ref[0, 1, j] += jnp.sum(ds[t:], axis=0, keepdims=True)
                    rsum = rsum + jnp.sum(ds, axis=-1, keepdims=True)
                return dq + jnp.dot(dsb, kt, preferred_element_type=F32), rsum

            init = (jnp.zeros((2 * t, qw), F32), jnp.zeros((2 * t, 1), F32))
            carry = lax.fori_loop(0, i, functools.partial(kv_step, masked=False), init)
            dq, rsum = kv_step(i, carry, True)
            dq = dq * scale
            dq_ref[pl.ds(r0, t), :] = jnp.where(lane_lt_hw, dq[:t], dq[t:]).astype(qk_dtype)
            if has_bias:
                dr_ref[0, 0, pl.ds(r0, t), :] = rsum[:t]
                dr_ref[0, 1, pl.ds(r0, t), :] = rsum[t:]
            return 0

        lax.fori_loop(0, nb, q_block, 0)

        def k_block(j, _):
            c0 = pl.multiple_of(j * t, t)
            dk_ref[pl.ds(c0, t), :] = (dkt_ref[j].T * scale).astype(qk_dtype)
            dv_ref[pl.ds(c0, t), :] = dvt_ref[j].T.astype(BF16)
            return 0

        lax.fori_loop(0, nb, k_block, 0)
        _ride_wait(rider, ride_refs, pl.program_id(0) == n_pairs - 1)

    in_specs = [_resident((s, qw), lambda p: (0, q_off + p)), _resident((s, qw), lambda p: (0, k_off + p)),
                _resident((s, LANES), lambda p: (0, v_off + p)),
                _resident((s, LANES), lambda p: (0, p)), _resident((s, LANES), lambda p: (0, p)),
                _resident((1, 2, s, 1), lambda p: (p, 0, 0, 0))]
    args = [q, k, v, do, o, lse]
    out_specs = [pl.BlockSpec((s, qw), lambda p: (0, p)), pl.BlockSpec((s, qw), lambda p: (0, p)),
                 pl.BlockSpec((s, LANES), lambda p: (0, p))]
    out_shape = [jax.ShapeDtypeStruct((s, n_pairs * qw), qk_dtype), jax.ShapeDtypeStruct((s, n_pairs * qw), qk_dtype),
                 jax.ShapeDtypeStruct((s, n_pairs * LANES), BF16)]
    if has_bias:
        in_specs.append(_resident((1, 2, nb, 1, t), lambda p: (p, 0, 0, 0, 0)))
        args.append(bias)
        out_specs.append(pl.BlockSpec((1, 2, nb, 1, t), lambda p: (p, 0, 0, 0, 0)))
        out_shape.append(jax.ShapeDtypeStruct((n_pairs, 2, nb, 1, t), F32))
        out_specs.append(pl.BlockSpec((1, 2, s, 1), lambda p: (p, 0, 0, 0)))
        out_shape.append(jax.ShapeDtypeStruct((n_pairs, 2, s, 1), F32))
    scratch = [pltpu.VMEM((nb, qw, t), F32), pltpu.VMEM((nb, LANES, t), F32)]
    scratch += _add_rider(rider, in_specs, args, out_specs, out_shape)
    return pl.pallas_call(
        body, name=name, grid=(n_pairs,), in_specs=in_specs, out_specs=out_specs, out_shape=out_shape,
        scratch_shapes=scratch,
        compiler_params=_params(("parallel",) if rider is None else ("arbitrary",), VMEM_LIMIT),
    )(*args)


def _alibi_slope(h):
    return 2.0 ** (-8.0 * (h + 1.0) / SWA_HEADS)


SWA_ROWS = 512
SWA_SCALE = SWA_DIM ** -0.5


def _swa_geometry(i):
    w = WINDOW
    r0 = pl.multiple_of(i * w, w)
    b0 = pl.multiple_of(jnp.maximum(i - 1, 0) * w, w)
    row = lax.broadcasted_iota(jnp.int32, (w, 2 * w), 0)
    col = lax.broadcasted_iota(jnp.int32, (w, 2 * w), 1)
    dist = row - col + (r0 - b0)
    valid = (dist >= 0) & (dist < w)
    return r0, b0, dist.astype(F32), valid


def _swa_q_head(qblk, h):
    kv = h // (SWA_HEADS // SWA_KV_HEADS)
    if h % 2 != kv:
        qblk = pltpu.roll(qblk, 64, axis=1)
    return jnp.where(_head_mask(qblk.shape, kv), qblk, 0.0)


def _swa_fwd(z0b, sinks, *, name):
    s = z0b.shape[0]
    w = WINDOW
    rows = min(SWA_ROWS, s)
    per_step = rows // w
    qcols = SWA_HEADS * SWA_DIM

    def body(sink_ref, q_ref, k_ref, v_ref, o_ref, lse_ref):
        g = pl.program_id(0)
        for ii in range(per_step):
            r0, b0, dist, valid = _swa_geometry(g * per_step + ii)
            kb = k_ref[pl.ds(b0, 2 * w), :]
            vb = v_ref[pl.ds(b0, 2 * w), :]
            o_heads = []
            for h in range(SWA_HEADS):
                kv = h // (SWA_HEADS // SWA_KV_HEADS)
                blk = h // 2
                qh = _swa_q_head(q_ref[ii * w:(ii + 1) * w, blk * LANES:(blk + 1) * LANES].astype(F32), h).astype(BF16)
                sc = lax.dot_general(qh, kb, _NT, preferred_element_type=F32) * SWA_SCALE - _alibi_slope(h) * dist
                sc = jnp.where(valid, sc, NEG_INF)
                sink = sink_ref[0, h]
                m = jnp.maximum(jnp.max(sc, axis=-1, keepdims=True), sink)
                p = jnp.exp(sc - m)
                l = jnp.sum(p, axis=-1, keepdims=True) + jnp.exp(sink - m)
                oh = jnp.dot(p.astype(BF16), vb, preferred_element_type=F32) / l
                if h % 2 != kv:
                    oh = pltpu.roll(oh, 64, axis=1)
                o_heads.append(oh)
                lse_ref[h, ii * w:(ii + 1) * w, :] = m + jnp.log(l)
            lt64 = lax.broadcasted_iota(jnp.int32, (w, LANES), 1) < 64
            o_ref[ii * w:(ii + 1) * w, :] = jnp.concatenate(
                [jnp.where(lt64, o_heads[2 * b], o_heads[2 * b + 1]) for b in range(SWA_HEADS // 2)], axis=1)

    return pl.pallas_call(
        body, name=name, grid=(s // rows,),
        in_specs=[pl.BlockSpec(memory_space=pltpu.SMEM),
                  pl.BlockSpec((rows, qcols), lambda g: (g, 0)),
                  pl.BlockSpec((s, LANES), lambda g: (0, 4)), pl.BlockSpec((s, LANES), lambda g: (0, 5))],
        out_specs=[pl.BlockSpec((rows, qcols), lambda g: (g, 0)), pl.BlockSpec((SWA_HEADS, rows, 1), lambda g: (0, g, 0))],
        out_shape=[jax.ShapeDtypeStruct((s, qcols), F32), jax.ShapeDtypeStruct((SWA_HEADS, s, 1), F32)],
        compiler_params=_params(("parallel",), VMEM_LIMIT),
    )(sinks, z0b, z0b, z0b)


def _swa_bwd(z0b, sinks, do, o, lse, *, name):
    s = z0b.shape[0]
    w = WINDOW
    rows = min(SWA_ROWS, s)
    per_step = rows // w
    qcols = SWA_HEADS * SWA_DIM
    nblk = s // w

    def body(sink_ref, q_ref, k_ref, v_ref, do_ref, o_ref, lse_ref, dq_ref, dkt_ref, dvt_ref, dsink_ref):
        g = pl.program_id(0)

        @pl.when(g == 0)
        def _():
            dkt_ref[...] = jnp.zeros_like(dkt_ref)
            dvt_ref[...] = jnp.zeros_like(dvt_ref)
            dsink_ref[...] = jnp.zeros_like(dsink_ref)

        for ii in range(per_step):
            i = g * per_step + ii
            r0, b0, dist, valid = _swa_geometry(i)
            j0 = jnp.maximum(i - 1, 0)
            kb = k_ref[pl.ds(b0, 2 * w), :]
            vb = v_ref[pl.ds(b0, 2 * w), :]
            dq_heads = []
            for h in range(SWA_HEADS):
                kv = h // (SWA_HEADS // SWA_KV_HEADS)
                blk = h // 2
                cs = slice(blk * LANES, (blk + 1) * LANES)
                rs = slice(ii * w, (ii + 1) * w)
                qh32 = _swa_q_head(q_ref[rs, cs].astype(F32), h)
                qh = qh32.astype(BF16)
                doh32 = _swa_q_head(do_ref[rs, cs], h)
                oh32 = _swa_q_head(o_ref[rs, cs], h)
                delta = jnp.sum(doh32 * oh32, axis=-1, keepdims=True)
                lse = lse_ref[h, rs, :]
                sink = sink_ref[0, h]
                sc = lax.dot_general(qh, kb, _NT, preferred_element_type=F32) * SWA_SCALE - _alibi_slope(h) * dist
                sc = jnp.where(valid, sc, NEG_INF)
                p = jnp.exp(sc - lse)
                dp = lax.dot_general(doh32.astype(BF16), vb, _NT, preferred_element_type=F32)
                ds = p * (dp - delta)
                dsb = ds.astype(BF16)
                pb = p.astype(BF16)
                dsink_ref[h:h + 1, :] += jnp.broadcast_to(-jnp.sum(jnp.exp(sink - lse) * delta), (1, LANES))
                do_t = doh32.T.astype(BF16)
                q_t = qh32.T.astype(BF16)
                dvt = jnp.dot(do_t, pb, preferred_element_type=F32)
                dkt = jnp.dot(q_t, dsb, preferred_element_type=F32) * SWA_SCALE
                dvt_ref[j0] += dvt[:, :w]
                dvt_ref[j0 + 1] += dvt[:, w:]
                dkt_ref[j0] += dkt[:, :w]
                dkt_ref[j0 + 1] += dkt[:, w:]
                dq = jnp.dot(dsb, kb, preferred_element_type=F32) * SWA_SCALE
                if h % 2 != kv:
                    dq = pltpu.roll(dq, 64, axis=1)
                dq_heads.append(dq)
            lt64 = lax.broadcasted_iota(jnp.int32, (w, LANES), 1) < 64
            dq_ref[ii * w:(ii + 1) * w, :] = jnp.concatenate(
                [jnp.where(lt64, dq_heads[2 * b], dq_heads[2 * b + 1]) for b in range(SWA_HEADS // 2)], axis=1)

    return pl.pallas_call(
        body, name=name, grid=(s // rows,),
        in_specs=[pl.BlockSpec(memory_space=pltpu.SMEM),
                  pl.BlockSpec((rows, qcols), lambda g: (g, 0)),
                  pl.BlockSpec((s, LANES), lambda g: (0, 4)), pl.BlockSpec((s, LANES), lambda g: (0, 5)),
                  pl.BlockSpec((rows, qcols), lambda g: (g, 0)), pl.BlockSpec((rows, qcols), lambda g: (g, 0)),
                  pl.BlockSpec((SWA_HEADS, rows, 1), lambda g: (0, g, 0))],
        out_specs=[pl.BlockSpec((rows, qcols), lambda g: (g, 0)),
                   pl.BlockSpec((nblk, LANES, w), lambda g: (0, 0, 0)),
                   pl.BlockSpec((nblk, LANES, w), lambda g: (0, 0, 0)),
                   pl.BlockSpec((SWA_HEADS, LANES), lambda g: (0, 0))],
        out_shape=[jax.ShapeDtypeStruct((s, qcols), F32),
                   jax.ShapeDtypeStruct((nblk, LANES, w), F32), jax.ShapeDtypeStruct((nblk, LANES, w), F32),
                   jax.ShapeDtypeStruct((SWA_HEADS, LANES), F32)],
        compiler_params=_params(("arbitrary",), VMEM_LIMIT),
    )(sinks, z0b, z0b, z0b, do, o, lse)


CUM_T = 256


def _split3(x):
    hi = x.astype(BF16)
    r1 = x - hi.astype(F32)
    mid = r1.astype(BF16)
    lo = (r1 - mid.astype(F32)).astype(BF16)
    return hi, mid, lo


def _tri_dot(tri, x):
    hi, mid, lo = _split3(x)
    out = jnp.dot(tri, hi, preferred_element_type=F32)
    out = out + jnp.dot(tri, mid, preferred_element_type=F32)
    return out + jnp.dot(tri, lo, preferred_element_type=F32)


def _logf_fwd(zf, bf, *, name):
    s = zf.shape[0]
    t = CUM_T
    nb = s // t

    def body(z_ref, b_ref, c_ref, carry_ref):
        i = pl.program_id(0)

        @pl.when(i == 0)
        def _():
            carry_ref[...] = jnp.zeros_like(carry_ref)

        x = z_ref[...] + b_ref[...]
        lf = jnp.minimum(x, 0.0) - jnp.log(1.0 + jnp.exp(-jnp.abs(x)))
        row = lax.broadcasted_iota(jnp.int32, (t, t), 0)
        col = lax.broadcasted_iota(jnp.int32, (t, t), 1)
        tri = jnp.where(col <= row, 1.0, 0.0).astype(BF16)
        c = _tri_dot(tri, lf) + carry_ref[...]
        c_ref[...] = c
        carry_ref[...] = c[t - 1:t, :]

    return pl.pallas_call(
        body, name=name, grid=(nb,),
        in_specs=[pl.BlockSpec((t, LANES), lambda i: (i, 0)), pl.BlockSpec((1, LANES), lambda i: (0, 0))],
        out_specs=pl.BlockSpec((t, LANES), lambda i: (i, 0)),
        out_shape=jax.ShapeDtypeStruct((s, LANES), F32),
        scratch_shapes=[pltpu.VMEM((1, LANES), F32)],
        compiler_params=_params(("arbitrary",)),
    )(zf, bf)


def _logf_bwd(dc, zf, bf, *, name):
    s = zf.shape[0]
    t = CUM_T
    nb = s // t

    def body(dc_ref, z_ref, b_ref, dz_ref, db_ref, carry_ref):
        i = pl.program_id(0)

        @pl.when(i == 0)
        def _():
            carry_ref[...] = jnp.zeros_like(carry_ref)
            db_ref[...] = jnp.zeros_like(db_ref)

        row = lax.broadcasted_iota(jnp.int32, (t, t), 0)
        col = lax.broadcasted_iota(jnp.int32, (t, t), 1)
        tri = jnp.where(col >= row, 1.0, 0.0).astype(BF16)
        dlf = _tri_dot(tri, dc_ref[...]) + carry_ref[...]
        carry_ref[...] = dlf[0:1, :]
        x = z_ref[...] + b_ref[...]
        dz = dlf * _sigmoid(-x)
        dz_ref[...] = dz.astype(BF16)
        db_ref[...] += jnp.sum(dz, axis=0, keepdims=True)

    return pl.pallas_call(
        body, name=name, grid=(nb,),
        in_specs=[pl.BlockSpec((t, LANES), lambda i: (nb - 1 - i, 0)), pl.BlockSpec((t, LANES), lambda i: (nb - 1 - i, 0)),
                  pl.BlockSpec((1, LANES), lambda i: (0, 0))],
        out_specs=[pl.BlockSpec((t, LANES), lambda i: (nb - 1 - i, 0)), pl.BlockSpec((1, LANES), lambda i: (0, 0))],
        out_shape=[jax.ShapeDtypeStruct((s, LANES), BF16), jax.ShapeDtypeStruct((1, LANES), F32)],
        scratch_shapes=[pltpu.VMEM((1, LANES), F32)],
        compiler_params=_params(("arbitrary",)),
    )(dc, zf, bf)


def _loss_head(x2, g, target, *, name):
    s = x2.shape[0]
    tm = _tile(s, (256, 128))

    def body(x_ref, g_ref, t_ref, dx_ref, loss_ref, dg_ref):
        i = pl.program_id(0)

        @pl.when(i == 0)
        def _():
            loss_ref[...] = jnp.zeros_like(loss_ref)
            dg_ref[...] = jnp.zeros_like(dg_ref)

        xf = x_ref[...]
        r = lax.rsqrt(jnp.mean(xf * xf, axis=-1, keepdims=True) + EPS)
        xh = xf * r
        gv = g_ref[...]
        err = xh * gv - t_ref[...]
        loss_ref[...] += jnp.broadcast_to(0.5 * jnp.sum(jnp.mean(err * err, axis=-1, keepdims=True)), loss_ref.shape)
        dy = err * (1.0 / D_MODEL)
        dg_ref[...] += jnp.sum(dy * xh, axis=0, keepdims=True)
        dxh = dy * gv
        dx_ref[...] = r * (dxh - xh * jnp.mean(dxh * xh, axis=-1, keepdims=True))

    return pl.pallas_call(
        body, name=name, grid=(s // tm,),
        in_specs=[pl.BlockSpec((tm, D_MODEL), lambda i: (i, 0)), pl.BlockSpec((1, D_MODEL), lambda i: (0, 0)),
                  pl.BlockSpec((tm, D_MODEL), lambda i: (i, 0))],
        out_specs=[pl.BlockSpec((tm, D_MODEL), lambda i: (i, 0)), pl.BlockSpec((8, LANES), lambda i: (0, 0)),
                   pl.BlockSpec((1, D_MODEL), lambda i: (0, 0))],
        out_shape=[jax.ShapeDtypeStruct((s, D_MODEL), F32), jax.ShapeDtypeStruct((8, LANES), F32),
                   jax.ShapeDtypeStruct((1, D_MODEL), F32)],
        compiler_params=_params(("arbitrary",)),
    )(x2, g, target)


def _sum_pieces(p_ref):
    g = p_ref[0].astype(F32)
    for k in range(1, N_DEV):
        g = g + p_ref[k].astype(F32)
    return g


def _adam_update(g, w, m, v):
    bc1 = 1.0 - ADAM_B1 ** ADAM_STEP
    bc2 = 1.0 - ADAM_B2 ** ADAM_STEP
    nm = ADAM_B1 * m + (1.0 - ADAM_B1) * g
    nv = ADAM_B2 * v + (1.0 - ADAM_B2) * (g * g)
    m_hat = nm / bc1
    v_hat = nv / bc2
    return -ADAM_LR * (m_hat / (jnp.sqrt(v_hat) + ADAM_EPS) + ADAM_WD * w), nm, nv


def _adamw(pieces, w, m, v, *, name):
    rows, cols = w.shape
    tr = _tile(rows, (176, 144, SMALL_ROWS))

    def body(p_ref, w_ref, m_ref, v_ref, g_ref, d_ref, nm_ref, nv_ref):
        g = _sum_pieces(p_ref)
        g_ref[...] = g
        d_ref[...], nm_ref[...], nv_ref[...] = _adam_update(g, w_ref[...], m_ref[...], v_ref[...])

    spec = pl.BlockSpec((tr, cols), lambda i: (i, 0))
    shape = jax.ShapeDtypeStruct((rows, cols), F32)
    return pl.pallas_call(
        body, name=name, grid=(rows // tr,),
        in_specs=[pl.BlockSpec((N_DEV, tr, cols), lambda i: (0, i, 0)), spec, spec, spec],
        out_specs=[spec, spec, spec, spec], out_shape=[shape, shape, shape, shape],
        compiler_params=_params(("parallel",)),
    )(pieces, w, m, v)


def _sum8(pieces, rows, *, name):
    cols = pieces.shape[2]
    tr = _tile(rows, (176, 96))

    def body(p_ref, g_ref):
        g_ref[...] = _sum_pieces(p_ref)

    return pl.pallas_call(
        body, name=name, grid=(rows // tr,),
        in_specs=[pl.BlockSpec((N_DEV, tr, cols), lambda i: (0, i, 0))],
        out_specs=pl.BlockSpec((tr, cols), lambda i: (i, 0)),
        out_shape=jax.ShapeDtypeStruct((rows, cols), F32),
        compiler_params=_params(("parallel",)),
    )(pieces)


def _adamw_native(g, w, m, v, *, name):
    rows, cols = w.shape
    tr = _tile(rows, (256, 128))

    def body(g_ref, w_ref, m_ref, v_ref, d_ref, nm_ref, nv_ref):
        d_ref[...], nm_ref[...], nv_ref[...] = _adam_update(g_ref[...], w_ref[...], m_ref[...], v_ref[...])

    spec = pl.BlockSpec((tr, cols), lambda i: (i, 0))
    shape = jax.ShapeDtypeStruct((rows, cols), F32)
    return pl.pallas_call(
        body, name=name, grid=(rows // tr,), in_specs=[spec, spec, spec, spec],
        out_specs=[spec, spec, spec], out_shape=[shape, shape, shape],
        compiler_params=_params(("parallel",)),
    )(g, w, m, v)


MESH = pl.DeviceIdType.MESH
ANY = pl.BlockSpec(memory_space=pl.ANY)


def _all_gather(shard, *, name):
    rows, lanes = shard.shape

    def body(x_ref, out_ref, send_sems, recv_sems, local_sem):
        x, y, c = lax.axis_index("x"), lax.axis_index("y"), lax.axis_index("c")
        me, sibling = (x, y, c), (x, y, 1 - c)
        chips = [(1 - x, y), (x, 1 - y), (1 - x, 1 - y)]

        def block(px, py, pc):
            return out_ref.at[4 * px + 2 * py + pc]

        def copy(k, blk, to, src=None):
            return pltpu.make_async_remote_copy(
                src_ref=block(*blk) if src is None else src, dst_ref=block(*blk),
                send_sem=send_sems.at[k], recv_sem=recv_sems.at[k], device_id=to, device_id_type=MESH)

        mine = pltpu.make_async_copy(x_ref, block(*me), local_sem)
        mine.start()
        first = [copy(0, me, sibling, src=x_ref)]
        first += [copy(1 + j, me, (*chip, c), src=x_ref) for j, chip in enumerate(chips)]
        for cp in first:
            cp.start()
        passed = [copy(4 + j, (*chip, c), sibling) for j, chip in enumerate(chips)]
        for j, chip in enumerate(chips):
            copy(1 + j, (*chip, c), me).wait_recv()
            passed[j].start()
        copy(0, sibling, me).wait_recv()
        for j, chip in enumerate(chips):
            copy(4 + j, (*chip, 1 - c), me).wait_recv()
        for cp in first + passed:
            cp.wait_send()
        mine.wait()

    return pl.pallas_call(
        body, name=name, out_shape=jax.ShapeDtypeStruct((N_DEV, rows, lanes), shard.dtype),
        in_specs=[ANY], out_specs=ANY,
        scratch_shapes=[pltpu.SemaphoreType.DMA((7,)), pltpu.SemaphoreType.DMA((7,)), pltpu.SemaphoreType.DMA(())],
    )(shard)


def _peer_copies(kind, src_ref, out_ref, send_sems, recv_sems, local_sem):
    x, y, c = lax.axis_index("x"), lax.axis_index("y"), lax.axis_index("c")
    me = 4 * x + 2 * y + c

    def src(idx):
        return src_ref.at[idx] if kind == "exchange" else src_ref

    mine = pltpu.make_async_copy(src(me), out_ref.at[me], local_sem)
    copies = []
    for r in range(1, N_DEV):
        px = 1 - x if r & 4 else x
        py = 1 - y if r & 2 else y
        pc = 1 - c if r & 1 else c
        copies.append(pltpu.make_async_remote_copy(
            src_ref=src(4 * px + 2 * py + pc), dst_ref=out_ref.at[me],
            send_sem=send_sems.at[r - 1], recv_sem=recv_sems.at[r - 1],
            device_id=(px, py, pc), device_id_type=MESH))
    return mine, copies


PEER_SEMS = [pltpu.SemaphoreType.DMA((7,)), pltpu.SemaphoreType.DMA((7,)), pltpu.SemaphoreType.DMA(())]


def _exchange(pieces, *, name):
    def body(g_ref, out_ref, send_sems, recv_sems, local_sem):
        mine, copies = _peer_copies("exchange", g_ref, out_ref, send_sems, recv_sems, local_sem)
        mine.start()
        for cp in copies:
            cp.start()
        for cp in copies:
            cp.wait()
        mine.wait()

    return pl.pallas_call(
        body, name=name, out_shape=jax.ShapeDtypeStruct(pieces.shape, pieces.dtype),
        in_specs=[ANY], out_specs=ANY, scratch_shapes=list(PEER_SEMS),
    )(pieces)


def _add_rider(rider, in_specs, args, out_specs, out_shape):
    if rider is None:
        return []
    _, arr = rider
    in_specs.append(ANY)
    args.append(arr)
    out_specs.append(ANY)
    out_shape.append(jax.ShapeDtypeStruct((N_DEV,) + arr.shape[-2:], arr.dtype))
    return list(PEER_SEMS)


def _split_rider(refs, rider, n_in, n_out):
    if rider is None:
        return refs, None
    refs = list(refs)
    rin = refs.pop(n_in)
    rout = refs.pop(n_in + n_out)
    return refs[:-3], (rin, rout, *refs[-3:])


def _ride_start(rider, ride_refs, first):
    if rider is None:
        return

    @pl.when(first)
    def _():
        mine, copies = _peer_copies(rider[0], *ride_refs)
        mine.start()
        for cp in copies:
            cp.start()


def _ride_wait(rider, ride_refs, last):
    if rider is None:
        return

    @pl.when(last)
    def _():
        mine, copies = _peer_copies(rider[0], *ride_refs)
        for cp in copies:
            cp.wait()
        mine.wait()


def _gathered_cols(blocks, kdim):
    n = blocks.shape[1] * WIDE // kdim
    return blocks.reshape(N_DEV, kdim, n).transpose(1, 0, 2).reshape(kdim, N_DEV * n)


def _scatter_cols(dw):
    kdim, n8 = dw.shape
    n = n8 // N_DEV
    return dw.reshape(kdim, N_DEV, n).transpose(1, 0, 2).reshape(N_DEV, kdim * n // WIDE, WIDE)


def _pad_rows(a, rows):
    pad = [(0, 0)] * a.ndim
    pad[-2] = (0, rows - a.shape[-2])
    return jnp.pad(a, pad)


def _layer0_in_weight_t(wt):
    cq, ckv, kpe = wt[0:256], wt[256:384], wt[384:416]
    q_s, k_s, v_s, gate = wt[416:928], wt[928:1056], wt[1056:1184], wt[1184:2208]
    z = jnp.zeros((64, wt.shape[1]), wt.dtype)
    return jnp.concatenate([gate, cq, ckv, z, kpe, z[:32], q_s, k_s, v_s], axis=0)


def _layer0_in_grad_t(dwt):
    gate, cq, ckv, kpe = dwt[0:1024], dwt[1024:1280], dwt[1280:1408], dwt[1472:1504]
    q_s, k_s, v_s = dwt[1536:2048], dwt[2048:2176], dwt[2176:2304]
    return jnp.concatenate([cq, ckv, kpe, q_s, k_s, v_s, gate], axis=0)


def _layer1_in_weight_t(wt):
    main = jnp.concatenate([wt[:3 * D_MODEL], wt[3 * D_MODEL + FOX_HEADS:]], axis=0)
    return main, _pad_rows(wt[3 * D_MODEL:3 * D_MODEL + FOX_HEADS], LANES)


def _layer1_in_grad_t(d_main, d_wft):
    return jnp.concatenate([d_main[:3 * D_MODEL], d_wft[:FOX_HEADS], d_main[3 * D_MODEL:]], axis=0)


def _q_up_weight(w):
    return jnp.pad(w.reshape(MLA_Q_RANK, MLA_HEADS, 96), ((0, 0), (0, 0), (0, 32))).reshape(MLA_Q_RANK, MLA_HEADS * LANES)


def _q_up_grad(dwp):
    return dwp.reshape(MLA_Q_RANK, MLA_HEADS, LANES)[:, :, :96].reshape(MLA_Q_RANK, MLA_HEADS * 96)


def _kv_up_weight(w):
    w4 = w.reshape(MLA_KV_RANK, MLA_HEADS, 2, 64)
    kp = jnp.pad(w4[:, :, 0, :], ((0, 0), (0, 0), (0, 64))).reshape(MLA_KV_RANK, MLA_HEADS * LANES)
    vp = w4[:, :, 1, :].reshape(MLA_KV_RANK, MLA_HEADS * 64)
    return jnp.concatenate([kp, vp], axis=1)


def _kv_up_grad(dwp):
    dk = dwp[:, :MLA_HEADS * LANES].reshape(MLA_KV_RANK, MLA_HEADS, LANES)[:, :, :64]
    dv = dwp[:, MLA_HEADS * LANES:].reshape(MLA_KV_RANK, MLA_HEADS, 64)
    return jnp.stack([dk, dv], axis=2).reshape(MLA_KV_RANK, MLA_HEADS * LANES)


def _pad_lanes(a):
    return jnp.pad(a, ((0, 0), (0, LANES - a.shape[1])))


def _small_pack(g_in, g_final, g_q_a, g_kv_a, sinks, b_f, loss):
    rows = [g_in.reshape(8, LANES), g_final.reshape(8, LANES), g_q_a.reshape(2, LANES), g_kv_a.reshape(1, LANES),
            _pad_lanes(sinks.reshape(1, -1)), _pad_lanes(b_f.reshape(1, -1)), _pad_lanes(loss.reshape(1, 1)),
            jnp.zeros((2, LANES), F32)]
    return jnp.concatenate(rows, axis=0)


def _small_unpack(a):
    return (a[0:8].reshape(1, D_MODEL), a[8:16].reshape(D_MODEL), a[16:18].reshape(1, MLA_Q_RANK),
            a[18:19].reshape(1, MLA_KV_RANK), a[19:20, :SWA_HEADS], a[20:21, :FOX_HEADS], a[21, 0])


def _local_step(x, positions, target, e_g_in, w0t, e_g_q_a, wq, e_g_kv_a, wkv, e_sinks, wo0,
                layer1, o_b_f, g_final, scatter1=None):
    s = x.shape[0]
    att_t = min(ATT_T, s)
    nb = s // att_t
    mla_scale = (MLA_NOPE + MLA_ROPE) ** -0.5
    fox_scale = FOX_DIM ** -0.5
    n0a = Z0A_UNITS * LANES

    inv_freq = 1.0 / (ROPE_THETA ** (jnp.arange(0, MLA_ROPE, 2, dtype=F32) / MLA_ROPE))
    ang = positions.astype(F32)[:, None] * inv_freq
    cos, sin = jnp.cos(ang), jnp.sin(ang)
    ones, zeros = jnp.ones((s, 64), F32), jnp.zeros((s, 64), F32)
    cos_t = jnp.concatenate([ones, cos, cos, ones[:, :32]], axis=1)
    sin_t = jnp.concatenate([zeros, -sin, sin, zeros[:, :32]], axis=1)

    h0 = _rmsnorm_fwd(x, e_g_in, width=D_MODEL, col_blk=0, name="l0_norm")
    z0a = _matmul(h0, w0t, tb=True, b_rows=(0, n0a), name="l0_in_a")
    z0b = _matmul(h0, w0t, tb=True, b_rows=(n0a, Z0B_UNITS * LANES), name="l0_in_b", out_dtype=BF16)
    cqn = _rmsnorm_fwd(z0a, e_g_q_a, width=MLA_Q_RANK, col_blk=4, name="l0_q_norm")
    ckvn = _rmsnorm_fwd(z0a, e_g_kv_a, width=MLA_KV_RANK, col_blk=10, name="l0_kv_norm")
    qp = _matmul(cqn, wq, name="l0_q_up")
    kvp = _matmul(ckvn, wkv, name="l0_kv_up", out_dtype=BF16)
    qm, km = _rope_fwd(qp, kvp, z0a, cos_t, sin_t, name="l0_rope")
    gathers = len(layer1) == 2
    res = _flash_fwd(qm, km, kvp, None, n_pairs=MLA_HEADS // 2, hw=LANES, q_off=0, k_off=0, v_off=MLA_HEADS,
                     scale=mla_scale, name="l0_mla_fwd", rider=("gather", layer1[0]) if gathers else None)
    o_mla, lse_mla = res[0], res[1]
    o_g_in, w1t, wft, wo1 = layer1[1](res[2]) if gathers else layer1
    o_swa, lse_swa = _swa_fwd(z0b, e_sinks, name="l0_swa_fwd")
    og0 = _gate_fwd([o_mla, o_swa], z0a, name="l0_gate")
    x1 = _matmul(og0, wo0, add=x, name="l0_out")

    h1 = _rmsnorm_fwd(x1, o_g_in, width=D_MODEL, col_blk=0, name="l1_norm")
    z1 = _matmul(h1, w1t, tb=True, b_rows=(0, 3 * D_MODEL), name="l1_in_qkv", out_dtype=BF16)
    gate1 = _matmul(h1, w1t, tb=True, b_rows=(3 * D_MODEL, D_MODEL), name="l1_in_gate")
    zf = _matmul(h1, wft, tb=True, name="l1_in_f")
    bf = _pad_lanes(o_b_f)
    log_cum = _logf_fwd(zf, bf, name="l1_logf")
    bias = (-LOG2E * log_cum[:, :FOX_HEADS]).T.reshape(FOX_HEADS // 2, 2, nb, 1, att_t)
    o_fox, lse_fox = _flash_fwd(z1, z1, z1, bias, n_pairs=FOX_HEADS // 2, hw=64, q_off=0, k_off=8, v_off=16,
                                scale=fox_scale, name="l1_fox_fwd")
    og1 = _gate_fwd([o_fox], gate1, name="l1_gate")
    x2 = _matmul(og1, wo1, add=x1, name="l1_out")

    dx2, loss_part, d_g_final = _loss_head(x2, g_final.reshape(1, D_MODEL), target, name="loss_head")

    d_wo1 = _matmul(og1, dx2, ta=True, name="l1_out_dw")
    d_og1 = _matmul(dx2, wo1, tb=True, name="l1_out_dx")
    do_fox, d_gate1 = _gate_bwd(d_og1, [o_fox], gate1, name="l1_gate_bwd")
    dq1, dk1, dv1, dbias, drow = _flash_bwd(z1, z1, z1, do_fox, o_fox, lse_fox, bias, n_pairs=FOX_HEADS // 2, hw=64,
                                            q_off=0, k_off=8, v_off=16, scale=fox_scale, qk_dtype=BF16,
                                            name="l1_fox_bwd")
    d_log_cum = (drow.reshape(FOX_HEADS, s) - dbias.reshape(FOX_HEADS, s)).T
    d_log_cum = jnp.pad(d_log_cum, ((0, 0), (0, LANES - FOX_HEADS)))
    d_zf, d_bf = _logf_bwd(d_log_cum, zf, bf, name="l1_logf_bwd")
    dz1 = jnp.concatenate([dq1, dk1, dv1, d_gate1], axis=1)
    d_w1t = _matmul(dz1, h1, ta=True, name="l1_in_dw")
    d_wft = _matmul(d_zf, h1, ta=True, name="l1_in_f_dw")
    dh1 = _matmul(dz1, w1t, name="l1_in_dx")
    dh1 = _matmul(d_zf, wft, add=dh1, name="l1_in_f_dx")
    dx1, d_o_g_in = _rmsnorm_bwd(x1, o_g_in, dh1, width=D_MODEL, col_blk=0, add=dx2, name="l1_norm_bwd")

    d_wo0 = _matmul(og0, dx1, ta=True, name="l0_out_dw")
    d_og0 = _matmul(dx1, wo0, tb=True, name="l0_out_dx")
    do_mla, do_swa, d_gate0 = _gate_bwd(d_og0, [o_mla, o_swa], z0a, name="l0_gate_bwd")
    dq_s, dkt_s, dvt_s, d_sinks = _swa_bwd(z0b, e_sinks, do_swa, o_swa, lse_swa, name="l0_swa_bwd")
    dk_s = dkt_s.transpose(0, 2, 1).reshape(s, LANES)
    dv_s = dvt_s.transpose(0, 2, 1).reshape(s, LANES)
    rider = None
    if scatter1 is not None:
        rider = ("exchange", scatter1(dict(w1t=d_w1t, wft=d_wft, wo1=d_wo1, o_g_in=d_o_g_in)))
    res = _flash_bwd(qm, km, kvp, do_mla, o_mla, lse_mla, None, n_pairs=MLA_HEADS // 2, hw=LANES, q_off=0, k_off=0,
                     v_off=MLA_HEADS, scale=mla_scale, qk_dtype=F32, name="l0_mla_bwd", rider=rider)
    dqm, dkm, dvm = res[0], res[1], res[2]
    recv1 = res[3] if rider is not None else None
    d_qp, d_kvp, d_kpe = _rope_bwd(dqm, dkm, dvm, cos_t, sin_t, name="l0_rope_bwd")
    d_wq = _matmul(cqn, d_qp, ta=True, name="l0_q_up_dw")
    d_cqn = _matmul(d_qp, wq, tb=True, name="l0_q_up_dx")
    d_wkv = _matmul(ckvn, d_kvp, ta=True, name="l0_kv_up_dw")
    d_ckvn = _matmul(d_kvp, wkv, tb=True, name="l0_kv_up_dx")
    d_cq, d_g_q_a = _rmsnorm_bwd(z0a, e_g_q_a, d_cqn, width=MLA_Q_RANK, col_blk=4, out_dtype=BF16, name="l0_q_norm_bwd")
    d_ckv, d_g_kv_a = _rmsnorm_bwd(z0a, e_g_kv_a, d_ckvn, width=MLA_KV_RANK, col_blk=10, out_dtype=BF16,
                                   name="l0_kv_norm_bwd")
    dz0 = jnp.concatenate([d_gate0, d_cq, d_ckv, d_kpe, dq_s.astype(BF16), dk_s.astype(BF16), dv_s.astype(BF16)], axis=1)
    d_w0t = _matmul(dz0, h0, ta=True, name="l0_in_dw")
    dh0 = _matmul(dz0, w0t, name="l0_in_dx")
    grad_x, d_e_g_in = _rmsnorm_bwd(x, e_g_in, dh0, width=D_MODEL, col_blk=0, add=dx1, name="l0_norm_bwd")

    return dict(recv1=recv1, loss=loss_part[0, 0], grad_x=grad_x, e_g_in=d_e_g_in, w0t=d_w0t, e_g_q_a=d_g_q_a, wq=d_wq,
                e_g_kv_a=d_g_kv_a, wkv=d_wkv, e_sinks=d_sinks[:, 0].reshape(1, SWA_HEADS), wo0=d_wo0,
                o_g_in=d_o_g_in, w1t=d_w1t, wft=d_wft, o_b_f=d_bf[:, :FOX_HEADS], wo1=d_wo1, g_final=d_g_final.reshape(D_MODEL))


def _wide(a, rows):
    flat = a.reshape(-1)
    return jnp.pad(flat, (0, rows * WIDE - flat.shape[0])).reshape(rows, WIDE)


def _rows_b0(w_out, w_q, w_kv):
    return jnp.concatenate([w_out, _wide(w_q, 32), _wide(w_kv, 16)], axis=0)


def _unflat_b0(f):
    return f[128:152].reshape(1, MLA_Q_RANK, 96), f[160:176].reshape(1, MLA_KV_RANK, 128), f[0:128][None]


def _rows_b1(w_out, g_in):
    return jnp.concatenate([w_out, _wide(g_in, 16)], axis=0)


def _unflat_b1(f):
    return f[0:128][None], f[128:129, :LANES]


def kernel(x, positions, e_g_in, e_w_in, e_g_q_a, e_w_q_up, e_g_kv_a, e_w_kv_up, e_sinks, e_w_out, o_g_in, o_w_in, o_b_f, o_w_out, g_final, loss_target, m_e_g_in, m_e_w_in, m_e_g_q_a, m_e_w_q_up, m_e_g_kv_a, m_e_w_kv_up, m_e_sinks, m_e_w_out, m_o_g_in, m_o_w_in, m_o_b_f, m_o_w_out, m_g_final, v_e_g_in, v_e_w_in, v_e_g_q_a, v_e_w_q_up, v_e_g_kv_a, v_e_w_kv_up, v_e_sinks, v_e_w_out, v_o_g_in, v_o_w_in, v_o_b_f, v_o_w_out, v_g_final):
    def bf(a):
        return a.astype(BF16)

    shard0 = jnp.concatenate([_pad_rows(bf(e_w_in[0]).T, RA0), _rows_b0(bf(e_w_out[0]), bf(e_w_q_up[0]), bf(e_w_kv_up[0]))],
                             axis=0)
    gath0 = _all_gather(shard0, name="weights0_all_gather")
    w0t = _layer0_in_weight_t(gath0[:, :N_E_IN].reshape(N_DEV * N_E_IN, WIDE))
    wo0 = gath0[:, RA0:RA0 + 128].reshape(D_MODEL, D_MODEL)
    wq = _q_up_weight(_gathered_cols(gath0[:, RA0 + 128:RA0 + 152], MLA_Q_RANK))
    wkv = _kv_up_weight(_gathered_cols(gath0[:, RA0 + 160:RA0 + 176], MLA_KV_RANK))

    g_bits = lax.bitcast_convert_type(o_g_in.reshape(LANES), BF16)
    shard1 = jnp.concatenate([_pad_rows(bf(o_w_in[0]).T, RA1), bf(o_w_out[0]), _wide(g_bits, 16)], axis=0)

    def unpack1(gath1):
        w1t, wft = _layer1_in_weight_t(gath1[:, :N_O_IN].reshape(N_DEV * N_O_IN, WIDE))
        wo1 = gath1[:, RA1:RA1 + 128].reshape(D_MODEL, D_MODEL)
        bits = gath1[:, RA1 + 128, :2 * LANES].reshape(N_DEV, LANES, 2)
        return lax.bitcast_convert_type(bits, F32).reshape(1, D_MODEL), w1t, wft, wo1

    def scatter1(g):
        d_in_t = _layer1_in_grad_t(g["w1t"], g["wft"]).reshape(N_DEV, N_O_IN, WIDE)
        d_o_g = jnp.pad(g["o_g_in"].reshape(N_DEV, 1, LANES), ((0, 0), (0, 15), (0, WIDE - LANES)))
        return jnp.concatenate([_pad_rows(d_in_t, RA1), g["wo1"].reshape(N_DEV, 128, WIDE), d_o_g], axis=1).astype(BF16)

    gr = _local_step(x[0], positions[0], loss_target[0], e_g_in, w0t, e_g_q_a, wq, e_g_kv_a, wkv, e_sinks, wo0,
                     (shard1, unpack1), o_b_f, g_final, scatter1=scatter1)

    pieces0 = jnp.concatenate([
        _pad_rows(_layer0_in_grad_t(gr["w0t"]).reshape(N_DEV, N_E_IN, WIDE), RA0), gr["wo0"].reshape(N_DEV, 128, WIDE),
        _pad_rows(_scatter_cols(_q_up_grad(gr["wq"])), 32), _scatter_cols(_kv_up_grad(gr["wkv"]))], axis=1)
    recv0 = _exchange(pieces0.astype(BF16), name="grads0_exchange")

    def in_projection(recv, ra, n, w, m, v, name):
        g = _sum8(recv, ra, name=name + "_grad_sum")[:n].T
        d, nm, nv = _adamw_native(g, w[0], m[0], v[0], name=name + "_adamw")
        return g[None], d[None], nm[None], nv[None]

    e_in = in_projection(recv0, RA0, N_E_IN, e_w_in, m_e_w_in, v_e_w_in, "e_w_in")
    o_in = in_projection(gr["recv1"], RA1, N_O_IN, o_w_in, m_o_w_in, v_o_w_in, "o_w_in")
    b0 = _adamw(recv0[:, RA0:], _rows_b0(e_w_out[0], e_w_q_up[0], e_w_kv_up[0]),
                _rows_b0(m_e_w_out[0], m_e_w_q_up[0], m_e_w_kv_up[0]),
                _rows_b0(v_e_w_out[0], v_e_w_q_up[0], v_e_w_kv_up[0]), name="adamw_layer0")
    b1 = _adamw(gr["recv1"][:, RA1:], _rows_b1(o_w_out[0], o_g_in), _rows_b1(m_o_w_out[0], m_o_g_in),
                _rows_b1(v_o_w_out[0], v_o_g_in), name="adamw_layer1")
    g_sh, d_sh, m_sh, v_sh = [(e_in[k], *_unflat_b0(b0[k]), o_in[k], *_unflat_b1(b1[k])) for k in range(4)]

    small = _small_pack(gr["e_g_in"], gr["g_final"], gr["e_g_q_a"], gr["e_g_kv_a"], gr["e_sinks"], gr["o_b_f"], gr["loss"])
    small_all = _all_gather(small, name="small_all_gather")
    zero = jnp.zeros((), F32)
    w_small = _small_pack(e_g_in, g_final, e_g_q_a, e_g_kv_a, e_sinks, o_b_f, zero)
    m_small = _small_pack(m_e_g_in, m_g_final, m_e_g_q_a, m_e_g_kv_a, m_e_sinks, m_o_b_f, zero)
    v_small = _small_pack(v_e_g_in, v_g_final, v_e_g_q_a, v_e_g_kv_a, v_e_sinks, v_o_b_f, zero)
    smalls = _adamw(small_all, w_small, m_small, v_small, name="adamw_replicated")
    g_sm, d_sm, m_sm, v_sm = [_small_unpack(a) for a in smalls]
    loss = g_sm[6]

    def leaves(sh, sm):
        return (sm[0], sh[0], sm[2], sh[1], sm[3], sh[2], sm[4], sh[3], sh[6], sh[4], sm[5], sh[5], sm[1])

    return (loss, gr["grad_x"][None], *leaves(g_sh, g_sm), *leaves(d_sh, d_sm), *leaves(m_sh, m_sm), *leaves(v_sh, v_sm))
```

```python
import functools

import jax
import jax.numpy as jnp
from jax import lax
from jax.experimental import pallas as pl
from jax.experimental.pallas import tpu as pltpu

F32 = jnp.float32
BF16 = jnp.bfloat16
NEG_INF = float("-inf")

N_DEV = 8
LANES = 128
D_MODEL = 1024
EPS = 1e-6
ROPE_THETA = 10000.0
MLA_HEADS = 8
MLA_Q_RANK = 256
MLA_KV_RANK = 128
MLA_NOPE = 64
MLA_ROPE = 32
MLA_V = 64
SWA_HEADS = 8
SWA_KV_HEADS = 2
SWA_DIM = 64
WINDOW = 128
FOX_HEADS = 16
FOX_DIM = 64

ADAM_LR = 0.001
ADAM_B1 = 0.9
ADAM_B2 = 0.999
ADAM_EPS = 1e-08
ADAM_WD = 0.01
ADAM_STEP = 10

ATT_T = 512
VMEM_LIMIT = 56 * 1024 * 1024

Z0A_UNITS = 12
Z0B_UNITS = 6

WIDE = 1024
N_E_IN = 276
N_O_IN = 514
RA0 = 288
RB0 = 32 + 16
RA1 = 528
RB1 = 128 + 128 + 16
SMALL_ROWS = 24


def _tile(n, cands):
    for c in cands:
        if n % c == 0:
            return c
    raise ValueError(f"no tile for {n}")


def _params(sem, vmem=None):
    return pltpu.CompilerParams(dimension_semantics=sem, vmem_limit_bytes=vmem)


def _matmul(a, b, *, name, ta=False, tb=False, add=None, out_dtype=F32, b_rows=None):
    if ta:
        kdim, m = a.shape
    else:
        m, kdim = a.shape
    if tb:
        n, kb = b.shape
    else:
        kb, n = b.shape
    assert kdim == kb, (a.shape, b.shape)
    b_start = 0
    if b_rows is not None:
        assert tb
        b_start, n = b_rows
    tm = _tile(m, (512, 256, 128))
    tn = _tile(n, (768, 512, 384, 256, 128))
    assert b_start % tn == 0, (b_start, tn)
    b_off = b_start // tn
    dims = (((0 if ta else 1,), (1 if tb else 0,)), ((), ()))

    def body(*refs):
        if add is None:
            a_ref, b_ref, o_ref = refs
            add_ref = None
        else:
            a_ref, b_ref, add_ref, o_ref = refs
        r = lax.dot_general(a_ref[...].astype(BF16), b_ref[...].astype(BF16), dims, preferred_element_type=F32)
        if add_ref is not None:
            r = r + add_ref[...]
        o_ref[...] = r.astype(out_dtype)

    a_spec = pl.BlockSpec((kdim, tm), lambda i, j: (0, i)) if ta else pl.BlockSpec((tm, kdim), lambda i, j: (i, 0))
    b_spec = pl.BlockSpec((tn, kdim), lambda i, j: (j + b_off, 0)) if tb else pl.BlockSpec((kdim, tn), lambda i, j: (0, j))
    in_specs = [a_spec, b_spec]
    args = [a, b]
    if add is not None:
        in_specs.append(pl.BlockSpec((tm, tn), lambda i, j: (i, j)))
        args.append(add)
    return pl.pallas_call(
        body, name=name, grid=(m // tm, n // tn),
        in_specs=in_specs, out_specs=pl.BlockSpec((tm, tn), lambda i, j: (i, j)),
        out_shape=jax.ShapeDtypeStruct((m, n), out_dtype),
        compiler_params=_params(("parallel", "parallel"), VMEM_LIMIT),
    )(*args)


def _rmsnorm_fwd(x, g, *, width, col_blk, name):
    s = x.shape[0]
    tm = _tile(s, (256, 128))

    def body(x_ref, g_ref, y_ref):
        xf = x_ref[...].astype(F32)
        r = lax.rsqrt(jnp.mean(xf * xf, axis=-1, keepdims=True) + EPS)
        y_ref[...] = ((xf * r) * g_ref[...]).astype(BF16)

    return pl.pallas_call(
        body, name=name, grid=(s // tm,),
        in_specs=[pl.BlockSpec((tm, width), lambda i: (i, col_blk)), pl.BlockSpec((1, width), lambda i: (0, 0))],
        out_specs=pl.BlockSpec((tm, width), lambda i: (i, 0)),
        out_shape=jax.ShapeDtypeStruct((s, width), BF16),
        compiler_params=_params(("parallel",)),
    )(x, g)


def _rmsnorm_bwd(x, g, dy, *, width, col_blk, name, add=None, out_dtype=F32):
    s = x.shape[0]
    tm = _tile(s, (256, 128))

    def body(*refs):
        if add is None:
            x_ref, g_ref, dy_ref, dx_ref, dg_ref = refs
            add_ref = None
        else:
            x_ref, g_ref, dy_ref, add_ref, dx_ref, dg_ref = refs
        i = pl.program_id(0)
        xf = x_ref[...].astype(F32)
        r = lax.rsqrt(jnp.mean(xf * xf, axis=-1, keepdims=True) + EPS)
        xh = xf * r
        dyf = dy_ref[...].astype(F32)

        @pl.when(i == 0)
        def _():
            dg_ref[...] = jnp.zeros_like(dg_ref)

        dg_ref[...] += jnp.sum(dyf * xh, axis=0, keepdims=True)
        dxh = dyf * g_ref[...]
        dx = r * (dxh - xh * jnp.mean(dxh * xh, axis=-1, keepdims=True))
        if add_ref is not None:
            dx = dx + add_ref[...]
        dx_ref[...] = dx.astype(out_dtype)

    in_specs = [pl.BlockSpec((tm, width), lambda i: (i, col_blk)), pl.BlockSpec((1, width), lambda i: (0, 0)),
                pl.BlockSpec((tm, width), lambda i: (i, 0))]
    args = [x, g, dy]
    if add is not None:
        in_specs.append(pl.BlockSpec((tm, width), lambda i: (i, 0)))
        args.append(add)
    return pl.pallas_call(
        body, name=name, grid=(s // tm,),
        in_specs=in_specs,
        out_specs=[pl.BlockSpec((tm, width), lambda i: (i, 0)), pl.BlockSpec((1, width), lambda i: (0, 0))],
        out_shape=[jax.ShapeDtypeStruct((s, width), out_dtype), jax.ShapeDtypeStruct((1, width), F32)],
        compiler_params=_params(("arbitrary",)),
    )(*args)


def _sigmoid(x):
    return 1.0 / (1.0 + jnp.exp(-x))


def _gate_fwd(o_parts, gate, *, name):
    s = gate.shape[0]
    tm = _tile(s, (256, 128))
    n_o = len(o_parts)

    def body(*refs):
        o_refs, g_ref, y_ref = refs[:n_o], refs[n_o], refs[n_o + 1]
        o = o_refs[0][...] if n_o == 1 else jnp.concatenate([r[...] for r in o_refs], axis=1)
        gt = g_ref[...]
        y_ref[...] = (o * (gt * _sigmoid(gt))).astype(BF16)

    in_specs = [pl.BlockSpec((tm, o.shape[1]), lambda i: (i, 0)) for o in o_parts]
    in_specs.append(pl.BlockSpec((tm, D_MODEL), lambda i: (i, 0)))
    return pl.pallas_call(
        body, name=name, grid=(s // tm,), in_specs=in_specs,
        out_specs=pl.BlockSpec((tm, D_MODEL), lambda i: (i, 0)),
        out_shape=jax.ShapeDtypeStruct((s, D_MODEL), BF16),
        compiler_params=_params(("parallel",)),
    )(*o_parts, gate)


def _gate_bwd(d_og, o_parts, gate, *, name):
    s = gate.shape[0]
    tm = _tile(s, (256, 128))
    n_o = len(o_parts)
    widths = [o.shape[1] for o in o_parts]

    def body(*refs):
        d_ref, o_refs, g_ref = refs[0], refs[1:1 + n_o], refs[1 + n_o]
        do_refs, dg_ref = refs[2 + n_o:2 + 2 * n_o], refs[2 + 2 * n_o]
        d = d_ref[...]
        gt = g_ref[...]
        sg = _sigmoid(gt)
        silu = gt * sg
        dsilu = sg * (1.0 + gt * (1.0 - sg))
        o = o_refs[0][...] if n_o == 1 else jnp.concatenate([r[...] for r in o_refs], axis=1)
        dg_ref[...] = (d * o * dsilu).astype(BF16)
        do = d * silu
        off = 0
        for r, w in zip(do_refs, widths):
            r[...] = do[:, off:off + w]
            off += w

    in_specs = [pl.BlockSpec((tm, D_MODEL), lambda i: (i, 0))]
    in_specs += [pl.BlockSpec((tm, w), lambda i: (i, 0)) for w in widths]
    in_specs.append(pl.BlockSpec((tm, D_MODEL), lambda i: (i, 0)))
    out_specs = [pl.BlockSpec((tm, w), lambda i: (i, 0)) for w in widths]
    out_specs.append(pl.BlockSpec((tm, D_MODEL), lambda i: (i, 0)))
    out_shape = [jax.ShapeDtypeStruct((s, w), F32) for w in widths]
    out_shape.append(jax.ShapeDtypeStruct((s, D_MODEL), BF16))
    return pl.pallas_call(
        body, name=name, grid=(s // tm,), in_specs=in_specs, out_specs=out_specs, out_shape=out_shape,
        compiler_params=_params(("parallel",)),
    )(d_og, *o_parts, gate)


def _rot_half(x):
    lane = lax.broadcasted_iota(jnp.int32, x.shape, 1)
    return jnp.where(lane < 80, pltpu.roll(x, LANES - 16, axis=1), pltpu.roll(x, 16, axis=1))


def _rot_half_t(g):
    lane = lax.broadcasted_iota(jnp.int32, g.shape, 1)
    lo = (lane >= MLA_NOPE) & (lane < MLA_NOPE + MLA_ROPE // 2)
    hi = (lane >= MLA_NOPE + MLA_ROPE // 2) & (lane < MLA_NOPE + MLA_ROPE)
    return jnp.where(lo, pltpu.roll(g, LANES - 16, axis=1), jnp.where(hi, pltpu.roll(g, 16, axis=1), 0.0))


def _rope_fwd(qp, kvp, z0a, cos_t, sin_t, *, name):
    s = qp.shape[0]
    tm = _tile(s, (256, 128))
    hw = MLA_HEADS * LANES

    def body(q_ref, k_ref, kpe_ref, c_ref, s_ref, qm_ref, km_ref):
        c = c_ref[...]
        sn = s_ref[...]
        kpe = kpe_ref[...]
        kpe_r = (kpe * c + _rot_half(kpe) * sn).astype(BF16)
        lane = lax.broadcasted_iota(jnp.int32, kpe.shape, 1)
        for h in range(MLA_HEADS):
            sl = slice(h * LANES, (h + 1) * LANES)
            qh = q_ref[:, sl]
            qm_ref[:, sl] = (qh * c + _rot_half(qh) * sn).astype(BF16)
            km_ref[:, sl] = jnp.where(lane < MLA_NOPE, k_ref[:, sl], kpe_r)

    return pl.pallas_call(
        body, name=name, grid=(s // tm,),
        in_specs=[pl.BlockSpec((tm, hw), lambda i: (i, 0)), pl.BlockSpec((tm, hw), lambda i: (i, 0)),
                  pl.BlockSpec((tm, LANES), lambda i: (i, 11)),
                  pl.BlockSpec((tm, LANES), lambda i: (i, 0)), pl.BlockSpec((tm, LANES), lambda i: (i, 0))],
        out_specs=[pl.BlockSpec((tm, hw), lambda i: (i, 0)), pl.BlockSpec((tm, hw), lambda i: (i, 0))],
        out_shape=[jax.ShapeDtypeStruct((s, hw), BF16), jax.ShapeDtypeStruct((s, hw), BF16)],
        compiler_params=_params(("parallel",)),
    )(qp, kvp, z0a, cos_t, sin_t)


def _rope_bwd(dqm, dkm, dvm, cos_t, sin_t, *, name):
    s = dqm.shape[0]
    tm = _tile(s, (256, 128))
    hw = MLA_HEADS * LANES
    vw = MLA_HEADS * MLA_V

    def body(dq_ref, dk_ref, dv_ref, c_ref, s_ref, dqp_ref, dkv_ref, dkpe_ref):
        c = c_ref[...]
        sn = s_ref[...]
        ksum = jnp.zeros((tm, LANES), F32)
        for h in range(MLA_HEADS):
            sl = slice(h * LANES, (h + 1) * LANES)
            dq = dq_ref[:, sl]
            dqp_ref[:, sl] = (dq * c + _rot_half_t(dq * sn)).astype(BF16)
            dk = dk_ref[:, sl]
            dkv_ref[:, sl] = dk.astype(BF16)
            ksum = ksum + dk
        dkv_ref[:, hw:] = dv_ref[...]
        lane = lax.broadcasted_iota(jnp.int32, ksum.shape, 1)
        dkpe = ksum * c + _rot_half_t(ksum * sn)
        dkpe_ref[...] = jnp.where((lane >= MLA_NOPE) & (lane < MLA_NOPE + MLA_ROPE), dkpe, 0.0).astype(BF16)

    return pl.pallas_call(
        body, name=name, grid=(s // tm,),
        in_specs=[pl.BlockSpec((tm, hw), lambda i: (i, 0)), pl.BlockSpec((tm, hw), lambda i: (i, 0)),
                  pl.BlockSpec((tm, vw), lambda i: (i, 0)),
                  pl.BlockSpec((tm, LANES), lambda i: (i, 0)), pl.BlockSpec((tm, LANES), lambda i: (i, 0))],
        out_specs=[pl.BlockSpec((tm, hw), lambda i: (i, 0)), pl.BlockSpec((tm, hw + vw), lambda i: (i, 0)),
                   pl.BlockSpec((tm, LANES), lambda i: (i, 0))],
        out_shape=[jax.ShapeDtypeStruct((s, hw), BF16), jax.ShapeDtypeStruct((s, hw + vw), BF16),
                   jax.ShapeDtypeStruct((s, LANES), BF16)],
        compiler_params=_params(("parallel",)),
    )(dqm, dkm, dvm, cos_t, sin_t)


def _head_mask(shape, a):
    lane = lax.broadcasted_iota(jnp.int32, shape, 1)
    return (lane >= 64 * a) & (lane < 64 * (a + 1))


def _causal_mask(t):
    row = lax.broadcasted_iota(jnp.int32, (t, t), 0)
    col = lax.broadcasted_iota(jnp.int32, (t, t), 1)
    return col <= row


_NT = (((1,), (1,)), ((), ()))
LOG2E = 1.4426950408889634


def _stack_heads(tile, hw):
    lane = lax.broadcasted_iota(jnp.int32, tile.shape, 1)
    z = jnp.zeros_like(tile)
    return jnp.concatenate([jnp.where(lane < hw, tile, z), jnp.where(lane >= hw, tile, z)], axis=0)


def _stacked_rows(r0, r1, t):
    n = r0.shape[-1]
    return jnp.concatenate([jnp.broadcast_to(r0, (t, n)), jnp.broadcast_to(r1, (t, n))], axis=0)


def _stacked_causal_mask(t):
    m = _causal_mask(t)
    return jnp.concatenate([m, m], axis=0)


def _resident(block, index_map):
    return pl.BlockSpec(block, index_map, pipeline_mode=pl.Buffered(1))


def _flash_fwd(q, k, v, bias, *, n_pairs, hw, q_off, k_off, v_off, scale, name, rider=None):
    s = q.shape[0]
    t = min(ATT_T, s)
    nb = s // t
    qw = 2 * hw
    has_bias = bias is not None
    c1 = scale * LOG2E

    def body(*refs):
        refs, ride_refs = _split_rider(refs, rider, n_in=4 if has_bias else 3, n_out=2)
        if has_bias:
            q_ref, k_ref, v_ref, b_ref, o_ref, lse_ref, vt_ref = refs
        else:
            q_ref, k_ref, v_ref, o_ref, lse_ref, vt_ref = refs
            b_ref = None
        _ride_start(rider, ride_refs, pl.program_id(0) == 0)
        row = lax.broadcasted_iota(jnp.int32, (t, t), 0)
        col = lax.broadcasted_iota(jnp.int32, (t, t), 1)
        cmask_t = jnp.concatenate([row <= col, row <= col], axis=1)
        lane_lt64 = lax.broadcasted_iota(jnp.int32, (t, LANES), 1) < 64

        def v_block(j, _):
            c0 = pl.multiple_of(j * t, t)
            vt_ref[j] = v_ref[pl.ds(c0, t), :].astype(F32).T.astype(BF16)
            return 0

        lax.fori_loop(0, nb, v_block, 0)

        def q_block(i, _):
            r0 = pl.multiple_of(i * t, t)
            qs_t = _stack_heads(q_ref[pl.ds(r0, t), :], hw).astype(F32).T.astype(BF16)

            def kv_step(j, carry, masked):
                m, l, acc = carry
                rows = pl.ds(pl.multiple_of(j * t, t), t)
                sc = jnp.dot(k_ref[rows, :], qs_t, preferred_element_type=F32) * c1
                if has_bias:
                    sc = sc + jnp.concatenate([jnp.broadcast_to(b_ref[0, 0, rows, :], (t, t)),
                                               jnp.broadcast_to(b_ref[0, 1, rows, :], (t, t))], axis=1)
                if masked:
                    sc = jnp.where(cmask_t, sc, NEG_INF)
                m_new = jnp.maximum(m, jnp.max(sc, axis=0, keepdims=True))
                alpha = jnp.exp2(m - m_new)
                p = jnp.exp2(sc - m_new)
                l_new = alpha * l + jnp.sum(p, axis=0, keepdims=True)
                pv = jnp.dot(vt_ref[j], p.astype(BF16), preferred_element_type=F32)
                return m_new, l_new, alpha * acc + pv

            init = (jnp.full((1, 2 * t), NEG_INF, F32), jnp.zeros((1, 2 * t), F32), jnp.zeros((LANES, 2 * t), F32))
            carry = lax.fori_loop(0, i, functools.partial(kv_step, masked=False), init)
            m, l, acc = kv_step(i, carry, True)
            out = (acc / l).T
            lse2 = m + jnp.log2(l)
            lse_ref[0, 0, i] = lse2[:, :t]
            lse_ref[0, 1, i] = lse2[:, t:]
            o_ref[pl.ds(r0, t), :] = jnp.where(lane_lt64, out[:t], out[t:])
            return 0

        lax.fori_loop(0, nb, q_block, 0)
        _ride_wait(rider, ride_refs, pl.program_id(0) == n_pairs - 1)

    in_specs = [_resident((s, qw), lambda p: (0, q_off + p)), _resident((s, qw), lambda p: (0, k_off + p)),
                _resident((s, LANES), lambda p: (0, v_off + p))]
    args = [q, k, v]
    if has_bias:
        in_specs.append(_resident((1, 2, s, 1), lambda p: (p, 0, 0, 0)))
        args.append(bias)
    out_specs = [pl.BlockSpec((s, LANES), lambda p: (0, p)), pl.BlockSpec((1, 2, nb, 1, t), lambda p: (p, 0, 0, 0, 0))]
    out_shape = [jax.ShapeDtypeStruct((s, n_pairs * LANES), F32), jax.ShapeDtypeStruct((n_pairs, 2, nb, 1, t), F32)]
    scratch = [pltpu.VMEM((nb, LANES, t), BF16)] + _add_rider(rider, in_specs, args, out_specs, out_shape)
    return pl.pallas_call(
        body, name=name, grid=(n_pairs,), in_specs=in_specs, out_specs=out_specs, out_shape=out_shape,
        scratch_shapes=scratch,
        compiler_params=_params(("parallel",) if rider is None else ("arbitrary",), VMEM_LIMIT),
    )(*args)


def _flash_bwd(q, k, v, do, o, lse, bias, *, n_pairs, hw, q_off, k_off, v_off, scale, qk_dtype, name, rider=None):
    s = q.shape[0]
    t = min(ATT_T, s)
    nb = s // t
    qw = 2 * hw
    has_bias = bias is not None
    c1 = scale * LOG2E

    def body(*refs):
        refs, ride_refs = _split_rider(refs, rider, n_in=7 if has_bias else 6, n_out=5 if has_bias else 3)
        if has_bias:
            (q_ref, k_ref, v_ref, do_ref, o_ref, lse_ref, b_ref, dq_ref, dk_ref, dv_ref, db_ref, dr_ref,
             dkt_ref, dvt_ref) = refs
            db_ref[...] = jnp.zeros_like(db_ref)
        else:
            q_ref, k_ref, v_ref, do_ref, o_ref, lse_ref, dq_ref, dk_ref, dv_ref, dkt_ref, dvt_ref = refs
            b_ref = db_ref = dr_ref = None
        _ride_start(rider, ride_refs, pl.program_id(0) == 0)
        dkt_ref[...] = jnp.zeros_like(dkt_ref)
        dvt_ref[...] = jnp.zeros_like(dvt_ref)
        cmask = _stacked_causal_mask(t)
        lane_lt_hw = lax.broadcasted_iota(jnp.int32, (t, qw), 1) < hw

        def q_block(i, _):
            r0 = pl.multiple_of(i * t, t)
            qs = _stack_heads(q_ref[pl.ds(r0, t), :], hw)
            dos = _stack_heads(do_ref[pl.ds(r0, t), :], 64)
            ot = o_ref[pl.ds(r0, t), :]
            delta = jnp.sum(dos * jnp.concatenate([ot, ot], axis=0), axis=-1, keepdims=True)
            lse2 = jnp.concatenate([lse_ref[0, 0, pl.ds(r0, t), :], lse_ref[0, 1, pl.ds(r0, t), :]], axis=0)
            dosb = dos.astype(BF16)
            dos_t = dos.T.astype(BF16)
            qs_t = qs.astype(F32).T.astype(BF16)

            def kv_step(j, carry, masked):
                dq, rsum = carry
                c0 = pl.multiple_of(j * t, t)
                kt = k_ref[pl.ds(c0, t), :]
                vt = v_ref[pl.ds(c0, t), :]
                sc = lax.dot_general(qs, kt, _NT, preferred_element_type=F32) * c1
                if has_bias:
                    sc = sc + _stacked_rows(b_ref[0, 0, j], b_ref[0, 1, j], t)
                if masked:
                    sc = jnp.where(cmask, sc, NEG_INF)
                p = jnp.exp2(sc - lse2)
                dp = lax.dot_general(dosb, vt, _NT, preferred_element_type=F32)
                ds = p * (dp - delta)
                dsb = ds.astype(BF16)
                dvt_ref[j] += jnp.dot(dos_t, p.astype(BF16), preferred_element_type=F32)
                dkt_ref[j] += jnp.dot(qs_t, dsb, preferred_element_type=F32)
                if has_bias:
                    db_ref[0, 0, j] += jnp.sum(ds[:t], axis=0, keepdims=True)
                    db_ref[0, 1, j] += jnp.sum(ds[t:], axis=0, keepdims=True)
                    rsum = rsum + jnp.sum(ds, axis=-1, keepdims=True)
                return dq + jnp.dot(dsb, kt, preferred_element_type=F32), rsum

            init = (jnp.zeros((2 * t, qw), F32), jnp.zeros((2 * t, 1), F32))
            carry = lax.fori_loop(0, i, functools.partial(kv_step, masked=False), init)
            dq, rsum = kv_step(i, carry, True)
            dq = dq * scale
            dq_ref[pl.ds(r0, t), :] = jnp.where(lane_lt_hw, dq[:t], dq[t:]).astype(qk_dtype)
            if has_bias:
                dr_ref[0, 0, pl.ds(r0, t), :] = rsum[:t]
                dr_ref[0, 1, pl.ds(r0, t), :] = rsum[t:]
            return 0

        lax.fori_loop(0, nb, q_block, 0)

        def k_block(j, _):
            c0 = pl.multiple_of(j * t, t)
            dk_ref[pl.ds(c0, t), :] = (dkt_ref[j].T * scale).astype(qk_dtype)
            dv_ref[pl.ds(c0, t), :] = dvt_ref[j].T.astype(BF16)
            return 0

        lax.fori_loop(0, nb, k_block, 0)
        _ride_wait(rider, ride_refs, pl.program_id(0) == n_pairs - 1)

    in_specs = [_resident((s, qw), lambda p: (0, q_off + p)), _resident((s, qw), lambda p: (0, k_off + p)),
                _resident((s, LANES), lambda p: (0, v_off + p)),
                _resident((s, LANES), lambda p: (0, p)), _resident((s, LANES), lambda p: (0, p)),
                _resident((1, 2, s, 1), lambda p: (p, 0, 0, 0))]
    args = [q, k, v, do, o, lse]
    out_specs = [pl.BlockSpec((s, qw), lambda p: (0, p)), pl.BlockSpec((s, qw), lambda p: (0, p)),
                 pl.BlockSpec((s, LANES), lambda p: (0, p))]
    out_shape = [jax.ShapeDtypeStruct((s, n_pairs * qw), qk_dtype), jax.ShapeDtypeStruct((s, n_pairs * qw), qk_dtype),
                 jax.ShapeDtypeStruct((s, n_pairs * LANES), BF16)]
    if has_bias:
        in_specs.append(_resident((1, 2, nb, 1, t), lambda p: (p, 0, 0, 0, 0)))
        args.append(bias)
        out_specs.append(pl.BlockSpec((1, 2, nb, 1, t), lambda p: (p, 0, 0, 0, 0)))
        out_shape.append(jax.ShapeDtypeStruct((n_pairs, 2, nb, 1, t), F32))
        out_specs.append(pl.BlockSpec((1, 2, s, 1), lambda p: (p, 0, 0, 0)))
        out_shape.append(jax.ShapeDtypeStruct((n_pairs, 2, s, 1), F32))
    scratch = [pltpu.VMEM((nb, qw, t), F32), pltpu.VMEM((nb, LANES, t), F32)]
    scratch += _add_rider(rider, in_specs, args, out_specs, out_shape)
    return pl.pallas_call(
        body, name=name, grid=(n_pairs,), in_specs=in_specs, out_specs=out_specs, out_shape=out_shape,
        scratch_shapes=scratch,
        compiler_params=_params(("parallel",) if rider is None else ("arbitrary",), VMEM_LIMIT),
    )(*args)


def _alibi_slope(h):
    return 2.0 ** (-8.0 * (h + 1.0) / SWA_HEADS)


SWA_ROWS = 512
SWA_SCALE = SWA_DIM ** -0.5


def _swa_geometry(i):
    w = WINDOW
    r0 = pl.multiple_of(i * w, w)
    b0 = pl.multiple_of(jnp.maximum(i - 1, 0) * w, w)
    row = lax.broadcasted_iota(jnp.int32, (w, 2 * w), 0)
    col = lax.broadcasted_iota(jnp.int32, (w, 2 * w), 1)
    dist = row - col + (r0 - b0)
    valid = (dist >= 0) & (dist < w)
    return r0, b0, dist.astype(F32), valid


def _swa_q_head(qblk, h):
    kv = h // (SWA_HEADS // SWA_KV_HEADS)
    if h % 2 != kv:
        qblk = pltpu.roll(qblk, 64, axis=1)
    return jnp.where(_head_mask(qblk.shape, kv), qblk, 0.0)


def _swa_fwd(z0b, sinks, *, name):
    s = z0b.shape[0]
    w = WINDOW
    rows = min(SWA_ROWS, s)
    per_step = rows // w
    qcols = SWA_HEADS * SWA_DIM

    def body(sink_ref, q_ref, k_ref, v_ref, o_ref, lse_ref):
        g = pl.program_id(0)
        for ii in range(per_step):
            r0, b0, dist, valid = _swa_geometry(g * per_step + ii)
            kb = k_ref[pl.ds(b0, 2 * w), :]
            vb = v_ref[pl.ds(b0, 2 * w), :]
            o_heads = []
            for h in range(SWA_HEADS):
                kv = h // (SWA_HEADS // SWA_KV_HEADS)
                blk = h // 2
                qh = _swa_q_head(q_ref[ii * w:(ii + 1) * w, blk * LANES:(blk + 1) * LANES].astype(F32), h).astype(BF16)
                sc = lax.dot_general(qh, kb, _NT, preferred_element_type=F32) * SWA_SCALE - _alibi_slope(h) * dist
                sc = jnp.where(valid, sc, NEG_INF)
                sink = sink_ref[0, h]
                m = jnp.maximum(jnp.max(sc, axis=-1, keepdims=True), sink)
                p = jnp.exp(sc - m)
                l = jnp.sum(p, axis=-1, keepdims=True) + jnp.exp(sink - m)
                oh = jnp.dot(p.astype(BF16), vb, preferred_element_type=F32) / l
                if h % 2 != kv:
                    oh = pltpu.roll(oh, 64, axis=1)
                o_heads.append(oh)
                lse_ref[h, ii * w:(ii + 1) * w, :] = m + jnp.log(l)
            lt64 = lax.broadcasted_iota(jnp.int32, (w, LANES), 1) < 64
            o_ref[ii * w:(ii + 1) * w, :] = jnp.concatenate(
                [jnp.where(lt64, o_heads[2 * b], o_heads[2 * b + 1]) for b in range(SWA_HEADS // 2)], axis=1)

    return pl.pallas_call(
        body, name=name, grid=(s // rows,),
        in_specs=[pl.BlockSpec(memory_space=pltpu.SMEM),
                  pl.BlockSpec((rows, qcols), lambda g: (g, 0)),
                  pl.BlockSpec((s, LANES), lambda g: (0, 4)), pl.BlockSpec((s, LANES), lambda g: (0, 5))],
        out_specs=[pl.BlockSpec((rows, qcols), lambda g: (g, 0)), pl.BlockSpec((SWA_HEADS, rows, 1), lambda g: (0, g, 0))],
        out_shape=[jax.ShapeDtypeStruct((s, qcols), F32), jax.ShapeDtypeStruct((SWA_HEADS, s, 1), F32)],
        compiler_params=_params(("parallel",), VMEM_LIMIT),
    )(sinks, z0b, z0b, z0b)


def _swa_bwd(z0b, sinks, do, o, lse, *, name):
    s = z0b.shape[0]
    w = WINDOW
    rows = min(SWA_ROWS, s)
    per_step = rows // w
    qcols = SWA_HEADS * SWA_DIM
    nblk = s // w

    def body(sink_ref, q_ref, k_ref, v_ref, do_ref, o_ref, lse_ref, dq_ref, dkt_ref, dvt_ref, dsink_ref):
        g = pl.program_id(0)

        @pl.when(g == 0)
        def _():
            dkt_ref[...] = jnp.zeros_like(dkt_ref)
            dvt_ref[...] = jnp.zeros_like(dvt_ref)
            dsink_ref[...] = jnp.zeros_like(dsink_ref)

        for ii in range(per_step):
            i = g * per_step + ii
            r0, b0, dist, valid = _swa_geometry(i)
            j0 = jnp.maximum(i - 1, 0)
            kb = k_ref[pl.ds(b0, 2 * w), :]
            vb = v_ref[pl.ds(b0, 2 * w), :]
            dq_heads = []
            for h in range(SWA_HEADS):
                kv = h // (SWA_HEADS // SWA_KV_HEADS)
                blk = h // 2
                cs = slice(blk * LANES, (blk + 1) * LANES)
                rs = slice(ii * w, (ii + 1) * w)
                qh32 = _swa_q_head(q_ref[rs, cs].astype(F32), h)
                qh = qh32.astype(BF16)
                doh32 = _swa_q_head(do_ref[rs, cs], h)
                oh32 = _swa_q_head(o_ref[rs, cs], h)
                delta = jnp.sum(doh32 * oh32, axis=-1, keepdims=True)
                lse = lse_ref[h, rs, :]
                sink = sink_ref[0, h]
                sc = lax.dot_general(qh, kb, _NT, preferred_element_type=F32) * SWA_SCALE - _alibi_slope(h) * dist
                sc = jnp.where(valid, sc, NEG_INF)
                p = jnp.exp(sc - lse)
                dp = lax.dot_general(doh32.astype(BF16), vb, _NT, preferred_element_type=F32)
                ds = p * (dp - delta)
                dsb = ds.astype(BF16)
                pb = p.astype(BF16)
                dsink_ref[h:h + 1, :] += jnp.broadcast_to(-jnp.sum(jnp.exp(sink - lse) * delta), (1, LANES))
                do_t = doh32.T.astype(BF16)
                q_t = qh32.T.astype(BF16)
                dvt = jnp.dot(do_t, pb, preferred_element_type=F32)
                dkt = jnp.dot(q_t, dsb, preferred_element_type=F32) * SWA_SCALE
                dvt_ref[j0] += dvt[:, :w]
                dvt_ref[j0 + 1] += dvt[:, w:]
                dkt_ref[j0] += dkt[:, :w]
                dkt_ref[j0 + 1] += dkt[:, w:]
                dq = jnp.dot(dsb, kb, preferred_element_type=F32) * SWA_SCALE
                if h % 2 != kv:
                    dq = pltpu.roll(dq, 64, axis=1)
                dq_heads.append(dq)
            lt64 = lax.broadcasted_iota(jnp.int32, (w, LANES), 1) < 64
            dq_ref[ii * w:(ii + 1) * w, :] = jnp.concatenate(
                [jnp.where(lt64, dq_heads[2 * b], dq_heads[2 * b + 1]) for b in range(SWA_HEADS // 2)], axis=1)

    return pl.pallas_call(
        body, name=name, grid=(s // rows,),
        in_specs=[pl.BlockSpec(memory_space=pltpu.SMEM),
                  pl.BlockSpec((rows, qcols), lambda g: (g, 0)),
                  pl.BlockSpec((s, LANES), lambda g: (0, 4)), pl.BlockSpec((s, LANES), lambda g: (0, 5)),
                  pl.BlockSpec((rows, qcols), lambda g: (g, 0)), pl.BlockSpec((rows, qcols), lambda g: (g, 0)),
                  pl.BlockSpec((SWA_HEADS, rows, 1), lambda g: (0, g, 0))],
        out_specs=[pl.BlockSpec((rows, qcols), lambda g: (g, 0)),
                   pl.BlockSpec((nblk, LANES, w), lambda g: (0, 0, 0)),
                   pl.BlockSpec((nblk, LANES, w), lambda g: (0, 0, 0)),
                   pl.BlockSpec((SWA_HEADS, LANES), lambda g: (0, 0))],
        out_shape=[jax.ShapeDtypeStruct((s, qcols), F32),
                   jax.ShapeDtypeStruct((nblk, LANES, w), F32), jax.ShapeDtypeStruct((nblk, LANES, w), F32),
                   jax.ShapeDtypeStruct((SWA_HEADS, LANES), F32)],
        compiler_params=_params(("arbitrary",), VMEM_LIMIT),
    )(sinks, z0b, z0b, z0b, do, o, lse)


CUM_T = 256


def _split3(x):
    hi = x.astype(BF16)
    r1 = x - hi.astype(F32)
    mid = r1.astype(BF16)
    lo = (r1 - mid.astype(F32)).astype(BF16)
    return hi, mid, lo


def _tri_dot(tri, x):
    hi, mid, lo = _split3(x)
    out = jnp.dot(tri, hi, preferred_element_type=F32)
    out = out + jnp.dot(tri, mid, preferred_element_type=F32)
    return out + jnp.dot(tri, lo, preferred_element_type=F32)


def _logf_fwd(zf, bf, *, name):
    s = zf.shape[0]
    t = CUM_T
    nb = s // t

    def body(z_ref, b_ref, c_ref, carry_ref):
        i = pl.program_id(0)

        @pl.when(i == 0)
        def _():
            carry_ref[...] = jnp.zeros_like(carry_ref)

        x = z_ref[...] + b_ref[...]
        lf = jnp.minimum(x, 0.0) - jnp.log(1.0 + jnp.exp(-jnp.abs(x)))
        row = lax.broadcasted_iota(jnp.int32, (t, t), 0)
        col = lax.broadcasted_iota(jnp.int32, (t, t), 1)
        tri = jnp.where(col <= row, 1.0, 0.0).astype(BF16)
        c = _tri_dot(tri, lf) + carry_ref[...]
        c_ref[...] = c
        carry_ref[...] = c[t - 1:t, :]

    return pl.pallas_call(
        body, name=name, grid=(nb,),
        in_specs=[pl.BlockSpec((t, LANES), lambda i: (i, 0)), pl.BlockSpec((1, LANES), lambda i: (0, 0))],
        out_specs=pl.BlockSpec((t, LANES), lambda i: (i, 0)),
        out_shape=jax.ShapeDtypeStruct((s, LANES), F32),
        scratch_shapes=[pltpu.VMEM((1, LANES), F32)],
        compiler_params=_params(("arbitrary",)),
    )(zf, bf)


def _logf_bwd(dc, zf, bf, *, name):
    s = zf.shape[0]
    t = CUM_T
    nb = s // t

    def body(dc_ref, z_ref, b_ref, dz_ref, db_ref, carry_ref):
        i = pl.program_id(0)

        @pl.when(i == 0)
        def _():
            carry_ref[...] = jnp.zeros_like(carry_ref)
            db_ref[...] = jnp.zeros_like(db_ref)

        row = lax.broadcasted_iota(jnp.int32, (t, t), 0)
        col = lax.broadcasted_iota(jnp.int32, (t, t), 1)
        tri = jnp.where(col >= row, 1.0, 0.0).astype(BF16)
        dlf = _tri_dot(tri, dc_ref[...]) + carry_ref[...]
        carry_ref[...] = dlf[0:1, :]
        x = z_ref[...] + b_ref[...]
        dz = dlf * _sigmoid(-x)
        dz_ref[...] = dz.astype(BF16)
        db_ref[...] += jnp.sum(dz, axis=0, keepdims=True)

    return pl.pallas_call(
        body, name=name, grid=(nb,),
        in_specs=[pl.BlockSpec((t, LANES), lambda i: (nb - 1 - i, 0)), pl.BlockSpec((t, LANES), lambda i: (nb - 1 - i, 0)),
                  pl.BlockSpec((1, LANES), lambda i: (0, 0))],
        out_specs=[pl.BlockSpec((t, LANES), lambda i: (nb - 1 - i, 0)), pl.BlockSpec((1, LANES), lambda i: (0, 0))],
        out_shape=[jax.ShapeDtypeStruct((s, LANES), BF16), jax.ShapeDtypeStruct((1, LANES), F32)],
        scratch_shapes=[pltpu.VMEM((1, LANES), F32)],
        compiler_params=_params(("arbitrary",)),
    )(dc, zf, bf)


def _loss_head(x2, g, target, *, name):
    s = x2.shape[0]
    tm = _tile(s, (256, 128))

    def body(x_ref, g_ref, t_ref, dx_ref, loss_ref, dg_ref):
        i = pl.program_id(0)

        @pl.when(i == 0)
        def _():
            loss_ref[...] = jnp.zeros_like(loss_ref)
            dg_ref[...] = jnp.zeros_like(dg_ref)

        xf = x_ref[...]
        r = lax.rsqrt(jnp.mean(xf * xf, axis=-1, keepdims=True) + EPS)
        xh = xf * r
        gv = g_ref[...]
        err = xh * gv - t_ref[...]
        loss_ref[...] += jnp.broadcast_to(0.5 * jnp.sum(jnp.mean(err * err, axis=-1, keepdims=True)), loss_ref.shape)
        dy = err * (1.0 / D_MODEL)
        dg_ref[...] += jnp.sum(dy * xh, axis=0, keepdims=True)
        dxh = dy * gv
        dx_ref[...] = r * (dxh - xh * jnp.mean(dxh * xh, axis=-1, keepdims=True))

    return pl.pallas_call(
        body, name=name, grid=(s // tm,),
        in_specs=[pl.BlockSpec((tm, D_MODEL), lambda i: (i, 0)), pl.BlockSpec((1, D_MODEL), lambda i: (0, 0)),
                  pl.BlockSpec((tm, D_MODEL), lambda i: (i, 0))],
        out_specs=[pl.BlockSpec((tm, D_MODEL), lambda i: (i, 0)), pl.BlockSpec((8, LANES), lambda i: (0, 0)),
                   pl.BlockSpec((1, D_MODEL), lambda i: (0, 0))],
        out_shape=[jax.ShapeDtypeStruct((s, D_MODEL), F32), jax.ShapeDtypeStruct((8, LANES), F32),
                   jax.ShapeDtypeStruct((1, D_MODEL), F32)],
        compiler_params=_params(("arbitrary",)),
    )(x2, g, target)


def _sum_pieces(p_ref):
    g = p_ref[0].astype(F32)
    for k in range(1, N_DEV):
        g = g + p_ref[k].astype(F32)
    return g


def _adam_update(g, w, m, v):
    bc1 = 1.0 - ADAM_B1 ** ADAM_STEP
    bc2 = 1.0 - ADAM_B2 ** ADAM_STEP
    nm = ADAM_B1 * m + (1.0 - ADAM_B1) * g
    nv = ADAM_B2 * v + (1.0 - ADAM_B2) * (g * g)
    m_hat = nm / bc1
    v_hat = nv / bc2
    return -ADAM_LR * (m_hat / (jnp.sqrt(v_hat) + ADAM_EPS) + ADAM_WD * w), nm, nv


def _adamw(pieces, w, m, v, *, name):
    rows, cols = w.shape
    tr = _tile(rows, (RB1, RB0, SMALL_ROWS))

    def body(p_ref, w_ref, m_ref, v_ref, g_ref, d_ref, nm_ref, nv_ref):
        g = _sum_pieces(p_ref)
        g_ref[...] = g
        d_ref[...], nm_ref[...], nv_ref[...] = _adam_update(g, w_ref[...], m_ref[...], v_ref[...])

    spec = pl.BlockSpec((tr, cols), lambda i: (i, 0))
    shape = jax.ShapeDtypeStruct((rows, cols), F32)
    return pl.pallas_call(
        body, name=name, grid=(rows // tr,),
        in_specs=[pl.BlockSpec((N_DEV, tr, cols), lambda i: (0, i, 0)), spec, spec, spec],
        out_specs=[spec, spec, spec, spec], out_shape=[shape, shape, shape, shape],
        compiler_params=_params(("parallel",)),
    )(pieces, w, m, v)


def _sum8(pieces, rows, *, name):
    cols = pieces.shape[2]
    tr = _tile(rows, (176, 96))

    def body(p_ref, g_ref):
        g_ref[...] = _sum_pieces(p_ref)

    return pl.pallas_call(
        body, name=name, grid=(rows // tr,),
        in_specs=[pl.BlockSpec((N_DEV, tr, cols), lambda i: (0, i, 0))],
        out_specs=pl.BlockSpec((tr, cols), lambda i: (i, 0)),
        out_shape=jax.ShapeDtypeStruct((rows, cols), F32),
        compiler_params=_params(("parallel",)),
    )(pieces)


def _adamw_native(g, w, m, v, *, name):
    rows, cols = w.shape
    tr = _tile(rows, (256, 128))

    def body(g_ref, w_ref, m_ref, v_ref, d_ref, nm_ref, nv_ref):
        d_ref[...], nm_ref[...], nv_ref[...] = _adam_update(g_ref[...], w_ref[...], m_ref[...], v_ref[...])

    spec = pl.BlockSpec((tr, cols), lambda i: (i, 0))
    shape = jax.ShapeDtypeStruct((rows, cols), F32)
    return pl.pallas_call(
        body, name=name, grid=(rows // tr,), in_specs=[spec, spec, spec, spec],
        out_specs=[spec, spec, spec], out_shape=[shape, shape, shape],
        compiler_params=_params(("parallel",)),
    )(g, w, m, v)


MESH = pl.DeviceIdType.MESH
ANY = pl.BlockSpec(memory_space=pl.ANY)


def _all_gather(shard, *, name):
    rows, lanes = shard.shape

    def body(x_ref, out_ref, send_sems, recv_sems, local_sem):
        x, y, c = lax.axis_index("x"), lax.axis_index("y"), lax.axis_index("c")
        me, sibling = (x, y, c), (x, y, 1 - c)
        chips = [(1 - x, y), (x, 1 - y), (1 - x, 1 - y)]

        def block(px, py, pc):
            return out_ref.at[4 * px + 2 * py + pc]

        def copy(k, blk, to, src=None):
            return pltpu.make_async_remote_copy(
                src_ref=block(*blk) if src is None else src, dst_ref=block(*blk),
                send_sem=send_sems.at[k], recv_sem=recv_sems.at[k], device_id=to, device_id_type=MESH)

        mine = pltpu.make_async_copy(x_ref, block(*me), local_sem)
        mine.start()
        first = [copy(0, me, sibling, src=x_ref)]
        first += [copy(1 + j, me, (*chip, c), src=x_ref) for j, chip in enumerate(chips)]
        for cp in first:
            cp.start()
        passed = [copy(4 + j, (*chip, c), sibling) for j, chip in enumerate(chips)]
        for j, chip in enumerate(chips):
            copy(1 + j, (*chip, c), me).wait_recv()
            passed[j].start()
        copy(0, sibling, me).wait_recv()
        for j, chip in enumerate(chips):
            copy(4 + j, (*chip, 1 - c), me).wait_recv()
        for cp in first + passed:
            cp.wait_send()
        mine.wait()

    return pl.pallas_call(
        body, name=name, out_shape=jax.ShapeDtypeStruct((N_DEV, rows, lanes), shard.dtype),
        in_specs=[ANY], out_specs=ANY,
        scratch_shapes=[pltpu.SemaphoreType.DMA((7,)), pltpu.SemaphoreType.DMA((7,)), pltpu.SemaphoreType.DMA(())],
    )(shard)


def _peer_copies(kind, src_ref, out_ref, send_sems, recv_sems, local_sem):
    x, y, c = lax.axis_index("x"), lax.axis_index("y"), lax.axis_index("c")
    me = 4 * x + 2 * y + c

    def src(idx):
        return src_ref.at[idx] if kind == "exchange" else src_ref

    mine = pltpu.make_async_copy(src(me), out_ref.at[me], local_sem)
    copies = []
    for r in range(1, N_DEV):
        px = 1 - x if r & 4 else x
        py = 1 - y if r & 2 else y
        pc = 1 - c if r & 1 else c
        copies.append(pltpu.make_async_remote_copy(
            src_ref=src(4 * px + 2 * py + pc), dst_ref=out_ref.at[me],
            send_sem=send_sems.at[r - 1], recv_sem=recv_sems.at[r - 1],
            device_id=(px, py, pc), device_id_type=MESH))
    return mine, copies


PEER_SEMS = [pltpu.SemaphoreType.DMA((7,)), pltpu.SemaphoreType.DMA((7,)), pltpu.SemaphoreType.DMA(())]


def _exchange(pieces, *, name):
    def body(g_ref, out_ref, send_sems, recv_sems, local_sem):
        mine, copies = _peer_copies("exchange", g_ref, out_ref, send_sems, recv_sems, local_sem)
        mine.start()
        for cp in copies:
            cp.start()
        for cp in copies:
            cp.wait()
        mine.wait()

    return pl.pallas_call(
        body, name=name, out_shape=jax.ShapeDtypeStruct(pieces.shape, pieces.dtype),
        in_specs=[ANY], out_specs=ANY, scratch_shapes=list(PEER_SEMS),
    )(pieces)


def _add_rider(rider, in_specs, args, out_specs, out_shape):
    if rider is None:
        return []
    _, arr = rider
    in_specs.append(ANY)
    args.append(arr)
    out_specs.append(ANY)
    out_shape.append(jax.ShapeDtypeStruct((N_DEV,) + arr.shape[-2:], arr.dtype))
    return list(PEER_SEMS)


def _split_rider(refs, rider, n_in, n_out):
    if rider is None:
        return refs, None
    refs = list(refs)
    rin = refs.pop(n_in)
    rout = refs.pop(n_in + n_out)
    return refs[:-3], (rin, rout, *refs[-3:])


def _ride_start(rider, ride_refs, first):
    if rider is None:
        return

    @pl.when(first)
    def _():
        mine, copies = _peer_copies(rider[0], *ride_refs)
        mine.start()
        for cp in copies:
            cp.start()


def _ride_wait(rider, ride_refs, last):
    if rider is None:
        return

    @pl.when(last)
    def _():
        mine, copies = _peer_copies(rider[0], *ride_refs)
        for cp in copies:
            cp.wait()
        mine.wait()


def _gathered_cols(blocks, kdim):
    n = blocks.shape[1] * WIDE // kdim
    return blocks.reshape(N_DEV, kdim, n).transpose(1, 0, 2).reshape(kdim, N_DEV * n)


def _scatter_cols(dw):
    kdim, n8 = dw.shape
    n = n8 // N_DEV
    return dw.reshape(kdim, N_DEV, n).transpose(1, 0, 2).reshape(N_DEV, kdim * n // WIDE, WIDE)


def _pad_rows(a, rows):
    pad = [(0, 0)] * a.ndim
    pad[-2] = (0, rows - a.shape[-2])
    return jnp.pad(a, pad)


def _layer0_in_weight_t(wt):
    cq, ckv, kpe = wt[0:256], wt[256:384], wt[384:416]
    q_s, k_s, v_s, gate = wt[416:928], wt[928:1056], wt[1056:1184], wt[1184:2208]
    z = jnp.zeros((64, wt.shape[1]), wt.dtype)
    return jnp.concatenate([gate, cq, ckv, z, kpe, z[:32], q_s, k_s, v_s], axis=0)


def _layer0_in_grad_t(dwt):
    gate, cq, ckv, kpe = dwt[0:1024], dwt[1024:1280], dwt[1280:1408], dwt[1472:1504]
    q_s, k_s, v_s = dwt[1536:2048], dwt[2048:2176], dwt[2176:2304]
    return jnp.concatenate([cq, ckv, kpe, q_s, k_s, v_s, gate], axis=0)


def _layer1_in_weight_t(wt):
    main = jnp.concatenate([wt[:3 * D_MODEL], wt[3 * D_MODEL + FOX_HEADS:]], axis=0)
    return main, _pad_rows(wt[3 * D_MODEL:3 * D_MODEL + FOX_HEADS], LANES)


def _layer1_in_grad_t(d_main, d_wft):
    return jnp.concatenate([d_main[:3 * D_MODEL], d_wft[:FOX_HEADS], d_main[3 * D_MODEL:]], axis=0)


def _q_up_weight(w):
    return jnp.pad(w.reshape(MLA_Q_RANK, MLA_HEADS, 96), ((0, 0), (0, 0), (0, 32))).reshape(MLA_Q_RANK, MLA_HEADS * LANES)


def _q_up_grad(dwp):
    return dwp.reshape(MLA_Q_RANK, MLA_HEADS, LANES)[:, :, :96].reshape(MLA_Q_RANK, MLA_HEADS * 96)


def _kv_up_weight(w):
    w4 = w.reshape(MLA_KV_RANK, MLA_HEADS, 2, 64)
    kp = jnp.pad(w4[:, :, 0, :], ((0, 0), (0, 0), (0, 64))).reshape(MLA_KV_RANK, MLA_HEADS * LANES)
    vp = w4[:, :, 1, :].reshape(MLA_KV_RANK, MLA_HEADS * 64)
    return jnp.concatenate([kp, vp], axis=1)


def _kv_up_grad(dwp):
    dk = dwp[:, :MLA_HEADS * LANES].reshape(MLA_KV_RANK, MLA_HEADS, LANES)[:, :, :64]
    dv = dwp[:, MLA_HEADS * LANES:].reshape(MLA_KV_RANK, MLA_HEADS, 64)
    return jnp.stack([dk, dv], axis=2).reshape(MLA_KV_RANK, MLA_HEADS * LANES)


def _pad_lanes(a):
    return jnp.pad(a, ((0, 0), (0, LANES - a.shape[1])))


def _small_pack(g_in, g_final, g_q_a, g_kv_a, sinks, b_f, loss):
    rows = [g_in.reshape(8, LANES), g_final.reshape(8, LANES), g_q_a.reshape(2, LANES), g_kv_a.reshape(1, LANES),
            _pad_lanes(sinks.reshape(1, -1)), _pad_lanes(b_f.reshape(1, -1)), _pad_lanes(loss.reshape(1, 1)),
            jnp.zeros((2, LANES), F32)]
    return jnp.concatenate(rows, axis=0)


def _small_unpack(a):
    return (a[0:8].reshape(1, D_MODEL), a[8:16].reshape(D_MODEL), a[16:18].reshape(1, MLA_Q_RANK),
            a[18:19].reshape(1, MLA_KV_RANK), a[19:20, :SWA_HEADS], a[20:21, :FOX_HEADS], a[21, 0])


def _local_step(x, positions, target, e_g_in, w0t, e_g_q_a, wq, e_g_kv_a, wkv, e_sinks,
                late, o_b_f, g_final, scatter1=None):
    s = x.shape[0]
    att_t = min(ATT_T, s)
    nb = s // att_t
    mla_scale = (MLA_NOPE + MLA_ROPE) ** -0.5
    fox_scale = FOX_DIM ** -0.5
    n0a = Z0A_UNITS * LANES

    inv_freq = 1.0 / (ROPE_THETA ** (jnp.arange(0, MLA_ROPE, 2, dtype=F32) / MLA_ROPE))
    ang = positions.astype(F32)[:, None] * inv_freq
    cos, sin = jnp.cos(ang), jnp.sin(ang)
    ones, zeros = jnp.ones((s, 64), F32), jnp.zeros((s, 64), F32)
    cos_t = jnp.concatenate([ones, cos, cos, ones[:, :32]], axis=1)
    sin_t = jnp.concatenate([zeros, -sin, sin, zeros[:, :32]], axis=1)

    h0 = _rmsnorm_fwd(x, e_g_in, width=D_MODEL, col_blk=0, name="l0_norm")
    z0a = _matmul(h0, w0t, tb=True, b_rows=(0, n0a), name="l0_in_a")
    z0b = _matmul(h0, w0t, tb=True, b_rows=(n0a, Z0B_UNITS * LANES), name="l0_in_b", out_dtype=BF16)
    cqn = _rmsnorm_fwd(z0a, e_g_q_a, width=MLA_Q_RANK, col_blk=4, name="l0_q_norm")
    ckvn = _rmsnorm_fwd(z0a, e_g_kv_a, width=MLA_KV_RANK, col_blk=10, name="l0_kv_norm")
    qp = _matmul(cqn, wq, name="l0_q_up")
    kvp = _matmul(ckvn, wkv, name="l0_kv_up", out_dtype=BF16)
    qm, km = _rope_fwd(qp, kvp, z0a, cos_t, sin_t, name="l0_rope")
    gathers = len(late) == 2
    res = _flash_fwd(qm, km, kvp, None, n_pairs=MLA_HEADS // 2, hw=LANES, q_off=0, k_off=0, v_off=MLA_HEADS,
                     scale=mla_scale, name="l0_mla_fwd", rider=("gather", late[0]) if gathers else None)
    o_mla, lse_mla = res[0], res[1].reshape(MLA_HEADS // 2, 2, s, 1)
    wo0, o_g_in, w1t, wft, wo1 = late[1](res[2]) if gathers else late
    o_swa, lse_swa = _swa_fwd(z0b, e_sinks, name="l0_swa_fwd")
    og0 = _gate_fwd([o_mla, o_swa], z0a, name="l0_gate")
    x1 = _matmul(og0, wo0, add=x, name="l0_out")

    h1 = _rmsnorm_fwd(x1, o_g_in, width=D_MODEL, col_blk=0, name="l1_norm")
    z1 = _matmul(h1, w1t, tb=True, b_rows=(0, 3 * D_MODEL), name="l1_in_qkv", out_dtype=BF16)
    gate1 = _matmul(h1, w1t, tb=True, b_rows=(3 * D_MODEL, D_MODEL), name="l1_in_gate")
    zf = _matmul(h1, wft, tb=True, name="l1_in_f")
    bf = _pad_lanes(o_b_f)
    log_cum = _logf_fwd(zf, bf, name="l1_logf")
    bias2 = (-LOG2E * log_cum[:, :FOX_HEADS]).T
    bias = bias2.reshape(FOX_HEADS // 2, 2, nb, 1, att_t)
    o_fox, lse_fox = _flash_fwd(z1, z1, z1, bias2.reshape(FOX_HEADS // 2, 2, s, 1), n_pairs=FOX_HEADS // 2, hw=64,
                                q_off=0, k_off=8, v_off=16, scale=fox_scale, name="l1_fox_fwd")
    lse_fox = lse_fox.reshape(FOX_HEADS // 2, 2, s, 1)
    og1 = _gate_fwd([o_fox], gate1, name="l1_gate")
    x2 = _matmul(og1, wo1, add=x1, name="l1_out")

    dx2, loss_part, d_g_final = _loss_head(x2, g_final.reshape(1, D_MODEL), target, name="loss_head")

    d_wo1 = _matmul(og1, dx2, ta=True, name="l1_out_dw")
    d_og1 = _matmul(dx2, wo1, tb=True, name="l1_out_dx")
    do_fox, d_gate1 = _gate_bwd(d_og1, [o_fox], gate1, name="l1_gate_bwd")
    dq1, dk1, dv1, dbias, drow = _flash_bwd(z1, z1, z1, do_fox, o_fox, lse_fox, bias, n_pairs=FOX_HEADS // 2, hw=64,
                                            q_off=0, k_off=8, v_off=16, scale=fox_scale, qk_dtype=BF16,
                                            name="l1_fox_bwd")
    d_log_cum = (drow.reshape(FOX_HEADS, s) - dbias.reshape(FOX_HEADS, s)).T
    d_log_cum = jnp.pad(d_log_cum, ((0, 0), (0, LANES - FOX_HEADS)))
    d_zf, d_bf = _logf_bwd(d_log_cum, zf, bf, name="l1_logf_bwd")
    dz1 = jnp.concatenate([dq1, dk1, dv1, d_gate1], axis=1)
    d_w1t = _matmul(dz1, h1, ta=True, name="l1_in_dw")
    d_wft = _matmul(d_zf, h1, ta=True, name="l1_in_f_dw")
    dh1 = _matmul(dz1, w1t, name="l1_in_dx")
    dh1 = _matmul(d_zf, wft, add=dh1, name="l1_in_f_dx")
    dx1, d_o_g_in = _rmsnorm_bwd(x1, o_g_in, dh1, width=D_MODEL, col_blk=0, add=dx2, name="l1_norm_bwd")

    d_wo0 = _matmul(og0, dx1, ta=True, name="l0_out_dw")
    d_og0 = _matmul(dx1, wo0, tb=True, name="l0_out_dx")
    do_mla, do_swa, d_gate0 = _gate_bwd(d_og0, [o_mla, o_swa], z0a, name="l0_gate_bwd")
    dq_s, dkt_s, dvt_s, d_sinks = _swa_bwd(z0b, e_sinks, do_swa, o_swa, lse_swa, name="l0_swa_bwd")
    dk_s = dkt_s.transpose(0, 2, 1).reshape(s, LANES)
    dv_s = dvt_s.transpose(0, 2, 1).reshape(s, LANES)
    rider = None
    if scatter1 is not None:
        rider = ("exchange", scatter1(dict(w1t=d_w1t, wft=d_wft, wo1=d_wo1, o_g_in=d_o_g_in, wo0=d_wo0)))
    res = _flash_bwd(qm, km, kvp, do_mla, o_mla, lse_mla, None, n_pairs=MLA_HEADS // 2, hw=LANES, q_off=0, k_off=0,
                     v_off=MLA_HEADS, scale=mla_scale, qk_dtype=F32, name="l0_mla_bwd", rider=rider)
    dqm, dkm, dvm = res[0], res[1], res[2]
    recv1 = res[3] if rider is not None else None
    d_qp, d_kvp, d_kpe = _rope_bwd(dqm, dkm, dvm, cos_t, sin_t, name="l0_rope_bwd")
    d_wq = _matmul(cqn, d_qp, ta=True, name="l0_q_up_dw")
    d_cqn = _matmul(d_qp, wq, tb=True, name="l0_q_up_dx")
    d_wkv = _matmul(ckvn, d_kvp, ta=True, name="l0_kv_up_dw")
    d_ckvn = _matmul(d_kvp, wkv, tb=True, name="l0_kv_up_dx")
    d_cq, d_g_q_a = _rmsnorm_bwd(z0a, e_g_q_a, d_cqn, width=MLA_Q_RANK, col_blk=4, out_dtype=BF16, name="l0_q_norm_bwd")
    d_ckv, d_g_kv_a = _rmsnorm_bwd(z0a, e_g_kv_a, d_ckvn, width=MLA_KV_RANK, col_blk=10, out_dtype=BF16,
                                   name="l0_kv_norm_bwd")
    dz0 = jnp.concatenate([d_gate0, d_cq, d_ckv, d_kpe, dq_s.astype(BF16), dk_s.astype(BF16), dv_s.astype(BF16)], axis=1)
    d_w0t = _matmul(dz0, h0, ta=True, name="l0_in_dw")
    dh0 = _matmul(dz0, w0t, name="l0_in_dx")
    grad_x, d_e_g_in = _rmsnorm_bwd(x, e_g_in, dh0, width=D_MODEL, col_blk=0, add=dx1, name="l0_norm_bwd")

    return dict(recv1=recv1, loss=loss_part[0, 0], grad_x=grad_x, e_g_in=d_e_g_in, w0t=d_w0t, e_g_q_a=d_g_q_a, wq=d_wq,
                e_g_kv_a=d_g_kv_a, wkv=d_wkv, e_sinks=d_sinks[:, 0].reshape(1, SWA_HEADS), wo0=d_wo0,
                o_g_in=d_o_g_in, w1t=d_w1t, wft=d_wft, o_b_f=d_bf[:, :FOX_HEADS], wo1=d_wo1, g_final=d_g_final.reshape(D_MODEL))


def _wide(a, rows):
    flat = a.reshape(-1)
    return jnp.pad(flat, (0, rows * WIDE - flat.shape[0])).reshape(rows, WIDE)


def _rows_b0(w_q, w_kv):
    return jnp.concatenate([_wide(w_q, 32), _wide(w_kv, 16)], axis=0)


def _unflat_b0(f):
    return f[0:24].reshape(1, MLA_Q_RANK, 96), f[32:48].reshape(1, MLA_KV_RANK, 128)


def _rows_b1(o_w_out, e_w_out, g_in):
    return jnp.concatenate([o_w_out, e_w_out, _wide(g_in, 16)], axis=0)


def _unflat_b1(f):
    return f[0:128][None], f[128:256][None], f[256:257, :LANES]


def kernel(x, positions, e_g_in, e_w_in, e_g_q_a, e_w_q_up, e_g_kv_a, e_w_kv_up, e_sinks, e_w_out, o_g_in, o_w_in, o_b_f, o_w_out, g_final, loss_target, m_e_g_in, m_e_w_in, m_e_g_q_a, m_e_w_q_up, m_e_g_kv_a, m_e_w_kv_up, m_e_sinks, m_e_w_out, m_o_g_in, m_o_w_in, m_o_b_f, m_o_w_out, m_g_final, v_e_g_in, v_e_w_in, v_e_g_q_a, v_e_w_q_up, v_e_g_kv_a, v_e_w_kv_up, v_e_sinks, v_e_w_out, v_o_g_in, v_o_w_in, v_o_b_f, v_o_w_out, v_g_final):
    def bf(a):
        return a.astype(BF16)

    shard0 = jnp.concatenate([_pad_rows(bf(e_w_in[0]).T, RA0), _rows_b0(bf(e_w_q_up[0]), bf(e_w_kv_up[0]))], axis=0)
    gath0 = _all_gather(shard0, name="weights0_all_gather")
    w0t = _layer0_in_weight_t(gath0[:, :N_E_IN].reshape(N_DEV * N_E_IN, WIDE))
    wq = _q_up_weight(_gathered_cols(gath0[:, RA0:RA0 + 24], MLA_Q_RANK))
    wkv = _kv_up_weight(_gathered_cols(gath0[:, RA0 + 32:RA0 + 48], MLA_KV_RANK))

    g_bits = lax.bitcast_convert_type(o_g_in.reshape(LANES), BF16)
    shard1 = jnp.concatenate([_pad_rows(bf(o_w_in[0]).T, RA1), _rows_b1(bf(o_w_out[0]), bf(e_w_out[0]), g_bits)], axis=0)

    def unpack1(gath1):
        w1t, wft = _layer1_in_weight_t(gath1[:, :N_O_IN].reshape(N_DEV * N_O_IN, WIDE))
        wo1 = gath1[:, RA1:RA1 + 128].reshape(D_MODEL, D_MODEL)
        wo0 = gath1[:, RA1 + 128:RA1 + 256].reshape(D_MODEL, D_MODEL)
        bits = gath1[:, RA1 + 256, :2 * LANES].reshape(N_DEV, LANES, 2)
        return wo0, lax.bitcast_convert_type(bits, F32).reshape(1, D_MODEL), w1t, wft, wo1

    def scatter1(g):
        d_in_t = _layer1_in_grad_t(g["w1t"], g["wft"]).reshape(N_DEV, N_O_IN, WIDE)
        d_o_g = jnp.pad(g["o_g_in"].reshape(N_DEV, 1, LANES), ((0, 0), (0, 15), (0, WIDE - LANES)))
        return jnp.concatenate([_pad_rows(d_in_t, RA1), g["wo1"].reshape(N_DEV, 128, WIDE),
                                g["wo0"].reshape(N_DEV, 128, WIDE), d_o_g], axis=1).astype(BF16)

    gr = _local_step(x[0], positions[0], loss_target[0], e_g_in, w0t, e_g_q_a, wq, e_g_kv_a, wkv, e_sinks,
                     (shard1, unpack1), o_b_f, g_final, scatter1=scatter1)

    pieces0 = jnp.concatenate([
        _pad_rows(_layer0_in_grad_t(gr["w0t"]).reshape(N_DEV, N_E_IN, WIDE), RA0),
        _pad_rows(_scatter_cols(_q_up_grad(gr["wq"])), 32), _scatter_cols(_kv_up_grad(gr["wkv"]))], axis=1)
    recv0 = _exchange(pieces0.astype(BF16), name="grads0_exchange")

    def in_projection(recv, ra, n, w, m, v, name):
        g = _sum8(recv, ra, name=name + "_grad_sum")[:n].T
        d, nm, nv = _adamw_native(g, w[0], m[0], v[0], name=name + "_adamw")
        return g[None], d[None], nm[None], nv[None]

    e_in = in_projection(recv0, RA0, N_E_IN, e_w_in, m_e_w_in, v_e_w_in, "e_w_in")
    o_in = in_projection(gr["recv1"], RA1, N_O_IN, o_w_in, m_o_w_in, v_o_w_in, "o_w_in")
    b0 = _adamw(recv0[:, RA0:], _rows_b0(e_w_q_up[0], e_w_kv_up[0]), _rows_b0(m_e_w_q_up[0], m_e_w_kv_up[0]),
                _rows_b0(v_e_w_q_up[0], v_e_w_kv_up[0]), name="adamw_early")
    b1 = _adamw(gr["recv1"][:, RA1:], _rows_b1(o_w_out[0], e_w_out[0], o_g_in),
                _rows_b1(m_o_w_out[0], m_e_w_out[0], m_o_g_in), _rows_b1(v_o_w_out[0], v_e_w_out[0], v_o_g_in),
                name="adamw_late")

    def sharded(k):
        q_up, kv_up = _unflat_b0(b0[k])
        o_out, e_out, o_g = _unflat_b1(b1[k])
        return e_in[k], q_up, kv_up, e_out, o_in[k], o_out, o_g

    g_sh, d_sh, m_sh, v_sh = [sharded(k) for k in range(4)]

    small = _small_pack(gr["e_g_in"], gr["g_final"], gr["e_g_q_a"], gr["e_g_kv_a"], gr["e_sinks"], gr["o_b_f"], gr["loss"])
    small_all = _all_gather(small, name="small_all_gather")
    zero = jnp.zeros((), F32)
    w_small = _small_pack(e_g_in, g_final, e_g_q_a, e_g_kv_a, e_sinks, o_b_f, zero)
    m_small = _small_pack(m_e_g_in, m_g_final, m_e_g_q_a, m_e_g_kv_a, m_e_sinks, m_o_b_f, zero)
    v_small = _small_pack(v_e_g_in, v_g_final, v_e_g_q_a, v_e_g_kv_a, v_e_sinks, v_o_b_f, zero)
    smalls = _adamw(small_all, w_small, m_small, v_small, name="adamw_replicated")
    g_sm, d_sm, m_sm, v_sm = [_small_unpack(a) for a in smalls]
    loss = g_sm[6]

    def leaves(sh, sm):
        return (sm[0], sh[0], sm[2], sh[1], sm[3], sh[2], sm[4], sh[3], sh[6], sh[4], sm[5], sh[5], sm[1])

    return (loss, gr["grad_x"][None], *leaves(g_sh, g_sm), *leaves(d_sh, d_sm), *leaves(m_sh, m_sm), *leaves(v_sh, v_sm))
```

```python
import functools

import jax
import jax.numpy as jnp
from jax import lax
from jax.experimental import pallas as pl
from jax.experimental.pallas import tpu as pltpu

F32 = jnp.float32
BF16 = jnp.bfloat16
NEG_INF = float("-inf")

N_DEV = 8
LANES = 128
D_MODEL = 1024
EPS = 1e-6
ROPE_THETA = 10000.0
MLA_HEADS = 8
MLA_Q_RANK = 256
MLA_KV_RANK = 128
MLA_NOPE = 64
MLA_ROPE = 32
MLA_V = 64
SWA_HEADS = 8
SWA_KV_HEADS = 2
SWA_DIM = 64
WINDOW = 128
FOX_HEADS = 16
FOX_DIM = 64

ADAM_LR = 0.001
ADAM_B1 = 0.9
ADAM_B2 = 0.999
ADAM_EPS = 1e-08
ADAM_WD = 0.01
ADAM_STEP = 10

ATT_T = 512
VMEM_LIMIT = 56 * 1024 * 1024

Z0A_UNITS = 12
Z0B_UNITS = 6

WIDE = 1024
N_E_IN = 276
N_O_IN = 514
RA0 = 288
RB0 = 32 + 16
RA1 = 528
RB1 = 128 + 128 + 16
SMALL_ROWS = 24


def _tile(n, cands):
    for c in cands:
        if n % c == 0:
            return c
    raise ValueError(f"no tile for {n}")


def _params(sem, vmem=None):
    return pltpu.CompilerParams(dimension_semantics=sem, vmem_limit_bytes=vmem)


def _matmul(a, b, *, name, ta=False, tb=False, add=None, out_dtype=F32, b_rows=None):
    if ta:
        kdim, m = a.shape
    else:
        m, kdim = a.shape
    if tb:
        n, kb = b.shape
    else:
        kb, n = b.shape
    assert kdim == kb, (a.shape, b.shape)
    b_start = 0
    if b_rows is not None:
        assert tb
        b_start, n = b_rows
    tm = _tile(m, (512, 256, 128))
    tn = _tile(n, (768, 512, 384, 256, 128))
    assert b_start % tn == 0, (b_start, tn)
    b_off = b_start // tn
    dims = (((0 if ta else 1,), (1 if tb else 0,)), ((), ()))

    def body(*refs):
        if add is None:
            a_ref, b_ref, o_ref = refs
            add_ref = None
        else:
            a_ref, b_ref, add_ref, o_ref = refs
        r = lax.dot_general(a_ref[...].astype(BF16), b_ref[...].astype(BF16), dims, preferred_element_type=F32)
        if add_ref is not None:
            r = r + add_ref[...]
        o_ref[...] = r.astype(out_dtype)

    a_spec = pl.BlockSpec((kdim, tm), lambda i, j: (0, i)) if ta else pl.BlockSpec((tm, kdim), lambda i, j: (i, 0))
    b_spec = pl.BlockSpec((tn, kdim), lambda i, j: (j + b_off, 0)) if tb else pl.BlockSpec((kdim, tn), lambda i, j: (0, j))
    in_specs = [a_spec, b_spec]
    args = [a, b]
    if add is not None:
        in_specs.append(pl.BlockSpec((tm, tn), lambda i, j: (i, j)))
        args.append(add)
    return pl.pallas_call(
        body, name=name, grid=(m // tm, n // tn),
        in_specs=in_specs, out_specs=pl.BlockSpec((tm, tn), lambda i, j: (i, j)),
        out_shape=jax.ShapeDtypeStruct((m, n), out_dtype),
        compiler_params=_params(("parallel", "parallel"), VMEM_LIMIT),
    )(*args)


def _rmsnorm_fwd(x, g, *, width, col_blk, name):
    s = x.shape[0]
    tm = _tile(s, (256, 128))

    def body(x_ref, g_ref, y_ref):
        xf = x_ref[...].astype(F32)
        r = lax.rsqrt(jnp.mean(xf * xf, axis=-1, keepdims=True) + EPS)
        y_ref[...] = ((xf * r) * g_ref[...]).astype(BF16)

    return pl.pallas_call(
        body, name=name, grid=(s // tm,),
        in_specs=[pl.BlockSpec((tm, width), lambda i: (i, col_blk)), pl.BlockSpec((1, width), lambda i: (0, 0))],
        out_specs=pl.BlockSpec((tm, width), lambda i: (i, 0)),
        out_shape=jax.ShapeDtypeStruct((s, width), BF16),
        compiler_params=_params(("parallel",)),
    )(x, g)


def _rmsnorm_bwd(x, g, dy, *, width, col_blk, name, add=None, out_dtype=F32):
    s = x.shape[0]
    tm = _tile(s, (256, 128))

    def body(*refs):
        if add is None:
            x_ref, g_ref, dy_ref, dx_ref, dg_ref = refs
            add_ref = None
        else:
            x_ref, g_ref, dy_ref, add_ref, dx_ref, dg_ref = refs
        i = pl.program_id(0)
        xf = x_ref[...].astype(F32)
        r = lax.rsqrt(jnp.mean(xf * xf, axis=-1, keepdims=True) + EPS)
        xh = xf * r
        dyf = dy_ref[...].astype(F32)

        @pl.when(i == 0)
        def _():
            dg_ref[...] = jnp.zeros_like(dg_ref)

        dg_ref[...] += jnp.sum(dyf * xh, axis=0, keepdims=True)
        dxh = dyf * g_ref[...]
        dx = r * (dxh - xh * jnp.mean(dxh * xh, axis=-1, keepdims=True))
        if add_ref is not None:
            dx = dx + add_ref[...]
        dx_ref[...] = dx.astype(out_dtype)

    in_specs = [pl.BlockSpec((tm, width), lambda i: (i, col_blk)), pl.BlockSpec((1, width), lambda i: (0, 0)),
                pl.BlockSpec((tm, width), lambda i: (i, 0))]
    args = [x, g, dy]
    if add is not None:
        in_specs.append(pl.BlockSpec((tm, width), lambda i: (i, 0)))
        args.append(add)
    return pl.pallas_call(
        body, name=name, grid=(s // tm,),
        in_specs=in_specs,
        out_specs=[pl.BlockSpec((tm, width), lambda i: (i, 0)), pl.BlockSpec((1, width), lambda i: (0, 0))],
        out_shape=[jax.ShapeDtypeStruct((s, width), out_dtype), jax.ShapeDtypeStruct((1, width), F32)],
        compiler_params=_params(("arbitrary",)),
    )(*args)


def _sigmoid(x):
    return 1.0 / (1.0 + jnp.exp(-x))


def _gate_fwd(o_parts, gate, *, name):
    s = gate.shape[0]
    tm = _tile(s, (256, 128))
    n_o = len(o_parts)

    def body(*refs):
        o_refs, g_ref, y_ref = refs[:n_o], refs[n_o], refs[n_o + 1]
        o = o_refs[0][...] if n_o == 1 else jnp.concatenate([r[...] for r in o_refs], axis=1)
        gt = g_ref[...]
        y_ref[...] = (o * (gt * _sigmoid(gt))).astype(BF16)

    in_specs = [pl.BlockSpec((tm, o.shape[1]), lambda i: (i, 0)) for o in o_parts]
    in_specs.append(pl.BlockSpec((tm, D_MODEL), lambda i: (i, 0)))
    return pl.pallas_call(
        body, name=name, grid=(s // tm,), in_specs=in_specs,
        out_specs=pl.BlockSpec((tm, D_MODEL), lambda i: (i, 0)),
        out_shape=jax.ShapeDtypeStruct((s, D_MODEL), BF16),
        compiler_params=_params(("parallel",)),
    )(*o_parts, gate)


def _gate_bwd(d_og, o_parts, gate, *, name):
    s = gate.shape[0]
    tm = _tile(s, (256, 128))
    n_o = len(o_parts)
    widths = [o.shape[1] for o in o_parts]

    def body(*refs):
        d_ref, o_refs, g_ref = refs[0], refs[1:1 + n_o], refs[1 + n_o]
        do_refs, dg_ref = refs[2 + n_o:2 + 2 * n_o], refs[2 + 2 * n_o]
        d = d_ref[...]
        gt = g_ref[...]
        sg = _sigmoid(gt)
        silu = gt * sg
        dsilu = sg * (1.0 + gt * (1.0 - sg))
        o = o_refs[0][...] if n_o == 1 else jnp.concatenate([r[...] for r in o_refs], axis=1)
        dg_ref[...] = (d * o * dsilu).astype(BF16)
        do = d * silu
        off = 0
        for r, w in zip(do_refs, widths):
            r[...] = do[:, off:off + w]
            off += w

    in_specs = [pl.BlockSpec((tm, D_MODEL), lambda i: (i, 0))]
    in_specs += [pl.BlockSpec((tm, w), lambda i: (i, 0)) for w in widths]
    in_specs.append(pl.BlockSpec((tm, D_MODEL), lambda i: (i, 0)))
    out_specs = [pl.BlockSpec((tm, w), lambda i: (i, 0)) for w in widths]
    out_specs.append(pl.BlockSpec((tm, D_MODEL), lambda i: (i, 0)))
    out_shape = [jax.ShapeDtypeStruct((s, w), F32) for w in widths]
    out_shape.append(jax.ShapeDtypeStruct((s, D_MODEL), BF16))
    return pl.pallas_call(
        body, name=name, grid=(s // tm,), in_specs=in_specs, out_specs=out_specs, out_shape=out_shape,
        compiler_params=_params(("parallel",)),
    )(d_og, *o_parts, gate)


def _rot_half(x):
    lane = lax.broadcasted_iota(jnp.int32, x.shape, 1)
    return jnp.where(lane < 80, pltpu.roll(x, LANES - 16, axis=1), pltpu.roll(x, 16, axis=1))


def _rot_half_t(g):
    lane = lax.broadcasted_iota(jnp.int32, g.shape, 1)
    lo = (lane >= MLA_NOPE) & (lane < MLA_NOPE + MLA_ROPE // 2)
    hi = (lane >= MLA_NOPE + MLA_ROPE // 2) & (lane < MLA_NOPE + MLA_ROPE)
    return jnp.where(lo, pltpu.roll(g, LANES - 16, axis=1), jnp.where(hi, pltpu.roll(g, 16, axis=1), 0.0))


def _rope_fwd(qp, kvp, z0a, cos_t, sin_t, *, name):
    s = qp.shape[0]
    tm = _tile(s, (256, 128))
    hw = MLA_HEADS * LANES

    def body(q_ref, k_ref, kpe_ref, c_ref, s_ref, qm_ref, km_ref):
        c = c_ref[...]
        sn = s_ref[...]
        kpe = kpe_ref[...]
        kpe_r = (kpe * c + _rot_half(kpe) * sn).astype(BF16)
        lane = lax.broadcasted_iota(jnp.int32, kpe.shape, 1)
        for h in range(MLA_HEADS):
            sl = slice(h * LANES, (h + 1) * LANES)
            qh = q_ref[:, sl]
            qm_ref[:, sl] = (qh * c + _rot_half(qh) * sn).astype(BF16)
            km_ref[:, sl] = jnp.where(lane < MLA_NOPE, k_ref[:, sl], kpe_r)

    return pl.pallas_call(
        body, name=name, grid=(s // tm,),
        in_specs=[pl.BlockSpec((tm, hw), lambda i: (i, 0)), pl.BlockSpec((tm, hw), lambda i: (i, 0)),
                  pl.BlockSpec((tm, LANES), lambda i: (i, 11)),
                  pl.BlockSpec((tm, LANES), lambda i: (i, 0)), pl.BlockSpec((tm, LANES), lambda i: (i, 0))],
        out_specs=[pl.BlockSpec((tm, hw), lambda i: (i, 0)), pl.BlockSpec((tm, hw), lambda i: (i, 0))],
        out_shape=[jax.ShapeDtypeStruct((s, hw), BF16), jax.ShapeDtypeStruct((s, hw), BF16)],
        compiler_params=_params(("parallel",)),
    )(qp, kvp, z0a, cos_t, sin_t)


def _rope_bwd(dqm, dkm, dvm, cos_t, sin_t, *, name):
    s = dqm.shape[0]
    tm = _tile(s, (256, 128))
    hw = MLA_HEADS * LANES
    vw = MLA_HEADS * MLA_V

    def body(dq_ref, dk_ref, dv_ref, c_ref, s_ref, dqp_ref, dkv_ref, dkpe_ref):
        c = c_ref[...]
        sn = s_ref[...]
        ksum = jnp.zeros((tm, LANES), F32)
        for h in range(MLA_HEADS):
            sl = slice(h * LANES, (h + 1) * LANES)
            dq = dq_ref[:, sl]
            dqp_ref[:, sl] = (dq * c + _rot_half_t(dq * sn)).astype(BF16)
            dk = dk_ref[:, sl]
            dkv_ref[:, sl] = dk.astype(BF16)
            ksum = ksum + dk
        dkv_ref[:, hw:] = dv_ref[...]
        lane = lax.broadcasted_iota(jnp.int32, ksum.shape, 1)
        dkpe = ksum * c + _rot_half_t(ksum * sn)
        dkpe_ref[...] = jnp.where((lane >= MLA_NOPE) & (lane < MLA_NOPE + MLA_ROPE), dkpe, 0.0).astype(BF16)

    return pl.pallas_call(
        body, name=name, grid=(s // tm,),
        in_specs=[pl.BlockSpec((tm, hw), lambda i: (i, 0)), pl.BlockSpec((tm, hw), lambda i: (i, 0)),
                  pl.BlockSpec((tm, vw), lambda i: (i, 0)),
                  pl.BlockSpec((tm, LANES), lambda i: (i, 0)), pl.BlockSpec((tm, LANES), lambda i: (i, 0))],
        out_specs=[pl.BlockSpec((tm, hw), lambda i: (i, 0)), pl.BlockSpec((tm, hw + vw), lambda i: (i, 0)),
                   pl.BlockSpec((tm, LANES), lambda i: (i, 0))],
        out_shape=[jax.ShapeDtypeStruct((s, hw), BF16), jax.ShapeDtypeStruct((s, hw + vw), BF16),
                   jax.ShapeDtypeStruct((s, LANES), BF16)],
        compiler_params=_params(("parallel",)),
    )(dqm, dkm, dvm, cos_t, sin_t)


def _head_mask(shape, a):
    lane = lax.broadcasted_iota(jnp.int32, shape, 1)
    return (lane >= 64 * a) & (lane < 64 * (a + 1))


def _causal_mask(t):
    row = lax.broadcasted_iota(jnp.int32, (t, t), 0)
    col = lax.broadcasted_iota(jnp.int32, (t, t), 1)
    return col <= row


_NT = (((1,), (1,)), ((), ()))
LOG2E = 1.4426950408889634


def _stack_heads(tile, hw):
    lane = lax.broadcasted_iota(jnp.int32, tile.shape, 1)
    z = jnp.zeros_like(tile)
    return jnp.concatenate([jnp.where(lane < hw, tile, z), jnp.where(lane >= hw, tile, z)], axis=0)


def _stacked_rows(r0, r1, t):
    n = r0.shape[-1]
    return jnp.concatenate([jnp.broadcast_to(r0, (t, n)), jnp.broadcast_to(r1, (t, n))], axis=0)


def _stacked_causal_mask(t):
    m = _causal_mask(t)
    return jnp.concatenate([m, m], axis=0)


def _resident(block, index_map):
    return pl.BlockSpec(block, index_map, pipeline_mode=pl.Buffered(1))


def _flash_fwd(q, k, v, bias, *, n_pairs, hw, q_off, k_off, v_off, scale, name, rider=None):
    s = q.shape[0]
    t = min(ATT_T, s)
    nb = s // t
    qw = 2 * hw
    has_bias = bias is not None
    c1 = scale * LOG2E

    def body(*refs):
        refs, ride_refs = _split_rider(refs, rider, n_in=4 if has_bias else 3, n_out=2)
        if has_bias:
            q_ref, k_ref, v_ref, b_ref, o_ref, lse_ref, vt_ref, bcol_ref = refs
        else:
            q_ref, k_ref, v_ref, o_ref, lse_ref, vt_ref = refs
            b_ref = bcol_ref = None
        _ride_start(rider, ride_refs, pl.program_id(0) == 0)
        row = lax.broadcasted_iota(jnp.int32, (t, t), 0)
        col = lax.broadcasted_iota(jnp.int32, (t, t), 1)
        cmask_t = jnp.concatenate([row <= col, row <= col], axis=1)
        lane_lt64 = lax.broadcasted_iota(jnp.int32, (t, LANES), 1) < 64

        def as_column(r):
            return jnp.broadcast_to(r, (8, r.shape[1])).T[:, 0:1]

        def v_block(j, _):
            c0 = pl.multiple_of(j * t, t)
            vt_ref[j] = v_ref[pl.ds(c0, t), :].astype(F32).T.astype(BF16)
            if has_bias:
                for a in range(2):
                    bcol_ref[a, pl.ds(c0, t), :] = as_column(b_ref[0, a, j])
            return 0

        lax.fori_loop(0, nb, v_block, 0)

        def q_block(i, _):
            r0 = pl.multiple_of(i * t, t)
            qs_t = _stack_heads(q_ref[pl.ds(r0, t), :], hw).astype(F32).T.astype(BF16)

            def kv_step(j, carry, masked):
                m, l, acc = carry
                rows = pl.ds(pl.multiple_of(j * t, t), t)
                sc = jnp.dot(k_ref[rows, :], qs_t, preferred_element_type=F32) * c1
                if has_bias:
                    sc = sc + jnp.concatenate([jnp.broadcast_to(bcol_ref[0, rows, :], (t, t)),
                                               jnp.broadcast_to(bcol_ref[1, rows, :], (t, t))], axis=1)
                if masked:
                    sc = jnp.where(cmask_t, sc, NEG_INF)
                m_new = jnp.maximum(m, jnp.max(sc, axis=0, keepdims=True))
                alpha = jnp.exp2(m - m_new)
                p = jnp.exp2(sc - m_new)
                l_new = alpha * l + jnp.sum(p, axis=0, keepdims=True)
                pv = jnp.dot(vt_ref[j], p.astype(BF16), preferred_element_type=F32)
                return m_new, l_new, alpha * acc + pv

            init = (jnp.full((1, 2 * t), NEG_INF, F32), jnp.zeros((1, 2 * t), F32), jnp.zeros((LANES, 2 * t), F32))
            carry = lax.fori_loop(0, i, functools.partial(kv_step, masked=False), init)
            m, l, acc = kv_step(i, carry, True)
            out = (acc / l).T
            lse2 = as_column(m + jnp.log2(l))
            lse_ref[0, 0, pl.ds(r0, t), :] = lse2[:t]
            lse_ref[0, 1, pl.ds(r0, t), :] = lse2[t:]
            o_ref[pl.ds(r0, t), :] = jnp.where(lane_lt64, out[:t], out[t:])
            return 0

        lax.fori_loop(0, nb, q_block, 0)
        _ride_wait(rider, ride_refs, pl.program_id(0) == n_pairs - 1)

    in_specs = [_resident((s, qw), lambda p: (0, q_off + p)), _resident((s, qw), lambda p: (0, k_off + p)),
                _resident((s, LANES), lambda p: (0, v_off + p))]
    args = [q, k, v]
    if has_bias:
        in_specs.append(_resident((1, 2, nb, 1, t), lambda p: (p, 0, 0, 0, 0)))
        args.append(bias)
    out_specs = [pl.BlockSpec((s, LANES), lambda p: (0, p)), pl.BlockSpec((1, 2, s, 1), lambda p: (p, 0, 0, 0))]
    out_shape = [jax.ShapeDtypeStruct((s, n_pairs * LANES), F32), jax.ShapeDtypeStruct((n_pairs, 2, s, 1), F32)]
    scratch = [pltpu.VMEM((nb, LANES, t), BF16)] + ([pltpu.VMEM((2, s, 1), F32)] if has_bias else [])
    scratch += _add_rider(rider, in_specs, args, out_specs, out_shape)
    return pl.pallas_call(
        body, name=name, grid=(n_pairs,), in_specs=in_specs, out_specs=out_specs, out_shape=out_shape,
        scratch_shapes=scratch,
        compiler_params=_params(("parallel",) if rider is None else ("arbitrary",), VMEM_LIMIT),
    )(*args)


def _flash_bwd(q, k, v, do, o, lse, bias, *, n_pairs, hw, q_off, k_off, v_off, scale, qk_dtype, name, rider=None):
    s = q.shape[0]
    t = min(ATT_T, s)
    nb = s // t
    qw = 2 * hw
    has_bias = bias is not None
    c1 = scale * LOG2E

    def body(*refs):
        refs, ride_refs = _split_rider(refs, rider, n_in=7 if has_bias else 6, n_out=5 if has_bias else 3)
        if has_bias:
            (q_ref, k_ref, v_ref, do_ref, o_ref, lse_ref, b_ref, dq_ref, dk_ref, dv_ref, db_ref, dr_ref,
             dkt_ref, dvt_ref) = refs
            db_ref[...] = jnp.zeros_like(db_ref)
        else:
            q_ref, k_ref, v_ref, do_ref, o_ref, lse_ref, dq_ref, dk_ref, dv_ref, dkt_ref, dvt_ref = refs
            b_ref = db_ref = dr_ref = None
        _ride_start(rider, ride_refs, pl.program_id(0) == 0)
        dkt_ref[...] = jnp.zeros_like(dkt_ref)
        dvt_ref[...] = jnp.zeros_like(dvt_ref)
        cmask = _stacked_causal_mask(t)
        lane_lt_hw = lax.broadcasted_iota(jnp.int32, (t, qw), 1) < hw

        def q_block(i, _):
            r0 = pl.multiple_of(i * t, t)
            qs = _stack_heads(q_ref[pl.ds(r0, t), :], hw)
            dos = _stack_heads(do_ref[pl.ds(r0, t), :], 64)
            ot = o_ref[pl.ds(r0, t), :]
            delta = jnp.sum(dos * jnp.concatenate([ot, ot], axis=0), axis=-1, keepdims=True)
            lse2 = jnp.concatenate([lse_ref[0, 0, pl.ds(r0, t), :], lse_ref[0, 1, pl.ds(r0, t), :]], axis=0)
            dosb = dos.astype(BF16)
            dos_t = dos.T.astype(BF16)
            qs_t = qs.astype(F32).T.astype(BF16)

            def kv_step(j, carry, masked):
                dq, rsum = carry
                c0 = pl.multiple_of(j * t, t)
                kt = k_ref[pl.ds(c0, t), :]
                vt = v_ref[pl.ds(c0, t), :]
                sc = lax.dot_general(qs, kt, _NT, preferred_element_type=F32) * c1
                if has_bias:
                    sc = sc + _stacked_rows(b_ref[0, 0, j], b_ref[0, 1, j], t)
                if masked:
                    sc = jnp.where(cmask, sc, NEG_INF)
                p = jnp.exp2(sc - lse2)
                dp = lax.dot_general(dosb, vt, _NT, preferred_element_type=F32)
                ds = p * (dp - delta)
                dsb = ds.astype(BF16)
                dvt_ref[j] += jnp.dot(dos_t, p.astype(BF16), preferred_element_type=F32)
                dkt_ref[j] += jnp.dot(qs_t, dsb, preferred_element_type=F32)
                if has_bias:
                    db_ref[0, 0, j] += jnp.sum(ds[:t], axis=0, keepdims=True)
                    db_ref[0, 1, j] += jnp.sum(ds[t:], axis=0, keepdims=True)
                    rsum = rsum + jnp.sum(ds, axis=-1, keepdims=True)
                return dq + jnp.dot(dsb, kt, preferred_element_type=F32), rsum

            init = (jnp.zeros((2 * t, qw), F32), jnp.zeros((2 * t, 1), F32))
            carry = lax.fori_loop(0, i, functools.partial(kv_step, masked=False), init)
            dq, rsum = kv_step(i, carry, True)
            dq = dq * scale
            dq_ref[pl.ds(r0, t), :] = jnp.where(lane_lt_hw, dq[:t], dq[t:]).astype(qk_dtype)
            if has_bias:
                rsum_row = jnp.broadcast_to(rsum, (2 * t, LANES)).T[0:1]
                dr_ref[0, 0, i] = rsum_row[:, :t]
                dr_ref[0, 1, i] = rsum_row[:, t:]
            return 0

        lax.fori_loop(0, nb, q_block, 0)

        def k_block(j, _):
            c0 = pl.multiple_of(j * t, t)
            dk_ref[pl.ds(c0, t), :] = (dkt_ref[j].T * scale).astype(qk_dtype)
            dv_ref[pl.ds(c0, t), :] = dvt_ref[j].T.astype(BF16)
            return 0

        lax.fori_loop(0, nb, k_block, 0)
        _ride_wait(rider, ride_refs, pl.program_id(0) == n_pairs - 1)

    in_specs = [_resident((s, qw), lambda p: (0, q_off + p)), _resident((s, qw), lambda p: (0, k_off + p)),
                _resident((s, LANES), lambda p: (0, v_off + p)),
                _resident((s, LANES), lambda p: (0, p)), _resident((s, LANES), lambda p: (0, p)),
                _resident((1, 2, s, 1), lambda p: (p, 0, 0, 0))]
    args = [q, k, v, do, o, lse]
    out_specs = [pl.BlockSpec((s, qw), lambda p: (0, p)), pl.BlockSpec((s, qw), lambda p: (0, p)),
                 pl.BlockSpec((s, LANES), lambda p: (0, p))]
    out_shape = [jax.ShapeDtypeStruct((s, n_pairs * qw), qk_dtype), jax.ShapeDtypeStruct((s, n_pairs * qw), qk_dtype),
                 jax.ShapeDtypeStruct((s, n_pairs * LANES), BF16)]
    if has_bias:
        in_specs.append(_resident((1, 2, nb, 1, t), lambda p: (p, 0, 0, 0, 0)))
        args.append(bias)
        out_specs.append(pl.BlockSpec((1, 2, nb, 1, t), lambda p: (p, 0, 0, 0, 0)))
        out_shape.append(jax.ShapeDtypeStruct((n_pairs, 2, nb, 1, t), F32))
        out_specs.append(pl.BlockSpec((1, 2, nb, 1, t), lambda p: (p, 0, 0, 0, 0)))
        out_shape.append(jax.ShapeDtypeStruct((n_pairs, 2, nb, 1, t), F32))
    scratch = [pltpu.VMEM((nb, qw, t), F32), pltpu.VMEM((nb, LANES, t), F32)]
    scratch += _add_rider(rider, in_specs, args, out_specs, out_shape)
    return pl.pallas_call(
        body, name=name, grid=(n_pairs,), in_specs=in_specs, out_specs=out_specs, out_shape=out_shape,
        scratch_shapes=scratch,
        compiler_params=_params(("parallel",) if rider is None else ("arbitrary",), VMEM_LIMIT),
    )(*args)


def _alibi_slope(h):
    return 2.0 ** (-8.0 * (h + 1.0) / SWA_HEADS)


SWA_ROWS = 512
SWA_SCALE = SWA_DIM ** -0.5


def _swa_geometry(i):
    w = WINDOW
    r0 = pl.multiple_of(i * w, w)
    b0 = pl.multiple_of(jnp.maximum(i - 1, 0) * w, w)
    row = lax.broadcasted_iota(jnp.int32, (w, 2 * w), 0)
    col = lax.broadcasted_iota(jnp.int32, (w, 2 * w), 1)
    dist = row - col + (r0 - b0)
    valid = (dist >= 0) & (dist < w)
    return r0, b0, dist.astype(F32), valid


def _swa_q_head(qblk, h):
    kv = h // (SWA_HEADS // SWA_KV_HEADS)
    if h % 2 != kv:
        qblk = pltpu.roll(qblk, 64, axis=1)
    return jnp.where(_head_mask(qblk.shape, kv), qblk, 0.0)


def _swa_fwd(z0b, sinks, *, name):
    s = z0b.shape[0]
    w = WINDOW
    rows = min(SWA_ROWS, s)
    per_step = rows // w
    qcols = SWA_HEADS * SWA_DIM

    def body(sink_ref, q_ref, k_ref, v_ref, o_ref, lse_ref):
        g = pl.program_id(0)
        for ii in range(per_step):
            r0, b0, dist, valid = _swa_geometry(g * per_step + ii)
            kb = k_ref[pl.ds(b0, 2 * w), :]
            vb = v_ref[pl.ds(b0, 2 * w), :]
            o_heads = []
            for h in range(SWA_HEADS):
                kv = h // (SWA_HEADS // SWA_KV_HEADS)
                blk = h // 2
                qh = _swa_q_head(q_ref[ii * w:(ii + 1) * w, blk * LANES:(blk + 1) * LANES].astype(F32), h).astype(BF16)
                sc = lax.dot_general(qh, kb, _NT, preferred_element_type=F32) * SWA_SCALE - _alibi_slope(h) * dist
                sc = jnp.where(valid, sc, NEG_INF)
                sink = sink_ref[0, h]
                m = jnp.maximum(jnp.max(sc, axis=-1, keepdims=True), sink)
                p = jnp.exp(sc - m)
                l = jnp.sum(p, axis=-1, keepdims=True) + jnp.exp(sink - m)
                oh = jnp.dot(p.astype(BF16), vb, preferred_element_type=F32) / l
                if h % 2 != kv:
                    oh = pltpu.roll(oh, 64, axis=1)
                o_heads.append(oh)
                lse_ref[h, ii * w:(ii + 1) * w, :] = m + jnp.log(l)
            lt64 = lax.broadcasted_iota(jnp.int32, (w, LANES), 1) < 64
            o_ref[ii * w:(ii + 1) * w, :] = jnp.concatenate(
                [jnp.where(lt64, o_heads[2 * b], o_heads[2 * b + 1]) for b in range(SWA_HEADS // 2)], axis=1)

    return pl.pallas_call(
        body, name=name, grid=(s // rows,),
        in_specs=[pl.BlockSpec(memory_space=pltpu.SMEM),
                  pl.BlockSpec((rows, qcols), lambda g: (g, 0)),
                  pl.BlockSpec((s, LANES), lambda g: (0, 4)), pl.BlockSpec((s, LANES), lambda g: (0, 5))],
        out_specs=[pl.BlockSpec((rows, qcols), lambda g: (g, 0)), pl.BlockSpec((SWA_HEADS, rows, 1), lambda g: (0, g, 0))],
        out_shape=[jax.ShapeDtypeStruct((s, qcols), F32), jax.ShapeDtypeStruct((SWA_HEADS, s, 1), F32)],
        compiler_params=_params(("parallel",), VMEM_LIMIT),
    )(sinks, z0b, z0b, z0b)


def _swa_bwd(z0b, sinks, do, o, lse, *, name):
    s = z0b.shape[0]
    w = WINDOW
    rows = min(SWA_ROWS, s)
    per_step = rows // w
    qcols = SWA_HEADS * SWA_DIM
    nblk = s // w

    def body(sink_ref, q_ref, k_ref, v_ref, do_ref, o_ref, lse_ref, dq_ref, dkt_ref, dvt_ref, dsink_ref):
        g = pl.program_id(0)

        @pl.when(g == 0)
        def _():
            dkt_ref[...] = jnp.zeros_like(dkt_ref)
            dvt_ref[...] = jnp.zeros_like(dvt_ref)
            dsink_ref[...] = jnp.zeros_like(dsink_ref)

        for ii in range(per_step):
            i = g * per_step + ii
            r0, b0, dist, valid = _swa_geometry(i)
            j0 = jnp.maximum(i - 1, 0)
            kb = k_ref[pl.ds(b0, 2 * w), :]
            vb = v_ref[pl.ds(b0, 2 * w), :]
            dq_heads = []
            for h in range(SWA_HEADS):
                kv = h // (SWA_HEADS // SWA_KV_HEADS)
                blk = h // 2
                cs = slice(blk * LANES, (blk + 1) * LANES)
                rs = slice(ii * w, (ii + 1) * w)
                qh32 = _swa_q_head(q_ref[rs, cs].astype(F32), h)
                qh = qh32.astype(BF16)
                doh32 = _swa_q_head(do_ref[rs, cs], h)
                oh32 = _swa_q_head(o_ref[rs, cs], h)
                delta = jnp.sum(doh32 * oh32, axis=-1, keepdims=True)
                lse = lse_ref[h, rs, :]
                sink = sink_ref[0, h]
                sc = lax.dot_general(qh, kb, _NT, preferred_element_type=F32) * SWA_SCALE - _alibi_slope(h) * dist
                sc = jnp.where(valid, sc, NEG_INF)
                p = jnp.exp(sc - lse)
                dp = lax.dot_general(doh32.astype(BF16), vb, _NT, preferred_element_type=F32)
                ds = p * (dp - delta)
                dsb = ds.astype(BF16)
                pb = p.astype(BF16)
                dsink_ref[h:h + 1, :] += jnp.broadcast_to(-jnp.sum(jnp.exp(sink - lse) * delta), (1, LANES))
                do_t = doh32.T.astype(BF16)
                q_t = qh32.T.astype(BF16)
                dvt = jnp.dot(do_t, pb, preferred_element_type=F32)
                dkt = jnp.dot(q_t, dsb, preferred_element_type=F32) * SWA_SCALE
                dvt_ref[j0] += dvt[:, :w]
                dvt_ref[j0 + 1] += dvt[:, w:]
                dkt_ref[j0] += dkt[:, :w]
                dkt_ref[j0 + 1] += dkt[:, w:]
                dq = jnp.dot(dsb, kb, preferred_element_type=F32) * SWA_SCALE
                if h % 2 != kv:
                    dq = pltpu.roll(dq, 64, axis=1)
                dq_heads.append(dq)
            lt64 = lax.broadcasted_iota(jnp.int32, (w, LANES), 1) < 64
            dq_ref[ii * w:(ii + 1) * w, :] = jnp.concatenate(
                [jnp.where(lt64, dq_heads[2 * b], dq_heads[2 * b + 1]) for b in range(SWA_HEADS // 2)], axis=1)

    return pl.pallas_call(
        body, name=name, grid=(s // rows,),
        in_specs=[pl.BlockSpec(memory_space=pltpu.SMEM),
                  pl.BlockSpec((rows, qcols), lambda g: (g, 0)),
                  pl.BlockSpec((s, LANES), lambda g: (0, 4)), pl.BlockSpec((s, LANES), lambda g: (0, 5)),
                  pl.BlockSpec((rows, qcols), lambda g: (g, 0)), pl.BlockSpec((rows, qcols), lambda g: (g, 0)),
                  pl.BlockSpec((SWA_HEADS, rows, 1), lambda g: (0, g, 0))],
        out_specs=[pl.BlockSpec((rows, qcols), lambda g: (g, 0)),
                   pl.BlockSpec((nblk, LANES, w), lambda g: (0, 0, 0)),
                   pl.BlockSpec((nblk, LANES, w), lambda g: (0, 0, 0)),
                   pl.BlockSpec((SWA_HEADS, LANES), lambda g: (0, 0))],
        out_shape=[jax.ShapeDtypeStruct((s, qcols), F32),
                   jax.ShapeDtypeStruct((nblk, LANES, w), F32), jax.ShapeDtypeStruct((nblk, LANES, w), F32),
                   jax.ShapeDtypeStruct((SWA_HEADS, LANES), F32)],
        compiler_params=_params(("arbitrary",), VMEM_LIMIT),
    )(sinks, z0b, z0b, z0b, do, o, lse)


CUM_T = 256


def _split3(x):
    hi = x.astype(BF16)
    r1 = x - hi.astype(F32)
    mid = r1.astype(BF16)
    lo = (r1 - mid.astype(F32)).astype(BF16)
    return hi, mid, lo


def _tri_dot(tri, x):
    hi, mid, lo = _split3(x)
    out = jnp.dot(tri, hi, preferred_element_type=F32)
    out = out + jnp.dot(tri, mid, preferred_element_type=F32)
    return out + jnp.dot(tri, lo, preferred_element_type=F32)


def _logf_fwd(zf, bf, *, name):
    s = zf.shape[0]
    t = CUM_T
    nb = s // t

    def body(z_ref, b_ref, c_ref, carry_ref):
        i = pl.program_id(0)

        @pl.when(i == 0)
        def _():
            carry_ref[...] = jnp.zeros_like(carry_ref)

        x = z_ref[...] + b_ref[...]
        lf = jnp.minimum(x, 0.0) - jnp.log(1.0 + jnp.exp(-jnp.abs(x)))
        row = lax.broadcasted_iota(jnp.int32, (t, t), 0)
        col = lax.broadcasted_iota(jnp.int32, (t, t), 1)
        tri = jnp.where(col <= row, 1.0, 0.0).astype(BF16)
        c = _tri_dot(tri, lf) + carry_ref[...]
        c_ref[...] = c
        carry_ref[...] = c[t - 1:t, :]

    return pl.pallas_call(
        body, name=name, grid=(nb,),
        in_specs=[pl.BlockSpec((t, LANES), lambda i: (i, 0)), pl.BlockSpec((1, LANES), lambda i: (0, 0))],
        out_specs=pl.BlockSpec((t, LANES), lambda i: (i, 0)),
        out_shape=jax.ShapeDtypeStruct((s, LANES), F32),
        scratch_shapes=[pltpu.VMEM((1, LANES), F32)],
        compiler_params=_params(("arbitrary",)),
    )(zf, bf)


def _logf_bwd(dc, zf, bf, *, name):
    s = zf.shape[0]
    t = CUM_T
    nb = s // t

    def body(dc_ref, z_ref, b_ref, dz_ref, db_ref, carry_ref):
        i = pl.program_id(0)

        @pl.when(i == 0)
        def _():
            carry_ref[...] = jnp.zeros_like(carry_ref)
            db_ref[...] = jnp.zeros_like(db_ref)

        row = lax.broadcasted_iota(jnp.int32, (t, t), 0)
        col = lax.broadcasted_iota(jnp.int32, (t, t), 1)
        tri = jnp.where(col >= row, 1.0, 0.0).astype(BF16)
        dlf = _tri_dot(tri, dc_ref[...]) + carry_ref[...]
        carry_ref[...] = dlf[0:1, :]
        x = z_ref[...] + b_ref[...]
        dz = dlf * _sigmoid(-x)
        dz_ref[...] = dz.astype(BF16)
        db_ref[...] += jnp.sum(dz, axis=0, keepdims=True)

    return pl.pallas_call(
        body, name=name, grid=(nb,),
        in_specs=[pl.BlockSpec((t, LANES), lambda i: (nb - 1 - i, 0)), pl.BlockSpec((t, LANES), lambda i: (nb - 1 - i, 0)),
                  pl.BlockSpec((1, LANES), lambda i: (0, 0))],
        out_specs=[pl.BlockSpec((t, LANES), lambda i: (nb - 1 - i, 0)), pl.BlockSpec((1, LANES), lambda i: (0, 0))],
        out_shape=[jax.ShapeDtypeStruct((s, LANES), BF16), jax.ShapeDtypeStruct((1, LANES), F32)],
        scratch_shapes=[pltpu.VMEM((1, LANES), F32)],
        compiler_params=_params(("arbitrary",)),
    )(dc, zf, bf)


def _loss_head(x2, g, target, *, name):
    s = x2.shape[0]
    tm = _tile(s, (256, 128))

    def body(x_ref, g_ref, t_ref, dx_ref, loss_ref, dg_ref):
        i = pl.program_id(0)

        @pl.when(i == 0)
        def _():
            loss_ref[...] = jnp.zeros_like(loss_ref)
            dg_ref[...] = jnp.zeros_like(dg_ref)

        xf = x_ref[...]
        r = lax.rsqrt(jnp.mean(xf * xf, axis=-1, keepdims=True) + EPS)
        xh = xf * r
        gv = g_ref[...]
        err = xh * gv - t_ref[...]
        loss_ref[...] += jnp.broadcast_to(0.5 * jnp.sum(jnp.mean(err * err, axis=-1, keepdims=True)), loss_ref.shape)
        dy = err * (1.0 / D_MODEL)
        dg_ref[...] += jnp.sum(dy * xh, axis=0, keepdims=True)
        dxh = dy * gv
        dx_ref[...] = r * (dxh - xh * jnp.mean(dxh * xh, axis=-1, keepdims=True))

    return pl.pallas_call(
        body, name=name, grid=(s // tm,),
        in_specs=[pl.BlockSpec((tm, D_MODEL), lambda i: (i, 0)), pl.BlockSpec((1, D_MODEL), lambda i: (0, 0)),
                  pl.BlockSpec((tm, D_MODEL), lambda i: (i, 0))],
        out_specs=[pl.BlockSpec((tm, D_MODEL), lambda i: (i, 0)), pl.BlockSpec((8, LANES), lambda i: (0, 0)),
                   pl.BlockSpec((1, D_MODEL), lambda i: (0, 0))],
        out_shape=[jax.ShapeDtypeStruct((s, D_MODEL), F32), jax.ShapeDtypeStruct((8, LANES), F32),
                   jax.ShapeDtypeStruct((1, D_MODEL), F32)],
        compiler_params=_params(("arbitrary",)),
    )(x2, g, target)


def _sum_pieces(p_ref):
    g = p_ref[0].astype(F32)
    for k in range(1, N_DEV):
        g = g + p_ref[k].astype(F32)
    return g


def _adam_update(g, w, m, v):
    bc1 = 1.0 - ADAM_B1 ** ADAM_STEP
    bc2 = 1.0 - ADAM_B2 ** ADAM_STEP
    nm = ADAM_B1 * m + (1.0 - ADAM_B1) * g
    nv = ADAM_B2 * v + (1.0 - ADAM_B2) * (g * g)
    m_hat = nm / bc1
    v_hat = nv / bc2
    return -ADAM_LR * (m_hat / (jnp.sqrt(v_hat) + ADAM_EPS) + ADAM_WD * w), nm, nv


def _adamw(pieces, w, m, v, *, name):
    rows, cols = w.shape
    tr = _tile(rows, (RB1, RB0, SMALL_ROWS))

    def body(p_ref, w_ref, m_ref, v_ref, g_ref, d_ref, nm_ref, nv_ref):
        g = _sum_pieces(p_ref)
        g_ref[...] = g
        d_ref[...], nm_ref[...], nv_ref[...] = _adam_update(g, w_ref[...], m_ref[...], v_ref[...])

    spec = pl.BlockSpec((tr, cols), lambda i: (i, 0))
    shape = jax.ShapeDtypeStruct((rows, cols), F32)
    return pl.pallas_call(
        body, name=name, grid=(rows // tr,),
        in_specs=[pl.BlockSpec((N_DEV, tr, cols), lambda i: (0, i, 0)), spec, spec, spec],
        out_specs=[spec, spec, spec, spec], out_shape=[shape, shape, shape, shape],
        compiler_params=_params(("parallel",)),
    )(pieces, w, m, v)


def _sum8(pieces, rows, *, name):
    cols = pieces.shape[2]
    tr = _tile(rows, (176, 96))

    def body(p_ref, g_ref):
        g_ref[...] = _sum_pieces(p_ref)

    return pl.pallas_call(
        body, name=name, grid=(rows // tr,),
        in_specs=[pl.BlockSpec((N_DEV, tr, cols), lambda i: (0, i, 0))],
        out_specs=pl.BlockSpec((tr, cols), lambda i: (i, 0)),
        out_shape=jax.ShapeDtypeStruct((rows, cols), F32),
        compiler_params=_params(("parallel",)),
    )(pieces)


def _adamw_native(g, w, m, v, *, name):
    rows, cols = w.shape
    tr = _tile(rows, (256, 128))

    def body(g_ref, w_ref, m_ref, v_ref, d_ref, nm_ref, nv_ref):
        d_ref[...], nm_ref[...], nv_ref[...] = _adam_update(g_ref[...], w_ref[...], m_ref[...], v_ref[...])

    spec = pl.BlockSpec((tr, cols), lambda i: (i, 0))
    shape = jax.ShapeDtypeStruct((rows, cols), F32)
    return pl.pallas_call(
        body, name=name, grid=(rows // tr,), in_specs=[spec, spec, spec, spec],
        out_specs=[spec, spec, spec], out_shape=[shape, shape, shape],
        compiler_params=_params(("parallel",)),
    )(g, w, m, v)


MESH = pl.DeviceIdType.MESH
ANY = pl.BlockSpec(memory_space=pl.ANY)


def _all_gather(shard, *, name):
    rows, lanes = shard.shape

    def body(x_ref, out_ref, send_sems, recv_sems, local_sem):
        x, y, c = lax.axis_index("x"), lax.axis_index("y"), lax.axis_index("c")
        me, sibling = (x, y, c), (x, y, 1 - c)
        chips = [(1 - x, y), (x, 1 - y), (1 - x, 1 - y)]

        def block(px, py, pc):
            return out_ref.at[4 * px + 2 * py + pc]

        def copy(k, blk, to, src=None):
            return pltpu.make_async_remote_copy(
                src_ref=block(*blk) if src is None else src, dst_ref=block(*blk),
                send_sem=send_sems.at[k], recv_sem=recv_sems.at[k], device_id=to, device_id_type=MESH)

        mine = pltpu.make_async_copy(x_ref, block(*me), local_sem)
        mine.start()
        first = [copy(0, me, sibling, src=x_ref)]
        first += [copy(1 + j, me, (*chip, c), src=x_ref) for j, chip in enumerate(chips)]
        for cp in first:
            cp.start()
        passed = [copy(4 + j, (*chip, c), sibling) for j, chip in enumerate(chips)]
        for j, chip in enumerate(chips):
            copy(1 + j, (*chip, c), me).wait_recv()
            passed[j].start()
        copy(0, sibling, me).wait_recv()
        for j, chip in enumerate(chips):
            copy(4 + j, (*chip, 1 - c), me).wait_recv()
        for cp in first + passed:
            cp.wait_send()
        mine.wait()

    return pl.pallas_call(
        body, name=name, out_shape=jax.ShapeDtypeStruct((N_DEV, rows, lanes), shard.dtype),
        in_specs=[ANY], out_specs=ANY,
        scratch_shapes=[pltpu.SemaphoreType.DMA((7,)), pltpu.SemaphoreType.DMA((7,)), pltpu.SemaphoreType.DMA(())],
    )(shard)


def _peer_copies(kind, src_ref, out_ref, send_sems, recv_sems, local_sem):
    x, y, c = lax.axis_index("x"), lax.axis_index("y"), lax.axis_index("c")
    me = 4 * x + 2 * y + c

    def src(idx):
        return src_ref.at[idx] if kind == "exchange" else src_ref

    mine = pltpu.make_async_copy(src(me), out_ref.at[me], local_sem)
    copies = []
    for r in range(1, N_DEV):
        px = 1 - x if r & 4 else x
        py = 1 - y if r & 2 else y
        pc = 1 - c if r & 1 else c
        copies.append(pltpu.make_async_remote_copy(
            src_ref=src(4 * px + 2 * py + pc), dst_ref=out_ref.at[me],
            send_sem=send_sems.at[r - 1], recv_sem=recv_sems.at[r - 1],
            device_id=(px, py, pc), device_id_type=MESH))
    return mine, copies


PEER_SEMS = [pltpu.SemaphoreType.DMA((7,)), pltpu.SemaphoreType.DMA((7,)), pltpu.SemaphoreType.DMA(())]


def _exchange(pieces, *, name):
    def body(g_ref, out_ref, send_sems, recv_sems, local_sem):
        mine, copies = _peer_copies("exchange", g_ref, out_ref, send_sems, recv_sems, local_sem)
        mine.start()
        for cp in copies:
            cp.start()
        for cp in copies:
            cp.wait()
        mine.wait()

    return pl.pallas_call(
        body, name=name, out_shape=jax.ShapeDtypeStruct(pieces.shape, pieces.dtype),
        in_specs=[ANY], out_specs=ANY, scratch_shapes=list(PEER_SEMS),
    )(pieces)


def _add_rider(rider, in_specs, args, out_specs, out_shape):
    if rider is None:
        return []
    _, arr = rider
    in_specs.append(ANY)
    args.append(arr)
    out_specs.append(ANY)
    out_shape.append(jax.ShapeDtypeStruct((N_DEV,) + arr.shape[-2:], arr.dtype))
    return list(PEER_SEMS)


def _split_rider(refs, rider, n_in, n_out):
    if rider is None:
        return refs, None
    refs = list(refs)
    rin = refs.pop(n_in)
    rout = refs.pop(n_in + n_out)
    return refs[:-3], (rin, rout, *refs[-3:])


def _ride_start(rider, ride_refs, first):
    if rider is None:
        return

    @pl.when(first)
    def _():
        mine, copies = _peer_copies(rider[0], *ride_refs)
        mine.start()
        for cp in copies:
            cp.start()


def _ride_wait(rider, ride_refs, last):
    if rider is None:
        return

    @pl.when(last)
    def _():
        mine, copies = _peer_copies(rider[0], *ride_refs)
        for cp in copies:
            cp.wait()
        mine.wait()


def _gathered_cols(blocks, kdim):
    n = blocks.shape[1] * WIDE // kdim
    return blocks.reshape(N_DEV, kdim, n).transpose(1, 0, 2).reshape(kdim, N_DEV * n)


def _scatter_cols(dw):
    kdim, n8 = dw.shape
    n = n8 // N_DEV
    return dw.reshape(kdim, N_DEV, n).transpose(1, 0, 2).reshape(N_DEV, kdim * n // WIDE, WIDE)


def _pad_rows(a, rows):
    pad = [(0, 0)] * a.ndim
    pad[-2] = (0, rows - a.shape[-2])
    return jnp.pad(a, pad)


def _layer0_in_weight_t(wt):
    cq, ckv, kpe = wt[0:256], wt[256:384], wt[384:416]
    q_s, k_s, v_s, gate = wt[416:928], wt[928:1056], wt[1056:1184], wt[1184:2208]
    z = jnp.zeros((64, wt.shape[1]), wt.dtype)
    return jnp.concatenate([gate, cq, ckv, z, kpe, z[:32], q_s, k_s, v_s], axis=0)


def _layer0_in_grad_t(dwt):
    gate, cq, ckv, kpe = dwt[0:1024], dwt[1024:1280], dwt[1280:1408], dwt[1472:1504]
    q_s, k_s, v_s = dwt[1536:2048], dwt[2048:2176], dwt[2176:2304]
    return jnp.concatenate([cq, ckv, kpe, q_s, k_s, v_s, gate], axis=0)


def _layer1_in_weight_t(wt):
    main = jnp.concatenate([wt[:3 * D_MODEL], wt[3 * D_MODEL + FOX_HEADS:]], axis=0)
    return main, _pad_rows(wt[3 * D_MODEL:3 * D_MODEL + FOX_HEADS], LANES)


def _layer1_in_grad_t(d_main, d_wft):
    return jnp.concatenate([d_main[:3 * D_MODEL], d_wft[:FOX_HEADS], d_main[3 * D_MODEL:]], axis=0)


def _q_up_weight(w):
    return jnp.pad(w.reshape(MLA_Q_RANK, MLA_HEADS, 96), ((0, 0), (0, 0), (0, 32))).reshape(MLA_Q_RANK, MLA_HEADS * LANES)


def _q_up_grad(dwp):
    return dwp.reshape(MLA_Q_RANK, MLA_HEADS, LANES)[:, :, :96].reshape(MLA_Q_RANK, MLA_HEADS * 96)


def _kv_up_weight(w):
    w4 = w.reshape(MLA_KV_RANK, MLA_HEADS, 2, 64)
    kp = jnp.pad(w4[:, :, 0, :], ((0, 0), (0, 0), (0, 64))).reshape(MLA_KV_RANK, MLA_HEADS * LANES)
    vp = w4[:, :, 1, :].reshape(MLA_KV_RANK, MLA_HEADS * 64)
    return jnp.concatenate([kp, vp], axis=1)


def _kv_up_grad(dwp):
    dk = dwp[:, :MLA_HEADS * LANES].reshape(MLA_KV_RANK, MLA_HEADS, LANES)[:, :, :64]
    dv = dwp[:, MLA_HEADS * LANES:].reshape(MLA_KV_RANK, MLA_HEADS, 64)
    return jnp.stack([dk, dv], axis=2).reshape(MLA_KV_RANK, MLA_HEADS * LANES)


def _pad_lanes(a):
    return jnp.pad(a, ((0, 0), (0, LANES - a.shape[1])))


def _small_pack(g_in, g_final, g_q_a, g_kv_a, sinks, b_f, loss):
    rows = [g_in.reshape(8, LANES), g_final.reshape(8, LANES), g_q_a.reshape(2, LANES), g_kv_a.reshape(1, LANES),
            _pad_lanes(sinks.reshape(1, -1)), _pad_lanes(b_f.reshape(1, -1)), _pad_lanes(loss.reshape(1, 1)),
            jnp.zeros((2, LANES), F32)]
    return jnp.concatenate(rows, axis=0)


def _small_unpack(a):
    return (a[0:8].reshape(1, D_MODEL), a[8:16].reshape(D_MODEL), a[16:18].reshape(1, MLA_Q_RANK),
            a[18:19].reshape(1, MLA_KV_RANK), a[19:20, :SWA_HEADS], a[20:21, :FOX_HEADS], a[21, 0])


def _local_step(x, positions, target, e_g_in, w0t, e_g_q_a, wq, e_g_kv_a, wkv, e_sinks,
                late, o_b_f, g_final, scatter1=None):
    s = x.shape[0]
    att_t = min(ATT_T, s)
    nb = s // att_t
    mla_scale = (MLA_NOPE + MLA_ROPE) ** -0.5
    fox_scale = FOX_DIM ** -0.5
    n0a = Z0A_UNITS * LANES

    inv_freq = 1.0 / (ROPE_THETA ** (jnp.arange(0, MLA_ROPE, 2, dtype=F32) / MLA_ROPE))
    ang = positions.astype(F32)[:, None] * inv_freq
    cos, sin = jnp.cos(ang), jnp.sin(ang)
    ones, zeros = jnp.ones((s, 64), F32), jnp.zeros((s, 64), F32)
    cos_t = jnp.concatenate([ones, cos, cos, ones[:, :32]], axis=1)
    sin_t = jnp.concatenate([zeros, -sin, sin, zeros[:, :32]], axis=1)

    h0 = _rmsnorm_fwd(x, e_g_in, width=D_MODEL, col_blk=0, name="l0_norm")
    z0a = _matmul(h0, w0t, tb=True, b_rows=(0, n0a), name="l0_in_a")
    z0b = _matmul(h0, w0t, tb=True, b_rows=(n0a, Z0B_UNITS * LANES), name="l0_in_b", out_dtype=BF16)
    cqn = _rmsnorm_fwd(z0a, e_g_q_a, width=MLA_Q_RANK, col_blk=4, name="l0_q_norm")
    ckvn = _rmsnorm_fwd(z0a, e_g_kv_a, width=MLA_KV_RANK, col_blk=10, name="l0_kv_norm")
    qp = _matmul(cqn, wq, name="l0_q_up")
    kvp = _matmul(ckvn, wkv, name="l0_kv_up", out_dtype=BF16)
    qm, km = _rope_fwd(qp, kvp, z0a, cos_t, sin_t, name="l0_rope")
    gathers = len(late) == 2
    res = _flash_fwd(qm, km, kvp, None, n_pairs=MLA_HEADS // 2, hw=LANES, q_off=0, k_off=0, v_off=MLA_HEADS,
                     scale=mla_scale, name="l0_mla_fwd", rider=("gather", late[0]) if gathers else None)
    o_mla, lse_mla = res[0], res[1]
    wo0, o_g_in, w1t, wft, wo1 = late[1](res[2]) if gathers else late
    o_swa, lse_swa = _swa_fwd(z0b, e_sinks, name="l0_swa_fwd")
    og0 = _gate_fwd([o_mla, o_swa], z0a, name="l0_gate")
    x1 = _matmul(og0, wo0, add=x, name="l0_out")

    h1 = _rmsnorm_fwd(x1, o_g_in, width=D_MODEL, col_blk=0, name="l1_norm")
    z1 = _matmul(h1, w1t, tb=True, b_rows=(0, 3 * D_MODEL), name="l1_in_qkv", out_dtype=BF16)
    gate1 = _matmul(h1, w1t, tb=True, b_rows=(3 * D_MODEL, D_MODEL), name="l1_in_gate")
    zf = _matmul(h1, wft, tb=True, name="l1_in_f")
    bf = _pad_lanes(o_b_f)
    log_cum = _logf_fwd(zf, bf, name="l1_logf")
    bias = (-LOG2E * log_cum[:, :FOX_HEADS]).T.reshape(FOX_HEADS // 2, 2, nb, 1, att_t)
    o_fox, lse_fox = _flash_fwd(z1, z1, z1, bias, n_pairs=FOX_HEADS // 2, hw=64, q_off=0, k_off=8, v_off=16,
                                scale=fox_scale, name="l1_fox_fwd")
    og1 = _gate_fwd([o_fox], gate1, name="l1_gate")
    x2 = _matmul(og1, wo1, add=x1, name="l1_out")

    dx2, loss_part, d_g_final = _loss_head(x2, g_final.reshape(1, D_MODEL), target, name="loss_head")

    d_wo1 = _matmul(og1, dx2, ta=True, name="l1_out_dw")
    d_og1 = _matmul(dx2, wo1, tb=True, name="l1_out_dx")
    do_fox, d_gate1 = _gate_bwd(d_og1, [o_fox], gate1, name="l1_gate_bwd")
    dq1, dk1, dv1, dbias, drow = _flash_bwd(z1, z1, z1, do_fox, o_fox, lse_fox, bias, n_pairs=FOX_HEADS // 2, hw=64,
                                            q_off=0, k_off=8, v_off=16, scale=fox_scale, qk_dtype=BF16,
                                            name="l1_fox_bwd")
    d_log_cum = (drow.reshape(FOX_HEADS, s) - dbias.reshape(FOX_HEADS, s)).T
    d_log_cum = jnp.pad(d_log_cum, ((0, 0), (0, LANES - FOX_HEADS)))
    d_zf, d_bf = _logf_bwd(d_log_cum, zf, bf, name="l1_logf_bwd")
    dz1 = jnp.concatenate([dq1, dk1, dv1, d_gate1], axis=1)
    d_w1t = _matmul(dz1, h1, ta=True, name="l1_in_dw")
    d_wft = _matmul(d_zf, h1, ta=True, name="l1_in_f_dw")
    dh1 = _matmul(dz1, w1t, name="l1_in_dx")
    dh1 = _matmul(d_zf, wft, add=dh1, name="l1_in_f_dx")
    dx1, d_o_g_in = _rmsnorm_bwd(x1, o_g_in, dh1, width=D_MODEL, col_blk=0, add=dx2, name="l1_norm_bwd")

    d_wo0 = _matmul(og0, dx1, ta=True, name="l0_out_dw")
    d_og0 = _matmul(dx1, wo0, tb=True, name="l0_out_dx")
    do_mla, do_swa, d_gate0 = _gate_bwd(d_og0, [o_mla, o_swa], z0a, name="l0_gate_bwd")
    dq_s, dkt_s, dvt_s, d_sinks = _swa_bwd(z0b, e_sinks, do_swa, o_swa, lse_swa, name="l0_swa_bwd")
    dk_s = dkt_s.transpose(0, 2, 1).reshape(s, LANES)
    dv_s = dvt_s.transpose(0, 2, 1).reshape(s, LANES)
    rider = None
    if scatter1 is not None:
        rider = ("exchange", scatter1(dict(w1t=d_w1t, wft=d_wft, wo1=d_wo1, o_g_in=d_o_g_in, wo0=d_wo0)))
    res = _flash_bwd(qm, km, kvp, do_mla, o_mla, lse_mla, None, n_pairs=MLA_HEADS // 2, hw=LANES, q_off=0, k_off=0,
                     v_off=MLA_HEADS, scale=mla_scale, qk_dtype=F32, name="l0_mla_bwd", rider=rider)
    dqm, dkm, dvm = res[0], res[1], res[2]
    recv1 = res[3] if rider is not None else None
    d_qp, d_kvp, d_kpe = _rope_bwd(dqm, dkm, dvm, cos_t, sin_t, name="l0_rope_bwd")
    d_wq = _matmul(cqn, d_qp, ta=True, name="l0_q_up_dw")
    d_cqn = _matmul(d_qp, wq, tb=True, name="l0_q_up_dx")
    d_wkv = _matmul(ckvn, d_kvp, ta=True, name="l0_kv_up_dw")
    d_ckvn = _matmul(d_kvp, wkv, tb=True, name="l0_kv_up_dx")
    d_cq, d_g_q_a = _rmsnorm_bwd(z0a, e_g_q_a, d_cqn, width=MLA_Q_RANK, col_blk=4, out_dtype=BF16, name="l0_q_norm_bwd")
    d_ckv, d_g_kv_a = _rmsnorm_bwd(z0a, e_g_kv_a, d_ckvn, width=MLA_KV_RANK, col_blk=10, out_dtype=BF16,
                                   name="l0_kv_norm_bwd")
    dz0 = jnp.concatenate([d_gate0, d_cq, d_ckv, d_kpe, dq_s.astype(BF16), dk_s.astype(BF16), dv_s.astype(BF16)], axis=1)
    d_w0t = _matmul(dz0, h0, ta=True, name="l0_in_dw")
    dh0 = _matmul(dz0, w0t, name="l0_in_dx")
    grad_x, d_e_g_in = _rmsnorm_bwd(x, e_g_in, dh0, width=D_MODEL, col_blk=0, add=dx1, name="l0_norm_bwd")

    return dict(recv1=recv1, loss=loss_part[0, 0], grad_x=grad_x, e_g_in=d_e_g_in, w0t=d_w0t, e_g_q_a=d_g_q_a, wq=d_wq,
                e_g_kv_a=d_g_kv_a, wkv=d_wkv, e_sinks=d_sinks[:, 0].reshape(1, SWA_HEADS), wo0=d_wo0,
                o_g_in=d_o_g_in, w1t=d_w1t, wft=d_wft, o_b_f=d_bf[:, :FOX_HEADS], wo1=d_wo1, g_final=d_g_final.reshape(D_MODEL))


def _wide(a, rows):
    flat = a.reshape(-1)
    return jnp.pad(flat, (0, rows * WIDE - flat.shape[0])).reshape(rows, WIDE)


def _rows_b0(w_q, w_kv):
    return jnp.concatenate([_wide(w_q, 32), _wide(w_kv, 16)], axis=0)


def _unflat_b0(f):
    return f[0:24].reshape(1, MLA_Q_RANK, 96), f[32:48].reshape(1, MLA_KV_RANK, 128)


def _rows_b1(o_w_out, e_w_out, g_in):
    return jnp.concatenate([o_w_out, e_w_out, _wide(g_in, 16)], axis=0)


def _unflat_b1(f):
    return f[0:128][None], f[128:256][None], f[256:257, :LANES]


def kernel(x, positions, e_g_in, e_w_in, e_g_q_a, e_w_q_up, e_g_kv_a, e_w_kv_up, e_sinks, e_w_out, o_g_in, o_w_in, o_b_f, o_w_out, g_final, loss_target, m_e_g_in, m_e_w_in, m_e_g_q_a, m_e_w_q_up, m_e_g_kv_a, m_e_w_kv_up, m_e_sinks, m_e_w_out, m_o_g_in, m_o_w_in, m_o_b_f, m_o_w_out, m_g_final, v_e_g_in, v_e_w_in, v_e_g_q_a, v_e_w_q_up, v_e_g_kv_a, v_e_w_kv_up, v_e_sinks, v_e_w_out, v_o_g_in, v_o_w_in, v_o_b_f, v_o_w_out, v_g_final):
    def bf(a):
        return a.astype(BF16)

    shard0 = jnp.concatenate([_pad_rows(bf(e_w_in[0]).T, RA0), _rows_b0(bf(e_w_q_up[0]), bf(e_w_kv_up[0]))], axis=0)
    gath0 = _all_gather(shard0, name="weights0_all_gather")
    w0t = _layer0_in_weight_t(gath0[:, :N_E_IN].reshape(N_DEV * N_E_IN, WIDE))
    wq = _q_up_weight(_gathered_cols(gath0[:, RA0:RA0 + 24], MLA_Q_RANK))
    wkv = _kv_up_weight(_gathered_cols(gath0[:, RA0 + 32:RA0 + 48], MLA_KV_RANK))

    g_bits = lax.bitcast_convert_type(o_g_in.reshape(LANES), BF16)
    shard1 = jnp.concatenate([_pad_rows(bf(o_w_in[0]).T, RA1), _rows_b1(bf(o_w_out[0]), bf(e_w_out[0]), g_bits)], axis=0)

    def unpack1(gath1):
        w1t, wft = _layer1_in_weight_t(gath1[:, :N_O_IN].reshape(N_DEV * N_O_IN, WIDE))
        wo1 = gath1[:, RA1:RA1 + 128].reshape(D_MODEL, D_MODEL)
        wo0 = gath1[:, RA1 + 128:RA1 + 256].reshape(D_MODEL, D_MODEL)
        bits = gath1[:, RA1 + 256, :2 * LANES].reshape(N_DEV, LANES, 2)
        return wo0, lax.bitcast_convert_type(bits, F32).reshape(1, D_MODEL), w1t, wft, wo1

    def scatter1(g):
        d_in_t = _layer1_in_grad_t(g["w1t"], g["wft"]).reshape(N_DEV, N_O_IN, WIDE)
        d_o_g = jnp.pad(g["o_g_in"].reshape(N_DEV, 1, LANES), ((0, 0), (0, 15), (0, WIDE - LANES)))
        return jnp.concatenate([_pad_rows(d_in_t, RA1), g["wo1"].reshape(N_DEV, 128, WIDE),
                                g["wo0"].reshape(N_DEV, 128, WIDE), d_o_g], axis=1).astype(BF16)

    gr = _local_step(x[0], positions[0], loss_target[0], e_g_in, w0t, e_g_q_a, wq, e_g_kv_a, wkv, e_sinks,
                     (shard1, unpack1), o_b_f, g_final, scatter1=scatter1)

    pieces0 = jnp.concatenate([
        _pad_rows(_layer0_in_grad_t(gr["w0t"]).reshape(N_DEV, N_E_IN, WIDE), RA0),
        _pad_rows(_scatter_cols(_q_up_grad(gr["wq"])), 32), _scatter_cols(_kv_up_grad(gr["wkv"]))], axis=1)
    recv0 = _exchange(pieces0.astype(BF16), name="grads0_exchange")

    def in_projection(recv, ra, n, w, m, v, name):
        g = _sum8(recv, ra, name=name + "_grad_sum")[:n].T
        d, nm, nv = _adamw_native(g, w[0], m[0], v[0], name=name + "_adamw")
        return g[None], d[None], nm[None], nv[None]

    e_in = in_projection(recv0, RA0, N_E_IN, e_w_in, m_e_w_in, v_e_w_in, "e_w_in")
    o_in = in_projection(gr["recv1"], RA1, N_O_IN, o_w_in, m_o_w_in, v_o_w_in, "o_w_in")
    b0 = _adamw(recv0[:, RA0:], _rows_b0(e_w_q_up[0], e_w_kv_up[0]), _rows_b0(m_e_w_q_up[0], m_e_w_kv_up[0]),
                _rows_b0(v_e_w_q_up[0], v_e_w_kv_up[0]), name="adamw_early")
    b1 = _adamw(gr["recv1"][:, RA1:], _rows_b1(o_w_out[0], e_w_out[0], o_g_in),
                _rows_b1(m_o_w_out[0], m_e_w_out[0], m_o_g_in), _rows_b1(v_o_w_out[0], v_e_w_out[0], v_o_g_in),
                name="adamw_late")

    def sharded(k):
        q_up, kv_up = _unflat_b0(b0[k])
        o_out, e_out, o_g = _unflat_b1(b1[k])
        return e_in[k], q_up, kv_up, e_out, o_in[k], o_out, o_g

    g_sh, d_sh, m_sh, v_sh = [sharded(k) for k in range(4)]

    small = _small_pack(gr["e_g_in"], gr["g_final"], gr["e_g_q_a"], gr["e_g_kv_a"], gr["e_sinks"], gr["o_b_f"], gr["loss"])
    small_all = _all_gather(small, name="small_all_gather")
    zero = jnp.zeros((), F32)
    w_small = _small_pack(e_g_in, g_final, e_g_q_a, e_g_kv_a, e_sinks, o_b_f, zero)
    m_small = _small_pack(m_e_g_in, m_g_final, m_e_g_q_a, m_e_g_kv_a, m_e_sinks, m_o_b_f, zero)
    v_small = _small_pack(v_e_g_in, v_g_final, v_e_g_q_a, v_e_g_kv_a, v_e_sinks, v_o_b_f, zero)
    smalls = _adamw(small_all, w_small, m_small, v_small, name="adamw_replicated")
    g_sm, d_sm, m_sm, v_sm = [_small_unpack(a) for a in smalls]
    loss = g_sm[6]

    def leaves(sh, sm):
        return (sm[0], sh[0], sm[2], sh[1], sm[3], sh[2], sm[4], sh[3], sh[6], sh[4], sm[5], sh[5], sm[1])

    return (loss, gr["grad_x"][None], *leaves(g_sh, g_sm), *leaves(d_sh, d_sm), *leaves(m_sh, m_sm), *leaves(v_sh, v_sm))
```

```python
import functools

import jax
import jax.numpy as jnp
from jax import lax
from jax.experimental import pallas as pl
from jax.experimental.pallas import tpu as pltpu

F32 = jnp.float32
BF16 = jnp.bfloat16
NEG_INF = float("-inf")

N_DEV = 8
LANES = 128
D_MODEL = 1024
EPS = 1e-6
ROPE_THETA = 10000.0
MLA_HEADS = 8
MLA_Q_RANK = 256
MLA_KV_RANK = 128
MLA_NOPE = 64
MLA_ROPE = 32
MLA_V = 64
SWA_HEADS = 8
SWA_KV_HEADS = 2
SWA_DIM = 64
WINDOW = 128
FOX_HEADS = 16
FOX_DIM = 64

ADAM_LR = 0.001
ADAM_B1 = 0.9
ADAM_B2 = 0.999
ADAM_EPS = 1e-08
ADAM_WD = 0.01
ADAM_STEP = 10

ATT_T = 512
VMEM_LIMIT = 56 * 1024 * 1024
MATMUL_B_BLOCK_BYTES = 8 * 1024 * 1024

Z0A_UNITS = 12
Z0B_UNITS = 6

WIDE = 1024
N_E_IN = 276
N_O_IN = 514
RA0 = 288
RB0 = 32 + 16
RA1 = 528
RB1 = 128 + 128 + 16
SMALL_ROWS = 24


def _tile(n, cands):
    for c in cands:
        if n % c == 0:
            return c
    raise ValueError(f"no tile for {n}")


def _params(sem, vmem=None):
    return pltpu.CompilerParams(dimension_semantics=sem, vmem_limit_bytes=vmem)


def _matmul(a, b, *, name, ta=False, tb=False, add=None, out_dtype=F32, b_rows=None):
    if ta:
        kdim, m = a.shape
    else:
        m, kdim = a.shape
    if tb:
        n, kb = b.shape
    else:
        kb, n = b.shape
    assert kdim == kb, (a.shape, b.shape)
    b_start = 0
    if b_rows is not None:
        assert tb
        b_start, n = b_rows
    tm = _tile(m, (512, 256, 128))
    tn = _tile(n, [c for c in (1024, 768, 512, 384, 256, 128)
                   if c * kdim * b.dtype.itemsize <= MATMUL_B_BLOCK_BYTES and b_start % c == 0])
    assert b_start % tn == 0, (b_start, tn)
    b_off = b_start // tn
    dims = (((0 if ta else 1,), (1 if tb else 0,)), ((), ()))

    def body(*refs):
        if add is None:
            a_ref, b_ref, o_ref = refs
            add_ref = None
        else:
            a_ref, b_ref, add_ref, o_ref = refs
        r = lax.dot_general(a_ref[...].astype(BF16), b_ref[...].astype(BF16), dims, preferred_element_type=F32)
        if add_ref is not None:
            r = r + add_ref[...]
        o_ref[...] = r.astype(out_dtype)

    a_spec = pl.BlockSpec((kdim, tm), lambda i, j: (0, i)) if ta else pl.BlockSpec((tm, kdim), lambda i, j: (i, 0))
    b_spec = pl.BlockSpec((tn, kdim), lambda i, j: (j + b_off, 0)) if tb else pl.BlockSpec((kdim, tn), lambda i, j: (0, j))
    in_specs = [a_spec, b_spec]
    args = [a, b]
    if add is not None:
        in_specs.append(pl.BlockSpec((tm, tn), lambda i, j: (i, j)))
        args.append(add)
    return pl.pallas_call(
        body, name=name, grid=(m // tm, n // tn),
        in_specs=in_specs, out_specs=pl.BlockSpec((tm, tn), lambda i, j: (i, j)),
        out_shape=jax.ShapeDtypeStruct((m, n), out_dtype),
        compiler_params=_params(("parallel", "parallel"), VMEM_LIMIT),
    )(*args)


def _rmsnorm_fwd(x, g, *, width, col_blk, name):
    s = x.shape[0]
    tm = _tile(s, (256, 128))

    def body(x_ref, g_ref, y_ref):
        xf = x_ref[...].astype(F32)
        r = lax.rsqrt(jnp.mean(xf * xf, axis=-1, keepdims=True) + EPS)
        y_ref[...] = ((xf * r) * g_ref[...]).astype(BF16)

    return pl.pallas_call(
        body, name=name, grid=(s // tm,),
        in_specs=[pl.BlockSpec((tm, width), lambda i: (i, col_blk)), pl.BlockSpec((1, width), lambda i: (0, 0))],
        out_specs=pl.BlockSpec((tm, width), lambda i: (i, 0)),
        out_shape=jax.ShapeDtypeStruct((s, width), BF16),
        compiler_params=_params(("parallel",)),
    )(x, g)


def _rmsnorm_bwd(x, g, dy, *, width, col_blk, name, add=None, out_dtype=F32):
    s = x.shape[0]
    tm = _tile(s, (256, 128))

    def body(*refs):
        if add is None:
            x_ref, g_ref, dy_ref, dx_ref, dg_ref = refs
            add_ref = None
        else:
            x_ref, g_ref, dy_ref, add_ref, dx_ref, dg_ref = refs
        i = pl.program_id(0)
        xf = x_ref[...].astype(F32)
        r = lax.rsqrt(jnp.mean(xf * xf, axis=-1, keepdims=True) + EPS)
        xh = xf * r
        dyf = dy_ref[...].astype(F32)

        @pl.when(i == 0)
        def _():
            dg_ref[...] = jnp.zeros_like(dg_ref)

        dg_ref[...] += jnp.sum(dyf * xh, axis=0, keepdims=True)
        dxh = dyf * g_ref[...]
        dx = r * (dxh - xh * jnp.mean(dxh * xh, axis=-1, keepdims=True))
        if add_ref is not None:
            dx = dx + add_ref[...]
        dx_ref[...] = dx.astype(out_dtype)

    in_specs = [pl.BlockSpec((tm, width), lambda i: (i, col_blk)), pl.BlockSpec((1, width), lambda i: (0, 0)),
                pl.BlockSpec((tm, width), lambda i: (i, 0))]
    args = [x, g, dy]
    if add is not None:
        in_specs.append(pl.BlockSpec((tm, width), lambda i: (i, 0)))
        args.append(add)
    return pl.pallas_call(
        body, name=name, grid=(s // tm,),
        in_specs=in_specs,
        out_specs=[pl.BlockSpec((tm, width), lambda i: (i, 0)), pl.BlockSpec((1, width), lambda i: (0, 0))],
        out_shape=[jax.ShapeDtypeStruct((s, width), out_dtype), jax.ShapeDtypeStruct((1, width), F32)],
        compiler_params=_params(("arbitrary",)),
    )(*args)


def _sigmoid(x):
    return 1.0 / (1.0 + jnp.exp(-x))


def _gate_fwd(o_parts, gate, *, name):
    s = gate.shape[0]
    tm = _tile(s, (256, 128))
    n_o = len(o_parts)

    def body(*refs):
        o_refs, g_ref, y_ref = refs[:n_o], refs[n_o], refs[n_o + 1]
        o = o_refs[0][...] if n_o == 1 else jnp.concatenate([r[...] for r in o_refs], axis=1)
        gt = g_ref[...]
        y_ref[...] = (o * (gt * _sigmoid(gt))).astype(BF16)

    in_specs = [pl.BlockSpec((tm, o.shape[1]), lambda i: (i, 0)) for o in o_parts]
    in_specs.append(pl.BlockSpec((tm, D_MODEL), lambda i: (i, 0)))
    return pl.pallas_call(
        body, name=name, grid=(s // tm,), in_specs=in_specs,
        out_specs=pl.BlockSpec((tm, D_MODEL), lambda i: (i, 0)),
        out_shape=jax.ShapeDtypeStruct((s, D_MODEL), BF16),
        compiler_params=_params(("parallel",)),
    )(*o_parts, gate)


def _gate_bwd(d_og, o_parts, gate, *, name):
    s = gate.shape[0]
    tm = _tile(s, (256, 128))
    n_o = len(o_parts)
    widths = [o.shape[1] for o in o_parts]

    def body(*refs):
        d_ref, o_refs, g_ref = refs[0], refs[1:1 + n_o], refs[1 + n_o]
        do_refs, dg_ref = refs[2 + n_o:2 + 2 * n_o], refs[2 + 2 * n_o]
        d = d_ref[...]
        gt = g_ref[...]
        sg = _sigmoid(gt)
        silu = gt * sg
        dsilu = sg * (1.0 + gt * (1.0 - sg))
        o = o_refs[0][...] if n_o == 1 else jnp.concatenate([r[...] for r in o_refs], axis=1)
        dg_ref[...] = (d * o * dsilu).astype(BF16)
        do = d * silu
        off = 0
        for r, w in zip(do_refs, widths):
            r[...] = do[:, off:off + w]
            off += w

    in_specs = [pl.BlockSpec((tm, D_MODEL), lambda i: (i, 0))]
    in_specs += [pl.BlockSpec((tm, w), lambda i: (i, 0)) for w in widths]
    in_specs.append(pl.BlockSpec((tm, D_MODEL), lambda i: (i, 0)))
    out_specs = [pl.BlockSpec((tm, w), lambda i: (i, 0)) for w in widths]
    out_specs.append(pl.BlockSpec((tm, D_MODEL), lambda i: (i, 0)))
    out_shape = [jax.ShapeDtypeStruct((s, w), F32) for w in widths]
    out_shape.append(jax.ShapeDtypeStruct((s, D_MODEL), BF16))
    return pl.pallas_call(
        body, name=name, grid=(s // tm,), in_specs=in_specs, out_specs=out_specs, out_shape=out_shape,
        compiler_params=_params(("parallel",)),
    )(d_og, *o_parts, gate)


def _rot_half(x):
    lane = lax.broadcasted_iota(jnp.int32, x.shape, 1)
    return jnp.where(lane < 80, pltpu.roll(x, LANES - 16, axis=1), pltpu.roll(x, 16, axis=1))


def _rot_half_t(g):
    lane = lax.broadcasted_iota(jnp.int32, g.shape, 1)
    lo = (lane >= MLA_NOPE) & (lane < MLA_NOPE + MLA_ROPE // 2)
    hi = (lane >= MLA_NOPE + MLA_ROPE // 2) & (lane < MLA_NOPE + MLA_ROPE)
    return jnp.where(lo, pltpu.roll(g, LANES - 16, axis=1), jnp.where(hi, pltpu.roll(g, 16, axis=1), 0.0))


def _rope_fwd(qp, kvp, z0a, cos_t, sin_t, *, name):
    s = qp.shape[0]
    tm = _tile(s, (256, 128))
    hw = MLA_HEADS * LANES

    def body(q_ref, k_ref, kpe_ref, c_ref, s_ref, qm_ref, km_ref):
        c = c_ref[...]
        sn = s_ref[...]
        kpe = kpe_ref[...]
        kpe_r = (kpe * c + _rot_half(kpe) * sn).astype(BF16)
        lane = lax.broadcasted_iota(jnp.int32, kpe.shape, 1)
        for h in range(MLA_HEADS):
            sl = slice(h * LANES, (h + 1) * LANES)
            qh = q_ref[:, sl]
            qm_ref[:, sl] = (qh * c + _rot_half(qh) * sn).astype(BF16)
            km_ref[:, sl] = jnp.where(lane < MLA_NOPE, k_ref[:, sl], kpe_r)

    return pl.pallas_call(
        body, name=name, grid=(s // tm,),
        in_specs=[pl.BlockSpec((tm, hw), lambda i: (i, 0)), pl.BlockSpec((tm, hw), lambda i: (i, 0)),
                  pl.BlockSpec((tm, LANES), lambda i: (i, 11)),
                  pl.BlockSpec((tm, LANES), lambda i: (i, 0)), pl.BlockSpec((tm, LANES), lambda i: (i, 0))],
        out_specs=[pl.BlockSpec((tm, hw), lambda i: (i, 0)), pl.BlockSpec((tm, hw), lambda i: (i, 0))],
        out_shape=[jax.ShapeDtypeStruct((s, hw), BF16), jax.ShapeDtypeStruct((s, hw), BF16)],
        compiler_params=_params(("parallel",)),
    )(qp, kvp, z0a, cos_t, sin_t)


def _rope_bwd(dqm, dkm, dvm, cos_t, sin_t, *, name):
    s = dqm.shape[0]
    tm = _tile(s, (256, 128))
    hw = MLA_HEADS * LANES
    vw = MLA_HEADS * MLA_V

    def body(dq_ref, dk_ref, dv_ref, c_ref, s_ref, dqp_ref, dkv_ref, dkpe_ref):
        c = c_ref[...]
        sn = s_ref[...]
        ksum = jnp.zeros((tm, LANES), F32)
        for h in range(MLA_HEADS):
            sl = slice(h * LANES, (h + 1) * LANES)
            dq = dq_ref[:, sl]
            dqp_ref[:, sl] = (dq * c + _rot_half_t(dq * sn)).astype(BF16)
            dk = dk_ref[:, sl]
            dkv_ref[:, sl] = dk.astype(BF16)
            ksum = ksum + dk
        dkv_ref[:, hw:] = dv_ref[...]
        lane = lax.broadcasted_iota(jnp.int32, ksum.shape, 1)
        dkpe = ksum * c + _rot_half_t(ksum * sn)
        dkpe_ref[...] = jnp.where((lane >= MLA_NOPE) & (lane < MLA_NOPE + MLA_ROPE), dkpe, 0.0).astype(BF16)

    return pl.pallas_call(
        body, name=name, grid=(s // tm,),
        in_specs=[pl.BlockSpec((tm, hw), lambda i: (i, 0)), pl.BlockSpec((tm, hw), lambda i: (i, 0)),
                  pl.BlockSpec((tm, vw), lambda i: (i, 0)),
                  pl.BlockSpec((tm, LANES), lambda i: (i, 0)), pl.BlockSpec((tm, LANES), lambda i: (i, 0))],
        out_specs=[pl.BlockSpec((tm, hw), lambda i: (i, 0)), pl.BlockSpec((tm, hw + vw), lambda i: (i, 0)),
                   pl.BlockSpec((tm, LANES), lambda i: (i, 0))],
        out_shape=[jax.ShapeDtypeStruct((s, hw), BF16), jax.ShapeDtypeStruct((s, hw + vw), BF16),
                   jax.ShapeDtypeStruct((s, LANES), BF16)],
        compiler_params=_params(("parallel",)),
    )(dqm, dkm, dvm, cos_t, sin_t)


def _head_mask(shape, a):
    lane = lax.broadcasted_iota(jnp.int32, shape, 1)
    return (lane >= 64 * a) & (lane < 64 * (a + 1))


def _causal_mask(t):
    row = lax.broadcasted_iota(jnp.int32, (t, t), 0)
    col = lax.broadcasted_iota(jnp.int32, (t, t), 1)
    return col <= row


_NT = (((1,), (1,)), ((), ()))
LOG2E = 1.4426950408889634


def _stack_heads(tile, hw):
    lane = lax.broadcasted_iota(jnp.int32, tile.shape, 1)
    z = jnp.zeros_like(tile)
    return jnp.concatenate([jnp.where(lane < hw, tile, z), jnp.where(lane >= hw, tile, z)], axis=0)


def _stacked_rows(r0, r1, t):
    n = r0.shape[-1]
    return jnp.concatenate([jnp.broadcast_to(r0, (t, n)), jnp.broadcast_to(r1, (t, n))], axis=0)


def _stacked_causal_mask(t):
    m = _causal_mask(t)
    return jnp.concatenate([m, m], axis=0)


def _resident(block, index_map):
    return pl.BlockSpec(block, index_map, pipeline_mode=pl.Buffered(1))


def _flash_fwd(q, k, v, bias, *, n_pairs, hw, q_off, k_off, v_off, scale, name, rider=None):
    s = q.shape[0]
    t = min(ATT_T, s)
    nb = s // t
    qw = 2 * hw
    has_bias = bias is not None
    c1 = scale * LOG2E

    def body(*refs):
        refs, ride_refs = _split_rider(refs, rider, n_in=4 if has_bias else 3, n_out=2)
        if has_bias:
            q_ref, k_ref, v_ref, b_ref, o_ref, lse_ref, vt_ref, bcol_ref = refs
        else:
            q_ref, k_ref, v_ref, o_ref, lse_ref, vt_ref = refs
            b_ref = bcol_ref = None
        _ride_start(rider, ride_refs, pl.program_id(0) == 0)
        row = lax.broadcasted_iota(jnp.int32, (t, t), 0)
        col = lax.broadcasted_iota(jnp.int32, (t, t), 1)
        cmask_t = jnp.concatenate([row <= col, row <= col], axis=1)
        lane_lt64 = lax.broadcasted_iota(jnp.int32, (t, LANES), 1) < 64

        def as_column(r):
            return jnp.broadcast_to(r, (8, r.shape[1])).T[:, 0:1]

        def v_block(j, _):
            c0 = pl.multiple_of(j * t, t)
            vt_ref[j] = v_ref[pl.ds(c0, t), :].astype(F32).T.astype(BF16)
            if has_bias:
                for a in range(2):
                    bcol_ref[a, pl.ds(c0, t), :] = as_column(b_ref[0, a, j])
            return 0

        lax.fori_loop(0, nb, v_block, 0)

        def stacked_queries(i):
            return _stack_heads(q_ref[pl.ds(pl.multiple_of(i * t, t), t), :], hw).astype(F32).T.astype(BF16)

        def kv_step(j, carry, qs_t, masked):
            m, l, acc = carry
            rows = pl.ds(pl.multiple_of(j * t, t), t)
            sc = jnp.dot(k_ref[rows, :], qs_t, preferred_element_type=F32) * c1
            if has_bias:
                sc = sc + jnp.concatenate([jnp.broadcast_to(bcol_ref[0, rows, :], (t, t)),
                                           jnp.broadcast_to(bcol_ref[1, rows, :], (t, t))], axis=1)
            if masked:
                sc = jnp.where(cmask_t, sc, NEG_INF)
            m_new = jnp.maximum(m, jnp.max(sc, axis=0, keepdims=True))
            alpha = jnp.exp2(m - m_new)
            p = jnp.exp2(sc - m_new)
            l_new = alpha * l + jnp.sum(p, axis=0, keepdims=True)
            pv = jnp.dot(vt_ref[j], p.astype(BF16), preferred_element_type=F32)
            return m_new, l_new, alpha * acc + pv

        def finish(i, carry):
            m, l, acc = carry
            r0 = pl.multiple_of(i * t, t)
            out = (acc / l).T
            lse2 = as_column(m + jnp.log2(l))
            lse_ref[0, 0, pl.ds(r0, t), :] = lse2[:t]
            lse_ref[0, 1, pl.ds(r0, t), :] = lse2[t:]
            o_ref[pl.ds(r0, t), :] = jnp.where(lane_lt64, out[:t], out[t:])

        init = (jnp.full((1, 2 * t), NEG_INF, F32), jnp.zeros((1, 2 * t), F32), jnp.zeros((LANES, 2 * t), F32))

        def q_block(i, _):
            qs_t = stacked_queries(i)
            carry = lax.fori_loop(0, i, lambda j, c: kv_step(j, c, qs_t, False), init)
            finish(i, kv_step(i, carry, qs_t, True))
            return 0

        lax.fori_loop(0, nb, q_block, 0)
        _ride_wait(rider, ride_refs, pl.program_id(0) == n_pairs - 1)

    in_specs = [_resident((s, qw), lambda p: (0, q_off + p)), _resident((s, qw), lambda p: (0, k_off + p)),
                _resident((s, LANES), lambda p: (0, v_off + p))]
    args = [q, k, v]
    if has_bias:
        in_specs.append(_resident((1, 2, nb, 1, t), lambda p: (p, 0, 0, 0, 0)))
        args.append(bias)
    out_specs = [pl.BlockSpec((s, LANES), lambda p: (0, p)), pl.BlockSpec((1, 2, s, 1), lambda p: (p, 0, 0, 0))]
    out_shape = [jax.ShapeDtypeStruct((s, n_pairs * LANES), F32), jax.ShapeDtypeStruct((n_pairs, 2, s, 1), F32)]
    scratch = [pltpu.VMEM((nb, LANES, t), BF16)] + ([pltpu.VMEM((2, s, 1), F32)] if has_bias else [])
    scratch += _add_rider(rider, in_specs, args, out_specs, out_shape)
    return pl.pallas_call(
        body, name=name, grid=(n_pairs,), in_specs=in_specs, out_specs=out_specs, out_shape=out_shape,
        scratch_shapes=scratch,
        compiler_params=_params(("parallel",) if rider is None else ("arbitrary",), VMEM_LIMIT),
    )(*args)


def _flash_bwd(q, k, v, do, o, lse, bias, *, n_pairs, hw, q_off, k_off, v_off, scale, qk_dtype, name, rider=None):
    s = q.shape[0]
    t = min(ATT_T, s)
    nb = s // t
    qw = 2 * hw
    has_bias = bias is not None
    c1 = scale * LOG2E

    def body(*refs):
        refs, ride_refs = _split_rider(refs, rider, n_in=7 if has_bias else 6, n_out=5 if has_bias else 3)
        if has_bias:
            (q_ref, k_ref, v_ref, do_ref, o_ref, lse_ref, b_ref, dq_ref, dk_ref, dv_ref, db_ref, dr_ref,
             dkt_ref, dvt_ref) = refs
            db_ref[...] = jnp.zeros_like(db_ref)
        else:
            q_ref, k_ref, v_ref, do_ref, o_ref, lse_ref, dq_ref, dk_ref, dv_ref, dkt_ref, dvt_ref = refs
            b_ref = db_ref = dr_ref = None
        _ride_start(rider, ride_refs, pl.program_id(0) == 0)
        dkt_ref[...] = jnp.zeros_like(dkt_ref)
        dvt_ref[...] = jnp.zeros_like(dvt_ref)
        cmask = _stacked_causal_mask(t)
        lane_lt_hw = lax.broadcasted_iota(jnp.int32, (t, qw), 1) < hw

        def q_block(i, _):
            r0 = pl.multiple_of(i * t, t)
            qs = _stack_heads(q_ref[pl.ds(r0, t), :], hw)
            dos = _stack_heads(do_ref[pl.ds(r0, t), :], 64)
            ot = o_ref[pl.ds(r0, t), :]
            delta = jnp.sum(dos * jnp.concatenate([ot, ot], axis=0), axis=-1, keepdims=True)
            lse2 = jnp.concatenate([lse_ref[0, 0, pl.ds(r0, t), :], lse_ref[0, 1, pl.ds(r0, t), :]], axis=0)
            dosb = dos.astype(BF16)
            dos_t = dos.T.astype(BF16)
            qs_t = qs.astype(F32).T.astype(BF16)

            def kv_step(j, carry, masked):
                dq, rsum = carry
                c0 = pl.multiple_of(j * t, t)
                kt = k_ref[pl.ds(c0, t), :]
                vt = v_ref[pl.ds(c0, t), :]
                sc = lax.dot_general(qs, kt, _NT, preferred_element_type=F32) * c1
                if has_bias:
                    sc = sc + _stacked_rows(b_ref[0, 0, j], b_ref[0, 1, j], t)
                if masked:
                    sc = jnp.where(cmask, sc, NEG_INF)
                p = jnp.exp2(sc - lse2)
                dp = lax.dot_general(dosb, vt, _NT, preferred_element_type=F32)
                ds = p * (dp - delta)
                dsb = ds.astype(BF16)
                pb = p.astype(BF16)
                dvt_ref[j] += jnp.concatenate(
                    [jnp.dot(dos_t[:64, :t], pb[:t], preferred_element_type=F32),
                     jnp.dot(dos_t[64:, t:], pb[t:], preferred_element_type=F32)], axis=0)
                dkt_ref[j] += jnp.concatenate(
                    [jnp.dot(qs_t[:hw, :t], dsb[:t], preferred_element_type=F32),
                     jnp.dot(qs_t[hw:, t:], dsb[t:], preferred_element_type=F32)], axis=0)
                if has_bias:
                    db_ref[0, 0, j] += jnp.sum(ds[:t], axis=0, keepdims=True)
                    db_ref[0, 1, j] += jnp.sum(ds[t:], axis=0, keepdims=True)
                    rsum = rsum + jnp.sum(ds, axis=-1, keepdims=True)
                return dq + jnp.dot(dsb, kt, preferred_element_type=F32), rsum

            init = (jnp.zeros((2 * t, qw), F32), jnp.zeros((2 * t, 1), F32))
            carry = lax.fori_loop(0, i, functools.partial(kv_step, masked=False), init)
            dq, rsum = kv_step(i, carry, True)
            dq = dq * scale
            dq_ref[pl.ds(r0, t), :] = jnp.where(lane_lt_hw, dq[:t], dq[t:]).astype(qk_dtype)
            if has_bias:
                rsum_row = jnp.broadcast_to(rsum, (2 * t, LANES)).T[0:1]
                dr_ref[0, 0, i] = rsum_row[:, :t]
                dr_ref[0, 1, i] = rsum_row[:, t:]
            return 0

        lax.fori_loop(0, nb, q_block, 0)

        def k_block(j, _):
            c0 = pl.multiple_of(j * t, t)
            dk_ref[pl.ds(c0, t), :] = (dkt_ref[j].T * scale).astype(qk_dtype)
            dv_ref[pl.ds(c0, t), :] = dvt_ref[j].T.astype(BF16)
            return 0

        lax.fori_loop(0, nb, k_block, 0)
        _ride_wait(rider, ride_refs, pl.program_id(0) == n_pairs - 1)

    in_specs = [_resident((s, qw), lambda p: (0, q_off + p)), _resident((s, qw), lambda p: (0, k_off + p)),
                _resident((s, LANES), lambda p: (0, v_off + p)),
                _resident((s, LANES), lambda p: (0, p)), _resident((s, LANES), lambda p: (0, p)),
                _resident((1, 2, s, 1), lambda p: (p, 0, 0, 0))]
    args = [q, k, v, do, o, lse]
    out_specs = [pl.BlockSpec((s, qw), lambda p: (0, p)), pl.BlockSpec((s, qw), lambda p: (0, p)),
                 pl.BlockSpec((s, LANES), lambda p: (0, p))]
    out_shape = [jax.ShapeDtypeStruct((s, n_pairs * qw), qk_dtype), jax.ShapeDtypeStruct((s, n_pairs * qw), qk_dtype),
                 jax.ShapeDtypeStruct((s, n_pairs * LANES), BF16)]
    if has_bias:
        in_specs.append(_resident((1, 2, nb, 1, t), lambda p: (p, 0, 0, 0, 0)))
        args.append(bias)
        out_specs.append(pl.BlockSpec((1, 2, nb, 1, t), lambda p: (p, 0, 0, 0, 0)))
        out_shape.append(jax.ShapeDtypeStruct((n_pairs, 2, nb, 1, t), F32))
        out_specs.append(pl.BlockSpec((1, 2, nb, 1, t), lambda p: (p, 0, 0, 0, 0)))
        out_shape.append(jax.ShapeDtypeStruct((n_pairs, 2, nb, 1, t), F32))
    scratch = [pltpu.VMEM((nb, qw, t), F32), pltpu.VMEM((nb, LANES, t), F32)]
    scratch += _add_rider(rider, in_specs, args, out_specs, out_shape)
    return pl.pallas_call(
        body, name=name, grid=(n_pairs,), in_specs=in_specs, out_specs=out_specs, out_shape=out_shape,
        scratch_shapes=scratch,
        compiler_params=_params(("parallel",) if rider is None else ("arbitrary",), VMEM_LIMIT),
    )(*args)


def _alibi_slope(h):
    return 2.0 ** (-8.0 * (h + 1.0) / SWA_HEADS)


SWA_ROWS = 512
SWA_SCALE = SWA_DIM ** -0.5


def _swa_geometry(i):
    w = WINDOW
    r0 = pl.multiple_of(i * w, w)
    b0 = pl.multiple_of(jnp.maximum(i - 1, 0) * w, w)
    row = lax.broadcasted_iota(jnp.int32, (w, 2 * w), 0)
    col = lax.broadcasted_iota(jnp.int32, (w, 2 * w), 1)
    dist = row - col + (r0 - b0)
    valid = (dist >= 0) & (dist < w)
    return r0, b0, dist.astype(F32), valid


def _swa_q_head(qblk, h):
    kv = h // (SWA_HEADS // SWA_KV_HEADS)
    if h % 2 != kv:
        qblk = pltpu.roll(qblk, 64, axis=1)
    return jnp.where(_head_mask(qblk.shape, kv), qblk, 0.0)


SWA_GROUP = SWA_HEADS // SWA_KV_HEADS


def _swa_stack(ref, rs, grp):
    parts = []
    for a in range(SWA_GROUP):
        h = SWA_GROUP * grp + a
        parts.append(_swa_q_head(ref[rs, (h // 2) * LANES:(h // 2 + 1) * LANES].astype(F32), h))
    return jnp.concatenate(parts, axis=0)


def _swa_unstack(x, grp):
    tiles = []
    for a in range(SWA_GROUP):
        h = SWA_GROUP * grp + a
        tile = x[a * WINDOW:(a + 1) * WINDOW]
        tiles.append(pltpu.roll(tile, 64, axis=1) if h % 2 != grp else tile)
    return tiles


def _swa_head_column(vals):
    return jnp.concatenate([jnp.full((WINDOW, 1), v, F32) for v in vals], axis=0)


def _swa_logits(qs, kb, dist, valid, grp):
    slopes = _swa_head_column([_alibi_slope(SWA_GROUP * grp + a) for a in range(SWA_GROUP)])
    dist4 = jnp.concatenate([dist] * SWA_GROUP, axis=0)
    valid4 = jnp.concatenate([valid] * SWA_GROUP, axis=0)
    sc = lax.dot_general(qs, kb, _NT, preferred_element_type=F32) * SWA_SCALE - slopes * dist4
    return jnp.where(valid4, sc, NEG_INF)


def _swa_merge_heads(tiles):
    lt64 = lax.broadcasted_iota(jnp.int32, (WINDOW, LANES), 1) < 64
    return jnp.concatenate([jnp.where(lt64, tiles[2 * b], tiles[2 * b + 1]) for b in range(SWA_HEADS // 2)], axis=1)


def _swa_fwd(z0b, sinks, *, name):
    s = z0b.shape[0]
    w = WINDOW
    rows = min(SWA_ROWS, s)
    per_step = rows // w
    qcols = SWA_HEADS * SWA_DIM

    def body(sink_ref, q_ref, k_ref, v_ref, o_ref, lse_ref):
        g = pl.program_id(0)
        for ii in range(per_step):
            rs = slice(ii * w, (ii + 1) * w)
            r0, b0, dist, valid = _swa_geometry(g * per_step + ii)
            kb = k_ref[pl.ds(b0, 2 * w), :]
            vb = v_ref[pl.ds(b0, 2 * w), :]
            o_tiles = []
            for h in range(SWA_HEADS):
                kv = h // SWA_GROUP
                qh = _swa_q_head(q_ref[rs, (h // 2) * LANES:(h // 2 + 1) * LANES].astype(F32), h).astype(BF16)
                sc = lax.dot_general(qh, kb, _NT, preferred_element_type=F32) * SWA_SCALE - _alibi_slope(h) * dist
                sc = jnp.where(valid, sc, NEG_INF)
                sink = sink_ref[0, h]
                m = jnp.maximum(jnp.max(sc, axis=-1, keepdims=True), sink)
                p = jnp.exp(sc - m)
                l = jnp.sum(p, axis=-1, keepdims=True) + jnp.exp(sink - m)
                oh = jnp.dot(p.astype(BF16), vb, preferred_element_type=F32) / l
                o_tiles.append(pltpu.roll(oh, 64, axis=1) if h % 2 != kv else oh)
                lse_ref[h, rs, :] = m + jnp.log(l)
            o_ref[rs, :] = _swa_merge_heads(o_tiles)

    return pl.pallas_call(
        body, name=name, grid=(s // rows,),
        in_specs=[pl.BlockSpec(memory_space=pltpu.SMEM),
                  pl.BlockSpec((rows, qcols), lambda g: (g, 0)),
                  pl.BlockSpec((s, LANES), lambda g: (0, 4)), pl.BlockSpec((s, LANES), lambda g: (0, 5))],
        out_specs=[pl.BlockSpec((rows, qcols), lambda g: (g, 0)), pl.BlockSpec((SWA_HEADS, rows, 1), lambda g: (0, g, 0))],
        out_shape=[jax.ShapeDtypeStruct((s, qcols), F32), jax.ShapeDtypeStruct((SWA_HEADS, s, 1), F32)],
        compiler_params=_params(("parallel",), VMEM_LIMIT),
    )(sinks, z0b, z0b, z0b)


def _swa_bwd(z0b, sinks, do, o, lse, *, name):
    s = z0b.shape[0]
    w = WINDOW
    rows = min(SWA_ROWS, s)
    per_step = rows // w
    qcols = SWA_HEADS * SWA_DIM
    nblk = s // w

    def body(sink_ref, q_ref, k_ref, v_ref, do_ref, o_ref, lse_ref, dq_ref, dkt_ref, dvt_ref, dsink_ref):
        g = pl.program_id(0)

        @pl.when(g == 0)
        def _():
            dkt_ref[...] = jnp.zeros_like(dkt_ref)
            dvt_ref[...] = jnp.zeros_like(dvt_ref)
            dsink_ref[...] = jnp.zeros_like(dsink_ref)

        for ii in range(per_step):
            i = g * per_step + ii
            rs = slice(ii * w, (ii + 1) * w)
            r0, b0, dist, valid = _swa_geometry(i)
            j0 = jnp.maximum(i - 1, 0)
            kb = k_ref[pl.ds(b0, 2 * w), :]
            vb = v_ref[pl.ds(b0, 2 * w), :]
            dq_tiles = []
            for grp in range(SWA_KV_HEADS):
                heads = [SWA_GROUP * grp + a for a in range(SWA_GROUP)]
                qs32 = _swa_stack(q_ref, rs, grp)
                dos32 = _swa_stack(do_ref, rs, grp)
                delta = jnp.sum(dos32 * _swa_stack(o_ref, rs, grp), axis=-1, keepdims=True)
                lse = jnp.concatenate([lse_ref[h, rs, :] for h in heads], axis=0)
                sink = _swa_head_column([sink_ref[0, h] for h in heads])
                p = jnp.exp(_swa_logits(qs32.astype(BF16), kb, dist, valid, grp) - lse)
                dp = lax.dot_general(dos32.astype(BF16), vb, _NT, preferred_element_type=F32)
                ds = p * (dp - delta)
                dsb = ds.astype(BF16)
                d_sink = jnp.exp(sink - lse) * delta
                for a, h in enumerate(heads):
                    dsink_ref[h:h + 1, :] += jnp.broadcast_to(-jnp.sum(d_sink[a * w:(a + 1) * w]), (1, LANES))
                dvt = jnp.dot(dos32.T.astype(BF16), p.astype(BF16), preferred_element_type=F32)
                dkt = jnp.dot(qs32.T.astype(BF16), dsb, preferred_element_type=F32) * SWA_SCALE
                dvt_ref[j0] += dvt[:, :w]
                dvt_ref[j0 + 1] += dvt[:, w:]
                dkt_ref[j0] += dkt[:, :w]
                dkt_ref[j0 + 1] += dkt[:, w:]
                dq_tiles += _swa_unstack(jnp.dot(dsb, kb, preferred_element_type=F32) * SWA_SCALE, grp)
            dq_ref[rs, :] = _swa_merge_heads(dq_tiles)

    return pl.pallas_call(
        body, name=name, grid=(s // rows,),
        in_specs=[pl.BlockSpec(memory_space=pltpu.SMEM),
                  pl.BlockSpec((rows, qcols), lambda g: (g, 0)),
                  pl.BlockSpec((s, LANES), lambda g: (0, 4)), pl.BlockSpec((s, LANES), lambda g: (0, 5)),
                  pl.BlockSpec((rows, qcols), lambda g: (g, 0)), pl.BlockSpec((rows, qcols), lambda g: (g, 0)),
                  pl.BlockSpec((SWA_HEADS, rows, 1), lambda g: (0, g, 0))],
        out_specs=[pl.BlockSpec((rows, qcols), lambda g: (g, 0)),
                   pl.BlockSpec((nblk, LANES, w), lambda g: (0, 0, 0)),
                   pl.BlockSpec((nblk, LANES, w), lambda g: (0, 0, 0)),
                   pl.BlockSpec((SWA_HEADS, LANES), lambda g: (0, 0))],
        out_shape=[jax.ShapeDtypeStruct((s, qcols), F32),
                   jax.ShapeDtypeStruct((nblk, LANES, w), F32), jax.ShapeDtypeStruct((nblk, LANES, w), F32),
                   jax.ShapeDtypeStruct((SWA_HEADS, LANES), F32)],
        compiler_params=_params(("arbitrary",), VMEM_LIMIT),
    )(sinks, z0b, z0b, z0b, do, o, lse)


CUM_T = 256


def _split3(x):
    hi = x.astype(BF16)
    r1 = x - hi.astype(F32)
    mid = r1.astype(BF16)
    lo = (r1 - mid.astype(F32)).astype(BF16)
    return hi, mid, lo


def _tri_dot(tri, x):
    hi, mid, lo = _split3(x)
    out = jnp.dot(tri, hi, preferred_element_type=F32)
    out = out + jnp.dot(tri, mid, preferred_element_type=F32)
    return out + jnp.dot(tri, lo, preferred_element_type=F32)


def _logf_fwd(zf, bf, *, name):
    s = zf.shape[0]
    t = CUM_T
    nb = s // t

    def body(z_ref, b_ref, c_ref, carry_ref):
        i = pl.program_id(0)

        @pl.when(i == 0)
        def _():
            carry_ref[...] = jnp.zeros_like(carry_ref)

        x = z_ref[...] + b_ref[...]
        lf = jnp.minimum(x, 0.0) - jnp.log(1.0 + jnp.exp(-jnp.abs(x)))
        row = lax.broadcasted_iota(jnp.int32, (t, t), 0)
        col = lax.broadcasted_iota(jnp.int32, (t, t), 1)
        tri = jnp.where(col <= row, 1.0, 0.0).astype(BF16)
        c = _tri_dot(tri, lf) + carry_ref[...]
        c_ref[...] = c
        carry_ref[...] = c[t - 1:t, :]

    return pl.pallas_call(
        body, name=name, grid=(nb,),
        in_specs=[pl.BlockSpec((t, LANES), lambda i: (i, 0)), pl.BlockSpec((1, LANES), lambda i: (0, 0))],
        out_specs=pl.BlockSpec((t, LANES), lambda i: (i, 0)),
        out_shape=jax.ShapeDtypeStruct((s, LANES), F32),
        scratch_shapes=[pltpu.VMEM((1, LANES), F32)],
        compiler_params=_params(("arbitrary",)),
    )(zf, bf)


def _logf_bwd(dc, zf, bf, *, name):
    s = zf.shape[0]
    t = CUM_T
    nb = s // t

    def body(dc_ref, z_ref, b_ref, dz_ref, db_ref, carry_ref):
        i = pl.program_id(0)

        @pl.when(i == 0)
        def _():
            carry_ref[...] = jnp.zeros_like(carry_ref)
            db_ref[...] = jnp.zeros_like(db_ref)

        row = lax.broadcasted_iota(jnp.int32, (t, t), 0)
        col = lax.broadcasted_iota(jnp.int32, (t, t), 1)
        tri = jnp.where(col >= row, 1.0, 0.0).astype(BF16)
        dlf = _tri_dot(tri, dc_ref[...]) + carry_ref[...]
        carry_ref[...] = dlf[0:1, :]
        x = z_ref[...] + b_ref[...]
        dz = dlf * _sigmoid(-x)
        dz_ref[...] = dz.astype(BF16)
        db_ref[...] += jnp.sum(dz, axis=0, keepdims=True)

    return pl.pallas_call(
        body, name=name, grid=(nb,),
        in_specs=[pl.BlockSpec((t, LANES), lambda i: (nb - 1 - i, 0)), pl.BlockSpec((t, LANES), lambda i: (nb - 1 - i, 0)),
                  pl.BlockSpec((1, LANES), lambda i: (0, 0))],
        out_specs=[pl.BlockSpec((t, LANES), lambda i: (nb - 1 - i, 0)), pl.BlockSpec((1, LANES), lambda i: (0, 0))],
        out_shape=[jax.ShapeDtypeStruct((s, LANES), BF16), jax.ShapeDtypeStruct((1, LANES), F32)],
        scratch_shapes=[pltpu.VMEM((1, LANES), F32)],
        compiler_params=_params(("arbitrary",)),
    )(dc, zf, bf)


def _loss_head(x2, g, target, *, name):
    s = x2.shape[0]
    tm = _tile(s, (256, 128))

    def body(x_ref, g_ref, t_ref, dx_ref, loss_ref, dg_ref):
        i = pl.program_id(0)

        @pl.when(i == 0)
        def _():
            loss_ref[...] = jnp.zeros_like(loss_ref)
            dg_ref[...] = jnp.zeros_like(dg_ref)

        xf = x_ref[...]
        r = lax.rsqrt(jnp.mean(xf * xf, axis=-1, keepdims=True) + EPS)
        xh = xf * r
        gv = g_ref[...]
        err = xh * gv - t_ref[...]
        loss_ref[...] += jnp.broadcast_to(0.5 * jnp.sum(jnp.mean(err * err, axis=-1, keepdims=True)), loss_ref.shape)
        dy = err * (1.0 / D_MODEL)
        dg_ref[...] += jnp.sum(dy * xh, axis=0, keepdims=True)
        dxh = dy * gv
        dx_ref[...] = r * (dxh - xh * jnp.mean(dxh * xh, axis=-1, keepdims=True))

    return pl.pallas_call(
        body, name=name, grid=(s // tm,),
        in_specs=[pl.BlockSpec((tm, D_MODEL), lambda i: (i, 0)), pl.BlockSpec((1, D_MODEL), lambda i: (0, 0)),
                  pl.BlockSpec((tm, D_MODEL), lambda i: (i, 0))],
        out_specs=[pl.BlockSpec((tm, D_MODEL), lambda i: (i, 0)), pl.BlockSpec((8, LANES), lambda i: (0, 0)),
                   pl.BlockSpec((1, D_MODEL), lambda i: (0, 0))],
        out_shape=[jax.ShapeDtypeStruct((s, D_MODEL), F32), jax.ShapeDtypeStruct((8, LANES), F32),
                   jax.ShapeDtypeStruct((1, D_MODEL), F32)],
        compiler_params=_params(("arbitrary",)),
    )(x2, g, target)


def _sum_pieces(p_ref):
    g = p_ref[0].astype(F32)
    for k in range(1, N_DEV):
        g = g + p_ref[k].astype(F32)
    return g


def _adam_update(g, w, m, v):
    bc1 = 1.0 - ADAM_B1 ** ADAM_STEP
    bc2 = 1.0 - ADAM_B2 ** ADAM_STEP
    nm = ADAM_B1 * m + (1.0 - ADAM_B1) * g
    nv = ADAM_B2 * v + (1.0 - ADAM_B2) * (g * g)
    m_hat = nm / bc1
    v_hat = nv / bc2
    return -ADAM_LR * (m_hat / (jnp.sqrt(v_hat) + ADAM_EPS) + ADAM_WD * w), nm, nv


def _adamw(pieces, w, m, v, *, name):
    rows, cols = w.shape
    tr = _tile(rows, (RB1, RB0, SMALL_ROWS))

    def body(p_ref, w_ref, m_ref, v_ref, g_ref, d_ref, nm_ref, nv_ref):
        g = _sum_pieces(p_ref)
        g_ref[...] = g
        d_ref[...], nm_ref[...], nv_ref[...] = _adam_update(g, w_ref[...], m_ref[...], v_ref[...])

    spec = pl.BlockSpec((tr, cols), lambda i: (i, 0))
    shape = jax.ShapeDtypeStruct((rows, cols), F32)
    return pl.pallas_call(
        body, name=name, grid=(rows // tr,),
        in_specs=[pl.BlockSpec((N_DEV, tr, cols), lambda i: (0, i, 0)), spec, spec, spec],
        out_specs=[spec, spec, spec, spec], out_shape=[shape, shape, shape, shape],
        compiler_params=_params(("parallel",)),
    )(pieces, w, m, v)


def _sum8(pieces, rows, *, name):
    cols = pieces.shape[2]
    tr = _tile(rows, (176, 96))

    def body(p_ref, g_ref):
        g_ref[...] = _sum_pieces(p_ref)

    return pl.pallas_call(
        body, name=name, grid=(rows // tr,),
        in_specs=[pl.BlockSpec((N_DEV, tr, cols), lambda i: (0, i, 0))],
        out_specs=pl.BlockSpec((tr, cols), lambda i: (i, 0)),
        out_shape=jax.ShapeDtypeStruct((rows, cols), F32),
        compiler_params=_params(("parallel",)),
    )(pieces)


def _adamw_native(g, w, m, v, *, name):
    rows, cols = w.shape
    tr = _tile(rows, (256, 128))

    def body(g_ref, w_ref, m_ref, v_ref, d_ref, nm_ref, nv_ref):
        d_ref[...], nm_ref[...], nv_ref[...] = _adam_update(g_ref[...], w_ref[...], m_ref[...], v_ref[...])

    spec = pl.BlockSpec((tr, cols), lambda i: (i, 0))
    shape = jax.ShapeDtypeStruct((rows, cols), F32)
    return pl.pallas_call(
        body, name=name, grid=(rows // tr,), in_specs=[spec, spec, spec, spec],
        out_specs=[spec, spec, spec], out_shape=[shape, shape, shape],
        compiler_params=_params(("parallel",)),
    )(g, w, m, v)


MESH = pl.DeviceIdType.MESH
ANY = pl.BlockSpec(memory_space=pl.ANY)


def _all_gather(shard, *, name):
    rows, lanes = shard.shape

    def body(x_ref, out_ref, send_sems, recv_sems, local_sem):
        x, y, c = lax.axis_index("x"), lax.axis_index("y"), lax.axis_index("c")
        me, sibling = (x, y, c), (x, y, 1 - c)
        chips = [(1 - x, y), (x, 1 - y), (1 - x, 1 - y)]

        def block(px, py, pc):
            return out_ref.at[4 * px + 2 * py + pc]

        def copy(k, blk, to, src=None):
            return pltpu.make_async_remote_copy(
                src_ref=block(*blk) if src is None else src, dst_ref=block(*blk),
                send_sem=send_sems.at[k], recv_sem=recv_sems.at[k], device_id=to, device_id_type=MESH)

        mine = pltpu.make_async_copy(x_ref, block(*me), local_sem)
        mine.start()
        first = [copy(0, me, sibling, src=x_ref)]
        first += [copy(1 + j, me, (*chip, c), src=x_ref) for j, chip in enumerate(chips)]
        for cp in first:
            cp.start()
        passed = [copy(4 + j, (*chip, c), sibling) for j, chip in enumerate(chips)]
        for j, chip in enumerate(chips):
            copy(1 + j, (*chip, c), me).wait_recv()
            passed[j].start()
        copy(0, sibling, me).wait_recv()
        for j, chip in enumerate(chips):
            copy(4 + j, (*chip, 1 - c), me).wait_recv()
        for cp in first + passed:
            cp.wait_send()
        mine.wait()

    return pl.pallas_call(
        body, name=name, out_shape=jax.ShapeDtypeStruct((N_DEV, rows, lanes), shard.dtype),
        in_specs=[ANY], out_specs=ANY,
        scratch_shapes=[pltpu.SemaphoreType.DMA((7,)), pltpu.SemaphoreType.DMA((7,)), pltpu.SemaphoreType.DMA(())],
    )(shard)


def _peer_copies(kind, src_ref, out_ref, send_sems, recv_sems, local_sem):
    x, y, c = lax.axis_index("x"), lax.axis_index("y"), lax.axis_index("c")
    me = 4 * x + 2 * y + c

    def src(idx):
        return src_ref.at[idx] if kind == "exchange" else src_ref

    mine = pltpu.make_async_copy(src(me), out_ref.at[me], local_sem)
    copies = []
    for r in range(1, N_DEV):
        px = 1 - x if r & 4 else x
        py = 1 - y if r & 2 else y
        pc = 1 - c if r & 1 else c
        copies.append(pltpu.make_async_remote_copy(
            src_ref=src(4 * px + 2 * py + pc), dst_ref=out_ref.at[me],
            send_sem=send_sems.at[r - 1], recv_sem=recv_sems.at[r - 1],
            device_id=(px, py, pc), device_id_type=MESH))
    return mine, copies


PEER_SEMS = [pltpu.SemaphoreType.DMA((7,)), pltpu.SemaphoreType.DMA((7,)), pltpu.SemaphoreType.DMA(())]


def _exchange(pieces, *, name):
    def body(g_ref, out_ref, send_sems, recv_sems, local_sem):
        mine, copies = _peer_copies("exchange", g_ref, out_ref, send_sems, recv_sems, local_sem)
        mine.start()
        for cp in copies:
            cp.start()
        for cp in copies:
            cp.wait()
        mine.wait()

    return pl.pallas_call(
        body, name=name, out_shape=jax.ShapeDtypeStruct(pieces.shape, pieces.dtype),
        in_specs=[ANY], out_specs=ANY, scratch_shapes=list(PEER_SEMS),
    )(pieces)


def _add_rider(rider, in_specs, args, out_specs, out_shape):
    if rider is None:
        return []
    _, arr = rider
    in_specs.append(ANY)
    args.append(arr)
    out_specs.append(ANY)
    out_shape.append(jax.ShapeDtypeStruct((N_DEV,) + arr.shape[-2:], arr.dtype))
    return list(PEER_SEMS)


def _split_rider(refs, rider, n_in, n_out):
    if rider is None:
        return refs, None
    refs = list(refs)
    rin = refs.pop(n_in)
    rout = refs.pop(n_in + n_out)
    return refs[:-3], (rin, rout, *refs[-3:])


def _ride_start(rider, ride_refs, first):
    if rider is None:
        return

    @pl.when(first)
    def _():
        mine, copies = _peer_copies(rider[0], *ride_refs)
        mine.start()
        for cp in copies:
            cp.start()


def _ride_wait(rider, ride_refs, last):
    if rider is None:
        return

    @pl.when(last)
    def _():
        mine, copies = _peer_copies(rider[0], *ride_refs)
        for cp in copies:
            cp.wait()
        mine.wait()


def _gathered_cols(blocks, kdim):
    n = blocks.shape[1] * WIDE // kdim
    return blocks.reshape(N_DEV, kdim, n).transpose(1, 0, 2).reshape(kdim, N_DEV * n)


def _scatter_cols(dw):
    kdim, n8 = dw.shape
    n = n8 // N_DEV
    return dw.reshape(kdim, N_DEV, n).transpose(1, 0, 2).reshape(N_DEV, kdim * n // WIDE, WIDE)


def _pad_rows(a, rows):
    pad = [(0, 0)] * a.ndim
    pad[-2] = (0, rows - a.shape[-2])
    return jnp.pad(a, pad)


def _layer0_in_weight_t(wt):
    cq, ckv, kpe = wt[0:256], wt[256:384], wt[384:416]
    q_s, k_s, v_s, gate = wt[416:928], wt[928:1056], wt[1056:1184], wt[1184:2208]
    z = jnp.zeros((64, wt.shape[1]), wt.dtype)
    return jnp.concatenate([gate, cq, ckv, z, kpe, z[:32], q_s, k_s, v_s], axis=0)


def _layer0_in_grad_t(dwt):
    gate, cq, ckv, kpe = dwt[0:1024], dwt[1024:1280], dwt[1280:1408], dwt[1472:1504]
    q_s, k_s, v_s = dwt[1536:2048], dwt[2048:2176], dwt[2176:2304]
    return jnp.concatenate([cq, ckv, kpe, q_s, k_s, v_s, gate], axis=0)


def _layer1_in_weight_t(wt):
    main = jnp.concatenate([wt[:3 * D_MODEL], wt[3 * D_MODEL + FOX_HEADS:]], axis=0)
    return main, _pad_rows(wt[3 * D_MODEL:3 * D_MODEL + FOX_HEADS], LANES)


def _layer1_in_grad_t(d_main, d_wft):
    return jnp.concatenate([d_main[:3 * D_MODEL], d_wft[:FOX_HEADS], d_main[3 * D_MODEL:]], axis=0)


def _q_up_weight(w):
    return jnp.pad(w.reshape(MLA_Q_RANK, MLA_HEADS, 96), ((0, 0), (0, 0), (0, 32))).reshape(MLA_Q_RANK, MLA_HEADS * LANES)


def _q_up_grad(dwp):
    return dwp.reshape(MLA_Q_RANK, MLA_HEADS, LANES)[:, :, :96].reshape(MLA_Q_RANK, MLA_HEADS * 96)


def _kv_up_weight(w):
    w4 = w.reshape(MLA_KV_RANK, MLA_HEADS, 2, 64)
    kp = jnp.pad(w4[:, :, 0, :], ((0, 0), (0, 0), (0, 64))).reshape(MLA_KV_RANK, MLA_HEADS * LANES)
    vp = w4[:, :, 1, :].reshape(MLA_KV_RANK, MLA_HEADS * 64)
    return jnp.concatenate([kp, vp], axis=1)


def _kv_up_grad(dwp):
    dk = dwp[:, :MLA_HEADS * LANES].reshape(MLA_KV_RANK, MLA_HEADS, LANES)[:, :, :64]
    dv = dwp[:, MLA_HEADS * LANES:].reshape(MLA_KV_RANK, MLA_HEADS, 64)
    return jnp.stack([dk, dv], axis=2).reshape(MLA_KV_RANK, MLA_HEADS * LANES)


def _pad_lanes(a):
    return jnp.pad(a, ((0, 0), (0, LANES - a.shape[1])))


def _small_pack(g_in, g_final, g_q_a, g_kv_a, sinks, b_f, loss):
    rows = [g_in.reshape(8, LANES), g_final.reshape(8, LANES), g_q_a.reshape(2, LANES), g_kv_a.reshape(1, LANES),
            _pad_lanes(sinks.reshape(1, -1)), _pad_lanes(b_f.reshape(1, -1)), _pad_lanes(loss.reshape(1, 1)),
            jnp.zeros((2, LANES), F32)]
    return jnp.concatenate(rows, axis=0)


def _small_unpack(a):
    return (a[0:8].reshape(1, D_MODEL), a[8:16].reshape(D_MODEL), a[16:18].reshape(1, MLA_Q_RANK),
            a[18:19].reshape(1, MLA_KV_RANK), a[19:20, :SWA_HEADS], a[20:21, :FOX_HEADS], a[21, 0])


def _local_step(x, positions, target, e_g_in, w0t, e_g_q_a, wq, e_g_kv_a, wkv, e_sinks,
                late, o_b_f, g_final, scatter1=None):
    s = x.shape[0]
    att_t = min(ATT_T, s)
    nb = s // att_t
    mla_scale = (MLA_NOPE + MLA_ROPE) ** -0.5
    fox_scale = FOX_DIM ** -0.5
    n0a = Z0A_UNITS * LANES

    inv_freq = 1.0 / (ROPE_THETA ** (jnp.arange(0, MLA_ROPE, 2, dtype=F32) / MLA_ROPE))
    ang = positions.astype(F32)[:, None] * inv_freq
    cos, sin = jnp.cos(ang), jnp.sin(ang)
    ones, zeros = jnp.ones((s, 64), F32), jnp.zeros((s, 64), F32)
    cos_t = jnp.concatenate([ones, cos, cos, ones[:, :32]], axis=1)
    sin_t = jnp.concatenate([zeros, -sin, sin, zeros[:, :32]], axis=1)

    h0 = _rmsnorm_fwd(x, e_g_in, width=D_MODEL, col_blk=0, name="l0_norm")
    z0a = _matmul(h0, w0t, tb=True, b_rows=(0, n0a), name="l0_in_a")
    z0b = _matmul(h0, w0t, tb=True, b_rows=(n0a, Z0B_UNITS * LANES), name="l0_in_b", out_dtype=BF16)
    cqn = _rmsnorm_fwd(z0a, e_g_q_a, width=MLA_Q_RANK, col_blk=4, name="l0_q_norm")
    ckvn = _rmsnorm_fwd(z0a, e_g_kv_a, width=MLA_KV_RANK, col_blk=10, name="l0_kv_norm")
    qp = _matmul(cqn, wq, name="l0_q_up")
    kvp = _matmul(ckvn, wkv, name="l0_kv_up", out_dtype=BF16)
    qm, km = _rope_fwd(qp, kvp, z0a, cos_t, sin_t, name="l0_rope")
    gathers = len(late) == 2
    res = _flash_fwd(qm, km, kvp, None, n_pairs=MLA_HEADS // 2, hw=LANES, q_off=0, k_off=0, v_off=MLA_HEADS,
                     scale=mla_scale, name="l0_mla_fwd", rider=("gather", late[0]) if gathers else None)
    o_mla, lse_mla = res[0], res[1]
    wo0, o_g_in, w1t, wft, wo1 = late[1](res[2]) if gathers else late
    o_swa, lse_swa = _swa_fwd(z0b, e_sinks, name="l0_swa_fwd")
    og0 = _gate_fwd([o_mla, o_swa], z0a, name="l0_gate")
    x1 = _matmul(og0, wo0, add=x, name="l0_out")

    h1 = _rmsnorm_fwd(x1, o_g_in, width=D_MODEL, col_blk=0, name="l1_norm")
    z1 = _matmul(h1, w1t, tb=True, b_rows=(0, 3 * D_MODEL), name="l1_in_qkv", out_dtype=BF16)
    gate1 = _matmul(h1, w1t, tb=True, b_rows=(3 * D_MODEL, D_MODEL), name="l1_in_gate")
    zf = _matmul(h1, wft, tb=True, name="l1_in_f")
    bf = _pad_lanes(o_b_f)
    log_cum = _logf_fwd(zf, bf, name="l1_logf")
    bias = (-LOG2E * log_cum[:, :FOX_HEADS]).T.reshape(FOX_HEADS // 2, 2, nb, 1, att_t)
    o_fox, lse_fox = _flash_fwd(z1, z1, z1, bias, n_pairs=FOX_HEADS // 2, hw=64, q_off=0, k_off=8, v_off=16,
                                scale=fox_scale, name="l1_fox_fwd")
    og1 = _gate_fwd([o_fox], gate1, name="l1_gate")
    x2 = _matmul(og1, wo1, add=x1, name="l1_out")

    dx2, loss_part, d_g_final = _loss_head(x2, g_final.reshape(1, D_MODEL), target, name="loss_head")

    d_wo1 = _matmul(og1, dx2, ta=True, name="l1_out_dw")
    d_og1 = _matmul(dx2, wo1, tb=True, name="l1_out_dx")
    do_fox, d_gate1 = _gate_bwd(d_og1, [o_fox], gate1, name="l1_gate_bwd")
    dq1, dk1, dv1, dbias, drow = _flash_bwd(z1, z1, z1, do_fox, o_fox, lse_fox, bias, n_pairs=FOX_HEADS // 2, hw=64,
                                            q_off=0, k_off=8, v_off=16, scale=fox_scale, qk_dtype=BF16,
                                            name="l1_fox_bwd")
    d_log_cum = (drow.reshape(FOX_HEADS, s) - dbias.reshape(FOX_HEADS, s)).T
    d_log_cum = jnp.pad(d_log_cum, ((0, 0), (0, LANES - FOX_HEADS)))
    d_zf, d_bf = _logf_bwd(d_log_cum, zf, bf, name="l1_logf_bwd")
    dz1 = jnp.concatenate([dq1, dk1, dv1, d_gate1], axis=1)
    d_w1t = _matmul(dz1, h1, ta=True, name="l1_in_dw")
    d_wft = _matmul(d_zf, h1, ta=True, name="l1_in_f_dw")
    dh1 = _matmul(dz1, w1t, name="l1_in_dx")
    dh1 = _matmul(d_zf, wft, add=dh1, name="l1_in_f_dx")
    dx1, d_o_g_in = _rmsnorm_bwd(x1, o_g_in, dh1, width=D_MODEL, col_blk=0, add=dx2, name="l1_norm_bwd")

    d_wo0 = _matmul(og0, dx1, ta=True, name="l0_out_dw")
    d_og0 = _matmul(dx1, wo0, tb=True, name="l0_out_dx")
    do_mla, do_swa, d_gate0 = _gate_bwd(d_og0, [o_mla, o_swa], z0a, name="l0_gate_bwd")
    dq_s, dkt_s, dvt_s, d_sinks = _swa_bwd(z0b, e_sinks, do_swa, o_swa, lse_swa, name="l0_swa_bwd")
    dk_s = dkt_s.transpose(0, 2, 1).reshape(s, LANES)
    dv_s = dvt_s.transpose(0, 2, 1).reshape(s, LANES)
    rider = None
    if scatter1 is not None:
        rider = ("exchange", scatter1(dict(w1t=d_w1t, wft=d_wft, wo1=d_wo1, o_g_in=d_o_g_in, wo0=d_wo0)))
    res = _flash_bwd(qm, km, kvp, do_mla, o_mla, lse_mla, None, n_pairs=MLA_HEADS // 2, hw=LANES, q_off=0, k_off=0,
                     v_off=MLA_HEADS, scale=mla_scale, qk_dtype=F32, name="l0_mla_bwd", rider=rider)
    dqm, dkm, dvm = res[0], res[1], res[2]
    recv1 = res[3] if rider is not None else None
    d_qp, d_kvp, d_kpe = _rope_bwd(dqm, dkm, dvm, cos_t, sin_t, name="l0_rope_bwd")
    d_wq = _matmul(cqn, d_qp, ta=True, name="l0_q_up_dw")
    d_cqn = _matmul(d_qp, wq, tb=True, name="l0_q_up_dx")
    d_wkv = _matmul(ckvn, d_kvp, ta=True, name="l0_kv_up_dw")
    d_ckvn = _matmul(d_kvp, wkv, tb=True, name="l0_kv_up_dx")
    d_cq, d_g_q_a = _rmsnorm_bwd(z0a, e_g_q_a, d_cqn, width=MLA_Q_RANK, col_blk=4, out_dtype=BF16, name="l0_q_norm_bwd")
    d_ckv, d_g_kv_a = _rmsnorm_bwd(z0a, e_g_kv_a, d_ckvn, width=MLA_KV_RANK, col_blk=10, out_dtype=BF16,
                                   name="l0_kv_norm_bwd")
    dz0 = jnp.concatenate([d_gate0, d_cq, d_ckv, d_kpe, dq_s.astype(BF16), dk_s.astype(BF16), dv_s.astype(BF16)], axis=1)
    d_w0t = _matmul(dz0, h0, ta=True, name="l0_in_dw")
    dh0 = _matmul(dz0, w0t, name="l0_in_dx")
    grad_x, d_e_g_in = _rmsnorm_bwd(x, e_g_in, dh0, width=D_MODEL, col_blk=0, add=dx1, name="l0_norm_bwd")

    return dict(recv1=recv1, loss=loss_part[0, 0], grad_x=grad_x, e_g_in=d_e_g_in, w0t=d_w0t, e_g_q_a=d_g_q_a, wq=d_wq,
                e_g_kv_a=d_g_kv_a, wkv=d_wkv, e_sinks=d_sinks[:, 0].reshape(1, SWA_HEADS), wo0=d_wo0,
                o_g_in=d_o_g_in, w1t=d_w1t, wft=d_wft, o_b_f=d_bf[:, :FOX_HEADS], wo1=d_wo1, g_final=d_g_final.reshape(D_MODEL))


def _wide(a, rows):
    flat = a.reshape(-1)
    return jnp.pad(flat, (0, rows * WIDE - flat.shape[0])).reshape(rows, WIDE)


def _rows_b0(w_q, w_kv):
    return jnp.concatenate([_wide(w_q, 32), _wide(w_kv, 16)], axis=0)


def _unflat_b0(f):
    return f[0:24].reshape(1, MLA_Q_RANK, 96), f[32:48].reshape(1, MLA_KV_RANK, 128)


def _rows_b1(o_w_out, e_w_out, g_in):
    return jnp.concatenate([o_w_out, e_w_out, _wide(g_in, 16)], axis=0)


def _unflat_b1(f):
    return f[0:128][None], f[128:256][None], f[256:257, :LANES]


def kernel(x, positions, e_g_in, e_w_in, e_g_q_a, e_w_q_up, e_g_kv_a, e_w_kv_up, e_sinks, e_w_out, o_g_in, o_w_in, o_b_f, o_w_out, g_final, loss_target, m_e_g_in, m_e_w_in, m_e_g_q_a, m_e_w_q_up, m_e_g_kv_a, m_e_w_kv_up, m_e_sinks, m_e_w_out, m_o_g_in, m_o_w_in, m_o_b_f, m_o_w_out, m_g_final, v_e_g_in, v_e_w_in, v_e_g_q_a, v_e_w_q_up, v_e_g_kv_a, v_e_w_kv_up, v_e_sinks, v_e_w_out, v_o_g_in, v_o_w_in, v_o_b_f, v_o_w_out, v_g_final):
    def bf(a):
        return a.astype(BF16)

    shard0 = jnp.concatenate([_pad_rows(bf(e_w_in[0]).T, RA0), _rows_b0(bf(e_w_q_up[0]), bf(e_w_kv_up[0]))], axis=0)
    gath0 = _all_gather(shard0, name="weights0_all_gather")
    w0t = _layer0_in_weight_t(gath0[:, :N_E_IN].reshape(N_DEV * N_E_IN, WIDE))
    wq = _q_up_weight(_gathered_cols(gath0[:, RA0:RA0 + 24], MLA_Q_RANK))
    wkv = _kv_up_weight(_gathered_cols(gath0[:, RA0 + 32:RA0 + 48], MLA_KV_RANK))

    g_bits = lax.bitcast_convert_type(o_g_in.reshape(LANES), BF16)
    shard1 = jnp.concatenate([_pad_rows(bf(o_w_in[0]).T, RA1), _rows_b1(bf(o_w_out[0]), bf(e_w_out[0]), g_bits)], axis=0)

    def unpack1(gath1):
        w1t, wft = _layer1_in_weight_t(gath1[:, :N_O_IN].reshape(N_DEV * N_O_IN, WIDE))
        wo1 = gath1[:, RA1:RA1 + 128].reshape(D_MODEL, D_MODEL)
        wo0 = gath1[:, RA1 + 128:RA1 + 256].reshape(D_MODEL, D_MODEL)
        bits = gath1[:, RA1 + 256, :2 * LANES].reshape(N_DEV, LANES, 2)
        return wo0, lax.bitcast_convert_type(bits, F32).reshape(1, D_MODEL), w1t, wft, wo1

    def scatter1(g):
        d_in_t = _layer1_in_grad_t(g["w1t"], g["wft"]).reshape(N_DEV, N_O_IN, WIDE)
        d_o_g = jnp.pad(g["o_g_in"].reshape(N_DEV, 1, LANES), ((0, 0), (0, 15), (0, WIDE - LANES)))
        return jnp.concatenate([_pad_rows(d_in_t, RA1), g["wo1"].reshape(N_DEV, 128, WIDE),
                                g["wo0"].reshape(N_DEV, 128, WIDE), d_o_g], axis=1).astype(BF16)

    gr = _local_step(x[0], positions[0], loss_target[0], e_g_in, w0t, e_g_q_a, wq, e_g_kv_a, wkv, e_sinks,
                     (shard1, unpack1), o_b_f, g_final, scatter1=scatter1)

    pieces0 = jnp.concatenate([
        _pad_rows(_layer0_in_grad_t(gr["w0t"]).reshape(N_DEV, N_E_IN, WIDE), RA0),
        _pad_rows(_scatter_cols(_q_up_grad(gr["wq"])), 32), _scatter_cols(_kv_up_grad(gr["wkv"]))], axis=1)
    recv0 = _exchange(pieces0.astype(BF16), name="grads0_exchange")

    def in_projection(recv, ra, n, w, m, v, name):
        g = _sum8(recv, ra, name=name + "_grad_sum")[:n].T
        d, nm, nv = _adamw_native(g, w[0], m[0], v[0], name=name + "_adamw")
        return g[None], d[None], nm[None], nv[None]

    e_in = in_projection(recv0, RA0, N_E_IN, e_w_in, m_e_w_in, v_e_w_in, "e_w_in")
    o_in = in_projection(gr["recv1"], RA1, N_O_IN, o_w_in, m_o_w_in, v_o_w_in, "o_w_in")
    b0 = _adamw(recv0[:, RA0:], _rows_b0(e_w_q_up[0], e_w_kv_up[0]), _rows_b0(m_e_w_q_up[0], m_e_w_kv_up[0]),
                _rows_b0(v_e_w_q_up[0], v_e_w_kv_up[0]), name="adamw_early")
    b1 = _adamw(gr["recv1"][:, RA1:], _rows_b1(o_w_out[0], e_w_out[0], o_g_in),
                _rows_b1(m_o_w_out[0], m_e_w_out[0], m_o_g_in), _rows_b1(v_o_w_out[0], v_e_w_out[0], v_o_g_in),
                name="adamw_late")

    def sharded(k):
        q_up, kv_up = _unflat_b0(b0[k])
        o_out, e_out, o_g = _unflat_b1(b1[k])
        return e_in[k], q_up, kv_up, e_out, o_in[k], o_out, o_g

    g_sh, d_sh, m_sh, v_sh = [sharded(k) for k in range(4)]

    small = _small_pack(gr["e_g_in"], gr["g_final"], gr["e_g_q_a"], gr["e_g_kv_a"], gr["e_sinks"], gr["o_b_f"], gr["loss"])
    small_all = _all_gather(small, name="small_all_gather")
    zero = jnp.zeros((), F32)
    w_small = _small_pack(e_g_in, g_final, e_g_q_a, e_g_kv_a, e_sinks, o_b_f, zero)
    m_small = _small_pack(m_e_g_in, m_g_final, m_e_g_q_a, m_e_g_kv_a, m_e_sinks, m_o_b_f, zero)
    v_small = _small_pack(v_e_g_in, v_g_final, v_e_g_q_a, v_e_g_kv_a, v_e_sinks, v_o_b_f, zero)
    smalls = _adamw(small_all, w_small, m_small, v_small, name="adamw_replicated")
    g_sm, d_sm, m_sm, v_sm = [_small_unpack(a) for a in smalls]
    loss = g_sm[6]

    def leaves(sh, sm):
        return (sm[0], sh[0], sm[2], sh[1], sm[3], sh[2], sm[4], sh[3], sh[6], sh[4], sm[5], sh[5], sm[1])

    return (loss, gr["grad_x"][None], *leaves(g_sh, g_sm), *leaves(d_sh, d_sm), *leaves(m_sh, m_sm), *leaves(v_sh, v_sm))
```

```python
import functools

import jax
import jax.numpy as jnp
from jax import lax
from jax.experimental import pallas as pl
from jax.experimental.pallas import tpu as pltpu

F32 = jnp.float32
BF16 = jnp.bfloat16
NEG_INF = float("-inf")

N_DEV = 8
LANES = 128
D_MODEL = 1024
EPS = 1e-6
ROPE_THETA = 10000.0
MLA_HEADS = 8
MLA_Q_RANK = 256
MLA_KV_RANK = 128
MLA_NOPE = 64
MLA_ROPE = 32
MLA_V = 64
SWA_HEADS = 8
SWA_KV_HEADS = 2
SWA_DIM = 64
WINDOW = 128
FOX_HEADS = 16
FOX_DIM = 64

ADAM_LR = 0.001
ADAM_B1 = 0.9
ADAM_B2 = 0.999
ADAM_EPS = 1e-08
ADAM_WD = 0.01
ADAM_STEP = 10

ATT_T = 512
VMEM_LIMIT = 56 * 1024 * 1024
MATMUL_B_BLOCK_BYTES = 8 * 1024 * 1024

Z0A_UNITS = 12
Z0B_UNITS = 6

WIDE = 1024
N_E_IN = 276
N_O_IN = 514
RA0 = 288
RB0 = 32 + 16
RA1 = 528
RB1 = 128 + 128 + 16
SMALL_ROWS = 24


def _tile(n, cands):
    for c in cands:
        if n % c == 0:
            return c
    raise ValueError(f"no tile for {n}")


def _params(sem, vmem=None):
    return pltpu.CompilerParams(dimension_semantics=sem, vmem_limit_bytes=vmem)


def _matmul(a, b, *, name, ta=False, tb=False, add=None, out_dtype=F32, b_rows=None, rider=None):
    if ta:
        kdim, m = a.shape
    else:
        m, kdim = a.shape
    if tb:
        n, kb = b.shape
    else:
        kb, n = b.shape
    assert kdim == kb, (a.shape, b.shape)
    b_start = 0
    if b_rows is not None:
        assert tb
        b_start, n = b_rows
    tm = _tile(m, (512, 256, 128))
    tn = _tile(n, [c for c in (1024, 768, 512, 384, 256, 128)
                   if c * kdim * b.dtype.itemsize <= MATMUL_B_BLOCK_BYTES and b_start % c == 0])
    assert b_start % tn == 0, (b_start, tn)
    b_off = b_start // tn
    dims = (((0 if ta else 1,), (1 if tb else 0,)), ((), ()))

    grid = (m // tm, n // tn)

    def body(*refs):
        refs, ride_refs = _split_rider(refs, rider, n_in=2 if add is None else 3, n_out=1)
        if add is None:
            a_ref, b_ref, o_ref = refs
            add_ref = None
        else:
            a_ref, b_ref, add_ref, o_ref = refs
        i, j = pl.program_id(0), pl.program_id(1)
        _ride_start(rider, ride_refs, (i == 0) & (j == 0))
        r = lax.dot_general(a_ref[...].astype(BF16), b_ref[...].astype(BF16), dims, preferred_element_type=F32)
        if add_ref is not None:
            r = r + add_ref[...]
        o_ref[...] = r.astype(out_dtype)
        _ride_wait(rider, ride_refs, (i == grid[0] - 1) & (j == grid[1] - 1))

    a_spec = pl.BlockSpec((kdim, tm), lambda i, j: (0, i)) if ta else pl.BlockSpec((tm, kdim), lambda i, j: (i, 0))
    b_spec = pl.BlockSpec((tn, kdim), lambda i, j: (j + b_off, 0)) if tb else pl.BlockSpec((kdim, tn), lambda i, j: (0, j))
    in_specs = [a_spec, b_spec]
    args = [a, b]
    if add is not None:
        in_specs.append(pl.BlockSpec((tm, tn), lambda i, j: (i, j)))
        args.append(add)
    out_specs = [pl.BlockSpec((tm, tn), lambda i, j: (i, j))]
    out_shape = [jax.ShapeDtypeStruct((m, n), out_dtype)]
    scratch = _add_rider(rider, in_specs, args, out_specs, out_shape)
    res = pl.pallas_call(
        body, name=name, grid=grid, in_specs=in_specs, out_specs=out_specs, out_shape=out_shape, scratch_shapes=scratch,
        compiler_params=_params(("parallel", "parallel") if rider is None else ("arbitrary", "arbitrary"), VMEM_LIMIT),
    )(*args)
    return res[0] if rider is None else res


def _rmsnorm_fwd(x, g, *, width, col_blk, name):
    s = x.shape[0]
    tm = _tile(s, (256, 128))

    def body(x_ref, g_ref, y_ref):
        xf = x_ref[...].astype(F32)
        r = lax.rsqrt(jnp.mean(xf * xf, axis=-1, keepdims=True) + EPS)
        y_ref[...] = ((xf * r) * g_ref[...]).astype(BF16)

    return pl.pallas_call(
        body, name=name, grid=(s // tm,),
        in_specs=[pl.BlockSpec((tm, width), lambda i: (i, col_blk)), pl.BlockSpec((1, width), lambda i: (0, 0))],
        out_specs=pl.BlockSpec((tm, width), lambda i: (i, 0)),
        out_shape=jax.ShapeDtypeStruct((s, width), BF16),
        compiler_params=_params(("parallel",)),
    )(x, g)


def _rmsnorm_bwd(x, g, dy, *, width, col_blk, name, add=None, out_dtype=F32):
    s = x.shape[0]
    tm = _tile(s, (256, 128))

    def body(*refs):
        if add is None:
            x_ref, g_ref, dy_ref, dx_ref, dg_ref = refs
            add_ref = None
        else:
            x_ref, g_ref, dy_ref, add_ref, dx_ref, dg_ref = refs
        i = pl.program_id(0)
        xf = x_ref[...].astype(F32)
        r = lax.rsqrt(jnp.mean(xf * xf, axis=-1, keepdims=True) + EPS)
        xh = xf * r
        dyf = dy_ref[...].astype(F32)

        @pl.when(i == 0)
        def _():
            dg_ref[...] = jnp.zeros_like(dg_ref)

        dg_ref[...] += jnp.sum(dyf * xh, axis=0, keepdims=True)
        dxh = dyf * g_ref[...]
        dx = r * (dxh - xh * jnp.mean(dxh * xh, axis=-1, keepdims=True))
        if add_ref is not None:
            dx = dx + add_ref[...]
        dx_ref[...] = dx.astype(out_dtype)

    in_specs = [pl.BlockSpec((tm, width), lambda i: (i, col_blk)), pl.BlockSpec((1, width), lambda i: (0, 0)),
                pl.BlockSpec((tm, width), lambda i: (i, 0))]
    args = [x, g, dy]
    if add is not None:
        in_specs.append(pl.BlockSpec((tm, width), lambda i: (i, 0)))
        args.append(add)
    return pl.pallas_call(
        body, name=name, grid=(s // tm,),
        in_specs=in_specs,
        out_specs=[pl.BlockSpec((tm, width), lambda i: (i, 0)), pl.BlockSpec((1, width), lambda i: (0, 0))],
        out_shape=[jax.ShapeDtypeStruct((s, width), out_dtype), jax.ShapeDtypeStruct((1, width), F32)],
        compiler_params=_params(("arbitrary",)),
    )(*args)


def _sigmoid(x):
    return 1.0 / (1.0 + jnp.exp(-x))


def _gate_fwd(o_parts, gate, *, name):
    s = gate.shape[0]
    tm = _tile(s, (256, 128))
    n_o = len(o_parts)

    def body(*refs):
        o_refs, g_ref, y_ref = refs[:n_o], refs[n_o], refs[n_o + 1]
        o = o_refs[0][...] if n_o == 1 else jnp.concatenate([r[...] for r in o_refs], axis=1)
        gt = g_ref[...]
        y_ref[...] = (o * (gt * _sigmoid(gt))).astype(BF16)

    in_specs = [pl.BlockSpec((tm, o.shape[1]), lambda i: (i, 0)) for o in o_parts]
    in_specs.append(pl.BlockSpec((tm, D_MODEL), lambda i: (i, 0)))
    return pl.pallas_call(
        body, name=name, grid=(s // tm,), in_specs=in_specs,
        out_specs=pl.BlockSpec((tm, D_MODEL), lambda i: (i, 0)),
        out_shape=jax.ShapeDtypeStruct((s, D_MODEL), BF16),
        compiler_params=_params(("parallel",)),
    )(*o_parts, gate)


def _gate_bwd(d_og, o_parts, gate, *, name):
    s = gate.shape[0]
    tm = _tile(s, (256, 128))
    n_o = len(o_parts)
    widths = [o.shape[1] for o in o_parts]

    def body(*refs):
        d_ref, o_refs, g_ref = refs[0], refs[1:1 + n_o], refs[1 + n_o]
        do_refs, dg_ref = refs[2 + n_o:2 + 2 * n_o], refs[2 + 2 * n_o]
        d = d_ref[...]
        gt = g_ref[...]
        sg = _sigmoid(gt)
        silu = gt * sg
        dsilu = sg * (1.0 + gt * (1.0 - sg))
        o = o_refs[0][...] if n_o == 1 else jnp.concatenate([r[...] for r in o_refs], axis=1)
        dg_ref[...] = (d * o * dsilu).astype(BF16)
        do = d * silu
        off = 0
        for r, w in zip(do_refs, widths):
            r[...] = do[:, off:off + w]
            off += w

    in_specs = [pl.BlockSpec((tm, D_MODEL), lambda i: (i, 0))]
    in_specs += [pl.BlockSpec((tm, w), lambda i: (i, 0)) for w in widths]
    in_specs.append(pl.BlockSpec((tm, D_MODEL), lambda i: (i, 0)))
    out_specs = [pl.BlockSpec((tm, w), lambda i: (i, 0)) for w in widths]
    out_specs.append(pl.BlockSpec((tm, D_MODEL), lambda i: (i, 0)))
    out_shape = [jax.ShapeDtypeStruct((s, w), F32) for w in widths]
    out_shape.append(jax.ShapeDtypeStruct((s, D_MODEL), BF16))
    return pl.pallas_call(
        body, name=name, grid=(s // tm,), in_specs=in_specs, out_specs=out_specs, out_shape=out_shape,
        compiler_params=_params(("parallel",)),
    )(d_og, *o_parts, gate)


def _rot_half(x):
    lane = lax.broadcasted_iota(jnp.int32, x.shape, 1)
    return jnp.where(lane < 80, pltpu.roll(x, LANES - 16, axis=1), pltpu.roll(x, 16, axis=1))


def _rot_half_t(g):
    lane = lax.broadcasted_iota(jnp.int32, g.shape, 1)
    lo = (lane >= MLA_NOPE) & (lane < MLA_NOPE + MLA_ROPE // 2)
    hi = (lane >= MLA_NOPE + MLA_ROPE // 2) & (lane < MLA_NOPE + MLA_ROPE)
    return jnp.where(lo, pltpu.roll(g, LANES - 16, axis=1), jnp.where(hi, pltpu.roll(g, 16, axis=1), 0.0))


def _rope_fwd(qp, kvp, z0a, cos_t, sin_t, *, name):
    s = qp.shape[0]
    tm = _tile(s, (256, 128))
    hw = MLA_HEADS * LANES

    def body(q_ref, k_ref, kpe_ref, c_ref, s_ref, qm_ref, km_ref):
        c = c_ref[...]
        sn = s_ref[...]
        kpe = kpe_ref[...]
        kpe_r = (kpe * c + _rot_half(kpe) * sn).astype(BF16)
        lane = lax.broadcasted_iota(jnp.int32, kpe.shape, 1)
        for h in range(MLA_HEADS):
            sl = slice(h * LANES, (h + 1) * LANES)
            qh = q_ref[:, sl]
            qm_ref[:, sl] = (qh * c + _rot_half(qh) * sn).astype(BF16)
            km_ref[:, sl] = jnp.where(lane < MLA_NOPE, k_ref[:, sl], kpe_r)

    return pl.pallas_call(
        body, name=name, grid=(s // tm,),
        in_specs=[pl.BlockSpec((tm, hw), lambda i: (i, 0)), pl.BlockSpec((tm, hw), lambda i: (i, 0)),
                  pl.BlockSpec((tm, LANES), lambda i: (i, 11)),
                  pl.BlockSpec((tm, LANES), lambda i: (i, 0)), pl.BlockSpec((tm, LANES), lambda i: (i, 0))],
        out_specs=[pl.BlockSpec((tm, hw), lambda i: (i, 0)), pl.BlockSpec((tm, hw), lambda i: (i, 0))],
        out_shape=[jax.ShapeDtypeStruct((s, hw), BF16), jax.ShapeDtypeStruct((s, hw), BF16)],
        compiler_params=_params(("parallel",)),
    )(qp, kvp, z0a, cos_t, sin_t)


def _rope_bwd(dqm, dkm, dvm, cos_t, sin_t, *, name):
    s = dqm.shape[0]
    tm = _tile(s, (256, 128))
    hw = MLA_HEADS * LANES
    vw = MLA_HEADS * MLA_V

    def body(dq_ref, dk_ref, dv_ref, c_ref, s_ref, dqp_ref, dkv_ref, dkpe_ref):
        c = c_ref[...]
        sn = s_ref[...]
        ksum = jnp.zeros((tm, LANES), F32)
        for h in range(MLA_HEADS):
            sl = slice(h * LANES, (h + 1) * LANES)
            dq = dq_ref[:, sl]
            dqp_ref[:, sl] = (dq * c + _rot_half_t(dq * sn)).astype(BF16)
            dk = dk_ref[:, sl]
            dkv_ref[:, sl] = dk.astype(BF16)
            ksum = ksum + dk
        dkv_ref[:, hw:] = dv_ref[...]
        lane = lax.broadcasted_iota(jnp.int32, ksum.shape, 1)
        dkpe = ksum * c + _rot_half_t(ksum * sn)
        dkpe_ref[...] = jnp.where((lane >= MLA_NOPE) & (lane < MLA_NOPE + MLA_ROPE), dkpe, 0.0).astype(BF16)

    return pl.pallas_call(
        body, name=name, grid=(s // tm,),
        in_specs=[pl.BlockSpec((tm, hw), lambda i: (i, 0)), pl.BlockSpec((tm, hw), lambda i: (i, 0)),
                  pl.BlockSpec((tm, vw), lambda i: (i, 0)),
                  pl.BlockSpec((tm, LANES), lambda i: (i, 0)), pl.BlockSpec((tm, LANES), lambda i: (i, 0))],
        out_specs=[pl.BlockSpec((tm, hw), lambda i: (i, 0)), pl.BlockSpec((tm, hw + vw), lambda i: (i, 0)),
                   pl.BlockSpec((tm, LANES), lambda i: (i, 0))],
        out_shape=[jax.ShapeDtypeStruct((s, hw), BF16), jax.ShapeDtypeStruct((s, hw + vw), BF16),
                   jax.ShapeDtypeStruct((s, LANES), BF16)],
        compiler_params=_params(("parallel",)),
    )(dqm, dkm, dvm, cos_t, sin_t)


def _head_mask(shape, a):
    lane = lax.broadcasted_iota(jnp.int32, shape, 1)
    return (lane >= 64 * a) & (lane < 64 * (a + 1))


def _causal_mask(t):
    row = lax.broadcasted_iota(jnp.int32, (t, t), 0)
    col = lax.broadcasted_iota(jnp.int32, (t, t), 1)
    return col <= row


_NT = (((1,), (1,)), ((), ()))
LOG2E = 1.4426950408889634


def _stack_heads(tile, hw):
    lane = lax.broadcasted_iota(jnp.int32, tile.shape, 1)
    z = jnp.zeros_like(tile)
    return jnp.concatenate([jnp.where(lane < hw, tile, z), jnp.where(lane >= hw, tile, z)], axis=0)


def _stacked_rows(r0, r1, t):
    n = r0.shape[-1]
    return jnp.concatenate([jnp.broadcast_to(r0, (t, n)), jnp.broadcast_to(r1, (t, n))], axis=0)


def _stacked_causal_mask(t):
    m = _causal_mask(t)
    return jnp.concatenate([m, m], axis=0)


def _resident(block, index_map):
    return pl.BlockSpec(block, index_map, pipeline_mode=pl.Buffered(1))


def _flash_fwd(q, k, v, bias, *, n_pairs, hw, q_off, k_off, v_off, scale, name, rider=None):
    s = q.shape[0]
    t = min(ATT_T, s)
    nb = s // t
    qw = 2 * hw
    has_bias = bias is not None
    c1 = scale * LOG2E

    def body(*refs):
        refs, ride_refs = _split_rider(refs, rider, n_in=4 if has_bias else 3, n_out=2)
        if has_bias:
            q_ref, k_ref, v_ref, b_ref, o_ref, lse_ref, vt_ref, bcol_ref = refs
        else:
            q_ref, k_ref, v_ref, o_ref, lse_ref, vt_ref = refs
            b_ref = bcol_ref = None
        _ride_start(rider, ride_refs, pl.program_id(0) == 0)
        row = lax.broadcasted_iota(jnp.int32, (t, t), 0)
        col = lax.broadcasted_iota(jnp.int32, (t, t), 1)
        cmask_t = jnp.concatenate([row <= col, row <= col], axis=1)
        lane_lt64 = lax.broadcasted_iota(jnp.int32, (t, LANES), 1) < 64

        def as_column(r):
            return jnp.broadcast_to(r, (8, r.shape[1])).T[:, 0:1]

        def v_block(j, _):
            c0 = pl.multiple_of(j * t, t)
            vt_ref[j] = v_ref[pl.ds(c0, t), :].astype(F32).T.astype(BF16)
            if has_bias:
                for a in range(2):
                    bcol_ref[a, pl.ds(c0, t), :] = as_column(b_ref[0, a, j])
            return 0

        lax.fori_loop(0, nb, v_block, 0)

        def stacked_queries(i):
            return _stack_heads(q_ref[pl.ds(pl.multiple_of(i * t, t), t), :], hw).astype(F32).T.astype(BF16)

        def kv_step(j, carry, qs_t, masked):
            m, l, acc = carry
            rows = pl.ds(pl.multiple_of(j * t, t), t)
            sc = jnp.dot(k_ref[rows, :], qs_t, preferred_element_type=F32) * c1
            if has_bias:
                sc = sc + jnp.concatenate([jnp.broadcast_to(bcol_ref[0, rows, :], (t, t)),
                                           jnp.broadcast_to(bcol_ref[1, rows, :], (t, t))], axis=1)
            if masked:
                sc = jnp.where(cmask_t, sc, NEG_INF)
            m_new = jnp.maximum(m, jnp.max(sc, axis=0, keepdims=True))
            alpha = jnp.exp2(m - m_new)
            p = jnp.exp2(sc - m_new)
            l_new = alpha * l + jnp.sum(p, axis=0, keepdims=True)
            pv = jnp.dot(vt_ref[j], p.astype(BF16), preferred_element_type=F32)
            return m_new, l_new, alpha * acc + pv

        def finish(i, carry):
            m, l, acc = carry
            r0 = pl.multiple_of(i * t, t)
            out = (acc / l).T
            lse2 = as_column(m + jnp.log2(l))
            lse_ref[0, 0, pl.ds(r0, t), :] = lse2[:t]
            lse_ref[0, 1, pl.ds(r0, t), :] = lse2[t:]
            o_ref[pl.ds(r0, t), :] = jnp.where(lane_lt64, out[:t], out[t:])

        init = (jnp.full((1, 2 * t), NEG_INF, F32), jnp.zeros((1, 2 * t), F32), jnp.zeros((LANES, 2 * t), F32))

        def q_block(i, _):
            qs_t = stacked_queries(i)
            carry = lax.fori_loop(0, i, lambda j, c: kv_step(j, c, qs_t, False), init)
            finish(i, kv_step(i, carry, qs_t, True))
            return 0

        lax.fori_loop(0, nb, q_block, 0)
        _ride_wait(rider, ride_refs, pl.program_id(0) == n_pairs - 1)

    in_specs = [_resident((s, qw), lambda p: (0, q_off + p)), _resident((s, qw), lambda p: (0, k_off + p)),
                _resident((s, LANES), lambda p: (0, v_off + p))]
    args = [q, k, v]
    if has_bias:
        in_specs.append(_resident((1, 2, nb, 1, t), lambda p: (p, 0, 0, 0, 0)))
        args.append(bias)
    out_specs = [pl.BlockSpec((s, LANES), lambda p: (0, p)), pl.BlockSpec((1, 2, s, 1), lambda p: (p, 0, 0, 0))]
    out_shape = [jax.ShapeDtypeStruct((s, n_pairs * LANES), F32), jax.ShapeDtypeStruct((n_pairs, 2, s, 1), F32)]
    scratch = [pltpu.VMEM((nb, LANES, t), BF16)] + ([pltpu.VMEM((2, s, 1), F32)] if has_bias else [])
    scratch += _add_rider(rider, in_specs, args, out_specs, out_shape)
    return pl.pallas_call(
        body, name=name, grid=(n_pairs,), in_specs=in_specs, out_specs=out_specs, out_shape=out_shape,
        scratch_shapes=scratch,
        compiler_params=_params(("parallel",) if rider is None else ("arbitrary",), VMEM_LIMIT),
    )(*args)


def _flash_bwd(q, k, v, do, o, lse, bias, *, n_pairs, hw, q_off, k_off, v_off, scale, qk_dtype, name, rider=None):
    s = q.shape[0]
    t = min(ATT_T, s)
    nb = s // t
    qw = 2 * hw
    has_bias = bias is not None
    c1 = scale * LOG2E

    def body(*refs):
        refs, ride_refs = _split_rider(refs, rider, n_in=7 if has_bias else 6, n_out=5 if has_bias else 3)
        if has_bias:
            (q_ref, k_ref, v_ref, do_ref, o_ref, lse_ref, b_ref, dq_ref, dk_ref, dv_ref, db_ref, dr_ref,
             dkt_ref, dvt_ref) = refs
            db_ref[...] = jnp.zeros_like(db_ref)
        else:
            q_ref, k_ref, v_ref, do_ref, o_ref, lse_ref, dq_ref, dk_ref, dv_ref, dkt_ref, dvt_ref = refs
            b_ref = db_ref = dr_ref = None
        _ride_start(rider, ride_refs, pl.program_id(0) == 0)
        dkt_ref[...] = jnp.zeros_like(dkt_ref)
        dvt_ref[...] = jnp.zeros_like(dvt_ref)
        cmask = _stacked_causal_mask(t)
        lane_lt_hw = lax.broadcasted_iota(jnp.int32, (t, qw), 1) < hw

        def q_block(i, _):
            r0 = pl.multiple_of(i * t, t)
            qs = _stack_heads(q_ref[pl.ds(r0, t), :], hw)
            dos = _stack_heads(do_ref[pl.ds(r0, t), :], 64)
            ot = o_ref[pl.ds(r0, t), :]
            delta = jnp.sum(dos * jnp.concatenate([ot, ot], axis=0), axis=-1, keepdims=True)
            lse2 = jnp.concatenate([lse_ref[0, 0, pl.ds(r0, t), :], lse_ref[0, 1, pl.ds(r0, t), :]], axis=0)
            dosb = dos.astype(BF16)
            dos_t = dos.T.astype(BF16)
            qs_t = qs.astype(F32).T.astype(BF16)

            def kv_step(j, carry, masked):
                dq, rsum = carry
                c0 = pl.multiple_of(j * t, t)
                kt = k_ref[pl.ds(c0, t), :]
                vt = v_ref[pl.ds(c0, t), :]
                sc = lax.dot_general(qs, kt, _NT, preferred_element_type=F32) * c1
                if has_bias:
                    sc = sc + _stacked_rows(b_ref[0, 0, j], b_ref[0, 1, j], t)
                if masked:
                    sc = jnp.where(cmask, sc, NEG_INF)
                p = jnp.exp2(sc - lse2)
                dp = lax.dot_general(dosb, vt, _NT, preferred_element_type=F32)
                ds = p * (dp - delta)
                dsb = ds.astype(BF16)
                pb = p.astype(BF16)
                if hw == LANES:
                    dvt_ref[j] += jnp.concatenate(
                        [jnp.dot(dos_t[:64, :t], pb[:t], preferred_element_type=F32),
                         jnp.dot(dos_t[64:, t:], pb[t:], preferred_element_type=F32)], axis=0)
                    dkt_ref[j] += jnp.concatenate(
                        [jnp.dot(qs_t[:hw, :t], dsb[:t], preferred_element_type=F32),
                         jnp.dot(qs_t[hw:, t:], dsb[t:], preferred_element_type=F32)], axis=0)
                else:
                    dvt_ref[j] += jnp.dot(dos_t, pb, preferred_element_type=F32)
                    dkt_ref[j] += jnp.dot(qs_t, dsb, preferred_element_type=F32)
                if has_bias:
                    db_ref[0, 0, j] += jnp.sum(ds[:t], axis=0, keepdims=True)
                    db_ref[0, 1, j] += jnp.sum(ds[t:], axis=0, keepdims=True)
                    rsum = rsum + jnp.sum(ds, axis=-1, keepdims=True)
                return dq + jnp.dot(dsb, kt, preferred_element_type=F32), rsum

            init = (jnp.zeros((2 * t, qw), F32), jnp.zeros((2 * t, 1), F32))
            carry = lax.fori_loop(0, i, functools.partial(kv_step, masked=False), init)
            dq, rsum = kv_step(i, carry, True)
            dq = dq * scale
            dq_ref[pl.ds(r0, t), :] = jnp.where(lane_lt_hw, dq[:t], dq[t:]).astype(qk_dtype)
            if has_bias:
                rsum_row = jnp.broadcast_to(rsum, (2 * t, LANES)).T[0:1]
                dr_ref[0, 0, i] = rsum_row[:, :t]
                dr_ref[0, 1, i] = rsum_row[:, t:]
            return 0

        lax.fori_loop(0, nb, q_block, 0)

        def k_block(j, _):
            c0 = pl.multiple_of(j * t, t)
            dk_ref[pl.ds(c0, t), :] = (dkt_ref[j].T * scale).astype(qk_dtype)
            dv_ref[pl.ds(c0, t), :] = dvt_ref[j].T.astype(BF16)
            return 0

        lax.fori_loop(0, nb, k_block, 0)
        _ride_wait(rider, ride_refs, pl.program_id(0) == n_pairs - 1)

    in_specs = [_resident((s, qw), lambda p: (0, q_off + p)), _resident((s, qw), lambda p: (0, k_off + p)),
                _resident((s, LANES), lambda p: (0, v_off + p)),
                _resident((s, LANES), lambda p: (0, p)), _resident((s, LANES), lambda p: (0, p)),
                _resident((1, 2, s, 1), lambda p: (p, 0, 0, 0))]
    args = [q, k, v, do, o, lse]
    out_specs = [pl.BlockSpec((s, qw), lambda p: (0, p)), pl.BlockSpec((s, qw), lambda p: (0, p)),
                 pl.BlockSpec((s, LANES), lambda p: (0, p))]
    out_shape = [jax.ShapeDtypeStruct((s, n_pairs * qw), qk_dtype), jax.ShapeDtypeStruct((s, n_pairs * qw), qk_dtype),
                 jax.ShapeDtypeStruct((s, n_pairs * LANES), BF16)]
    if has_bias:
        in_specs.append(_resident((1, 2, nb, 1, t), lambda p: (p, 0, 0, 0, 0)))
        args.append(bias)
        out_specs.append(pl.BlockSpec((1, 2, nb, 1, t), lambda p: (p, 0, 0, 0, 0)))
        out_shape.append(jax.ShapeDtypeStruct((n_pairs, 2, nb, 1, t), F32))
        out_specs.append(pl.BlockSpec((1, 2, nb, 1, t), lambda p: (p, 0, 0, 0, 0)))
        out_shape.append(jax.ShapeDtypeStruct((n_pairs, 2, nb, 1, t), F32))
    scratch = [pltpu.VMEM((nb, qw, t), F32), pltpu.VMEM((nb, LANES, t), F32)]
    scratch += _add_rider(rider, in_specs, args, out_specs, out_shape)
    return pl.pallas_call(
        body, name=name, grid=(n_pairs,), in_specs=in_specs, out_specs=out_specs, out_shape=out_shape,
        scratch_shapes=scratch,
        compiler_params=_params(("parallel",) if rider is None else ("arbitrary",), VMEM_LIMIT),
    )(*args)


def _alibi_slope(h):
    return 2.0 ** (-8.0 * (h + 1.0) / SWA_HEADS)


SWA_ROWS = 512
SWA_SCALE = SWA_DIM ** -0.5


def _swa_geometry(i):
    w = WINDOW
    r0 = pl.multiple_of(i * w, w)
    b0 = pl.multiple_of(jnp.maximum(i - 1, 0) * w, w)
    row = lax.broadcasted_iota(jnp.int32, (w, 2 * w), 0)
    col = lax.broadcasted_iota(jnp.int32, (w, 2 * w), 1)
    dist = row - col + (r0 - b0)
    valid = (dist >= 0) & (dist < w)
    return r0, b0, dist.astype(F32), valid


def _swa_q_head(qblk, h):
    kv = h // (SWA_HEADS // SWA_KV_HEADS)
    if h % 2 != kv:
        qblk = pltpu.roll(qblk, 64, axis=1)
    return jnp.where(_head_mask(qblk.shape, kv), qblk, 0.0)


SWA_GROUP = SWA_HEADS // SWA_KV_HEADS


def _swa_stack(ref, rs, grp):
    parts = []
    for a in range(SWA_GROUP):
        h = SWA_GROUP * grp + a
        parts.append(_swa_q_head(ref[rs, (h // 2) * LANES:(h // 2 + 1) * LANES].astype(F32), h))
    return jnp.concatenate(parts, axis=0)


def _swa_unstack(x, grp):
    tiles = []
    for a in range(SWA_GROUP):
        h = SWA_GROUP * grp + a
        tile = x[a * WINDOW:(a + 1) * WINDOW]
        tiles.append(pltpu.roll(tile, 64, axis=1) if h % 2 != grp else tile)
    return tiles


def _swa_head_column(vals):
    return jnp.concatenate([jnp.full((WINDOW, 1), v, F32) for v in vals], axis=0)


def _swa_logits(qs, kb, dist, valid, grp):
    slopes = _swa_head_column([_alibi_slope(SWA_GROUP * grp + a) for a in range(SWA_GROUP)])
    dist4 = jnp.concatenate([dist] * SWA_GROUP, axis=0)
    valid4 = jnp.concatenate([valid] * SWA_GROUP, axis=0)
    sc = lax.dot_general(qs, kb, _NT, preferred_element_type=F32) * SWA_SCALE - slopes * dist4
    return jnp.where(valid4, sc, NEG_INF)


def _swa_merge_heads(tiles):
    lt64 = lax.broadcasted_iota(jnp.int32, (WINDOW, LANES), 1) < 64
    return jnp.concatenate([jnp.where(lt64, tiles[2 * b], tiles[2 * b + 1]) for b in range(SWA_HEADS // 2)], axis=1)


def _swa_fwd(z0b, sinks, *, name):
    s = z0b.shape[0]
    w = WINDOW
    rows = min(SWA_ROWS, s)
    per_step = rows // w
    qcols = SWA_HEADS * SWA_DIM

    def body(sink_ref, q_ref, k_ref, v_ref, o_ref, lse_ref):
        g = pl.program_id(0)
        for ii in range(per_step):
            rs = slice(ii * w, (ii + 1) * w)
            r0, b0, dist, valid = _swa_geometry(g * per_step + ii)
            kb = k_ref[pl.ds(b0, 2 * w), :]
            vb = v_ref[pl.ds(b0, 2 * w), :]
            o_tiles = []
            for h in range(SWA_HEADS):
                kv = h // SWA_GROUP
                qh = _swa_q_head(q_ref[rs, (h // 2) * LANES:(h // 2 + 1) * LANES].astype(F32), h).astype(BF16)
                sc = lax.dot_general(qh, kb, _NT, preferred_element_type=F32) * SWA_SCALE - _alibi_slope(h) * dist
                sc = jnp.where(valid, sc, NEG_INF)
                sink = sink_ref[0, h]
                m = jnp.maximum(jnp.max(sc, axis=-1, keepdims=True), sink)
                p = jnp.exp(sc - m)
                l = jnp.sum(p, axis=-1, keepdims=True) + jnp.exp(sink - m)
                oh = jnp.dot(p.astype(BF16), vb, preferred_element_type=F32) / l
                o_tiles.append(pltpu.roll(oh, 64, axis=1) if h % 2 != kv else oh)
                lse_ref[h, rs, :] = m + jnp.log(l)
            o_ref[rs, :] = _swa_merge_heads(o_tiles)

    return pl.pallas_call(
        body, name=name, grid=(s // rows,),
        in_specs=[pl.BlockSpec(memory_space=pltpu.SMEM),
                  pl.BlockSpec((rows, qcols), lambda g: (g, 0)),
                  pl.BlockSpec((s, LANES), lambda g: (0, 4)), pl.BlockSpec((s, LANES), lambda g: (0, 5))],
        out_specs=[pl.BlockSpec((rows, qcols), lambda g: (g, 0)), pl.BlockSpec((SWA_HEADS, rows, 1), lambda g: (0, g, 0))],
        out_shape=[jax.ShapeDtypeStruct((s, qcols), F32), jax.ShapeDtypeStruct((SWA_HEADS, s, 1), F32)],
        compiler_params=_params(("parallel",), VMEM_LIMIT),
    )(sinks, z0b, z0b, z0b)


def _swa_bwd(z0b, sinks, do, o, lse, *, name):
    s = z0b.shape[0]
    w = WINDOW
    rows = min(SWA_ROWS, s)
    per_step = rows // w
    qcols = SWA_HEADS * SWA_DIM
    nblk = s // w

    def body(sink_ref, q_ref, k_ref, v_ref, do_ref, o_ref, lse_ref, dq_ref, dkt_ref, dvt_ref, dsink_ref):
        g = pl.program_id(0)

        @pl.when(g == 0)
        def _():
            dkt_ref[...] = jnp.zeros_like(dkt_ref)
            dvt_ref[...] = jnp.zeros_like(dvt_ref)
            dsink_ref[...] = jnp.zeros_like(dsink_ref)

        for ii in range(per_step):
            i = g * per_step + ii
            rs = slice(ii * w, (ii + 1) * w)
            r0, b0, dist, valid = _swa_geometry(i)
            j0 = jnp.maximum(i - 1, 0)
            kb = k_ref[pl.ds(b0, 2 * w), :]
            vb = v_ref[pl.ds(b0, 2 * w), :]
            dq_tiles = []
            for grp in range(SWA_KV_HEADS):
                heads = [SWA_GROUP * grp + a for a in range(SWA_GROUP)]
                qs32 = _swa_stack(q_ref, rs, grp)
                dos32 = _swa_stack(do_ref, rs, grp)
                delta = jnp.sum(dos32 * _swa_stack(o_ref, rs, grp), axis=-1, keepdims=True)
                lse = jnp.concatenate([lse_ref[h, rs, :] for h in heads], axis=0)
                sink = _swa_head_column([sink_ref[0, h] for h in heads])
                p = jnp.exp(_swa_logits(qs32.astype(BF16), kb, dist, valid, grp) - lse)
                dp = lax.dot_general(dos32.astype(BF16), vb, _NT, preferred_element_type=F32)
                ds = p * (dp - delta)
                dsb = ds.astype(BF16)
                d_sink = jnp.exp(sink - lse) * delta
                for a, h in enumerate(heads):
                    dsink_ref[h:h + 1, :] += jnp.broadcast_to(-jnp.sum(d_sink[a * w:(a + 1) * w]), (1, LANES))
                dvt = jnp.dot(dos32.T.astype(BF16), p.astype(BF16), preferred_element_type=F32)
                dkt = jnp.dot(qs32.T.astype(BF16), dsb, preferred_element_type=F32) * SWA_SCALE
                dvt_ref[j0] += dvt[:, :w]
                dvt_ref[j0 + 1] += dvt[:, w:]
                dkt_ref[j0] += dkt[:, :w]
                dkt_ref[j0 + 1] += dkt[:, w:]
                dq_tiles += _swa_unstack(jnp.dot(dsb, kb, preferred_element_type=F32) * SWA_SCALE, grp)
            dq_ref[rs, :] = _swa_merge_heads(dq_tiles)

    return pl.pallas_call(
        body, name=name, grid=(s // rows,),
        in_specs=[pl.BlockSpec(memory_space=pltpu.SMEM),
                  pl.BlockSpec((rows, qcols), lambda g: (g, 0)),
                  pl.BlockSpec((s, LANES), lambda g: (0, 4)), pl.BlockSpec((s, LANES), lambda g: (0, 5)),
                  pl.BlockSpec((rows, qcols), lambda g: (g, 0)), pl.BlockSpec((rows, qcols), lambda g: (g, 0)),
                  pl.BlockSpec((SWA_HEADS, rows, 1), lambda g: (0, g, 0))],
        out_specs=[pl.BlockSpec((rows, qcols), lambda g: (g, 0)),
                   pl.BlockSpec((nblk, LANES, w), lambda g: (0, 0, 0)),
                   pl.BlockSpec((nblk, LANES, w), lambda g: (0, 0, 0)),
                   pl.BlockSpec((SWA_HEADS, LANES), lambda g: (0, 0))],
        out_shape=[jax.ShapeDtypeStruct((s, qcols), F32),
                   jax.ShapeDtypeStruct((nblk, LANES, w), F32), jax.ShapeDtypeStruct((nblk, LANES, w), F32),
                   jax.ShapeDtypeStruct((SWA_HEADS, LANES), F32)],
        compiler_params=_params(("arbitrary",), VMEM_LIMIT),
    )(sinks, z0b, z0b, z0b, do, o, lse)


CUM_T = 256


def _split3(x):
    hi = x.astype(BF16)
    r1 = x - hi.astype(F32)
    mid = r1.astype(BF16)
    lo = (r1 - mid.astype(F32)).astype(BF16)
    return hi, mid, lo


def _tri_dot(tri, x):
    hi, mid, lo = _split3(x)
    out = jnp.dot(tri, hi, preferred_element_type=F32)
    out = out + jnp.dot(tri, mid, preferred_element_type=F32)
    return out + jnp.dot(tri, lo, preferred_element_type=F32)


def _logf_fwd(zf, bf, *, name):
    s = zf.shape[0]
    t = CUM_T
    nb = s // t

    def body(z_ref, b_ref, c_ref, carry_ref):
        i = pl.program_id(0)

        @pl.when(i == 0)
        def _():
            carry_ref[...] = jnp.zeros_like(carry_ref)

        x = z_ref[...] + b_ref[...]
        lf = jnp.minimum(x, 0.0) - jnp.log(1.0 + jnp.exp(-jnp.abs(x)))
        row = lax.broadcasted_iota(jnp.int32, (t, t), 0)
        col = lax.broadcasted_iota(jnp.int32, (t, t), 1)
        tri = jnp.where(col <= row, 1.0, 0.0).astype(BF16)
        c = _tri_dot(tri, lf) + carry_ref[...]
        c_ref[...] = c
        carry_ref[...] = c[t - 1:t, :]

    return pl.pallas_call(
        body, name=name, grid=(nb,),
        in_specs=[pl.BlockSpec((t, LANES), lambda i: (i, 0)), pl.BlockSpec((1, LANES), lambda i: (0, 0))],
        out_specs=pl.BlockSpec((t, LANES), lambda i: (i, 0)),
        out_shape=jax.ShapeDtypeStruct((s, LANES), F32),
        scratch_shapes=[pltpu.VMEM((1, LANES), F32)],
        compiler_params=_params(("arbitrary",)),
    )(zf, bf)


def _logf_bwd(dc, zf, bf, *, name):
    s = zf.shape[0]
    t = CUM_T
    nb = s // t

    def body(dc_ref, z_ref, b_ref, dz_ref, db_ref, carry_ref):
        i = pl.program_id(0)

        @pl.when(i == 0)
        def _():
            carry_ref[...] = jnp.zeros_like(carry_ref)
            db_ref[...] = jnp.zeros_like(db_ref)

        row = lax.broadcasted_iota(jnp.int32, (t, t), 0)
        col = lax.broadcasted_iota(jnp.int32, (t, t), 1)
        tri = jnp.where(col >= row, 1.0, 0.0).astype(BF16)
        dlf = _tri_dot(tri, dc_ref[...]) + carry_ref[...]
        carry_ref[...] = dlf[0:1, :]
        x = z_ref[...] + b_ref[...]
        dz = dlf * _sigmoid(-x)
        dz_ref[...] = dz.astype(BF16)
        db_ref[...] += jnp.sum(dz, axis=0, keepdims=True)

    return pl.pallas_call(
        body, name=name, grid=(nb,),
        in_specs=[pl.BlockSpec((t, LANES), lambda i: (nb - 1 - i, 0)), pl.BlockSpec((t, LANES), lambda i: (nb - 1 - i, 0)),
                  pl.BlockSpec((1, LANES), lambda i: (0, 0))],
        out_specs=[pl.BlockSpec((t, LANES), lambda i: (nb - 1 - i, 0)), pl.BlockSpec((1, LANES), lambda i: (0, 0))],
        out_shape=[jax.ShapeDtypeStruct((s, LANES), BF16), jax.ShapeDtypeStruct((1, LANES), F32)],
        scratch_shapes=[pltpu.VMEM((1, LANES), F32)],
        compiler_params=_params(("arbitrary",)),
    )(dc, zf, bf)


def _loss_head(x2, g, target, *, name):
    s = x2.shape[0]
    tm = _tile(s, (256, 128))

    def body(x_ref, g_ref, t_ref, dx_ref, loss_ref, dg_ref):
        i = pl.program_id(0)

        @pl.when(i == 0)
        def _():
            loss_ref[...] = jnp.zeros_like(loss_ref)
            dg_ref[...] = jnp.zeros_like(dg_ref)

        xf = x_ref[...]
        r = lax.rsqrt(jnp.mean(xf * xf, axis=-1, keepdims=True) + EPS)
        xh = xf * r
        gv = g_ref[...]
        err = xh * gv - t_ref[...]
        loss_ref[...] += jnp.broadcast_to(0.5 * jnp.sum(jnp.mean(err * err, axis=-1, keepdims=True)), loss_ref.shape)
        dy = err * (1.0 / D_MODEL)
        dg_ref[...] += jnp.sum(dy * xh, axis=0, keepdims=True)
        dxh = dy * gv
        dx_ref[...] = r * (dxh - xh * jnp.mean(dxh * xh, axis=-1, keepdims=True))

    return pl.pallas_call(
        body, name=name, grid=(s // tm,),
        in_specs=[pl.BlockSpec((tm, D_MODEL), lambda i: (i, 0)), pl.BlockSpec((1, D_MODEL), lambda i: (0, 0)),
                  pl.BlockSpec((tm, D_MODEL), lambda i: (i, 0))],
        out_specs=[pl.BlockSpec((tm, D_MODEL), lambda i: (i, 0)), pl.BlockSpec((8, LANES), lambda i: (0, 0)),
                   pl.BlockSpec((1, D_MODEL), lambda i: (0, 0))],
        out_shape=[jax.ShapeDtypeStruct((s, D_MODEL), F32), jax.ShapeDtypeStruct((8, LANES), F32),
                   jax.ShapeDtypeStruct((1, D_MODEL), F32)],
        compiler_params=_params(("arbitrary",)),
    )(x2, g, target)


def _sum_pieces(p_ref):
    g = p_ref[0].astype(F32)
    for k in range(1, N_DEV):
        g = g + p_ref[k].astype(F32)
    return g


def _adam_update(g, w, m, v):
    bc1 = 1.0 - ADAM_B1 ** ADAM_STEP
    bc2 = 1.0 - ADAM_B2 ** ADAM_STEP
    nm = ADAM_B1 * m + (1.0 - ADAM_B1) * g
    nv = ADAM_B2 * v + (1.0 - ADAM_B2) * (g * g)
    m_hat = nm / bc1
    v_hat = nv / bc2
    return -ADAM_LR * (m_hat / (jnp.sqrt(v_hat) + ADAM_EPS) + ADAM_WD * w), nm, nv


def _adamw(pieces, w, m, v, *, name):
    rows, cols = w.shape
    tr = _tile(rows, (RB1, RB0, SMALL_ROWS))

    def body(p_ref, w_ref, m_ref, v_ref, g_ref, d_ref, nm_ref, nv_ref):
        g = _sum_pieces(p_ref)
        g_ref[...] = g
        d_ref[...], nm_ref[...], nv_ref[...] = _adam_update(g, w_ref[...], m_ref[...], v_ref[...])

    spec = pl.BlockSpec((tr, cols), lambda i: (i, 0))
    shape = jax.ShapeDtypeStruct((rows, cols), F32)
    return pl.pallas_call(
        body, name=name, grid=(rows // tr,),
        in_specs=[pl.BlockSpec((N_DEV, tr, cols), lambda i: (0, i, 0)), spec, spec, spec],
        out_specs=[spec, spec, spec, spec], out_shape=[shape, shape, shape, shape],
        compiler_params=_params(("parallel",)),
    )(pieces, w, m, v)


def _sum8(pieces, rows, *, name):
    cols = pieces.shape[2]
    tr = _tile(rows, (176, 96))

    def body(p_ref, g_ref):
        g_ref[...] = _sum_pieces(p_ref)

    return pl.pallas_call(
        body, name=name, grid=(rows // tr,),
        in_specs=[pl.BlockSpec((N_DEV, tr, cols), lambda i: (0, i, 0))],
        out_specs=pl.BlockSpec((tr, cols), lambda i: (i, 0)),
        out_shape=jax.ShapeDtypeStruct((rows, cols), F32),
        compiler_params=_params(("parallel",)),
    )(pieces)


def _adamw_native(g, w, m, v, *, name):
    rows, cols = w.shape
    tr = _tile(rows, (256, 128))

    def body(g_ref, w_ref, m_ref, v_ref, d_ref, nm_ref, nv_ref):
        d_ref[...], nm_ref[...], nv_ref[...] = _adam_update(g_ref[...], w_ref[...], m_ref[...], v_ref[...])

    spec = pl.BlockSpec((tr, cols), lambda i: (i, 0))
    shape = jax.ShapeDtypeStruct((rows, cols), F32)
    return pl.pallas_call(
        body, name=name, grid=(rows // tr,), in_specs=[spec, spec, spec, spec],
        out_specs=[spec, spec, spec], out_shape=[shape, shape, shape],
        compiler_params=_params(("parallel",)),
    )(g, w, m, v)


MESH = pl.DeviceIdType.MESH
ANY = pl.BlockSpec(memory_space=pl.ANY)


def _all_gather(shard, *, name):
    rows, lanes = shard.shape

    def body(x_ref, out_ref, send_sems, recv_sems, local_sem):
        x, y, c = lax.axis_index("x"), lax.axis_index("y"), lax.axis_index("c")
        me, sibling = (x, y, c), (x, y, 1 - c)
        chips = [(1 - x, y), (x, 1 - y), (1 - x, 1 - y)]

        def block(px, py, pc):
            return out_ref.at[4 * px + 2 * py + pc]

        def copy(k, blk, to, src=None):
            return pltpu.make_async_remote_copy(
                src_ref=block(*blk) if src is None else src, dst_ref=block(*blk),
                send_sem=send_sems.at[k], recv_sem=recv_sems.at[k], device_id=to, device_id_type=MESH)

        mine = pltpu.make_async_copy(x_ref, block(*me), local_sem)
        mine.start()
        first = [copy(0, me, sibling, src=x_ref)]
        first += [copy(1 + j, me, (*chip, c), src=x_ref) for j, chip in enumerate(chips)]
        for cp in first:
            cp.start()
        passed = [copy(4 + j, (*chip, c), sibling) for j, chip in enumerate(chips)]
        for j, chip in enumerate(chips):
            copy(1 + j, (*chip, c), me).wait_recv()
            passed[j].start()
        copy(0, sibling, me).wait_recv()
        for j, chip in enumerate(chips):
            copy(4 + j, (*chip, 1 - c), me).wait_recv()
        for cp in first + passed:
            cp.wait_send()
        mine.wait()

    return pl.pallas_call(
        body, name=name, out_shape=jax.ShapeDtypeStruct((N_DEV, rows, lanes), shard.dtype),
        in_specs=[ANY], out_specs=ANY,
        scratch_shapes=[pltpu.SemaphoreType.DMA((7,)), pltpu.SemaphoreType.DMA((7,)), pltpu.SemaphoreType.DMA(())],
    )(shard)


def _peer_copies(kind, src_ref, out_ref, send_sems, recv_sems, local_sem):
    x, y, c = lax.axis_index("x"), lax.axis_index("y"), lax.axis_index("c")
    me = 4 * x + 2 * y + c

    def src(idx):
        return src_ref.at[idx] if kind == "exchange" else src_ref

    mine = pltpu.make_async_copy(src(me), out_ref.at[me], local_sem)
    copies = []
    for r in range(1, N_DEV):
        px = 1 - x if r & 4 else x
        py = 1 - y if r & 2 else y
        pc = 1 - c if r & 1 else c
        copies.append(pltpu.make_async_remote_copy(
            src_ref=src(4 * px + 2 * py + pc), dst_ref=out_ref.at[me],
            send_sem=send_sems.at[r - 1], recv_sem=recv_sems.at[r - 1],
            device_id=(px, py, pc), device_id_type=MESH))
    return mine, copies


PEER_SEMS = [pltpu.SemaphoreType.DMA((7,)), pltpu.SemaphoreType.DMA((7,)), pltpu.SemaphoreType.DMA(())]


def _add_rider(rider, in_specs, args, out_specs, out_shape):
    if rider is None:
        return []
    _, arr = rider
    in_specs.append(ANY)
    args.append(arr)
    out_specs.append(ANY)
    out_shape.append(jax.ShapeDtypeStruct((N_DEV,) + arr.shape[-2:], arr.dtype))
    return list(PEER_SEMS)


def _split_rider(refs, rider, n_in, n_out):
    if rider is None:
        return refs, None
    refs = list(refs)
    rin = refs.pop(n_in)
    rout = refs.pop(n_in + n_out)
    return refs[:-3], (rin, rout, *refs[-3:])


def _ride_start(rider, ride_refs, first):
    if rider is None:
        return

    @pl.when(first)
    def _():
        mine, copies = _peer_copies(rider[0], *ride_refs)
        mine.start()
        for cp in copies:
            cp.start()


def _ride_wait(rider, ride_refs, last):
    if rider is None:
        return

    @pl.when(last)
    def _():
        mine, copies = _peer_copies(rider[0], *ride_refs)
        for cp in copies:
            cp.wait()
        mine.wait()


def _gathered_cols(blocks, kdim):
    n = blocks.shape[1] * WIDE // kdim
    return blocks.reshape(N_DEV, kdim, n).transpose(1, 0, 2).reshape(kdim, N_DEV * n)


def _scatter_cols(dw):
    kdim, n8 = dw.shape
    n = n8 // N_DEV
    return dw.reshape(kdim, N_DEV, n).transpose(1, 0, 2).reshape(N_DEV, kdim * n // WIDE, WIDE)


def _pad_rows(a, rows):
    pad = [(0, 0)] * a.ndim
    pad[-2] = (0, rows - a.shape[-2])
    return jnp.pad(a, pad)


def _layer0_in_weight_t(wt):
    cq, ckv, kpe = wt[0:256], wt[256:384], wt[384:416]
    q_s, k_s, v_s, gate = wt[416:928], wt[928:1056], wt[1056:1184], wt[1184:2208]
    z = jnp.zeros((64, wt.shape[1]), wt.dtype)
    return jnp.concatenate([gate, cq, ckv, z, kpe, z[:32], q_s, k_s, v_s], axis=0)


def _layer0_in_grad_t(dwt):
    gate, cq, ckv, kpe = dwt[0:1024], dwt[1024:1280], dwt[1280:1408], dwt[1472:1504]
    q_s, k_s, v_s = dwt[1536:2048], dwt[2048:2176], dwt[2176:2304]
    return jnp.concatenate([cq, ckv, kpe, q_s, k_s, v_s, gate], axis=0)


def _layer1_in_weight_t(wt):
    main = jnp.concatenate([wt[:3 * D_MODEL], wt[3 * D_MODEL + FOX_HEADS:]], axis=0)
    return main, _pad_rows(wt[3 * D_MODEL:3 * D_MODEL + FOX_HEADS], LANES)


def _layer1_in_grad_t(d_main, d_wft):
    return jnp.concatenate([d_main[:3 * D_MODEL], d_wft[:FOX_HEADS], d_main[3 * D_MODEL:]], axis=0)


def _q_up_weight(w):
    return jnp.pad(w.reshape(MLA_Q_RANK, MLA_HEADS, 96), ((0, 0), (0, 0), (0, 32))).reshape(MLA_Q_RANK, MLA_HEADS * LANES)


def _q_up_grad(dwp):
    return dwp.reshape(MLA_Q_RANK, MLA_HEADS, LANES)[:, :, :96].reshape(MLA_Q_RANK, MLA_HEADS * 96)


def _kv_up_weight(w):
    w4 = w.reshape(MLA_KV_RANK, MLA_HEADS, 2, 64)
    kp = jnp.pad(w4[:, :, 0, :], ((0, 0), (0, 0), (0, 64))).reshape(MLA_KV_RANK, MLA_HEADS * LANES)
    vp = w4[:, :, 1, :].reshape(MLA_KV_RANK, MLA_HEADS * 64)
    return jnp.concatenate([kp, vp], axis=1)


def _kv_up_grad(dwp):
    dk = dwp[:, :MLA_HEADS * LANES].reshape(MLA_KV_RANK, MLA_HEADS, LANES)[:, :, :64]
    dv = dwp[:, MLA_HEADS * LANES:].reshape(MLA_KV_RANK, MLA_HEADS, 64)
    return jnp.stack([dk, dv], axis=2).reshape(MLA_KV_RANK, MLA_HEADS * LANES)


def _pad_lanes(a):
    return jnp.pad(a, ((0, 0), (0, LANES - a.shape[1])))


def _small_pack(g_in, g_final, g_q_a, g_kv_a, sinks, b_f, loss):
    rows = [g_in.reshape(8, LANES), g_final.reshape(8, LANES), g_q_a.reshape(2, LANES), g_kv_a.reshape(1, LANES),
            _pad_lanes(sinks.reshape(1, -1)), _pad_lanes(b_f.reshape(1, -1)), _pad_lanes(loss.reshape(1, 1)),
            jnp.zeros((2, LANES), F32)]
    return jnp.concatenate(rows, axis=0)


def _small_unpack(a):
    return (a[0:8].reshape(1, D_MODEL), a[8:16].reshape(D_MODEL), a[16:18].reshape(1, MLA_Q_RANK),
            a[18:19].reshape(1, MLA_KV_RANK), a[19:20, :SWA_HEADS], a[20:21, :FOX_HEADS], a[21, 0])


def _local_step(x, positions, target, e_g_in, w0t, e_g_q_a, wq, e_g_kv_a, wkv, e_sinks,
                late, o_b_f, g_final, scatter1=None, scatter0=None):
    s = x.shape[0]
    att_t = min(ATT_T, s)
    nb = s // att_t
    mla_scale = (MLA_NOPE + MLA_ROPE) ** -0.5
    fox_scale = FOX_DIM ** -0.5
    n0a = Z0A_UNITS * LANES

    inv_freq = 1.0 / (ROPE_THETA ** (jnp.arange(0, MLA_ROPE, 2, dtype=F32) / MLA_ROPE))
    ang = positions.astype(F32)[:, None] * inv_freq
    cos, sin = jnp.cos(ang), jnp.sin(ang)
    ones, zeros = jnp.ones((s, 64), F32), jnp.zeros((s, 64), F32)
    cos_t = jnp.concatenate([ones, cos, cos, ones[:, :32]], axis=1)
    sin_t = jnp.concatenate([zeros, -sin, sin, zeros[:, :32]], axis=1)

    h0 = _rmsnorm_fwd(x, e_g_in, width=D_MODEL, col_blk=0, name="l0_norm")
    z0a = _matmul(h0, w0t, tb=True, b_rows=(0, n0a), name="l0_in_a")
    z0b = _matmul(h0, w0t, tb=True, b_rows=(n0a, Z0B_UNITS * LANES), name="l0_in_b", out_dtype=BF16)
    cqn = _rmsnorm_fwd(z0a, e_g_q_a, width=MLA_Q_RANK, col_blk=4, name="l0_q_norm")
    ckvn = _rmsnorm_fwd(z0a, e_g_kv_a, width=MLA_KV_RANK, col_blk=10, name="l0_kv_norm")
    qp = _matmul(cqn, wq, name="l0_q_up")
    kvp = _matmul(ckvn, wkv, name="l0_kv_up", out_dtype=BF16)
    qm, km = _rope_fwd(qp, kvp, z0a, cos_t, sin_t, name="l0_rope")
    gathers = len(late) == 2
    res = _flash_fwd(qm, km, kvp, None, n_pairs=MLA_HEADS // 2, hw=LANES, q_off=0, k_off=0, v_off=MLA_HEADS,
                     scale=mla_scale, name="l0_mla_fwd", rider=("gather", late[0]) if gathers else None)
    o_mla, lse_mla = res[0], res[1]
    wo0, o_g_in, w1t, wft, wo1 = late[1](res[2]) if gathers else late
    o_swa, lse_swa = _swa_fwd(z0b, e_sinks, name="l0_swa_fwd")
    og0 = _gate_fwd([o_mla, o_swa], z0a, name="l0_gate")
    x1 = _matmul(og0, wo0, add=x, name="l0_out")

    h1 = _rmsnorm_fwd(x1, o_g_in, width=D_MODEL, col_blk=0, name="l1_norm")
    z1 = _matmul(h1, w1t, tb=True, b_rows=(0, 3 * D_MODEL), name="l1_in_qkv", out_dtype=BF16)
    gate1 = _matmul(h1, w1t, tb=True, b_rows=(3 * D_MODEL, D_MODEL), name="l1_in_gate")
    zf = _matmul(h1, wft, tb=True, name="l1_in_f")
    bf = _pad_lanes(o_b_f)
    log_cum = _logf_fwd(zf, bf, name="l1_logf")
    bias = (-LOG2E * log_cum[:, :FOX_HEADS]).T.reshape(FOX_HEADS // 2, 2, nb, 1, att_t)
    o_fox, lse_fox = _flash_fwd(z1, z1, z1, bias, n_pairs=FOX_HEADS // 2, hw=64, q_off=0, k_off=8, v_off=16,
                                scale=fox_scale, name="l1_fox_fwd")
    og1 = _gate_fwd([o_fox], gate1, name="l1_gate")
    x2 = _matmul(og1, wo1, add=x1, name="l1_out")

    dx2, loss_part, d_g_final = _loss_head(x2, g_final.reshape(1, D_MODEL), target, name="loss_head")

    d_wo1 = _matmul(og1, dx2, ta=True, name="l1_out_dw")
    d_og1 = _matmul(dx2, wo1, tb=True, name="l1_out_dx")
    do_fox, d_gate1 = _gate_bwd(d_og1, [o_fox], gate1, name="l1_gate_bwd")
    dq1, dk1, dv1, dbias, drow = _flash_bwd(z1, z1, z1, do_fox, o_fox, lse_fox, bias, n_pairs=FOX_HEADS // 2, hw=64,
                                            q_off=0, k_off=8, v_off=16, scale=fox_scale, qk_dtype=BF16,
                                            name="l1_fox_bwd")
    d_log_cum = (drow.reshape(FOX_HEADS, s) - dbias.reshape(FOX_HEADS, s)).T
    d_log_cum = jnp.pad(d_log_cum, ((0, 0), (0, LANES - FOX_HEADS)))
    d_zf, d_bf = _logf_bwd(d_log_cum, zf, bf, name="l1_logf_bwd")
    dz1 = jnp.concatenate([dq1, dk1, dv1, d_gate1], axis=1)
    d_w1t = _matmul(dz1, h1, ta=True, name="l1_in_dw")
    d_wft = _matmul(d_zf, h1, ta=True, name="l1_in_f_dw")
    dh1 = _matmul(dz1, w1t, name="l1_in_dx")
    dh1 = _matmul(d_zf, wft, add=dh1, name="l1_in_f_dx")
    dx1, d_o_g_in = _rmsnorm_bwd(x1, o_g_in, dh1, width=D_MODEL, col_blk=0, add=dx2, name="l1_norm_bwd")

    d_wo0 = _matmul(og0, dx1, ta=True, name="l0_out_dw")
    d_og0 = _matmul(dx1, wo0, tb=True, name="l0_out_dx")
    do_mla, do_swa, d_gate0 = _gate_bwd(d_og0, [o_mla, o_swa], z0a, name="l0_gate_bwd")
    dq_s, dkt_s, dvt_s, d_sinks = _swa_bwd(z0b, e_sinks, do_swa, o_swa, lse_swa, name="l0_swa_bwd")
    dk_s = dkt_s.transpose(0, 2, 1).reshape(s, LANES)
    dv_s = dvt_s.transpose(0, 2, 1).reshape(s, LANES)
    rider = None
    if scatter1 is not None:
        rider = ("exchange", scatter1(dict(w1t=d_w1t, wft=d_wft, wo1=d_wo1, o_g_in=d_o_g_in, wo0=d_wo0)))
    res = _flash_bwd(qm, km, kvp, do_mla, o_mla, lse_mla, None, n_pairs=MLA_HEADS // 2, hw=LANES, q_off=0, k_off=0,
                     v_off=MLA_HEADS, scale=mla_scale, qk_dtype=F32, name="l0_mla_bwd", rider=rider)
    dqm, dkm, dvm = res[0], res[1], res[2]
    recv1 = res[3] if rider is not None else None
    d_qp, d_kvp, d_kpe = _rope_bwd(dqm, dkm, dvm, cos_t, sin_t, name="l0_rope_bwd")
    d_wq = _matmul(cqn, d_qp, ta=True, name="l0_q_up_dw")
    d_cqn = _matmul(d_qp, wq, tb=True, name="l0_q_up_dx")
    d_wkv = _matmul(ckvn, d_kvp, ta=True, name="l0_kv_up_dw")
    d_ckvn = _matmul(d_kvp, wkv, tb=True, name="l0_kv_up_dx")
    d_cq, d_g_q_a = _rmsnorm_bwd(z0a, e_g_q_a, d_cqn, width=MLA_Q_RANK, col_blk=4, out_dtype=BF16, name="l0_q_norm_bwd")
    d_ckv, d_g_kv_a = _rmsnorm_bwd(z0a, e_g_kv_a, d_ckvn, width=MLA_KV_RANK, col_blk=10, out_dtype=BF16,
                                   name="l0_kv_norm_bwd")
    dz0 = jnp.concatenate([d_gate0, d_cq, d_ckv, d_kpe, dq_s.astype(BF16), dk_s.astype(BF16), dv_s.astype(BF16)], axis=1)
    d_w0t = _matmul(dz0, h0, ta=True, name="l0_in_dw")
    recv0 = None
    if scatter0 is None:
        dh0 = _matmul(dz0, w0t, name="l0_in_dx")
    else:
        dh0, recv0 = _matmul(dz0, w0t, name="l0_in_dx", rider=("exchange", scatter0(dict(w0t=d_w0t, wq=d_wq, wkv=d_wkv))))
    grad_x, d_e_g_in = _rmsnorm_bwd(x, e_g_in, dh0, width=D_MODEL, col_blk=0, add=dx1, name="l0_norm_bwd")

    return dict(recv0=recv0, recv1=recv1, loss=loss_part[0, 0], grad_x=grad_x, e_g_in=d_e_g_in, w0t=d_w0t, e_g_q_a=d_g_q_a, wq=d_wq,
                e_g_kv_a=d_g_kv_a, wkv=d_wkv, e_sinks=d_sinks[:, 0].reshape(1, SWA_HEADS), wo0=d_wo0,
                o_g_in=d_o_g_in, w1t=d_w1t, wft=d_wft, o_b_f=d_bf[:, :FOX_HEADS], wo1=d_wo1, g_final=d_g_final.reshape(D_MODEL))


def _wide(a, rows):
    flat = a.reshape(-1)
    return jnp.pad(flat, (0, rows * WIDE - flat.shape[0])).reshape(rows, WIDE)


def _rows_b0(w_q, w_kv):
    return jnp.concatenate([_wide(w_q, 32), _wide(w_kv, 16)], axis=0)


def _unflat_b0(f):
    return f[0:24].reshape(1, MLA_Q_RANK, 96), f[32:48].reshape(1, MLA_KV_RANK, 128)


def _rows_b1(o_w_out, e_w_out, g_in):
    return jnp.concatenate([o_w_out, e_w_out, _wide(g_in, 16)], axis=0)


def _unflat_b1(f):
    return f[0:128][None], f[128:256][None], f[256:257, :LANES]


def kernel(x, positions, e_g_in, e_w_in, e_g_q_a, e_w_q_up, e_g_kv_a, e_w_kv_up, e_sinks, e_w_out, o_g_in, o_w_in, o_b_f, o_w_out, g_final, loss_target, m_e_g_in, m_e_w_in, m_e_g_q_a, m_e_w_q_up, m_e_g_kv_a, m_e_w_kv_up, m_e_sinks, m_e_w_out, m_o_g_in, m_o_w_in, m_o_b_f, m_o_w_out, m_g_final, v_e_g_in, v_e_w_in, v_e_g_q_a, v_e_w_q_up, v_e_g_kv_a, v_e_w_kv_up, v_e_sinks, v_e_w_out, v_o_g_in, v_o_w_in, v_o_b_f, v_o_w_out, v_g_final):
    def bf(a):
        return a.astype(BF16)

    shard0 = jnp.concatenate([_pad_rows(bf(e_w_in[0]).T, RA0), _rows_b0(bf(e_w_q_up[0]), bf(e_w_kv_up[0]))], axis=0)
    gath0 = _all_gather(shard0, name="weights0_all_gather")
    w0t = _layer0_in_weight_t(gath0[:, :N_E_IN].reshape(N_DEV * N_E_IN, WIDE))
    wq = _q_up_weight(_gathered_cols(gath0[:, RA0:RA0 + 24], MLA_Q_RANK))
    wkv = _kv_up_weight(_gathered_cols(gath0[:, RA0 + 32:RA0 + 48], MLA_KV_RANK))

    g_bits = lax.bitcast_convert_type(o_g_in.reshape(LANES), BF16)
    shard1 = jnp.concatenate([_pad_rows(bf(o_w_in[0]).T, RA1), _rows_b1(bf(o_w_out[0]), bf(e_w_out[0]), g_bits)], axis=0)

    def unpack1(gath1):
        w1t, wft = _layer1_in_weight_t(gath1[:, :N_O_IN].reshape(N_DEV * N_O_IN, WIDE))
        wo1 = gath1[:, RA1:RA1 + 128].reshape(D_MODEL, D_MODEL)
        wo0 = gath1[:, RA1 + 128:RA1 + 256].reshape(D_MODEL, D_MODEL)
        bits = gath1[:, RA1 + 256, :2 * LANES].reshape(N_DEV, LANES, 2)
        return wo0, lax.bitcast_convert_type(bits, F32).reshape(1, D_MODEL), w1t, wft, wo1

    def scatter1(g):
        d_in_t = _layer1_in_grad_t(g["w1t"], g["wft"]).reshape(N_DEV, N_O_IN, WIDE)
        d_o_g = jnp.pad(g["o_g_in"].reshape(N_DEV, 1, LANES), ((0, 0), (0, 15), (0, WIDE - LANES)))
        return jnp.concatenate([_pad_rows(d_in_t, RA1), g["wo1"].reshape(N_DEV, 128, WIDE),
                                g["wo0"].reshape(N_DEV, 128, WIDE), d_o_g], axis=1).astype(BF16)

    def scatter0(g):
        return jnp.concatenate([
            _pad_rows(_layer0_in_grad_t(g["w0t"]).reshape(N_DEV, N_E_IN, WIDE), RA0),
            _pad_rows(_scatter_cols(_q_up_grad(g["wq"])), 32), _scatter_cols(_kv_up_grad(g["wkv"]))], axis=1).astype(BF16)

    gr = _local_step(x[0], positions[0], loss_target[0], e_g_in, w0t, e_g_q_a, wq, e_g_kv_a, wkv, e_sinks,
                     (shard1, unpack1), o_b_f, g_final, scatter1=scatter1, scatter0=scatter0)
    recv0 = gr["recv0"]

    def in_projection(recv, ra, n, w, m, v, name):
        g = _sum8(recv, ra, name=name + "_grad_sum")[:n].T
        d, nm, nv = _adamw_native(g, w[0], m[0], v[0], name=name + "_adamw")
        return g[None], d[None], nm[None], nv[None]

    e_in = in_projection(recv0, RA0, N_E_IN, e_w_in, m_e_w_in, v_e_w_in, "e_w_in")
    o_in = in_projection(gr["recv1"], RA1, N_O_IN, o_w_in, m_o_w_in, v_o_w_in, "o_w_in")
    b0 = _adamw(recv0[:, RA0:], _rows_b0(e_w_q_up[0], e_w_kv_up[0]), _rows_b0(m_e_w_q_up[0], m_e_w_kv_up[0]),
                _rows_b0(v_e_w_q_up[0], v_e_w_kv_up[0]), name="adamw_early")
    b1 = _adamw(gr["recv1"][:, RA1:], _rows_b1(o_w_out[0], e_w_out[0], o_g_in),
                _rows_b1(m_o_w_out[0], m_e_w_out[0], m_o_g_in), _rows_b1(v_o_w_out[0], v_e_w_out[0], v_o_g_in),
                name="adamw_late")

    def sharded(k):
        q_up, kv_up = _unflat_b0(b0[k])
        o_out, e_out, o_g = _unflat_b1(b1[k])
        return e_in[k], q_up, kv_up, e_out, o_in[k], o_out, o_g

    g_sh, d_sh, m_sh, v_sh = [sharded(k) for k in range(4)]

    small = _small_pack(gr["e_g_in"], gr["g_final"], gr["e_g_q_a"], gr["e_g_kv_a"], gr["e_sinks"], gr["o_b_f"], gr["loss"])
    small_all = _all_gather(small, name="small_all_gather")
    zero = jnp.zeros((), F32)
    w_small = _small_pack(e_g_in, g_final, e_g_q_a, e_g_kv_a, e_sinks, o_b_f, zero)
    m_small = _small_pack(m_e_g_in, m_g_final, m_e_g_q_a, m_e_g_kv_a, m_e_sinks, m_o_b_f, zero)
    v_small = _small_pack(v_e_g_in, v_g_final, v_e_g_q_a, v_e_g_kv_a, v_e_sinks, v_o_b_f, zero)
    smalls = _adamw(small_all, w_small, m_small, v_small, name="adamw_replicated")
    g_sm, d_sm, m_sm, v_sm = [_small_unpack(a) for a in smalls]
    loss = g_sm[6]

    def leaves(sh, sm):
        return (sm[0], sh[0], sm[2], sh[1], sm[3], sh[2], sm[4], sh[3], sh[6], sh[4], sm[5], sh[5], sm[1])

    return (loss, gr["grad_x"][None], *leaves(g_sh, g_sm), *leaves(d_sh, d_sm), *leaves(m_sh, m_sm), *leaves(v_sh, v_sm))
```

```python
import functools

import jax
import jax.numpy as jnp
from jax import lax
from jax.experimental import pallas as pl
from jax.experimental.pallas import tpu as pltpu

F32 = jnp.float32
BF16 = jnp.bfloat16
NEG_INF = float("-inf")

N_DEV = 8
LANES = 128
D_MODEL = 1024
EPS = 1e-6
ROPE_THETA = 10000.0
MLA_HEADS = 8
MLA_Q_RANK = 256
MLA_KV_RANK = 128
MLA_NOPE = 64
MLA_ROPE = 32
MLA_V = 64
SWA_HEADS = 8
SWA_KV_HEADS = 2
SWA_DIM = 64
WINDOW = 128
FOX_HEADS = 16
FOX_DIM = 64

ADAM_LR = 0.001
ADAM_B1 = 0.9
ADAM_B2 = 0.999
ADAM_EPS = 1e-08
ADAM_WD = 0.01
ADAM_STEP = 10

ATT_T = 512
ATT_T_FWD = 1024
VMEM_LIMIT = 56 * 1024 * 1024
MATMUL_B_BLOCK_BYTES = 8 * 1024 * 1024

Z0A_UNITS = 12
Z0B_UNITS = 6

WIDE = 1024
N_E_IN = 276
N_O_IN = 514
RA0 = 288
RB0 = 32 + 16
RA1 = 528
RB1 = 128 + 128 + 16
SMALL_ROWS = 24


def _tile(n, cands):
    for c in cands:
        if n % c == 0:
            return c
    raise ValueError(f"no tile for {n}")


def _params(sem, vmem=None):
    return pltpu.CompilerParams(dimension_semantics=sem, vmem_limit_bytes=vmem)


def _matmul(a, b, *, name, ta=False, tb=False, add=None, out_dtype=F32, b_rows=None, rider=None):
    if ta:
        kdim, m = a.shape
    else:
        m, kdim = a.shape
    if tb:
        n, kb = b.shape
    else:
        kb, n = b.shape
    assert kdim == kb, (a.shape, b.shape)
    b_start = 0
    if b_rows is not None:
        assert tb
        b_start, n = b_rows
    tm = _tile(m, (512, 256, 128))
    tn = _tile(n, [c for c in (1024, 768, 512, 384, 256, 128)
                   if c * kdim * b.dtype.itemsize <= MATMUL_B_BLOCK_BYTES and b_start % c == 0])
    assert b_start % tn == 0, (b_start, tn)
    b_off = b_start // tn
    dims = (((0 if ta else 1,), (1 if tb else 0,)), ((), ()))

    grid = (m // tm, n // tn)

    def body(*refs):
        refs, ride_refs = _split_rider(refs, rider, n_in=2 if add is None else 3, n_out=1)
        if add is None:
            a_ref, b_ref, o_ref = refs
            add_ref = None
        else:
            a_ref, b_ref, add_ref, o_ref = refs
        i, j = pl.program_id(0), pl.program_id(1)
        _ride_start(rider, ride_refs, (i == 0) & (j == 0))
        r = lax.dot_general(a_ref[...].astype(BF16), b_ref[...].astype(BF16), dims, preferred_element_type=F32)
        if add_ref is not None:
            r = r + add_ref[...]
        o_ref[...] = r.astype(out_dtype)
        _ride_wait(rider, ride_refs, (i == grid[0] - 1) & (j == grid[1] - 1))

    a_spec = pl.BlockSpec((kdim, tm), lambda i, j: (0, i)) if ta else pl.BlockSpec((tm, kdim), lambda i, j: (i, 0))
    b_spec = pl.BlockSpec((tn, kdim), lambda i, j: (j + b_off, 0)) if tb else pl.BlockSpec((kdim, tn), lambda i, j: (0, j))
    in_specs = [a_spec, b_spec]
    args = [a, b]
    if add is not None:
        in_specs.append(pl.BlockSpec((tm, tn), lambda i, j: (i, j)))
        args.append(add)
    out_specs = [pl.BlockSpec((tm, tn), lambda i, j: (i, j))]
    out_shape = [jax.ShapeDtypeStruct((m, n), out_dtype)]
    scratch = _add_rider(rider, in_specs, args, out_specs, out_shape)
    res = pl.pallas_call(
        body, name=name, grid=grid, in_specs=in_specs, out_specs=out_specs, out_shape=out_shape, scratch_shapes=scratch,
        compiler_params=_params(("parallel", "parallel") if rider is None else ("arbitrary", "arbitrary"), VMEM_LIMIT),
    )(*args)
    return res[0] if rider is None else res


def _rmsnorm_fwd(x, g, *, width, col_blk, name):
    s = x.shape[0]
    tm = _tile(s, (256, 128))

    def body(x_ref, g_ref, y_ref):
        xf = x_ref[...].astype(F32)
        r = lax.rsqrt(jnp.mean(xf * xf, axis=-1, keepdims=True) + EPS)
        y_ref[...] = ((xf * r) * g_ref[...]).astype(BF16)

    return pl.pallas_call(
        body, name=name, grid=(s // tm,),
        in_specs=[pl.BlockSpec((tm, width), lambda i: (i, col_blk)), pl.BlockSpec((1, width), lambda i: (0, 0))],
        out_specs=pl.BlockSpec((tm, width), lambda i: (i, 0)),
        out_shape=jax.ShapeDtypeStruct((s, width), BF16),
        compiler_params=_params(("parallel",)),
    )(x, g)


def _rmsnorm_bwd(x, g, dy, *, width, col_blk, name, add=None, out_dtype=F32):
    s = x.shape[0]
    tm = _tile(s, (256, 128))

    def body(*refs):
        if add is None:
            x_ref, g_ref, dy_ref, dx_ref, dg_ref = refs
            add_ref = None
        else:
            x_ref, g_ref, dy_ref, add_ref, dx_ref, dg_ref = refs
        i = pl.program_id(0)
        xf = x_ref[...].astype(F32)
        r = lax.rsqrt(jnp.mean(xf * xf, axis=-1, keepdims=True) + EPS)
        xh = xf * r
        dyf = dy_ref[...].astype(F32)

        @pl.when(i == 0)
        def _():
            dg_ref[...] = jnp.zeros_like(dg_ref)

        dg_ref[...] += jnp.sum(dyf * xh, axis=0, keepdims=True)
        dxh = dyf * g_ref[...]
        dx = r * (dxh - xh * jnp.mean(dxh * xh, axis=-1, keepdims=True))
        if add_ref is not None:
            dx = dx + add_ref[...]
        dx_ref[...] = dx.astype(out_dtype)

    in_specs = [pl.BlockSpec((tm, width), lambda i: (i, col_blk)), pl.BlockSpec((1, width), lambda i: (0, 0)),
                pl.BlockSpec((tm, width), lambda i: (i, 0))]
    args = [x, g, dy]
    if add is not None:
        in_specs.append(pl.BlockSpec((tm, width), lambda i: (i, 0)))
        args.append(add)
    return pl.pallas_call(
        body, name=name, grid=(s // tm,),
        in_specs=in_specs,
        out_specs=[pl.BlockSpec((tm, width), lambda i: (i, 0)), pl.BlockSpec((1, width), lambda i: (0, 0))],
        out_shape=[jax.ShapeDtypeStruct((s, width), out_dtype), jax.ShapeDtypeStruct((1, width), F32)],
        compiler_params=_params(("arbitrary",)),
    )(*args)


def _sigmoid(x):
    return 1.0 / (1.0 + jnp.exp(-x))


def _gate_fwd(o_parts, gate, *, name):
    s = gate.shape[0]
    tm = _tile(s, (256, 128))
    n_o = len(o_parts)

    def body(*refs):
        o_refs, g_ref, y_ref = refs[:n_o], refs[n_o], refs[n_o + 1]
        o = o_refs[0][...] if n_o == 1 else jnp.concatenate([r[...] for r in o_refs], axis=1)
        gt = g_ref[...]
        y_ref[...] = (o * (gt * _sigmoid(gt))).astype(BF16)

    in_specs = [pl.BlockSpec((tm, o.shape[1]), lambda i: (i, 0)) for o in o_parts]
    in_specs.append(pl.BlockSpec((tm, D_MODEL), lambda i: (i, 0)))
    return pl.pallas_call(
        body, name=name, grid=(s // tm,), in_specs=in_specs,
        out_specs=pl.BlockSpec((tm, D_MODEL), lambda i: (i, 0)),
        out_shape=jax.ShapeDtypeStruct((s, D_MODEL), BF16),
        compiler_params=_params(("parallel",)),
    )(*o_parts, gate)


def _gate_bwd(d_og, o_parts, gate, *, name):
    s = gate.shape[0]
    tm = _tile(s, (256, 128))
    n_o = len(o_parts)
    widths = [o.shape[1] for o in o_parts]

    def body(*refs):
        d_ref, o_refs, g_ref = refs[0], refs[1:1 + n_o], refs[1 + n_o]
        do_refs, dg_ref = refs[2 + n_o:2 + 2 * n_o], refs[2 + 2 * n_o]
        d = d_ref[...]
        gt = g_ref[...]
        sg = _sigmoid(gt)
        silu = gt * sg
        dsilu = sg * (1.0 + gt * (1.0 - sg))
        o = o_refs[0][...] if n_o == 1 else jnp.concatenate([r[...] for r in o_refs], axis=1)
        dg_ref[...] = (d * o * dsilu).astype(BF16)
        do = d * silu
        off = 0
        for r, w in zip(do_refs, widths):
            r[...] = do[:, off:off + w]
            off += w

    in_specs = [pl.BlockSpec((tm, D_MODEL), lambda i: (i, 0))]
    in_specs += [pl.BlockSpec((tm, w), lambda i: (i, 0)) for w in widths]
    in_specs.append(pl.BlockSpec((tm, D_MODEL), lambda i: (i, 0)))
    out_specs = [pl.BlockSpec((tm, w), lambda i: (i, 0)) for w in widths]
    out_specs.append(pl.BlockSpec((tm, D_MODEL), lambda i: (i, 0)))
    out_shape = [jax.ShapeDtypeStruct((s, w), F32) for w in widths]
    out_shape.append(jax.ShapeDtypeStruct((s, D_MODEL), BF16))
    return pl.pallas_call(
        body, name=name, grid=(s // tm,), in_specs=in_specs, out_specs=out_specs, out_shape=out_shape,
        compiler_params=_params(("parallel",)),
    )(d_og, *o_parts, gate)


def _rot_half(x):
    lane = lax.broadcasted_iota(jnp.int32, x.shape, 1)
    return jnp.where(lane < 80, pltpu.roll(x, LANES - 16, axis=1), pltpu.roll(x, 16, axis=1))


def _rot_half_t(g):
    lane = lax.broadcasted_iota(jnp.int32, g.shape, 1)
    lo = (lane >= MLA_NOPE) & (lane < MLA_NOPE + MLA_ROPE // 2)
    hi = (lane >= MLA_NOPE + MLA_ROPE // 2) & (lane < MLA_NOPE + MLA_ROPE)
    return jnp.where(lo, pltpu.roll(g, LANES - 16, axis=1), jnp.where(hi, pltpu.roll(g, 16, axis=1), 0.0))


def _rope_fwd(qp, kvp, z0a, cos_t, sin_t, *, name):
    s = qp.shape[0]
    tm = _tile(s, (256, 128))
    hw = MLA_HEADS * LANES

    def body(q_ref, k_ref, kpe_ref, c_ref, s_ref, qm_ref, km_ref):
        c = c_ref[...]
        sn = s_ref[...]
        kpe = kpe_ref[...]
        kpe_r = (kpe * c + _rot_half(kpe) * sn).astype(BF16)
        lane = lax.broadcasted_iota(jnp.int32, kpe.shape, 1)
        for h in range(MLA_HEADS):
            sl = slice(h * LANES, (h + 1) * LANES)
            qh = q_ref[:, sl]
            qm_ref[:, sl] = (qh * c + _rot_half(qh) * sn).astype(BF16)
            km_ref[:, sl] = jnp.where(lane < MLA_NOPE, k_ref[:, sl], kpe_r)

    return pl.pallas_call(
        body, name=name, grid=(s // tm,),
        in_specs=[pl.BlockSpec((tm, hw), lambda i: (i, 0)), pl.BlockSpec((tm, hw), lambda i: (i, 0)),
                  pl.BlockSpec((tm, LANES), lambda i: (i, 11)),
                  pl.BlockSpec((tm, LANES), lambda i: (i, 0)), pl.BlockSpec((tm, LANES), lambda i: (i, 0))],
        out_specs=[pl.BlockSpec((tm, hw), lambda i: (i, 0)), pl.BlockSpec((tm, hw), lambda i: (i, 0))],
        out_shape=[jax.ShapeDtypeStruct((s, hw), BF16), jax.ShapeDtypeStruct((s, hw), BF16)],
        compiler_params=_params(("parallel",)),
    )(qp, kvp, z0a, cos_t, sin_t)


def _rope_bwd(dqm, dkm, dvm, cos_t, sin_t, *, name):
    s = dqm.shape[0]
    tm = _tile(s, (256, 128))
    hw = MLA_HEADS * LANES
    vw = MLA_HEADS * MLA_V

    def body(dq_ref, dk_ref, dv_ref, c_ref, s_ref, dqp_ref, dkv_ref, dkpe_ref):
        c = c_ref[...]
        sn = s_ref[...]
        ksum = jnp.zeros((tm, LANES), F32)
        for h in range(MLA_HEADS):
            sl = slice(h * LANES, (h + 1) * LANES)
            dq = dq_ref[:, sl]
            dqp_ref[:, sl] = (dq * c + _rot_half_t(dq * sn)).astype(BF16)
            dk = dk_ref[:, sl]
            dkv_ref[:, sl] = dk.astype(BF16)
            ksum = ksum + dk
        dkv_ref[:, hw:] = dv_ref[...]
        lane = lax.broadcasted_iota(jnp.int32, ksum.shape, 1)
        dkpe = ksum * c + _rot_half_t(ksum * sn)
        dkpe_ref[...] = jnp.where((lane >= MLA_NOPE) & (lane < MLA_NOPE + MLA_ROPE), dkpe, 0.0).astype(BF16)

    return pl.pallas_call(
        body, name=name, grid=(s // tm,),
        in_specs=[pl.BlockSpec((tm, hw), lambda i: (i, 0)), pl.BlockSpec((tm, hw), lambda i: (i, 0)),
                  pl.BlockSpec((tm, vw), lambda i: (i, 0)),
                  pl.BlockSpec((tm, LANES), lambda i: (i, 0)), pl.BlockSpec((tm, LANES), lambda i: (i, 0))],
        out_specs=[pl.BlockSpec((tm, hw), lambda i: (i, 0)), pl.BlockSpec((tm, hw + vw), lambda i: (i, 0)),
                   pl.BlockSpec((tm, LANES), lambda i: (i, 0))],
        out_shape=[jax.ShapeDtypeStruct((s, hw), BF16), jax.ShapeDtypeStruct((s, hw + vw), BF16),
                   jax.ShapeDtypeStruct((s, LANES), BF16)],
        compiler_params=_params(("parallel",)),
    )(dqm, dkm, dvm, cos_t, sin_t)


def _head_mask(shape, a):
    lane = lax.broadcasted_iota(jnp.int32, shape, 1)
    return (lane >= 64 * a) & (lane < 64 * (a + 1))


def _causal_mask(t):
    row = lax.broadcasted_iota(jnp.int32, (t, t), 0)
    col = lax.broadcasted_iota(jnp.int32, (t, t), 1)
    return col <= row


_NT = (((1,), (1,)), ((), ()))
LOG2E = 1.4426950408889634


def _stack_heads(tile, hw):
    lane = lax.broadcasted_iota(jnp.int32, tile.shape, 1)
    z = jnp.zeros_like(tile)
    return jnp.concatenate([jnp.where(lane < hw, tile, z), jnp.where(lane >= hw, tile, z)], axis=0)


def _stacked_rows(r0, r1, t):
    n = r0.shape[-1]
    return jnp.concatenate([jnp.broadcast_to(r0, (t, n)), jnp.broadcast_to(r1, (t, n))], axis=0)


def _stacked_causal_mask(t):
    m = _causal_mask(t)
    return jnp.concatenate([m, m], axis=0)


def _resident(block, index_map):
    return pl.BlockSpec(block, index_map, pipeline_mode=pl.Buffered(1))


def _fwd_tile(s):
    return ATT_T_FWD if s % ATT_T_FWD == 0 else min(ATT_T, s)


def _flash_fwd(q, k, v, bias, *, n_pairs, hw, q_off, k_off, v_off, scale, name, rider=None):
    s = q.shape[0]
    t = _fwd_tile(s)
    nb = s // t
    qw = 2 * hw
    has_bias = bias is not None
    c1 = scale * LOG2E

    def body(*refs):
        refs, ride_refs = _split_rider(refs, rider, n_in=4 if has_bias else 3, n_out=2)
        if has_bias:
            q_ref, k_ref, v_ref, b_ref, o_ref, lse_ref, vt_ref, bcol_ref = refs
        else:
            q_ref, k_ref, v_ref, o_ref, lse_ref, vt_ref = refs
            b_ref = bcol_ref = None
        _ride_start(rider, ride_refs, pl.program_id(0) == 0)
        row = lax.broadcasted_iota(jnp.int32, (t, t), 0)
        col = lax.broadcasted_iota(jnp.int32, (t, t), 1)
        cmask_t = jnp.concatenate([row <= col, row <= col], axis=1)
        lane_lt64 = lax.broadcasted_iota(jnp.int32, (t, LANES), 1) < 64

        def as_column(r):
            return jnp.broadcast_to(r, (8, r.shape[1])).T[:, 0:1]

        def v_block(j, _):
            c0 = pl.multiple_of(j * t, t)
            vt_ref[j] = v_ref[pl.ds(c0, t), :].astype(F32).T.astype(BF16)
            if has_bias:
                for a in range(2):
                    bcol_ref[a, pl.ds(c0, t), :] = as_column(b_ref[0, a, j])
            return 0

        lax.fori_loop(0, nb, v_block, 0)

        def stacked_queries(i):
            return _stack_heads(q_ref[pl.ds(pl.multiple_of(i * t, t), t), :], hw).astype(F32).T.astype(BF16)

        def kv_step(j, carry, qs_t, masked):
            m, l, acc = carry
            rows = pl.ds(pl.multiple_of(j * t, t), t)
            sc = jnp.dot(k_ref[rows, :], qs_t, preferred_element_type=F32) * c1
            if has_bias:
                sc = sc + jnp.concatenate([jnp.broadcast_to(bcol_ref[0, rows, :], (t, t)),
                                           jnp.broadcast_to(bcol_ref[1, rows, :], (t, t))], axis=1)
            if masked:
                sc = jnp.where(cmask_t, sc, NEG_INF)
            m_new = jnp.maximum(m, jnp.max(sc, axis=0, keepdims=True))
            alpha = jnp.exp2(m - m_new)
            p = jnp.exp2(sc - m_new)
            l_new = alpha * l + jnp.sum(p, axis=0, keepdims=True)
            pv = jnp.dot(vt_ref[j], p.astype(BF16), preferred_element_type=F32)
            return m_new, l_new, alpha * acc + pv

        def finish(i, carry):
            m, l, acc = carry
            r0 = pl.multiple_of(i * t, t)
            out = (acc / l).T
            lse2 = as_column(m + jnp.log2(l))
            lse_ref[0, 0, pl.ds(r0, t), :] = lse2[:t]
            lse_ref[0, 1, pl.ds(r0, t), :] = lse2[t:]
            o_ref[pl.ds(r0, t), :] = jnp.where(lane_lt64, out[:t], out[t:])

        init = (jnp.full((1, 2 * t), NEG_INF, F32), jnp.zeros((1, 2 * t), F32), jnp.zeros((LANES, 2 * t), F32))

        def q_block(i, _):
            qs_t = stacked_queries(i)
            carry = lax.fori_loop(0, i, lambda j, c: kv_step(j, c, qs_t, False), init)
            finish(i, kv_step(i, carry, qs_t, True))
            return 0

        lax.fori_loop(0, nb, q_block, 0)
        _ride_wait(rider, ride_refs, pl.program_id(0) == n_pairs - 1)

    in_specs = [_resident((s, qw), lambda p: (0, q_off + p)), _resident((s, qw), lambda p: (0, k_off + p)),
                _resident((s, LANES), lambda p: (0, v_off + p))]
    args = [q, k, v]
    if has_bias:
        in_specs.append(_resident((1, 2, nb, 1, t), lambda p: (p, 0, 0, 0, 0)))
        args.append(bias)
    out_specs = [pl.BlockSpec((s, LANES), lambda p: (0, p)), pl.BlockSpec((1, 2, s, 1), lambda p: (p, 0, 0, 0))]
    out_shape = [jax.ShapeDtypeStruct((s, n_pairs * LANES), F32), jax.ShapeDtypeStruct((n_pairs, 2, s, 1), F32)]
    scratch = [pltpu.VMEM((nb, LANES, t), BF16)] + ([pltpu.VMEM((2, s, 1), F32)] if has_bias else [])
    scratch += _add_rider(rider, in_specs, args, out_specs, out_shape)
    return pl.pallas_call(
        body, name=name, grid=(n_pairs,), in_specs=in_specs, out_specs=out_specs, out_shape=out_shape,
        scratch_shapes=scratch,
        compiler_params=_params(("parallel",) if rider is None else ("arbitrary",), VMEM_LIMIT),
    )(*args)


def _flash_bwd(q, k, v, do, o, lse, bias, *, n_pairs, hw, q_off, k_off, v_off, scale, qk_dtype, name, rider=None):
    s = q.shape[0]
    t = min(ATT_T, s)
    nb = s // t
    qw = 2 * hw
    has_bias = bias is not None
    c1 = scale * LOG2E

    def body(*refs):
        refs, ride_refs = _split_rider(refs, rider, n_in=7 if has_bias else 6, n_out=5 if has_bias else 3)
        if has_bias:
            (q_ref, k_ref, v_ref, do_ref, o_ref, lse_ref, b_ref, dq_ref, dk_ref, dv_ref, db_ref, dr_ref,
             dkt_ref, dvt_ref) = refs
            db_ref[...] = jnp.zeros_like(db_ref)
        else:
            q_ref, k_ref, v_ref, do_ref, o_ref, lse_ref, dq_ref, dk_ref, dv_ref, dkt_ref, dvt_ref = refs
            b_ref = db_ref = dr_ref = None
        _ride_start(rider, ride_refs, pl.program_id(0) == 0)
        dkt_ref[...] = jnp.zeros_like(dkt_ref)
        dvt_ref[...] = jnp.zeros_like(dvt_ref)
        cmask = _stacked_causal_mask(t)
        lane_lt_hw = lax.broadcasted_iota(jnp.int32, (t, qw), 1) < hw

        def q_block(i, _):
            r0 = pl.multiple_of(i * t, t)
            qs = _stack_heads(q_ref[pl.ds(r0, t), :], hw)
            dos = _stack_heads(do_ref[pl.ds(r0, t), :], 64)
            ot = o_ref[pl.ds(r0, t), :]
            delta = jnp.sum(dos * jnp.concatenate([ot, ot], axis=0), axis=-1, keepdims=True)
            lse2 = jnp.concatenate([lse_ref[0, 0, pl.ds(r0, t), :], lse_ref[0, 1, pl.ds(r0, t), :]], axis=0)
            dosb = dos.astype(BF16)
            dos_t = dos.T.astype(BF16)
            qs_t = qs.astype(F32).T.astype(BF16)

            def kv_step(j, carry, masked):
                dq, rsum = carry
                c0 = pl.multiple_of(j * t, t)
                kt = k_ref[pl.ds(c0, t), :]
                vt = v_ref[pl.ds(c0, t), :]
                sc = lax.dot_general(qs, kt, _NT, preferred_element_type=F32) * c1
                if has_bias:
                    sc = sc + _stacked_rows(b_ref[0, 0, j], b_ref[0, 1, j], t)
                if masked:
                    sc = jnp.where(cmask, sc, NEG_INF)
                p = jnp.exp2(sc - lse2)
                dp = lax.dot_general(dosb, vt, _NT, preferred_element_type=F32)
                ds = p * (dp - delta)
                dsb = ds.astype(BF16)
                pb = p.astype(BF16)
                if hw == LANES:
                    dvt_ref[j] += jnp.concatenate(
                        [jnp.dot(dos_t[:64, :t], pb[:t], preferred_element_type=F32),
                         jnp.dot(dos_t[64:, t:], pb[t:], preferred_element_type=F32)], axis=0)
                    dkt_ref[j] += jnp.concatenate(
                        [jnp.dot(qs_t[:hw, :t], dsb[:t], preferred_element_type=F32),
                         jnp.dot(qs_t[hw:, t:], dsb[t:], preferred_element_type=F32)], axis=0)
                else:
                    dvt_ref[j] += jnp.dot(dos_t, pb, preferred_element_type=F32)
                    dkt_ref[j] += jnp.dot(qs_t, dsb, preferred_element_type=F32)
                if has_bias:
                    db_ref[0, 0, j] += jnp.sum(ds[:t], axis=0, keepdims=True)
                    db_ref[0, 1, j] += jnp.sum(ds[t:], axis=0, keepdims=True)
                    rsum = rsum + jnp.sum(ds, axis=-1, keepdims=True)
                return dq + jnp.dot(dsb, kt, preferred_element_type=F32), rsum

            init = (jnp.zeros((2 * t, qw), F32), jnp.zeros((2 * t, 1), F32))
            carry = lax.fori_loop(0, i, functools.partial(kv_step, masked=False), init)
            dq, rsum = kv_step(i, carry, True)
            dq = dq * scale
            dq_ref[pl.ds(r0, t), :] = jnp.where(lane_lt_hw, dq[:t], dq[t:]).astype(qk_dtype)
            if has_bias:
                rsum_row = jnp.broadcast_to(rsum, (2 * t, LANES)).T[0:1]
                dr_ref[0, 0, i] = rsum_row[:, :t]
                dr_ref[0, 1, i] = rsum_row[:, t:]
            return 0

        lax.fori_loop(0, nb, q_block, 0)

        def k_block(j, _):
            c0 = pl.multiple_of(j * t, t)
            dk_ref[pl.ds(c0, t), :] = (dkt_ref[j].T * scale).astype(qk_dtype)
            dv_ref[pl.ds(c0, t), :] = dvt_ref[j].T.astype(BF16)
            return 0

        lax.fori_loop(0, nb, k_block, 0)
        _ride_wait(rider, ride_refs, pl.program_id(0) == n_pairs - 1)

    in_specs = [_resident((s, qw), lambda p: (0, q_off + p)), _resident((s, qw), lambda p: (0, k_off + p)),
                _resident((s, LANES), lambda p: (0, v_off + p)),
                _resident((s, LANES), lambda p: (0, p)), _resident((s, LANES), lambda p: (0, p)),
                _resident((1, 2, s, 1), lambda p: (p, 0, 0, 0))]
    args = [q, k, v, do, o, lse]
    out_specs = [pl.BlockSpec((s, qw), lambda p: (0, p)), pl.BlockSpec((s, qw), lambda p: (0, p)),
                 pl.BlockSpec((s, LANES), lambda p: (0, p))]
    out_shape = [jax.ShapeDtypeStruct((s, n_pairs * qw), qk_dtype), jax.ShapeDtypeStruct((s, n_pairs * qw), qk_dtype),
                 jax.ShapeDtypeStruct((s, n_pairs * LANES), BF16)]
    if has_bias:
        in_specs.append(_resident((1, 2, nb, 1, t), lambda p: (p, 0, 0, 0, 0)))
        args.append(bias)
        out_specs.append(pl.BlockSpec((1, 2, nb, 1, t), lambda p: (p, 0, 0, 0, 0)))
        out_shape.append(jax.ShapeDtypeStruct((n_pairs, 2, nb, 1, t), F32))
        out_specs.append(pl.BlockSpec((1, 2, nb, 1, t), lambda p: (p, 0, 0, 0, 0)))
        out_shape.append(jax.ShapeDtypeStruct((n_pairs, 2, nb, 1, t), F32))
    scratch = [pltpu.VMEM((nb, qw, t), F32), pltpu.VMEM((nb, LANES, t), F32)]
    scratch += _add_rider(rider, in_specs, args, out_specs, out_shape)
    return pl.pallas_call(
        body, name=name, grid=(n_pairs,), in_specs=in_specs, out_specs=out_specs, out_shape=out_shape,
        scratch_shapes=scratch,
        compiler_params=_params(("parallel",) if rider is None else ("arbitrary",), VMEM_LIMIT),
    )(*args)


def _alibi_slope(h):
    return 2.0 ** (-8.0 * (h + 1.0) / SWA_HEADS)


SWA_ROWS = 512
SWA_SCALE = SWA_DIM ** -0.5


def _swa_geometry(i):
    w = WINDOW
    r0 = pl.multiple_of(i * w, w)
    b0 = pl.multiple_of(jnp.maximum(i - 1, 0) * w, w)
    row = lax.broadcasted_iota(jnp.int32, (w, 2 * w), 0)
    col = lax.broadcasted_iota(jnp.int32, (w, 2 * w), 1)
    dist = row - col + (r0 - b0)
    valid = (dist >= 0) & (dist < w)
    return r0, b0, dist.astype(F32), valid


def _swa_q_head(qblk, h):
    kv = h // (SWA_HEADS // SWA_KV_HEADS)
    if h % 2 != kv:
        qblk = pltpu.roll(qblk, 64, axis=1)
    return jnp.where(_head_mask(qblk.shape, kv), qblk, 0.0)


SWA_GROUP = SWA_HEADS // SWA_KV_HEADS


def _swa_stack(ref, rs, grp):
    parts = []
    for a in range(SWA_GROUP):
        h = SWA_GROUP * grp + a
        parts.append(_swa_q_head(ref[rs, (h // 2) * LANES:(h // 2 + 1) * LANES].astype(F32), h))
    return jnp.concatenate(parts, axis=0)


def _swa_unstack(x, grp):
    tiles = []
    for a in range(SWA_GROUP):
        h = SWA_GROUP * grp + a
        tile = x[a * WINDOW:(a + 1) * WINDOW]
        tiles.append(pltpu.roll(tile, 64, axis=1) if h % 2 != grp else tile)
    return tiles


def _swa_head_column(vals):
    return jnp.concatenate([jnp.full((WINDOW, 1), v, F32) for v in vals], axis=0)


def _swa_logits(qs, kb, dist, valid, grp):
    slopes = _swa_head_column([_alibi_slope(SWA_GROUP * grp + a) for a in range(SWA_GROUP)])
    dist4 = jnp.concatenate([dist] * SWA_GROUP, axis=0)
    valid4 = jnp.concatenate([valid] * SWA_GROUP, axis=0)
    sc = lax.dot_general(qs, kb, _NT, preferred_element_type=F32) * SWA_SCALE - slopes * dist4
    return jnp.where(valid4, sc, NEG_INF)


def _swa_merge_heads(tiles):
    lt64 = lax.broadcasted_iota(jnp.int32, (WINDOW, LANES), 1) < 64
    return jnp.concatenate([jnp.where(lt64, tiles[2 * b], tiles[2 * b + 1]) for b in range(SWA_HEADS // 2)], axis=1)


def _swa_fwd(z0b, sinks, *, name):
    s = z0b.shape[0]
    w = WINDOW
    rows = min(SWA_ROWS, s)
    per_step = rows // w
    qcols = SWA_HEADS * SWA_DIM

    def body(sink_ref, q_ref, k_ref, v_ref, o_ref, lse_ref):
        g = pl.program_id(0)
        for ii in range(per_step):
            rs = slice(ii * w, (ii + 1) * w)
            r0, b0, dist, valid = _swa_geometry(g * per_step + ii)
            kb = k_ref[pl.ds(b0, 2 * w), :]
            vb = v_ref[pl.ds(b0, 2 * w), :]
            o_tiles = []
            for h in range(SWA_HEADS):
                kv = h // SWA_GROUP
                qh = _swa_q_head(q_ref[rs, (h // 2) * LANES:(h // 2 + 1) * LANES].astype(F32), h).astype(BF16)
                sc = lax.dot_general(qh, kb, _NT, preferred_element_type=F32) * SWA_SCALE - _alibi_slope(h) * dist
                sc = jnp.where(valid, sc, NEG_INF)
                sink = sink_ref[0, h]
                m = jnp.maximum(jnp.max(sc, axis=-1, keepdims=True), sink)
                p = jnp.exp(sc - m)
                l = jnp.sum(p, axis=-1, keepdims=True) + jnp.exp(sink - m)
                oh = jnp.dot(p.astype(BF16), vb, preferred_element_type=F32) / l
                o_tiles.append(pltpu.roll(oh, 64, axis=1) if h % 2 != kv else oh)
                lse_ref[h, rs, :] = m + jnp.log(l)
            o_ref[rs, :] = _swa_merge_heads(o_tiles)

    return pl.pallas_call(
        body, name=name, grid=(s // rows,),
        in_specs=[pl.BlockSpec(memory_space=pltpu.SMEM),
                  pl.BlockSpec((rows, qcols), lambda g: (g, 0)),
                  pl.BlockSpec((s, LANES), lambda g: (0, 4)), pl.BlockSpec((s, LANES), lambda g: (0, 5))],
        out_specs=[pl.BlockSpec((rows, qcols), lambda g: (g, 0)), pl.BlockSpec((SWA_HEADS, rows, 1), lambda g: (0, g, 0))],
        out_shape=[jax.ShapeDtypeStruct((s, qcols), F32), jax.ShapeDtypeStruct((SWA_HEADS, s, 1), F32)],
        compiler_params=_params(("parallel",), VMEM_LIMIT),
    )(sinks, z0b, z0b, z0b)


def _swa_bwd(z0b, sinks, do, o, lse, *, name):
    s = z0b.shape[0]
    w = WINDOW
    rows = min(SWA_ROWS, s)
    per_step = rows // w
    qcols = SWA_HEADS * SWA_DIM
    nblk = s // w

    def body(sink_ref, q_ref, k_ref, v_ref, do_ref, o_ref, lse_ref, dq_ref, dkt_ref, dvt_ref, dsink_ref):
        g = pl.program_id(0)

        @pl.when(g == 0)
        def _():
            dkt_ref[...] = jnp.zeros_like(dkt_ref)
            dvt_ref[...] = jnp.zeros_like(dvt_ref)
            dsink_ref[...] = jnp.zeros_like(dsink_ref)

        for ii in range(per_step):
            i = g * per_step + ii
            rs = slice(ii * w, (ii + 1) * w)
            r0, b0, dist, valid = _swa_geometry(i)
            j0 = jnp.maximum(i - 1, 0)
            kb = k_ref[pl.ds(b0, 2 * w), :]
            vb = v_ref[pl.ds(b0, 2 * w), :]
            dq_tiles = []
            for grp in range(SWA_KV_HEADS):
                heads = [SWA_GROUP * grp + a for a in range(SWA_GROUP)]
                qs32 = _swa_stack(q_ref, rs, grp)
                dos32 = _swa_stack(do_ref, rs, grp)
                delta = jnp.sum(dos32 * _swa_stack(o_ref, rs, grp), axis=-1, keepdims=True)
                lse = jnp.concatenate([lse_ref[h, rs, :] for h in heads], axis=0)
                sink = _swa_head_column([sink_ref[0, h] for h in heads])
                p = jnp.exp(_swa_logits(qs32.astype(BF16), kb, dist, valid, grp) - lse)
                dp = lax.dot_general(dos32.astype(BF16), vb, _NT, preferred_element_type=F32)
                ds = p * (dp - delta)
                dsb = ds.astype(BF16)
                d_sink = jnp.exp(sink - lse) * delta
                for a, h in enumerate(heads):
                    dsink_ref[h:h + 1, :] += jnp.broadcast_to(-jnp.sum(d_sink[a * w:(a + 1) * w]), (1, LANES))
                dvt = jnp.dot(dos32.T.astype(BF16), p.astype(BF16), preferred_element_type=F32)
                dkt = jnp.dot(qs32.T.astype(BF16), dsb, preferred_element_type=F32) * SWA_SCALE
                dvt_ref[j0] += dvt[:, :w]
                dvt_ref[j0 + 1] += dvt[:, w:]
                dkt_ref[j0] += dkt[:, :w]
                dkt_ref[j0 + 1] += dkt[:, w:]
                dq_tiles += _swa_unstack(jnp.dot(dsb, kb, preferred_element_type=F32) * SWA_SCALE, grp)
            dq_ref[rs, :] = _swa_merge_heads(dq_tiles)

    return pl.pallas_call(
        body, name=name, grid=(s // rows,),
        in_specs=[pl.BlockSpec(memory_space=pltpu.SMEM),
                  pl.BlockSpec((rows, qcols), lambda g: (g, 0)),
                  pl.BlockSpec((s, LANES), lambda g: (0, 4)), pl.BlockSpec((s, LANES), lambda g: (0, 5)),
                  pl.BlockSpec((rows, qcols), lambda g: (g, 0)), pl.BlockSpec((rows, qcols), lambda g: (g, 0)),
                  pl.BlockSpec((SWA_HEADS, rows, 1), lambda g: (0, g, 0))],
        out_specs=[pl.BlockSpec((rows, qcols), lambda g: (g, 0)),
                   pl.BlockSpec((nblk, LANES, w), lambda g: (0, 0, 0)),
                   pl.BlockSpec((nblk, LANES, w), lambda g: (0, 0, 0)),
                   pl.BlockSpec((SWA_HEADS, LANES), lambda g: (0, 0))],
        out_shape=[jax.ShapeDtypeStruct((s, qcols), F32),
                   jax.ShapeDtypeStruct((nblk, LANES, w), F32), jax.ShapeDtypeStruct((nblk, LANES, w), F32),
                   jax.ShapeDtypeStruct((SWA_HEADS, LANES), F32)],
        compiler_params=_params(("arbitrary",), VMEM_LIMIT),
    )(sinks, z0b, z0b, z0b, do, o, lse)


CUM_T = 256


def _split3(x):
    hi = x.astype(BF16)
    r1 = x - hi.astype(F32)
    mid = r1.astype(BF16)
    lo = (r1 - mid.astype(F32)).astype(BF16)
    return hi, mid, lo


def _tri_dot(tri, x):
    hi, mid, lo = _split3(x)
    out = jnp.dot(tri, hi, preferred_element_type=F32)
    out = out + jnp.dot(tri, mid, preferred_element_type=F32)
    return out + jnp.dot(tri, lo, preferred_element_type=F32)


def _logf_fwd(zf, bf, *, name):
    s = zf.shape[0]
    t = CUM_T
    nb = s // t

    def body(z_ref, b_ref, c_ref, carry_ref):
        i = pl.program_id(0)

        @pl.when(i == 0)
        def _():
            carry_ref[...] = jnp.zeros_like(carry_ref)

        x = z_ref[...] + b_ref[...]
        lf = jnp.minimum(x, 0.0) - jnp.log(1.0 + jnp.exp(-jnp.abs(x)))
        row = lax.broadcasted_iota(jnp.int32, (t, t), 0)
        col = lax.broadcasted_iota(jnp.int32, (t, t), 1)
        tri = jnp.where(col <= row, 1.0, 0.0).astype(BF16)
        c = _tri_dot(tri, lf) + carry_ref[...]
        c_ref[...] = c
        carry_ref[...] = c[t - 1:t, :]

    return pl.pallas_call(
        body, name=name, grid=(nb,),
        in_specs=[pl.BlockSpec((t, LANES), lambda i: (i, 0)), pl.BlockSpec((1, LANES), lambda i: (0, 0))],
        out_specs=pl.BlockSpec((t, LANES), lambda i: (i, 0)),
        out_shape=jax.ShapeDtypeStruct((s, LANES), F32),
        scratch_shapes=[pltpu.VMEM((1, LANES), F32)],
        compiler_params=_params(("arbitrary",)),
    )(zf, bf)


def _logf_bwd(dc, zf, bf, *, name):
    s = zf.shape[0]
    t = CUM_T
    nb = s // t

    def body(dc_ref, z_ref, b_ref, dz_ref, db_ref, carry_ref):
        i = pl.program_id(0)

        @pl.when(i == 0)
        def _():
            carry_ref[...] = jnp.zeros_like(carry_ref)
            db_ref[...] = jnp.zeros_like(db_ref)

        row = lax.broadcasted_iota(jnp.int32, (t, t), 0)
        col = lax.broadcasted_iota(jnp.int32, (t, t), 1)
        tri = jnp.where(col >= row, 1.0, 0.0).astype(BF16)
        dlf = _tri_dot(tri, dc_ref[...]) + carry_ref[...]
        carry_ref[...] = dlf[0:1, :]
        x = z_ref[...] + b_ref[...]
        dz = dlf * _sigmoid(-x)
        dz_ref[...] = dz.astype(BF16)
        db_ref[...] += jnp.sum(dz, axis=0, keepdims=True)

    return pl.pallas_call(
        body, name=name, grid=(nb,),
        in_specs=[pl.BlockSpec((t, LANES), lambda i: (nb - 1 - i, 0)), pl.BlockSpec((t, LANES), lambda i: (nb - 1 - i, 0)),
                  pl.BlockSpec((1, LANES), lambda i: (0, 0))],
        out_specs=[pl.BlockSpec((t, LANES), lambda i: (nb - 1 - i, 0)), pl.BlockSpec((1, LANES), lambda i: (0, 0))],
        out_shape=[jax.ShapeDtypeStruct((s, LANES), BF16), jax.ShapeDtypeStruct((1, LANES), F32)],
        scratch_shapes=[pltpu.VMEM((1, LANES), F32)],
        compiler_params=_params(("arbitrary",)),
    )(dc, zf, bf)


def _loss_head(x2, g, target, *, name):
    s = x2.shape[0]
    tm = _tile(s, (256, 128))

    def body(x_ref, g_ref, t_ref, dx_ref, loss_ref, dg_ref):
        i = pl.program_id(0)

        @pl.when(i == 0)
        def _():
            loss_ref[...] = jnp.zeros_like(loss_ref)
            dg_ref[...] = jnp.zeros_like(dg_ref)

        xf = x_ref[...]
        r = lax.rsqrt(jnp.mean(xf * xf, axis=-1, keepdims=True) + EPS)
        xh = xf * r
        gv = g_ref[...]
        err = xh * gv - t_ref[...]
        loss_ref[...] += jnp.broadcast_to(0.5 * jnp.sum(jnp.mean(err * err, axis=-1, keepdims=True)), loss_ref.shape)
        dy = err * (1.0 / D_MODEL)
        dg_ref[...] += jnp.sum(dy * xh, axis=0, keepdims=True)
        dxh = dy * gv
        dx_ref[...] = r * (dxh - xh * jnp.mean(dxh * xh, axis=-1, keepdims=True))

    return pl.pallas_call(
        body, name=name, grid=(s // tm,),
        in_specs=[pl.BlockSpec((tm, D_MODEL), lambda i: (i, 0)), pl.BlockSpec((1, D_MODEL), lambda i: (0, 0)),
                  pl.BlockSpec((tm, D_MODEL), lambda i: (i, 0))],
        out_specs=[pl.BlockSpec((tm, D_MODEL), lambda i: (i, 0)), pl.BlockSpec((8, LANES), lambda i: (0, 0)),
                   pl.BlockSpec((1, D_MODEL), lambda i: (0, 0))],
        out_shape=[jax.ShapeDtypeStruct((s, D_MODEL), F32), jax.ShapeDtypeStruct((8, LANES), F32),
                   jax.ShapeDtypeStruct((1, D_MODEL), F32)],
        compiler_params=_params(("arbitrary",)),
    )(x2, g, target)


def _sum_pieces(p_ref):
    g = p_ref[0].astype(F32)
    for k in range(1, N_DEV):
        g = g + p_ref[k].astype(F32)
    return g


def _adam_update(g, w, m, v):
    bc1 = 1.0 - ADAM_B1 ** ADAM_STEP
    bc2 = 1.0 - ADAM_B2 ** ADAM_STEP
    nm = ADAM_B1 * m + (1.0 - ADAM_B1) * g
    nv = ADAM_B2 * v + (1.0 - ADAM_B2) * (g * g)
    m_hat = nm / bc1
    v_hat = nv / bc2
    return -ADAM_LR * (m_hat / (jnp.sqrt(v_hat) + ADAM_EPS) + ADAM_WD * w), nm, nv


def _adamw(pieces, w, m, v, *, name):
    rows, cols = w.shape
    tr = _tile(rows, (RB1, RB0, SMALL_ROWS))

    def body(p_ref, w_ref, m_ref, v_ref, g_ref, d_ref, nm_ref, nv_ref):
        g = _sum_pieces(p_ref)
        g_ref[...] = g
        d_ref[...], nm_ref[...], nv_ref[...] = _adam_update(g, w_ref[...], m_ref[...], v_ref[...])

    spec = pl.BlockSpec((tr, cols), lambda i: (i, 0))
    shape = jax.ShapeDtypeStruct((rows, cols), F32)
    return pl.pallas_call(
        body, name=name, grid=(rows // tr,),
        in_specs=[pl.BlockSpec((N_DEV, tr, cols), lambda i: (0, i, 0)), spec, spec, spec],
        out_specs=[spec, spec, spec, spec], out_shape=[shape, shape, shape, shape],
        compiler_params=_params(("parallel",)),
    )(pieces, w, m, v)


def _sum8(pieces, rows, *, name):
    cols = pieces.shape[2]
    tr = _tile(rows, (176, 96))

    def body(p_ref, g_ref):
        g_ref[...] = _sum_pieces(p_ref)

    return pl.pallas_call(
        body, name=name, grid=(rows // tr,),
        in_specs=[pl.BlockSpec((N_DEV, tr, cols), lambda i: (0, i, 0))],
        out_specs=pl.BlockSpec((tr, cols), lambda i: (i, 0)),
        out_shape=jax.ShapeDtypeStruct((rows, cols), F32),
        compiler_params=_params(("parallel",)),
    )(pieces)


def _adamw_native(g, w, m, v, *, name):
    rows, cols = w.shape
    tr = _tile(rows, (256, 128))

    def body(g_ref, w_ref, m_ref, v_ref, d_ref, nm_ref, nv_ref):
        d_ref[...], nm_ref[...], nv_ref[...] = _adam_update(g_ref[...], w_ref[...], m_ref[...], v_ref[...])

    spec = pl.BlockSpec((tr, cols), lambda i: (i, 0))
    shape = jax.ShapeDtypeStruct((rows, cols), F32)
    return pl.pallas_call(
        body, name=name, grid=(rows // tr,), in_specs=[spec, spec, spec, spec],
        out_specs=[spec, spec, spec], out_shape=[shape, shape, shape],
        compiler_params=_params(("parallel",)),
    )(g, w, m, v)


MESH = pl.DeviceIdType.MESH
ANY = pl.BlockSpec(memory_space=pl.ANY)


def _all_gather(shard, *, name):
    rows, lanes = shard.shape

    def body(x_ref, out_ref, send_sems, recv_sems, local_sem):
        x, y, c = lax.axis_index("x"), lax.axis_index("y"), lax.axis_index("c")
        me, sibling = (x, y, c), (x, y, 1 - c)
        chips = [(1 - x, y), (x, 1 - y), (1 - x, 1 - y)]

        def block(px, py, pc):
            return out_ref.at[4 * px + 2 * py + pc]

        def copy(k, blk, to, src=None):
            return pltpu.make_async_remote_copy(
                src_ref=block(*blk) if src is None else src, dst_ref=block(*blk),
                send_sem=send_sems.at[k], recv_sem=recv_sems.at[k], device_id=to, device_id_type=MESH)

        mine = pltpu.make_async_copy(x_ref, block(*me), local_sem)
        mine.start()
        first = [copy(0, me, sibling, src=x_ref)]
        first += [copy(1 + j, me, (*chip, c), src=x_ref) for j, chip in enumerate(chips)]
        for cp in first:
            cp.start()
        passed = [copy(4 + j, (*chip, c), sibling) for j, chip in enumerate(chips)]
        for j, chip in enumerate(chips):
            copy(1 + j, (*chip, c), me).wait_recv()
            passed[j].start()
        copy(0, sibling, me).wait_recv()
        for j, chip in enumerate(chips):
            copy(4 + j, (*chip, 1 - c), me).wait_recv()
        for cp in first + passed:
            cp.wait_send()
        mine.wait()

    return pl.pallas_call(
        body, name=name, out_shape=jax.ShapeDtypeStruct((N_DEV, rows, lanes), shard.dtype),
        in_specs=[ANY], out_specs=ANY,
        scratch_shapes=[pltpu.SemaphoreType.DMA((7,)), pltpu.SemaphoreType.DMA((7,)), pltpu.SemaphoreType.DMA(())],
    )(shard)


def _peer_copies(kind, src_ref, out_ref, send_sems, recv_sems, local_sem):
    x, y, c = lax.axis_index("x"), lax.axis_index("y"), lax.axis_index("c")
    me = 4 * x + 2 * y + c

    def src(idx):
        return src_ref.at[idx] if kind == "exchange" else src_ref

    mine = pltpu.make_async_copy(src(me), out_ref.at[me], local_sem)
    copies = []
    for r in range(1, N_DEV):
        px = 1 - x if r & 4 else x
        py = 1 - y if r & 2 else y
        pc = 1 - c if r & 1 else c
        copies.append(pltpu.make_async_remote_copy(
            src_ref=src(4 * px + 2 * py + pc), dst_ref=out_ref.at[me],
            send_sem=send_sems.at[r - 1], recv_sem=recv_sems.at[r - 1],
            device_id=(px, py, pc), device_id_type=MESH))
    return mine, copies


PEER_SEMS = [pltpu.SemaphoreType.DMA((7,)), pltpu.SemaphoreType.DMA((7,)), pltpu.SemaphoreType.DMA(())]


def _add_rider(rider, in_specs, args, out_specs, out_shape):
    if rider is None:
        return []
    _, arr = rider
    in_specs.append(ANY)
    args.append(arr)
    out_specs.append(ANY)
    out_shape.append(jax.ShapeDtypeStruct((N_DEV,) + arr.shape[-2:], arr.dtype))
    return list(PEER_SEMS)


def _split_rider(refs, rider, n_in, n_out):
    if rider is None:
        return refs, None
    refs = list(refs)
    rin = refs.pop(n_in)
    rout = refs.pop(n_in + n_out)
    return refs[:-3], (rin, rout, *refs[-3:])


def _ride_start(rider, ride_refs, first):
    if rider is None:
        return

    @pl.when(first)
    def _():
        mine, copies = _peer_copies(rider[0], *ride_refs)
        mine.start()
        for cp in copies:
            cp.start()


def _ride_wait(rider, ride_refs, last):
    if rider is None:
        return

    @pl.when(last)
    def _():
        mine, copies = _peer_copies(rider[0], *ride_refs)
        for cp in copies:
            cp.wait()
        mine.wait()


def _gathered_cols(blocks, kdim):
    n = blocks.shape[1] * WIDE // kdim
    return blocks.reshape(N_DEV, kdim, n).transpose(1, 0, 2).reshape(kdim, N_DEV * n)


def _scatter_cols(dw):
    kdim, n8 = dw.shape
    n = n8 // N_DEV
    return dw.reshape(kdim, N_DEV, n).transpose(1, 0, 2).reshape(N_DEV, kdim * n // WIDE, WIDE)


def _pad_rows(a, rows):
    pad = [(0, 0)] * a.ndim
    pad[-2] = (0, rows - a.shape[-2])
    return jnp.pad(a, pad)


def _layer0_in_weight_t(wt):
    cq, ckv, kpe = wt[0:256], wt[256:384], wt[384:416]
    q_s, k_s, v_s, gate = wt[416:928], wt[928:1056], wt[1056:1184], wt[1184:2208]
    z = jnp.zeros((64, wt.shape[1]), wt.dtype)
    return jnp.concatenate([gate, cq, ckv, z, kpe, z[:32], q_s, k_s, v_s], axis=0)


def _layer0_in_grad_t(dwt):
    gate, cq, ckv, kpe = dwt[0:1024], dwt[1024:1280], dwt[1280:1408], dwt[1472:1504]
    q_s, k_s, v_s = dwt[1536:2048], dwt[2048:2176], dwt[2176:2304]
    return jnp.concatenate([cq, ckv, kpe, q_s, k_s, v_s, gate], axis=0)


def _layer1_in_weight_t(wt):
    main = jnp.concatenate([wt[:3 * D_MODEL], wt[3 * D_MODEL + FOX_HEADS:]], axis=0)
    return main, _pad_rows(wt[3 * D_MODEL:3 * D_MODEL + FOX_HEADS], LANES)


def _layer1_in_grad_t(d_main, d_wft):
    return jnp.concatenate([d_main[:3 * D_MODEL], d_wft[:FOX_HEADS], d_main[3 * D_MODEL:]], axis=0)


def _q_up_weight(w):
    return jnp.pad(w.reshape(MLA_Q_RANK, MLA_HEADS, 96), ((0, 0), (0, 0), (0, 32))).reshape(MLA_Q_RANK, MLA_HEADS * LANES)


def _q_up_grad(dwp):
    return dwp.reshape(MLA_Q_RANK, MLA_HEADS, LANES)[:, :, :96].reshape(MLA_Q_RANK, MLA_HEADS * 96)


def _kv_up_weight(w):
    w4 = w.reshape(MLA_KV_RANK, MLA_HEADS, 2, 64)
    kp = jnp.pad(w4[:, :, 0, :], ((0, 0), (0, 0), (0, 64))).reshape(MLA_KV_RANK, MLA_HEADS * LANES)
    vp = w4[:, :, 1, :].reshape(MLA_KV_RANK, MLA_HEADS * 64)
    return jnp.concatenate([kp, vp], axis=1)


def _kv_up_grad(dwp):
    dk = dwp[:, :MLA_HEADS * LANES].reshape(MLA_KV_RANK, MLA_HEADS, LANES)[:, :, :64]
    dv = dwp[:, MLA_HEADS * LANES:].reshape(MLA_KV_RANK, MLA_HEADS, 64)
    return jnp.stack([dk, dv], axis=2).reshape(MLA_KV_RANK, MLA_HEADS * LANES)


def _pad_lanes(a):
    return jnp.pad(a, ((0, 0), (0, LANES - a.shape[1])))


def _small_pack(g_in, g_final, g_q_a, g_kv_a, sinks, b_f, loss):
    rows = [g_in.reshape(8, LANES), g_final.reshape(8, LANES), g_q_a.reshape(2, LANES), g_kv_a.reshape(1, LANES),
            _pad_lanes(sinks.reshape(1, -1)), _pad_lanes(b_f.reshape(1, -1)), _pad_lanes(loss.reshape(1, 1)),
            jnp.zeros((2, LANES), F32)]
    return jnp.concatenate(rows, axis=0)


def _small_unpack(a):
    return (a[0:8].reshape(1, D_MODEL), a[8:16].reshape(D_MODEL), a[16:18].reshape(1, MLA_Q_RANK),
            a[18:19].reshape(1, MLA_KV_RANK), a[19:20, :SWA_HEADS], a[20:21, :FOX_HEADS], a[21, 0])


def _local_step(x, positions, target, e_g_in, w0t, e_g_q_a, wq, e_g_kv_a, wkv, e_sinks,
                late, o_b_f, g_final, scatter1=None, scatter0=None):
    s = x.shape[0]
    att_t = min(ATT_T, s)
    nb = s // att_t
    mla_scale = (MLA_NOPE + MLA_ROPE) ** -0.5
    fox_scale = FOX_DIM ** -0.5
    n0a = Z0A_UNITS * LANES

    inv_freq = 1.0 / (ROPE_THETA ** (jnp.arange(0, MLA_ROPE, 2, dtype=F32) / MLA_ROPE))
    ang = positions.astype(F32)[:, None] * inv_freq
    cos, sin = jnp.cos(ang), jnp.sin(ang)
    ones, zeros = jnp.ones((s, 64), F32), jnp.zeros((s, 64), F32)
    cos_t = jnp.concatenate([ones, cos, cos, ones[:, :32]], axis=1)
    sin_t = jnp.concatenate([zeros, -sin, sin, zeros[:, :32]], axis=1)

    h0 = _rmsnorm_fwd(x, e_g_in, width=D_MODEL, col_blk=0, name="l0_norm")
    z0a = _matmul(h0, w0t, tb=True, b_rows=(0, n0a), name="l0_in_a")
    z0b = _matmul(h0, w0t, tb=True, b_rows=(n0a, Z0B_UNITS * LANES), name="l0_in_b", out_dtype=BF16)
    cqn = _rmsnorm_fwd(z0a, e_g_q_a, width=MLA_Q_RANK, col_blk=4, name="l0_q_norm")
    ckvn = _rmsnorm_fwd(z0a, e_g_kv_a, width=MLA_KV_RANK, col_blk=10, name="l0_kv_norm")
    qp = _matmul(cqn, wq, name="l0_q_up")
    kvp = _matmul(ckvn, wkv, name="l0_kv_up", out_dtype=BF16)
    qm, km = _rope_fwd(qp, kvp, z0a, cos_t, sin_t, name="l0_rope")
    gathers = len(late) == 2
    res = _flash_fwd(qm, km, kvp, None, n_pairs=MLA_HEADS // 2, hw=LANES, q_off=0, k_off=0, v_off=MLA_HEADS,
                     scale=mla_scale, name="l0_mla_fwd", rider=("gather", late[0]) if gathers else None)
    o_mla, lse_mla = res[0], res[1]
    wo0, o_g_in, w1t, wft, wo1 = late[1](res[2]) if gathers else late
    o_swa, lse_swa = _swa_fwd(z0b, e_sinks, name="l0_swa_fwd")
    og0 = _gate_fwd([o_mla, o_swa], z0a, name="l0_gate")
    x1 = _matmul(og0, wo0, add=x, name="l0_out")

    h1 = _rmsnorm_fwd(x1, o_g_in, width=D_MODEL, col_blk=0, name="l1_norm")
    z1 = _matmul(h1, w1t, tb=True, b_rows=(0, 3 * D_MODEL), name="l1_in_qkv", out_dtype=BF16)
    gate1 = _matmul(h1, w1t, tb=True, b_rows=(3 * D_MODEL, D_MODEL), name="l1_in_gate")
    zf = _matmul(h1, wft, tb=True, name="l1_in_f")
    bf = _pad_lanes(o_b_f)
    log_cum = _logf_fwd(zf, bf, name="l1_logf")
    bias2 = (-LOG2E * log_cum[:, :FOX_HEADS]).T
    bias = bias2.reshape(FOX_HEADS // 2, 2, nb, 1, att_t)
    t_fwd = _fwd_tile(s)
    o_fox, lse_fox = _flash_fwd(z1, z1, z1, bias2.reshape(FOX_HEADS // 2, 2, s // t_fwd, 1, t_fwd),
                                n_pairs=FOX_HEADS // 2, hw=64, q_off=0, k_off=8, v_off=16, scale=fox_scale,
                                name="l1_fox_fwd")
    og1 = _gate_fwd([o_fox], gate1, name="l1_gate")
    x2 = _matmul(og1, wo1, add=x1, name="l1_out")

    dx2, loss_part, d_g_final = _loss_head(x2, g_final.reshape(1, D_MODEL), target, name="loss_head")

    d_wo1 = _matmul(og1, dx2, ta=True, name="l1_out_dw")
    d_og1 = _matmul(dx2, wo1, tb=True, name="l1_out_dx")
    do_fox, d_gate1 = _gate_bwd(d_og1, [o_fox], gate1, name="l1_gate_bwd")
    dq1, dk1, dv1, dbias, drow = _flash_bwd(z1, z1, z1, do_fox, o_fox, lse_fox, bias, n_pairs=FOX_HEADS // 2, hw=64,
                                            q_off=0, k_off=8, v_off=16, scale=fox_scale, qk_dtype=BF16,
                                            name="l1_fox_bwd")
    d_log_cum = (drow.reshape(FOX_HEADS, s) - dbias.reshape(FOX_HEADS, s)).T
    d_log_cum = jnp.pad(d_log_cum, ((0, 0), (0, LANES - FOX_HEADS)))
    d_zf, d_bf = _logf_bwd(d_log_cum, zf, bf, name="l1_logf_bwd")
    dz1 = jnp.concatenate([dq1, dk1, dv1, d_gate1], axis=1)
    d_w1t = _matmul(dz1, h1, ta=True, name="l1_in_dw")
    d_wft = _matmul(d_zf, h1, ta=True, name="l1_in_f_dw")
    dh1 = _matmul(dz1, w1t, name="l1_in_dx")
    dh1 = _matmul(d_zf, wft, add=dh1, name="l1_in_f_dx")
    dx1, d_o_g_in = _rmsnorm_bwd(x1, o_g_in, dh1, width=D_MODEL, col_blk=0, add=dx2, name="l1_norm_bwd")

    d_wo0 = _matmul(og0, dx1, ta=True, name="l0_out_dw")
    d_og0 = _matmul(dx1, wo0, tb=True, name="l0_out_dx")
    do_mla, do_swa, d_gate0 = _gate_bwd(d_og0, [o_mla, o_swa], z0a, name="l0_gate_bwd")
    dq_s, dkt_s, dvt_s, d_sinks = _swa_bwd(z0b, e_sinks, do_swa, o_swa, lse_swa, name="l0_swa_bwd")
    dk_s = dkt_s.transpose(0, 2, 1).reshape(s, LANES)
    dv_s = dvt_s.transpose(0, 2, 1).reshape(s, LANES)
    rider = None
    if scatter1 is not None:
        rider = ("exchange", scatter1(dict(w1t=d_w1t, wft=d_wft, wo1=d_wo1, o_g_in=d_o_g_in, wo0=d_wo0)))
    res = _flash_bwd(qm, km, kvp, do_mla, o_mla, lse_mla, None, n_pairs=MLA_HEADS // 2, hw=LANES, q_off=0, k_off=0,
                     v_off=MLA_HEADS, scale=mla_scale, qk_dtype=F32, name="l0_mla_bwd", rider=rider)
    dqm, dkm, dvm = res[0], res[1], res[2]
    recv1 = res[3] if rider is not None else None
    d_qp, d_kvp, d_kpe = _rope_bwd(dqm, dkm, dvm, cos_t, sin_t, name="l0_rope_bwd")
    d_wq = _matmul(cqn, d_qp, ta=True, name="l0_q_up_dw")
    d_cqn = _matmul(d_qp, wq, tb=True, name="l0_q_up_dx")
    d_wkv = _matmul(ckvn, d_kvp, ta=True, name="l0_kv_up_dw")
    d_ckvn = _matmul(d_kvp, wkv, tb=True, name="l0_kv_up_dx")
    d_cq, d_g_q_a = _rmsnorm_bwd(z0a, e_g_q_a, d_cqn, width=MLA_Q_RANK, col_blk=4, out_dtype=BF16, name="l0_q_norm_bwd")
    d_ckv, d_g_kv_a = _rmsnorm_bwd(z0a, e_g_kv_a, d_ckvn, width=MLA_KV_RANK, col_blk=10, out_dtype=BF16,
                                   name="l0_kv_norm_bwd")
    dz0 = jnp.concatenate([d_gate0, d_cq, d_ckv, d_kpe, dq_s.astype(BF16), dk_s.astype(BF16), dv_s.astype(BF16)], axis=1)
    d_w0t = _matmul(dz0, h0, ta=True, name="l0_in_dw")
    recv0 = None
    if scatter0 is None:
        dh0 = _matmul(dz0, w0t, name="l0_in_dx")
    else:
        dh0, recv0 = _matmul(dz0, w0t, name="l0_in_dx", rider=("exchange", scatter0(dict(w0t=d_w0t, wq=d_wq, wkv=d_wkv))))
    grad_x, d_e_g_in = _rmsnorm_bwd(x, e_g_in, dh0, width=D_MODEL, col_blk=0, add=dx1, name="l0_norm_bwd")

    return dict(recv0=recv0, recv1=recv1, loss=loss_part[0, 0], grad_x=grad_x, e_g_in=d_e_g_in, w0t=d_w0t, e_g_q_a=d_g_q_a, wq=d_wq,
                e_g_kv_a=d_g_kv_a, wkv=d_wkv, e_sinks=d_sinks[:, 0].reshape(1, SWA_HEADS), wo0=d_wo0,
                o_g_in=d_o_g_in, w1t=d_w1t, wft=d_wft, o_b_f=d_bf[:, :FOX_HEADS], wo1=d_wo1, g_final=d_g_final.reshape(D_MODEL))


def _wide(a, rows):
    flat = a.reshape(-1)
    return jnp.pad(flat, (0, rows * WIDE - flat.shape[0])).reshape(rows, WIDE)


def _rows_b0(w_q, w_kv):
    return jnp.concatenate([_wide(w_q, 32), _wide(w_kv, 16)], axis=0)


def _unflat_b0(f):
    return f[0:24].reshape(1, MLA_Q_RANK, 96), f[32:48].reshape(1, MLA_KV_RANK, 128)


def _rows_b1(o_w_out, e_w_out, g_in):
    return jnp.concatenate([o_w_out, e_w_out, _wide(g_in, 16)], axis=0)


def _unflat_b1(f):
    return f[0:128][None], f[128:256][None], f[256:257, :LANES]


def kernel(x, positions, e_g_in, e_w_in, e_g_q_a, e_w_q_up, e_g_kv_a, e_w_kv_up, e_sinks, e_w_out, o_g_in, o_w_in, o_b_f, o_w_out, g_final, loss_target, m_e_g_in, m_e_w_in, m_e_g_q_a, m_e_w_q_up, m_e_g_kv_a, m_e_w_kv_up, m_e_sinks, m_e_w_out, m_o_g_in, m_o_w_in, m_o_b_f, m_o_w_out, m_g_final, v_e_g_in, v_e_w_in, v_e_g_q_a, v_e_w_q_up, v_e_g_kv_a, v_e_w_kv_up, v_e_sinks, v_e_w_out, v_o_g_in, v_o_w_in, v_o_b_f, v_o_w_out, v_g_final):
    def bf(a):
        return a.astype(BF16)

    shard0 = jnp.concatenate([_pad_rows(bf(e_w_in[0]).T, RA0), _rows_b0(bf(e_w_q_up[0]), bf(e_w_kv_up[0]))], axis=0)
    gath0 = _all_gather(shard0, name="weights0_all_gather")
    w0t = _layer0_in_weight_t(gath0[:, :N_E_IN].reshape(N_DEV * N_E_IN, WIDE))
    wq = _q_up_weight(_gathered_cols(gath0[:, RA0:RA0 + 24], MLA_Q_RANK))
    wkv = _kv_up_weight(_gathered_cols(gath0[:, RA0 + 32:RA0 + 48], MLA_KV_RANK))

    g_bits = lax.bitcast_convert_type(o_g_in.reshape(LANES), BF16)
    shard1 = jnp.concatenate([_pad_rows(bf(o_w_in[0]).T, RA1), _rows_b1(bf(o_w_out[0]), bf(e_w_out[0]), g_bits)], axis=0)

    def unpack1(gath1):
        w1t, wft = _layer1_in_weight_t(gath1[:, :N_O_IN].reshape(N_DEV * N_O_IN, WIDE))
        wo1 = gath1[:, RA1:RA1 + 128].reshape(D_MODEL, D_MODEL)
        wo0 = gath1[:, RA1 + 128:RA1 + 256].reshape(D_MODEL, D_MODEL)
        bits = gath1[:, RA1 + 256, :2 * LANES].reshape(N_DEV, LANES, 2)
        return wo0, lax.bitcast_convert_type(bits, F32).reshape(1, D_MODEL), w1t, wft, wo1

    def scatter1(g):
        d_in_t = _layer1_in_grad_t(g["w1t"], g["wft"]).reshape(N_DEV, N_O_IN, WIDE)
        d_o_g = jnp.pad(g["o_g_in"].reshape(N_DEV, 1, LANES), ((0, 0), (0, 15), (0, WIDE - LANES)))
        return jnp.concatenate([_pad_rows(d_in_t, RA1), g["wo1"].reshape(N_DEV, 128, WIDE),
                                g["wo0"].reshape(N_DEV, 128, WIDE), d_o_g], axis=1).astype(BF16)

    def scatter0(g):
        return jnp.concatenate([
            _pad_rows(_layer0_in_grad_t(g["w0t"]).reshape(N_DEV, N_E_IN, WIDE), RA0),
            _pad_rows(_scatter_cols(_q_up_grad(g["wq"])), 32), _scatter_cols(_kv_up_grad(g["wkv"]))], axis=1).astype(BF16)

    gr = _local_step(x[0], positions[0], loss_target[0], e_g_in, w0t, e_g_q_a, wq, e_g_kv_a, wkv, e_sinks,
                     (shard1, unpack1), o_b_f, g_final, scatter1=scatter1, scatter0=scatter0)
    recv0 = gr["recv0"]

    def in_projection(recv, ra, n, w, m, v, name):
        g = _sum8(recv, ra, name=name + "_grad_sum")[:n].T
        d, nm, nv = _adamw_native(g, w[0], m[0], v[0], name=name + "_adamw")
        return g[None], d[None], nm[None], nv[None]

    e_in = in_projection(recv0, RA0, N_E_IN, e_w_in, m_e_w_in, v_e_w_in, "e_w_in")
    o_in = in_projection(gr["recv1"], RA1, N_O_IN, o_w_in, m_o_w_in, v_o_w_in, "o_w_in")
    b0 = _adamw(recv0[:, RA0:], _rows_b0(e_w_q_up[0], e_w_kv_up[0]), _rows_b0(m_e_w_q_up[0], m_e_w_kv_up[0]),
                _rows_b0(v_e_w_q_up[0], v_e_w_kv_up[0]), name="adamw_early")
    b1 = _adamw(gr["recv1"][:, RA1:], _rows_b1(o_w_out[0], e_w_out[0], o_g_in),
                _rows_b1(m_o_w_out[0], m_e_w_out[0], m_o_g_in), _rows_b1(v_o_w_out[0], v_e_w_out[0], v_o_g_in),
                name="adamw_late")

    def sharded(k):
        q_up, kv_up = _unflat_b0(b0[k])
        o_out, e_out, o_g = _unflat_b1(b1[k])
        return e_in[k], q_up, kv_up, e_out, o_in[k], o_out, o_g

    g_sh, d_sh, m_sh, v_sh = [sharded(k) for k in range(4)]

    small = _small_pack(gr["e_g_in"], gr["g_final"], gr["e_g_q_a"], gr["e_g_kv_a"], gr["e_sinks"], gr["o_b_f"], gr["loss"])
    small_all = _all_gather(small, name="small_all_gather")
    zero = jnp.zeros((), F32)
    w_small = _small_pack(e_g_in, g_final, e_g_q_a, e_g_kv_a, e_sinks, o_b_f, zero)
    m_small = _small_pack(m_e_g_in, m_g_final, m_e_g_q_a, m_e_g_kv_a, m_e_sinks, m_o_b_f, zero)
    v_small = _small_pack(v_e_g_in, v_g_final, v_e_g_q_a, v_e_g_kv_a, v_e_sinks, v_o_b_f, zero)
    smalls = _adamw(small_all, w_small, m_small, v_small, name="adamw_replicated")
    g_sm, d_sm, m_sm, v_sm = [_small_unpack(a) for a in smalls]
    loss = g_sm[6]

    def leaves(sh, sm):
        return (sm[0], sh[0], sm[2], sh[1], sm[3], sh[2], sm[4], sh[3], sh[6], sh[4], sm[5], sh[5], sm[1])

    return (loss, gr["grad_x"][None], *leaves(g_sh, g_sm), *leaves(d_sh, d_sm), *leaves(m_sh, m_sm), *leaves(v_sh, v_sm))
```

```python
import functools

import jax
import jax.numpy as jnp
from jax import lax
from jax.experimental import pallas as pl
from jax.experimental.pallas import tpu as pltpu

F32 = jnp.float32
BF16 = jnp.bfloat16
NEG_INF = float("-inf")

N_DEV = 8
LANES = 128
D_MODEL = 1024
EPS = 1e-6
ROPE_THETA = 10000.0
MLA_HEADS = 8
MLA_Q_RANK = 256
MLA_KV_RANK = 128
MLA_NOPE = 64
MLA_ROPE = 32
MLA_V = 64
SWA_HEADS = 8
SWA_KV_HEADS = 2
SWA_DIM = 64
WINDOW = 128
FOX_HEADS = 16
FOX_DIM = 64

ADAM_LR = 0.001
ADAM_B1 = 0.9
ADAM_B2 = 0.999
ADAM_EPS = 1e-08
ADAM_WD = 0.01
ADAM_STEP = 10

ATT_T = 512
ATT_TK_BWD = 512
ATT_T_FWD = 1024
VMEM_LIMIT = 56 * 1024 * 1024
MATMUL_B_BLOCK_BYTES = 8 * 1024 * 1024

Z0A_UNITS = 12
Z0B_UNITS = 6

WIDE = 1024
N_E_IN = 276
N_O_IN = 514
RA0 = 288
RB0 = 32 + 16
RA1 = 528
RB1 = 128 + 128 + 16
SMALL_ROWS = 24


def _tile(n, cands):
    for c in cands:
        if n % c == 0:
            return c
    raise ValueError(f"no tile for {n}")


ROW_TILES = (512, 256, 128)


def _params(sem, vmem=VMEM_LIMIT):
    return pltpu.CompilerParams(dimension_semantics=sem, vmem_limit_bytes=vmem)


def _matmul(a, b, *, name, ta=False, tb=False, add=None, out_dtype=F32, b_rows=None, rider=None):
    if ta:
        kdim, m = a.shape
    else:
        m, kdim = a.shape
    if tb:
        n, kb = b.shape
    else:
        kb, n = b.shape
    assert kdim == kb, (a.shape, b.shape)
    b_start = 0
    if b_rows is not None:
        assert tb
        b_start, n = b_rows
    tm = _tile(m, (512, 256, 128))
    tn = _tile(n, [c for c in (1024, 768, 512, 384, 256, 128)
                   if c * kdim * b.dtype.itemsize <= MATMUL_B_BLOCK_BYTES and b_start % c == 0])
    assert b_start % tn == 0, (b_start, tn)
    b_off = b_start // tn
    dims = (((0 if ta else 1,), (1 if tb else 0,)), ((), ()))

    grid = (m // tm, n // tn)

    def body(*refs):
        refs, ride_refs = _split_rider(refs, rider, n_in=2 if add is None else 3, n_out=1)
        if add is None:
            a_ref, b_ref, o_ref = refs
            add_ref = None
        else:
            a_ref, b_ref, add_ref, o_ref = refs
        i, j = pl.program_id(0), pl.program_id(1)
        _ride_start(rider, ride_refs, (i == 0) & (j == 0))
        r = lax.dot_general(a_ref[...].astype(BF16), b_ref[...].astype(BF16), dims, preferred_element_type=F32)
        if add_ref is not None:
            r = r + add_ref[...]
        o_ref[...] = r.astype(out_dtype)
        _ride_wait(rider, ride_refs, (i == grid[0] - 1) & (j == grid[1] - 1))

    a_spec = pl.BlockSpec((kdim, tm), lambda i, j: (0, i)) if ta else pl.BlockSpec((tm, kdim), lambda i, j: (i, 0))
    b_spec = pl.BlockSpec((tn, kdim), lambda i, j: (j + b_off, 0)) if tb else pl.BlockSpec((kdim, tn), lambda i, j: (0, j))
    in_specs = [a_spec, b_spec]
    args = [a, b]
    if add is not None:
        in_specs.append(pl.BlockSpec((tm, tn), lambda i, j: (i, j)))
        args.append(add)
    out_specs = [pl.BlockSpec((tm, tn), lambda i, j: (i, j))]
    out_shape = [jax.ShapeDtypeStruct((m, n), out_dtype)]
    scratch = _add_rider(rider, in_specs, args, out_specs, out_shape)
    res = pl.pallas_call(
        body, name=name, grid=grid, in_specs=in_specs, out_specs=out_specs, out_shape=out_shape, scratch_shapes=scratch,
        compiler_params=_params(("parallel", "parallel") if rider is None else ("arbitrary", "arbitrary"), VMEM_LIMIT),
    )(*args)
    return res[0] if rider is None else res


def _rmsnorm_fwd(x, g, *, width, col_blk, name):
    s = x.shape[0]
    tm = _tile(s, ROW_TILES)

    def body(x_ref, g_ref, y_ref):
        xf = x_ref[...].astype(F32)
        r = lax.rsqrt(jnp.mean(xf * xf, axis=-1, keepdims=True) + EPS)
        y_ref[...] = ((xf * r) * g_ref[...]).astype(BF16)

    return pl.pallas_call(
        body, name=name, grid=(s // tm,),
        in_specs=[pl.BlockSpec((tm, width), lambda i: (i, col_blk)), pl.BlockSpec((1, width), lambda i: (0, 0))],
        out_specs=pl.BlockSpec((tm, width), lambda i: (i, 0)),
        out_shape=jax.ShapeDtypeStruct((s, width), BF16),
        compiler_params=_params(("parallel",)),
    )(x, g)


def _rmsnorm_bwd(x, g, dy, *, width, col_blk, name, add=None, out_dtype=F32):
    s = x.shape[0]
    tm = _tile(s, ROW_TILES)

    def body(*refs):
        if add is None:
            x_ref, g_ref, dy_ref, dx_ref, dg_ref = refs
            add_ref = None
        else:
            x_ref, g_ref, dy_ref, add_ref, dx_ref, dg_ref = refs
        i = pl.program_id(0)
        xf = x_ref[...].astype(F32)
        r = lax.rsqrt(jnp.mean(xf * xf, axis=-1, keepdims=True) + EPS)
        xh = xf * r
        dyf = dy_ref[...].astype(F32)

        @pl.when(i == 0)
        def _():
            dg_ref[...] = jnp.zeros_like(dg_ref)

        dg_ref[...] += jnp.sum(dyf * xh, axis=0, keepdims=True)
        dxh = dyf * g_ref[...]
        dx = r * (dxh - xh * jnp.mean(dxh * xh, axis=-1, keepdims=True))
        if add_ref is not None:
            dx = dx + add_ref[...]
        dx_ref[...] = dx.astype(out_dtype)

    in_specs = [pl.BlockSpec((tm, width), lambda i: (i, col_blk)), pl.BlockSpec((1, width), lambda i: (0, 0)),
                pl.BlockSpec((tm, width), lambda i: (i, 0))]
    args = [x, g, dy]
    if add is not None:
        in_specs.append(pl.BlockSpec((tm, width), lambda i: (i, 0)))
        args.append(add)
    return pl.pallas_call(
        body, name=name, grid=(s // tm,),
        in_specs=in_specs,
        out_specs=[pl.BlockSpec((tm, width), lambda i: (i, 0)), pl.BlockSpec((1, width), lambda i: (0, 0))],
        out_shape=[jax.ShapeDtypeStruct((s, width), out_dtype), jax.ShapeDtypeStruct((1, width), F32)],
        compiler_params=_params(("arbitrary",)),
    )(*args)


def _sigmoid(x):
    return 1.0 / (1.0 + jnp.exp(-x))


def _gate_fwd(o_parts, gate, *, name):
    s = gate.shape[0]
    tm = _tile(s, ROW_TILES)
    n_o = len(o_parts)

    def body(*refs):
        o_refs, g_ref, y_ref = refs[:n_o], refs[n_o], refs[n_o + 1]
        o = o_refs[0][...] if n_o == 1 else jnp.concatenate([r[...] for r in o_refs], axis=1)
        gt = g_ref[...]
        y_ref[...] = (o * (gt * _sigmoid(gt))).astype(BF16)

    in_specs = [pl.BlockSpec((tm, o.shape[1]), lambda i: (i, 0)) for o in o_parts]
    in_specs.append(pl.BlockSpec((tm, D_MODEL), lambda i: (i, 0)))
    return pl.pallas_call(
        body, name=name, grid=(s // tm,), in_specs=in_specs,
        out_specs=pl.BlockSpec((tm, D_MODEL), lambda i: (i, 0)),
        out_shape=jax.ShapeDtypeStruct((s, D_MODEL), BF16),
        compiler_params=_params(("parallel",)),
    )(*o_parts, gate)


def _gate_bwd(d_og, o_parts, gate, *, name):
    s = gate.shape[0]
    tm = _tile(s, ROW_TILES)
    n_o = len(o_parts)
    widths = [o.shape[1] for o in o_parts]

    def body(*refs):
        d_ref, o_refs, g_ref = refs[0], refs[1:1 + n_o], refs[1 + n_o]
        do_refs, dg_ref = refs[2 + n_o:2 + 2 * n_o], refs[2 + 2 * n_o]
        d = d_ref[...]
        gt = g_ref[...]
        sg = _sigmoid(gt)
        silu = gt * sg
        dsilu = sg * (1.0 + gt * (1.0 - sg))
        o = o_refs[0][...] if n_o == 1 else jnp.concatenate([r[...] for r in o_refs], axis=1)
        dg_ref[...] = (d * o * dsilu).astype(BF16)
        do = d * silu
        off = 0
        for r, w in zip(do_refs, widths):
            r[...] = do[:, off:off + w]
            off += w

    in_specs = [pl.BlockSpec((tm, D_MODEL), lambda i: (i, 0))]
    in_specs += [pl.BlockSpec((tm, w), lambda i: (i, 0)) for w in widths]
    in_specs.append(pl.BlockSpec((tm, D_MODEL), lambda i: (i, 0)))
    out_specs = [pl.BlockSpec((tm, w), lambda i: (i, 0)) for w in widths]
    out_specs.append(pl.BlockSpec((tm, D_MODEL), lambda i: (i, 0)))
    out_shape = [jax.ShapeDtypeStruct((s, w), F32) for w in widths]
    out_shape.append(jax.ShapeDtypeStruct((s, D_MODEL), BF16))
    return pl.pallas_call(
        body, name=name, grid=(s // tm,), in_specs=in_specs, out_specs=out_specs, out_shape=out_shape,
        compiler_params=_params(("parallel",)),
    )(d_og, *o_parts, gate)


def _rot_half(x):
    lane = lax.broadcasted_iota(jnp.int32, x.shape, 1)
    return jnp.where(lane < 80, pltpu.roll(x, LANES - 16, axis=1), pltpu.roll(x, 16, axis=1))


def _rot_half_t(g):
    lane = lax.broadcasted_iota(jnp.int32, g.shape, 1)
    lo = (lane >= MLA_NOPE) & (lane < MLA_NOPE + MLA_ROPE // 2)
    hi = (lane >= MLA_NOPE + MLA_ROPE // 2) & (lane < MLA_NOPE + MLA_ROPE)
    return jnp.where(lo, pltpu.roll(g, LANES - 16, axis=1), jnp.where(hi, pltpu.roll(g, 16, axis=1), 0.0))


def _rope_fwd(qp, kvp, z0a, cos_t, sin_t, *, name):
    s = qp.shape[0]
    tm = _tile(s, ROW_TILES)
    hw = MLA_HEADS * LANES

    def body(q_ref, k_ref, kpe_ref, c_ref, s_ref, qm_ref, km_ref):
        c = c_ref[...]
        sn = s_ref[...]
        kpe = kpe_ref[...]
        kpe_r = (kpe * c + _rot_half(kpe) * sn).astype(BF16)
        lane = lax.broadcasted_iota(jnp.int32, kpe.shape, 1)
        for h in range(MLA_HEADS):
            sl = slice(h * LANES, (h + 1) * LANES)
            qh = q_ref[:, sl]
            qm_ref[:, sl] = (qh * c + _rot_half(qh) * sn).astype(BF16)
            km_ref[:, sl] = jnp.where(lane < MLA_NOPE, k_ref[:, sl], kpe_r)

    return pl.pallas_call(
        body, name=name, grid=(s // tm,),
        in_specs=[pl.BlockSpec((tm, hw), lambda i: (i, 0)), pl.BlockSpec((tm, hw), lambda i: (i, 0)),
                  pl.BlockSpec((tm, LANES), lambda i: (i, 11)),
                  pl.BlockSpec((tm, LANES), lambda i: (i, 0)), pl.BlockSpec((tm, LANES), lambda i: (i, 0))],
        out_specs=[pl.BlockSpec((tm, hw), lambda i: (i, 0)), pl.BlockSpec((tm, hw), lambda i: (i, 0))],
        out_shape=[jax.ShapeDtypeStruct((s, hw), BF16), jax.ShapeDtypeStruct((s, hw), BF16)],
        compiler_params=_params(("parallel",)),
    )(qp, kvp, z0a, cos_t, sin_t)


def _rope_bwd(dqm, dkm, dvm, cos_t, sin_t, *, name):
    s = dqm.shape[0]
    tm = _tile(s, ROW_TILES)
    hw = MLA_HEADS * LANES
    vw = MLA_HEADS * MLA_V

    def body(dq_ref, dk_ref, dv_ref, c_ref, s_ref, dqp_ref, dkv_ref, dkpe_ref):
        c = c_ref[...]
        sn = s_ref[...]
        ksum = jnp.zeros((tm, LANES), F32)
        for h in range(MLA_HEADS):
            sl = slice(h * LANES, (h + 1) * LANES)
            dq = dq_ref[:, sl]
            dqp_ref[:, sl] = (dq * c + _rot_half_t(dq * sn)).astype(BF16)
            dk = dk_ref[:, sl]
            dkv_ref[:, sl] = dk.astype(BF16)
            ksum = ksum + dk
        dkv_ref[:, hw:] = dv_ref[...]
        lane = lax.broadcasted_iota(jnp.int32, ksum.shape, 1)
        dkpe = ksum * c + _rot_half_t(ksum * sn)
        dkpe_ref[...] = jnp.where((lane >= MLA_NOPE) & (lane < MLA_NOPE + MLA_ROPE), dkpe, 0.0).astype(BF16)

    return pl.pallas_call(
        body, name=name, grid=(s // tm,),
        in_specs=[pl.BlockSpec((tm, hw), lambda i: (i, 0)), pl.BlockSpec((tm, hw), lambda i: (i, 0)),
                  pl.BlockSpec((tm, vw), lambda i: (i, 0)),
                  pl.BlockSpec((tm, LANES), lambda i: (i, 0)), pl.BlockSpec((tm, LANES), lambda i: (i, 0))],
        out_specs=[pl.BlockSpec((tm, hw), lambda i: (i, 0)), pl.BlockSpec((tm, hw + vw), lambda i: (i, 0)),
                   pl.BlockSpec((tm, LANES), lambda i: (i, 0))],
        out_shape=[jax.ShapeDtypeStruct((s, hw), BF16), jax.ShapeDtypeStruct((s, hw + vw), BF16),
                   jax.ShapeDtypeStruct((s, LANES), BF16)],
        compiler_params=_params(("parallel",)),
    )(dqm, dkm, dvm, cos_t, sin_t)


def _head_mask(shape, a):
    lane = lax.broadcasted_iota(jnp.int32, shape, 1)
    return (lane >= 64 * a) & (lane < 64 * (a + 1))


_NT = (((1,), (1,)), ((), ()))
LOG2E = 1.4426950408889634


def _stack_heads(tile, hw):
    lane = lax.broadcasted_iota(jnp.int32, tile.shape, 1)
    z = jnp.zeros_like(tile)
    return jnp.concatenate([jnp.where(lane < hw, tile, z), jnp.where(lane >= hw, tile, z)], axis=0)


def _stacked_rows(r0, r1, t):
    n = r0.shape[-1]
    return jnp.concatenate([jnp.broadcast_to(r0, (t, n)), jnp.broadcast_to(r1, (t, n))], axis=0)


def _resident(block, index_map):
    return pl.BlockSpec(block, index_map, pipeline_mode=pl.Buffered(1))


def _fwd_tile(s):
    return ATT_T_FWD if s % ATT_T_FWD == 0 else min(ATT_T, s)


def _flash_fwd(q, k, v, bias, *, n_pairs, hw, q_off, k_off, v_off, scale, name, rider=None):
    s = q.shape[0]
    t = _fwd_tile(s)
    nb = s // t
    qw = 2 * hw
    has_bias = bias is not None
    c1 = scale * LOG2E

    def body(*refs):
        refs, ride_refs = _split_rider(refs, rider, n_in=4 if has_bias else 3, n_out=2)
        if has_bias:
            q_ref, k_ref, v_ref, b_ref, o_ref, lse_ref, vt_ref, bcol_ref = refs
        else:
            q_ref, k_ref, v_ref, o_ref, lse_ref, vt_ref = refs
            b_ref = bcol_ref = None
        _ride_start(rider, ride_refs, pl.program_id(0) == 0)
        row = lax.broadcasted_iota(jnp.int32, (t, t), 0)
        col = lax.broadcasted_iota(jnp.int32, (t, t), 1)
        cmask_t = jnp.concatenate([row <= col, row <= col], axis=1)
        lane_lt64 = lax.broadcasted_iota(jnp.int32, (t, LANES), 1) < 64

        def as_column(r):
            return jnp.broadcast_to(r, (8, r.shape[1])).T[:, 0:1]

        def v_block(j, _):
            c0 = pl.multiple_of(j * t, t)
            vt_ref[j] = v_ref[pl.ds(c0, t), :].astype(F32).T.astype(BF16)
            if has_bias:
                for a in range(2):
                    bcol_ref[a, pl.ds(c0, t), :] = as_column(b_ref[0, a, j])
            return 0

        lax.fori_loop(0, nb, v_block, 0)

        def stacked_queries(i):
            return _stack_heads(q_ref[pl.ds(pl.multiple_of(i * t, t), t), :], hw).astype(F32).T.astype(BF16)

        def kv_step(j, carry, qs_t, masked):
            m, l, acc = carry
            rows = pl.ds(pl.multiple_of(j * t, t), t)
            sc = jnp.dot(k_ref[rows, :], qs_t, preferred_element_type=F32) * c1
            if has_bias:
                sc = sc + jnp.concatenate([jnp.broadcast_to(bcol_ref[0, rows, :], (t, t)),
                                           jnp.broadcast_to(bcol_ref[1, rows, :], (t, t))], axis=1)
            if masked:
                sc = jnp.where(cmask_t, sc, NEG_INF)
            m_new = jnp.maximum(m, jnp.max(sc, axis=0, keepdims=True))
            alpha = jnp.exp2(m - m_new)
            p = jnp.exp2(sc - m_new)
            l_new = alpha * l + jnp.sum(p, axis=0, keepdims=True)
            pv = jnp.dot(vt_ref[j], p.astype(BF16), preferred_element_type=F32)
            return m_new, l_new, alpha * acc + pv

        def finish(i, carry):
            m, l, acc = carry
            r0 = pl.multiple_of(i * t, t)
            out = (acc / l).T
            lse2 = as_column(m + jnp.log2(l))
            lse_ref[0, 0, pl.ds(r0, t), :] = lse2[:t]
            lse_ref[0, 1, pl.ds(r0, t), :] = lse2[t:]
            o_ref[pl.ds(r0, t), :] = jnp.where(lane_lt64, out[:t], out[t:])

        init = (jnp.full((1, 2 * t), NEG_INF, F32), jnp.zeros((1, 2 * t), F32), jnp.zeros((LANES, 2 * t), F32))

        def q_block(i, _):
            qs_t = stacked_queries(i)
            carry = lax.fori_loop(0, i, lambda j, c: kv_step(j, c, qs_t, False), init)
            finish(i, kv_step(i, carry, qs_t, True))
            return 0

        lax.fori_loop(0, nb, q_block, 0)
        _ride_wait(rider, ride_refs, pl.program_id(0) == n_pairs - 1)

    in_specs = [_resident((s, qw), lambda p: (0, q_off + p)), _resident((s, qw), lambda p: (0, k_off + p)),
                _resident((s, LANES), lambda p: (0, v_off + p))]
    args = [q, k, v]
    if has_bias:
        in_specs.append(_resident((1, 2, nb, 1, t), lambda p: (p, 0, 0, 0, 0)))
        args.append(bias)
    out_specs = [pl.BlockSpec((s, LANES), lambda p: (0, p)), pl.BlockSpec((1, 2, s, 1), lambda p: (p, 0, 0, 0))]
    out_shape = [jax.ShapeDtypeStruct((s, n_pairs * LANES), F32), jax.ShapeDtypeStruct((n_pairs, 2, s, 1), F32)]
    scratch = [pltpu.VMEM((nb, LANES, t), BF16)] + ([pltpu.VMEM((2, s, 1), F32)] if has_bias else [])
    scratch += _add_rider(rider, in_specs, args, out_specs, out_shape)
    return pl.pallas_call(
        body, name=name, grid=(n_pairs,), in_specs=in_specs, out_specs=out_specs, out_shape=out_shape,
        scratch_shapes=scratch,
        compiler_params=_params(("parallel",) if rider is None else ("arbitrary",), VMEM_LIMIT),
    )(*args)


def _bwd_key_tile(s):
    return ATT_TK_BWD if s % ATT_TK_BWD == 0 else min(ATT_T, s)


def _flash_bwd(q, k, v, do, o, lse, bias, *, n_pairs, hw, q_off, k_off, v_off, scale, qk_dtype, name, rider=None):
    s = q.shape[0]
    t = min(ATT_T, s)
    nb = s // t
    tk = _bwd_key_tile(s)
    nbk = s // tk
    qw = 2 * hw
    has_bias = bias is not None
    c1 = scale * LOG2E

    def body(*refs):
        refs, ride_refs = _split_rider(refs, rider, n_in=7 if has_bias else 6, n_out=5 if has_bias else 3)
        if has_bias:
            (q_ref, k_ref, v_ref, do_ref, o_ref, lse_ref, b_ref, dq_ref, dk_ref, dv_ref, db_ref, dr_ref,
             dkt_ref, dvt_ref) = refs
            db_ref[...] = jnp.zeros_like(db_ref)
        else:
            q_ref, k_ref, v_ref, do_ref, o_ref, lse_ref, dq_ref, dk_ref, dv_ref, dkt_ref, dvt_ref = refs
            b_ref = db_ref = dr_ref = None
        _ride_start(rider, ride_refs, pl.program_id(0) == 0)
        dkt_ref[...] = jnp.zeros_like(dkt_ref)
        dvt_ref[...] = jnp.zeros_like(dvt_ref)
        q_in_tile = lax.broadcasted_iota(jnp.int32, (2 * t, tk), 0) % t
        k_in_tile = lax.broadcasted_iota(jnp.int32, (2 * t, tk), 1)
        lane_lt_hw = lax.broadcasted_iota(jnp.int32, (t, qw), 1) < hw

        def q_block(i, _):
            r0 = pl.multiple_of(i * t, t)
            qs = _stack_heads(q_ref[pl.ds(r0, t), :], hw)
            dos = _stack_heads(do_ref[pl.ds(r0, t), :], 64)
            ot = o_ref[pl.ds(r0, t), :]
            delta = jnp.sum(dos * jnp.concatenate([ot, ot], axis=0), axis=-1, keepdims=True)
            lse2 = jnp.concatenate([lse_ref[0, 0, pl.ds(r0, t), :], lse_ref[0, 1, pl.ds(r0, t), :]], axis=0)
            dosb = dos.astype(BF16)
            dos_t = dos.T.astype(BF16)
            qs_t = qs.astype(F32).T.astype(BF16)

            def kv_step(j, carry, masked):
                dq, rsum = carry
                c0 = pl.multiple_of(j * tk, tk)
                kt = k_ref[pl.ds(c0, tk), :]
                vt = v_ref[pl.ds(c0, tk), :]
                sc = lax.dot_general(qs, kt, _NT, preferred_element_type=F32) * c1
                if has_bias:
                    sc = sc + _stacked_rows(b_ref[0, 0, j], b_ref[0, 1, j], t)
                if masked:
                    sc = jnp.where(k_in_tile <= q_in_tile + (r0 - c0), sc, NEG_INF)
                p = jnp.exp2(sc - lse2)
                dp = lax.dot_general(dosb, vt, _NT, preferred_element_type=F32)
                ds = p * (dp - delta)
                dsb = ds.astype(BF16)
                pb = p.astype(BF16)
                if hw == LANES:
                    dvt_ref[j] += jnp.concatenate(
                        [jnp.dot(dos_t[:64, :t], pb[:t], preferred_element_type=F32),
                         jnp.dot(dos_t[64:, t:], pb[t:], preferred_element_type=F32)], axis=0)
                    dkt_ref[j] += jnp.concatenate(
                        [jnp.dot(qs_t[:hw, :t], dsb[:t], preferred_element_type=F32),
                         jnp.dot(qs_t[hw:, t:], dsb[t:], preferred_element_type=F32)], axis=0)
                else:
                    dvt_ref[j] += jnp.dot(dos_t, pb, preferred_element_type=F32)
                    dkt_ref[j] += jnp.dot(qs_t, dsb, preferred_element_type=F32)
                if has_bias:
                    db_ref[0, 0, j] += jnp.sum(ds[:t], axis=0, keepdims=True)
                    db_ref[0, 1, j] += jnp.sum(ds[t:], axis=0, keepdims=True)
                    rsum = rsum + jnp.sum(ds, axis=-1, keepdims=True)
                return dq + jnp.dot(dsb, kt, preferred_element_type=F32), rsum

            init = (jnp.zeros((2 * t, qw), F32), jnp.zeros((2 * t, 1), F32))
            diag = r0 // tk
            carry = lax.fori_loop(0, diag, functools.partial(kv_step, masked=False), init)
            dq, rsum = kv_step(diag, carry, True)
            dq = dq * scale
            dq_ref[pl.ds(r0, t), :] = jnp.where(lane_lt_hw, dq[:t], dq[t:]).astype(qk_dtype)
            if has_bias:
                rsum_row = jnp.broadcast_to(rsum, (2 * t, LANES)).T[0:1]
                dr_ref[0, 0, i] = rsum_row[:, :t]
                dr_ref[0, 1, i] = rsum_row[:, t:]
            return 0

        lax.fori_loop(0, nb, q_block, 0)

        def k_block(j, _):
            c0 = pl.multiple_of(j * tk, tk)
            dk_ref[pl.ds(c0, tk), :] = (dkt_ref[j].T * scale).astype(qk_dtype)
            dv_ref[pl.ds(c0, tk), :] = dvt_ref[j].T.astype(BF16)
            return 0

        lax.fori_loop(0, nbk, k_block, 0)
        _ride_wait(rider, ride_refs, pl.program_id(0) == n_pairs - 1)

    in_specs = [_resident((s, qw), lambda p: (0, q_off + p)), _resident((s, qw), lambda p: (0, k_off + p)),
                _resident((s, LANES), lambda p: (0, v_off + p)),
                _resident((s, LANES), lambda p: (0, p)), _resident((s, LANES), lambda p: (0, p)),
                _resident((1, 2, s, 1), lambda p: (p, 0, 0, 0))]
    args = [q, k, v, do, o, lse]
    out_specs = [pl.BlockSpec((s, qw), lambda p: (0, p)), pl.BlockSpec((s, qw), lambda p: (0, p)),
                 pl.BlockSpec((s, LANES), lambda p: (0, p))]
    out_shape = [jax.ShapeDtypeStruct((s, n_pairs * qw), qk_dtype), jax.ShapeDtypeStruct((s, n_pairs * qw), qk_dtype),
                 jax.ShapeDtypeStruct((s, n_pairs * LANES), BF16)]
    if has_bias:
        in_specs.append(_resident((1, 2, nbk, 1, tk), lambda p: (p, 0, 0, 0, 0)))
        args.append(bias)
        out_specs.append(pl.BlockSpec((1, 2, nbk, 1, tk), lambda p: (p, 0, 0, 0, 0)))
        out_shape.append(jax.ShapeDtypeStruct((n_pairs, 2, nbk, 1, tk), F32))
        out_specs.append(pl.BlockSpec((1, 2, nb, 1, t), lambda p: (p, 0, 0, 0, 0)))
        out_shape.append(jax.ShapeDtypeStruct((n_pairs, 2, nb, 1, t), F32))
    scratch = [pltpu.VMEM((nbk, qw, tk), F32), pltpu.VMEM((nbk, LANES, tk), F32)]
    scratch += _add_rider(rider, in_specs, args, out_specs, out_shape)
    return pl.pallas_call(
        body, name=name, grid=(n_pairs,), in_specs=in_specs, out_specs=out_specs, out_shape=out_shape,
        scratch_shapes=scratch,
        compiler_params=_params(("parallel",) if rider is None else ("arbitrary",), VMEM_LIMIT),
    )(*args)


def _alibi_slope(h):
    return 2.0 ** (-8.0 * (h + 1.0) / SWA_HEADS)


SWA_ROWS = 512
SWA_SCALE = SWA_DIM ** -0.5


def _swa_geometry(i):
    w = WINDOW
    r0 = pl.multiple_of(i * w, w)
    b0 = pl.multiple_of(jnp.maximum(i - 1, 0) * w, w)
    row = lax.broadcasted_iota(jnp.int32, (w, 2 * w), 0)
    col = lax.broadcasted_iota(jnp.int32, (w, 2 * w), 1)
    dist = row - col + (r0 - b0)
    valid = (dist >= 0) & (dist < w)
    return r0, b0, dist.astype(F32), valid


def _swa_q_head(qblk, h):
    kv = h // (SWA_HEADS // SWA_KV_HEADS)
    if h % 2 != kv:
        qblk = pltpu.roll(qblk, 64, axis=1)
    return jnp.where(_head_mask(qblk.shape, kv), qblk, 0.0)


SWA_GROUP = SWA_HEADS // SWA_KV_HEADS


def _swa_stack(ref, rs, grp):
    parts = []
    for a in range(SWA_GROUP):
        h = SWA_GROUP * grp + a
        parts.append(_swa_q_head(ref[rs, (h // 2) * LANES:(h // 2 + 1) * LANES].astype(F32), h))
    return jnp.concatenate(parts, axis=0)


def _swa_unstack(x, grp):
    tiles = []
    for a in range(SWA_GROUP):
        h = SWA_GROUP * grp + a
        tile = x[a * WINDOW:(a + 1) * WINDOW]
        tiles.append(pltpu.roll(tile, 64, axis=1) if h % 2 != grp else tile)
    return tiles


def _swa_head_column(vals):
    return jnp.concatenate([jnp.full((WINDOW, 1), v, F32) for v in vals], axis=0)


def _swa_logits(qs, kb, dist, valid, grp):
    slopes = _swa_head_column([_alibi_slope(SWA_GROUP * grp + a) for a in range(SWA_GROUP)])
    dist4 = jnp.concatenate([dist] * SWA_GROUP, axis=0)
    valid4 = jnp.concatenate([valid] * SWA_GROUP, axis=0)
    sc = lax.dot_general(qs, kb, _NT, preferred_element_type=F32) * SWA_SCALE - slopes * dist4
    return jnp.where(valid4, sc, NEG_INF)


def _swa_merge_heads(tiles):
    lt64 = lax.broadcasted_iota(jnp.int32, (WINDOW, LANES), 1) < 64
    return jnp.concatenate([jnp.where(lt64, tiles[2 * b], tiles[2 * b + 1]) for b in range(SWA_HEADS // 2)], axis=1)


def _swa_fwd(z0b, sinks, *, name):
    s = z0b.shape[0]
    w = WINDOW
    rows = min(SWA_ROWS, s)
    per_step = rows // w
    qcols = SWA_HEADS * SWA_DIM

    def body(sink_ref, q_ref, k_ref, v_ref, o_ref, lse_ref):
        g = pl.program_id(0)
        for ii in range(per_step):
            rs = slice(ii * w, (ii + 1) * w)
            r0, b0, dist, valid = _swa_geometry(g * per_step + ii)
            kb = k_ref[pl.ds(b0, 2 * w), :]
            vb = v_ref[pl.ds(b0, 2 * w), :]
            o_tiles = []
            for h in range(SWA_HEADS):
                kv = h // SWA_GROUP
                qh = _swa_q_head(q_ref[rs, (h // 2) * LANES:(h // 2 + 1) * LANES].astype(F32), h).astype(BF16)
                sc = lax.dot_general(qh, kb, _NT, preferred_element_type=F32) * SWA_SCALE - _alibi_slope(h) * dist
                sc = jnp.where(valid, sc, NEG_INF)
                sink = sink_ref[0, h]
                m = jnp.maximum(jnp.max(sc, axis=-1, keepdims=True), sink)
                p = jnp.exp(sc - m)
                l = jnp.sum(p, axis=-1, keepdims=True) + jnp.exp(sink - m)
                oh = jnp.dot(p.astype(BF16), vb, preferred_element_type=F32) / l
                o_tiles.append(pltpu.roll(oh, 64, axis=1) if h % 2 != kv else oh)
                lse_ref[h, rs, :] = m + jnp.log(l)
            o_ref[rs, :] = _swa_merge_heads(o_tiles)

    return pl.pallas_call(
        body, name=name, grid=(s // rows,),
        in_specs=[pl.BlockSpec(memory_space=pltpu.SMEM),
                  pl.BlockSpec((rows, qcols), lambda g: (g, 0)),
                  pl.BlockSpec((s, LANES), lambda g: (0, 4)), pl.BlockSpec((s, LANES), lambda g: (0, 5))],
        out_specs=[pl.BlockSpec((rows, qcols), lambda g: (g, 0)), pl.BlockSpec((SWA_HEADS, rows, 1), lambda g: (0, g, 0))],
        out_shape=[jax.ShapeDtypeStruct((s, qcols), F32), jax.ShapeDtypeStruct((SWA_HEADS, s, 1), F32)],
        compiler_params=_params(("parallel",), VMEM_LIMIT),
    )(sinks, z0b, z0b, z0b)


def _swa_bwd(z0b, sinks, do, o, lse, *, name):
    s = z0b.shape[0]
    w = WINDOW
    rows = min(SWA_ROWS, s)
    per_step = rows // w
    qcols = SWA_HEADS * SWA_DIM
    nblk = s // w

    def body(sink_ref, q_ref, k_ref, v_ref, do_ref, o_ref, lse_ref, dq_ref, dkt_ref, dvt_ref, dsink_ref):
        g = pl.program_id(0)

        @pl.when(g == 0)
        def _():
            dkt_ref[...] = jnp.zeros_like(dkt_ref)
            dvt_ref[...] = jnp.zeros_like(dvt_ref)
            dsink_ref[...] = jnp.zeros_like(dsink_ref)

        for ii in range(per_step):
            i = g * per_step + ii
            rs = slice(ii * w, (ii + 1) * w)
            r0, b0, dist, valid = _swa_geometry(i)
            j0 = jnp.maximum(i - 1, 0)
            kb = k_ref[pl.ds(b0, 2 * w), :]
            vb = v_ref[pl.ds(b0, 2 * w), :]
            dq_tiles = []
            for grp in range(SWA_KV_HEADS):
                heads = [SWA_GROUP * grp + a for a in range(SWA_GROUP)]
                qs32 = _swa_stack(q_ref, rs, grp)
                dos32 = _swa_stack(do_ref, rs, grp)
                delta = jnp.sum(dos32 * _swa_stack(o_ref, rs, grp), axis=-1, keepdims=True)
                lse = jnp.concatenate([lse_ref[h, rs, :] for h in heads], axis=0)
                sink = _swa_head_column([sink_ref[0, h] for h in heads])
                p = jnp.exp(_swa_logits(qs32.astype(BF16), kb, dist, valid, grp) - lse)
                dp = lax.dot_general(dos32.astype(BF16), vb, _NT, preferred_element_type=F32)
                ds = p * (dp - delta)
                dsb = ds.astype(BF16)
                d_sink = jnp.exp(sink - lse) * delta
                for a, h in enumerate(heads):
                    dsink_ref[h:h + 1, :] += jnp.broadcast_to(-jnp.sum(d_sink[a * w:(a + 1) * w]), (1, LANES))
                dvt = jnp.dot(dos32.T.astype(BF16), p.astype(BF16), preferred_element_type=F32)
                dkt = jnp.dot(qs32.T.astype(BF16), dsb, preferred_element_type=F32) * SWA_SCALE
                dvt_ref[j0] += dvt[:, :w]
                dvt_ref[j0 + 1] += dvt[:, w:]
                dkt_ref[j0] += dkt[:, :w]
                dkt_ref[j0 + 1] += dkt[:, w:]
                dq_tiles += _swa_unstack(jnp.dot(dsb, kb, preferred_element_type=F32) * SWA_SCALE, grp)
            dq_ref[rs, :] = _swa_merge_heads(dq_tiles)

    return pl.pallas_call(
        body, name=name, grid=(s // rows,),
        in_specs=[pl.BlockSpec(memory_space=pltpu.SMEM),
                  pl.BlockSpec((rows, qcols), lambda g: (g, 0)),
                  pl.BlockSpec((s, LANES), lambda g: (0, 4)), pl.BlockSpec((s, LANES), lambda g: (0, 5)),
                  pl.BlockSpec((rows, qcols), lambda g: (g, 0)), pl.BlockSpec((rows, qcols), lambda g: (g, 0)),
                  pl.BlockSpec((SWA_HEADS, rows, 1), lambda g: (0, g, 0))],
        out_specs=[pl.BlockSpec((rows, qcols), lambda g: (g, 0)),
                   pl.BlockSpec((nblk, LANES, w), lambda g: (0, 0, 0)),
                   pl.BlockSpec((nblk, LANES, w), lambda g: (0, 0, 0)),
                   pl.BlockSpec((SWA_HEADS, LANES), lambda g: (0, 0))],
        out_shape=[jax.ShapeDtypeStruct((s, qcols), F32),
                   jax.ShapeDtypeStruct((nblk, LANES, w), F32), jax.ShapeDtypeStruct((nblk, LANES, w), F32),
                   jax.ShapeDtypeStruct((SWA_HEADS, LANES), F32)],
        compiler_params=_params(("arbitrary",), VMEM_LIMIT),
    )(sinks, z0b, z0b, z0b, do, o, lse)


CUM_T = 256


def _split3(x):
    hi = x.astype(BF16)
    r1 = x - hi.astype(F32)
    mid = r1.astype(BF16)
    lo = (r1 - mid.astype(F32)).astype(BF16)
    return hi, mid, lo


def _tri_dot(tri, x):
    hi, mid, lo = _split3(x)
    out = jnp.dot(tri, hi, preferred_element_type=F32)
    out = out + jnp.dot(tri, mid, preferred_element_type=F32)
    return out + jnp.dot(tri, lo, preferred_element_type=F32)


def _logf_fwd(zf, bf, *, name):
    s = zf.shape[0]
    t = CUM_T
    nb = s // t

    def body(z_ref, b_ref, c_ref, carry_ref):
        i = pl.program_id(0)

        @pl.when(i == 0)
        def _():
            carry_ref[...] = jnp.zeros_like(carry_ref)

        x = z_ref[...] + b_ref[...]
        lf = jnp.minimum(x, 0.0) - jnp.log(1.0 + jnp.exp(-jnp.abs(x)))
        row = lax.broadcasted_iota(jnp.int32, (t, t), 0)
        col = lax.broadcasted_iota(jnp.int32, (t, t), 1)
        tri = jnp.where(col <= row, 1.0, 0.0).astype(BF16)
        c = _tri_dot(tri, lf) + carry_ref[...]
        c_ref[...] = c
        carry_ref[...] = c[t - 1:t, :]

    return pl.pallas_call(
        body, name=name, grid=(nb,),
        in_specs=[pl.BlockSpec((t, LANES), lambda i: (i, 0)), pl.BlockSpec((1, LANES), lambda i: (0, 0))],
        out_specs=pl.BlockSpec((t, LANES), lambda i: (i, 0)),
        out_shape=jax.ShapeDtypeStruct((s, LANES), F32),
        scratch_shapes=[pltpu.VMEM((1, LANES), F32)],
        compiler_params=_params(("arbitrary",)),
    )(zf, bf)


def _logf_bwd(dc, zf, bf, *, name):
    s = zf.shape[0]
    t = CUM_T
    nb = s // t

    def body(dc_ref, z_ref, b_ref, dz_ref, db_ref, carry_ref):
        i = pl.program_id(0)

        @pl.when(i == 0)
        def _():
            carry_ref[...] = jnp.zeros_like(carry_ref)
            db_ref[...] = jnp.zeros_like(db_ref)

        row = lax.broadcasted_iota(jnp.int32, (t, t), 0)
        col = lax.broadcasted_iota(jnp.int32, (t, t), 1)
        tri = jnp.where(col >= row, 1.0, 0.0).astype(BF16)
        dlf = _tri_dot(tri, dc_ref[...]) + carry_ref[...]
        carry_ref[...] = dlf[0:1, :]
        x = z_ref[...] + b_ref[...]
        dz = dlf * _sigmoid(-x)
        dz_ref[...] = dz.astype(BF16)
        db_ref[...] += jnp.sum(dz, axis=0, keepdims=True)

    return pl.pallas_call(
        body, name=name, grid=(nb,),
        in_specs=[pl.BlockSpec((t, LANES), lambda i: (nb - 1 - i, 0)), pl.BlockSpec((t, LANES), lambda i: (nb - 1 - i, 0)),
                  pl.BlockSpec((1, LANES), lambda i: (0, 0))],
        out_specs=[pl.BlockSpec((t, LANES), lambda i: (nb - 1 - i, 0)), pl.BlockSpec((1, LANES), lambda i: (0, 0))],
        out_shape=[jax.ShapeDtypeStruct((s, LANES), BF16), jax.ShapeDtypeStruct((1, LANES), F32)],
        scratch_shapes=[pltpu.VMEM((1, LANES), F32)],
        compiler_params=_params(("arbitrary",)),
    )(dc, zf, bf)


def _loss_head(x2, g, target, *, name):
    s = x2.shape[0]
    tm = _tile(s, ROW_TILES)

    def body(x_ref, g_ref, t_ref, dx_ref, loss_ref, dg_ref):
        i = pl.program_id(0)

        @pl.when(i == 0)
        def _():
            loss_ref[...] = jnp.zeros_like(loss_ref)
            dg_ref[...] = jnp.zeros_like(dg_ref)

        xf = x_ref[...]
        r = lax.rsqrt(jnp.mean(xf * xf, axis=-1, keepdims=True) + EPS)
        xh = xf * r
        gv = g_ref[...]
        err = xh * gv - t_ref[...]
        loss_ref[...] += jnp.broadcast_to(0.5 * jnp.sum(jnp.mean(err * err, axis=-1, keepdims=True)), loss_ref.shape)
        dy = err * (1.0 / D_MODEL)
        dg_ref[...] += jnp.sum(dy * xh, axis=0, keepdims=True)
        dxh = dy * gv
        dx_ref[...] = r * (dxh - xh * jnp.mean(dxh * xh, axis=-1, keepdims=True))

    return pl.pallas_call(
        body, name=name, grid=(s // tm,),
        in_specs=[pl.BlockSpec((tm, D_MODEL), lambda i: (i, 0)), pl.BlockSpec((1, D_MODEL), lambda i: (0, 0)),
                  pl.BlockSpec((tm, D_MODEL), lambda i: (i, 0))],
        out_specs=[pl.BlockSpec((tm, D_MODEL), lambda i: (i, 0)), pl.BlockSpec((8, LANES), lambda i: (0, 0)),
                   pl.BlockSpec((1, D_MODEL), lambda i: (0, 0))],
        out_shape=[jax.ShapeDtypeStruct((s, D_MODEL), F32), jax.ShapeDtypeStruct((8, LANES), F32),
                   jax.ShapeDtypeStruct((1, D_MODEL), F32)],
        compiler_params=_params(("arbitrary",)),
    )(x2, g, target)


def _sum_pieces(p_ref):
    g = p_ref[0].astype(F32)
    for k in range(1, N_DEV):
        g = g + p_ref[k].astype(F32)
    return g


def _adam_update(g, w, m, v):
    bc1 = 1.0 - ADAM_B1 ** ADAM_STEP
    bc2 = 1.0 - ADAM_B2 ** ADAM_STEP
    nm = ADAM_B1 * m + (1.0 - ADAM_B1) * g
    nv = ADAM_B2 * v + (1.0 - ADAM_B2) * (g * g)
    m_hat = nm / bc1
    v_hat = nv / bc2
    return -ADAM_LR * (m_hat / (jnp.sqrt(v_hat) + ADAM_EPS) + ADAM_WD * w), nm, nv


def _adamw(pieces, w, m, v, *, name):
    rows, cols = w.shape
    tr = _tile(rows, (RB1, RB0, SMALL_ROWS))

    def body(p_ref, w_ref, m_ref, v_ref, g_ref, d_ref, nm_ref, nv_ref):
        g = _sum_pieces(p_ref)
        g_ref[...] = g
        d_ref[...], nm_ref[...], nv_ref[...] = _adam_update(g, w_ref[...], m_ref[...], v_ref[...])

    spec = pl.BlockSpec((tr, cols), lambda i: (i, 0))
    shape = jax.ShapeDtypeStruct((rows, cols), F32)
    return pl.pallas_call(
        body, name=name, grid=(rows // tr,),
        in_specs=[pl.BlockSpec((N_DEV, tr, cols), lambda i: (0, i, 0)), spec, spec, spec],
        out_specs=[spec, spec, spec, spec], out_shape=[shape, shape, shape, shape],
        compiler_params=_params(("parallel",)),
    )(pieces, w, m, v)


def _sum8(pieces, rows, *, name):
    cols = pieces.shape[2]
    tr = _tile(rows, (176, 96))

    def body(p_ref, g_ref):
        g_ref[...] = _sum_pieces(p_ref)

    return pl.pallas_call(
        body, name=name, grid=(rows // tr,),
        in_specs=[pl.BlockSpec((N_DEV, tr, cols), lambda i: (0, i, 0))],
        out_specs=pl.BlockSpec((tr, cols), lambda i: (i, 0)),
        out_shape=jax.ShapeDtypeStruct((rows, cols), F32),
        compiler_params=_params(("parallel",)),
    )(pieces)


def _adamw_native(g, w, m, v, *, name):
    rows, cols = w.shape
    tr = _tile(rows, (256, 128))

    def body(g_ref, w_ref, m_ref, v_ref, d_ref, nm_ref, nv_ref):
        d_ref[...], nm_ref[...], nv_ref[...] = _adam_update(g_ref[...], w_ref[...], m_ref[...], v_ref[...])

    spec = pl.BlockSpec((tr, cols), lambda i: (i, 0))
    shape = jax.ShapeDtypeStruct((rows, cols), F32)
    return pl.pallas_call(
        body, name=name, grid=(rows // tr,), in_specs=[spec, spec, spec, spec],
        out_specs=[spec, spec, spec], out_shape=[shape, shape, shape],
        compiler_params=_params(("parallel",)),
    )(g, w, m, v)


MESH = pl.DeviceIdType.MESH
ANY = pl.BlockSpec(memory_space=pl.ANY)


def _all_gather(shard, *, name):
    rows, lanes = shard.shape

    def body(x_ref, out_ref, send_sems, recv_sems, local_sem):
        x, y, c = lax.axis_index("x"), lax.axis_index("y"), lax.axis_index("c")
        me, sibling = (x, y, c), (x, y, 1 - c)
        chips = [(1 - x, y), (x, 1 - y), (1 - x, 1 - y)]

        def block(px, py, pc):
            return out_ref.at[4 * px + 2 * py + pc]

        def copy(k, blk, to, src=None):
            return pltpu.make_async_remote_copy(
                src_ref=block(*blk) if src is None else src, dst_ref=block(*blk),
                send_sem=send_sems.at[k], recv_sem=recv_sems.at[k], device_id=to, device_id_type=MESH)

        mine = pltpu.make_async_copy(x_ref, block(*me), local_sem)
        mine.start()
        first = [copy(0, me, sibling, src=x_ref)]
        first += [copy(1 + j, me, (*chip, c), src=x_ref) for j, chip in enumerate(chips)]
        for cp in first:
            cp.start()
        passed = [copy(4 + j, (*chip, c), sibling) for j, chip in enumerate(chips)]
        for j, chip in enumerate(chips):
            copy(1 + j, (*chip, c), me).wait_recv()
            passed[j].start()
        copy(0, sibling, me).wait_recv()
        for j, chip in enumerate(chips):
            copy(4 + j, (*chip, 1 - c), me).wait_recv()
        for cp in first + passed:
            cp.wait_send()
        mine.wait()

    return pl.pallas_call(
        body, name=name, out_shape=jax.ShapeDtypeStruct((N_DEV, rows, lanes), shard.dtype),
        in_specs=[ANY], out_specs=ANY,
        scratch_shapes=[pltpu.SemaphoreType.DMA((7,)), pltpu.SemaphoreType.DMA((7,)), pltpu.SemaphoreType.DMA(())],
    )(shard)


def _peer_copies(kind, src_ref, out_ref, send_sems, recv_sems, local_sem):
    x, y, c = lax.axis_index("x"), lax.axis_index("y"), lax.axis_index("c")
    me = 4 * x + 2 * y + c

    def src(idx):
        return src_ref.at[idx] if kind == "exchange" else src_ref

    mine = pltpu.make_async_copy(src(me), out_ref.at[me], local_sem)
    copies = []
    for r in range(1, N_DEV):
        px = 1 - x if r & 4 else x
        py = 1 - y if r & 2 else y
        pc = 1 - c if r & 1 else c
        copies.append(pltpu.make_async_remote_copy(
            src_ref=src(4 * px + 2 * py + pc), dst_ref=out_ref.at[me],
            send_sem=send_sems.at[r - 1], recv_sem=recv_sems.at[r - 1],
            device_id=(px, py, pc), device_id_type=MESH))
    return mine, copies


PEER_SEMS = [pltpu.SemaphoreType.DMA((7,)), pltpu.SemaphoreType.DMA((7,)), pltpu.SemaphoreType.DMA(())]


def _add_rider(rider, in_specs, args, out_specs, out_shape):
    if rider is None:
        return []
    _, arr = rider
    in_specs.append(ANY)
    args.append(arr)
    out_specs.append(ANY)
    out_shape.append(jax.ShapeDtypeStruct((N_DEV,) + arr.shape[-2:], arr.dtype))
    return list(PEER_SEMS)


def _split_rider(refs, rider, n_in, n_out):
    if rider is None:
        return refs, None
    refs = list(refs)
    rin = refs.pop(n_in)
    rout = refs.pop(n_in + n_out)
    return refs[:-3], (rin, rout, *refs[-3:])


def _ride_start(rider, ride_refs, first):
    if rider is None:
        return

    @pl.when(first)
    def _():
        mine, copies = _peer_copies(rider[0], *ride_refs)
        mine.start()
        for cp in copies:
            cp.start()


def _ride_wait(rider, ride_refs, last):
    if rider is None:
        return

    @pl.when(last)
    def _():
        mine, copies = _peer_copies(rider[0], *ride_refs)
        for cp in copies:
            cp.wait()
        mine.wait()


def _gathered_cols(blocks, kdim):
    n = blocks.shape[1] * WIDE // kdim
    return blocks.reshape(N_DEV, kdim, n).transpose(1, 0, 2).reshape(kdim, N_DEV * n)


def _scatter_cols(dw):
    kdim, n8 = dw.shape
    n = n8 // N_DEV
    return dw.reshape(kdim, N_DEV, n).transpose(1, 0, 2).reshape(N_DEV, kdim * n // WIDE, WIDE)


def _pad_rows(a, rows):
    pad = [(0, 0)] * a.ndim
    pad[-2] = (0, rows - a.shape[-2])
    return jnp.pad(a, pad)


def _layer0_in_weight_t(wt):
    cq, ckv, kpe = wt[0:256], wt[256:384], wt[384:416]
    q_s, k_s, v_s, gate = wt[416:928], wt[928:1056], wt[1056:1184], wt[1184:2208]
    z = jnp.zeros((64, wt.shape[1]), wt.dtype)
    return jnp.concatenate([gate, cq, ckv, z, kpe, z[:32], q_s, k_s, v_s], axis=0)


def _layer0_in_grad_t(dwt):
    gate, cq, ckv, kpe = dwt[0:1024], dwt[1024:1280], dwt[1280:1408], dwt[1472:1504]
    q_s, k_s, v_s = dwt[1536:2048], dwt[2048:2176], dwt[2176:2304]
    return jnp.concatenate([cq, ckv, kpe, q_s, k_s, v_s, gate], axis=0)


def _layer1_in_weight_t(wt):
    main = jnp.concatenate([wt[:3 * D_MODEL], wt[3 * D_MODEL + FOX_HEADS:]], axis=0)
    return main, _pad_rows(wt[3 * D_MODEL:3 * D_MODEL + FOX_HEADS], LANES)


def _layer1_in_grad_t(d_main, d_wft):
    return jnp.concatenate([d_main[:3 * D_MODEL], d_wft[:FOX_HEADS], d_main[3 * D_MODEL:]], axis=0)


def _q_up_weight(w):
    return jnp.pad(w.reshape(MLA_Q_RANK, MLA_HEADS, 96), ((0, 0), (0, 0), (0, 32))).reshape(MLA_Q_RANK, MLA_HEADS * LANES)


def _q_up_grad(dwp):
    return dwp.reshape(MLA_Q_RANK, MLA_HEADS, LANES)[:, :, :96].reshape(MLA_Q_RANK, MLA_HEADS * 96)


def _kv_up_weight(w):
    w4 = w.reshape(MLA_KV_RANK, MLA_HEADS, 2, 64)
    kp = jnp.pad(w4[:, :, 0, :], ((0, 0), (0, 0), (0, 64))).reshape(MLA_KV_RANK, MLA_HEADS * LANES)
    vp = w4[:, :, 1, :].reshape(MLA_KV_RANK, MLA_HEADS * 64)
    return jnp.concatenate([kp, vp], axis=1)


def _kv_up_grad(dwp):
    dk = dwp[:, :MLA_HEADS * LANES].reshape(MLA_KV_RANK, MLA_HEADS, LANES)[:, :, :64]
    dv = dwp[:, MLA_HEADS * LANES:].reshape(MLA_KV_RANK, MLA_HEADS, 64)
    return jnp.stack([dk, dv], axis=2).reshape(MLA_KV_RANK, MLA_HEADS * LANES)


def _pad_lanes(a):
    return jnp.pad(a, ((0, 0), (0, LANES - a.shape[1])))


def _small_pack(g_in, g_final, g_q_a, g_kv_a, sinks, b_f, loss):
    rows = [g_in.reshape(8, LANES), g_final.reshape(8, LANES), g_q_a.reshape(2, LANES), g_kv_a.reshape(1, LANES),
            _pad_lanes(sinks.reshape(1, -1)), _pad_lanes(b_f.reshape(1, -1)), _pad_lanes(loss.reshape(1, 1)),
            jnp.zeros((2, LANES), F32)]
    return jnp.concatenate(rows, axis=0)


def _small_unpack(a):
    return (a[0:8].reshape(1, D_MODEL), a[8:16].reshape(D_MODEL), a[16:18].reshape(1, MLA_Q_RANK),
            a[18:19].reshape(1, MLA_KV_RANK), a[19:20, :SWA_HEADS], a[20:21, :FOX_HEADS], a[21, 0])


def _local_step(x, positions, target, e_g_in, w0t, e_g_q_a, wq, e_g_kv_a, wkv, e_sinks,
                late, o_b_f, g_final, scatter1=None, scatter0=None):
    s = x.shape[0]
    mla_scale = (MLA_NOPE + MLA_ROPE) ** -0.5
    fox_scale = FOX_DIM ** -0.5
    n0a = Z0A_UNITS * LANES

    inv_freq = 1.0 / (ROPE_THETA ** (jnp.arange(0, MLA_ROPE, 2, dtype=F32) / MLA_ROPE))
    ang = positions.astype(F32)[:, None] * inv_freq
    cos, sin = jnp.cos(ang), jnp.sin(ang)
    ones, zeros = jnp.ones((s, 64), F32), jnp.zeros((s, 64), F32)
    cos_t = jnp.concatenate([ones, cos, cos, ones[:, :32]], axis=1)
    sin_t = jnp.concatenate([zeros, -sin, sin, zeros[:, :32]], axis=1)

    h0 = _rmsnorm_fwd(x, e_g_in, width=D_MODEL, col_blk=0, name="l0_norm")
    z0a = _matmul(h0, w0t, tb=True, b_rows=(0, n0a), name="l0_in_a")
    z0b = _matmul(h0, w0t, tb=True, b_rows=(n0a, Z0B_UNITS * LANES), name="l0_in_b", out_dtype=BF16)
    cqn = _rmsnorm_fwd(z0a, e_g_q_a, width=MLA_Q_RANK, col_blk=4, name="l0_q_norm")
    ckvn = _rmsnorm_fwd(z0a, e_g_kv_a, width=MLA_KV_RANK, col_blk=10, name="l0_kv_norm")
    qp = _matmul(cqn, wq, name="l0_q_up")
    kvp = _matmul(ckvn, wkv, name="l0_kv_up", out_dtype=BF16)
    qm, km = _rope_fwd(qp, kvp, z0a, cos_t, sin_t, name="l0_rope")
    gathers = len(late) == 2
    res = _flash_fwd(qm, km, kvp, None, n_pairs=MLA_HEADS // 2, hw=LANES, q_off=0, k_off=0, v_off=MLA_HEADS,
                     scale=mla_scale, name="l0_mla_fwd", rider=("gather", late[0]) if gathers else None)
    o_mla, lse_mla = res[0], res[1]
    wo0, o_g_in, w1t, wft, wo1 = late[1](res[2]) if gathers else late
    o_swa, lse_swa = _swa_fwd(z0b, e_sinks, name="l0_swa_fwd")
    og0 = _gate_fwd([o_mla, o_swa], z0a, name="l0_gate")
    x1 = _matmul(og0, wo0, add=x, name="l0_out")

    h1 = _rmsnorm_fwd(x1, o_g_in, width=D_MODEL, col_blk=0, name="l1_norm")
    z1 = _matmul(h1, w1t, tb=True, b_rows=(0, 3 * D_MODEL), name="l1_in_qkv", out_dtype=BF16)
    gate1 = _matmul(h1, w1t, tb=True, b_rows=(3 * D_MODEL, D_MODEL), name="l1_in_gate")
    zf = _matmul(h1, wft, tb=True, name="l1_in_f")
    bf = _pad_lanes(o_b_f)
    log_cum = _logf_fwd(zf, bf, name="l1_logf")
    bias2 = (-LOG2E * log_cum[:, :FOX_HEADS]).T
    tk_bwd = _bwd_key_tile(s)
    bias = bias2.reshape(FOX_HEADS // 2, 2, s // tk_bwd, 1, tk_bwd)
    t_fwd = _fwd_tile(s)
    o_fox, lse_fox = _flash_fwd(z1, z1, z1, bias2.reshape(FOX_HEADS // 2, 2, s // t_fwd, 1, t_fwd),
                                n_pairs=FOX_HEADS // 2, hw=64, q_off=0, k_off=8, v_off=16, scale=fox_scale,
                                name="l1_fox_fwd")
    og1 = _gate_fwd([o_fox], gate1, name="l1_gate")
    x2 = _matmul(og1, wo1, add=x1, name="l1_out")

    dx2, loss_part, d_g_final = _loss_head(x2, g_final.reshape(1, D_MODEL), target, name="loss_head")

    d_wo1 = _matmul(og1, dx2, ta=True, name="l1_out_dw")
    d_og1 = _matmul(dx2, wo1, tb=True, name="l1_out_dx")
    do_fox, d_gate1 = _gate_bwd(d_og1, [o_fox], gate1, name="l1_gate_bwd")
    dq1, dk1, dv1, dbias, drow = _flash_bwd(z1, z1, z1, do_fox, o_fox, lse_fox, bias, n_pairs=FOX_HEADS // 2, hw=64,
                                            q_off=0, k_off=8, v_off=16, scale=fox_scale, qk_dtype=BF16,
                                            name="l1_fox_bwd")
    d_log_cum = (drow.reshape(FOX_HEADS, s) - dbias.reshape(FOX_HEADS, s)).T
    d_log_cum = jnp.pad(d_log_cum, ((0, 0), (0, LANES - FOX_HEADS)))
    d_zf, d_bf = _logf_bwd(d_log_cum, zf, bf, name="l1_logf_bwd")
    dz1 = jnp.concatenate([dq1, dk1, dv1, d_gate1], axis=1)
    d_w1t = _matmul(dz1, h1, ta=True, name="l1_in_dw")
    d_wft = _matmul(d_zf, h1, ta=True, name="l1_in_f_dw")
    dh1 = _matmul(dz1, w1t, name="l1_in_dx")
    dh1 = _matmul(d_zf, wft, add=dh1, name="l1_in_f_dx")
    dx1, d_o_g_in = _rmsnorm_bwd(x1, o_g_in, dh1, width=D_MODEL, col_blk=0, add=dx2, name="l1_norm_bwd")

    d_wo0 = _matmul(og0, dx1, ta=True, name="l0_out_dw")
    d_og0 = _matmul(dx1, wo0, tb=True, name="l0_out_dx")
    do_mla, do_swa, d_gate0 = _gate_bwd(d_og0, [o_mla, o_swa], z0a, name="l0_gate_bwd")
    dq_s, dkt_s, dvt_s, d_sinks = _swa_bwd(z0b, e_sinks, do_swa, o_swa, lse_swa, name="l0_swa_bwd")
    dk_s = dkt_s.transpose(0, 2, 1).reshape(s, LANES)
    dv_s = dvt_s.transpose(0, 2, 1).reshape(s, LANES)
    rider = None
    if scatter1 is not None:
        rider = ("exchange", scatter1(dict(w1t=d_w1t, wft=d_wft, wo1=d_wo1, o_g_in=d_o_g_in, wo0=d_wo0)))
    res = _flash_bwd(qm, km, kvp, do_mla, o_mla, lse_mla, None, n_pairs=MLA_HEADS // 2, hw=LANES, q_off=0, k_off=0,
                     v_off=MLA_HEADS, scale=mla_scale, qk_dtype=F32, name="l0_mla_bwd", rider=rider)
    dqm, dkm, dvm = res[0], res[1], res[2]
    recv1 = res[3] if rider is not None else None
    d_qp, d_kvp, d_kpe = _rope_bwd(dqm, dkm, dvm, cos_t, sin_t, name="l0_rope_bwd")
    d_wq = _matmul(cqn, d_qp, ta=True, name="l0_q_up_dw")
    d_cqn = _matmul(d_qp, wq, tb=True, name="l0_q_up_dx")
    d_wkv = _matmul(ckvn, d_kvp, ta=True, name="l0_kv_up_dw")
    d_ckvn = _matmul(d_kvp, wkv, tb=True, name="l0_kv_up_dx")
    d_cq, d_g_q_a = _rmsnorm_bwd(z0a, e_g_q_a, d_cqn, width=MLA_Q_RANK, col_blk=4, out_dtype=BF16, name="l0_q_norm_bwd")
    d_ckv, d_g_kv_a = _rmsnorm_bwd(z0a, e_g_kv_a, d_ckvn, width=MLA_KV_RANK, col_blk=10, out_dtype=BF16,
                                   name="l0_kv_norm_bwd")
    dz0 = jnp.concatenate([d_gate0, d_cq, d_ckv, d_kpe, dq_s.astype(BF16), dk_s.astype(BF16), dv_s.astype(BF16)], axis=1)
    d_w0t = _matmul(dz0, h0, ta=True, name="l0_in_dw")
    recv0 = None
    if scatter0 is None:
        dh0 = _matmul(dz0, w0t, name="l0_in_dx")
    else:
        dh0, recv0 = _matmul(dz0, w0t, name="l0_in_dx", rider=("exchange", scatter0(dict(w0t=d_w0t, wq=d_wq, wkv=d_wkv))))
    grad_x, d_e_g_in = _rmsnorm_bwd(x, e_g_in, dh0, width=D_MODEL, col_blk=0, add=dx1, name="l0_norm_bwd")

    return dict(recv0=recv0, recv1=recv1, loss=loss_part[0, 0], grad_x=grad_x, e_g_in=d_e_g_in, w0t=d_w0t, e_g_q_a=d_g_q_a, wq=d_wq,
                e_g_kv_a=d_g_kv_a, wkv=d_wkv, e_sinks=d_sinks[:, 0].reshape(1, SWA_HEADS), wo0=d_wo0,
                o_g_in=d_o_g_in, w1t=d_w1t, wft=d_wft, o_b_f=d_bf[:, :FOX_HEADS], wo1=d_wo1, g_final=d_g_final.reshape(D_MODEL))


def _wide(a, rows):
    flat = a.reshape(-1)
    return jnp.pad(flat, (0, rows * WIDE - flat.shape[0])).reshape(rows, WIDE)


def _rows_b0(w_q, w_kv):
    return jnp.concatenate([_wide(w_q, 32), _wide(w_kv, 16)], axis=0)


def _unflat_b0(f):
    return f[0:24].reshape(1, MLA_Q_RANK, 96), f[32:48].reshape(1, MLA_KV_RANK, 128)


def _rows_b1(o_w_out, e_w_out, g_in):
    return jnp.concatenate([o_w_out, e_w_out, _wide(g_in, 16)], axis=0)


def _unflat_b1(f):
    return f[0:128][None], f[128:256][None], f[256:257, :LANES]


def kernel(x, positions, e_g_in, e_w_in, e_g_q_a, e_w_q_up, e_g_kv_a, e_w_kv_up, e_sinks, e_w_out, o_g_in, o_w_in, o_b_f, o_w_out, g_final, loss_target, m_e_g_in, m_e_w_in, m_e_g_q_a, m_e_w_q_up, m_e_g_kv_a, m_e_w_kv_up, m_e_sinks, m_e_w_out, m_o_g_in, m_o_w_in, m_o_b_f, m_o_w_out, m_g_final, v_e_g_in, v_e_w_in, v_e_g_q_a, v_e_w_q_up, v_e_g_kv_a, v_e_w_kv_up, v_e_sinks, v_e_w_out, v_o_g_in, v_o_w_in, v_o_b_f, v_o_w_out, v_g_final):
    def bf(a):
        return a.astype(BF16)

    shard0 = jnp.concatenate([_pad_rows(bf(e_w_in[0]).T, RA0), _rows_b0(bf(e_w_q_up[0]), bf(e_w_kv_up[0]))], axis=0)
    gath0 = _all_gather(shard0, name="weights0_all_gather")
    w0t = _layer0_in_weight_t(gath0[:, :N_E_IN].reshape(N_DEV * N_E_IN, WIDE))
    wq = _q_up_weight(_gathered_cols(gath0[:, RA0:RA0 + 24], MLA_Q_RANK))
    wkv = _kv_up_weight(_gathered_cols(gath0[:, RA0 + 32:RA0 + 48], MLA_KV_RANK))

    g_bits = lax.bitcast_convert_type(o_g_in.reshape(LANES), BF16)
    shard1 = jnp.concatenate([_pad_rows(bf(o_w_in[0]).T, RA1), _rows_b1(bf(o_w_out[0]), bf(e_w_out[0]), g_bits)], axis=0)

    def unpack1(gath1):
        w1t, wft = _layer1_in_weight_t(gath1[:, :N_O_IN].reshape(N_DEV * N_O_IN, WIDE))
        wo1 = gath1[:, RA1:RA1 + 128].reshape(D_MODEL, D_MODEL)
        wo0 = gath1[:, RA1 + 128:RA1 + 256].reshape(D_MODEL, D_MODEL)
        bits = gath1[:, RA1 + 256, :2 * LANES].reshape(N_DEV, LANES, 2)
        return wo0, lax.bitcast_convert_type(bits, F32).reshape(1, D_MODEL), w1t, wft, wo1

    def scatter1(g):
        d_in_t = _layer1_in_grad_t(g["w1t"], g["wft"]).reshape(N_DEV, N_O_IN, WIDE)
        d_o_g = jnp.pad(g["o_g_in"].reshape(N_DEV, 1, LANES), ((0, 0), (0, 15), (0, WIDE - LANES)))
        return jnp.concatenate([_pad_rows(d_in_t, RA1), g["wo1"].reshape(N_DEV, 128, WIDE),
                                g["wo0"].reshape(N_DEV, 128, WIDE), d_o_g], axis=1).astype(BF16)

    def scatter0(g):
        return jnp.concatenate([
            _pad_rows(_layer0_in_grad_t(g["w0t"]).reshape(N_DEV, N_E_IN, WIDE), RA0),
            _pad_rows(_scatter_cols(_q_up_grad(g["wq"])), 32), _scatter_cols(_kv_up_grad(g["wkv"]))], axis=1).astype(BF16)

    gr = _local_step(x[0], positions[0], loss_target[0], e_g_in, w0t, e_g_q_a, wq, e_g_kv_a, wkv, e_sinks,
                     (shard1, unpack1), o_b_f, g_final, scatter1=scatter1, scatter0=scatter0)
    recv0 = gr["recv0"]

    def in_projection(recv, ra, n, w, m, v, name):
        g = _sum8(recv, ra, name=name + "_grad_sum")[:n].T
        d, nm, nv = _adamw_native(g, w[0], m[0], v[0], name=name + "_adamw")
        return g[None], d[None], nm[None], nv[None]

    e_in = in_projection(recv0, RA0, N_E_IN, e_w_in, m_e_w_in, v_e_w_in, "e_w_in")
    o_in = in_projection(gr["recv1"], RA1, N_O_IN, o_w_in, m_o_w_in, v_o_w_in, "o_w_in")
    b0 = _adamw(recv0[:, RA0:], _rows_b0(e_w_q_up[0], e_w_kv_up[0]), _rows_b0(m_e_w_q_up[0], m_e_w_kv_up[0]),
                _rows_b0(v_e_w_q_up[0], v_e_w_kv_up[0]), name="adamw_early")
    b1 = _adamw(gr["recv1"][:, RA1:], _rows_b1(o_w_out[0], e_w_out[0], o_g_in),
                _rows_b1(m_o_w_out[0], m_e_w_out[0], m_o_g_in), _rows_b1(v_o_w_out[0], v_e_w_out[0], v_o_g_in),
                name="adamw_late")

    def sharded(k):
        q_up, kv_up = _unflat_b0(b0[k])
        o_out, e_out, o_g = _unflat_b1(b1[k])
        return e_in[k], q_up, kv_up, e_out, o_in[k], o_out, o_g

    g_sh, d_sh, m_sh, v_sh = [sharded(k) for k in range(4)]

    small = _small_pack(gr["e_g_in"], gr["g_final"], gr["e_g_q_a"], gr["e_g_kv_a"], gr["e_sinks"], gr["o_b_f"], gr["loss"])
    small_all = _all_gather(small, name="small_all_gather")
    zero = jnp.zeros((), F32)
    w_small = _small_pack(e_g_in, g_final, e_g_q_a, e_g_kv_a, e_sinks, o_b_f, zero)
    m_small = _small_pack(m_e_g_in, m_g_final, m_e_g_q_a, m_e_g_kv_a, m_e_sinks, m_o_b_f, zero)
    v_small = _small_pack(v_e_g_in, v_g_final, v_e_g_q_a, v_e_g_kv_a, v_e_sinks, v_o_b_f, zero)
    smalls = _adamw(small_all, w_small, m_small, v_small, name="adamw_replicated")
    g_sm, d_sm, m_sm, v_sm = [_small_unpack(a) for a in smalls]
    loss = g_sm[6]

    def leaves(sh, sm):
        return (sm[0], sh[0], sm[2], sh[1], sm[3], sh[2], sm[4], sh[3], sh[6], sh[4], sm[5], sh[5], sm[1])

    return (loss, gr["grad_x"][None], *leaves(g_sh, g_sm), *leaves(d_sh, d_sm), *leaves(m_sh, m_sm), *leaves(v_sh, v_sm))
```

```python
import functools

import jax
import jax.numpy as jnp
from jax import lax
from jax.experimental import pallas as pl
from jax.experimental.pallas import tpu as pltpu

F32 = jnp.float32
BF16 = jnp.bfloat16
NEG_INF = float("-inf")

N_DEV = 8
LANES = 128
D_MODEL = 1024
EPS = 1e-6
ROPE_THETA = 10000.0
MLA_HEADS = 8
MLA_Q_RANK = 256
MLA_KV_RANK = 128
MLA_NOPE = 64
MLA_ROPE = 32
MLA_V = 64
SWA_HEADS = 8
SWA_KV_HEADS = 2
SWA_DIM = 64
WINDOW = 128
FOX_HEADS = 16
FOX_DIM = 64

ADAM_LR = 0.001
ADAM_B1 = 0.9
ADAM_B2 = 0.999
ADAM_EPS = 1e-08
ADAM_WD = 0.01
ADAM_STEP = 10

ATT_T = 512
ATT_TK_BWD = 512
ATT_T_FWD = 1024
VMEM_LIMIT = 56 * 1024 * 1024
MATMUL_B_BLOCK_BYTES = 8 * 1024 * 1024

Z0A_UNITS = 12
Z0B_UNITS = 6

WIDE = 1024
N_E_IN = 276
N_O_IN = 514
RA0 = 288
RB0 = 32 + 16
RA1 = 528
RB1 = 128 + 128 + 16
SMALL_ROWS = 24


def _tile(n, cands):
    for c in cands:
        if n % c == 0:
            return c
    raise ValueError(f"no tile for {n}")


ROW_TILES = (512, 256, 128)


def _params(sem, vmem=VMEM_LIMIT):
    return pltpu.CompilerParams(dimension_semantics=sem, vmem_limit_bytes=vmem)


def _matmul(a, b, *, name, ta=False, tb=False, out_dtype=F32, b_rows=None):
    if ta:
        kdim, m = a.shape
    else:
        m, kdim = a.shape
    if tb:
        n, kb = b.shape
    else:
        kb, n = b.shape
    assert kdim == kb, (a.shape, b.shape)
    b_start = 0
    if b_rows is not None:
        assert tb
        b_start, n = b_rows
    tm = _tile(m, (512, 256, 128))
    tn = _tile(n, [c for c in (1024, 768, 512, 384, 256, 128)
                   if c * kdim * b.dtype.itemsize <= MATMUL_B_BLOCK_BYTES and b_start % c == 0])
    assert b_start % tn == 0, (b_start, tn)
    b_off = b_start // tn
    dims = (((0 if ta else 1,), (1 if tb else 0,)), ((), ()))

    def body(a_ref, b_ref, o_ref):
        r = lax.dot_general(a_ref[...].astype(BF16), b_ref[...].astype(BF16), dims, preferred_element_type=F32)
        o_ref[...] = r.astype(out_dtype)

    a_spec = pl.BlockSpec((kdim, tm), lambda i, j: (0, i)) if ta else pl.BlockSpec((tm, kdim), lambda i, j: (i, 0))
    b_spec = pl.BlockSpec((tn, kdim), lambda i, j: (j + b_off, 0)) if tb else pl.BlockSpec((kdim, tn), lambda i, j: (0, j))
    return pl.pallas_call(
        body, name=name, grid=(m // tm, n // tn), in_specs=[a_spec, b_spec],
        out_specs=pl.BlockSpec((tm, tn), lambda i, j: (i, j)), out_shape=jax.ShapeDtypeStruct((m, n), out_dtype),
        compiler_params=_params(("parallel", "parallel")),
    )(a, b)


def _rmsnorm_fwd(x, g, *, width, col_blk, name):
    s = x.shape[0]
    tm = _tile(s, ROW_TILES)

    def body(x_ref, g_ref, y_ref):
        xf = x_ref[...].astype(F32)
        r = lax.rsqrt(jnp.mean(xf * xf, axis=-1, keepdims=True) + EPS)
        y_ref[...] = ((xf * r) * g_ref[...]).astype(BF16)

    return pl.pallas_call(
        body, name=name, grid=(s // tm,),
        in_specs=[pl.BlockSpec((tm, width), lambda i: (i, col_blk)), pl.BlockSpec((1, width), lambda i: (0, 0))],
        out_specs=pl.BlockSpec((tm, width), lambda i: (i, 0)),
        out_shape=jax.ShapeDtypeStruct((s, width), BF16),
        compiler_params=_params(("parallel",)),
    )(x, g)


def _rmsnorm_bwd(x, g, dy, *, width, col_blk, name):
    s = x.shape[0]
    tm = _tile(s, ROW_TILES)

    def body(x_ref, g_ref, dy_ref, dx_ref, dg_ref):
        @pl.when(pl.program_id(0) == 0)
        def _():
            dg_ref[...] = jnp.zeros_like(dg_ref)

        dx, dg = _rms_bwd_epilogue(dy_ref[...], x_ref[...], 0.0, g_ref[...])
        dg_ref[...] += dg
        dx_ref[...] = dx.astype(BF16)

    return pl.pallas_call(
        body, name=name, grid=(s // tm,),
        in_specs=[pl.BlockSpec((tm, width), lambda i: (i, col_blk)), pl.BlockSpec((1, width), lambda i: (0, 0)),
                  pl.BlockSpec((tm, width), lambda i: (i, 0))],
        out_specs=[pl.BlockSpec((tm, width), lambda i: (i, 0)), pl.BlockSpec((1, width), lambda i: (0, 0))],
        out_shape=[jax.ShapeDtypeStruct((s, width), BF16), jax.ShapeDtypeStruct((1, width), F32)],
        compiler_params=_params(("arbitrary",)),
    )(x, g, dy)


def _sigmoid(x):
    return 1.0 / (1.0 + jnp.exp(-x))


def _gate_fwd(o_parts, gate, *, name):
    s = gate.shape[0]
    tm = _tile(s, ROW_TILES)
    n_o = len(o_parts)

    def body(*refs):
        o_refs, g_ref, y_ref = refs[:n_o], refs[n_o], refs[n_o + 1]
        o = o_refs[0][...] if n_o == 1 else jnp.concatenate([r[...] for r in o_refs], axis=1)
        gt = g_ref[...]
        y_ref[...] = (o * (gt * _sigmoid(gt))).astype(BF16)

    in_specs = [pl.BlockSpec((tm, o.shape[1]), lambda i: (i, 0)) for o in o_parts]
    in_specs.append(pl.BlockSpec((tm, D_MODEL), lambda i: (i, 0)))
    return pl.pallas_call(
        body, name=name, grid=(s // tm,), in_specs=in_specs,
        out_specs=pl.BlockSpec((tm, D_MODEL), lambda i: (i, 0)),
        out_shape=jax.ShapeDtypeStruct((s, D_MODEL), BF16),
        compiler_params=_params(("parallel",)),
    )(*o_parts, gate)


def _matmul_rows(terms, row_inputs, params, epilogue, outs, *, name, rider=None):
    s = terms[0][0].shape[0]
    tm = _tile(s, ROW_TILES)
    steps = s // tm
    n_t, n_r, n_p, n_o = len(terms), len(row_inputs), len(params), len(outs)

    def body(*refs):
        refs, ride_refs = _split_rider(refs, rider, n_in=2 * n_t + n_r + n_p, n_out=n_o)
        t_refs, r_refs = refs[:2 * n_t], refs[2 * n_t:2 * n_t + n_r]
        p_refs, o_refs = refs[2 * n_t + n_r:2 * n_t + n_r + n_p], refs[2 * n_t + n_r + n_p:]
        i = pl.program_id(0)
        _ride_start(rider, ride_refs, i == 0)
        acc = None
        for k, (_, _, tb) in enumerate(terms):
            dims = (((1,), (1 if tb else 0,)), ((), ()))
            part = lax.dot_general(t_refs[2 * k][...].astype(BF16), t_refs[2 * k + 1][...].astype(BF16), dims,
                                   preferred_element_type=F32)
            acc = part if acc is None else acc + part
        vals = epilogue(acc, *[r[...] for r in r_refs], *[p[...] for p in p_refs])
        for ref, val, out in zip(o_refs, vals, outs):
            if out[0] == "rows":
                ref[...] = val.astype(ref.dtype)
            else:
                @pl.when(i == 0)
                def _(ref=ref):
                    ref[...] = jnp.zeros_like(ref)

                ref[...] += val
        _ride_wait(rider, ride_refs, i == steps - 1)

    in_specs, args = [], []
    for a, b, _ in terms:
        in_specs += [pl.BlockSpec((tm, a.shape[1]), lambda i: (i, 0)), _resident(b.shape, lambda i: (0, 0))]
        args += [a, b]
    for arr, width, col_blk in row_inputs:
        in_specs.append(pl.BlockSpec((tm, width), lambda i, col_blk=col_blk: (i, col_blk)))
        args.append(arr)
    for p in params:
        in_specs.append(pl.BlockSpec(p.shape, lambda i: (0, 0)))
        args.append(p)
    out_specs, out_shape = [], []
    for out in outs:
        if out[0] == "rows":
            out_specs.append(pl.BlockSpec((tm, out[1]), lambda i: (i, 0)))
            out_shape.append(jax.ShapeDtypeStruct((s, out[1]), out[2]))
        else:
            out_specs.append(pl.BlockSpec(out[1], lambda i: (0, 0)))
            out_shape.append(jax.ShapeDtypeStruct(out[1], F32))
    scratch = _add_rider(rider, in_specs, args, out_specs, out_shape)
    return pl.pallas_call(
        body, name=name, grid=(steps,), in_specs=in_specs, out_specs=out_specs, out_shape=out_shape,
        scratch_shapes=scratch, compiler_params=_params(("arbitrary",)),
    )(*args)


def _rms_stats(x):
    r = lax.rsqrt(jnp.mean(x * x, axis=-1, keepdims=True) + EPS)
    return r, x * r


def _residual_norm_epilogue(r, x, g):
    x1 = x + r
    _, xh = _rms_stats(x1)
    return x1, xh * g


def _rms_bwd_epilogue(dy, x, add, g):
    r, xh = _rms_stats(x)
    dxh = dy * g
    dx = r * (dxh - xh * jnp.mean(dxh * xh, axis=-1, keepdims=True)) + add
    return dx, jnp.sum(dy * xh, axis=0, keepdims=True)


def _loss_epilogue(r, x1, target, g):
    rs, xh = _rms_stats(x1 + r)
    err = xh * g - target
    loss = jnp.broadcast_to(0.5 * jnp.sum(jnp.mean(err * err, axis=-1, keepdims=True)), (8, LANES))
    dy = err * (1.0 / D_MODEL)
    dxh = dy * g
    dx = rs * (dxh - xh * jnp.mean(dxh * xh, axis=-1, keepdims=True))
    return dx, loss, jnp.sum(dy * xh, axis=0, keepdims=True)


def _gate_bwd_epilogue(widths):
    def epilogue(d, *rows):
        o_parts, gt = rows[:-1], rows[-1]
        o = o_parts[0] if len(o_parts) == 1 else jnp.concatenate(o_parts, axis=1)
        sg = _sigmoid(gt)
        do = d * (gt * sg)
        d_gate = d * o * (sg * (1.0 + gt * (1.0 - sg)))
        cuts = [sum(widths[:k]) for k in range(len(widths) + 1)]
        return tuple(do[:, cuts[k]:cuts[k + 1]] for k in range(len(widths))) + (d_gate,)

    return epilogue


def _rot_half(x):
    lane = lax.broadcasted_iota(jnp.int32, x.shape, 1)
    return jnp.where(lane < 80, pltpu.roll(x, LANES - 16, axis=1), pltpu.roll(x, 16, axis=1))


def _rot_half_t(g):
    lane = lax.broadcasted_iota(jnp.int32, g.shape, 1)
    lo = (lane >= MLA_NOPE) & (lane < MLA_NOPE + MLA_ROPE // 2)
    hi = (lane >= MLA_NOPE + MLA_ROPE // 2) & (lane < MLA_NOPE + MLA_ROPE)
    return jnp.where(lo, pltpu.roll(g, LANES - 16, axis=1), jnp.where(hi, pltpu.roll(g, 16, axis=1), 0.0))


def _rope_fwd(qp, kvp, z0a, cos_t, sin_t, *, name):
    s = qp.shape[0]
    tm = _tile(s, ROW_TILES)
    hw = MLA_HEADS * LANES

    def body(q_ref, k_ref, kpe_ref, c_ref, s_ref, qm_ref, km_ref):
        c = c_ref[...]
        sn = s_ref[...]
        kpe = kpe_ref[...]
        kpe_r = (kpe * c + _rot_half(kpe) * sn).astype(BF16)
        lane = lax.broadcasted_iota(jnp.int32, kpe.shape, 1)
        for h in range(MLA_HEADS):
            sl = slice(h * LANES, (h + 1) * LANES)
            qh = q_ref[:, sl]
            qm_ref[:, sl] = (qh * c + _rot_half(qh) * sn).astype(BF16)
            km_ref[:, sl] = jnp.where(lane < MLA_NOPE, k_ref[:, sl], kpe_r)

    return pl.pallas_call(
        body, name=name, grid=(s // tm,),
        in_specs=[pl.BlockSpec((tm, hw), lambda i: (i, 0)), pl.BlockSpec((tm, hw), lambda i: (i, 0)),
                  pl.BlockSpec((tm, LANES), lambda i: (i, 11)),
                  pl.BlockSpec((tm, LANES), lambda i: (i, 0)), pl.BlockSpec((tm, LANES), lambda i: (i, 0))],
        out_specs=[pl.BlockSpec((tm, hw), lambda i: (i, 0)), pl.BlockSpec((tm, hw), lambda i: (i, 0))],
        out_shape=[jax.ShapeDtypeStruct((s, hw), BF16), jax.ShapeDtypeStruct((s, hw), BF16)],
        compiler_params=_params(("parallel",)),
    )(qp, kvp, z0a, cos_t, sin_t)


def _rope_bwd(dqm, dkm, dvm, cos_t, sin_t, *, name):
    s = dqm.shape[0]
    tm = _tile(s, ROW_TILES)
    hw = MLA_HEADS * LANES
    vw = MLA_HEADS * MLA_V

    def body(dq_ref, dk_ref, dv_ref, c_ref, s_ref, dqp_ref, dkv_ref, dkpe_ref):
        c = c_ref[...]
        sn = s_ref[...]
        ksum = jnp.zeros((tm, LANES), F32)
        for h in range(MLA_HEADS):
            sl = slice(h * LANES, (h + 1) * LANES)
            dq = dq_ref[:, sl]
            dqp_ref[:, sl] = (dq * c + _rot_half_t(dq * sn)).astype(BF16)
            dk = dk_ref[:, sl]
            dkv_ref[:, sl] = dk.astype(BF16)
            ksum = ksum + dk
        dkv_ref[:, hw:] = dv_ref[...]
        lane = lax.broadcasted_iota(jnp.int32, ksum.shape, 1)
        dkpe = ksum * c + _rot_half_t(ksum * sn)
        dkpe_ref[...] = jnp.where((lane >= MLA_NOPE) & (lane < MLA_NOPE + MLA_ROPE), dkpe, 0.0).astype(BF16)

    return pl.pallas_call(
        body, name=name, grid=(s // tm,),
        in_specs=[pl.BlockSpec((tm, hw), lambda i: (i, 0)), pl.BlockSpec((tm, hw), lambda i: (i, 0)),
                  pl.BlockSpec((tm, vw), lambda i: (i, 0)),
                  pl.BlockSpec((tm, LANES), lambda i: (i, 0)), pl.BlockSpec((tm, LANES), lambda i: (i, 0))],
        out_specs=[pl.BlockSpec((tm, hw), lambda i: (i, 0)), pl.BlockSpec((tm, hw + vw), lambda i: (i, 0)),
                   pl.BlockSpec((tm, LANES), lambda i: (i, 0))],
        out_shape=[jax.ShapeDtypeStruct((s, hw), BF16), jax.ShapeDtypeStruct((s, hw + vw), BF16),
                   jax.ShapeDtypeStruct((s, LANES), BF16)],
        compiler_params=_params(("parallel",)),
    )(dqm, dkm, dvm, cos_t, sin_t)


def _head_mask(shape, a):
    lane = lax.broadcasted_iota(jnp.int32, shape, 1)
    return (lane >= 64 * a) & (lane < 64 * (a + 1))


_NT = (((1,), (1,)), ((), ()))
LOG2E = 1.4426950408889634


def _stack_heads(tile, hw):
    lane = lax.broadcasted_iota(jnp.int32, tile.shape, 1)
    z = jnp.zeros_like(tile)
    return jnp.concatenate([jnp.where(lane < hw, tile, z), jnp.where(lane >= hw, tile, z)], axis=0)


def _stacked_rows(r0, r1, t):
    n = r0.shape[-1]
    return jnp.concatenate([jnp.broadcast_to(r0, (t, n)), jnp.broadcast_to(r1, (t, n))], axis=0)


def _resident(block, index_map):
    return pl.BlockSpec(block, index_map, pipeline_mode=pl.Buffered(1))


def _fwd_tile(s):
    return ATT_T_FWD if s % ATT_T_FWD == 0 else min(ATT_T, s)


def _flash_fwd(q, k, v, bias, *, n_pairs, hw, q_off, k_off, v_off, scale, name, rider=None):
    s = q.shape[0]
    t = _fwd_tile(s)
    nb = s // t
    qw = 2 * hw
    has_bias = bias is not None
    c1 = scale * LOG2E

    def body(*refs):
        refs, ride_refs = _split_rider(refs, rider, n_in=4 if has_bias else 3, n_out=2)
        if has_bias:
            q_ref, k_ref, v_ref, b_ref, o_ref, lse_ref, vt_ref, bcol_ref = refs
        else:
            q_ref, k_ref, v_ref, o_ref, lse_ref, vt_ref = refs
            b_ref = bcol_ref = None
        _ride_start(rider, ride_refs, pl.program_id(0) == 0)
        row = lax.broadcasted_iota(jnp.int32, (t, t), 0)
        col = lax.broadcasted_iota(jnp.int32, (t, t), 1)
        cmask_t = jnp.concatenate([row <= col, row <= col], axis=1)
        lane_lt64 = lax.broadcasted_iota(jnp.int32, (t, LANES), 1) < 64

        def as_column(r):
            return jnp.broadcast_to(r, (8, r.shape[1])).T[:, 0:1]

        def v_block(j, _):
            c0 = pl.multiple_of(j * t, t)
            vt_ref[j] = v_ref[pl.ds(c0, t), :].astype(F32).T.astype(BF16)
            if has_bias:
                for a in range(2):
                    bcol_ref[a, pl.ds(c0, t), :] = as_column(b_ref[0, a, j])
            return 0

        lax.fori_loop(0, nb, v_block, 0)

        def stacked_queries(i):
            return _stack_heads(q_ref[pl.ds(pl.multiple_of(i * t, t), t), :], hw).astype(F32).T.astype(BF16)

        def kv_step(j, carry, qs_t, masked):
            m, l, acc = carry
            rows = pl.ds(pl.multiple_of(j * t, t), t)
            sc = jnp.dot(k_ref[rows, :], qs_t, preferred_element_type=F32) * c1
            if has_bias:
                sc = sc + jnp.concatenate([jnp.broadcast_to(bcol_ref[0, rows, :], (t, t)),
                                           jnp.broadcast_to(bcol_ref[1, rows, :], (t, t))], axis=1)
            if masked:
                sc = jnp.where(cmask_t, sc, NEG_INF)
            m_new = jnp.maximum(m, jnp.max(sc, axis=0, keepdims=True))
            alpha = jnp.exp2(m - m_new)
            p = jnp.exp2(sc - m_new)
            l_new = alpha * l + jnp.sum(p, axis=0, keepdims=True)
            pv = jnp.dot(vt_ref[j], p.astype(BF16), preferred_element_type=F32)
            return m_new, l_new, alpha * acc + pv

        def finish(i, carry):
            m, l, acc = carry
            r0 = pl.multiple_of(i * t, t)
            out = (acc / l).T
            lse2 = as_column(m + jnp.log2(l))
            lse_ref[0, 0, pl.ds(r0, t), :] = lse2[:t]
            lse_ref[0, 1, pl.ds(r0, t), :] = lse2[t:]
            o_ref[pl.ds(r0, t), :] = jnp.where(lane_lt64, out[:t], out[t:])

        init = (jnp.full((1, 2 * t), NEG_INF, F32), jnp.zeros((1, 2 * t), F32), jnp.zeros((LANES, 2 * t), F32))

        def q_block(i, _):
            qs_t = stacked_queries(i)
            carry = lax.fori_loop(0, i, lambda j, c: kv_step(j, c, qs_t, False), init)
            finish(i, kv_step(i, carry, qs_t, True))
            return 0

        lax.fori_loop(0, nb, q_block, 0)
        _ride_wait(rider, ride_refs, pl.program_id(0) == n_pairs - 1)

    in_specs = [_resident((s, qw), lambda p: (0, q_off + p)), _resident((s, qw), lambda p: (0, k_off + p)),
                _resident((s, LANES), lambda p: (0, v_off + p))]
    args = [q, k, v]
    if has_bias:
        in_specs.append(_resident((1, 2, nb, 1, t), lambda p: (p, 0, 0, 0, 0)))
        args.append(bias)
    out_specs = [pl.BlockSpec((s, LANES), lambda p: (0, p)), pl.BlockSpec((1, 2, s, 1), lambda p: (p, 0, 0, 0))]
    out_shape = [jax.ShapeDtypeStruct((s, n_pairs * LANES), F32), jax.ShapeDtypeStruct((n_pairs, 2, s, 1), F32)]
    scratch = [pltpu.VMEM((nb, LANES, t), BF16)] + ([pltpu.VMEM((2, s, 1), F32)] if has_bias else [])
    scratch += _add_rider(rider, in_specs, args, out_specs, out_shape)
    return pl.pallas_call(
        body, name=name, grid=(n_pairs,), in_specs=in_specs, out_specs=out_specs, out_shape=out_shape,
        scratch_shapes=scratch,
        compiler_params=_params(("parallel",) if rider is None else ("arbitrary",)),
    )(*args)


def _bwd_key_tile(s):
    return ATT_TK_BWD if s % ATT_TK_BWD == 0 else min(ATT_T, s)


def _flash_bwd(q, k, v, do, o, lse, bias, *, n_pairs, hw, q_off, k_off, v_off, scale, qk_dtype, name, rider=None):
    s = q.shape[0]
    t = min(ATT_T, s)
    nb = s // t
    tk = _bwd_key_tile(s)
    nbk = s // tk
    qw = 2 * hw
    has_bias = bias is not None
    c1 = scale * LOG2E

    def body(*refs):
        refs, ride_refs = _split_rider(refs, rider, n_in=7 if has_bias else 6, n_out=5 if has_bias else 3)
        if has_bias:
            (q_ref, k_ref, v_ref, do_ref, o_ref, lse_ref, b_ref, dq_ref, dk_ref, dv_ref, db_ref, dr_ref,
             dkt_ref, dvt_ref) = refs
            db_ref[...] = jnp.zeros_like(db_ref)
        else:
            q_ref, k_ref, v_ref, do_ref, o_ref, lse_ref, dq_ref, dk_ref, dv_ref, dkt_ref, dvt_ref = refs
            b_ref = db_ref = dr_ref = None
        _ride_start(rider, ride_refs, pl.program_id(0) == 0)
        dkt_ref[...] = jnp.zeros_like(dkt_ref)
        dvt_ref[...] = jnp.zeros_like(dvt_ref)
        q_in_tile = lax.broadcasted_iota(jnp.int32, (2 * t, tk), 0) % t
        k_in_tile = lax.broadcasted_iota(jnp.int32, (2 * t, tk), 1)
        lane_lt_hw = lax.broadcasted_iota(jnp.int32, (t, qw), 1) < hw

        def q_block(i, _):
            r0 = pl.multiple_of(i * t, t)
            qs = _stack_heads(q_ref[pl.ds(r0, t), :], hw)
            dos = _stack_heads(do_ref[pl.ds(r0, t), :], 64)
            ot = o_ref[pl.ds(r0, t), :]
            delta = jnp.sum(dos * jnp.concatenate([ot, ot], axis=0), axis=-1, keepdims=True)
            lse2 = jnp.concatenate([lse_ref[0, 0, pl.ds(r0, t), :], lse_ref[0, 1, pl.ds(r0, t), :]], axis=0)
            dosb = dos.astype(BF16)
            dos_t = dos.T.astype(BF16)
            qs_t = qs.astype(F32).T.astype(BF16)

            def kv_step(j, carry, masked):
                dq, rsum = carry
                c0 = pl.multiple_of(j * tk, tk)
                kt = k_ref[pl.ds(c0, tk), :]
                vt = v_ref[pl.ds(c0, tk), :]
                sc = lax.dot_general(qs, kt, _NT, preferred_element_type=F32) * c1
                if has_bias:
                    sc = sc + _stacked_rows(b_ref[0, 0, j], b_ref[0, 1, j], t)
                if masked:
                    sc = jnp.where(k_in_tile <= q_in_tile + (r0 - c0), sc, NEG_INF)
                p = jnp.exp2(sc - lse2)
                dp = lax.dot_general(dosb, vt, _NT, preferred_element_type=F32)
                ds = p * (dp - delta)
                dsb = ds.astype(BF16)
                pb = p.astype(BF16)
                if hw == LANES:
                    dvt_ref[j] += jnp.concatenate(
                        [jnp.dot(dos_t[:64, :t], pb[:t], preferred_element_type=F32),
                         jnp.dot(dos_t[64:, t:], pb[t:], preferred_element_type=F32)], axis=0)
                    dkt_ref[j] += jnp.concatenate(
                        [jnp.dot(qs_t[:hw, :t], dsb[:t], preferred_element_type=F32),
                         jnp.dot(qs_t[hw:, t:], dsb[t:], preferred_element_type=F32)], axis=0)
                else:
                    dvt_ref[j] += jnp.dot(dos_t, pb, preferred_element_type=F32)
                    dkt_ref[j] += jnp.dot(qs_t, dsb, preferred_element_type=F32)
                if has_bias:
                    db_ref[0, 0, j] += jnp.sum(ds[:t], axis=0, keepdims=True)
                    db_ref[0, 1, j] += jnp.sum(ds[t:], axis=0, keepdims=True)
                    rsum = rsum + jnp.sum(ds, axis=-1, keepdims=True)
                return dq + jnp.dot(dsb, kt, preferred_element_type=F32), rsum

            init = (jnp.zeros((2 * t, qw), F32), jnp.zeros((2 * t, 1), F32))
            diag = r0 // tk
            carry = lax.fori_loop(0, diag, functools.partial(kv_step, masked=False), init)
            dq, rsum = kv_step(diag, carry, True)
            dq = dq * scale
            dq_ref[pl.ds(r0, t), :] = jnp.where(lane_lt_hw, dq[:t], dq[t:]).astype(qk_dtype)
            if has_bias:
                rsum_row = jnp.broadcast_to(rsum, (2 * t, LANES)).T[0:1]
                dr_ref[0, 0, i] = rsum_row[:, :t]
                dr_ref[0, 1, i] = rsum_row[:, t:]
            return 0

        lax.fori_loop(0, nb, q_block, 0)

        def k_block(j, _):
            c0 = pl.multiple_of(j * tk, tk)
            dk_ref[pl.ds(c0, tk), :] = (dkt_ref[j].T * scale).astype(qk_dtype)
            dv_ref[pl.ds(c0, tk), :] = dvt_ref[j].T.astype(BF16)
            return 0

        lax.fori_loop(0, nbk, k_block, 0)
        _ride_wait(rider, ride_refs, pl.program_id(0) == n_pairs - 1)

    in_specs = [_resident((s, qw), lambda p: (0, q_off + p)), _resident((s, qw), lambda p: (0, k_off + p)),
                _resident((s, LANES), lambda p: (0, v_off + p)),
                _resident((s, LANES), lambda p: (0, p)), _resident((s, LANES), lambda p: (0, p)),
                _resident((1, 2, s, 1), lambda p: (p, 0, 0, 0))]
    args = [q, k, v, do, o, lse]
    out_specs = [pl.BlockSpec((s, qw), lambda p: (0, p)), pl.BlockSpec((s, qw), lambda p: (0, p)),
                 pl.BlockSpec((s, LANES), lambda p: (0, p))]
    out_shape = [jax.ShapeDtypeStruct((s, n_pairs * qw), qk_dtype), jax.ShapeDtypeStruct((s, n_pairs * qw), qk_dtype),
                 jax.ShapeDtypeStruct((s, n_pairs * LANES), BF16)]
    if has_bias:
        in_specs.append(_resident((1, 2, nbk, 1, tk), lambda p: (p, 0, 0, 0, 0)))
        args.append(bias)
        out_specs.append(pl.BlockSpec((1, 2, nbk, 1, tk), lambda p: (p, 0, 0, 0, 0)))
        out_shape.append(jax.ShapeDtypeStruct((n_pairs, 2, nbk, 1, tk), F32))
        out_specs.append(pl.BlockSpec((1, 2, nb, 1, t), lambda p: (p, 0, 0, 0, 0)))
        out_shape.append(jax.ShapeDtypeStruct((n_pairs, 2, nb, 1, t), F32))
    scratch = [pltpu.VMEM((nbk, qw, tk), F32), pltpu.VMEM((nbk, LANES, tk), F32)]
    scratch += _add_rider(rider, in_specs, args, out_specs, out_shape)
    return pl.pallas_call(
        body, name=name, grid=(n_pairs,), in_specs=in_specs, out_specs=out_specs, out_shape=out_shape,
        scratch_shapes=scratch,
        compiler_params=_params(("parallel",) if rider is None else ("arbitrary",)),
    )(*args)


def _alibi_slope(h):
    return 2.0 ** (-8.0 * (h + 1.0) / SWA_HEADS)


SWA_ROWS = 512
SWA_SCALE = SWA_DIM ** -0.5


def _swa_geometry(i):
    w = WINDOW
    r0 = pl.multiple_of(i * w, w)
    b0 = pl.multiple_of(jnp.maximum(i - 1, 0) * w, w)
    row = lax.broadcasted_iota(jnp.int32, (w, 2 * w), 0)
    col = lax.broadcasted_iota(jnp.int32, (w, 2 * w), 1)
    dist = row - col + (r0 - b0)
    valid = (dist >= 0) & (dist < w)
    return r0, b0, dist.astype(F32), valid


def _swa_q_head(qblk, h):
    kv = h // (SWA_HEADS // SWA_KV_HEADS)
    if h % 2 != kv:
        qblk = pltpu.roll(qblk, 64, axis=1)
    return jnp.where(_head_mask(qblk.shape, kv), qblk, 0.0)


SWA_GROUP = SWA_HEADS // SWA_KV_HEADS


def _swa_stack(ref, rs, grp):
    parts = []
    for a in range(SWA_GROUP):
        h = SWA_GROUP * grp + a
        parts.append(_swa_q_head(ref[rs, (h // 2) * LANES:(h // 2 + 1) * LANES].astype(F32), h))
    return jnp.concatenate(parts, axis=0)


def _swa_unstack(x, grp):
    tiles = []
    for a in range(SWA_GROUP):
        h = SWA_GROUP * grp + a
        tile = x[a * WINDOW:(a + 1) * WINDOW]
        tiles.append(pltpu.roll(tile, 64, axis=1) if h % 2 != grp else tile)
    return tiles


def _swa_head_column(vals):
    return jnp.concatenate([jnp.full((WINDOW, 1), v, F32) for v in vals], axis=0)


def _swa_logits(qs, kb, dist, valid, grp):
    slopes = _swa_head_column([_alibi_slope(SWA_GROUP * grp + a) for a in range(SWA_GROUP)])
    dist4 = jnp.concatenate([dist] * SWA_GROUP, axis=0)
    valid4 = jnp.concatenate([valid] * SWA_GROUP, axis=0)
    sc = lax.dot_general(qs, kb, _NT, preferred_element_type=F32) * SWA_SCALE - slopes * dist4
    return jnp.where(valid4, sc, NEG_INF)


def _swa_merge_heads(tiles):
    lt64 = lax.broadcasted_iota(jnp.int32, (WINDOW, LANES), 1) < 64
    return jnp.concatenate([jnp.where(lt64, tiles[2 * b], tiles[2 * b + 1]) for b in range(SWA_HEADS // 2)], axis=1)


def _swa_fwd(z0b, sinks, *, name):
    s = z0b.shape[0]
    w = WINDOW
    rows = min(SWA_ROWS, s)
    per_step = rows // w
    qcols = SWA_HEADS * SWA_DIM

    def body(sink_ref, q_ref, k_ref, v_ref, o_ref, lse_ref):
        g = pl.program_id(0)
        for ii in range(per_step):
            rs = slice(ii * w, (ii + 1) * w)
            r0, b0, dist, valid = _swa_geometry(g * per_step + ii)
            kb = k_ref[pl.ds(b0, 2 * w), :]
            vb = v_ref[pl.ds(b0, 2 * w), :]
            o_tiles = []
            for h in range(SWA_HEADS):
                kv = h // SWA_GROUP
                qh = _swa_q_head(q_ref[rs, (h // 2) * LANES:(h // 2 + 1) * LANES].astype(F32), h).astype(BF16)
                sc = lax.dot_general(qh, kb, _NT, preferred_element_type=F32) * SWA_SCALE - _alibi_slope(h) * dist
                sc = jnp.where(valid, sc, NEG_INF)
                sink = sink_ref[0, h]
                m = jnp.maximum(jnp.max(sc, axis=-1, keepdims=True), sink)
                p = jnp.exp(sc - m)
                l = jnp.sum(p, axis=-1, keepdims=True) + jnp.exp(sink - m)
                oh = jnp.dot(p.astype(BF16), vb, preferred_element_type=F32) / l
                o_tiles.append(pltpu.roll(oh, 64, axis=1) if h % 2 != kv else oh)
                lse_ref[h, rs, :] = m + jnp.log(l)
            o_ref[rs, :] = _swa_merge_heads(o_tiles)

    return pl.pallas_call(
        body, name=name, grid=(s // rows,),
        in_specs=[pl.BlockSpec(memory_space=pltpu.SMEM),
                  pl.BlockSpec((rows, qcols), lambda g: (g, 0)),
                  pl.BlockSpec((s, LANES), lambda g: (0, 4)), pl.BlockSpec((s, LANES), lambda g: (0, 5))],
        out_specs=[pl.BlockSpec((rows, qcols), lambda g: (g, 0)), pl.BlockSpec((SWA_HEADS, rows, 1), lambda g: (0, g, 0))],
        out_shape=[jax.ShapeDtypeStruct((s, qcols), F32), jax.ShapeDtypeStruct((SWA_HEADS, s, 1), F32)],
        compiler_params=_params(("parallel",)),
    )(sinks, z0b, z0b, z0b)


def _swa_bwd(z0b, sinks, do, o, lse, *, name):
    s = z0b.shape[0]
    w = WINDOW
    rows = min(SWA_ROWS, s)
    per_step = rows // w
    qcols = SWA_HEADS * SWA_DIM
    nblk = s // w

    def body(sink_ref, q_ref, k_ref, v_ref, do_ref, o_ref, lse_ref, dq_ref, dkt_ref, dvt_ref, dsink_ref):
        g = pl.program_id(0)

        @pl.when(g == 0)
        def _():
            dkt_ref[...] = jnp.zeros_like(dkt_ref)
            dvt_ref[...] = jnp.zeros_like(dvt_ref)
            dsink_ref[...] = jnp.zeros_like(dsink_ref)

        for ii in range(per_step):
            i = g * per_step + ii
            rs = slice(ii * w, (ii + 1) * w)
            r0, b0, dist, valid = _swa_geometry(i)
            j0 = jnp.maximum(i - 1, 0)
            kb = k_ref[pl.ds(b0, 2 * w), :]
            vb = v_ref[pl.ds(b0, 2 * w), :]
            dq_tiles = []
            for grp in range(SWA_KV_HEADS):
                heads = [SWA_GROUP * grp + a for a in range(SWA_GROUP)]
                qs32 = _swa_stack(q_ref, rs, grp)
                dos32 = _swa_stack(do_ref, rs, grp)
                delta = jnp.sum(dos32 * _swa_stack(o_ref, rs, grp), axis=-1, keepdims=True)
                lse = jnp.concatenate([lse_ref[h, rs, :] for h in heads], axis=0)
                sink = _swa_head_column([sink_ref[0, h] for h in heads])
                p = jnp.exp(_swa_logits(qs32.astype(BF16), kb, dist, valid, grp) - lse)
                dp = lax.dot_general(dos32.astype(BF16), vb, _NT, preferred_element_type=F32)
                ds = p * (dp - delta)
                dsb = ds.astype(BF16)
                d_sink = jnp.exp(sink - lse) * delta
                for a, h in enumerate(heads):
                    dsink_ref[h:h + 1, :] += jnp.broadcast_to(-jnp.sum(d_sink[a * w:(a + 1) * w]), (1, LANES))
                dvt = jnp.dot(dos32.T.astype(BF16), p.astype(BF16), preferred_element_type=F32)
                dkt = jnp.dot(qs32.T.astype(BF16), dsb, preferred_element_type=F32) * SWA_SCALE
                dvt_ref[j0] += dvt[:, :w]
                dvt_ref[j0 + 1] += dvt[:, w:]
                dkt_ref[j0] += dkt[:, :w]
                dkt_ref[j0 + 1] += dkt[:, w:]
                dq_tiles += _swa_unstack(jnp.dot(dsb, kb, preferred_element_type=F32) * SWA_SCALE, grp)
            dq_ref[rs, :] = _swa_merge_heads(dq_tiles)

    return pl.pallas_call(
        body, name=name, grid=(s // rows,),
        in_specs=[pl.BlockSpec(memory_space=pltpu.SMEM),
                  pl.BlockSpec((rows, qcols), lambda g: (g, 0)),
                  pl.BlockSpec((s, LANES), lambda g: (0, 4)), pl.BlockSpec((s, LANES), lambda g: (0, 5)),
                  pl.BlockSpec((rows, qcols), lambda g: (g, 0)), pl.BlockSpec((rows, qcols), lambda g: (g, 0)),
                  pl.BlockSpec((SWA_HEADS, rows, 1), lambda g: (0, g, 0))],
        out_specs=[pl.BlockSpec((rows, qcols), lambda g: (g, 0)),
                   pl.BlockSpec((nblk, LANES, w), lambda g: (0, 0, 0)),
                   pl.BlockSpec((nblk, LANES, w), lambda g: (0, 0, 0)),
                   pl.BlockSpec((SWA_HEADS, LANES), lambda g: (0, 0))],
        out_shape=[jax.ShapeDtypeStruct((s, qcols), F32),
                   jax.ShapeDtypeStruct((nblk, LANES, w), F32), jax.ShapeDtypeStruct((nblk, LANES, w), F32),
                   jax.ShapeDtypeStruct((SWA_HEADS, LANES), F32)],
        compiler_params=_params(("arbitrary",)),
    )(sinks, z0b, z0b, z0b, do, o, lse)


CUM_T = 256


def _split3(x):
    hi = x.astype(BF16)
    r1 = x - hi.astype(F32)
    mid = r1.astype(BF16)
    lo = (r1 - mid.astype(F32)).astype(BF16)
    return hi, mid, lo


def _tri_dot(tri, x):
    hi, mid, lo = _split3(x)
    out = jnp.dot(tri, hi, preferred_element_type=F32)
    out = out + jnp.dot(tri, mid, preferred_element_type=F32)
    return out + jnp.dot(tri, lo, preferred_element_type=F32)


def _logf_fwd(zf, bf, *, name):
    s = zf.shape[0]
    t = CUM_T
    nb = s // t

    def body(z_ref, b_ref, c_ref, carry_ref):
        i = pl.program_id(0)

        @pl.when(i == 0)
        def _():
            carry_ref[...] = jnp.zeros_like(carry_ref)

        x = z_ref[...] + b_ref[...]
        lf = jnp.minimum(x, 0.0) - jnp.log(1.0 + jnp.exp(-jnp.abs(x)))
        row = lax.broadcasted_iota(jnp.int32, (t, t), 0)
        col = lax.broadcasted_iota(jnp.int32, (t, t), 1)
        tri = jnp.where(col <= row, 1.0, 0.0).astype(BF16)
        c = _tri_dot(tri, lf) + carry_ref[...]
        c_ref[...] = c
        carry_ref[...] = c[t - 1:t, :]

    return pl.pallas_call(
        body, name=name, grid=(nb,),
        in_specs=[pl.BlockSpec((t, LANES), lambda i: (i, 0)), pl.BlockSpec((1, LANES), lambda i: (0, 0))],
        out_specs=pl.BlockSpec((t, LANES), lambda i: (i, 0)),
        out_shape=jax.ShapeDtypeStruct((s, LANES), F32),
        scratch_shapes=[pltpu.VMEM((1, LANES), F32)],
        compiler_params=_params(("arbitrary",)),
    )(zf, bf)


def _logf_bwd(dc, zf, bf, *, name):
    s = zf.shape[0]
    t = CUM_T
    nb = s // t

    def body(dc_ref, z_ref, b_ref, dz_ref, db_ref, carry_ref):
        i = pl.program_id(0)

        @pl.when(i == 0)
        def _():
            carry_ref[...] = jnp.zeros_like(carry_ref)
            db_ref[...] = jnp.zeros_like(db_ref)

        row = lax.broadcasted_iota(jnp.int32, (t, t), 0)
        col = lax.broadcasted_iota(jnp.int32, (t, t), 1)
        tri = jnp.where(col >= row, 1.0, 0.0).astype(BF16)
        dlf = _tri_dot(tri, dc_ref[...]) + carry_ref[...]
        carry_ref[...] = dlf[0:1, :]
        x = z_ref[...] + b_ref[...]
        dz = dlf * _sigmoid(-x)
        dz_ref[...] = dz.astype(BF16)
        db_ref[...] += jnp.sum(dz, axis=0, keepdims=True)

    return pl.pallas_call(
        body, name=name, grid=(nb,),
        in_specs=[pl.BlockSpec((t, LANES), lambda i: (nb - 1 - i, 0)), pl.BlockSpec((t, LANES), lambda i: (nb - 1 - i, 0)),
                  pl.BlockSpec((1, LANES), lambda i: (0, 0))],
        out_specs=[pl.BlockSpec((t, LANES), lambda i: (nb - 1 - i, 0)), pl.BlockSpec((1, LANES), lambda i: (0, 0))],
        out_shape=[jax.ShapeDtypeStruct((s, LANES), BF16), jax.ShapeDtypeStruct((1, LANES), F32)],
        scratch_shapes=[pltpu.VMEM((1, LANES), F32)],
        compiler_params=_params(("arbitrary",)),
    )(dc, zf, bf)


def _sum_pieces(p_ref):
    g = p_ref[0].astype(F32)
    for k in range(1, N_DEV):
        g = g + p_ref[k].astype(F32)
    return g


def _adam_update(g, w, m, v):
    bc1 = 1.0 - ADAM_B1 ** ADAM_STEP
    bc2 = 1.0 - ADAM_B2 ** ADAM_STEP
    nm = ADAM_B1 * m + (1.0 - ADAM_B1) * g
    nv = ADAM_B2 * v + (1.0 - ADAM_B2) * (g * g)
    m_hat = nm / bc1
    v_hat = nv / bc2
    return -ADAM_LR * (m_hat / (jnp.sqrt(v_hat) + ADAM_EPS) + ADAM_WD * w), nm, nv


def _adamw(pieces, w, m, v, *, name):
    rows, cols = w.shape
    tr = _tile(rows, (RB1, RB0, SMALL_ROWS))

    def body(p_ref, w_ref, m_ref, v_ref, g_ref, d_ref, nm_ref, nv_ref):
        g = _sum_pieces(p_ref)
        g_ref[...] = g
        d_ref[...], nm_ref[...], nv_ref[...] = _adam_update(g, w_ref[...], m_ref[...], v_ref[...])

    spec = pl.BlockSpec((tr, cols), lambda i: (i, 0))
    shape = jax.ShapeDtypeStruct((rows, cols), F32)
    return pl.pallas_call(
        body, name=name, grid=(rows // tr,),
        in_specs=[pl.BlockSpec((N_DEV, tr, cols), lambda i: (0, i, 0)), spec, spec, spec],
        out_specs=[spec, spec, spec, spec], out_shape=[shape, shape, shape, shape],
        compiler_params=_params(("parallel",)),
    )(pieces, w, m, v)


def _sum8(pieces, rows, *, name):
    cols = pieces.shape[2]
    tr = _tile(rows, (176, 96))

    def body(p_ref, g_ref):
        g_ref[...] = _sum_pieces(p_ref)

    return pl.pallas_call(
        body, name=name, grid=(rows // tr,),
        in_specs=[pl.BlockSpec((N_DEV, tr, cols), lambda i: (0, i, 0))],
        out_specs=pl.BlockSpec((tr, cols), lambda i: (i, 0)),
        out_shape=jax.ShapeDtypeStruct((rows, cols), F32),
        compiler_params=_params(("parallel",)),
    )(pieces)


def _adamw_native(g, w, m, v, *, name):
    rows, cols = w.shape
    tr = _tile(rows, (256, 128))

    def body(g_ref, w_ref, m_ref, v_ref, d_ref, nm_ref, nv_ref):
        d_ref[...], nm_ref[...], nv_ref[...] = _adam_update(g_ref[...], w_ref[...], m_ref[...], v_ref[...])

    spec = pl.BlockSpec((tr, cols), lambda i: (i, 0))
    shape = jax.ShapeDtypeStruct((rows, cols), F32)
    return pl.pallas_call(
        body, name=name, grid=(rows // tr,), in_specs=[spec, spec, spec, spec],
        out_specs=[spec, spec, spec], out_shape=[shape, shape, shape],
        compiler_params=_params(("parallel",)),
    )(g, w, m, v)


MESH = pl.DeviceIdType.MESH
ANY = pl.BlockSpec(memory_space=pl.ANY)


def _all_gather(shard, *, name):
    rows, lanes = shard.shape

    def body(x_ref, out_ref, send_sems, recv_sems, local_sem):
        x, y, c = lax.axis_index("x"), lax.axis_index("y"), lax.axis_index("c")
        me, sibling = (x, y, c), (x, y, 1 - c)
        chips = [(1 - x, y), (x, 1 - y), (1 - x, 1 - y)]

        def block(px, py, pc):
            return out_ref.at[4 * px + 2 * py + pc]

        def copy(k, blk, to, src=None):
            return pltpu.make_async_remote_copy(
                src_ref=block(*blk) if src is None else src, dst_ref=block(*blk),
                send_sem=send_sems.at[k], recv_sem=recv_sems.at[k], device_id=to, device_id_type=MESH)

        mine = pltpu.make_async_copy(x_ref, block(*me), local_sem)
        mine.start()
        first = [copy(0, me, sibling, src=x_ref)]
        first += [copy(1 + j, me, (*chip, c), src=x_ref) for j, chip in enumerate(chips)]
        for cp in first:
            cp.start()
        passed = [copy(4 + j, (*chip, c), sibling) for j, chip in enumerate(chips)]
        for j, chip in enumerate(chips):
            copy(1 + j, (*chip, c), me).wait_recv()
            passed[j].start()
        copy(0, sibling, me).wait_recv()
        for j, chip in enumerate(chips):
            copy(4 + j, (*chip, 1 - c), me).wait_recv()
        for cp in first + passed:
            cp.wait_send()
        mine.wait()

    return pl.pallas_call(
        body, name=name, out_shape=jax.ShapeDtypeStruct((N_DEV, rows, lanes), shard.dtype),
        in_specs=[ANY], out_specs=ANY,
        scratch_shapes=[pltpu.SemaphoreType.DMA((7,)), pltpu.SemaphoreType.DMA((7,)), pltpu.SemaphoreType.DMA(())],
    )(shard)


def _peer_copies(kind, src_ref, out_ref, send_sems, recv_sems, local_sem):
    x, y, c = lax.axis_index("x"), lax.axis_index("y"), lax.axis_index("c")
    me = 4 * x + 2 * y + c

    def src(idx):
        return src_ref.at[idx] if kind == "exchange" else src_ref

    mine = pltpu.make_async_copy(src(me), out_ref.at[me], local_sem)
    copies = []
    for r in range(1, N_DEV):
        px = 1 - x if r & 4 else x
        py = 1 - y if r & 2 else y
        pc = 1 - c if r & 1 else c
        copies.append(pltpu.make_async_remote_copy(
            src_ref=src(4 * px + 2 * py + pc), dst_ref=out_ref.at[me],
            send_sem=send_sems.at[r - 1], recv_sem=recv_sems.at[r - 1],
            device_id=(px, py, pc), device_id_type=MESH))
    return mine, copies


PEER_SEMS = [pltpu.SemaphoreType.DMA((7,)), pltpu.SemaphoreType.DMA((7,)), pltpu.SemaphoreType.DMA(())]


def _add_rider(rider, in_specs, args, out_specs, out_shape):
    if rider is None:
        return []
    _, arr = rider
    in_specs.append(ANY)
    args.append(arr)
    out_specs.append(ANY)
    out_shape.append(jax.ShapeDtypeStruct((N_DEV,) + arr.shape[-2:], arr.dtype))
    return list(PEER_SEMS)


def _split_rider(refs, rider, n_in, n_out):
    if rider is None:
        return refs, None
    refs = list(refs)
    rin = refs.pop(n_in)
    rout = refs.pop(n_in + n_out)
    return refs[:-3], (rin, rout, *refs[-3:])


def _ride_start(rider, ride_refs, first):
    if rider is None:
        return

    @pl.when(first)
    def _():
        mine, copies = _peer_copies(rider[0], *ride_refs)
        mine.start()
        for cp in copies:
            cp.start()


def _ride_wait(rider, ride_refs, last):
    if rider is None:
        return

    @pl.when(last)
    def _():
        mine, copies = _peer_copies(rider[0], *ride_refs)
        for cp in copies:
            cp.wait()
        mine.wait()


def _gathered_cols(blocks, kdim):
    n = blocks.shape[1] * WIDE // kdim
    return blocks.reshape(N_DEV, kdim, n).transpose(1, 0, 2).reshape(kdim, N_DEV * n)


def _scatter_cols(dw):
    kdim, n8 = dw.shape
    n = n8 // N_DEV
    return dw.reshape(kdim, N_DEV, n).transpose(1, 0, 2).reshape(N_DEV, kdim * n // WIDE, WIDE)


def _pad_rows(a, rows):
    pad = [(0, 0)] * a.ndim
    pad[-2] = (0, rows - a.shape[-2])
    return jnp.pad(a, pad)


def _layer0_in_weight_t(wt):
    cq, ckv, kpe = wt[0:256], wt[256:384], wt[384:416]
    q_s, k_s, v_s, gate = wt[416:928], wt[928:1056], wt[1056:1184], wt[1184:2208]
    z = jnp.zeros((64, wt.shape[1]), wt.dtype)
    return jnp.concatenate([gate, cq, ckv, z, kpe, z[:32], q_s, k_s, v_s], axis=0)


def _layer0_in_grad_t(dwt):
    gate, cq, ckv, kpe = dwt[0:1024], dwt[1024:1280], dwt[1280:1408], dwt[1472:1504]
    q_s, k_s, v_s = dwt[1536:2048], dwt[2048:2176], dwt[2176:2304]
    return jnp.concatenate([cq, ckv, kpe, q_s, k_s, v_s, gate], axis=0)


def _layer1_in_weight_t(wt):
    main = jnp.concatenate([wt[:3 * D_MODEL], wt[3 * D_MODEL + FOX_HEADS:]], axis=0)
    return main, _pad_rows(wt[3 * D_MODEL:3 * D_MODEL + FOX_HEADS], LANES)


def _layer1_in_grad_t(d_main, d_wft):
    return jnp.concatenate([d_main[:3 * D_MODEL], d_wft[:FOX_HEADS], d_main[3 * D_MODEL:]], axis=0)


def _q_up_weight(w):
    return jnp.pad(w.reshape(MLA_Q_RANK, MLA_HEADS, 96), ((0, 0), (0, 0), (0, 32))).reshape(MLA_Q_RANK, MLA_HEADS * LANES)


def _q_up_grad(dwp):
    return dwp.reshape(MLA_Q_RANK, MLA_HEADS, LANES)[:, :, :96].reshape(MLA_Q_RANK, MLA_HEADS * 96)


def _kv_up_weight(w):
    w4 = w.reshape(MLA_KV_RANK, MLA_HEADS, 2, 64)
    kp = jnp.pad(w4[:, :, 0, :], ((0, 0), (0, 0), (0, 64))).reshape(MLA_KV_RANK, MLA_HEADS * LANES)
    vp = w4[:, :, 1, :].reshape(MLA_KV_RANK, MLA_HEADS * 64)
    return jnp.concatenate([kp, vp], axis=1)


def _kv_up_grad(dwp):
    dk = dwp[:, :MLA_HEADS * LANES].reshape(MLA_KV_RANK, MLA_HEADS, LANES)[:, :, :64]
    dv = dwp[:, MLA_HEADS * LANES:].reshape(MLA_KV_RANK, MLA_HEADS, 64)
    return jnp.stack([dk, dv], axis=2).reshape(MLA_KV_RANK, MLA_HEADS * LANES)


def _pad_lanes(a):
    return jnp.pad(a, ((0, 0), (0, LANES - a.shape[1])))


def _small_pack(g_in, g_final, g_q_a, g_kv_a, sinks, b_f, loss):
    rows = [g_in.reshape(8, LANES), g_final.reshape(8, LANES), g_q_a.reshape(2, LANES), g_kv_a.reshape(1, LANES),
            _pad_lanes(sinks.reshape(1, -1)), _pad_lanes(b_f.reshape(1, -1)), _pad_lanes(loss.reshape(1, 1)),
            jnp.zeros((2, LANES), F32)]
    return jnp.concatenate(rows, axis=0)


def _small_unpack(a):
    return (a[0:8].reshape(1, D_MODEL), a[8:16].reshape(D_MODEL), a[16:18].reshape(1, MLA_Q_RANK),
            a[18:19].reshape(1, MLA_KV_RANK), a[19:20, :SWA_HEADS], a[20:21, :FOX_HEADS], a[21, 0])


def _local_step(x, positions, target, e_g_in, w0t, e_g_q_a, wq, e_g_kv_a, wkv, e_sinks,
                late, o_b_f, g_final, scatter1=None, scatter0=None):
    s = x.shape[0]
    mla_scale = (MLA_NOPE + MLA_ROPE) ** -0.5
    fox_scale = FOX_DIM ** -0.5
    n0a = Z0A_UNITS * LANES

    inv_freq = 1.0 / (ROPE_THETA ** (jnp.arange(0, MLA_ROPE, 2, dtype=F32) / MLA_ROPE))
    ang = positions.astype(F32)[:, None] * inv_freq
    cos, sin = jnp.cos(ang), jnp.sin(ang)
    ones, zeros = jnp.ones((s, 64), F32), jnp.zeros((s, 64), F32)
    cos_t = jnp.concatenate([ones, cos, cos, ones[:, :32]], axis=1)
    sin_t = jnp.concatenate([zeros, -sin, sin, zeros[:, :32]], axis=1)

    h0 = _rmsnorm_fwd(x, e_g_in, width=D_MODEL, col_blk=0, name="l0_norm")
    z0a = _matmul(h0, w0t, tb=True, b_rows=(0, n0a), name="l0_in_a")
    z0b = _matmul(h0, w0t, tb=True, b_rows=(n0a, Z0B_UNITS * LANES), name="l0_in_b", out_dtype=BF16)
    cqn = _rmsnorm_fwd(z0a, e_g_q_a, width=MLA_Q_RANK, col_blk=4, name="l0_q_norm")
    ckvn = _rmsnorm_fwd(z0a, e_g_kv_a, width=MLA_KV_RANK, col_blk=10, name="l0_kv_norm")
    qp = _matmul(cqn, wq, name="l0_q_up")
    kvp = _matmul(ckvn, wkv, name="l0_kv_up", out_dtype=BF16)
    qm, km = _rope_fwd(qp, kvp, z0a, cos_t, sin_t, name="l0_rope")
    gathers = len(late) == 2
    res = _flash_fwd(qm, km, kvp, None, n_pairs=MLA_HEADS // 2, hw=LANES, q_off=0, k_off=0, v_off=MLA_HEADS,
                     scale=mla_scale, name="l0_mla_fwd", rider=("gather", late[0]) if gathers else None)
    o_mla, lse_mla = res[0], res[1]
    wo0, o_g_in, w1t, wft, wo1 = late[1](res[2]) if gathers else late
    o_swa, lse_swa = _swa_fwd(z0b, e_sinks, name="l0_swa_fwd")
    og0 = _gate_fwd([o_mla, o_swa], z0a, name="l0_gate")

    x1, h1 = _matmul_rows([(og0, wo0, False)], [(x, D_MODEL, 0)], [o_g_in], _residual_norm_epilogue,
                          [("rows", D_MODEL, F32), ("rows", D_MODEL, BF16)], name="l0_out")
    z1 = _matmul(h1, w1t, tb=True, b_rows=(0, 3 * D_MODEL), name="l1_in_qkv", out_dtype=BF16)
    gate1 = _matmul(h1, w1t, tb=True, b_rows=(3 * D_MODEL, D_MODEL), name="l1_in_gate")
    zf = _matmul(h1, wft, tb=True, name="l1_in_f")
    bf = _pad_lanes(o_b_f)
    log_cum = _logf_fwd(zf, bf, name="l1_logf")
    bias2 = (-LOG2E * log_cum[:, :FOX_HEADS]).T
    tk_bwd = _bwd_key_tile(s)
    bias = bias2.reshape(FOX_HEADS // 2, 2, s // tk_bwd, 1, tk_bwd)
    t_fwd = _fwd_tile(s)
    o_fox, lse_fox = _flash_fwd(z1, z1, z1, bias2.reshape(FOX_HEADS // 2, 2, s // t_fwd, 1, t_fwd),
                                n_pairs=FOX_HEADS // 2, hw=64, q_off=0, k_off=8, v_off=16, scale=fox_scale,
                                name="l1_fox_fwd")
    og1 = _gate_fwd([o_fox], gate1, name="l1_gate")

    dx2, loss_part, d_g_final = _matmul_rows(
        [(og1, wo1, False)], [(x1, D_MODEL, 0), (target, D_MODEL, 0)], [g_final.reshape(1, D_MODEL)], _loss_epilogue,
        [("rows", D_MODEL, F32), ("sum", (8, LANES)), ("sum", (1, D_MODEL))], name="l1_out_loss")

    d_wo1 = _matmul(og1, dx2, ta=True, name="l1_out_dw")
    do_fox, d_gate1 = _matmul_rows([(dx2, wo1, True)], [(o_fox, D_MODEL, 0), (gate1, D_MODEL, 0)], [],
                                   _gate_bwd_epilogue([D_MODEL]), [("rows", D_MODEL, F32), ("rows", D_MODEL, BF16)],
                                   name="l1_out_dx")
    dq1, dk1, dv1, dbias, drow = _flash_bwd(z1, z1, z1, do_fox, o_fox, lse_fox, bias, n_pairs=FOX_HEADS // 2, hw=64,
                                            q_off=0, k_off=8, v_off=16, scale=fox_scale, qk_dtype=BF16,
                                            name="l1_fox_bwd")
    d_log_cum = (drow.reshape(FOX_HEADS, s) - dbias.reshape(FOX_HEADS, s)).T
    d_log_cum = jnp.pad(d_log_cum, ((0, 0), (0, LANES - FOX_HEADS)))
    d_zf, d_bf = _logf_bwd(d_log_cum, zf, bf, name="l1_logf_bwd")
    dz1 = jnp.concatenate([dq1, dk1, dv1, d_gate1], axis=1)
    d_w1t = _matmul(dz1, h1, ta=True, name="l1_in_dw")
    d_wft = _matmul(d_zf, h1, ta=True, name="l1_in_f_dw")
    dx1, d_o_g_in = _matmul_rows([(dz1, w1t, False), (d_zf, wft, False)], [(x1, D_MODEL, 0), (dx2, D_MODEL, 0)],
                                 [o_g_in], _rms_bwd_epilogue, [("rows", D_MODEL, F32), ("sum", (1, D_MODEL))],
                                 name="l1_in_dx")

    d_wo0 = _matmul(og0, dx1, ta=True, name="l0_out_dw")
    half = D_MODEL // 2
    do_mla, do_swa, d_gate0 = _matmul_rows(
        [(dx1, wo0, True)], [(o_mla, half, 0), (o_swa, half, 0), (z0a, D_MODEL, 0)], [], _gate_bwd_epilogue([half, half]),
        [("rows", half, F32), ("rows", half, F32), ("rows", D_MODEL, BF16)], name="l0_out_dx")
    dq_s, dkt_s, dvt_s, d_sinks = _swa_bwd(z0b, e_sinks, do_swa, o_swa, lse_swa, name="l0_swa_bwd")
    dk_s = dkt_s.transpose(0, 2, 1).reshape(s, LANES)
    dv_s = dvt_s.transpose(0, 2, 1).reshape(s, LANES)
    rider = None
    if scatter1 is not None:
        rider = ("exchange", scatter1(dict(w1t=d_w1t, wft=d_wft, wo1=d_wo1, o_g_in=d_o_g_in, wo0=d_wo0)))
    res = _flash_bwd(qm, km, kvp, do_mla, o_mla, lse_mla, None, n_pairs=MLA_HEADS // 2, hw=LANES, q_off=0, k_off=0,
                     v_off=MLA_HEADS, scale=mla_scale, qk_dtype=F32, name="l0_mla_bwd", rider=rider)
    dqm, dkm, dvm = res[0], res[1], res[2]
    recv1 = res[3] if rider is not None else None
    d_qp, d_kvp, d_kpe = _rope_bwd(dqm, dkm, dvm, cos_t, sin_t, name="l0_rope_bwd")
    d_wq = _matmul(cqn, d_qp, ta=True, name="l0_q_up_dw")
    d_cqn = _matmul(d_qp, wq, tb=True, name="l0_q_up_dx")
    d_wkv = _matmul(ckvn, d_kvp, ta=True, name="l0_kv_up_dw")
    d_ckvn = _matmul(d_kvp, wkv, tb=True, name="l0_kv_up_dx")
    d_cq, d_g_q_a = _rmsnorm_bwd(z0a, e_g_q_a, d_cqn, width=MLA_Q_RANK, col_blk=4, name="l0_q_norm_bwd")
    d_ckv, d_g_kv_a = _rmsnorm_bwd(z0a, e_g_kv_a, d_ckvn, width=MLA_KV_RANK, col_blk=10, name="l0_kv_norm_bwd")
    dz0 = jnp.concatenate([d_gate0, d_cq, d_ckv, d_kpe, dq_s.astype(BF16), dk_s.astype(BF16), dv_s.astype(BF16)], axis=1)
    d_w0t = _matmul(dz0, h0, ta=True, name="l0_in_dw")
    rider = None if scatter0 is None else ("exchange", scatter0(dict(w0t=d_w0t, wq=d_wq, wkv=d_wkv)))
    res = _matmul_rows([(dz0, w0t, False)], [(x, D_MODEL, 0), (dx1, D_MODEL, 0)], [e_g_in], _rms_bwd_epilogue,
                       [("rows", D_MODEL, F32), ("sum", (1, D_MODEL))], name="l0_in_dx", rider=rider)
    grad_x, d_e_g_in = res[0], res[1]
    recv0 = res[2] if rider is not None else None

    return dict(recv0=recv0, recv1=recv1, loss=loss_part[0, 0], grad_x=grad_x, e_g_in=d_e_g_in, w0t=d_w0t, e_g_q_a=d_g_q_a, wq=d_wq,
                e_g_kv_a=d_g_kv_a, wkv=d_wkv, e_sinks=d_sinks[:, 0].reshape(1, SWA_HEADS), wo0=d_wo0,
                o_g_in=d_o_g_in, w1t=d_w1t, wft=d_wft, o_b_f=d_bf[:, :FOX_HEADS], wo1=d_wo1, g_final=d_g_final.reshape(D_MODEL))


def _wide(a, rows):
    flat = a.reshape(-1)
    return jnp.pad(flat, (0, rows * WIDE - flat.shape[0])).reshape(rows, WIDE)


def _rows_b0(w_q, w_kv):
    return jnp.concatenate([_wide(w_q, 32), _wide(w_kv, 16)], axis=0)


def _unflat_b0(f):
    return f[0:24].reshape(1, MLA_Q_RANK, 96), f[32:48].reshape(1, MLA_KV_RANK, 128)


def _rows_b1(o_w_out, e_w_out, g_in):
    return jnp.concatenate([o_w_out, e_w_out, _wide(g_in, 16)], axis=0)


def _unflat_b1(f):
    return f[0:128][None], f[128:256][None], f[256:257, :LANES]


def kernel(x, positions, e_g_in, e_w_in, e_g_q_a, e_w_q_up, e_g_kv_a, e_w_kv_up, e_sinks, e_w_out, o_g_in, o_w_in, o_b_f, o_w_out, g_final, loss_target, m_e_g_in, m_e_w_in, m_e_g_q_a, m_e_w_q_up, m_e_g_kv_a, m_e_w_kv_up, m_e_sinks, m_e_w_out, m_o_g_in, m_o_w_in, m_o_b_f, m_o_w_out, m_g_final, v_e_g_in, v_e_w_in, v_e_g_q_a, v_e_w_q_up, v_e_g_kv_a, v_e_w_kv_up, v_e_sinks, v_e_w_out, v_o_g_in, v_o_w_in, v_o_b_f, v_o_w_out, v_g_final):
    def bf(a):
        return a.astype(BF16)

    shard0 = jnp.concatenate([_pad_rows(bf(e_w_in[0]).T, RA0), _rows_b0(bf(e_w_q_up[0]), bf(e_w_kv_up[0]))], axis=0)
    gath0 = _all_gather(shard0, name="weights0_all_gather")
    w0t = _layer0_in_weight_t(gath0[:, :N_E_IN].reshape(N_DEV * N_E_IN, WIDE))
    wq = _q_up_weight(_gathered_cols(gath0[:, RA0:RA0 + 24], MLA_Q_RANK))
    wkv = _kv_up_weight(_gathered_cols(gath0[:, RA0 + 32:RA0 + 48], MLA_KV_RANK))

    g_bits = lax.bitcast_convert_type(o_g_in.reshape(LANES), BF16)
    shard1 = jnp.concatenate([_pad_rows(bf(o_w_in[0]).T, RA1), _rows_b1(bf(o_w_out[0]), bf(e_w_out[0]), g_bits)], axis=0)

    def unpack1(gath1):
        w1t, wft = _layer1_in_weight_t(gath1[:, :N_O_IN].reshape(N_DEV * N_O_IN, WIDE))
        wo1 = gath1[:, RA1:RA1 + 128].reshape(D_MODEL, D_MODEL)
        wo0 = gath1[:, RA1 + 128:RA1 + 256].reshape(D_MODEL, D_MODEL)
        bits = gath1[:, RA1 + 256, :2 * LANES].reshape(N_DEV, LANES, 2)
        return wo0, lax.bitcast_convert_type(bits, F32).reshape(1, D_MODEL), w1t, wft, wo1

    def scatter1(g):
        d_in_t = _layer1_in_grad_t(g["w1t"], g["wft"]).reshape(N_DEV, N_O_IN, WIDE)
        d_o_g = jnp.pad(g["o_g_in"].reshape(N_DEV, 1, LANES), ((0, 0), (0, 15), (0, WIDE - LANES)))
        return jnp.concatenate([_pad_rows(d_in_t, RA1), g["wo1"].reshape(N_DEV, 128, WIDE),
                                g["wo0"].reshape(N_DEV, 128, WIDE), d_o_g], axis=1).astype(BF16)

    def scatter0(g):
        return jnp.concatenate([
            _pad_rows(_layer0_in_grad_t(g["w0t"]).reshape(N_DEV, N_E_IN, WIDE), RA0),
            _pad_rows(_scatter_cols(_q_up_grad(g["wq"])), 32), _scatter_cols(_kv_up_grad(g["wkv"]))], axis=1).astype(BF16)

    gr = _local_step(x[0], positions[0], loss_target[0], e_g_in, w0t, e_g_q_a, wq, e_g_kv_a, wkv, e_sinks,
                     (shard1, unpack1), o_b_f, g_final, scatter1=scatter1, scatter0=scatter0)
    recv0 = gr["recv0"]

    def in_projection(recv, ra, n, w, m, v, name):
        g = _sum8(recv, ra, name=name + "_grad_sum")[:n].T
        d, nm, nv = _adamw_native(g, w[0], m[0], v[0], name=name + "_adamw")
        return g[None], d[None], nm[None], nv[None]

    e_in = in_projection(recv0, RA0, N_E_IN, e_w_in, m_e_w_in, v_e_w_in, "e_w_in")
    o_in = in_projection(gr["recv1"], RA1, N_O_IN, o_w_in, m_o_w_in, v_o_w_in, "o_w_in")
    b0 = _adamw(recv0[:, RA0:], _rows_b0(e_w_q_up[0], e_w_kv_up[0]), _rows_b0(m_e_w_q_up[0], m_e_w_kv_up[0]),
                _rows_b0(v_e_w_q_up[0], v_e_w_kv_up[0]), name="adamw_early")
    b1 = _adamw(gr["recv1"][:, RA1:], _rows_b1(o_w_out[0], e_w_out[0], o_g_in),
                _rows_b1(m_o_w_out[0], m_e_w_out[0], m_o_g_in), _rows_b1(v_o_w_out[0], v_e_w_out[0], v_o_g_in),
                name="adamw_late")

    def sharded(k):
        q_up, kv_up = _unflat_b0(b0[k])
        o_out, e_out, o_g = _unflat_b1(b1[k])
        return e_in[k], q_up, kv_up, e_out, o_in[k], o_out, o_g

    g_sh, d_sh, m_sh, v_sh = [sharded(k) for k in range(4)]

    small = _small_pack(gr["e_g_in"], gr["g_final"], gr["e_g_q_a"], gr["e_g_kv_a"], gr["e_sinks"], gr["o_b_f"], gr["loss"])
    small_all = _all_gather(small, name="small_all_gather")
    zero = jnp.zeros((), F32)
    w_small = _small_pack(e_g_in, g_final, e_g_q_a, e_g_kv_a, e_sinks, o_b_f, zero)
    m_small = _small_pack(m_e_g_in, m_g_final, m_e_g_q_a, m_e_g_kv_a, m_e_sinks, m_o_b_f, zero)
    v_small = _small_pack(v_e_g_in, v_g_final, v_e_g_q_a, v_e_g_kv_a, v_e_sinks, v_o_b_f, zero)
    smalls = _adamw(small_all, w_small, m_small, v_small, name="adamw_replicated")
    g_sm, d_sm, m_sm, v_sm = [_small_unpack(a) for a in smalls]
    loss = g_sm[6]

    def leaves(sh, sm):
        return (sm[0], sh[0], sm[2], sh[1], sm[3], sh[2], sm[4], sh[3], sh[6], sh[4], sm[5], sh[5], sm[1])

    return (loss, gr["grad_x"][None], *leaves(g_sh, g_sm), *leaves(d_sh, d_sm), *leaves(m_sh, m_sm), *leaves(v_sh, v_sm))
```

```python
import functools

import jax
import jax.numpy as jnp
from jax import lax
from jax.experimental import pallas as pl
from jax.experimental.pallas import tpu as pltpu

F32 = jnp.float32
BF16 = jnp.bfloat16
NEG_INF = float("-inf")

N_DEV = 8
LANES = 128
D_MODEL = 1024
EPS = 1e-6
ROPE_THETA = 10000.0
MLA_HEADS = 8
MLA_Q_RANK = 256
MLA_KV_RANK = 128
MLA_NOPE = 64
MLA_ROPE = 32
MLA_V = 64
SWA_HEADS = 8
SWA_KV_HEADS = 2
SWA_DIM = 64
WINDOW = 128
FOX_HEADS = 16
FOX_DIM = 64

ADAM_LR = 0.001
ADAM_B1 = 0.9
ADAM_B2 = 0.999
ADAM_EPS = 1e-08
ADAM_WD = 0.01
ADAM_STEP = 10

ATT_T = 512
ATT_T_FWD = 1024
VMEM_LIMIT = 56 * 1024 * 1024
MATMUL_B_BLOCK_BYTES = 8 * 1024 * 1024

Z0A_UNITS = 12
Z0B_UNITS = 6

WIDE = 1024
N_E_IN = 276
N_O_IN = 514
RA0 = 288
RB0 = 32 + 16
RA1 = 528
RB1 = 128 + 128 + 16
SMALL_ROWS = 24


def _tile(n, cands):
    for c in cands:
        if n % c == 0:
            return c
    raise ValueError(f"no tile for {n}")


ROW_TILES = (512, 256, 128)


def _params(sem, vmem=VMEM_LIMIT):
    return pltpu.CompilerParams(dimension_semantics=sem, vmem_limit_bytes=vmem)


def _matmul(a, b, *, name, ta=False, tb=False, out_dtype=F32, b_rows=None):
    if ta:
        kdim, m = a.shape
    else:
        m, kdim = a.shape
    if tb:
        n, kb = b.shape
    else:
        kb, n = b.shape
    assert kdim == kb, (a.shape, b.shape)
    b_start = 0
    if b_rows is not None:
        assert tb
        b_start, n = b_rows
    tm = _tile(m, (512, 256, 128))
    tn = _tile(n, [c for c in (1024, 768, 512, 384, 256, 128)
                   if c * kdim * b.dtype.itemsize <= MATMUL_B_BLOCK_BYTES and b_start % c == 0])
    assert b_start % tn == 0, (b_start, tn)
    b_off = b_start // tn
    dims = (((0 if ta else 1,), (1 if tb else 0,)), ((), ()))

    def body(a_ref, b_ref, o_ref):
        r = lax.dot_general(a_ref[...].astype(BF16), b_ref[...].astype(BF16), dims, preferred_element_type=F32)
        o_ref[...] = r.astype(out_dtype)

    a_spec = pl.BlockSpec((kdim, tm), lambda i, j: (0, i)) if ta else pl.BlockSpec((tm, kdim), lambda i, j: (i, 0))
    b_spec = pl.BlockSpec((tn, kdim), lambda i, j: (j + b_off, 0)) if tb else pl.BlockSpec((kdim, tn), lambda i, j: (0, j))
    return pl.pallas_call(
        body, name=name, grid=(m // tm, n // tn), in_specs=[a_spec, b_spec],
        out_specs=pl.BlockSpec((tm, tn), lambda i, j: (i, j)), out_shape=jax.ShapeDtypeStruct((m, n), out_dtype),
        compiler_params=_params(("parallel", "parallel")),
    )(a, b)


def _rmsnorm_fwd(x, g, *, width, col_blk, name):
    s = x.shape[0]
    tm = _tile(s, ROW_TILES)

    def body(x_ref, g_ref, y_ref):
        xf = x_ref[...].astype(F32)
        r = lax.rsqrt(jnp.mean(xf * xf, axis=-1, keepdims=True) + EPS)
        y_ref[...] = ((xf * r) * g_ref[...]).astype(BF16)

    return pl.pallas_call(
        body, name=name, grid=(s // tm,),
        in_specs=[pl.BlockSpec((tm, width), lambda i: (i, col_blk)), pl.BlockSpec((1, width), lambda i: (0, 0))],
        out_specs=pl.BlockSpec((tm, width), lambda i: (i, 0)),
        out_shape=jax.ShapeDtypeStruct((s, width), BF16),
        compiler_params=_params(("parallel",)),
    )(x, g)


def _rmsnorm_bwd(x, g, dy, *, width, col_blk, name):
    s = x.shape[0]
    tm = _tile(s, ROW_TILES)

    def body(x_ref, g_ref, dy_ref, dx_ref, dg_ref):
        @pl.when(pl.program_id(0) == 0)
        def _():
            dg_ref[...] = jnp.zeros_like(dg_ref)

        dx, dg = _rms_bwd_epilogue(dy_ref[...], x_ref[...], 0.0, g_ref[...])
        dg_ref[...] += dg
        dx_ref[...] = dx.astype(BF16)

    return pl.pallas_call(
        body, name=name, grid=(s // tm,),
        in_specs=[pl.BlockSpec((tm, width), lambda i: (i, col_blk)), pl.BlockSpec((1, width), lambda i: (0, 0)),
                  pl.BlockSpec((tm, width), lambda i: (i, 0))],
        out_specs=[pl.BlockSpec((tm, width), lambda i: (i, 0)), pl.BlockSpec((1, width), lambda i: (0, 0))],
        out_shape=[jax.ShapeDtypeStruct((s, width), BF16), jax.ShapeDtypeStruct((1, width), F32)],
        compiler_params=_params(("arbitrary",)),
    )(x, g, dy)


def _sigmoid(x):
    return 1.0 / (1.0 + jnp.exp(-x))


def _gate_fwd(o_parts, gate, *, name):
    s = gate.shape[0]
    tm = _tile(s, ROW_TILES)
    n_o = len(o_parts)

    def body(*refs):
        o_refs, g_ref, y_ref = refs[:n_o], refs[n_o], refs[n_o + 1]
        o = o_refs[0][...] if n_o == 1 else jnp.concatenate([r[...] for r in o_refs], axis=1)
        gt = g_ref[...]
        y_ref[...] = (o * (gt * _sigmoid(gt))).astype(BF16)

    in_specs = [pl.BlockSpec((tm, o.shape[1]), lambda i: (i, 0)) for o in o_parts]
    in_specs.append(pl.BlockSpec((tm, D_MODEL), lambda i: (i, 0)))
    return pl.pallas_call(
        body, name=name, grid=(s // tm,), in_specs=in_specs,
        out_specs=pl.BlockSpec((tm, D_MODEL), lambda i: (i, 0)),
        out_shape=jax.ShapeDtypeStruct((s, D_MODEL), BF16),
        compiler_params=_params(("parallel",)),
    )(*o_parts, gate)


def _matmul_rows(terms, row_inputs, params, epilogue, outs, *, name, rider=None):
    s = terms[0][0].shape[0]
    tm = _tile(s, ROW_TILES)
    steps = s // tm
    n_t, n_r, n_p, n_o = len(terms), len(row_inputs), len(params), len(outs)

    def body(*refs):
        refs, ride_refs = _split_rider(refs, rider, n_in=2 * n_t + n_r + n_p, n_out=n_o)
        t_refs, r_refs = refs[:2 * n_t], refs[2 * n_t:2 * n_t + n_r]
        p_refs, o_refs = refs[2 * n_t + n_r:2 * n_t + n_r + n_p], refs[2 * n_t + n_r + n_p:]
        i = pl.program_id(0)
        _ride_start(rider, ride_refs, i == 0)
        acc = None
        for k, term in enumerate(terms):
            dims = (((1,), (1 if term[2] else 0,)), ((), ()))
            part = lax.dot_general(t_refs[2 * k][...].astype(BF16), t_refs[2 * k + 1][...].astype(BF16), dims,
                                   preferred_element_type=F32)
            acc = part if acc is None else acc + part
        vals = epilogue(acc, *[r[...] for r in r_refs], *[p[...] for p in p_refs])
        for ref, val, out in zip(o_refs, vals, outs):
            if out[0] == "rows":
                ref[...] = val.astype(ref.dtype)
            else:
                @pl.when(i == 0)
                def _(ref=ref):
                    ref[...] = jnp.zeros_like(ref)

                ref[...] += val
        _ride_wait(rider, ride_refs, i == steps - 1)

    in_specs, args = [], []
    for term in terms:
        a, b = term[0], term[1]
        b_rows = b.shape[0] if term[2] or len(term) < 4 else a.shape[1]
        b_blk = 0 if len(term) < 4 else term[3] // b_rows
        in_specs += [pl.BlockSpec((tm, a.shape[1]), lambda i: (i, 0)),
                     _resident((b_rows, b.shape[1]), lambda i, b_blk=b_blk: (b_blk, 0))]
        args += [a, b]
    for arr, width, col_blk in row_inputs:
        in_specs.append(pl.BlockSpec((tm, width), lambda i, col_blk=col_blk: (i, col_blk)))
        args.append(arr)
    for p in params:
        in_specs.append(pl.BlockSpec(p.shape, lambda i: (0, 0)))
        args.append(p)
    out_specs, out_shape = [], []
    for out in outs:
        if out[0] == "rows":
            out_specs.append(pl.BlockSpec((tm, out[1]), lambda i: (i, 0)))
            out_shape.append(jax.ShapeDtypeStruct((s, out[1]), out[2]))
        else:
            out_specs.append(pl.BlockSpec(out[1], lambda i: (0, 0)))
            out_shape.append(jax.ShapeDtypeStruct(out[1], F32))
    scratch = _add_rider(rider, in_specs, args, out_specs, out_shape)
    return pl.pallas_call(
        body, name=name, grid=(steps,), in_specs=in_specs, out_specs=out_specs, out_shape=out_shape,
        scratch_shapes=scratch, compiler_params=_params(("arbitrary",)),
    )(*args)


def _rms_stats(x):
    r = lax.rsqrt(jnp.mean(x * x, axis=-1, keepdims=True) + EPS)
    return r, x * r


def _residual_norm_epilogue(r, x, g):
    x1 = x + r
    _, xh = _rms_stats(x1)
    return x1, xh * g


def _rms_bwd_epilogue(dy, x, add, g):
    r, xh = _rms_stats(x)
    dxh = dy * g
    dx = r * (dxh - xh * jnp.mean(dxh * xh, axis=-1, keepdims=True)) + add
    return dx, jnp.sum(dy * xh, axis=0, keepdims=True)


def _loss_epilogue(r, x1, target, g):
    rs, xh = _rms_stats(x1 + r)
    err = xh * g - target
    loss = jnp.broadcast_to(0.5 * jnp.sum(jnp.mean(err * err, axis=-1, keepdims=True)), (8, LANES))
    dy = err * (1.0 / D_MODEL)
    dxh = dy * g
    dx = rs * (dxh - xh * jnp.mean(dxh * xh, axis=-1, keepdims=True))
    return dx, loss, jnp.sum(dy * xh, axis=0, keepdims=True)


def _gate_bwd_epilogue(widths):
    def epilogue(d, *rows):
        o_parts, gt = rows[:-1], rows[-1]
        o = o_parts[0] if len(o_parts) == 1 else jnp.concatenate(o_parts, axis=1)
        sg = _sigmoid(gt)
        do = d * (gt * sg)
        d_gate = d * o * (sg * (1.0 + gt * (1.0 - sg)))
        cuts = [sum(widths[:k]) for k in range(len(widths) + 1)]
        return tuple(do[:, cuts[k]:cuts[k + 1]] for k in range(len(widths))) + (d_gate,)

    return epilogue


def _rot_half(x):
    lane = lax.broadcasted_iota(jnp.int32, x.shape, 1)
    return jnp.where(lane < 80, pltpu.roll(x, LANES - 16, axis=1), pltpu.roll(x, 16, axis=1))


def _rot_half_t(g):
    lane = lax.broadcasted_iota(jnp.int32, g.shape, 1)
    lo = (lane >= MLA_NOPE) & (lane < MLA_NOPE + MLA_ROPE // 2)
    hi = (lane >= MLA_NOPE + MLA_ROPE // 2) & (lane < MLA_NOPE + MLA_ROPE)
    return jnp.where(lo, pltpu.roll(g, LANES - 16, axis=1), jnp.where(hi, pltpu.roll(g, 16, axis=1), 0.0))


def _rope_fwd(qp, kvp, z0a, cos_t, sin_t, *, name):
    s = qp.shape[0]
    tm = _tile(s, ROW_TILES)
    hw = MLA_HEADS * LANES

    def body(q_ref, k_ref, kpe_ref, c_ref, s_ref, qm_ref, km_ref):
        c = c_ref[...]
        sn = s_ref[...]
        kpe = kpe_ref[...]
        kpe_r = (kpe * c + _rot_half(kpe) * sn).astype(BF16)
        lane = lax.broadcasted_iota(jnp.int32, kpe.shape, 1)
        for h in range(MLA_HEADS):
            sl = slice(h * LANES, (h + 1) * LANES)
            qh = q_ref[:, sl]
            qm_ref[:, sl] = (qh * c + _rot_half(qh) * sn).astype(BF16)
            km_ref[:, sl] = jnp.where(lane < MLA_NOPE, k_ref[:, sl], kpe_r)

    return pl.pallas_call(
        body, name=name, grid=(s // tm,),
        in_specs=[pl.BlockSpec((tm, hw), lambda i: (i, 0)), pl.BlockSpec((tm, hw), lambda i: (i, 0)),
                  pl.BlockSpec((tm, LANES), lambda i: (i, 11)),
                  pl.BlockSpec((tm, LANES), lambda i: (i, 0)), pl.BlockSpec((tm, LANES), lambda i: (i, 0))],
        out_specs=[pl.BlockSpec((tm, hw), lambda i: (i, 0)), pl.BlockSpec((tm, hw), lambda i: (i, 0))],
        out_shape=[jax.ShapeDtypeStruct((s, hw), BF16), jax.ShapeDtypeStruct((s, hw), BF16)],
        compiler_params=_params(("parallel",)),
    )(qp, kvp, z0a, cos_t, sin_t)


def _rope_bwd(dqm, dkm, dvm, cos_t, sin_t, *, name):
    s = dqm.shape[0]
    tm = _tile(s, ROW_TILES)
    hw = MLA_HEADS * LANES
    vw = MLA_HEADS * MLA_V

    def body(dq_ref, dk_ref, dv_ref, c_ref, s_ref, dqp_ref, dkv_ref, dkpe_ref):
        c = c_ref[...]
        sn = s_ref[...]
        ksum = jnp.zeros((tm, LANES), F32)
        for h in range(MLA_HEADS):
            sl = slice(h * LANES, (h + 1) * LANES)
            dq = dq_ref[:, sl]
            dqp_ref[:, sl] = (dq * c + _rot_half_t(dq * sn)).astype(BF16)
            dk = dk_ref[:, sl]
            dkv_ref[:, sl] = dk.astype(BF16)
            ksum = ksum + dk
        dkv_ref[:, hw:] = dv_ref[...]
        lane = lax.broadcasted_iota(jnp.int32, ksum.shape, 1)
        dkpe = ksum * c + _rot_half_t(ksum * sn)
        dkpe_ref[...] = jnp.where((lane >= MLA_NOPE) & (lane < MLA_NOPE + MLA_ROPE), dkpe, 0.0).astype(BF16)

    return pl.pallas_call(
        body, name=name, grid=(s // tm,),
        in_specs=[pl.BlockSpec((tm, hw), lambda i: (i, 0)), pl.BlockSpec((tm, hw), lambda i: (i, 0)),
                  pl.BlockSpec((tm, vw), lambda i: (i, 0)),
                  pl.BlockSpec((tm, LANES), lambda i: (i, 0)), pl.BlockSpec((tm, LANES), lambda i: (i, 0))],
        out_specs=[pl.BlockSpec((tm, hw), lambda i: (i, 0)), pl.BlockSpec((tm, hw + vw), lambda i: (i, 0)),
                   pl.BlockSpec((tm, LANES), lambda i: (i, 0))],
        out_shape=[jax.ShapeDtypeStruct((s, hw), BF16), jax.ShapeDtypeStruct((s, hw + vw), BF16),
                   jax.ShapeDtypeStruct((s, LANES), BF16)],
        compiler_params=_params(("parallel",)),
    )(dqm, dkm, dvm, cos_t, sin_t)


def _head_mask(shape, a):
    lane = lax.broadcasted_iota(jnp.int32, shape, 1)
    return (lane >= 64 * a) & (lane < 64 * (a + 1))


_NT = (((1,), (1,)), ((), ()))
LOG2E = 1.4426950408889634


def _stack_heads(tile, hw):
    lane = lax.broadcasted_iota(jnp.int32, tile.shape, 1)
    z = jnp.zeros_like(tile)
    return jnp.concatenate([jnp.where(lane < hw, tile, z), jnp.where(lane >= hw, tile, z)], axis=0)


def _stacked_rows(r0, r1, t):
    n = r0.shape[-1]
    return jnp.concatenate([jnp.broadcast_to(r0, (t, n)), jnp.broadcast_to(r1, (t, n))], axis=0)


def _resident(block, index_map):
    return pl.BlockSpec(block, index_map, pipeline_mode=pl.Buffered(1))


def _fwd_tile(s):
    return ATT_T_FWD if s % ATT_T_FWD == 0 else min(ATT_T, s)


def _flash_fwd(q, k, v, bias, *, n_pairs, hw, q_off, k_off, v_off, scale, name, rider=None):
    s = q.shape[0]
    t = _fwd_tile(s)
    nb = s // t
    qw = 2 * hw
    has_bias = bias is not None
    c1 = scale * LOG2E

    def body(*refs):
        refs, ride_refs = _split_rider(refs, rider, n_in=4 if has_bias else 3, n_out=2)
        if has_bias:
            q_ref, k_ref, v_ref, b_ref, o_ref, lse_ref, vt_ref, bcol_ref = refs
        else:
            q_ref, k_ref, v_ref, o_ref, lse_ref, vt_ref = refs
            b_ref = bcol_ref = None
        _ride_start(rider, ride_refs, pl.program_id(0) == 0)
        row = lax.broadcasted_iota(jnp.int32, (t, t), 0)
        col = lax.broadcasted_iota(jnp.int32, (t, t), 1)
        cmask_t = jnp.concatenate([row <= col, row <= col], axis=1)
        lane_lt64 = lax.broadcasted_iota(jnp.int32, (t, LANES), 1) < 64

        def as_column(r):
            return jnp.broadcast_to(r, (8, r.shape[1])).T[:, 0:1]

        def v_block(j, _):
            c0 = pl.multiple_of(j * t, t)
            vt_ref[j] = v_ref[pl.ds(c0, t), :].astype(F32).T.astype(BF16)
            if has_bias:
                for a in range(2):
                    bcol_ref[a, pl.ds(c0, t), :] = as_column(b_ref[0, a, j])
            return 0

        lax.fori_loop(0, nb, v_block, 0)

        def stacked_queries(i):
            return _stack_heads(q_ref[pl.ds(pl.multiple_of(i * t, t), t), :], hw).astype(F32).T.astype(BF16)

        def kv_step(j, carry, qs_t, masked):
            m, l, acc = carry
            rows = pl.ds(pl.multiple_of(j * t, t), t)
            sc = jnp.dot(k_ref[rows, :], qs_t, preferred_element_type=F32) * c1
            if has_bias:
                sc = sc + jnp.concatenate([jnp.broadcast_to(bcol_ref[0, rows, :], (t, t)),
                                           jnp.broadcast_to(bcol_ref[1, rows, :], (t, t))], axis=1)
            if masked:
                sc = jnp.where(cmask_t, sc, NEG_INF)
            m_new = jnp.maximum(m, jnp.max(sc, axis=0, keepdims=True))
            alpha = jnp.exp2(m - m_new)
            p = jnp.exp2(sc - m_new)
            l_new = alpha * l + jnp.sum(p, axis=0, keepdims=True)
            pv = jnp.dot(vt_ref[j], p.astype(BF16), preferred_element_type=F32)
            return m_new, l_new, alpha * acc + pv

        def finish(i, carry):
            m, l, acc = carry
            r0 = pl.multiple_of(i * t, t)
            out = (acc / l).T
            lse2 = as_column(m + jnp.log2(l))
            lse_ref[0, 0, pl.ds(r0, t), :] = lse2[:t]
            lse_ref[0, 1, pl.ds(r0, t), :] = lse2[t:]
            o_ref[pl.ds(r0, t), :] = jnp.where(lane_lt64, out[:t], out[t:])

        init = (jnp.full((1, 2 * t), NEG_INF, F32), jnp.zeros((1, 2 * t), F32), jnp.zeros((LANES, 2 * t), F32))

        def q_block(i, _):
            qs_t = stacked_queries(i)
            carry = lax.fori_loop(0, i, lambda j, c: kv_step(j, c, qs_t, False), init)
            finish(i, kv_step(i, carry, qs_t, True))
            return 0

        lax.fori_loop(0, nb, q_block, 0)
        _ride_wait(rider, ride_refs, pl.program_id(0) == n_pairs - 1)

    in_specs = [_resident((s, qw), lambda p: (0, q_off + p)), _resident((s, qw), lambda p: (0, k_off + p)),
                _resident((s, LANES), lambda p: (0, v_off + p))]
    args = [q, k, v]
    if has_bias:
        in_specs.append(_resident((1, 2, nb, 1, t), lambda p: (p, 0, 0, 0, 0)))
        args.append(bias)
    out_specs = [pl.BlockSpec((s, LANES), lambda p: (0, p)), pl.BlockSpec((1, 2, s, 1), lambda p: (p, 0, 0, 0))]
    out_shape = [jax.ShapeDtypeStruct((s, n_pairs * LANES), F32), jax.ShapeDtypeStruct((n_pairs, 2, s, 1), F32)]
    scratch = [pltpu.VMEM((nb, LANES, t), BF16)] + ([pltpu.VMEM((2, s, 1), F32)] if has_bias else [])
    scratch += _add_rider(rider, in_specs, args, out_specs, out_shape)
    return pl.pallas_call(
        body, name=name, grid=(n_pairs,), in_specs=in_specs, out_specs=out_specs, out_shape=out_shape,
        scratch_shapes=scratch,
        compiler_params=_params(("parallel",) if rider is None else ("arbitrary",)),
    )(*args)


def _flash_bwd(q, k, v, do, o, lse, bias, *, n_pairs, hw, q_off, k_off, v_off, scale, qk_dtype, name, rider=None):
    s = q.shape[0]
    t = min(ATT_T, s)
    nb = s // t
    qw = 2 * hw
    has_bias = bias is not None
    c1 = scale * LOG2E

    def body(*refs):
        refs, ride_refs = _split_rider(refs, rider, n_in=7 if has_bias else 6, n_out=5 if has_bias else 3)
        if has_bias:
            (q_ref, k_ref, v_ref, do_ref, o_ref, lse_ref, b_ref, dq_ref, dk_ref, dv_ref, db_ref, dr_ref,
             dkt_ref, dvt_ref) = refs
            db_ref[...] = jnp.zeros_like(db_ref)
        else:
            q_ref, k_ref, v_ref, do_ref, o_ref, lse_ref, dq_ref, dk_ref, dv_ref, dkt_ref, dvt_ref = refs
            b_ref = db_ref = dr_ref = None
        _ride_start(rider, ride_refs, pl.program_id(0) == 0)
        dkt_ref[...] = jnp.zeros_like(dkt_ref)
        dvt_ref[...] = jnp.zeros_like(dvt_ref)
        causal = lax.broadcasted_iota(jnp.int32, (t, t), 1) <= lax.broadcasted_iota(jnp.int32, (t, t), 0)
        cmask = jnp.concatenate([causal, causal], axis=0)
        lane_lt_hw = lax.broadcasted_iota(jnp.int32, (t, qw), 1) < hw

        def q_block(i, _):
            r0 = pl.multiple_of(i * t, t)
            qs = _stack_heads(q_ref[pl.ds(r0, t), :], hw)
            dos = _stack_heads(do_ref[pl.ds(r0, t), :], 64)
            ot = o_ref[pl.ds(r0, t), :]
            delta = jnp.sum(dos * jnp.concatenate([ot, ot], axis=0), axis=-1, keepdims=True)
            lse2 = jnp.concatenate([lse_ref[0, 0, pl.ds(r0, t), :], lse_ref[0, 1, pl.ds(r0, t), :]], axis=0)
            dosb = dos.astype(BF16)
            dos_t = dos.T.astype(BF16)
            qs_t = qs.astype(F32).T.astype(BF16)

            def kv_step(j, carry, masked):
                dq, rsum = carry
                c0 = pl.multiple_of(j * t, t)
                kt = k_ref[pl.ds(c0, t), :]
                vt = v_ref[pl.ds(c0, t), :]
                sc = lax.dot_general(qs, kt, _NT, preferred_element_type=F32) * c1
                if has_bias:
                    sc = sc + _stacked_rows(b_ref[0, 0, j], b_ref[0, 1, j], t)
                if masked:
                    sc = jnp.where(cmask, sc, NEG_INF)
                p = jnp.exp2(sc - lse2)
                dp = lax.dot_general(dosb, vt, _NT, preferred_element_type=F32)
                ds = p * (dp - delta)
                dsb = ds.astype(BF16)
                pb = p.astype(BF16)
                if hw == LANES:
                    dvt_ref[j] += jnp.concatenate(
                        [jnp.dot(dos_t[:64, :t], pb[:t], preferred_element_type=F32),
                         jnp.dot(dos_t[64:, t:], pb[t:], preferred_element_type=F32)], axis=0)
                    dkt_ref[j] += jnp.concatenate(
                        [jnp.dot(qs_t[:hw, :t], dsb[:t], preferred_element_type=F32),
                         jnp.dot(qs_t[hw:, t:], dsb[t:], preferred_element_type=F32)], axis=0)
                else:
                    dvt_ref[j] += jnp.dot(dos_t, pb, preferred_element_type=F32)
                    dkt_ref[j] += jnp.dot(qs_t, dsb, preferred_element_type=F32)
                if has_bias:
                    db_ref[0, 0, j] += jnp.sum(ds[:t], axis=0, keepdims=True)
                    db_ref[0, 1, j] += jnp.sum(ds[t:], axis=0, keepdims=True)
                    rsum = rsum + jnp.sum(ds, axis=-1, keepdims=True)
                return dq + jnp.dot(dsb, kt, preferred_element_type=F32), rsum

            init = (jnp.zeros((2 * t, qw), F32), jnp.zeros((2 * t, 1), F32))
            carry = lax.fori_loop(0, i, functools.partial(kv_step, masked=False), init)
            dq, rsum = kv_step(i, carry, True)
            dq = dq * scale
            dq_ref[pl.ds(r0, t), :] = jnp.where(lane_lt_hw, dq[:t], dq[t:]).astype(qk_dtype)
            if has_bias:
                rsum_row = jnp.broadcast_to(rsum, (2 * t, LANES)).T[0:1]
                dr_ref[0, 0, i] = rsum_row[:, :t]
                dr_ref[0, 1, i] = rsum_row[:, t:]
            return 0

        lax.fori_loop(0, nb, q_block, 0)

        def k_block(j, _):
            c0 = pl.multiple_of(j * t, t)
            dk_ref[pl.ds(c0, t), :] = (dkt_ref[j].T * scale).astype(qk_dtype)
            dv_ref[pl.ds(c0, t), :] = dvt_ref[j].T.astype(BF16)
            return 0

        lax.fori_loop(0, nb, k_block, 0)
        _ride_wait(rider, ride_refs, pl.program_id(0) == n_pairs - 1)

    in_specs = [_resident((s, qw), lambda p: (0, q_off + p)), _resident((s, qw), lambda p: (0, k_off + p)),
                _resident((s, LANES), lambda p: (0, v_off + p)),
                _resident((s, LANES), lambda p: (0, p)), _resident((s, LANES), lambda p: (0, p)),
                _resident((1, 2, s, 1), lambda p: (p, 0, 0, 0))]
    args = [q, k, v, do, o, lse]
    out_specs = [pl.BlockSpec((s, qw), lambda p: (0, p)), pl.BlockSpec((s, qw), lambda p: (0, p)),
                 pl.BlockSpec((s, LANES), lambda p: (0, p))]
    out_shape = [jax.ShapeDtypeStruct((s, n_pairs * qw), qk_dtype), jax.ShapeDtypeStruct((s, n_pairs * qw), qk_dtype),
                 jax.ShapeDtypeStruct((s, n_pairs * LANES), BF16)]
    if has_bias:
        in_specs.append(_resident((1, 2, nb, 1, t), lambda p: (p, 0, 0, 0, 0)))
        args.append(bias)
        for _ in range(2):
            out_specs.append(pl.BlockSpec((1, 2, nb, 1, t), lambda p: (p, 0, 0, 0, 0)))
            out_shape.append(jax.ShapeDtypeStruct((n_pairs, 2, nb, 1, t), F32))
    scratch = [pltpu.VMEM((nb, qw, t), F32), pltpu.VMEM((nb, LANES, t), F32)]
    scratch += _add_rider(rider, in_specs, args, out_specs, out_shape)
    return pl.pallas_call(
        body, name=name, grid=(n_pairs,), in_specs=in_specs, out_specs=out_specs, out_shape=out_shape,
        scratch_shapes=scratch,
        compiler_params=_params(("parallel",) if rider is None else ("arbitrary",)),
    )(*args)


def _alibi_slope(h):
    return 2.0 ** (-8.0 * (h + 1.0) / SWA_HEADS)


SWA_ROWS = 512
SWA_SCALE = SWA_DIM ** -0.5


def _swa_geometry(i):
    w = WINDOW
    r0 = pl.multiple_of(i * w, w)
    b0 = pl.multiple_of(jnp.maximum(i - 1, 0) * w, w)
    row = lax.broadcasted_iota(jnp.int32, (w, 2 * w), 0)
    col = lax.broadcasted_iota(jnp.int32, (w, 2 * w), 1)
    dist = row - col + (r0 - b0)
    valid = (dist >= 0) & (dist < w)
    return r0, b0, dist.astype(F32), valid


def _swa_q_head(qblk, h):
    kv = h // (SWA_HEADS // SWA_KV_HEADS)
    if h % 2 != kv:
        qblk = pltpu.roll(qblk, 64, axis=1)
    return jnp.where(_head_mask(qblk.shape, kv), qblk, 0.0)


SWA_GROUP = SWA_HEADS // SWA_KV_HEADS


def _swa_stack(ref, rs, grp):
    parts = []
    for a in range(SWA_GROUP):
        h = SWA_GROUP * grp + a
        parts.append(_swa_q_head(ref[rs, (h // 2) * LANES:(h // 2 + 1) * LANES].astype(F32), h))
    return jnp.concatenate(parts, axis=0)


def _swa_unstack(x, grp):
    tiles = []
    for a in range(SWA_GROUP):
        h = SWA_GROUP * grp + a
        tile = x[a * WINDOW:(a + 1) * WINDOW]
        tiles.append(pltpu.roll(tile, 64, axis=1) if h % 2 != grp else tile)
    return tiles


def _swa_head_column(vals):
    return jnp.concatenate([jnp.full((WINDOW, 1), v, F32) for v in vals], axis=0)


def _swa_logits(qs, kb, dist, valid, grp):
    slopes = _swa_head_column([_alibi_slope(SWA_GROUP * grp + a) for a in range(SWA_GROUP)])
    dist4 = jnp.concatenate([dist] * SWA_GROUP, axis=0)
    valid4 = jnp.concatenate([valid] * SWA_GROUP, axis=0)
    sc = lax.dot_general(qs, kb, _NT, preferred_element_type=F32) * SWA_SCALE - slopes * dist4
    return jnp.where(valid4, sc, NEG_INF)


def _swa_merge_heads(tiles):
    lt64 = lax.broadcasted_iota(jnp.int32, (WINDOW, LANES), 1) < 64
    return jnp.concatenate([jnp.where(lt64, tiles[2 * b], tiles[2 * b + 1]) for b in range(SWA_HEADS // 2)], axis=1)


def _swa_fwd(z0b, sinks, *, name):
    s = z0b.shape[0]
    w = WINDOW
    rows = min(SWA_ROWS, s)
    per_step = rows // w
    qcols = SWA_HEADS * SWA_DIM

    def body(sink_ref, q_ref, k_ref, v_ref, o_ref, lse_ref):
        g = pl.program_id(0)
        for ii in range(per_step):
            rs = slice(ii * w, (ii + 1) * w)
            r0, b0, dist, valid = _swa_geometry(g * per_step + ii)
            kb = k_ref[pl.ds(b0, 2 * w), :]
            vb = v_ref[pl.ds(b0, 2 * w), :]
            o_tiles = []
            for h in range(SWA_HEADS):
                kv = h // SWA_GROUP
                qh = _swa_q_head(q_ref[rs, (h // 2) * LANES:(h // 2 + 1) * LANES].astype(F32), h).astype(BF16)
                sc = lax.dot_general(qh, kb, _NT, preferred_element_type=F32) * SWA_SCALE - _alibi_slope(h) * dist
                sc = jnp.where(valid, sc, NEG_INF)
                sink = sink_ref[0, h]
                m = jnp.maximum(jnp.max(sc, axis=-1, keepdims=True), sink)
                p = jnp.exp(sc - m)
                l = jnp.sum(p, axis=-1, keepdims=True) + jnp.exp(sink - m)
                oh = jnp.dot(p.astype(BF16), vb, preferred_element_type=F32) / l
                o_tiles.append(pltpu.roll(oh, 64, axis=1) if h % 2 != kv else oh)
                lse_ref[h, rs, :] = m + jnp.log(l)
            o_ref[rs, :] = _swa_merge_heads(o_tiles)

    return pl.pallas_call(
        body, name=name, grid=(s // rows,),
        in_specs=[pl.BlockSpec(memory_space=pltpu.SMEM),
                  pl.BlockSpec((rows, qcols), lambda g: (g, 0)),
                  pl.BlockSpec((s, LANES), lambda g: (0, 4)), pl.BlockSpec((s, LANES), lambda g: (0, 5))],
        out_specs=[pl.BlockSpec((rows, qcols), lambda g: (g, 0)), pl.BlockSpec((SWA_HEADS, rows, 1), lambda g: (0, g, 0))],
        out_shape=[jax.ShapeDtypeStruct((s, qcols), F32), jax.ShapeDtypeStruct((SWA_HEADS, s, 1), F32)],
        compiler_params=_params(("parallel",)),
    )(sinks, z0b, z0b, z0b)


def _swa_bwd(z0b, sinks, do, o, lse, *, name):
    s = z0b.shape[0]
    w = WINDOW
    rows = min(SWA_ROWS, s)
    per_step = rows // w
    qcols = SWA_HEADS * SWA_DIM
    nblk = s // w

    def body(sink_ref, q_ref, k_ref, v_ref, do_ref, o_ref, lse_ref, dq_ref, dkt_ref, dvt_ref, dsink_ref):
        g = pl.program_id(0)

        @pl.when(g == 0)
        def _():
            dkt_ref[...] = jnp.zeros_like(dkt_ref)
            dvt_ref[...] = jnp.zeros_like(dvt_ref)
            dsink_ref[...] = jnp.zeros_like(dsink_ref)

        for ii in range(per_step):
            i = g * per_step + ii
            rs = slice(ii * w, (ii + 1) * w)
            r0, b0, dist, valid = _swa_geometry(i)
            j0 = jnp.maximum(i - 1, 0)
            kb = k_ref[pl.ds(b0, 2 * w), :]
            vb = v_ref[pl.ds(b0, 2 * w), :]
            dq_tiles = []
            for grp in range(SWA_KV_HEADS):
                heads = [SWA_GROUP * grp + a for a in range(SWA_GROUP)]
                qs32 = _swa_stack(q_ref, rs, grp)
                dos32 = _swa_stack(do_ref, rs, grp)
                delta = jnp.sum(dos32 * _swa_stack(o_ref, rs, grp), axis=-1, keepdims=True)
                lse = jnp.concatenate([lse_ref[h, rs, :] for h in heads], axis=0)
                sink = _swa_head_column([sink_ref[0, h] for h in heads])
                p = jnp.exp(_swa_logits(qs32.astype(BF16), kb, dist, valid, grp) - lse)
                dp = lax.dot_general(dos32.astype(BF16), vb, _NT, preferred_element_type=F32)
                ds = p * (dp - delta)
                dsb = ds.astype(BF16)
                d_sink = jnp.exp(sink - lse) * delta
                for a, h in enumerate(heads):
                    dsink_ref[h:h + 1, :] += jnp.broadcast_to(-jnp.sum(d_sink[a * w:(a + 1) * w]), (1, LANES))
                dvt = jnp.dot(dos32.T.astype(BF16), p.astype(BF16), preferred_element_type=F32)
                dkt = jnp.dot(qs32.T.astype(BF16), dsb, preferred_element_type=F32) * SWA_SCALE
                dvt_ref[j0] += dvt[:, :w]
                dvt_ref[j0 + 1] += dvt[:, w:]
                dkt_ref[j0] += dkt[:, :w]
                dkt_ref[j0 + 1] += dkt[:, w:]
                dq_tiles += _swa_unstack(jnp.dot(dsb, kb, preferred_element_type=F32) * SWA_SCALE, grp)
            dq_ref[rs, :] = _swa_merge_heads(dq_tiles)

    return pl.pallas_call(
        body, name=name, grid=(s // rows,),
        in_specs=[pl.BlockSpec(memory_space=pltpu.SMEM),
                  pl.BlockSpec((rows, qcols), lambda g: (g, 0)),
                  pl.BlockSpec((s, LANES), lambda g: (0, 4)), pl.BlockSpec((s, LANES), lambda g: (0, 5)),
                  pl.BlockSpec((rows, qcols), lambda g: (g, 0)), pl.BlockSpec((rows, qcols), lambda g: (g, 0)),
                  pl.BlockSpec((SWA_HEADS, rows, 1), lambda g: (0, g, 0))],
        out_specs=[pl.BlockSpec((rows, qcols), lambda g: (g, 0)),
                   pl.BlockSpec((nblk, LANES, w), lambda g: (0, 0, 0)),
                   pl.BlockSpec((nblk, LANES, w), lambda g: (0, 0, 0)),
                   pl.BlockSpec((SWA_HEADS, LANES), lambda g: (0, 0))],
        out_shape=[jax.ShapeDtypeStruct((s, qcols), F32),
                   jax.ShapeDtypeStruct((nblk, LANES, w), F32), jax.ShapeDtypeStruct((nblk, LANES, w), F32),
                   jax.ShapeDtypeStruct((SWA_HEADS, LANES), F32)],
        compiler_params=_params(("arbitrary",)),
    )(sinks, z0b, z0b, z0b, do, o, lse)


CUM_T = 256


def _split3(x):
    hi = x.astype(BF16)
    r1 = x - hi.astype(F32)
    mid = r1.astype(BF16)
    lo = (r1 - mid.astype(F32)).astype(BF16)
    return hi, mid, lo


def _tri_dot(tri, x):
    hi, mid, lo = _split3(x)
    out = jnp.dot(tri, hi, preferred_element_type=F32)
    out = out + jnp.dot(tri, mid, preferred_element_type=F32)
    return out + jnp.dot(tri, lo, preferred_element_type=F32)


def _logf_fwd(zf, bf, *, name):
    s = zf.shape[0]
    t = CUM_T
    nb = s // t

    def body(z_ref, b_ref, c_ref, carry_ref):
        i = pl.program_id(0)

        @pl.when(i == 0)
        def _():
            carry_ref[...] = jnp.zeros_like(carry_ref)

        x = z_ref[...] + b_ref[...]
        lf = jnp.minimum(x, 0.0) - jnp.log(1.0 + jnp.exp(-jnp.abs(x)))
        row = lax.broadcasted_iota(jnp.int32, (t, t), 0)
        col = lax.broadcasted_iota(jnp.int32, (t, t), 1)
        tri = jnp.where(col <= row, 1.0, 0.0).astype(BF16)
        c = _tri_dot(tri, lf) + carry_ref[...]
        c_ref[...] = c
        carry_ref[...] = c[t - 1:t, :]

    return pl.pallas_call(
        body, name=name, grid=(nb,),
        in_specs=[pl.BlockSpec((t, LANES), lambda i: (i, 0)), pl.BlockSpec((1, LANES), lambda i: (0, 0))],
        out_specs=pl.BlockSpec((t, LANES), lambda i: (i, 0)),
        out_shape=jax.ShapeDtypeStruct((s, LANES), F32),
        scratch_shapes=[pltpu.VMEM((1, LANES), F32)],
        compiler_params=_params(("arbitrary",)),
    )(zf, bf)


def _logf_bwd(dc, zf, bf, *, name):
    s = zf.shape[0]
    t = CUM_T
    nb = s // t

    def body(dc_ref, z_ref, b_ref, dz_ref, db_ref, carry_ref):
        i = pl.program_id(0)

        @pl.when(i == 0)
        def _():
            carry_ref[...] = jnp.zeros_like(carry_ref)
            db_ref[...] = jnp.zeros_like(db_ref)

        row = lax.broadcasted_iota(jnp.int32, (t, t), 0)
        col = lax.broadcasted_iota(jnp.int32, (t, t), 1)
        tri = jnp.where(col >= row, 1.0, 0.0).astype(BF16)
        dlf = _tri_dot(tri, dc_ref[...]) + carry_ref[...]
        carry_ref[...] = dlf[0:1, :]
        x = z_ref[...] + b_ref[...]
        dz = dlf * _sigmoid(-x)
        dz_ref[...] = dz.astype(BF16)
        db_ref[...] += jnp.sum(dz, axis=0, keepdims=True)

    return pl.pallas_call(
        body, name=name, grid=(nb,),
        in_specs=[pl.BlockSpec((t, LANES), lambda i: (nb - 1 - i, 0)), pl.BlockSpec((t, LANES), lambda i: (nb - 1 - i, 0)),
                  pl.BlockSpec((1, LANES), lambda i: (0, 0))],
        out_specs=[pl.BlockSpec((t, LANES), lambda i: (nb - 1 - i, 0)), pl.BlockSpec((1, LANES), lambda i: (0, 0))],
        out_shape=[jax.ShapeDtypeStruct((s, LANES), BF16), jax.ShapeDtypeStruct((1, LANES), F32)],
        scratch_shapes=[pltpu.VMEM((1, LANES), F32)],
        compiler_params=_params(("arbitrary",)),
    )(dc, zf, bf)


def _sum_pieces(p_ref):
    g = p_ref[0].astype(F32)
    for k in range(1, N_DEV):
        g = g + p_ref[k].astype(F32)
    return g


def _adam_update(g, w, m, v):
    bc1 = 1.0 - ADAM_B1 ** ADAM_STEP
    bc2 = 1.0 - ADAM_B2 ** ADAM_STEP
    nm = ADAM_B1 * m + (1.0 - ADAM_B1) * g
    nv = ADAM_B2 * v + (1.0 - ADAM_B2) * (g * g)
    m_hat = nm / bc1
    v_hat = nv / bc2
    return -ADAM_LR * (m_hat / (jnp.sqrt(v_hat) + ADAM_EPS) + ADAM_WD * w), nm, nv


def _adamw(pieces, w, m, v, *, name):
    rows, cols = w.shape
    tr = _tile(rows, (RB1, RB0, SMALL_ROWS))

    def body(p_ref, w_ref, m_ref, v_ref, g_ref, d_ref, nm_ref, nv_ref):
        g = _sum_pieces(p_ref)
        g_ref[...] = g
        d_ref[...], nm_ref[...], nv_ref[...] = _adam_update(g, w_ref[...], m_ref[...], v_ref[...])

    spec = pl.BlockSpec((tr, cols), lambda i: (i, 0))
    shape = jax.ShapeDtypeStruct((rows, cols), F32)
    return pl.pallas_call(
        body, name=name, grid=(rows // tr,),
        in_specs=[pl.BlockSpec((N_DEV, tr, cols), lambda i: (0, i, 0)), spec, spec, spec],
        out_specs=[spec, spec, spec, spec], out_shape=[shape, shape, shape, shape],
        compiler_params=_params(("parallel",)),
    )(pieces, w, m, v)


def _sum8(pieces, rows, *, name):
    cols = pieces.shape[2]
    tr = _tile(rows, (176, 96))

    def body(p_ref, g_ref):
        g_ref[...] = _sum_pieces(p_ref)

    return pl.pallas_call(
        body, name=name, grid=(rows // tr,),
        in_specs=[pl.BlockSpec((N_DEV, tr, cols), lambda i: (0, i, 0))],
        out_specs=pl.BlockSpec((tr, cols), lambda i: (i, 0)),
        out_shape=jax.ShapeDtypeStruct((rows, cols), F32),
        compiler_params=_params(("parallel",)),
    )(pieces)


def _adamw_native(g, w, m, v, *, name):
    rows, cols = w.shape
    tr = _tile(rows, (256, 128))

    def body(g_ref, w_ref, m_ref, v_ref, d_ref, nm_ref, nv_ref):
        d_ref[...], nm_ref[...], nv_ref[...] = _adam_update(g_ref[...], w_ref[...], m_ref[...], v_ref[...])

    spec = pl.BlockSpec((tr, cols), lambda i: (i, 0))
    shape = jax.ShapeDtypeStruct((rows, cols), F32)
    return pl.pallas_call(
        body, name=name, grid=(rows // tr,), in_specs=[spec, spec, spec, spec],
        out_specs=[spec, spec, spec], out_shape=[shape, shape, shape],
        compiler_params=_params(("parallel",)),
    )(g, w, m, v)


MESH = pl.DeviceIdType.MESH
ANY = pl.BlockSpec(memory_space=pl.ANY)


def _all_gather(shard, *, name):
    rows, lanes = shard.shape

    def body(x_ref, out_ref, send_sems, recv_sems, local_sem):
        x, y, c = lax.axis_index("x"), lax.axis_index("y"), lax.axis_index("c")
        me, sibling = (x, y, c), (x, y, 1 - c)
        chips = [(1 - x, y), (x, 1 - y), (1 - x, 1 - y)]

        def block(px, py, pc):
            return out_ref.at[4 * px + 2 * py + pc]

        def copy(k, blk, to, src=None):
            return pltpu.make_async_remote_copy(
                src_ref=block(*blk) if src is None else src, dst_ref=block(*blk),
                send_sem=send_sems.at[k], recv_sem=recv_sems.at[k], device_id=to, device_id_type=MESH)

        mine = pltpu.make_async_copy(x_ref, block(*me), local_sem)
        mine.start()
        first = [copy(0, me, sibling, src=x_ref)]
        first += [copy(1 + j, me, (*chip, c), src=x_ref) for j, chip in enumerate(chips)]
        for cp in first:
            cp.start()
        passed = [copy(4 + j, (*chip, c), sibling) for j, chip in enumerate(chips)]
        for j, chip in enumerate(chips):
            copy(1 + j, (*chip, c), me).wait_recv()
            passed[j].start()
        copy(0, sibling, me).wait_recv()
        for j, chip in enumerate(chips):
            copy(4 + j, (*chip, 1 - c), me).wait_recv()
        for cp in first + passed:
            cp.wait_send()
        mine.wait()

    return pl.pallas_call(
        body, name=name, out_shape=jax.ShapeDtypeStruct((N_DEV, rows, lanes), shard.dtype),
        in_specs=[ANY], out_specs=ANY,
        scratch_shapes=[pltpu.SemaphoreType.DMA((7,)), pltpu.SemaphoreType.DMA((7,)), pltpu.SemaphoreType.DMA(())],
    )(shard)


def _peer_copies(kind, src_ref, out_ref, send_sems, recv_sems, local_sem):
    x, y, c = lax.axis_index("x"), lax.axis_index("y"), lax.axis_index("c")
    me = 4 * x + 2 * y + c

    def src(idx):
        return src_ref.at[idx] if kind == "exchange" else src_ref

    mine = pltpu.make_async_copy(src(me), out_ref.at[me], local_sem)
    copies = []
    for r in range(1, N_DEV):
        px = 1 - x if r & 4 else x
        py = 1 - y if r & 2 else y
        pc = 1 - c if r & 1 else c
        copies.append(pltpu.make_async_remote_copy(
            src_ref=src(4 * px + 2 * py + pc), dst_ref=out_ref.at[me],
            send_sem=send_sems.at[r - 1], recv_sem=recv_sems.at[r - 1],
            device_id=(px, py, pc), device_id_type=MESH))
    return mine, copies


PEER_SEMS = [pltpu.SemaphoreType.DMA((7,)), pltpu.SemaphoreType.DMA((7,)), pltpu.SemaphoreType.DMA(())]


def _add_rider(rider, in_specs, args, out_specs, out_shape):
    if rider is None:
        return []
    _, arr = rider
    in_specs.append(ANY)
    args.append(arr)
    out_specs.append(ANY)
    out_shape.append(jax.ShapeDtypeStruct((N_DEV,) + arr.shape[-2:], arr.dtype))
    return list(PEER_SEMS)


def _split_rider(refs, rider, n_in, n_out):
    if rider is None:
        return refs, None
    refs = list(refs)
    rin = refs.pop(n_in)
    rout = refs.pop(n_in + n_out)
    return refs[:-3], (rin, rout, *refs[-3:])


def _ride_start(rider, ride_refs, first):
    if rider is None:
        return

    @pl.when(first)
    def _():
        mine, copies = _peer_copies(rider[0], *ride_refs)
        mine.start()
        for cp in copies:
            cp.start()


def _ride_wait(rider, ride_refs, last):
    if rider is None:
        return

    @pl.when(last)
    def _():
        mine, copies = _peer_copies(rider[0], *ride_refs)
        for cp in copies:
            cp.wait()
        mine.wait()


def _gathered_cols(blocks, kdim):
    n = blocks.shape[1] * WIDE // kdim
    return blocks.reshape(N_DEV, kdim, n).transpose(1, 0, 2).reshape(kdim, N_DEV * n)


def _scatter_cols(dw):
    kdim, n8 = dw.shape
    n = n8 // N_DEV
    return dw.reshape(kdim, N_DEV, n).transpose(1, 0, 2).reshape(N_DEV, kdim * n // WIDE, WIDE)


def _pad_rows(a, rows):
    pad = [(0, 0)] * a.ndim
    pad[-2] = (0, rows - a.shape[-2])
    return jnp.pad(a, pad)


def _layer0_in_weight_t(wt):
    cq, ckv, kpe = wt[0:256], wt[256:384], wt[384:416]
    q_s, k_s, v_s, gate = wt[416:928], wt[928:1056], wt[1056:1184], wt[1184:2208]
    z = jnp.zeros((64, wt.shape[1]), wt.dtype)
    return jnp.concatenate([gate, cq, ckv, z, kpe, z[:32], q_s, k_s, v_s], axis=0)


def _layer0_in_grad_t(dwt):
    gate, cq, ckv, kpe = dwt[0:1024], dwt[1024:1280], dwt[1280:1408], dwt[1472:1504]
    q_s, k_s, v_s = dwt[1536:2048], dwt[2048:2176], dwt[2176:2304]
    return jnp.concatenate([cq, ckv, kpe, q_s, k_s, v_s, gate], axis=0)


def _layer1_in_weight_t(wt):
    main = jnp.concatenate([wt[:3 * D_MODEL], wt[3 * D_MODEL + FOX_HEADS:]], axis=0)
    return main, _pad_rows(wt[3 * D_MODEL:3 * D_MODEL + FOX_HEADS], LANES)


def _layer1_in_grad_t(d_blocks, d_wft):
    return jnp.concatenate([*d_blocks[:3], d_wft[:FOX_HEADS], d_blocks[3]], axis=0)


def _q_up_weight(w):
    return jnp.pad(w.reshape(MLA_Q_RANK, MLA_HEADS, 96), ((0, 0), (0, 0), (0, 32))).reshape(MLA_Q_RANK, MLA_HEADS * LANES)


def _q_up_grad(dwp):
    return dwp.reshape(MLA_Q_RANK, MLA_HEADS, LANES)[:, :, :96].reshape(MLA_Q_RANK, MLA_HEADS * 96)


def _kv_up_weight(w):
    w4 = w.reshape(MLA_KV_RANK, MLA_HEADS, 2, 64)
    kp = jnp.pad(w4[:, :, 0, :], ((0, 0), (0, 0), (0, 64))).reshape(MLA_KV_RANK, MLA_HEADS * LANES)
    vp = w4[:, :, 1, :].reshape(MLA_KV_RANK, MLA_HEADS * 64)
    return jnp.concatenate([kp, vp], axis=1)


def _kv_up_grad(dwp):
    dk = dwp[:, :MLA_HEADS * LANES].reshape(MLA_KV_RANK, MLA_HEADS, LANES)[:, :, :64]
    dv = dwp[:, MLA_HEADS * LANES:].reshape(MLA_KV_RANK, MLA_HEADS, 64)
    return jnp.stack([dk, dv], axis=2).reshape(MLA_KV_RANK, MLA_HEADS * LANES)


def _pad_lanes(a):
    return jnp.pad(a, ((0, 0), (0, LANES - a.shape[1])))


def _small_pack(g_in, g_final, g_q_a, g_kv_a, sinks, b_f, loss):
    rows = [g_in.reshape(8, LANES), g_final.reshape(8, LANES), g_q_a.reshape(2, LANES), g_kv_a.reshape(1, LANES),
            _pad_lanes(sinks.reshape(1, -1)), _pad_lanes(b_f.reshape(1, -1)), _pad_lanes(loss.reshape(1, 1)),
            jnp.zeros((2, LANES), F32)]
    return jnp.concatenate(rows, axis=0)


def _small_unpack(a):
    return (a[0:8].reshape(1, D_MODEL), a[8:16].reshape(D_MODEL), a[16:18].reshape(1, MLA_Q_RANK),
            a[18:19].reshape(1, MLA_KV_RANK), a[19:20, :SWA_HEADS], a[20:21, :FOX_HEADS], a[21, 0])


def _local_step(x, positions, target, e_g_in, w0t, e_g_q_a, wq, e_g_kv_a, wkv, e_sinks,
                late, o_b_f, g_final, scatter1=None, scatter0=None):
    s = x.shape[0]
    mla_scale = (MLA_NOPE + MLA_ROPE) ** -0.5
    fox_scale = FOX_DIM ** -0.5
    n0a = Z0A_UNITS * LANES

    inv_freq = 1.0 / (ROPE_THETA ** (jnp.arange(0, MLA_ROPE, 2, dtype=F32) / MLA_ROPE))
    ang = positions.astype(F32)[:, None] * inv_freq
    cos, sin = jnp.cos(ang), jnp.sin(ang)
    ones, zeros = jnp.ones((s, 64), F32), jnp.zeros((s, 64), F32)
    cos_t = jnp.concatenate([ones, cos, cos, ones[:, :32]], axis=1)
    sin_t = jnp.concatenate([zeros, -sin, sin, zeros[:, :32]], axis=1)

    h0 = _rmsnorm_fwd(x, e_g_in, width=D_MODEL, col_blk=0, name="l0_norm")
    z0a = _matmul(h0, w0t, tb=True, b_rows=(0, n0a), name="l0_in_a")
    z0b = _matmul(h0, w0t, tb=True, b_rows=(n0a, Z0B_UNITS * LANES), name="l0_in_b", out_dtype=BF16)
    cqn = _rmsnorm_fwd(z0a, e_g_q_a, width=MLA_Q_RANK, col_blk=4, name="l0_q_norm")
    ckvn = _rmsnorm_fwd(z0a, e_g_kv_a, width=MLA_KV_RANK, col_blk=10, name="l0_kv_norm")
    qp = _matmul(cqn, wq, name="l0_q_up")
    kvp = _matmul(ckvn, wkv, name="l0_kv_up", out_dtype=BF16)
    qm, km = _rope_fwd(qp, kvp, z0a, cos_t, sin_t, name="l0_rope")
    gathers = len(late) == 2
    res = _flash_fwd(qm, km, kvp, None, n_pairs=MLA_HEADS // 2, hw=LANES, q_off=0, k_off=0, v_off=MLA_HEADS,
                     scale=mla_scale, name="l0_mla_fwd", rider=("gather", late[0]) if gathers else None)
    o_mla, lse_mla = res[0], res[1]
    wo0, o_g_in, w1t, wft, wo1 = late[1](res[2]) if gathers else late
    o_swa, lse_swa = _swa_fwd(z0b, e_sinks, name="l0_swa_fwd")
    og0 = _gate_fwd([o_mla, o_swa], z0a, name="l0_gate")

    x1, h1 = _matmul_rows([(og0, wo0, False)], [(x, D_MODEL, 0)], [o_g_in], _residual_norm_epilogue,
                          [("rows", D_MODEL, F32), ("rows", D_MODEL, BF16)], name="l0_out")
    z1 = _matmul(h1, w1t, tb=True, b_rows=(0, 3 * D_MODEL), name="l1_in_qkv", out_dtype=BF16)
    gate1 = _matmul(h1, w1t, tb=True, b_rows=(3 * D_MODEL, D_MODEL), name="l1_in_gate")
    zf = _matmul(h1, wft, tb=True, name="l1_in_f")
    bf = _pad_lanes(o_b_f)
    log_cum = _logf_fwd(zf, bf, name="l1_logf")
    bias2 = (-LOG2E * log_cum[:, :FOX_HEADS]).T
    t_bwd = min(ATT_T, s)
    bias = bias2.reshape(FOX_HEADS // 2, 2, s // t_bwd, 1, t_bwd)
    t_fwd = _fwd_tile(s)
    o_fox, lse_fox = _flash_fwd(z1, z1, z1, bias2.reshape(FOX_HEADS // 2, 2, s // t_fwd, 1, t_fwd),
                                n_pairs=FOX_HEADS // 2, hw=64, q_off=0, k_off=8, v_off=16, scale=fox_scale,
                                name="l1_fox_fwd")
    og1 = _gate_fwd([o_fox], gate1, name="l1_gate")

    dx2, loss_part, d_g_final = _matmul_rows(
        [(og1, wo1, False)], [(x1, D_MODEL, 0), (target, D_MODEL, 0)], [g_final.reshape(1, D_MODEL)], _loss_epilogue,
        [("rows", D_MODEL, F32), ("sum", (8, LANES)), ("sum", (1, D_MODEL))], name="l1_out_loss")

    d_wo1 = _matmul(og1, dx2, ta=True, name="l1_out_dw")
    do_fox, d_gate1 = _matmul_rows([(dx2, wo1, True)], [(o_fox, D_MODEL, 0), (gate1, D_MODEL, 0)], [],
                                   _gate_bwd_epilogue([D_MODEL]), [("rows", D_MODEL, F32), ("rows", D_MODEL, BF16)],
                                   name="l1_out_dx")
    dq1, dk1, dv1, dbias, drow = _flash_bwd(z1, z1, z1, do_fox, o_fox, lse_fox, bias, n_pairs=FOX_HEADS // 2, hw=64,
                                            q_off=0, k_off=8, v_off=16, scale=fox_scale, qk_dtype=BF16,
                                            name="l1_fox_bwd")
    d_log_cum = (drow.reshape(FOX_HEADS, s) - dbias.reshape(FOX_HEADS, s)).T
    d_log_cum = jnp.pad(d_log_cum, ((0, 0), (0, LANES - FOX_HEADS)))
    d_zf, d_bf = _logf_bwd(d_log_cum, zf, bf, name="l1_logf_bwd")
    dz1 = (dq1, dk1, dv1, d_gate1)
    d_w1t = tuple(_matmul(d, h1, ta=True, name=f"l1_in_dw_{k}") for k, d in enumerate(dz1))
    d_wft = _matmul(d_zf, h1, ta=True, name="l1_in_f_dw")
    dx1, d_o_g_in = _matmul_rows([(d, w1t, False, k * D_MODEL) for k, d in enumerate(dz1)] + [(d_zf, wft, False)],
                                 [(x1, D_MODEL, 0), (dx2, D_MODEL, 0)],
                                 [o_g_in], _rms_bwd_epilogue, [("rows", D_MODEL, F32), ("sum", (1, D_MODEL))],
                                 name="l1_in_dx")

    d_wo0 = _matmul(og0, dx1, ta=True, name="l0_out_dw")
    half = D_MODEL // 2
    do_mla, do_swa, d_gate0 = _matmul_rows(
        [(dx1, wo0, True)], [(o_mla, half, 0), (o_swa, half, 0), (z0a, D_MODEL, 0)], [], _gate_bwd_epilogue([half, half]),
        [("rows", half, F32), ("rows", half, F32), ("rows", D_MODEL, BF16)], name="l0_out_dx")
    dq_s, dkt_s, dvt_s, d_sinks = _swa_bwd(z0b, e_sinks, do_swa, o_swa, lse_swa, name="l0_swa_bwd")
    dk_s = dkt_s.transpose(0, 2, 1).reshape(s, LANES)
    dv_s = dvt_s.transpose(0, 2, 1).reshape(s, LANES)
    rider = None
    if scatter1 is not None:
        rider = ("exchange", scatter1(dict(w1t=d_w1t, wft=d_wft, wo1=d_wo1, o_g_in=d_o_g_in, wo0=d_wo0)))
    res = _flash_bwd(qm, km, kvp, do_mla, o_mla, lse_mla, None, n_pairs=MLA_HEADS // 2, hw=LANES, q_off=0, k_off=0,
                     v_off=MLA_HEADS, scale=mla_scale, qk_dtype=F32, name="l0_mla_bwd", rider=rider)
    dqm, dkm, dvm = res[0], res[1], res[2]
    recv1 = res[3] if rider is not None else None
    d_qp, d_kvp, d_kpe = _rope_bwd(dqm, dkm, dvm, cos_t, sin_t, name="l0_rope_bwd")
    d_wq = _matmul(cqn, d_qp, ta=True, name="l0_q_up_dw")
    d_cqn = _matmul(d_qp, wq, tb=True, name="l0_q_up_dx")
    d_wkv = _matmul(ckvn, d_kvp, ta=True, name="l0_kv_up_dw")
    d_ckvn = _matmul(d_kvp, wkv, tb=True, name="l0_kv_up_dx")
    d_cq, d_g_q_a = _rmsnorm_bwd(z0a, e_g_q_a, d_cqn, width=MLA_Q_RANK, col_blk=4, name="l0_q_norm_bwd")
    d_ckv, d_g_kv_a = _rmsnorm_bwd(z0a, e_g_kv_a, d_ckvn, width=MLA_KV_RANK, col_blk=10, name="l0_kv_norm_bwd")
    dz0 = jnp.concatenate([d_gate0, d_cq, d_ckv, d_kpe, dq_s.astype(BF16), dk_s.astype(BF16), dv_s.astype(BF16)], axis=1)
    d_w0t = _matmul(dz0, h0, ta=True, name="l0_in_dw")
    rider = None if scatter0 is None else ("exchange", scatter0(dict(w0t=d_w0t, wq=d_wq, wkv=d_wkv)))
    res = _matmul_rows([(dz0, w0t, False)], [(x, D_MODEL, 0), (dx1, D_MODEL, 0)], [e_g_in], _rms_bwd_epilogue,
                       [("rows", D_MODEL, F32), ("sum", (1, D_MODEL))], name="l0_in_dx", rider=rider)
    grad_x, d_e_g_in = res[0], res[1]
    recv0 = res[2] if rider is not None else None

    return dict(recv0=recv0, recv1=recv1, loss=loss_part[0, 0], grad_x=grad_x, e_g_in=d_e_g_in, w0t=d_w0t, e_g_q_a=d_g_q_a, wq=d_wq,
                e_g_kv_a=d_g_kv_a, wkv=d_wkv, e_sinks=d_sinks[:, 0].reshape(1, SWA_HEADS), wo0=d_wo0,
                o_g_in=d_o_g_in, w1t=d_w1t, wft=d_wft, o_b_f=d_bf[:, :FOX_HEADS], wo1=d_wo1, g_final=d_g_final.reshape(D_MODEL))


def _wide(a, rows):
    flat = a.reshape(-1)
    return jnp.pad(flat, (0, rows * WIDE - flat.shape[0])).reshape(rows, WIDE)


def _rows_b0(w_q, w_kv):
    return jnp.concatenate([_wide(w_q, 32), _wide(w_kv, 16)], axis=0)


def _unflat_b0(f):
    return f[0:24].reshape(1, MLA_Q_RANK, 96), f[32:48].reshape(1, MLA_KV_RANK, 128)


def _rows_b1(o_w_out, e_w_out, g_in):
    return jnp.concatenate([o_w_out, e_w_out, _wide(g_in, 16)], axis=0)


def _unflat_b1(f):
    return f[0:128][None], f[128:256][None], f[256:257, :LANES]


def kernel(x, positions, e_g_in, e_w_in, e_g_q_a, e_w_q_up, e_g_kv_a, e_w_kv_up, e_sinks, e_w_out, o_g_in, o_w_in, o_b_f, o_w_out, g_final, loss_target, m_e_g_in, m_e_w_in, m_e_g_q_a, m_e_w_q_up, m_e_g_kv_a, m_e_w_kv_up, m_e_sinks, m_e_w_out, m_o_g_in, m_o_w_in, m_o_b_f, m_o_w_out, m_g_final, v_e_g_in, v_e_w_in, v_e_g_q_a, v_e_w_q_up, v_e_g_kv_a, v_e_w_kv_up, v_e_sinks, v_e_w_out, v_o_g_in, v_o_w_in, v_o_b_f, v_o_w_out, v_g_final):
    def bf(a):
        return a.astype(BF16)

    shard0 = jnp.concatenate([_pad_rows(bf(e_w_in[0]).T, RA0), _rows_b0(bf(e_w_q_up[0]), bf(e_w_kv_up[0]))], axis=0)
    gath0 = _all_gather(shard0, name="weights0_all_gather")
    w0t = _layer0_in_weight_t(gath0[:, :N_E_IN].reshape(N_DEV * N_E_IN, WIDE))
    wq = _q_up_weight(_gathered_cols(gath0[:, RA0:RA0 + 24], MLA_Q_RANK))
    wkv = _kv_up_weight(_gathered_cols(gath0[:, RA0 + 32:RA0 + 48], MLA_KV_RANK))

    g_bits = lax.bitcast_convert_type(o_g_in.reshape(LANES), BF16)
    shard1 = jnp.concatenate([_pad_rows(bf(o_w_in[0]).T, RA1), _rows_b1(bf(o_w_out[0]), bf(e_w_out[0]), g_bits)], axis=0)

    def unpack1(gath1):
        w1t, wft = _layer1_in_weight_t(gath1[:, :N_O_IN].reshape(N_DEV * N_O_IN, WIDE))
        wo1 = gath1[:, RA1:RA1 + 128].reshape(D_MODEL, D_MODEL)
        wo0 = gath1[:, RA1 + 128:RA1 + 256].reshape(D_MODEL, D_MODEL)
        bits = gath1[:, RA1 + 256, :2 * LANES].reshape(N_DEV, LANES, 2)
        return wo0, lax.bitcast_convert_type(bits, F32).reshape(1, D_MODEL), w1t, wft, wo1

    def scatter1(g):
        d_in_t = _layer1_in_grad_t(g["w1t"], g["wft"]).reshape(N_DEV, N_O_IN, WIDE)
        d_o_g = jnp.pad(g["o_g_in"].reshape(N_DEV, 1, LANES), ((0, 0), (0, 15), (0, WIDE - LANES)))
        return jnp.concatenate([_pad_rows(d_in_t, RA1), g["wo1"].reshape(N_DEV, 128, WIDE),
                                g["wo0"].reshape(N_DEV, 128, WIDE), d_o_g], axis=1).astype(BF16)

    def scatter0(g):
        return jnp.concatenate([
            _pad_rows(_layer0_in_grad_t(g["w0t"]).reshape(N_DEV, N_E_IN, WIDE), RA0),
            _pad_rows(_scatter_cols(_q_up_grad(g["wq"])), 32), _scatter_cols(_kv_up_grad(g["wkv"]))], axis=1).astype(BF16)

    gr = _local_step(x[0], positions[0], loss_target[0], e_g_in, w0t, e_g_q_a, wq, e_g_kv_a, wkv, e_sinks,
                     (shard1, unpack1), o_b_f, g_final, scatter1=scatter1, scatter0=scatter0)
    recv0 = gr["recv0"]

    def in_projection(recv, ra, n, w, m, v, name):
        g = _sum8(recv, ra, name=name + "_grad_sum")[:n].T
        d, nm, nv = _adamw_native(g, w[0], m[0], v[0], name=name + "_adamw")
        return g[None], d[None], nm[None], nv[None]

    e_in = in_projection(recv0, RA0, N_E_IN, e_w_in, m_e_w_in, v_e_w_in, "e_w_in")
    o_in = in_projection(gr["recv1"], RA1, N_O_IN, o_w_in, m_o_w_in, v_o_w_in, "o_w_in")
    b0 = _adamw(recv0[:, RA0:], _rows_b0(e_w_q_up[0], e_w_kv_up[0]), _rows_b0(m_e_w_q_up[0], m_e_w_kv_up[0]),
                _rows_b0(v_e_w_q_up[0], v_e_w_kv_up[0]), name="adamw_early")
    b1 = _adamw(gr["recv1"][:, RA1:], _rows_b1(o_w_out[0], e_w_out[0], o_g_in),
                _rows_b1(m_o_w_out[0], m_e_w_out[0], m_o_g_in), _rows_b1(v_o_w_out[0], v_e_w_out[0], v_o_g_in),
                name="adamw_late")

    def sharded(k):
        q_up, kv_up = _unflat_b0(b0[k])
        o_out, e_out, o_g = _unflat_b1(b1[k])
        return e_in[k], q_up, kv_up, e_out, o_in[k], o_out, o_g

    g_sh, d_sh, m_sh, v_sh = [sharded(k) for k in range(4)]

    small = _small_pack(gr["e_g_in"], gr["g_final"], gr["e_g_q_a"], gr["e_g_kv_a"], gr["e_sinks"], gr["o_b_f"], gr["loss"])
    small_all = _all_gather(small, name="small_all_gather")
    zero = jnp.zeros((), F32)
    w_small = _small_pack(e_g_in, g_final, e_g_q_a, e_g_kv_a, e_sinks, o_b_f, zero)
    m_small = _small_pack(m_e_g_in, m_g_final, m_e_g_q_a, m_e_g_kv_a, m_e_sinks, m_o_b_f, zero)
    v_small = _small_pack(v_e_g_in, v_g_final, v_e_g_q_a, v_e_g_kv_a, v_e_sinks, v_o_b_f, zero)
    smalls = _adamw(small_all, w_small, m_small, v_small, name="adamw_replicated")
    g_sm, d_sm, m_sm, v_sm = [_small_unpack(a) for a in smalls]
    loss = g_sm[6]

    def leaves(sh, sm):
        return (sm[0], sh[0], sm[2], sh[1], sm[3], sh[2], sm[4], sh[3], sh[6], sh[4], sm[5], sh[5], sm[1])

    return (loss, gr["grad_x"][None], *leaves(g_sh, g_sm), *leaves(d_sh, d_sm), *leaves(m_sh, m_sm), *leaves(v_sh, v_sm))
```

```python
import functools

import jax
import jax.numpy as jnp
from jax import lax
from jax.experimental import pallas as pl
from jax.experimental.pallas import tpu as pltpu

F32 = jnp.float32
BF16 = jnp.bfloat16
NEG_INF = float("-inf")

N_DEV = 8
LANES = 128
D_MODEL = 1024
EPS = 1e-6
ROPE_THETA = 10000.0
MLA_HEADS = 8
MLA_Q_RANK = 256
MLA_KV_RANK = 128
MLA_NOPE = 64
MLA_ROPE = 32
MLA_V = 64
SWA_HEADS = 8
SWA_KV_HEADS = 2
SWA_DIM = 64
WINDOW = 128
FOX_HEADS = 16
FOX_DIM = 64

ADAM_LR = 0.001
ADAM_B1 = 0.9
ADAM_B2 = 0.999
ADAM_EPS = 1e-08
ADAM_WD = 0.01
ADAM_STEP = 10

ATT_T = 512
ATT_T_FWD = 1024
VMEM_LIMIT = 56 * 1024 * 1024
MATMUL_B_BLOCK_BYTES = 8 * 1024 * 1024

Z0A_UNITS = 12
Z0B_UNITS = 6

WIDE = 1024
N_E_IN = 276
N_O_IN = 514
RA0 = 288
RB0 = 32 + 16
RA1 = 528
RB1 = 128 + 128 + 16
SMALL_ROWS = 24


def _tile(n, cands):
    for c in cands:
        if n % c == 0:
            return c
    raise ValueError(f"no tile for {n}")


ROW_TILES = (512, 256, 128)


def _params(sem, vmem=VMEM_LIMIT):
    return pltpu.CompilerParams(dimension_semantics=sem, vmem_limit_bytes=vmem)


def _matmul(a, b, *, name, ta=False, tb=False, out_dtype=F32, b_rows=None):
    if ta:
        kdim, m = a.shape
    else:
        m, kdim = a.shape
    if tb:
        n, kb = b.shape
    else:
        kb, n = b.shape
    assert kdim == kb, (a.shape, b.shape)
    b_start = 0
    if b_rows is not None:
        assert tb
        b_start, n = b_rows
    tm = _tile(m, (512, 256, 128))
    tn = _tile(n, [c for c in (1024, 768, 512, 384, 256, 128)
                   if c * kdim * b.dtype.itemsize <= MATMUL_B_BLOCK_BYTES and b_start % c == 0])
    assert b_start % tn == 0, (b_start, tn)
    b_off = b_start // tn
    dims = (((0 if ta else 1,), (1 if tb else 0,)), ((), ()))

    def body(a_ref, b_ref, o_ref):
        r = lax.dot_general(a_ref[...].astype(BF16), b_ref[...].astype(BF16), dims, preferred_element_type=F32)
        o_ref[...] = r.astype(out_dtype)

    a_spec = pl.BlockSpec((kdim, tm), lambda i, j: (0, i)) if ta else pl.BlockSpec((tm, kdim), lambda i, j: (i, 0))
    b_spec = pl.BlockSpec((tn, kdim), lambda i, j: (j + b_off, 0)) if tb else pl.BlockSpec((kdim, tn), lambda i, j: (0, j))
    return pl.pallas_call(
        body, name=name, grid=(m // tm, n // tn), in_specs=[a_spec, b_spec],
        out_specs=pl.BlockSpec((tm, tn), lambda i, j: (i, j)), out_shape=jax.ShapeDtypeStruct((m, n), out_dtype),
        compiler_params=_params(("parallel", "parallel")),
    )(a, b)


def _rmsnorm_fwd(x, g, *, width, col_blk, name):
    s = x.shape[0]
    tm = _tile(s, ROW_TILES)

    def body(x_ref, g_ref, y_ref):
        xf = x_ref[...].astype(F32)
        r = lax.rsqrt(jnp.mean(xf * xf, axis=-1, keepdims=True) + EPS)
        y_ref[...] = ((xf * r) * g_ref[...]).astype(BF16)

    return pl.pallas_call(
        body, name=name, grid=(s // tm,),
        in_specs=[pl.BlockSpec((tm, width), lambda i: (i, col_blk)), pl.BlockSpec((1, width), lambda i: (0, 0))],
        out_specs=pl.BlockSpec((tm, width), lambda i: (i, 0)),
        out_shape=jax.ShapeDtypeStruct((s, width), BF16),
        compiler_params=_params(("parallel",)),
    )(x, g)


def _rmsnorm_bwd(x, g, dy, *, width, col_blk, name):
    s = x.shape[0]
    tm = _tile(s, ROW_TILES)

    def body(x_ref, g_ref, dy_ref, dx_ref, dg_ref):
        @pl.when(pl.program_id(0) == 0)
        def _():
            dg_ref[...] = jnp.zeros_like(dg_ref)

        dx, dg = _rms_bwd_epilogue(dy_ref[...], x_ref[...], 0.0, g_ref[...])
        dg_ref[...] += dg
        dx_ref[...] = dx.astype(BF16)

    return pl.pallas_call(
        body, name=name, grid=(s // tm,),
        in_specs=[pl.BlockSpec((tm, width), lambda i: (i, col_blk)), pl.BlockSpec((1, width), lambda i: (0, 0)),
                  pl.BlockSpec((tm, width), lambda i: (i, 0))],
        out_specs=[pl.BlockSpec((tm, width), lambda i: (i, 0)), pl.BlockSpec((1, width), lambda i: (0, 0))],
        out_shape=[jax.ShapeDtypeStruct((s, width), BF16), jax.ShapeDtypeStruct((1, width), F32)],
        compiler_params=_params(("arbitrary",)),
    )(x, g, dy)


def _sigmoid(x):
    return 1.0 / (1.0 + jnp.exp(-x))


def _gate_fwd(o_parts, gate, *, name):
    s = gate.shape[0]
    tm = _tile(s, ROW_TILES)
    n_o = len(o_parts)

    def body(*refs):
        o_refs, g_ref, y_ref = refs[:n_o], refs[n_o], refs[n_o + 1]
        o = o_refs[0][...] if n_o == 1 else jnp.concatenate([r[...] for r in o_refs], axis=1)
        gt = g_ref[...]
        y_ref[...] = (o * (gt * _sigmoid(gt))).astype(BF16)

    in_specs = [pl.BlockSpec((tm, o.shape[1]), lambda i: (i, 0)) for o in o_parts]
    in_specs.append(pl.BlockSpec((tm, D_MODEL), lambda i: (i, 0)))
    return pl.pallas_call(
        body, name=name, grid=(s // tm,), in_specs=in_specs,
        out_specs=pl.BlockSpec((tm, D_MODEL), lambda i: (i, 0)),
        out_shape=jax.ShapeDtypeStruct((s, D_MODEL), BF16),
        compiler_params=_params(("parallel",)),
    )(*o_parts, gate)


def _matmul_rows(terms, row_inputs, params, epilogue, outs, *, name, rider=None):
    s = terms[0][0].shape[0]
    tm = _tile(s, ROW_TILES)
    steps = s // tm
    n_t, n_r, n_p, n_o = len(terms), len(row_inputs), len(params), len(outs)

    def body(*refs):
        refs, ride_refs = _split_rider(refs, rider, n_in=2 * n_t + n_r + n_p, n_out=n_o)
        t_refs, r_refs = refs[:2 * n_t], refs[2 * n_t:2 * n_t + n_r]
        p_refs, o_refs = refs[2 * n_t + n_r:2 * n_t + n_r + n_p], refs[2 * n_t + n_r + n_p:]
        i = pl.program_id(0)
        _ride_start(rider, ride_refs, i == 0)
        acc = None
        for k, term in enumerate(terms):
            dims = (((1,), (1 if term[2] else 0,)), ((), ()))
            part = lax.dot_general(t_refs[2 * k][...].astype(BF16), t_refs[2 * k + 1][...].astype(BF16), dims,
                                   preferred_element_type=F32)
            acc = part if acc is None else acc + part
        vals = epilogue(acc, *[r[...] for r in r_refs], *[p[...] for p in p_refs])
        for ref, val, out in zip(o_refs, vals, outs):
            if out[0] == "rows":
                ref[...] = val.astype(ref.dtype)
            else:
                @pl.when(i == 0)
                def _(ref=ref):
                    ref[...] = jnp.zeros_like(ref)

                ref[...] += val
        _ride_wait(rider, ride_refs, i == steps - 1)

    in_specs, args = [], []
    for term in terms:
        a, b = term[0], term[1]
        b_rows = b.shape[0] if term[2] or len(term) < 4 else a.shape[1]
        b_blk = 0 if len(term) < 4 else term[3] // b_rows
        in_specs += [pl.BlockSpec((tm, a.shape[1]), lambda i: (i, 0)),
                     _resident((b_rows, b.shape[1]), lambda i, b_blk=b_blk: (b_blk, 0))]
        args += [a, b]
    for arr, width, col_blk in row_inputs:
        in_specs.append(pl.BlockSpec((tm, width), lambda i, col_blk=col_blk: (i, col_blk)))
        args.append(arr)
    for p in params:
        in_specs.append(pl.BlockSpec(p.shape, lambda i: (0, 0)))
        args.append(p)
    out_specs, out_shape = [], []
    for out in outs:
        if out[0] == "rows":
            out_specs.append(pl.BlockSpec((tm, out[1]), lambda i: (i, 0)))
            out_shape.append(jax.ShapeDtypeStruct((s, out[1]), out[2]))
        else:
            out_specs.append(pl.BlockSpec(out[1], lambda i: (0, 0)))
            out_shape.append(jax.ShapeDtypeStruct(out[1], F32))
    scratch = _add_rider(rider, in_specs, args, out_specs, out_shape)
    return pl.pallas_call(
        body, name=name, grid=(steps,), in_specs=in_specs, out_specs=out_specs, out_shape=out_shape,
        scratch_shapes=scratch, compiler_params=_params(("arbitrary",)),
    )(*args)


def _rms_stats(x):
    r = lax.rsqrt(jnp.mean(x * x, axis=-1, keepdims=True) + EPS)
    return r, x * r


def _residual_norm_epilogue(r, x, g):
    x1 = x + r
    _, xh = _rms_stats(x1)
    return x1, xh * g


def _rms_bwd_epilogue(dy, x, add, g):
    r, xh = _rms_stats(x)
    dxh = dy * g
    dx = r * (dxh - xh * jnp.mean(dxh * xh, axis=-1, keepdims=True)) + add
    return dx, jnp.sum(dy * xh, axis=0, keepdims=True)


def _loss_epilogue(r, x1, target, g):
    rs, xh = _rms_stats(x1 + r)
    err = xh * g - target
    loss = jnp.broadcast_to(0.5 * jnp.sum(jnp.mean(err * err, axis=-1, keepdims=True)), (8, LANES))
    dy = err * (1.0 / D_MODEL)
    dxh = dy * g
    dx = rs * (dxh - xh * jnp.mean(dxh * xh, axis=-1, keepdims=True))
    return dx, loss, jnp.sum(dy * xh, axis=0, keepdims=True)


def _gate_bwd_epilogue(widths):
    def epilogue(d, *rows):
        o_parts, gt = rows[:-1], rows[-1]
        o = o_parts[0] if len(o_parts) == 1 else jnp.concatenate(o_parts, axis=1)
        sg = _sigmoid(gt)
        do = d * (gt * sg)
        d_gate = d * o * (sg * (1.0 + gt * (1.0 - sg)))
        cuts = [sum(widths[:k]) for k in range(len(widths) + 1)]
        return tuple(do[:, cuts[k]:cuts[k + 1]] for k in range(len(widths))) + (d_gate,)

    return epilogue


def _rot_half(x):
    lane = lax.broadcasted_iota(jnp.int32, x.shape, 1)
    return jnp.where(lane < 80, pltpu.roll(x, LANES - 16, axis=1), pltpu.roll(x, 16, axis=1))


def _rot_half_t(g):
    lane = lax.broadcasted_iota(jnp.int32, g.shape, 1)
    lo = (lane >= MLA_NOPE) & (lane < MLA_NOPE + MLA_ROPE // 2)
    hi = (lane >= MLA_NOPE + MLA_ROPE // 2) & (lane < MLA_NOPE + MLA_ROPE)
    return jnp.where(lo, pltpu.roll(g, LANES - 16, axis=1), jnp.where(hi, pltpu.roll(g, 16, axis=1), 0.0))


def _rope_fwd(qp, kvp, z0a, cos_t, sin_t, *, name):
    s = qp.shape[0]
    tm = _tile(s, ROW_TILES)
    hw = MLA_HEADS * LANES

    def body(q_ref, k_ref, kpe_ref, c_ref, s_ref, qm_ref, km_ref):
        c = c_ref[...]
        sn = s_ref[...]
        kpe = kpe_ref[...]
        kpe_r = (kpe * c + _rot_half(kpe) * sn).astype(BF16)
        lane = lax.broadcasted_iota(jnp.int32, kpe.shape, 1)
        for h in range(MLA_HEADS):
            sl = slice(h * LANES, (h + 1) * LANES)
            qh = q_ref[:, sl]
            qm_ref[:, sl] = (qh * c + _rot_half(qh) * sn).astype(BF16)
            km_ref[:, sl] = jnp.where(lane < MLA_NOPE, k_ref[:, sl], kpe_r)

    return pl.pallas_call(
        body, name=name, grid=(s // tm,),
        in_specs=[pl.BlockSpec((tm, hw), lambda i: (i, 0)), pl.BlockSpec((tm, hw), lambda i: (i, 0)),
                  pl.BlockSpec((tm, LANES), lambda i: (i, 11)),
                  pl.BlockSpec((tm, LANES), lambda i: (i, 0)), pl.BlockSpec((tm, LANES), lambda i: (i, 0))],
        out_specs=[pl.BlockSpec((tm, hw), lambda i: (i, 0)), pl.BlockSpec((tm, hw), lambda i: (i, 0))],
        out_shape=[jax.ShapeDtypeStruct((s, hw), BF16), jax.ShapeDtypeStruct((s, hw), BF16)],
        compiler_params=_params(("parallel",)),
    )(qp, kvp, z0a, cos_t, sin_t)


def _rope_bwd(dqm, dkm, dvm, cos_t, sin_t, *, name):
    s = dqm.shape[0]
    tm = _tile(s, ROW_TILES)
    hw = MLA_HEADS * LANES
    vw = MLA_HEADS * MLA_V

    def body(dq_ref, dk_ref, dv_ref, c_ref, s_ref, dqp_ref, dkv_ref, dkpe_ref):
        c = c_ref[...]
        sn = s_ref[...]
        ksum = jnp.zeros((tm, LANES), F32)
        for h in range(MLA_HEADS):
            sl = slice(h * LANES, (h + 1) * LANES)
            dq = dq_ref[:, sl]
            dqp_ref[:, sl] = (dq * c + _rot_half_t(dq * sn)).astype(BF16)
            dk = dk_ref[:, sl]
            dkv_ref[:, sl] = dk.astype(BF16)
            ksum = ksum + dk
        dkv_ref[:, hw:] = dv_ref[...]
        lane = lax.broadcasted_iota(jnp.int32, ksum.shape, 1)
        dkpe = ksum * c + _rot_half_t(ksum * sn)
        dkpe_ref[...] = jnp.where((lane >= MLA_NOPE) & (lane < MLA_NOPE + MLA_ROPE), dkpe, 0.0).astype(BF16)

    return pl.pallas_call(
        body, name=name, grid=(s // tm,),
        in_specs=[pl.BlockSpec((tm, hw), lambda i: (i, 0)), pl.BlockSpec((tm, hw), lambda i: (i, 0)),
                  pl.BlockSpec((tm, vw), lambda i: (i, 0)),
                  pl.BlockSpec((tm, LANES), lambda i: (i, 0)), pl.BlockSpec((tm, LANES), lambda i: (i, 0))],
        out_specs=[pl.BlockSpec((tm, hw), lambda i: (i, 0)), pl.BlockSpec((tm, hw + vw), lambda i: (i, 0)),
                   pl.BlockSpec((tm, LANES), lambda i: (i, 0))],
        out_shape=[jax.ShapeDtypeStruct((s, hw), BF16), jax.ShapeDtypeStruct((s, hw + vw), BF16),
                   jax.ShapeDtypeStruct((s, LANES), BF16)],
        compiler_params=_params(("parallel",)),
    )(dqm, dkm, dvm, cos_t, sin_t)


def _head_mask(shape, a):
    lane = lax.broadcasted_iota(jnp.int32, shape, 1)
    return (lane >= 64 * a) & (lane < 64 * (a + 1))


_NT = (((1,), (1,)), ((), ()))
LOG2E = 1.4426950408889634


def _stack_heads(tile, hw):
    lane = lax.broadcasted_iota(jnp.int32, tile.shape, 1)
    z = jnp.zeros_like(tile)
    return jnp.concatenate([jnp.where(lane < hw, tile, z), jnp.where(lane >= hw, tile, z)], axis=0)


def _stacked_rows(r0, r1, t):
    n = r0.shape[-1]
    return jnp.concatenate([jnp.broadcast_to(r0, (t, n)), jnp.broadcast_to(r1, (t, n))], axis=0)


def _resident(block, index_map):
    return pl.BlockSpec(block, index_map, pipeline_mode=pl.Buffered(1))


def _fwd_tile(s):
    return ATT_T_FWD if s % ATT_T_FWD == 0 else min(ATT_T, s)


def _flash_fwd(q, k, v, bias, *, n_pairs, hw, q_off, k_off, v_off, scale, name, rider=None):
    s = q.shape[0]
    t = _fwd_tile(s)
    nb = s // t
    qw = 2 * hw
    has_bias = bias is not None
    c1 = scale * LOG2E

    def body(*refs):
        refs, ride_refs = _split_rider(refs, rider, n_in=4 if has_bias else 3, n_out=2)
        if has_bias:
            q_ref, k_ref, v_ref, b_ref, o_ref, lse_ref, vt_ref, bcol_ref = refs
        else:
            q_ref, k_ref, v_ref, o_ref, lse_ref, vt_ref = refs
            b_ref = bcol_ref = None
        _ride_start(rider, ride_refs, pl.program_id(0) == 0)
        row = lax.broadcasted_iota(jnp.int32, (t, t), 0)
        col = lax.broadcasted_iota(jnp.int32, (t, t), 1)
        cmask_t = jnp.concatenate([row <= col, row <= col], axis=1)
        lane_lt64 = lax.broadcasted_iota(jnp.int32, (t, LANES), 1) < 64

        def as_column(r):
            return jnp.broadcast_to(r, (8, r.shape[1])).T[:, 0:1]

        def v_block(j, _):
            c0 = pl.multiple_of(j * t, t)
            vt_ref[j] = v_ref[pl.ds(c0, t), :].astype(F32).T.astype(BF16)
            if has_bias:
                for a in range(2):
                    bcol_ref[a, pl.ds(c0, t), :] = as_column(b_ref[0, a, j])
            return 0

        lax.fori_loop(0, nb, v_block, 0)

        def stacked_queries(i):
            return _stack_heads(q_ref[pl.ds(pl.multiple_of(i * t, t), t), :], hw).astype(F32).T.astype(BF16)

        def kv_step(j, carry, qs_t, masked):
            m, l, acc = carry
            rows = pl.ds(pl.multiple_of(j * t, t), t)
            sc = jnp.dot(k_ref[rows, :], qs_t, preferred_element_type=F32) * c1
            if has_bias:
                sc = sc + jnp.concatenate([jnp.broadcast_to(bcol_ref[0, rows, :], (t, t)),
                                           jnp.broadcast_to(bcol_ref[1, rows, :], (t, t))], axis=1)
            if masked:
                sc = jnp.where(cmask_t, sc, NEG_INF)
            m_new = jnp.maximum(m, jnp.max(sc, axis=0, keepdims=True))
            alpha = jnp.exp2(m - m_new)
            p = jnp.exp2(sc - m_new)
            l_new = alpha * l + jnp.sum(p, axis=0, keepdims=True)
            pv = jnp.dot(vt_ref[j], p.astype(BF16), preferred_element_type=F32)
            return m_new, l_new, alpha * acc + pv

        def finish(i, carry):
            m, l, acc = carry
            r0 = pl.multiple_of(i * t, t)
            out = (acc / l).T
            lse2 = as_column(m + jnp.log2(l))
            lse_ref[0, 0, pl.ds(r0, t), :] = lse2[:t]
            lse_ref[0, 1, pl.ds(r0, t), :] = lse2[t:]
            o_ref[pl.ds(r0, t), :] = jnp.where(lane_lt64, out[:t], out[t:])

        init = (jnp.full((1, 2 * t), NEG_INF, F32), jnp.zeros((1, 2 * t), F32), jnp.zeros((LANES, 2 * t), F32))

        def q_block(i, _):
            qs_t = stacked_queries(i)
            carry = lax.fori_loop(0, i, lambda j, c: kv_step(j, c, qs_t, False), init)
            finish(i, kv_step(i, carry, qs_t, True))
            return 0

        lax.fori_loop(0, nb, q_block, 0)
        _ride_wait(rider, ride_refs, pl.program_id(0) == n_pairs - 1)

    in_specs = [_resident((s, qw), lambda p: (0, q_off + p)), _resident((s, qw), lambda p: (0, k_off + p)),
                _resident((s, LANES), lambda p: (0, v_off + p))]
    args = [q, k, v]
    if has_bias:
        in_specs.append(_resident((1, 2, nb, 1, t), lambda p: (p, 0, 0, 0, 0)))
        args.append(bias)
    out_specs = [pl.BlockSpec((s, LANES), lambda p: (0, p)), pl.BlockSpec((1, 2, s, 1), lambda p: (p, 0, 0, 0))]
    out_shape = [jax.ShapeDtypeStruct((s, n_pairs * LANES), F32), jax.ShapeDtypeStruct((n_pairs, 2, s, 1), F32)]
    scratch = [pltpu.VMEM((nb, LANES, t), BF16)] + ([pltpu.VMEM((2, s, 1), F32)] if has_bias else [])
    scratch += _add_rider(rider, in_specs, args, out_specs, out_shape)
    return pl.pallas_call(
        body, name=name, grid=(n_pairs,), in_specs=in_specs, out_specs=out_specs, out_shape=out_shape,
        scratch_shapes=scratch,
        compiler_params=_params(("parallel",) if rider is None else ("arbitrary",)),
    )(*args)


def _flash_bwd(q, k, v, do, o, lse, bias, *, n_pairs, hw, q_off, k_off, v_off, scale, qk_dtype, name, rider=None):
    s = q.shape[0]
    t = min(ATT_T, s)
    nb = s // t
    qw = 2 * hw
    has_bias = bias is not None
    c1 = scale * LOG2E

    def body(*refs):
        refs, ride_refs = _split_rider(refs, rider, n_in=7 if has_bias else 6, n_out=5 if has_bias else 3)
        if has_bias:
            (q_ref, k_ref, v_ref, do_ref, o_ref, lse_ref, b_ref, dq_ref, dk_ref, dv_ref, db_ref, dr_ref,
             dkt_ref, dvt_ref) = refs
            db_ref[...] = jnp.zeros_like(db_ref)
        else:
            q_ref, k_ref, v_ref, do_ref, o_ref, lse_ref, dq_ref, dk_ref, dv_ref, dkt_ref, dvt_ref = refs
            b_ref = db_ref = dr_ref = None
        _ride_start(rider, ride_refs, pl.program_id(0) == 0)
        dkt_ref[...] = jnp.zeros_like(dkt_ref)
        dvt_ref[...] = jnp.zeros_like(dvt_ref)
        causal = lax.broadcasted_iota(jnp.int32, (t, t), 1) <= lax.broadcasted_iota(jnp.int32, (t, t), 0)
        cmask = jnp.concatenate([causal, causal], axis=0)
        lane_lt_hw = lax.broadcasted_iota(jnp.int32, (t, qw), 1) < hw

        def q_block(i, _):
            r0 = pl.multiple_of(i * t, t)
            qs = _stack_heads(q_ref[pl.ds(r0, t), :], hw)
            dos = _stack_heads(do_ref[pl.ds(r0, t), :], 64)
            ot = o_ref[pl.ds(r0, t), :]
            delta = jnp.sum(dos * jnp.concatenate([ot, ot], axis=0), axis=-1, keepdims=True)
            lse2 = jnp.concatenate([lse_ref[0, 0, pl.ds(r0, t), :], lse_ref[0, 1, pl.ds(r0, t), :]], axis=0)
            dosb = dos.astype(BF16)
            dos_t = dos.T.astype(BF16)
            qs_t = qs.astype(F32).T.astype(BF16)

            def kv_step(j, carry, masked):
                dq, rsum = carry
                c0 = pl.multiple_of(j * t, t)
                kt = k_ref[pl.ds(c0, t), :]
                vt = v_ref[pl.ds(c0, t), :]
                sc = lax.dot_general(qs, kt, _NT, preferred_element_type=F32) * c1
                if has_bias:
                    sc = sc + _stacked_rows(b_ref[0, 0, j], b_ref[0, 1, j], t)
                if masked:
                    sc = jnp.where(cmask, sc, NEG_INF)
                p = jnp.exp2(sc - lse2)
                dp = lax.dot_general(dosb, vt, _NT, preferred_element_type=F32)
                ds = p * (dp - delta)
                dsb = ds.astype(BF16)
                pb = p.astype(BF16)
                if hw == LANES:
                    dvt_ref[j] += jnp.concatenate(
                        [jnp.dot(dos_t[:64, :t], pb[:t], preferred_element_type=F32),
                         jnp.dot(dos_t[64:, t:], pb[t:], preferred_element_type=F32)], axis=0)
                    dkt_ref[j] += jnp.concatenate(
                        [jnp.dot(qs_t[:hw, :t], dsb[:t], preferred_element_type=F32),
                         jnp.dot(qs_t[hw:, t:], dsb[t:], preferred_element_type=F32)], axis=0)
                else:
                    dvt_ref[j] += jnp.dot(dos_t, pb, preferred_element_type=F32)
                    dkt_ref[j] += jnp.dot(qs_t, dsb, preferred_element_type=F32)
                if has_bias:
                    db_ref[0, 0, j] += jnp.sum(ds[:t], axis=0, keepdims=True)
                    db_ref[0, 1, j] += jnp.sum(ds[t:], axis=0, keepdims=True)
                    rsum = rsum + jnp.sum(ds, axis=-1, keepdims=True)
                return dq + jnp.dot(dsb, kt, preferred_element_type=F32), rsum

            init = (jnp.zeros((2 * t, qw), F32), jnp.zeros((2 * t, 1), F32))
            carry = lax.fori_loop(0, i, functools.partial(kv_step, masked=False), init)
            dq, rsum = kv_step(i, carry, True)
            dq = dq * scale
            dq_ref[pl.ds(r0, t), :] = jnp.where(lane_lt_hw, dq[:t], dq[t:]).astype(qk_dtype)
            if has_bias:
                rsum_row = jnp.broadcast_to(rsum, (2 * t, LANES)).T[0:1]
                dr_ref[0, 0, i] = rsum_row[:, :t]
                dr_ref[0, 1, i] = rsum_row[:, t:]
            return 0

        lax.fori_loop(0, nb, q_block, 0)

        def k_block(j, _):
            c0 = pl.multiple_of(j * t, t)
            dk_ref[pl.ds(c0, t), :] = (dkt_ref[j].T * scale).astype(qk_dtype)
            dv_ref[pl.ds(c0, t), :] = dvt_ref[j].T.astype(BF16)
            return 0

        lax.fori_loop(0, nb, k_block, 0)
        _ride_wait(rider, ride_refs, pl.program_id(0) == n_pairs - 1)

    in_specs = [_resident((s, qw), lambda p: (0, q_off + p)), _resident((s, qw), lambda p: (0, k_off + p)),
                _resident((s, LANES), lambda p: (0, v_off + p)),
                _resident((s, LANES), lambda p: (0, p)), _resident((s, LANES), lambda p: (0, p)),
                _resident((1, 2, s, 1), lambda p: (p, 0, 0, 0))]
    args = [q, k, v, do, o, lse]
    out_specs = [pl.BlockSpec((s, qw), lambda p: (0, p)), pl.BlockSpec((s, qw), lambda p: (0, p)),
                 pl.BlockSpec((s, LANES), lambda p: (0, p))]
    out_shape = [jax.ShapeDtypeStruct((s, n_pairs * qw), qk_dtype), jax.ShapeDtypeStruct((s, n_pairs * qw), qk_dtype),
                 jax.ShapeDtypeStruct((s, n_pairs * LANES), BF16)]
    if has_bias:
        in_specs.append(_resident((1, 2, nb, 1, t), lambda p: (p, 0, 0, 0, 0)))
        args.append(bias)
        for _ in range(2):
            out_specs.append(pl.BlockSpec((1, 2, nb, 1, t), lambda p: (p, 0, 0, 0, 0)))
            out_shape.append(jax.ShapeDtypeStruct((n_pairs, 2, nb, 1, t), F32))
    scratch = [pltpu.VMEM((nb, qw, t), F32), pltpu.VMEM((nb, LANES, t), F32)]
    scratch += _add_rider(rider, in_specs, args, out_specs, out_shape)
    return pl.pallas_call(
        body, name=name, grid=(n_pairs,), in_specs=in_specs, out_specs=out_specs, out_shape=out_shape,
        scratch_shapes=scratch,
        compiler_params=_params(("parallel",) if rider is None else ("arbitrary",)),
    )(*args)


def _alibi_slope(h):
    return 2.0 ** (-8.0 * (h + 1.0) / SWA_HEADS)


SWA_ROWS = 512
SWA_SCALE = SWA_DIM ** -0.5


def _swa_geometry(i):
    w = WINDOW
    r0 = pl.multiple_of(i * w, w)
    b0 = pl.multiple_of(jnp.maximum(i - 1, 0) * w, w)
    row = lax.broadcasted_iota(jnp.int32, (w, 2 * w), 0)
    col = lax.broadcasted_iota(jnp.int32, (w, 2 * w), 1)
    dist = row - col + (r0 - b0)
    valid = (dist >= 0) & (dist < w)
    return r0, b0, dist.astype(F32), valid


def _swa_q_head(qblk, h):
    kv = h // (SWA_HEADS // SWA_KV_HEADS)
    if h % 2 != kv:
        qblk = pltpu.roll(qblk, 64, axis=1)
    return jnp.where(_head_mask(qblk.shape, kv), qblk, 0.0)


SWA_GROUP = SWA_HEADS // SWA_KV_HEADS


def _swa_stack(ref, rs, grp):
    parts = []
    for a in range(SWA_GROUP):
        h = SWA_GROUP * grp + a
        parts.append(_swa_q_head(ref[rs, (h // 2) * LANES:(h // 2 + 1) * LANES].astype(F32), h))
    return jnp.concatenate(parts, axis=0)


def _swa_unstack(x, grp):
    tiles = []
    for a in range(SWA_GROUP):
        h = SWA_GROUP * grp + a
        tile = x[a * WINDOW:(a + 1) * WINDOW]
        tiles.append(pltpu.roll(tile, 64, axis=1) if h % 2 != grp else tile)
    return tiles


def _swa_head_column(vals):
    return jnp.concatenate([jnp.full((WINDOW, 1), v, F32) for v in vals], axis=0)


def _swa_logits(qs, kb, dist, valid, grp):
    slopes = _swa_head_column([_alibi_slope(SWA_GROUP * grp + a) for a in range(SWA_GROUP)])
    dist4 = jnp.concatenate([dist] * SWA_GROUP, axis=0)
    valid4 = jnp.concatenate([valid] * SWA_GROUP, axis=0)
    sc = lax.dot_general(qs, kb, _NT, preferred_element_type=F32) * SWA_SCALE - slopes * dist4
    return jnp.where(valid4, sc, NEG_INF)


def _swa_merge_heads(tiles):
    lt64 = lax.broadcasted_iota(jnp.int32, (WINDOW, LANES), 1) < 64
    return jnp.concatenate([jnp.where(lt64, tiles[2 * b], tiles[2 * b + 1]) for b in range(SWA_HEADS // 2)], axis=1)


def _swa_fwd(z0b, sinks, *, name):
    s = z0b.shape[0]
    w = WINDOW
    rows = min(SWA_ROWS, s)
    per_step = rows // w
    qcols = SWA_HEADS * SWA_DIM

    def body(sink_ref, q_ref, k_ref, v_ref, o_ref, lse_ref):
        g = pl.program_id(0)
        for ii in range(per_step):
            rs = slice(ii * w, (ii + 1) * w)
            r0, b0, dist, valid = _swa_geometry(g * per_step + ii)
            kb = k_ref[pl.ds(b0, 2 * w), :]
            vb = v_ref[pl.ds(b0, 2 * w), :]
            o_tiles = []
            for h in range(SWA_HEADS):
                kv = h // SWA_GROUP
                qh = _swa_q_head(q_ref[rs, (h // 2) * LANES:(h // 2 + 1) * LANES].astype(F32), h).astype(BF16)
                sc = lax.dot_general(qh, kb, _NT, preferred_element_type=F32) * SWA_SCALE - _alibi_slope(h) * dist
                sc = jnp.where(valid, sc, NEG_INF)
                sink = sink_ref[0, h]
                m = jnp.maximum(jnp.max(sc, axis=-1, keepdims=True), sink)
                p = jnp.exp(sc - m)
                l = jnp.sum(p, axis=-1, keepdims=True) + jnp.exp(sink - m)
                oh = jnp.dot(p.astype(BF16), vb, preferred_element_type=F32) / l
                o_tiles.append(pltpu.roll(oh, 64, axis=1) if h % 2 != kv else oh)
                lse_ref[h, rs, :] = m + jnp.log(l)
            o_ref[rs, :] = _swa_merge_heads(o_tiles)

    return pl.pallas_call(
        body, name=name, grid=(s // rows,),
        in_specs=[pl.BlockSpec(memory_space=pltpu.SMEM),
                  pl.BlockSpec((rows, qcols), lambda g: (g, 0)),
                  pl.BlockSpec((s, LANES), lambda g: (0, 4)), pl.BlockSpec((s, LANES), lambda g: (0, 5))],
        out_specs=[pl.BlockSpec((rows, qcols), lambda g: (g, 0)), pl.BlockSpec((SWA_HEADS, rows, 1), lambda g: (0, g, 0))],
        out_shape=[jax.ShapeDtypeStruct((s, qcols), F32), jax.ShapeDtypeStruct((SWA_HEADS, s, 1), F32)],
        compiler_params=_params(("parallel",)),
    )(sinks, z0b, z0b, z0b)


def _swa_bwd(z0b, sinks, do, o, lse, *, name):
    s = z0b.shape[0]
    w = WINDOW
    rows = min(SWA_ROWS, s)
    per_step = rows // w
    qcols = SWA_HEADS * SWA_DIM
    nblk = s // w

    def body(sink_ref, q_ref, k_ref, v_ref, do_ref, o_ref, lse_ref, dq_ref, dkt_ref, dvt_ref, dsink_ref):
        g = pl.program_id(0)

        @pl.when(g == 0)
        def _():
            dkt_ref[...] = jnp.zeros_like(dkt_ref)
            dvt_ref[...] = jnp.zeros_like(dvt_ref)
            dsink_ref[...] = jnp.zeros_like(dsink_ref)

        for ii in range(per_step):
            i = g * per_step + ii
            rs = slice(ii * w, (ii + 1) * w)
            r0, b0, dist, valid = _swa_geometry(i)
            j0 = jnp.maximum(i - 1, 0)
            kb = k_ref[pl.ds(b0, 2 * w), :]
            vb = v_ref[pl.ds(b0, 2 * w), :]
            dq_tiles = []
            for grp in range(SWA_KV_HEADS):
                heads = [SWA_GROUP * grp + a for a in range(SWA_GROUP)]
                qs32 = _swa_stack(q_ref, rs, grp)
                dos32 = _swa_stack(do_ref, rs, grp)
                delta = jnp.sum(dos32 * _swa_stack(o_ref, rs, grp), axis=-1, keepdims=True)
                lse = jnp.concatenate([lse_ref[h, rs, :] for h in heads], axis=0)
                sink = _swa_head_column([sink_ref[0, h] for h in heads])
                p = jnp.exp(_swa_logits(qs32.astype(BF16), kb, dist, valid, grp) - lse)
                dp = lax.dot_general(dos32.astype(BF16), vb, _NT, preferred_element_type=F32)
                ds = p * (dp - delta)
                dsb = ds.astype(BF16)
                d_sink = jnp.exp(sink - lse) * delta
                for a, h in enumerate(heads):
                    dsink_ref[h:h + 1, :] += jnp.broadcast_to(-jnp.sum(d_sink[a * w:(a + 1) * w]), (1, LANES))
                dvt = jnp.dot(dos32.T.astype(BF16), p.astype(BF16), preferred_element_type=F32)
                dkt = jnp.dot(qs32.T.astype(BF16), dsb, preferred_element_type=F32) * SWA_SCALE
                dvt_ref[j0] += dvt[:, :w]
                dvt_ref[j0 + 1] += dvt[:, w:]
                dkt_ref[j0] += dkt[:, :w]
                dkt_ref[j0 + 1] += dkt[:, w:]
                dq_tiles += _swa_unstack(jnp.dot(dsb, kb, preferred_element_type=F32) * SWA_SCALE, grp)
            dq_ref[rs, :] = _swa_merge_heads(dq_tiles)

    return pl.pallas_call(
        body, name=name, grid=(s // rows,),
        in_specs=[pl.BlockSpec(memory_space=pltpu.SMEM),
                  pl.BlockSpec((rows, qcols), lambda g: (g, 0)),
                  pl.BlockSpec((s, LANES), lambda g: (0, 4)), pl.BlockSpec((s, LANES), lambda g: (0, 5)),
                  pl.BlockSpec((rows, qcols), lambda g: (g, 0)), pl.BlockSpec((rows, qcols), lambda g: (g, 0)),
                  pl.BlockSpec((SWA_HEADS, rows, 1), lambda g: (0, g, 0))],
        out_specs=[pl.BlockSpec((rows, qcols), lambda g: (g, 0)),
                   pl.BlockSpec((nblk, LANES, w), lambda g: (0, 0, 0)),
                   pl.BlockSpec((nblk, LANES, w), lambda g: (0, 0, 0)),
                   pl.BlockSpec((SWA_HEADS, LANES), lambda g: (0, 0))],
        out_shape=[jax.ShapeDtypeStruct((s, qcols), F32),
                   jax.ShapeDtypeStruct((nblk, LANES, w), F32), jax.ShapeDtypeStruct((nblk, LANES, w), F32),
                   jax.ShapeDtypeStruct((SWA_HEADS, LANES), F32)],
        compiler_params=_params(("arbitrary",)),
    )(sinks, z0b, z0b, z0b, do, o, lse)


CUM_T = 256


def _split3(x):
    hi = x.astype(BF16)
    r1 = x - hi.astype(F32)
    mid = r1.astype(BF16)
    lo = (r1 - mid.astype(F32)).astype(BF16)
    return hi, mid, lo


def _tri_dot(tri, x):
    hi, mid, lo = _split3(x)
    out = jnp.dot(tri, hi, preferred_element_type=F32)
    out = out + jnp.dot(tri, mid, preferred_element_type=F32)
    return out + jnp.dot(tri, lo, preferred_element_type=F32)


def _logf_fwd(zf, bf, *, name):
    s = zf.shape[0]
    t = CUM_T
    nb = s // t

    def body(z_ref, b_ref, c_ref, carry_ref):
        i = pl.program_id(0)

        @pl.when(i == 0)
        def _():
            carry_ref[...] = jnp.zeros_like(carry_ref)

        x = z_ref[...] + b_ref[...]
        lf = jnp.minimum(x, 0.0) - jnp.log(1.0 + jnp.exp(-jnp.abs(x)))
        row = lax.broadcasted_iota(jnp.int32, (t, t), 0)
        col = lax.broadcasted_iota(jnp.int32, (t, t), 1)
        tri = jnp.where(col <= row, 1.0, 0.0).astype(BF16)
        c = _tri_dot(tri, lf) + carry_ref[...]
        c_ref[...] = c
        carry_ref[...] = c[t - 1:t, :]

    return pl.pallas_call(
        body, name=name, grid=(nb,),
        in_specs=[pl.BlockSpec((t, LANES), lambda i: (i, 0)), pl.BlockSpec((1, LANES), lambda i: (0, 0))],
        out_specs=pl.BlockSpec((t, LANES), lambda i: (i, 0)),
        out_shape=jax.ShapeDtypeStruct((s, LANES), F32),
        scratch_shapes=[pltpu.VMEM((1, LANES), F32)],
        compiler_params=_params(("arbitrary",)),
    )(zf, bf)


def _logf_bwd(dc, zf, bf, *, name):
    s = zf.shape[0]
    t = CUM_T
    nb = s // t

    def body(dc_ref, z_ref, b_ref, dz_ref, db_ref, carry_ref):
        i = pl.program_id(0)

        @pl.when(i == 0)
        def _():
            carry_ref[...] = jnp.zeros_like(carry_ref)
            db_ref[...] = jnp.zeros_like(db_ref)

        row = lax.broadcasted_iota(jnp.int32, (t, t), 0)
        col = lax.broadcasted_iota(jnp.int32, (t, t), 1)
        tri = jnp.where(col >= row, 1.0, 0.0).astype(BF16)
        dlf = _tri_dot(tri, dc_ref[...]) + carry_ref[...]
        carry_ref[...] = dlf[0:1, :]
        x = z_ref[...] + b_ref[...]
        dz = dlf * _sigmoid(-x)
        dz_ref[...] = dz.astype(BF16)
        db_ref[...] += jnp.sum(dz, axis=0, keepdims=True)

    return pl.pallas_call(
        body, name=name, grid=(nb,),
        in_specs=[pl.BlockSpec((t, LANES), lambda i: (nb - 1 - i, 0)), pl.BlockSpec((t, LANES), lambda i: (nb - 1 - i, 0)),
                  pl.BlockSpec((1, LANES), lambda i: (0, 0))],
        out_specs=[pl.BlockSpec((t, LANES), lambda i: (nb - 1 - i, 0)), pl.BlockSpec((1, LANES), lambda i: (0, 0))],
        out_shape=[jax.ShapeDtypeStruct((s, LANES), BF16), jax.ShapeDtypeStruct((1, LANES), F32)],
        scratch_shapes=[pltpu.VMEM((1, LANES), F32)],
        compiler_params=_params(("arbitrary",)),
    )(dc, zf, bf)


def _sum_pieces(p_ref):
    g = p_ref[0].astype(F32)
    for k in range(1, N_DEV):
        g = g + p_ref[k].astype(F32)
    return g


def _adam_update(g, w, m, v):
    bc1 = 1.0 - ADAM_B1 ** ADAM_STEP
    bc2 = 1.0 - ADAM_B2 ** ADAM_STEP
    nm = ADAM_B1 * m + (1.0 - ADAM_B1) * g
    nv = ADAM_B2 * v + (1.0 - ADAM_B2) * (g * g)
    m_hat = nm / bc1
    v_hat = nv / bc2
    return -ADAM_LR * (m_hat / (jnp.sqrt(v_hat) + ADAM_EPS) + ADAM_WD * w), nm, nv


def _adamw(pieces, w, m, v, *, name):
    rows, cols = w.shape
    tr = _tile(rows, (RB1, RB0, SMALL_ROWS))

    def body(p_ref, w_ref, m_ref, v_ref, g_ref, d_ref, nm_ref, nv_ref):
        g = _sum_pieces(p_ref)
        g_ref[...] = g
        d_ref[...], nm_ref[...], nv_ref[...] = _adam_update(g, w_ref[...], m_ref[...], v_ref[...])

    spec = pl.BlockSpec((tr, cols), lambda i: (i, 0))
    shape = jax.ShapeDtypeStruct((rows, cols), F32)
    return pl.pallas_call(
        body, name=name, grid=(rows // tr,),
        in_specs=[pl.BlockSpec((N_DEV, tr, cols), lambda i: (0, i, 0)), spec, spec, spec],
        out_specs=[spec, spec, spec, spec], out_shape=[shape, shape, shape, shape],
        compiler_params=_params(("parallel",)),
    )(pieces, w, m, v)


def _sum8(pieces, rows, *, name):
    cols = pieces.shape[2]
    tr = _tile(rows, (176, 96))

    def body(p_ref, g_ref):
        g_ref[...] = _sum_pieces(p_ref)

    return pl.pallas_call(
        body, name=name, grid=(rows // tr,),
        in_specs=[pl.BlockSpec((N_DEV, tr, cols), lambda i: (0, i, 0))],
        out_specs=pl.BlockSpec((tr, cols), lambda i: (i, 0)),
        out_shape=jax.ShapeDtypeStruct((rows, cols), F32),
        compiler_params=_params(("parallel",)),
    )(pieces)


def _adamw_native(g, w, m, v, *, name):
    rows, cols = w.shape
    tr = _tile(rows, (256, 128))

    def body(g_ref, w_ref, m_ref, v_ref, d_ref, nm_ref, nv_ref):
        d_ref[...], nm_ref[...], nv_ref[...] = _adam_update(g_ref[...], w_ref[...], m_ref[...], v_ref[...])

    spec = pl.BlockSpec((tr, cols), lambda i: (i, 0))
    shape = jax.ShapeDtypeStruct((rows, cols), F32)
    return pl.pallas_call(
        body, name=name, grid=(rows // tr,), in_specs=[spec, spec, spec, spec],
        out_specs=[spec, spec, spec], out_shape=[shape, shape, shape],
        compiler_params=_params(("parallel",)),
    )(g, w, m, v)


MESH = pl.DeviceIdType.MESH
ANY = pl.BlockSpec(memory_space=pl.ANY)


def _all_gather(shard, *, name):
    rows, lanes = shard.shape

    def body(x_ref, out_ref, send_sems, recv_sems, local_sem):
        x, y, c = lax.axis_index("x"), lax.axis_index("y"), lax.axis_index("c")
        me, sibling = (x, y, c), (x, y, 1 - c)
        chips = [(1 - x, y), (x, 1 - y), (1 - x, 1 - y)]

        def block(px, py, pc):
            return out_ref.at[4 * px + 2 * py + pc]

        def copy(k, blk, to, src=None):
            return pltpu.make_async_remote_copy(
                src_ref=block(*blk) if src is None else src, dst_ref=block(*blk),
                send_sem=send_sems.at[k], recv_sem=recv_sems.at[k], device_id=to, device_id_type=MESH)

        mine = pltpu.make_async_copy(x_ref, block(*me), local_sem)
        mine.start()
        first = [copy(0, me, sibling, src=x_ref)]
        first += [copy(1 + j, me, (*chip, c), src=x_ref) for j, chip in enumerate(chips)]
        for cp in first:
            cp.start()
        passed = [copy(4 + j, (*chip, c), sibling) for j, chip in enumerate(chips)]
        for j, chip in enumerate(chips):
            copy(1 + j, (*chip, c), me).wait_recv()
            passed[j].start()
        copy(0, sibling, me).wait_recv()
        for j, chip in enumerate(chips):
            copy(4 + j, (*chip, 1 - c), me).wait_recv()
        for cp in first + passed:
            cp.wait_send()
        mine.wait()

    return pl.pallas_call(
        body, name=name, out_shape=jax.ShapeDtypeStruct((N_DEV, rows, lanes), shard.dtype),
        in_specs=[ANY], out_specs=ANY,
        scratch_shapes=[pltpu.SemaphoreType.DMA((7,)), pltpu.SemaphoreType.DMA((7,)), pltpu.SemaphoreType.DMA(())],
    )(shard)


def _peer_copies(kind, src_ref, out_ref, send_sems, recv_sems, local_sem):
    x, y, c = lax.axis_index("x"), lax.axis_index("y"), lax.axis_index("c")
    me = 4 * x + 2 * y + c

    def src(idx):
        return src_ref.at[idx] if kind == "exchange" else src_ref

    mine = None if local_sem is None else pltpu.make_async_copy(src(me), out_ref.at[me], local_sem)
    copies = []
    for r in range(1, N_DEV):
        px = 1 - x if r & 4 else x
        py = 1 - y if r & 2 else y
        pc = 1 - c if r & 1 else c
        copies.append(pltpu.make_async_remote_copy(
            src_ref=src(4 * px + 2 * py + pc), dst_ref=out_ref.at[me],
            send_sem=send_sems.at[r - 1], recv_sem=recv_sems.at[r - 1],
            device_id=(px, py, pc), device_id_type=MESH))
    return mine, copies


PEER_SEMS = [pltpu.SemaphoreType.DMA((7,)), pltpu.SemaphoreType.DMA((7,)), pltpu.SemaphoreType.DMA(())]


HBM = pl.BlockSpec(memory_space=pltpu.HBM)
SEMAPHORES = pl.BlockSpec(memory_space=pltpu.SEMAPHORE)


def _peer_start(kind, arr, *, name):
    land = lax.empty((N_DEV,) + arr.shape[-2:], arr.dtype)

    def body(src_ref, land_ref, send_sems, recv_sems, src_thru, land_thru, token):
        _, copies = _peer_copies(kind, src_ref, land_ref, send_sems, recv_sems, None)
        for cp in copies:
            cp.start()
        token[...] = jnp.zeros_like(token)

    return pl.pallas_call(
        body, name=name,
        out_shape=(pltpu.SemaphoreType.DMA((N_DEV - 1,)), pltpu.SemaphoreType.DMA((N_DEV - 1,)),
                   pltpu.HBM(arr.shape, arr.dtype), pltpu.HBM(land.shape, land.dtype), jax.ShapeDtypeStruct((8, LANES), F32)),
        in_specs=(HBM, HBM), out_specs=(SEMAPHORES, SEMAPHORES, HBM, HBM, pl.BlockSpec(memory_space=pltpu.VMEM)),
        input_output_aliases={0: 2, 1: 3},
        compiler_params=pltpu.CompilerParams(has_side_effects=pltpu.SideEffectType.DATAFLOW_SIDE_EFFECTING),
    )(pltpu.with_memory_space_constraint(arr, pltpu.HBM), pltpu.with_memory_space_constraint(land, pltpu.HBM))


def _peer_wait(kind, send_sems, recv_sems, src_thru, land_thru, after, *, name):
    def body(src_ref, land_ref, send_sems, recv_sems, *_):
        _, copies = _peer_copies(kind, src_ref, land_ref, send_sems, recv_sems, None)
        for cp in copies:
            cp.wait_send()
            cp.wait_recv()

    return pl.pallas_call(
        body, name=name,
        out_shape=(pltpu.HBM(src_thru.shape, src_thru.dtype), pltpu.HBM(land_thru.shape, land_thru.dtype)),
        in_specs=(HBM, HBM, SEMAPHORES, SEMAPHORES) + (ANY,) * len(after), out_specs=(HBM, HBM),
        input_output_aliases={0: 0, 1: 1},
        compiler_params=pltpu.CompilerParams(has_side_effects=pltpu.SideEffectType.DATAFLOW_SIDE_EFFECTING),
    )(src_thru, land_thru, send_sems, recv_sems, *after)


def _add_rider(rider, in_specs, args, out_specs, out_shape):
    if rider is None:
        return []
    _, arr = rider
    in_specs.append(ANY)
    args.append(arr)
    out_specs.append(ANY)
    out_shape.append(jax.ShapeDtypeStruct((N_DEV,) + arr.shape[-2:], arr.dtype))
    return list(PEER_SEMS)


def _split_rider(refs, rider, n_in, n_out):
    if rider is None:
        return refs, None
    refs = list(refs)
    rin = refs.pop(n_in)
    rout = refs.pop(n_in + n_out)
    return refs[:-3], (rin, rout, *refs[-3:])


def _ride_start(rider, ride_refs, first):
    if rider is None:
        return

    @pl.when(first)
    def _():
        mine, copies = _peer_copies(rider[0], *ride_refs)
        mine.start()
        for cp in copies:
            cp.start()


def _ride_wait(rider, ride_refs, last):
    if rider is None:
        return

    @pl.when(last)
    def _():
        mine, copies = _peer_copies(rider[0], *ride_refs)
        for cp in copies:
            cp.wait()
        mine.wait()


def _gathered_cols(blocks, kdim):
    n = blocks.shape[1] * WIDE // kdim
    return blocks.reshape(N_DEV, kdim, n).transpose(1, 0, 2).reshape(kdim, N_DEV * n)


def _scatter_cols(dw):
    kdim, n8 = dw.shape
    n = n8 // N_DEV
    return dw.reshape(kdim, N_DEV, n).transpose(1, 0, 2).reshape(N_DEV, kdim * n // WIDE, WIDE)


def _pad_rows(a, rows):
    pad = [(0, 0)] * a.ndim
    pad[-2] = (0, rows - a.shape[-2])
    return jnp.pad(a, pad)


def _layer0_in_weight_t(wt):
    cq, ckv, kpe = wt[0:256], wt[256:384], wt[384:416]
    q_s, k_s, v_s, gate = wt[416:928], wt[928:1056], wt[1056:1184], wt[1184:2208]
    z = jnp.zeros((64, wt.shape[1]), wt.dtype)
    return jnp.concatenate([gate, cq, ckv, z, kpe, z[:32], q_s, k_s, v_s], axis=0)


def _layer0_in_grad_t(dwt):
    gate, cq, ckv, kpe = dwt[0:1024], dwt[1024:1280], dwt[1280:1408], dwt[1472:1504]
    q_s, k_s, v_s = dwt[1536:2048], dwt[2048:2176], dwt[2176:2304]
    return jnp.concatenate([cq, ckv, kpe, q_s, k_s, v_s, gate], axis=0)


def _layer1_in_weight_t(wt):
    main = jnp.concatenate([wt[:3 * D_MODEL], wt[3 * D_MODEL + FOX_HEADS:]], axis=0)
    return main, _pad_rows(wt[3 * D_MODEL:3 * D_MODEL + FOX_HEADS], LANES)


def _layer1_in_grad_t(d_blocks, d_wft):
    return jnp.concatenate([*d_blocks[:3], d_wft[:FOX_HEADS], d_blocks[3]], axis=0)


def _q_up_weight(w):
    return jnp.pad(w.reshape(MLA_Q_RANK, MLA_HEADS, 96), ((0, 0), (0, 0), (0, 32))).reshape(MLA_Q_RANK, MLA_HEADS * LANES)


def _q_up_grad(dwp):
    return dwp.reshape(MLA_Q_RANK, MLA_HEADS, LANES)[:, :, :96].reshape(MLA_Q_RANK, MLA_HEADS * 96)


def _kv_up_weight(w):
    w4 = w.reshape(MLA_KV_RANK, MLA_HEADS, 2, 64)
    kp = jnp.pad(w4[:, :, 0, :], ((0, 0), (0, 0), (0, 64))).reshape(MLA_KV_RANK, MLA_HEADS * LANES)
    vp = w4[:, :, 1, :].reshape(MLA_KV_RANK, MLA_HEADS * 64)
    return jnp.concatenate([kp, vp], axis=1)


def _kv_up_grad(dwp):
    dk = dwp[:, :MLA_HEADS * LANES].reshape(MLA_KV_RANK, MLA_HEADS, LANES)[:, :, :64]
    dv = dwp[:, MLA_HEADS * LANES:].reshape(MLA_KV_RANK, MLA_HEADS, 64)
    return jnp.stack([dk, dv], axis=2).reshape(MLA_KV_RANK, MLA_HEADS * LANES)


def _pad_lanes(a):
    return jnp.pad(a, ((0, 0), (0, LANES - a.shape[1])))


def _small_pack(g_in, g_final, g_q_a, g_kv_a, sinks, b_f, loss):
    rows = [g_in.reshape(8, LANES), g_final.reshape(8, LANES), g_q_a.reshape(2, LANES), g_kv_a.reshape(1, LANES),
            _pad_lanes(sinks.reshape(1, -1)), _pad_lanes(b_f.reshape(1, -1)), _pad_lanes(loss.reshape(1, 1)),
            jnp.zeros((2, LANES), F32)]
    return jnp.concatenate(rows, axis=0)


def _small_unpack(a):
    return (a[0:8].reshape(1, D_MODEL), a[8:16].reshape(D_MODEL), a[16:18].reshape(1, MLA_Q_RANK),
            a[18:19].reshape(1, MLA_KV_RANK), a[19:20, :SWA_HEADS], a[20:21, :FOX_HEADS], a[21, 0])


def _local_step(x, positions, target, e_g_in, w0t, e_g_q_a, wq, e_g_kv_a, wkv, e_sinks,
                late, o_b_f, g_final, scatter1=None, scatter0=None):
    s = x.shape[0]
    mla_scale = (MLA_NOPE + MLA_ROPE) ** -0.5
    fox_scale = FOX_DIM ** -0.5
    n0a = Z0A_UNITS * LANES

    inv_freq = 1.0 / (ROPE_THETA ** (jnp.arange(0, MLA_ROPE, 2, dtype=F32) / MLA_ROPE))
    ang = positions.astype(F32)[:, None] * inv_freq
    cos, sin = jnp.cos(ang), jnp.sin(ang)
    ones, zeros = jnp.ones((s, 64), F32), jnp.zeros((s, 64), F32)
    cos_t = jnp.concatenate([ones, cos, cos, ones[:, :32]], axis=1)
    sin_t = jnp.concatenate([zeros, -sin, sin, zeros[:, :32]], axis=1)

    h0 = _rmsnorm_fwd(x, e_g_in, width=D_MODEL, col_blk=0, name="l0_norm")
    z0a = _matmul(h0, w0t, tb=True, b_rows=(0, n0a), name="l0_in_a")
    z0b = _matmul(h0, w0t, tb=True, b_rows=(n0a, Z0B_UNITS * LANES), name="l0_in_b", out_dtype=BF16)
    cqn = _rmsnorm_fwd(z0a, e_g_q_a, width=MLA_Q_RANK, col_blk=4, name="l0_q_norm")
    ckvn = _rmsnorm_fwd(z0a, e_g_kv_a, width=MLA_KV_RANK, col_blk=10, name="l0_kv_norm")
    qp = _matmul(cqn, wq, name="l0_q_up")
    kvp = _matmul(ckvn, wkv, name="l0_kv_up", out_dtype=BF16)
    qm, km = _rope_fwd(qp, kvp, z0a, cos_t, sin_t, name="l0_rope")
    gathers = len(late) == 2
    res = _flash_fwd(qm, km, kvp, None, n_pairs=MLA_HEADS // 2, hw=LANES, q_off=0, k_off=0, v_off=MLA_HEADS,
                     scale=mla_scale, name="l0_mla_fwd", rider=("gather", late[0]) if gathers else None)
    o_mla, lse_mla = res[0], res[1]
    wo0, o_g_in, w1t, wft, wo1 = late[1](res[2]) if gathers else late
    o_swa, lse_swa = _swa_fwd(z0b, e_sinks, name="l0_swa_fwd")
    og0 = _gate_fwd([o_mla, o_swa], z0a, name="l0_gate")

    x1, h1 = _matmul_rows([(og0, wo0, False)], [(x, D_MODEL, 0)], [o_g_in], _residual_norm_epilogue,
                          [("rows", D_MODEL, F32), ("rows", D_MODEL, BF16)], name="l0_out")
    z1 = _matmul(h1, w1t, tb=True, b_rows=(0, 3 * D_MODEL), name="l1_in_qkv", out_dtype=BF16)
    gate1 = _matmul(h1, w1t, tb=True, b_rows=(3 * D_MODEL, D_MODEL), name="l1_in_gate")
    zf = _matmul(h1, wft, tb=True, name="l1_in_f")
    bf = _pad_lanes(o_b_f)
    log_cum = _logf_fwd(zf, bf, name="l1_logf")
    bias2 = (-LOG2E * log_cum[:, :FOX_HEADS]).T
    t_bwd = min(ATT_T, s)
    bias = bias2.reshape(FOX_HEADS // 2, 2, s // t_bwd, 1, t_bwd)
    t_fwd = _fwd_tile(s)
    o_fox, lse_fox = _flash_fwd(z1, z1, z1, bias2.reshape(FOX_HEADS // 2, 2, s // t_fwd, 1, t_fwd),
                                n_pairs=FOX_HEADS // 2, hw=64, q_off=0, k_off=8, v_off=16, scale=fox_scale,
                                name="l1_fox_fwd")
    og1 = _gate_fwd([o_fox], gate1, name="l1_gate")

    dx2, loss_part, d_g_final = _matmul_rows(
        [(og1, wo1, False)], [(x1, D_MODEL, 0), (target, D_MODEL, 0)], [g_final.reshape(1, D_MODEL)], _loss_epilogue,
        [("rows", D_MODEL, F32), ("sum", (8, LANES)), ("sum", (1, D_MODEL))], name="l1_out_loss")

    d_wo1 = _matmul(og1, dx2, ta=True, name="l1_out_dw")
    do_fox, d_gate1 = _matmul_rows([(dx2, wo1, True)], [(o_fox, D_MODEL, 0), (gate1, D_MODEL, 0)], [],
                                   _gate_bwd_epilogue([D_MODEL]), [("rows", D_MODEL, F32), ("rows", D_MODEL, BF16)],
                                   name="l1_out_dx")
    dq1, dk1, dv1, dbias, drow = _flash_bwd(z1, z1, z1, do_fox, o_fox, lse_fox, bias, n_pairs=FOX_HEADS // 2, hw=64,
                                            q_off=0, k_off=8, v_off=16, scale=fox_scale, qk_dtype=BF16,
                                            name="l1_fox_bwd")
    d_log_cum = (drow.reshape(FOX_HEADS, s) - dbias.reshape(FOX_HEADS, s)).T
    d_log_cum = jnp.pad(d_log_cum, ((0, 0), (0, LANES - FOX_HEADS)))
    d_zf, d_bf = _logf_bwd(d_log_cum, zf, bf, name="l1_logf_bwd")
    dz1 = (dq1, dk1, dv1, d_gate1)
    d_w1t = tuple(_matmul(d, h1, ta=True, name=f"l1_in_dw_{k}") for k, d in enumerate(dz1))
    d_wft = _matmul(d_zf, h1, ta=True, name="l1_in_f_dw")
    dx1, d_o_g_in = _matmul_rows([(d, w1t, False, k * D_MODEL) for k, d in enumerate(dz1)] + [(d_zf, wft, False)],
                                 [(x1, D_MODEL, 0), (dx2, D_MODEL, 0)],
                                 [o_g_in], _rms_bwd_epilogue, [("rows", D_MODEL, F32), ("sum", (1, D_MODEL))],
                                 name="l1_in_dx")

    d_wo0 = _matmul(og0, dx1, ta=True, name="l0_out_dw")
    half = D_MODEL // 2
    do_mla, do_swa, d_gate0 = _matmul_rows(
        [(dx1, wo0, True)], [(o_mla, half, 0), (o_swa, half, 0), (z0a, D_MODEL, 0)], [], _gate_bwd_epilogue([half, half]),
        [("rows", half, F32), ("rows", half, F32), ("rows", D_MODEL, BF16)], name="l0_out_dx")
    dq_s, dkt_s, dvt_s, d_sinks = _swa_bwd(z0b, e_sinks, do_swa, o_swa, lse_swa, name="l0_swa_bwd")
    dk_s = dkt_s.transpose(0, 2, 1).reshape(s, LANES)
    dv_s = dvt_s.transpose(0, 2, 1).reshape(s, LANES)
    rider = None
    if scatter1 is not None:
        rider = ("exchange", scatter1(dict(w1t=d_w1t, wft=d_wft, wo1=d_wo1, o_g_in=d_o_g_in, wo0=d_wo0)))
    res = _flash_bwd(qm, km, kvp, do_mla, o_mla, lse_mla, None, n_pairs=MLA_HEADS // 2, hw=LANES, q_off=0, k_off=0,
                     v_off=MLA_HEADS, scale=mla_scale, qk_dtype=F32, name="l0_mla_bwd", rider=rider)
    dqm, dkm, dvm = res[0], res[1], res[2]
    recv1 = res[3] if rider is not None else None
    d_qp, d_kvp, d_kpe = _rope_bwd(dqm, dkm, dvm, cos_t, sin_t, name="l0_rope_bwd")
    d_wq = _matmul(cqn, d_qp, ta=True, name="l0_q_up_dw")
    d_cqn = _matmul(d_qp, wq, tb=True, name="l0_q_up_dx")
    d_wkv = _matmul(ckvn, d_kvp, ta=True, name="l0_kv_up_dw")
    d_ckvn = _matmul(d_kvp, wkv, tb=True, name="l0_kv_up_dx")
    d_cq, d_g_q_a = _rmsnorm_bwd(z0a, e_g_q_a, d_cqn, width=MLA_Q_RANK, col_blk=4, name="l0_q_norm_bwd")
    d_ckv, d_g_kv_a = _rmsnorm_bwd(z0a, e_g_kv_a, d_ckvn, width=MLA_KV_RANK, col_blk=10, name="l0_kv_norm_bwd")
    dz0 = jnp.concatenate([d_gate0, d_cq, d_ckv, d_kpe, dq_s.astype(BF16), dk_s.astype(BF16), dv_s.astype(BF16)], axis=1)
    d_w0t = _matmul(dz0, h0, ta=True, name="l0_in_dw")
    pending0, after_start = None, []
    if scatter0 is not None:
        *pending0, token = _peer_start("exchange", scatter0(dict(w0t=d_w0t, wq=d_wq, wkv=d_wkv)), name="grads0_start")
        after_start = [token]
    grad_x, d_e_g_in = _matmul_rows(
        [(dz0, w0t, False)], [(x, D_MODEL, 0), (dx1, D_MODEL, 0)], [e_g_in] + after_start,
        lambda dy, xt, add, g, *_: _rms_bwd_epilogue(dy, xt, add, g),
        [("rows", D_MODEL, F32), ("sum", (1, D_MODEL))], name="l0_in_dx")

    return dict(pending0=pending0, recv1=recv1, loss=loss_part[0, 0], grad_x=grad_x, e_g_in=d_e_g_in, w0t=d_w0t, e_g_q_a=d_g_q_a, wq=d_wq,
                e_g_kv_a=d_g_kv_a, wkv=d_wkv, e_sinks=d_sinks[:, 0].reshape(1, SWA_HEADS), wo0=d_wo0,
                o_g_in=d_o_g_in, w1t=d_w1t, wft=d_wft, o_b_f=d_bf[:, :FOX_HEADS], wo1=d_wo1, g_final=d_g_final.reshape(D_MODEL))


def _wide(a, rows):
    flat = a.reshape(-1)
    return jnp.pad(flat, (0, rows * WIDE - flat.shape[0])).reshape(rows, WIDE)


def _rows_b0(w_q, w_kv):
    return jnp.concatenate([_wide(w_q, 32), _wide(w_kv, 16)], axis=0)


def _unflat_b0(f):
    return f[0:24].reshape(1, MLA_Q_RANK, 96), f[32:48].reshape(1, MLA_KV_RANK, 128)


def _rows_b1(o_w_out, e_w_out, g_in):
    return jnp.concatenate([o_w_out, e_w_out, _wide(g_in, 16)], axis=0)


def _unflat_b1(f):
    return f[0:128][None], f[128:256][None], f[256:257, :LANES]


def kernel(x, positions, e_g_in, e_w_in, e_g_q_a, e_w_q_up, e_g_kv_a, e_w_kv_up, e_sinks, e_w_out, o_g_in, o_w_in, o_b_f, o_w_out, g_final, loss_target, m_e_g_in, m_e_w_in, m_e_g_q_a, m_e_w_q_up, m_e_g_kv_a, m_e_w_kv_up, m_e_sinks, m_e_w_out, m_o_g_in, m_o_w_in, m_o_b_f, m_o_w_out, m_g_final, v_e_g_in, v_e_w_in, v_e_g_q_a, v_e_w_q_up, v_e_g_kv_a, v_e_w_kv_up, v_e_sinks, v_e_w_out, v_o_g_in, v_o_w_in, v_o_b_f, v_o_w_out, v_g_final):
    def bf(a):
        return a.astype(BF16)

    shard0 = jnp.concatenate([_pad_rows(bf(e_w_in[0]).T, RA0), _rows_b0(bf(e_w_q_up[0]), bf(e_w_kv_up[0]))], axis=0)
    gath0 = _all_gather(shard0, name="weights0_all_gather")
    w0t = _layer0_in_weight_t(gath0[:, :N_E_IN].reshape(N_DEV * N_E_IN, WIDE))
    wq = _q_up_weight(_gathered_cols(gath0[:, RA0:RA0 + 24], MLA_Q_RANK))
    wkv = _kv_up_weight(_gathered_cols(gath0[:, RA0 + 32:RA0 + 48], MLA_KV_RANK))

    g_bits = lax.bitcast_convert_type(o_g_in.reshape(LANES), BF16)
    shard1 = jnp.concatenate([_pad_rows(bf(o_w_in[0]).T, RA1), _rows_b1(bf(o_w_out[0]), bf(e_w_out[0]), g_bits)], axis=0)

    def unpack1(gath1):
        w1t, wft = _layer1_in_weight_t(gath1[:, :N_O_IN].reshape(N_DEV * N_O_IN, WIDE))
        wo1 = gath1[:, RA1:RA1 + 128].reshape(D_MODEL, D_MODEL)
        wo0 = gath1[:, RA1 + 128:RA1 + 256].reshape(D_MODEL, D_MODEL)
        bits = gath1[:, RA1 + 256, :2 * LANES].reshape(N_DEV, LANES, 2)
        return wo0, lax.bitcast_convert_type(bits, F32).reshape(1, D_MODEL), w1t, wft, wo1

    def scatter1(g):
        d_in_t = _layer1_in_grad_t(g["w1t"], g["wft"]).reshape(N_DEV, N_O_IN, WIDE)
        d_o_g = jnp.pad(g["o_g_in"].reshape(N_DEV, 1, LANES), ((0, 0), (0, 15), (0, WIDE - LANES)))
        return jnp.concatenate([_pad_rows(d_in_t, RA1), g["wo1"].reshape(N_DEV, 128, WIDE),
                                g["wo0"].reshape(N_DEV, 128, WIDE), d_o_g], axis=1).astype(BF16)

    def scatter0(g):
        return jnp.concatenate([
            _pad_rows(_layer0_in_grad_t(g["w0t"]).reshape(N_DEV, N_E_IN, WIDE), RA0),
            _pad_rows(_scatter_cols(_q_up_grad(g["wq"])), 32), _scatter_cols(_kv_up_grad(g["wkv"]))], axis=1).astype(BF16)

    gr = _local_step(x[0], positions[0], loss_target[0], e_g_in, w0t, e_g_q_a, wq, e_g_kv_a, wkv, e_sinks,
                     (shard1, unpack1), o_b_f, g_final, scatter1=scatter1, scatter0=scatter0)

    def in_projection(recv, ra, n, w, m, v, name):
        g = _sum8(recv, ra, name=name + "_grad_sum")[:n].T
        d, nm, nv = _adamw_native(g, w[0], m[0], v[0], name=name + "_adamw")
        return g[None], d[None], nm[None], nv[None]

    o_in = in_projection(gr["recv1"], RA1, N_O_IN, o_w_in, m_o_w_in, v_o_w_in, "o_w_in")
    b1 = _adamw(gr["recv1"][:, RA1:], _rows_b1(o_w_out[0], e_w_out[0], o_g_in),
                _rows_b1(m_o_w_out[0], m_e_w_out[0], m_o_g_in), _rows_b1(v_o_w_out[0], v_e_w_out[0], v_o_g_in),
                name="adamw_late")

    sent0, recv0 = _peer_wait("exchange", *gr["pending0"], after=[o_in[1], b1[1]], name="grads0_wait")
    me = 4 * lax.axis_index("x") + 2 * lax.axis_index("y") + lax.axis_index("c")
    own = lax.dynamic_slice_in_dim(sent0, me, 1, axis=0)
    recv0 = lax.dynamic_update_slice_in_dim(recv0, own, me, axis=0)
    e_in = in_projection(recv0, RA0, N_E_IN, e_w_in, m_e_w_in, v_e_w_in, "e_w_in")
    b0 = _adamw(recv0[:, RA0:], _rows_b0(e_w_q_up[0], e_w_kv_up[0]), _rows_b0(m_e_w_q_up[0], m_e_w_kv_up[0]),
                _rows_b0(v_e_w_q_up[0], v_e_w_kv_up[0]), name="adamw_early")

    def sharded(k):
        q_up, kv_up = _unflat_b0(b0[k])
        o_out, e_out, o_g = _unflat_b1(b1[k])
        return e_in[k], q_up, kv_up, e_out, o_in[k], o_out, o_g

    g_sh, d_sh, m_sh, v_sh = [sharded(k) for k in range(4)]

    small = _small_pack(gr["e_g_in"], gr["g_final"], gr["e_g_q_a"], gr["e_g_kv_a"], gr["e_sinks"], gr["o_b_f"], gr["loss"])
    small_all = _all_gather(small, name="small_all_gather")
    zero = jnp.zeros((), F32)
    w_small = _small_pack(e_g_in, g_final, e_g_q_a, e_g_kv_a, e_sinks, o_b_f, zero)
    m_small = _small_pack(m_e_g_in, m_g_final, m_e_g_q_a, m_e_g_kv_a, m_e_sinks, m_o_b_f, zero)
    v_small = _small_pack(v_e_g_in, v_g_final, v_e_g_q_a, v_e_g_kv_a, v_e_sinks, v_o_b_f, zero)
    smalls = _adamw(small_all, w_small, m_small, v_small, name="adamw_replicated")
    g_sm, d_sm, m_sm, v_sm = [_small_unpack(a) for a in smalls]
    loss = g_sm[6]

    def leaves(sh, sm):
        return (sm[0], sh[0], sm[2], sh[1], sm[3], sh[2], sm[4], sh[3], sh[6], sh[4], sm[5], sh[5], sm[1])

    return (loss, gr["grad_x"][None], *leaves(g_sh, g_sm), *leaves(d_sh, d_sm), *leaves(m_sh, m_sm), *leaves(v_sh, v_sm))
```

```python
import functools

import jax
import jax.numpy as jnp
from jax import lax
from jax.experimental import pallas as pl
from jax.experimental.pallas import tpu as pltpu

F32 = jnp.float32
BF16 = jnp.bfloat16
NEG_INF = float("-inf")

N_DEV = 8
LANES = 128
D_MODEL = 1024
EPS = 1e-6
ROPE_THETA = 10000.0
MLA_HEADS = 8
MLA_Q_RANK = 256
MLA_KV_RANK = 128
MLA_NOPE = 64
MLA_ROPE = 32
MLA_V = 64
SWA_HEADS = 8
SWA_KV_HEADS = 2
SWA_DIM = 64
WINDOW = 128
FOX_HEADS = 16
FOX_DIM = 64

ADAM_LR = 0.001
ADAM_B1 = 0.9
ADAM_B2 = 0.999
ADAM_EPS = 1e-08
ADAM_WD = 0.01
ADAM_STEP = 10

ATT_T = 512
ATT_T_FWD = 1024
VMEM_LIMIT = 56 * 1024 * 1024
MATMUL_B_BLOCK_BYTES = 8 * 1024 * 1024

Z0A_UNITS = 12
Z0B_UNITS = 6

WIDE = 1024
N_E_IN = 276
N_O_IN = 514
RA0 = 288
RB0 = 32 + 16
RA1 = 528
RB1 = 128 + 128 + 16
SMALL_ROWS = 24


def _tile(n, cands):
    for c in cands:
        if n % c == 0:
            return c
    raise ValueError(f"no tile for {n}")


ROW_TILES = (512, 256, 128)


def _params(sem, vmem=VMEM_LIMIT):
    return pltpu.CompilerParams(dimension_semantics=sem, vmem_limit_bytes=vmem)


def _matmul(a, b, *, name, ta=False, tb=False, out_dtype=F32, b_rows=None):
    if ta:
        kdim, m = a.shape
    else:
        m, kdim = a.shape
    if tb:
        n, kb = b.shape
    else:
        kb, n = b.shape
    assert kdim == kb, (a.shape, b.shape)
    b_start = 0
    if b_rows is not None:
        assert tb
        b_start, n = b_rows
    tm = _tile(m, (512, 256, 128))
    tn = _tile(n, [c for c in (1024, 768, 512, 384, 256, 128)
                   if c * kdim * b.dtype.itemsize <= MATMUL_B_BLOCK_BYTES and b_start % c == 0])
    assert b_start % tn == 0, (b_start, tn)
    b_off = b_start // tn
    dims = (((0 if ta else 1,), (1 if tb else 0,)), ((), ()))

    def body(a_ref, b_ref, o_ref):
        r = lax.dot_general(a_ref[...].astype(BF16), b_ref[...].astype(BF16), dims, preferred_element_type=F32)
        o_ref[...] = r.astype(out_dtype)

    a_spec = pl.BlockSpec((kdim, tm), lambda i, j: (0, i)) if ta else pl.BlockSpec((tm, kdim), lambda i, j: (i, 0))
    b_spec = pl.BlockSpec((tn, kdim), lambda i, j: (j + b_off, 0)) if tb else pl.BlockSpec((kdim, tn), lambda i, j: (0, j))
    return pl.pallas_call(
        body, name=name, grid=(m // tm, n // tn), in_specs=[a_spec, b_spec],
        out_specs=pl.BlockSpec((tm, tn), lambda i, j: (i, j)), out_shape=jax.ShapeDtypeStruct((m, n), out_dtype),
        compiler_params=_params(("parallel", "parallel")),
    )(a, b)


def _rmsnorm_fwd(x, g, *, width, col_blk, name, after=()):
    s = x.shape[0]
    tm = _tile(s, ROW_TILES)

    def body(x_ref, g_ref, *rest):
        y_ref = rest[-1]
        xf = x_ref[...].astype(F32)
        r = lax.rsqrt(jnp.mean(xf * xf, axis=-1, keepdims=True) + EPS)
        y_ref[...] = ((xf * r) * g_ref[...]).astype(BF16)

    return pl.pallas_call(
        body, name=name, grid=(s // tm,),
        in_specs=[pl.BlockSpec((tm, width), lambda i: (i, col_blk)), pl.BlockSpec((1, width), lambda i: (0, 0))]
        + [ANY] * len(after),
        out_specs=pl.BlockSpec((tm, width), lambda i: (i, 0)),
        out_shape=jax.ShapeDtypeStruct((s, width), BF16),
        compiler_params=_params(("parallel",)),
    )(x, g, *after)


def _rmsnorm_bwd(x, g, dy, *, width, col_blk, name):
    s = x.shape[0]
    tm = _tile(s, ROW_TILES)

    def body(x_ref, g_ref, dy_ref, dx_ref, dg_ref):
        @pl.when(pl.program_id(0) == 0)
        def _():
            dg_ref[...] = jnp.zeros_like(dg_ref)

        dx, dg = _rms_bwd_epilogue(dy_ref[...], x_ref[...], 0.0, g_ref[...])
        dg_ref[...] += dg
        dx_ref[...] = dx.astype(BF16)

    return pl.pallas_call(
        body, name=name, grid=(s // tm,),
        in_specs=[pl.BlockSpec((tm, width), lambda i: (i, col_blk)), pl.BlockSpec((1, width), lambda i: (0, 0)),
                  pl.BlockSpec((tm, width), lambda i: (i, 0))],
        out_specs=[pl.BlockSpec((tm, width), lambda i: (i, 0)), pl.BlockSpec((1, width), lambda i: (0, 0))],
        out_shape=[jax.ShapeDtypeStruct((s, width), BF16), jax.ShapeDtypeStruct((1, width), F32)],
        compiler_params=_params(("arbitrary",)),
    )(x, g, dy)


def _sigmoid(x):
    return 1.0 / (1.0 + jnp.exp(-x))


def _gate_fwd(o_parts, gate, *, name):
    s = gate.shape[0]
    tm = _tile(s, ROW_TILES)
    n_o = len(o_parts)

    def body(*refs):
        o_refs, g_ref, y_ref = refs[:n_o], refs[n_o], refs[n_o + 1]
        o = o_refs[0][...] if n_o == 1 else jnp.concatenate([r[...] for r in o_refs], axis=1)
        gt = g_ref[...]
        y_ref[...] = (o * (gt * _sigmoid(gt))).astype(BF16)

    in_specs = [pl.BlockSpec((tm, o.shape[1]), lambda i: (i, 0)) for o in o_parts]
    in_specs.append(pl.BlockSpec((tm, D_MODEL), lambda i: (i, 0)))
    return pl.pallas_call(
        body, name=name, grid=(s // tm,), in_specs=in_specs,
        out_specs=pl.BlockSpec((tm, D_MODEL), lambda i: (i, 0)),
        out_shape=jax.ShapeDtypeStruct((s, D_MODEL), BF16),
        compiler_params=_params(("parallel",)),
    )(*o_parts, gate)


def _matmul_rows(terms, row_inputs, params, epilogue, outs, *, name, rider=None):
    s = terms[0][0].shape[0]
    tm = _tile(s, ROW_TILES)
    steps = s // tm
    n_t, n_r, n_p, n_o = len(terms), len(row_inputs), len(params), len(outs)

    def body(*refs):
        refs, ride_refs = _split_rider(refs, rider, n_in=2 * n_t + n_r + n_p, n_out=n_o)
        t_refs, r_refs = refs[:2 * n_t], refs[2 * n_t:2 * n_t + n_r]
        p_refs, o_refs = refs[2 * n_t + n_r:2 * n_t + n_r + n_p], refs[2 * n_t + n_r + n_p:]
        i = pl.program_id(0)
        _ride_start(rider, ride_refs, i == 0)
        acc = None
        for k, term in enumerate(terms):
            dims = (((1,), (1 if term[2] else 0,)), ((), ()))
            part = lax.dot_general(t_refs[2 * k][...].astype(BF16), t_refs[2 * k + 1][...].astype(BF16), dims,
                                   preferred_element_type=F32)
            acc = part if acc is None else acc + part
        vals = epilogue(acc, *[r[...] for r in r_refs], *[p[...] for p in p_refs])
        for ref, val, out in zip(o_refs, vals, outs):
            if out[0] == "rows":
                ref[...] = val.astype(ref.dtype)
            else:
                @pl.when(i == 0)
                def _(ref=ref):
                    ref[...] = jnp.zeros_like(ref)

                ref[...] += val
        _ride_wait(rider, ride_refs, i == steps - 1)

    in_specs, args = [], []
    for term in terms:
        a, b = term[0], term[1]
        b_rows = b.shape[0] if term[2] or len(term) < 4 else a.shape[1]
        b_blk = 0 if len(term) < 4 else term[3] // b_rows
        in_specs += [pl.BlockSpec((tm, a.shape[1]), lambda i: (i, 0)),
                     _resident((b_rows, b.shape[1]), lambda i, b_blk=b_blk: (b_blk, 0))]
        args += [a, b]
    for arr, width, col_blk in row_inputs:
        in_specs.append(pl.BlockSpec((tm, width), lambda i, col_blk=col_blk: (i, col_blk)))
        args.append(arr)
    for p in params:
        in_specs.append(pl.BlockSpec(p.shape, lambda i: (0, 0)))
        args.append(p)
    out_specs, out_shape = [], []
    for out in outs:
        if out[0] == "rows":
            out_specs.append(pl.BlockSpec((tm, out[1]), lambda i: (i, 0)))
            out_shape.append(jax.ShapeDtypeStruct((s, out[1]), out[2]))
        else:
            out_specs.append(pl.BlockSpec(out[1], lambda i: (0, 0)))
            out_shape.append(jax.ShapeDtypeStruct(out[1], F32))
    scratch = _add_rider(rider, in_specs, args, out_specs, out_shape)
    return pl.pallas_call(
        body, name=name, grid=(steps,), in_specs=in_specs, out_specs=out_specs, out_shape=out_shape,
        scratch_shapes=scratch, compiler_params=_params(("arbitrary",)),
    )(*args)


def _rms_stats(x):
    r = lax.rsqrt(jnp.mean(x * x, axis=-1, keepdims=True) + EPS)
    return r, x * r


def _residual_norm_epilogue(r, x, g):
    x1 = x + r
    _, xh = _rms_stats(x1)
    return x1, xh * g


def _rms_bwd_epilogue(dy, x, add, g):
    r, xh = _rms_stats(x)
    dxh = dy * g
    dx = r * (dxh - xh * jnp.mean(dxh * xh, axis=-1, keepdims=True)) + add
    return dx, jnp.sum(dy * xh, axis=0, keepdims=True)


def _loss_epilogue(r, x1, target, g):
    rs, xh = _rms_stats(x1 + r)
    err = xh * g - target
    loss = jnp.broadcast_to(0.5 * jnp.sum(jnp.mean(err * err, axis=-1, keepdims=True)), (8, LANES))
    dy = err * (1.0 / D_MODEL)
    dxh = dy * g
    dx = rs * (dxh - xh * jnp.mean(dxh * xh, axis=-1, keepdims=True))
    return dx, loss, jnp.sum(dy * xh, axis=0, keepdims=True)


def _gate_bwd_epilogue(widths):
    def epilogue(d, *rows):
        o_parts, gt = rows[:-1], rows[-1]
        o = o_parts[0] if len(o_parts) == 1 else jnp.concatenate(o_parts, axis=1)
        sg = _sigmoid(gt)
        do = d * (gt * sg)
        d_gate = d * o * (sg * (1.0 + gt * (1.0 - sg)))
        cuts = [sum(widths[:k]) for k in range(len(widths) + 1)]
        return tuple(do[:, cuts[k]:cuts[k + 1]] for k in range(len(widths))) + (d_gate,)

    return epilogue


def _rot_half(x):
    lane = lax.broadcasted_iota(jnp.int32, x.shape, 1)
    return jnp.where(lane < 80, pltpu.roll(x, LANES - 16, axis=1), pltpu.roll(x, 16, axis=1))


def _rot_half_t(g):
    lane = lax.broadcasted_iota(jnp.int32, g.shape, 1)
    lo = (lane >= MLA_NOPE) & (lane < MLA_NOPE + MLA_ROPE // 2)
    hi = (lane >= MLA_NOPE + MLA_ROPE // 2) & (lane < MLA_NOPE + MLA_ROPE)
    return jnp.where(lo, pltpu.roll(g, LANES - 16, axis=1), jnp.where(hi, pltpu.roll(g, 16, axis=1), 0.0))


def _rope_fwd(qp, kvp, z0a, cos_t, sin_t, *, name):
    s = qp.shape[0]
    tm = _tile(s, ROW_TILES)
    hw = MLA_HEADS * LANES

    def body(q_ref, k_ref, kpe_ref, c_ref, s_ref, qm_ref, km_ref):
        c = c_ref[...]
        sn = s_ref[...]
        kpe = kpe_ref[...]
        kpe_r = (kpe * c + _rot_half(kpe) * sn).astype(BF16)
        lane = lax.broadcasted_iota(jnp.int32, kpe.shape, 1)
        for h in range(MLA_HEADS):
            sl = slice(h * LANES, (h + 1) * LANES)
            qh = q_ref[:, sl]
            qm_ref[:, sl] = (qh * c + _rot_half(qh) * sn).astype(BF16)
            km_ref[:, sl] = jnp.where(lane < MLA_NOPE, k_ref[:, sl], kpe_r)

    return pl.pallas_call(
        body, name=name, grid=(s // tm,),
        in_specs=[pl.BlockSpec((tm, hw), lambda i: (i, 0)), pl.BlockSpec((tm, hw), lambda i: (i, 0)),
                  pl.BlockSpec((tm, LANES), lambda i: (i, 11)),
                  pl.BlockSpec((tm, LANES), lambda i: (i, 0)), pl.BlockSpec((tm, LANES), lambda i: (i, 0))],
        out_specs=[pl.BlockSpec((tm, hw), lambda i: (i, 0)), pl.BlockSpec((tm, hw), lambda i: (i, 0))],
        out_shape=[jax.ShapeDtypeStruct((s, hw), BF16), jax.ShapeDtypeStruct((s, hw), BF16)],
        compiler_params=_params(("parallel",)),
    )(qp, kvp, z0a, cos_t, sin_t)


def _rope_bwd(dqm, dkm, dvm, cos_t, sin_t, *, name):
    s = dqm.shape[0]
    tm = _tile(s, ROW_TILES)
    hw = MLA_HEADS * LANES
    vw = MLA_HEADS * MLA_V

    def body(dq_ref, dk_ref, dv_ref, c_ref, s_ref, dqp_ref, dkv_ref, dkpe_ref):
        c = c_ref[...]
        sn = s_ref[...]
        ksum = jnp.zeros((tm, LANES), F32)
        for h in range(MLA_HEADS):
            sl = slice(h * LANES, (h + 1) * LANES)
            dq = dq_ref[:, sl]
            dqp_ref[:, sl] = (dq * c + _rot_half_t(dq * sn)).astype(BF16)
            dk = dk_ref[:, sl]
            dkv_ref[:, sl] = dk.astype(BF16)
            ksum = ksum + dk
        dkv_ref[:, hw:] = dv_ref[...]
        lane = lax.broadcasted_iota(jnp.int32, ksum.shape, 1)
        dkpe = ksum * c + _rot_half_t(ksum * sn)
        dkpe_ref[...] = jnp.where((lane >= MLA_NOPE) & (lane < MLA_NOPE + MLA_ROPE), dkpe, 0.0).astype(BF16)

    return pl.pallas_call(
        body, name=name, grid=(s // tm,),
        in_specs=[pl.BlockSpec((tm, hw), lambda i: (i, 0)), pl.BlockSpec((tm, hw), lambda i: (i, 0)),
                  pl.BlockSpec((tm, vw), lambda i: (i, 0)),
                  pl.BlockSpec((tm, LANES), lambda i: (i, 0)), pl.BlockSpec((tm, LANES), lambda i: (i, 0))],
        out_specs=[pl.BlockSpec((tm, hw), lambda i: (i, 0)), pl.BlockSpec((tm, hw + vw), lambda i: (i, 0)),
                   pl.BlockSpec((tm, LANES), lambda i: (i, 0))],
        out_shape=[jax.ShapeDtypeStruct((s, hw), BF16), jax.ShapeDtypeStruct((s, hw + vw), BF16),
                   jax.ShapeDtypeStruct((s, LANES), BF16)],
        compiler_params=_params(("parallel",)),
    )(dqm, dkm, dvm, cos_t, sin_t)


def _head_mask(shape, a):
    lane = lax.broadcasted_iota(jnp.int32, shape, 1)
    return (lane >= 64 * a) & (lane < 64 * (a + 1))


_NT = (((1,), (1,)), ((), ()))
LOG2E = 1.4426950408889634


def _stack_heads(tile, hw):
    lane = lax.broadcasted_iota(jnp.int32, tile.shape, 1)
    z = jnp.zeros_like(tile)
    return jnp.concatenate([jnp.where(lane < hw, tile, z), jnp.where(lane >= hw, tile, z)], axis=0)


def _stacked_rows(r0, r1, t):
    n = r0.shape[-1]
    return jnp.concatenate([jnp.broadcast_to(r0, (t, n)), jnp.broadcast_to(r1, (t, n))], axis=0)


def _resident(block, index_map):
    return pl.BlockSpec(block, index_map, pipeline_mode=pl.Buffered(1))


def _fwd_tile(s):
    return ATT_T_FWD if s % ATT_T_FWD == 0 else min(ATT_T, s)


def _flash_fwd(q, k, v, bias, *, n_pairs, hw, q_off, k_off, v_off, scale, name, rider=None):
    s = q.shape[0]
    t = _fwd_tile(s)
    nb = s // t
    qw = 2 * hw
    has_bias = bias is not None
    c1 = scale * LOG2E

    def body(*refs):
        refs, ride_refs = _split_rider(refs, rider, n_in=4 if has_bias else 3, n_out=2)
        if has_bias:
            q_ref, k_ref, v_ref, b_ref, o_ref, lse_ref, vt_ref, bcol_ref = refs
        else:
            q_ref, k_ref, v_ref, o_ref, lse_ref, vt_ref = refs
            b_ref = bcol_ref = None
        _ride_start(rider, ride_refs, pl.program_id(0) == 0)
        row = lax.broadcasted_iota(jnp.int32, (t, t), 0)
        col = lax.broadcasted_iota(jnp.int32, (t, t), 1)
        cmask_t = jnp.concatenate([row <= col, row <= col], axis=1)
        lane_lt64 = lax.broadcasted_iota(jnp.int32, (t, LANES), 1) < 64

        def as_column(r):
            return jnp.broadcast_to(r, (8, r.shape[1])).T[:, 0:1]

        def v_block(j, _):
            c0 = pl.multiple_of(j * t, t)
            vt_ref[j] = v_ref[pl.ds(c0, t), :].astype(F32).T.astype(BF16)
            if has_bias:
                for a in range(2):
                    bcol_ref[a, pl.ds(c0, t), :] = as_column(b_ref[0, a, j])
            return 0

        lax.fori_loop(0, nb, v_block, 0)

        def stacked_queries(i):
            return _stack_heads(q_ref[pl.ds(pl.multiple_of(i * t, t), t), :], hw).astype(F32).T.astype(BF16)

        def kv_step(j, carry, qs_t, masked):
            m, l, acc = carry
            rows = pl.ds(pl.multiple_of(j * t, t), t)
            sc = jnp.dot(k_ref[rows, :], qs_t, preferred_element_type=F32) * c1
            if has_bias:
                sc = sc + jnp.concatenate([jnp.broadcast_to(bcol_ref[0, rows, :], (t, t)),
                                           jnp.broadcast_to(bcol_ref[1, rows, :], (t, t))], axis=1)
            if masked:
                sc = jnp.where(cmask_t, sc, NEG_INF)
            m_new = jnp.maximum(m, jnp.max(sc, axis=0, keepdims=True))
            alpha = jnp.exp2(m - m_new)
            p = jnp.exp2(sc - m_new)
            l_new = alpha * l + jnp.sum(p, axis=0, keepdims=True)
            pv = jnp.dot(vt_ref[j], p.astype(BF16), preferred_element_type=F32)
            return m_new, l_new, alpha * acc + pv

        def finish(i, carry):
            m, l, acc = carry
            r0 = pl.multiple_of(i * t, t)
            out = (acc / l).T
            lse2 = as_column(m + jnp.log2(l))
            lse_ref[0, 0, pl.ds(r0, t), :] = lse2[:t]
            lse_ref[0, 1, pl.ds(r0, t), :] = lse2[t:]
            o_ref[pl.ds(r0, t), :] = jnp.where(lane_lt64, out[:t], out[t:])

        init = (jnp.full((1, 2 * t), NEG_INF, F32), jnp.zeros((1, 2 * t), F32), jnp.zeros((LANES, 2 * t), F32))

        def q_block(i, _):
            qs_t = stacked_queries(i)
            carry = lax.fori_loop(0, i, lambda j, c: kv_step(j, c, qs_t, False), init)
            finish(i, kv_step(i, carry, qs_t, True))
            return 0

        lax.fori_loop(0, nb, q_block, 0)
        _ride_wait(rider, ride_refs, pl.program_id(0) == n_pairs - 1)

    in_specs = [_resident((s, qw), lambda p: (0, q_off + p)), _resident((s, qw), lambda p: (0, k_off + p)),
                _resident((s, LANES), lambda p: (0, v_off + p))]
    args = [q, k, v]
    if has_bias:
        in_specs.append(_resident((1, 2, nb, 1, t), lambda p: (p, 0, 0, 0, 0)))
        args.append(bias)
    out_specs = [pl.BlockSpec((s, LANES), lambda p: (0, p)), pl.BlockSpec((1, 2, s, 1), lambda p: (p, 0, 0, 0))]
    out_shape = [jax.ShapeDtypeStruct((s, n_pairs * LANES), F32), jax.ShapeDtypeStruct((n_pairs, 2, s, 1), F32)]
    scratch = [pltpu.VMEM((nb, LANES, t), BF16)] + ([pltpu.VMEM((2, s, 1), F32)] if has_bias else [])
    scratch += _add_rider(rider, in_specs, args, out_specs, out_shape)
    return pl.pallas_call(
        body, name=name, grid=(n_pairs,), in_specs=in_specs, out_specs=out_specs, out_shape=out_shape,
        scratch_shapes=scratch,
        compiler_params=_params(("parallel",) if rider is None else ("arbitrary",)),
    )(*args)


def _flash_bwd(q, k, v, do, o, lse, bias, *, n_pairs, hw, q_off, k_off, v_off, scale, qk_dtype, name, rider=None):
    s = q.shape[0]
    t = min(ATT_T, s)
    nb = s // t
    qw = 2 * hw
    has_bias = bias is not None
    c1 = scale * LOG2E

    def body(*refs):
        refs, ride_refs = _split_rider(refs, rider, n_in=7 if has_bias else 6, n_out=5 if has_bias else 3)
        if has_bias:
            (q_ref, k_ref, v_ref, do_ref, o_ref, lse_ref, b_ref, dq_ref, dk_ref, dv_ref, db_ref, dr_ref,
             dkt_ref, dvt_ref) = refs
            db_ref[...] = jnp.zeros_like(db_ref)
        else:
            q_ref, k_ref, v_ref, do_ref, o_ref, lse_ref, dq_ref, dk_ref, dv_ref, dkt_ref, dvt_ref = refs
            b_ref = db_ref = dr_ref = None
        _ride_start(rider, ride_refs, pl.program_id(0) == 0)
        dkt_ref[...] = jnp.zeros_like(dkt_ref)
        dvt_ref[...] = jnp.zeros_like(dvt_ref)
        causal = lax.broadcasted_iota(jnp.int32, (t, t), 1) <= lax.broadcasted_iota(jnp.int32, (t, t), 0)
        cmask = jnp.concatenate([causal, causal], axis=0)
        lane_lt_hw = lax.broadcasted_iota(jnp.int32, (t, qw), 1) < hw

        def q_block(i, _):
            r0 = pl.multiple_of(i * t, t)
            qs = _stack_heads(q_ref[pl.ds(r0, t), :], hw)
            dos = _stack_heads(do_ref[pl.ds(r0, t), :], 64)
            ot = o_ref[pl.ds(r0, t), :]
            delta = jnp.sum(dos * jnp.concatenate([ot, ot], axis=0), axis=-1, keepdims=True)
            lse2 = jnp.concatenate([lse_ref[0, 0, pl.ds(r0, t), :], lse_ref[0, 1, pl.ds(r0, t), :]], axis=0)
            dosb = dos.astype(BF16)
            dos_t = dos.T.astype(BF16)
            qs_t = qs.astype(F32).T.astype(BF16)

            def kv_step(j, carry, masked):
                dq, rsum = carry
                c0 = pl.multiple_of(j * t, t)
                kt = k_ref[pl.ds(c0, t), :]
                vt = v_ref[pl.ds(c0, t), :]
                sc = lax.dot_general(qs, kt, _NT, preferred_element_type=F32) * c1
                if has_bias:
                    sc = sc + _stacked_rows(b_ref[0, 0, j], b_ref[0, 1, j], t)
                if masked:
                    sc = jnp.where(cmask, sc, NEG_INF)
                p = jnp.exp2(sc - lse2)
                dp = lax.dot_general(dosb, vt, _NT, preferred_element_type=F32)
                ds = p * (dp - delta)
                dsb = ds.astype(BF16)
                pb = p.astype(BF16)
                if hw == LANES:
                    dvt_ref[j] += jnp.concatenate(
                        [jnp.dot(dos_t[:64, :t], pb[:t], preferred_element_type=F32),
                         jnp.dot(dos_t[64:, t:], pb[t:], preferred_element_type=F32)], axis=0)
                    dkt_ref[j] += jnp.concatenate(
                        [jnp.dot(qs_t[:hw, :t], dsb[:t], preferred_element_type=F32),
                         jnp.dot(qs_t[hw:, t:], dsb[t:], preferred_element_type=F32)], axis=0)
                else:
                    dvt_ref[j] += jnp.dot(dos_t, pb, preferred_element_type=F32)
                    dkt_ref[j] += jnp.dot(qs_t, dsb, preferred_element_type=F32)
                if has_bias:
                    db_ref[0, 0, j] += jnp.sum(ds[:t], axis=0, keepdims=True)
                    db_ref[0, 1, j] += jnp.sum(ds[t:], axis=0, keepdims=True)
                    rsum = rsum + jnp.sum(ds, axis=-1, keepdims=True)
                return dq + jnp.dot(dsb, kt, preferred_element_type=F32), rsum

            init = (jnp.zeros((2 * t, qw), F32), jnp.zeros((2 * t, 1), F32))
            carry = lax.fori_loop(0, i, functools.partial(kv_step, masked=False), init)
            dq, rsum = kv_step(i, carry, True)
            dq = dq * scale
            dq_ref[pl.ds(r0, t), :] = jnp.where(lane_lt_hw, dq[:t], dq[t:]).astype(qk_dtype)
            if has_bias:
                rsum_row = jnp.broadcast_to(rsum, (2 * t, LANES)).T[0:1]
                dr_ref[0, 0, i] = rsum_row[:, :t]
                dr_ref[0, 1, i] = rsum_row[:, t:]
            return 0

        lax.fori_loop(0, nb, q_block, 0)

        def k_block(j, _):
            c0 = pl.multiple_of(j * t, t)
            dk_ref[pl.ds(c0, t), :] = (dkt_ref[j].T * scale).astype(qk_dtype)
            dv_ref[pl.ds(c0, t), :] = dvt_ref[j].T.astype(BF16)
            return 0

        lax.fori_loop(0, nb, k_block, 0)
        _ride_wait(rider, ride_refs, pl.program_id(0) == n_pairs - 1)

    in_specs = [_resident((s, qw), lambda p: (0, q_off + p)), _resident((s, qw), lambda p: (0, k_off + p)),
                _resident((s, LANES), lambda p: (0, v_off + p)),
                _resident((s, LANES), lambda p: (0, p)), _resident((s, LANES), lambda p: (0, p)),
                _resident((1, 2, s, 1), lambda p: (p, 0, 0, 0))]
    args = [q, k, v, do, o, lse]
    out_specs = [pl.BlockSpec((s, qw), lambda p: (0, p)), pl.BlockSpec((s, qw), lambda p: (0, p)),
                 pl.BlockSpec((s, LANES), lambda p: (0, p))]
    out_shape = [jax.ShapeDtypeStruct((s, n_pairs * qw), qk_dtype), jax.ShapeDtypeStruct((s, n_pairs * qw), qk_dtype),
                 jax.ShapeDtypeStruct((s, n_pairs * LANES), BF16)]
    if has_bias:
        in_specs.append(_resident((1, 2, nb, 1, t), lambda p: (p, 0, 0, 0, 0)))
        args.append(bias)
        for _ in range(2):
            out_specs.append(pl.BlockSpec((1, 2, nb, 1, t), lambda p: (p, 0, 0, 0, 0)))
            out_shape.append(jax.ShapeDtypeStruct((n_pairs, 2, nb, 1, t), F32))
    scratch = [pltpu.VMEM((nb, qw, t), F32), pltpu.VMEM((nb, LANES, t), F32)]
    scratch += _add_rider(rider, in_specs, args, out_specs, out_shape)
    return pl.pallas_call(
        body, name=name, grid=(n_pairs,), in_specs=in_specs, out_specs=out_specs, out_shape=out_shape,
        scratch_shapes=scratch,
        compiler_params=_params(("parallel",) if rider is None else ("arbitrary",)),
    )(*args)


def _alibi_slope(h):
    return 2.0 ** (-8.0 * (h + 1.0) / SWA_HEADS)


SWA_ROWS = 512
SWA_SCALE = SWA_DIM ** -0.5


def _swa_geometry(i):
    w = WINDOW
    r0 = pl.multiple_of(i * w, w)
    b0 = pl.multiple_of(jnp.maximum(i - 1, 0) * w, w)
    row = lax.broadcasted_iota(jnp.int32, (w, 2 * w), 0)
    col = lax.broadcasted_iota(jnp.int32, (w, 2 * w), 1)
    dist = row - col + (r0 - b0)
    valid = (dist >= 0) & (dist < w)
    return r0, b0, dist.astype(F32), valid


def _swa_q_head(qblk, h):
    kv = h // (SWA_HEADS // SWA_KV_HEADS)
    if h % 2 != kv:
        qblk = pltpu.roll(qblk, 64, axis=1)
    return jnp.where(_head_mask(qblk.shape, kv), qblk, 0.0)


SWA_GROUP = SWA_HEADS // SWA_KV_HEADS


def _swa_stack(ref, rs, grp):
    parts = []
    for a in range(SWA_GROUP):
        h = SWA_GROUP * grp + a
        parts.append(_swa_q_head(ref[rs, (h // 2) * LANES:(h // 2 + 1) * LANES].astype(F32), h))
    return jnp.concatenate(parts, axis=0)


def _swa_unstack(x, grp):
    tiles = []
    for a in range(SWA_GROUP):
        h = SWA_GROUP * grp + a
        tile = x[a * WINDOW:(a + 1) * WINDOW]
        tiles.append(pltpu.roll(tile, 64, axis=1) if h % 2 != grp else tile)
    return tiles


def _swa_head_column(vals):
    return jnp.concatenate([jnp.full((WINDOW, 1), v, F32) for v in vals], axis=0)


def _swa_logits(qs, kb, dist, valid, grp):
    slopes = _swa_head_column([_alibi_slope(SWA_GROUP * grp + a) for a in range(SWA_GROUP)])
    dist4 = jnp.concatenate([dist] * SWA_GROUP, axis=0)
    valid4 = jnp.concatenate([valid] * SWA_GROUP, axis=0)
    sc = lax.dot_general(qs, kb, _NT, preferred_element_type=F32) * SWA_SCALE - slopes * dist4
    return jnp.where(valid4, sc, NEG_INF)


def _swa_merge_heads(tiles):
    lt64 = lax.broadcasted_iota(jnp.int32, (WINDOW, LANES), 1) < 64
    return jnp.concatenate([jnp.where(lt64, tiles[2 * b], tiles[2 * b + 1]) for b in range(SWA_HEADS // 2)], axis=1)


def _swa_fwd(z0b, sinks, *, name):
    s = z0b.shape[0]
    w = WINDOW
    rows = min(SWA_ROWS, s)
    per_step = rows // w
    qcols = SWA_HEADS * SWA_DIM

    def body(sink_ref, q_ref, k_ref, v_ref, o_ref, lse_ref):
        g = pl.program_id(0)
        for ii in range(per_step):
            rs = slice(ii * w, (ii + 1) * w)
            r0, b0, dist, valid = _swa_geometry(g * per_step + ii)
            kb = k_ref[pl.ds(b0, 2 * w), :]
            vb = v_ref[pl.ds(b0, 2 * w), :]
            o_tiles = []
            for h in range(SWA_HEADS):
                kv = h // SWA_GROUP
                qh = _swa_q_head(q_ref[rs, (h // 2) * LANES:(h // 2 + 1) * LANES].astype(F32), h).astype(BF16)
                sc = lax.dot_general(qh, kb, _NT, preferred_element_type=F32) * SWA_SCALE - _alibi_slope(h) * dist
                sc = jnp.where(valid, sc, NEG_INF)
                sink = sink_ref[0, h]
                m = jnp.maximum(jnp.max(sc, axis=-1, keepdims=True), sink)
                p = jnp.exp(sc - m)
                l = jnp.sum(p, axis=-1, keepdims=True) + jnp.exp(sink - m)
                oh = jnp.dot(p.astype(BF16), vb, preferred_element_type=F32) / l
                o_tiles.append(pltpu.roll(oh, 64, axis=1) if h % 2 != kv else oh)
                lse_ref[h, rs, :] = m + jnp.log(l)
            o_ref[rs, :] = _swa_merge_heads(o_tiles)

    return pl.pallas_call(
        body, name=name, grid=(s // rows,),
        in_specs=[pl.BlockSpec(memory_space=pltpu.SMEM),
                  pl.BlockSpec((rows, qcols), lambda g: (g, 0)),
                  pl.BlockSpec((s, LANES), lambda g: (0, 4)), pl.BlockSpec((s, LANES), lambda g: (0, 5))],
        out_specs=[pl.BlockSpec((rows, qcols), lambda g: (g, 0)), pl.BlockSpec((SWA_HEADS, rows, 1), lambda g: (0, g, 0))],
        out_shape=[jax.ShapeDtypeStruct((s, qcols), F32), jax.ShapeDtypeStruct((SWA_HEADS, s, 1), F32)],
        compiler_params=_params(("parallel",)),
    )(sinks, z0b, z0b, z0b)


def _swa_bwd(z0b, sinks, do, o, lse, *, name):
    s = z0b.shape[0]
    w = WINDOW
    rows = min(SWA_ROWS, s)
    per_step = rows // w
    qcols = SWA_HEADS * SWA_DIM
    nblk = s // w

    def body(sink_ref, q_ref, k_ref, v_ref, do_ref, o_ref, lse_ref, dq_ref, dkt_ref, dvt_ref, dsink_ref):
        g = pl.program_id(0)

        @pl.when(g == 0)
        def _():
            dkt_ref[...] = jnp.zeros_like(dkt_ref)
            dvt_ref[...] = jnp.zeros_like(dvt_ref)
            dsink_ref[...] = jnp.zeros_like(dsink_ref)

        for ii in range(per_step):
            i = g * per_step + ii
            rs = slice(ii * w, (ii + 1) * w)
            r0, b0, dist, valid = _swa_geometry(i)
            j0 = jnp.maximum(i - 1, 0)
            kb = k_ref[pl.ds(b0, 2 * w), :]
            vb = v_ref[pl.ds(b0, 2 * w), :]
            dq_tiles = []
            for grp in range(SWA_KV_HEADS):
                heads = [SWA_GROUP * grp + a for a in range(SWA_GROUP)]
                qs32 = _swa_stack(q_ref, rs, grp)
                dos32 = _swa_stack(do_ref, rs, grp)
                delta = jnp.sum(dos32 * _swa_stack(o_ref, rs, grp), axis=-1, keepdims=True)
                lse = jnp.concatenate([lse_ref[h, rs, :] for h in heads], axis=0)
                sink = _swa_head_column([sink_ref[0, h] for h in heads])
                p = jnp.exp(_swa_logits(qs32.astype(BF16), kb, dist, valid, grp) - lse)
                dp = lax.dot_general(dos32.astype(BF16), vb, _NT, preferred_element_type=F32)
                ds = p * (dp - delta)
                dsb = ds.astype(BF16)
                d_sink = jnp.exp(sink - lse) * delta
                for a, h in enumerate(heads):
                    dsink_ref[h:h + 1, :] += jnp.broadcast_to(-jnp.sum(d_sink[a * w:(a + 1) * w]), (1, LANES))
                dvt = jnp.dot(dos32.T.astype(BF16), p.astype(BF16), preferred_element_type=F32)
                dkt = jnp.dot(qs32.T.astype(BF16), dsb, preferred_element_type=F32) * SWA_SCALE
                dvt_ref[j0] += dvt[:, :w]
                dvt_ref[j0 + 1] += dvt[:, w:]
                dkt_ref[j0] += dkt[:, :w]
                dkt_ref[j0 + 1] += dkt[:, w:]
                dq_tiles += _swa_unstack(jnp.dot(dsb, kb, preferred_element_type=F32) * SWA_SCALE, grp)
            dq_ref[rs, :] = _swa_merge_heads(dq_tiles)

    return pl.pallas_call(
        body, name=name, grid=(s // rows,),
        in_specs=[pl.BlockSpec(memory_space=pltpu.SMEM),
                  pl.BlockSpec((rows, qcols), lambda g: (g, 0)),
                  pl.BlockSpec((s, LANES), lambda g: (0, 4)), pl.BlockSpec((s, LANES), lambda g: (0, 5)),
                  pl.BlockSpec((rows, qcols), lambda g: (g, 0)), pl.BlockSpec((rows, qcols), lambda g: (g, 0)),
                  pl.BlockSpec((SWA_HEADS, rows, 1), lambda g: (0, g, 0))],
        out_specs=[pl.BlockSpec((rows, qcols), lambda g: (g, 0)),
                   pl.BlockSpec((nblk, LANES, w), lambda g: (0, 0, 0)),
                   pl.BlockSpec((nblk, LANES, w), lambda g: (0, 0, 0)),
                   pl.BlockSpec((SWA_HEADS, LANES), lambda g: (0, 0))],
        out_shape=[jax.ShapeDtypeStruct((s, qcols), F32),
                   jax.ShapeDtypeStruct((nblk, LANES, w), F32), jax.ShapeDtypeStruct((nblk, LANES, w), F32),
                   jax.ShapeDtypeStruct((SWA_HEADS, LANES), F32)],
        compiler_params=_params(("arbitrary",)),
    )(sinks, z0b, z0b, z0b, do, o, lse)


CUM_T = 256


def _split3(x):
    hi = x.astype(BF16)
    r1 = x - hi.astype(F32)
    mid = r1.astype(BF16)
    lo = (r1 - mid.astype(F32)).astype(BF16)
    return hi, mid, lo


def _tri_dot(tri, x):
    hi, mid, lo = _split3(x)
    out = jnp.dot(tri, hi, preferred_element_type=F32)
    out = out + jnp.dot(tri, mid, preferred_element_type=F32)
    return out + jnp.dot(tri, lo, preferred_element_type=F32)


def _logf_fwd(zf, bf, *, name):
    s = zf.shape[0]
    t = CUM_T
    nb = s // t

    def body(z_ref, b_ref, c_ref, carry_ref):
        i = pl.program_id(0)

        @pl.when(i == 0)
        def _():
            carry_ref[...] = jnp.zeros_like(carry_ref)

        x = z_ref[...] + b_ref[...]
        lf = jnp.minimum(x, 0.0) - jnp.log(1.0 + jnp.exp(-jnp.abs(x)))
        row = lax.broadcasted_iota(jnp.int32, (t, t), 0)
        col = lax.broadcasted_iota(jnp.int32, (t, t), 1)
        tri = jnp.where(col <= row, 1.0, 0.0).astype(BF16)
        c = _tri_dot(tri, lf) + carry_ref[...]
        c_ref[...] = c
        carry_ref[...] = c[t - 1:t, :]

    return pl.pallas_call(
        body, name=name, grid=(nb,),
        in_specs=[pl.BlockSpec((t, LANES), lambda i: (i, 0)), pl.BlockSpec((1, LANES), lambda i: (0, 0))],
        out_specs=pl.BlockSpec((t, LANES), lambda i: (i, 0)),
        out_shape=jax.ShapeDtypeStruct((s, LANES), F32),
        scratch_shapes=[pltpu.VMEM((1, LANES), F32)],
        compiler_params=_params(("arbitrary",)),
    )(zf, bf)


def _logf_bwd(dc, zf, bf, *, name):
    s = zf.shape[0]
    t = CUM_T
    nb = s // t

    def body(dc_ref, z_ref, b_ref, dz_ref, db_ref, carry_ref):
        i = pl.program_id(0)

        @pl.when(i == 0)
        def _():
            carry_ref[...] = jnp.zeros_like(carry_ref)
            db_ref[...] = jnp.zeros_like(db_ref)

        row = lax.broadcasted_iota(jnp.int32, (t, t), 0)
        col = lax.broadcasted_iota(jnp.int32, (t, t), 1)
        tri = jnp.where(col >= row, 1.0, 0.0).astype(BF16)
        dlf = _tri_dot(tri, dc_ref[...]) + carry_ref[...]
        carry_ref[...] = dlf[0:1, :]
        x = z_ref[...] + b_ref[...]
        dz = dlf * _sigmoid(-x)
        dz_ref[...] = dz.astype(BF16)
        db_ref[...] += jnp.sum(dz, axis=0, keepdims=True)

    return pl.pallas_call(
        body, name=name, grid=(nb,),
        in_specs=[pl.BlockSpec((t, LANES), lambda i: (nb - 1 - i, 0)), pl.BlockSpec((t, LANES), lambda i: (nb - 1 - i, 0)),
                  pl.BlockSpec((1, LANES), lambda i: (0, 0))],
        out_specs=[pl.BlockSpec((t, LANES), lambda i: (nb - 1 - i, 0)), pl.BlockSpec((1, LANES), lambda i: (0, 0))],
        out_shape=[jax.ShapeDtypeStruct((s, LANES), BF16), jax.ShapeDtypeStruct((1, LANES), F32)],
        scratch_shapes=[pltpu.VMEM((1, LANES), F32)],
        compiler_params=_params(("arbitrary",)),
    )(dc, zf, bf)


def _sum_pieces(p_ref):
    g = p_ref[0].astype(F32)
    for k in range(1, N_DEV):
        g = g + p_ref[k].astype(F32)
    return g


def _adam_update(g, w, m, v):
    bc1 = 1.0 - ADAM_B1 ** ADAM_STEP
    bc2 = 1.0 - ADAM_B2 ** ADAM_STEP
    nm = ADAM_B1 * m + (1.0 - ADAM_B1) * g
    nv = ADAM_B2 * v + (1.0 - ADAM_B2) * (g * g)
    m_hat = nm / bc1
    v_hat = nv / bc2
    return -ADAM_LR * (m_hat / (jnp.sqrt(v_hat) + ADAM_EPS) + ADAM_WD * w), nm, nv


def _adamw(pieces, w, m, v, *, name):
    rows, cols = w.shape
    tr = _tile(rows, (RB1, RB0, SMALL_ROWS))

    def body(p_ref, w_ref, m_ref, v_ref, g_ref, d_ref, nm_ref, nv_ref):
        g = _sum_pieces(p_ref)
        g_ref[...] = g
        d_ref[...], nm_ref[...], nv_ref[...] = _adam_update(g, w_ref[...], m_ref[...], v_ref[...])

    spec = pl.BlockSpec((tr, cols), lambda i: (i, 0))
    shape = jax.ShapeDtypeStruct((rows, cols), F32)
    return pl.pallas_call(
        body, name=name, grid=(rows // tr,),
        in_specs=[pl.BlockSpec((N_DEV, tr, cols), lambda i: (0, i, 0)), spec, spec, spec],
        out_specs=[spec, spec, spec, spec], out_shape=[shape, shape, shape, shape],
        compiler_params=_params(("parallel",)),
    )(pieces, w, m, v)


def _sum8(pieces, rows, *, name):
    cols = pieces.shape[2]
    tr = _tile(rows, (176, 96))

    def body(p_ref, g_ref):
        g_ref[...] = _sum_pieces(p_ref)

    return pl.pallas_call(
        body, name=name, grid=(rows // tr,),
        in_specs=[pl.BlockSpec((N_DEV, tr, cols), lambda i: (0, i, 0))],
        out_specs=pl.BlockSpec((tr, cols), lambda i: (i, 0)),
        out_shape=jax.ShapeDtypeStruct((rows, cols), F32),
        compiler_params=_params(("parallel",)),
    )(pieces)


def _adamw_native(g, w, m, v, *, name):
    rows, cols = w.shape
    tr = _tile(rows, (256, 128))

    def body(g_ref, w_ref, m_ref, v_ref, d_ref, nm_ref, nv_ref):
        d_ref[...], nm_ref[...], nv_ref[...] = _adam_update(g_ref[...], w_ref[...], m_ref[...], v_ref[...])

    spec = pl.BlockSpec((tr, cols), lambda i: (i, 0))
    shape = jax.ShapeDtypeStruct((rows, cols), F32)
    return pl.pallas_call(
        body, name=name, grid=(rows // tr,), in_specs=[spec, spec, spec, spec],
        out_specs=[spec, spec, spec], out_shape=[shape, shape, shape],
        compiler_params=_params(("parallel",)),
    )(g, w, m, v)


MESH = pl.DeviceIdType.MESH
ANY = pl.BlockSpec(memory_space=pl.ANY)


def _all_gather(shard, *, name):
    rows, lanes = shard.shape

    def body(x_ref, out_ref, send_sems, recv_sems, local_sem):
        x, y, c = lax.axis_index("x"), lax.axis_index("y"), lax.axis_index("c")
        me, sibling = (x, y, c), (x, y, 1 - c)
        chips = [(1 - x, y), (x, 1 - y), (1 - x, 1 - y)]

        def block(px, py, pc):
            return out_ref.at[4 * px + 2 * py + pc]

        def copy(k, blk, to, src=None):
            return pltpu.make_async_remote_copy(
                src_ref=block(*blk) if src is None else src, dst_ref=block(*blk),
                send_sem=send_sems.at[k], recv_sem=recv_sems.at[k], device_id=to, device_id_type=MESH)

        mine = pltpu.make_async_copy(x_ref, block(*me), local_sem)
        mine.start()
        first = [copy(0, me, sibling, src=x_ref)]
        first += [copy(1 + j, me, (*chip, c), src=x_ref) for j, chip in enumerate(chips)]
        for cp in first:
            cp.start()
        passed = [copy(4 + j, (*chip, c), sibling) for j, chip in enumerate(chips)]
        for j, chip in enumerate(chips):
            copy(1 + j, (*chip, c), me).wait_recv()
            passed[j].start()
        copy(0, sibling, me).wait_recv()
        for j, chip in enumerate(chips):
            copy(4 + j, (*chip, 1 - c), me).wait_recv()
        for cp in first + passed:
            cp.wait_send()
        mine.wait()

    return pl.pallas_call(
        body, name=name, out_shape=jax.ShapeDtypeStruct((N_DEV, rows, lanes), shard.dtype),
        in_specs=[ANY], out_specs=ANY,
        scratch_shapes=[pltpu.SemaphoreType.DMA((7,)), pltpu.SemaphoreType.DMA((7,)), pltpu.SemaphoreType.DMA(())],
    )(shard)


def _peer_copies(kind, src_ref, out_ref, send_sems, recv_sems, local_sem):
    x, y, c = lax.axis_index("x"), lax.axis_index("y"), lax.axis_index("c")
    me = 4 * x + 2 * y + c

    def src(idx):
        return src_ref.at[idx] if kind == "exchange" else src_ref

    mine = None if local_sem is None else pltpu.make_async_copy(src(me), out_ref.at[me], local_sem)
    copies = []
    for r in range(1, N_DEV):
        px = 1 - x if r & 4 else x
        py = 1 - y if r & 2 else y
        pc = 1 - c if r & 1 else c
        copies.append(pltpu.make_async_remote_copy(
            src_ref=src(4 * px + 2 * py + pc), dst_ref=out_ref.at[me],
            send_sem=send_sems.at[r - 1], recv_sem=recv_sems.at[r - 1],
            device_id=(px, py, pc), device_id_type=MESH))
    return mine, copies


PEER_SEMS = [pltpu.SemaphoreType.DMA((7,)), pltpu.SemaphoreType.DMA((7,)), pltpu.SemaphoreType.DMA(())]


HBM = pl.BlockSpec(memory_space=pltpu.HBM)
SEMAPHORES = pl.BlockSpec(memory_space=pltpu.SEMAPHORE)


def _peer_start(kind, arr, *, name):
    land = lax.empty((N_DEV,) + arr.shape[-2:], arr.dtype)

    def body(src_ref, land_ref, send_sems, recv_sems, src_thru, land_thru, token):
        _, copies = _peer_copies(kind, src_ref, land_ref, send_sems, recv_sems, None)
        for cp in copies:
            cp.start()
        token[...] = jnp.zeros_like(token)

    return pl.pallas_call(
        body, name=name,
        out_shape=(pltpu.SemaphoreType.DMA((N_DEV - 1,)), pltpu.SemaphoreType.DMA((N_DEV - 1,)),
                   pltpu.HBM(arr.shape, arr.dtype), pltpu.HBM(land.shape, land.dtype), jax.ShapeDtypeStruct((8, LANES), F32)),
        in_specs=(HBM, HBM), out_specs=(SEMAPHORES, SEMAPHORES, HBM, HBM, pl.BlockSpec(memory_space=pltpu.VMEM)),
        input_output_aliases={0: 2, 1: 3},
        compiler_params=pltpu.CompilerParams(has_side_effects=pltpu.SideEffectType.DATAFLOW_SIDE_EFFECTING),
    )(pltpu.with_memory_space_constraint(arr, pltpu.HBM), pltpu.with_memory_space_constraint(land, pltpu.HBM))


def _peer_wait(kind, send_sems, recv_sems, src_thru, land_thru, after, *, name):
    def body(src_ref, land_ref, send_sems, recv_sems, *_):
        _, copies = _peer_copies(kind, src_ref, land_ref, send_sems, recv_sems, None)
        for cp in copies:
            cp.wait_send()
            cp.wait_recv()

    return pl.pallas_call(
        body, name=name,
        out_shape=(pltpu.HBM(src_thru.shape, src_thru.dtype), pltpu.HBM(land_thru.shape, land_thru.dtype)),
        in_specs=(HBM, HBM, SEMAPHORES, SEMAPHORES) + (ANY,) * len(after), out_specs=(HBM, HBM),
        input_output_aliases={0: 0, 1: 1},
        compiler_params=pltpu.CompilerParams(has_side_effects=pltpu.SideEffectType.DATAFLOW_SIDE_EFFECTING),
    )(src_thru, land_thru, send_sems, recv_sems, *after)


def _add_rider(rider, in_specs, args, out_specs, out_shape):
    if rider is None:
        return []
    _, arr = rider
    in_specs.append(ANY)
    args.append(arr)
    out_specs.append(ANY)
    out_shape.append(jax.ShapeDtypeStruct((N_DEV,) + arr.shape[-2:], arr.dtype))
    return list(PEER_SEMS)


def _split_rider(refs, rider, n_in, n_out):
    if rider is None:
        return refs, None
    refs = list(refs)
    rin = refs.pop(n_in)
    rout = refs.pop(n_in + n_out)
    return refs[:-3], (rin, rout, *refs[-3:])


def _ride_start(rider, ride_refs, first):
    if rider is None:
        return

    @pl.when(first)
    def _():
        mine, copies = _peer_copies(rider[0], *ride_refs)
        mine.start()
        for cp in copies:
            cp.start()


def _ride_wait(rider, ride_refs, last):
    if rider is None:
        return

    @pl.when(last)
    def _():
        mine, copies = _peer_copies(rider[0], *ride_refs)
        for cp in copies:
            cp.wait()
        mine.wait()


def _gathered_cols(blocks, kdim):
    n = blocks.shape[1] * WIDE // kdim
    return blocks.reshape(N_DEV, kdim, n).transpose(1, 0, 2).reshape(kdim, N_DEV * n)


def _scatter_cols(dw):
    kdim, n8 = dw.shape
    n = n8 // N_DEV
    return dw.reshape(kdim, N_DEV, n).transpose(1, 0, 2).reshape(N_DEV, kdim * n // WIDE, WIDE)


def _pad_rows(a, rows):
    pad = [(0, 0)] * a.ndim
    pad[-2] = (0, rows - a.shape[-2])
    return jnp.pad(a, pad)


def _layer0_in_weight_t(wt):
    cq, ckv, kpe = wt[0:256], wt[256:384], wt[384:416]
    q_s, k_s, v_s, gate = wt[416:928], wt[928:1056], wt[1056:1184], wt[1184:2208]
    z = jnp.zeros((64, wt.shape[1]), wt.dtype)
    return jnp.concatenate([gate, cq, ckv, z, kpe, z[:32], q_s, k_s, v_s], axis=0)


def _layer0_in_grad_t(dwt):
    gate, cq, ckv, kpe = dwt[0:1024], dwt[1024:1280], dwt[1280:1408], dwt[1472:1504]
    q_s, k_s, v_s = dwt[1536:2048], dwt[2048:2176], dwt[2176:2304]
    return jnp.concatenate([cq, ckv, kpe, q_s, k_s, v_s, gate], axis=0)


def _layer1_in_weight_t(wt):
    main = jnp.concatenate([wt[:3 * D_MODEL], wt[3 * D_MODEL + FOX_HEADS:]], axis=0)
    return main, _pad_rows(wt[3 * D_MODEL:3 * D_MODEL + FOX_HEADS], LANES)


def _layer1_in_grad_t(d_blocks, d_wft):
    return jnp.concatenate([*d_blocks[:3], d_wft[:FOX_HEADS], d_blocks[3]], axis=0)


def _q_up_weight(w):
    return jnp.pad(w.reshape(MLA_Q_RANK, MLA_HEADS, 96), ((0, 0), (0, 0), (0, 32))).reshape(MLA_Q_RANK, MLA_HEADS * LANES)


def _q_up_grad(dwp):
    return dwp.reshape(MLA_Q_RANK, MLA_HEADS, LANES)[:, :, :96].reshape(MLA_Q_RANK, MLA_HEADS * 96)


def _kv_up_weight(w):
    w4 = w.reshape(MLA_KV_RANK, MLA_HEADS, 2, 64)
    kp = jnp.pad(w4[:, :, 0, :], ((0, 0), (0, 0), (0, 64))).reshape(MLA_KV_RANK, MLA_HEADS * LANES)
    vp = w4[:, :, 1, :].reshape(MLA_KV_RANK, MLA_HEADS * 64)
    return jnp.concatenate([kp, vp], axis=1)


def _kv_up_grad(dwp):
    dk = dwp[:, :MLA_HEADS * LANES].reshape(MLA_KV_RANK, MLA_HEADS, LANES)[:, :, :64]
    dv = dwp[:, MLA_HEADS * LANES:].reshape(MLA_KV_RANK, MLA_HEADS, 64)
    return jnp.stack([dk, dv], axis=2).reshape(MLA_KV_RANK, MLA_HEADS * LANES)


def _pad_lanes(a):
    return jnp.pad(a, ((0, 0), (0, LANES - a.shape[1])))


def _small_pack(g_in, g_final, g_q_a, g_kv_a, sinks, b_f, loss):
    rows = [g_in.reshape(8, LANES), g_final.reshape(8, LANES), g_q_a.reshape(2, LANES), g_kv_a.reshape(1, LANES),
            _pad_lanes(sinks.reshape(1, -1)), _pad_lanes(b_f.reshape(1, -1)), _pad_lanes(loss.reshape(1, 1)),
            jnp.zeros((2, LANES), F32)]
    return jnp.concatenate(rows, axis=0)


def _small_unpack(a):
    return (a[0:8].reshape(1, D_MODEL), a[8:16].reshape(D_MODEL), a[16:18].reshape(1, MLA_Q_RANK),
            a[18:19].reshape(1, MLA_KV_RANK), a[19:20, :SWA_HEADS], a[20:21, :FOX_HEADS], a[21, 0])


def _local_step(x, positions, target, e_g_in, early, e_g_q_a, e_g_kv_a, e_sinks,
                late, o_b_f, g_final, scatter1=None, scatter0=None):
    s = x.shape[0]
    mla_scale = (MLA_NOPE + MLA_ROPE) ** -0.5
    fox_scale = FOX_DIM ** -0.5
    n0a = Z0A_UNITS * LANES

    inv_freq = 1.0 / (ROPE_THETA ** (jnp.arange(0, MLA_ROPE, 2, dtype=F32) / MLA_ROPE))
    ang = positions.astype(F32)[:, None] * inv_freq
    cos, sin = jnp.cos(ang), jnp.sin(ang)
    ones, zeros = jnp.ones((s, 64), F32), jnp.zeros((s, 64), F32)
    cos_t = jnp.concatenate([ones, cos, cos, ones[:, :32]], axis=1)
    sin_t = jnp.concatenate([zeros, -sin, sin, zeros[:, :32]], axis=1)

    if len(early) == 3:
        h0 = _rmsnorm_fwd(x, e_g_in, width=D_MODEL, col_blk=0, name="l0_norm")
        w0t, wq, wkv = early
    else:
        pending, token, unpack, prep = early
        h0 = _rmsnorm_fwd(x, e_g_in, width=D_MODEL, col_blk=0, name="l0_norm", after=[token])
        w0t, wq, wkv = unpack(*_peer_wait("gather", *pending, after=[h0] + prep, name="weights0_wait"))
    z0a = _matmul(h0, w0t, tb=True, b_rows=(0, n0a), name="l0_in_a")
    z0b = _matmul(h0, w0t, tb=True, b_rows=(n0a, Z0B_UNITS * LANES), name="l0_in_b", out_dtype=BF16)
    cqn = _rmsnorm_fwd(z0a, e_g_q_a, width=MLA_Q_RANK, col_blk=4, name="l0_q_norm")
    ckvn = _rmsnorm_fwd(z0a, e_g_kv_a, width=MLA_KV_RANK, col_blk=10, name="l0_kv_norm")
    qp = _matmul(cqn, wq, name="l0_q_up")
    kvp = _matmul(ckvn, wkv, name="l0_kv_up", out_dtype=BF16)
    qm, km = _rope_fwd(qp, kvp, z0a, cos_t, sin_t, name="l0_rope")
    gathers = len(late) == 2
    res = _flash_fwd(qm, km, kvp, None, n_pairs=MLA_HEADS // 2, hw=LANES, q_off=0, k_off=0, v_off=MLA_HEADS,
                     scale=mla_scale, name="l0_mla_fwd", rider=("gather", late[0]) if gathers else None)
    o_mla, lse_mla = res[0], res[1]
    wo0, o_g_in, w1t, wft, wo1 = late[1](res[2]) if gathers else late
    o_swa, lse_swa = _swa_fwd(z0b, e_sinks, name="l0_swa_fwd")
    og0 = _gate_fwd([o_mla, o_swa], z0a, name="l0_gate")

    x1, h1 = _matmul_rows([(og0, wo0, False)], [(x, D_MODEL, 0)], [o_g_in], _residual_norm_epilogue,
                          [("rows", D_MODEL, F32), ("rows", D_MODEL, BF16)], name="l0_out")
    z1 = _matmul(h1, w1t, tb=True, b_rows=(0, 3 * D_MODEL), name="l1_in_qkv", out_dtype=BF16)
    gate1 = _matmul(h1, w1t, tb=True, b_rows=(3 * D_MODEL, D_MODEL), name="l1_in_gate")
    zf = _matmul(h1, wft, tb=True, name="l1_in_f")
    bf = _pad_lanes(o_b_f)
    log_cum = _logf_fwd(zf, bf, name="l1_logf")
    bias2 = (-LOG2E * log_cum[:, :FOX_HEADS]).T
    t_bwd = min(ATT_T, s)
    bias = bias2.reshape(FOX_HEADS // 2, 2, s // t_bwd, 1, t_bwd)
    t_fwd = _fwd_tile(s)
    o_fox, lse_fox = _flash_fwd(z1, z1, z1, bias2.reshape(FOX_HEADS // 2, 2, s // t_fwd, 1, t_fwd),
                                n_pairs=FOX_HEADS // 2, hw=64, q_off=0, k_off=8, v_off=16, scale=fox_scale,
                                name="l1_fox_fwd")
    og1 = _gate_fwd([o_fox], gate1, name="l1_gate")

    dx2, loss_part, d_g_final = _matmul_rows(
        [(og1, wo1, False)], [(x1, D_MODEL, 0), (target, D_MODEL, 0)], [g_final.reshape(1, D_MODEL)], _loss_epilogue,
        [("rows", D_MODEL, F32), ("sum", (8, LANES)), ("sum", (1, D_MODEL))], name="l1_out_loss")

    d_wo1 = _matmul(og1, dx2, ta=True, name="l1_out_dw")
    do_fox, d_gate1 = _matmul_rows([(dx2, wo1, True)], [(o_fox, D_MODEL, 0), (gate1, D_MODEL, 0)], [],
                                   _gate_bwd_epilogue([D_MODEL]), [("rows", D_MODEL, F32), ("rows", D_MODEL, BF16)],
                                   name="l1_out_dx")
    dq1, dk1, dv1, dbias, drow = _flash_bwd(z1, z1, z1, do_fox, o_fox, lse_fox, bias, n_pairs=FOX_HEADS // 2, hw=64,
                                            q_off=0, k_off=8, v_off=16, scale=fox_scale, qk_dtype=BF16,
                                            name="l1_fox_bwd")
    d_log_cum = (drow.reshape(FOX_HEADS, s) - dbias.reshape(FOX_HEADS, s)).T
    d_log_cum = jnp.pad(d_log_cum, ((0, 0), (0, LANES - FOX_HEADS)))
    d_zf, d_bf = _logf_bwd(d_log_cum, zf, bf, name="l1_logf_bwd")
    dz1 = (dq1, dk1, dv1, d_gate1)
    d_w1t = tuple(_matmul(d, h1, ta=True, name=f"l1_in_dw_{k}") for k, d in enumerate(dz1))
    d_wft = _matmul(d_zf, h1, ta=True, name="l1_in_f_dw")
    dx1, d_o_g_in = _matmul_rows([(d, w1t, False, k * D_MODEL) for k, d in enumerate(dz1)] + [(d_zf, wft, False)],
                                 [(x1, D_MODEL, 0), (dx2, D_MODEL, 0)],
                                 [o_g_in], _rms_bwd_epilogue, [("rows", D_MODEL, F32), ("sum", (1, D_MODEL))],
                                 name="l1_in_dx")

    d_wo0 = _matmul(og0, dx1, ta=True, name="l0_out_dw")
    half = D_MODEL // 2
    do_mla, do_swa, d_gate0 = _matmul_rows(
        [(dx1, wo0, True)], [(o_mla, half, 0), (o_swa, half, 0), (z0a, D_MODEL, 0)], [], _gate_bwd_epilogue([half, half]),
        [("rows", half, F32), ("rows", half, F32), ("rows", D_MODEL, BF16)], name="l0_out_dx")
    dq_s, dkt_s, dvt_s, d_sinks = _swa_bwd(z0b, e_sinks, do_swa, o_swa, lse_swa, name="l0_swa_bwd")
    dk_s = dkt_s.transpose(0, 2, 1).reshape(s, LANES)
    dv_s = dvt_s.transpose(0, 2, 1).reshape(s, LANES)
    rider = None
    if scatter1 is not None:
        rider = ("exchange", scatter1(dict(w1t=d_w1t, wft=d_wft, wo1=d_wo1, o_g_in=d_o_g_in, wo0=d_wo0)))
    res = _flash_bwd(qm, km, kvp, do_mla, o_mla, lse_mla, None, n_pairs=MLA_HEADS // 2, hw=LANES, q_off=0, k_off=0,
                     v_off=MLA_HEADS, scale=mla_scale, qk_dtype=F32, name="l0_mla_bwd", rider=rider)
    dqm, dkm, dvm = res[0], res[1], res[2]
    recv1 = res[3] if rider is not None else None
    d_qp, d_kvp, d_kpe = _rope_bwd(dqm, dkm, dvm, cos_t, sin_t, name="l0_rope_bwd")
    d_wq = _matmul(cqn, d_qp, ta=True, name="l0_q_up_dw")
    d_cqn = _matmul(d_qp, wq, tb=True, name="l0_q_up_dx")
    d_wkv = _matmul(ckvn, d_kvp, ta=True, name="l0_kv_up_dw")
    d_ckvn = _matmul(d_kvp, wkv, tb=True, name="l0_kv_up_dx")
    d_cq, d_g_q_a = _rmsnorm_bwd(z0a, e_g_q_a, d_cqn, width=MLA_Q_RANK, col_blk=4, name="l0_q_norm_bwd")
    d_ckv, d_g_kv_a = _rmsnorm_bwd(z0a, e_g_kv_a, d_ckvn, width=MLA_KV_RANK, col_blk=10, name="l0_kv_norm_bwd")
    dz0 = jnp.concatenate([d_gate0, d_cq, d_ckv, d_kpe, dq_s.astype(BF16), dk_s.astype(BF16), dv_s.astype(BF16)], axis=1)
    d_w0t = _matmul(dz0, h0, ta=True, name="l0_in_dw")
    pending0, after_start = None, []
    if scatter0 is not None:
        *pending0, token = _peer_start("exchange", scatter0(dict(w0t=d_w0t, wq=d_wq, wkv=d_wkv)), name="grads0_start")
        after_start = [token]
    grad_x, d_e_g_in = _matmul_rows(
        [(dz0, w0t, False)], [(x, D_MODEL, 0), (dx1, D_MODEL, 0)], [e_g_in] + after_start,
        lambda dy, xt, add, g, *_: _rms_bwd_epilogue(dy, xt, add, g),
        [("rows", D_MODEL, F32), ("sum", (1, D_MODEL))], name="l0_in_dx")

    return dict(pending0=pending0, recv1=recv1, loss=loss_part[0, 0], grad_x=grad_x, e_g_in=d_e_g_in, w0t=d_w0t, e_g_q_a=d_g_q_a, wq=d_wq,
                e_g_kv_a=d_g_kv_a, wkv=d_wkv, e_sinks=d_sinks[:, 0].reshape(1, SWA_HEADS), wo0=d_wo0,
                o_g_in=d_o_g_in, w1t=d_w1t, wft=d_wft, o_b_f=d_bf[:, :FOX_HEADS], wo1=d_wo1, g_final=d_g_final.reshape(D_MODEL))


def _wide(a, rows):
    flat = a.reshape(-1)
    return jnp.pad(flat, (0, rows * WIDE - flat.shape[0])).reshape(rows, WIDE)


def _rows_b0(w_q, w_kv):
    return jnp.concatenate([_wide(w_q, 32), _wide(w_kv, 16)], axis=0)


def _unflat_b0(f):
    return f[0:24].reshape(1, MLA_Q_RANK, 96), f[32:48].reshape(1, MLA_KV_RANK, 128)


def _rows_b1(o_w_out, e_w_out, g_in):
    return jnp.concatenate([o_w_out, e_w_out, _wide(g_in, 16)], axis=0)


def _unflat_b1(f):
    return f[0:128][None], f[128:256][None], f[256:257, :LANES]


def kernel(x, positions, e_g_in, e_w_in, e_g_q_a, e_w_q_up, e_g_kv_a, e_w_kv_up, e_sinks, e_w_out, o_g_in, o_w_in, o_b_f, o_w_out, g_final, loss_target, m_e_g_in, m_e_w_in, m_e_g_q_a, m_e_w_q_up, m_e_g_kv_a, m_e_w_kv_up, m_e_sinks, m_e_w_out, m_o_g_in, m_o_w_in, m_o_b_f, m_o_w_out, m_g_final, v_e_g_in, v_e_w_in, v_e_g_q_a, v_e_w_q_up, v_e_g_kv_a, v_e_w_kv_up, v_e_sinks, v_e_w_out, v_o_g_in, v_o_w_in, v_o_b_f, v_o_w_out, v_g_final):
    def bf(a):
        return a.astype(BF16)

    me = 4 * lax.axis_index("x") + 2 * lax.axis_index("y") + lax.axis_index("c")
    shard0 = jnp.concatenate([_pad_rows(bf(e_w_in[0]).T, RA0), _rows_b0(bf(e_w_q_up[0]), bf(e_w_kv_up[0]))], axis=0)
    *pending_w0, token_w0 = _peer_start("gather", shard0, name="weights0_start")

    def unpack0(sent, gath0):
        gath0 = lax.dynamic_update_slice_in_dim(gath0, sent[None], me, axis=0)
        w0t = _layer0_in_weight_t(gath0[:, :N_E_IN].reshape(N_DEV * N_E_IN, WIDE))
        wq = _q_up_weight(_gathered_cols(gath0[:, RA0:RA0 + 24], MLA_Q_RANK))
        wkv = _kv_up_weight(_gathered_cols(gath0[:, RA0 + 32:RA0 + 48], MLA_KV_RANK))
        return w0t, wq, wkv

    rows_b0 = [_rows_b0(q[0], kv[0]) for q, kv in ((e_w_q_up, e_w_kv_up), (m_e_w_q_up, m_e_w_kv_up), (v_e_w_q_up, v_e_w_kv_up))]
    rows_b1 = [_rows_b1(o[0], e[0], g) for o, e, g in ((o_w_out, e_w_out, o_g_in), (m_o_w_out, m_e_w_out, m_o_g_in),
                                                       (v_o_w_out, v_e_w_out, v_o_g_in))]

    g_bits = lax.bitcast_convert_type(o_g_in.reshape(LANES), BF16)
    shard1 = jnp.concatenate([_pad_rows(bf(o_w_in[0]).T, RA1), _rows_b1(bf(o_w_out[0]), bf(e_w_out[0]), g_bits)], axis=0)

    def unpack1(gath1):
        w1t, wft = _layer1_in_weight_t(gath1[:, :N_O_IN].reshape(N_DEV * N_O_IN, WIDE))
        wo1 = gath1[:, RA1:RA1 + 128].reshape(D_MODEL, D_MODEL)
        wo0 = gath1[:, RA1 + 128:RA1 + 256].reshape(D_MODEL, D_MODEL)
        bits = gath1[:, RA1 + 256, :2 * LANES].reshape(N_DEV, LANES, 2)
        return wo0, lax.bitcast_convert_type(bits, F32).reshape(1, D_MODEL), w1t, wft, wo1

    def scatter1(g):
        d_in_t = _layer1_in_grad_t(g["w1t"], g["wft"]).reshape(N_DEV, N_O_IN, WIDE)
        d_o_g = jnp.pad(g["o_g_in"].reshape(N_DEV, 1, LANES), ((0, 0), (0, 15), (0, WIDE - LANES)))
        return jnp.concatenate([_pad_rows(d_in_t, RA1), g["wo1"].reshape(N_DEV, 128, WIDE),
                                g["wo0"].reshape(N_DEV, 128, WIDE), d_o_g], axis=1).astype(BF16)

    def scatter0(g):
        return jnp.concatenate([
            _pad_rows(_layer0_in_grad_t(g["w0t"]).reshape(N_DEV, N_E_IN, WIDE), RA0),
            _pad_rows(_scatter_cols(_q_up_grad(g["wq"])), 32), _scatter_cols(_kv_up_grad(g["wkv"]))], axis=1).astype(BF16)

    gr = _local_step(x[0], positions[0], loss_target[0], e_g_in,
                     (pending_w0, token_w0, unpack0, [shard1] + rows_b0 + rows_b1), e_g_q_a, e_g_kv_a, e_sinks,
                     (shard1, unpack1), o_b_f, g_final, scatter1=scatter1, scatter0=scatter0)

    def in_projection(recv, ra, n, w, m, v, name):
        g = _sum8(recv, ra, name=name + "_grad_sum")[:n].T
        d, nm, nv = _adamw_native(g, w[0], m[0], v[0], name=name + "_adamw")
        return g[None], d[None], nm[None], nv[None]

    o_in = in_projection(gr["recv1"], RA1, N_O_IN, o_w_in, m_o_w_in, v_o_w_in, "o_w_in")
    b1 = _adamw(gr["recv1"][:, RA1:], *rows_b1, name="adamw_late")

    small = _small_pack(gr["e_g_in"], gr["g_final"], gr["e_g_q_a"], gr["e_g_kv_a"], gr["e_sinks"], gr["o_b_f"], gr["loss"])
    small_all = _all_gather(small, name="small_all_gather")
    zero = jnp.zeros((), F32)
    w_small = _small_pack(e_g_in, g_final, e_g_q_a, e_g_kv_a, e_sinks, o_b_f, zero)
    m_small = _small_pack(m_e_g_in, m_g_final, m_e_g_q_a, m_e_g_kv_a, m_e_sinks, m_o_b_f, zero)
    v_small = _small_pack(v_e_g_in, v_g_final, v_e_g_q_a, v_e_g_kv_a, v_e_sinks, v_o_b_f, zero)
    smalls = _adamw(small_all, w_small, m_small, v_small, name="adamw_replicated")
    g_sm, d_sm, m_sm, v_sm = [_small_unpack(a) for a in smalls]
    loss = g_sm[6]

    sent0, recv0 = _peer_wait("exchange", *gr["pending0"], after=[o_in[1], b1[1], smalls[1]], name="grads0_wait")
    own = lax.dynamic_slice_in_dim(sent0, me, 1, axis=0)
    recv0 = lax.dynamic_update_slice_in_dim(recv0, own, me, axis=0)
    e_in = in_projection(recv0, RA0, N_E_IN, e_w_in, m_e_w_in, v_e_w_in, "e_w_in")
    b0 = _adamw(recv0[:, RA0:], *rows_b0, name="adamw_early")

    def sharded(k):
        q_up, kv_up = _unflat_b0(b0[k])
        o_out, e_out, o_g = _unflat_b1(b1[k])
        return e_in[k], q_up, kv_up, e_out, o_in[k], o_out, o_g

    g_sh, d_sh, m_sh, v_sh = [sharded(k) for k in range(4)]

    def leaves(sh, sm):
        return (sm[0], sh[0], sm[2], sh[1], sm[3], sh[2], sm[4], sh[3], sh[6], sh[4], sm[5], sh[5], sm[1])

    return (loss, gr["grad_x"][None], *leaves(g_sh, g_sm), *leaves(d_sh, d_sm), *leaves(m_sh, m_sm), *leaves(v_sh, v_sm))
```

```python
import functools

import jax
import jax.numpy as jnp
from jax import lax
from jax.experimental import pallas as pl
from jax.experimental.pallas import tpu as pltpu

F32 = jnp.float32
BF16 = jnp.bfloat16
NEG_INF = float("-inf")

N_DEV = 8
LANES = 128
D_MODEL = 1024
EPS = 1e-6
ROPE_THETA = 10000.0
MLA_HEADS = 8
MLA_Q_RANK = 256
MLA_KV_RANK = 128
MLA_NOPE = 64
MLA_ROPE = 32
MLA_V = 64
SWA_HEADS = 8
SWA_KV_HEADS = 2
SWA_DIM = 64
WINDOW = 128
FOX_HEADS = 16
FOX_DIM = 64

ADAM_LR = 0.001
ADAM_B1 = 0.9
ADAM_B2 = 0.999
ADAM_EPS = 1e-08
ADAM_WD = 0.01
ADAM_STEP = 10

ATT_T = 512
ATT_T_FWD = 1024
VMEM_LIMIT = 56 * 1024 * 1024
MATMUL_B_BLOCK_BYTES = 8 * 1024 * 1024

Z0A_UNITS = 12
Z0B_UNITS = 6

WIDE = 1024
N_E_IN = 276
N_O_IN = 514
RA0 = 288
RB0 = 32 + 16
RA1 = 528
RB1 = 128 + 128 + 16
SMALL_ROWS = 24


def _tile(n, cands):
    for c in cands:
        if n % c == 0:
            return c
    raise ValueError(f"no tile for {n}")


ROW_TILES = (512, 256, 128)


def _params(sem, vmem=VMEM_LIMIT):
    return pltpu.CompilerParams(dimension_semantics=sem, vmem_limit_bytes=vmem)


def _matmul(a, b, *, name, ta=False, tb=False, out_dtype=F32, b_rows=None):
    if ta:
        kdim, m = a.shape
    else:
        m, kdim = a.shape
    if tb:
        n, kb = b.shape
    else:
        kb, n = b.shape
    assert kdim == kb, (a.shape, b.shape)
    b_start = 0
    if b_rows is not None:
        assert tb
        b_start, n = b_rows
    tm = _tile(m, (512, 256, 128))
    tn = _tile(n, [c for c in (1024, 768, 512, 384, 256, 128)
                   if c * kdim * b.dtype.itemsize <= MATMUL_B_BLOCK_BYTES and b_start % c == 0])
    assert b_start % tn == 0, (b_start, tn)
    b_off = b_start // tn
    dims = (((0 if ta else 1,), (1 if tb else 0,)), ((), ()))

    def body(a_ref, b_ref, o_ref):
        r = lax.dot_general(a_ref[...].astype(BF16), b_ref[...].astype(BF16), dims, preferred_element_type=F32)
        o_ref[...] = r.astype(out_dtype)

    a_spec = pl.BlockSpec((kdim, tm), lambda i, j: (0, i)) if ta else pl.BlockSpec((tm, kdim), lambda i, j: (i, 0))
    b_spec = pl.BlockSpec((tn, kdim), lambda i, j: (j + b_off, 0)) if tb else pl.BlockSpec((kdim, tn), lambda i, j: (0, j))
    return pl.pallas_call(
        body, name=name, grid=(m // tm, n // tn), in_specs=[a_spec, b_spec],
        out_specs=pl.BlockSpec((tm, tn), lambda i, j: (i, j)), out_shape=jax.ShapeDtypeStruct((m, n), out_dtype),
        compiler_params=_params(("parallel", "parallel")),
    )(a, b)


def _rmsnorm_fwd(x, g, *, width, col_blk, name, after=()):
    s = x.shape[0]
    tm = _tile(s, ROW_TILES)

    def body(x_ref, g_ref, *rest):
        y_ref = rest[-1]
        xf = x_ref[...].astype(F32)
        r = lax.rsqrt(jnp.mean(xf * xf, axis=-1, keepdims=True) + EPS)
        y_ref[...] = ((xf * r) * g_ref[...]).astype(BF16)

    return pl.pallas_call(
        body, name=name, grid=(s // tm,),
        in_specs=[pl.BlockSpec((tm, width), lambda i: (i, col_blk)), pl.BlockSpec((1, width), lambda i: (0, 0))]
        + [ANY] * len(after),
        out_specs=pl.BlockSpec((tm, width), lambda i: (i, 0)),
        out_shape=jax.ShapeDtypeStruct((s, width), BF16),
        compiler_params=_params(("parallel",)),
    )(x, g, *after)


def _rmsnorm_bwd(x, g, dy, *, width, col_blk, name):
    s = x.shape[0]
    tm = _tile(s, ROW_TILES)

    def body(x_ref, g_ref, dy_ref, dx_ref, dg_ref):
        @pl.when(pl.program_id(0) == 0)
        def _():
            dg_ref[...] = jnp.zeros_like(dg_ref)

        dx, dg = _rms_bwd_epilogue(dy_ref[...], x_ref[...], 0.0, g_ref[...])
        dg_ref[...] += dg
        dx_ref[...] = dx.astype(BF16)

    return pl.pallas_call(
        body, name=name, grid=(s // tm,),
        in_specs=[pl.BlockSpec((tm, width), lambda i: (i, col_blk)), pl.BlockSpec((1, width), lambda i: (0, 0)),
                  pl.BlockSpec((tm, width), lambda i: (i, 0))],
        out_specs=[pl.BlockSpec((tm, width), lambda i: (i, 0)), pl.BlockSpec((1, width), lambda i: (0, 0))],
        out_shape=[jax.ShapeDtypeStruct((s, width), BF16), jax.ShapeDtypeStruct((1, width), F32)],
        compiler_params=_params(("arbitrary",)),
    )(x, g, dy)


def _sigmoid(x):
    return 1.0 / (1.0 + jnp.exp(-x))


def _gate_fwd(o_parts, gate, *, name):
    s = gate.shape[0]
    tm = _tile(s, ROW_TILES)
    n_o = len(o_parts)

    def body(*refs):
        o_refs, g_ref, y_ref = refs[:n_o], refs[n_o], refs[n_o + 1]
        o = o_refs[0][...] if n_o == 1 else jnp.concatenate([r[...] for r in o_refs], axis=1)
        gt = g_ref[...]
        y_ref[...] = (o * (gt * _sigmoid(gt))).astype(BF16)

    in_specs = [pl.BlockSpec((tm, o.shape[1]), lambda i: (i, 0)) for o in o_parts]
    in_specs.append(pl.BlockSpec((tm, D_MODEL), lambda i: (i, 0)))
    return pl.pallas_call(
        body, name=name, grid=(s // tm,), in_specs=in_specs,
        out_specs=pl.BlockSpec((tm, D_MODEL), lambda i: (i, 0)),
        out_shape=jax.ShapeDtypeStruct((s, D_MODEL), BF16),
        compiler_params=_params(("parallel",)),
    )(*o_parts, gate)


def _matmul_rows(terms, row_inputs, params, epilogue, outs, *, name):
    s = terms[0][0].shape[0]
    tm = _tile(s, ROW_TILES)
    steps = s // tm
    n_t, n_r, n_p, n_o = len(terms), len(row_inputs), len(params), len(outs)

    def body(*refs):
        t_refs, r_refs = refs[:2 * n_t], refs[2 * n_t:2 * n_t + n_r]
        p_refs, o_refs = refs[2 * n_t + n_r:2 * n_t + n_r + n_p], refs[2 * n_t + n_r + n_p:]
        i = pl.program_id(0)
        acc = None
        for k, term in enumerate(terms):
            dims = (((1,), (1 if term[2] else 0,)), ((), ()))
            part = lax.dot_general(t_refs[2 * k][...].astype(BF16), t_refs[2 * k + 1][...].astype(BF16), dims,
                                   preferred_element_type=F32)
            acc = part if acc is None else acc + part
        vals = epilogue(acc, *[r[...] for r in r_refs], *[p[...] for p in p_refs])
        for ref, val, out in zip(o_refs, vals, outs):
            if out[0] == "rows":
                ref[...] = val.astype(ref.dtype)
            else:
                @pl.when(i == 0)
                def _(ref=ref):
                    ref[...] = jnp.zeros_like(ref)

                ref[...] += val

    in_specs, args = [], []
    for term in terms:
        a, b = term[0], term[1]
        b_rows = b.shape[0] if term[2] or len(term) < 4 else a.shape[1]
        b_blk = 0 if len(term) < 4 else term[3] // b_rows
        in_specs += [pl.BlockSpec((tm, a.shape[1]), lambda i: (i, 0)),
                     _resident((b_rows, b.shape[1]), lambda i, b_blk=b_blk: (b_blk, 0))]
        args += [a, b]
    for arr, width, col_blk in row_inputs:
        in_specs.append(pl.BlockSpec((tm, width), lambda i, col_blk=col_blk: (i, col_blk)))
        args.append(arr)
    for p in params:
        in_specs.append(pl.BlockSpec(p.shape, lambda i: (0, 0)))
        args.append(p)
    out_specs, out_shape = [], []
    for out in outs:
        if out[0] == "rows":
            out_specs.append(pl.BlockSpec((tm, out[1]), lambda i: (i, 0)))
            out_shape.append(jax.ShapeDtypeStruct((s, out[1]), out[2]))
        else:
            out_specs.append(pl.BlockSpec(out[1], lambda i: (0, 0)))
            out_shape.append(jax.ShapeDtypeStruct(out[1], F32))
    return pl.pallas_call(
        body, name=name, grid=(steps,), in_specs=in_specs, out_specs=out_specs, out_shape=out_shape,
        compiler_params=_params(("arbitrary",)),
    )(*args)


def _rms_stats(x):
    r = lax.rsqrt(jnp.mean(x * x, axis=-1, keepdims=True) + EPS)
    return r, x * r


def _residual_norm_epilogue(r, x, g):
    x1 = x + r
    _, xh = _rms_stats(x1)
    return x1, xh * g


def _rms_bwd_epilogue(dy, x, add, g):
    r, xh = _rms_stats(x)
    dxh = dy * g
    dx = r * (dxh - xh * jnp.mean(dxh * xh, axis=-1, keepdims=True)) + add
    return dx, jnp.sum(dy * xh, axis=0, keepdims=True)


def _loss_epilogue(r, x1, target, g):
    rs, xh = _rms_stats(x1 + r)
    err = xh * g - target
    loss = jnp.broadcast_to(0.5 * jnp.sum(jnp.mean(err * err, axis=-1, keepdims=True)), (8, LANES))
    dy = err * (1.0 / D_MODEL)
    dxh = dy * g
    dx = rs * (dxh - xh * jnp.mean(dxh * xh, axis=-1, keepdims=True))
    return dx, loss, jnp.sum(dy * xh, axis=0, keepdims=True)


def _gate_bwd_epilogue(widths):
    def epilogue(d, *rows):
        o_parts, gt = rows[:-1], rows[-1]
        o = o_parts[0] if len(o_parts) == 1 else jnp.concatenate(o_parts, axis=1)
        sg = _sigmoid(gt)
        do = d * (gt * sg)
        d_gate = d * o * (sg * (1.0 + gt * (1.0 - sg)))
        cuts = [sum(widths[:k]) for k in range(len(widths) + 1)]
        return tuple(do[:, cuts[k]:cuts[k + 1]] for k in range(len(widths))) + (d_gate,)

    return epilogue


def _rot_half(x):
    lane = lax.broadcasted_iota(jnp.int32, x.shape, 1)
    return jnp.where(lane < 80, pltpu.roll(x, LANES - 16, axis=1), pltpu.roll(x, 16, axis=1))


def _rot_half_t(g):
    lane = lax.broadcasted_iota(jnp.int32, g.shape, 1)
    lo = (lane >= MLA_NOPE) & (lane < MLA_NOPE + MLA_ROPE // 2)
    hi = (lane >= MLA_NOPE + MLA_ROPE // 2) & (lane < MLA_NOPE + MLA_ROPE)
    return jnp.where(lo, pltpu.roll(g, LANES - 16, axis=1), jnp.where(hi, pltpu.roll(g, 16, axis=1), 0.0))


def _rope_fwd(qp, kvp, z0a, cos_t, sin_t, *, name):
    s = qp.shape[0]
    tm = _tile(s, ROW_TILES)
    hw = MLA_HEADS * LANES

    def body(q_ref, k_ref, kpe_ref, c_ref, s_ref, qm_ref, km_ref):
        c = c_ref[...]
        sn = s_ref[...]
        kpe = kpe_ref[...]
        kpe_r = (kpe * c + _rot_half(kpe) * sn).astype(BF16)
        lane = lax.broadcasted_iota(jnp.int32, kpe.shape, 1)
        for h in range(MLA_HEADS):
            sl = slice(h * LANES, (h + 1) * LANES)
            qh = q_ref[:, sl]
            qm_ref[:, sl] = (qh * c + _rot_half(qh) * sn).astype(BF16)
            km_ref[:, sl] = jnp.where(lane < MLA_NOPE, k_ref[:, sl], kpe_r)

    return pl.pallas_call(
        body, name=name, grid=(s // tm,),
        in_specs=[pl.BlockSpec((tm, hw), lambda i: (i, 0)), pl.BlockSpec((tm, hw), lambda i: (i, 0)),
                  pl.BlockSpec((tm, LANES), lambda i: (i, 11)),
                  pl.BlockSpec((tm, LANES), lambda i: (i, 0)), pl.BlockSpec((tm, LANES), lambda i: (i, 0))],
        out_specs=[pl.BlockSpec((tm, hw), lambda i: (i, 0)), pl.BlockSpec((tm, hw), lambda i: (i, 0))],
        out_shape=[jax.ShapeDtypeStruct((s, hw), BF16), jax.ShapeDtypeStruct((s, hw), BF16)],
        compiler_params=_params(("parallel",)),
    )(qp, kvp, z0a, cos_t, sin_t)


def _rope_bwd(dqm, dkm, dvm, cos_t, sin_t, *, name):
    s = dqm.shape[0]
    tm = _tile(s, ROW_TILES)
    hw = MLA_HEADS * LANES
    vw = MLA_HEADS * MLA_V

    def body(dq_ref, dk_ref, dv_ref, c_ref, s_ref, dqp_ref, dkv_ref, dkpe_ref):
        c = c_ref[...]
        sn = s_ref[...]
        ksum = jnp.zeros((tm, LANES), F32)
        for h in range(MLA_HEADS):
            sl = slice(h * LANES, (h + 1) * LANES)
            dq = dq_ref[:, sl]
            dqp_ref[:, sl] = (dq * c + _rot_half_t(dq * sn)).astype(BF16)
            dk = dk_ref[:, sl]
            dkv_ref[:, sl] = dk.astype(BF16)
            ksum = ksum + dk
        dkv_ref[:, hw:] = dv_ref[...]
        lane = lax.broadcasted_iota(jnp.int32, ksum.shape, 1)
        dkpe = ksum * c + _rot_half_t(ksum * sn)
        dkpe_ref[...] = jnp.where((lane >= MLA_NOPE) & (lane < MLA_NOPE + MLA_ROPE), dkpe, 0.0).astype(BF16)

    return pl.pallas_call(
        body, name=name, grid=(s // tm,),
        in_specs=[pl.BlockSpec((tm, hw), lambda i: (i, 0)), pl.BlockSpec((tm, hw), lambda i: (i, 0)),
                  pl.BlockSpec((tm, vw), lambda i: (i, 0)),
                  pl.BlockSpec((tm, LANES), lambda i: (i, 0)), pl.BlockSpec((tm, LANES), lambda i: (i, 0))],
        out_specs=[pl.BlockSpec((tm, hw), lambda i: (i, 0)), pl.BlockSpec((tm, hw + vw), lambda i: (i, 0)),
                   pl.BlockSpec((tm, LANES), lambda i: (i, 0))],
        out_shape=[jax.ShapeDtypeStruct((s, hw), BF16), jax.ShapeDtypeStruct((s, hw + vw), BF16),
                   jax.ShapeDtypeStruct((s, LANES), BF16)],
        compiler_params=_params(("parallel",)),
    )(dqm, dkm, dvm, cos_t, sin_t)


def _head_mask(shape, a):
    lane = lax.broadcasted_iota(jnp.int32, shape, 1)
    return (lane >= 64 * a) & (lane < 64 * (a + 1))


_NT = (((1,), (1,)), ((), ()))
LOG2E = 1.4426950408889634


def _stack_heads(tile, hw):
    lane = lax.broadcasted_iota(jnp.int32, tile.shape, 1)
    z = jnp.zeros_like(tile)
    return jnp.concatenate([jnp.where(lane < hw, tile, z), jnp.where(lane >= hw, tile, z)], axis=0)


def _stacked_rows(r0, r1, t):
    n = r0.shape[-1]
    return jnp.concatenate([jnp.broadcast_to(r0, (t, n)), jnp.broadcast_to(r1, (t, n))], axis=0)


def _resident(block, index_map):
    return pl.BlockSpec(block, index_map, pipeline_mode=pl.Buffered(1))


def _fwd_tile(s):
    return ATT_T_FWD if s % ATT_T_FWD == 0 else min(ATT_T, s)


def _flash_fwd(q, k, v, bias, *, n_pairs, hw, q_off, k_off, v_off, scale, name, rider=None):
    s = q.shape[0]
    t = _fwd_tile(s)
    nb = s // t
    qw = 2 * hw
    has_bias = bias is not None
    c1 = scale * LOG2E

    def body(*refs):
        refs, ride_refs = _split_rider(refs, rider, n_in=4 if has_bias else 3, n_out=2)
        if has_bias:
            q_ref, k_ref, v_ref, b_ref, o_ref, lse_ref, vt_ref, bcol_ref = refs
        else:
            q_ref, k_ref, v_ref, o_ref, lse_ref, vt_ref = refs
            b_ref = bcol_ref = None
        _ride_start(rider, ride_refs, pl.program_id(0) == 0)
        row = lax.broadcasted_iota(jnp.int32, (t, t), 0)
        col = lax.broadcasted_iota(jnp.int32, (t, t), 1)
        cmask_t = jnp.concatenate([row <= col, row <= col], axis=1)
        lane_lt64 = lax.broadcasted_iota(jnp.int32, (t, LANES), 1) < 64

        def as_column(r):
            return jnp.broadcast_to(r, (8, r.shape[1])).T[:, 0:1]

        def v_block(j, _):
            c0 = pl.multiple_of(j * t, t)
            vt_ref[j] = v_ref[pl.ds(c0, t), :].astype(F32).T.astype(BF16)
            if has_bias:
                for a in range(2):
                    bcol_ref[a, pl.ds(c0, t), :] = as_column(b_ref[0, a, j])
            return 0

        lax.fori_loop(0, nb, v_block, 0)

        def stacked_queries(i):
            return _stack_heads(q_ref[pl.ds(pl.multiple_of(i * t, t), t), :], hw).astype(F32).T.astype(BF16)

        def kv_step(j, carry, qs_t, masked):
            m, l, acc = carry
            rows = pl.ds(pl.multiple_of(j * t, t), t)
            sc = jnp.dot(k_ref[rows, :], qs_t, preferred_element_type=F32) * c1
            if has_bias:
                sc = sc + jnp.concatenate([jnp.broadcast_to(bcol_ref[0, rows, :], (t, t)),
                                           jnp.broadcast_to(bcol_ref[1, rows, :], (t, t))], axis=1)
            if masked:
                sc = jnp.where(cmask_t, sc, NEG_INF)
            m_new = jnp.maximum(m, jnp.max(sc, axis=0, keepdims=True))
            alpha = jnp.exp2(m - m_new)
            p = jnp.exp2(sc - m_new)
            l_new = alpha * l + jnp.sum(p, axis=0, keepdims=True)
            pv = jnp.dot(vt_ref[j], p.astype(BF16), preferred_element_type=F32)
            return m_new, l_new, alpha * acc + pv

        def finish(i, carry):
            m, l, acc = carry
            r0 = pl.multiple_of(i * t, t)
            out = (acc / l).T
            lse2 = as_column(m + jnp.log2(l))
            lse_ref[0, 0, pl.ds(r0, t), :] = lse2[:t]
            lse_ref[0, 1, pl.ds(r0, t), :] = lse2[t:]
            o_ref[pl.ds(r0, t), :] = jnp.where(lane_lt64, out[:t], out[t:])

        init = (jnp.full((1, 2 * t), NEG_INF, F32), jnp.zeros((1, 2 * t), F32), jnp.zeros((LANES, 2 * t), F32))

        def q_block(i, _):
            qs_t = stacked_queries(i)
            carry = lax.fori_loop(0, i, lambda j, c: kv_step(j, c, qs_t, False), init)
            finish(i, kv_step(i, carry, qs_t, True))
            return 0

        lax.fori_loop(0, nb, q_block, 0)
        _ride_wait(rider, ride_refs, pl.program_id(0) == n_pairs - 1)

    in_specs = [_resident((s, qw), lambda p: (0, q_off + p)), _resident((s, qw), lambda p: (0, k_off + p)),
                _resident((s, LANES), lambda p: (0, v_off + p))]
    args = [q, k, v]
    if has_bias:
        in_specs.append(_resident((1, 2, nb, 1, t), lambda p: (p, 0, 0, 0, 0)))
        args.append(bias)
    out_specs = [pl.BlockSpec((s, LANES), lambda p: (0, p)), pl.BlockSpec((1, 2, s, 1), lambda p: (p, 0, 0, 0))]
    out_shape = [jax.ShapeDtypeStruct((s, n_pairs * LANES), F32), jax.ShapeDtypeStruct((n_pairs, 2, s, 1), F32)]
    scratch = [pltpu.VMEM((nb, LANES, t), BF16)] + ([pltpu.VMEM((2, s, 1), F32)] if has_bias else [])
    scratch += _add_rider(rider, in_specs, args, out_specs, out_shape)
    return pl.pallas_call(
        body, name=name, grid=(n_pairs,), in_specs=in_specs, out_specs=out_specs, out_shape=out_shape,
        scratch_shapes=scratch,
        compiler_params=_params(("parallel",) if rider is None else ("arbitrary",)),
    )(*args)


def _flash_bwd(q, k, v, do, o, lse, bias, *, n_pairs, hw, q_off, k_off, v_off, scale, qk_dtype, name, rider=None):
    s = q.shape[0]
    t = min(ATT_T, s)
    nb = s // t
    qw = 2 * hw
    has_bias = bias is not None
    c1 = scale * LOG2E

    def body(*refs):
        refs, ride_refs = _split_rider(refs, rider, n_in=7 if has_bias else 6, n_out=5 if has_bias else 3)
        if has_bias:
            (q_ref, k_ref, v_ref, do_ref, o_ref, lse_ref, b_ref, dq_ref, dk_ref, dv_ref, db_ref, dr_ref,
             dkt_ref, dvt_ref) = refs
            db_ref[...] = jnp.zeros_like(db_ref)
        else:
            q_ref, k_ref, v_ref, do_ref, o_ref, lse_ref, dq_ref, dk_ref, dv_ref, dkt_ref, dvt_ref = refs
            b_ref = db_ref = dr_ref = None
        _ride_start(rider, ride_refs, pl.program_id(0) == 0)
        dkt_ref[...] = jnp.zeros_like(dkt_ref)
        dvt_ref[...] = jnp.zeros_like(dvt_ref)
        causal = lax.broadcasted_iota(jnp.int32, (t, t), 1) <= lax.broadcasted_iota(jnp.int32, (t, t), 0)
        cmask = jnp.concatenate([causal, causal], axis=0)
        lane_lt_hw = lax.broadcasted_iota(jnp.int32, (t, qw), 1) < hw

        def q_block(i, _):
            r0 = pl.multiple_of(i * t, t)
            qs = _stack_heads(q_ref[pl.ds(r0, t), :], hw)
            dos = _stack_heads(do_ref[pl.ds(r0, t), :], 64)
            ot = o_ref[pl.ds(r0, t), :]
            delta = jnp.sum(dos * jnp.concatenate([ot, ot], axis=0), axis=-1, keepdims=True)
            lse2 = jnp.concatenate([lse_ref[0, 0, pl.ds(r0, t), :], lse_ref[0, 1, pl.ds(r0, t), :]], axis=0)
            dosb = dos.astype(BF16)
            dos_t = dos.T.astype(BF16)
            qs_t = qs.astype(F32).T.astype(BF16)

            def kv_step(j, carry, masked):
                dq, rsum = carry
                c0 = pl.multiple_of(j * t, t)
                kt = k_ref[pl.ds(c0, t), :]
                vt = v_ref[pl.ds(c0, t), :]
                sc = lax.dot_general(qs, kt, _NT, preferred_element_type=F32) * c1
                if has_bias:
                    sc = sc + _stacked_rows(b_ref[0, 0, j], b_ref[0, 1, j], t)
                if masked:
                    sc = jnp.where(cmask, sc, NEG_INF)
                p = jnp.exp2(sc - lse2)
                dp = lax.dot_general(dosb, vt, _NT, preferred_element_type=F32)
                ds = p * (dp - delta)
                dsb = ds.astype(BF16)
                pb = p.astype(BF16)
                if hw == LANES:
                    dvt_ref[j] += jnp.concatenate(
                        [jnp.dot(dos_t[:64, :t], pb[:t], preferred_element_type=F32),
                         jnp.dot(dos_t[64:, t:], pb[t:], preferred_element_type=F32)], axis=0)
                    dkt_ref[j] += jnp.concatenate(
                        [jnp.dot(qs_t[:hw, :t], dsb[:t], preferred_element_type=F32),
                         jnp.dot(qs_t[hw:, t:], dsb[t:], preferred_element_type=F32)], axis=0)
                else:
                    dvt_ref[j] += jnp.dot(dos_t, pb, preferred_element_type=F32)
                    dkt_ref[j] += jnp.dot(qs_t, dsb, preferred_element_type=F32)
                if has_bias:
                    db_ref[0, 0, j] += jnp.sum(ds[:t], axis=0, keepdims=True)
                    db_ref[0, 1, j] += jnp.sum(ds[t:], axis=0, keepdims=True)
                    rsum = rsum + jnp.sum(ds, axis=-1, keepdims=True)
                return dq + jnp.dot(dsb, kt, preferred_element_type=F32), rsum

            init = (jnp.zeros((2 * t, qw), F32), jnp.zeros((2 * t, 1), F32))
            carry = lax.fori_loop(0, i, functools.partial(kv_step, masked=False), init)
            dq, rsum = kv_step(i, carry, True)
            dq = dq * scale
            dq_ref[pl.ds(r0, t), :] = jnp.where(lane_lt_hw, dq[:t], dq[t:]).astype(qk_dtype)
            if has_bias:
                rsum_row = jnp.broadcast_to(rsum, (2 * t, LANES)).T[0:1]
                dr_ref[0, 0, i] = rsum_row[:, :t]
                dr_ref[0, 1, i] = rsum_row[:, t:]
            return 0

        lax.fori_loop(0, nb, q_block, 0)

        def k_block(j, _):
            c0 = pl.multiple_of(j * t, t)
            dk_ref[pl.ds(c0, t), :] = (dkt_ref[j].T * scale).astype(qk_dtype)
            dv_ref[pl.ds(c0, t), :] = dvt_ref[j].T.astype(BF16)
            return 0

        lax.fori_loop(0, nb, k_block, 0)
        _ride_wait(rider, ride_refs, pl.program_id(0) == n_pairs - 1)

    in_specs = [_resident((s, qw), lambda p: (0, q_off + p)), _resident((s, qw), lambda p: (0, k_off + p)),
                _resident((s, LANES), lambda p: (0, v_off + p)),
                _resident((s, LANES), lambda p: (0, p)), _resident((s, LANES), lambda p: (0, p)),
                _resident((1, 2, s, 1), lambda p: (p, 0, 0, 0))]
    args = [q, k, v, do, o, lse]
    out_specs = [pl.BlockSpec((s, qw), lambda p: (0, p)), pl.BlockSpec((s, qw), lambda p: (0, p)),
                 pl.BlockSpec((s, LANES), lambda p: (0, p))]
    out_shape = [jax.ShapeDtypeStruct((s, n_pairs * qw), qk_dtype), jax.ShapeDtypeStruct((s, n_pairs * qw), qk_dtype),
                 jax.ShapeDtypeStruct((s, n_pairs * LANES), BF16)]
    if has_bias:
        in_specs.append(_resident((1, 2, nb, 1, t), lambda p: (p, 0, 0, 0, 0)))
        args.append(bias)
        for _ in range(2):
            out_specs.append(pl.BlockSpec((1, 2, nb, 1, t), lambda p: (p, 0, 0, 0, 0)))
            out_shape.append(jax.ShapeDtypeStruct((n_pairs, 2, nb, 1, t), F32))
    scratch = [pltpu.VMEM((nb, qw, t), F32), pltpu.VMEM((nb, LANES, t), F32)]
    scratch += _add_rider(rider, in_specs, args, out_specs, out_shape)
    return pl.pallas_call(
        body, name=name, grid=(n_pairs,), in_specs=in_specs, out_specs=out_specs, out_shape=out_shape,
        scratch_shapes=scratch,
        compiler_params=_params(("parallel",) if rider is None else ("arbitrary",)),
    )(*args)


def _alibi_slope(h):
    return 2.0 ** (-8.0 * (h + 1.0) / SWA_HEADS)


SWA_ROWS = 512
SWA_SCALE = SWA_DIM ** -0.5


def _swa_geometry(i):
    w = WINDOW
    r0 = pl.multiple_of(i * w, w)
    b0 = pl.multiple_of(jnp.maximum(i - 1, 0) * w, w)
    row = lax.broadcasted_iota(jnp.int32, (w, 2 * w), 0)
    col = lax.broadcasted_iota(jnp.int32, (w, 2 * w), 1)
    dist = row - col + (r0 - b0)
    valid = (dist >= 0) & (dist < w)
    return r0, b0, dist.astype(F32), valid


def _swa_q_head(qblk, h):
    kv = h // (SWA_HEADS // SWA_KV_HEADS)
    if h % 2 != kv:
        qblk = pltpu.roll(qblk, 64, axis=1)
    return jnp.where(_head_mask(qblk.shape, kv), qblk, 0.0)


SWA_GROUP = SWA_HEADS // SWA_KV_HEADS


def _swa_stack(ref, rs, grp):
    parts = []
    for a in range(SWA_GROUP):
        h = SWA_GROUP * grp + a
        parts.append(_swa_q_head(ref[rs, (h // 2) * LANES:(h // 2 + 1) * LANES].astype(F32), h))
    return jnp.concatenate(parts, axis=0)


def _swa_unstack(x, grp):
    tiles = []
    for a in range(SWA_GROUP):
        h = SWA_GROUP * grp + a
        tile = x[a * WINDOW:(a + 1) * WINDOW]
        tiles.append(pltpu.roll(tile, 64, axis=1) if h % 2 != grp else tile)
    return tiles


def _swa_head_column(vals):
    return jnp.concatenate([jnp.full((WINDOW, 1), v, F32) for v in vals], axis=0)


def _swa_logits(qs, kb, dist, valid, grp):
    slopes = _swa_head_column([_alibi_slope(SWA_GROUP * grp + a) for a in range(SWA_GROUP)])
    dist4 = jnp.concatenate([dist] * SWA_GROUP, axis=0)
    valid4 = jnp.concatenate([valid] * SWA_GROUP, axis=0)
    sc = lax.dot_general(qs, kb, _NT, preferred_element_type=F32) * SWA_SCALE - slopes * dist4
    return jnp.where(valid4, sc, NEG_INF)


def _swa_merge_heads(tiles):
    lt64 = lax.broadcasted_iota(jnp.int32, (WINDOW, LANES), 1) < 64
    return jnp.concatenate([jnp.where(lt64, tiles[2 * b], tiles[2 * b + 1]) for b in range(SWA_HEADS // 2)], axis=1)


def _swa_fwd(z0b, sinks, *, name):
    s = z0b.shape[0]
    w = WINDOW
    rows = min(SWA_ROWS, s)
    per_step = rows // w
    qcols = SWA_HEADS * SWA_DIM

    def body(sink_ref, q_ref, k_ref, v_ref, o_ref, lse_ref):
        g = pl.program_id(0)
        for ii in range(per_step):
            rs = slice(ii * w, (ii + 1) * w)
            r0, b0, dist, valid = _swa_geometry(g * per_step + ii)
            kb = k_ref[pl.ds(b0, 2 * w), :]
            vb = v_ref[pl.ds(b0, 2 * w), :]
            o_tiles = []
            for h in range(SWA_HEADS):
                kv = h // SWA_GROUP
                qh = _swa_q_head(q_ref[rs, (h // 2) * LANES:(h // 2 + 1) * LANES].astype(F32), h).astype(BF16)
                sc = lax.dot_general(qh, kb, _NT, preferred_element_type=F32) * SWA_SCALE - _alibi_slope(h) * dist
                sc = jnp.where(valid, sc, NEG_INF)
                sink = sink_ref[0, h]
                m = jnp.maximum(jnp.max(sc, axis=-1, keepdims=True), sink)
                p = jnp.exp(sc - m)
                l = jnp.sum(p, axis=-1, keepdims=True) + jnp.exp(sink - m)
                oh = jnp.dot(p.astype(BF16), vb, preferred_element_type=F32) / l
                o_tiles.append(pltpu.roll(oh, 64, axis=1) if h % 2 != kv else oh)
                lse_ref[h, rs, :] = m + jnp.log(l)
            o_ref[rs, :] = _swa_merge_heads(o_tiles)

    return pl.pallas_call(
        body, name=name, grid=(s // rows,),
        in_specs=[pl.BlockSpec(memory_space=pltpu.SMEM),
                  pl.BlockSpec((rows, qcols), lambda g: (g, 0)),
                  pl.BlockSpec((s, LANES), lambda g: (0, 4)), pl.BlockSpec((s, LANES), lambda g: (0, 5))],
        out_specs=[pl.BlockSpec((rows, qcols), lambda g: (g, 0)), pl.BlockSpec((SWA_HEADS, rows, 1), lambda g: (0, g, 0))],
        out_shape=[jax.ShapeDtypeStruct((s, qcols), F32), jax.ShapeDtypeStruct((SWA_HEADS, s, 1), F32)],
        compiler_params=_params(("parallel",)),
    )(sinks, z0b, z0b, z0b)


def _swa_bwd(z0b, sinks, do, o, lse, *, name):
    s = z0b.shape[0]
    w = WINDOW
    rows = min(SWA_ROWS, s)
    per_step = rows // w
    qcols = SWA_HEADS * SWA_DIM
    nblk = s // w

    def body(sink_ref, q_ref, k_ref, v_ref, do_ref, o_ref, lse_ref, dq_ref, dkt_ref, dvt_ref, dsink_ref):
        g = pl.program_id(0)

        @pl.when(g == 0)
        def _():
            dkt_ref[...] = jnp.zeros_like(dkt_ref)
            dvt_ref[...] = jnp.zeros_like(dvt_ref)
            dsink_ref[...] = jnp.zeros_like(dsink_ref)

        for ii in range(per_step):
            i = g * per_step + ii
            rs = slice(ii * w, (ii + 1) * w)
            r0, b0, dist, valid = _swa_geometry(i)
            j0 = jnp.maximum(i - 1, 0)
            kb = k_ref[pl.ds(b0, 2 * w), :]
            vb = v_ref[pl.ds(b0, 2 * w), :]
            dq_tiles = []
            for grp in range(SWA_KV_HEADS):
                heads = [SWA_GROUP * grp + a for a in range(SWA_GROUP)]
                qs32 = _swa_stack(q_ref, rs, grp)
                dos32 = _swa_stack(do_ref, rs, grp)
                delta = jnp.sum(dos32 * _swa_stack(o_ref, rs, grp), axis=-1, keepdims=True)
                lse = jnp.concatenate([lse_ref[h, rs, :] for h in heads], axis=0)
                sink = _swa_head_column([sink_ref[0, h] for h in heads])
                p = jnp.exp(_swa_logits(qs32.astype(BF16), kb, dist, valid, grp) - lse)
                dp = lax.dot_general(dos32.astype(BF16), vb, _NT, preferred_element_type=F32)
                ds = p * (dp - delta)
                dsb = ds.astype(BF16)
                d_sink = jnp.exp(sink - lse) * delta
                for a, h in enumerate(heads):
                    dsink_ref[h:h + 1, :] += jnp.broadcast_to(-jnp.sum(d_sink[a * w:(a + 1) * w]), (1, LANES))
                dvt = jnp.dot(dos32.T.astype(BF16), p.astype(BF16), preferred_element_type=F32)
                dkt = jnp.dot(qs32.T.astype(BF16), dsb, preferred_element_type=F32) * SWA_SCALE
                dvt_ref[j0] += dvt[:, :w]
                dvt_ref[j0 + 1] += dvt[:, w:]
                dkt_ref[j0] += dkt[:, :w]
                dkt_ref[j0 + 1] += dkt[:, w:]
                dq_tiles += _swa_unstack(jnp.dot(dsb, kb, preferred_element_type=F32) * SWA_SCALE, grp)
            dq_ref[rs, :] = _swa_merge_heads(dq_tiles)

    return pl.pallas_call(
        body, name=name, grid=(s // rows,),
        in_specs=[pl.BlockSpec(memory_space=pltpu.SMEM),
                  pl.BlockSpec((rows, qcols), lambda g: (g, 0)),
                  pl.BlockSpec((s, LANES), lambda g: (0, 4)), pl.BlockSpec((s, LANES), lambda g: (0, 5)),
                  pl.BlockSpec((rows, qcols), lambda g: (g, 0)), pl.BlockSpec((rows, qcols), lambda g: (g, 0)),
                  pl.BlockSpec((SWA_HEADS, rows, 1), lambda g: (0, g, 0))],
        out_specs=[pl.BlockSpec((rows, qcols), lambda g: (g, 0)),
                   pl.BlockSpec((nblk, LANES, w), lambda g: (0, 0, 0)),
                   pl.BlockSpec((nblk, LANES, w), lambda g: (0, 0, 0)),
                   pl.BlockSpec((SWA_HEADS, LANES), lambda g: (0, 0))],
        out_shape=[jax.ShapeDtypeStruct((s, qcols), F32),
                   jax.ShapeDtypeStruct((nblk, LANES, w), F32), jax.ShapeDtypeStruct((nblk, LANES, w), F32),
                   jax.ShapeDtypeStruct((SWA_HEADS, LANES), F32)],
        compiler_params=_params(("arbitrary",)),
    )(sinks, z0b, z0b, z0b, do, o, lse)


CUM_T = 256


def _split3(x):
    hi = x.astype(BF16)
    r1 = x - hi.astype(F32)
    mid = r1.astype(BF16)
    lo = (r1 - mid.astype(F32)).astype(BF16)
    return hi, mid, lo


def _tri_dot(tri, x):
    hi, mid, lo = _split3(x)
    out = jnp.dot(tri, hi, preferred_element_type=F32)
    out = out + jnp.dot(tri, mid, preferred_element_type=F32)
    return out + jnp.dot(tri, lo, preferred_element_type=F32)


def _logf_fwd(zf, bf, *, name):
    s = zf.shape[0]
    t = CUM_T
    nb = s // t

    def body(z_ref, b_ref, c_ref, carry_ref):
        i = pl.program_id(0)

        @pl.when(i == 0)
        def _():
            carry_ref[...] = jnp.zeros_like(carry_ref)

        x = z_ref[...] + b_ref[...]
        lf = jnp.minimum(x, 0.0) - jnp.log(1.0 + jnp.exp(-jnp.abs(x)))
        row = lax.broadcasted_iota(jnp.int32, (t, t), 0)
        col = lax.broadcasted_iota(jnp.int32, (t, t), 1)
        tri = jnp.where(col <= row, 1.0, 0.0).astype(BF16)
        c = _tri_dot(tri, lf) + carry_ref[...]
        c_ref[...] = c
        carry_ref[...] = c[t - 1:t, :]

    return pl.pallas_call(
        body, name=name, grid=(nb,),
        in_specs=[pl.BlockSpec((t, LANES), lambda i: (i, 0)), pl.BlockSpec((1, LANES), lambda i: (0, 0))],
        out_specs=pl.BlockSpec((t, LANES), lambda i: (i, 0)),
        out_shape=jax.ShapeDtypeStruct((s, LANES), F32),
        scratch_shapes=[pltpu.VMEM((1, LANES), F32)],
        compiler_params=_params(("arbitrary",)),
    )(zf, bf)


def _logf_bwd(dc, zf, bf, *, name):
    s = zf.shape[0]
    t = CUM_T
    nb = s // t

    def body(dc_ref, z_ref, b_ref, dz_ref, db_ref, carry_ref):
        i = pl.program_id(0)

        @pl.when(i == 0)
        def _():
            carry_ref[...] = jnp.zeros_like(carry_ref)
            db_ref[...] = jnp.zeros_like(db_ref)

        row = lax.broadcasted_iota(jnp.int32, (t, t), 0)
        col = lax.broadcasted_iota(jnp.int32, (t, t), 1)
        tri = jnp.where(col >= row, 1.0, 0.0).astype(BF16)
        dlf = _tri_dot(tri, dc_ref[...]) + carry_ref[...]
        carry_ref[...] = dlf[0:1, :]
        x = z_ref[...] + b_ref[...]
        dz = dlf * _sigmoid(-x)
        dz_ref[...] = dz.astype(BF16)
        db_ref[...] += jnp.sum(dz, axis=0, keepdims=True)

    return pl.pallas_call(
        body, name=name, grid=(nb,),
        in_specs=[pl.BlockSpec((t, LANES), lambda i: (nb - 1 - i, 0)), pl.BlockSpec((t, LANES), lambda i: (nb - 1 - i, 0)),
                  pl.BlockSpec((1, LANES), lambda i: (0, 0))],
        out_specs=[pl.BlockSpec((t, LANES), lambda i: (nb - 1 - i, 0)), pl.BlockSpec((1, LANES), lambda i: (0, 0))],
        out_shape=[jax.ShapeDtypeStruct((s, LANES), BF16), jax.ShapeDtypeStruct((1, LANES), F32)],
        scratch_shapes=[pltpu.VMEM((1, LANES), F32)],
        compiler_params=_params(("arbitrary",)),
    )(dc, zf, bf)


def _sum_pieces(p_ref):
    g = p_ref[0].astype(F32)
    for k in range(1, N_DEV):
        g = g + p_ref[k].astype(F32)
    return g


def _adam_update(g, w, m, v):
    bc1 = 1.0 - ADAM_B1 ** ADAM_STEP
    bc2 = 1.0 - ADAM_B2 ** ADAM_STEP
    nm = ADAM_B1 * m + (1.0 - ADAM_B1) * g
    nv = ADAM_B2 * v + (1.0 - ADAM_B2) * (g * g)
    m_hat = nm / bc1
    v_hat = nv / bc2
    return -ADAM_LR * (m_hat / (jnp.sqrt(v_hat) + ADAM_EPS) + ADAM_WD * w), nm, nv


def _adamw(pieces, w, m, v, *, name):
    rows, cols = w.shape
    tr = _tile(rows, (RB1, RB0, SMALL_ROWS))

    def body(p_ref, w_ref, m_ref, v_ref, g_ref, d_ref, nm_ref, nv_ref):
        g = _sum_pieces(p_ref)
        g_ref[...] = g
        d_ref[...], nm_ref[...], nv_ref[...] = _adam_update(g, w_ref[...], m_ref[...], v_ref[...])

    spec = pl.BlockSpec((tr, cols), lambda i: (i, 0))
    shape = jax.ShapeDtypeStruct((rows, cols), F32)
    return pl.pallas_call(
        body, name=name, grid=(rows // tr,),
        in_specs=[pl.BlockSpec((N_DEV, tr, cols), lambda i: (0, i, 0)), spec, spec, spec],
        out_specs=[spec, spec, spec, spec], out_shape=[shape, shape, shape, shape],
        compiler_params=_params(("parallel",)),
    )(pieces, w, m, v)


def _sum8(pieces, rows, *, name):
    cols = pieces.shape[2]
    tr = _tile(rows, (176, 96))

    def body(p_ref, g_ref):
        g_ref[...] = _sum_pieces(p_ref)

    return pl.pallas_call(
        body, name=name, grid=(rows // tr,),
        in_specs=[pl.BlockSpec((N_DEV, tr, cols), lambda i: (0, i, 0))],
        out_specs=pl.BlockSpec((tr, cols), lambda i: (i, 0)),
        out_shape=jax.ShapeDtypeStruct((rows, cols), F32),
        compiler_params=_params(("parallel",)),
    )(pieces)


def _adamw_native(g, w, m, v, *, name):
    rows, cols = w.shape
    tr = _tile(rows, (256, 128))

    def body(g_ref, w_ref, m_ref, v_ref, d_ref, nm_ref, nv_ref):
        d_ref[...], nm_ref[...], nv_ref[...] = _adam_update(g_ref[...], w_ref[...], m_ref[...], v_ref[...])

    spec = pl.BlockSpec((tr, cols), lambda i: (i, 0))
    shape = jax.ShapeDtypeStruct((rows, cols), F32)
    return pl.pallas_call(
        body, name=name, grid=(rows // tr,), in_specs=[spec, spec, spec, spec],
        out_specs=[spec, spec, spec], out_shape=[shape, shape, shape],
        compiler_params=_params(("parallel",)),
    )(g, w, m, v)


MESH = pl.DeviceIdType.MESH
ANY = pl.BlockSpec(memory_space=pl.ANY)


def _all_gather(shard, *, name):
    rows, lanes = shard.shape

    def body(x_ref, out_ref, send_sems, recv_sems, local_sem):
        x, y, c = lax.axis_index("x"), lax.axis_index("y"), lax.axis_index("c")
        me, sibling = (x, y, c), (x, y, 1 - c)
        chips = [(1 - x, y), (x, 1 - y), (1 - x, 1 - y)]

        def block(px, py, pc):
            return out_ref.at[4 * px + 2 * py + pc]

        def copy(k, blk, to, src=None):
            return pltpu.make_async_remote_copy(
                src_ref=block(*blk) if src is None else src, dst_ref=block(*blk),
                send_sem=send_sems.at[k], recv_sem=recv_sems.at[k], device_id=to, device_id_type=MESH)

        mine = pltpu.make_async_copy(x_ref, block(*me), local_sem)
        mine.start()
        first = [copy(0, me, sibling, src=x_ref)]
        first += [copy(1 + j, me, (*chip, c), src=x_ref) for j, chip in enumerate(chips)]
        for cp in first:
            cp.start()
        passed = [copy(4 + j, (*chip, c), sibling) for j, chip in enumerate(chips)]
        for j, chip in enumerate(chips):
            copy(1 + j, (*chip, c), me).wait_recv()
            passed[j].start()
        copy(0, sibling, me).wait_recv()
        for j, chip in enumerate(chips):
            copy(4 + j, (*chip, 1 - c), me).wait_recv()
        for cp in first + passed:
            cp.wait_send()
        mine.wait()

    return pl.pallas_call(
        body, name=name, out_shape=jax.ShapeDtypeStruct((N_DEV, rows, lanes), shard.dtype),
        in_specs=[ANY], out_specs=ANY,
        scratch_shapes=[pltpu.SemaphoreType.DMA((7,)), pltpu.SemaphoreType.DMA((7,)), pltpu.SemaphoreType.DMA(())],
    )(shard)


def _peer_copies(kind, src_ref, out_ref, send_sems, recv_sems, local_sem):
    x, y, c = lax.axis_index("x"), lax.axis_index("y"), lax.axis_index("c")
    me = 4 * x + 2 * y + c

    def src(idx):
        return src_ref.at[idx] if kind == "exchange" else src_ref

    mine = None if local_sem is None else pltpu.make_async_copy(src(me), out_ref.at[me], local_sem)
    copies = []
    for r in range(1, N_DEV):
        px = 1 - x if r & 4 else x
        py = 1 - y if r & 2 else y
        pc = 1 - c if r & 1 else c
        copies.append(pltpu.make_async_remote_copy(
            src_ref=src(4 * px + 2 * py + pc), dst_ref=out_ref.at[me],
            send_sem=send_sems.at[r - 1], recv_sem=recv_sems.at[r - 1],
            device_id=(px, py, pc), device_id_type=MESH))
    return mine, copies


PEER_SEMS = [pltpu.SemaphoreType.DMA((7,)), pltpu.SemaphoreType.DMA((7,)), pltpu.SemaphoreType.DMA(())]


HBM = pl.BlockSpec(memory_space=pltpu.HBM)
SEMAPHORES = pl.BlockSpec(memory_space=pltpu.SEMAPHORE)


def _peer_start(kind, arr, *, name):
    land = lax.empty((N_DEV,) + arr.shape[-2:], arr.dtype)

    def body(src_ref, land_ref, send_sems, recv_sems, src_thru, land_thru, token):
        _, copies = _peer_copies(kind, src_ref, land_ref, send_sems, recv_sems, None)
        for cp in copies:
            cp.start()
        token[...] = jnp.zeros_like(token)

    return pl.pallas_call(
        body, name=name,
        out_shape=(pltpu.SemaphoreType.DMA((N_DEV - 1,)), pltpu.SemaphoreType.DMA((N_DEV - 1,)),
                   pltpu.HBM(arr.shape, arr.dtype), pltpu.HBM(land.shape, land.dtype), jax.ShapeDtypeStruct((8, LANES), F32)),
        in_specs=(HBM, HBM), out_specs=(SEMAPHORES, SEMAPHORES, HBM, HBM, pl.BlockSpec(memory_space=pltpu.VMEM)),
        input_output_aliases={0: 2, 1: 3},
        compiler_params=pltpu.CompilerParams(has_side_effects=pltpu.SideEffectType.DATAFLOW_SIDE_EFFECTING),
    )(pltpu.with_memory_space_constraint(arr, pltpu.HBM), pltpu.with_memory_space_constraint(land, pltpu.HBM))


def _peer_wait(kind, send_sems, recv_sems, src_thru, land_thru, after, *, name):
    def body(src_ref, land_ref, send_sems, recv_sems, *_):
        _, copies = _peer_copies(kind, src_ref, land_ref, send_sems, recv_sems, None)
        for cp in copies:
            cp.wait_send()
            cp.wait_recv()

    return pl.pallas_call(
        body, name=name,
        out_shape=(pltpu.HBM(src_thru.shape, src_thru.dtype), pltpu.HBM(land_thru.shape, land_thru.dtype)),
        in_specs=(HBM, HBM, SEMAPHORES, SEMAPHORES) + (ANY,) * len(after), out_specs=(HBM, HBM),
        input_output_aliases={0: 0, 1: 1},
        compiler_params=pltpu.CompilerParams(has_side_effects=pltpu.SideEffectType.DATAFLOW_SIDE_EFFECTING),
    )(src_thru, land_thru, send_sems, recv_sems, *after)


def _add_rider(rider, in_specs, args, out_specs, out_shape):
    if rider is None:
        return []
    _, arr = rider
    in_specs.append(ANY)
    args.append(arr)
    out_specs.append(ANY)
    out_shape.append(jax.ShapeDtypeStruct((N_DEV,) + arr.shape[-2:], arr.dtype))
    return list(PEER_SEMS)


def _split_rider(refs, rider, n_in, n_out):
    if rider is None:
        return refs, None
    refs = list(refs)
    rin = refs.pop(n_in)
    rout = refs.pop(n_in + n_out)
    return refs[:-3], (rin, rout, *refs[-3:])


def _ride_start(rider, ride_refs, first):
    if rider is None:
        return

    @pl.when(first)
    def _():
        mine, copies = _peer_copies(rider[0], *ride_refs)
        mine.start()
        for cp in copies:
            cp.start()


def _ride_wait(rider, ride_refs, last):
    if rider is None:
        return

    @pl.when(last)
    def _():
        mine, copies = _peer_copies(rider[0], *ride_refs)
        for cp in copies:
            cp.wait()
        mine.wait()


def _gathered_cols(blocks, kdim):
    n = blocks.shape[1] * WIDE // kdim
    return blocks.reshape(N_DEV, kdim, n).transpose(1, 0, 2).reshape(kdim, N_DEV * n)


def _scatter_cols(dw):
    kdim, n8 = dw.shape
    n = n8 // N_DEV
    return dw.reshape(kdim, N_DEV, n).transpose(1, 0, 2).reshape(N_DEV, kdim * n // WIDE, WIDE)


def _pad_rows(a, rows):
    pad = [(0, 0)] * a.ndim
    pad[-2] = (0, rows - a.shape[-2])
    return jnp.pad(a, pad)


def _layer0_in_weight_t(wt):
    cq, ckv, kpe = wt[0:256], wt[256:384], wt[384:416]
    q_s, k_s, v_s, gate = wt[416:928], wt[928:1056], wt[1056:1184], wt[1184:2208]
    z = jnp.zeros((64, wt.shape[1]), wt.dtype)
    return jnp.concatenate([gate, cq, ckv, z, kpe, z[:32], q_s, k_s, v_s], axis=0)


def _layer0_in_grad_t(dwt):
    gate, cq, ckv, kpe = dwt[0:1024], dwt[1024:1280], dwt[1280:1408], dwt[1472:1504]
    q_s, k_s, v_s = dwt[1536:2048], dwt[2048:2176], dwt[2176:2304]
    return jnp.concatenate([cq, ckv, kpe, q_s, k_s, v_s, gate], axis=0)


def _layer1_in_weight_t(wt):
    main = jnp.concatenate([wt[:3 * D_MODEL], wt[3 * D_MODEL + FOX_HEADS:]], axis=0)
    return main, _pad_rows(wt[3 * D_MODEL:3 * D_MODEL + FOX_HEADS], LANES)


def _layer1_in_grad_t(d_blocks, d_wft):
    return jnp.concatenate([*d_blocks[:3], d_wft[:FOX_HEADS], d_blocks[3]], axis=0)


def _q_up_weight(w):
    return jnp.pad(w.reshape(MLA_Q_RANK, MLA_HEADS, 96), ((0, 0), (0, 0), (0, 32))).reshape(MLA_Q_RANK, MLA_HEADS * LANES)


def _q_up_grad(dwp):
    return dwp.reshape(MLA_Q_RANK, MLA_HEADS, LANES)[:, :, :96].reshape(MLA_Q_RANK, MLA_HEADS * 96)


def _kv_up_weight(w):
    w4 = w.reshape(MLA_KV_RANK, MLA_HEADS, 2, 64)
    kp = jnp.pad(w4[:, :, 0, :], ((0, 0), (0, 0), (0, 64))).reshape(MLA_KV_RANK, MLA_HEADS * LANES)
    vp = w4[:, :, 1, :].reshape(MLA_KV_RANK, MLA_HEADS * 64)
    return jnp.concatenate([kp, vp], axis=1)


def _kv_up_grad(dwp):
    dk = dwp[:, :MLA_HEADS * LANES].reshape(MLA_KV_RANK, MLA_HEADS, LANES)[:, :, :64]
    dv = dwp[:, MLA_HEADS * LANES:].reshape(MLA_KV_RANK, MLA_HEADS, 64)
    return jnp.stack([dk, dv], axis=2).reshape(MLA_KV_RANK, MLA_HEADS * LANES)


def _pad_lanes(a):
    return jnp.pad(a, ((0, 0), (0, LANES - a.shape[1])))


def _small_pack(g_in, g_final, g_q_a, g_kv_a, sinks, b_f, loss):
    rows = [g_in.reshape(8, LANES), g_final.reshape(8, LANES), g_q_a.reshape(2, LANES), g_kv_a.reshape(1, LANES),
            _pad_lanes(sinks.reshape(1, -1)), _pad_lanes(b_f.reshape(1, -1)), _pad_lanes(loss.reshape(1, 1)),
            jnp.zeros((2, LANES), F32)]
    return jnp.concatenate(rows, axis=0)


def _small_unpack(a):
    return (a[0:8].reshape(1, D_MODEL), a[8:16].reshape(D_MODEL), a[16:18].reshape(1, MLA_Q_RANK),
            a[18:19].reshape(1, MLA_KV_RANK), a[19:20, :SWA_HEADS], a[20:21, :FOX_HEADS], a[21, 0])


def _local_step(x, positions, target, e_g_in, early, e_g_q_a, e_g_kv_a, e_sinks,
                late, o_b_f, g_final, scatter1=None, scatter0=None):
    s = x.shape[0]
    mla_scale = (MLA_NOPE + MLA_ROPE) ** -0.5
    fox_scale = FOX_DIM ** -0.5
    n0a = Z0A_UNITS * LANES

    inv_freq = 1.0 / (ROPE_THETA ** (jnp.arange(0, MLA_ROPE, 2, dtype=F32) / MLA_ROPE))
    ang = positions.astype(F32)[:, None] * inv_freq
    cos, sin = jnp.cos(ang), jnp.sin(ang)
    ones, zeros = jnp.ones((s, 64), F32), jnp.zeros((s, 64), F32)
    cos_t = jnp.concatenate([ones, cos, cos, ones[:, :32]], axis=1)
    sin_t = jnp.concatenate([zeros, -sin, sin, zeros[:, :32]], axis=1)

    if len(early) == 3:
        h0 = _rmsnorm_fwd(x, e_g_in, width=D_MODEL, col_blk=0, name="l0_norm")
        w0t, wq, wkv = early
    else:
        pending, token, unpack, prep = early
        h0 = _rmsnorm_fwd(x, e_g_in, width=D_MODEL, col_blk=0, name="l0_norm", after=[token])
        w0t, wq, wkv = unpack(*_peer_wait("gather", *pending, after=[h0] + prep, name="weights0_wait"))
    z0a = _matmul(h0, w0t, tb=True, b_rows=(0, n0a), name="l0_in_a")
    z0b = _matmul(h0, w0t, tb=True, b_rows=(n0a, Z0B_UNITS * LANES), name="l0_in_b", out_dtype=BF16)
    cqn = _rmsnorm_fwd(z0a, e_g_q_a, width=MLA_Q_RANK, col_blk=4, name="l0_q_norm")
    ckvn = _rmsnorm_fwd(z0a, e_g_kv_a, width=MLA_KV_RANK, col_blk=10, name="l0_kv_norm")
    qp = _matmul(cqn, wq, name="l0_q_up")
    kvp = _matmul(ckvn, wkv, name="l0_kv_up", out_dtype=BF16)
    qm, km = _rope_fwd(qp, kvp, z0a, cos_t, sin_t, name="l0_rope")
    gathers = len(late) == 2
    res = _flash_fwd(qm, km, kvp, None, n_pairs=MLA_HEADS // 2, hw=LANES, q_off=0, k_off=0, v_off=MLA_HEADS,
                     scale=mla_scale, name="l0_mla_fwd", rider=("gather", late[0]) if gathers else None)
    o_mla, lse_mla = res[0], res[1]
    wo0, o_g_in, w1t, wft, wo1 = late[1](res[2]) if gathers else late
    o_swa, lse_swa = _swa_fwd(z0b, e_sinks, name="l0_swa_fwd")
    og0 = _gate_fwd([o_mla, o_swa], z0a, name="l0_gate")

    x1, h1 = _matmul_rows([(og0, wo0, False)], [(x, D_MODEL, 0)], [o_g_in], _residual_norm_epilogue,
                          [("rows", D_MODEL, F32), ("rows", D_MODEL, BF16)], name="l0_out")
    z1 = _matmul(h1, w1t, tb=True, b_rows=(0, 3 * D_MODEL), name="l1_in_qkv", out_dtype=BF16)
    gate1 = _matmul(h1, w1t, tb=True, b_rows=(3 * D_MODEL, D_MODEL), name="l1_in_gate")
    zf = _matmul(h1, wft, tb=True, name="l1_in_f")
    bf = _pad_lanes(o_b_f)
    log_cum = _logf_fwd(zf, bf, name="l1_logf")
    bias2 = (-LOG2E * log_cum[:, :FOX_HEADS]).T
    t_bwd = min(ATT_T, s)
    bias = bias2.reshape(FOX_HEADS // 2, 2, s // t_bwd, 1, t_bwd)
    t_fwd = _fwd_tile(s)
    o_fox, lse_fox = _flash_fwd(z1, z1, z1, bias2.reshape(FOX_HEADS // 2, 2, s // t_fwd, 1, t_fwd),
                                n_pairs=FOX_HEADS // 2, hw=64, q_off=0, k_off=8, v_off=16, scale=fox_scale,
                                name="l1_fox_fwd")
    og1 = _gate_fwd([o_fox], gate1, name="l1_gate")

    dx2, loss_part, d_g_final = _matmul_rows(
        [(og1, wo1, False)], [(x1, D_MODEL, 0), (target, D_MODEL, 0)], [g_final.reshape(1, D_MODEL)], _loss_epilogue,
        [("rows", D_MODEL, F32), ("sum", (8, LANES)), ("sum", (1, D_MODEL))], name="l1_out_loss")

    d_wo1 = _matmul(og1, dx2, ta=True, name="l1_out_dw")
    do_fox, d_gate1 = _matmul_rows([(dx2, wo1, True)], [(o_fox, D_MODEL, 0), (gate1, D_MODEL, 0)], [],
                                   _gate_bwd_epilogue([D_MODEL]), [("rows", D_MODEL, F32), ("rows", D_MODEL, BF16)],
                                   name="l1_out_dx")
    dq1, dk1, dv1, dbias, drow = _flash_bwd(z1, z1, z1, do_fox, o_fox, lse_fox, bias, n_pairs=FOX_HEADS // 2, hw=64,
                                            q_off=0, k_off=8, v_off=16, scale=fox_scale, qk_dtype=BF16,
                                            name="l1_fox_bwd")
    d_log_cum = (drow.reshape(FOX_HEADS, s) - dbias.reshape(FOX_HEADS, s)).T
    d_log_cum = jnp.pad(d_log_cum, ((0, 0), (0, LANES - FOX_HEADS)))
    d_zf, d_bf = _logf_bwd(d_log_cum, zf, bf, name="l1_logf_bwd")
    dz1 = (dq1, dk1, dv1, d_gate1)
    d_w1t = tuple(_matmul(d, h1, ta=True, name=f"l1_in_dw_{k}") for k, d in enumerate(dz1))
    d_wft = _matmul(d_zf, h1, ta=True, name="l1_in_f_dw")
    dx1, d_o_g_in = _matmul_rows([(d, w1t, False, k * D_MODEL) for k, d in enumerate(dz1)] + [(d_zf, wft, False)],
                                 [(x1, D_MODEL, 0), (dx2, D_MODEL, 0)],
                                 [o_g_in], _rms_bwd_epilogue, [("rows", D_MODEL, F32), ("sum", (1, D_MODEL))],
                                 name="l1_in_dx")

    d_wo0 = _matmul(og0, dx1, ta=True, name="l0_out_dw")
    half = D_MODEL // 2
    do_mla, do_swa, d_gate0 = _matmul_rows(
        [(dx1, wo0, True)], [(o_mla, half, 0), (o_swa, half, 0), (z0a, D_MODEL, 0)], [], _gate_bwd_epilogue([half, half]),
        [("rows", half, F32), ("rows", half, F32), ("rows", D_MODEL, BF16)], name="l0_out_dx")
    dq_s, dkt_s, dvt_s, d_sinks = _swa_bwd(z0b, e_sinks, do_swa, o_swa, lse_swa, name="l0_swa_bwd")
    dk_s = dkt_s.transpose(0, 2, 1).reshape(s, LANES)
    dv_s = dvt_s.transpose(0, 2, 1).reshape(s, LANES)
    rider = None
    if scatter1 is not None:
        rider = ("exchange", scatter1(dict(w1t=d_w1t, wft=d_wft, wo1=d_wo1, o_g_in=d_o_g_in, wo0=d_wo0)))
    res = _flash_bwd(qm, km, kvp, do_mla, o_mla, lse_mla, None, n_pairs=MLA_HEADS // 2, hw=LANES, q_off=0, k_off=0,
                     v_off=MLA_HEADS, scale=mla_scale, qk_dtype=F32, name="l0_mla_bwd", rider=rider)
    dqm, dkm, dvm = res[0], res[1], res[2]
    recv1 = res[3] if rider is not None else None
    d_qp, d_kvp, d_kpe = _rope_bwd(dqm, dkm, dvm, cos_t, sin_t, name="l0_rope_bwd")
    d_wq = _matmul(cqn, d_qp, ta=True, name="l0_q_up_dw")
    d_cqn = _matmul(d_qp, wq, tb=True, name="l0_q_up_dx")
    d_wkv = _matmul(ckvn, d_kvp, ta=True, name="l0_kv_up_dw")
    d_ckvn = _matmul(d_kvp, wkv, tb=True, name="l0_kv_up_dx")
    d_cq, d_g_q_a = _rmsnorm_bwd(z0a, e_g_q_a, d_cqn, width=MLA_Q_RANK, col_blk=4, name="l0_q_norm_bwd")
    d_ckv, d_g_kv_a = _rmsnorm_bwd(z0a, e_g_kv_a, d_ckvn, width=MLA_KV_RANK, col_blk=10, name="l0_kv_norm_bwd")
    dz0 = jnp.concatenate([d_gate0, d_cq, d_ckv, d_kpe, dq_s.astype(BF16), dk_s.astype(BF16), dv_s.astype(BF16)], axis=1)
    d_w0t = _matmul(dz0, h0, ta=True, name="l0_in_dw")
    pending0, after_start = None, []
    if scatter0 is not None:
        *pending0, token = _peer_start("exchange", scatter0(dict(w0t=d_w0t, wq=d_wq, wkv=d_wkv)), name="grads0_start")
        after_start = [token]
    grad_x, d_e_g_in = _matmul_rows(
        [(dz0, w0t, False)], [(x, D_MODEL, 0), (dx1, D_MODEL, 0)], [e_g_in] + after_start,
        lambda dy, xt, add, g, *_: _rms_bwd_epilogue(dy, xt, add, g),
        [("rows", D_MODEL, F32), ("sum", (1, D_MODEL))], name="l0_in_dx")

    return dict(pending0=pending0, recv1=recv1, loss=loss_part[0, 0], grad_x=grad_x, e_g_in=d_e_g_in, w0t=d_w0t, e_g_q_a=d_g_q_a, wq=d_wq,
                e_g_kv_a=d_g_kv_a, wkv=d_wkv, e_sinks=d_sinks[:, 0].reshape(1, SWA_HEADS), wo0=d_wo0,
                o_g_in=d_o_g_in, w1t=d_w1t, wft=d_wft, o_b_f=d_bf[:, :FOX_HEADS], wo1=d_wo1, g_final=d_g_final.reshape(D_MODEL))


def _wide(a, rows):
    flat = a.reshape(-1)
    return jnp.pad(flat, (0, rows * WIDE - flat.shape[0])).reshape(rows, WIDE)


def _rows_b0(w_q, w_kv):
    return jnp.concatenate([_wide(w_q, 32), _wide(w_kv, 16)], axis=0)


def _unflat_b0(f):
    return f[0:24].reshape(1, MLA_Q_RANK, 96), f[32:48].reshape(1, MLA_KV_RANK, 128)


def _rows_b1(o_w_out, e_w_out, g_in):
    return jnp.concatenate([o_w_out, e_w_out, _wide(g_in, 16)], axis=0)


def _unflat_b1(f):
    return f[0:128][None], f[128:256][None], f[256:257, :LANES]


def kernel(x, positions, e_g_in, e_w_in, e_g_q_a, e_w_q_up, e_g_kv_a, e_w_kv_up, e_sinks, e_w_out, o_g_in, o_w_in, o_b_f, o_w_out, g_final, loss_target, m_e_g_in, m_e_w_in, m_e_g_q_a, m_e_w_q_up, m_e_g_kv_a, m_e_w_kv_up, m_e_sinks, m_e_w_out, m_o_g_in, m_o_w_in, m_o_b_f, m_o_w_out, m_g_final, v_e_g_in, v_e_w_in, v_e_g_q_a, v_e_w_q_up, v_e_g_kv_a, v_e_w_kv_up, v_e_sinks, v_e_w_out, v_o_g_in, v_o_w_in, v_o_b_f, v_o_w_out, v_g_final):
    def bf(a):
        return a.astype(BF16)

    me = 4 * lax.axis_index("x") + 2 * lax.axis_index("y") + lax.axis_index("c")
    shard0 = jnp.concatenate([_pad_rows(bf(e_w_in[0]).T, RA0), _rows_b0(bf(e_w_q_up[0]), bf(e_w_kv_up[0]))], axis=0)
    *pending_w0, token_w0 = _peer_start("gather", shard0, name="weights0_start")

    def unpack0(sent, gath0):
        gath0 = lax.dynamic_update_slice_in_dim(gath0, sent[None], me, axis=0)
        w0t = _layer0_in_weight_t(gath0[:, :N_E_IN].reshape(N_DEV * N_E_IN, WIDE))
        wq = _q_up_weight(_gathered_cols(gath0[:, RA0:RA0 + 24], MLA_Q_RANK))
        wkv = _kv_up_weight(_gathered_cols(gath0[:, RA0 + 32:RA0 + 48], MLA_KV_RANK))
        return w0t, wq, wkv

    rows_b0 = [_rows_b0(q[0], kv[0]) for q, kv in ((e_w_q_up, e_w_kv_up), (m_e_w_q_up, m_e_w_kv_up), (v_e_w_q_up, v_e_w_kv_up))]
    rows_b1 = [_rows_b1(o[0], e[0], g) for o, e, g in ((o_w_out, e_w_out, o_g_in), (m_o_w_out, m_e_w_out, m_o_g_in),
                                                       (v_o_w_out, v_e_w_out, v_o_g_in))]

    g_bits = lax.bitcast_convert_type(o_g_in.reshape(LANES), BF16)
    shard1 = jnp.concatenate([_pad_rows(bf(o_w_in[0]).T, RA1), _rows_b1(bf(o_w_out[0]), bf(e_w_out[0]), g_bits)], axis=0)

    def unpack1(gath1):
        w1t, wft = _layer1_in_weight_t(gath1[:, :N_O_IN].reshape(N_DEV * N_O_IN, WIDE))
        wo1 = gath1[:, RA1:RA1 + 128].reshape(D_MODEL, D_MODEL)
        wo0 = gath1[:, RA1 + 128:RA1 + 256].reshape(D_MODEL, D_MODEL)
        bits = gath1[:, RA1 + 256, :2 * LANES].reshape(N_DEV, LANES, 2)
        return wo0, lax.bitcast_convert_type(bits, F32).reshape(1, D_MODEL), w1t, wft, wo1

    def scatter1(g):
        d_in_t = _layer1_in_grad_t(g["w1t"], g["wft"]).reshape(N_DEV, N_O_IN, WIDE)
        d_o_g = jnp.pad(g["o_g_in"].reshape(N_DEV, 1, LANES), ((0, 0), (0, 15), (0, WIDE - LANES)))
        return jnp.concatenate([_pad_rows(d_in_t, RA1), g["wo1"].reshape(N_DEV, 128, WIDE),
                                g["wo0"].reshape(N_DEV, 128, WIDE), d_o_g], axis=1).astype(BF16)

    def scatter0(g):
        return jnp.concatenate([
            _pad_rows(_layer0_in_grad_t(g["w0t"]).reshape(N_DEV, N_E_IN, WIDE), RA0),
            _pad_rows(_scatter_cols(_q_up_grad(g["wq"])), 32), _scatter_cols(_kv_up_grad(g["wkv"]))], axis=1).astype(BF16)

    gr = _local_step(x[0], positions[0], loss_target[0], e_g_in,
                     (pending_w0, token_w0, unpack0, [shard1] + rows_b0 + rows_b1), e_g_q_a, e_g_kv_a, e_sinks,
                     (shard1, unpack1), o_b_f, g_final, scatter1=scatter1, scatter0=scatter0)

    def in_projection(recv, ra, n, w, m, v, name):
        g = _sum8(recv, ra, name=name + "_grad_sum")[:n].T
        d, nm, nv = _adamw_native(g, w[0], m[0], v[0], name=name + "_adamw")
        return g[None], d[None], nm[None], nv[None]

    o_in = in_projection(gr["recv1"], RA1, N_O_IN, o_w_in, m_o_w_in, v_o_w_in, "o_w_in")
    b1 = _adamw(gr["recv1"][:, RA1:], *rows_b1, name="adamw_late")

    sent0, recv0 = _peer_wait("exchange", *gr["pending0"], after=[o_in[1], b1[1]], name="grads0_wait")
    own = lax.dynamic_slice_in_dim(sent0, me, 1, axis=0)
    recv0 = lax.dynamic_update_slice_in_dim(recv0, own, me, axis=0)

    small = _small_pack(gr["e_g_in"], gr["g_final"], gr["e_g_q_a"], gr["e_g_kv_a"], gr["e_sinks"], gr["o_b_f"], gr["loss"])
    small_all = _all_gather(small, name="small_all_gather")
    zero = jnp.zeros((), F32)
    w_small = _small_pack(e_g_in, g_final, e_g_q_a, e_g_kv_a, e_sinks, o_b_f, zero)
    m_small = _small_pack(m_e_g_in, m_g_final, m_e_g_q_a, m_e_g_kv_a, m_e_sinks, m_o_b_f, zero)
    v_small = _small_pack(v_e_g_in, v_g_final, v_e_g_q_a, v_e_g_kv_a, v_e_sinks, v_o_b_f, zero)
    smalls = _adamw(small_all, w_small, m_small, v_small, name="adamw_replicated")
    g_sm, d_sm, m_sm, v_sm = [_small_unpack(a) for a in smalls]
    loss = g_sm[6]

    e_in = in_projection(recv0, RA0, N_E_IN, e_w_in, m_e_w_in, v_e_w_in, "e_w_in")
    b0 = _adamw(recv0[:, RA0:], *rows_b0, name="adamw_early")

    def sharded(k):
        q_up, kv_up = _unflat_b0(b0[k])
        o_out, e_out, o_g = _unflat_b1(b1[k])
        return e_in[k], q_up, kv_up, e_out, o_in[k], o_out, o_g

    g_sh, d_sh, m_sh, v_sh = [sharded(k) for k in range(4)]

    def leaves(sh, sm):
        return (sm[0], sh[0], sm[2], sh[1], sm[3], sh[2], sm[4], sh[3], sh[6], sh[4], sm[5], sh[5], sm[1])

    return (loss, gr["grad_x"][None], *leaves(g_sh, g_sm), *leaves(d_sh, d_sm), *leaves(m_sh, m_sm), *leaves(v_sh, v_sm))
```

```python
import functools

import jax
import jax.numpy as jnp
from jax import lax
from jax.experimental import pallas as pl
from jax.experimental.pallas import tpu as pltpu

F32 = jnp.float32
BF16 = jnp.bfloat16
NEG_INF = float("-inf")

N_DEV = 8
LANES = 128
D_MODEL = 1024
EPS = 1e-6
ROPE_THETA = 10000.0
MLA_HEADS = 8
MLA_Q_RANK = 256
MLA_KV_RANK = 128
MLA_NOPE = 64
MLA_ROPE = 32
MLA_V = 64
SWA_HEADS = 8
SWA_KV_HEADS = 2
SWA_DIM = 64
WINDOW = 128
FOX_HEADS = 16
FOX_DIM = 64

ADAM_LR = 0.001
ADAM_B1 = 0.9
ADAM_B2 = 0.999
ADAM_EPS = 1e-08
ADAM_WD = 0.01
ADAM_STEP = 10

ATT_T = 512
ATT_T_FWD = 1024
VMEM_LIMIT = 56 * 1024 * 1024
MATMUL_B_BLOCK_BYTES = 8 * 1024 * 1024

Z0A_UNITS = 12
Z0B_UNITS = 6

WIDE = 1024
N_E_IN = 276
N_O_IN = 514
RA0 = 288
RB0 = 32 + 16
RA1 = 528
RB1 = 128 + 128 + 16
SMALL_ROWS = 24


def _tile(n, cands):
    for c in cands:
        if n % c == 0:
            return c
    raise ValueError(f"no tile for {n}")


ROW_TILES = (512, 256, 128)


def _params(sem, vmem=VMEM_LIMIT):
    return pltpu.CompilerParams(dimension_semantics=sem, vmem_limit_bytes=vmem)


def _matmul(a, b, *, name, ta=False, tb=False, out_dtype=F32, b_rows=None):
    if ta:
        kdim, m = a.shape[-2], a.shape[-1] * (a.shape[0] if a.ndim == 3 else 1)
    else:
        m, kdim = a.shape
    if tb:
        n, kb = b.shape
    else:
        kb, n = b.shape
    assert kdim == kb, (a.shape, b.shape)
    b_start = 0
    if b_rows is not None:
        assert tb
        b_start, n = b_rows
    tm = _tile(m, (512, 256, 128))
    tn = _tile(n, [c for c in (1024, 768, 512, 384, 256, 128)
                   if c * kdim * b.dtype.itemsize <= MATMUL_B_BLOCK_BYTES and b_start % c == 0])
    assert b_start % tn == 0, (b_start, tn)
    b_off = b_start // tn
    dims = (((0 if ta else 1,), (1 if tb else 0,)), ((), ()))

    def body(a_ref, b_ref, o_ref):
        r = lax.dot_general(a_ref[...].astype(BF16), b_ref[...].astype(BF16), dims, preferred_element_type=F32)
        o_ref[...] = r.astype(out_dtype)

    if a.ndim == 3:
        per = a.shape[2] // tm
        a_spec = pl.BlockSpec((None, kdim, tm), lambda i, j: (i // per, 0, i % per))
    else:
        a_spec = pl.BlockSpec((kdim, tm), lambda i, j: (0, i)) if ta else pl.BlockSpec((tm, kdim), lambda i, j: (i, 0))
    b_spec = pl.BlockSpec((tn, kdim), lambda i, j: (j + b_off, 0)) if tb else pl.BlockSpec((kdim, tn), lambda i, j: (0, j))
    return pl.pallas_call(
        body, name=name, grid=(m // tm, n // tn), in_specs=[a_spec, b_spec],
        out_specs=pl.BlockSpec((tm, tn), lambda i, j: (i, j)), out_shape=jax.ShapeDtypeStruct((m, n), out_dtype),
        compiler_params=_params(("parallel", "parallel")),
    )(a, b)


def _rmsnorm_fwd(x, g, *, width, col_blk, name, after=()):
    s = x.shape[0]
    tm = _tile(s, ROW_TILES)

    def body(x_ref, g_ref, *rest):
        y_ref = rest[-1]
        xf = x_ref[...].astype(F32)
        r = lax.rsqrt(jnp.mean(xf * xf, axis=-1, keepdims=True) + EPS)
        y_ref[...] = ((xf * r) * g_ref[...]).astype(BF16)

    return pl.pallas_call(
        body, name=name, grid=(s // tm,),
        in_specs=[pl.BlockSpec((tm, width), lambda i: (i, col_blk)), pl.BlockSpec((1, width), lambda i: (0, 0))]
        + [ANY] * len(after),
        out_specs=pl.BlockSpec((tm, width), lambda i: (i, 0)),
        out_shape=jax.ShapeDtypeStruct((s, width), BF16),
        compiler_params=_params(("parallel",)),
    )(x, g, *after)


def _rmsnorm_bwd(x, g, dy, *, width, col_blk, name):
    s = x.shape[0]
    tm = _tile(s, ROW_TILES)

    def body(x_ref, g_ref, dy_ref, dx_ref, dg_ref):
        @pl.when(pl.program_id(0) == 0)
        def _():
            dg_ref[...] = jnp.zeros_like(dg_ref)

        dx, dg = _rms_bwd_epilogue(dy_ref[...], x_ref[...], 0.0, g_ref[...])
        dg_ref[...] += dg
        dx_ref[...] = dx.astype(BF16)

    return pl.pallas_call(
        body, name=name, grid=(s // tm,),
        in_specs=[pl.BlockSpec((tm, width), lambda i: (i, col_blk)), pl.BlockSpec((1, width), lambda i: (0, 0)),
                  pl.BlockSpec((tm, width), lambda i: (i, 0))],
        out_specs=[pl.BlockSpec((tm, width), lambda i: (i, 0)), pl.BlockSpec((1, width), lambda i: (0, 0))],
        out_shape=[jax.ShapeDtypeStruct((s, width), BF16), jax.ShapeDtypeStruct((1, width), F32)],
        compiler_params=_params(("arbitrary",)),
    )(x, g, dy)


def _sigmoid(x):
    return 1.0 / (1.0 + jnp.exp(-x))


def _gate_fwd(o_parts, gate, *, name):
    s = gate.shape[0]
    tm = _tile(s, ROW_TILES)
    n_o = len(o_parts)

    def body(*refs):
        o_refs, g_ref, y_ref = refs[:n_o], refs[n_o], refs[n_o + 1]
        o = o_refs[0][...] if n_o == 1 else jnp.concatenate([r[...] for r in o_refs], axis=1)
        gt = g_ref[...]
        y_ref[...] = (o * (gt * _sigmoid(gt))).astype(BF16)

    in_specs = [pl.BlockSpec((tm, o.shape[1]), lambda i: (i, 0)) for o in o_parts]
    in_specs.append(pl.BlockSpec((tm, D_MODEL), lambda i: (i, 0)))
    return pl.pallas_call(
        body, name=name, grid=(s // tm,), in_specs=in_specs,
        out_specs=pl.BlockSpec((tm, D_MODEL), lambda i: (i, 0)),
        out_shape=jax.ShapeDtypeStruct((s, D_MODEL), BF16),
        compiler_params=_params(("parallel",)),
    )(*o_parts, gate)


def _matmul_rows(terms, row_inputs, params, epilogue, outs, *, name):
    s = terms[0][0].shape[-2]
    tm = _tile(s, ROW_TILES)
    steps = s // tm
    n_t, n_r, n_p, n_o = len(terms), len(row_inputs), len(params), len(outs)

    def body(*refs):
        t_refs, r_refs = refs[:2 * n_t], refs[2 * n_t:2 * n_t + n_r]
        p_refs, o_refs = refs[2 * n_t + n_r:2 * n_t + n_r + n_p], refs[2 * n_t + n_r + n_p:]
        i = pl.program_id(0)
        acc = None
        for k, term in enumerate(terms):
            dims = (((1,), (1 if term[2] else 0,)), ((), ()))
            part = lax.dot_general(t_refs[2 * k][...].astype(BF16), t_refs[2 * k + 1][...].astype(BF16), dims,
                                   preferred_element_type=F32)
            acc = part if acc is None else acc + part
        vals = epilogue(acc, *[r[...] for r in r_refs], *[p[...] for p in p_refs])
        for ref, val, out in zip(o_refs, vals, outs):
            if out[0] == "rows":
                ref[...] = val.astype(ref.dtype)
            else:
                @pl.when(i == 0)
                def _(ref=ref):
                    ref[...] = jnp.zeros_like(ref)

                ref[...] += val

    in_specs, args = [], []
    for term in terms:
        a, b = term[0], term[1]
        b_rows = b.shape[0] if term[2] or len(term) < 4 else a.shape[-1]
        b_blk = 0 if len(term) < 4 else term[3] // b_rows
        if len(term) == 5:
            a_spec = pl.BlockSpec((None, tm, a.shape[2]), lambda i, c=term[4]: (c, i, 0))
        else:
            a_spec = pl.BlockSpec((tm, a.shape[1]), lambda i: (i, 0))
        in_specs += [a_spec, _resident((b_rows, b.shape[1]), lambda i, b_blk=b_blk: (b_blk, 0))]
        args += [a, b]
    for arr, width, col_blk in row_inputs:
        in_specs.append(pl.BlockSpec((tm, width), lambda i, col_blk=col_blk: (i, col_blk)))
        args.append(arr)
    for p in params:
        in_specs.append(pl.BlockSpec(p.shape, lambda i: (0, 0)))
        args.append(p)
    out_specs, out_shape = [], []
    for out in outs:
        if out[0] == "rows":
            out_specs.append(pl.BlockSpec((tm, out[1]), lambda i: (i, 0)))
            out_shape.append(jax.ShapeDtypeStruct((s, out[1]), out[2]))
        else:
            out_specs.append(pl.BlockSpec(out[1], lambda i: (0, 0)))
            out_shape.append(jax.ShapeDtypeStruct(out[1], F32))
    return pl.pallas_call(
        body, name=name, grid=(steps,), in_specs=in_specs, out_specs=out_specs, out_shape=out_shape,
        compiler_params=_params(("arbitrary",)),
    )(*args)


def _rms_stats(x):
    r = lax.rsqrt(jnp.mean(x * x, axis=-1, keepdims=True) + EPS)
    return r, x * r


def _residual_norm_epilogue(r, x, g):
    x1 = x + r
    _, xh = _rms_stats(x1)
    return x1, xh * g


def _rms_bwd_epilogue(dy, x, add, g):
    r, xh = _rms_stats(x)
    dxh = dy * g
    dx = r * (dxh - xh * jnp.mean(dxh * xh, axis=-1, keepdims=True)) + add
    return dx, jnp.sum(dy * xh, axis=0, keepdims=True)


def _loss_epilogue(r, x1, target, g):
    rs, xh = _rms_stats(x1 + r)
    err = xh * g - target
    loss = jnp.broadcast_to(0.5 * jnp.sum(jnp.mean(err * err, axis=-1, keepdims=True)), (8, LANES))
    dy = err * (1.0 / D_MODEL)
    dxh = dy * g
    dx = rs * (dxh - xh * jnp.mean(dxh * xh, axis=-1, keepdims=True))
    return dx, loss, jnp.sum(dy * xh, axis=0, keepdims=True)


def _gate_bwd_epilogue(widths):
    def epilogue(d, *rows):
        o_parts, gt = rows[:-1], rows[-1]
        o = o_parts[0] if len(o_parts) == 1 else jnp.concatenate(o_parts, axis=1)
        sg = _sigmoid(gt)
        do = d * (gt * sg)
        d_gate = d * o * (sg * (1.0 + gt * (1.0 - sg)))
        cuts = [sum(widths[:k]) for k in range(len(widths) + 1)]
        return tuple(do[:, cuts[k]:cuts[k + 1]] for k in range(len(widths))) + (d_gate,)

    return epilogue


def _rot_half(x):
    lane = lax.broadcasted_iota(jnp.int32, x.shape, 1)
    return jnp.where(lane < 80, pltpu.roll(x, LANES - 16, axis=1), pltpu.roll(x, 16, axis=1))


def _rot_half_t(g):
    lane = lax.broadcasted_iota(jnp.int32, g.shape, 1)
    lo = (lane >= MLA_NOPE) & (lane < MLA_NOPE + MLA_ROPE // 2)
    hi = (lane >= MLA_NOPE + MLA_ROPE // 2) & (lane < MLA_NOPE + MLA_ROPE)
    return jnp.where(lo, pltpu.roll(g, LANES - 16, axis=1), jnp.where(hi, pltpu.roll(g, 16, axis=1), 0.0))


def _rope_fwd(qp, kvp, z0a, cos_t, sin_t, *, name):
    s = qp.shape[0]
    tm = _tile(s, ROW_TILES)
    hw = MLA_HEADS * LANES

    def body(q_ref, k_ref, kpe_ref, c_ref, s_ref, qm_ref, km_ref):
        c = c_ref[...]
        sn = s_ref[...]
        kpe = kpe_ref[...]
        kpe_r = (kpe * c + _rot_half(kpe) * sn).astype(BF16)
        lane = lax.broadcasted_iota(jnp.int32, kpe.shape, 1)
        for h in range(MLA_HEADS):
            sl = slice(h * LANES, (h + 1) * LANES)
            qh = q_ref[:, sl]
            qm_ref[:, sl] = (qh * c + _rot_half(qh) * sn).astype(BF16)
            km_ref[:, sl] = jnp.where(lane < MLA_NOPE, k_ref[:, sl], kpe_r)

    return pl.pallas_call(
        body, name=name, grid=(s // tm,),
        in_specs=[pl.BlockSpec((tm, hw), lambda i: (i, 0)), pl.BlockSpec((tm, hw), lambda i: (i, 0)),
                  pl.BlockSpec((tm, LANES), lambda i: (i, 11)),
                  pl.BlockSpec((tm, LANES), lambda i: (i, 0)), pl.BlockSpec((tm, LANES), lambda i: (i, 0))],
        out_specs=[pl.BlockSpec((tm, hw), lambda i: (i, 0)), pl.BlockSpec((tm, hw), lambda i: (i, 0))],
        out_shape=[jax.ShapeDtypeStruct((s, hw), BF16), jax.ShapeDtypeStruct((s, hw), BF16)],
        compiler_params=_params(("parallel",)),
    )(qp, kvp, z0a, cos_t, sin_t)


def _rope_bwd(dqm, dkm, dvm, cos_t, sin_t, *, name):
    s = dqm.shape[0]
    tm = _tile(s, ROW_TILES)
    hw = MLA_HEADS * LANES
    vw = MLA_HEADS * MLA_V

    def body(dq_ref, dk_ref, dv_ref, c_ref, s_ref, dqp_ref, dkv_ref, dkpe_ref):
        c = c_ref[...]
        sn = s_ref[...]
        ksum = jnp.zeros((tm, LANES), F32)
        for h in range(MLA_HEADS):
            sl = slice(h * LANES, (h + 1) * LANES)
            dq = dq_ref[:, sl]
            dqp_ref[:, sl] = (dq * c + _rot_half_t(dq * sn)).astype(BF16)
            dk = dk_ref[:, sl]
            dkv_ref[:, sl] = dk.astype(BF16)
            ksum = ksum + dk
        dkv_ref[:, hw:] = dv_ref[...]
        lane = lax.broadcasted_iota(jnp.int32, ksum.shape, 1)
        dkpe = ksum * c + _rot_half_t(ksum * sn)
        dkpe_ref[...] = jnp.where((lane >= MLA_NOPE) & (lane < MLA_NOPE + MLA_ROPE), dkpe, 0.0).astype(BF16)

    return pl.pallas_call(
        body, name=name, grid=(s // tm,),
        in_specs=[pl.BlockSpec((tm, hw), lambda i: (i, 0)), pl.BlockSpec((tm, hw), lambda i: (i, 0)),
                  pl.BlockSpec((tm, vw), lambda i: (i, 0)),
                  pl.BlockSpec((tm, LANES), lambda i: (i, 0)), pl.BlockSpec((tm, LANES), lambda i: (i, 0))],
        out_specs=[pl.BlockSpec((tm, hw), lambda i: (i, 0)), pl.BlockSpec((tm, hw + vw), lambda i: (i, 0)),
                   pl.BlockSpec((tm, LANES), lambda i: (i, 0))],
        out_shape=[jax.ShapeDtypeStruct((s, hw), BF16), jax.ShapeDtypeStruct((s, hw + vw), BF16),
                   jax.ShapeDtypeStruct((s, LANES), BF16)],
        compiler_params=_params(("parallel",)),
    )(dqm, dkm, dvm, cos_t, sin_t)


def _head_mask(shape, a):
    lane = lax.broadcasted_iota(jnp.int32, shape, 1)
    return (lane >= 64 * a) & (lane < 64 * (a + 1))


_NT = (((1,), (1,)), ((), ()))
LOG2E = 1.4426950408889634


def _stack_heads(tile, hw):
    lane = lax.broadcasted_iota(jnp.int32, tile.shape, 1)
    z = jnp.zeros_like(tile)
    return jnp.concatenate([jnp.where(lane < hw, tile, z), jnp.where(lane >= hw, tile, z)], axis=0)


def _stacked_rows(r0, r1, t):
    n = r0.shape[-1]
    return jnp.concatenate([jnp.broadcast_to(r0, (t, n)), jnp.broadcast_to(r1, (t, n))], axis=0)


def _resident(block, index_map):
    return pl.BlockSpec(block, index_map, pipeline_mode=pl.Buffered(1))


def _fwd_tile(s):
    return ATT_T_FWD if s % ATT_T_FWD == 0 else min(ATT_T, s)


def _flash_fwd(q, k, v, bias, *, n_pairs, hw, q_off, k_off, v_off, scale, name, rider=None):
    s = q.shape[0]
    t = _fwd_tile(s)
    nb = s // t
    qw = 2 * hw
    has_bias = bias is not None
    c1 = scale * LOG2E

    def body(*refs):
        refs, ride_refs = _split_rider(refs, rider, n_in=4 if has_bias else 3, n_out=2)
        if has_bias:
            q_ref, k_ref, v_ref, b_ref, o_ref, lse_ref, vt_ref, bcol_ref = refs
        else:
            q_ref, k_ref, v_ref, o_ref, lse_ref, vt_ref = refs
            b_ref = bcol_ref = None
        _ride_start(rider, ride_refs, pl.program_id(0) == 0)
        row = lax.broadcasted_iota(jnp.int32, (t, t), 0)
        col = lax.broadcasted_iota(jnp.int32, (t, t), 1)
        cmask_t = jnp.concatenate([row <= col, row <= col], axis=1)
        lane_lt64 = lax.broadcasted_iota(jnp.int32, (t, LANES), 1) < 64

        def as_column(r):
            return jnp.broadcast_to(r, (8, r.shape[1])).T[:, 0:1]

        def v_block(j, _):
            c0 = pl.multiple_of(j * t, t)
            vt_ref[j] = v_ref[pl.ds(c0, t), :].astype(F32).T.astype(BF16)
            if has_bias:
                for a in range(2):
                    bcol_ref[a, pl.ds(c0, t), :] = as_column(b_ref[0, a, j])
            return 0

        lax.fori_loop(0, nb, v_block, 0)

        def stacked_queries(i):
            return _stack_heads(q_ref[pl.ds(pl.multiple_of(i * t, t), t), :], hw).astype(F32).T.astype(BF16)

        def kv_step(j, carry, qs_t, masked):
            m, l, acc = carry
            rows = pl.ds(pl.multiple_of(j * t, t), t)
            sc = jnp.dot(k_ref[rows, :], qs_t, preferred_element_type=F32) * c1
            if has_bias:
                sc = sc + jnp.concatenate([jnp.broadcast_to(bcol_ref[0, rows, :], (t, t)),
                                           jnp.broadcast_to(bcol_ref[1, rows, :], (t, t))], axis=1)
            if masked:
                sc = jnp.where(cmask_t, sc, NEG_INF)
            m_new = jnp.maximum(m, jnp.max(sc, axis=0, keepdims=True))
            alpha = jnp.exp2(m - m_new)
            p = jnp.exp2(sc - m_new)
            l_new = alpha * l + jnp.sum(p, axis=0, keepdims=True)
            pv = jnp.dot(vt_ref[j], p.astype(BF16), preferred_element_type=F32)
            return m_new, l_new, alpha * acc + pv

        def finish(i, carry):
            m, l, acc = carry
            r0 = pl.multiple_of(i * t, t)
            out = (acc / l).T
            lse2 = as_column(m + jnp.log2(l))
            lse_ref[0, 0, pl.ds(r0, t), :] = lse2[:t]
            lse_ref[0, 1, pl.ds(r0, t), :] = lse2[t:]
            o_ref[pl.ds(r0, t), :] = jnp.where(lane_lt64, out[:t], out[t:])

        init = (jnp.full((1, 2 * t), NEG_INF, F32), jnp.zeros((1, 2 * t), F32), jnp.zeros((LANES, 2 * t), F32))

        def q_block(i, _):
            qs_t = stacked_queries(i)
            carry = lax.fori_loop(0, i, lambda j, c: kv_step(j, c, qs_t, False), init)
            finish(i, kv_step(i, carry, qs_t, True))
            return 0

        lax.fori_loop(0, nb, q_block, 0)
        _ride_wait(rider, ride_refs, pl.program_id(0) == n_pairs - 1)

    in_specs = [_resident((s, qw), lambda p: (0, q_off + p)), _resident((s, qw), lambda p: (0, k_off + p)),
                _resident((s, LANES), lambda p: (0, v_off + p))]
    args = [q, k, v]
    if has_bias:
        in_specs.append(_resident((1, 2, nb, 1, t), lambda p: (p, 0, 0, 0, 0)))
        args.append(bias)
    out_specs = [pl.BlockSpec((s, LANES), lambda p: (0, p)), pl.BlockSpec((1, 2, s, 1), lambda p: (p, 0, 0, 0))]
    out_shape = [jax.ShapeDtypeStruct((s, n_pairs * LANES), F32), jax.ShapeDtypeStruct((n_pairs, 2, s, 1), F32)]
    scratch = [pltpu.VMEM((nb, LANES, t), BF16)] + ([pltpu.VMEM((2, s, 1), F32)] if has_bias else [])
    scratch += _add_rider(rider, in_specs, args, out_specs, out_shape)
    return pl.pallas_call(
        body, name=name, grid=(n_pairs,), in_specs=in_specs, out_specs=out_specs, out_shape=out_shape,
        scratch_shapes=scratch,
        compiler_params=_params(("parallel",) if rider is None else ("arbitrary",)),
    )(*args)


def _flash_bwd(q, k, v, do, o, lse, bias, *, n_pairs, hw, q_off, k_off, v_off, scale, qk_dtype, name, rider=None,
               stacked=False):
    s = q.shape[0]
    t = min(ATT_T, s)
    nb = s // t
    qw = 2 * hw
    has_bias = bias is not None
    c1 = scale * LOG2E

    def body(*refs):
        n_grads = 1 if stacked else 3
        refs, ride_refs = _split_rider(refs, rider, n_in=7 if has_bias else 6, n_out=n_grads + (2 if has_bias else 0))
        if stacked:
            refs = list(refs)
            n_in = 7 if has_bias else 6
            refs[n_in:n_in + 1] = [refs[n_in].at[0], refs[n_in].at[1], refs[n_in].at[2]]
        if has_bias:
            (q_ref, k_ref, v_ref, do_ref, o_ref, lse_ref, b_ref, dq_ref, dk_ref, dv_ref, db_ref, dr_ref,
             dkt_ref, dvt_ref) = refs
            db_ref[...] = jnp.zeros_like(db_ref)
        else:
            q_ref, k_ref, v_ref, do_ref, o_ref, lse_ref, dq_ref, dk_ref, dv_ref, dkt_ref, dvt_ref = refs
            b_ref = db_ref = dr_ref = None
        _ride_start(rider, ride_refs, pl.program_id(0) == 0)
        dkt_ref[...] = jnp.zeros_like(dkt_ref)
        dvt_ref[...] = jnp.zeros_like(dvt_ref)
        causal = lax.broadcasted_iota(jnp.int32, (t, t), 1) <= lax.broadcasted_iota(jnp.int32, (t, t), 0)
        cmask = jnp.concatenate([causal, causal], axis=0)
        lane_lt_hw = lax.broadcasted_iota(jnp.int32, (t, qw), 1) < hw

        def q_block(i, _):
            r0 = pl.multiple_of(i * t, t)
            qs = _stack_heads(q_ref[pl.ds(r0, t), :], hw)
            dos = _stack_heads(do_ref[pl.ds(r0, t), :], 64)
            ot = o_ref[pl.ds(r0, t), :]
            delta = jnp.sum(dos * jnp.concatenate([ot, ot], axis=0), axis=-1, keepdims=True)
            lse2 = jnp.concatenate([lse_ref[0, 0, pl.ds(r0, t), :], lse_ref[0, 1, pl.ds(r0, t), :]], axis=0)
            dosb = dos.astype(BF16)
            dos_t = dos.T.astype(BF16)
            qs_t = qs.astype(F32).T.astype(BF16)

            def kv_step(j, carry, masked):
                dq, rsum = carry
                c0 = pl.multiple_of(j * t, t)
                kt = k_ref[pl.ds(c0, t), :]
                vt = v_ref[pl.ds(c0, t), :]
                sc = lax.dot_general(qs, kt, _NT, preferred_element_type=F32) * c1
                if has_bias:
                    sc = sc + _stacked_rows(b_ref[0, 0, j], b_ref[0, 1, j], t)
                if masked:
                    sc = jnp.where(cmask, sc, NEG_INF)
                p = jnp.exp2(sc - lse2)
                dp = lax.dot_general(dosb, vt, _NT, preferred_element_type=F32)
                ds = p * (dp - delta)
                dsb = ds.astype(BF16)
                pb = p.astype(BF16)
                if hw == LANES:
                    dvt_ref[j] += jnp.concatenate(
                        [jnp.dot(dos_t[:64, :t], pb[:t], preferred_element_type=F32),
                         jnp.dot(dos_t[64:, t:], pb[t:], preferred_element_type=F32)], axis=0)
                    dkt_ref[j] += jnp.concatenate(
                        [jnp.dot(qs_t[:hw, :t], dsb[:t], preferred_element_type=F32),
                         jnp.dot(qs_t[hw:, t:], dsb[t:], preferred_element_type=F32)], axis=0)
                else:
                    dvt_ref[j] += jnp.dot(dos_t, pb, preferred_element_type=F32)
                    dkt_ref[j] += jnp.dot(qs_t, dsb, preferred_element_type=F32)
                if has_bias:
                    db_ref[0, 0, j] += jnp.sum(ds[:t], axis=0, keepdims=True)
                    db_ref[0, 1, j] += jnp.sum(ds[t:], axis=0, keepdims=True)
                    rsum = rsum + jnp.sum(ds, axis=-1, keepdims=True)
                return dq + jnp.dot(dsb, kt, preferred_element_type=F32), rsum

            init = (jnp.zeros((2 * t, qw), F32), jnp.zeros((2 * t, 1), F32))
            carry = lax.fori_loop(0, i, functools.partial(kv_step, masked=False), init)
            dq, rsum = kv_step(i, carry, True)
            dq = dq * scale
            dq_ref[pl.ds(r0, t), :] = jnp.where(lane_lt_hw, dq[:t], dq[t:]).astype(qk_dtype)
            if has_bias:
                rsum_row = jnp.broadcast_to(rsum, (2 * t, LANES)).T[0:1]
                dr_ref[0, 0, i] = rsum_row[:, :t]
                dr_ref[0, 1, i] = rsum_row[:, t:]
            return 0

        lax.fori_loop(0, nb, q_block, 0)

        def k_block(j, _):
            c0 = pl.multiple_of(j * t, t)
            dk_ref[pl.ds(c0, t), :] = (dkt_ref[j].T * scale).astype(qk_dtype)
            dv_ref[pl.ds(c0, t), :] = dvt_ref[j].T.astype(BF16)
            return 0

        lax.fori_loop(0, nb, k_block, 0)
        _ride_wait(rider, ride_refs, pl.program_id(0) == n_pairs - 1)

    in_specs = [_resident((s, qw), lambda p: (0, q_off + p)), _resident((s, qw), lambda p: (0, k_off + p)),
                _resident((s, LANES), lambda p: (0, v_off + p)),
                _resident((s, LANES), lambda p: (0, p)), _resident((s, LANES), lambda p: (0, p)),
                _resident((1, 2, s, 1), lambda p: (p, 0, 0, 0))]
    args = [q, k, v, do, o, lse]
    if stacked:
        assert qw == LANES and qk_dtype == BF16
        out_specs = [pl.BlockSpec((3, s, LANES), lambda p: (0, 0, p))]
        out_shape = [jax.ShapeDtypeStruct((3, s, n_pairs * LANES), BF16)]
    else:
        out_specs = [pl.BlockSpec((s, qw), lambda p: (0, p)), pl.BlockSpec((s, qw), lambda p: (0, p)),
                     pl.BlockSpec((s, LANES), lambda p: (0, p))]
        out_shape = [jax.ShapeDtypeStruct((s, n_pairs * qw), qk_dtype), jax.ShapeDtypeStruct((s, n_pairs * qw), qk_dtype),
                     jax.ShapeDtypeStruct((s, n_pairs * LANES), BF16)]
    if has_bias:
        in_specs.append(_resident((1, 2, nb, 1, t), lambda p: (p, 0, 0, 0, 0)))
        args.append(bias)
        for _ in range(2):
            out_specs.append(pl.BlockSpec((1, 2, nb, 1, t), lambda p: (p, 0, 0, 0, 0)))
            out_shape.append(jax.ShapeDtypeStruct((n_pairs, 2, nb, 1, t), F32))
    scratch = [pltpu.VMEM((nb, qw, t), F32), pltpu.VMEM((nb, LANES, t), F32)]
    scratch += _add_rider(rider, in_specs, args, out_specs, out_shape)
    return pl.pallas_call(
        body, name=name, grid=(n_pairs,), in_specs=in_specs, out_specs=out_specs, out_shape=out_shape,
        scratch_shapes=scratch,
        compiler_params=_params(("parallel",) if rider is None else ("arbitrary",)),
    )(*args)


def _alibi_slope(h):
    return 2.0 ** (-8.0 * (h + 1.0) / SWA_HEADS)


SWA_ROWS = 512
SWA_SCALE = SWA_DIM ** -0.5


def _swa_geometry(i):
    w = WINDOW
    r0 = pl.multiple_of(i * w, w)
    b0 = pl.multiple_of(jnp.maximum(i - 1, 0) * w, w)
    row = lax.broadcasted_iota(jnp.int32, (w, 2 * w), 0)
    col = lax.broadcasted_iota(jnp.int32, (w, 2 * w), 1)
    dist = row - col + (r0 - b0)
    valid = (dist >= 0) & (dist < w)
    return r0, b0, dist.astype(F32), valid


def _swa_q_head(qblk, h):
    kv = h // (SWA_HEADS // SWA_KV_HEADS)
    if h % 2 != kv:
        qblk = pltpu.roll(qblk, 64, axis=1)
    return jnp.where(_head_mask(qblk.shape, kv), qblk, 0.0)


SWA_GROUP = SWA_HEADS // SWA_KV_HEADS


def _swa_stack(ref, rs, grp):
    parts = []
    for a in range(SWA_GROUP):
        h = SWA_GROUP * grp + a
        parts.append(_swa_q_head(ref[rs, (h // 2) * LANES:(h // 2 + 1) * LANES].astype(F32), h))
    return jnp.concatenate(parts, axis=0)


def _swa_unstack(x, grp):
    tiles = []
    for a in range(SWA_GROUP):
        h = SWA_GROUP * grp + a
        tile = x[a * WINDOW:(a + 1) * WINDOW]
        tiles.append(pltpu.roll(tile, 64, axis=1) if h % 2 != grp else tile)
    return tiles


def _swa_head_column(vals):
    return jnp.concatenate([jnp.full((WINDOW, 1), v, F32) for v in vals], axis=0)


def _swa_logits(qs, kb, dist, valid, grp):
    slopes = _swa_head_column([_alibi_slope(SWA_GROUP * grp + a) for a in range(SWA_GROUP)])
    dist4 = jnp.concatenate([dist] * SWA_GROUP, axis=0)
    valid4 = jnp.concatenate([valid] * SWA_GROUP, axis=0)
    sc = lax.dot_general(qs, kb, _NT, preferred_element_type=F32) * SWA_SCALE - slopes * dist4
    return jnp.where(valid4, sc, NEG_INF)


def _swa_merge_heads(tiles):
    lt64 = lax.broadcasted_iota(jnp.int32, (WINDOW, LANES), 1) < 64
    return jnp.concatenate([jnp.where(lt64, tiles[2 * b], tiles[2 * b + 1]) for b in range(SWA_HEADS // 2)], axis=1)


def _swa_fwd(z0b, sinks, *, name):
    s = z0b.shape[0]
    w = WINDOW
    rows = min(SWA_ROWS, s)
    per_step = rows // w
    qcols = SWA_HEADS * SWA_DIM

    def body(sink_ref, q_ref, k_ref, v_ref, o_ref, lse_ref):
        g = pl.program_id(0)
        for ii in range(per_step):
            rs = slice(ii * w, (ii + 1) * w)
            r0, b0, dist, valid = _swa_geometry(g * per_step + ii)
            kb = k_ref[pl.ds(b0, 2 * w), :]
            vb = v_ref[pl.ds(b0, 2 * w), :]
            o_tiles = []
            for h in range(SWA_HEADS):
                kv = h // SWA_GROUP
                qh = _swa_q_head(q_ref[rs, (h // 2) * LANES:(h // 2 + 1) * LANES].astype(F32), h).astype(BF16)
                sc = lax.dot_general(qh, kb, _NT, preferred_element_type=F32) * SWA_SCALE - _alibi_slope(h) * dist
                sc = jnp.where(valid, sc, NEG_INF)
                sink = sink_ref[0, h]
                m = jnp.maximum(jnp.max(sc, axis=-1, keepdims=True), sink)
                p = jnp.exp(sc - m)
                l = jnp.sum(p, axis=-1, keepdims=True) + jnp.exp(sink - m)
                oh = jnp.dot(p.astype(BF16), vb, preferred_element_type=F32) / l
                o_tiles.append(pltpu.roll(oh, 64, axis=1) if h % 2 != kv else oh)
                lse_ref[h, rs, :] = m + jnp.log(l)
            o_ref[rs, :] = _swa_merge_heads(o_tiles)

    return pl.pallas_call(
        body, name=name, grid=(s // rows,),
        in_specs=[pl.BlockSpec(memory_space=pltpu.SMEM),
                  pl.BlockSpec((rows, qcols), lambda g: (g, 0)),
                  pl.BlockSpec((s, LANES), lambda g: (0, 4)), pl.BlockSpec((s, LANES), lambda g: (0, 5))],
        out_specs=[pl.BlockSpec((rows, qcols), lambda g: (g, 0)), pl.BlockSpec((SWA_HEADS, rows, 1), lambda g: (0, g, 0))],
        out_shape=[jax.ShapeDtypeStruct((s, qcols), F32), jax.ShapeDtypeStruct((SWA_HEADS, s, 1), F32)],
        compiler_params=_params(("parallel",)),
    )(sinks, z0b, z0b, z0b)


def _swa_bwd(z0b, sinks, do, o, lse, *, name):
    s = z0b.shape[0]
    w = WINDOW
    rows = min(SWA_ROWS, s)
    per_step = rows // w
    qcols = SWA_HEADS * SWA_DIM
    nblk = s // w

    def body(sink_ref, q_ref, k_ref, v_ref, do_ref, o_ref, lse_ref, dq_ref, dkt_ref, dvt_ref, dsink_ref):
        g = pl.program_id(0)

        @pl.when(g == 0)
        def _():
            dkt_ref[...] = jnp.zeros_like(dkt_ref)
            dvt_ref[...] = jnp.zeros_like(dvt_ref)
            dsink_ref[...] = jnp.zeros_like(dsink_ref)

        for ii in range(per_step):
            i = g * per_step + ii
            rs = slice(ii * w, (ii + 1) * w)
            r0, b0, dist, valid = _swa_geometry(i)
            j0 = jnp.maximum(i - 1, 0)
            kb = k_ref[pl.ds(b0, 2 * w), :]
            vb = v_ref[pl.ds(b0, 2 * w), :]
            dq_tiles = []
            for grp in range(SWA_KV_HEADS):
                heads = [SWA_GROUP * grp + a for a in range(SWA_GROUP)]
                qs32 = _swa_stack(q_ref, rs, grp)
                dos32 = _swa_stack(do_ref, rs, grp)
                delta = jnp.sum(dos32 * _swa_stack(o_ref, rs, grp), axis=-1, keepdims=True)
                lse = jnp.concatenate([lse_ref[h, rs, :] for h in heads], axis=0)
                sink = _swa_head_column([sink_ref[0, h] for h in heads])
                p = jnp.exp(_swa_logits(qs32.astype(BF16), kb, dist, valid, grp) - lse)
                dp = lax.dot_general(dos32.astype(BF16), vb, _NT, preferred_element_type=F32)
                ds = p * (dp - delta)
                dsb = ds.astype(BF16)
                d_sink = jnp.exp(sink - lse) * delta
                for a, h in enumerate(heads):
                    dsink_ref[h:h + 1, :] += jnp.broadcast_to(-jnp.sum(d_sink[a * w:(a + 1) * w]), (1, LANES))
                dvt = jnp.dot(dos32.T.astype(BF16), p.astype(BF16), preferred_element_type=F32)
                dkt = jnp.dot(qs32.T.astype(BF16), dsb, preferred_element_type=F32) * SWA_SCALE
                dvt_ref[j0] += dvt[:, :w]
                dvt_ref[j0 + 1] += dvt[:, w:]
                dkt_ref[j0] += dkt[:, :w]
                dkt_ref[j0 + 1] += dkt[:, w:]
                dq_tiles += _swa_unstack(jnp.dot(dsb, kb, preferred_element_type=F32) * SWA_SCALE, grp)
            dq_ref[rs, :] = _swa_merge_heads(dq_tiles)

    return pl.pallas_call(
        body, name=name, grid=(s // rows,),
        in_specs=[pl.BlockSpec(memory_space=pltpu.SMEM),
                  pl.BlockSpec((rows, qcols), lambda g: (g, 0)),
                  pl.BlockSpec((s, LANES), lambda g: (0, 4)), pl.BlockSpec((s, LANES), lambda g: (0, 5)),
                  pl.BlockSpec((rows, qcols), lambda g: (g, 0)), pl.BlockSpec((rows, qcols), lambda g: (g, 0)),
                  pl.BlockSpec((SWA_HEADS, rows, 1), lambda g: (0, g, 0))],
        out_specs=[pl.BlockSpec((rows, qcols), lambda g: (g, 0)),
                   pl.BlockSpec((nblk, LANES, w), lambda g: (0, 0, 0)),
                   pl.BlockSpec((nblk, LANES, w), lambda g: (0, 0, 0)),
                   pl.BlockSpec((SWA_HEADS, LANES), lambda g: (0, 0))],
        out_shape=[jax.ShapeDtypeStruct((s, qcols), F32),
                   jax.ShapeDtypeStruct((nblk, LANES, w), F32), jax.ShapeDtypeStruct((nblk, LANES, w), F32),
                   jax.ShapeDtypeStruct((SWA_HEADS, LANES), F32)],
        compiler_params=_params(("arbitrary",)),
    )(sinks, z0b, z0b, z0b, do, o, lse)


CUM_T = 256


def _split3(x):
    hi = x.astype(BF16)
    r1 = x - hi.astype(F32)
    mid = r1.astype(BF16)
    lo = (r1 - mid.astype(F32)).astype(BF16)
    return hi, mid, lo


def _tri_dot(tri, x):
    hi, mid, lo = _split3(x)
    out = jnp.dot(tri, hi, preferred_element_type=F32)
    out = out + jnp.dot(tri, mid, preferred_element_type=F32)
    return out + jnp.dot(tri, lo, preferred_element_type=F32)


def _logf_fwd(zf, bf, *, name):
    s = zf.shape[0]
    t = CUM_T
    nb = s // t

    def body(z_ref, b_ref, c_ref, carry_ref):
        i = pl.program_id(0)

        @pl.when(i == 0)
        def _():
            carry_ref[...] = jnp.zeros_like(carry_ref)

        x = z_ref[...] + b_ref[...]
        lf = jnp.minimum(x, 0.0) - jnp.log(1.0 + jnp.exp(-jnp.abs(x)))
        row = lax.broadcasted_iota(jnp.int32, (t, t), 0)
        col = lax.broadcasted_iota(jnp.int32, (t, t), 1)
        tri = jnp.where(col <= row, 1.0, 0.0).astype(BF16)
        c = _tri_dot(tri, lf) + carry_ref[...]
        c_ref[...] = c
        carry_ref[...] = c[t - 1:t, :]

    return pl.pallas_call(
        body, name=name, grid=(nb,),
        in_specs=[pl.BlockSpec((t, LANES), lambda i: (i, 0)), pl.BlockSpec((1, LANES), lambda i: (0, 0))],
        out_specs=pl.BlockSpec((t, LANES), lambda i: (i, 0)),
        out_shape=jax.ShapeDtypeStruct((s, LANES), F32),
        scratch_shapes=[pltpu.VMEM((1, LANES), F32)],
        compiler_params=_params(("arbitrary",)),
    )(zf, bf)


def _logf_bwd(dc, zf, bf, *, name):
    s = zf.shape[0]
    t = CUM_T
    nb = s // t

    def body(dc_ref, z_ref, b_ref, dz_ref, db_ref, carry_ref):
        i = pl.program_id(0)

        @pl.when(i == 0)
        def _():
            carry_ref[...] = jnp.zeros_like(carry_ref)
            db_ref[...] = jnp.zeros_like(db_ref)

        row = lax.broadcasted_iota(jnp.int32, (t, t), 0)
        col = lax.broadcasted_iota(jnp.int32, (t, t), 1)
        tri = jnp.where(col >= row, 1.0, 0.0).astype(BF16)
        dlf = _tri_dot(tri, dc_ref[...]) + carry_ref[...]
        carry_ref[...] = dlf[0:1, :]
        x = z_ref[...] + b_ref[...]
        dz = dlf * _sigmoid(-x)
        dz_ref[...] = dz.astype(BF16)
        db_ref[...] += jnp.sum(dz, axis=0, keepdims=True)

    return pl.pallas_call(
        body, name=name, grid=(nb,),
        in_specs=[pl.BlockSpec((t, LANES), lambda i: (nb - 1 - i, 0)), pl.BlockSpec((t, LANES), lambda i: (nb - 1 - i, 0)),
                  pl.BlockSpec((1, LANES), lambda i: (0, 0))],
        out_specs=[pl.BlockSpec((t, LANES), lambda i: (nb - 1 - i, 0)), pl.BlockSpec((1, LANES), lambda i: (0, 0))],
        out_shape=[jax.ShapeDtypeStruct((s, LANES), BF16), jax.ShapeDtypeStruct((1, LANES), F32)],
        scratch_shapes=[pltpu.VMEM((1, LANES), F32)],
        compiler_params=_params(("arbitrary",)),
    )(dc, zf, bf)


def _sum_pieces(p_ref):
    g = p_ref[0].astype(F32)
    for k in range(1, N_DEV):
        g = g + p_ref[k].astype(F32)
    return g


def _adam_update(g, w, m, v):
    bc1 = 1.0 - ADAM_B1 ** ADAM_STEP
    bc2 = 1.0 - ADAM_B2 ** ADAM_STEP
    nm = ADAM_B1 * m + (1.0 - ADAM_B1) * g
    nv = ADAM_B2 * v + (1.0 - ADAM_B2) * (g * g)
    m_hat = nm / bc1
    v_hat = nv / bc2
    return -ADAM_LR * (m_hat / (jnp.sqrt(v_hat) + ADAM_EPS) + ADAM_WD * w), nm, nv


def _adamw(pieces, w, m, v, *, name):
    rows, cols = w.shape
    tr = _tile(rows, (RB1, RB0, SMALL_ROWS))

    def body(p_ref, w_ref, m_ref, v_ref, g_ref, d_ref, nm_ref, nv_ref):
        g = _sum_pieces(p_ref)
        g_ref[...] = g
        d_ref[...], nm_ref[...], nv_ref[...] = _adam_update(g, w_ref[...], m_ref[...], v_ref[...])

    spec = pl.BlockSpec((tr, cols), lambda i: (i, 0))
    shape = jax.ShapeDtypeStruct((rows, cols), F32)
    return pl.pallas_call(
        body, name=name, grid=(rows // tr,),
        in_specs=[pl.BlockSpec((N_DEV, tr, cols), lambda i: (0, i, 0)), spec, spec, spec],
        out_specs=[spec, spec, spec, spec], out_shape=[shape, shape, shape, shape],
        compiler_params=_params(("parallel",)),
    )(pieces, w, m, v)


def _sum8(pieces, rows, *, name):
    cols = pieces.shape[2]
    tr = _tile(rows, (176, 96))

    def body(p_ref, g_ref):
        g_ref[...] = _sum_pieces(p_ref)

    return pl.pallas_call(
        body, name=name, grid=(rows // tr,),
        in_specs=[pl.BlockSpec((N_DEV, tr, cols), lambda i: (0, i, 0))],
        out_specs=pl.BlockSpec((tr, cols), lambda i: (i, 0)),
        out_shape=jax.ShapeDtypeStruct((rows, cols), F32),
        compiler_params=_params(("parallel",)),
    )(pieces)


def _adamw_native(g, w, m, v, *, name):
    rows, cols = w.shape
    tr = _tile(rows, (256, 128))

    def body(g_ref, w_ref, m_ref, v_ref, d_ref, nm_ref, nv_ref):
        d_ref[...], nm_ref[...], nv_ref[...] = _adam_update(g_ref[...], w_ref[...], m_ref[...], v_ref[...])

    spec = pl.BlockSpec((tr, cols), lambda i: (i, 0))
    shape = jax.ShapeDtypeStruct((rows, cols), F32)
    return pl.pallas_call(
        body, name=name, grid=(rows // tr,), in_specs=[spec, spec, spec, spec],
        out_specs=[spec, spec, spec], out_shape=[shape, shape, shape],
        compiler_params=_params(("parallel",)),
    )(g, w, m, v)


MESH = pl.DeviceIdType.MESH
ANY = pl.BlockSpec(memory_space=pl.ANY)


def _all_gather(shard, *, name):
    rows, lanes = shard.shape

    def body(x_ref, out_ref, send_sems, recv_sems, local_sem):
        x, y, c = lax.axis_index("x"), lax.axis_index("y"), lax.axis_index("c")
        me, sibling = (x, y, c), (x, y, 1 - c)
        chips = [(1 - x, y), (x, 1 - y), (1 - x, 1 - y)]

        def block(px, py, pc):
            return out_ref.at[4 * px + 2 * py + pc]

        def copy(k, blk, to, src=None):
            return pltpu.make_async_remote_copy(
                src_ref=block(*blk) if src is None else src, dst_ref=block(*blk),
                send_sem=send_sems.at[k], recv_sem=recv_sems.at[k], device_id=to, device_id_type=MESH)

        mine = pltpu.make_async_copy(x_ref, block(*me), local_sem)
        mine.start()
        first = [copy(0, me, sibling, src=x_ref)]
        first += [copy(1 + j, me, (*chip, c), src=x_ref) for j, chip in enumerate(chips)]
        for cp in first:
            cp.start()
        passed = [copy(4 + j, (*chip, c), sibling) for j, chip in enumerate(chips)]
        for j, chip in enumerate(chips):
            copy(1 + j, (*chip, c), me).wait_recv()
            passed[j].start()
        copy(0, sibling, me).wait_recv()
        for j, chip in enumerate(chips):
            copy(4 + j, (*chip, 1 - c), me).wait_recv()
        for cp in first + passed:
            cp.wait_send()
        mine.wait()

    return pl.pallas_call(
        body, name=name, out_shape=jax.ShapeDtypeStruct((N_DEV, rows, lanes), shard.dtype),
        in_specs=[ANY], out_specs=ANY,
        scratch_shapes=[pltpu.SemaphoreType.DMA((7,)), pltpu.SemaphoreType.DMA((7,)), pltpu.SemaphoreType.DMA(())],
    )(shard)


def _peer_copies(kind, src_ref, out_ref, send_sems, recv_sems, local_sem):
    x, y, c = lax.axis_index("x"), lax.axis_index("y"), lax.axis_index("c")
    me = 4 * x + 2 * y + c

    def src(idx):
        return src_ref.at[idx] if kind == "exchange" else src_ref

    mine = None if local_sem is None else pltpu.make_async_copy(src(me), out_ref.at[me], local_sem)
    copies = []
    for r in range(1, N_DEV):
        px = 1 - x if r & 4 else x
        py = 1 - y if r & 2 else y
        pc = 1 - c if r & 1 else c
        copies.append(pltpu.make_async_remote_copy(
            src_ref=src(4 * px + 2 * py + pc), dst_ref=out_ref.at[me],
            send_sem=send_sems.at[r - 1], recv_sem=recv_sems.at[r - 1],
            device_id=(px, py, pc), device_id_type=MESH))
    return mine, copies


PEER_SEMS = [pltpu.SemaphoreType.DMA((7,)), pltpu.SemaphoreType.DMA((7,)), pltpu.SemaphoreType.DMA(())]


HBM = pl.BlockSpec(memory_space=pltpu.HBM)
SEMAPHORES = pl.BlockSpec(memory_space=pltpu.SEMAPHORE)


def _peer_start(kind, arr, *, name):
    land = lax.empty((N_DEV,) + arr.shape[-2:], arr.dtype)

    def body(src_ref, land_ref, send_sems, recv_sems, src_thru, land_thru, token):
        _, copies = _peer_copies(kind, src_ref, land_ref, send_sems, recv_sems, None)
        for cp in copies:
            cp.start()
        token[...] = jnp.zeros_like(token)

    return pl.pallas_call(
        body, name=name,
        out_shape=(pltpu.SemaphoreType.DMA((N_DEV - 1,)), pltpu.SemaphoreType.DMA((N_DEV - 1,)),
                   pltpu.HBM(arr.shape, arr.dtype), pltpu.HBM(land.shape, land.dtype), jax.ShapeDtypeStruct((8, LANES), F32)),
        in_specs=(HBM, HBM), out_specs=(SEMAPHORES, SEMAPHORES, HBM, HBM, pl.BlockSpec(memory_space=pltpu.VMEM)),
        input_output_aliases={0: 2, 1: 3},
        compiler_params=pltpu.CompilerParams(has_side_effects=pltpu.SideEffectType.DATAFLOW_SIDE_EFFECTING),
    )(pltpu.with_memory_space_constraint(arr, pltpu.HBM), pltpu.with_memory_space_constraint(land, pltpu.HBM))


def _peer_wait(kind, send_sems, recv_sems, src_thru, land_thru, after, *, name):
    def body(src_ref, land_ref, send_sems, recv_sems, *_):
        _, copies = _peer_copies(kind, src_ref, land_ref, send_sems, recv_sems, None)
        for cp in copies:
            cp.wait_send()
            cp.wait_recv()

    return pl.pallas_call(
        body, name=name,
        out_shape=(pltpu.HBM(src_thru.shape, src_thru.dtype), pltpu.HBM(land_thru.shape, land_thru.dtype)),
        in_specs=(HBM, HBM, SEMAPHORES, SEMAPHORES) + (ANY,) * len(after), out_specs=(HBM, HBM),
        input_output_aliases={0: 0, 1: 1},
        compiler_params=pltpu.CompilerParams(has_side_effects=pltpu.SideEffectType.DATAFLOW_SIDE_EFFECTING),
    )(src_thru, land_thru, send_sems, recv_sems, *after)


def _add_rider(rider, in_specs, args, out_specs, out_shape):
    if rider is None:
        return []
    _, arr = rider
    in_specs.append(ANY)
    args.append(arr)
    out_specs.append(ANY)
    out_shape.append(jax.ShapeDtypeStruct((N_DEV,) + arr.shape[-2:], arr.dtype))
    return list(PEER_SEMS)


def _split_rider(refs, rider, n_in, n_out):
    if rider is None:
        return refs, None
    refs = list(refs)
    rin = refs.pop(n_in)
    rout = refs.pop(n_in + n_out)
    return refs[:-3], (rin, rout, *refs[-3:])


def _ride_start(rider, ride_refs, first):
    if rider is None:
        return

    @pl.when(first)
    def _():
        mine, copies = _peer_copies(rider[0], *ride_refs)
        mine.start()
        for cp in copies:
            cp.start()


def _ride_wait(rider, ride_refs, last):
    if rider is None:
        return

    @pl.when(last)
    def _():
        mine, copies = _peer_copies(rider[0], *ride_refs)
        for cp in copies:
            cp.wait()
        mine.wait()


def _gathered_cols(blocks, kdim):
    n = blocks.shape[1] * WIDE // kdim
    return blocks.reshape(N_DEV, kdim, n).transpose(1, 0, 2).reshape(kdim, N_DEV * n)


def _scatter_cols(dw):
    kdim, n8 = dw.shape
    n = n8 // N_DEV
    return dw.reshape(kdim, N_DEV, n).transpose(1, 0, 2).reshape(N_DEV, kdim * n // WIDE, WIDE)


def _pad_rows(a, rows):
    pad = [(0, 0)] * a.ndim
    pad[-2] = (0, rows - a.shape[-2])
    return jnp.pad(a, pad)


def _layer0_in_weight_t(wt):
    cq, ckv, kpe = wt[0:256], wt[256:384], wt[384:416]
    q_s, k_s, v_s, gate = wt[416:928], wt[928:1056], wt[1056:1184], wt[1184:2208]
    z = jnp.zeros((64, wt.shape[1]), wt.dtype)
    return jnp.concatenate([gate, cq, ckv, z, kpe, z[:32], q_s, k_s, v_s], axis=0)


def _layer0_in_grad_t(dwt):
    gate, cq, ckv, kpe = dwt[0:1024], dwt[1024:1280], dwt[1280:1408], dwt[1472:1504]
    q_s, k_s, v_s = dwt[1536:2048], dwt[2048:2176], dwt[2176:2304]
    return jnp.concatenate([cq, ckv, kpe, q_s, k_s, v_s, gate], axis=0)


def _layer1_in_weight_t(wt):
    main = jnp.concatenate([wt[:3 * D_MODEL], wt[3 * D_MODEL + FOX_HEADS:]], axis=0)
    return main, _pad_rows(wt[3 * D_MODEL:3 * D_MODEL + FOX_HEADS], LANES)


def _layer1_in_grad_t(d_blocks, d_wft):
    return jnp.concatenate([d_blocks[0], d_wft[:FOX_HEADS], d_blocks[1]], axis=0)


def _q_up_weight(w):
    return jnp.pad(w.reshape(MLA_Q_RANK, MLA_HEADS, 96), ((0, 0), (0, 0), (0, 32))).reshape(MLA_Q_RANK, MLA_HEADS * LANES)


def _q_up_grad(dwp):
    return dwp.reshape(MLA_Q_RANK, MLA_HEADS, LANES)[:, :, :96].reshape(MLA_Q_RANK, MLA_HEADS * 96)


def _kv_up_weight(w):
    w4 = w.reshape(MLA_KV_RANK, MLA_HEADS, 2, 64)
    kp = jnp.pad(w4[:, :, 0, :], ((0, 0), (0, 0), (0, 64))).reshape(MLA_KV_RANK, MLA_HEADS * LANES)
    vp = w4[:, :, 1, :].reshape(MLA_KV_RANK, MLA_HEADS * 64)
    return jnp.concatenate([kp, vp], axis=1)


def _kv_up_grad(dwp):
    dk = dwp[:, :MLA_HEADS * LANES].reshape(MLA_KV_RANK, MLA_HEADS, LANES)[:, :, :64]
    dv = dwp[:, MLA_HEADS * LANES:].reshape(MLA_KV_RANK, MLA_HEADS, 64)
    return jnp.stack([dk, dv], axis=2).reshape(MLA_KV_RANK, MLA_HEADS * LANES)


def _pad_lanes(a):
    return jnp.pad(a, ((0, 0), (0, LANES - a.shape[1])))


def _small_pack(g_in, g_final, g_q_a, g_kv_a, sinks, b_f, loss):
    rows = [g_in.reshape(8, LANES), g_final.reshape(8, LANES), g_q_a.reshape(2, LANES), g_kv_a.reshape(1, LANES),
            _pad_lanes(sinks.reshape(1, -1)), _pad_lanes(b_f.reshape(1, -1)), _pad_lanes(loss.reshape(1, 1)),
            jnp.zeros((2, LANES), F32)]
    return jnp.concatenate(rows, axis=0)


def _small_unpack(a):
    return (a[0:8].reshape(1, D_MODEL), a[8:16].reshape(D_MODEL), a[16:18].reshape(1, MLA_Q_RANK),
            a[18:19].reshape(1, MLA_KV_RANK), a[19:20, :SWA_HEADS], a[20:21, :FOX_HEADS], a[21, 0])


def _local_step(x, positions, target, e_g_in, early, e_g_q_a, e_g_kv_a, e_sinks,
                late, o_b_f, g_final, scatter1=None, scatter0=None):
    s = x.shape[0]
    mla_scale = (MLA_NOPE + MLA_ROPE) ** -0.5
    fox_scale = FOX_DIM ** -0.5
    n0a = Z0A_UNITS * LANES

    inv_freq = 1.0 / (ROPE_THETA ** (jnp.arange(0, MLA_ROPE, 2, dtype=F32) / MLA_ROPE))
    ang = positions.astype(F32)[:, None] * inv_freq
    cos, sin = jnp.cos(ang), jnp.sin(ang)
    ones, zeros = jnp.ones((s, 64), F32), jnp.zeros((s, 64), F32)
    cos_t = jnp.concatenate([ones, cos, cos, ones[:, :32]], axis=1)
    sin_t = jnp.concatenate([zeros, -sin, sin, zeros[:, :32]], axis=1)

    if len(early) == 3:
        h0 = _rmsnorm_fwd(x, e_g_in, width=D_MODEL, col_blk=0, name="l0_norm")
        w0t, wq, wkv = early
    else:
        pending, token, unpack, prep = early
        h0 = _rmsnorm_fwd(x, e_g_in, width=D_MODEL, col_blk=0, name="l0_norm", after=[token])
        w0t, wq, wkv = unpack(*_peer_wait("gather", *pending, after=[h0] + prep, name="weights0_wait"))
    z0a = _matmul(h0, w0t, tb=True, b_rows=(0, n0a), name="l0_in_a")
    z0b = _matmul(h0, w0t, tb=True, b_rows=(n0a, Z0B_UNITS * LANES), name="l0_in_b", out_dtype=BF16)
    cqn = _rmsnorm_fwd(z0a, e_g_q_a, width=MLA_Q_RANK, col_blk=4, name="l0_q_norm")
    ckvn = _rmsnorm_fwd(z0a, e_g_kv_a, width=MLA_KV_RANK, col_blk=10, name="l0_kv_norm")
    qp = _matmul(cqn, wq, name="l0_q_up")
    kvp = _matmul(ckvn, wkv, name="l0_kv_up", out_dtype=BF16)
    qm, km = _rope_fwd(qp, kvp, z0a, cos_t, sin_t, name="l0_rope")
    gathers = len(late) == 2
    res = _flash_fwd(qm, km, kvp, None, n_pairs=MLA_HEADS // 2, hw=LANES, q_off=0, k_off=0, v_off=MLA_HEADS,
                     scale=mla_scale, name="l0_mla_fwd", rider=("gather", late[0]) if gathers else None)
    o_mla, lse_mla = res[0], res[1]
    wo0, o_g_in, w1t, wft, wo1 = late[1](res[2]) if gathers else late
    o_swa, lse_swa = _swa_fwd(z0b, e_sinks, name="l0_swa_fwd")
    og0 = _gate_fwd([o_mla, o_swa], z0a, name="l0_gate")

    x1, h1 = _matmul_rows([(og0, wo0, False)], [(x, D_MODEL, 0)], [o_g_in], _residual_norm_epilogue,
                          [("rows", D_MODEL, F32), ("rows", D_MODEL, BF16)], name="l0_out")
    z1 = _matmul(h1, w1t, tb=True, b_rows=(0, 3 * D_MODEL), name="l1_in_qkv", out_dtype=BF16)
    gate1 = _matmul(h1, w1t, tb=True, b_rows=(3 * D_MODEL, D_MODEL), name="l1_in_gate")
    zf = _matmul(h1, wft, tb=True, name="l1_in_f")
    bf = _pad_lanes(o_b_f)
    log_cum = _logf_fwd(zf, bf, name="l1_logf")
    bias2 = (-LOG2E * log_cum[:, :FOX_HEADS]).T
    t_bwd = min(ATT_T, s)
    bias = bias2.reshape(FOX_HEADS // 2, 2, s // t_bwd, 1, t_bwd)
    t_fwd = _fwd_tile(s)
    o_fox, lse_fox = _flash_fwd(z1, z1, z1, bias2.reshape(FOX_HEADS // 2, 2, s // t_fwd, 1, t_fwd),
                                n_pairs=FOX_HEADS // 2, hw=64, q_off=0, k_off=8, v_off=16, scale=fox_scale,
                                name="l1_fox_fwd")
    og1 = _gate_fwd([o_fox], gate1, name="l1_gate")

    dx2, loss_part, d_g_final = _matmul_rows(
        [(og1, wo1, False)], [(x1, D_MODEL, 0), (target, D_MODEL, 0)], [g_final.reshape(1, D_MODEL)], _loss_epilogue,
        [("rows", D_MODEL, F32), ("sum", (8, LANES)), ("sum", (1, D_MODEL))], name="l1_out_loss")

    d_wo1 = _matmul(og1, dx2, ta=True, name="l1_out_dw")
    do_fox, d_gate1 = _matmul_rows([(dx2, wo1, True)], [(o_fox, D_MODEL, 0), (gate1, D_MODEL, 0)], [],
                                   _gate_bwd_epilogue([D_MODEL]), [("rows", D_MODEL, F32), ("rows", D_MODEL, BF16)],
                                   name="l1_out_dx")
    dqkv1, dbias, drow = _flash_bwd(z1, z1, z1, do_fox, o_fox, lse_fox, bias, n_pairs=FOX_HEADS // 2, hw=64, q_off=0,
                                    k_off=8, v_off=16, scale=fox_scale, qk_dtype=BF16, stacked=True, name="l1_fox_bwd")
    d_log_cum = (drow.reshape(FOX_HEADS, s) - dbias.reshape(FOX_HEADS, s)).T
    d_log_cum = jnp.pad(d_log_cum, ((0, 0), (0, LANES - FOX_HEADS)))
    d_zf, d_bf = _logf_bwd(d_log_cum, zf, bf, name="l1_logf_bwd")
    d_w1t = (_matmul(dqkv1, h1, ta=True, name="l1_in_dw_qkv"), _matmul(d_gate1, h1, ta=True, name="l1_in_dw_gate"))
    d_wft = _matmul(d_zf, h1, ta=True, name="l1_in_f_dw")
    dx1, d_o_g_in = _matmul_rows([(dqkv1, w1t, False, c * D_MODEL, c) for c in range(3)]
                                 + [(d_gate1, w1t, False, 3 * D_MODEL), (d_zf, wft, False)],
                                 [(x1, D_MODEL, 0), (dx2, D_MODEL, 0)],
                                 [o_g_in], _rms_bwd_epilogue, [("rows", D_MODEL, F32), ("sum", (1, D_MODEL))],
                                 name="l1_in_dx")

    d_wo0 = _matmul(og0, dx1, ta=True, name="l0_out_dw")
    half = D_MODEL // 2
    do_mla, do_swa, d_gate0 = _matmul_rows(
        [(dx1, wo0, True)], [(o_mla, half, 0), (o_swa, half, 0), (z0a, D_MODEL, 0)], [], _gate_bwd_epilogue([half, half]),
        [("rows", half, F32), ("rows", half, F32), ("rows", D_MODEL, BF16)], name="l0_out_dx")
    dq_s, dkt_s, dvt_s, d_sinks = _swa_bwd(z0b, e_sinks, do_swa, o_swa, lse_swa, name="l0_swa_bwd")
    dk_s = dkt_s.transpose(0, 2, 1).reshape(s, LANES)
    dv_s = dvt_s.transpose(0, 2, 1).reshape(s, LANES)
    rider = None
    if scatter1 is not None:
        rider = ("exchange", scatter1(dict(w1t=d_w1t, wft=d_wft, wo1=d_wo1, o_g_in=d_o_g_in, wo0=d_wo0)))
    res = _flash_bwd(qm, km, kvp, do_mla, o_mla, lse_mla, None, n_pairs=MLA_HEADS // 2, hw=LANES, q_off=0, k_off=0,
                     v_off=MLA_HEADS, scale=mla_scale, qk_dtype=F32, name="l0_mla_bwd", rider=rider)
    dqm, dkm, dvm = res[0], res[1], res[2]
    recv1 = res[3] if rider is not None else None
    d_qp, d_kvp, d_kpe = _rope_bwd(dqm, dkm, dvm, cos_t, sin_t, name="l0_rope_bwd")
    d_wq = _matmul(cqn, d_qp, ta=True, name="l0_q_up_dw")
    d_cqn = _matmul(d_qp, wq, tb=True, name="l0_q_up_dx")
    d_wkv = _matmul(ckvn, d_kvp, ta=True, name="l0_kv_up_dw")
    d_ckvn = _matmul(d_kvp, wkv, tb=True, name="l0_kv_up_dx")
    d_cq, d_g_q_a = _rmsnorm_bwd(z0a, e_g_q_a, d_cqn, width=MLA_Q_RANK, col_blk=4, name="l0_q_norm_bwd")
    d_ckv, d_g_kv_a = _rmsnorm_bwd(z0a, e_g_kv_a, d_ckvn, width=MLA_KV_RANK, col_blk=10, name="l0_kv_norm_bwd")
    dz0 = jnp.concatenate([d_gate0, d_cq, d_ckv, d_kpe, dq_s.astype(BF16), dk_s.astype(BF16), dv_s.astype(BF16)], axis=1)
    d_w0t = _matmul(dz0, h0, ta=True, name="l0_in_dw")
    pending0, after_start = None, []
    if scatter0 is not None:
        *pending0, token = _peer_start("exchange", scatter0(dict(w0t=d_w0t, wq=d_wq, wkv=d_wkv)), name="grads0_start")
        after_start = [token]
    grad_x, d_e_g_in = _matmul_rows(
        [(dz0, w0t, False)], [(x, D_MODEL, 0), (dx1, D_MODEL, 0)], [e_g_in] + after_start,
        lambda dy, xt, add, g, *_: _rms_bwd_epilogue(dy, xt, add, g),
        [("rows", D_MODEL, F32), ("sum", (1, D_MODEL))], name="l0_in_dx")

    return dict(pending0=pending0, recv1=recv1, loss=loss_part[0, 0], grad_x=grad_x, e_g_in=d_e_g_in, w0t=d_w0t, e_g_q_a=d_g_q_a, wq=d_wq,
                e_g_kv_a=d_g_kv_a, wkv=d_wkv, e_sinks=d_sinks[:, 0].reshape(1, SWA_HEADS), wo0=d_wo0,
                o_g_in=d_o_g_in, w1t=d_w1t, wft=d_wft, o_b_f=d_bf[:, :FOX_HEADS], wo1=d_wo1, g_final=d_g_final.reshape(D_MODEL))


def _wide(a, rows):
    flat = a.reshape(-1)
    return jnp.pad(flat, (0, rows * WIDE - flat.shape[0])).reshape(rows, WIDE)


def _rows_b0(w_q, w_kv):
    return jnp.concatenate([_wide(w_q, 32), _wide(w_kv, 16)], axis=0)


def _unflat_b0(f):
    return f[0:24].reshape(1, MLA_Q_RANK, 96), f[32:48].reshape(1, MLA_KV_RANK, 128)


def _rows_b1(o_w_out, e_w_out, g_in):
    return jnp.concatenate([o_w_out, e_w_out, _wide(g_in, 16)], axis=0)


def _unflat_b1(f):
    return f[0:128][None], f[128:256][None], f[256:257, :LANES]


def kernel(x, positions, e_g_in, e_w_in, e_g_q_a, e_w_q_up, e_g_kv_a, e_w_kv_up, e_sinks, e_w_out, o_g_in, o_w_in, o_b_f, o_w_out, g_final, loss_target, m_e_g_in, m_e_w_in, m_e_g_q_a, m_e_w_q_up, m_e_g_kv_a, m_e_w_kv_up, m_e_sinks, m_e_w_out, m_o_g_in, m_o_w_in, m_o_b_f, m_o_w_out, m_g_final, v_e_g_in, v_e_w_in, v_e_g_q_a, v_e_w_q_up, v_e_g_kv_a, v_e_w_kv_up, v_e_sinks, v_e_w_out, v_o_g_in, v_o_w_in, v_o_b_f, v_o_w_out, v_g_final):
    def bf(a):
        return a.astype(BF16)

    me = 4 * lax.axis_index("x") + 2 * lax.axis_index("y") + lax.axis_index("c")
    shard0 = jnp.concatenate([_pad_rows(bf(e_w_in[0]).T, RA0), _rows_b0(bf(e_w_q_up[0]), bf(e_w_kv_up[0]))], axis=0)
    *pending_w0, token_w0 = _peer_start("gather", shard0, name="weights0_start")

    def unpack0(sent, gath0):
        gath0 = lax.dynamic_update_slice_in_dim(gath0, sent[None], me, axis=0)
        w0t = _layer0_in_weight_t(gath0[:, :N_E_IN].reshape(N_DEV * N_E_IN, WIDE))
        wq = _q_up_weight(_gathered_cols(gath0[:, RA0:RA0 + 24], MLA_Q_RANK))
        wkv = _kv_up_weight(_gathered_cols(gath0[:, RA0 + 32:RA0 + 48], MLA_KV_RANK))
        return w0t, wq, wkv

    rows_b0 = [_rows_b0(q[0], kv[0]) for q, kv in ((e_w_q_up, e_w_kv_up), (m_e_w_q_up, m_e_w_kv_up), (v_e_w_q_up, v_e_w_kv_up))]
    rows_b1 = [_rows_b1(o[0], e[0], g) for o, e, g in ((o_w_out, e_w_out, o_g_in), (m_o_w_out, m_e_w_out, m_o_g_in),
                                                       (v_o_w_out, v_e_w_out, v_o_g_in))]

    g_bits = lax.bitcast_convert_type(o_g_in.reshape(LANES), BF16)
    shard1 = jnp.concatenate([_pad_rows(bf(o_w_in[0]).T, RA1), _rows_b1(bf(o_w_out[0]), bf(e_w_out[0]), g_bits)], axis=0)

    def unpack1(gath1):
        w1t, wft = _layer1_in_weight_t(gath1[:, :N_O_IN].reshape(N_DEV * N_O_IN, WIDE))
        wo1 = gath1[:, RA1:RA1 + 128].reshape(D_MODEL, D_MODEL)
        wo0 = gath1[:, RA1 + 128:RA1 + 256].reshape(D_MODEL, D_MODEL)
        bits = gath1[:, RA1 + 256, :2 * LANES].reshape(N_DEV, LANES, 2)
        return wo0, lax.bitcast_convert_type(bits, F32).reshape(1, D_MODEL), w1t, wft, wo1

    def scatter1(g):
        d_in_t = _layer1_in_grad_t(g["w1t"], g["wft"]).reshape(N_DEV, N_O_IN, WIDE)
        d_o_g = jnp.pad(g["o_g_in"].reshape(N_DEV, 1, LANES), ((0, 0), (0, 15), (0, WIDE - LANES)))
        return jnp.concatenate([_pad_rows(d_in_t, RA1), g["wo1"].reshape(N_DEV, 128, WIDE),
                                g["wo0"].reshape(N_DEV, 128, WIDE), d_o_g], axis=1).astype(BF16)

    def scatter0(g):
        return jnp.concatenate([
            _pad_rows(_layer0_in_grad_t(g["w0t"]).reshape(N_DEV, N_E_IN, WIDE), RA0),
            _pad_rows(_scatter_cols(_q_up_grad(g["wq"])), 32), _scatter_cols(_kv_up_grad(g["wkv"]))], axis=1).astype(BF16)

    gr = _local_step(x[0], positions[0], loss_target[0], e_g_in,
                     (pending_w0, token_w0, unpack0, [shard1] + rows_b0 + rows_b1), e_g_q_a, e_g_kv_a, e_sinks,
                     (shard1, unpack1), o_b_f, g_final, scatter1=scatter1, scatter0=scatter0)

    def in_projection(recv, ra, n, w, m, v, name):
        g = _sum8(recv, ra, name=name + "_grad_sum")[:n].T
        d, nm, nv = _adamw_native(g, w[0], m[0], v[0], name=name + "_adamw")
        return g[None], d[None], nm[None], nv[None]

    o_in = in_projection(gr["recv1"], RA1, N_O_IN, o_w_in, m_o_w_in, v_o_w_in, "o_w_in")
    b1 = _adamw(gr["recv1"][:, RA1:], *rows_b1, name="adamw_late")

    small = _small_pack(gr["e_g_in"], gr["g_final"], gr["e_g_q_a"], gr["e_g_kv_a"], gr["e_sinks"], gr["o_b_f"], gr["loss"])
    small_all = _all_gather(small, name="small_all_gather")
    zero = jnp.zeros((), F32)
    w_small = _small_pack(e_g_in, g_final, e_g_q_a, e_g_kv_a, e_sinks, o_b_f, zero)
    m_small = _small_pack(m_e_g_in, m_g_final, m_e_g_q_a, m_e_g_kv_a, m_e_sinks, m_o_b_f, zero)
    v_small = _small_pack(v_e_g_in, v_g_final, v_e_g_q_a, v_e_g_kv_a, v_e_sinks, v_o_b_f, zero)
    smalls = _adamw(small_all, w_small, m_small, v_small, name="adamw_replicated")
    g_sm, d_sm, m_sm, v_sm = [_small_unpack(a) for a in smalls]
    loss = g_sm[6]

    sent0, recv0 = _peer_wait("exchange", *gr["pending0"], after=[o_in[1], b1[1], smalls[1]], name="grads0_wait")
    own = lax.dynamic_slice_in_dim(sent0, me, 1, axis=0)
    recv0 = lax.dynamic_update_slice_in_dim(recv0, own, me, axis=0)
    e_in = in_projection(recv0, RA0, N_E_IN, e_w_in, m_e_w_in, v_e_w_in, "e_w_in")
    b0 = _adamw(recv0[:, RA0:], *rows_b0, name="adamw_early")

    def sharded(k):
        q_up, kv_up = _unflat_b0(b0[k])
        o_out, e_out, o_g = _unflat_b1(b1[k])
        return e_in[k], q_up, kv_up, e_out, o_in[k], o_out, o_g

    g_sh, d_sh, m_sh, v_sh = [sharded(k) for k in range(4)]

    def leaves(sh, sm):
        return (sm[0], sh[0], sm[2], sh[1], sm[3], sh[2], sm[4], sh[3], sh[6], sh[4], sm[5], sh[5], sm[1])

    return (loss, gr["grad_x"][None], *leaves(g_sh, g_sm), *leaves(d_sh, d_sm), *leaves(m_sh, m_sm), *leaves(v_sh, v_sm))
```

```python
import functools

import jax
import jax.numpy as jnp
from jax import lax
from jax.experimental import pallas as pl
from jax.experimental.pallas import tpu as pltpu

F32 = jnp.float32
BF16 = jnp.bfloat16
NEG_INF = float("-inf")

N_DEV = 8
LANES = 128
D_MODEL = 1024
EPS = 1e-6
ROPE_THETA = 10000.0
MLA_HEADS = 8
MLA_Q_RANK = 256
MLA_KV_RANK = 128
MLA_NOPE = 64
MLA_ROPE = 32
MLA_V = 64
SWA_HEADS = 8
SWA_KV_HEADS = 2
SWA_DIM = 64
WINDOW = 128
FOX_HEADS = 16
FOX_DIM = 64

ADAM_LR = 0.001
ADAM_B1 = 0.9
ADAM_B2 = 0.999
ADAM_EPS = 1e-08
ADAM_WD = 0.01
ADAM_STEP = 10

ATT_T = 512
ATT_T_FWD = 1024
VMEM_LIMIT = 56 * 1024 * 1024
MATMUL_B_BLOCK_BYTES = 8 * 1024 * 1024

Z0A_UNITS = 12
Z0B_UNITS = 6

WIDE = 1024
N_E_IN = 276
N_O_IN = 514
RA0 = 288
RB0 = 32 + 16
RA1 = 528
RB1 = 128 + 128 + 16
SMALL_ROWS = 24


def _tile(n, cands):
    for c in cands:
        if n % c == 0:
            return c
    raise ValueError(f"no tile for {n}")


ROW_TILES = (512, 256, 128)


def _params(sem, vmem=VMEM_LIMIT):
    return pltpu.CompilerParams(dimension_semantics=sem, vmem_limit_bytes=vmem)


def _matmul(a, b, *, name, ta=False, tb=False, out_dtype=F32, b_rows=None):
    if ta:
        kdim, m = a.shape[-2], a.shape[-1] * (a.shape[0] if a.ndim == 3 else 1)
    else:
        m, kdim = a.shape
    if tb:
        n, kb = b.shape
    else:
        kb, n = b.shape
    assert kdim == kb, (a.shape, b.shape)
    b_start = 0
    if b_rows is not None:
        assert tb
        b_start, n = b_rows
    tm = _tile(m, (512, 256, 128))
    tn = _tile(n, [c for c in (1024, 768, 512, 384, 256, 128)
                   if c * kdim * b.dtype.itemsize <= MATMUL_B_BLOCK_BYTES and b_start % c == 0])
    assert b_start % tn == 0, (b_start, tn)
    b_off = b_start // tn
    dims = (((0 if ta else 1,), (1 if tb else 0,)), ((), ()))

    def body(a_ref, b_ref, o_ref):
        r = lax.dot_general(a_ref[...].astype(BF16), b_ref[...].astype(BF16), dims, preferred_element_type=F32)
        o_ref[...] = r.astype(out_dtype)

    if a.ndim == 3:
        per = a.shape[2] // tm
        a_spec = pl.BlockSpec((None, kdim, tm), lambda i, j: (i // per, 0, i % per))
    else:
        a_spec = pl.BlockSpec((kdim, tm), lambda i, j: (0, i)) if ta else pl.BlockSpec((tm, kdim), lambda i, j: (i, 0))
    b_spec = pl.BlockSpec((tn, kdim), lambda i, j: (j + b_off, 0)) if tb else pl.BlockSpec((kdim, tn), lambda i, j: (0, j))
    return pl.pallas_call(
        body, name=name, grid=(m // tm, n // tn), in_specs=[a_spec, b_spec],
        out_specs=pl.BlockSpec((tm, tn), lambda i, j: (i, j)), out_shape=jax.ShapeDtypeStruct((m, n), out_dtype),
        compiler_params=_params(("parallel", "parallel")),
    )(a, b)


def _rmsnorm_fwd(x, g, *, width, col_blk, name, after=()):
    s = x.shape[0]
    tm = _tile(s, ROW_TILES)

    def body(x_ref, g_ref, *rest):
        y_ref = rest[-1]
        xf = x_ref[...].astype(F32)
        r = lax.rsqrt(jnp.mean(xf * xf, axis=-1, keepdims=True) + EPS)
        y_ref[...] = ((xf * r) * g_ref[...]).astype(BF16)

    return pl.pallas_call(
        body, name=name, grid=(s // tm,),
        in_specs=[pl.BlockSpec((tm, width), lambda i: (i, col_blk)), pl.BlockSpec((1, width), lambda i: (0, 0))]
        + [ANY] * len(after),
        out_specs=pl.BlockSpec((tm, width), lambda i: (i, 0)),
        out_shape=jax.ShapeDtypeStruct((s, width), BF16),
        compiler_params=_params(("parallel",)),
    )(x, g, *after)


def _rmsnorm_bwd(x, g, dy, *, width, col_blk, name):
    s = x.shape[0]
    tm = _tile(s, ROW_TILES)

    def body(x_ref, g_ref, dy_ref, dx_ref, dg_ref):
        @pl.when(pl.program_id(0) == 0)
        def _():
            dg_ref[...] = jnp.zeros_like(dg_ref)

        dx, dg = _rms_bwd_epilogue(dy_ref[...], x_ref[...], 0.0, g_ref[...])
        dg_ref[...] += dg
        dx_ref[...] = dx.astype(BF16)

    return pl.pallas_call(
        body, name=name, grid=(s // tm,),
        in_specs=[pl.BlockSpec((tm, width), lambda i: (i, col_blk)), pl.BlockSpec((1, width), lambda i: (0, 0)),
                  pl.BlockSpec((tm, width), lambda i: (i, 0))],
        out_specs=[pl.BlockSpec((tm, width), lambda i: (i, 0)), pl.BlockSpec((1, width), lambda i: (0, 0))],
        out_shape=[jax.ShapeDtypeStruct((s, width), BF16), jax.ShapeDtypeStruct((1, width), F32)],
        compiler_params=_params(("arbitrary",)),
    )(x, g, dy)


def _sigmoid(x):
    return 1.0 / (1.0 + jnp.exp(-x))


def _matmul_rows(terms, row_inputs, params, epilogue, outs, *, name, prologue=None):
    if prologue is not None:
        terms = [(None,) + tuple(terms[0])] + list(terms[1:])
    s = row_inputs[0][0].shape[0]
    tm = _tile(s, ROW_TILES)
    steps = s // tm
    n_r, n_p, n_o = len(row_inputs), len(params), len(outs)
    n_t = sum(1 if term[0] is None else 2 for term in terms)

    def body(*refs):
        t_refs, r_refs = list(refs[:n_t]), refs[n_t:n_t + n_r]
        p_refs, o_refs = refs[n_t + n_r:n_t + n_r + n_p], refs[n_t + n_r + n_p:]
        i = pl.program_id(0)
        rows, small = [r[...] for r in r_refs], [p[...] for p in p_refs]
        made = None if prologue is None else prologue(*rows, *small)
        acc = None
        for term in terms:
            a = made if term[0] is None else t_refs.pop(0)[...].astype(BF16)
            dims = (((1,), (1 if term[2] else 0,)), ((), ()))
            part = lax.dot_general(a, t_refs.pop(0)[...].astype(BF16), dims, preferred_element_type=F32)
            acc = part if acc is None else acc + part
        vals = epilogue(acc, *rows, *small) if prologue is None else epilogue(acc, *rows, *small, made)
        for ref, val, out in zip(o_refs, vals, outs):
            if out[0] == "rows":
                ref[...] = val.astype(ref.dtype)
            else:
                @pl.when(i == 0)
                def _(ref=ref):
                    ref[...] = jnp.zeros_like(ref)

                ref[...] += val

    in_specs, args = [], []
    for term in terms:
        a, b = term[0], term[1]
        if a is None:
            in_specs.append(_resident(b.shape, lambda i: (0, 0)))
            args.append(b)
            continue
        b_rows = b.shape[0] if term[2] or len(term) < 4 else a.shape[-1]
        b_blk = 0 if len(term) < 4 else term[3] // b_rows
        if len(term) == 5:
            a_spec = pl.BlockSpec((None, tm, a.shape[2]), lambda i, c=term[4]: (c, i, 0))
        else:
            a_spec = pl.BlockSpec((tm, a.shape[1]), lambda i: (i, 0))
        in_specs += [a_spec, _resident((b_rows, b.shape[1]), lambda i, b_blk=b_blk: (b_blk, 0))]
        args += [a, b]
    for arr, width, col_blk in row_inputs:
        in_specs.append(pl.BlockSpec((tm, width), lambda i, col_blk=col_blk: (i, col_blk)))
        args.append(arr)
    for p in params:
        in_specs.append(pl.BlockSpec(p.shape, lambda i: (0, 0)))
        args.append(p)
    out_specs, out_shape = [], []
    for out in outs:
        if out[0] == "rows":
            out_specs.append(pl.BlockSpec((tm, out[1]), lambda i: (i, 0)))
            out_shape.append(jax.ShapeDtypeStruct((s, out[1]), out[2]))
        else:
            out_specs.append(pl.BlockSpec(out[1], lambda i: (0, 0)))
            out_shape.append(jax.ShapeDtypeStruct(out[1], F32))
    return pl.pallas_call(
        body, name=name, grid=(steps,), in_specs=in_specs, out_specs=out_specs, out_shape=out_shape,
        compiler_params=_params(("arbitrary",)),
    )(*args)


def _rms_stats(x):
    r = lax.rsqrt(jnp.mean(x * x, axis=-1, keepdims=True) + EPS)
    return r, x * r


def _gated(o_parts, gate):
    o = o_parts[0] if len(o_parts) == 1 else jnp.concatenate(o_parts, axis=1)
    return (o * (gate * _sigmoid(gate))).astype(BF16)


def _residual_norm_epilogue(r, x, g):
    x1 = x + r
    _, xh = _rms_stats(x1)
    return x1, xh * g


def _rms_bwd_epilogue(dy, x, add, g):
    r, xh = _rms_stats(x)
    dxh = dy * g
    dx = r * (dxh - xh * jnp.mean(dxh * xh, axis=-1, keepdims=True)) + add
    return dx, jnp.sum(dy * xh, axis=0, keepdims=True)


def _loss_epilogue(r, x1, target, g):
    rs, xh = _rms_stats(x1 + r)
    err = xh * g - target
    loss = jnp.broadcast_to(0.5 * jnp.sum(jnp.mean(err * err, axis=-1, keepdims=True)), (8, LANES))
    dy = err * (1.0 / D_MODEL)
    dxh = dy * g
    dx = rs * (dxh - xh * jnp.mean(dxh * xh, axis=-1, keepdims=True))
    return dx, loss, jnp.sum(dy * xh, axis=0, keepdims=True)


def _gate_bwd_epilogue(widths):
    def epilogue(d, *rows):
        o_parts, gt = rows[:-1], rows[-1]
        o = o_parts[0] if len(o_parts) == 1 else jnp.concatenate(o_parts, axis=1)
        sg = _sigmoid(gt)
        do = d * (gt * sg)
        d_gate = d * o * (sg * (1.0 + gt * (1.0 - sg)))
        cuts = [sum(widths[:k]) for k in range(len(widths) + 1)]
        return tuple(do[:, cuts[k]:cuts[k + 1]] for k in range(len(widths))) + (d_gate,)

    return epilogue


def _rot_half(x):
    lane = lax.broadcasted_iota(jnp.int32, x.shape, 1)
    return jnp.where(lane < 80, pltpu.roll(x, LANES - 16, axis=1), pltpu.roll(x, 16, axis=1))


def _rot_half_t(g):
    lane = lax.broadcasted_iota(jnp.int32, g.shape, 1)
    lo = (lane >= MLA_NOPE) & (lane < MLA_NOPE + MLA_ROPE // 2)
    hi = (lane >= MLA_NOPE + MLA_ROPE // 2) & (lane < MLA_NOPE + MLA_ROPE)
    return jnp.where(lo, pltpu.roll(g, LANES - 16, axis=1), jnp.where(hi, pltpu.roll(g, 16, axis=1), 0.0))


def _rope_fwd(qp, kvp, z0a, cos_t, sin_t, *, name):
    s = qp.shape[0]
    tm = _tile(s, ROW_TILES)
    hw = MLA_HEADS * LANES

    def body(q_ref, k_ref, kpe_ref, c_ref, s_ref, qm_ref, km_ref):
        c = c_ref[...]
        sn = s_ref[...]
        kpe = kpe_ref[...]
        kpe_r = (kpe * c + _rot_half(kpe) * sn).astype(BF16)
        lane = lax.broadcasted_iota(jnp.int32, kpe.shape, 1)
        for h in range(MLA_HEADS):
            sl = slice(h * LANES, (h + 1) * LANES)
            qh = q_ref[:, sl]
            qm_ref[:, sl] = (qh * c + _rot_half(qh) * sn).astype(BF16)
            km_ref[:, sl] = jnp.where(lane < MLA_NOPE, k_ref[:, sl], kpe_r)

    return pl.pallas_call(
        body, name=name, grid=(s // tm,),
        in_specs=[pl.BlockSpec((tm, hw), lambda i: (i, 0)), pl.BlockSpec((tm, hw), lambda i: (i, 0)),
                  pl.BlockSpec((tm, LANES), lambda i: (i, 11)),
                  pl.BlockSpec((tm, LANES), lambda i: (i, 0)), pl.BlockSpec((tm, LANES), lambda i: (i, 0))],
        out_specs=[pl.BlockSpec((tm, hw), lambda i: (i, 0)), pl.BlockSpec((tm, hw), lambda i: (i, 0))],
        out_shape=[jax.ShapeDtypeStruct((s, hw), BF16), jax.ShapeDtypeStruct((s, hw), BF16)],
        compiler_params=_params(("parallel",)),
    )(qp, kvp, z0a, cos_t, sin_t)


def _rope_bwd(dqm, dkm, dvm, cos_t, sin_t, *, name):
    s = dqm.shape[0]
    tm = _tile(s, ROW_TILES)
    hw = MLA_HEADS * LANES
    vw = MLA_HEADS * MLA_V

    def body(dq_ref, dk_ref, dv_ref, c_ref, s_ref, dqp_ref, dkv_ref, dkpe_ref):
        c = c_ref[...]
        sn = s_ref[...]
        ksum = jnp.zeros((tm, LANES), F32)
        for h in range(MLA_HEADS):
            sl = slice(h * LANES, (h + 1) * LANES)
            dq = dq_ref[:, sl]
            dqp_ref[:, sl] = (dq * c + _rot_half_t(dq * sn)).astype(BF16)
            dk = dk_ref[:, sl]
            dkv_ref[:, sl] = dk.astype(BF16)
            ksum = ksum + dk
        dkv_ref[:, hw:] = dv_ref[...]
        lane = lax.broadcasted_iota(jnp.int32, ksum.shape, 1)
        dkpe = ksum * c + _rot_half_t(ksum * sn)
        dkpe_ref[...] = jnp.where((lane >= MLA_NOPE) & (lane < MLA_NOPE + MLA_ROPE), dkpe, 0.0).astype(BF16)

    return pl.pallas_call(
        body, name=name, grid=(s // tm,),
        in_specs=[pl.BlockSpec((tm, hw), lambda i: (i, 0)), pl.BlockSpec((tm, hw), lambda i: (i, 0)),
                  pl.BlockSpec((tm, vw), lambda i: (i, 0)),
                  pl.BlockSpec((tm, LANES), lambda i: (i, 0)), pl.BlockSpec((tm, LANES), lambda i: (i, 0))],
        out_specs=[pl.BlockSpec((tm, hw), lambda i: (i, 0)), pl.BlockSpec((tm, hw + vw), lambda i: (i, 0)),
                   pl.BlockSpec((tm, LANES), lambda i: (i, 0))],
        out_shape=[jax.ShapeDtypeStruct((s, hw), BF16), jax.ShapeDtypeStruct((s, hw + vw), BF16),
                   jax.ShapeDtypeStruct((s, LANES), BF16)],
        compiler_params=_params(("parallel",)),
    )(dqm, dkm, dvm, cos_t, sin_t)


def _head_mask(shape, a):
    lane = lax.broadcasted_iota(jnp.int32, shape, 1)
    return (lane >= 64 * a) & (lane < 64 * (a + 1))


_NT = (((1,), (1,)), ((), ()))
LOG2E = 1.4426950408889634


def _stack_heads(tile, hw):
    lane = lax.broadcasted_iota(jnp.int32, tile.shape, 1)
    z = jnp.zeros_like(tile)
    return jnp.concatenate([jnp.where(lane < hw, tile, z), jnp.where(lane >= hw, tile, z)], axis=0)


def _stacked_rows(r0, r1, t):
    n = r0.shape[-1]
    return jnp.concatenate([jnp.broadcast_to(r0, (t, n)), jnp.broadcast_to(r1, (t, n))], axis=0)


def _resident(block, index_map):
    return pl.BlockSpec(block, index_map, pipeline_mode=pl.Buffered(1))


def _fwd_tile(s):
    return ATT_T_FWD if s % ATT_T_FWD == 0 else min(ATT_T, s)


def _flash_fwd(q, k, v, bias, *, n_pairs, hw, q_off, k_off, v_off, scale, name, rider=None):
    s = q.shape[0]
    t = _fwd_tile(s)
    nb = s // t
    qw = 2 * hw
    has_bias = bias is not None
    c1 = scale * LOG2E

    def body(*refs):
        refs, ride_refs = _split_rider(refs, rider, n_in=4 if has_bias else 3, n_out=2)
        if has_bias:
            q_ref, k_ref, v_ref, b_ref, o_ref, lse_ref, vt_ref, bcol_ref = refs
        else:
            q_ref, k_ref, v_ref, o_ref, lse_ref, vt_ref = refs
            b_ref = bcol_ref = None
        _ride_start(rider, ride_refs, pl.program_id(0) == 0)
        row = lax.broadcasted_iota(jnp.int32, (t, t), 0)
        col = lax.broadcasted_iota(jnp.int32, (t, t), 1)
        cmask_t = jnp.concatenate([row <= col, row <= col], axis=1)
        lane_lt64 = lax.broadcasted_iota(jnp.int32, (t, LANES), 1) < 64

        def as_column(r):
            return jnp.broadcast_to(r, (8, r.shape[1])).T[:, 0:1]

        def v_block(j, _):
            c0 = pl.multiple_of(j * t, t)
            vt_ref[j] = v_ref[pl.ds(c0, t), :].astype(F32).T.astype(BF16)
            if has_bias:
                for a in range(2):
                    bcol_ref[a, pl.ds(c0, t), :] = as_column(b_ref[0, a, j])
            return 0

        lax.fori_loop(0, nb, v_block, 0)

        def stacked_queries(i):
            return _stack_heads(q_ref[pl.ds(pl.multiple_of(i * t, t), t), :], hw).astype(F32).T.astype(BF16)

        def kv_step(j, carry, qs_t, masked):
            m, l, acc = carry
            rows = pl.ds(pl.multiple_of(j * t, t), t)
            sc = jnp.dot(k_ref[rows, :], qs_t, preferred_element_type=F32) * c1
            if has_bias:
                sc = sc + jnp.concatenate([jnp.broadcast_to(bcol_ref[0, rows, :], (t, t)),
                                           jnp.broadcast_to(bcol_ref[1, rows, :], (t, t))], axis=1)
            if masked:
                sc = jnp.where(cmask_t, sc, NEG_INF)
            m_new = jnp.maximum(m, jnp.max(sc, axis=0, keepdims=True))
            alpha = jnp.exp2(m - m_new)
            p = jnp.exp2(sc - m_new)
            l_new = alpha * l + jnp.sum(p, axis=0, keepdims=True)
            pv = jnp.dot(vt_ref[j], p.astype(BF16), preferred_element_type=F32)
            return m_new, l_new, alpha * acc + pv

        def finish(i, carry):
            m, l, acc = carry
            r0 = pl.multiple_of(i * t, t)
            out = (acc / l).T
            lse2 = as_column(m + jnp.log2(l))
            lse_ref[0, 0, pl.ds(r0, t), :] = lse2[:t]
            lse_ref[0, 1, pl.ds(r0, t), :] = lse2[t:]
            o_ref[pl.ds(r0, t), :] = jnp.where(lane_lt64, out[:t], out[t:])

        init = (jnp.full((1, 2 * t), NEG_INF, F32), jnp.zeros((1, 2 * t), F32), jnp.zeros((LANES, 2 * t), F32))

        def q_block(i, _):
            qs_t = stacked_queries(i)
            carry = lax.fori_loop(0, i, lambda j, c: kv_step(j, c, qs_t, False), init)
            finish(i, kv_step(i, carry, qs_t, True))
            return 0

        lax.fori_loop(0, nb, q_block, 0)
        _ride_wait(rider, ride_refs, pl.program_id(0) == n_pairs - 1)

    in_specs = [_resident((s, qw), lambda p: (0, q_off + p)), _resident((s, qw), lambda p: (0, k_off + p)),
                _resident((s, LANES), lambda p: (0, v_off + p))]
    args = [q, k, v]
    if has_bias:
        in_specs.append(_resident((1, 2, nb, 1, t), lambda p: (p, 0, 0, 0, 0)))
        args.append(bias)
    out_specs = [pl.BlockSpec((s, LANES), lambda p: (0, p)), pl.BlockSpec((1, 2, s, 1), lambda p: (p, 0, 0, 0))]
    out_shape = [jax.ShapeDtypeStruct((s, n_pairs * LANES), F32), jax.ShapeDtypeStruct((n_pairs, 2, s, 1), F32)]
    scratch = [pltpu.VMEM((nb, LANES, t), BF16)] + ([pltpu.VMEM((2, s, 1), F32)] if has_bias else [])
    scratch += _add_rider(rider, in_specs, args, out_specs, out_shape)
    return pl.pallas_call(
        body, name=name, grid=(n_pairs,), in_specs=in_specs, out_specs=out_specs, out_shape=out_shape,
        scratch_shapes=scratch,
        compiler_params=_params(("parallel",) if rider is None else ("arbitrary",)),
    )(*args)


def _flash_bwd(q, k, v, do, o, lse, bias, *, n_pairs, hw, q_off, k_off, v_off, scale, qk_dtype, name, rider=None,
               stacked=False):
    s = q.shape[0]
    t = min(ATT_T, s)
    nb = s // t
    qw = 2 * hw
    has_bias = bias is not None
    c1 = scale * LOG2E

    def body(*refs):
        n_grads = 1 if stacked else 3
        refs, ride_refs = _split_rider(refs, rider, n_in=7 if has_bias else 6, n_out=n_grads + (2 if has_bias else 0))
        if stacked:
            refs = list(refs)
            n_in = 7 if has_bias else 6
            refs[n_in:n_in + 1] = [refs[n_in].at[0], refs[n_in].at[1], refs[n_in].at[2]]
        if has_bias:
            (q_ref, k_ref, v_ref, do_ref, o_ref, lse_ref, b_ref, dq_ref, dk_ref, dv_ref, db_ref, dr_ref,
             dkt_ref, dvt_ref) = refs
            db_ref[...] = jnp.zeros_like(db_ref)
        else:
            q_ref, k_ref, v_ref, do_ref, o_ref, lse_ref, dq_ref, dk_ref, dv_ref, dkt_ref, dvt_ref = refs
            b_ref = db_ref = dr_ref = None
        _ride_start(rider, ride_refs, pl.program_id(0) == 0)
        dkt_ref[...] = jnp.zeros_like(dkt_ref)
        dvt_ref[...] = jnp.zeros_like(dvt_ref)
        causal = lax.broadcasted_iota(jnp.int32, (t, t), 1) <= lax.broadcasted_iota(jnp.int32, (t, t), 0)
        cmask = jnp.concatenate([causal, causal], axis=0)
        lane_lt_hw = lax.broadcasted_iota(jnp.int32, (t, qw), 1) < hw

        def q_block(i, _):
            r0 = pl.multiple_of(i * t, t)
            qs = _stack_heads(q_ref[pl.ds(r0, t), :], hw)
            dos = _stack_heads(do_ref[pl.ds(r0, t), :], 64)
            ot = o_ref[pl.ds(r0, t), :]
            delta = jnp.sum(dos * jnp.concatenate([ot, ot], axis=0), axis=-1, keepdims=True)
            lse2 = jnp.concatenate([lse_ref[0, 0, pl.ds(r0, t), :], lse_ref[0, 1, pl.ds(r0, t), :]], axis=0)
            dosb = dos.astype(BF16)
            dos_t = dos.T.astype(BF16)
            qs_t = qs.astype(F32).T.astype(BF16)

            def kv_step(j, carry, masked):
                dq, rsum = carry
                c0 = pl.multiple_of(j * t, t)
                kt = k_ref[pl.ds(c0, t), :]
                vt = v_ref[pl.ds(c0, t), :]
                sc = lax.dot_general(qs, kt, _NT, preferred_element_type=F32) * c1
                if has_bias:
                    sc = sc + _stacked_rows(b_ref[0, 0, j], b_ref[0, 1, j], t)
                if masked:
                    sc = jnp.where(cmask, sc, NEG_INF)
                p = jnp.exp2(sc - lse2)
                dp = lax.dot_general(dosb, vt, _NT, preferred_element_type=F32)
                ds = p * (dp - delta)
                dsb = ds.astype(BF16)
                pb = p.astype(BF16)
                if hw == LANES:
                    dvt_ref[j] += jnp.concatenate(
                        [jnp.dot(dos_t[:64, :t], pb[:t], preferred_element_type=F32),
                         jnp.dot(dos_t[64:, t:], pb[t:], preferred_element_type=F32)], axis=0)
                    dkt_ref[j] += jnp.concatenate(
                        [jnp.dot(qs_t[:hw, :t], dsb[:t], preferred_element_type=F32),
                         jnp.dot(qs_t[hw:, t:], dsb[t:], preferred_element_type=F32)], axis=0)
                else:
                    dvt_ref[j] += jnp.dot(dos_t, pb, preferred_element_type=F32)
                    dkt_ref[j] += jnp.dot(qs_t, dsb, preferred_element_type=F32)
                if has_bias:
                    db_ref[0, 0, j] += jnp.sum(ds[:t], axis=0, keepdims=True)
                    db_ref[0, 1, j] += jnp.sum(ds[t:], axis=0, keepdims=True)
                    rsum = rsum + jnp.sum(ds, axis=-1, keepdims=True)
                return dq + jnp.dot(dsb, kt, preferred_element_type=F32), rsum

            init = (jnp.zeros((2 * t, qw), F32), jnp.zeros((2 * t, 1), F32))
            carry = lax.fori_loop(0, i, functools.partial(kv_step, masked=False), init)
            dq, rsum = kv_step(i, carry, True)
            dq = dq * scale
            dq_ref[pl.ds(r0, t), :] = jnp.where(lane_lt_hw, dq[:t], dq[t:]).astype(qk_dtype)
            if has_bias:
                rsum_row = jnp.broadcast_to(rsum, (2 * t, LANES)).T[0:1]
                dr_ref[0, 0, i] = rsum_row[:, :t]
                dr_ref[0, 1, i] = rsum_row[:, t:]
            return 0

        lax.fori_loop(0, nb, q_block, 0)

        def k_block(j, _):
            c0 = pl.multiple_of(j * t, t)
            dk_ref[pl.ds(c0, t), :] = (dkt_ref[j].T * scale).astype(qk_dtype)
            dv_ref[pl.ds(c0, t), :] = dvt_ref[j].T.astype(BF16)
            return 0

        lax.fori_loop(0, nb, k_block, 0)
        _ride_wait(rider, ride_refs, pl.program_id(0) == n_pairs - 1)

    in_specs = [_resident((s, qw), lambda p: (0, q_off + p)), _resident((s, qw), lambda p: (0, k_off + p)),
                _resident((s, LANES), lambda p: (0, v_off + p)),
                _resident((s, LANES), lambda p: (0, p)), _resident((s, LANES), lambda p: (0, p)),
                _resident((1, 2, s, 1), lambda p: (p, 0, 0, 0))]
    args = [q, k, v, do, o, lse]
    if stacked:
        assert qw == LANES and qk_dtype == BF16
        out_specs = [pl.BlockSpec((3, s, LANES), lambda p: (0, 0, p))]
        out_shape = [jax.ShapeDtypeStruct((3, s, n_pairs * LANES), BF16)]
    else:
        out_specs = [pl.BlockSpec((s, qw), lambda p: (0, p)), pl.BlockSpec((s, qw), lambda p: (0, p)),
                     pl.BlockSpec((s, LANES), lambda p: (0, p))]
        out_shape = [jax.ShapeDtypeStruct((s, n_pairs * qw), qk_dtype), jax.ShapeDtypeStruct((s, n_pairs * qw), qk_dtype),
                     jax.ShapeDtypeStruct((s, n_pairs * LANES), BF16)]
    if has_bias:
        in_specs.append(_resident((1, 2, nb, 1, t), lambda p: (p, 0, 0, 0, 0)))
        args.append(bias)
        for _ in range(2):
            out_specs.append(pl.BlockSpec((1, 2, nb, 1, t), lambda p: (p, 0, 0, 0, 0)))
            out_shape.append(jax.ShapeDtypeStruct((n_pairs, 2, nb, 1, t), F32))
    scratch = [pltpu.VMEM((nb, qw, t), F32), pltpu.VMEM((nb, LANES, t), F32)]
    scratch += _add_rider(rider, in_specs, args, out_specs, out_shape)
    return pl.pallas_call(
        body, name=name, grid=(n_pairs,), in_specs=in_specs, out_specs=out_specs, out_shape=out_shape,
        scratch_shapes=scratch,
        compiler_params=_params(("parallel",) if rider is None else ("arbitrary",)),
    )(*args)


def _alibi_slope(h):
    return 2.0 ** (-8.0 * (h + 1.0) / SWA_HEADS)


SWA_ROWS = 512
SWA_SCALE = SWA_DIM ** -0.5


def _swa_geometry(i):
    w = WINDOW
    r0 = pl.multiple_of(i * w, w)
    b0 = pl.multiple_of(jnp.maximum(i - 1, 0) * w, w)
    row = lax.broadcasted_iota(jnp.int32, (w, 2 * w), 0)
    col = lax.broadcasted_iota(jnp.int32, (w, 2 * w), 1)
    dist = row - col + (r0 - b0)
    valid = (dist >= 0) & (dist < w)
    return r0, b0, dist.astype(F32), valid


def _swa_q_head(qblk, h):
    kv = h // (SWA_HEADS // SWA_KV_HEADS)
    if h % 2 != kv:
        qblk = pltpu.roll(qblk, 64, axis=1)
    return jnp.where(_head_mask(qblk.shape, kv), qblk, 0.0)


SWA_GROUP = SWA_HEADS // SWA_KV_HEADS


def _swa_stack(ref, rs, grp):
    parts = []
    for a in range(SWA_GROUP):
        h = SWA_GROUP * grp + a
        parts.append(_swa_q_head(ref[rs, (h // 2) * LANES:(h // 2 + 1) * LANES].astype(F32), h))
    return jnp.concatenate(parts, axis=0)


def _swa_unstack(x, grp):
    tiles = []
    for a in range(SWA_GROUP):
        h = SWA_GROUP * grp + a
        tile = x[a * WINDOW:(a + 1) * WINDOW]
        tiles.append(pltpu.roll(tile, 64, axis=1) if h % 2 != grp else tile)
    return tiles


def _swa_head_column(vals):
    return jnp.concatenate([jnp.full((WINDOW, 1), v, F32) for v in vals], axis=0)


def _swa_logits(qs, kb, dist, valid, grp):
    slopes = _swa_head_column([_alibi_slope(SWA_GROUP * grp + a) for a in range(SWA_GROUP)])
    dist4 = jnp.concatenate([dist] * SWA_GROUP, axis=0)
    valid4 = jnp.concatenate([valid] * SWA_GROUP, axis=0)
    sc = lax.dot_general(qs, kb, _NT, preferred_element_type=F32) * SWA_SCALE - slopes * dist4
    return jnp.where(valid4, sc, NEG_INF)


def _swa_merge_heads(tiles):
    lt64 = lax.broadcasted_iota(jnp.int32, (WINDOW, LANES), 1) < 64
    return jnp.concatenate([jnp.where(lt64, tiles[2 * b], tiles[2 * b + 1]) for b in range(SWA_HEADS // 2)], axis=1)


def _swa_fwd(z0b, sinks, *, name):
    s = z0b.shape[0]
    w = WINDOW
    rows = min(SWA_ROWS, s)
    per_step = rows // w
    qcols = SWA_HEADS * SWA_DIM

    def body(sink_ref, q_ref, k_ref, v_ref, o_ref, lse_ref):
        g = pl.program_id(0)
        for ii in range(per_step):
            rs = slice(ii * w, (ii + 1) * w)
            r0, b0, dist, valid = _swa_geometry(g * per_step + ii)
            kb = k_ref[pl.ds(b0, 2 * w), :]
            vb = v_ref[pl.ds(b0, 2 * w), :]
            o_tiles = []
            for h in range(SWA_HEADS):
                kv = h // SWA_GROUP
                qh = _swa_q_head(q_ref[rs, (h // 2) * LANES:(h // 2 + 1) * LANES].astype(F32), h).astype(BF16)
                sc = lax.dot_general(qh, kb, _NT, preferred_element_type=F32) * SWA_SCALE - _alibi_slope(h) * dist
                sc = jnp.where(valid, sc, NEG_INF)
                sink = sink_ref[0, h]
                m = jnp.maximum(jnp.max(sc, axis=-1, keepdims=True), sink)
                p = jnp.exp(sc - m)
                l = jnp.sum(p, axis=-1, keepdims=True) + jnp.exp(sink - m)
                oh = jnp.dot(p.astype(BF16), vb, preferred_element_type=F32) / l
                o_tiles.append(pltpu.roll(oh, 64, axis=1) if h % 2 != kv else oh)
                lse_ref[h, rs, :] = m + jnp.log(l)
            o_ref[rs, :] = _swa_merge_heads(o_tiles)

    return pl.pallas_call(
        body, name=name, grid=(s // rows,),
        in_specs=[pl.BlockSpec(memory_space=pltpu.SMEM),
                  pl.BlockSpec((rows, qcols), lambda g: (g, 0)),
                  pl.BlockSpec((s, LANES), lambda g: (0, 4)), pl.BlockSpec((s, LANES), lambda g: (0, 5))],
        out_specs=[pl.BlockSpec((rows, qcols), lambda g: (g, 0)), pl.BlockSpec((SWA_HEADS, rows, 1), lambda g: (0, g, 0))],
        out_shape=[jax.ShapeDtypeStruct((s, qcols), F32), jax.ShapeDtypeStruct((SWA_HEADS, s, 1), F32)],
        compiler_params=_params(("parallel",)),
    )(sinks, z0b, z0b, z0b)


def _swa_bwd(z0b, sinks, do, o, lse, *, name):
    s = z0b.shape[0]
    w = WINDOW
    rows = min(SWA_ROWS, s)
    per_step = rows // w
    qcols = SWA_HEADS * SWA_DIM
    nblk = s // w

    def body(sink_ref, q_ref, k_ref, v_ref, do_ref, o_ref, lse_ref, dq_ref, dkt_ref, dvt_ref, dsink_ref):
        g = pl.program_id(0)

        @pl.when(g == 0)
        def _():
            dkt_ref[...] = jnp.zeros_like(dkt_ref)
            dvt_ref[...] = jnp.zeros_like(dvt_ref)
            dsink_ref[...] = jnp.zeros_like(dsink_ref)

        for ii in range(per_step):
            i = g * per_step + ii
            rs = slice(ii * w, (ii + 1) * w)
            r0, b0, dist, valid = _swa_geometry(i)
            j0 = jnp.maximum(i - 1, 0)
            kb = k_ref[pl.ds(b0, 2 * w), :]
            vb = v_ref[pl.ds(b0, 2 * w), :]
            dq_tiles = []
            for grp in range(SWA_KV_HEADS):
                heads = [SWA_GROUP * grp + a for a in range(SWA_GROUP)]
                qs32 = _swa_stack(q_ref, rs, grp)
                dos32 = _swa_stack(do_ref, rs, grp)
                delta = jnp.sum(dos32 * _swa_stack(o_ref, rs, grp), axis=-1, keepdims=True)
                lse = jnp.concatenate([lse_ref[h, rs, :] for h in heads], axis=0)
                sink = _swa_head_column([sink_ref[0, h] for h in heads])
                p = jnp.exp(_swa_logits(qs32.astype(BF16), kb, dist, valid, grp) - lse)
                dp = lax.dot_general(dos32.astype(BF16), vb, _NT, preferred_element_type=F32)
                ds = p * (dp - delta)
                dsb = ds.astype(BF16)
                d_sink = jnp.exp(sink - lse) * delta
                for a, h in enumerate(heads):
                    dsink_ref[h:h + 1, :] += jnp.broadcast_to(-jnp.sum(d_sink[a * w:(a + 1) * w]), (1, LANES))
                dvt = jnp.dot(dos32.T.astype(BF16), p.astype(BF16), preferred_element_type=F32)
                dkt = jnp.dot(qs32.T.astype(BF16), dsb, preferred_element_type=F32) * SWA_SCALE
                dvt_ref[j0] += dvt[:, :w]
                dvt_ref[j0 + 1] += dvt[:, w:]
                dkt_ref[j0] += dkt[:, :w]
                dkt_ref[j0 + 1] += dkt[:, w:]
                dq_tiles += _swa_unstack(jnp.dot(dsb, kb, preferred_element_type=F32) * SWA_SCALE, grp)
            dq_ref[rs, :] = _swa_merge_heads(dq_tiles)

    return pl.pallas_call(
        body, name=name, grid=(s // rows,),
        in_specs=[pl.BlockSpec(memory_space=pltpu.SMEM),
                  pl.BlockSpec((rows, qcols), lambda g: (g, 0)),
                  pl.BlockSpec((s, LANES), lambda g: (0, 4)), pl.BlockSpec((s, LANES), lambda g: (0, 5)),
                  pl.BlockSpec((rows, qcols), lambda g: (g, 0)), pl.BlockSpec((rows, qcols), lambda g: (g, 0)),
                  pl.BlockSpec((SWA_HEADS, rows, 1), lambda g: (0, g, 0))],
        out_specs=[pl.BlockSpec((rows, qcols), lambda g: (g, 0)),
                   pl.BlockSpec((nblk, LANES, w), lambda g: (0, 0, 0)),
                   pl.BlockSpec((nblk, LANES, w), lambda g: (0, 0, 0)),
                   pl.BlockSpec((SWA_HEADS, LANES), lambda g: (0, 0))],
        out_shape=[jax.ShapeDtypeStruct((s, qcols), F32),
                   jax.ShapeDtypeStruct((nblk, LANES, w), F32), jax.ShapeDtypeStruct((nblk, LANES, w), F32),
                   jax.ShapeDtypeStruct((SWA_HEADS, LANES), F32)],
        compiler_params=_params(("arbitrary",)),
    )(sinks, z0b, z0b, z0b, do, o, lse)


CUM_T = 256


def _split3(x):
    hi = x.astype(BF16)
    r1 = x - hi.astype(F32)
    mid = r1.astype(BF16)
    lo = (r1 - mid.astype(F32)).astype(BF16)
    return hi, mid, lo


def _tri_dot(tri, x):
    hi, mid, lo = _split3(x)
    out = jnp.dot(tri, hi, preferred_element_type=F32)
    out = out + jnp.dot(tri, mid, preferred_element_type=F32)
    return out + jnp.dot(tri, lo, preferred_element_type=F32)


def _logf_fwd(zf, bf, *, name):
    s = zf.shape[0]
    t = CUM_T
    nb = s // t

    def body(z_ref, b_ref, c_ref, carry_ref):
        i = pl.program_id(0)

        @pl.when(i == 0)
        def _():
            carry_ref[...] = jnp.zeros_like(carry_ref)

        x = z_ref[...] + b_ref[...]
        lf = jnp.minimum(x, 0.0) - jnp.log(1.0 + jnp.exp(-jnp.abs(x)))
        row = lax.broadcasted_iota(jnp.int32, (t, t), 0)
        col = lax.broadcasted_iota(jnp.int32, (t, t), 1)
        tri = jnp.where(col <= row, 1.0, 0.0).astype(BF16)
        c = _tri_dot(tri, lf) + carry_ref[...]
        c_ref[...] = c
        carry_ref[...] = c[t - 1:t, :]

    return pl.pallas_call(
        body, name=name, grid=(nb,),
        in_specs=[pl.BlockSpec((t, LANES), lambda i: (i, 0)), pl.BlockSpec((1, LANES), lambda i: (0, 0))],
        out_specs=pl.BlockSpec((t, LANES), lambda i: (i, 0)),
        out_shape=jax.ShapeDtypeStruct((s, LANES), F32),
        scratch_shapes=[pltpu.VMEM((1, LANES), F32)],
        compiler_params=_params(("arbitrary",)),
    )(zf, bf)


def _logf_bwd(dc, zf, bf, *, name):
    s = zf.shape[0]
    t = CUM_T
    nb = s // t

    def body(dc_ref, z_ref, b_ref, dz_ref, db_ref, carry_ref):
        i = pl.program_id(0)

        @pl.when(i == 0)
        def _():
            carry_ref[...] = jnp.zeros_like(carry_ref)
            db_ref[...] = jnp.zeros_like(db_ref)

        row = lax.broadcasted_iota(jnp.int32, (t, t), 0)
        col = lax.broadcasted_iota(jnp.int32, (t, t), 1)
        tri = jnp.where(col >= row, 1.0, 0.0).astype(BF16)
        dlf = _tri_dot(tri, dc_ref[...]) + carry_ref[...]
        carry_ref[...] = dlf[0:1, :]
        x = z_ref[...] + b_ref[...]
        dz = dlf * _sigmoid(-x)
        dz_ref[...] = dz.astype(BF16)
        db_ref[...] += jnp.sum(dz, axis=0, keepdims=True)

    return pl.pallas_call(
        body, name=name, grid=(nb,),
        in_specs=[pl.BlockSpec((t, LANES), lambda i: (nb - 1 - i, 0)), pl.BlockSpec((t, LANES), lambda i: (nb - 1 - i, 0)),
                  pl.BlockSpec((1, LANES), lambda i: (0, 0))],
        out_specs=[pl.BlockSpec((t, LANES), lambda i: (nb - 1 - i, 0)), pl.BlockSpec((1, LANES), lambda i: (0, 0))],
        out_shape=[jax.ShapeDtypeStruct((s, LANES), BF16), jax.ShapeDtypeStruct((1, LANES), F32)],
        scratch_shapes=[pltpu.VMEM((1, LANES), F32)],
        compiler_params=_params(("arbitrary",)),
    )(dc, zf, bf)


def _sum_pieces(p_ref):
    g = p_ref[0].astype(F32)
    for k in range(1, N_DEV):
        g = g + p_ref[k].astype(F32)
    return g


def _adam_update(g, w, m, v):
    bc1 = 1.0 - ADAM_B1 ** ADAM_STEP
    bc2 = 1.0 - ADAM_B2 ** ADAM_STEP
    nm = ADAM_B1 * m + (1.0 - ADAM_B1) * g
    nv = ADAM_B2 * v + (1.0 - ADAM_B2) * (g * g)
    m_hat = nm / bc1
    v_hat = nv / bc2
    return -ADAM_LR * (m_hat / (jnp.sqrt(v_hat) + ADAM_EPS) + ADAM_WD * w), nm, nv


def _adamw(pieces, w, m, v, *, name):
    rows, cols = w.shape
    tr = _tile(rows, (RB1, RB0, SMALL_ROWS))

    def body(p_ref, w_ref, m_ref, v_ref, g_ref, d_ref, nm_ref, nv_ref):
        g = _sum_pieces(p_ref)
        g_ref[...] = g
        d_ref[...], nm_ref[...], nv_ref[...] = _adam_update(g, w_ref[...], m_ref[...], v_ref[...])

    spec = pl.BlockSpec((tr, cols), lambda i: (i, 0))
    shape = jax.ShapeDtypeStruct((rows, cols), F32)
    return pl.pallas_call(
        body, name=name, grid=(rows // tr,),
        in_specs=[pl.BlockSpec((N_DEV, tr, cols), lambda i: (0, i, 0)), spec, spec, spec],
        out_specs=[spec, spec, spec, spec], out_shape=[shape, shape, shape, shape],
        compiler_params=_params(("parallel",)),
    )(pieces, w, m, v)


def _sum8(pieces, rows, *, name):
    cols = pieces.shape[2]
    tr = _tile(rows, (176, 96))

    def body(p_ref, g_ref):
        g_ref[...] = _sum_pieces(p_ref)

    return pl.pallas_call(
        body, name=name, grid=(rows // tr,),
        in_specs=[pl.BlockSpec((N_DEV, tr, cols), lambda i: (0, i, 0))],
        out_specs=pl.BlockSpec((tr, cols), lambda i: (i, 0)),
        out_shape=jax.ShapeDtypeStruct((rows, cols), F32),
        compiler_params=_params(("parallel",)),
    )(pieces)


def _adamw_native(g, w, m, v, *, name):
    rows, cols = w.shape
    tr = _tile(rows, (256, 128))

    def body(g_ref, w_ref, m_ref, v_ref, d_ref, nm_ref, nv_ref):
        d_ref[...], nm_ref[...], nv_ref[...] = _adam_update(g_ref[...], w_ref[...], m_ref[...], v_ref[...])

    spec = pl.BlockSpec((tr, cols), lambda i: (i, 0))
    shape = jax.ShapeDtypeStruct((rows, cols), F32)
    return pl.pallas_call(
        body, name=name, grid=(rows // tr,), in_specs=[spec, spec, spec, spec],
        out_specs=[spec, spec, spec], out_shape=[shape, shape, shape],
        compiler_params=_params(("parallel",)),
    )(g, w, m, v)


MESH = pl.DeviceIdType.MESH
ANY = pl.BlockSpec(memory_space=pl.ANY)


def _all_gather(shard, *, name):
    rows, lanes = shard.shape

    def body(x_ref, out_ref, send_sems, recv_sems, local_sem):
        x, y, c = lax.axis_index("x"), lax.axis_index("y"), lax.axis_index("c")
        me, sibling = (x, y, c), (x, y, 1 - c)
        chips = [(1 - x, y), (x, 1 - y), (1 - x, 1 - y)]

        def block(px, py, pc):
            return out_ref.at[4 * px + 2 * py + pc]

        def copy(k, blk, to, src=None):
            return pltpu.make_async_remote_copy(
                src_ref=block(*blk) if src is None else src, dst_ref=block(*blk),
                send_sem=send_sems.at[k], recv_sem=recv_sems.at[k], device_id=to, device_id_type=MESH)

        mine = pltpu.make_async_copy(x_ref, block(*me), local_sem)
        mine.start()
        first = [copy(0, me, sibling, src=x_ref)]
        first += [copy(1 + j, me, (*chip, c), src=x_ref) for j, chip in enumerate(chips)]
        for cp in first:
            cp.start()
        passed = [copy(4 + j, (*chip, c), sibling) for j, chip in enumerate(chips)]
        for j, chip in enumerate(chips):
            copy(1 + j, (*chip, c), me).wait_recv()
            passed[j].start()
        copy(0, sibling, me).wait_recv()
        for j, chip in enumerate(chips):
            copy(4 + j, (*chip, 1 - c), me).wait_recv()
        for cp in first + passed:
            cp.wait_send()
        mine.wait()

    return pl.pallas_call(
        body, name=name, out_shape=jax.ShapeDtypeStruct((N_DEV, rows, lanes), shard.dtype),
        in_specs=[ANY], out_specs=ANY,
        scratch_shapes=[pltpu.SemaphoreType.DMA((7,)), pltpu.SemaphoreType.DMA((7,)), pltpu.SemaphoreType.DMA(())],
    )(shard)


def _peer_copies(kind, src_ref, out_ref, send_sems, recv_sems, local_sem):
    x, y, c = lax.axis_index("x"), lax.axis_index("y"), lax.axis_index("c")
    me = 4 * x + 2 * y + c

    def src(idx):
        return src_ref.at[idx] if kind == "exchange" else src_ref

    mine = None if local_sem is None else pltpu.make_async_copy(src(me), out_ref.at[me], local_sem)
    copies = []
    for r in range(1, N_DEV):
        px = 1 - x if r & 4 else x
        py = 1 - y if r & 2 else y
        pc = 1 - c if r & 1 else c
        copies.append(pltpu.make_async_remote_copy(
            src_ref=src(4 * px + 2 * py + pc), dst_ref=out_ref.at[me],
            send_sem=send_sems.at[r - 1], recv_sem=recv_sems.at[r - 1],
            device_id=(px, py, pc), device_id_type=MESH))
    return mine, copies


PEER_SEMS = [pltpu.SemaphoreType.DMA((7,)), pltpu.SemaphoreType.DMA((7,)), pltpu.SemaphoreType.DMA(())]


HBM = pl.BlockSpec(memory_space=pltpu.HBM)
SEMAPHORES = pl.BlockSpec(memory_space=pltpu.SEMAPHORE)


def _peer_start(kind, arr, *, name):
    land = lax.empty((N_DEV,) + arr.shape[-2:], arr.dtype)

    def body(src_ref, land_ref, send_sems, recv_sems, src_thru, land_thru, token):
        _, copies = _peer_copies(kind, src_ref, land_ref, send_sems, recv_sems, None)
        for cp in copies:
            cp.start()
        token[...] = jnp.zeros_like(token)

    return pl.pallas_call(
        body, name=name,
        out_shape=(pltpu.SemaphoreType.DMA((N_DEV - 1,)), pltpu.SemaphoreType.DMA((N_DEV - 1,)),
                   pltpu.HBM(arr.shape, arr.dtype), pltpu.HBM(land.shape, land.dtype), jax.ShapeDtypeStruct((8, LANES), F32)),
        in_specs=(HBM, HBM), out_specs=(SEMAPHORES, SEMAPHORES, HBM, HBM, pl.BlockSpec(memory_space=pltpu.VMEM)),
        input_output_aliases={0: 2, 1: 3},
        compiler_params=pltpu.CompilerParams(has_side_effects=pltpu.SideEffectType.DATAFLOW_SIDE_EFFECTING),
    )(pltpu.with_memory_space_constraint(arr, pltpu.HBM), pltpu.with_memory_space_constraint(land, pltpu.HBM))


def _peer_wait(kind, send_sems, recv_sems, src_thru, land_thru, after, *, name):
    def body(src_ref, land_ref, send_sems, recv_sems, *_):
        _, copies = _peer_copies(kind, src_ref, land_ref, send_sems, recv_sems, None)
        for cp in copies:
            cp.wait_send()
            cp.wait_recv()

    return pl.pallas_call(
        body, name=name,
        out_shape=(pltpu.HBM(src_thru.shape, src_thru.dtype), pltpu.HBM(land_thru.shape, land_thru.dtype)),
        in_specs=(HBM, HBM, SEMAPHORES, SEMAPHORES) + (ANY,) * len(after), out_specs=(HBM, HBM),
        input_output_aliases={0: 0, 1: 1},
        compiler_params=pltpu.CompilerParams(has_side_effects=pltpu.SideEffectType.DATAFLOW_SIDE_EFFECTING),
    )(src_thru, land_thru, send_sems, recv_sems, *after)


def _add_rider(rider, in_specs, args, out_specs, out_shape):
    if rider is None:
        return []
    _, arr = rider
    in_specs.append(ANY)
    args.append(arr)
    out_specs.append(ANY)
    out_shape.append(jax.ShapeDtypeStruct((N_DEV,) + arr.shape[-2:], arr.dtype))
    return list(PEER_SEMS)


def _split_rider(refs, rider, n_in, n_out):
    if rider is None:
        return refs, None
    refs = list(refs)
    rin = refs.pop(n_in)
    rout = refs.pop(n_in + n_out)
    return refs[:-3], (rin, rout, *refs[-3:])


def _ride_start(rider, ride_refs, first):
    if rider is None:
        return

    @pl.when(first)
    def _():
        mine, copies = _peer_copies(rider[0], *ride_refs)
        mine.start()
        for cp in copies:
            cp.start()


def _ride_wait(rider, ride_refs, last):
    if rider is None:
        return

    @pl.when(last)
    def _():
        mine, copies = _peer_copies(rider[0], *ride_refs)
        for cp in copies:
            cp.wait()
        mine.wait()


def _gathered_cols(blocks, kdim):
    n = blocks.shape[1] * WIDE // kdim
    return blocks.reshape(N_DEV, kdim, n).transpose(1, 0, 2).reshape(kdim, N_DEV * n)


def _scatter_cols(dw):
    kdim, n8 = dw.shape
    n = n8 // N_DEV
    return dw.reshape(kdim, N_DEV, n).transpose(1, 0, 2).reshape(N_DEV, kdim * n // WIDE, WIDE)


def _pad_rows(a, rows):
    pad = [(0, 0)] * a.ndim
    pad[-2] = (0, rows - a.shape[-2])
    return jnp.pad(a, pad)


def _layer0_in_weight_t(wt):
    cq, ckv, kpe = wt[0:256], wt[256:384], wt[384:416]
    q_s, k_s, v_s, gate = wt[416:928], wt[928:1056], wt[1056:1184], wt[1184:2208]
    z = jnp.zeros((64, wt.shape[1]), wt.dtype)
    return jnp.concatenate([gate, cq, ckv, z, kpe, z[:32], q_s, k_s, v_s], axis=0)


def _layer0_in_grad_t(dwt):
    gate, cq, ckv, kpe = dwt[0:1024], dwt[1024:1280], dwt[1280:1408], dwt[1472:1504]
    q_s, k_s, v_s = dwt[1536:2048], dwt[2048:2176], dwt[2176:2304]
    return jnp.concatenate([cq, ckv, kpe, q_s, k_s, v_s, gate], axis=0)


def _layer1_in_weight_t(wt):
    main = jnp.concatenate([wt[:3 * D_MODEL], wt[3 * D_MODEL + FOX_HEADS:]], axis=0)
    return main, _pad_rows(wt[3 * D_MODEL:3 * D_MODEL + FOX_HEADS], LANES)


def _layer1_in_grad_t(d_blocks, d_wft):
    return jnp.concatenate([d_blocks[0], d_wft[:FOX_HEADS], d_blocks[1]], axis=0)


def _q_up_weight(w):
    return jnp.pad(w.reshape(MLA_Q_RANK, MLA_HEADS, 96), ((0, 0), (0, 0), (0, 32))).reshape(MLA_Q_RANK, MLA_HEADS * LANES)


def _q_up_grad(dwp):
    return dwp.reshape(MLA_Q_RANK, MLA_HEADS, LANES)[:, :, :96].reshape(MLA_Q_RANK, MLA_HEADS * 96)


def _kv_up_weight(w):
    w4 = w.reshape(MLA_KV_RANK, MLA_HEADS, 2, 64)
    kp = jnp.pad(w4[:, :, 0, :], ((0, 0), (0, 0), (0, 64))).reshape(MLA_KV_RANK, MLA_HEADS * LANES)
    vp = w4[:, :, 1, :].reshape(MLA_KV_RANK, MLA_HEADS * 64)
    return jnp.concatenate([kp, vp], axis=1)


def _kv_up_grad(dwp):
    dk = dwp[:, :MLA_HEADS * LANES].reshape(MLA_KV_RANK, MLA_HEADS, LANES)[:, :, :64]
    dv = dwp[:, MLA_HEADS * LANES:].reshape(MLA_KV_RANK, MLA_HEADS, 64)
    return jnp.stack([dk, dv], axis=2).reshape(MLA_KV_RANK, MLA_HEADS * LANES)


def _pad_lanes(a):
    return jnp.pad(a, ((0, 0), (0, LANES - a.shape[1])))


def _small_pack(g_in, g_final, g_q_a, g_kv_a, sinks, b_f, loss):
    rows = [g_in.reshape(8, LANES), g_final.reshape(8, LANES), g_q_a.reshape(2, LANES), g_kv_a.reshape(1, LANES),
            _pad_lanes(sinks.reshape(1, -1)), _pad_lanes(b_f.reshape(1, -1)), _pad_lanes(loss.reshape(1, 1)),
            jnp.zeros((2, LANES), F32)]
    return jnp.concatenate(rows, axis=0)


def _small_unpack(a):
    return (a[0:8].reshape(1, D_MODEL), a[8:16].reshape(D_MODEL), a[16:18].reshape(1, MLA_Q_RANK),
            a[18:19].reshape(1, MLA_KV_RANK), a[19:20, :SWA_HEADS], a[20:21, :FOX_HEADS], a[21, 0])


def _local_step(x, positions, target, e_g_in, early, e_g_q_a, e_g_kv_a, e_sinks,
                late, o_b_f, g_final, scatter1=None, scatter0=None):
    s = x.shape[0]
    mla_scale = (MLA_NOPE + MLA_ROPE) ** -0.5
    fox_scale = FOX_DIM ** -0.5
    n0a = Z0A_UNITS * LANES

    inv_freq = 1.0 / (ROPE_THETA ** (jnp.arange(0, MLA_ROPE, 2, dtype=F32) / MLA_ROPE))
    ang = positions.astype(F32)[:, None] * inv_freq
    cos, sin = jnp.cos(ang), jnp.sin(ang)
    ones, zeros = jnp.ones((s, 64), F32), jnp.zeros((s, 64), F32)
    cos_t = jnp.concatenate([ones, cos, cos, ones[:, :32]], axis=1)
    sin_t = jnp.concatenate([zeros, -sin, sin, zeros[:, :32]], axis=1)

    if len(early) == 3:
        h0 = _rmsnorm_fwd(x, e_g_in, width=D_MODEL, col_blk=0, name="l0_norm")
        w0t, wq, wkv = early
    else:
        pending, token, unpack, prep = early
        h0 = _rmsnorm_fwd(x, e_g_in, width=D_MODEL, col_blk=0, name="l0_norm", after=[token])
        w0t, wq, wkv = unpack(*_peer_wait("gather", *pending, after=[h0] + prep, name="weights0_wait"))
    z0a = _matmul(h0, w0t, tb=True, b_rows=(0, n0a), name="l0_in_a")
    z0b = _matmul(h0, w0t, tb=True, b_rows=(n0a, Z0B_UNITS * LANES), name="l0_in_b", out_dtype=BF16)
    cqn = _rmsnorm_fwd(z0a, e_g_q_a, width=MLA_Q_RANK, col_blk=4, name="l0_q_norm")
    ckvn = _rmsnorm_fwd(z0a, e_g_kv_a, width=MLA_KV_RANK, col_blk=10, name="l0_kv_norm")
    qp = _matmul(cqn, wq, name="l0_q_up")
    kvp = _matmul(ckvn, wkv, name="l0_kv_up", out_dtype=BF16)
    qm, km = _rope_fwd(qp, kvp, z0a, cos_t, sin_t, name="l0_rope")
    gathers = len(late) == 2
    res = _flash_fwd(qm, km, kvp, None, n_pairs=MLA_HEADS // 2, hw=LANES, q_off=0, k_off=0, v_off=MLA_HEADS,
                     scale=mla_scale, name="l0_mla_fwd", rider=("gather", late[0]) if gathers else None)
    o_mla, lse_mla = res[0], res[1]
    wo0, o_g_in, w1t, wft, wo1 = late[1](res[2]) if gathers else late
    o_swa, lse_swa = _swa_fwd(z0b, e_sinks, name="l0_swa_fwd")
    half = D_MODEL // 2

    x1, h1, og0 = _matmul_rows(
        [(wo0, False)], [(o_mla, half, 0), (o_swa, half, 0), (z0a, D_MODEL, 0), (x, D_MODEL, 0)], [o_g_in],
        lambda r, om, osw, gt, xt, g, made: (*_residual_norm_epilogue(r, xt, g), made),
        [("rows", D_MODEL, F32), ("rows", D_MODEL, BF16), ("rows", D_MODEL, BF16)], name="l0_out",
        prologue=lambda om, osw, gt, xt, g: _gated([om, osw], gt))
    z1 = _matmul(h1, w1t, tb=True, b_rows=(0, 3 * D_MODEL), name="l1_in_qkv", out_dtype=BF16)
    gate1 = _matmul(h1, w1t, tb=True, b_rows=(3 * D_MODEL, D_MODEL), name="l1_in_gate")
    zf = _matmul(h1, wft, tb=True, name="l1_in_f")
    bf = _pad_lanes(o_b_f)
    log_cum = _logf_fwd(zf, bf, name="l1_logf")
    bias2 = (-LOG2E * log_cum[:, :FOX_HEADS]).T
    t_bwd = min(ATT_T, s)
    bias = bias2.reshape(FOX_HEADS // 2, 2, s // t_bwd, 1, t_bwd)
    t_fwd = _fwd_tile(s)
    o_fox, lse_fox = _flash_fwd(z1, z1, z1, bias2.reshape(FOX_HEADS // 2, 2, s // t_fwd, 1, t_fwd),
                                n_pairs=FOX_HEADS // 2, hw=64, q_off=0, k_off=8, v_off=16, scale=fox_scale,
                                name="l1_fox_fwd")

    dx2, loss_part, d_g_final, og1 = _matmul_rows(
        [(wo1, False)], [(o_fox, D_MODEL, 0), (gate1, D_MODEL, 0), (x1, D_MODEL, 0), (target, D_MODEL, 0)],
        [g_final.reshape(1, D_MODEL)], lambda r, o, gt, xt, tg, g, made: (*_loss_epilogue(r, xt, tg, g), made),
        [("rows", D_MODEL, F32), ("sum", (8, LANES)), ("sum", (1, D_MODEL)), ("rows", D_MODEL, BF16)],
        name="l1_out_loss", prologue=lambda o, gt, xt, tg, g: _gated([o], gt))

    d_wo1 = _matmul(og1, dx2, ta=True, name="l1_out_dw")
    do_fox, d_gate1 = _matmul_rows([(dx2, wo1, True)], [(o_fox, D_MODEL, 0), (gate1, D_MODEL, 0)], [],
                                   _gate_bwd_epilogue([D_MODEL]), [("rows", D_MODEL, F32), ("rows", D_MODEL, BF16)],
                                   name="l1_out_dx")
    dqkv1, dbias, drow = _flash_bwd(z1, z1, z1, do_fox, o_fox, lse_fox, bias, n_pairs=FOX_HEADS // 2, hw=64, q_off=0,
                                    k_off=8, v_off=16, scale=fox_scale, qk_dtype=BF16, stacked=True, name="l1_fox_bwd")
    d_log_cum = (drow.reshape(FOX_HEADS, s) - dbias.reshape(FOX_HEADS, s)).T
    d_log_cum = jnp.pad(d_log_cum, ((0, 0), (0, LANES - FOX_HEADS)))
    d_zf, d_bf = _logf_bwd(d_log_cum, zf, bf, name="l1_logf_bwd")
    d_w1t = (_matmul(dqkv1, h1, ta=True, name="l1_in_dw_qkv"), _matmul(d_gate1, h1, ta=True, name="l1_in_dw_gate"))
    d_wft = _matmul(d_zf, h1, ta=True, name="l1_in_f_dw")
    dx1, d_o_g_in = _matmul_rows([(dqkv1, w1t, False, c * D_MODEL, c) for c in range(3)]
                                 + [(d_gate1, w1t, False, 3 * D_MODEL), (d_zf, wft, False)],
                                 [(x1, D_MODEL, 0), (dx2, D_MODEL, 0)],
                                 [o_g_in], _rms_bwd_epilogue, [("rows", D_MODEL, F32), ("sum", (1, D_MODEL))],
                                 name="l1_in_dx")

    d_wo0 = _matmul(og0, dx1, ta=True, name="l0_out_dw")
    do_mla, do_swa, d_gate0 = _matmul_rows(
        [(dx1, wo0, True)], [(o_mla, half, 0), (o_swa, half, 0), (z0a, D_MODEL, 0)], [], _gate_bwd_epilogue([half, half]),
        [("rows", half, F32), ("rows", half, F32), ("rows", D_MODEL, BF16)], name="l0_out_dx")
    dq_s, dkt_s, dvt_s, d_sinks = _swa_bwd(z0b, e_sinks, do_swa, o_swa, lse_swa, name="l0_swa_bwd")
    dk_s = dkt_s.transpose(0, 2, 1).reshape(s, LANES)
    dv_s = dvt_s.transpose(0, 2, 1).reshape(s, LANES)
    rider = None
    if scatter1 is not None:
        rider = ("exchange", scatter1(dict(w1t=d_w1t, wft=d_wft, wo1=d_wo1, o_g_in=d_o_g_in, wo0=d_wo0)))
    res = _flash_bwd(qm, km, kvp, do_mla, o_mla, lse_mla, None, n_pairs=MLA_HEADS // 2, hw=LANES, q_off=0, k_off=0,
                     v_off=MLA_HEADS, scale=mla_scale, qk_dtype=F32, name="l0_mla_bwd", rider=rider)
    dqm, dkm, dvm = res[0], res[1], res[2]
    recv1 = res[3] if rider is not None else None
    d_qp, d_kvp, d_kpe = _rope_bwd(dqm, dkm, dvm, cos_t, sin_t, name="l0_rope_bwd")
    d_wq = _matmul(cqn, d_qp, ta=True, name="l0_q_up_dw")
    d_cqn = _matmul(d_qp, wq, tb=True, name="l0_q_up_dx")
    d_wkv = _matmul(ckvn, d_kvp, ta=True, name="l0_kv_up_dw")
    d_ckvn = _matmul(d_kvp, wkv, tb=True, name="l0_kv_up_dx")
    d_cq, d_g_q_a = _rmsnorm_bwd(z0a, e_g_q_a, d_cqn, width=MLA_Q_RANK, col_blk=4, name="l0_q_norm_bwd")
    d_ckv, d_g_kv_a = _rmsnorm_bwd(z0a, e_g_kv_a, d_ckvn, width=MLA_KV_RANK, col_blk=10, name="l0_kv_norm_bwd")
    dz0 = jnp.concatenate([d_gate0, d_cq, d_ckv, d_kpe, dq_s.astype(BF16), dk_s.astype(BF16), dv_s.astype(BF16)], axis=1)
    d_w0t = _matmul(dz0, h0, ta=True, name="l0_in_dw")
    pending0, after_start = None, []
    if scatter0 is not None:
        *pending0, token = _peer_start("exchange", scatter0(dict(w0t=d_w0t, wq=d_wq, wkv=d_wkv)), name="grads0_start")
        after_start = [token]
    grad_x, d_e_g_in = _matmul_rows(
        [(dz0, w0t, False)], [(x, D_MODEL, 0), (dx1, D_MODEL, 0)], [e_g_in] + after_start,
        lambda dy, xt, add, g, *_: _rms_bwd_epilogue(dy, xt, add, g),
        [("rows", D_MODEL, F32), ("sum", (1, D_MODEL))], name="l0_in_dx")

    return dict(pending0=pending0, recv1=recv1, loss=loss_part[0, 0], grad_x=grad_x, e_g_in=d_e_g_in, w0t=d_w0t, e_g_q_a=d_g_q_a, wq=d_wq,
                e_g_kv_a=d_g_kv_a, wkv=d_wkv, e_sinks=d_sinks[:, 0].reshape(1, SWA_HEADS), wo0=d_wo0,
                o_g_in=d_o_g_in, w1t=d_w1t, wft=d_wft, o_b_f=d_bf[:, :FOX_HEADS], wo1=d_wo1, g_final=d_g_final.reshape(D_MODEL))


def _wide(a, rows):
    flat = a.reshape(-1)
    return jnp.pad(flat, (0, rows * WIDE - flat.shape[0])).reshape(rows, WIDE)


def _rows_b0(w_q, w_kv):
    return jnp.concatenate([_wide(w_q, 32), _wide(w_kv, 16)], axis=0)


def _unflat_b0(f):
    return f[0:24].reshape(1, MLA_Q_RANK, 96), f[32:48].reshape(1, MLA_KV_RANK, 128)


def _rows_b1(o_w_out, e_w_out, g_in):
    return jnp.concatenate([o_w_out, e_w_out, _wide(g_in, 16)], axis=0)


def _unflat_b1(f):
    return f[0:128][None], f[128:256][None], f[256:257, :LANES]


def kernel(x, positions, e_g_in, e_w_in, e_g_q_a, e_w_q_up, e_g_kv_a, e_w_kv_up, e_sinks, e_w_out, o_g_in, o_w_in, o_b_f, o_w_out, g_final, loss_target, m_e_g_in, m_e_w_in, m_e_g_q_a, m_e_w_q_up, m_e_g_kv_a, m_e_w_kv_up, m_e_sinks, m_e_w_out, m_o_g_in, m_o_w_in, m_o_b_f, m_o_w_out, m_g_final, v_e_g_in, v_e_w_in, v_e_g_q_a, v_e_w_q_up, v_e_g_kv_a, v_e_w_kv_up, v_e_sinks, v_e_w_out, v_o_g_in, v_o_w_in, v_o_b_f, v_o_w_out, v_g_final):
    def bf(a):
        return a.astype(BF16)

    me = 4 * lax.axis_index("x") + 2 * lax.axis_index("y") + lax.axis_index("c")
    shard0 = jnp.concatenate([_pad_rows(bf(e_w_in[0]).T, RA0), _rows_b0(bf(e_w_q_up[0]), bf(e_w_kv_up[0]))], axis=0)
    *pending_w0, token_w0 = _peer_start("gather", shard0, name="weights0_start")

    def unpack0(sent, gath0):
        gath0 = lax.dynamic_update_slice_in_dim(gath0, sent[None], me, axis=0)
        w0t = _layer0_in_weight_t(gath0[:, :N_E_IN].reshape(N_DEV * N_E_IN, WIDE))
        wq = _q_up_weight(_gathered_cols(gath0[:, RA0:RA0 + 24], MLA_Q_RANK))
        wkv = _kv_up_weight(_gathered_cols(gath0[:, RA0 + 32:RA0 + 48], MLA_KV_RANK))
        return w0t, wq, wkv

    rows_b0 = [_rows_b0(q[0], kv[0]) for q, kv in ((e_w_q_up, e_w_kv_up), (m_e_w_q_up, m_e_w_kv_up), (v_e_w_q_up, v_e_w_kv_up))]
    rows_b1 = [_rows_b1(o[0], e[0], g) for o, e, g in ((o_w_out, e_w_out, o_g_in), (m_o_w_out, m_e_w_out, m_o_g_in),
                                                       (v_o_w_out, v_e_w_out, v_o_g_in))]

    g_bits = lax.bitcast_convert_type(o_g_in.reshape(LANES), BF16)
    shard1 = jnp.concatenate([_pad_rows(bf(o_w_in[0]).T, RA1), _rows_b1(bf(o_w_out[0]), bf(e_w_out[0]), g_bits)], axis=0)

    def unpack1(gath1):
        w1t, wft = _layer1_in_weight_t(gath1[:, :N_O_IN].reshape(N_DEV * N_O_IN, WIDE))
        wo1 = gath1[:, RA1:RA1 + 128].reshape(D_MODEL, D_MODEL)
        wo0 = gath1[:, RA1 + 128:RA1 + 256].reshape(D_MODEL, D_MODEL)
        bits = gath1[:, RA1 + 256, :2 * LANES].reshape(N_DEV, LANES, 2)
        return wo0, lax.bitcast_convert_type(bits, F32).reshape(1, D_MODEL), w1t, wft, wo1

    def scatter1(g):
        d_in_t = _layer1_in_grad_t(g["w1t"], g["wft"]).reshape(N_DEV, N_O_IN, WIDE)
        d_o_g = jnp.pad(g["o_g_in"].reshape(N_DEV, 1, LANES), ((0, 0), (0, 15), (0, WIDE - LANES)))
        return jnp.concatenate([_pad_rows(d_in_t, RA1), g["wo1"].reshape(N_DEV, 128, WIDE),
                                g["wo0"].reshape(N_DEV, 128, WIDE), d_o_g], axis=1).astype(BF16)

    def scatter0(g):
        return jnp.concatenate([
            _pad_rows(_layer0_in_grad_t(g["w0t"]).reshape(N_DEV, N_E_IN, WIDE), RA0),
            _pad_rows(_scatter_cols(_q_up_grad(g["wq"])), 32), _scatter_cols(_kv_up_grad(g["wkv"]))], axis=1).astype(BF16)

    gr = _local_step(x[0], positions[0], loss_target[0], e_g_in,
                     (pending_w0, token_w0, unpack0, [shard1] + rows_b0 + rows_b1), e_g_q_a, e_g_kv_a, e_sinks,
                     (shard1, unpack1), o_b_f, g_final, scatter1=scatter1, scatter0=scatter0)

    def in_projection(recv, ra, n, w, m, v, name):
        g = _sum8(recv, ra, name=name + "_grad_sum")[:n].T
        d, nm, nv = _adamw_native(g, w[0], m[0], v[0], name=name + "_adamw")
        return g[None], d[None], nm[None], nv[None]

    o_in = in_projection(gr["recv1"], RA1, N_O_IN, o_w_in, m_o_w_in, v_o_w_in, "o_w_in")
    b1 = _adamw(gr["recv1"][:, RA1:], *rows_b1, name="adamw_late")

    small = _small_pack(gr["e_g_in"], gr["g_final"], gr["e_g_q_a"], gr["e_g_kv_a"], gr["e_sinks"], gr["o_b_f"], gr["loss"])
    small_all = _all_gather(small, name="small_all_gather")
    zero = jnp.zeros((), F32)
    w_small = _small_pack(e_g_in, g_final, e_g_q_a, e_g_kv_a, e_sinks, o_b_f, zero)
    m_small = _small_pack(m_e_g_in, m_g_final, m_e_g_q_a, m_e_g_kv_a, m_e_sinks, m_o_b_f, zero)
    v_small = _small_pack(v_e_g_in, v_g_final, v_e_g_q_a, v_e_g_kv_a, v_e_sinks, v_o_b_f, zero)
    smalls = _adamw(small_all, w_small, m_small, v_small, name="adamw_replicated")
    g_sm, d_sm, m_sm, v_sm = [_small_unpack(a) for a in smalls]
    loss = g_sm[6]

    sent0, recv0 = _peer_wait("exchange", *gr["pending0"], after=[o_in[1], b1[1], smalls[1]], name="grads0_wait")
    own = lax.dynamic_slice_in_dim(sent0, me, 1, axis=0)
    recv0 = lax.dynamic_update_slice_in_dim(recv0, own, me, axis=0)
    e_in = in_projection(recv0, RA0, N_E_IN, e_w_in, m_e_w_in, v_e_w_in, "e_w_in")
    b0 = _adamw(recv0[:, RA0:], *rows_b0, name="adamw_early")

    def sharded(k):
        q_up, kv_up = _unflat_b0(b0[k])
        o_out, e_out, o_g = _unflat_b1(b1[k])
        return e_in[k], q_up, kv_up, e_out, o_in[k], o_out, o_g

    g_sh, d_sh, m_sh, v_sh = [sharded(k) for k in range(4)]

    def leaves(sh, sm):
        return (sm[0], sh[0], sm[2], sh[1], sm[3], sh[2], sm[4], sh[3], sh[6], sh[4], sm[5], sh[5], sm[1])

    return (loss, gr["grad_x"][None], *leaves(g_sh, g_sm), *leaves(d_sh, d_sm), *leaves(m_sh, m_sm), *leaves(v_sh, v_sm))
```

```python
import functools

import jax
import jax.numpy as jnp
from jax import lax
from jax.experimental import pallas as pl
from jax.experimental.pallas import tpu as pltpu

F32 = jnp.float32
BF16 = jnp.bfloat16
NEG_INF = float("-inf")

N_DEV = 8
LANES = 128
D_MODEL = 1024
EPS = 1e-6
ROPE_THETA = 10000.0
MLA_HEADS = 8
MLA_Q_RANK = 256
MLA_KV_RANK = 128
MLA_NOPE = 64
MLA_ROPE = 32
MLA_V = 64
SWA_HEADS = 8
SWA_KV_HEADS = 2
SWA_DIM = 64
WINDOW = 128
FOX_HEADS = 16
FOX_DIM = 64

ADAM_LR = 0.001
ADAM_B1 = 0.9
ADAM_B2 = 0.999
ADAM_EPS = 1e-08
ADAM_WD = 0.01
ADAM_STEP = 10

ATT_T = 512
ATT_T_FWD = 1024
VMEM_LIMIT = 56 * 1024 * 1024
MATMUL_B_BLOCK_BYTES = 8 * 1024 * 1024

Z0A_UNITS = 12
Z0B_UNITS = 6

WIDE = 1024
N_E_IN = 276
N_O_IN = 514
RA0 = 288
RB0 = 32 + 16
RA1 = 528
RB1 = 128 + 128 + 16
SMALL_ROWS = 24


def _tile(n, cands):
    for c in cands:
        if n % c == 0:
            return c
    raise ValueError(f"no tile for {n}")


ROW_TILES = (512, 256, 128)


def _params(sem, vmem=VMEM_LIMIT):
    return pltpu.CompilerParams(dimension_semantics=sem, vmem_limit_bytes=vmem)


def _matmul(a, b, *, name, ta=False, tb=False, out_dtype=F32, b_rows=None):
    if ta:
        kdim, m = a.shape[-2], a.shape[-1] * (a.shape[0] if a.ndim == 3 else 1)
    else:
        m, kdim = a.shape
    if tb:
        n, kb = b.shape
    else:
        kb, n = b.shape
    assert kdim == kb, (a.shape, b.shape)
    b_start = 0
    if b_rows is not None:
        assert tb
        b_start, n = b_rows
    tm = _tile(m, (512, 256, 128))
    tn = _tile(n, [c for c in (1024, 768, 512, 384, 256, 128)
                   if c * kdim * b.dtype.itemsize <= MATMUL_B_BLOCK_BYTES and b_start % c == 0])
    assert b_start % tn == 0, (b_start, tn)
    b_off = b_start // tn
    dims = (((0 if ta else 1,), (1 if tb else 0,)), ((), ()))

    def body(a_ref, b_ref, o_ref):
        r = lax.dot_general(a_ref[...].astype(BF16), b_ref[...].astype(BF16), dims, preferred_element_type=F32)
        o_ref[...] = r.astype(out_dtype)

    if a.ndim == 3:
        per = a.shape[2] // tm
        a_spec = pl.BlockSpec((None, kdim, tm), lambda i, j: (i // per, 0, i % per))
    else:
        a_spec = pl.BlockSpec((kdim, tm), lambda i, j: (0, i)) if ta else pl.BlockSpec((tm, kdim), lambda i, j: (i, 0))
    b_spec = pl.BlockSpec((tn, kdim), lambda i, j: (j + b_off, 0)) if tb else pl.BlockSpec((kdim, tn), lambda i, j: (0, j))
    return pl.pallas_call(
        body, name=name, grid=(m // tm, n // tn), in_specs=[a_spec, b_spec],
        out_specs=pl.BlockSpec((tm, tn), lambda i, j: (i, j)), out_shape=jax.ShapeDtypeStruct((m, n), out_dtype),
        compiler_params=_params(("parallel", "parallel")),
    )(a, b)


def _rmsnorm_fwd(x, g, *, width, col_blk, name, after=()):
    s = x.shape[0]
    tm = _tile(s, ROW_TILES)

    def body(x_ref, g_ref, *rest):
        y_ref = rest[-1]
        xf = x_ref[...].astype(F32)
        r = lax.rsqrt(jnp.mean(xf * xf, axis=-1, keepdims=True) + EPS)
        y_ref[...] = ((xf * r) * g_ref[...]).astype(BF16)

    return pl.pallas_call(
        body, name=name, grid=(s // tm,),
        in_specs=[pl.BlockSpec((tm, width), lambda i: (i, col_blk)), pl.BlockSpec((1, width), lambda i: (0, 0))]
        + [ANY] * len(after),
        out_specs=pl.BlockSpec((tm, width), lambda i: (i, 0)),
        out_shape=jax.ShapeDtypeStruct((s, width), BF16),
        compiler_params=_params(("parallel",)),
    )(x, g, *after)


def _rmsnorm_bwd(x, g, dy, *, width, col_blk, name):
    s = x.shape[0]
    tm = _tile(s, ROW_TILES)

    def body(x_ref, g_ref, dy_ref, dx_ref, dg_ref):
        @pl.when(pl.program_id(0) == 0)
        def _():
            dg_ref[...] = jnp.zeros_like(dg_ref)

        dx, dg = _rms_bwd_epilogue(dy_ref[...], x_ref[...], 0.0, g_ref[...])
        dg_ref[...] += dg
        dx_ref[...] = dx.astype(BF16)

    return pl.pallas_call(
        body, name=name, grid=(s // tm,),
        in_specs=[pl.BlockSpec((tm, width), lambda i: (i, col_blk)), pl.BlockSpec((1, width), lambda i: (0, 0)),
                  pl.BlockSpec((tm, width), lambda i: (i, 0))],
        out_specs=[pl.BlockSpec((tm, width), lambda i: (i, 0)), pl.BlockSpec((1, width), lambda i: (0, 0))],
        out_shape=[jax.ShapeDtypeStruct((s, width), BF16), jax.ShapeDtypeStruct((1, width), F32)],
        compiler_params=_params(("arbitrary",)),
    )(x, g, dy)


def _sigmoid(x):
    return 1.0 / (1.0 + jnp.exp(-x))


def _matmul_rows(terms, row_inputs, params, epilogue, outs, *, name, prologue=None):
    if prologue is not None:
        terms = [(None,) + tuple(terms[0])] + list(terms[1:])
    s = row_inputs[0][0].shape[0]
    tm = _tile(s, ROW_TILES)
    steps = s // tm
    n_r, n_p, n_o = len(row_inputs), len(params), len(outs)
    n_t = sum(1 if term[0] is None else 2 for term in terms)

    def body(*refs):
        t_refs, r_refs = list(refs[:n_t]), refs[n_t:n_t + n_r]
        p_refs, o_refs = refs[n_t + n_r:n_t + n_r + n_p], refs[n_t + n_r + n_p:]
        i = pl.program_id(0)
        rows, small = [r[...] for r in r_refs], [p[...] for p in p_refs]
        made = None if prologue is None else prologue(*rows, *small)
        acc = None
        for term in terms:
            a = made if term[0] is None else t_refs.pop(0)[...].astype(BF16)
            dims = (((1,), (1 if term[2] else 0,)), ((), ()))
            part = lax.dot_general(a, t_refs.pop(0)[...].astype(BF16), dims, preferred_element_type=F32)
            acc = part if acc is None else acc + part
        vals = epilogue(acc, *rows, *small) if prologue is None else epilogue(acc, *rows, *small, made)
        for ref, val, out in zip(o_refs, vals, outs):
            if out[0] == "rows":
                ref[...] = val.astype(ref.dtype)
            else:
                @pl.when(i == 0)
                def _(ref=ref):
                    ref[...] = jnp.zeros_like(ref)

                ref[...] += val

    in_specs, args = [], []
    for term in terms:
        a, b = term[0], term[1]
        if a is None:
            in_specs.append(_resident(b.shape, lambda i: (0, 0)))
            args.append(b)
            continue
        b_rows = b.shape[0] if term[2] or len(term) < 4 else a.shape[-1]
        b_blk = 0 if len(term) < 4 else term[3] // b_rows
        if len(term) == 5:
            a_spec = pl.BlockSpec((None, tm, a.shape[2]), lambda i, c=term[4]: (c, i, 0))
        else:
            a_spec = pl.BlockSpec((tm, a.shape[1]), lambda i: (i, 0))
        in_specs += [a_spec, _resident((b_rows, b.shape[1]), lambda i, b_blk=b_blk: (b_blk, 0))]
        args += [a, b]
    for arr, width, col_blk in row_inputs:
        in_specs.append(pl.BlockSpec((tm, width), lambda i, col_blk=col_blk: (i, col_blk)))
        args.append(arr)
    for p in params:
        in_specs.append(pl.BlockSpec(p.shape, lambda i: (0, 0)))
        args.append(p)
    out_specs, out_shape = [], []
    for out in outs:
        if out[0] == "rows":
            out_specs.append(pl.BlockSpec((tm, out[1]), lambda i: (i, 0)))
            out_shape.append(jax.ShapeDtypeStruct((s, out[1]), out[2]))
        else:
            out_specs.append(pl.BlockSpec(out[1], lambda i: (0, 0)))
            out_shape.append(jax.ShapeDtypeStruct(out[1], F32))
    return pl.pallas_call(
        body, name=name, grid=(steps,), in_specs=in_specs, out_specs=out_specs, out_shape=out_shape,
        compiler_params=_params(("arbitrary",)),
    )(*args)


def _rms_stats(x):
    r = lax.rsqrt(jnp.mean(x * x, axis=-1, keepdims=True) + EPS)
    return r, x * r


def _gated(o_parts, gate):
    o = o_parts[0] if len(o_parts) == 1 else jnp.concatenate(o_parts, axis=1)
    return (o * (gate * _sigmoid(gate))).astype(BF16)


def _residual_norm_epilogue(r, x, g):
    x1 = x + r
    _, xh = _rms_stats(x1)
    return x1, xh * g


def _rms_bwd_epilogue(dy, x, add, g):
    r, xh = _rms_stats(x)
    dxh = dy * g
    dx = r * (dxh - xh * jnp.mean(dxh * xh, axis=-1, keepdims=True)) + add
    return dx, jnp.sum(dy * xh, axis=0, keepdims=True)


def _loss_epilogue(r, x1, target, g):
    rs, xh = _rms_stats(x1 + r)
    err = xh * g - target
    loss = jnp.broadcast_to(0.5 * jnp.sum(jnp.mean(err * err, axis=-1, keepdims=True)), (8, LANES))
    dy = err * (1.0 / D_MODEL)
    dxh = dy * g
    dx = rs * (dxh - xh * jnp.mean(dxh * xh, axis=-1, keepdims=True))
    return dx, loss, jnp.sum(dy * xh, axis=0, keepdims=True)


def _gate_bwd_epilogue(widths):
    def epilogue(d, *rows):
        o_parts, gt = rows[:-1], rows[-1]
        o = o_parts[0] if len(o_parts) == 1 else jnp.concatenate(o_parts, axis=1)
        sg = _sigmoid(gt)
        do = d * (gt * sg)
        d_gate = d * o * (sg * (1.0 + gt * (1.0 - sg)))
        cuts = [sum(widths[:k]) for k in range(len(widths) + 1)]
        return tuple(do[:, cuts[k]:cuts[k + 1]] for k in range(len(widths))) + (d_gate,)

    return epilogue


def _rot_half(x):
    lane = lax.broadcasted_iota(jnp.int32, x.shape, 1)
    return jnp.where(lane < 80, pltpu.roll(x, LANES - 16, axis=1), pltpu.roll(x, 16, axis=1))


def _rot_half_t(g):
    lane = lax.broadcasted_iota(jnp.int32, g.shape, 1)
    lo = (lane >= MLA_NOPE) & (lane < MLA_NOPE + MLA_ROPE // 2)
    hi = (lane >= MLA_NOPE + MLA_ROPE // 2) & (lane < MLA_NOPE + MLA_ROPE)
    return jnp.where(lo, pltpu.roll(g, LANES - 16, axis=1), jnp.where(hi, pltpu.roll(g, 16, axis=1), 0.0))


def _rope_fwd(qp, kvp, z0a, cos_t, sin_t, *, name):
    s = qp.shape[0]
    tm = _tile(s, ROW_TILES)
    hw = MLA_HEADS * LANES

    def body(q_ref, k_ref, kpe_ref, c_ref, s_ref, qm_ref, km_ref):
        c = c_ref[...]
        sn = s_ref[...]
        kpe = kpe_ref[...]
        kpe_r = (kpe * c + _rot_half(kpe) * sn).astype(BF16)
        lane = lax.broadcasted_iota(jnp.int32, kpe.shape, 1)
        for h in range(MLA_HEADS):
            sl = slice(h * LANES, (h + 1) * LANES)
            qh = q_ref[:, sl]
            qm_ref[:, sl] = (qh * c + _rot_half(qh) * sn).astype(BF16)
            km_ref[:, sl] = jnp.where(lane < MLA_NOPE, k_ref[:, sl], kpe_r)

    return pl.pallas_call(
        body, name=name, grid=(s // tm,),
        in_specs=[pl.BlockSpec((tm, hw), lambda i: (i, 0)), pl.BlockSpec((tm, hw), lambda i: (i, 0)),
                  pl.BlockSpec((tm, LANES), lambda i: (i, 11)),
                  pl.BlockSpec((tm, LANES), lambda i: (i, 0)), pl.BlockSpec((tm, LANES), lambda i: (i, 0))],
        out_specs=[pl.BlockSpec((tm, hw), lambda i: (i, 0)), pl.BlockSpec((tm, hw), lambda i: (i, 0))],
        out_shape=[jax.ShapeDtypeStruct((s, hw), BF16), jax.ShapeDtypeStruct((s, hw), BF16)],
        compiler_params=_params(("parallel",)),
    )(qp, kvp, z0a, cos_t, sin_t)


def _rope_bwd(dqm, dkm, dvm, cos_t, sin_t, *, name):
    s = dqm.shape[0]
    tm = _tile(s, ROW_TILES)
    hw = MLA_HEADS * LANES
    vw = MLA_HEADS * MLA_V

    def body(dq_ref, dk_ref, dv_ref, c_ref, s_ref, dqp_ref, dkv_ref, dkpe_ref):
        c = c_ref[...]
        sn = s_ref[...]
        ksum = jnp.zeros((tm, LANES), F32)
        for h in range(MLA_HEADS):
            sl = slice(h * LANES, (h + 1) * LANES)
            dq = dq_ref[:, sl]
            dqp_ref[:, sl] = (dq * c + _rot_half_t(dq * sn)).astype(BF16)
            dk = dk_ref[:, sl]
            dkv_ref[:, sl] = dk.astype(BF16)
            ksum = ksum + dk
        dkv_ref[:, hw:] = dv_ref[...]
        lane = lax.broadcasted_iota(jnp.int32, ksum.shape, 1)
        dkpe = ksum * c + _rot_half_t(ksum * sn)
        dkpe_ref[...] = jnp.where((lane >= MLA_NOPE) & (lane < MLA_NOPE + MLA_ROPE), dkpe, 0.0).astype(BF16)

    return pl.pallas_call(
        body, name=name, grid=(s // tm,),
        in_specs=[pl.BlockSpec((tm, hw), lambda i: (i, 0)), pl.BlockSpec((tm, hw), lambda i: (i, 0)),
                  pl.BlockSpec((tm, vw), lambda i: (i, 0)),
                  pl.BlockSpec((tm, LANES), lambda i: (i, 0)), pl.BlockSpec((tm, LANES), lambda i: (i, 0))],
        out_specs=[pl.BlockSpec((tm, hw), lambda i: (i, 0)), pl.BlockSpec((tm, hw + vw), lambda i: (i, 0)),
                   pl.BlockSpec((tm, LANES), lambda i: (i, 0))],
        out_shape=[jax.ShapeDtypeStruct((s, hw), BF16), jax.ShapeDtypeStruct((s, hw + vw), BF16),
                   jax.ShapeDtypeStruct((s, LANES), BF16)],
        compiler_params=_params(("parallel",)),
    )(dqm, dkm, dvm, cos_t, sin_t)


def _head_mask(shape, a):
    lane = lax.broadcasted_iota(jnp.int32, shape, 1)
    return (lane >= 64 * a) & (lane < 64 * (a + 1))


_NT = (((1,), (1,)), ((), ()))
LOG2E = 1.4426950408889634


def _stack_heads(tile, hw):
    lane = lax.broadcasted_iota(jnp.int32, tile.shape, 1)
    z = jnp.zeros_like(tile)
    return jnp.concatenate([jnp.where(lane < hw, tile, z), jnp.where(lane >= hw, tile, z)], axis=0)


def _stacked_rows(r0, r1, t):
    n = r0.shape[-1]
    return jnp.concatenate([jnp.broadcast_to(r0, (t, n)), jnp.broadcast_to(r1, (t, n))], axis=0)


def _resident(block, index_map):
    return pl.BlockSpec(block, index_map, pipeline_mode=pl.Buffered(1))


def _fwd_tile(s):
    return ATT_T_FWD if s % ATT_T_FWD == 0 else min(ATT_T, s)


def _flash_fwd(q, k, v, bias, *, n_pairs, hw, q_off, k_off, v_off, scale, name, rider=None):
    s = q.shape[0]
    t = _fwd_tile(s)
    nb = s // t
    qw = 2 * hw
    has_bias = bias is not None
    c1 = scale * LOG2E

    def body(*refs):
        refs, ride_refs = _split_rider(refs, rider, n_in=4 if has_bias else 3, n_out=2)
        if has_bias:
            q_ref, k_ref, v_ref, b_ref, o_ref, lse_ref, vt_ref, bcol_ref = refs
        else:
            q_ref, k_ref, v_ref, o_ref, lse_ref, vt_ref = refs
            b_ref = bcol_ref = None
        _ride_start(rider, ride_refs, pl.program_id(0) == 0)
        row = lax.broadcasted_iota(jnp.int32, (t, t), 0)
        col = lax.broadcasted_iota(jnp.int32, (t, t), 1)
        cmask_t = jnp.concatenate([row <= col, row <= col], axis=1)
        lane_lt64 = lax.broadcasted_iota(jnp.int32, (t, LANES), 1) < 64

        def as_column(r):
            return jnp.broadcast_to(r, (8, r.shape[1])).T[:, 0:1]

        def v_block(j, _):
            c0 = pl.multiple_of(j * t, t)
            vt_ref[j] = v_ref[pl.ds(c0, t), :].astype(F32).T.astype(BF16)
            if has_bias:
                for a in range(2):
                    bcol_ref[a, pl.ds(c0, t), :] = as_column(b_ref[0, a, j])
            return 0

        lax.fori_loop(0, nb, v_block, 0)

        def stacked_queries(i):
            return _stack_heads(q_ref[pl.ds(pl.multiple_of(i * t, t), t), :], hw).astype(F32).T.astype(BF16)

        def kv_step(j, carry, qs_t, masked):
            m, l, acc = carry
            rows = pl.ds(pl.multiple_of(j * t, t), t)
            sc = jnp.dot(k_ref[rows, :], qs_t, preferred_element_type=F32) * c1
            if has_bias:
                sc = sc + jnp.concatenate([jnp.broadcast_to(bcol_ref[0, rows, :], (t, t)),
                                           jnp.broadcast_to(bcol_ref[1, rows, :], (t, t))], axis=1)
            if masked:
                sc = jnp.where(cmask_t, sc, NEG_INF)
            m_new = jnp.maximum(m, jnp.max(sc, axis=0, keepdims=True))
            alpha = jnp.exp2(m - m_new)
            p = jnp.exp2(sc - m_new)
            l_new = alpha * l + jnp.sum(p, axis=0, keepdims=True)
            pv = jnp.dot(vt_ref[j], p.astype(BF16), preferred_element_type=F32)
            return m_new, l_new, alpha * acc + pv

        def finish(i, carry):
            m, l, acc = carry
            r0 = pl.multiple_of(i * t, t)
            out = (acc / l).T
            lse2 = as_column(m + jnp.log2(l))
            lse_ref[0, 0, pl.ds(r0, t), :] = lse2[:t]
            lse_ref[0, 1, pl.ds(r0, t), :] = lse2[t:]
            o_ref[pl.ds(r0, t), :] = jnp.where(lane_lt64, out[:t], out[t:])

        init = (jnp.full((1, 2 * t), NEG_INF, F32), jnp.zeros((1, 2 * t), F32), jnp.zeros((LANES, 2 * t), F32))

        def q_block(i, _):
            qs_t = stacked_queries(i)
            carry = lax.fori_loop(0, i, lambda j, c: kv_step(j, c, qs_t, False), init)
            finish(i, kv_step(i, carry, qs_t, True))
            return 0

        lax.fori_loop(0, nb, q_block, 0)
        _ride_wait(rider, ride_refs, pl.program_id(0) == n_pairs - 1)

    in_specs = [_resident((s, qw), lambda p: (0, q_off + p)), _resident((s, qw), lambda p: (0, k_off + p)),
                _resident((s, LANES), lambda p: (0, v_off + p))]
    args = [q, k, v]
    if has_bias:
        in_specs.append(_resident((1, 2, nb, 1, t), lambda p: (p, 0, 0, 0, 0)))
        args.append(bias)
    out_specs = [pl.BlockSpec((s, LANES), lambda p: (0, p)), pl.BlockSpec((1, 2, s, 1), lambda p: (p, 0, 0, 0))]
    out_shape = [jax.ShapeDtypeStruct((s, n_pairs * LANES), F32), jax.ShapeDtypeStruct((n_pairs, 2, s, 1), F32)]
    scratch = [pltpu.VMEM((nb, LANES, t), BF16)] + ([pltpu.VMEM((2, s, 1), F32)] if has_bias else [])
    scratch += _add_rider(rider, in_specs, args, out_specs, out_shape)
    return pl.pallas_call(
        body, name=name, grid=(n_pairs,), in_specs=in_specs, out_specs=out_specs, out_shape=out_shape,
        scratch_shapes=scratch,
        compiler_params=_params(("parallel",) if rider is None else ("arbitrary",)),
    )(*args)


def _flash_bwd(q, k, v, do, o, lse, bias, *, n_pairs, hw, q_off, k_off, v_off, scale, qk_dtype, name, rider=None,
               stacked=False):
    s = q.shape[0]
    t = min(ATT_T, s)
    nb = s // t
    qw = 2 * hw
    has_bias = bias is not None
    c1 = scale * LOG2E

    def body(*refs):
        n_grads = 1 if stacked else 3
        refs, ride_refs = _split_rider(refs, rider, n_in=7 if has_bias else 6, n_out=n_grads + (2 if has_bias else 0))
        if stacked:
            refs = list(refs)
            n_in = 7 if has_bias else 6
            refs[n_in:n_in + 1] = [refs[n_in].at[0], refs[n_in].at[1], refs[n_in].at[2]]
        if has_bias:
            (q_ref, k_ref, v_ref, do_ref, o_ref, lse_ref, b_ref, dq_ref, dk_ref, dv_ref, db_ref, dr_ref,
             dkt_ref, dvt_ref) = refs
            db_ref[...] = jnp.zeros_like(db_ref)
        else:
            q_ref, k_ref, v_ref, do_ref, o_ref, lse_ref, dq_ref, dk_ref, dv_ref, dkt_ref, dvt_ref = refs
            b_ref = db_ref = dr_ref = None
        _ride_start(rider, ride_refs, pl.program_id(0) == 0)
        dkt_ref[...] = jnp.zeros_like(dkt_ref)
        dvt_ref[...] = jnp.zeros_like(dvt_ref)
        causal = lax.broadcasted_iota(jnp.int32, (t, t), 1) <= lax.broadcasted_iota(jnp.int32, (t, t), 0)
        cmask = jnp.concatenate([causal, causal], axis=0)
        lane_lt_hw = lax.broadcasted_iota(jnp.int32, (t, qw), 1) < hw

        def q_block(i, _):
            r0 = pl.multiple_of(i * t, t)
            qs = _stack_heads(q_ref[pl.ds(r0, t), :], hw)
            dos = _stack_heads(do_ref[pl.ds(r0, t), :], 64)
            ot = o_ref[pl.ds(r0, t), :]
            delta = jnp.sum(dos * jnp.concatenate([ot, ot], axis=0), axis=-1, keepdims=True)
            lse2 = jnp.concatenate([lse_ref[0, 0, pl.ds(r0, t), :], lse_ref[0, 1, pl.ds(r0, t), :]], axis=0)
            dosb = dos.astype(BF16)
            dos_t = dos.T.astype(BF16)
            qs_t = qs.astype(F32).T.astype(BF16)

            def kv_step(j, carry, masked):
                dq, rsum = carry
                c0 = pl.multiple_of(j * t, t)
                kt = k_ref[pl.ds(c0, t), :]
                vt = v_ref[pl.ds(c0, t), :]
                sc = lax.dot_general(qs, kt, _NT, preferred_element_type=F32) * c1
                if has_bias:
                    sc = sc + _stacked_rows(b_ref[0, 0, j], b_ref[0, 1, j], t)
                if masked:
                    sc = jnp.where(cmask, sc, NEG_INF)
                p = jnp.exp2(sc - lse2)
                dp = lax.dot_general(dosb, vt, _NT, preferred_element_type=F32)
                ds = p * (dp - delta)
                dsb = ds.astype(BF16)
                pb = p.astype(BF16)
                if hw == LANES:
                    dvt_ref[j] += jnp.concatenate(
                        [jnp.dot(dos_t[:64, :t], pb[:t], preferred_element_type=F32),
                         jnp.dot(dos_t[64:, t:], pb[t:], preferred_element_type=F32)], axis=0)
                    dkt_ref[j] += jnp.concatenate(
                        [jnp.dot(qs_t[:hw, :t], dsb[:t], preferred_element_type=F32),
                         jnp.dot(qs_t[hw:, t:], dsb[t:], preferred_element_type=F32)], axis=0)
                else:
                    dvt_ref[j] += jnp.dot(dos_t, pb, preferred_element_type=F32)
                    dkt_ref[j] += jnp.dot(qs_t, dsb, preferred_element_type=F32)
                if has_bias:
                    db_ref[0, 0, j] += jnp.sum(ds[:t], axis=0, keepdims=True)
                    db_ref[0, 1, j] += jnp.sum(ds[t:], axis=0, keepdims=True)
                    rsum = rsum + jnp.sum(ds, axis=-1, keepdims=True)
                return dq + jnp.dot(dsb, kt, preferred_element_type=F32), rsum

            init = (jnp.zeros((2 * t, qw), F32), jnp.zeros((2 * t, 1), F32))
            carry = lax.fori_loop(0, i, functools.partial(kv_step, masked=False), init)
            dq, rsum = kv_step(i, carry, True)
            dq = dq * scale
            dq_ref[pl.ds(r0, t), :] = jnp.where(lane_lt_hw, dq[:t], dq[t:]).astype(qk_dtype)
            if has_bias:
                rsum_row = jnp.broadcast_to(rsum, (2 * t, LANES)).T[0:1]
                dr_ref[0, 0, i] = rsum_row[:, :t]
                dr_ref[0, 1, i] = rsum_row[:, t:]
            return 0

        lax.fori_loop(0, nb, q_block, 0)

        def k_block(j, _):
            c0 = pl.multiple_of(j * t, t)
            dk_ref[pl.ds(c0, t), :] = (dkt_ref[j].T * scale).astype(qk_dtype)
            dv_ref[pl.ds(c0, t), :] = dvt_ref[j].T.astype(BF16)
            return 0

        lax.fori_loop(0, nb, k_block, 0)
        _ride_wait(rider, ride_refs, pl.program_id(0) == n_pairs - 1)

    in_specs = [_resident((s, qw), lambda p: (0, q_off + p)), _resident((s, qw), lambda p: (0, k_off + p)),
                _resident((s, LANES), lambda p: (0, v_off + p)),
                _resident((s, LANES), lambda p: (0, p)), _resident((s, LANES), lambda p: (0, p)),
                _resident((1, 2, s, 1), lambda p: (p, 0, 0, 0))]
    args = [q, k, v, do, o, lse]
    if stacked:
        assert qw == LANES and qk_dtype == BF16
        out_specs = [pl.BlockSpec((3, s, LANES), lambda p: (0, 0, p))]
        out_shape = [jax.ShapeDtypeStruct((3, s, n_pairs * LANES), BF16)]
    else:
        out_specs = [pl.BlockSpec((s, qw), lambda p: (0, p)), pl.BlockSpec((s, qw), lambda p: (0, p)),
                     pl.BlockSpec((s, LANES), lambda p: (0, p))]
        out_shape = [jax.ShapeDtypeStruct((s, n_pairs * qw), qk_dtype), jax.ShapeDtypeStruct((s, n_pairs * qw), qk_dtype),
                     jax.ShapeDtypeStruct((s, n_pairs * LANES), BF16)]
    if has_bias:
        in_specs.append(_resident((1, 2, nb, 1, t), lambda p: (p, 0, 0, 0, 0)))
        args.append(bias)
        for _ in range(2):
            out_specs.append(pl.BlockSpec((1, 2, nb, 1, t), lambda p: (p, 0, 0, 0, 0)))
            out_shape.append(jax.ShapeDtypeStruct((n_pairs, 2, nb, 1, t), F32))
    scratch = [pltpu.VMEM((nb, qw, t), F32), pltpu.VMEM((nb, LANES, t), F32)]
    scratch += _add_rider(rider, in_specs, args, out_specs, out_shape)
    return pl.pallas_call(
        body, name=name, grid=(n_pairs,), in_specs=in_specs, out_specs=out_specs, out_shape=out_shape,
        scratch_shapes=scratch,
        compiler_params=_params(("parallel",) if rider is None else ("arbitrary",)),
    )(*args)


def _alibi_slope(h):
    return 2.0 ** (-8.0 * (h + 1.0) / SWA_HEADS)


SWA_ROWS = 512
SWA_SCALE = SWA_DIM ** -0.5


def _swa_geometry(i):
    w = WINDOW
    r0 = pl.multiple_of(i * w, w)
    b0 = pl.multiple_of(jnp.maximum(i - 1, 0) * w, w)
    row = lax.broadcasted_iota(jnp.int32, (w, 2 * w), 0)
    col = lax.broadcasted_iota(jnp.int32, (w, 2 * w), 1)
    dist = row - col + (r0 - b0)
    valid = (dist >= 0) & (dist < w)
    return r0, b0, dist.astype(F32), valid


def _swa_q_head(qblk, h):
    kv = h // (SWA_HEADS // SWA_KV_HEADS)
    if h % 2 != kv:
        qblk = pltpu.roll(qblk, 64, axis=1)
    return jnp.where(_head_mask(qblk.shape, kv), qblk, 0.0)


SWA_GROUP = SWA_HEADS // SWA_KV_HEADS


def _swa_stack(ref, rs, grp):
    parts = []
    for a in range(SWA_GROUP):
        h = SWA_GROUP * grp + a
        parts.append(_swa_q_head(ref[rs, (h // 2) * LANES:(h // 2 + 1) * LANES].astype(F32), h))
    return jnp.concatenate(parts, axis=0)


def _swa_unstack(x, grp):
    tiles = []
    for a in range(SWA_GROUP):
        h = SWA_GROUP * grp + a
        tile = x[a * WINDOW:(a + 1) * WINDOW]
        tiles.append(pltpu.roll(tile, 64, axis=1) if h % 2 != grp else tile)
    return tiles


def _swa_head_column(vals):
    return jnp.concatenate([jnp.full((WINDOW, 1), v, F32) for v in vals], axis=0)


def _swa_logits(qs, kb, dist, valid, grp):
    slopes = _swa_head_column([_alibi_slope(SWA_GROUP * grp + a) for a in range(SWA_GROUP)])
    dist4 = jnp.concatenate([dist] * SWA_GROUP, axis=0)
    valid4 = jnp.concatenate([valid] * SWA_GROUP, axis=0)
    sc = lax.dot_general(qs, kb, _NT, preferred_element_type=F32) * SWA_SCALE - slopes * dist4
    return jnp.where(valid4, sc, NEG_INF)


def _swa_merge_heads(tiles):
    lt64 = lax.broadcasted_iota(jnp.int32, (WINDOW, LANES), 1) < 64
    return jnp.concatenate([jnp.where(lt64, tiles[2 * b], tiles[2 * b + 1]) for b in range(SWA_HEADS // 2)], axis=1)


def _swa_fwd(z0b, sinks, *, name):
    s = z0b.shape[0]
    w = WINDOW
    rows = min(SWA_ROWS, s)
    per_step = rows // w
    qcols = SWA_HEADS * SWA_DIM

    def body(sink_ref, q_ref, k_ref, v_ref, o_ref, lse_ref):
        g = pl.program_id(0)
        for ii in range(per_step):
            rs = slice(ii * w, (ii + 1) * w)
            r0, b0, dist, valid = _swa_geometry(g * per_step + ii)
            kb = k_ref[pl.ds(b0, 2 * w), :]
            vb = v_ref[pl.ds(b0, 2 * w), :]
            o_tiles = []
            for h in range(SWA_HEADS):
                kv = h // SWA_GROUP
                qh = _swa_q_head(q_ref[rs, (h // 2) * LANES:(h // 2 + 1) * LANES].astype(F32), h).astype(BF16)
                sc = lax.dot_general(qh, kb, _NT, preferred_element_type=F32) * SWA_SCALE - _alibi_slope(h) * dist
                sc = jnp.where(valid, sc, NEG_INF)
                sink = sink_ref[0, h]
                m = jnp.maximum(jnp.max(sc, axis=-1, keepdims=True), sink)
                p = jnp.exp(sc - m)
                l = jnp.sum(p, axis=-1, keepdims=True) + jnp.exp(sink - m)
                oh = jnp.dot(p.astype(BF16), vb, preferred_element_type=F32) / l
                o_tiles.append(pltpu.roll(oh, 64, axis=1) if h % 2 != kv else oh)
                lse_ref[h, rs, :] = m + jnp.log(l)
            o_ref[rs, :] = _swa_merge_heads(o_tiles)

    return pl.pallas_call(
        body, name=name, grid=(s // rows,),
        in_specs=[pl.BlockSpec(memory_space=pltpu.SMEM),
                  pl.BlockSpec((rows, qcols), lambda g: (g, 0)),
                  pl.BlockSpec((s, LANES), lambda g: (0, 4)), pl.BlockSpec((s, LANES), lambda g: (0, 5))],
        out_specs=[pl.BlockSpec((rows, qcols), lambda g: (g, 0)), pl.BlockSpec((SWA_HEADS, rows, 1), lambda g: (0, g, 0))],
        out_shape=[jax.ShapeDtypeStruct((s, qcols), F32), jax.ShapeDtypeStruct((SWA_HEADS, s, 1), F32)],
        compiler_params=_params(("parallel",)),
    )(sinks, z0b, z0b, z0b)


def _swa_bwd(z0b, sinks, do, o, lse, *, name):
    s = z0b.shape[0]
    w = WINDOW
    rows = min(SWA_ROWS, s)
    per_step = rows // w
    qcols = SWA_HEADS * SWA_DIM
    nblk = s // w

    def body(sink_ref, q_ref, k_ref, v_ref, do_ref, o_ref, lse_ref, dq_ref, dkt_ref, dvt_ref, dsink_ref):
        g = pl.program_id(0)

        @pl.when(g == 0)
        def _():
            dkt_ref[...] = jnp.zeros_like(dkt_ref)
            dvt_ref[...] = jnp.zeros_like(dvt_ref)
            dsink_ref[...] = jnp.zeros_like(dsink_ref)

        for ii in range(per_step):
            i = g * per_step + ii
            rs = slice(ii * w, (ii + 1) * w)
            r0, b0, dist, valid = _swa_geometry(i)
            j0 = jnp.maximum(i - 1, 0)
            kb = k_ref[pl.ds(b0, 2 * w), :]
            vb = v_ref[pl.ds(b0, 2 * w), :]
            dq_tiles = []
            for grp in range(SWA_KV_HEADS):
                heads = [SWA_GROUP * grp + a for a in range(SWA_GROUP)]
                qs32 = _swa_stack(q_ref, rs, grp)
                dos32 = _swa_stack(do_ref, rs, grp)
                delta = jnp.sum(dos32 * _swa_stack(o_ref, rs, grp), axis=-1, keepdims=True)
                lse = jnp.concatenate([lse_ref[h, rs, :] for h in heads], axis=0)
                sink = _swa_head_column([sink_ref[0, h] for h in heads])
                p = jnp.exp(_swa_logits(qs32.astype(BF16), kb, dist, valid, grp) - lse)
                dp = lax.dot_general(dos32.astype(BF16), vb, _NT, preferred_element_type=F32)
                ds = p * (dp - delta)
                dsb = ds.astype(BF16)
                d_sink = jnp.exp(sink - lse) * delta
                for a, h in enumerate(heads):
                    dsink_ref[h:h + 1, :] += jnp.broadcast_to(-jnp.sum(d_sink[a * w:(a + 1) * w]), (1, LANES))
                dvt = jnp.dot(dos32.T.astype(BF16), p.astype(BF16), preferred_element_type=F32)
                dkt = jnp.dot(qs32.T.astype(BF16), dsb, preferred_element_type=F32) * SWA_SCALE
                dvt_ref[j0] += dvt[:, :w]
                dvt_ref[j0 + 1] += dvt[:, w:]
                dkt_ref[j0] += dkt[:, :w]
                dkt_ref[j0 + 1] += dkt[:, w:]
                dq_tiles += _swa_unstack(jnp.dot(dsb, kb, preferred_element_type=F32) * SWA_SCALE, grp)
            dq_ref[rs, :] = _swa_merge_heads(dq_tiles)

    return pl.pallas_call(
        body, name=name, grid=(s // rows,),
        in_specs=[pl.BlockSpec(memory_space=pltpu.SMEM),
                  pl.BlockSpec((rows, qcols), lambda g: (g, 0)),
                  pl.BlockSpec((s, LANES), lambda g: (0, 4)), pl.BlockSpec((s, LANES), lambda g: (0, 5)),
                  pl.BlockSpec((rows, qcols), lambda g: (g, 0)), pl.BlockSpec((rows, qcols), lambda g: (g, 0)),
                  pl.BlockSpec((SWA_HEADS, rows, 1), lambda g: (0, g, 0))],
        out_specs=[pl.BlockSpec((rows, qcols), lambda g: (g, 0)),
                   pl.BlockSpec((nblk, LANES, w), lambda g: (0, 0, 0)),
                   pl.BlockSpec((nblk, LANES, w), lambda g: (0, 0, 0)),
                   pl.BlockSpec((SWA_HEADS, LANES), lambda g: (0, 0))],
        out_shape=[jax.ShapeDtypeStruct((s, qcols), F32),
                   jax.ShapeDtypeStruct((nblk, LANES, w), F32), jax.ShapeDtypeStruct((nblk, LANES, w), F32),
                   jax.ShapeDtypeStruct((SWA_HEADS, LANES), F32)],
        compiler_params=_params(("arbitrary",)),
    )(sinks, z0b, z0b, z0b, do, o, lse)


CUM_T = 256


def _split3(x):
    hi = x.astype(BF16)
    r1 = x - hi.astype(F32)
    mid = r1.astype(BF16)
    lo = (r1 - mid.astype(F32)).astype(BF16)
    return hi, mid, lo


def _tri_dot(tri, x):
    hi, mid, lo = _split3(x)
    out = jnp.dot(tri, hi, preferred_element_type=F32)
    out = out + jnp.dot(tri, mid, preferred_element_type=F32)
    return out + jnp.dot(tri, lo, preferred_element_type=F32)


def _logf_fwd(zf, bf, *, name):
    s = zf.shape[0]
    t = CUM_T
    nb = s // t

    def body(z_ref, b_ref, c_ref, carry_ref):
        i = pl.program_id(0)

        @pl.when(i == 0)
        def _():
            carry_ref[...] = jnp.zeros_like(carry_ref)

        x = z_ref[...] + b_ref[...]
        lf = jnp.minimum(x, 0.0) - jnp.log(1.0 + jnp.exp(-jnp.abs(x)))
        row = lax.broadcasted_iota(jnp.int32, (t, t), 0)
        col = lax.broadcasted_iota(jnp.int32, (t, t), 1)
        tri = jnp.where(col <= row, 1.0, 0.0).astype(BF16)
        c = _tri_dot(tri, lf) + carry_ref[...]
        c_ref[...] = c
        carry_ref[...] = c[t - 1:t, :]

    return pl.pallas_call(
        body, name=name, grid=(nb,),
        in_specs=[pl.BlockSpec((t, LANES), lambda i: (i, 0)), pl.BlockSpec((1, LANES), lambda i: (0, 0))],
        out_specs=pl.BlockSpec((t, LANES), lambda i: (i, 0)),
        out_shape=jax.ShapeDtypeStruct((s, LANES), F32),
        scratch_shapes=[pltpu.VMEM((1, LANES), F32)],
        compiler_params=_params(("arbitrary",)),
    )(zf, bf)


def _logf_bwd(dc, zf, bf, *, name):
    s = zf.shape[0]
    t = CUM_T
    nb = s // t

    def body(dc_ref, z_ref, b_ref, dz_ref, db_ref, carry_ref):
        i = pl.program_id(0)

        @pl.when(i == 0)
        def _():
            carry_ref[...] = jnp.zeros_like(carry_ref)
            db_ref[...] = jnp.zeros_like(db_ref)

        row = lax.broadcasted_iota(jnp.int32, (t, t), 0)
        col = lax.broadcasted_iota(jnp.int32, (t, t), 1)
        tri = jnp.where(col >= row, 1.0, 0.0).astype(BF16)
        dlf = _tri_dot(tri, dc_ref[...]) + carry_ref[...]
        carry_ref[...] = dlf[0:1, :]
        x = z_ref[...] + b_ref[...]
        dz = dlf * _sigmoid(-x)
        dz_ref[...] = dz.astype(BF16)
        db_ref[...] += jnp.sum(dz, axis=0, keepdims=True)

    return pl.pallas_call(
        body, name=name, grid=(nb,),
        in_specs=[pl.BlockSpec((t, LANES), lambda i: (nb - 1 - i, 0)), pl.BlockSpec((t, LANES), lambda i: (nb - 1 - i, 0)),
                  pl.BlockSpec((1, LANES), lambda i: (0, 0))],
        out_specs=[pl.BlockSpec((t, LANES), lambda i: (nb - 1 - i, 0)), pl.BlockSpec((1, LANES), lambda i: (0, 0))],
        out_shape=[jax.ShapeDtypeStruct((s, LANES), BF16), jax.ShapeDtypeStruct((1, LANES), F32)],
        scratch_shapes=[pltpu.VMEM((1, LANES), F32)],
        compiler_params=_params(("arbitrary",)),
    )(dc, zf, bf)


def _sum_pieces(p_ref):
    g = p_ref[0].astype(F32)
    for k in range(1, N_DEV):
        g = g + p_ref[k].astype(F32)
    return g


def _adam_update(g, w, m, v):
    bc1 = 1.0 - ADAM_B1 ** ADAM_STEP
    bc2 = 1.0 - ADAM_B2 ** ADAM_STEP
    nm = ADAM_B1 * m + (1.0 - ADAM_B1) * g
    nv = ADAM_B2 * v + (1.0 - ADAM_B2) * (g * g)
    m_hat = nm / bc1
    v_hat = nv / bc2
    return -ADAM_LR * (m_hat / (jnp.sqrt(v_hat) + ADAM_EPS) + ADAM_WD * w), nm, nv


def _adamw(pieces, w, m, v, *, name):
    rows, cols = w.shape
    tr = _tile(rows, (RB1, RB0, SMALL_ROWS))

    def body(p_ref, w_ref, m_ref, v_ref, g_ref, d_ref, nm_ref, nv_ref):
        g = _sum_pieces(p_ref)
        g_ref[...] = g
        d_ref[...], nm_ref[...], nv_ref[...] = _adam_update(g, w_ref[...], m_ref[...], v_ref[...])

    spec = pl.BlockSpec((tr, cols), lambda i: (i, 0))
    shape = jax.ShapeDtypeStruct((rows, cols), F32)
    return pl.pallas_call(
        body, name=name, grid=(rows // tr,),
        in_specs=[pl.BlockSpec((N_DEV, tr, cols), lambda i: (0, i, 0)), spec, spec, spec],
        out_specs=[spec, spec, spec, spec], out_shape=[shape, shape, shape, shape],
        compiler_params=_params(("parallel",)),
    )(pieces, w, m, v)


def _sum8(pieces, rows, *, name):
    cols = pieces.shape[2]
    tr = _tile(rows, (176, 96))

    def body(p_ref, g_ref):
        g_ref[...] = _sum_pieces(p_ref)

    return pl.pallas_call(
        body, name=name, grid=(rows // tr,),
        in_specs=[pl.BlockSpec((N_DEV, tr, cols), lambda i: (0, i, 0))],
        out_specs=pl.BlockSpec((tr, cols), lambda i: (i, 0)),
        out_shape=jax.ShapeDtypeStruct((rows, cols), F32),
        compiler_params=_params(("parallel",)),
    )(pieces)


def _adamw_columns(g, w, m, v, *, name):
    n, _, k = w.shape
    tr = n // 2

    def body(g_ref, w_ref, m_ref, v_ref, d_ref, nm_ref, nv_ref):
        d_ref[...], nm_ref[...], nv_ref[...] = _adam_update(g_ref[...], w_ref[...], m_ref[...], v_ref[...])

    spec = pl.BlockSpec((tr, 1, k), lambda i: (i, 0, 0))
    shape = jax.ShapeDtypeStruct((n, 1, k), F32)
    return pl.pallas_call(
        body, name=name, grid=(n // tr,), in_specs=[spec, spec, spec, spec],
        out_specs=[spec, spec, spec], out_shape=[shape, shape, shape],
        compiler_params=_params(("parallel",)),
    )(g, w, m, v)


MESH = pl.DeviceIdType.MESH
ANY = pl.BlockSpec(memory_space=pl.ANY)


def _all_gather(shard, *, name):
    rows, lanes = shard.shape

    def body(x_ref, out_ref, send_sems, recv_sems, local_sem):
        x, y, c = lax.axis_index("x"), lax.axis_index("y"), lax.axis_index("c")
        me, sibling = (x, y, c), (x, y, 1 - c)
        chips = [(1 - x, y), (x, 1 - y), (1 - x, 1 - y)]

        def block(px, py, pc):
            return out_ref.at[4 * px + 2 * py + pc]

        def copy(k, blk, to, src=None):
            return pltpu.make_async_remote_copy(
                src_ref=block(*blk) if src is None else src, dst_ref=block(*blk),
                send_sem=send_sems.at[k], recv_sem=recv_sems.at[k], device_id=to, device_id_type=MESH)

        mine = pltpu.make_async_copy(x_ref, block(*me), local_sem)
        mine.start()
        first = [copy(0, me, sibling, src=x_ref)]
        first += [copy(1 + j, me, (*chip, c), src=x_ref) for j, chip in enumerate(chips)]
        for cp in first:
            cp.start()
        passed = [copy(4 + j, (*chip, c), sibling) for j, chip in enumerate(chips)]
        for j, chip in enumerate(chips):
            copy(1 + j, (*chip, c), me).wait_recv()
            passed[j].start()
        copy(0, sibling, me).wait_recv()
        for j, chip in enumerate(chips):
            copy(4 + j, (*chip, 1 - c), me).wait_recv()
        for cp in first + passed:
            cp.wait_send()
        mine.wait()

    return pl.pallas_call(
        body, name=name, out_shape=jax.ShapeDtypeStruct((N_DEV, rows, lanes), shard.dtype),
        in_specs=[ANY], out_specs=ANY,
        scratch_shapes=[pltpu.SemaphoreType.DMA((7,)), pltpu.SemaphoreType.DMA((7,)), pltpu.SemaphoreType.DMA(())],
    )(shard)


def _peer_copies(kind, src_ref, out_ref, send_sems, recv_sems, local_sem):
    x, y, c = lax.axis_index("x"), lax.axis_index("y"), lax.axis_index("c")
    me = 4 * x + 2 * y + c

    def src(idx):
        return src_ref.at[idx] if kind == "exchange" else src_ref

    mine = None if local_sem is None else pltpu.make_async_copy(src(me), out_ref.at[me], local_sem)
    copies = []
    for r in range(1, N_DEV):
        px = 1 - x if r & 4 else x
        py = 1 - y if r & 2 else y
        pc = 1 - c if r & 1 else c
        copies.append(pltpu.make_async_remote_copy(
            src_ref=src(4 * px + 2 * py + pc), dst_ref=out_ref.at[me],
            send_sem=send_sems.at[r - 1], recv_sem=recv_sems.at[r - 1],
            device_id=(px, py, pc), device_id_type=MESH))
    return mine, copies


PEER_SEMS = [pltpu.SemaphoreType.DMA((7,)), pltpu.SemaphoreType.DMA((7,)), pltpu.SemaphoreType.DMA(())]


HBM = pl.BlockSpec(memory_space=pltpu.HBM)
SEMAPHORES = pl.BlockSpec(memory_space=pltpu.SEMAPHORE)


def _peer_start(kind, arr, *, name):
    land = lax.empty((N_DEV,) + arr.shape[-2:], arr.dtype)

    def body(src_ref, land_ref, send_sems, recv_sems, src_thru, land_thru, token):
        _, copies = _peer_copies(kind, src_ref, land_ref, send_sems, recv_sems, None)
        for cp in copies:
            cp.start()
        token[...] = jnp.zeros_like(token)

    return pl.pallas_call(
        body, name=name,
        out_shape=(pltpu.SemaphoreType.DMA((N_DEV - 1,)), pltpu.SemaphoreType.DMA((N_DEV - 1,)),
                   pltpu.HBM(arr.shape, arr.dtype), pltpu.HBM(land.shape, land.dtype), jax.ShapeDtypeStruct((8, LANES), F32)),
        in_specs=(HBM, HBM), out_specs=(SEMAPHORES, SEMAPHORES, HBM, HBM, pl.BlockSpec(memory_space=pltpu.VMEM)),
        input_output_aliases={0: 2, 1: 3},
        compiler_params=pltpu.CompilerParams(has_side_effects=pltpu.SideEffectType.DATAFLOW_SIDE_EFFECTING),
    )(pltpu.with_memory_space_constraint(arr, pltpu.HBM), pltpu.with_memory_space_constraint(land, pltpu.HBM))


def _peer_wait(kind, send_sems, recv_sems, src_thru, land_thru, after, *, name):
    def body(src_ref, land_ref, send_sems, recv_sems, *_):
        _, copies = _peer_copies(kind, src_ref, land_ref, send_sems, recv_sems, None)
        for cp in copies:
            cp.wait_send()
            cp.wait_recv()

    return pl.pallas_call(
        body, name=name,
        out_shape=(pltpu.HBM(src_thru.shape, src_thru.dtype), pltpu.HBM(land_thru.shape, land_thru.dtype)),
        in_specs=(HBM, HBM, SEMAPHORES, SEMAPHORES) + (ANY,) * len(after), out_specs=(HBM, HBM),
        input_output_aliases={0: 0, 1: 1},
        compiler_params=pltpu.CompilerParams(has_side_effects=pltpu.SideEffectType.DATAFLOW_SIDE_EFFECTING),
    )(src_thru, land_thru, send_sems, recv_sems, *after)


def _add_rider(rider, in_specs, args, out_specs, out_shape):
    if rider is None:
        return []
    _, arr = rider
    in_specs.append(ANY)
    args.append(arr)
    out_specs.append(ANY)
    out_shape.append(jax.ShapeDtypeStruct((N_DEV,) + arr.shape[-2:], arr.dtype))
    return list(PEER_SEMS)


def _split_rider(refs, rider, n_in, n_out):
    if rider is None:
        return refs, None
    refs = list(refs)
    rin = refs.pop(n_in)
    rout = refs.pop(n_in + n_out)
    return refs[:-3], (rin, rout, *refs[-3:])


def _ride_start(rider, ride_refs, first):
    if rider is None:
        return

    @pl.when(first)
    def _():
        mine, copies = _peer_copies(rider[0], *ride_refs)
        mine.start()
        for cp in copies:
            cp.start()


def _ride_wait(rider, ride_refs, last):
    if rider is None:
        return

    @pl.when(last)
    def _():
        mine, copies = _peer_copies(rider[0], *ride_refs)
        for cp in copies:
            cp.wait()
        mine.wait()


def _gathered_cols(blocks, kdim):
    n = blocks.shape[1] * WIDE // kdim
    return blocks.reshape(N_DEV, kdim, n).transpose(1, 0, 2).reshape(kdim, N_DEV * n)


def _scatter_cols(dw):
    kdim, n8 = dw.shape
    n = n8 // N_DEV
    return dw.reshape(kdim, N_DEV, n).transpose(1, 0, 2).reshape(N_DEV, kdim * n // WIDE, WIDE)


def _pad_rows(a, rows):
    pad = [(0, 0)] * a.ndim
    pad[-2] = (0, rows - a.shape[-2])
    return jnp.pad(a, pad)


def _layer0_in_weight_t(wt):
    cq, ckv, kpe = wt[0:256], wt[256:384], wt[384:416]
    q_s, k_s, v_s, gate = wt[416:928], wt[928:1056], wt[1056:1184], wt[1184:2208]
    z = jnp.zeros((64, wt.shape[1]), wt.dtype)
    return jnp.concatenate([gate, cq, ckv, z, kpe, z[:32], q_s, k_s, v_s], axis=0)


def _layer0_in_grad_t(dwt):
    gate, cq, ckv, kpe = dwt[0:1024], dwt[1024:1280], dwt[1280:1408], dwt[1472:1504]
    q_s, k_s, v_s = dwt[1536:2048], dwt[2048:2176], dwt[2176:2304]
    return jnp.concatenate([cq, ckv, kpe, q_s, k_s, v_s, gate], axis=0)


def _layer1_in_weight_t(wt):
    main = jnp.concatenate([wt[:3 * D_MODEL], wt[3 * D_MODEL + FOX_HEADS:]], axis=0)
    return main, _pad_rows(wt[3 * D_MODEL:3 * D_MODEL + FOX_HEADS], LANES)


def _layer1_in_grad_t(d_blocks, d_wft):
    return jnp.concatenate([d_blocks[0], d_wft[:FOX_HEADS], d_blocks[1]], axis=0)


def _q_up_weight(w):
    return jnp.pad(w.reshape(MLA_Q_RANK, MLA_HEADS, 96), ((0, 0), (0, 0), (0, 32))).reshape(MLA_Q_RANK, MLA_HEADS * LANES)


def _q_up_grad(dwp):
    return dwp.reshape(MLA_Q_RANK, MLA_HEADS, LANES)[:, :, :96].reshape(MLA_Q_RANK, MLA_HEADS * 96)


def _kv_up_weight(w):
    w4 = w.reshape(MLA_KV_RANK, MLA_HEADS, 2, 64)
    kp = jnp.pad(w4[:, :, 0, :], ((0, 0), (0, 0), (0, 64))).reshape(MLA_KV_RANK, MLA_HEADS * LANES)
    vp = w4[:, :, 1, :].reshape(MLA_KV_RANK, MLA_HEADS * 64)
    return jnp.concatenate([kp, vp], axis=1)


def _kv_up_grad(dwp):
    dk = dwp[:, :MLA_HEADS * LANES].reshape(MLA_KV_RANK, MLA_HEADS, LANES)[:, :, :64]
    dv = dwp[:, MLA_HEADS * LANES:].reshape(MLA_KV_RANK, MLA_HEADS, 64)
    return jnp.stack([dk, dv], axis=2).reshape(MLA_KV_RANK, MLA_HEADS * LANES)


def _pad_lanes(a):
    return jnp.pad(a, ((0, 0), (0, LANES - a.shape[1])))


def _small_pack(g_in, g_final, g_q_a, g_kv_a, sinks, b_f, loss):
    rows = [g_in.reshape(8, LANES), g_final.reshape(8, LANES), g_q_a.reshape(2, LANES), g_kv_a.reshape(1, LANES),
            _pad_lanes(sinks.reshape(1, -1)), _pad_lanes(b_f.reshape(1, -1)), _pad_lanes(loss.reshape(1, 1)),
            jnp.zeros((2, LANES), F32)]
    return jnp.concatenate(rows, axis=0)


def _small_unpack(a):
    return (a[0:8].reshape(1, D_MODEL), a[8:16].reshape(D_MODEL), a[16:18].reshape(1, MLA_Q_RANK),
            a[18:19].reshape(1, MLA_KV_RANK), a[19:20, :SWA_HEADS], a[20:21, :FOX_HEADS], a[21, 0])


def _local_step(x, positions, target, e_g_in, early, e_g_q_a, e_g_kv_a, e_sinks,
                late, o_b_f, g_final, scatter1=None, scatter0=None):
    s = x.shape[0]
    mla_scale = (MLA_NOPE + MLA_ROPE) ** -0.5
    fox_scale = FOX_DIM ** -0.5
    n0a = Z0A_UNITS * LANES

    inv_freq = 1.0 / (ROPE_THETA ** (jnp.arange(0, MLA_ROPE, 2, dtype=F32) / MLA_ROPE))
    ang = positions.astype(F32)[:, None] * inv_freq
    cos, sin = jnp.cos(ang), jnp.sin(ang)
    ones, zeros = jnp.ones((s, 64), F32), jnp.zeros((s, 64), F32)
    cos_t = jnp.concatenate([ones, cos, cos, ones[:, :32]], axis=1)
    sin_t = jnp.concatenate([zeros, -sin, sin, zeros[:, :32]], axis=1)

    if len(early) == 3:
        h0 = _rmsnorm_fwd(x, e_g_in, width=D_MODEL, col_blk=0, name="l0_norm")
        w0t, wq, wkv = early
    else:
        pending, token, unpack, prep = early
        h0 = _rmsnorm_fwd(x, e_g_in, width=D_MODEL, col_blk=0, name="l0_norm", after=[token])
        w0t, wq, wkv = unpack(*_peer_wait("gather", *pending, after=[h0] + prep, name="weights0_wait"))
    z0a = _matmul(h0, w0t, tb=True, b_rows=(0, n0a), name="l0_in_a")
    z0b = _matmul(h0, w0t, tb=True, b_rows=(n0a, Z0B_UNITS * LANES), name="l0_in_b", out_dtype=BF16)
    cqn = _rmsnorm_fwd(z0a, e_g_q_a, width=MLA_Q_RANK, col_blk=4, name="l0_q_norm")
    ckvn = _rmsnorm_fwd(z0a, e_g_kv_a, width=MLA_KV_RANK, col_blk=10, name="l0_kv_norm")
    qp = _matmul(cqn, wq, name="l0_q_up")
    kvp = _matmul(ckvn, wkv, name="l0_kv_up", out_dtype=BF16)
    qm, km = _rope_fwd(qp, kvp, z0a, cos_t, sin_t, name="l0_rope")
    gathers = len(late) == 2
    res = _flash_fwd(qm, km, kvp, None, n_pairs=MLA_HEADS // 2, hw=LANES, q_off=0, k_off=0, v_off=MLA_HEADS,
                     scale=mla_scale, name="l0_mla_fwd", rider=("gather", late[0]) if gathers else None)
    o_mla, lse_mla = res[0], res[1]
    wo0, o_g_in, w1t, wft, wo1 = late[1](res[2]) if gathers else late
    o_swa, lse_swa = _swa_fwd(z0b, e_sinks, name="l0_swa_fwd")
    half = D_MODEL // 2

    x1, h1, og0 = _matmul_rows(
        [(wo0, False)], [(o_mla, half, 0), (o_swa, half, 0), (z0a, D_MODEL, 0), (x, D_MODEL, 0)], [o_g_in],
        lambda r, om, osw, gt, xt, g, made: (*_residual_norm_epilogue(r, xt, g), made),
        [("rows", D_MODEL, F32), ("rows", D_MODEL, BF16), ("rows", D_MODEL, BF16)], name="l0_out",
        prologue=lambda om, osw, gt, xt, g: _gated([om, osw], gt))
    z1 = _matmul(h1, w1t, tb=True, b_rows=(0, 3 * D_MODEL), name="l1_in_qkv", out_dtype=BF16)
    gate1 = _matmul(h1, w1t, tb=True, b_rows=(3 * D_MODEL, D_MODEL), name="l1_in_gate")
    zf = _matmul(h1, wft, tb=True, name="l1_in_f")
    bf = _pad_lanes(o_b_f)
    log_cum = _logf_fwd(zf, bf, name="l1_logf")
    bias2 = (-LOG2E * log_cum[:, :FOX_HEADS]).T
    t_bwd = min(ATT_T, s)
    bias = bias2.reshape(FOX_HEADS // 2, 2, s // t_bwd, 1, t_bwd)
    t_fwd = _fwd_tile(s)
    o_fox, lse_fox = _flash_fwd(z1, z1, z1, bias2.reshape(FOX_HEADS // 2, 2, s // t_fwd, 1, t_fwd),
                                n_pairs=FOX_HEADS // 2, hw=64, q_off=0, k_off=8, v_off=16, scale=fox_scale,
                                name="l1_fox_fwd")

    dx2, loss_part, d_g_final, og1 = _matmul_rows(
        [(wo1, False)], [(o_fox, D_MODEL, 0), (gate1, D_MODEL, 0), (x1, D_MODEL, 0), (target, D_MODEL, 0)],
        [g_final.reshape(1, D_MODEL)], lambda r, o, gt, xt, tg, g, made: (*_loss_epilogue(r, xt, tg, g), made),
        [("rows", D_MODEL, F32), ("sum", (8, LANES)), ("sum", (1, D_MODEL)), ("rows", D_MODEL, BF16)],
        name="l1_out_loss", prologue=lambda o, gt, xt, tg, g: _gated([o], gt))

    d_wo1 = _matmul(og1, dx2, ta=True, out_dtype=BF16, name="l1_out_dw")
    do_fox, d_gate1 = _matmul_rows([(dx2, wo1, True)], [(o_fox, D_MODEL, 0), (gate1, D_MODEL, 0)], [],
                                   _gate_bwd_epilogue([D_MODEL]), [("rows", D_MODEL, F32), ("rows", D_MODEL, BF16)],
                                   name="l1_out_dx")
    dqkv1, dbias, drow = _flash_bwd(z1, z1, z1, do_fox, o_fox, lse_fox, bias, n_pairs=FOX_HEADS // 2, hw=64, q_off=0,
                                    k_off=8, v_off=16, scale=fox_scale, qk_dtype=BF16, stacked=True, name="l1_fox_bwd")
    d_log_cum = (drow.reshape(FOX_HEADS, s) - dbias.reshape(FOX_HEADS, s)).T
    d_log_cum = jnp.pad(d_log_cum, ((0, 0), (0, LANES - FOX_HEADS)))
    d_zf, d_bf = _logf_bwd(d_log_cum, zf, bf, name="l1_logf_bwd")
    d_w1t = (_matmul(dqkv1, h1, ta=True, out_dtype=BF16, name="l1_in_dw_qkv"),
             _matmul(d_gate1, h1, ta=True, out_dtype=BF16, name="l1_in_dw_gate"))
    d_wft = _matmul(d_zf, h1, ta=True, out_dtype=BF16, name="l1_in_f_dw")
    dx1, d_o_g_in = _matmul_rows([(dqkv1, w1t, False, c * D_MODEL, c) for c in range(3)]
                                 + [(d_gate1, w1t, False, 3 * D_MODEL), (d_zf, wft, False)],
                                 [(x1, D_MODEL, 0), (dx2, D_MODEL, 0)],
                                 [o_g_in], _rms_bwd_epilogue, [("rows", D_MODEL, F32), ("sum", (1, D_MODEL))],
                                 name="l1_in_dx")

    d_wo0 = _matmul(og0, dx1, ta=True, out_dtype=BF16, name="l0_out_dw")
    do_mla, do_swa, d_gate0 = _matmul_rows(
        [(dx1, wo0, True)], [(o_mla, half, 0), (o_swa, half, 0), (z0a, D_MODEL, 0)], [], _gate_bwd_epilogue([half, half]),
        [("rows", half, F32), ("rows", half, F32), ("rows", D_MODEL, BF16)], name="l0_out_dx")
    dq_s, dkt_s, dvt_s, d_sinks = _swa_bwd(z0b, e_sinks, do_swa, o_swa, lse_swa, name="l0_swa_bwd")
    dk_s = dkt_s.transpose(0, 2, 1).reshape(s, LANES)
    dv_s = dvt_s.transpose(0, 2, 1).reshape(s, LANES)
    rider = None
    if scatter1 is not None:
        rider = ("exchange", scatter1(dict(w1t=d_w1t, wft=d_wft, wo1=d_wo1, o_g_in=d_o_g_in, wo0=d_wo0)))
    res = _flash_bwd(qm, km, kvp, do_mla, o_mla, lse_mla, None, n_pairs=MLA_HEADS // 2, hw=LANES, q_off=0, k_off=0,
                     v_off=MLA_HEADS, scale=mla_scale, qk_dtype=F32, name="l0_mla_bwd", rider=rider)
    dqm, dkm, dvm = res[0], res[1], res[2]
    recv1 = res[3] if rider is not None else None
    d_qp, d_kvp, d_kpe = _rope_bwd(dqm, dkm, dvm, cos_t, sin_t, name="l0_rope_bwd")
    d_wq = _matmul(cqn, d_qp, ta=True, out_dtype=BF16, name="l0_q_up_dw")
    d_cqn = _matmul(d_qp, wq, tb=True, name="l0_q_up_dx")
    d_wkv = _matmul(ckvn, d_kvp, ta=True, out_dtype=BF16, name="l0_kv_up_dw")
    d_ckvn = _matmul(d_kvp, wkv, tb=True, name="l0_kv_up_dx")
    d_cq, d_g_q_a = _rmsnorm_bwd(z0a, e_g_q_a, d_cqn, width=MLA_Q_RANK, col_blk=4, name="l0_q_norm_bwd")
    d_ckv, d_g_kv_a = _rmsnorm_bwd(z0a, e_g_kv_a, d_ckvn, width=MLA_KV_RANK, col_blk=10, name="l0_kv_norm_bwd")
    dz0 = jnp.concatenate([d_gate0, d_cq, d_ckv, d_kpe, dq_s.astype(BF16), dk_s.astype(BF16), dv_s.astype(BF16)], axis=1)
    d_w0t = _matmul(dz0, h0, ta=True, out_dtype=BF16, name="l0_in_dw")
    pending0, after_start = None, []
    if scatter0 is not None:
        *pending0, token = _peer_start("exchange", scatter0(dict(w0t=d_w0t, wq=d_wq, wkv=d_wkv)), name="grads0_start")
        after_start = [token]
    grad_x, d_e_g_in = _matmul_rows(
        [(dz0, w0t, False)], [(x, D_MODEL, 0), (dx1, D_MODEL, 0)], [e_g_in] + after_start,
        lambda dy, xt, add, g, *_: _rms_bwd_epilogue(dy, xt, add, g),
        [("rows", D_MODEL, F32), ("sum", (1, D_MODEL))], name="l0_in_dx")

    return dict(pending0=pending0, recv1=recv1, loss=loss_part[0, 0], grad_x=grad_x, e_g_in=d_e_g_in, w0t=d_w0t, e_g_q_a=d_g_q_a, wq=d_wq,
                e_g_kv_a=d_g_kv_a, wkv=d_wkv, e_sinks=d_sinks[:, 0].reshape(1, SWA_HEADS), wo0=d_wo0,
                o_g_in=d_o_g_in, w1t=d_w1t, wft=d_wft, o_b_f=d_bf[:, :FOX_HEADS], wo1=d_wo1, g_final=d_g_final.reshape(D_MODEL))


def _wide(a, rows):
    flat = a.reshape(-1)
    return jnp.pad(flat, (0, rows * WIDE - flat.shape[0])).reshape(rows, WIDE)


def _rows_b0(w_q, w_kv):
    return jnp.concatenate([_wide(w_q, 32), _wide(w_kv, 16)], axis=0)


def _unflat_b0(f):
    return f[0:24].reshape(1, MLA_Q_RANK, 96), f[32:48].reshape(1, MLA_KV_RANK, 128)


def _rows_b1(o_w_out, e_w_out, g_in):
    return jnp.concatenate([o_w_out, e_w_out, _wide(g_in, 16)], axis=0)


def _unflat_b1(f):
    return f[0:128][None], f[128:256][None], f[256:257, :LANES]


def kernel(x, positions, e_g_in, e_w_in, e_g_q_a, e_w_q_up, e_g_kv_a, e_w_kv_up, e_sinks, e_w_out, o_g_in, o_w_in, o_b_f, o_w_out, g_final, loss_target, m_e_g_in, m_e_w_in, m_e_g_q_a, m_e_w_q_up, m_e_g_kv_a, m_e_w_kv_up, m_e_sinks, m_e_w_out, m_o_g_in, m_o_w_in, m_o_b_f, m_o_w_out, m_g_final, v_e_g_in, v_e_w_in, v_e_g_q_a, v_e_w_q_up, v_e_g_kv_a, v_e_w_kv_up, v_e_sinks, v_e_w_out, v_o_g_in, v_o_w_in, v_o_b_f, v_o_w_out, v_g_final):
    def bf(a):
        return a.astype(BF16)

    me = 4 * lax.axis_index("x") + 2 * lax.axis_index("y") + lax.axis_index("c")
    shard0 = jnp.concatenate([_pad_rows(bf(e_w_in[0]).T, RA0), _rows_b0(bf(e_w_q_up[0]), bf(e_w_kv_up[0]))], axis=0)
    *pending_w0, token_w0 = _peer_start("gather", shard0, name="weights0_start")

    def unpack0(sent, gath0):
        gath0 = lax.dynamic_update_slice_in_dim(gath0, sent[None], me, axis=0)
        w0t = _layer0_in_weight_t(gath0[:, :N_E_IN].reshape(N_DEV * N_E_IN, WIDE))
        wq = _q_up_weight(_gathered_cols(gath0[:, RA0:RA0 + 24], MLA_Q_RANK))
        wkv = _kv_up_weight(_gathered_cols(gath0[:, RA0 + 32:RA0 + 48], MLA_KV_RANK))
        return w0t, wq, wkv

    rows_b0 = [_rows_b0(q[0], kv[0]) for q, kv in ((e_w_q_up, e_w_kv_up), (m_e_w_q_up, m_e_w_kv_up), (v_e_w_q_up, v_e_w_kv_up))]
    rows_b1 = [_rows_b1(o[0], e[0], g) for o, e, g in ((o_w_out, e_w_out, o_g_in), (m_o_w_out, m_e_w_out, m_o_g_in),
                                                       (v_o_w_out, v_e_w_out, v_o_g_in))]

    g_bits = lax.bitcast_convert_type(o_g_in.reshape(LANES), BF16)
    shard1 = jnp.concatenate([_pad_rows(bf(o_w_in[0]).T, RA1), _rows_b1(bf(o_w_out[0]), bf(e_w_out[0]), g_bits)], axis=0)

    def unpack1(gath1):
        w1t, wft = _layer1_in_weight_t(gath1[:, :N_O_IN].reshape(N_DEV * N_O_IN, WIDE))
        wo1 = gath1[:, RA1:RA1 + 128].reshape(D_MODEL, D_MODEL)
        wo0 = gath1[:, RA1 + 128:RA1 + 256].reshape(D_MODEL, D_MODEL)
        bits = gath1[:, RA1 + 256, :2 * LANES].reshape(N_DEV, LANES, 2)
        return wo0, lax.bitcast_convert_type(bits, F32).reshape(1, D_MODEL), w1t, wft, wo1

    def scatter1(g):
        d_in_t = _layer1_in_grad_t(g["w1t"], g["wft"]).reshape(N_DEV, N_O_IN, WIDE)
        d_o_g = jnp.pad(bf(g["o_g_in"]).reshape(N_DEV, 1, LANES), ((0, 0), (0, 15), (0, WIDE - LANES)))
        return jnp.concatenate([_pad_rows(d_in_t, RA1), g["wo1"].reshape(N_DEV, 128, WIDE),
                                g["wo0"].reshape(N_DEV, 128, WIDE), d_o_g], axis=1)

    def scatter0(g):
        return jnp.concatenate([
            _pad_rows(_layer0_in_grad_t(g["w0t"]).reshape(N_DEV, N_E_IN, WIDE), RA0),
            _pad_rows(_scatter_cols(_q_up_grad(g["wq"])), 32), _scatter_cols(_kv_up_grad(g["wkv"]))], axis=1)

    gr = _local_step(x[0], positions[0], loss_target[0], e_g_in,
                     (pending_w0, token_w0, unpack0, [shard1] + rows_b0 + rows_b1), e_g_q_a, e_g_kv_a, e_sinks,
                     (shard1, unpack1), o_b_f, g_final, scatter1=scatter1, scatter0=scatter0)

    def in_projection(recv, ra, n, w, m, v, name):
        g = _sum8(recv, ra, name=name + "_grad_sum")[:n].reshape(n, 1, D_MODEL)
        w, m, v = [jnp.transpose(a, (2, 0, 1)) for a in (w, m, v)]
        return (g, *_adamw_columns(g, w, m, v, name=name + "_adamw"))

    o_in = in_projection(gr["recv1"], RA1, N_O_IN, o_w_in, m_o_w_in, v_o_w_in, "o_w_in")
    b1 = _adamw(gr["recv1"][:, RA1:], *rows_b1, name="adamw_late")

    small = _small_pack(gr["e_g_in"], gr["g_final"], gr["e_g_q_a"], gr["e_g_kv_a"], gr["e_sinks"], gr["o_b_f"], gr["loss"])
    small_all = _all_gather(small, name="small_all_gather")
    zero = jnp.zeros((), F32)
    w_small = _small_pack(e_g_in, g_final, e_g_q_a, e_g_kv_a, e_sinks, o_b_f, zero)
    m_small = _small_pack(m_e_g_in, m_g_final, m_e_g_q_a, m_e_g_kv_a, m_e_sinks, m_o_b_f, zero)
    v_small = _small_pack(v_e_g_in, v_g_final, v_e_g_q_a, v_e_g_kv_a, v_e_sinks, v_o_b_f, zero)
    smalls = _adamw(small_all, w_small, m_small, v_small, name="adamw_replicated")
    g_sm, d_sm, m_sm, v_sm = [_small_unpack(a) for a in smalls]
    loss = g_sm[6]

    sent0, recv0 = _peer_wait("exchange", *gr["pending0"], after=[o_in[1], b1[1], smalls[1]], name="grads0_wait")
    own = lax.dynamic_slice_in_dim(sent0, me, 1, axis=0)
    recv0 = lax.dynamic_update_slice_in_dim(recv0, own, me, axis=0)
    e_in = in_projection(recv0, RA0, N_E_IN, e_w_in, m_e_w_in, v_e_w_in, "e_w_in")
    b0 = _adamw(recv0[:, RA0:], *rows_b0, name="adamw_early")

    def sharded(k):
        q_up, kv_up = _unflat_b0(b0[k])
        o_out, e_out, o_g = _unflat_b1(b1[k])
        return jnp.transpose(e_in[k], (1, 2, 0)), q_up, kv_up, e_out, jnp.transpose(o_in[k], (1, 2, 0)), o_out, o_g

    g_sh, d_sh, m_sh, v_sh = [sharded(k) for k in range(4)]

    def leaves(sh, sm):
        return (sm[0], sh[0], sm[2], sh[1], sm[3], sh[2], sm[4], sh[3], sh[6], sh[4], sm[5], sh[5], sm[1])

    return (loss, gr["grad_x"][None], *leaves(g_sh, g_sm), *leaves(d_sh, d_sm), *leaves(m_sh, m_sm), *leaves(v_sh, v_sm))
```

```python
import functools

import jax
import jax.numpy as jnp
from jax import lax
from jax.experimental import pallas as pl
from jax.experimental.pallas import tpu as pltpu

F32 = jnp.float32
BF16 = jnp.bfloat16
NEG_INF = float("-inf")

N_DEV = 8
LANES = 128
D_MODEL = 1024
EPS = 1e-6
ROPE_THETA = 10000.0
MLA_HEADS = 8
MLA_Q_RANK = 256
MLA_KV_RANK = 128
MLA_NOPE = 64
MLA_ROPE = 32
MLA_V = 64
SWA_HEADS = 8
SWA_KV_HEADS = 2
SWA_DIM = 64
WINDOW = 128
FOX_HEADS = 16
FOX_DIM = 64

ADAM_LR = 0.001
ADAM_B1 = 0.9
ADAM_B2 = 0.999
ADAM_EPS = 1e-08
ADAM_WD = 0.01
ADAM_STEP = 10

ATT_T = 512
ATT_T_FWD = 1024
VMEM_LIMIT = 56 * 1024 * 1024
MATMUL_B_BLOCK_BYTES = 8 * 1024 * 1024

Z0A_UNITS = 12
Z0B_UNITS = 6

WIDE = 1024
N_E_IN = 276
N_O_IN = 514
RA0 = 288
RB0 = 32 + 16
RA1 = 528
RB1 = 128 + 128 + 16
SMALL_ROWS = 24


def _tile(n, cands):
    for c in cands:
        if n % c == 0:
            return c
    raise ValueError(f"no tile for {n}")


ROW_TILES = (512, 256, 128)


def _params(sem, vmem=VMEM_LIMIT):
    return pltpu.CompilerParams(dimension_semantics=sem, vmem_limit_bytes=vmem)


def _matmul(a, b, *, name, ta=False, tb=False, out_dtype=F32, b_rows=None):
    if ta:
        kdim, m = a.shape[-2], a.shape[-1] * (a.shape[0] if a.ndim == 3 else 1)
    else:
        m, kdim = a.shape
    if tb:
        n, kb = b.shape
    else:
        kb, n = b.shape
    assert kdim == kb, (a.shape, b.shape)
    b_start = 0
    if b_rows is not None:
        assert tb
        b_start, n = b_rows
    tm = _tile(m, (512, 256, 128))
    tn = _tile(n, [c for c in (1024, 768, 512, 384, 256, 128)
                   if c * kdim * b.dtype.itemsize <= MATMUL_B_BLOCK_BYTES and b_start % c == 0])
    assert b_start % tn == 0, (b_start, tn)
    b_off = b_start // tn
    dims = (((0 if ta else 1,), (1 if tb else 0,)), ((), ()))

    def body(a_ref, b_ref, o_ref):
        r = lax.dot_general(a_ref[...].astype(BF16), b_ref[...].astype(BF16), dims, preferred_element_type=F32)
        o_ref[...] = r.astype(out_dtype)

    if a.ndim == 3:
        per = a.shape[2] // tm
        a_spec = pl.BlockSpec((None, kdim, tm), lambda i, j: (i // per, 0, i % per))
    else:
        a_spec = pl.BlockSpec((kdim, tm), lambda i, j: (0, i)) if ta else pl.BlockSpec((tm, kdim), lambda i, j: (i, 0))
    b_spec = pl.BlockSpec((tn, kdim), lambda i, j: (j + b_off, 0)) if tb else pl.BlockSpec((kdim, tn), lambda i, j: (0, j))
    return pl.pallas_call(
        body, name=name, grid=(m // tm, n // tn), in_specs=[a_spec, b_spec],
        out_specs=pl.BlockSpec((tm, tn), lambda i, j: (i, j)), out_shape=jax.ShapeDtypeStruct((m, n), out_dtype),
        compiler_params=_params(("parallel", "parallel")),
    )(a, b)


def _rmsnorm_fwd(x, g, *, width, col_blk, name, after=()):
    s = x.shape[0]
    tm = _tile(s, ROW_TILES)

    def body(x_ref, g_ref, *rest):
        y_ref = rest[-1]
        xf = x_ref[...].astype(F32)
        r = lax.rsqrt(jnp.mean(xf * xf, axis=-1, keepdims=True) + EPS)
        y_ref[...] = ((xf * r) * g_ref[...]).astype(BF16)

    return pl.pallas_call(
        body, name=name, grid=(s // tm,),
        in_specs=[pl.BlockSpec((tm, width), lambda i: (i, col_blk)), pl.BlockSpec((1, width), lambda i: (0, 0))]
        + [ANY] * len(after),
        out_specs=pl.BlockSpec((tm, width), lambda i: (i, 0)),
        out_shape=jax.ShapeDtypeStruct((s, width), BF16),
        compiler_params=_params(("parallel",)),
    )(x, g, *after)


def _rmsnorm_bwd(x, g, dy, *, width, col_blk, name):
    s = x.shape[0]
    tm = _tile(s, ROW_TILES)

    def body(x_ref, g_ref, dy_ref, dx_ref, dg_ref):
        @pl.when(pl.program_id(0) == 0)
        def _():
            dg_ref[...] = jnp.zeros_like(dg_ref)

        dx, dg = _rms_bwd_epilogue(dy_ref[...], x_ref[...], 0.0, g_ref[...])
        dg_ref[...] += dg
        dx_ref[...] = dx.astype(BF16)

    return pl.pallas_call(
        body, name=name, grid=(s // tm,),
        in_specs=[pl.BlockSpec((tm, width), lambda i: (i, col_blk)), pl.BlockSpec((1, width), lambda i: (0, 0)),
                  pl.BlockSpec((tm, width), lambda i: (i, 0))],
        out_specs=[pl.BlockSpec((tm, width), lambda i: (i, 0)), pl.BlockSpec((1, width), lambda i: (0, 0))],
        out_shape=[jax.ShapeDtypeStruct((s, width), BF16), jax.ShapeDtypeStruct((1, width), F32)],
        compiler_params=_params(("arbitrary",)),
    )(x, g, dy)


def _sigmoid(x):
    return 1.0 / (1.0 + jnp.exp(-x))


def _matmul_rows(terms, row_inputs, params, epilogue, outs, *, name, prologue=None, separate=False):
    s = row_inputs[0][0].shape[0] if row_inputs else terms[0][0].shape[-2]
    tm = _tile(s, ROW_TILES)
    steps = s // tm
    n_r, n_p, n_o = len(row_inputs), len(params), len(outs)
    n_t = sum(1 if term[0] is None else 2 for term in terms)

    def body(*refs):
        t_refs, r_refs = list(refs[:n_t]), refs[n_t:n_t + n_r]
        p_refs, o_refs = refs[n_t + n_r:n_t + n_r + n_p], refs[n_t + n_r + n_p:]
        i = pl.program_id(0)
        rows, small = [r[...] for r in r_refs], [p[...] for p in p_refs]
        made = None if prologue is None else prologue(*rows, *small)
        parts = []
        for term in terms:
            a = made if term[0] is None else t_refs.pop(0)[...].astype(BF16)
            dims = (((1,), (1 if term[2] else 0,)), ((), ()))
            parts.append(lax.dot_general(a, t_refs.pop(0)[...].astype(BF16), dims, preferred_element_type=F32))
        acc = parts if separate else sum(parts[1:], parts[0])
        vals = epilogue(acc, *rows, *small) if prologue is None else epilogue(acc, *rows, *small, made)
        for ref, val, out in zip(o_refs, vals, outs):
            if out[0] == "rows":
                ref[...] = val.astype(ref.dtype)
            else:
                @pl.when(i == 0)
                def _(ref=ref):
                    ref[...] = jnp.zeros_like(ref)

                ref[...] += val

    in_specs, args = [], []
    for term in terms:
        a, b = term[0], term[1]
        if a is None:
            in_specs.append(_resident(b.shape, lambda i: (0, 0)))
            args.append(b)
            continue
        b_rows = b.shape[0] if term[2] or len(term) < 4 else a.shape[-1]
        b_blk = 0 if len(term) < 4 else term[3] // b_rows
        if len(term) == 5:
            a_spec = pl.BlockSpec((None, tm, a.shape[2]), lambda i, c=term[4]: (c, i, 0))
        else:
            a_spec = pl.BlockSpec((tm, a.shape[1]), lambda i: (i, 0))
        in_specs += [a_spec, _resident((b_rows, b.shape[1]), lambda i, b_blk=b_blk: (b_blk, 0))]
        args += [a, b]
    for arr, width, col_blk in row_inputs:
        in_specs.append(pl.BlockSpec((tm, width), lambda i, col_blk=col_blk: (i, col_blk)))
        args.append(arr)
    for p in params:
        in_specs.append(pl.BlockSpec(p.shape, lambda i: (0, 0)))
        args.append(p)
    out_specs, out_shape = [], []
    for out in outs:
        if out[0] == "rows":
            out_specs.append(pl.BlockSpec((tm, out[1]), lambda i: (i, 0)))
            out_shape.append(jax.ShapeDtypeStruct((s, out[1]), out[2]))
        else:
            out_specs.append(pl.BlockSpec(out[1], lambda i: (0, 0)))
            out_shape.append(jax.ShapeDtypeStruct(out[1], F32))
    return pl.pallas_call(
        body, name=name, grid=(steps,), in_specs=in_specs, out_specs=out_specs, out_shape=out_shape,
        compiler_params=_params(("arbitrary",)),
    )(*args)


def _rms_stats(x):
    r = lax.rsqrt(jnp.mean(x * x, axis=-1, keepdims=True) + EPS)
    return r, x * r


def _gated(o_parts, gate):
    o = o_parts[0] if len(o_parts) == 1 else jnp.concatenate(o_parts, axis=1)
    return (o * (gate * _sigmoid(gate))).astype(BF16)


def _and_first(vals, *more):
    return (*vals, *more, vals[0])


def _residual_norm_epilogue(r, x, g):
    x1 = x + r
    _, xh = _rms_stats(x1)
    return x1, xh * g


def _rms_bwd_epilogue(dy, x, add, g):
    r, xh = _rms_stats(x)
    dxh = dy * g
    dx = r * (dxh - xh * jnp.mean(dxh * xh, axis=-1, keepdims=True)) + add
    return dx, jnp.sum(dy * xh, axis=0, keepdims=True)


def _loss_epilogue(r, x1, target, g):
    rs, xh = _rms_stats(x1 + r)
    err = xh * g - target
    loss = jnp.broadcast_to(0.5 * jnp.sum(jnp.mean(err * err, axis=-1, keepdims=True)), (8, LANES))
    dy = err * (1.0 / D_MODEL)
    dxh = dy * g
    dx = rs * (dxh - xh * jnp.mean(dxh * xh, axis=-1, keepdims=True))
    return dx, loss, jnp.sum(dy * xh, axis=0, keepdims=True)


def _gate_bwd_epilogue(widths):
    def epilogue(d, *rows):
        o_parts, gt = rows[:-1], rows[-1]
        o = o_parts[0] if len(o_parts) == 1 else jnp.concatenate(o_parts, axis=1)
        sg = _sigmoid(gt)
        do = d * (gt * sg)
        d_gate = d * o * (sg * (1.0 + gt * (1.0 - sg)))
        cuts = [sum(widths[:k]) for k in range(len(widths) + 1)]
        return tuple(do[:, cuts[k]:cuts[k + 1]] for k in range(len(widths))) + (d_gate,)

    return epilogue


def _rot_half(x):
    lane = lax.broadcasted_iota(jnp.int32, x.shape, 1)
    return jnp.where(lane < 80, pltpu.roll(x, LANES - 16, axis=1), pltpu.roll(x, 16, axis=1))


def _rot_half_t(g):
    lane = lax.broadcasted_iota(jnp.int32, g.shape, 1)
    lo = (lane >= MLA_NOPE) & (lane < MLA_NOPE + MLA_ROPE // 2)
    hi = (lane >= MLA_NOPE + MLA_ROPE // 2) & (lane < MLA_NOPE + MLA_ROPE)
    return jnp.where(lo, pltpu.roll(g, LANES - 16, axis=1), jnp.where(hi, pltpu.roll(g, 16, axis=1), 0.0))


def _rope_fwd(qp, kvp, z0a, cos_t, sin_t, *, name):
    s = qp.shape[0]
    tm = _tile(s, ROW_TILES)
    hw = MLA_HEADS * LANES

    def body(q_ref, k_ref, kpe_ref, c_ref, s_ref, qm_ref, km_ref):
        c = c_ref[...]
        sn = s_ref[...]
        kpe = kpe_ref[...]
        kpe_r = (kpe * c + _rot_half(kpe) * sn).astype(BF16)
        lane = lax.broadcasted_iota(jnp.int32, kpe.shape, 1)
        for h in range(MLA_HEADS):
            sl = slice(h * LANES, (h + 1) * LANES)
            qh = q_ref[:, sl]
            qm_ref[:, sl] = (qh * c + _rot_half(qh) * sn).astype(BF16)
            km_ref[:, sl] = jnp.where(lane < MLA_NOPE, k_ref[:, sl], kpe_r)

    return pl.pallas_call(
        body, name=name, grid=(s // tm,),
        in_specs=[pl.BlockSpec((tm, hw), lambda i: (i, 0)), pl.BlockSpec((tm, hw), lambda i: (i, 0)),
                  pl.BlockSpec((tm, LANES), lambda i: (i, 11)),
                  pl.BlockSpec((tm, LANES), lambda i: (i, 0)), pl.BlockSpec((tm, LANES), lambda i: (i, 0))],
        out_specs=[pl.BlockSpec((tm, hw), lambda i: (i, 0)), pl.BlockSpec((tm, hw), lambda i: (i, 0))],
        out_shape=[jax.ShapeDtypeStruct((s, hw), BF16), jax.ShapeDtypeStruct((s, hw), BF16)],
        compiler_params=_params(("parallel",)),
    )(qp, kvp, z0a, cos_t, sin_t)


def _rope_bwd(dqm, dkm, dvm, cos_t, sin_t, *, name):
    s = dqm.shape[0]
    tm = _tile(s, ROW_TILES)
    hw = MLA_HEADS * LANES
    vw = MLA_HEADS * MLA_V

    def body(dq_ref, dk_ref, dv_ref, c_ref, s_ref, dqp_ref, dkv_ref, dkpe_ref):
        c = c_ref[...]
        sn = s_ref[...]
        ksum = jnp.zeros((tm, LANES), F32)
        for h in range(MLA_HEADS):
            sl = slice(h * LANES, (h + 1) * LANES)
            dq = dq_ref[:, sl]
            dqp_ref[:, sl] = (dq * c + _rot_half_t(dq * sn)).astype(BF16)
            dk = dk_ref[:, sl]
            dkv_ref[:, sl] = dk.astype(BF16)
            ksum = ksum + dk
        dkv_ref[:, hw:] = dv_ref[...]
        lane = lax.broadcasted_iota(jnp.int32, ksum.shape, 1)
        dkpe = ksum * c + _rot_half_t(ksum * sn)
        dkpe_ref[...] = jnp.where((lane >= MLA_NOPE) & (lane < MLA_NOPE + MLA_ROPE), dkpe, 0.0).astype(BF16)

    return pl.pallas_call(
        body, name=name, grid=(s // tm,),
        in_specs=[pl.BlockSpec((tm, hw), lambda i: (i, 0)), pl.BlockSpec((tm, hw), lambda i: (i, 0)),
                  pl.BlockSpec((tm, vw), lambda i: (i, 0)),
                  pl.BlockSpec((tm, LANES), lambda i: (i, 0)), pl.BlockSpec((tm, LANES), lambda i: (i, 0))],
        out_specs=[pl.BlockSpec((tm, hw), lambda i: (i, 0)), pl.BlockSpec((tm, hw + vw), lambda i: (i, 0)),
                   pl.BlockSpec((tm, LANES), lambda i: (i, 0))],
        out_shape=[jax.ShapeDtypeStruct((s, hw), BF16), jax.ShapeDtypeStruct((s, hw + vw), BF16),
                   jax.ShapeDtypeStruct((s, LANES), BF16)],
        compiler_params=_params(("parallel",)),
    )(dqm, dkm, dvm, cos_t, sin_t)


def _head_mask(shape, a):
    lane = lax.broadcasted_iota(jnp.int32, shape, 1)
    return (lane >= 64 * a) & (lane < 64 * (a + 1))


_NT = (((1,), (1,)), ((), ()))
LOG2E = 1.4426950408889634


def _stack_heads(tile, hw):
    lane = lax.broadcasted_iota(jnp.int32, tile.shape, 1)
    z = jnp.zeros_like(tile)
    return jnp.concatenate([jnp.where(lane < hw, tile, z), jnp.where(lane >= hw, tile, z)], axis=0)


def _stacked_rows(r0, r1, t):
    n = r0.shape[-1]
    return jnp.concatenate([jnp.broadcast_to(r0, (t, n)), jnp.broadcast_to(r1, (t, n))], axis=0)


def _resident(block, index_map):
    return pl.BlockSpec(block, index_map, pipeline_mode=pl.Buffered(1))


def _fwd_tile(s):
    return ATT_T_FWD if s % ATT_T_FWD == 0 else min(ATT_T, s)


def _flash_fwd(q, k, v, bias, *, n_pairs, hw, q_off, k_off, v_off, scale, name, rider=None):
    s = q.shape[0]
    t = _fwd_tile(s)
    nb = s // t
    qw = 2 * hw
    has_bias = bias is not None
    c1 = scale * LOG2E

    def body(*refs):
        refs, ride_refs = _split_rider(refs, rider, n_in=4 if has_bias else 3, n_out=2)
        if has_bias:
            q_ref, k_ref, v_ref, b_ref, o_ref, lse_ref, vt_ref, bcol_ref = refs
        else:
            q_ref, k_ref, v_ref, o_ref, lse_ref, vt_ref = refs
            b_ref = bcol_ref = None
        _ride_start(rider, ride_refs, pl.program_id(0) == 0)
        row = lax.broadcasted_iota(jnp.int32, (t, t), 0)
        col = lax.broadcasted_iota(jnp.int32, (t, t), 1)
        cmask_t = jnp.concatenate([row <= col, row <= col], axis=1)
        lane_lt64 = lax.broadcasted_iota(jnp.int32, (t, LANES), 1) < 64

        def as_column(r):
            return jnp.broadcast_to(r, (8, r.shape[1])).T[:, 0:1]

        def v_block(j, _):
            c0 = pl.multiple_of(j * t, t)
            vt_ref[j] = v_ref[pl.ds(c0, t), :].astype(F32).T.astype(BF16)
            if has_bias:
                for a in range(2):
                    bcol_ref[a, pl.ds(c0, t), :] = as_column(b_ref[0, a, j])
            return 0

        lax.fori_loop(0, nb, v_block, 0)

        def stacked_queries(i):
            return _stack_heads(q_ref[pl.ds(pl.multiple_of(i * t, t), t), :], hw).astype(F32).T.astype(BF16)

        def kv_step(j, carry, qs_t, masked):
            m, l, acc = carry
            rows = pl.ds(pl.multiple_of(j * t, t), t)
            sc = jnp.dot(k_ref[rows, :], qs_t, preferred_element_type=F32) * c1
            if has_bias:
                sc = sc + jnp.concatenate([jnp.broadcast_to(bcol_ref[0, rows, :], (t, t)),
                                           jnp.broadcast_to(bcol_ref[1, rows, :], (t, t))], axis=1)
            if masked:
                sc = jnp.where(cmask_t, sc, NEG_INF)
            m_new = jnp.maximum(m, jnp.max(sc, axis=0, keepdims=True))
            alpha = jnp.exp2(m - m_new)
            p = jnp.exp2(sc - m_new)
            l_new = alpha * l + jnp.sum(p, axis=0, keepdims=True)
            pv = jnp.dot(vt_ref[j], p.astype(BF16), preferred_element_type=F32)
            return m_new, l_new, alpha * acc + pv

        def finish(i, carry):
            m, l, acc = carry
            r0 = pl.multiple_of(i * t, t)
            out = (acc / l).T
            lse2 = as_column(m + jnp.log2(l))
            lse_ref[0, 0, pl.ds(r0, t), :] = lse2[:t]
            lse_ref[0, 1, pl.ds(r0, t), :] = lse2[t:]
            o_ref[pl.ds(r0, t), :] = jnp.where(lane_lt64, out[:t], out[t:])

        init = (jnp.full((1, 2 * t), NEG_INF, F32), jnp.zeros((1, 2 * t), F32), jnp.zeros((LANES, 2 * t), F32))

        def q_block(i, _):
            qs_t = stacked_queries(i)
            carry = lax.fori_loop(0, i, lambda j, c: kv_step(j, c, qs_t, False), init)
            finish(i, kv_step(i, carry, qs_t, True))
            return 0

        lax.fori_loop(0, nb, q_block, 0)
        _ride_wait(rider, ride_refs, pl.program_id(0) == n_pairs - 1)

    in_specs = [_resident((s, qw), lambda p: (0, q_off + p)), _resident((s, qw), lambda p: (0, k_off + p)),
                _resident((s, LANES), lambda p: (0, v_off + p))]
    args = [q, k, v]
    if has_bias:
        in_specs.append(_resident((1, 2, nb, 1, t), lambda p: (p, 0, 0, 0, 0)))
        args.append(bias)
    out_specs = [pl.BlockSpec((s, LANES), lambda p: (0, p)), pl.BlockSpec((1, 2, s, 1), lambda p: (p, 0, 0, 0))]
    out_shape = [jax.ShapeDtypeStruct((s, n_pairs * LANES), F32), jax.ShapeDtypeStruct((n_pairs, 2, s, 1), F32)]
    scratch = [pltpu.VMEM((nb, LANES, t), BF16)] + ([pltpu.VMEM((2, s, 1), F32)] if has_bias else [])
    scratch += _add_rider(rider, in_specs, args, out_specs, out_shape)
    return pl.pallas_call(
        body, name=name, grid=(n_pairs,), in_specs=in_specs, out_specs=out_specs, out_shape=out_shape,
        scratch_shapes=scratch,
        compiler_params=_params(("parallel",) if rider is None else ("arbitrary",)),
    )(*args)


def _flash_bwd(q, k, v, do, o, lse, bias, *, n_pairs, hw, q_off, k_off, v_off, scale, qk_dtype, name, rider=None,
               stacked=False):
    s = q.shape[0]
    t = min(ATT_T, s)
    nb = s // t
    qw = 2 * hw
    has_bias = bias is not None
    c1 = scale * LOG2E

    def body(*refs):
        n_grads = 1 if stacked else 3
        refs, ride_refs = _split_rider(refs, rider, n_in=7 if has_bias else 6, n_out=n_grads + (2 if has_bias else 0))
        if stacked:
            refs = list(refs)
            n_in = 7 if has_bias else 6
            refs[n_in:n_in + 1] = [refs[n_in].at[0], refs[n_in].at[1], refs[n_in].at[2]]
        if has_bias:
            (q_ref, k_ref, v_ref, do_ref, o_ref, lse_ref, b_ref, dq_ref, dk_ref, dv_ref, db_ref, dr_ref,
             dkt_ref, dvt_ref) = refs
            db_ref[...] = jnp.zeros_like(db_ref)
        else:
            q_ref, k_ref, v_ref, do_ref, o_ref, lse_ref, dq_ref, dk_ref, dv_ref, dkt_ref, dvt_ref = refs
            b_ref = db_ref = dr_ref = None
        _ride_start(rider, ride_refs, pl.program_id(0) == 0)
        dkt_ref[...] = jnp.zeros_like(dkt_ref)
        dvt_ref[...] = jnp.zeros_like(dvt_ref)
        causal = lax.broadcasted_iota(jnp.int32, (t, t), 1) <= lax.broadcasted_iota(jnp.int32, (t, t), 0)
        cmask = jnp.concatenate([causal, causal], axis=0)
        lane_lt_hw = lax.broadcasted_iota(jnp.int32, (t, qw), 1) < hw

        def q_block(i, _):
            r0 = pl.multiple_of(i * t, t)
            qs = _stack_heads(q_ref[pl.ds(r0, t), :], hw)
            dos = _stack_heads(do_ref[pl.ds(r0, t), :], 64)
            ot = o_ref[pl.ds(r0, t), :]
            delta = jnp.sum(dos * jnp.concatenate([ot, ot], axis=0), axis=-1, keepdims=True)
            lse2 = jnp.concatenate([lse_ref[0, 0, pl.ds(r0, t), :], lse_ref[0, 1, pl.ds(r0, t), :]], axis=0)
            dosb = dos.astype(BF16)
            dos_t = dos.T.astype(BF16)
            qs_t = qs.astype(F32).T.astype(BF16)

            def kv_step(j, carry, masked):
                dq, rsum = carry
                c0 = pl.multiple_of(j * t, t)
                kt = k_ref[pl.ds(c0, t), :]
                vt = v_ref[pl.ds(c0, t), :]
                sc = lax.dot_general(qs, kt, _NT, preferred_element_type=F32) * c1
                if has_bias:
                    sc = sc + _stacked_rows(b_ref[0, 0, j], b_ref[0, 1, j], t)
                if masked:
                    sc = jnp.where(cmask, sc, NEG_INF)
                p = jnp.exp2(sc - lse2)
                dp = lax.dot_general(dosb, vt, _NT, preferred_element_type=F32)
                ds = p * (dp - delta)
                dsb = ds.astype(BF16)
                pb = p.astype(BF16)
                if hw == LANES:
                    dvt_ref[j] += jnp.concatenate(
                        [jnp.dot(dos_t[:64, :t], pb[:t], preferred_element_type=F32),
                         jnp.dot(dos_t[64:, t:], pb[t:], preferred_element_type=F32)], axis=0)
                    dkt_ref[j] += jnp.concatenate(
                        [jnp.dot(qs_t[:hw, :t], dsb[:t], preferred_element_type=F32),
                         jnp.dot(qs_t[hw:, t:], dsb[t:], preferred_element_type=F32)], axis=0)
                else:
                    dvt_ref[j] += jnp.dot(dos_t, pb, preferred_element_type=F32)
                    dkt_ref[j] += jnp.dot(qs_t, dsb, preferred_element_type=F32)
                if has_bias:
                    db_ref[0, 0, j] += jnp.sum(ds[:t], axis=0, keepdims=True)
                    db_ref[0, 1, j] += jnp.sum(ds[t:], axis=0, keepdims=True)
                    rsum = rsum + jnp.sum(ds, axis=-1, keepdims=True)
                return dq + jnp.dot(dsb, kt, preferred_element_type=F32), rsum

            init = (jnp.zeros((2 * t, qw), F32), jnp.zeros((2 * t, 1), F32))
            carry = lax.fori_loop(0, i, functools.partial(kv_step, masked=False), init)
            dq, rsum = kv_step(i, carry, True)
            dq = dq * scale
            dq_ref[pl.ds(r0, t), :] = jnp.where(lane_lt_hw, dq[:t], dq[t:]).astype(qk_dtype)
            if has_bias:
                rsum_row = jnp.broadcast_to(rsum, (2 * t, LANES)).T[0:1]
                dr_ref[0, 0, i] = rsum_row[:, :t]
                dr_ref[0, 1, i] = rsum_row[:, t:]
            return 0

        lax.fori_loop(0, nb, q_block, 0)

        def k_block(j, _):
            c0 = pl.multiple_of(j * t, t)
            dk_ref[pl.ds(c0, t), :] = (dkt_ref[j].T * scale).astype(qk_dtype)
            dv_ref[pl.ds(c0, t), :] = dvt_ref[j].T.astype(BF16)
            return 0

        lax.fori_loop(0, nb, k_block, 0)
        _ride_wait(rider, ride_refs, pl.program_id(0) == n_pairs - 1)

    in_specs = [_resident((s, qw), lambda p: (0, q_off + p)), _resident((s, qw), lambda p: (0, k_off + p)),
                _resident((s, LANES), lambda p: (0, v_off + p)),
                _resident((s, LANES), lambda p: (0, p)), _resident((s, LANES), lambda p: (0, p)),
                _resident((1, 2, s, 1), lambda p: (p, 0, 0, 0))]
    args = [q, k, v, do, o, lse]
    if stacked:
        assert qw == LANES and qk_dtype == BF16
        out_specs = [pl.BlockSpec((3, s, LANES), lambda p: (0, 0, p))]
        out_shape = [jax.ShapeDtypeStruct((3, s, n_pairs * LANES), BF16)]
    else:
        out_specs = [pl.BlockSpec((s, qw), lambda p: (0, p)), pl.BlockSpec((s, qw), lambda p: (0, p)),
                     pl.BlockSpec((s, LANES), lambda p: (0, p))]
        out_shape = [jax.ShapeDtypeStruct((s, n_pairs * qw), qk_dtype), jax.ShapeDtypeStruct((s, n_pairs * qw), qk_dtype),
                     jax.ShapeDtypeStruct((s, n_pairs * LANES), BF16)]
    if has_bias:
        in_specs.append(_resident((1, 2, nb, 1, t), lambda p: (p, 0, 0, 0, 0)))
        args.append(bias)
        for _ in range(2):
            out_specs.append(pl.BlockSpec((1, 2, nb, 1, t), lambda p: (p, 0, 0, 0, 0)))
            out_shape.append(jax.ShapeDtypeStruct((n_pairs, 2, nb, 1, t), F32))
    scratch = [pltpu.VMEM((nb, qw, t), F32), pltpu.VMEM((nb, LANES, t), F32)]
    scratch += _add_rider(rider, in_specs, args, out_specs, out_shape)
    return pl.pallas_call(
        body, name=name, grid=(n_pairs,), in_specs=in_specs, out_specs=out_specs, out_shape=out_shape,
        scratch_shapes=scratch,
        compiler_params=_params(("parallel",) if rider is None else ("arbitrary",)),
    )(*args)


def _alibi_slope(h):
    return 2.0 ** (-8.0 * (h + 1.0) / SWA_HEADS)


SWA_ROWS = 512
SWA_SCALE = SWA_DIM ** -0.5


def _swa_geometry(i):
    w = WINDOW
    r0 = pl.multiple_of(i * w, w)
    b0 = pl.multiple_of(jnp.maximum(i - 1, 0) * w, w)
    row = lax.broadcasted_iota(jnp.int32, (w, 2 * w), 0)
    col = lax.broadcasted_iota(jnp.int32, (w, 2 * w), 1)
    dist = row - col + (r0 - b0)
    valid = (dist >= 0) & (dist < w)
    return r0, b0, dist.astype(F32), valid


def _swa_q_head(qblk, h):
    kv = h // (SWA_HEADS // SWA_KV_HEADS)
    if h % 2 != kv:
        qblk = pltpu.roll(qblk, 64, axis=1)
    return jnp.where(_head_mask(qblk.shape, kv), qblk, 0.0)


SWA_GROUP = SWA_HEADS // SWA_KV_HEADS


def _swa_stack(ref, rs, grp):
    parts = []
    for a in range(SWA_GROUP):
        h = SWA_GROUP * grp + a
        parts.append(_swa_q_head(ref[rs, (h // 2) * LANES:(h // 2 + 1) * LANES].astype(F32), h))
    return jnp.concatenate(parts, axis=0)


def _swa_unstack(x, grp):
    tiles = []
    for a in range(SWA_GROUP):
        h = SWA_GROUP * grp + a
        tile = x[a * WINDOW:(a + 1) * WINDOW]
        tiles.append(pltpu.roll(tile, 64, axis=1) if h % 2 != grp else tile)
    return tiles


def _swa_head_column(vals):
    return jnp.concatenate([jnp.full((WINDOW, 1), v, F32) for v in vals], axis=0)


def _swa_logits(qs, kb, dist, valid, grp):
    slopes = _swa_head_column([_alibi_slope(SWA_GROUP * grp + a) for a in range(SWA_GROUP)])
    dist4 = jnp.concatenate([dist] * SWA_GROUP, axis=0)
    valid4 = jnp.concatenate([valid] * SWA_GROUP, axis=0)
    sc = lax.dot_general(qs, kb, _NT, preferred_element_type=F32) * SWA_SCALE - slopes * dist4
    return jnp.where(valid4, sc, NEG_INF)


def _swa_merge_heads(tiles):
    lt64 = lax.broadcasted_iota(jnp.int32, (WINDOW, LANES), 1) < 64
    return jnp.concatenate([jnp.where(lt64, tiles[2 * b], tiles[2 * b + 1]) for b in range(SWA_HEADS // 2)], axis=1)


def _swa_fwd(z0b, sinks, *, name):
    s = z0b.shape[0]
    w = WINDOW
    rows = min(SWA_ROWS, s)
    per_step = rows // w
    qcols = SWA_HEADS * SWA_DIM

    def body(sink_ref, q_ref, k_ref, v_ref, o_ref, lse_ref):
        g = pl.program_id(0)
        for ii in range(per_step):
            rs = slice(ii * w, (ii + 1) * w)
            r0, b0, dist, valid = _swa_geometry(g * per_step + ii)
            kb = k_ref[pl.ds(b0, 2 * w), :]
            vb = v_ref[pl.ds(b0, 2 * w), :]
            o_tiles = []
            for h in range(SWA_HEADS):
                kv = h // SWA_GROUP
                qh = _swa_q_head(q_ref[rs, (h // 2) * LANES:(h // 2 + 1) * LANES].astype(F32), h).astype(BF16)
                sc = lax.dot_general(qh, kb, _NT, preferred_element_type=F32) * SWA_SCALE - _alibi_slope(h) * dist
                sc = jnp.where(valid, sc, NEG_INF)
                sink = sink_ref[0, h]
                m = jnp.maximum(jnp.max(sc, axis=-1, keepdims=True), sink)
                p = jnp.exp(sc - m)
                l = jnp.sum(p, axis=-1, keepdims=True) + jnp.exp(sink - m)
                oh = jnp.dot(p.astype(BF16), vb, preferred_element_type=F32) / l
                o_tiles.append(pltpu.roll(oh, 64, axis=1) if h % 2 != kv else oh)
                lse_ref[h, rs, :] = m + jnp.log(l)
            o_ref[rs, :] = _swa_merge_heads(o_tiles)

    return pl.pallas_call(
        body, name=name, grid=(s // rows,),
        in_specs=[pl.BlockSpec(memory_space=pltpu.SMEM),
                  pl.BlockSpec((rows, qcols), lambda g: (g, 0)),
                  pl.BlockSpec((s, LANES), lambda g: (0, 4)), pl.BlockSpec((s, LANES), lambda g: (0, 5))],
        out_specs=[pl.BlockSpec((rows, qcols), lambda g: (g, 0)), pl.BlockSpec((SWA_HEADS, rows, 1), lambda g: (0, g, 0))],
        out_shape=[jax.ShapeDtypeStruct((s, qcols), F32), jax.ShapeDtypeStruct((SWA_HEADS, s, 1), F32)],
        compiler_params=_params(("parallel",)),
    )(sinks, z0b, z0b, z0b)


def _swa_bwd(z0b, sinks, do, o, lse, *, name):
    s = z0b.shape[0]
    w = WINDOW
    rows = min(SWA_ROWS, s)
    per_step = rows // w
    qcols = SWA_HEADS * SWA_DIM
    nblk = s // w

    def body(sink_ref, q_ref, k_ref, v_ref, do_ref, o_ref, lse_ref, dq_ref, dkt_ref, dvt_ref, dsink_ref):
        g = pl.program_id(0)

        @pl.when(g == 0)
        def _():
            dkt_ref[...] = jnp.zeros_like(dkt_ref)
            dvt_ref[...] = jnp.zeros_like(dvt_ref)
            dsink_ref[...] = jnp.zeros_like(dsink_ref)

        for ii in range(per_step):
            i = g * per_step + ii
            rs = slice(ii * w, (ii + 1) * w)
            r0, b0, dist, valid = _swa_geometry(i)
            j0 = jnp.maximum(i - 1, 0)
            kb = k_ref[pl.ds(b0, 2 * w), :]
            vb = v_ref[pl.ds(b0, 2 * w), :]
            dq_tiles = []
            for grp in range(SWA_KV_HEADS):
                heads = [SWA_GROUP * grp + a for a in range(SWA_GROUP)]
                qs32 = _swa_stack(q_ref, rs, grp)
                dos32 = _swa_stack(do_ref, rs, grp)
                delta = jnp.sum(dos32 * _swa_stack(o_ref, rs, grp), axis=-1, keepdims=True)
                lse = jnp.concatenate([lse_ref[h, rs, :] for h in heads], axis=0)
                sink = _swa_head_column([sink_ref[0, h] for h in heads])
                p = jnp.exp(_swa_logits(qs32.astype(BF16), kb, dist, valid, grp) - lse)
                dp = lax.dot_general(dos32.astype(BF16), vb, _NT, preferred_element_type=F32)
                ds = p * (dp - delta)
                dsb = ds.astype(BF16)
                d_sink = jnp.exp(sink - lse) * delta
                for a, h in enumerate(heads):
                    dsink_ref[h:h + 1, :] += jnp.broadcast_to(-jnp.sum(d_sink[a * w:(a + 1) * w]), (1, LANES))
                dvt = jnp.dot(dos32.T.astype(BF16), p.astype(BF16), preferred_element_type=F32)
                dkt = jnp.dot(qs32.T.astype(BF16), dsb, preferred_element_type=F32) * SWA_SCALE
                dvt_ref[j0] += dvt[:, :w]
                dvt_ref[j0 + 1] += dvt[:, w:]
                dkt_ref[j0] += dkt[:, :w]
                dkt_ref[j0 + 1] += dkt[:, w:]
                dq_tiles += _swa_unstack(jnp.dot(dsb, kb, preferred_element_type=F32) * SWA_SCALE, grp)
            dq_ref[rs, :] = _swa_merge_heads(dq_tiles)

    return pl.pallas_call(
        body, name=name, grid=(s // rows,),
        in_specs=[pl.BlockSpec(memory_space=pltpu.SMEM),
                  pl.BlockSpec((rows, qcols), lambda g: (g, 0)),
                  pl.BlockSpec((s, LANES), lambda g: (0, 4)), pl.BlockSpec((s, LANES), lambda g: (0, 5)),
                  pl.BlockSpec((rows, qcols), lambda g: (g, 0)), pl.BlockSpec((rows, qcols), lambda g: (g, 0)),
                  pl.BlockSpec((SWA_HEADS, rows, 1), lambda g: (0, g, 0))],
        out_specs=[pl.BlockSpec((rows, qcols), lambda g: (g, 0)),
                   pl.BlockSpec((nblk, LANES, w), lambda g: (0, 0, 0)),
                   pl.BlockSpec((nblk, LANES, w), lambda g: (0, 0, 0)),
                   pl.BlockSpec((SWA_HEADS, LANES), lambda g: (0, 0))],
        out_shape=[jax.ShapeDtypeStruct((s, qcols), F32),
                   jax.ShapeDtypeStruct((nblk, LANES, w), F32), jax.ShapeDtypeStruct((nblk, LANES, w), F32),
                   jax.ShapeDtypeStruct((SWA_HEADS, LANES), F32)],
        compiler_params=_params(("arbitrary",)),
    )(sinks, z0b, z0b, z0b, do, o, lse)


CUM_T = 256


def _split3(x):
    hi = x.astype(BF16)
    r1 = x - hi.astype(F32)
    mid = r1.astype(BF16)
    lo = (r1 - mid.astype(F32)).astype(BF16)
    return hi, mid, lo


def _tri_dot(tri, x):
    hi, mid, lo = _split3(x)
    out = jnp.dot(tri, hi, preferred_element_type=F32)
    out = out + jnp.dot(tri, mid, preferred_element_type=F32)
    return out + jnp.dot(tri, lo, preferred_element_type=F32)


def _logf_fwd(zf, bf, *, name):
    s = zf.shape[0]
    t = CUM_T
    nb = s // t

    def body(z_ref, b_ref, c_ref, carry_ref):
        i = pl.program_id(0)

        @pl.when(i == 0)
        def _():
            carry_ref[...] = jnp.zeros_like(carry_ref)

        x = z_ref[...] + b_ref[...]
        lf = jnp.minimum(x, 0.0) - jnp.log(1.0 + jnp.exp(-jnp.abs(x)))
        row = lax.broadcasted_iota(jnp.int32, (t, t), 0)
        col = lax.broadcasted_iota(jnp.int32, (t, t), 1)
        tri = jnp.where(col <= row, 1.0, 0.0).astype(BF16)
        c = _tri_dot(tri, lf) + carry_ref[...]
        c_ref[...] = c
        carry_ref[...] = c[t - 1:t, :]

    return pl.pallas_call(
        body, name=name, grid=(nb,),
        in_specs=[pl.BlockSpec((t, LANES), lambda i: (i, 0)), pl.BlockSpec((1, LANES), lambda i: (0, 0))],
        out_specs=pl.BlockSpec((t, LANES), lambda i: (i, 0)),
        out_shape=jax.ShapeDtypeStruct((s, LANES), F32),
        scratch_shapes=[pltpu.VMEM((1, LANES), F32)],
        compiler_params=_params(("arbitrary",)),
    )(zf, bf)


def _logf_bwd(dc, zf, bf, *, name):
    s = zf.shape[0]
    t = CUM_T
    nb = s // t

    def body(dc_ref, z_ref, b_ref, dz_ref, db_ref, carry_ref):
        i = pl.program_id(0)

        @pl.when(i == 0)
        def _():
            carry_ref[...] = jnp.zeros_like(carry_ref)
            db_ref[...] = jnp.zeros_like(db_ref)

        row = lax.broadcasted_iota(jnp.int32, (t, t), 0)
        col = lax.broadcasted_iota(jnp.int32, (t, t), 1)
        tri = jnp.where(col >= row, 1.0, 0.0).astype(BF16)
        dlf = _tri_dot(tri, dc_ref[...]) + carry_ref[...]
        carry_ref[...] = dlf[0:1, :]
        x = z_ref[...] + b_ref[...]
        dz = dlf * _sigmoid(-x)
        dz_ref[...] = dz.astype(BF16)
        db_ref[...] += jnp.sum(dz, axis=0, keepdims=True)

    return pl.pallas_call(
        body, name=name, grid=(nb,),
        in_specs=[pl.BlockSpec((t, LANES), lambda i: (nb - 1 - i, 0)), pl.BlockSpec((t, LANES), lambda i: (nb - 1 - i, 0)),
                  pl.BlockSpec((1, LANES), lambda i: (0, 0))],
        out_specs=[pl.BlockSpec((t, LANES), lambda i: (nb - 1 - i, 0)), pl.BlockSpec((1, LANES), lambda i: (0, 0))],
        out_shape=[jax.ShapeDtypeStruct((s, LANES), BF16), jax.ShapeDtypeStruct((1, LANES), F32)],
        scratch_shapes=[pltpu.VMEM((1, LANES), F32)],
        compiler_params=_params(("arbitrary",)),
    )(dc, zf, bf)


def _sum_pieces(p_ref):
    g = p_ref[0].astype(F32)
    for k in range(1, N_DEV):
        g = g + p_ref[k].astype(F32)
    return g


def _adam_update(g, w, m, v):
    bc1 = 1.0 - ADAM_B1 ** ADAM_STEP
    bc2 = 1.0 - ADAM_B2 ** ADAM_STEP
    nm = ADAM_B1 * m + (1.0 - ADAM_B1) * g
    nv = ADAM_B2 * v + (1.0 - ADAM_B2) * (g * g)
    m_hat = nm / bc1
    v_hat = nv / bc2
    return -ADAM_LR * (m_hat / (jnp.sqrt(v_hat) + ADAM_EPS) + ADAM_WD * w), nm, nv


def _adamw(pieces, w, m, v, *, name):
    rows, cols = w.shape
    tr = _tile(rows, (RB1, RB0, SMALL_ROWS))

    def body(p_ref, w_ref, m_ref, v_ref, g_ref, d_ref, nm_ref, nv_ref):
        g = _sum_pieces(p_ref)
        g_ref[...] = g
        d_ref[...], nm_ref[...], nv_ref[...] = _adam_update(g, w_ref[...], m_ref[...], v_ref[...])

    spec = pl.BlockSpec((tr, cols), lambda i: (i, 0))
    shape = jax.ShapeDtypeStruct((rows, cols), F32)
    return pl.pallas_call(
        body, name=name, grid=(rows // tr,),
        in_specs=[pl.BlockSpec((N_DEV, tr, cols), lambda i: (0, i, 0)), spec, spec, spec],
        out_specs=[spec, spec, spec, spec], out_shape=[shape, shape, shape, shape],
        compiler_params=_params(("parallel",)),
    )(pieces, w, m, v)


def _sum8(pieces, rows, *, name):
    cols = pieces.shape[2]
    tr = _tile(rows, (176, 96))

    def body(p_ref, g_ref):
        g_ref[...] = _sum_pieces(p_ref)

    return pl.pallas_call(
        body, name=name, grid=(rows // tr,),
        in_specs=[pl.BlockSpec((N_DEV, tr, cols), lambda i: (0, i, 0))],
        out_specs=pl.BlockSpec((tr, cols), lambda i: (i, 0)),
        out_shape=jax.ShapeDtypeStruct((rows, cols), F32),
        compiler_params=_params(("parallel",)),
    )(pieces)


def _adamw_columns(g, w, m, v, *, name):
    n, _, k = w.shape
    tr = n // 2

    def body(g_ref, w_ref, m_ref, v_ref, d_ref, nm_ref, nv_ref):
        d_ref[...], nm_ref[...], nv_ref[...] = _adam_update(g_ref[...], w_ref[...], m_ref[...], v_ref[...])

    spec = pl.BlockSpec((tr, 1, k), lambda i: (i, 0, 0))
    shape = jax.ShapeDtypeStruct((n, 1, k), F32)
    return pl.pallas_call(
        body, name=name, grid=(n // tr,), in_specs=[spec, spec, spec, spec],
        out_specs=[spec, spec, spec], out_shape=[shape, shape, shape],
        compiler_params=_params(("parallel",)),
    )(g, w, m, v)


MESH = pl.DeviceIdType.MESH
ANY = pl.BlockSpec(memory_space=pl.ANY)


def _all_gather(shard, *, name):
    rows, lanes = shard.shape

    def body(x_ref, out_ref, send_sems, recv_sems, local_sem):
        x, y, c = lax.axis_index("x"), lax.axis_index("y"), lax.axis_index("c")
        me, sibling = (x, y, c), (x, y, 1 - c)
        chips = [(1 - x, y), (x, 1 - y), (1 - x, 1 - y)]

        def block(px, py, pc):
            return out_ref.at[4 * px + 2 * py + pc]

        def copy(k, blk, to, src=None):
            return pltpu.make_async_remote_copy(
                src_ref=block(*blk) if src is None else src, dst_ref=block(*blk),
                send_sem=send_sems.at[k], recv_sem=recv_sems.at[k], device_id=to, device_id_type=MESH)

        mine = pltpu.make_async_copy(x_ref, block(*me), local_sem)
        mine.start()
        first = [copy(0, me, sibling, src=x_ref)]
        first += [copy(1 + j, me, (*chip, c), src=x_ref) for j, chip in enumerate(chips)]
        for cp in first:
            cp.start()
        passed = [copy(4 + j, (*chip, c), sibling) for j, chip in enumerate(chips)]
        for j, chip in enumerate(chips):
            copy(1 + j, (*chip, c), me).wait_recv()
            passed[j].start()
        copy(0, sibling, me).wait_recv()
        for j, chip in enumerate(chips):
            copy(4 + j, (*chip, 1 - c), me).wait_recv()
        for cp in first + passed:
            cp.wait_send()
        mine.wait()

    return pl.pallas_call(
        body, name=name, out_shape=jax.ShapeDtypeStruct((N_DEV, rows, lanes), shard.dtype),
        in_specs=[ANY], out_specs=ANY,
        scratch_shapes=[pltpu.SemaphoreType.DMA((7,)), pltpu.SemaphoreType.DMA((7,)), pltpu.SemaphoreType.DMA(())],
    )(shard)


def _peer_copies(kind, src_ref, out_ref, send_sems, recv_sems, local_sem):
    x, y, c = lax.axis_index("x"), lax.axis_index("y"), lax.axis_index("c")
    me = 4 * x + 2 * y + c

    def src(idx):
        return src_ref.at[idx] if kind == "exchange" else src_ref

    mine = None if local_sem is None else pltpu.make_async_copy(src(me), out_ref.at[me], local_sem)
    copies = []
    for r in range(1, N_DEV):
        px = 1 - x if r & 4 else x
        py = 1 - y if r & 2 else y
        pc = 1 - c if r & 1 else c
        copies.append(pltpu.make_async_remote_copy(
            src_ref=src(4 * px + 2 * py + pc), dst_ref=out_ref.at[me],
            send_sem=send_sems.at[r - 1], recv_sem=recv_sems.at[r - 1],
            device_id=(px, py, pc), device_id_type=MESH))
    return mine, copies


PEER_SEMS = [pltpu.SemaphoreType.DMA((7,)), pltpu.SemaphoreType.DMA((7,)), pltpu.SemaphoreType.DMA(())]


HBM = pl.BlockSpec(memory_space=pltpu.HBM)
SEMAPHORES = pl.BlockSpec(memory_space=pltpu.SEMAPHORE)


def _peer_start(kind, arr, *, name):
    land = lax.empty((N_DEV,) + arr.shape[-2:], arr.dtype)

    def body(src_ref, land_ref, send_sems, recv_sems, src_thru, land_thru, token):
        _, copies = _peer_copies(kind, src_ref, land_ref, send_sems, recv_sems, None)
        for cp in copies:
            cp.start()
        token[...] = jnp.zeros_like(token)

    return pl.pallas_call(
        body, name=name,
        out_shape=(pltpu.SemaphoreType.DMA((N_DEV - 1,)), pltpu.SemaphoreType.DMA((N_DEV - 1,)),
                   pltpu.HBM(arr.shape, arr.dtype), pltpu.HBM(land.shape, land.dtype), jax.ShapeDtypeStruct((8, LANES), F32)),
        in_specs=(HBM, HBM), out_specs=(SEMAPHORES, SEMAPHORES, HBM, HBM, pl.BlockSpec(memory_space=pltpu.VMEM)),
        input_output_aliases={0: 2, 1: 3},
        compiler_params=pltpu.CompilerParams(has_side_effects=pltpu.SideEffectType.DATAFLOW_SIDE_EFFECTING),
    )(pltpu.with_memory_space_constraint(arr, pltpu.HBM), pltpu.with_memory_space_constraint(land, pltpu.HBM))


def _peer_wait(kind, send_sems, recv_sems, src_thru, land_thru, after, *, name):
    def body(src_ref, land_ref, send_sems, recv_sems, *_):
        _, copies = _peer_copies(kind, src_ref, land_ref, send_sems, recv_sems, None)
        for cp in copies:
            cp.wait_send()
            cp.wait_recv()

    return pl.pallas_call(
        body, name=name,
        out_shape=(pltpu.HBM(src_thru.shape, src_thru.dtype), pltpu.HBM(land_thru.shape, land_thru.dtype)),
        in_specs=(HBM, HBM, SEMAPHORES, SEMAPHORES) + (ANY,) * len(after), out_specs=(HBM, HBM),
        input_output_aliases={0: 0, 1: 1},
        compiler_params=pltpu.CompilerParams(has_side_effects=pltpu.SideEffectType.DATAFLOW_SIDE_EFFECTING),
    )(src_thru, land_thru, send_sems, recv_sems, *after)


def _add_rider(rider, in_specs, args, out_specs, out_shape):
    if rider is None:
        return []
    _, arr = rider
    in_specs.append(ANY)
    args.append(arr)
    out_specs.append(ANY)
    out_shape.append(jax.ShapeDtypeStruct((N_DEV,) + arr.shape[-2:], arr.dtype))
    return list(PEER_SEMS)


def _split_rider(refs, rider, n_in, n_out):
    if rider is None:
        return refs, None
    refs = list(refs)
    rin = refs.pop(n_in)
    rout = refs.pop(n_in + n_out)
    return refs[:-3], (rin, rout, *refs[-3:])


def _ride_start(rider, ride_refs, first):
    if rider is None:
        return

    @pl.when(first)
    def _():
        mine, copies = _peer_copies(rider[0], *ride_refs)
        mine.start()
        for cp in copies:
            cp.start()


def _ride_wait(rider, ride_refs, last):
    if rider is None:
        return

    @pl.when(last)
    def _():
        mine, copies = _peer_copies(rider[0], *ride_refs)
        for cp in copies:
            cp.wait()
        mine.wait()


def _gathered_cols(blocks, kdim):
    n = blocks.shape[1] * WIDE // kdim
    return blocks.reshape(N_DEV, kdim, n).transpose(1, 0, 2).reshape(kdim, N_DEV * n)


def _scatter_cols(dw):
    kdim, n8 = dw.shape
    n = n8 // N_DEV
    return dw.reshape(kdim, N_DEV, n).transpose(1, 0, 2).reshape(N_DEV, kdim * n // WIDE, WIDE)


def _pad_rows(a, rows):
    pad = [(0, 0)] * a.ndim
    pad[-2] = (0, rows - a.shape[-2])
    return jnp.pad(a, pad)


def _layer0_in_weight_t(wt):
    cq, ckv, kpe = wt[0:256], wt[256:384], wt[384:416]
    q_s, k_s, v_s, gate = wt[416:928], wt[928:1056], wt[1056:1184], wt[1184:2208]
    z = jnp.zeros((64, wt.shape[1]), wt.dtype)
    return jnp.concatenate([gate, cq, ckv, z, kpe, z[:32], q_s, k_s, v_s], axis=0)


def _layer0_in_grad_t(dwt):
    gate, cq, ckv, kpe = dwt[0:1024], dwt[1024:1280], dwt[1280:1408], dwt[1472:1504]
    q_s, k_s, v_s = dwt[1536:2048], dwt[2048:2176], dwt[2176:2304]
    return jnp.concatenate([cq, ckv, kpe, q_s, k_s, v_s, gate], axis=0)


def _layer1_in_weight_t(wt):
    main = jnp.concatenate([wt[:3 * D_MODEL], wt[3 * D_MODEL + FOX_HEADS:]], axis=0)
    return main, _pad_rows(wt[3 * D_MODEL:3 * D_MODEL + FOX_HEADS], LANES)


def _layer1_in_grad_t(d_blocks, d_wft):
    return jnp.concatenate([d_blocks[0], d_wft[:FOX_HEADS], d_blocks[1]], axis=0)


def _q_up_weight(w):
    return jnp.pad(w.reshape(MLA_Q_RANK, MLA_HEADS, 96), ((0, 0), (0, 0), (0, 32))).reshape(MLA_Q_RANK, MLA_HEADS * LANES)


def _q_up_grad(dwp):
    return dwp.reshape(MLA_Q_RANK, MLA_HEADS, LANES)[:, :, :96].reshape(MLA_Q_RANK, MLA_HEADS * 96)


def _kv_up_weight(w):
    w4 = w.reshape(MLA_KV_RANK, MLA_HEADS, 2, 64)
    kp = jnp.pad(w4[:, :, 0, :], ((0, 0), (0, 0), (0, 64))).reshape(MLA_KV_RANK, MLA_HEADS * LANES)
    vp = w4[:, :, 1, :].reshape(MLA_KV_RANK, MLA_HEADS * 64)
    return jnp.concatenate([kp, vp], axis=1)


def _kv_up_grad(dwp):
    dk = dwp[:, :MLA_HEADS * LANES].reshape(MLA_KV_RANK, MLA_HEADS, LANES)[:, :, :64]
    dv = dwp[:, MLA_HEADS * LANES:].reshape(MLA_KV_RANK, MLA_HEADS, 64)
    return jnp.stack([dk, dv], axis=2).reshape(MLA_KV_RANK, MLA_HEADS * LANES)


def _pad_lanes(a):
    return jnp.pad(a, ((0, 0), (0, LANES - a.shape[1])))


def _small_pack(g_in, g_final, g_q_a, g_kv_a, sinks, b_f, loss):
    rows = [g_in.reshape(8, LANES), g_final.reshape(8, LANES), g_q_a.reshape(2, LANES), g_kv_a.reshape(1, LANES),
            _pad_lanes(sinks.reshape(1, -1)), _pad_lanes(b_f.reshape(1, -1)), _pad_lanes(loss.reshape(1, 1)),
            jnp.zeros((2, LANES), F32)]
    return jnp.concatenate(rows, axis=0)


def _small_unpack(a):
    return (a[0:8].reshape(1, D_MODEL), a[8:16].reshape(D_MODEL), a[16:18].reshape(1, MLA_Q_RANK),
            a[18:19].reshape(1, MLA_KV_RANK), a[19:20, :SWA_HEADS], a[20:21, :FOX_HEADS], a[21, 0])


def _local_step(x, positions, target, e_g_in, early, e_g_q_a, e_g_kv_a, e_sinks,
                late, o_b_f, g_final, scatter1=None, scatter0=None):
    s = x.shape[0]
    mla_scale = (MLA_NOPE + MLA_ROPE) ** -0.5
    fox_scale = FOX_DIM ** -0.5
    n0a = Z0A_UNITS * LANES

    inv_freq = 1.0 / (ROPE_THETA ** (jnp.arange(0, MLA_ROPE, 2, dtype=F32) / MLA_ROPE))
    ang = positions.astype(F32)[:, None] * inv_freq
    cos, sin = jnp.cos(ang), jnp.sin(ang)
    ones, zeros = jnp.ones((s, 64), F32), jnp.zeros((s, 64), F32)
    cos_t = jnp.concatenate([ones, cos, cos, ones[:, :32]], axis=1)
    sin_t = jnp.concatenate([zeros, -sin, sin, zeros[:, :32]], axis=1)

    if len(early) == 3:
        h0 = _rmsnorm_fwd(x, e_g_in, width=D_MODEL, col_blk=0, name="l0_norm")
        w0t, wq, wkv = early
    else:
        pending, token, unpack, prep = early
        h0 = _rmsnorm_fwd(x, e_g_in, width=D_MODEL, col_blk=0, name="l0_norm", after=[token])
        w0t, wq, wkv = unpack(*_peer_wait("gather", *pending, after=[h0] + prep, name="weights0_wait"))
    z0a, z0b = _matmul_rows([(h0, w0t, True)], [], [], lambda r: (r[:, :n0a], r[:, n0a:]),
                            [("rows", n0a, F32), ("rows", Z0B_UNITS * LANES, BF16)], name="l0_in")
    cqn = _rmsnorm_fwd(z0a, e_g_q_a, width=MLA_Q_RANK, col_blk=4, name="l0_q_norm")
    ckvn = _rmsnorm_fwd(z0a, e_g_kv_a, width=MLA_KV_RANK, col_blk=10, name="l0_kv_norm")
    qp = _matmul(cqn, wq, name="l0_q_up")
    kvp = _matmul(ckvn, wkv, name="l0_kv_up", out_dtype=BF16)
    qm, km = _rope_fwd(qp, kvp, z0a, cos_t, sin_t, name="l0_rope")
    gathers = len(late) == 2
    res = _flash_fwd(qm, km, kvp, None, n_pairs=MLA_HEADS // 2, hw=LANES, q_off=0, k_off=0, v_off=MLA_HEADS,
                     scale=mla_scale, name="l0_mla_fwd", rider=("gather", late[0]) if gathers else None)
    o_mla, lse_mla = res[0], res[1]
    wo0, o_g_in, w1t, wft, wo1 = late[1](res[2]) if gathers else late
    o_swa, lse_swa = _swa_fwd(z0b, e_sinks, name="l0_swa_fwd")
    half = D_MODEL // 2

    x1, h1, og0 = _matmul_rows(
        [(None, wo0, False)], [(o_mla, half, 0), (o_swa, half, 0), (z0a, D_MODEL, 0), (x, D_MODEL, 0)], [o_g_in],
        lambda r, om, osw, gt, xt, g, made: (*_residual_norm_epilogue(r, xt, g), made),
        [("rows", D_MODEL, F32), ("rows", D_MODEL, BF16), ("rows", D_MODEL, BF16)], name="l0_out",
        prologue=lambda om, osw, gt, xt, g: _gated([om, osw], gt))
    z1, gate1, zf = _matmul_rows(
        [(None, w1t, True), (None, wft, True)], [(h1, D_MODEL, 0)], [],
        lambda r, h, made: (r[0][:, :3 * D_MODEL], r[0][:, 3 * D_MODEL:], r[1]),
        [("rows", 3 * D_MODEL, BF16), ("rows", D_MODEL, F32), ("rows", LANES, F32)], name="l1_in",
        prologue=lambda h: h, separate=True)
    bf = _pad_lanes(o_b_f)
    log_cum = _logf_fwd(zf, bf, name="l1_logf")
    bias2 = (-LOG2E * log_cum[:, :FOX_HEADS]).T
    t_bwd = min(ATT_T, s)
    bias = bias2.reshape(FOX_HEADS // 2, 2, s // t_bwd, 1, t_bwd)
    t_fwd = _fwd_tile(s)
    o_fox, lse_fox = _flash_fwd(z1, z1, z1, bias2.reshape(FOX_HEADS // 2, 2, s // t_fwd, 1, t_fwd),
                                n_pairs=FOX_HEADS // 2, hw=64, q_off=0, k_off=8, v_off=16, scale=fox_scale,
                                name="l1_fox_fwd")

    dx2, loss_part, d_g_final, og1, dx2_bf = _matmul_rows(
        [(None, wo1, False)], [(o_fox, D_MODEL, 0), (gate1, D_MODEL, 0), (x1, D_MODEL, 0), (target, D_MODEL, 0)],
        [g_final.reshape(1, D_MODEL)],
        lambda r, o, gt, xt, tg, g, made: _and_first(_loss_epilogue(r, xt, tg, g), made),
        [("rows", D_MODEL, F32), ("sum", (8, LANES)), ("sum", (1, D_MODEL)), ("rows", D_MODEL, BF16),
         ("rows", D_MODEL, BF16)], name="l1_out_loss", prologue=lambda o, gt, xt, tg, g: _gated([o], gt))

    d_wo1 = _matmul(og1, dx2_bf, ta=True, out_dtype=BF16, name="l1_out_dw")
    do_fox, d_gate1 = _matmul_rows([(dx2_bf, wo1, True)], [(o_fox, D_MODEL, 0), (gate1, D_MODEL, 0)], [],
                                   _gate_bwd_epilogue([D_MODEL]), [("rows", D_MODEL, F32), ("rows", D_MODEL, BF16)],
                                   name="l1_out_dx")
    dqkv1, dbias, drow = _flash_bwd(z1, z1, z1, do_fox, o_fox, lse_fox, bias, n_pairs=FOX_HEADS // 2, hw=64, q_off=0,
                                    k_off=8, v_off=16, scale=fox_scale, qk_dtype=BF16, stacked=True, name="l1_fox_bwd")
    d_log_cum = (drow.reshape(FOX_HEADS, s) - dbias.reshape(FOX_HEADS, s)).T
    d_log_cum = jnp.pad(d_log_cum, ((0, 0), (0, LANES - FOX_HEADS)))
    d_zf, d_bf = _logf_bwd(d_log_cum, zf, bf, name="l1_logf_bwd")
    d_w1t = (_matmul(dqkv1, h1, ta=True, out_dtype=BF16, name="l1_in_dw_qkv"),
             _matmul(d_gate1, h1, ta=True, out_dtype=BF16, name="l1_in_dw_gate"))
    d_wft = _matmul(d_zf, h1, ta=True, out_dtype=BF16, name="l1_in_f_dw")
    dx1, d_o_g_in, dx1_bf = _matmul_rows([(dqkv1, w1t, False, c * D_MODEL, c) for c in range(3)]
                                         + [(d_gate1, w1t, False, 3 * D_MODEL), (d_zf, wft, False)],
                                         [(x1, D_MODEL, 0), (dx2, D_MODEL, 0)], [o_g_in],
                                         lambda *a: _and_first(_rms_bwd_epilogue(*a)),
                                         [("rows", D_MODEL, F32), ("sum", (1, D_MODEL)), ("rows", D_MODEL, BF16)],
                                         name="l1_in_dx")

    d_wo0 = _matmul(og0, dx1_bf, ta=True, out_dtype=BF16, name="l0_out_dw")
    do_mla, do_swa, d_gate0 = _matmul_rows(
        [(dx1_bf, wo0, True)], [(o_mla, half, 0), (o_swa, half, 0), (z0a, D_MODEL, 0)], [], _gate_bwd_epilogue([half, half]),
        [("rows", half, F32), ("rows", half, F32), ("rows", D_MODEL, BF16)], name="l0_out_dx")
    dq_s, dkt_s, dvt_s, d_sinks = _swa_bwd(z0b, e_sinks, do_swa, o_swa, lse_swa, name="l0_swa_bwd")
    dk_s = dkt_s.transpose(0, 2, 1).reshape(s, LANES)
    dv_s = dvt_s.transpose(0, 2, 1).reshape(s, LANES)
    rider = None
    if scatter1 is not None:
        rider = ("exchange", scatter1(dict(w1t=d_w1t, wft=d_wft, wo1=d_wo1, o_g_in=d_o_g_in, wo0=d_wo0)))
    res = _flash_bwd(qm, km, kvp, do_mla, o_mla, lse_mla, None, n_pairs=MLA_HEADS // 2, hw=LANES, q_off=0, k_off=0,
                     v_off=MLA_HEADS, scale=mla_scale, qk_dtype=F32, name="l0_mla_bwd", rider=rider)
    dqm, dkm, dvm = res[0], res[1], res[2]
    recv1 = res[3] if rider is not None else None
    d_qp, d_kvp, d_kpe = _rope_bwd(dqm, dkm, dvm, cos_t, sin_t, name="l0_rope_bwd")
    d_wq = _matmul(cqn, d_qp, ta=True, out_dtype=BF16, name="l0_q_up_dw")
    d_cqn = _matmul(d_qp, wq, tb=True, name="l0_q_up_dx")
    d_wkv = _matmul(ckvn, d_kvp, ta=True, out_dtype=BF16, name="l0_kv_up_dw")
    d_ckvn = _matmul(d_kvp, wkv, tb=True, name="l0_kv_up_dx")
    d_cq, d_g_q_a = _rmsnorm_bwd(z0a, e_g_q_a, d_cqn, width=MLA_Q_RANK, col_blk=4, name="l0_q_norm_bwd")
    d_ckv, d_g_kv_a = _rmsnorm_bwd(z0a, e_g_kv_a, d_ckvn, width=MLA_KV_RANK, col_blk=10, name="l0_kv_norm_bwd")
    dz0 = jnp.concatenate([d_gate0, d_cq, d_ckv, d_kpe, dq_s.astype(BF16), dk_s.astype(BF16), dv_s.astype(BF16)], axis=1)
    d_w0t = _matmul(dz0, h0, ta=True, out_dtype=BF16, name="l0_in_dw")
    pending0, after_start = None, []
    if scatter0 is not None:
        *pending0, token = _peer_start("exchange", scatter0(dict(w0t=d_w0t, wq=d_wq, wkv=d_wkv)), name="grads0_start")
        after_start = [token]
    grad_x, d_e_g_in = _matmul_rows(
        [(dz0, w0t, False)], [(x, D_MODEL, 0), (dx1, D_MODEL, 0)], [e_g_in] + after_start,
        lambda dy, xt, add, g, *_: _rms_bwd_epilogue(dy, xt, add, g),
        [("rows", D_MODEL, F32), ("sum", (1, D_MODEL))], name="l0_in_dx")

    return dict(pending0=pending0, recv1=recv1, loss=loss_part[0, 0], grad_x=grad_x, e_g_in=d_e_g_in, w0t=d_w0t, e_g_q_a=d_g_q_a, wq=d_wq,
                e_g_kv_a=d_g_kv_a, wkv=d_wkv, e_sinks=d_sinks[:, 0].reshape(1, SWA_HEADS), wo0=d_wo0,
                o_g_in=d_o_g_in, w1t=d_w1t, wft=d_wft, o_b_f=d_bf[:, :FOX_HEADS], wo1=d_wo1, g_final=d_g_final.reshape(D_MODEL))


def _wide(a, rows):
    flat = a.reshape(-1)
    return jnp.pad(flat, (0, rows * WIDE - flat.shape[0])).reshape(rows, WIDE)


def _rows_b0(w_q, w_kv):
    return jnp.concatenate([_wide(w_q, 32), _wide(w_kv, 16)], axis=0)


def _unflat_b0(f):
    return f[0:24].reshape(1, MLA_Q_RANK, 96), f[32:48].reshape(1, MLA_KV_RANK, 128)


def _rows_b1(o_w_out, e_w_out, g_in):
    return jnp.concatenate([o_w_out, e_w_out, _wide(g_in, 16)], axis=0)


def _unflat_b1(f):
    return f[0:128][None], f[128:256][None], f[256:257, :LANES]


def kernel(x, positions, e_g_in, e_w_in, e_g_q_a, e_w_q_up, e_g_kv_a, e_w_kv_up, e_sinks, e_w_out, o_g_in, o_w_in, o_b_f, o_w_out, g_final, loss_target, m_e_g_in, m_e_w_in, m_e_g_q_a, m_e_w_q_up, m_e_g_kv_a, m_e_w_kv_up, m_e_sinks, m_e_w_out, m_o_g_in, m_o_w_in, m_o_b_f, m_o_w_out, m_g_final, v_e_g_in, v_e_w_in, v_e_g_q_a, v_e_w_q_up, v_e_g_kv_a, v_e_w_kv_up, v_e_sinks, v_e_w_out, v_o_g_in, v_o_w_in, v_o_b_f, v_o_w_out, v_g_final):
    def bf(a):
        return a.astype(BF16)

    me = 4 * lax.axis_index("x") + 2 * lax.axis_index("y") + lax.axis_index("c")
    shard0 = jnp.concatenate([_pad_rows(bf(e_w_in[0]).T, RA0), _rows_b0(bf(e_w_q_up[0]), bf(e_w_kv_up[0]))], axis=0)
    *pending_w0, token_w0 = _peer_start("gather", shard0, name="weights0_start")

    def unpack0(sent, gath0):
        gath0 = lax.dynamic_update_slice_in_dim(gath0, sent[None], me, axis=0)
        w0t = _layer0_in_weight_t(gath0[:, :N_E_IN].reshape(N_DEV * N_E_IN, WIDE))
        wq = _q_up_weight(_gathered_cols(gath0[:, RA0:RA0 + 24], MLA_Q_RANK))
        wkv = _kv_up_weight(_gathered_cols(gath0[:, RA0 + 32:RA0 + 48], MLA_KV_RANK))
        return w0t, wq, wkv

    rows_b0 = [_rows_b0(q[0], kv[0]) for q, kv in ((e_w_q_up, e_w_kv_up), (m_e_w_q_up, m_e_w_kv_up), (v_e_w_q_up, v_e_w_kv_up))]
    rows_b1 = [_rows_b1(o[0], e[0], g) for o, e, g in ((o_w_out, e_w_out, o_g_in), (m_o_w_out, m_e_w_out, m_o_g_in),
                                                       (v_o_w_out, v_e_w_out, v_o_g_in))]

    g_bits = lax.bitcast_convert_type(o_g_in.reshape(LANES), BF16)
    shard1 = jnp.concatenate([_pad_rows(bf(o_w_in[0]).T, RA1), _rows_b1(bf(o_w_out[0]), bf(e_w_out[0]), g_bits)], axis=0)

    def unpack1(gath1):
        w1t, wft = _layer1_in_weight_t(gath1[:, :N_O_IN].reshape(N_DEV * N_O_IN, WIDE))
        wo1 = gath1[:, RA1:RA1 + 128].reshape(D_MODEL, D_MODEL)
        wo0 = gath1[:, RA1 + 128:RA1 + 256].reshape(D_MODEL, D_MODEL)
        bits = gath1[:, RA1 + 256, :2 * LANES].reshape(N_DEV, LANES, 2)
        return wo0, lax.bitcast_convert_type(bits, F32).reshape(1, D_MODEL), w1t, wft, wo1

    def scatter1(g):
        d_in_t = _layer1_in_grad_t(g["w1t"], g["wft"]).reshape(N_DEV, N_O_IN, WIDE)
        d_o_g = jnp.pad(bf(g["o_g_in"]).reshape(N_DEV, 1, LANES), ((0, 0), (0, 15), (0, WIDE - LANES)))
        return jnp.concatenate([_pad_rows(d_in_t, RA1), g["wo1"].reshape(N_DEV, 128, WIDE),
                                g["wo0"].reshape(N_DEV, 128, WIDE), d_o_g], axis=1)

    def scatter0(g):
        return jnp.concatenate([
            _pad_rows(_layer0_in_grad_t(g["w0t"]).reshape(N_DEV, N_E_IN, WIDE), RA0),
            _pad_rows(_scatter_cols(_q_up_grad(g["wq"])), 32), _scatter_cols(_kv_up_grad(g["wkv"]))], axis=1)

    gr = _local_step(x[0], positions[0], loss_target[0], e_g_in,
                     (pending_w0, token_w0, unpack0, [shard1] + rows_b0 + rows_b1), e_g_q_a, e_g_kv_a, e_sinks,
                     (shard1, unpack1), o_b_f, g_final, scatter1=scatter1, scatter0=scatter0)

    def in_projection(recv, ra, n, w, m, v, name):
        g = _sum8(recv, ra, name=name + "_grad_sum")[:n].reshape(n, 1, D_MODEL)
        w, m, v = [jnp.transpose(a, (2, 0, 1)) for a in (w, m, v)]
        return (g, *_adamw_columns(g, w, m, v, name=name + "_adamw"))

    o_in = in_projection(gr["recv1"], RA1, N_O_IN, o_w_in, m_o_w_in, v_o_w_in, "o_w_in")
    b1 = _adamw(gr["recv1"][:, RA1:], *rows_b1, name="adamw_late")

    small = _small_pack(gr["e_g_in"], gr["g_final"], gr["e_g_q_a"], gr["e_g_kv_a"], gr["e_sinks"], gr["o_b_f"], gr["loss"])
    small_all = _all_gather(small, name="small_all_gather")
    zero = jnp.zeros((), F32)
    w_small = _small_pack(e_g_in, g_final, e_g_q_a, e_g_kv_a, e_sinks, o_b_f, zero)
    m_small = _small_pack(m_e_g_in, m_g_final, m_e_g_q_a, m_e_g_kv_a, m_e_sinks, m_o_b_f, zero)
    v_small = _small_pack(v_e_g_in, v_g_final, v_e_g_q_a, v_e_g_kv_a, v_e_sinks, v_o_b_f, zero)
    smalls = _adamw(small_all, w_small, m_small, v_small, name="adamw_replicated")
    g_sm, d_sm, m_sm, v_sm = [_small_unpack(a) for a in smalls]
    loss = g_sm[6]

    sent0, recv0 = _peer_wait("exchange", *gr["pending0"], after=[o_in[1], b1[1], smalls[1]], name="grads0_wait")
    own = lax.dynamic_slice_in_dim(sent0, me, 1, axis=0)
    recv0 = lax.dynamic_update_slice_in_dim(recv0, own, me, axis=0)
    e_in = in_projection(recv0, RA0, N_E_IN, e_w_in, m_e_w_in, v_e_w_in, "e_w_in")
    b0 = _adamw(recv0[:, RA0:], *rows_b0, name="adamw_early")

    def sharded(k):
        q_up, kv_up = _unflat_b0(b0[k])
        o_out, e_out, o_g = _unflat_b1(b1[k])
        return jnp.transpose(e_in[k], (1, 2, 0)), q_up, kv_up, e_out, jnp.transpose(o_in[k], (1, 2, 0)), o_out, o_g

    g_sh, d_sh, m_sh, v_sh = [sharded(k) for k in range(4)]

    def leaves(sh, sm):
        return (sm[0], sh[0], sm[2], sh[1], sm[3], sh[2], sm[4], sh[3], sh[6], sh[4], sm[5], sh[5], sm[1])

    return (loss, gr["grad_x"][None], *leaves(g_sh, g_sm), *leaves(d_sh, d_sm), *leaves(m_sh, m_sm), *leaves(v_sh, v_sm))
```

```python
import functools

import jax
import jax.numpy as jnp
from jax import lax
from jax.experimental import pallas as pl
from jax.experimental.pallas import tpu as pltpu

F32 = jnp.float32
BF16 = jnp.bfloat16
NEG_INF = float("-inf")

N_DEV = 8
LANES = 128
D_MODEL = 1024
EPS = 1e-6
ROPE_THETA = 10000.0
MLA_HEADS = 8
MLA_Q_RANK = 256
MLA_KV_RANK = 128
MLA_NOPE = 64
MLA_ROPE = 32
MLA_V = 64
SWA_HEADS = 8
SWA_KV_HEADS = 2
SWA_DIM = 64
WINDOW = 128
FOX_HEADS = 16
FOX_DIM = 64

ADAM_LR = 0.001
ADAM_B1 = 0.9
ADAM_B2 = 0.999
ADAM_EPS = 1e-08
ADAM_WD = 0.01
ADAM_STEP = 10

ATT_T = 512
ATT_T_FWD = 1024
VMEM_LIMIT = 56 * 1024 * 1024
MATMUL_B_BLOCK_BYTES = 8 * 1024 * 1024

Z0A_UNITS = 12
Z0B_UNITS = 6

WIDE = 1024
N_E_IN = 276
N_O_IN = 514
RA0 = 288
RB0 = 32 + 16
RA1 = 528
RB1 = 128 + 128 + 16
SMALL_ROWS = 24


def _tile(n, cands):
    for c in cands:
        if n % c == 0:
            return c
    raise ValueError(f"no tile for {n}")


ROW_TILES = (512, 256, 128)


def _params(sem, vmem=VMEM_LIMIT):
    return pltpu.CompilerParams(dimension_semantics=sem, vmem_limit_bytes=vmem)


def _matmul(a, b, *, name, ta=False, tb=False, out_dtype=F32, b_rows=None):
    if ta:
        kdim, m = a.shape[-2], a.shape[-1] * (a.shape[0] if a.ndim == 3 else 1)
    else:
        m, kdim = a.shape
    if tb:
        n, kb = b.shape
    else:
        kb, n = b.shape
    assert kdim == kb, (a.shape, b.shape)
    b_start = 0
    if b_rows is not None:
        assert tb
        b_start, n = b_rows
    tm = _tile(m, (512, 256, 128))
    tn = _tile(n, [c for c in (1024, 768, 512, 384, 256, 128)
                   if c * kdim * b.dtype.itemsize <= MATMUL_B_BLOCK_BYTES and b_start % c == 0])
    assert b_start % tn == 0, (b_start, tn)
    b_off = b_start // tn
    dims = (((0 if ta else 1,), (1 if tb else 0,)), ((), ()))

    def body(a_ref, b_ref, o_ref):
        r = lax.dot_general(a_ref[...].astype(BF16), b_ref[...].astype(BF16), dims, preferred_element_type=F32)
        o_ref[...] = r.astype(out_dtype)

    if a.ndim == 3:
        per = a.shape[2] // tm
        a_spec = pl.BlockSpec((None, kdim, tm), lambda i, j: (i // per, 0, i % per))
    else:
        a_spec = pl.BlockSpec((kdim, tm), lambda i, j: (0, i)) if ta else pl.BlockSpec((tm, kdim), lambda i, j: (i, 0))
    b_spec = pl.BlockSpec((tn, kdim), lambda i, j: (j + b_off, 0)) if tb else pl.BlockSpec((kdim, tn), lambda i, j: (0, j))
    return pl.pallas_call(
        body, name=name, grid=(m // tm, n // tn), in_specs=[a_spec, b_spec],
        out_specs=pl.BlockSpec((tm, tn), lambda i, j: (i, j)), out_shape=jax.ShapeDtypeStruct((m, n), out_dtype),
        compiler_params=_params(("parallel", "parallel")),
    )(a, b)


def _rmsnorm_fwd(x, g, *, width, col_blk, name, after=()):
    s = x.shape[0]
    tm = _tile(s, ROW_TILES)

    def body(x_ref, g_ref, *rest):
        y_ref = rest[-1]
        xf = x_ref[...].astype(F32)
        r = lax.rsqrt(jnp.mean(xf * xf, axis=-1, keepdims=True) + EPS)
        y_ref[...] = ((xf * r) * g_ref[...]).astype(BF16)

    return pl.pallas_call(
        body, name=name, grid=(s // tm,),
        in_specs=[pl.BlockSpec((tm, width), lambda i: (i, col_blk)), pl.BlockSpec((1, width), lambda i: (0, 0))]
        + [ANY] * len(after),
        out_specs=pl.BlockSpec((tm, width), lambda i: (i, 0)),
        out_shape=jax.ShapeDtypeStruct((s, width), BF16),
        compiler_params=_params(("parallel",)),
    )(x, g, *after)


def _rmsnorm_bwd(x, g, dy, *, width, col_blk, name):
    s = x.shape[0]
    tm = _tile(s, ROW_TILES)

    def body(x_ref, g_ref, dy_ref, dx_ref, dg_ref):
        @pl.when(pl.program_id(0) == 0)
        def _():
            dg_ref[...] = jnp.zeros_like(dg_ref)

        dx, dg = _rms_bwd_epilogue(dy_ref[...], x_ref[...], 0.0, g_ref[...])
        dg_ref[...] += dg
        dx_ref[...] = dx.astype(BF16)

    return pl.pallas_call(
        body, name=name, grid=(s // tm,),
        in_specs=[pl.BlockSpec((tm, width), lambda i: (i, col_blk)), pl.BlockSpec((1, width), lambda i: (0, 0)),
                  pl.BlockSpec((tm, width), lambda i: (i, 0))],
        out_specs=[pl.BlockSpec((tm, width), lambda i: (i, 0)), pl.BlockSpec((1, width), lambda i: (0, 0))],
        out_shape=[jax.ShapeDtypeStruct((s, width), BF16), jax.ShapeDtypeStruct((1, width), F32)],
        compiler_params=_params(("arbitrary",)),
    )(x, g, dy)


def _sigmoid(x):
    return 1.0 / (1.0 + jnp.exp(-x))


def _matmul_rows(terms, row_inputs, params, epilogue, outs, *, name, prologue=None, separate=False):
    s = row_inputs[0][0].shape[0] if row_inputs else terms[0][0].shape[-2]
    tm = _tile(s, ROW_TILES)
    steps = s // tm
    n_r, n_p, n_o = len(row_inputs), len(params), len(outs)
    n_t = sum(1 if term[0] is None else 2 for term in terms)

    def body(*refs):
        t_refs, r_refs = list(refs[:n_t]), refs[n_t:n_t + n_r]
        p_refs, o_refs = refs[n_t + n_r:n_t + n_r + n_p], refs[n_t + n_r + n_p:]
        i = pl.program_id(0)
        rows, small = [r[...] for r in r_refs], [p[...] for p in p_refs]
        made = None if prologue is None else prologue(*rows, *small)
        parts = []
        for term in terms:
            a = made if term[0] is None else t_refs.pop(0)[...].astype(BF16)
            dims = (((1,), (1 if term[2] else 0,)), ((), ()))
            parts.append(lax.dot_general(a, t_refs.pop(0)[...].astype(BF16), dims, preferred_element_type=F32))
        acc = parts if separate else sum(parts[1:], parts[0])
        vals = epilogue(acc, *rows, *small) if prologue is None else epilogue(acc, *rows, *small, made)
        for ref, val, out in zip(o_refs, vals, outs):
            if out[0] == "rows":
                ref[...] = val.astype(ref.dtype)
            else:
                @pl.when(i == 0)
                def _(ref=ref):
                    ref[...] = jnp.zeros_like(ref)

                ref[...] += val

    in_specs, args = [], []
    for term in terms:
        a, b = term[0], term[1]
        if a is None:
            in_specs.append(_resident(b.shape, lambda i: (0, 0)))
            args.append(b)
            continue
        b_rows = b.shape[0] if term[2] or len(term) < 4 else a.shape[-1]
        b_blk = 0 if len(term) < 4 else term[3] // b_rows
        if len(term) == 5:
            a_spec = pl.BlockSpec((None, tm, a.shape[2]), lambda i, c=term[4]: (c, i, 0))
        else:
            a_spec = pl.BlockSpec((tm, a.shape[1]), lambda i: (i, 0))
        in_specs += [a_spec, _resident((b_rows, b.shape[1]), lambda i, b_blk=b_blk: (b_blk, 0))]
        args += [a, b]
    for arr, width, col_blk in row_inputs:
        in_specs.append(pl.BlockSpec((tm, width), lambda i, col_blk=col_blk: (i, col_blk)))
        args.append(arr)
    for p in params:
        in_specs.append(pl.BlockSpec(p.shape, lambda i: (0, 0)))
        args.append(p)
    out_specs, out_shape = [], []
    for out in outs:
        if out[0] == "rows":
            out_specs.append(pl.BlockSpec((tm, out[1]), lambda i: (i, 0)))
            out_shape.append(jax.ShapeDtypeStruct((s, out[1]), out[2]))
        else:
            out_specs.append(pl.BlockSpec(out[1], lambda i: (0, 0)))
            out_shape.append(jax.ShapeDtypeStruct(out[1], F32))
    return pl.pallas_call(
        body, name=name, grid=(steps,), in_specs=in_specs, out_specs=out_specs, out_shape=out_shape,
        compiler_params=_params(("arbitrary",)),
    )(*args)


def _rms_stats(x):
    r = lax.rsqrt(jnp.mean(x * x, axis=-1, keepdims=True) + EPS)
    return r, x * r


def _gated(o_parts, gate):
    o = o_parts[0] if len(o_parts) == 1 else jnp.concatenate(o_parts, axis=1)
    return (o * (gate * _sigmoid(gate))).astype(BF16)


def _and_first(vals, *more):
    return (*vals, *more, vals[0])


def _residual_norm_epilogue(r, x, g):
    x1 = x + r
    _, xh = _rms_stats(x1)
    return x1, xh * g


def _rms_bwd_epilogue(dy, x, add, g):
    r, xh = _rms_stats(x)
    dxh = dy * g
    dx = r * (dxh - xh * jnp.mean(dxh * xh, axis=-1, keepdims=True)) + add
    return dx, jnp.sum(dy * xh, axis=0, keepdims=True)


def _loss_epilogue(r, x1, target, g):
    rs, xh = _rms_stats(x1 + r)
    err = xh * g - target
    loss = jnp.broadcast_to(0.5 * jnp.sum(jnp.mean(err * err, axis=-1, keepdims=True)), (8, LANES))
    dy = err * (1.0 / D_MODEL)
    dxh = dy * g
    dx = rs * (dxh - xh * jnp.mean(dxh * xh, axis=-1, keepdims=True))
    return dx, loss, jnp.sum(dy * xh, axis=0, keepdims=True)


def _gate_bwd_epilogue(widths):
    def epilogue(d, *rows):
        o_parts, gt = rows[:-1], rows[-1]
        o = o_parts[0] if len(o_parts) == 1 else jnp.concatenate(o_parts, axis=1)
        sg = _sigmoid(gt)
        do = d * (gt * sg)
        d_gate = d * o * (sg * (1.0 + gt * (1.0 - sg)))
        cuts = [sum(widths[:k]) for k in range(len(widths) + 1)]
        return tuple(do[:, cuts[k]:cuts[k + 1]] for k in range(len(widths))) + (d_gate,)

    return epilogue


def _rot_half(x):
    lane = lax.broadcasted_iota(jnp.int32, x.shape, 1)
    return jnp.where(lane < 80, pltpu.roll(x, LANES - 16, axis=1), pltpu.roll(x, 16, axis=1))


def _rot_half_t(g):
    lane = lax.broadcasted_iota(jnp.int32, g.shape, 1)
    lo = (lane >= MLA_NOPE) & (lane < MLA_NOPE + MLA_ROPE // 2)
    hi = (lane >= MLA_NOPE + MLA_ROPE // 2) & (lane < MLA_NOPE + MLA_ROPE)
    return jnp.where(lo, pltpu.roll(g, LANES - 16, axis=1), jnp.where(hi, pltpu.roll(g, 16, axis=1), 0.0))


def _rope_fwd(qp, kvp, z0a, cos_t, sin_t, *, name):
    s = qp.shape[0]
    tm = _tile(s, ROW_TILES)
    hw = MLA_HEADS * LANES

    def body(q_ref, k_ref, kpe_ref, c_ref, s_ref, qm_ref, km_ref):
        c = c_ref[...]
        sn = s_ref[...]
        kpe = kpe_ref[...]
        kpe_r = (kpe * c + _rot_half(kpe) * sn).astype(BF16)
        lane = lax.broadcasted_iota(jnp.int32, kpe.shape, 1)
        for h in range(MLA_HEADS):
            sl = slice(h * LANES, (h + 1) * LANES)
            qh = q_ref[:, sl]
            qm_ref[:, sl] = (qh * c + _rot_half(qh) * sn).astype(BF16)
            km_ref[:, sl] = jnp.where(lane < MLA_NOPE, k_ref[:, sl], kpe_r)

    return pl.pallas_call(
        body, name=name, grid=(s // tm,),
        in_specs=[pl.BlockSpec((tm, hw), lambda i: (i, 0)), pl.BlockSpec((tm, hw), lambda i: (i, 0)),
                  pl.BlockSpec((tm, LANES), lambda i: (i, 11)),
                  pl.BlockSpec((tm, LANES), lambda i: (i, 0)), pl.BlockSpec((tm, LANES), lambda i: (i, 0))],
        out_specs=[pl.BlockSpec((tm, hw), lambda i: (i, 0)), pl.BlockSpec((tm, hw), lambda i: (i, 0))],
        out_shape=[jax.ShapeDtypeStruct((s, hw), BF16), jax.ShapeDtypeStruct((s, hw), BF16)],
        compiler_params=_params(("parallel",)),
    )(qp, kvp, z0a, cos_t, sin_t)


def _rope_bwd(dqm, dkm, dvm, cos_t, sin_t, *, name):
    s = dqm.shape[0]
    tm = _tile(s, ROW_TILES)
    hw = MLA_HEADS * LANES
    vw = MLA_HEADS * MLA_V

    def body(dq_ref, dk_ref, dv_ref, c_ref, s_ref, dqp_ref, dkv_ref, dkpe_ref):
        c = c_ref[...]
        sn = s_ref[...]
        ksum = jnp.zeros((tm, LANES), F32)
        for h in range(MLA_HEADS):
            sl = slice(h * LANES, (h + 1) * LANES)
            dq = dq_ref[:, sl]
            dqp_ref[:, sl] = (dq * c + _rot_half_t(dq * sn)).astype(BF16)
            dk = dk_ref[:, sl]
            dkv_ref[:, sl] = dk.astype(BF16)
            ksum = ksum + dk
        dkv_ref[:, hw:] = dv_ref[...]
        lane = lax.broadcasted_iota(jnp.int32, ksum.shape, 1)
        dkpe = ksum * c + _rot_half_t(ksum * sn)
        dkpe_ref[...] = jnp.where((lane >= MLA_NOPE) & (lane < MLA_NOPE + MLA_ROPE), dkpe, 0.0).astype(BF16)

    return pl.pallas_call(
        body, name=name, grid=(s // tm,),
        in_specs=[pl.BlockSpec((tm, hw), lambda i: (i, 0)), pl.BlockSpec((tm, hw), lambda i: (i, 0)),
                  pl.BlockSpec((tm, vw), lambda i: (i, 0)),
                  pl.BlockSpec((tm, LANES), lambda i: (i, 0)), pl.BlockSpec((tm, LANES), lambda i: (i, 0))],
        out_specs=[pl.BlockSpec((tm, hw), lambda i: (i, 0)), pl.BlockSpec((tm, hw + vw), lambda i: (i, 0)),
                   pl.BlockSpec((tm, LANES), lambda i: (i, 0))],
        out_shape=[jax.ShapeDtypeStruct((s, hw), BF16), jax.ShapeDtypeStruct((s, hw + vw), BF16),
                   jax.ShapeDtypeStruct((s, LANES), BF16)],
        compiler_params=_params(("parallel",)),
    )(dqm, dkm, dvm, cos_t, sin_t)


def _head_mask(shape, a):
    lane = lax.broadcasted_iota(jnp.int32, shape, 1)
    return (lane >= 64 * a) & (lane < 64 * (a + 1))


_NT = (((1,), (1,)), ((), ()))
LOG2E = 1.4426950408889634


def _stack_heads(tile, hw):
    lane = lax.broadcasted_iota(jnp.int32, tile.shape, 1)
    z = jnp.zeros_like(tile)
    return jnp.concatenate([jnp.where(lane < hw, tile, z), jnp.where(lane >= hw, tile, z)], axis=0)


def _stacked_rows(r0, r1, t):
    n = r0.shape[-1]
    return jnp.concatenate([jnp.broadcast_to(r0, (t, n)), jnp.broadcast_to(r1, (t, n))], axis=0)


def _resident(block, index_map):
    return pl.BlockSpec(block, index_map, pipeline_mode=pl.Buffered(1))


def _fwd_tile(s):
    return ATT_T_FWD if s % ATT_T_FWD == 0 else min(ATT_T, s)


def _flash_fwd(q, k, v, bias, *, n_pairs, hw, q_off, k_off, v_off, scale, name, rider=None):
    s = q.shape[0]
    t = _fwd_tile(s)
    nb = s // t
    qw = 2 * hw
    has_bias = bias is not None
    c1 = scale * LOG2E

    def body(*refs):
        refs, ride_refs = _split_rider(refs, rider, n_in=4 if has_bias else 3, n_out=2)
        if has_bias:
            q_ref, k_ref, v_ref, b_ref, o_ref, lse_ref, vt_ref, bcol_ref = refs
        else:
            q_ref, k_ref, v_ref, o_ref, lse_ref, vt_ref = refs
            b_ref = bcol_ref = None
        _ride_start(rider, ride_refs, pl.program_id(0) == 0)
        row = lax.broadcasted_iota(jnp.int32, (t, t), 0)
        col = lax.broadcasted_iota(jnp.int32, (t, t), 1)
        cmask_t = jnp.concatenate([row <= col, row <= col], axis=1)
        lane_lt64 = lax.broadcasted_iota(jnp.int32, (t, LANES), 1) < 64

        def as_column(r):
            return jnp.broadcast_to(r, (8, r.shape[1])).T[:, 0:1]

        def v_block(j, _):
            c0 = pl.multiple_of(j * t, t)
            vt_ref[j] = v_ref[pl.ds(c0, t), :].astype(F32).T.astype(BF16)
            if has_bias:
                for a in range(2):
                    bcol_ref[a, pl.ds(c0, t), :] = as_column(b_ref[0, a, j])
            return 0

        lax.fori_loop(0, nb, v_block, 0)

        def stacked_queries(i):
            return _stack_heads(q_ref[pl.ds(pl.multiple_of(i * t, t), t), :], hw).astype(F32).T.astype(BF16)

        def kv_step(j, carry, qs_t, masked):
            m, l, acc = carry
            rows = pl.ds(pl.multiple_of(j * t, t), t)
            sc = jnp.dot(k_ref[rows, :], qs_t, preferred_element_type=F32) * c1
            if has_bias:
                sc = sc + jnp.concatenate([jnp.broadcast_to(bcol_ref[0, rows, :], (t, t)),
                                           jnp.broadcast_to(bcol_ref[1, rows, :], (t, t))], axis=1)
            if masked:
                sc = jnp.where(cmask_t, sc, NEG_INF)
            m_new = jnp.maximum(m, jnp.max(sc, axis=0, keepdims=True))
            alpha = jnp.exp2(m - m_new)
            p = jnp.exp2(sc - m_new)
            l_new = alpha * l + jnp.sum(p, axis=0, keepdims=True)
            pv = jnp.dot(vt_ref[j], p.astype(BF16), preferred_element_type=F32)
            return m_new, l_new, alpha * acc + pv

        def finish(i, carry):
            m, l, acc = carry
            r0 = pl.multiple_of(i * t, t)
            out = (acc / l).T
            lse2 = as_column(m + jnp.log2(l))
            lse_ref[0, 0, pl.ds(r0, t), :] = lse2[:t]
            lse_ref[0, 1, pl.ds(r0, t), :] = lse2[t:]
            o_ref[pl.ds(r0, t), :] = jnp.where(lane_lt64, out[:t], out[t:])

        init = (jnp.full((1, 2 * t), NEG_INF, F32), jnp.zeros((1, 2 * t), F32), jnp.zeros((LANES, 2 * t), F32))

        def q_block(i, _):
            qs_t = stacked_queries(i)
            carry = lax.fori_loop(0, i, lambda j, c: kv_step(j, c, qs_t, False), init)
            finish(i, kv_step(i, carry, qs_t, True))
            return 0

        lax.fori_loop(0, nb, q_block, 0)
        _ride_wait(rider, ride_refs, pl.program_id(0) == n_pairs - 1)

    in_specs = [_resident((s, qw), lambda p: (0, q_off + p)), _resident((s, qw), lambda p: (0, k_off + p)),
                _resident((s, LANES), lambda p: (0, v_off + p))]
    args = [q, k, v]
    if has_bias:
        in_specs.append(_resident((1, 2, nb, 1, t), lambda p: (p, 0, 0, 0, 0)))
        args.append(bias)
    out_specs = [pl.BlockSpec((s, LANES), lambda p: (0, p)), pl.BlockSpec((1, 2, s, 1), lambda p: (p, 0, 0, 0))]
    out_shape = [jax.ShapeDtypeStruct((s, n_pairs * LANES), F32), jax.ShapeDtypeStruct((n_pairs, 2, s, 1), F32)]
    scratch = [pltpu.VMEM((nb, LANES, t), BF16)] + ([pltpu.VMEM((2, s, 1), F32)] if has_bias else [])
    scratch += _add_rider(rider, in_specs, args, out_specs, out_shape)
    return pl.pallas_call(
        body, name=name, grid=(n_pairs,), in_specs=in_specs, out_specs=out_specs, out_shape=out_shape,
        scratch_shapes=scratch,
        compiler_params=_params(("parallel",) if rider is None else ("arbitrary",)),
    )(*args)


def _flash_bwd(q, k, v, do, o, lse, bias, *, n_pairs, hw, q_off, k_off, v_off, scale, qk_dtype, name, rider=None,
               stacked=False):
    s = q.shape[0]
    t = min(ATT_T, s)
    nb = s // t
    qw = 2 * hw
    has_bias = bias is not None
    c1 = scale * LOG2E

    def body(*refs):
        n_grads = 1 if stacked else 3
        refs, ride_refs = _split_rider(refs, rider, n_in=7 if has_bias else 6, n_out=n_grads + (2 if has_bias else 0))
        if stacked:
            refs = list(refs)
            n_in = 7 if has_bias else 6
            refs[n_in:n_in + 1] = [refs[n_in].at[0], refs[n_in].at[1], refs[n_in].at[2]]
        if has_bias:
            (q_ref, k_ref, v_ref, do_ref, o_ref, lse_ref, b_ref, dq_ref, dk_ref, dv_ref, db_ref, dr_ref,
             dkt_ref, dvt_ref) = refs
            db_ref[...] = jnp.zeros_like(db_ref)
        else:
            q_ref, k_ref, v_ref, do_ref, o_ref, lse_ref, dq_ref, dk_ref, dv_ref, dkt_ref, dvt_ref = refs
            b_ref = db_ref = dr_ref = None
        _ride_start(rider, ride_refs, pl.program_id(0) == 0)
        dkt_ref[...] = jnp.zeros_like(dkt_ref)
        dvt_ref[...] = jnp.zeros_like(dvt_ref)
        causal = lax.broadcasted_iota(jnp.int32, (t, t), 1) <= lax.broadcasted_iota(jnp.int32, (t, t), 0)
        cmask = jnp.concatenate([causal, causal], axis=0)
        lane_lt_hw = lax.broadcasted_iota(jnp.int32, (t, qw), 1) < hw

        def q_block(i, _):
            r0 = pl.multiple_of(i * t, t)
            qs = _stack_heads(q_ref[pl.ds(r0, t), :], hw)
            dos = _stack_heads(do_ref[pl.ds(r0, t), :], 64)
            ot = o_ref[pl.ds(r0, t), :]
            delta = jnp.sum(dos * jnp.concatenate([ot, ot], axis=0), axis=-1, keepdims=True)
            lse2 = jnp.concatenate([lse_ref[0, 0, pl.ds(r0, t), :], lse_ref[0, 1, pl.ds(r0, t), :]], axis=0)
            dosb = dos.astype(BF16)
            dos_t = dos.T.astype(BF16)
            qs_t = qs.astype(F32).T.astype(BF16)

            def kv_step(j, carry, masked):
                dq, rsum = carry
                c0 = pl.multiple_of(j * t, t)
                kt = k_ref[pl.ds(c0, t), :]
                vt = v_ref[pl.ds(c0, t), :]
                sc = lax.dot_general(qs, kt, _NT, preferred_element_type=F32) * c1
                if has_bias:
                    sc = sc + _stacked_rows(b_ref[0, 0, j], b_ref[0, 1, j], t)
                if masked:
                    sc = jnp.where(cmask, sc, NEG_INF)
                p = jnp.exp2(sc - lse2)
                dp = lax.dot_general(dosb, vt, _NT, preferred_element_type=F32)
                ds = p * (dp - delta)
                dsb = ds.astype(BF16)
                pb = p.astype(BF16)
                if hw == LANES:
                    dvt_ref[j] += jnp.concatenate(
                        [jnp.dot(dos_t[:64, :t], pb[:t], preferred_element_type=F32),
                         jnp.dot(dos_t[64:, t:], pb[t:], preferred_element_type=F32)], axis=0)
                    dkt_ref[j] += jnp.concatenate(
                        [jnp.dot(qs_t[:hw, :t], dsb[:t], preferred_element_type=F32),
                         jnp.dot(qs_t[hw:, t:], dsb[t:], preferred_element_type=F32)], axis=0)
                else:
                    dvt_ref[j] += jnp.dot(dos_t, pb, preferred_element_type=F32)
                    dkt_ref[j] += jnp.dot(qs_t, dsb, preferred_element_type=F32)
                if has_bias:
                    db_ref[0, 0, j] += jnp.sum(ds[:t], axis=0, keepdims=True)
                    db_ref[0, 1, j] += jnp.sum(ds[t:], axis=0, keepdims=True)
                    rsum = rsum + jnp.sum(ds, axis=-1, keepdims=True)
                return dq + jnp.dot(dsb, kt, preferred_element_type=F32), rsum

            init = (jnp.zeros((2 * t, qw), F32), jnp.zeros((2 * t, 1), F32))
            carry = lax.fori_loop(0, i, functools.partial(kv_step, masked=False), init)
            dq, rsum = kv_step(i, carry, True)
            dq = dq * scale
            dq_ref[pl.ds(r0, t), :] = jnp.where(lane_lt_hw, dq[:t], dq[t:]).astype(qk_dtype)
            if has_bias:
                rsum_row = jnp.broadcast_to(rsum, (2 * t, LANES)).T[0:1]
                dr_ref[0, 0, i] = rsum_row[:, :t]
                dr_ref[0, 1, i] = rsum_row[:, t:]
            return 0

        lax.fori_loop(0, nb, q_block, 0)

        def k_block(j, _):
            c0 = pl.multiple_of(j * t, t)
            dk_ref[pl.ds(c0, t), :] = (dkt_ref[j].T * scale).astype(qk_dtype)
            dv_ref[pl.ds(c0, t), :] = dvt_ref[j].T.astype(BF16)
            return 0

        lax.fori_loop(0, nb, k_block, 0)
        _ride_wait(rider, ride_refs, pl.program_id(0) == n_pairs - 1)

    in_specs = [_resident((s, qw), lambda p: (0, q_off + p)), _resident((s, qw), lambda p: (0, k_off + p)),
                _resident((s, LANES), lambda p: (0, v_off + p)),
                _resident((s, LANES), lambda p: (0, p)), _resident((s, LANES), lambda p: (0, p)),
                _resident((1, 2, s, 1), lambda p: (p, 0, 0, 0))]
    args = [q, k, v, do, o, lse]
    if stacked:
        assert qw == LANES and qk_dtype == BF16
        out_specs = [pl.BlockSpec((3, s, LANES), lambda p: (0, 0, p))]
        out_shape = [jax.ShapeDtypeStruct((3, s, n_pairs * LANES), BF16)]
    else:
        out_specs = [pl.BlockSpec((s, qw), lambda p: (0, p)), pl.BlockSpec((s, qw), lambda p: (0, p)),
                     pl.BlockSpec((s, LANES), lambda p: (0, p))]
        out_shape = [jax.ShapeDtypeStruct((s, n_pairs * qw), qk_dtype), jax.ShapeDtypeStruct((s, n_pairs * qw), qk_dtype),
                     jax.ShapeDtypeStruct((s, n_pairs * LANES), BF16)]
    if has_bias:
        in_specs.append(_resident((1, 2, nb, 1, t), lambda p: (p, 0, 0, 0, 0)))
        args.append(bias)
        for _ in range(2):
            out_specs.append(pl.BlockSpec((1, 2, nb, 1, t), lambda p: (p, 0, 0, 0, 0)))
            out_shape.append(jax.ShapeDtypeStruct((n_pairs, 2, nb, 1, t), F32))
    scratch = [pltpu.VMEM((nb, qw, t), F32), pltpu.VMEM((nb, LANES, t), F32)]
    scratch += _add_rider(rider, in_specs, args, out_specs, out_shape)
    return pl.pallas_call(
        body, name=name, grid=(n_pairs,), in_specs=in_specs, out_specs=out_specs, out_shape=out_shape,
        scratch_shapes=scratch,
        compiler_params=_params(("parallel",) if rider is None else ("arbitrary",)),
    )(*args)


def _alibi_slope(h):
    return 2.0 ** (-8.0 * (h + 1.0) / SWA_HEADS)


SWA_ROWS = 512
SWA_SCALE = SWA_DIM ** -0.5


def _swa_geometry(i):
    w = WINDOW
    r0 = pl.multiple_of(i * w, w)
    b0 = pl.multiple_of(jnp.maximum(i - 1, 0) * w, w)
    row = lax.broadcasted_iota(jnp.int32, (w, 2 * w), 0)
    col = lax.broadcasted_iota(jnp.int32, (w, 2 * w), 1)
    dist = row - col + (r0 - b0)
    valid = (dist >= 0) & (dist < w)
    return r0, b0, dist.astype(F32), valid


def _swa_q_head(qblk, h):
    kv = h // (SWA_HEADS // SWA_KV_HEADS)
    if h % 2 != kv:
        qblk = pltpu.roll(qblk, 64, axis=1)
    return jnp.where(_head_mask(qblk.shape, kv), qblk, 0.0)


SWA_GROUP = SWA_HEADS // SWA_KV_HEADS


def _swa_stack(ref, rs, grp):
    parts = []
    for a in range(SWA_GROUP):
        h = SWA_GROUP * grp + a
        parts.append(_swa_q_head(ref[rs, (h // 2) * LANES:(h // 2 + 1) * LANES].astype(F32), h))
    return jnp.concatenate(parts, axis=0)


def _swa_unstack(x, grp):
    tiles = []
    for a in range(SWA_GROUP):
        h = SWA_GROUP * grp + a
        tile = x[a * WINDOW:(a + 1) * WINDOW]
        tiles.append(pltpu.roll(tile, 64, axis=1) if h % 2 != grp else tile)
    return tiles


def _swa_head_column(vals):
    return jnp.concatenate([jnp.full((WINDOW, 1), v, F32) for v in vals], axis=0)


def _swa_logits(qs, kb, dist, valid, grp):
    slopes = _swa_head_column([_alibi_slope(SWA_GROUP * grp + a) for a in range(SWA_GROUP)])
    dist4 = jnp.concatenate([dist] * SWA_GROUP, axis=0)
    valid4 = jnp.concatenate([valid] * SWA_GROUP, axis=0)
    sc = lax.dot_general(qs, kb, _NT, preferred_element_type=F32) * SWA_SCALE - slopes * dist4
    return jnp.where(valid4, sc, NEG_INF)


def _swa_merge_heads(tiles):
    lt64 = lax.broadcasted_iota(jnp.int32, (WINDOW, LANES), 1) < 64
    return jnp.concatenate([jnp.where(lt64, tiles[2 * b], tiles[2 * b + 1]) for b in range(SWA_HEADS // 2)], axis=1)


def _swa_fwd(z0b, sinks, *, name):
    s = z0b.shape[0]
    w = WINDOW
    rows = min(SWA_ROWS, s)
    per_step = rows // w
    qcols = SWA_HEADS * SWA_DIM

    def body(sink_ref, q_ref, k_ref, v_ref, o_ref, lse_ref):
        g = pl.program_id(0)
        for ii in range(per_step):
            rs = slice(ii * w, (ii + 1) * w)
            r0, b0, dist, valid = _swa_geometry(g * per_step + ii)
            kb = k_ref[pl.ds(b0, 2 * w), :]
            vb = v_ref[pl.ds(b0, 2 * w), :]
            o_tiles = []
            for h in range(SWA_HEADS):
                kv = h // SWA_GROUP
                qh = _swa_q_head(q_ref[rs, (h // 2) * LANES:(h // 2 + 1) * LANES].astype(F32), h).astype(BF16)
                sc = lax.dot_general(qh, kb, _NT, preferred_element_type=F32) * SWA_SCALE - _alibi_slope(h) * dist
                sc = jnp.where(valid, sc, NEG_INF)
                sink = sink_ref[0, h]
                m = jnp.maximum(jnp.max(sc, axis=-1, keepdims=True), sink)
                p = jnp.exp(sc - m)
                l = jnp.sum(p, axis=-1, keepdims=True) + jnp.exp(sink - m)
                oh = jnp.dot(p.astype(BF16), vb, preferred_element_type=F32) / l
                o_tiles.append(pltpu.roll(oh, 64, axis=1) if h % 2 != kv else oh)
                lse_ref[h, rs, :] = m + jnp.log(l)
            o_ref[rs, :] = _swa_merge_heads(o_tiles)

    return pl.pallas_call(
        body, name=name, grid=(s // rows,),
        in_specs=[pl.BlockSpec(memory_space=pltpu.SMEM),
                  pl.BlockSpec((rows, qcols), lambda g: (g, 0)),
                  pl.BlockSpec((s, LANES), lambda g: (0, 4)), pl.BlockSpec((s, LANES), lambda g: (0, 5))],
        out_specs=[pl.BlockSpec((rows, qcols), lambda g: (g, 0)), pl.BlockSpec((SWA_HEADS, rows, 1), lambda g: (0, g, 0))],
        out_shape=[jax.ShapeDtypeStruct((s, qcols), F32), jax.ShapeDtypeStruct((SWA_HEADS, s, 1), F32)],
        compiler_params=_params(("parallel",)),
    )(sinks, z0b, z0b, z0b)


def _swa_bwd(z0b, sinks, do, o, lse, *, name):
    s = z0b.shape[0]
    w = WINDOW
    rows = min(SWA_ROWS, s)
    per_step = rows // w
    qcols = SWA_HEADS * SWA_DIM
    nblk = s // w

    def body(sink_ref, q_ref, k_ref, v_ref, do_ref, o_ref, lse_ref, dq_ref, dkt_ref, dvt_ref, dsink_ref):
        g = pl.program_id(0)

        @pl.when(g == 0)
        def _():
            dkt_ref[...] = jnp.zeros_like(dkt_ref)
            dvt_ref[...] = jnp.zeros_like(dvt_ref)
            dsink_ref[...] = jnp.zeros_like(dsink_ref)

        for ii in range(per_step):
            i = g * per_step + ii
            rs = slice(ii * w, (ii + 1) * w)
            r0, b0, dist, valid = _swa_geometry(i)
            j0 = jnp.maximum(i - 1, 0)
            kb = k_ref[pl.ds(b0, 2 * w), :]
            vb = v_ref[pl.ds(b0, 2 * w), :]
            dq_tiles = []
            for grp in range(SWA_KV_HEADS):
                heads = [SWA_GROUP * grp + a for a in range(SWA_GROUP)]
                qs32 = _swa_stack(q_ref, rs, grp)
                dos32 = _swa_stack(do_ref, rs, grp)
                delta = jnp.sum(dos32 * _swa_stack(o_ref, rs, grp), axis=-1, keepdims=True)
                lse = jnp.concatenate([lse_ref[h, rs, :] for h in heads], axis=0)
                sink = _swa_head_column([sink_ref[0, h] for h in heads])
                p = jnp.exp(_swa_logits(qs32.astype(BF16), kb, dist, valid, grp) - lse)
                dp = lax.dot_general(dos32.astype(BF16), vb, _NT, preferred_element_type=F32)
                ds = p * (dp - delta)
                dsb = ds.astype(BF16)
                d_sink = jnp.exp(sink - lse) * delta
                for a, h in enumerate(heads):
                    dsink_ref[h:h + 1, :] += jnp.broadcast_to(-jnp.sum(d_sink[a * w:(a + 1) * w]), (1, LANES))
                dvt = jnp.dot(dos32.T.astype(BF16), p.astype(BF16), preferred_element_type=F32)
                dkt = jnp.dot(qs32.T.astype(BF16), dsb, preferred_element_type=F32) * SWA_SCALE
                dvt_ref[j0] += dvt[:, :w]
                dvt_ref[j0 + 1] += dvt[:, w:]
                dkt_ref[j0] += dkt[:, :w]
                dkt_ref[j0 + 1] += dkt[:, w:]
                dq_tiles += _swa_unstack(jnp.dot(dsb, kb, preferred_element_type=F32) * SWA_SCALE, grp)
            dq_ref[rs, :] = _swa_merge_heads(dq_tiles)

    return pl.pallas_call(
        body, name=name, grid=(s // rows,),
        in_specs=[pl.BlockSpec(memory_space=pltpu.SMEM),
                  pl.BlockSpec((rows, qcols), lambda g: (g, 0)),
                  pl.BlockSpec((s, LANES), lambda g: (0, 4)), pl.BlockSpec((s, LANES), lambda g: (0, 5)),
                  pl.BlockSpec((rows, qcols), lambda g: (g, 0)), pl.BlockSpec((rows, qcols), lambda g: (g, 0)),
                  pl.BlockSpec((SWA_HEADS, rows, 1), lambda g: (0, g, 0))],
        out_specs=[pl.BlockSpec((rows, qcols), lambda g: (g, 0)),
                   pl.BlockSpec((nblk, LANES, w), lambda g: (0, 0, 0)),
                   pl.BlockSpec((nblk, LANES, w), lambda g: (0, 0, 0)),
                   pl.BlockSpec((SWA_HEADS, LANES), lambda g: (0, 0))],
        out_shape=[jax.ShapeDtypeStruct((s, qcols), F32),
                   jax.ShapeDtypeStruct((nblk, LANES, w), F32), jax.ShapeDtypeStruct((nblk, LANES, w), F32),
                   jax.ShapeDtypeStruct((SWA_HEADS, LANES), F32)],
        compiler_params=_params(("arbitrary",)),
    )(sinks, z0b, z0b, z0b, do, o, lse)


CUM_T = 256


def _split3(x):
    hi = x.astype(BF16)
    r1 = x - hi.astype(F32)
    mid = r1.astype(BF16)
    lo = (r1 - mid.astype(F32)).astype(BF16)
    return hi, mid, lo


def _tri_dot(tri, x):
    hi, mid, lo = _split3(x)
    out = jnp.dot(tri, hi, preferred_element_type=F32)
    out = out + jnp.dot(tri, mid, preferred_element_type=F32)
    return out + jnp.dot(tri, lo, preferred_element_type=F32)


def _logf_fwd(zf, bf, *, name):
    s = zf.shape[0]
    t = CUM_T
    nb = s // t

    def body(z_ref, b_ref, c_ref, carry_ref):
        i = pl.program_id(0)

        @pl.when(i == 0)
        def _():
            carry_ref[...] = jnp.zeros_like(carry_ref)

        x = z_ref[...] + b_ref[...]
        lf = jnp.minimum(x, 0.0) - jnp.log(1.0 + jnp.exp(-jnp.abs(x)))
        row = lax.broadcasted_iota(jnp.int32, (t, t), 0)
        col = lax.broadcasted_iota(jnp.int32, (t, t), 1)
        tri = jnp.where(col <= row, 1.0, 0.0).astype(BF16)
        c = _tri_dot(tri, lf) + carry_ref[...]
        c_ref[...] = c
        carry_ref[...] = c[t - 1:t, :]

    return pl.pallas_call(
        body, name=name, grid=(nb,),
        in_specs=[pl.BlockSpec((t, LANES), lambda i: (i, 0)), pl.BlockSpec((1, LANES), lambda i: (0, 0))],
        out_specs=pl.BlockSpec((t, LANES), lambda i: (i, 0)),
        out_shape=jax.ShapeDtypeStruct((s, LANES), F32),
        scratch_shapes=[pltpu.VMEM((1, LANES), F32)],
        compiler_params=_params(("arbitrary",)),
    )(zf, bf)


def _logf_bwd(dc, zf, bf, *, name):
    s = zf.shape[0]
    t = CUM_T
    nb = s // t

    def body(dc_ref, z_ref, b_ref, dz_ref, db_ref, carry_ref):
        i = pl.program_id(0)

        @pl.when(i == 0)
        def _():
            carry_ref[...] = jnp.zeros_like(carry_ref)
            db_ref[...] = jnp.zeros_like(db_ref)

        row = lax.broadcasted_iota(jnp.int32, (t, t), 0)
        col = lax.broadcasted_iota(jnp.int32, (t, t), 1)
        tri = jnp.where(col >= row, 1.0, 0.0).astype(BF16)
        dlf = _tri_dot(tri, dc_ref[...]) + carry_ref[...]
        carry_ref[...] = dlf[0:1, :]
        x = z_ref[...] + b_ref[...]
        dz = dlf * _sigmoid(-x)
        dz_ref[...] = dz.astype(BF16)
        db_ref[...] += jnp.sum(dz, axis=0, keepdims=True)

    return pl.pallas_call(
        body, name=name, grid=(nb,),
        in_specs=[pl.BlockSpec((t, LANES), lambda i: (nb - 1 - i, 0)), pl.BlockSpec((t, LANES), lambda i: (nb - 1 - i, 0)),
                  pl.BlockSpec((1, LANES), lambda i: (0, 0))],
        out_specs=[pl.BlockSpec((t, LANES), lambda i: (nb - 1 - i, 0)), pl.BlockSpec((1, LANES), lambda i: (0, 0))],
        out_shape=[jax.ShapeDtypeStruct((s, LANES), BF16), jax.ShapeDtypeStruct((1, LANES), F32)],
        scratch_shapes=[pltpu.VMEM((1, LANES), F32)],
        compiler_params=_params(("arbitrary",)),
    )(dc, zf, bf)


def _sum_pieces(p_ref):
    g = p_ref[0].astype(F32)
    for k in range(1, N_DEV):
        g = g + p_ref[k].astype(F32)
    return g


def _adam_update(g, w, m, v):
    bc1 = 1.0 - ADAM_B1 ** ADAM_STEP
    bc2 = 1.0 - ADAM_B2 ** ADAM_STEP
    nm = ADAM_B1 * m + (1.0 - ADAM_B1) * g
    nv = ADAM_B2 * v + (1.0 - ADAM_B2) * (g * g)
    m_hat = nm / bc1
    v_hat = nv / bc2
    return -ADAM_LR * (m_hat / (jnp.sqrt(v_hat) + ADAM_EPS) + ADAM_WD * w), nm, nv


def _adamw(pieces, w, m, v, *, name):
    rows, cols = w.shape
    tr = _tile(rows, (RB1, RB0, SMALL_ROWS))

    def body(p_ref, w_ref, m_ref, v_ref, g_ref, d_ref, nm_ref, nv_ref):
        g = _sum_pieces(p_ref)
        g_ref[...] = g
        d_ref[...], nm_ref[...], nv_ref[...] = _adam_update(g, w_ref[...], m_ref[...], v_ref[...])

    spec = pl.BlockSpec((tr, cols), lambda i: (i, 0))
    shape = jax.ShapeDtypeStruct((rows, cols), F32)
    return pl.pallas_call(
        body, name=name, grid=(rows // tr,),
        in_specs=[pl.BlockSpec((N_DEV, tr, cols), lambda i: (0, i, 0)), spec, spec, spec],
        out_specs=[spec, spec, spec, spec], out_shape=[shape, shape, shape, shape],
        compiler_params=_params(("parallel",)),
    )(pieces, w, m, v)


def _sum8(pieces, rows, *, name):
    cols = pieces.shape[2]
    tr = _tile(rows, (176, 96))

    def body(p_ref, g_ref):
        g_ref[...] = _sum_pieces(p_ref)

    return pl.pallas_call(
        body, name=name, grid=(rows // tr,),
        in_specs=[pl.BlockSpec((N_DEV, tr, cols), lambda i: (0, i, 0))],
        out_specs=pl.BlockSpec((tr, cols), lambda i: (i, 0)),
        out_shape=jax.ShapeDtypeStruct((rows, cols), F32),
        compiler_params=_params(("parallel",)),
    )(pieces)


def _adamw_columns(g, w, m, v, *, name):
    n, _, k = w.shape
    tr = n // 2

    def body(g_ref, w_ref, m_ref, v_ref, d_ref, nm_ref, nv_ref):
        d_ref[...], nm_ref[...], nv_ref[...] = _adam_update(g_ref[...], w_ref[...], m_ref[...], v_ref[...])

    spec = pl.BlockSpec((tr, 1, k), lambda i: (i, 0, 0))
    shape = jax.ShapeDtypeStruct((n, 1, k), F32)
    return pl.pallas_call(
        body, name=name, grid=(n // tr,), in_specs=[spec, spec, spec, spec],
        out_specs=[spec, spec, spec], out_shape=[shape, shape, shape],
        compiler_params=_params(("parallel",)),
    )(g, w, m, v)


MESH = pl.DeviceIdType.MESH
ANY = pl.BlockSpec(memory_space=pl.ANY)


def _all_gather(shard, *, name):
    rows, lanes = shard.shape

    def body(x_ref, out_ref, send_sems, recv_sems, local_sem):
        x, y, c = lax.axis_index("x"), lax.axis_index("y"), lax.axis_index("c")
        me, sibling = (x, y, c), (x, y, 1 - c)
        chips = [(1 - x, y), (x, 1 - y), (1 - x, 1 - y)]

        def block(px, py, pc):
            return out_ref.at[4 * px + 2 * py + pc]

        def copy(k, blk, to, src=None):
            return pltpu.make_async_remote_copy(
                src_ref=block(*blk) if src is None else src, dst_ref=block(*blk),
                send_sem=send_sems.at[k], recv_sem=recv_sems.at[k], device_id=to, device_id_type=MESH)

        mine = pltpu.make_async_copy(x_ref, block(*me), local_sem)
        mine.start()
        first = [copy(0, me, sibling, src=x_ref)]
        first += [copy(1 + j, me, (*chip, c), src=x_ref) for j, chip in enumerate(chips)]
        for cp in first:
            cp.start()
        passed = [copy(4 + j, (*chip, c), sibling) for j, chip in enumerate(chips)]
        for j, chip in enumerate(chips):
            copy(1 + j, (*chip, c), me).wait_recv()
            passed[j].start()
        copy(0, sibling, me).wait_recv()
        for j, chip in enumerate(chips):
            copy(4 + j, (*chip, 1 - c), me).wait_recv()
        for cp in first + passed:
            cp.wait_send()
        mine.wait()

    return pl.pallas_call(
        body, name=name, out_shape=jax.ShapeDtypeStruct((N_DEV, rows, lanes), shard.dtype),
        in_specs=[ANY], out_specs=ANY,
        scratch_shapes=[pltpu.SemaphoreType.DMA((7,)), pltpu.SemaphoreType.DMA((7,)), pltpu.SemaphoreType.DMA(())],
    )(shard)


def _pack_rows(arrays, plan, rows, *, name):
    for slot in range(N_DEV):
        at = 0
        for to, n in sorted((to, n) for _, _, n, s2, to in plan if s2 == slot):
            assert to == at, plan
            at += n
        assert at == rows, plan
    assert all(v % 16 == 0 for _, r0, n, _, to in plan for v in (r0, n, to)), plan

    def body(*refs):
        srcs, out_ref, sems = refs[:len(arrays)], refs[len(arrays)], refs[len(arrays) + 1]
        copies = [pltpu.make_async_copy(srcs[a].at[pl.ds(r0, n)], out_ref.at[slot, pl.ds(to, n)], sems.at[k])
                  for k, (a, r0, n, slot, to) in enumerate(plan)]
        for cp in copies:
            cp.start()
        for cp in copies:
            cp.wait()

    return pl.pallas_call(
        body, name=name, out_shape=jax.ShapeDtypeStruct((N_DEV, rows, WIDE), BF16),
        in_specs=[ANY] * len(arrays), out_specs=ANY, scratch_shapes=[pltpu.SemaphoreType.DMA((len(plan),))],
    )(*arrays)


def _peer_copies(kind, src_ref, out_ref, send_sems, recv_sems, local_sem):
    x, y, c = lax.axis_index("x"), lax.axis_index("y"), lax.axis_index("c")
    me = 4 * x + 2 * y + c

    def src(idx):
        return src_ref.at[idx] if kind == "exchange" else src_ref

    mine = None if local_sem is None else pltpu.make_async_copy(src(me), out_ref.at[me], local_sem)
    copies = []
    for r in range(1, N_DEV):
        px = 1 - x if r & 4 else x
        py = 1 - y if r & 2 else y
        pc = 1 - c if r & 1 else c
        copies.append(pltpu.make_async_remote_copy(
            src_ref=src(4 * px + 2 * py + pc), dst_ref=out_ref.at[me],
            send_sem=send_sems.at[r - 1], recv_sem=recv_sems.at[r - 1],
            device_id=(px, py, pc), device_id_type=MESH))
    return mine, copies


PEER_SEMS = [pltpu.SemaphoreType.DMA((7,)), pltpu.SemaphoreType.DMA((7,)), pltpu.SemaphoreType.DMA(())]


HBM = pl.BlockSpec(memory_space=pltpu.HBM)
SEMAPHORES = pl.BlockSpec(memory_space=pltpu.SEMAPHORE)


def _peer_start(kind, arr, *, name):
    land = lax.empty((N_DEV,) + arr.shape[-2:], arr.dtype)

    def body(src_ref, land_ref, send_sems, recv_sems, src_thru, land_thru, token):
        _, copies = _peer_copies(kind, src_ref, land_ref, send_sems, recv_sems, None)
        for cp in copies:
            cp.start()
        token[...] = jnp.zeros_like(token)

    return pl.pallas_call(
        body, name=name,
        out_shape=(pltpu.SemaphoreType.DMA((N_DEV - 1,)), pltpu.SemaphoreType.DMA((N_DEV - 1,)),
                   pltpu.HBM(arr.shape, arr.dtype), pltpu.HBM(land.shape, land.dtype), jax.ShapeDtypeStruct((8, LANES), F32)),
        in_specs=(HBM, HBM), out_specs=(SEMAPHORES, SEMAPHORES, HBM, HBM, pl.BlockSpec(memory_space=pltpu.VMEM)),
        input_output_aliases={0: 2, 1: 3},
        compiler_params=pltpu.CompilerParams(has_side_effects=pltpu.SideEffectType.DATAFLOW_SIDE_EFFECTING),
    )(pltpu.with_memory_space_constraint(arr, pltpu.HBM), pltpu.with_memory_space_constraint(land, pltpu.HBM))


def _peer_wait(kind, send_sems, recv_sems, src_thru, land_thru, after, *, name):
    def body(src_ref, land_ref, send_sems, recv_sems, *_):
        _, copies = _peer_copies(kind, src_ref, land_ref, send_sems, recv_sems, None)
        for cp in copies:
            cp.wait_send()
            cp.wait_recv()

    return pl.pallas_call(
        body, name=name,
        out_shape=(pltpu.HBM(src_thru.shape, src_thru.dtype), pltpu.HBM(land_thru.shape, land_thru.dtype)),
        in_specs=(HBM, HBM, SEMAPHORES, SEMAPHORES) + (ANY,) * len(after), out_specs=(HBM, HBM),
        input_output_aliases={0: 0, 1: 1},
        compiler_params=pltpu.CompilerParams(has_side_effects=pltpu.SideEffectType.DATAFLOW_SIDE_EFFECTING),
    )(src_thru, land_thru, send_sems, recv_sems, *after)


def _add_rider(rider, in_specs, args, out_specs, out_shape):
    if rider is None:
        return []
    _, arr = rider
    in_specs.append(ANY)
    args.append(arr)
    out_specs.append(ANY)
    out_shape.append(jax.ShapeDtypeStruct((N_DEV,) + arr.shape[-2:], arr.dtype))
    return list(PEER_SEMS)


def _split_rider(refs, rider, n_in, n_out):
    if rider is None:
        return refs, None
    refs = list(refs)
    rin = refs.pop(n_in)
    rout = refs.pop(n_in + n_out)
    return refs[:-3], (rin, rout, *refs[-3:])


def _ride_start(rider, ride_refs, first):
    if rider is None:
        return

    @pl.when(first)
    def _():
        mine, copies = _peer_copies(rider[0], *ride_refs)
        mine.start()
        for cp in copies:
            cp.start()


def _ride_wait(rider, ride_refs, last):
    if rider is None:
        return

    @pl.when(last)
    def _():
        mine, copies = _peer_copies(rider[0], *ride_refs)
        for cp in copies:
            cp.wait()
        mine.wait()


def _gathered_cols(blocks, kdim):
    n = blocks.shape[1] * WIDE // kdim
    return blocks.reshape(N_DEV, kdim, n).transpose(1, 0, 2).reshape(kdim, N_DEV * n)


def _scatter_cols(dw):
    kdim, n8 = dw.shape
    n = n8 // N_DEV
    return dw.reshape(kdim, N_DEV, n).transpose(1, 0, 2).reshape(N_DEV, kdim * n // WIDE, WIDE)


def _pad_rows(a, rows):
    pad = [(0, 0)] * a.ndim
    pad[-2] = (0, rows - a.shape[-2])
    return jnp.pad(a, pad)


O_STRIDE = N_O_IN // 16 * 16
assert (N_DEV - 1) * (N_O_IN - O_STRIDE) + N_O_IN <= RA1


def _window_of_shard(rows, me):
    return lax.dynamic_update_slice_in_dim(jnp.zeros((RA1, WIDE), rows.dtype), rows, me * (N_O_IN - O_STRIDE), axis=0)


def _from_windows(g):
    body = g[:, :O_STRIDE].reshape(N_DEV * O_STRIDE, WIDE)
    tail = _pad_rows(g[:, O_STRIDE:], O_STRIDE).reshape(N_DEV * O_STRIDE, WIDE)
    n = N_DEV * N_O_IN
    return _pad_rows(body, n) + jnp.pad(tail, ((O_STRIDE, 0), (0, 0)))[:n]


def _window_plan(sizes, p):
    plan, at = [], 0
    for k, size in enumerate(sizes):
        lo, hi = max(p * O_STRIDE, at), min(p * O_STRIDE + RA1, at + size)
        if lo < hi:
            plan.append((k, lo - at, hi - lo, p, lo - p * O_STRIDE))
        at += size
    return plan


def _layer0_in_weight_t(wt):
    cq, ckv, kpe = wt[0:256], wt[256:384], wt[384:416]
    q_s, k_s, v_s, gate = wt[416:928], wt[928:1056], wt[1056:1184], wt[1184:2208]
    z = jnp.zeros((64, wt.shape[1]), wt.dtype)
    return jnp.concatenate([gate, cq, ckv, z, kpe, z[:32], q_s, k_s, v_s], axis=0)


def _layer0_in_grad_t(dwt):
    gate, cq, ckv, kpe = dwt[0:1024], dwt[1024:1280], dwt[1280:1408], dwt[1472:1504]
    q_s, k_s, v_s = dwt[1536:2048], dwt[2048:2176], dwt[2176:2304]
    return jnp.concatenate([cq, ckv, kpe, q_s, k_s, v_s, gate], axis=0)


def _layer1_in_weight_t(wt):
    main = jnp.concatenate([wt[:3 * D_MODEL], wt[3 * D_MODEL + FOX_HEADS:]], axis=0)
    return main, _pad_rows(wt[3 * D_MODEL:3 * D_MODEL + FOX_HEADS], LANES)


def _q_up_weight(w):
    return jnp.pad(w.reshape(MLA_Q_RANK, MLA_HEADS, 96), ((0, 0), (0, 0), (0, 32))).reshape(MLA_Q_RANK, MLA_HEADS * LANES)


def _q_up_grad(dwp):
    return dwp.reshape(MLA_Q_RANK, MLA_HEADS, LANES)[:, :, :96].reshape(MLA_Q_RANK, MLA_HEADS * 96)


def _kv_up_weight(w):
    w4 = w.reshape(MLA_KV_RANK, MLA_HEADS, 2, 64)
    kp = jnp.pad(w4[:, :, 0, :], ((0, 0), (0, 0), (0, 64))).reshape(MLA_KV_RANK, MLA_HEADS * LANES)
    vp = w4[:, :, 1, :].reshape(MLA_KV_RANK, MLA_HEADS * 64)
    return jnp.concatenate([kp, vp], axis=1)


def _kv_up_grad(dwp):
    dk = dwp[:, :MLA_HEADS * LANES].reshape(MLA_KV_RANK, MLA_HEADS, LANES)[:, :, :64]
    dv = dwp[:, MLA_HEADS * LANES:].reshape(MLA_KV_RANK, MLA_HEADS, 64)
    return jnp.stack([dk, dv], axis=2).reshape(MLA_KV_RANK, MLA_HEADS * LANES)


def _pad_lanes(a):
    return jnp.pad(a, ((0, 0), (0, LANES - a.shape[1])))


def _small_pack(g_in, g_final, g_q_a, g_kv_a, sinks, b_f, loss):
    rows = [g_in.reshape(8, LANES), g_final.reshape(8, LANES), g_q_a.reshape(2, LANES), g_kv_a.reshape(1, LANES),
            _pad_lanes(sinks.reshape(1, -1)), _pad_lanes(b_f.reshape(1, -1)), _pad_lanes(loss.reshape(1, 1)),
            jnp.zeros((2, LANES), F32)]
    return jnp.concatenate(rows, axis=0)


def _small_unpack(a):
    return (a[0:8].reshape(1, D_MODEL), a[8:16].reshape(D_MODEL), a[16:18].reshape(1, MLA_Q_RANK),
            a[18:19].reshape(1, MLA_KV_RANK), a[19:20, :SWA_HEADS], a[20:21, :FOX_HEADS], a[21, 0])


def _local_step(x, positions, target, e_g_in, early, e_g_q_a, e_g_kv_a, e_sinks,
                late, o_b_f, g_final, scatter1=None, scatter0=None):
    s = x.shape[0]
    mla_scale = (MLA_NOPE + MLA_ROPE) ** -0.5
    fox_scale = FOX_DIM ** -0.5
    n0a = Z0A_UNITS * LANES

    inv_freq = 1.0 / (ROPE_THETA ** (jnp.arange(0, MLA_ROPE, 2, dtype=F32) / MLA_ROPE))
    ang = positions.astype(F32)[:, None] * inv_freq
    cos, sin = jnp.cos(ang), jnp.sin(ang)
    ones, zeros = jnp.ones((s, 64), F32), jnp.zeros((s, 64), F32)
    cos_t = jnp.concatenate([ones, cos, cos, ones[:, :32]], axis=1)
    sin_t = jnp.concatenate([zeros, -sin, sin, zeros[:, :32]], axis=1)

    if len(early) == 3:
        h0 = _rmsnorm_fwd(x, e_g_in, width=D_MODEL, col_blk=0, name="l0_norm")
        w0t, wq, wkv = early
    else:
        pending, token, unpack, prep = early
        h0 = _rmsnorm_fwd(x, e_g_in, width=D_MODEL, col_blk=0, name="l0_norm", after=[token])
        w0t, wq, wkv = unpack(*_peer_wait("gather", *pending, after=[h0] + prep, name="weights0_wait"))
    z0a, z0b = _matmul_rows([(h0, w0t, True)], [], [], lambda r: (r[:, :n0a], r[:, n0a:]),
                            [("rows", n0a, F32), ("rows", Z0B_UNITS * LANES, BF16)], name="l0_in")
    cqn = _rmsnorm_fwd(z0a, e_g_q_a, width=MLA_Q_RANK, col_blk=4, name="l0_q_norm")
    ckvn = _rmsnorm_fwd(z0a, e_g_kv_a, width=MLA_KV_RANK, col_blk=10, name="l0_kv_norm")
    qp = _matmul(cqn, wq, name="l0_q_up")
    kvp = _matmul(ckvn, wkv, name="l0_kv_up", out_dtype=BF16)
    qm, km = _rope_fwd(qp, kvp, z0a, cos_t, sin_t, name="l0_rope")
    gathers = len(late) == 2
    res = _flash_fwd(qm, km, kvp, None, n_pairs=MLA_HEADS // 2, hw=LANES, q_off=0, k_off=0, v_off=MLA_HEADS,
                     scale=mla_scale, name="l0_mla_fwd", rider=("gather", late[0]) if gathers else None)
    o_mla, lse_mla = res[0], res[1]
    wo0, o_g_in, w1t, wft, wo1 = late[1](res[2]) if gathers else late
    o_swa, lse_swa = _swa_fwd(z0b, e_sinks, name="l0_swa_fwd")
    half = D_MODEL // 2

    x1, h1, og0 = _matmul_rows(
        [(None, wo0, False)], [(o_mla, half, 0), (o_swa, half, 0), (z0a, D_MODEL, 0), (x, D_MODEL, 0)], [o_g_in],
        lambda r, om, osw, gt, xt, g, made: (*_residual_norm_epilogue(r, xt, g), made),
        [("rows", D_MODEL, F32), ("rows", D_MODEL, BF16), ("rows", D_MODEL, BF16)], name="l0_out",
        prologue=lambda om, osw, gt, xt, g: _gated([om, osw], gt))
    z1, gate1, zf = _matmul_rows(
        [(None, w1t, True), (None, wft, True)], [(h1, D_MODEL, 0)], [],
        lambda r, h, made: (r[0][:, :3 * D_MODEL], r[0][:, 3 * D_MODEL:], r[1]),
        [("rows", 3 * D_MODEL, BF16), ("rows", D_MODEL, F32), ("rows", LANES, F32)], name="l1_in",
        prologue=lambda h: h, separate=True)
    bf = _pad_lanes(o_b_f)
    log_cum = _logf_fwd(zf, bf, name="l1_logf")
    bias2 = (-LOG2E * log_cum[:, :FOX_HEADS]).T
    t_bwd = min(ATT_T, s)
    bias = bias2.reshape(FOX_HEADS // 2, 2, s // t_bwd, 1, t_bwd)
    t_fwd = _fwd_tile(s)
    o_fox, lse_fox = _flash_fwd(z1, z1, z1, bias2.reshape(FOX_HEADS // 2, 2, s // t_fwd, 1, t_fwd),
                                n_pairs=FOX_HEADS // 2, hw=64, q_off=0, k_off=8, v_off=16, scale=fox_scale,
                                name="l1_fox_fwd")

    dx2, loss_part, d_g_final, og1, dx2_bf = _matmul_rows(
        [(None, wo1, False)], [(o_fox, D_MODEL, 0), (gate1, D_MODEL, 0), (x1, D_MODEL, 0), (target, D_MODEL, 0)],
        [g_final.reshape(1, D_MODEL)],
        lambda r, o, gt, xt, tg, g, made: _and_first(_loss_epilogue(r, xt, tg, g), made),
        [("rows", D_MODEL, F32), ("sum", (8, LANES)), ("sum", (1, D_MODEL)), ("rows", D_MODEL, BF16),
         ("rows", D_MODEL, BF16)], name="l1_out_loss", prologue=lambda o, gt, xt, tg, g: _gated([o], gt))

    d_wo1 = _matmul(og1, dx2_bf, ta=True, out_dtype=BF16, name="l1_out_dw")
    do_fox, d_gate1 = _matmul_rows([(dx2_bf, wo1, True)], [(o_fox, D_MODEL, 0), (gate1, D_MODEL, 0)], [],
                                   _gate_bwd_epilogue([D_MODEL]), [("rows", D_MODEL, F32), ("rows", D_MODEL, BF16)],
                                   name="l1_out_dx")
    dqkv1, dbias, drow = _flash_bwd(z1, z1, z1, do_fox, o_fox, lse_fox, bias, n_pairs=FOX_HEADS // 2, hw=64, q_off=0,
                                    k_off=8, v_off=16, scale=fox_scale, qk_dtype=BF16, stacked=True, name="l1_fox_bwd")
    d_log_cum = (drow.reshape(FOX_HEADS, s) - dbias.reshape(FOX_HEADS, s)).T
    d_log_cum = jnp.pad(d_log_cum, ((0, 0), (0, LANES - FOX_HEADS)))
    d_zf, d_bf = _logf_bwd(d_log_cum, zf, bf, name="l1_logf_bwd")
    d_w1t = (_matmul(dqkv1, h1, ta=True, out_dtype=BF16, name="l1_in_dw_qkv"),
             _matmul(d_gate1, h1, ta=True, out_dtype=BF16, name="l1_in_dw_gate"))
    d_wft = _matmul(d_zf, h1, ta=True, out_dtype=BF16, name="l1_in_f_dw")
    dx1, d_o_g_in, dx1_bf = _matmul_rows([(dqkv1, w1t, False, c * D_MODEL, c) for c in range(3)]
                                         + [(d_gate1, w1t, False, 3 * D_MODEL), (d_zf, wft, False)],
                                         [(x1, D_MODEL, 0), (dx2, D_MODEL, 0)], [o_g_in],
                                         lambda *a: _and_first(_rms_bwd_epilogue(*a)),
                                         [("rows", D_MODEL, F32), ("sum", (1, D_MODEL)), ("rows", D_MODEL, BF16)],
                                         name="l1_in_dx")

    d_wo0 = _matmul(og0, dx1_bf, ta=True, out_dtype=BF16, name="l0_out_dw")
    do_mla, do_swa, d_gate0 = _matmul_rows(
        [(dx1_bf, wo0, True)], [(o_mla, half, 0), (o_swa, half, 0), (z0a, D_MODEL, 0)], [], _gate_bwd_epilogue([half, half]),
        [("rows", half, F32), ("rows", half, F32), ("rows", D_MODEL, BF16)], name="l0_out_dx")
    dq_s, dkt_s, dvt_s, d_sinks = _swa_bwd(z0b, e_sinks, do_swa, o_swa, lse_swa, name="l0_swa_bwd")
    dk_s = dkt_s.transpose(0, 2, 1).reshape(s, LANES)
    dv_s = dvt_s.transpose(0, 2, 1).reshape(s, LANES)
    rider = None
    if scatter1 is not None:
        rider = ("exchange", scatter1(dict(w1t=d_w1t, wft=d_wft, wo1=d_wo1, o_g_in=d_o_g_in, wo0=d_wo0)))
    res = _flash_bwd(qm, km, kvp, do_mla, o_mla, lse_mla, None, n_pairs=MLA_HEADS // 2, hw=LANES, q_off=0, k_off=0,
                     v_off=MLA_HEADS, scale=mla_scale, qk_dtype=F32, name="l0_mla_bwd", rider=rider)
    dqm, dkm, dvm = res[0], res[1], res[2]
    recv1 = res[3] if rider is not None else None
    d_qp, d_kvp, d_kpe = _rope_bwd(dqm, dkm, dvm, cos_t, sin_t, name="l0_rope_bwd")
    d_wq = _matmul(cqn, d_qp, ta=True, out_dtype=BF16, name="l0_q_up_dw")
    d_cqn = _matmul(d_qp, wq, tb=True, name="l0_q_up_dx")
    d_wkv = _matmul(ckvn, d_kvp, ta=True, out_dtype=BF16, name="l0_kv_up_dw")
    d_ckvn = _matmul(d_kvp, wkv, tb=True, name="l0_kv_up_dx")
    d_cq, d_g_q_a = _rmsnorm_bwd(z0a, e_g_q_a, d_cqn, width=MLA_Q_RANK, col_blk=4, name="l0_q_norm_bwd")
    d_ckv, d_g_kv_a = _rmsnorm_bwd(z0a, e_g_kv_a, d_ckvn, width=MLA_KV_RANK, col_blk=10, name="l0_kv_norm_bwd")
    dz0 = jnp.concatenate([d_gate0, d_cq, d_ckv, d_kpe, dq_s.astype(BF16), dk_s.astype(BF16), dv_s.astype(BF16)], axis=1)
    d_w0t = _matmul(dz0, h0, ta=True, out_dtype=BF16, name="l0_in_dw")
    pending0, after_start = None, []
    if scatter0 is not None:
        *pending0, token = _peer_start("exchange", scatter0(dict(w0t=d_w0t, wq=d_wq, wkv=d_wkv)), name="grads0_start")
        after_start = [token]
    grad_x, d_e_g_in = _matmul_rows(
        [(dz0, w0t, False)], [(x, D_MODEL, 0), (dx1, D_MODEL, 0)], [e_g_in] + after_start,
        lambda dy, xt, add, g, *_: _rms_bwd_epilogue(dy, xt, add, g),
        [("rows", D_MODEL, F32), ("sum", (1, D_MODEL))], name="l0_in_dx")

    return dict(pending0=pending0, recv1=recv1, loss=loss_part[0, 0], grad_x=grad_x, e_g_in=d_e_g_in, w0t=d_w0t, e_g_q_a=d_g_q_a, wq=d_wq,
                e_g_kv_a=d_g_kv_a, wkv=d_wkv, e_sinks=d_sinks[:, 0].reshape(1, SWA_HEADS), wo0=d_wo0,
                o_g_in=d_o_g_in, w1t=d_w1t, wft=d_wft, o_b_f=d_bf[:, :FOX_HEADS], wo1=d_wo1, g_final=d_g_final.reshape(D_MODEL))


def _wide(a, rows):
    flat = a.reshape(-1)
    return jnp.pad(flat, (0, rows * WIDE - flat.shape[0])).reshape(rows, WIDE)


def _rows_b0(w_q, w_kv):
    return jnp.concatenate([_wide(w_q, 32), _wide(w_kv, 16)], axis=0)


def _unflat_b0(f):
    return f[0:24].reshape(1, MLA_Q_RANK, 96), f[32:48].reshape(1, MLA_KV_RANK, 128)


def _rows_b1(o_w_out, e_w_out, g_in):
    return jnp.concatenate([o_w_out, e_w_out, _wide(g_in, 16)], axis=0)


def _unflat_b1(f):
    return f[0:128][None], f[128:256][None], f[256:257, :LANES]


def kernel(x, positions, e_g_in, e_w_in, e_g_q_a, e_w_q_up, e_g_kv_a, e_w_kv_up, e_sinks, e_w_out, o_g_in, o_w_in, o_b_f, o_w_out, g_final, loss_target, m_e_g_in, m_e_w_in, m_e_g_q_a, m_e_w_q_up, m_e_g_kv_a, m_e_w_kv_up, m_e_sinks, m_e_w_out, m_o_g_in, m_o_w_in, m_o_b_f, m_o_w_out, m_g_final, v_e_g_in, v_e_w_in, v_e_g_q_a, v_e_w_q_up, v_e_g_kv_a, v_e_w_kv_up, v_e_sinks, v_e_w_out, v_o_g_in, v_o_w_in, v_o_b_f, v_o_w_out, v_g_final):
    def bf(a):
        return a.astype(BF16)

    me = 4 * lax.axis_index("x") + 2 * lax.axis_index("y") + lax.axis_index("c")
    shard0 = jnp.concatenate([_pad_rows(bf(e_w_in[0]).T, RA0), _rows_b0(bf(e_w_q_up[0]), bf(e_w_kv_up[0]))], axis=0)
    *pending_w0, token_w0 = _peer_start("gather", shard0, name="weights0_start")

    def unpack0(sent, gath0):
        gath0 = lax.dynamic_update_slice_in_dim(gath0, sent[None], me, axis=0)
        w0t = _layer0_in_weight_t(gath0[:, :N_E_IN].reshape(N_DEV * N_E_IN, WIDE))
        wq = _q_up_weight(_gathered_cols(gath0[:, RA0:RA0 + 24], MLA_Q_RANK))
        wkv = _kv_up_weight(_gathered_cols(gath0[:, RA0 + 32:RA0 + 48], MLA_KV_RANK))
        return w0t, wq, wkv

    rows_b0 = [_rows_b0(q[0], kv[0]) for q, kv in ((e_w_q_up, e_w_kv_up), (m_e_w_q_up, m_e_w_kv_up), (v_e_w_q_up, v_e_w_kv_up))]
    rows_b1 = [_rows_b1(o[0], e[0], g) for o, e, g in ((o_w_out, e_w_out, o_g_in), (m_o_w_out, m_e_w_out, m_o_g_in),
                                                       (v_o_w_out, v_e_w_out, v_o_g_in))]

    g_bits = lax.bitcast_convert_type(o_g_in.reshape(LANES), BF16)
    shard1 = jnp.concatenate([_window_of_shard(bf(o_w_in[0]).T, me),
                              _rows_b1(bf(o_w_out[0]), bf(e_w_out[0]), g_bits)], axis=0)

    def unpack1(gath1):
        w1t, wft = _layer1_in_weight_t(_from_windows(gath1[:, :RA1]))
        wo1 = gath1[:, RA1:RA1 + 128].reshape(D_MODEL, D_MODEL)
        wo0 = gath1[:, RA1 + 128:RA1 + 256].reshape(D_MODEL, D_MODEL)
        bits = gath1[:, RA1 + 256, :2 * LANES].reshape(N_DEV, LANES, 2)
        return wo0, lax.bitcast_convert_type(bits, F32).reshape(1, D_MODEL), w1t, wft, wo1

    def scatter1(g):
        d_o_g = jnp.pad(bf(g["o_g_in"]).reshape(N_DEV, 1, LANES), ((0, 0), (0, 15), (0, WIDE - LANES)))
        arrays = [g["w1t"][0], g["wft"], g["w1t"][1], g["wo1"], g["wo0"], d_o_g.reshape(N_DEV * 16, WIDE)]
        plan = []
        for p in range(N_DEV):
            plan += _window_plan([3 * D_MODEL, FOX_HEADS, D_MODEL], p)
            plan += [(3, 128 * p, 128, p, RA1), (4, 128 * p, 128, p, RA1 + 128), (5, 16 * p, 16, p, RA1 + 256)]
        return _pack_rows(arrays, plan, RA1 + RB1, name="grads1_pack")

    def scatter0(g):
        return jnp.concatenate([
            _pad_rows(_layer0_in_grad_t(g["w0t"]).reshape(N_DEV, N_E_IN, WIDE), RA0),
            _pad_rows(_scatter_cols(_q_up_grad(g["wq"])), 32), _scatter_cols(_kv_up_grad(g["wkv"]))], axis=1)

    gr = _local_step(x[0], positions[0], loss_target[0], e_g_in,
                     (pending_w0, token_w0, unpack0, [shard1] + rows_b0 + rows_b1), e_g_q_a, e_g_kv_a, e_sinks,
                     (shard1, unpack1), o_b_f, g_final, scatter1=scatter1, scatter0=scatter0)

    def in_projection(recv, ra, first, n, w, m, v, name):
        g = _sum8(recv, ra, name=name + "_grad_sum")
        g = lax.dynamic_slice_in_dim(g, first, n, axis=0).reshape(n, 1, D_MODEL)
        w, m, v = [jnp.transpose(a, (2, 0, 1)) for a in (w, m, v)]
        return (g, *_adamw_columns(g, w, m, v, name=name + "_adamw"))

    o_in = in_projection(gr["recv1"], RA1, me * (N_O_IN - O_STRIDE), N_O_IN, o_w_in, m_o_w_in, v_o_w_in, "o_w_in")
    b1 = _adamw(gr["recv1"][:, RA1:], *rows_b1, name="adamw_late")

    small = _small_pack(gr["e_g_in"], gr["g_final"], gr["e_g_q_a"], gr["e_g_kv_a"], gr["e_sinks"], gr["o_b_f"], gr["loss"])
    small_all = _all_gather(small, name="small_all_gather")
    zero = jnp.zeros((), F32)
    w_small = _small_pack(e_g_in, g_final, e_g_q_a, e_g_kv_a, e_sinks, o_b_f, zero)
    m_small = _small_pack(m_e_g_in, m_g_final, m_e_g_q_a, m_e_g_kv_a, m_e_sinks, m_o_b_f, zero)
    v_small = _small_pack(v_e_g_in, v_g_final, v_e_g_q_a, v_e_g_kv_a, v_e_sinks, v_o_b_f, zero)
    smalls = _adamw(small_all, w_small, m_small, v_small, name="adamw_replicated")
    g_sm, d_sm, m_sm, v_sm = [_small_unpack(a) for a in smalls]
    loss = g_sm[6]

    sent0, recv0 = _peer_wait("exchange", *gr["pending0"], after=[o_in[1], b1[1], smalls[1]], name="grads0_wait")
    own = lax.dynamic_slice_in_dim(sent0, me, 1, axis=0)
    recv0 = lax.dynamic_update_slice_in_dim(recv0, own, me, axis=0)
    e_in = in_projection(recv0, RA0, 0, N_E_IN, e_w_in, m_e_w_in, v_e_w_in, "e_w_in")
    b0 = _adamw(recv0[:, RA0:], *rows_b0, name="adamw_early")

    def sharded(k):
        q_up, kv_up = _unflat_b0(b0[k])
        o_out, e_out, o_g = _unflat_b1(b1[k])
        return jnp.transpose(e_in[k], (1, 2, 0)), q_up, kv_up, e_out, jnp.transpose(o_in[k], (1, 2, 0)), o_out, o_g

    g_sh, d_sh, m_sh, v_sh = [sharded(k) for k in range(4)]

    def leaves(sh, sm):
        return (sm[0], sh[0], sm[2], sh[1], sm[3], sh[2], sm[4], sh[3], sh[6], sh[4], sm[5], sh[5], sm[1])

    return (loss, gr["grad_x"][None], *leaves(g_sh, g_sm), *leaves(d_sh, d_sm), *leaves(m_sh, m_sm), *leaves(v_sh, v_sm))
```

```python
import functools

import jax
import jax.numpy as jnp
from jax import lax
from jax.experimental import pallas as pl
from jax.experimental.pallas import tpu as pltpu

F32 = jnp.float32
BF16 = jnp.bfloat16
NEG_INF = float("-inf")

N_DEV = 8
LANES = 128
D_MODEL = 1024
EPS = 1e-6
ROPE_THETA = 10000.0
MLA_HEADS = 8
MLA_Q_RANK = 256
MLA_KV_RANK = 128
MLA_NOPE = 64
MLA_ROPE = 32
MLA_V = 64
SWA_HEADS = 8
SWA_KV_HEADS = 2
SWA_DIM = 64
WINDOW = 128
FOX_HEADS = 16
FOX_DIM = 64

ADAM_LR = 0.001
ADAM_B1 = 0.9
ADAM_B2 = 0.999
ADAM_EPS = 1e-08
ADAM_WD = 0.01
ADAM_STEP = 10

ATT_T = 512
ATT_T_FWD = 1024
VMEM_LIMIT = 56 * 1024 * 1024
MATMUL_B_BLOCK_BYTES = 8 * 1024 * 1024

Z0A_UNITS = 12
Z0B_UNITS = 6

WIDE = 1024
N_E_IN = 276
N_O_IN = 514
RA0 = 288
RB0 = 32 + 16
RA1 = 528
RB1 = 128 + 128 + 16
SMALL_ROWS = 24


def _tile(n, cands):
    for c in cands:
        if n % c == 0:
            return c
    raise ValueError(f"no tile for {n}")


ROW_TILES = (512, 256, 128)


def _params(sem, vmem=VMEM_LIMIT):
    return pltpu.CompilerParams(dimension_semantics=sem, vmem_limit_bytes=vmem)


def _matmul(a, b, *, name, ta=False, tb=False, out_dtype=F32, b_rows=None):
    if ta:
        kdim, m = a.shape[-2], a.shape[-1] * (a.shape[0] if a.ndim == 3 else 1)
    else:
        m, kdim = a.shape
    if tb:
        n, kb = b.shape
    else:
        kb, n = b.shape
    assert kdim == kb, (a.shape, b.shape)
    b_start = 0
    if b_rows is not None:
        assert tb
        b_start, n = b_rows
    tm = _tile(m, (512, 256, 128))
    tn = _tile(n, [c for c in (1024, 768, 512, 384, 256, 128)
                   if c * kdim * b.dtype.itemsize <= MATMUL_B_BLOCK_BYTES and b_start % c == 0])
    assert b_start % tn == 0, (b_start, tn)
    b_off = b_start // tn
    dims = (((0 if ta else 1,), (1 if tb else 0,)), ((), ()))

    def body(a_ref, b_ref, o_ref):
        r = lax.dot_general(a_ref[...].astype(BF16), b_ref[...].astype(BF16), dims, preferred_element_type=F32)
        o_ref[...] = r.astype(out_dtype)

    if a.ndim == 3:
        per = a.shape[2] // tm
        a_spec = pl.BlockSpec((None, kdim, tm), lambda i, j: (i // per, 0, i % per))
    else:
        a_spec = pl.BlockSpec((kdim, tm), lambda i, j: (0, i)) if ta else pl.BlockSpec((tm, kdim), lambda i, j: (i, 0))
    b_spec = pl.BlockSpec((tn, kdim), lambda i, j: (j + b_off, 0)) if tb else pl.BlockSpec((kdim, tn), lambda i, j: (0, j))
    return pl.pallas_call(
        body, name=name, grid=(m // tm, n // tn), in_specs=[a_spec, b_spec],
        out_specs=pl.BlockSpec((tm, tn), lambda i, j: (i, j)), out_shape=jax.ShapeDtypeStruct((m, n), out_dtype),
        compiler_params=_params(("parallel", "parallel")),
    )(a, b)


def _rmsnorm_fwd(x, g, *, width, col_blk, name, after=()):
    s = x.shape[0]
    tm = _tile(s, ROW_TILES)

    def body(x_ref, g_ref, *rest):
        y_ref = rest[-1]
        xf = x_ref[...].astype(F32)
        r = lax.rsqrt(jnp.mean(xf * xf, axis=-1, keepdims=True) + EPS)
        y_ref[...] = ((xf * r) * g_ref[...]).astype(BF16)

    return pl.pallas_call(
        body, name=name, grid=(s // tm,),
        in_specs=[pl.BlockSpec((tm, width), lambda i: (i, col_blk)), pl.BlockSpec((1, width), lambda i: (0, 0))]
        + [ANY] * len(after),
        out_specs=pl.BlockSpec((tm, width), lambda i: (i, 0)),
        out_shape=jax.ShapeDtypeStruct((s, width), BF16),
        compiler_params=_params(("parallel",)),
    )(x, g, *after)


def _rmsnorm_bwd(x, g, dy, *, width, col_blk, name):
    s = x.shape[0]
    tm = _tile(s, ROW_TILES)

    def body(x_ref, g_ref, dy_ref, dx_ref, dg_ref):
        @pl.when(pl.program_id(0) == 0)
        def _():
            dg_ref[...] = jnp.zeros_like(dg_ref)

        dx, dg = _rms_bwd_epilogue(dy_ref[...], x_ref[...], 0.0, g_ref[...])
        dg_ref[...] += dg
        dx_ref[...] = dx.astype(BF16)

    return pl.pallas_call(
        body, name=name, grid=(s // tm,),
        in_specs=[pl.BlockSpec((tm, width), lambda i: (i, col_blk)), pl.BlockSpec((1, width), lambda i: (0, 0)),
                  pl.BlockSpec((tm, width), lambda i: (i, 0))],
        out_specs=[pl.BlockSpec((tm, width), lambda i: (i, 0)), pl.BlockSpec((1, width), lambda i: (0, 0))],
        out_shape=[jax.ShapeDtypeStruct((s, width), BF16), jax.ShapeDtypeStruct((1, width), F32)],
        compiler_params=_params(("arbitrary",)),
    )(x, g, dy)


def _sigmoid(x):
    return 1.0 / (1.0 + jnp.exp(-x))


def _matmul_rows(terms, row_inputs, params, epilogue, outs, *, name, prologue=None, separate=False):
    s = row_inputs[0][0].shape[0] if row_inputs else terms[0][0].shape[-2]
    tm = _tile(s, ROW_TILES)
    steps = s // tm
    n_r, n_p, n_o = len(row_inputs), len(params), len(outs)
    n_t = sum(1 if term[0] is None else 2 for term in terms)

    def body(*refs):
        t_refs, r_refs = list(refs[:n_t]), refs[n_t:n_t + n_r]
        p_refs, o_refs = refs[n_t + n_r:n_t + n_r + n_p], refs[n_t + n_r + n_p:]
        i = pl.program_id(0)
        rows, small = [r[...] for r in r_refs], [p[...] for p in p_refs]
        made = None if prologue is None else prologue(*rows, *small)
        parts = []
        for term in terms:
            a = made if term[0] is None else t_refs.pop(0)[...].astype(BF16)
            dims = (((1,), (1 if term[2] else 0,)), ((), ()))
            parts.append(lax.dot_general(a, t_refs.pop(0)[...].astype(BF16), dims, preferred_element_type=F32))
        acc = parts if separate else sum(parts[1:], parts[0])
        vals = epilogue(acc, *rows, *small) if prologue is None else epilogue(acc, *rows, *small, made)
        for ref, val, out in zip(o_refs, vals, outs):
            if out[0] == "rows":
                ref[...] = val.astype(ref.dtype)
            else:
                @pl.when(i == 0)
                def _(ref=ref):
                    ref[...] = jnp.zeros_like(ref)

                ref[...] += val

    in_specs, args = [], []
    for term in terms:
        a, b = term[0], term[1]
        if a is None:
            in_specs.append(_resident(b.shape, lambda i: (0, 0)))
            args.append(b)
            continue
        b_rows = b.shape[0] if term[2] or len(term) < 4 else a.shape[-1]
        b_blk = 0 if len(term) < 4 else term[3] // b_rows
        if len(term) == 5:
            a_spec = pl.BlockSpec((None, tm, a.shape[2]), lambda i, c=term[4]: (c, i, 0))
        else:
            a_spec = pl.BlockSpec((tm, a.shape[1]), lambda i: (i, 0))
        in_specs += [a_spec, _resident((b_rows, b.shape[1]), lambda i, b_blk=b_blk: (b_blk, 0))]
        args += [a, b]
    for arr, width, col_blk in row_inputs:
        in_specs.append(pl.BlockSpec((tm, width), lambda i, col_blk=col_blk: (i, col_blk)))
        args.append(arr)
    for p in params:
        in_specs.append(pl.BlockSpec(p.shape, lambda i: (0, 0)))
        args.append(p)
    out_specs, out_shape = [], []
    for out in outs:
        if out[0] == "rows":
            out_specs.append(pl.BlockSpec((tm, out[1]), lambda i: (i, 0)))
            out_shape.append(jax.ShapeDtypeStruct((s, out[1]), out[2]))
        else:
            out_specs.append(pl.BlockSpec(out[1], lambda i: (0, 0)))
            out_shape.append(jax.ShapeDtypeStruct(out[1], F32))
    return pl.pallas_call(
        body, name=name, grid=(steps,), in_specs=in_specs, out_specs=out_specs, out_shape=out_shape,
        compiler_params=_params(("arbitrary",)),
    )(*args)


def _rms_stats(x):
    r = lax.rsqrt(jnp.mean(x * x, axis=-1, keepdims=True) + EPS)
    return r, x * r


def _gated(o_parts, gate):
    o = o_parts[0] if len(o_parts) == 1 else jnp.concatenate(o_parts, axis=1)
    return (o * (gate * _sigmoid(gate))).astype(BF16)


def _and_first(vals, *more):
    return (*vals, *more, vals[0])


def _residual_norm_epilogue(r, x, g):
    x1 = x + r
    _, xh = _rms_stats(x1)
    return x1, xh * g


def _rms_bwd_epilogue(dy, x, add, g):
    r, xh = _rms_stats(x)
    dxh = dy * g
    dx = r * (dxh - xh * jnp.mean(dxh * xh, axis=-1, keepdims=True)) + add
    return dx, jnp.sum(dy * xh, axis=0, keepdims=True)


def _loss_epilogue(r, x1, target, g):
    rs, xh = _rms_stats(x1 + r)
    err = xh * g - target
    loss = jnp.broadcast_to(0.5 * jnp.sum(jnp.mean(err * err, axis=-1, keepdims=True)), (8, LANES))
    dy = err * (1.0 / D_MODEL)
    dxh = dy * g
    dx = rs * (dxh - xh * jnp.mean(dxh * xh, axis=-1, keepdims=True))
    return dx, loss, jnp.sum(dy * xh, axis=0, keepdims=True)


def _gate_bwd_epilogue(widths):
    def epilogue(d, *rows):
        o_parts, gt = rows[:-1], rows[-1]
        o = o_parts[0] if len(o_parts) == 1 else jnp.concatenate(o_parts, axis=1)
        sg = _sigmoid(gt)
        do = d * (gt * sg)
        d_gate = d * o * (sg * (1.0 + gt * (1.0 - sg)))
        cuts = [sum(widths[:k]) for k in range(len(widths) + 1)]
        return tuple(do[:, cuts[k]:cuts[k + 1]] for k in range(len(widths))) + (d_gate,)

    return epilogue


def _rot_half(x):
    lane = lax.broadcasted_iota(jnp.int32, x.shape, 1)
    return jnp.where(lane < 80, pltpu.roll(x, LANES - 16, axis=1), pltpu.roll(x, 16, axis=1))


def _rot_half_t(g):
    lane = lax.broadcasted_iota(jnp.int32, g.shape, 1)
    lo = (lane >= MLA_NOPE) & (lane < MLA_NOPE + MLA_ROPE // 2)
    hi = (lane >= MLA_NOPE + MLA_ROPE // 2) & (lane < MLA_NOPE + MLA_ROPE)
    return jnp.where(lo, pltpu.roll(g, LANES - 16, axis=1), jnp.where(hi, pltpu.roll(g, 16, axis=1), 0.0))


def _rope_q_epilogue(q, c, sn):
    heads = [q[:, h * LANES:(h + 1) * LANES] for h in range(MLA_HEADS)]
    return (jnp.concatenate([qh * c + _rot_half(qh) * sn for qh in heads], axis=1),)


def _rope_k_epilogue(kv, kpe, c, sn):
    kpe_r = kpe * c + _rot_half(kpe) * sn
    lane = lax.broadcasted_iota(jnp.int32, kpe.shape, 1)
    heads = [jnp.where(lane < MLA_NOPE, kv[:, h * LANES:(h + 1) * LANES], kpe_r) for h in range(MLA_HEADS)]
    return (jnp.concatenate(heads + [kv[:, MLA_HEADS * LANES:]], axis=1),)


def _rope_bwd(dqm, dkm, dvm, cos_t, sin_t, *, name):
    s = dqm.shape[0]
    tm = _tile(s, ROW_TILES)
    hw = MLA_HEADS * LANES
    vw = MLA_HEADS * MLA_V

    def body(dq_ref, dk_ref, dv_ref, c_ref, s_ref, dqp_ref, dkv_ref, dkpe_ref):
        c = c_ref[...]
        sn = s_ref[...]
        ksum = jnp.zeros((tm, LANES), F32)
        for h in range(MLA_HEADS):
            sl = slice(h * LANES, (h + 1) * LANES)
            dq = dq_ref[:, sl]
            dqp_ref[:, sl] = (dq * c + _rot_half_t(dq * sn)).astype(BF16)
            dk = dk_ref[:, sl]
            dkv_ref[:, sl] = dk.astype(BF16)
            ksum = ksum + dk
        dkv_ref[:, hw:] = dv_ref[...]
        lane = lax.broadcasted_iota(jnp.int32, ksum.shape, 1)
        dkpe = ksum * c + _rot_half_t(ksum * sn)
        dkpe_ref[...] = jnp.where((lane >= MLA_NOPE) & (lane < MLA_NOPE + MLA_ROPE), dkpe, 0.0).astype(BF16)

    return pl.pallas_call(
        body, name=name, grid=(s // tm,),
        in_specs=[pl.BlockSpec((tm, hw), lambda i: (i, 0)), pl.BlockSpec((tm, hw), lambda i: (i, 0)),
                  pl.BlockSpec((tm, vw), lambda i: (i, 0)),
                  pl.BlockSpec((tm, LANES), lambda i: (i, 0)), pl.BlockSpec((tm, LANES), lambda i: (i, 0))],
        out_specs=[pl.BlockSpec((tm, hw), lambda i: (i, 0)), pl.BlockSpec((tm, hw + vw), lambda i: (i, 0)),
                   pl.BlockSpec((tm, LANES), lambda i: (i, 0))],
        out_shape=[jax.ShapeDtypeStruct((s, hw), BF16), jax.ShapeDtypeStruct((s, hw + vw), BF16),
                   jax.ShapeDtypeStruct((s, LANES), BF16)],
        compiler_params=_params(("parallel",)),
    )(dqm, dkm, dvm, cos_t, sin_t)


def _head_mask(shape, a):
    lane = lax.broadcasted_iota(jnp.int32, shape, 1)
    return (lane >= 64 * a) & (lane < 64 * (a + 1))


_NT = (((1,), (1,)), ((), ()))
LOG2E = 1.4426950408889634


def _stack_heads(tile, hw):
    lane = lax.broadcasted_iota(jnp.int32, tile.shape, 1)
    z = jnp.zeros_like(tile)
    return jnp.concatenate([jnp.where(lane < hw, tile, z), jnp.where(lane >= hw, tile, z)], axis=0)


def _stacked_rows(r0, r1, t):
    n = r0.shape[-1]
    return jnp.concatenate([jnp.broadcast_to(r0, (t, n)), jnp.broadcast_to(r1, (t, n))], axis=0)


def _resident(block, index_map):
    return pl.BlockSpec(block, index_map, pipeline_mode=pl.Buffered(1))


def _fwd_tile(s):
    return ATT_T_FWD if s % ATT_T_FWD == 0 else min(ATT_T, s)


def _flash_fwd(q, k, v, bias, *, n_pairs, hw, q_off, k_off, v_off, scale, name, rider=None):
    s = q.shape[0]
    t = _fwd_tile(s)
    nb = s // t
    qw = 2 * hw
    has_bias = bias is not None
    c1 = scale * LOG2E

    def body(*refs):
        refs, ride_refs = _split_rider(refs, rider, n_in=4 if has_bias else 3, n_out=2)
        if has_bias:
            q_ref, k_ref, v_ref, b_ref, o_ref, lse_ref, vt_ref, bcol_ref = refs
        else:
            q_ref, k_ref, v_ref, o_ref, lse_ref, vt_ref = refs
            b_ref = bcol_ref = None
        _ride_start(rider, ride_refs, pl.program_id(0) == 0)
        row = lax.broadcasted_iota(jnp.int32, (t, t), 0)
        col = lax.broadcasted_iota(jnp.int32, (t, t), 1)
        cmask_t = jnp.concatenate([row <= col, row <= col], axis=1)
        lane_lt64 = lax.broadcasted_iota(jnp.int32, (t, LANES), 1) < 64

        def as_column(r):
            return jnp.broadcast_to(r, (8, r.shape[1])).T[:, 0:1]

        def v_block(j, _):
            c0 = pl.multiple_of(j * t, t)
            vt_ref[j] = v_ref[pl.ds(c0, t), :].astype(F32).T.astype(BF16)
            if has_bias:
                for a in range(2):
                    bcol_ref[a, pl.ds(c0, t), :] = as_column(b_ref[0, a, j])
            return 0

        lax.fori_loop(0, nb, v_block, 0)

        def stacked_queries(i):
            return _stack_heads(q_ref[pl.ds(pl.multiple_of(i * t, t), t), :], hw).astype(F32).T.astype(BF16)

        def kv_step(j, carry, qs_t, masked):
            m, l, acc = carry
            rows = pl.ds(pl.multiple_of(j * t, t), t)
            sc = jnp.dot(k_ref[rows, :], qs_t, preferred_element_type=F32) * c1
            if has_bias:
                sc = sc + jnp.concatenate([jnp.broadcast_to(bcol_ref[0, rows, :], (t, t)),
                                           jnp.broadcast_to(bcol_ref[1, rows, :], (t, t))], axis=1)
            if masked:
                sc = jnp.where(cmask_t, sc, NEG_INF)
            m_new = jnp.maximum(m, jnp.max(sc, axis=0, keepdims=True))
            alpha = jnp.exp2(m - m_new)
            p = jnp.exp2(sc - m_new)
            l_new = alpha * l + jnp.sum(p, axis=0, keepdims=True)
            pv = jnp.dot(vt_ref[j], p.astype(BF16), preferred_element_type=F32)
            return m_new, l_new, alpha * acc + pv

        def finish(i, carry):
            m, l, acc = carry
            r0 = pl.multiple_of(i * t, t)
            out = (acc / l).T
            lse2 = as_column(m + jnp.log2(l))
            lse_ref[0, 0, pl.ds(r0, t), :] = lse2[:t]
            lse_ref[0, 1, pl.ds(r0, t), :] = lse2[t:]
            o_ref[pl.ds(r0, t), :] = jnp.where(lane_lt64, out[:t], out[t:])

        init = (jnp.full((1, 2 * t), NEG_INF, F32), jnp.zeros((1, 2 * t), F32), jnp.zeros((LANES, 2 * t), F32))

        def q_block(i, _):
            qs_t = stacked_queries(i)
            carry = lax.fori_loop(0, i, lambda j, c: kv_step(j, c, qs_t, False), init)
            finish(i, kv_step(i, carry, qs_t, True))
            return 0

        lax.fori_loop(0, nb, q_block, 0)
        _ride_wait(rider, ride_refs, pl.program_id(0) == n_pairs - 1)

    in_specs = [_resident((s, qw), lambda p: (0, q_off + p)), _resident((s, qw), lambda p: (0, k_off + p)),
                _resident((s, LANES), lambda p: (0, v_off + p))]
    args = [q, k, v]
    if has_bias:
        in_specs.append(_resident((1, 2, nb, 1, t), lambda p: (p, 0, 0, 0, 0)))
        args.append(bias)
    out_specs = [pl.BlockSpec((s, LANES), lambda p: (0, p)), pl.BlockSpec((1, 2, s, 1), lambda p: (p, 0, 0, 0))]
    out_shape = [jax.ShapeDtypeStruct((s, n_pairs * LANES), F32), jax.ShapeDtypeStruct((n_pairs, 2, s, 1), F32)]
    scratch = [pltpu.VMEM((nb, LANES, t), BF16)] + ([pltpu.VMEM((2, s, 1), F32)] if has_bias else [])
    scratch += _add_rider(rider, in_specs, args, out_specs, out_shape)
    return pl.pallas_call(
        body, name=name, grid=(n_pairs,), in_specs=in_specs, out_specs=out_specs, out_shape=out_shape,
        scratch_shapes=scratch,
        compiler_params=_params(("parallel",) if rider is None else ("arbitrary",)),
    )(*args)


def _flash_bwd(q, k, v, do, o, lse, bias, *, n_pairs, hw, q_off, k_off, v_off, scale, qk_dtype, name, rider=None,
               stacked=False):
    s = q.shape[0]
    t = min(ATT_T, s)
    nb = s // t
    qw = 2 * hw
    has_bias = bias is not None
    c1 = scale * LOG2E

    def body(*refs):
        n_grads = 1 if stacked else 3
        refs, ride_refs = _split_rider(refs, rider, n_in=7 if has_bias else 6, n_out=n_grads + (2 if has_bias else 0))
        if stacked:
            refs = list(refs)
            n_in = 7 if has_bias else 6
            refs[n_in:n_in + 1] = [refs[n_in].at[0], refs[n_in].at[1], refs[n_in].at[2]]
        if has_bias:
            (q_ref, k_ref, v_ref, do_ref, o_ref, lse_ref, b_ref, dq_ref, dk_ref, dv_ref, db_ref, dr_ref,
             dkt_ref, dvt_ref) = refs
            db_ref[...] = jnp.zeros_like(db_ref)
        else:
            q_ref, k_ref, v_ref, do_ref, o_ref, lse_ref, dq_ref, dk_ref, dv_ref, dkt_ref, dvt_ref = refs
            b_ref = db_ref = dr_ref = None
        _ride_start(rider, ride_refs, pl.program_id(0) == 0)
        dkt_ref[...] = jnp.zeros_like(dkt_ref)
        dvt_ref[...] = jnp.zeros_like(dvt_ref)
        causal = lax.broadcasted_iota(jnp.int32, (t, t), 1) <= lax.broadcasted_iota(jnp.int32, (t, t), 0)
        cmask = jnp.concatenate([causal, causal], axis=0)
        lane_lt_hw = lax.broadcasted_iota(jnp.int32, (t, qw), 1) < hw

        def q_block(i, _):
            r0 = pl.multiple_of(i * t, t)
            qs = _stack_heads(q_ref[pl.ds(r0, t), :], hw)
            dos = _stack_heads(do_ref[pl.ds(r0, t), :], 64)
            ot = o_ref[pl.ds(r0, t), :]
            delta = jnp.sum(dos * jnp.concatenate([ot, ot], axis=0), axis=-1, keepdims=True)
            lse2 = jnp.concatenate([lse_ref[0, 0, pl.ds(r0, t), :], lse_ref[0, 1, pl.ds(r0, t), :]], axis=0)
            dosb = dos.astype(BF16)
            dos_t = dos.T.astype(BF16)
            qs_t = qs.astype(F32).T.astype(BF16)

            def kv_step(j, carry, masked):
                dq, rsum = carry
                c0 = pl.multiple_of(j * t, t)
                kt = k_ref[pl.ds(c0, t), :]
                vt = v_ref[pl.ds(c0, t), :]
                sc = lax.dot_general(qs, kt, _NT, preferred_element_type=F32) * c1
                if has_bias:
                    sc = sc + _stacked_rows(b_ref[0, 0, j], b_ref[0, 1, j], t)
                if masked:
                    sc = jnp.where(cmask, sc, NEG_INF)
                p = jnp.exp2(sc - lse2)
                dp = lax.dot_general(dosb, vt, _NT, preferred_element_type=F32)
                ds = p * (dp - delta)
                dsb = ds.astype(BF16)
                pb = p.astype(BF16)
                if hw == LANES:
                    dvt_ref[j] += jnp.concatenate(
                        [jnp.dot(dos_t[:64, :t], pb[:t], preferred_element_type=F32),
                         jnp.dot(dos_t[64:, t:], pb[t:], preferred_element_type=F32)], axis=0)
                    dkt_ref[j] += jnp.concatenate(
                        [jnp.dot(qs_t[:hw, :t], dsb[:t], preferred_element_type=F32),
                         jnp.dot(qs_t[hw:, t:], dsb[t:], preferred_element_type=F32)], axis=0)
                else:
                    dvt_ref[j] += jnp.dot(dos_t, pb, preferred_element_type=F32)
                    dkt_ref[j] += jnp.dot(qs_t, dsb, preferred_element_type=F32)
                if has_bias:
                    db_ref[0, 0, j] += jnp.sum(ds[:t], axis=0, keepdims=True)
                    db_ref[0, 1, j] += jnp.sum(ds[t:], axis=0, keepdims=True)
                    rsum = rsum + jnp.sum(ds, axis=-1, keepdims=True)
                return dq + jnp.dot(dsb, kt, preferred_element_type=F32), rsum

            init = (jnp.zeros((2 * t, qw), F32), jnp.zeros((2 * t, 1), F32))
            carry = lax.fori_loop(0, i, functools.partial(kv_step, masked=False), init)
            dq, rsum = kv_step(i, carry, True)
            dq = dq * scale
            dq_ref[pl.ds(r0, t), :] = jnp.where(lane_lt_hw, dq[:t], dq[t:]).astype(qk_dtype)
            if has_bias:
                rsum_row = jnp.broadcast_to(rsum, (2 * t, LANES)).T[0:1]
                dr_ref[0, 0, i] = rsum_row[:, :t]
                dr_ref[0, 1, i] = rsum_row[:, t:]
            return 0

        lax.fori_loop(0, nb, q_block, 0)

        def k_block(j, _):
            c0 = pl.multiple_of(j * t, t)
            dk_ref[pl.ds(c0, t), :] = (dkt_ref[j].T * scale).astype(qk_dtype)
            dv_ref[pl.ds(c0, t), :] = dvt_ref[j].T.astype(BF16)
            return 0

        lax.fori_loop(0, nb, k_block, 0)
        _ride_wait(rider, ride_refs, pl.program_id(0) == n_pairs - 1)

    in_specs = [_resident((s, qw), lambda p: (0, q_off + p)), _resident((s, qw), lambda p: (0, k_off + p)),
                _resident((s, LANES), lambda p: (0, v_off + p)),
                _resident((s, LANES), lambda p: (0, p)), _resident((s, LANES), lambda p: (0, p)),
                _resident((1, 2, s, 1), lambda p: (p, 0, 0, 0))]
    args = [q, k, v, do, o, lse]
    if stacked:
        assert qw == LANES and qk_dtype == BF16
        out_specs = [pl.BlockSpec((3, s, LANES), lambda p: (0, 0, p))]
        out_shape = [jax.ShapeDtypeStruct((3, s, n_pairs * LANES), BF16)]
    else:
        out_specs = [pl.BlockSpec((s, qw), lambda p: (0, p)), pl.BlockSpec((s, qw), lambda p: (0, p)),
                     pl.BlockSpec((s, LANES), lambda p: (0, p))]
        out_shape = [jax.ShapeDtypeStruct((s, n_pairs * qw), qk_dtype), jax.ShapeDtypeStruct((s, n_pairs * qw), qk_dtype),
                     jax.ShapeDtypeStruct((s, n_pairs * LANES), BF16)]
    if has_bias:
        in_specs.append(_resident((1, 2, nb, 1, t), lambda p: (p, 0, 0, 0, 0)))
        args.append(bias)
        for _ in range(2):
            out_specs.append(pl.BlockSpec((1, 2, nb, 1, t), lambda p: (p, 0, 0, 0, 0)))
            out_shape.append(jax.ShapeDtypeStruct((n_pairs, 2, nb, 1, t), F32))
    scratch = [pltpu.VMEM((nb, qw, t), F32), pltpu.VMEM((nb, LANES, t), F32)]
    scratch += _add_rider(rider, in_specs, args, out_specs, out_shape)
    return pl.pallas_call(
        body, name=name, grid=(n_pairs,), in_specs=in_specs, out_specs=out_specs, out_shape=out_shape,
        scratch_shapes=scratch,
        compiler_params=_params(("parallel",) if rider is None else ("arbitrary",)),
    )(*args)


def _alibi_slope(h):
    return 2.0 ** (-8.0 * (h + 1.0) / SWA_HEADS)


SWA_ROWS = 512
SWA_SCALE = SWA_DIM ** -0.5


def _swa_geometry(i):
    w = WINDOW
    r0 = pl.multiple_of(i * w, w)
    b0 = pl.multiple_of(jnp.maximum(i - 1, 0) * w, w)
    row = lax.broadcasted_iota(jnp.int32, (w, 2 * w), 0)
    col = lax.broadcasted_iota(jnp.int32, (w, 2 * w), 1)
    dist = row - col + (r0 - b0)
    valid = (dist >= 0) & (dist < w)
    return r0, b0, dist.astype(F32), valid


def _swa_q_head(qblk, h):
    kv = h // (SWA_HEADS // SWA_KV_HEADS)
    if h % 2 != kv:
        qblk = pltpu.roll(qblk, 64, axis=1)
    return jnp.where(_head_mask(qblk.shape, kv), qblk, 0.0)


SWA_GROUP = SWA_HEADS // SWA_KV_HEADS


def _swa_stack(ref, rs, grp):
    parts = []
    for a in range(SWA_GROUP):
        h = SWA_GROUP * grp + a
        parts.append(_swa_q_head(ref[rs, (h // 2) * LANES:(h // 2 + 1) * LANES].astype(F32), h))
    return jnp.concatenate(parts, axis=0)


def _swa_unstack(x, grp):
    tiles = []
    for a in range(SWA_GROUP):
        h = SWA_GROUP * grp + a
        tile = x[a * WINDOW:(a + 1) * WINDOW]
        tiles.append(pltpu.roll(tile, 64, axis=1) if h % 2 != grp else tile)
    return tiles


def _swa_head_column(vals):
    return jnp.concatenate([jnp.full((WINDOW, 1), v, F32) for v in vals], axis=0)


def _swa_logits(qs, kb, dist, valid, grp):
    slopes = _swa_head_column([_alibi_slope(SWA_GROUP * grp + a) for a in range(SWA_GROUP)])
    dist4 = jnp.concatenate([dist] * SWA_GROUP, axis=0)
    valid4 = jnp.concatenate([valid] * SWA_GROUP, axis=0)
    sc = lax.dot_general(qs, kb, _NT, preferred_element_type=F32) * SWA_SCALE - slopes * dist4
    return jnp.where(valid4, sc, NEG_INF)


def _swa_merge_heads(tiles):
    lt64 = lax.broadcasted_iota(jnp.int32, (WINDOW, LANES), 1) < 64
    return jnp.concatenate([jnp.where(lt64, tiles[2 * b], tiles[2 * b + 1]) for b in range(SWA_HEADS // 2)], axis=1)


def _swa_fwd(z0b, sinks, *, name):
    s = z0b.shape[0]
    w = WINDOW
    rows = min(SWA_ROWS, s)
    per_step = rows // w
    qcols = SWA_HEADS * SWA_DIM

    def body(sink_ref, q_ref, k_ref, v_ref, o_ref, lse_ref):
        g = pl.program_id(0)
        for ii in range(per_step):
            rs = slice(ii * w, (ii + 1) * w)
            r0, b0, dist, valid = _swa_geometry(g * per_step + ii)
            kb = k_ref[pl.ds(b0, 2 * w), :]
            vb = v_ref[pl.ds(b0, 2 * w), :]
            o_tiles = []
            for h in range(SWA_HEADS):
                kv = h // SWA_GROUP
                qh = _swa_q_head(q_ref[rs, (h // 2) * LANES:(h // 2 + 1) * LANES].astype(F32), h).astype(BF16)
                sc = lax.dot_general(qh, kb, _NT, preferred_element_type=F32) * SWA_SCALE - _alibi_slope(h) * dist
                sc = jnp.where(valid, sc, NEG_INF)
                sink = sink_ref[0, h]
                m = jnp.maximum(jnp.max(sc, axis=-1, keepdims=True), sink)
                p = jnp.exp(sc - m)
                l = jnp.sum(p, axis=-1, keepdims=True) + jnp.exp(sink - m)
                oh = jnp.dot(p.astype(BF16), vb, preferred_element_type=F32) / l
                o_tiles.append(pltpu.roll(oh, 64, axis=1) if h % 2 != kv else oh)
                lse_ref[h, rs, :] = m + jnp.log(l)
            o_ref[rs, :] = _swa_merge_heads(o_tiles)

    return pl.pallas_call(
        body, name=name, grid=(s // rows,),
        in_specs=[pl.BlockSpec(memory_space=pltpu.SMEM),
                  pl.BlockSpec((rows, qcols), lambda g: (g, 0)),
                  pl.BlockSpec((s, LANES), lambda g: (0, 4)), pl.BlockSpec((s, LANES), lambda g: (0, 5))],
        out_specs=[pl.BlockSpec((rows, qcols), lambda g: (g, 0)), pl.BlockSpec((SWA_HEADS, rows, 1), lambda g: (0, g, 0))],
        out_shape=[jax.ShapeDtypeStruct((s, qcols), F32), jax.ShapeDtypeStruct((SWA_HEADS, s, 1), F32)],
        compiler_params=_params(("parallel",)),
    )(sinks, z0b, z0b, z0b)


def _swa_bwd(z0b, sinks, do, o, lse, *, name):
    s = z0b.shape[0]
    w = WINDOW
    rows = min(SWA_ROWS, s)
    per_step = rows // w
    qcols = SWA_HEADS * SWA_DIM
    nblk = s // w

    def body(sink_ref, q_ref, k_ref, v_ref, do_ref, o_ref, lse_ref, dq_ref, dkt_ref, dvt_ref, dsink_ref):
        g = pl.program_id(0)

        @pl.when(g == 0)
        def _():
            dkt_ref[...] = jnp.zeros_like(dkt_ref)
            dvt_ref[...] = jnp.zeros_like(dvt_ref)
            dsink_ref[...] = jnp.zeros_like(dsink_ref)

        for ii in range(per_step):
            i = g * per_step + ii
            rs = slice(ii * w, (ii + 1) * w)
            r0, b0, dist, valid = _swa_geometry(i)
            j0 = jnp.maximum(i - 1, 0)
            kb = k_ref[pl.ds(b0, 2 * w), :]
            vb = v_ref[pl.ds(b0, 2 * w), :]
            dq_tiles = []
            for grp in range(SWA_KV_HEADS):
                heads = [SWA_GROUP * grp + a for a in range(SWA_GROUP)]
                qs32 = _swa_stack(q_ref, rs, grp)
                dos32 = _swa_stack(do_ref, rs, grp)
                delta = jnp.sum(dos32 * _swa_stack(o_ref, rs, grp), axis=-1, keepdims=True)
                lse = jnp.concatenate([lse_ref[h, rs, :] for h in heads], axis=0)
                sink = _swa_head_column([sink_ref[0, h] for h in heads])
                p = jnp.exp(_swa_logits(qs32.astype(BF16), kb, dist, valid, grp) - lse)
                dp = lax.dot_general(dos32.astype(BF16), vb, _NT, preferred_element_type=F32)
                ds = p * (dp - delta)
                dsb = ds.astype(BF16)
                d_sink = jnp.exp(sink - lse) * delta
                for a, h in enumerate(heads):
                    dsink_ref[h:h + 1, :] += jnp.broadcast_to(-jnp.sum(d_sink[a * w:(a + 1) * w]), (1, LANES))
                dvt = jnp.dot(dos32.T.astype(BF16), p.astype(BF16), preferred_element_type=F32)
                dkt = jnp.dot(qs32.T.astype(BF16), dsb, preferred_element_type=F32) * SWA_SCALE
                dvt_ref[j0] += dvt[:, :w]
                dvt_ref[j0 + 1] += dvt[:, w:]
                dkt_ref[j0] += dkt[:, :w]
                dkt_ref[j0 + 1] += dkt[:, w:]
                dq_tiles += _swa_unstack(jnp.dot(dsb, kb, preferred_element_type=F32) * SWA_SCALE, grp)
            dq_ref[rs, :] = _swa_merge_heads(dq_tiles)

    return pl.pallas_call(
        body, name=name, grid=(s // rows,),
        in_specs=[pl.BlockSpec(memory_space=pltpu.SMEM),
                  pl.BlockSpec((rows, qcols), lambda g: (g, 0)),
                  pl.BlockSpec((s, LANES), lambda g: (0, 4)), pl.BlockSpec((s, LANES), lambda g: (0, 5)),
                  pl.BlockSpec((rows, qcols), lambda g: (g, 0)), pl.BlockSpec((rows, qcols), lambda g: (g, 0)),
                  pl.BlockSpec((SWA_HEADS, rows, 1), lambda g: (0, g, 0))],
        out_specs=[pl.BlockSpec((rows, qcols), lambda g: (g, 0)),
                   pl.BlockSpec((nblk, LANES, w), lambda g: (0, 0, 0)),
                   pl.BlockSpec((nblk, LANES, w), lambda g: (0, 0, 0)),
                   pl.BlockSpec((SWA_HEADS, LANES), lambda g: (0, 0))],
        out_shape=[jax.ShapeDtypeStruct((s, qcols), F32),
                   jax.ShapeDtypeStruct((nblk, LANES, w), F32), jax.ShapeDtypeStruct((nblk, LANES, w), F32),
                   jax.ShapeDtypeStruct((SWA_HEADS, LANES), F32)],
        compiler_params=_params(("arbitrary",)),
    )(sinks, z0b, z0b, z0b, do, o, lse)


CUM_T = 256


def _split3(x):
    hi = x.astype(BF16)
    r1 = x - hi.astype(F32)
    mid = r1.astype(BF16)
    lo = (r1 - mid.astype(F32)).astype(BF16)
    return hi, mid, lo


def _tri_dot(tri, x):
    hi, mid, lo = _split3(x)
    out = jnp.dot(tri, hi, preferred_element_type=F32)
    out = out + jnp.dot(tri, mid, preferred_element_type=F32)
    return out + jnp.dot(tri, lo, preferred_element_type=F32)


def _logf_fwd(zf, bf, *, name):
    s = zf.shape[0]
    t = CUM_T
    nb = s // t

    def body(z_ref, b_ref, c_ref, carry_ref):
        i = pl.program_id(0)

        @pl.when(i == 0)
        def _():
            carry_ref[...] = jnp.zeros_like(carry_ref)

        x = z_ref[...] + b_ref[...]
        lf = jnp.minimum(x, 0.0) - jnp.log(1.0 + jnp.exp(-jnp.abs(x)))
        row = lax.broadcasted_iota(jnp.int32, (t, t), 0)
        col = lax.broadcasted_iota(jnp.int32, (t, t), 1)
        tri = jnp.where(col <= row, 1.0, 0.0).astype(BF16)
        c = _tri_dot(tri, lf) + carry_ref[...]
        c_ref[...] = c
        carry_ref[...] = c[t - 1:t, :]

    return pl.pallas_call(
        body, name=name, grid=(nb,),
        in_specs=[pl.BlockSpec((t, LANES), lambda i: (i, 0)), pl.BlockSpec((1, LANES), lambda i: (0, 0))],
        out_specs=pl.BlockSpec((t, LANES), lambda i: (i, 0)),
        out_shape=jax.ShapeDtypeStruct((s, LANES), F32),
        scratch_shapes=[pltpu.VMEM((1, LANES), F32)],
        compiler_params=_params(("arbitrary",)),
    )(zf, bf)


def _logf_bwd(dc, zf, bf, *, name):
    s = zf.shape[0]
    t = CUM_T
    nb = s // t

    def body(dc_ref, z_ref, b_ref, dz_ref, db_ref, carry_ref):
        i = pl.program_id(0)

        @pl.when(i == 0)
        def _():
            carry_ref[...] = jnp.zeros_like(carry_ref)
            db_ref[...] = jnp.zeros_like(db_ref)

        row = lax.broadcasted_iota(jnp.int32, (t, t), 0)
        col = lax.broadcasted_iota(jnp.int32, (t, t), 1)
        tri = jnp.where(col >= row, 1.0, 0.0).astype(BF16)
        dlf = _tri_dot(tri, dc_ref[...]) + carry_ref[...]
        carry_ref[...] = dlf[0:1, :]
        x = z_ref[...] + b_ref[...]
        dz = dlf * _sigmoid(-x)
        dz_ref[...] = dz.astype(BF16)
        db_ref[...] += jnp.sum(dz, axis=0, keepdims=True)

    return pl.pallas_call(
        body, name=name, grid=(nb,),
        in_specs=[pl.BlockSpec((t, LANES), lambda i: (nb - 1 - i, 0)), pl.BlockSpec((t, LANES), lambda i: (nb - 1 - i, 0)),
                  pl.BlockSpec((1, LANES), lambda i: (0, 0))],
        out_specs=[pl.BlockSpec((t, LANES), lambda i: (nb - 1 - i, 0)), pl.BlockSpec((1, LANES), lambda i: (0, 0))],
        out_shape=[jax.ShapeDtypeStruct((s, LANES), BF16), jax.ShapeDtypeStruct((1, LANES), F32)],
        scratch_shapes=[pltpu.VMEM((1, LANES), F32)],
        compiler_params=_params(("arbitrary",)),
    )(dc, zf, bf)


def _sum_pieces(p_ref):
    g = p_ref[0].astype(F32)
    for k in range(1, N_DEV):
        g = g + p_ref[k].astype(F32)
    return g


def _adam_update(g, w, m, v):
    bc1 = 1.0 - ADAM_B1 ** ADAM_STEP
    bc2 = 1.0 - ADAM_B2 ** ADAM_STEP
    nm = ADAM_B1 * m + (1.0 - ADAM_B1) * g
    nv = ADAM_B2 * v + (1.0 - ADAM_B2) * (g * g)
    m_hat = nm / bc1
    v_hat = nv / bc2
    return -ADAM_LR * (m_hat / (jnp.sqrt(v_hat) + ADAM_EPS) + ADAM_WD * w), nm, nv


def _adamw(pieces, w, m, v, *, name):
    rows, cols = w.shape
    tr = _tile(rows, (RB1, RB0, SMALL_ROWS))

    def body(p_ref, w_ref, m_ref, v_ref, g_ref, d_ref, nm_ref, nv_ref):
        g = _sum_pieces(p_ref)
        g_ref[...] = g
        d_ref[...], nm_ref[...], nv_ref[...] = _adam_update(g, w_ref[...], m_ref[...], v_ref[...])

    spec = pl.BlockSpec((tr, cols), lambda i: (i, 0))
    shape = jax.ShapeDtypeStruct((rows, cols), F32)
    return pl.pallas_call(
        body, name=name, grid=(rows // tr,),
        in_specs=[pl.BlockSpec((N_DEV, tr, cols), lambda i: (0, i, 0)), spec, spec, spec],
        out_specs=[spec, spec, spec, spec], out_shape=[shape, shape, shape, shape],
        compiler_params=_params(("parallel",)),
    )(pieces, w, m, v)


def _sum8(pieces, rows, *, name):
    cols = pieces.shape[2]
    tr = _tile(rows, (176, 96))

    def body(p_ref, g_ref):
        g_ref[...] = _sum_pieces(p_ref)

    return pl.pallas_call(
        body, name=name, grid=(rows // tr,),
        in_specs=[pl.BlockSpec((N_DEV, tr, cols), lambda i: (0, i, 0))],
        out_specs=pl.BlockSpec((tr, cols), lambda i: (i, 0)),
        out_shape=jax.ShapeDtypeStruct((rows, cols), F32),
        compiler_params=_params(("parallel",)),
    )(pieces)


def _adamw_columns(g, w, m, v, *, name):
    n, _, k = w.shape
    tr = n // 2

    def body(g_ref, w_ref, m_ref, v_ref, d_ref, nm_ref, nv_ref):
        d_ref[...], nm_ref[...], nv_ref[...] = _adam_update(g_ref[...], w_ref[...], m_ref[...], v_ref[...])

    spec = pl.BlockSpec((tr, 1, k), lambda i: (i, 0, 0))
    shape = jax.ShapeDtypeStruct((n, 1, k), F32)
    return pl.pallas_call(
        body, name=name, grid=(n // tr,), in_specs=[spec, spec, spec, spec],
        out_specs=[spec, spec, spec], out_shape=[shape, shape, shape],
        compiler_params=_params(("parallel",)),
    )(g, w, m, v)


MESH = pl.DeviceIdType.MESH
ANY = pl.BlockSpec(memory_space=pl.ANY)


def _all_gather(shard, *, name):
    rows, lanes = shard.shape

    def body(x_ref, out_ref, send_sems, recv_sems, local_sem):
        x, y, c = lax.axis_index("x"), lax.axis_index("y"), lax.axis_index("c")
        me, sibling = (x, y, c), (x, y, 1 - c)
        chips = [(1 - x, y), (x, 1 - y), (1 - x, 1 - y)]

        def block(px, py, pc):
            return out_ref.at[4 * px + 2 * py + pc]

        def copy(k, blk, to, src=None):
            return pltpu.make_async_remote_copy(
                src_ref=block(*blk) if src is None else src, dst_ref=block(*blk),
                send_sem=send_sems.at[k], recv_sem=recv_sems.at[k], device_id=to, device_id_type=MESH)

        mine = pltpu.make_async_copy(x_ref, block(*me), local_sem)
        mine.start()
        first = [copy(0, me, sibling, src=x_ref)]
        first += [copy(1 + j, me, (*chip, c), src=x_ref) for j, chip in enumerate(chips)]
        for cp in first:
            cp.start()
        passed = [copy(4 + j, (*chip, c), sibling) for j, chip in enumerate(chips)]
        for j, chip in enumerate(chips):
            copy(1 + j, (*chip, c), me).wait_recv()
            passed[j].start()
        copy(0, sibling, me).wait_recv()
        for j, chip in enumerate(chips):
            copy(4 + j, (*chip, 1 - c), me).wait_recv()
        for cp in first + passed:
            cp.wait_send()
        mine.wait()

    return pl.pallas_call(
        body, name=name, out_shape=jax.ShapeDtypeStruct((N_DEV, rows, lanes), shard.dtype),
        in_specs=[ANY], out_specs=ANY,
        scratch_shapes=[pltpu.SemaphoreType.DMA((7,)), pltpu.SemaphoreType.DMA((7,)), pltpu.SemaphoreType.DMA(())],
    )(shard)


def _peer_copies(kind, src_ref, out_ref, send_sems, recv_sems, local_sem):
    x, y, c = lax.axis_index("x"), lax.axis_index("y"), lax.axis_index("c")
    me = 4 * x + 2 * y + c

    def src(idx):
        return src_ref.at[idx] if kind == "exchange" else src_ref

    mine = None if local_sem is None else pltpu.make_async_copy(src(me), out_ref.at[me], local_sem)
    copies = []
    for r in range(1, N_DEV):
        px = 1 - x if r & 4 else x
        py = 1 - y if r & 2 else y
        pc = 1 - c if r & 1 else c
        copies.append(pltpu.make_async_remote_copy(
            src_ref=src(4 * px + 2 * py + pc), dst_ref=out_ref.at[me],
            send_sem=send_sems.at[r - 1], recv_sem=recv_sems.at[r - 1],
            device_id=(px, py, pc), device_id_type=MESH))
    return mine, copies


PEER_SEMS = [pltpu.SemaphoreType.DMA((7,)), pltpu.SemaphoreType.DMA((7,)), pltpu.SemaphoreType.DMA(())]


HBM = pl.BlockSpec(memory_space=pltpu.HBM)
SEMAPHORES = pl.BlockSpec(memory_space=pltpu.SEMAPHORE)


def _peer_start(kind, arr, *, name):
    land = lax.empty((N_DEV,) + arr.shape[-2:], arr.dtype)

    def body(src_ref, land_ref, send_sems, recv_sems, src_thru, land_thru, token):
        _, copies = _peer_copies(kind, src_ref, land_ref, send_sems, recv_sems, None)
        for cp in copies:
            cp.start()
        token[...] = jnp.zeros_like(token)

    return pl.pallas_call(
        body, name=name,
        out_shape=(pltpu.SemaphoreType.DMA((N_DEV - 1,)), pltpu.SemaphoreType.DMA((N_DEV - 1,)),
                   pltpu.HBM(arr.shape, arr.dtype), pltpu.HBM(land.shape, land.dtype), jax.ShapeDtypeStruct((8, LANES), F32)),
        in_specs=(HBM, HBM), out_specs=(SEMAPHORES, SEMAPHORES, HBM, HBM, pl.BlockSpec(memory_space=pltpu.VMEM)),
        input_output_aliases={0: 2, 1: 3},
        compiler_params=pltpu.CompilerParams(has_side_effects=pltpu.SideEffectType.DATAFLOW_SIDE_EFFECTING),
    )(pltpu.with_memory_space_constraint(arr, pltpu.HBM), pltpu.with_memory_space_constraint(land, pltpu.HBM))


def _peer_wait(kind, send_sems, recv_sems, src_thru, land_thru, after, *, name):
    def body(src_ref, land_ref, send_sems, recv_sems, *_):
        _, copies = _peer_copies(kind, src_ref, land_ref, send_sems, recv_sems, None)
        for cp in copies:
            cp.wait_send()
            cp.wait_recv()

    return pl.pallas_call(
        body, name=name,
        out_shape=(pltpu.HBM(src_thru.shape, src_thru.dtype), pltpu.HBM(land_thru.shape, land_thru.dtype)),
        in_specs=(HBM, HBM, SEMAPHORES, SEMAPHORES) + (ANY,) * len(after), out_specs=(HBM, HBM),
        input_output_aliases={0: 0, 1: 1},
        compiler_params=pltpu.CompilerParams(has_side_effects=pltpu.SideEffectType.DATAFLOW_SIDE_EFFECTING),
    )(src_thru, land_thru, send_sems, recv_sems, *after)


def _add_rider(rider, in_specs, args, out_specs, out_shape):
    if rider is None:
        return []
    _, arr = rider
    in_specs.append(ANY)
    args.append(arr)
    out_specs.append(ANY)
    out_shape.append(jax.ShapeDtypeStruct((N_DEV,) + arr.shape[-2:], arr.dtype))
    return list(PEER_SEMS)


def _split_rider(refs, rider, n_in, n_out):
    if rider is None:
        return refs, None
    refs = list(refs)
    rin = refs.pop(n_in)
    rout = refs.pop(n_in + n_out)
    return refs[:-3], (rin, rout, *refs[-3:])


def _ride_start(rider, ride_refs, first):
    if rider is None:
        return

    @pl.when(first)
    def _():
        mine, copies = _peer_copies(rider[0], *ride_refs)
        mine.start()
        for cp in copies:
            cp.start()


def _ride_wait(rider, ride_refs, last):
    if rider is None:
        return

    @pl.when(last)
    def _():
        mine, copies = _peer_copies(rider[0], *ride_refs)
        for cp in copies:
            cp.wait()
        mine.wait()


def _gathered_cols(blocks, kdim):
    n = blocks.shape[1] * WIDE // kdim
    return blocks.reshape(N_DEV, kdim, n).transpose(1, 0, 2).reshape(kdim, N_DEV * n)


def _scatter_cols(dw):
    kdim, n8 = dw.shape
    n = n8 // N_DEV
    return dw.reshape(kdim, N_DEV, n).transpose(1, 0, 2).reshape(N_DEV, kdim * n // WIDE, WIDE)


def _pad_rows(a, rows):
    pad = [(0, 0)] * a.ndim
    pad[-2] = (0, rows - a.shape[-2])
    return jnp.pad(a, pad)


def _layer0_in_weight_t(wt):
    cq, ckv, kpe = wt[0:256], wt[256:384], wt[384:416]
    q_s, k_s, v_s, gate = wt[416:928], wt[928:1056], wt[1056:1184], wt[1184:2208]
    z = jnp.zeros((64, wt.shape[1]), wt.dtype)
    return jnp.concatenate([gate, cq, ckv, z, kpe, z[:32], q_s, k_s, v_s], axis=0)


def _layer0_in_grad_t(dwt):
    gate, cq, ckv, kpe = dwt[0:1024], dwt[1024:1280], dwt[1280:1408], dwt[1472:1504]
    q_s, k_s, v_s = dwt[1536:2048], dwt[2048:2176], dwt[2176:2304]
    return jnp.concatenate([cq, ckv, kpe, q_s, k_s, v_s, gate], axis=0)


def _layer1_in_weight_t(wt):
    main = jnp.concatenate([wt[:3 * D_MODEL], wt[3 * D_MODEL + FOX_HEADS:]], axis=0)
    return main, _pad_rows(wt[3 * D_MODEL:3 * D_MODEL + FOX_HEADS], LANES)


def _layer1_in_grad_t(d_blocks, d_wft):
    return jnp.concatenate([d_blocks[0], d_wft[:FOX_HEADS], d_blocks[1]], axis=0)


def _q_up_weight(w):
    return jnp.pad(w.reshape(MLA_Q_RANK, MLA_HEADS, 96), ((0, 0), (0, 0), (0, 32))).reshape(MLA_Q_RANK, MLA_HEADS * LANES)


def _q_up_grad(dwp):
    return dwp.reshape(MLA_Q_RANK, MLA_HEADS, LANES)[:, :, :96].reshape(MLA_Q_RANK, MLA_HEADS * 96)


def _kv_up_weight(w):
    w4 = w.reshape(MLA_KV_RANK, MLA_HEADS, 2, 64)
    kp = jnp.pad(w4[:, :, 0, :], ((0, 0), (0, 0), (0, 64))).reshape(MLA_KV_RANK, MLA_HEADS * LANES)
    vp = w4[:, :, 1, :].reshape(MLA_KV_RANK, MLA_HEADS * 64)
    return jnp.concatenate([kp, vp], axis=1)


def _kv_up_grad(dwp):
    dk = dwp[:, :MLA_HEADS * LANES].reshape(MLA_KV_RANK, MLA_HEADS, LANES)[:, :, :64]
    dv = dwp[:, MLA_HEADS * LANES:].reshape(MLA_KV_RANK, MLA_HEADS, 64)
    return jnp.stack([dk, dv], axis=2).reshape(MLA_KV_RANK, MLA_HEADS * LANES)


def _pad_lanes(a):
    return jnp.pad(a, ((0, 0), (0, LANES - a.shape[1])))


def _small_pack(g_in, g_final, g_q_a, g_kv_a, sinks, b_f, loss):
    rows = [g_in.reshape(8, LANES), g_final.reshape(8, LANES), g_q_a.reshape(2, LANES), g_kv_a.reshape(1, LANES),
            _pad_lanes(sinks.reshape(1, -1)), _pad_lanes(b_f.reshape(1, -1)), _pad_lanes(loss.reshape(1, 1)),
            jnp.zeros((2, LANES), F32)]
    return jnp.concatenate(rows, axis=0)


def _small_unpack(a):
    return (a[0:8].reshape(1, D_MODEL), a[8:16].reshape(D_MODEL), a[16:18].reshape(1, MLA_Q_RANK),
            a[18:19].reshape(1, MLA_KV_RANK), a[19:20, :SWA_HEADS], a[20:21, :FOX_HEADS], a[21, 0])


def _local_step(x, positions, target, e_g_in, early, e_g_q_a, e_g_kv_a, e_sinks,
                late, o_b_f, g_final, scatter1=None, scatter0=None):
    s = x.shape[0]
    mla_scale = (MLA_NOPE + MLA_ROPE) ** -0.5
    fox_scale = FOX_DIM ** -0.5
    n0a = Z0A_UNITS * LANES

    inv_freq = 1.0 / (ROPE_THETA ** (jnp.arange(0, MLA_ROPE, 2, dtype=F32) / MLA_ROPE))
    ang = positions.astype(F32)[:, None] * inv_freq
    cos, sin = jnp.cos(ang), jnp.sin(ang)
    ones, zeros = jnp.ones((s, 64), F32), jnp.zeros((s, 64), F32)
    cos_t = jnp.concatenate([ones, cos, cos, ones[:, :32]], axis=1)
    sin_t = jnp.concatenate([zeros, -sin, sin, zeros[:, :32]], axis=1)

    if len(early) == 3:
        h0 = _rmsnorm_fwd(x, e_g_in, width=D_MODEL, col_blk=0, name="l0_norm")
        w0t, wq, wkv = early
    else:
        pending, token, unpack, prep = early
        h0 = _rmsnorm_fwd(x, e_g_in, width=D_MODEL, col_blk=0, name="l0_norm", after=[token])
        w0t, wq, wkv = unpack(*_peer_wait("gather", *pending, after=[h0] + prep, name="weights0_wait"))
    z0a, z0b = _matmul_rows([(h0, w0t, True)], [], [], lambda r: (r[:, :n0a], r[:, n0a:]),
                            [("rows", n0a, F32), ("rows", Z0B_UNITS * LANES, BF16)], name="l0_in")
    cqn = _rmsnorm_fwd(z0a, e_g_q_a, width=MLA_Q_RANK, col_blk=4, name="l0_q_norm")
    ckvn = _rmsnorm_fwd(z0a, e_g_kv_a, width=MLA_KV_RANK, col_blk=10, name="l0_kv_norm")
    rope_rows = [(cos_t, LANES, 0), (sin_t, LANES, 0)]
    qm, = _matmul_rows([(cqn, wq, False)], rope_rows, [], _rope_q_epilogue, [("rows", MLA_HEADS * LANES, BF16)],
                       name="l0_q_up")
    kvm, = _matmul_rows([(ckvn, wkv, False)], [(z0a, LANES, 11)] + rope_rows, [], _rope_k_epilogue,
                        [("rows", MLA_HEADS * (LANES + MLA_V), BF16)], name="l0_kv_up")
    gathers = len(late) == 2
    res = _flash_fwd(qm, kvm, kvm, None, n_pairs=MLA_HEADS // 2, hw=LANES, q_off=0, k_off=0, v_off=MLA_HEADS,
                     scale=mla_scale, name="l0_mla_fwd", rider=("gather", late[0]) if gathers else None)
    o_mla, lse_mla = res[0], res[1]
    wo0, o_g_in, w1t, wft, wo1 = late[1](res[2]) if gathers else late
    o_swa, lse_swa = _swa_fwd(z0b, e_sinks, name="l0_swa_fwd")
    half = D_MODEL // 2

    x1, h1, og0 = _matmul_rows(
        [(None, wo0, False)], [(o_mla, half, 0), (o_swa, half, 0), (z0a, D_MODEL, 0), (x, D_MODEL, 0)], [o_g_in],
        lambda r, om, osw, gt, xt, g, made: (*_residual_norm_epilogue(r, xt, g), made),
        [("rows", D_MODEL, F32), ("rows", D_MODEL, BF16), ("rows", D_MODEL, BF16)], name="l0_out",
        prologue=lambda om, osw, gt, xt, g: _gated([om, osw], gt))
    z1, gate1, zf = _matmul_rows(
        [(None, w1t, True), (None, wft, True)], [(h1, D_MODEL, 0)], [],
        lambda r, h, made: (r[0][:, :3 * D_MODEL], r[0][:, 3 * D_MODEL:], r[1]),
        [("rows", 3 * D_MODEL, BF16), ("rows", D_MODEL, F32), ("rows", LANES, F32)], name="l1_in",
        prologue=lambda h: h, separate=True)
    bf = _pad_lanes(o_b_f)
    log_cum = _logf_fwd(zf, bf, name="l1_logf")
    bias2 = (-LOG2E * log_cum[:, :FOX_HEADS]).T
    t_bwd = min(ATT_T, s)
    bias = bias2.reshape(FOX_HEADS // 2, 2, s // t_bwd, 1, t_bwd)
    t_fwd = _fwd_tile(s)
    o_fox, lse_fox = _flash_fwd(z1, z1, z1, bias2.reshape(FOX_HEADS // 2, 2, s // t_fwd, 1, t_fwd),
                                n_pairs=FOX_HEADS // 2, hw=64, q_off=0, k_off=8, v_off=16, scale=fox_scale,
                                name="l1_fox_fwd")

    dx2, loss_part, d_g_final, og1, dx2_bf = _matmul_rows(
        [(None, wo1, False)], [(o_fox, D_MODEL, 0), (gate1, D_MODEL, 0), (x1, D_MODEL, 0), (target, D_MODEL, 0)],
        [g_final.reshape(1, D_MODEL)],
        lambda r, o, gt, xt, tg, g, made: _and_first(_loss_epilogue(r, xt, tg, g), made),
        [("rows", D_MODEL, F32), ("sum", (8, LANES)), ("sum", (1, D_MODEL)), ("rows", D_MODEL, BF16),
         ("rows", D_MODEL, BF16)], name="l1_out_loss", prologue=lambda o, gt, xt, tg, g: _gated([o], gt))

    d_wo1 = _matmul(og1, dx2_bf, ta=True, out_dtype=BF16, name="l1_out_dw")
    do_fox, d_gate1 = _matmul_rows([(dx2_bf, wo1, True)], [(o_fox, D_MODEL, 0), (gate1, D_MODEL, 0)], [],
                                   _gate_bwd_epilogue([D_MODEL]), [("rows", D_MODEL, F32), ("rows", D_MODEL, BF16)],
                                   name="l1_out_dx")
    dqkv1, dbias, drow = _flash_bwd(z1, z1, z1, do_fox, o_fox, lse_fox, bias, n_pairs=FOX_HEADS // 2, hw=64, q_off=0,
                                    k_off=8, v_off=16, scale=fox_scale, qk_dtype=BF16, stacked=True, name="l1_fox_bwd")
    d_log_cum = (drow.reshape(FOX_HEADS, s) - dbias.reshape(FOX_HEADS, s)).T
    d_log_cum = jnp.pad(d_log_cum, ((0, 0), (0, LANES - FOX_HEADS)))
    d_zf, d_bf = _logf_bwd(d_log_cum, zf, bf, name="l1_logf_bwd")
    d_w1t = (_matmul(dqkv1, h1, ta=True, out_dtype=BF16, name="l1_in_dw_qkv"),
             _matmul(d_gate1, h1, ta=True, out_dtype=BF16, name="l1_in_dw_gate"))
    d_wft = _matmul(d_zf, h1, ta=True, out_dtype=BF16, name="l1_in_f_dw")
    dx1, d_o_g_in, dx1_bf = _matmul_rows([(dqkv1, w1t, False, c * D_MODEL, c) for c in range(3)]
                                         + [(d_gate1, w1t, False, 3 * D_MODEL), (d_zf, wft, False)],
                                         [(x1, D_MODEL, 0), (dx2, D_MODEL, 0)], [o_g_in],
                                         lambda *a: _and_first(_rms_bwd_epilogue(*a)),
                                         [("rows", D_MODEL, F32), ("sum", (1, D_MODEL)), ("rows", D_MODEL, BF16)],
                                         name="l1_in_dx")

    d_wo0 = _matmul(og0, dx1_bf, ta=True, out_dtype=BF16, name="l0_out_dw")
    do_mla, do_swa, d_gate0 = _matmul_rows(
        [(dx1_bf, wo0, True)], [(o_mla, half, 0), (o_swa, half, 0), (z0a, D_MODEL, 0)], [], _gate_bwd_epilogue([half, half]),
        [("rows", half, F32), ("rows", half, F32), ("rows", D_MODEL, BF16)], name="l0_out_dx")
    dq_s, dkt_s, dvt_s, d_sinks = _swa_bwd(z0b, e_sinks, do_swa, o_swa, lse_swa, name="l0_swa_bwd")
    dk_s = dkt_s.transpose(0, 2, 1).reshape(s, LANES)
    dv_s = dvt_s.transpose(0, 2, 1).reshape(s, LANES)
    rider = None
    if scatter1 is not None:
        rider = ("exchange", scatter1(dict(w1t=d_w1t, wft=d_wft, wo1=d_wo1, o_g_in=d_o_g_in, wo0=d_wo0)))
    res = _flash_bwd(qm, kvm, kvm, do_mla, o_mla, lse_mla, None, n_pairs=MLA_HEADS // 2, hw=LANES, q_off=0, k_off=0,
                     v_off=MLA_HEADS, scale=mla_scale, qk_dtype=F32, name="l0_mla_bwd", rider=rider)
    dqm, dkm, dvm = res[0], res[1], res[2]
    recv1 = res[3] if rider is not None else None
    d_qp, d_kvp, d_kpe = _rope_bwd(dqm, dkm, dvm, cos_t, sin_t, name="l0_rope_bwd")
    d_wq = _matmul(cqn, d_qp, ta=True, out_dtype=BF16, name="l0_q_up_dw")
    d_cqn = _matmul(d_qp, wq, tb=True, name="l0_q_up_dx")
    d_wkv = _matmul(ckvn, d_kvp, ta=True, out_dtype=BF16, name="l0_kv_up_dw")
    d_ckvn = _matmul(d_kvp, wkv, tb=True, name="l0_kv_up_dx")
    d_cq, d_g_q_a = _rmsnorm_bwd(z0a, e_g_q_a, d_cqn, width=MLA_Q_RANK, col_blk=4, name="l0_q_norm_bwd")
    d_ckv, d_g_kv_a = _rmsnorm_bwd(z0a, e_g_kv_a, d_ckvn, width=MLA_KV_RANK, col_blk=10, name="l0_kv_norm_bwd")
    dz0 = jnp.concatenate([d_gate0, d_cq, d_ckv, d_kpe, dq_s.astype(BF16), dk_s.astype(BF16), dv_s.astype(BF16)], axis=1)
    d_w0t = _matmul(dz0, h0, ta=True, out_dtype=BF16, name="l0_in_dw")
    pending0, after_start = None, []
    if scatter0 is not None:
        *pending0, token = _peer_start("exchange", scatter0(dict(w0t=d_w0t, wq=d_wq, wkv=d_wkv)), name="grads0_start")
        after_start = [token]
    grad_x, d_e_g_in = _matmul_rows(
        [(dz0, w0t, False)], [(x, D_MODEL, 0), (dx1, D_MODEL, 0)], [e_g_in] + after_start,
        lambda dy, xt, add, g, *_: _rms_bwd_epilogue(dy, xt, add, g),
        [("rows", D_MODEL, F32), ("sum", (1, D_MODEL))], name="l0_in_dx")

    return dict(pending0=pending0, recv1=recv1, loss=loss_part[0, 0], grad_x=grad_x, e_g_in=d_e_g_in, w0t=d_w0t, e_g_q_a=d_g_q_a, wq=d_wq,
                e_g_kv_a=d_g_kv_a, wkv=d_wkv, e_sinks=d_sinks[:, 0].reshape(1, SWA_HEADS), wo0=d_wo0,
                o_g_in=d_o_g_in, w1t=d_w1t, wft=d_wft, o_b_f=d_bf[:, :FOX_HEADS], wo1=d_wo1, g_final=d_g_final.reshape(D_MODEL))


def _wide(a, rows):
    flat = a.reshape(-1)
    return jnp.pad(flat, (0, rows * WIDE - flat.shape[0])).reshape(rows, WIDE)


def _rows_b0(w_q, w_kv):
    return jnp.concatenate([_wide(w_q, 32), _wide(w_kv, 16)], axis=0)


def _unflat_b0(f):
    return f[0:24].reshape(1, MLA_Q_RANK, 96), f[32:48].reshape(1, MLA_KV_RANK, 128)


def _rows_b1(o_w_out, e_w_out, g_in):
    return jnp.concatenate([o_w_out, e_w_out, _wide(g_in, 16)], axis=0)


def _unflat_b1(f):
    return f[0:128][None], f[128:256][None], f[256:257, :LANES]


def kernel(x, positions, e_g_in, e_w_in, e_g_q_a, e_w_q_up, e_g_kv_a, e_w_kv_up, e_sinks, e_w_out, o_g_in, o_w_in, o_b_f, o_w_out, g_final, loss_target, m_e_g_in, m_e_w_in, m_e_g_q_a, m_e_w_q_up, m_e_g_kv_a, m_e_w_kv_up, m_e_sinks, m_e_w_out, m_o_g_in, m_o_w_in, m_o_b_f, m_o_w_out, m_g_final, v_e_g_in, v_e_w_in, v_e_g_q_a, v_e_w_q_up, v_e_g_kv_a, v_e_w_kv_up, v_e_sinks, v_e_w_out, v_o_g_in, v_o_w_in, v_o_b_f, v_o_w_out, v_g_final):
    def bf(a):
        return a.astype(BF16)

    me = 4 * lax.axis_index("x") + 2 * lax.axis_index("y") + lax.axis_index("c")
    shard0 = jnp.concatenate([_pad_rows(bf(e_w_in[0]).T, RA0), _rows_b0(bf(e_w_q_up[0]), bf(e_w_kv_up[0]))], axis=0)
    *pending_w0, token_w0 = _peer_start("gather", shard0, name="weights0_start")

    def unpack0(sent, gath0):
        gath0 = lax.dynamic_update_slice_in_dim(gath0, sent[None], me, axis=0)
        w0t = _layer0_in_weight_t(gath0[:, :N_E_IN].reshape(N_DEV * N_E_IN, WIDE))
        wq = _q_up_weight(_gathered_cols(gath0[:, RA0:RA0 + 24], MLA_Q_RANK))
        wkv = _kv_up_weight(_gathered_cols(gath0[:, RA0 + 32:RA0 + 48], MLA_KV_RANK))
        return w0t, wq, wkv

    rows_b0 = [_rows_b0(q[0], kv[0]) for q, kv in ((e_w_q_up, e_w_kv_up), (m_e_w_q_up, m_e_w_kv_up), (v_e_w_q_up, v_e_w_kv_up))]
    rows_b1 = [_rows_b1(o[0], e[0], g) for o, e, g in ((o_w_out, e_w_out, o_g_in), (m_o_w_out, m_e_w_out, m_o_g_in),
                                                       (v_o_w_out, v_e_w_out, v_o_g_in))]

    g_bits = lax.bitcast_convert_type(o_g_in.reshape(LANES), BF16)
    shard1 = jnp.concatenate([_pad_rows(bf(o_w_in[0]).T, RA1), _rows_b1(bf(o_w_out[0]), bf(e_w_out[0]), g_bits)], axis=0)

    def unpack1(gath1):
        w1t, wft = _layer1_in_weight_t(gath1[:, :N_O_IN].reshape(N_DEV * N_O_IN, WIDE))
        wo1 = gath1[:, RA1:RA1 + 128].reshape(D_MODEL, D_MODEL)
        wo0 = gath1[:, RA1 + 128:RA1 + 256].reshape(D_MODEL, D_MODEL)
        bits = gath1[:, RA1 + 256, :2 * LANES].reshape(N_DEV, LANES, 2)
        return wo0, lax.bitcast_convert_type(bits, F32).reshape(1, D_MODEL), w1t, wft, wo1

    def scatter1(g):
        d_in_t = _layer1_in_grad_t(g["w1t"], g["wft"]).reshape(N_DEV, N_O_IN, WIDE)
        d_o_g = jnp.pad(bf(g["o_g_in"]).reshape(N_DEV, 1, LANES), ((0, 0), (0, 15), (0, WIDE - LANES)))
        return jnp.concatenate([_pad_rows(d_in_t, RA1), g["wo1"].reshape(N_DEV, 128, WIDE),
                                g["wo0"].reshape(N_DEV, 128, WIDE), d_o_g], axis=1)

    def scatter0(g):
        return jnp.concatenate([
            _pad_rows(_layer0_in_grad_t(g["w0t"]).reshape(N_DEV, N_E_IN, WIDE), RA0),
            _pad_rows(_scatter_cols(_q_up_grad(g["wq"])), 32), _scatter_cols(_kv_up_grad(g["wkv"]))], axis=1)

    gr = _local_step(x[0], positions[0], loss_target[0], e_g_in,
                     (pending_w0, token_w0, unpack0, [shard1] + rows_b0 + rows_b1), e_g_q_a, e_g_kv_a, e_sinks,
                     (shard1, unpack1), o_b_f, g_final, scatter1=scatter1, scatter0=scatter0)

    def in_projection(recv, ra, n, w, m, v, name):
        g = _sum8(recv, ra, name=name + "_grad_sum")[:n].reshape(n, 1, D_MODEL)
        w, m, v = [jnp.transpose(a, (2, 0, 1)) for a in (w, m, v)]
        return (g, *_adamw_columns(g, w, m, v, name=name + "_adamw"))

    o_in = in_projection(gr["recv1"], RA1, N_O_IN, o_w_in, m_o_w_in, v_o_w_in, "o_w_in")
    b1 = _adamw(gr["recv1"][:, RA1:], *rows_b1, name="adamw_late")

    small = _small_pack(gr["e_g_in"], gr["g_final"], gr["e_g_q_a"], gr["e_g_kv_a"], gr["e_sinks"], gr["o_b_f"], gr["loss"])
    small_all = _all_gather(small, name="small_all_gather")
    zero = jnp.zeros((), F32)
    w_small = _small_pack(e_g_in, g_final, e_g_q_a, e_g_kv_a, e_sinks, o_b_f, zero)
    m_small = _small_pack(m_e_g_in, m_g_final, m_e_g_q_a, m_e_g_kv_a, m_e_sinks, m_o_b_f, zero)
    v_small = _small_pack(v_e_g_in, v_g_final, v_e_g_q_a, v_e_g_kv_a, v_e_sinks, v_o_b_f, zero)
    smalls = _adamw(small_all, w_small, m_small, v_small, name="adamw_replicated")
    g_sm, d_sm, m_sm, v_sm = [_small_unpack(a) for a in smalls]
    loss = g_sm[6]

    sent0, recv0 = _peer_wait("exchange", *gr["pending0"], after=[o_in[1], b1[1], smalls[1]], name="grads0_wait")
    own = lax.dynamic_slice_in_dim(sent0, me, 1, axis=0)
    recv0 = lax.dynamic_update_slice_in_dim(recv0, own, me, axis=0)
    e_in = in_projection(recv0, RA0, N_E_IN, e_w_in, m_e_w_in, v_e_w_in, "e_w_in")
    b0 = _adamw(recv0[:, RA0:], *rows_b0, name="adamw_early")

    def sharded(k):
        q_up, kv_up = _unflat_b0(b0[k])
        o_out, e_out, o_g = _unflat_b1(b1[k])
        return jnp.transpose(e_in[k], (1, 2, 0)), q_up, kv_up, e_out, jnp.transpose(o_in[k], (1, 2, 0)), o_out, o_g

    g_sh, d_sh, m_sh, v_sh = [sharded(k) for k in range(4)]

    def leaves(sh, sm):
        return (sm[0], sh[0], sm[2], sh[1], sm[3], sh[2], sm[4], sh[3], sh[6], sh[4], sm[5], sh[5], sm[1])

    return (loss, gr["grad_x"][None], *leaves(g_sh, g_sm), *leaves(d_sh, d_sm), *leaves(m_sh, m_sm), *leaves(v_sh, v_sm))
```

```python
import functools

import jax
import jax.numpy as jnp
from jax import lax
from jax.experimental import pallas as pl
from jax.experimental.pallas import tpu as pltpu

F32 = jnp.float32
BF16 = jnp.bfloat16
NEG_INF = float("-inf")

N_DEV = 8
LANES = 128
D_MODEL = 1024
EPS = 1e-6
ROPE_THETA = 10000.0
MLA_HEADS = 8
MLA_Q_RANK = 256
MLA_KV_RANK = 128
MLA_NOPE = 64
MLA_ROPE = 32
MLA_V = 64
SWA_HEADS = 8
SWA_KV_HEADS = 2
SWA_DIM = 64
WINDOW = 128
FOX_HEADS = 16
FOX_DIM = 64

ADAM_LR = 0.001
ADAM_B1 = 0.9
ADAM_B2 = 0.999
ADAM_EPS = 1e-08
ADAM_WD = 0.01
ADAM_STEP = 10

ATT_T = 512
ATT_T_FWD = 1024
VMEM_LIMIT = 56 * 1024 * 1024
MATMUL_B_BLOCK_BYTES = 8 * 1024 * 1024

Z0A_UNITS = 12
Z0B_UNITS = 6

WIDE = 1024
N_E_IN = 276
N_O_IN = 514
RA0 = 288
RB0 = 32 + 16
RA1 = 528
RB1 = 128 + 128 + 16
SMALL_ROWS = 24


def _tile(n, cands):
    for c in cands:
        if n % c == 0:
            return c
    raise ValueError(f"no tile for {n}")


ROW_TILES = (512, 256, 128)


def _params(sem, vmem=VMEM_LIMIT):
    return pltpu.CompilerParams(dimension_semantics=sem, vmem_limit_bytes=vmem)


def _matmul(a, b, *, name, ta=False, tb=False, out_dtype=F32, b_rows=None):
    if ta:
        kdim, m = a.shape[-2], a.shape[-1] * (a.shape[0] if a.ndim == 3 else 1)
    else:
        m, kdim = a.shape
    if tb:
        n, kb = b.shape
    else:
        kb, n = b.shape
    assert kdim == kb, (a.shape, b.shape)
    b_start = 0
    if b_rows is not None:
        assert tb
        b_start, n = b_rows
    tm = _tile(m, (512, 256, 128))
    tn = _tile(n, [c for c in (1024, 768, 512, 384, 256, 128)
                   if c * kdim * b.dtype.itemsize <= MATMUL_B_BLOCK_BYTES and b_start % c == 0])
    assert b_start % tn == 0, (b_start, tn)
    b_off = b_start // tn
    dims = (((0 if ta else 1,), (1 if tb else 0,)), ((), ()))

    def body(a_ref, b_ref, o_ref):
        r = lax.dot_general(a_ref[...].astype(BF16), b_ref[...].astype(BF16), dims, preferred_element_type=F32)
        o_ref[...] = r.astype(out_dtype)

    if a.ndim == 3:
        per = a.shape[2] // tm
        a_spec = pl.BlockSpec((None, kdim, tm), lambda i, j: (i // per, 0, i % per))
    else:
        a_spec = pl.BlockSpec((kdim, tm), lambda i, j: (0, i)) if ta else pl.BlockSpec((tm, kdim), lambda i, j: (i, 0))
    b_spec = pl.BlockSpec((tn, kdim), lambda i, j: (j + b_off, 0)) if tb else pl.BlockSpec((kdim, tn), lambda i, j: (0, j))
    return pl.pallas_call(
        body, name=name, grid=(m // tm, n // tn), in_specs=[a_spec, b_spec],
        out_specs=pl.BlockSpec((tm, tn), lambda i, j: (i, j)), out_shape=jax.ShapeDtypeStruct((m, n), out_dtype),
        compiler_params=_params(("parallel", "parallel")),
    )(a, b)


def _rmsnorm_fwd(x, g, *, width, col_blk, name, after=()):
    s = x.shape[0]
    tm = _tile(s, ROW_TILES)

    def body(x_ref, g_ref, *rest):
        y_ref = rest[-1]
        xf = x_ref[...].astype(F32)
        r = lax.rsqrt(jnp.mean(xf * xf, axis=-1, keepdims=True) + EPS)
        y_ref[...] = ((xf * r) * g_ref[...]).astype(BF16)

    return pl.pallas_call(
        body, name=name, grid=(s // tm,),
        in_specs=[pl.BlockSpec((tm, width), lambda i: (i, col_blk)), pl.BlockSpec((1, width), lambda i: (0, 0))]
        + [ANY] * len(after),
        out_specs=pl.BlockSpec((tm, width), lambda i: (i, 0)),
        out_shape=jax.ShapeDtypeStruct((s, width), BF16),
        compiler_params=_params(("parallel",)),
    )(x, g, *after)


def _rmsnorm_bwd(x, g, dy, *, width, col_blk, name):
    s = x.shape[0]
    tm = _tile(s, ROW_TILES)

    def body(x_ref, g_ref, dy_ref, dx_ref, dg_ref):
        @pl.when(pl.program_id(0) == 0)
        def _():
            dg_ref[...] = jnp.zeros_like(dg_ref)

        dx, dg = _rms_bwd_epilogue(dy_ref[...], x_ref[...], 0.0, g_ref[...])
        dg_ref[...] += dg
        dx_ref[...] = dx.astype(BF16)

    return pl.pallas_call(
        body, name=name, grid=(s // tm,),
        in_specs=[pl.BlockSpec((tm, width), lambda i: (i, col_blk)), pl.BlockSpec((1, width), lambda i: (0, 0)),
                  pl.BlockSpec((tm, width), lambda i: (i, 0))],
        out_specs=[pl.BlockSpec((tm, width), lambda i: (i, 0)), pl.BlockSpec((1, width), lambda i: (0, 0))],
        out_shape=[jax.ShapeDtypeStruct((s, width), BF16), jax.ShapeDtypeStruct((1, width), F32)],
        compiler_params=_params(("arbitrary",)),
    )(x, g, dy)


def _sigmoid(x):
    return 1.0 / (1.0 + jnp.exp(-x))


def _matmul_rows(terms, row_inputs, params, epilogue, outs, *, name, prologue=None, separate=False):
    s = row_inputs[0][0].shape[0] if row_inputs else terms[0][0].shape[-2]
    tm = _tile(s, ROW_TILES)
    steps = s // tm
    n_r, n_p, n_o = len(row_inputs), len(params), len(outs)
    n_t = sum(1 if term[0] is None else 2 for term in terms)

    def body(*refs):
        t_refs, r_refs = list(refs[:n_t]), refs[n_t:n_t + n_r]
        p_refs, o_refs = refs[n_t + n_r:n_t + n_r + n_p], refs[n_t + n_r + n_p:]
        i = pl.program_id(0)
        rows, small = [r[...] for r in r_refs], [p[...] for p in p_refs]
        made = None if prologue is None else prologue(*rows, *small)
        parts = []
        for term in terms:
            a = made if term[0] is None else t_refs.pop(0)[...].astype(BF16)
            dims = (((1,), (1 if term[2] else 0,)), ((), ()))
            parts.append(lax.dot_general(a, t_refs.pop(0)[...].astype(BF16), dims, preferred_element_type=F32))
        acc = parts if separate else sum(parts[1:], parts[0])
        vals = epilogue(acc, *rows, *small) if prologue is None else epilogue(acc, *rows, *small, made)
        for ref, val, out in zip(o_refs, vals, outs):
            if out[0] == "rows":
                ref[...] = val.astype(ref.dtype)
            else:
                @pl.when(i == 0)
                def _(ref=ref):
                    ref[...] = jnp.zeros_like(ref)

                ref[...] += val

    in_specs, args = [], []
    for term in terms:
        a, b = term[0], term[1]
        if a is None:
            in_specs.append(_resident(b.shape, lambda i: (0, 0)))
            args.append(b)
            continue
        b_rows = b.shape[0] if term[2] or len(term) < 4 else a.shape[-1]
        b_blk = 0 if len(term) < 4 else term[3] // b_rows
        if len(term) == 5:
            a_spec = pl.BlockSpec((None, tm, a.shape[2]), lambda i, c=term[4]: (c, i, 0))
        else:
            a_spec = pl.BlockSpec((tm, a.shape[1]), lambda i: (i, 0))
        in_specs += [a_spec, _resident((b_rows, b.shape[1]), lambda i, b_blk=b_blk: (b_blk, 0))]
        args += [a, b]
    for arr, width, col_blk in row_inputs:
        in_specs.append(pl.BlockSpec((tm, width), lambda i, col_blk=col_blk: (i, col_blk)))
        args.append(arr)
    for p in params:
        in_specs.append(pl.BlockSpec(p.shape, lambda i: (0, 0)))
        args.append(p)
    out_specs, out_shape = [], []
    for out in outs:
        if out[0] == "rows":
            out_specs.append(pl.BlockSpec((tm, out[1]), lambda i: (i, 0)))
            out_shape.append(jax.ShapeDtypeStruct((s, out[1]), out[2]))
        else:
            out_specs.append(pl.BlockSpec(out[1], lambda i: (0, 0)))
            out_shape.append(jax.ShapeDtypeStruct(out[1], F32))
    return pl.pallas_call(
        body, name=name, grid=(steps,), in_specs=in_specs, out_specs=out_specs, out_shape=out_shape,
        compiler_params=_params(("arbitrary",)),
    )(*args)


def _rms_stats(x):
    r = lax.rsqrt(jnp.mean(x * x, axis=-1, keepdims=True) + EPS)
    return r, x * r


def _gated(o_parts, gate):
    o = o_parts[0] if len(o_parts) == 1 else jnp.concatenate(o_parts, axis=1)
    return (o * (gate * _sigmoid(gate))).astype(BF16)


def _and_first(vals, *more):
    return (*vals, *more, vals[0])


def _residual_norm_epilogue(r, x, g):
    x1 = x + r
    _, xh = _rms_stats(x1)
    return x1, xh * g


def _rms_bwd_epilogue(dy, x, add, g):
    r, xh = _rms_stats(x)
    dxh = dy * g
    dx = r * (dxh - xh * jnp.mean(dxh * xh, axis=-1, keepdims=True)) + add
    return dx, jnp.sum(dy * xh, axis=0, keepdims=True)


def _loss_epilogue(r, x1, target, g):
    rs, xh = _rms_stats(x1 + r)
    err = xh * g - target
    loss = jnp.broadcast_to(0.5 * jnp.sum(jnp.mean(err * err, axis=-1, keepdims=True)), (8, LANES))
    dy = err * (1.0 / D_MODEL)
    dxh = dy * g
    dx = rs * (dxh - xh * jnp.mean(dxh * xh, axis=-1, keepdims=True))
    return dx, loss, jnp.sum(dy * xh, axis=0, keepdims=True)


def _gate_bwd_epilogue(widths):
    def epilogue(d, *rows):
        o_parts, gt = rows[:-1], rows[-1]
        o = o_parts[0] if len(o_parts) == 1 else jnp.concatenate(o_parts, axis=1)
        sg = _sigmoid(gt)
        do = d * (gt * sg)
        d_gate = d * o * (sg * (1.0 + gt * (1.0 - sg)))
        cuts = [sum(widths[:k]) for k in range(len(widths) + 1)]
        return tuple(do[:, cuts[k]:cuts[k + 1]] for k in range(len(widths))) + (d_gate,)

    return epilogue


def _rot_half(x):
    lane = lax.broadcasted_iota(jnp.int32, x.shape, 1)
    return jnp.where(lane < 80, pltpu.roll(x, LANES - 16, axis=1), pltpu.roll(x, 16, axis=1))


def _rot_half_t(g):
    lane = lax.broadcasted_iota(jnp.int32, g.shape, 1)
    lo = (lane >= MLA_NOPE) & (lane < MLA_NOPE + MLA_ROPE // 2)
    hi = (lane >= MLA_NOPE + MLA_ROPE // 2) & (lane < MLA_NOPE + MLA_ROPE)
    return jnp.where(lo, pltpu.roll(g, LANES - 16, axis=1), jnp.where(hi, pltpu.roll(g, 16, axis=1), 0.0))


def _rope_q_epilogue(q, c, sn):
    heads = [q[:, h * LANES:(h + 1) * LANES] for h in range(MLA_HEADS)]
    return (jnp.concatenate([qh * c + _rot_half(qh) * sn for qh in heads], axis=1),)


def _rope_k_epilogue(kv, kpe, c, sn):
    kpe_r = kpe * c + _rot_half(kpe) * sn
    lane = lax.broadcasted_iota(jnp.int32, kpe.shape, 1)
    heads = [jnp.where(lane < MLA_NOPE, kv[:, h * LANES:(h + 1) * LANES], kpe_r) for h in range(MLA_HEADS)]
    return (jnp.concatenate(heads + [kv[:, MLA_HEADS * LANES:]], axis=1),)


def _rope_bwd(dqm, dkm, dvm, cos_t, sin_t, *, name):
    s = dqm.shape[0]
    tm = _tile(s, ROW_TILES)
    hw = MLA_HEADS * LANES
    vw = MLA_HEADS * MLA_V

    def body(dq_ref, dk_ref, dv_ref, c_ref, s_ref, dqp_ref, dkv_ref, dkpe_ref):
        c = c_ref[...]
        sn = s_ref[...]
        ksum = jnp.zeros((tm, LANES), F32)
        for h in range(MLA_HEADS):
            sl = slice(h * LANES, (h + 1) * LANES)
            dq = dq_ref[:, sl]
            dqp_ref[:, sl] = (dq * c + _rot_half_t(dq * sn)).astype(BF16)
            dk = dk_ref[:, sl]
            dkv_ref[:, sl] = dk.astype(BF16)
            ksum = ksum + dk
        dkv_ref[:, hw:] = dv_ref[...]
        lane = lax.broadcasted_iota(jnp.int32, ksum.shape, 1)
        dkpe = ksum * c + _rot_half_t(ksum * sn)
        dkpe_ref[...] = jnp.where((lane >= MLA_NOPE) & (lane < MLA_NOPE + MLA_ROPE), dkpe, 0.0).astype(BF16)

    return pl.pallas_call(
        body, name=name, grid=(s // tm,),
        in_specs=[pl.BlockSpec((tm, hw), lambda i: (i, 0)), pl.BlockSpec((tm, hw), lambda i: (i, 0)),
                  pl.BlockSpec((tm, vw), lambda i: (i, 0)),
                  pl.BlockSpec((tm, LANES), lambda i: (i, 0)), pl.BlockSpec((tm, LANES), lambda i: (i, 0))],
        out_specs=[pl.BlockSpec((tm, hw), lambda i: (i, 0)), pl.BlockSpec((tm, hw + vw), lambda i: (i, 0)),
                   pl.BlockSpec((tm, LANES), lambda i: (i, 0))],
        out_shape=[jax.ShapeDtypeStruct((s, hw), BF16), jax.ShapeDtypeStruct((s, hw + vw), BF16),
                   jax.ShapeDtypeStruct((s, LANES), BF16)],
        compiler_params=_params(("parallel",)),
    )(dqm, dkm, dvm, cos_t, sin_t)


def _head_mask(shape, a):
    lane = lax.broadcasted_iota(jnp.int32, shape, 1)
    return (lane >= 64 * a) & (lane < 64 * (a + 1))


_NT = (((1,), (1,)), ((), ()))
LOG2E = 1.4426950408889634


def _stack_heads(tile, hw):
    lane = lax.broadcasted_iota(jnp.int32, tile.shape, 1)
    z = jnp.zeros_like(tile)
    return jnp.concatenate([jnp.where(lane < hw, tile, z), jnp.where(lane >= hw, tile, z)], axis=0)


def _stacked_rows(r0, r1, t):
    n = r0.shape[-1]
    return jnp.concatenate([jnp.broadcast_to(r0, (t, n)), jnp.broadcast_to(r1, (t, n))], axis=0)


def _resident(block, index_map):
    return pl.BlockSpec(block, index_map, pipeline_mode=pl.Buffered(1))


def _fwd_tile(s):
    return ATT_T_FWD if s % ATT_T_FWD == 0 else min(ATT_T, s)


def _flash_fwd(q, k, v, bias, *, n_pairs, hw, q_off, k_off, v_off, scale, name, rider=None):
    s = q.shape[0]
    t = _fwd_tile(s)
    nb = s // t
    qw = 2 * hw
    has_bias = bias is not None
    c1 = scale * LOG2E

    def body(*refs):
        refs, ride_refs = _split_rider(refs, rider, n_in=4 if has_bias else 3, n_out=2)
        if has_bias:
            q_ref, k_ref, v_ref, b_ref, o_ref, lse_ref, vt_ref, bcol_ref = refs
        else:
            q_ref, k_ref, v_ref, o_ref, lse_ref, vt_ref = refs
            b_ref = bcol_ref = None
        _ride_start(rider, ride_refs, pl.program_id(0) == 0)
        row = lax.broadcasted_iota(jnp.int32, (t, t), 0)
        col = lax.broadcasted_iota(jnp.int32, (t, t), 1)
        cmask_t = jnp.concatenate([row <= col, row <= col], axis=1)
        lane_lt64 = lax.broadcasted_iota(jnp.int32, (t, LANES), 1) < 64

        def as_column(r):
            return jnp.broadcast_to(r, (8, r.shape[1])).T[:, 0:1]

        def v_block(j, _):
            c0 = pl.multiple_of(j * t, t)
            vt_ref[j] = v_ref[pl.ds(c0, t), :].astype(F32).T.astype(BF16)
            if has_bias:
                for a in range(2):
                    bcol_ref[a, pl.ds(c0, t), :] = as_column(b_ref[0, a, j])
            return 0

        lax.fori_loop(0, nb, v_block, 0)

        def stacked_queries(i):
            return _stack_heads(q_ref[pl.ds(pl.multiple_of(i * t, t), t), :], hw).astype(F32).T.astype(BF16)

        def kv_step(j, carry, qs_t, masked):
            m, l, acc = carry
            rows = pl.ds(pl.multiple_of(j * t, t), t)
            sc = jnp.dot(k_ref[rows, :], qs_t, preferred_element_type=F32) * c1
            if has_bias:
                sc = sc + jnp.concatenate([jnp.broadcast_to(bcol_ref[0, rows, :], (t, t)),
                                           jnp.broadcast_to(bcol_ref[1, rows, :], (t, t))], axis=1)
            if masked:
                sc = jnp.where(cmask_t, sc, NEG_INF)
            m_new = jnp.maximum(m, jnp.max(sc, axis=0, keepdims=True))
            alpha = jnp.exp2(m - m_new)
            p = jnp.exp2(sc - m_new)
            l_new = alpha * l + jnp.sum(p, axis=0, keepdims=True)
            pv = jnp.dot(vt_ref[j], p.astype(BF16), preferred_element_type=F32)
            return m_new, l_new, alpha * acc + pv

        def finish(i, carry):
            m, l, acc = carry
            r0 = pl.multiple_of(i * t, t)
            out = (acc / l).T
            lse2 = as_column(m + jnp.log2(l))
            lse_ref[0, 0, pl.ds(r0, t), :] = lse2[:t]
            lse_ref[0, 1, pl.ds(r0, t), :] = lse2[t:]
            o_ref[pl.ds(r0, t), :] = jnp.where(lane_lt64, out[:t], out[t:])

        init = (jnp.full((1, 2 * t), NEG_INF, F32), jnp.zeros((1, 2 * t), F32), jnp.zeros((LANES, 2 * t), F32))

        def q_block(i, _):
            qs_t = stacked_queries(i)
            carry = lax.fori_loop(0, i, lambda j, c: kv_step(j, c, qs_t, False), init)
            finish(i, kv_step(i, carry, qs_t, True))
            return 0

        lax.fori_loop(0, nb, q_block, 0)
        _ride_wait(rider, ride_refs, pl.program_id(0) == n_pairs - 1)

    in_specs = [_resident((s, qw), lambda p: (0, q_off + p)), _resident((s, qw), lambda p: (0, k_off + p)),
                _resident((s, LANES), lambda p: (0, v_off + p))]
    args = [q, k, v]
    if has_bias:
        in_specs.append(_resident((1, 2, nb, 1, t), lambda p: (p, 0, 0, 0, 0)))
        args.append(bias)
    out_specs = [pl.BlockSpec((s, LANES), lambda p: (0, p)), pl.BlockSpec((1, 2, s, 1), lambda p: (p, 0, 0, 0))]
    out_shape = [jax.ShapeDtypeStruct((s, n_pairs * LANES), F32), jax.ShapeDtypeStruct((n_pairs, 2, s, 1), F32)]
    scratch = [pltpu.VMEM((nb, LANES, t), BF16)] + ([pltpu.VMEM((2, s, 1), F32)] if has_bias else [])
    scratch += _add_rider(rider, in_specs, args, out_specs, out_shape)
    return pl.pallas_call(
        body, name=name, grid=(n_pairs,), in_specs=in_specs, out_specs=out_specs, out_shape=out_shape,
        scratch_shapes=scratch,
        compiler_params=_params(("parallel",) if rider is None else ("arbitrary",)),
    )(*args)


def _flash_bwd(q, k, v, do, o, lse, bias, *, n_pairs, hw, q_off, k_off, v_off, scale, qk_dtype, name, rider=None,
               stacked=False):
    s = q.shape[0]
    t = min(ATT_T, s)
    nb = s // t
    qw = 2 * hw
    has_bias = bias is not None
    c1 = scale * LOG2E

    def body(*refs):
        n_grads = 1 if stacked else 3
        refs, ride_refs = _split_rider(refs, rider, n_in=7 if has_bias else 6, n_out=n_grads + (2 if has_bias else 0))
        if stacked:
            refs = list(refs)
            n_in = 7 if has_bias else 6
            refs[n_in:n_in + 1] = [refs[n_in].at[0], refs[n_in].at[1], refs[n_in].at[2]]
        if has_bias:
            (q_ref, k_ref, v_ref, do_ref, o_ref, lse_ref, b_ref, dq_ref, dk_ref, dv_ref, db_ref, dr_ref,
             dkt_ref, dvt_ref) = refs
            db_ref[...] = jnp.zeros_like(db_ref)
        else:
            q_ref, k_ref, v_ref, do_ref, o_ref, lse_ref, dq_ref, dk_ref, dv_ref, dkt_ref, dvt_ref = refs
            b_ref = db_ref = dr_ref = None
        _ride_start(rider, ride_refs, pl.program_id(0) == 0)
        dkt_ref[...] = jnp.zeros_like(dkt_ref)
        dvt_ref[...] = jnp.zeros_like(dvt_ref)
        causal = lax.broadcasted_iota(jnp.int32, (t, t), 1) <= lax.broadcasted_iota(jnp.int32, (t, t), 0)
        cmask = jnp.concatenate([causal, causal], axis=0)
        lane_lt_hw = lax.broadcasted_iota(jnp.int32, (t, qw), 1) < hw

        def q_block(i, _):
            r0 = pl.multiple_of(i * t, t)
            qs = _stack_heads(q_ref[pl.ds(r0, t), :], hw)
            dos = _stack_heads(do_ref[pl.ds(r0, t), :], 64)
            ot = o_ref[pl.ds(r0, t), :]
            delta = jnp.sum(dos * jnp.concatenate([ot, ot], axis=0), axis=-1, keepdims=True)
            lse2 = jnp.concatenate([lse_ref[0, 0, pl.ds(r0, t), :], lse_ref[0, 1, pl.ds(r0, t), :]], axis=0)
            dosb = dos.astype(BF16)
            dos_t = dos.T.astype(BF16)
            qs_t = qs.astype(F32).T.astype(BF16)

            def kv_step(j, carry, masked):
                dq, rsum = carry
                c0 = pl.multiple_of(j * t, t)
                kt = k_ref[pl.ds(c0, t), :]
                vt = v_ref[pl.ds(c0, t), :]
                sc = lax.dot_general(qs, kt, _NT, preferred_element_type=F32) * c1
                if has_bias:
                    sc = sc + _stacked_rows(b_ref[0, 0, j], b_ref[0, 1, j], t)
                if masked:
                    sc = jnp.where(cmask, sc, NEG_INF)
                p = jnp.exp2(sc - lse2)
                dp = lax.dot_general(dosb, vt, _NT, preferred_element_type=F32)
                ds = p * (dp - delta)
                dsb = ds.astype(BF16)
                pb = p.astype(BF16)
                if hw == LANES:
                    dvt_ref[j] += jnp.concatenate(
                        [jnp.dot(dos_t[:64, :t], pb[:t], preferred_element_type=F32),
                         jnp.dot(dos_t[64:, t:], pb[t:], preferred_element_type=F32)], axis=0)
                    dkt_ref[j] += jnp.concatenate(
                        [jnp.dot(qs_t[:hw, :t], dsb[:t], preferred_element_type=F32),
                         jnp.dot(qs_t[hw:, t:], dsb[t:], preferred_element_type=F32)], axis=0)
                else:
                    dvt_ref[j] += jnp.dot(dos_t, pb, preferred_element_type=F32)
                    dkt_ref[j] += jnp.dot(qs_t, dsb, preferred_element_type=F32)
                if has_bias:
                    db_ref[0, 0, j] += jnp.sum(ds[:t], axis=0, keepdims=True)
                    db_ref[0, 1, j] += jnp.sum(ds[t:], axis=0, keepdims=True)
                    rsum = rsum + jnp.sum(ds, axis=-1, keepdims=True)
                return dq + jnp.dot(dsb, kt, preferred_element_type=F32), rsum

            init = (jnp.zeros((2 * t, qw), F32), jnp.zeros((2 * t, 1), F32))
            carry = lax.fori_loop(0, i, functools.partial(kv_step, masked=False), init)
            dq, rsum = kv_step(i, carry, True)
            dq = dq * scale
            dq_ref[pl.ds(r0, t), :] = jnp.where(lane_lt_hw, dq[:t], dq[t:]).astype(qk_dtype)
            if has_bias:
                rsum_row = jnp.broadcast_to(rsum, (2 * t, LANES)).T[0:1]
                dr_ref[0, 0, i] = rsum_row[:, :t]
                dr_ref[0, 1, i] = rsum_row[:, t:]
            return 0

        lax.fori_loop(0, nb, q_block, 0)

        def k_block(j, _):
            c0 = pl.multiple_of(j * t, t)
            dk_ref[pl.ds(c0, t), :] = (dkt_ref[j].T * scale).astype(qk_dtype)
            dv_ref[pl.ds(c0, t), :] = dvt_ref[j].T.astype(BF16)
            return 0

        lax.fori_loop(0, nb, k_block, 0)
        _ride_wait(rider, ride_refs, pl.program_id(0) == n_pairs - 1)

    in_specs = [_resident((s, qw), lambda p: (0, q_off + p)), _resident((s, qw), lambda p: (0, k_off + p)),
                _resident((s, LANES), lambda p: (0, v_off + p)),
                _resident((s, LANES), lambda p: (0, p)), _resident((s, LANES), lambda p: (0, p)),
                _resident((1, 2, s, 1), lambda p: (p, 0, 0, 0))]
    args = [q, k, v, do, o, lse]
    if stacked:
        assert qw == LANES and qk_dtype == BF16
        out_specs = [pl.BlockSpec((3, s, LANES), lambda p: (0, 0, p))]
        out_shape = [jax.ShapeDtypeStruct((3, s, n_pairs * LANES), BF16)]
    else:
        out_specs = [pl.BlockSpec((s, qw), lambda p: (0, p)), pl.BlockSpec((s, qw), lambda p: (0, p)),
                     pl.BlockSpec((s, LANES), lambda p: (0, p))]
        out_shape = [jax.ShapeDtypeStruct((s, n_pairs * qw), qk_dtype), jax.ShapeDtypeStruct((s, n_pairs * qw), qk_dtype),
                     jax.ShapeDtypeStruct((s, n_pairs * LANES), BF16)]
    if has_bias:
        in_specs.append(_resident((1, 2, nb, 1, t), lambda p: (p, 0, 0, 0, 0)))
        args.append(bias)
        for _ in range(2):
            out_specs.append(pl.BlockSpec((1, 2, nb, 1, t), lambda p: (p, 0, 0, 0, 0)))
            out_shape.append(jax.ShapeDtypeStruct((n_pairs, 2, nb, 1, t), F32))
    scratch = [pltpu.VMEM((nb, qw, t), F32), pltpu.VMEM((nb, LANES, t), F32)]
    scratch += _add_rider(rider, in_specs, args, out_specs, out_shape)
    return pl.pallas_call(
        body, name=name, grid=(n_pairs,), in_specs=in_specs, out_specs=out_specs, out_shape=out_shape,
        scratch_shapes=scratch,
        compiler_params=_params(("parallel",) if rider is None else ("arbitrary",)),
    )(*args)


def _alibi_slope(h):
    return 2.0 ** (-8.0 * (h + 1.0) / SWA_HEADS)


SWA_ROWS = 512
SWA_SCALE = SWA_DIM ** -0.5


def _swa_geometry(i):
    w = WINDOW
    r0 = pl.multiple_of(i * w, w)
    b0 = pl.multiple_of(jnp.maximum(i - 1, 0) * w, w)
    row = lax.broadcasted_iota(jnp.int32, (w, 2 * w), 0)
    col = lax.broadcasted_iota(jnp.int32, (w, 2 * w), 1)
    dist = row - col + (r0 - b0)
    valid = (dist >= 0) & (dist < w)
    return r0, b0, dist.astype(F32), valid


def _swa_q_head(qblk, h):
    kv = h // (SWA_HEADS // SWA_KV_HEADS)
    if h % 2 != kv:
        qblk = pltpu.roll(qblk, 64, axis=1)
    return jnp.where(_head_mask(qblk.shape, kv), qblk, 0.0)


SWA_GROUP = SWA_HEADS // SWA_KV_HEADS


def _swa_stack(ref, rs, grp):
    parts = []
    for a in range(SWA_GROUP):
        h = SWA_GROUP * grp + a
        parts.append(_swa_q_head(ref[rs, (h // 2) * LANES:(h // 2 + 1) * LANES].astype(F32), h))
    return jnp.concatenate(parts, axis=0)


def _swa_unstack(x, grp):
    tiles = []
    for a in range(SWA_GROUP):
        h = SWA_GROUP * grp + a
        tile = x[a * WINDOW:(a + 1) * WINDOW]
        tiles.append(pltpu.roll(tile, 64, axis=1) if h % 2 != grp else tile)
    return tiles


def _swa_head_column(vals):
    return jnp.concatenate([jnp.full((WINDOW, 1), v, F32) for v in vals], axis=0)


def _swa_logits(qs, kb, dist, valid, grp):
    slopes = _swa_head_column([_alibi_slope(SWA_GROUP * grp + a) for a in range(SWA_GROUP)])
    dist4 = jnp.concatenate([dist] * SWA_GROUP, axis=0)
    valid4 = jnp.concatenate([valid] * SWA_GROUP, axis=0)
    sc = lax.dot_general(qs, kb, _NT, preferred_element_type=F32) * SWA_SCALE - slopes * dist4
    return jnp.where(valid4, sc, NEG_INF)


def _swa_merge_heads(tiles):
    lt64 = lax.broadcasted_iota(jnp.int32, (WINDOW, LANES), 1) < 64
    return jnp.concatenate([jnp.where(lt64, tiles[2 * b], tiles[2 * b + 1]) for b in range(SWA_HEADS // 2)], axis=1)


def _swa_fwd(z0b, sinks, *, name):
    s = z0b.shape[0]
    w = WINDOW
    rows = min(SWA_ROWS, s)
    per_step = rows // w
    qcols = SWA_HEADS * SWA_DIM

    def body(sink_ref, q_ref, k_ref, v_ref, o_ref, lse_ref):
        g = pl.program_id(0)
        for ii in range(per_step):
            rs = slice(ii * w, (ii + 1) * w)
            r0, b0, dist, valid = _swa_geometry(g * per_step + ii)
            kb = k_ref[pl.ds(b0, 2 * w), :]
            vb = v_ref[pl.ds(b0, 2 * w), :]
            o_tiles = []
            for h in range(SWA_HEADS):
                kv = h // SWA_GROUP
                qh = _swa_q_head(q_ref[rs, (h // 2) * LANES:(h // 2 + 1) * LANES].astype(F32), h).astype(BF16)
                sc = lax.dot_general(qh, kb, _NT, preferred_element_type=F32) * SWA_SCALE - _alibi_slope(h) * dist
                sc = jnp.where(valid, sc, NEG_INF)
                sink = sink_ref[0, h]
                m = jnp.maximum(jnp.max(sc, axis=-1, keepdims=True), sink)
                p = jnp.exp(sc - m)
                l = jnp.sum(p, axis=-1, keepdims=True) + jnp.exp(sink - m)
                oh = jnp.dot(p.astype(BF16), vb, preferred_element_type=F32) / l
                o_tiles.append(pltpu.roll(oh, 64, axis=1) if h % 2 != kv else oh)
                lse_ref[h, rs, :] = m + jnp.log(l)
            o_ref[rs, :] = _swa_merge_heads(o_tiles)

    return pl.pallas_call(
        body, name=name, grid=(s // rows,),
        in_specs=[pl.BlockSpec(memory_space=pltpu.SMEM),
                  pl.BlockSpec((rows, qcols), lambda g: (g, 0)),
                  pl.BlockSpec((s, LANES), lambda g: (0, 4)), pl.BlockSpec((s, LANES), lambda g: (0, 5))],
        out_specs=[pl.BlockSpec((rows, qcols), lambda g: (g, 0)), pl.BlockSpec((SWA_HEADS, rows, 1), lambda g: (0, g, 0))],
        out_shape=[jax.ShapeDtypeStruct((s, qcols), F32), jax.ShapeDtypeStruct((SWA_HEADS, s, 1), F32)],
        compiler_params=_params(("parallel",)),
    )(sinks, z0b, z0b, z0b)


def _swa_bwd(z0b, sinks, do, o, lse, *, name):
    s = z0b.shape[0]
    w = WINDOW
    rows = min(SWA_ROWS, s)
    per_step = rows // w
    qcols = SWA_HEADS * SWA_DIM
    nblk = s // w

    def body(sink_ref, q_ref, k_ref, v_ref, do_ref, o_ref, lse_ref, dq_ref, dkt_ref, dvt_ref, dsink_ref):
        g = pl.program_id(0)

        @pl.when(g == 0)
        def _():
            dkt_ref[...] = jnp.zeros_like(dkt_ref)
            dvt_ref[...] = jnp.zeros_like(dvt_ref)
            dsink_ref[...] = jnp.zeros_like(dsink_ref)

        for ii in range(per_step):
            i = g * per_step + ii
            rs = slice(ii * w, (ii + 1) * w)
            r0, b0, dist, valid = _swa_geometry(i)
            j0 = jnp.maximum(i - 1, 0)
            kb = k_ref[pl.ds(b0, 2 * w), :]
            vb = v_ref[pl.ds(b0, 2 * w), :]
            dq_tiles = []
            for grp in range(SWA_KV_HEADS):
                heads = [SWA_GROUP * grp + a for a in range(SWA_GROUP)]
                qs32 = _swa_stack(q_ref, rs, grp)
                dos32 = _swa_stack(do_ref, rs, grp)
                delta = jnp.sum(dos32 * _swa_stack(o_ref, rs, grp), axis=-1, keepdims=True)
                lse = jnp.concatenate([lse_ref[h, rs, :] for h in heads], axis=0)
                sink = _swa_head_column([sink_ref[0, h] for h in heads])
                p = jnp.exp(_swa_logits(qs32.astype(BF16), kb, dist, valid, grp) - lse)
                dp = lax.dot_general(dos32.astype(BF16), vb, _NT, preferred_element_type=F32)
                ds = p * (dp - delta)
                dsb = ds.astype(BF16)
                d_sink = jnp.exp(sink - lse) * delta
                for a, h in enumerate(heads):
                    dsink_ref[h:h + 1, :] += jnp.broadcast_to(-jnp.sum(d_sink[a * w:(a + 1) * w]), (1, LANES))
                dvt = jnp.dot(dos32.T.astype(BF16), p.astype(BF16), preferred_element_type=F32)
                dkt = jnp.dot(qs32.T.astype(BF16), dsb, preferred_element_type=F32) * SWA_SCALE
                dvt_ref[j0] += dvt[:, :w]
                dvt_ref[j0 + 1] += dvt[:, w:]
                dkt_ref[j0] += dkt[:, :w]
                dkt_ref[j0 + 1] += dkt[:, w:]
                dq_tiles += _swa_unstack(jnp.dot(dsb, kb, preferred_element_type=F32) * SWA_SCALE, grp)
            dq_ref[rs, :] = _swa_merge_heads(dq_tiles)

    return pl.pallas_call(
        body, name=name, grid=(s // rows,),
        in_specs=[pl.BlockSpec(memory_space=pltpu.SMEM),
                  pl.BlockSpec((rows, qcols), lambda g: (g, 0)),
                  pl.BlockSpec((s, LANES), lambda g: (0, 4)), pl.BlockSpec((s, LANES), lambda g: (0, 5)),
                  pl.BlockSpec((rows, qcols), lambda g: (g, 0)), pl.BlockSpec((rows, qcols), lambda g: (g, 0)),
                  pl.BlockSpec((SWA_HEADS, rows, 1), lambda g: (0, g, 0))],
        out_specs=[pl.BlockSpec((rows, qcols), lambda g: (g, 0)),
                   pl.BlockSpec((nblk, LANES, w), lambda g: (0, 0, 0)),
                   pl.BlockSpec((nblk, LANES, w), lambda g: (0, 0, 0)),
                   pl.BlockSpec((SWA_HEADS, LANES), lambda g: (0, 0))],
        out_shape=[jax.ShapeDtypeStruct((s, qcols), F32),
                   jax.ShapeDtypeStruct((nblk, LANES, w), F32), jax.ShapeDtypeStruct((nblk, LANES, w), F32),
                   jax.ShapeDtypeStruct((SWA_HEADS, LANES), F32)],
        compiler_params=_params(("arbitrary",)),
    )(sinks, z0b, z0b, z0b, do, o, lse)


CUM_T = 256


def _split3(x):
    hi = x.astype(BF16)
    r1 = x - hi.astype(F32)
    mid = r1.astype(BF16)
    lo = (r1 - mid.astype(F32)).astype(BF16)
    return hi, mid, lo


def _tri_dot(tri, x):
    hi, mid, lo = _split3(x)
    out = jnp.dot(tri, hi, preferred_element_type=F32)
    out = out + jnp.dot(tri, mid, preferred_element_type=F32)
    return out + jnp.dot(tri, lo, preferred_element_type=F32)


def _logf_fwd(zf, bf, *, name):
    s = zf.shape[0]
    t = CUM_T
    nb = s // t

    def body(z_ref, b_ref, c_ref, carry_ref):
        i = pl.program_id(0)

        @pl.when(i == 0)
        def _():
            carry_ref[...] = jnp.zeros_like(carry_ref)

        x = z_ref[...] + b_ref[...]
        lf = jnp.minimum(x, 0.0) - jnp.log(1.0 + jnp.exp(-jnp.abs(x)))
        row = lax.broadcasted_iota(jnp.int32, (t, t), 0)
        col = lax.broadcasted_iota(jnp.int32, (t, t), 1)
        tri = jnp.where(col <= row, 1.0, 0.0).astype(BF16)
        c = _tri_dot(tri, lf) + carry_ref[...]
        c_ref[...] = c
        carry_ref[...] = c[t - 1:t, :]

    return pl.pallas_call(
        body, name=name, grid=(nb,),
        in_specs=[pl.BlockSpec((t, LANES), lambda i: (i, 0)), pl.BlockSpec((1, LANES), lambda i: (0, 0))],
        out_specs=pl.BlockSpec((t, LANES), lambda i: (i, 0)),
        out_shape=jax.ShapeDtypeStruct((s, LANES), F32),
        scratch_shapes=[pltpu.VMEM((1, LANES), F32)],
        compiler_params=_params(("arbitrary",)),
    )(zf, bf)


def _logf_bwd(dc, zf, bf, *, name):
    s = zf.shape[0]
    t = CUM_T
    nb = s // t

    def body(dc_ref, z_ref, b_ref, dz_ref, db_ref, carry_ref):
        i = pl.program_id(0)

        @pl.when(i == 0)
        def _():
            carry_ref[...] = jnp.zeros_like(carry_ref)
            db_ref[...] = jnp.zeros_like(db_ref)

        row = lax.broadcasted_iota(jnp.int32, (t, t), 0)
        col = lax.broadcasted_iota(jnp.int32, (t, t), 1)
        tri = jnp.where(col >= row, 1.0, 0.0).astype(BF16)
        dlf = _tri_dot(tri, dc_ref[...]) + carry_ref[...]
        carry_ref[...] = dlf[0:1, :]
        x = z_ref[...] + b_ref[...]
        dz = dlf * _sigmoid(-x)
        dz_ref[...] = dz.astype(BF16)
        db_ref[...] += jnp.sum(dz, axis=0, keepdims=True)

    return pl.pallas_call(
        body, name=name, grid=(nb,),
        in_specs=[pl.BlockSpec((t, LANES), lambda i: (nb - 1 - i, 0)), pl.BlockSpec((t, LANES), lambda i: (nb - 1 - i, 0)),
                  pl.BlockSpec((1, LANES), lambda i: (0, 0))],
        out_specs=[pl.BlockSpec((t, LANES), lambda i: (nb - 1 - i, 0)), pl.BlockSpec((1, LANES), lambda i: (0, 0))],
        out_shape=[jax.ShapeDtypeStruct((s, LANES), BF16), jax.ShapeDtypeStruct((1, LANES), F32)],
        scratch_shapes=[pltpu.VMEM((1, LANES), F32)],
        compiler_params=_params(("arbitrary",)),
    )(dc, zf, bf)


def _sum_pieces(p_ref):
    g = p_ref[0].astype(F32)
    for k in range(1, N_DEV):
        g = g + p_ref[k].astype(F32)
    return g


def _adam_update(g, w, m, v):
    bc1 = 1.0 - ADAM_B1 ** ADAM_STEP
    bc2 = 1.0 - ADAM_B2 ** ADAM_STEP
    nm = ADAM_B1 * m + (1.0 - ADAM_B1) * g
    nv = ADAM_B2 * v + (1.0 - ADAM_B2) * (g * g)
    m_hat = nm / bc1
    v_hat = nv / bc2
    return -ADAM_LR * (m_hat / (jnp.sqrt(v_hat) + ADAM_EPS) + ADAM_WD * w), nm, nv


def _adamw(pieces, w, m, v, *, name):
    rows, cols = w.shape
    tr = _tile(rows, (RB1, RB0, SMALL_ROWS))

    def body(p_ref, w_ref, m_ref, v_ref, g_ref, d_ref, nm_ref, nv_ref):
        g = _sum_pieces(p_ref)
        g_ref[...] = g
        d_ref[...], nm_ref[...], nv_ref[...] = _adam_update(g, w_ref[...], m_ref[...], v_ref[...])

    spec = pl.BlockSpec((tr, cols), lambda i: (i, 0))
    shape = jax.ShapeDtypeStruct((rows, cols), F32)
    return pl.pallas_call(
        body, name=name, grid=(rows // tr,),
        in_specs=[pl.BlockSpec((N_DEV, tr, cols), lambda i: (0, i, 0)), spec, spec, spec],
        out_specs=[spec, spec, spec, spec], out_shape=[shape, shape, shape, shape],
        compiler_params=_params(("parallel",)),
    )(pieces, w, m, v)


def _sum8(pieces, rows, *, name):
    cols = pieces.shape[2]
    tr = _tile(rows, (176, 96))

    def body(p_ref, g_ref):
        g_ref[...] = _sum_pieces(p_ref)

    return pl.pallas_call(
        body, name=name, grid=(rows // tr,),
        in_specs=[pl.BlockSpec((N_DEV, tr, cols), lambda i: (0, i, 0))],
        out_specs=pl.BlockSpec((tr, cols), lambda i: (i, 0)),
        out_shape=jax.ShapeDtypeStruct((rows, cols), F32),
        compiler_params=_params(("parallel",)),
    )(pieces)


def _adamw_columns(g, w, m, v, *, name):
    n, _, k = w.shape
    tr = n // 2

    def body(g_ref, w_ref, m_ref, v_ref, d_ref, nm_ref, nv_ref):
        d_ref[...], nm_ref[...], nv_ref[...] = _adam_update(g_ref[...], w_ref[...], m_ref[...], v_ref[...])

    spec = pl.BlockSpec((tr, 1, k), lambda i: (i, 0, 0))
    shape = jax.ShapeDtypeStruct((n, 1, k), F32)
    return pl.pallas_call(
        body, name=name, grid=(n // tr,), in_specs=[spec, spec, spec, spec],
        out_specs=[spec, spec, spec], out_shape=[shape, shape, shape],
        compiler_params=_params(("parallel",)),
    )(g, w, m, v)


MESH = pl.DeviceIdType.MESH
ANY = pl.BlockSpec(memory_space=pl.ANY)


def _all_gather(shard, *, name):
    rows, lanes = shard.shape

    def body(x_ref, out_ref, send_sems, recv_sems, local_sem):
        x, y, c = lax.axis_index("x"), lax.axis_index("y"), lax.axis_index("c")
        me, sibling = (x, y, c), (x, y, 1 - c)
        chips = [(1 - x, y), (x, 1 - y), (1 - x, 1 - y)]

        def block(px, py, pc):
            return out_ref.at[4 * px + 2 * py + pc]

        def copy(k, blk, to, src=None):
            return pltpu.make_async_remote_copy(
                src_ref=block(*blk) if src is None else src, dst_ref=block(*blk),
                send_sem=send_sems.at[k], recv_sem=recv_sems.at[k], device_id=to, device_id_type=MESH)

        mine = pltpu.make_async_copy(x_ref, block(*me), local_sem)
        mine.start()
        first = [copy(0, me, sibling, src=x_ref)]
        first += [copy(1 + j, me, (*chip, c), src=x_ref) for j, chip in enumerate(chips)]
        for cp in first:
            cp.start()
        passed = [copy(4 + j, (*chip, c), sibling) for j, chip in enumerate(chips)]
        for j, chip in enumerate(chips):
            copy(1 + j, (*chip, c), me).wait_recv()
            passed[j].start()
        copy(0, sibling, me).wait_recv()
        for j, chip in enumerate(chips):
            copy(4 + j, (*chip, 1 - c), me).wait_recv()
        for cp in first + passed:
            cp.wait_send()
        mine.wait()

    return pl.pallas_call(
        body, name=name, out_shape=jax.ShapeDtypeStruct((N_DEV, rows, lanes), shard.dtype),
        in_specs=[ANY], out_specs=ANY,
        scratch_shapes=[pltpu.SemaphoreType.DMA((7,)), pltpu.SemaphoreType.DMA((7,)), pltpu.SemaphoreType.DMA(())],
    )(shard)


def _peer_copies(kind, src_ref, out_ref, send_sems, recv_sems, local_sem):
    x, y, c = lax.axis_index("x"), lax.axis_index("y"), lax.axis_index("c")
    me = 4 * x + 2 * y + c

    def src(idx):
        return src_ref.at[idx] if kind == "exchange" else src_ref

    mine = None if local_sem is None else pltpu.make_async_copy(src(me), out_ref.at[me], local_sem)
    copies = []
    for r in (2, 4, 6) if kind == "across" else range(1, N_DEV):
        px = 1 - x if r & 4 else x
        py = 1 - y if r & 2 else y
        pc = 1 - c if r & 1 else c
        copies.append(pltpu.make_async_remote_copy(
            src_ref=src(4 * px + 2 * py + pc), dst_ref=out_ref.at[me],
            send_sem=send_sems.at[r - 1], recv_sem=recv_sems.at[r - 1],
            device_id=(px, py, pc), device_id_type=MESH))
    return mine, copies


def _to_other_core(shard, land, *, name):
    def body(src_ref, land_ref, out_ref, send_sems, recv_sems):
        x, y, c = lax.axis_index("x"), lax.axis_index("y"), lax.axis_index("c")
        copies = []
        for k, r in enumerate((0, 2, 4, 6)):
            slot = 4 * (1 - x if r & 4 else x) + 2 * (1 - y if r & 2 else y) + c
            copies.append(pltpu.make_async_remote_copy(
                src_ref=src_ref if r == 0 else land_ref.at[slot], dst_ref=out_ref.at[slot],
                send_sem=send_sems.at[k], recv_sem=recv_sems.at[k], device_id=(x, y, 1 - c), device_id_type=MESH))
        for cp in copies:
            cp.start()
        for cp in copies:
            cp.wait()

    return pl.pallas_call(
        body, name=name, out_shape=jax.ShapeDtypeStruct(land.shape, land.dtype), in_specs=[ANY, ANY], out_specs=ANY,
        input_output_aliases={1: 0}, scratch_shapes=[pltpu.SemaphoreType.DMA((4,)), pltpu.SemaphoreType.DMA((4,))],
    )(shard, land)


PEER_SEMS = [pltpu.SemaphoreType.DMA((7,)), pltpu.SemaphoreType.DMA((7,)), pltpu.SemaphoreType.DMA(())]


HBM = pl.BlockSpec(memory_space=pltpu.HBM)
SEMAPHORES = pl.BlockSpec(memory_space=pltpu.SEMAPHORE)


def _peer_start(kind, arr, *, name):
    land = lax.empty((N_DEV,) + arr.shape[-2:], arr.dtype)

    def body(src_ref, land_ref, send_sems, recv_sems, src_thru, land_thru, token):
        _, copies = _peer_copies(kind, src_ref, land_ref, send_sems, recv_sems, None)
        for cp in copies:
            cp.start()
        token[...] = jnp.zeros_like(token)

    return pl.pallas_call(
        body, name=name,
        out_shape=(pltpu.SemaphoreType.DMA((N_DEV - 1,)), pltpu.SemaphoreType.DMA((N_DEV - 1,)),
                   pltpu.HBM(arr.shape, arr.dtype), pltpu.HBM(land.shape, land.dtype), jax.ShapeDtypeStruct((8, LANES), F32)),
        in_specs=(HBM, HBM), out_specs=(SEMAPHORES, SEMAPHORES, HBM, HBM, pl.BlockSpec(memory_space=pltpu.VMEM)),
        input_output_aliases={0: 2, 1: 3},
        compiler_params=pltpu.CompilerParams(has_side_effects=pltpu.SideEffectType.DATAFLOW_SIDE_EFFECTING),
    )(pltpu.with_memory_space_constraint(arr, pltpu.HBM), pltpu.with_memory_space_constraint(land, pltpu.HBM))


def _peer_wait(kind, send_sems, recv_sems, src_thru, land_thru, after, *, name):
    def body(src_ref, land_ref, send_sems, recv_sems, *_):
        _, copies = _peer_copies(kind, src_ref, land_ref, send_sems, recv_sems, None)
        for cp in copies:
            cp.wait_send()
            cp.wait_recv()

    return pl.pallas_call(
        body, name=name,
        out_shape=(pltpu.HBM(src_thru.shape, src_thru.dtype), pltpu.HBM(land_thru.shape, land_thru.dtype)),
        in_specs=(HBM, HBM, SEMAPHORES, SEMAPHORES) + (ANY,) * len(after), out_specs=(HBM, HBM),
        input_output_aliases={0: 0, 1: 1},
        compiler_params=pltpu.CompilerParams(has_side_effects=pltpu.SideEffectType.DATAFLOW_SIDE_EFFECTING),
    )(src_thru, land_thru, send_sems, recv_sems, *after)


def _add_rider(rider, in_specs, args, out_specs, out_shape):
    if rider is None:
        return []
    _, arr = rider
    in_specs.append(ANY)
    args.append(arr)
    out_specs.append(ANY)
    out_shape.append(jax.ShapeDtypeStruct((N_DEV,) + arr.shape[-2:], arr.dtype))
    return list(PEER_SEMS)


def _split_rider(refs, rider, n_in, n_out):
    if rider is None:
        return refs, None
    refs = list(refs)
    rin = refs.pop(n_in)
    rout = refs.pop(n_in + n_out)
    return refs[:-3], (rin, rout, *refs[-3:])


def _ride_start(rider, ride_refs, first):
    if rider is None:
        return

    @pl.when(first)
    def _():
        mine, copies = _peer_copies(rider[0], *ride_refs)
        mine.start()
        for cp in copies:
            cp.start()


def _ride_wait(rider, ride_refs, last):
    if rider is None:
        return

    @pl.when(last)
    def _():
        mine, copies = _peer_copies(rider[0], *ride_refs)
        for cp in copies:
            cp.wait()
        mine.wait()


def _gathered_cols(blocks, kdim):
    n = blocks.shape[1] * WIDE // kdim
    return blocks.reshape(N_DEV, kdim, n).transpose(1, 0, 2).reshape(kdim, N_DEV * n)


def _scatter_cols(dw):
    kdim, n8 = dw.shape
    n = n8 // N_DEV
    return dw.reshape(kdim, N_DEV, n).transpose(1, 0, 2).reshape(N_DEV, kdim * n // WIDE, WIDE)


def _pad_rows(a, rows):
    pad = [(0, 0)] * a.ndim
    pad[-2] = (0, rows - a.shape[-2])
    return jnp.pad(a, pad)


def _layer0_in_weight_t(wt):
    cq, ckv, kpe = wt[0:256], wt[256:384], wt[384:416]
    q_s, k_s, v_s, gate = wt[416:928], wt[928:1056], wt[1056:1184], wt[1184:2208]
    z = jnp.zeros((64, wt.shape[1]), wt.dtype)
    return jnp.concatenate([gate, cq, ckv, z, kpe, z[:32], q_s, k_s, v_s], axis=0)


def _layer0_in_grad_t(dwt):
    gate, cq, ckv, kpe = dwt[0:1024], dwt[1024:1280], dwt[1280:1408], dwt[1472:1504]
    q_s, k_s, v_s = dwt[1536:2048], dwt[2048:2176], dwt[2176:2304]
    return jnp.concatenate([cq, ckv, kpe, q_s, k_s, v_s, gate], axis=0)


def _layer1_in_weight_t(wt):
    main = jnp.concatenate([wt[:3 * D_MODEL], wt[3 * D_MODEL + FOX_HEADS:]], axis=0)
    return main, _pad_rows(wt[3 * D_MODEL:3 * D_MODEL + FOX_HEADS], LANES)


def _layer1_in_grad_t(d_blocks, d_wft):
    return jnp.concatenate([d_blocks[0], d_wft[:FOX_HEADS], d_blocks[1]], axis=0)


def _q_up_weight(w):
    return jnp.pad(w.reshape(MLA_Q_RANK, MLA_HEADS, 96), ((0, 0), (0, 0), (0, 32))).reshape(MLA_Q_RANK, MLA_HEADS * LANES)


def _q_up_grad(dwp):
    return dwp.reshape(MLA_Q_RANK, MLA_HEADS, LANES)[:, :, :96].reshape(MLA_Q_RANK, MLA_HEADS * 96)


def _kv_up_weight(w):
    w4 = w.reshape(MLA_KV_RANK, MLA_HEADS, 2, 64)
    kp = jnp.pad(w4[:, :, 0, :], ((0, 0), (0, 0), (0, 64))).reshape(MLA_KV_RANK, MLA_HEADS * LANES)
    vp = w4[:, :, 1, :].reshape(MLA_KV_RANK, MLA_HEADS * 64)
    return jnp.concatenate([kp, vp], axis=1)


def _kv_up_grad(dwp):
    dk = dwp[:, :MLA_HEADS * LANES].reshape(MLA_KV_RANK, MLA_HEADS, LANES)[:, :, :64]
    dv = dwp[:, MLA_HEADS * LANES:].reshape(MLA_KV_RANK, MLA_HEADS, 64)
    return jnp.stack([dk, dv], axis=2).reshape(MLA_KV_RANK, MLA_HEADS * LANES)


def _pad_lanes(a):
    return jnp.pad(a, ((0, 0), (0, LANES - a.shape[1])))


def _small_pack(g_in, g_final, g_q_a, g_kv_a, sinks, b_f, loss):
    rows = [g_in.reshape(8, LANES), g_final.reshape(8, LANES), g_q_a.reshape(2, LANES), g_kv_a.reshape(1, LANES),
            _pad_lanes(sinks.reshape(1, -1)), _pad_lanes(b_f.reshape(1, -1)), _pad_lanes(loss.reshape(1, 1)),
            jnp.zeros((2, LANES), F32)]
    return jnp.concatenate(rows, axis=0)


def _small_unpack(a):
    return (a[0:8].reshape(1, D_MODEL), a[8:16].reshape(D_MODEL), a[16:18].reshape(1, MLA_Q_RANK),
            a[18:19].reshape(1, MLA_KV_RANK), a[19:20, :SWA_HEADS], a[20:21, :FOX_HEADS], a[21, 0])


def _local_step(x, positions, target, e_g_in, early, e_g_q_a, e_g_kv_a, e_sinks,
                late, o_b_f, g_final, scatter1=None, scatter0=None):
    s = x.shape[0]
    mla_scale = (MLA_NOPE + MLA_ROPE) ** -0.5
    fox_scale = FOX_DIM ** -0.5
    n0a = Z0A_UNITS * LANES

    inv_freq = 1.0 / (ROPE_THETA ** (jnp.arange(0, MLA_ROPE, 2, dtype=F32) / MLA_ROPE))
    ang = positions.astype(F32)[:, None] * inv_freq
    cos, sin = jnp.cos(ang), jnp.sin(ang)
    ones, zeros = jnp.ones((s, 64), F32), jnp.zeros((s, 64), F32)
    cos_t = jnp.concatenate([ones, cos, cos, ones[:, :32]], axis=1)
    sin_t = jnp.concatenate([zeros, -sin, sin, zeros[:, :32]], axis=1)
    cos_t, sin_t = lax.optimization_barrier((cos_t, sin_t))

    if len(early) == 3:
        h0 = _rmsnorm_fwd(x, e_g_in, width=D_MODEL, col_blk=0, name="l0_norm")
        w0t, wq, wkv = early
    else:
        pending, token, unpack, prep = early
        h0 = _rmsnorm_fwd(x, e_g_in, width=D_MODEL, col_blk=0, name="l0_norm", after=[token])
        sent, across = _peer_wait("across", *pending, after=[h0] + prep, name="weights0_wait")
        w0t, wq, wkv = unpack(sent, _to_other_core(sent, across, name="weights0_over"))
    z0a, z0b = _matmul_rows([(h0, w0t, True)], [], [], lambda r: (r[:, :n0a], r[:, n0a:]),
                            [("rows", n0a, F32), ("rows", Z0B_UNITS * LANES, BF16)], name="l0_in")
    cqn = _rmsnorm_fwd(z0a, e_g_q_a, width=MLA_Q_RANK, col_blk=4, name="l0_q_norm")
    ckvn = _rmsnorm_fwd(z0a, e_g_kv_a, width=MLA_KV_RANK, col_blk=10, name="l0_kv_norm")
    rope_rows = [(cos_t, LANES, 0), (sin_t, LANES, 0)]
    qm, = _matmul_rows([(cqn, wq, False)], rope_rows, [], _rope_q_epilogue, [("rows", MLA_HEADS * LANES, BF16)],
                       name="l0_q_up")
    kvm, = _matmul_rows([(ckvn, wkv, False)], [(z0a, LANES, 11)] + rope_rows, [], _rope_k_epilogue,
                        [("rows", MLA_HEADS * (LANES + MLA_V), BF16)], name="l0_kv_up")
    gathers = len(late) == 2
    res = _flash_fwd(qm, kvm, kvm, None, n_pairs=MLA_HEADS // 2, hw=LANES, q_off=0, k_off=0, v_off=MLA_HEADS,
                     scale=mla_scale, name="l0_mla_fwd", rider=("gather", late[0]) if gathers else None)
    o_mla, lse_mla = res[0], res[1]
    wo0, o_g_in, w1t, wft, wo1 = late[1](res[2]) if gathers else late
    o_swa, lse_swa = _swa_fwd(z0b, e_sinks, name="l0_swa_fwd")
    half = D_MODEL // 2

    x1, h1, og0 = _matmul_rows(
        [(None, wo0, False)], [(o_mla, half, 0), (o_swa, half, 0), (z0a, D_MODEL, 0), (x, D_MODEL, 0)], [o_g_in],
        lambda r, om, osw, gt, xt, g, made: (*_residual_norm_epilogue(r, xt, g), made),
        [("rows", D_MODEL, F32), ("rows", D_MODEL, BF16), ("rows", D_MODEL, BF16)], name="l0_out",
        prologue=lambda om, osw, gt, xt, g: _gated([om, osw], gt))
    z1, gate1, zf = _matmul_rows(
        [(None, w1t, True), (None, wft, True)], [(h1, D_MODEL, 0)], [],
        lambda r, h, made: (r[0][:, :3 * D_MODEL], r[0][:, 3 * D_MODEL:], r[1]),
        [("rows", 3 * D_MODEL, BF16), ("rows", D_MODEL, F32), ("rows", LANES, F32)], name="l1_in",
        prologue=lambda h: h, separate=True)
    bf = _pad_lanes(o_b_f)
    log_cum = _logf_fwd(zf, bf, name="l1_logf")
    bias2 = (-LOG2E * log_cum[:, :FOX_HEADS]).T
    t_bwd = min(ATT_T, s)
    bias = bias2.reshape(FOX_HEADS // 2, 2, s // t_bwd, 1, t_bwd)
    t_fwd = _fwd_tile(s)
    o_fox, lse_fox = _flash_fwd(z1, z1, z1, bias2.reshape(FOX_HEADS // 2, 2, s // t_fwd, 1, t_fwd),
                                n_pairs=FOX_HEADS // 2, hw=64, q_off=0, k_off=8, v_off=16, scale=fox_scale,
                                name="l1_fox_fwd")

    dx2, loss_part, d_g_final, og1, dx2_bf = _matmul_rows(
        [(None, wo1, False)], [(o_fox, D_MODEL, 0), (gate1, D_MODEL, 0), (x1, D_MODEL, 0), (target, D_MODEL, 0)],
        [g_final.reshape(1, D_MODEL)],
        lambda r, o, gt, xt, tg, g, made: _and_first(_loss_epilogue(r, xt, tg, g), made),
        [("rows", D_MODEL, F32), ("sum", (8, LANES)), ("sum", (1, D_MODEL)), ("rows", D_MODEL, BF16),
         ("rows", D_MODEL, BF16)], name="l1_out_loss", prologue=lambda o, gt, xt, tg, g: _gated([o], gt))

    d_wo1 = _matmul(og1, dx2_bf, ta=True, out_dtype=BF16, name="l1_out_dw")
    do_fox, d_gate1 = _matmul_rows([(dx2_bf, wo1, True)], [(o_fox, D_MODEL, 0), (gate1, D_MODEL, 0)], [],
                                   _gate_bwd_epilogue([D_MODEL]), [("rows", D_MODEL, F32), ("rows", D_MODEL, BF16)],
                                   name="l1_out_dx")
    dqkv1, dbias, drow = _flash_bwd(z1, z1, z1, do_fox, o_fox, lse_fox, bias, n_pairs=FOX_HEADS // 2, hw=64, q_off=0,
                                    k_off=8, v_off=16, scale=fox_scale, qk_dtype=BF16, stacked=True, name="l1_fox_bwd")
    d_log_cum = (drow.reshape(FOX_HEADS, s) - dbias.reshape(FOX_HEADS, s)).T
    d_log_cum = jnp.pad(d_log_cum, ((0, 0), (0, LANES - FOX_HEADS)))
    d_zf, d_bf = _logf_bwd(d_log_cum, zf, bf, name="l1_logf_bwd")
    d_w1t = (_matmul(dqkv1, h1, ta=True, out_dtype=BF16, name="l1_in_dw_qkv"),
             _matmul(d_gate1, h1, ta=True, out_dtype=BF16, name="l1_in_dw_gate"))
    d_wft = _matmul(d_zf, h1, ta=True, out_dtype=BF16, name="l1_in_f_dw")
    dx1, d_o_g_in, dx1_bf = _matmul_rows([(dqkv1, w1t, False, c * D_MODEL, c) for c in range(3)]
                                         + [(d_gate1, w1t, False, 3 * D_MODEL), (d_zf, wft, False)],
                                         [(x1, D_MODEL, 0), (dx2, D_MODEL, 0)], [o_g_in],
                                         lambda *a: _and_first(_rms_bwd_epilogue(*a)),
                                         [("rows", D_MODEL, F32), ("sum", (1, D_MODEL)), ("rows", D_MODEL, BF16)],
                                         name="l1_in_dx")

    d_wo0 = _matmul(og0, dx1_bf, ta=True, out_dtype=BF16, name="l0_out_dw")
    do_mla, do_swa, d_gate0 = _matmul_rows(
        [(dx1_bf, wo0, True)], [(o_mla, half, 0), (o_swa, half, 0), (z0a, D_MODEL, 0)], [], _gate_bwd_epilogue([half, half]),
        [("rows", half, F32), ("rows", half, F32), ("rows", D_MODEL, BF16)], name="l0_out_dx")
    dq_s, dkt_s, dvt_s, d_sinks = _swa_bwd(z0b, e_sinks, do_swa, o_swa, lse_swa, name="l0_swa_bwd")
    dk_s = dkt_s.transpose(0, 2, 1).reshape(s, LANES)
    dv_s = dvt_s.transpose(0, 2, 1).reshape(s, LANES)
    rider = None
    if scatter1 is not None:
        rider = ("exchange", scatter1(dict(w1t=d_w1t, wft=d_wft, wo1=d_wo1, o_g_in=d_o_g_in, wo0=d_wo0)))
    res = _flash_bwd(qm, kvm, kvm, do_mla, o_mla, lse_mla, None, n_pairs=MLA_HEADS // 2, hw=LANES, q_off=0, k_off=0,
                     v_off=MLA_HEADS, scale=mla_scale, qk_dtype=F32, name="l0_mla_bwd", rider=rider)
    dqm, dkm, dvm = res[0], res[1], res[2]
    recv1 = res[3] if rider is not None else None
    d_qp, d_kvp, d_kpe = _rope_bwd(dqm, dkm, dvm, cos_t, sin_t, name="l0_rope_bwd")
    d_wq = _matmul(cqn, d_qp, ta=True, out_dtype=BF16, name="l0_q_up_dw")
    d_cqn = _matmul(d_qp, wq, tb=True, name="l0_q_up_dx")
    d_wkv = _matmul(ckvn, d_kvp, ta=True, out_dtype=BF16, name="l0_kv_up_dw")
    d_ckvn = _matmul(d_kvp, wkv, tb=True, name="l0_kv_up_dx")
    d_cq, d_g_q_a = _rmsnorm_bwd(z0a, e_g_q_a, d_cqn, width=MLA_Q_RANK, col_blk=4, name="l0_q_norm_bwd")
    d_ckv, d_g_kv_a = _rmsnorm_bwd(z0a, e_g_kv_a, d_ckvn, width=MLA_KV_RANK, col_blk=10, name="l0_kv_norm_bwd")
    dz0 = jnp.concatenate([d_gate0, d_cq, d_ckv, d_kpe, dq_s.astype(BF16), dk_s.astype(BF16), dv_s.astype(BF16)], axis=1)
    d_w0t = _matmul(dz0, h0, ta=True, out_dtype=BF16, name="l0_in_dw")
    pending0, after_start = None, []
    if scatter0 is not None:
        *pending0, token = _peer_start("exchange", scatter0(dict(w0t=d_w0t, wq=d_wq, wkv=d_wkv)), name="grads0_start")
        after_start = [token]
    grad_x, d_e_g_in = _matmul_rows(
        [(dz0, w0t, False)], [(x, D_MODEL, 0), (dx1, D_MODEL, 0)], [e_g_in] + after_start,
        lambda dy, xt, add, g, *_: _rms_bwd_epilogue(dy, xt, add, g),
        [("rows", D_MODEL, F32), ("sum", (1, D_MODEL))], name="l0_in_dx")

    return dict(pending0=pending0, recv1=recv1, loss=loss_part[0, 0], grad_x=grad_x, e_g_in=d_e_g_in, w0t=d_w0t, e_g_q_a=d_g_q_a, wq=d_wq,
                e_g_kv_a=d_g_kv_a, wkv=d_wkv, e_sinks=d_sinks[:, 0].reshape(1, SWA_HEADS), wo0=d_wo0,
                o_g_in=d_o_g_in, w1t=d_w1t, wft=d_wft, o_b_f=d_bf[:, :FOX_HEADS], wo1=d_wo1, g_final=d_g_final.reshape(D_MODEL))


def _wide(a, rows):
    flat = a.reshape(-1)
    return jnp.pad(flat, (0, rows * WIDE - flat.shape[0])).reshape(rows, WIDE)


def _rows_b0(w_q, w_kv):
    return jnp.concatenate([_wide(w_q, 32), _wide(w_kv, 16)], axis=0)


def _unflat_b0(f):
    return f[0:24].reshape(1, MLA_Q_RANK, 96), f[32:48].reshape(1, MLA_KV_RANK, 128)


def _rows_b1(o_w_out, e_w_out, g_in):
    return jnp.concatenate([o_w_out, e_w_out, _wide(g_in, 16)], axis=0)


def _unflat_b1(f):
    return f[0:128][None], f[128:256][None], f[256:257, :LANES]


def kernel(x, positions, e_g_in, e_w_in, e_g_q_a, e_w_q_up, e_g_kv_a, e_w_kv_up, e_sinks, e_w_out, o_g_in, o_w_in, o_b_f, o_w_out, g_final, loss_target, m_e_g_in, m_e_w_in, m_e_g_q_a, m_e_w_q_up, m_e_g_kv_a, m_e_w_kv_up, m_e_sinks, m_e_w_out, m_o_g_in, m_o_w_in, m_o_b_f, m_o_w_out, m_g_final, v_e_g_in, v_e_w_in, v_e_g_q_a, v_e_w_q_up, v_e_g_kv_a, v_e_w_kv_up, v_e_sinks, v_e_w_out, v_o_g_in, v_o_w_in, v_o_b_f, v_o_w_out, v_g_final):
    def bf(a):
        return a.astype(BF16)

    me = 4 * lax.axis_index("x") + 2 * lax.axis_index("y") + lax.axis_index("c")
    shard0 = jnp.concatenate([_pad_rows(bf(e_w_in[0]).T, RA0), _rows_b0(bf(e_w_q_up[0]), bf(e_w_kv_up[0]))], axis=0)
    *pending_w0, token_w0 = _peer_start("across", shard0, name="weights0_start")

    def unpack0(sent, gath0):
        gath0 = lax.dynamic_update_slice_in_dim(gath0, sent[None], me, axis=0)
        w0t = _layer0_in_weight_t(gath0[:, :N_E_IN].reshape(N_DEV * N_E_IN, WIDE))
        wq = _q_up_weight(_gathered_cols(gath0[:, RA0:RA0 + 24], MLA_Q_RANK))
        wkv = _kv_up_weight(_gathered_cols(gath0[:, RA0 + 32:RA0 + 48], MLA_KV_RANK))
        return w0t, wq, wkv

    rows_b0 = [_rows_b0(q[0], kv[0]) for q, kv in ((e_w_q_up, e_w_kv_up), (m_e_w_q_up, m_e_w_kv_up), (v_e_w_q_up, v_e_w_kv_up))]
    rows_b1 = [_rows_b1(o[0], e[0], g) for o, e, g in ((o_w_out, e_w_out, o_g_in), (m_o_w_out, m_e_w_out, m_o_g_in),
                                                       (v_o_w_out, v_e_w_out, v_o_g_in))]

    g_bits = lax.bitcast_convert_type(o_g_in.reshape(LANES), BF16)
    shard1 = jnp.concatenate([_pad_rows(bf(o_w_in[0]).T, RA1), _rows_b1(bf(o_w_out[0]), bf(e_w_out[0]), g_bits)], axis=0)

    def unpack1(gath1):
        w1t, wft = _layer1_in_weight_t(gath1[:, :N_O_IN].reshape(N_DEV * N_O_IN, WIDE))
        wo1 = gath1[:, RA1:RA1 + 128].reshape(D_MODEL, D_MODEL)
        wo0 = gath1[:, RA1 + 128:RA1 + 256].reshape(D_MODEL, D_MODEL)
        bits = gath1[:, RA1 + 256, :2 * LANES].reshape(N_DEV, LANES, 2)
        return wo0, lax.bitcast_convert_type(bits, F32).reshape(1, D_MODEL), w1t, wft, wo1

    def scatter1(g):
        d_in_t = _layer1_in_grad_t(g["w1t"], g["wft"]).reshape(N_DEV, N_O_IN, WIDE)
        d_o_g = jnp.pad(bf(g["o_g_in"]).reshape(N_DEV, 1, LANES), ((0, 0), (0, 15), (0, WIDE - LANES)))
        return jnp.concatenate([_pad_rows(d_in_t, RA1), g["wo1"].reshape(N_DEV, 128, WIDE),
                                g["wo0"].reshape(N_DEV, 128, WIDE), d_o_g], axis=1)

    def scatter0(g):
        return jnp.concatenate([
            _pad_rows(_layer0_in_grad_t(g["w0t"]).reshape(N_DEV, N_E_IN, WIDE), RA0),
            _pad_rows(_scatter_cols(_q_up_grad(g["wq"])), 32), _scatter_cols(_kv_up_grad(g["wkv"]))], axis=1)

    gr = _local_step(x[0], positions[0], loss_target[0], e_g_in,
                     (pending_w0, token_w0, unpack0, [shard1] + rows_b0 + rows_b1), e_g_q_a, e_g_kv_a, e_sinks,
                     (shard1, unpack1), o_b_f, g_final, scatter1=scatter1, scatter0=scatter0)

    def in_projection(recv, ra, n, w, m, v, name):
        g = _sum8(recv, ra, name=name + "_grad_sum")[:n].reshape(n, 1, D_MODEL)
        w, m, v = [jnp.transpose(a, (2, 0, 1)) for a in (w, m, v)]
        return (g, *_adamw_columns(g, w, m, v, name=name + "_adamw"))

    o_in = in_projection(gr["recv1"], RA1, N_O_IN, o_w_in, m_o_w_in, v_o_w_in, "o_w_in")
    b1 = _adamw(gr["recv1"][:, RA1:], *rows_b1, name="adamw_late")

    small = _small_pack(gr["e_g_in"], gr["g_final"], gr["e_g_q_a"], gr["e_g_kv_a"], gr["e_sinks"], gr["o_b_f"], gr["loss"])
    small_all = _all_gather(small, name="small_all_gather")
    zero = jnp.zeros((), F32)
    w_small = _small_pack(e_g_in, g_final, e_g_q_a, e_g_kv_a, e_sinks, o_b_f, zero)
    m_small = _small_pack(m_e_g_in, m_g_final, m_e_g_q_a, m_e_g_kv_a, m_e_sinks, m_o_b_f, zero)
    v_small = _small_pack(v_e_g_in, v_g_final, v_e_g_q_a, v_e_g_kv_a, v_e_sinks, v_o_b_f, zero)
    smalls = _adamw(small_all, w_small, m_small, v_small, name="adamw_replicated")
    g_sm, d_sm, m_sm, v_sm = [_small_unpack(a) for a in smalls]
    loss = g_sm[6]

    sent0, recv0 = _peer_wait("exchange", *gr["pending0"], after=[o_in[1], b1[1], smalls[1]], name="grads0_wait")
    own = lax.dynamic_slice_in_dim(sent0, me, 1, axis=0)
    recv0 = lax.dynamic_update_slice_in_dim(recv0, own, me, axis=0)
    e_in = in_projection(recv0, RA0, N_E_IN, e_w_in, m_e_w_in, v_e_w_in, "e_w_in")
    b0 = _adamw(recv0[:, RA0:], *rows_b0, name="adamw_early")

    def sharded(k):
        q_up, kv_up = _unflat_b0(b0[k])
        o_out, e_out, o_g = _unflat_b1(b1[k])
        return jnp.transpose(e_in[k], (1, 2, 0)), q_up, kv_up, e_out, jnp.transpose(o_in[k], (1, 2, 0)), o_out, o_g

    g_sh, d_sh, m_sh, v_sh = [sharded(k) for k in range(4)]

    def leaves(sh, sm):
        return (sm[0], sh[0], sm[2], sh[1], sm[3], sh[2], sm[4], sh[3], sh[6], sh[4], sm[5], sh[5], sm[1])

    return (loss, gr["grad_x"][None], *leaves(g_sh, g_sm), *leaves(d_sh, d_sm), *leaves(m_sh, m_sm), *leaves(v_sh, v_sm))
```

```python
import functools

import jax
import jax.numpy as jnp
from jax import lax
from jax.experimental import pallas as pl
from jax.experimental.pallas import tpu as pltpu

F32 = jnp.float32
BF16 = jnp.bfloat16
NEG_INF = float("-inf")

N_DEV = 8
LANES = 128
D_MODEL = 1024
EPS = 1e-6
ROPE_THETA = 10000.0
MLA_HEADS = 8
MLA_Q_RANK = 256
MLA_KV_RANK = 128
MLA_NOPE = 64
MLA_ROPE = 32
MLA_V = 64
SWA_HEADS = 8
SWA_KV_HEADS = 2
SWA_DIM = 64
WINDOW = 128
FOX_HEADS = 16
FOX_DIM = 64

ADAM_LR = 0.001
ADAM_B1 = 0.9
ADAM_B2 = 0.999
ADAM_EPS = 1e-08
ADAM_WD = 0.01
ADAM_STEP = 10

ATT_T = 512
ATT_T_FWD = 1024
VMEM_LIMIT = 56 * 1024 * 1024
MATMUL_B_BLOCK_BYTES = 8 * 1024 * 1024

Z0A_UNITS = 12
Z0B_UNITS = 6

WIDE = 1024
N_E_IN = 276
N_O_IN = 514
RA0 = 288
RB0 = 32 + 16
RA1 = 528
RB1 = 128 + 128 + 16
SMALL_ROWS = 24


def _tile(n, cands):
    for c in cands:
        if n % c == 0:
            return c
    raise ValueError(f"no tile for {n}")


ROW_TILES = (512, 256, 128)


def _params(sem, vmem=VMEM_LIMIT):
    return pltpu.CompilerParams(dimension_semantics=sem, vmem_limit_bytes=vmem)


def _matmul(a, b, *, name, ta=False, tb=False, out_dtype=F32, b_rows=None):
    if ta:
        kdim, m = a.shape[-2], a.shape[-1] * (a.shape[0] if a.ndim == 3 else 1)
    else:
        m, kdim = a.shape
    if tb:
        n, kb = b.shape
    else:
        kb, n = b.shape
    assert kdim == kb, (a.shape, b.shape)
    b_start = 0
    if b_rows is not None:
        assert tb
        b_start, n = b_rows
    tm = _tile(m, (512, 256, 128))
    tn = _tile(n, [c for c in (1024, 768, 512, 384, 256, 128)
                   if c * kdim * b.dtype.itemsize <= MATMUL_B_BLOCK_BYTES and b_start % c == 0])
    assert b_start % tn == 0, (b_start, tn)
    b_off = b_start // tn
    dims = (((0 if ta else 1,), (1 if tb else 0,)), ((), ()))

    def body(a_ref, b_ref, o_ref):
        r = lax.dot_general(a_ref[...].astype(BF16), b_ref[...].astype(BF16), dims, preferred_element_type=F32)
        o_ref[...] = r.astype(out_dtype)

    if a.ndim == 3:
        per = a.shape[2] // tm
        a_spec = pl.BlockSpec((None, kdim, tm), lambda i, j: (i // per, 0, i % per))
    else:
        a_spec = pl.BlockSpec((kdim, tm), lambda i, j: (0, i)) if ta else pl.BlockSpec((tm, kdim), lambda i, j: (i, 0))
    b_spec = pl.BlockSpec((tn, kdim), lambda i, j: (j + b_off, 0)) if tb else pl.BlockSpec((kdim, tn), lambda i, j: (0, j))
    return pl.pallas_call(
        body, name=name, grid=(m // tm, n // tn), in_specs=[a_spec, b_spec],
        out_specs=pl.BlockSpec((tm, tn), lambda i, j: (i, j)), out_shape=jax.ShapeDtypeStruct((m, n), out_dtype),
        compiler_params=_params(("parallel", "parallel")),
    )(a, b)


def _rmsnorm_fwd(x, g, *, width, col_blk, name, after=()):
    s = x.shape[0]
    tm = _tile(s, ROW_TILES)

    def body(x_ref, g_ref, *rest):
        y_ref = rest[-1]
        xf = x_ref[...].astype(F32)
        r = lax.rsqrt(jnp.mean(xf * xf, axis=-1, keepdims=True) + EPS)
        y_ref[...] = ((xf * r) * g_ref[...]).astype(BF16)

    return pl.pallas_call(
        body, name=name, grid=(s // tm,),
        in_specs=[pl.BlockSpec((tm, width), lambda i: (i, col_blk)), pl.BlockSpec((1, width), lambda i: (0, 0))]
        + [ANY] * len(after),
        out_specs=pl.BlockSpec((tm, width), lambda i: (i, 0)),
        out_shape=jax.ShapeDtypeStruct((s, width), BF16),
        compiler_params=_params(("parallel",)),
    )(x, g, *after)


def _rmsnorm_bwd(x, g, dy, *, width, col_blk, name):
    s = x.shape[0]
    tm = _tile(s, ROW_TILES)

    def body(x_ref, g_ref, dy_ref, dx_ref, dg_ref):
        @pl.when(pl.program_id(0) == 0)
        def _():
            dg_ref[...] = jnp.zeros_like(dg_ref)

        dx, dg = _rms_bwd_epilogue(dy_ref[...], x_ref[...], 0.0, g_ref[...])
        dg_ref[...] += dg
        dx_ref[...] = dx.astype(BF16)

    return pl.pallas_call(
        body, name=name, grid=(s // tm,),
        in_specs=[pl.BlockSpec((tm, width), lambda i: (i, col_blk)), pl.BlockSpec((1, width), lambda i: (0, 0)),
                  pl.BlockSpec((tm, width), lambda i: (i, 0))],
        out_specs=[pl.BlockSpec((tm, width), lambda i: (i, 0)), pl.BlockSpec((1, width), lambda i: (0, 0))],
        out_shape=[jax.ShapeDtypeStruct((s, width), BF16), jax.ShapeDtypeStruct((1, width), F32)],
        compiler_params=_params(("arbitrary",)),
    )(x, g, dy)


def _sigmoid(x):
    return 1.0 / (1.0 + jnp.exp(-x))


def _matmul_rows(terms, row_inputs, params, epilogue, outs, *, name, prologue=None, separate=False):
    s = row_inputs[0][0].shape[0] if row_inputs else terms[0][0].shape[-2]
    tm = _tile(s, ROW_TILES)
    steps = s // tm
    n_r, n_p, n_o = len(row_inputs), len(params), len(outs)
    n_t = sum(1 if term[0] is None else 2 for term in terms)

    def body(*refs):
        t_refs, r_refs = list(refs[:n_t]), refs[n_t:n_t + n_r]
        p_refs, o_refs = refs[n_t + n_r:n_t + n_r + n_p], refs[n_t + n_r + n_p:]
        i = pl.program_id(0)
        rows, small = [r[...] for r in r_refs], [p[...] for p in p_refs]
        made = None if prologue is None else prologue(*rows, *small)
        parts = []
        for term in terms:
            a = made if term[0] is None else t_refs.pop(0)[...].astype(BF16)
            dims = (((1,), (1 if term[2] else 0,)), ((), ()))
            parts.append(lax.dot_general(a, t_refs.pop(0)[...].astype(BF16), dims, preferred_element_type=F32))
        acc = parts if separate else sum(parts[1:], parts[0])
        vals = epilogue(acc, *rows, *small) if prologue is None else epilogue(acc, *rows, *small, made)
        for ref, val, out in zip(o_refs, vals, outs):
            if out[0] == "rows":
                ref[...] = val.astype(ref.dtype)
            else:
                @pl.when(i == 0)
                def _(ref=ref):
                    ref[...] = jnp.zeros_like(ref)

                ref[...] += val

    in_specs, args = [], []
    for term in terms:
        a, b = term[0], term[1]
        if a is None:
            in_specs.append(_resident(b.shape, lambda i: (0, 0)))
            args.append(b)
            continue
        b_rows = b.shape[0] if term[2] or len(term) < 4 else a.shape[-1]
        b_blk = 0 if len(term) < 4 else term[3] // b_rows
        if len(term) == 5:
            a_spec = pl.BlockSpec((None, tm, a.shape[2]), lambda i, c=term[4]: (c, i, 0))
        else:
            a_spec = pl.BlockSpec((tm, a.shape[1]), lambda i: (i, 0))
        in_specs += [a_spec, _resident((b_rows, b.shape[1]), lambda i, b_blk=b_blk: (b_blk, 0))]
        args += [a, b]
    for arr, width, col_blk in row_inputs:
        in_specs.append(pl.BlockSpec((tm, width), lambda i, col_blk=col_blk: (i, col_blk)))
        args.append(arr)
    for p in params:
        in_specs.append(pl.BlockSpec(p.shape, lambda i: (0, 0)))
        args.append(p)
    out_specs, out_shape = [], []
    for out in outs:
        if out[0] == "rows":
            out_specs.append(pl.BlockSpec((tm, out[1]), lambda i: (i, 0)))
            out_shape.append(jax.ShapeDtypeStruct((s, out[1]), out[2]))
        else:
            out_specs.append(pl.BlockSpec(out[1], lambda i: (0, 0)))
            out_shape.append(jax.ShapeDtypeStruct(out[1], F32))
    return pl.pallas_call(
        body, name=name, grid=(steps,), in_specs=in_specs, out_specs=out_specs, out_shape=out_shape,
        compiler_params=_params(("arbitrary",)),
    )(*args)


def _rms_stats(x):
    r = lax.rsqrt(jnp.mean(x * x, axis=-1, keepdims=True) + EPS)
    return r, x * r


def _gated(o_parts, gate):
    o = o_parts[0] if len(o_parts) == 1 else jnp.concatenate(o_parts, axis=1)
    return (o * (gate * _sigmoid(gate))).astype(BF16)


def _and_first(vals, *more):
    return (*vals, *more, vals[0])


def _residual_norm_epilogue(r, x, g):
    x1 = x + r
    _, xh = _rms_stats(x1)
    return x1, xh * g


def _rms_bwd_epilogue(dy, x, add, g):
    r, xh = _rms_stats(x)
    dxh = dy * g
    dx = r * (dxh - xh * jnp.mean(dxh * xh, axis=-1, keepdims=True)) + add
    return dx, jnp.sum(dy * xh, axis=0, keepdims=True)


def _loss_epilogue(r, x1, target, g):
    rs, xh = _rms_stats(x1 + r)
    err = xh * g - target
    loss = jnp.broadcast_to(0.5 * jnp.sum(jnp.mean(err * err, axis=-1, keepdims=True)), (8, LANES))
    dy = err * (1.0 / D_MODEL)
    dxh = dy * g
    dx = rs * (dxh - xh * jnp.mean(dxh * xh, axis=-1, keepdims=True))
    return dx, loss, jnp.sum(dy * xh, axis=0, keepdims=True)


def _gate_bwd_epilogue(widths):
    def epilogue(d, *rows):
        o_parts, gt = rows[:-1], rows[-1]
        o = o_parts[0] if len(o_parts) == 1 else jnp.concatenate(o_parts, axis=1)
        sg = _sigmoid(gt)
        do = d * (gt * sg)
        d_gate = d * o * (sg * (1.0 + gt * (1.0 - sg)))
        cuts = [sum(widths[:k]) for k in range(len(widths) + 1)]
        return tuple(do[:, cuts[k]:cuts[k + 1]] for k in range(len(widths))) + (d_gate,)

    return epilogue


def _rot_half(x):
    lane = lax.broadcasted_iota(jnp.int32, x.shape, 1)
    return jnp.where(lane < 80, pltpu.roll(x, LANES - 16, axis=1), pltpu.roll(x, 16, axis=1))


def _rot_half_t(g):
    lane = lax.broadcasted_iota(jnp.int32, g.shape, 1)
    lo = (lane >= MLA_NOPE) & (lane < MLA_NOPE + MLA_ROPE // 2)
    hi = (lane >= MLA_NOPE + MLA_ROPE // 2) & (lane < MLA_NOPE + MLA_ROPE)
    return jnp.where(lo, pltpu.roll(g, LANES - 16, axis=1), jnp.where(hi, pltpu.roll(g, 16, axis=1), 0.0))


def _rope_q_epilogue(q, c, sn):
    heads = [q[:, h * LANES:(h + 1) * LANES] for h in range(MLA_HEADS)]
    return (jnp.concatenate([qh * c + _rot_half(qh) * sn for qh in heads], axis=1),)


def _rope_k_epilogue(kv, kpe, c, sn):
    kpe_r = kpe * c + _rot_half(kpe) * sn
    lane = lax.broadcasted_iota(jnp.int32, kpe.shape, 1)
    heads = [jnp.where(lane < MLA_NOPE, kv[:, h * LANES:(h + 1) * LANES], kpe_r) for h in range(MLA_HEADS)]
    return (jnp.concatenate(heads + [kv[:, MLA_HEADS * LANES:]], axis=1),)


def _rope_bwd(dqm, dkm, dvm, cos_t, sin_t, *, name):
    s = dqm.shape[0]
    tm = _tile(s, ROW_TILES)
    hw = MLA_HEADS * LANES
    vw = MLA_HEADS * MLA_V

    def body(dq_ref, dk_ref, dv_ref, c_ref, s_ref, dqp_ref, dkv_ref, dkpe_ref):
        c = c_ref[...]
        sn = s_ref[...]
        ksum = jnp.zeros((tm, LANES), F32)
        for h in range(MLA_HEADS):
            sl = slice(h * LANES, (h + 1) * LANES)
            dq = dq_ref[:, sl]
            dqp_ref[:, sl] = (dq * c + _rot_half_t(dq * sn)).astype(BF16)
            dk = dk_ref[:, sl]
            dkv_ref[:, sl] = dk.astype(BF16)
            ksum = ksum + dk
        dkv_ref[:, hw:] = dv_ref[...]
        lane = lax.broadcasted_iota(jnp.int32, ksum.shape, 1)
        dkpe = ksum * c + _rot_half_t(ksum * sn)
        dkpe_ref[...] = jnp.where((lane >= MLA_NOPE) & (lane < MLA_NOPE + MLA_ROPE), dkpe, 0.0).astype(BF16)

    return pl.pallas_call(
        body, name=name, grid=(s // tm,),
        in_specs=[pl.BlockSpec((tm, hw), lambda i: (i, 0)), pl.BlockSpec((tm, hw), lambda i: (i, 0)),
                  pl.BlockSpec((tm, vw), lambda i: (i, 0)),
                  pl.BlockSpec((tm, LANES), lambda i: (i, 0)), pl.BlockSpec((tm, LANES), lambda i: (i, 0))],
        out_specs=[pl.BlockSpec((tm, hw), lambda i: (i, 0)), pl.BlockSpec((tm, hw + vw), lambda i: (i, 0)),
                   pl.BlockSpec((tm, LANES), lambda i: (i, 0))],
        out_shape=[jax.ShapeDtypeStruct((s, hw), BF16), jax.ShapeDtypeStruct((s, hw + vw), BF16),
                   jax.ShapeDtypeStruct((s, LANES), BF16)],
        compiler_params=_params(("parallel",)),
    )(dqm, dkm, dvm, cos_t, sin_t)


def _head_mask(shape, a):
    lane = lax.broadcasted_iota(jnp.int32, shape, 1)
    return (lane >= 64 * a) & (lane < 64 * (a + 1))


_NT = (((1,), (1,)), ((), ()))
LOG2E = 1.4426950408889634


def _stack_heads(tile, hw):
    lane = lax.broadcasted_iota(jnp.int32, tile.shape, 1)
    z = jnp.zeros_like(tile)
    return jnp.concatenate([jnp.where(lane < hw, tile, z), jnp.where(lane >= hw, tile, z)], axis=0)


def _stacked_rows(r0, r1, t):
    n = r0.shape[-1]
    return jnp.concatenate([jnp.broadcast_to(r0, (t, n)), jnp.broadcast_to(r1, (t, n))], axis=0)


def _resident(block, index_map):
    return pl.BlockSpec(block, index_map, pipeline_mode=pl.Buffered(1))


def _fwd_tile(s):
    return ATT_T_FWD if s % ATT_T_FWD == 0 else min(ATT_T, s)


def _flash_fwd(q, k, v, bias, *, n_pairs, hw, q_off, k_off, v_off, scale, name, rider=None):
    s = q.shape[0]
    t = _fwd_tile(s)
    nb = s // t
    qw = 2 * hw
    has_bias = bias is not None
    c1 = scale * LOG2E

    def body(*refs):
        refs, ride_refs = _split_rider(refs, rider, n_in=4 if has_bias else 3, n_out=2)
        if has_bias:
            q_ref, k_ref, v_ref, b_ref, o_ref, lse_ref, vt_ref, bcol_ref = refs
        else:
            q_ref, k_ref, v_ref, o_ref, lse_ref, vt_ref = refs
            b_ref = bcol_ref = None
        _ride_start(rider, ride_refs, pl.program_id(0) == 0)
        row = lax.broadcasted_iota(jnp.int32, (t, t), 0)
        col = lax.broadcasted_iota(jnp.int32, (t, t), 1)
        cmask_t = jnp.concatenate([row <= col, row <= col], axis=1)
        lane_lt64 = lax.broadcasted_iota(jnp.int32, (t, LANES), 1) < 64

        def as_column(r):
            return jnp.broadcast_to(r, (8, r.shape[1])).T[:, 0:1]

        def v_block(j, _):
            c0 = pl.multiple_of(j * t, t)
            vt_ref[j] = v_ref[pl.ds(c0, t), :].astype(F32).T.astype(BF16)
            if has_bias:
                for a in range(2):
                    bcol_ref[a, pl.ds(c0, t), :] = as_column(b_ref[0, a, j])
            return 0

        lax.fori_loop(0, nb, v_block, 0)

        def stacked_queries(i):
            return _stack_heads(q_ref[pl.ds(pl.multiple_of(i * t, t), t), :], hw).astype(F32).T.astype(BF16)

        def kv_step(j, carry, qs_t, masked):
            m, l, acc = carry
            rows = pl.ds(pl.multiple_of(j * t, t), t)
            sc = jnp.dot(k_ref[rows, :], qs_t, preferred_element_type=F32) * c1
            if has_bias:
                sc = sc + jnp.concatenate([jnp.broadcast_to(bcol_ref[0, rows, :], (t, t)),
                                           jnp.broadcast_to(bcol_ref[1, rows, :], (t, t))], axis=1)
            if masked:
                sc = jnp.where(cmask_t, sc, NEG_INF)
            m_new = jnp.maximum(m, jnp.max(sc, axis=0, keepdims=True))
            alpha = jnp.exp2(m - m_new)
            p = jnp.exp2(sc - m_new)
            l_new = alpha * l + jnp.sum(p, axis=0, keepdims=True)
            pv = jnp.dot(vt_ref[j], p.astype(BF16), preferred_element_type=F32)
            return m_new, l_new, alpha * acc + pv

        def finish(i, carry):
            m, l, acc = carry
            r0 = pl.multiple_of(i * t, t)
            out = (acc / l).T
            lse2 = as_column(m + jnp.log2(l))
            lse_ref[0, 0, pl.ds(r0, t), :] = lse2[:t]
            lse_ref[0, 1, pl.ds(r0, t), :] = lse2[t:]
            o_ref[pl.ds(r0, t), :] = jnp.where(lane_lt64, out[:t], out[t:])

        init = (jnp.full((1, 2 * t), NEG_INF, F32), jnp.zeros((1, 2 * t), F32), jnp.zeros((LANES, 2 * t), F32))

        def q_block(i, _):
            qs_t = stacked_queries(i)
            carry = lax.fori_loop(0, i, lambda j, c: kv_step(j, c, qs_t, False), init)
            finish(i, kv_step(i, carry, qs_t, True))
            return 0

        lax.fori_loop(0, nb, q_block, 0)
        _ride_wait(rider, ride_refs, pl.program_id(0) == n_pairs - 1)

    in_specs = [_resident((s, qw), lambda p: (0, q_off + p)), _resident((s, qw), lambda p: (0, k_off + p)),
                _resident((s, LANES), lambda p: (0, v_off + p))]
    args = [q, k, v]
    if has_bias:
        in_specs.append(_resident((1, 2, nb, 1, t), lambda p: (p, 0, 0, 0, 0)))
        args.append(bias)
    out_specs = [pl.BlockSpec((s, LANES), lambda p: (0, p)), pl.BlockSpec((1, 2, s, 1), lambda p: (p, 0, 0, 0))]
    out_shape = [jax.ShapeDtypeStruct((s, n_pairs * LANES), F32), jax.ShapeDtypeStruct((n_pairs, 2, s, 1), F32)]
    scratch = [pltpu.VMEM((nb, LANES, t), BF16)] + ([pltpu.VMEM((2, s, 1), F32)] if has_bias else [])
    scratch += _add_rider(rider, in_specs, args, out_specs, out_shape)
    return pl.pallas_call(
        body, name=name, grid=(n_pairs,), in_specs=in_specs, out_specs=out_specs, out_shape=out_shape,
        scratch_shapes=scratch,
        compiler_params=_params(("parallel",) if rider is None else ("arbitrary",)),
    )(*args)


def _flash_bwd(q, k, v, do, o, lse, bias, *, n_pairs, hw, q_off, k_off, v_off, scale, qk_dtype, name, rider=None,
               stacked=False):
    s = q.shape[0]
    t = min(ATT_T, s)
    nb = s // t
    qw = 2 * hw
    has_bias = bias is not None
    c1 = scale * LOG2E

    def body(*refs):
        n_grads = 1 if stacked else 3
        refs, ride_refs = _split_rider(refs, rider, n_in=7 if has_bias else 6, n_out=n_grads + (2 if has_bias else 0))
        if stacked:
            refs = list(refs)
            n_in = 7 if has_bias else 6
            refs[n_in:n_in + 1] = [refs[n_in].at[0], refs[n_in].at[1], refs[n_in].at[2]]
        if has_bias:
            (q_ref, k_ref, v_ref, do_ref, o_ref, lse_ref, b_ref, dq_ref, dk_ref, dv_ref, db_ref, dr_ref,
             dkt_ref, dvt_ref) = refs
            db_ref[...] = jnp.zeros_like(db_ref)
        else:
            q_ref, k_ref, v_ref, do_ref, o_ref, lse_ref, dq_ref, dk_ref, dv_ref, dkt_ref, dvt_ref = refs
            b_ref = db_ref = dr_ref = None
        _ride_start(rider, ride_refs, pl.program_id(0) == 0)
        dkt_ref[...] = jnp.zeros_like(dkt_ref)
        dvt_ref[...] = jnp.zeros_like(dvt_ref)
        causal = lax.broadcasted_iota(jnp.int32, (t, t), 1) <= lax.broadcasted_iota(jnp.int32, (t, t), 0)
        cmask = jnp.concatenate([causal, causal], axis=0)
        lane_lt_hw = lax.broadcasted_iota(jnp.int32, (t, qw), 1) < hw

        def q_block(i, _):
            r0 = pl.multiple_of(i * t, t)
            qs = _stack_heads(q_ref[pl.ds(r0, t), :], hw)
            dos = _stack_heads(do_ref[pl.ds(r0, t), :], 64)
            ot = o_ref[pl.ds(r0, t), :]
            delta = jnp.sum(dos * jnp.concatenate([ot, ot], axis=0), axis=-1, keepdims=True)
            lse2 = jnp.concatenate([lse_ref[0, 0, pl.ds(r0, t), :], lse_ref[0, 1, pl.ds(r0, t), :]], axis=0)
            dosb = dos.astype(BF16)
            dos_t = dos.T.astype(BF16)
            qs_t = qs.astype(F32).T.astype(BF16)

            def kv_step(j, carry, masked):
                dq, rsum = carry
                c0 = pl.multiple_of(j * t, t)
                kt = k_ref[pl.ds(c0, t), :]
                vt = v_ref[pl.ds(c0, t), :]
                sc = lax.dot_general(qs, kt, _NT, preferred_element_type=F32) * c1
                if has_bias:
                    sc = sc + _stacked_rows(b_ref[0, 0, j], b_ref[0, 1, j], t)
                if masked:
                    sc = jnp.where(cmask, sc, NEG_INF)
                p = jnp.exp2(sc - lse2)
                dp = lax.dot_general(dosb, vt, _NT, preferred_element_type=F32)
                ds = p * (dp - delta)
                dsb = ds.astype(BF16)
                pb = p.astype(BF16)
                if hw == LANES:
                    dvt_ref[j] += jnp.concatenate(
                        [jnp.dot(dos_t[:64, :t], pb[:t], preferred_element_type=F32),
                         jnp.dot(dos_t[64:, t:], pb[t:], preferred_element_type=F32)], axis=0)
                    dkt_ref[j] += jnp.concatenate(
                        [jnp.dot(qs_t[:hw, :t], dsb[:t], preferred_element_type=F32),
                         jnp.dot(qs_t[hw:, t:], dsb[t:], preferred_element_type=F32)], axis=0)
                else:
                    dvt_ref[j] += jnp.dot(dos_t, pb, preferred_element_type=F32)
                    dkt_ref[j] += jnp.dot(qs_t, dsb, preferred_element_type=F32)
                if has_bias:
                    db_ref[0, 0, j] += jnp.sum(ds[:t], axis=0, keepdims=True)
                    db_ref[0, 1, j] += jnp.sum(ds[t:], axis=0, keepdims=True)
                    rsum = rsum + jnp.sum(ds, axis=-1, keepdims=True)
                return dq + jnp.dot(dsb, kt, preferred_element_type=F32), rsum

            init = (jnp.zeros((2 * t, qw), F32), jnp.zeros((2 * t, 1), F32))
            carry = lax.fori_loop(0, i, functools.partial(kv_step, masked=False), init)
            dq, rsum = kv_step(i, carry, True)
            dq = dq * scale
            dq_ref[pl.ds(r0, t), :] = jnp.where(lane_lt_hw, dq[:t], dq[t:]).astype(qk_dtype)
            if has_bias:
                rsum_row = jnp.broadcast_to(rsum, (2 * t, LANES)).T[0:1]
                dr_ref[0, 0, i] = rsum_row[:, :t]
                dr_ref[0, 1, i] = rsum_row[:, t:]
            return 0

        lax.fori_loop(0, nb, q_block, 0)

        def k_block(j, _):
            c0 = pl.multiple_of(j * t, t)
            dk_ref[pl.ds(c0, t), :] = (dkt_ref[j].T * scale).astype(qk_dtype)
            dv_ref[pl.ds(c0, t), :] = dvt_ref[j].T.astype(BF16)
            return 0

        lax.fori_loop(0, nb, k_block, 0)
        _ride_wait(rider, ride_refs, pl.program_id(0) == n_pairs - 1)

    in_specs = [_resident((s, qw), lambda p: (0, q_off + p)), _resident((s, qw), lambda p: (0, k_off + p)),
                _resident((s, LANES), lambda p: (0, v_off + p)),
                _resident((s, LANES), lambda p: (0, p)), _resident((s, LANES), lambda p: (0, p)),
                _resident((1, 2, s, 1), lambda p: (p, 0, 0, 0))]
    args = [q, k, v, do, o, lse]
    if stacked:
        assert qw == LANES and qk_dtype == BF16
        out_specs = [pl.BlockSpec((3, s, LANES), lambda p: (0, 0, p))]
        out_shape = [jax.ShapeDtypeStruct((3, s, n_pairs * LANES), BF16)]
    else:
        out_specs = [pl.BlockSpec((s, qw), lambda p: (0, p)), pl.BlockSpec((s, qw), lambda p: (0, p)),
                     pl.BlockSpec((s, LANES), lambda p: (0, p))]
        out_shape = [jax.ShapeDtypeStruct((s, n_pairs * qw), qk_dtype), jax.ShapeDtypeStruct((s, n_pairs * qw), qk_dtype),
                     jax.ShapeDtypeStruct((s, n_pairs * LANES), BF16)]
    if has_bias:
        in_specs.append(_resident((1, 2, nb, 1, t), lambda p: (p, 0, 0, 0, 0)))
        args.append(bias)
        for _ in range(2):
            out_specs.append(pl.BlockSpec((1, 2, nb, 1, t), lambda p: (p, 0, 0, 0, 0)))
            out_shape.append(jax.ShapeDtypeStruct((n_pairs, 2, nb, 1, t), F32))
    scratch = [pltpu.VMEM((nb, qw, t), F32), pltpu.VMEM((nb, LANES, t), F32)]
    scratch += _add_rider(rider, in_specs, args, out_specs, out_shape)
    return pl.pallas_call(
        body, name=name, grid=(n_pairs,), in_specs=in_specs, out_specs=out_specs, out_shape=out_shape,
        scratch_shapes=scratch,
        compiler_params=_params(("parallel",) if rider is None else ("arbitrary",)),
    )(*args)


def _alibi_slope(h):
    return 2.0 ** (-8.0 * (h + 1.0) / SWA_HEADS)


SWA_ROWS = 512
SWA_SCALE = SWA_DIM ** -0.5


def _swa_geometry(i):
    w = WINDOW
    r0 = pl.multiple_of(i * w, w)
    b0 = pl.multiple_of(jnp.maximum(i - 1, 0) * w, w)
    row = lax.broadcasted_iota(jnp.int32, (w, 2 * w), 0)
    col = lax.broadcasted_iota(jnp.int32, (w, 2 * w), 1)
    dist = row - col + (r0 - b0)
    valid = (dist >= 0) & (dist < w)
    return r0, b0, dist.astype(F32), valid


def _swa_q_head(qblk, h):
    kv = h // (SWA_HEADS // SWA_KV_HEADS)
    if h % 2 != kv:
        qblk = pltpu.roll(qblk, 64, axis=1)
    return jnp.where(_head_mask(qblk.shape, kv), qblk, 0.0)


SWA_GROUP = SWA_HEADS // SWA_KV_HEADS


def _swa_stack(ref, rs, grp):
    parts = []
    for a in range(SWA_GROUP):
        h = SWA_GROUP * grp + a
        parts.append(_swa_q_head(ref[rs, (h // 2) * LANES:(h // 2 + 1) * LANES].astype(F32), h))
    return jnp.concatenate(parts, axis=0)


def _swa_unstack(x, grp):
    tiles = []
    for a in range(SWA_GROUP):
        h = SWA_GROUP * grp + a
        tile = x[a * WINDOW:(a + 1) * WINDOW]
        tiles.append(pltpu.roll(tile, 64, axis=1) if h % 2 != grp else tile)
    return tiles


def _swa_head_column(vals):
    return jnp.concatenate([jnp.full((WINDOW, 1), v, F32) for v in vals], axis=0)


def _swa_logits(qs, kb, dist, valid, grp):
    slopes = _swa_head_column([_alibi_slope(SWA_GROUP * grp + a) for a in range(SWA_GROUP)])
    dist4 = jnp.concatenate([dist] * SWA_GROUP, axis=0)
    valid4 = jnp.concatenate([valid] * SWA_GROUP, axis=0)
    sc = lax.dot_general(qs, kb, _NT, preferred_element_type=F32) * SWA_SCALE - slopes * dist4
    return jnp.where(valid4, sc, NEG_INF)


def _swa_merge_heads(tiles):
    lt64 = lax.broadcasted_iota(jnp.int32, (WINDOW, LANES), 1) < 64
    return jnp.concatenate([jnp.where(lt64, tiles[2 * b], tiles[2 * b + 1]) for b in range(SWA_HEADS // 2)], axis=1)


def _swa_fwd(z0b, sinks, *, name):
    s = z0b.shape[0]
    w = WINDOW
    rows = min(SWA_ROWS, s)
    per_step = rows // w
    qcols = SWA_HEADS * SWA_DIM

    def body(sink_ref, q_ref, k_ref, v_ref, o_ref, lse_ref):
        g = pl.program_id(0)
        for ii in range(per_step):
            rs = slice(ii * w, (ii + 1) * w)
            r0, b0, dist, valid = _swa_geometry(g * per_step + ii)
            kb = k_ref[pl.ds(b0, 2 * w), :]
            vb = v_ref[pl.ds(b0, 2 * w), :]
            o_tiles = []
            for h in range(SWA_HEADS):
                kv = h // SWA_GROUP
                qh = _swa_q_head(q_ref[rs, (h // 2) * LANES:(h // 2 + 1) * LANES].astype(F32), h).astype(BF16)
                sc = lax.dot_general(qh, kb, _NT, preferred_element_type=F32) * SWA_SCALE - _alibi_slope(h) * dist
                sc = jnp.where(valid, sc, NEG_INF)
                sink = sink_ref[0, h]
                m = jnp.maximum(jnp.max(sc, axis=-1, keepdims=True), sink)
                p = jnp.exp(sc - m)
                l = jnp.sum(p, axis=-1, keepdims=True) + jnp.exp(sink - m)
                oh = jnp.dot(p.astype(BF16), vb, preferred_element_type=F32) / l
                o_tiles.append(pltpu.roll(oh, 64, axis=1) if h % 2 != kv else oh)
                lse_ref[h, rs, :] = m + jnp.log(l)
            o_ref[rs, :] = _swa_merge_heads(o_tiles)

    return pl.pallas_call(
        body, name=name, grid=(s // rows,),
        in_specs=[pl.BlockSpec(memory_space=pltpu.SMEM),
                  pl.BlockSpec((rows, qcols), lambda g: (g, 0)),
                  pl.BlockSpec((s, LANES), lambda g: (0, 4)), pl.BlockSpec((s, LANES), lambda g: (0, 5))],
        out_specs=[pl.BlockSpec((rows, qcols), lambda g: (g, 0)), pl.BlockSpec((SWA_HEADS, rows, 1), lambda g: (0, g, 0))],
        out_shape=[jax.ShapeDtypeStruct((s, qcols), F32), jax.ShapeDtypeStruct((SWA_HEADS, s, 1), F32)],
        compiler_params=_params(("parallel",)),
    )(sinks, z0b, z0b, z0b)


def _swa_bwd(z0b, sinks, do, o, lse, *, name):
    s = z0b.shape[0]
    w = WINDOW
    rows = min(SWA_ROWS, s)
    per_step = rows // w
    qcols = SWA_HEADS * SWA_DIM
    nblk = s // w

    def body(sink_ref, q_ref, k_ref, v_ref, do_ref, o_ref, lse_ref, dq_ref, dkt_ref, dvt_ref, dsink_ref):
        g = pl.program_id(0)

        @pl.when(g == 0)
        def _():
            dkt_ref[...] = jnp.zeros_like(dkt_ref)
            dvt_ref[...] = jnp.zeros_like(dvt_ref)
            dsink_ref[...] = jnp.zeros_like(dsink_ref)

        for ii in range(per_step):
            i = g * per_step + ii
            rs = slice(ii * w, (ii + 1) * w)
            r0, b0, dist, valid = _swa_geometry(i)
            j0 = jnp.maximum(i - 1, 0)
            kb = k_ref[pl.ds(b0, 2 * w), :]
            vb = v_ref[pl.ds(b0, 2 * w), :]
            dq_tiles = []
            for grp in range(SWA_KV_HEADS):
                heads = [SWA_GROUP * grp + a for a in range(SWA_GROUP)]
                qs32 = _swa_stack(q_ref, rs, grp)
                dos32 = _swa_stack(do_ref, rs, grp)
                delta = jnp.sum(dos32 * _swa_stack(o_ref, rs, grp), axis=-1, keepdims=True)
                lse = jnp.concatenate([lse_ref[h, rs, :] for h in heads], axis=0)
                sink = _swa_head_column([sink_ref[0, h] for h in heads])
                p = jnp.exp(_swa_logits(qs32.astype(BF16), kb, dist, valid, grp) - lse)
                dp = lax.dot_general(dos32.astype(BF16), vb, _NT, preferred_element_type=F32)
                ds = p * (dp - delta)
                dsb = ds.astype(BF16)
                d_sink = jnp.exp(sink - lse) * delta
                for a, h in enumerate(heads):
                    dsink_ref[h:h + 1, :] += jnp.broadcast_to(-jnp.sum(d_sink[a * w:(a + 1) * w]), (1, LANES))
                dvt = jnp.dot(dos32.T.astype(BF16), p.astype(BF16), preferred_element_type=F32)
                dkt = jnp.dot(qs32.T.astype(BF16), dsb, preferred_element_type=F32) * SWA_SCALE
                dvt_ref[j0] += dvt[:, :w]
                dvt_ref[j0 + 1] += dvt[:, w:]
                dkt_ref[j0] += dkt[:, :w]
                dkt_ref[j0 + 1] += dkt[:, w:]
                dq_tiles += _swa_unstack(jnp.dot(dsb, kb, preferred_element_type=F32) * SWA_SCALE, grp)
            dq_ref[rs, :] = _swa_merge_heads(dq_tiles)

    return pl.pallas_call(
        body, name=name, grid=(s // rows,),
        in_specs=[pl.BlockSpec(memory_space=pltpu.SMEM),
                  pl.BlockSpec((rows, qcols), lambda g: (g, 0)),
                  pl.BlockSpec((s, LANES), lambda g: (0, 4)), pl.BlockSpec((s, LANES), lambda g: (0, 5)),
                  pl.BlockSpec((rows, qcols), lambda g: (g, 0)), pl.BlockSpec((rows, qcols), lambda g: (g, 0)),
                  pl.BlockSpec((SWA_HEADS, rows, 1), lambda g: (0, g, 0))],
        out_specs=[pl.BlockSpec((rows, qcols), lambda g: (g, 0)),
                   pl.BlockSpec((nblk, LANES, w), lambda g: (0, 0, 0)),
                   pl.BlockSpec((nblk, LANES, w), lambda g: (0, 0, 0)),
                   pl.BlockSpec((SWA_HEADS, LANES), lambda g: (0, 0))],
        out_shape=[jax.ShapeDtypeStruct((s, qcols), F32),
                   jax.ShapeDtypeStruct((nblk, LANES, w), F32), jax.ShapeDtypeStruct((nblk, LANES, w), F32),
                   jax.ShapeDtypeStruct((SWA_HEADS, LANES), F32)],
        compiler_params=_params(("arbitrary",)),
    )(sinks, z0b, z0b, z0b, do, o, lse)


CUM_T = 256


def _split3(x):
    hi = x.astype(BF16)
    r1 = x - hi.astype(F32)
    mid = r1.astype(BF16)
    lo = (r1 - mid.astype(F32)).astype(BF16)
    return hi, mid, lo


def _tri_dot(tri, x):
    hi, mid, lo = _split3(x)
    out = jnp.dot(tri, hi, preferred_element_type=F32)
    out = out + jnp.dot(tri, mid, preferred_element_type=F32)
    return out + jnp.dot(tri, lo, preferred_element_type=F32)


def _logf_fwd(zf, bf, *, name):
    s = zf.shape[0]
    t = CUM_T
    nb = s // t

    def body(z_ref, b_ref, c_ref, carry_ref):
        i = pl.program_id(0)

        @pl.when(i == 0)
        def _():
            carry_ref[...] = jnp.zeros_like(carry_ref)

        x = z_ref[...] + b_ref[...]
        lf = jnp.minimum(x, 0.0) - jnp.log(1.0 + jnp.exp(-jnp.abs(x)))
        row = lax.broadcasted_iota(jnp.int32, (t, t), 0)
        col = lax.broadcasted_iota(jnp.int32, (t, t), 1)
        tri = jnp.where(col <= row, 1.0, 0.0).astype(BF16)
        c = _tri_dot(tri, lf) + carry_ref[...]
        c_ref[...] = c
        carry_ref[...] = c[t - 1:t, :]

    return pl.pallas_call(
        body, name=name, grid=(nb,),
        in_specs=[pl.BlockSpec((t, LANES), lambda i: (i, 0)), pl.BlockSpec((1, LANES), lambda i: (0, 0))],
        out_specs=pl.BlockSpec((t, LANES), lambda i: (i, 0)),
        out_shape=jax.ShapeDtypeStruct((s, LANES), F32),
        scratch_shapes=[pltpu.VMEM((1, LANES), F32)],
        compiler_params=_params(("arbitrary",)),
    )(zf, bf)


def _logf_bwd(dc, zf, bf, *, name):
    s = zf.shape[0]
    t = CUM_T
    nb = s // t

    def body(dc_ref, z_ref, b_ref, dz_ref, db_ref, carry_ref):
        i = pl.program_id(0)

        @pl.when(i == 0)
        def _():
            carry_ref[...] = jnp.zeros_like(carry_ref)
            db_ref[...] = jnp.zeros_like(db_ref)

        row = lax.broadcasted_iota(jnp.int32, (t, t), 0)
        col = lax.broadcasted_iota(jnp.int32, (t, t), 1)
        tri = jnp.where(col >= row, 1.0, 0.0).astype(BF16)
        dlf = _tri_dot(tri, dc_ref[...]) + carry_ref[...]
        carry_ref[...] = dlf[0:1, :]
        x = z_ref[...] + b_ref[...]
        dz = dlf * _sigmoid(-x)
        dz_ref[...] = dz.astype(BF16)
        db_ref[...] += jnp.sum(dz, axis=0, keepdims=True)

    return pl.pallas_call(
        body, name=name, grid=(nb,),
        in_specs=[pl.BlockSpec((t, LANES), lambda i: (nb - 1 - i, 0)), pl.BlockSpec((t, LANES), lambda i: (nb - 1 - i, 0)),
                  pl.BlockSpec((1, LANES), lambda i: (0, 0))],
        out_specs=[pl.BlockSpec((t, LANES), lambda i: (nb - 1 - i, 0)), pl.BlockSpec((1, LANES), lambda i: (0, 0))],
        out_shape=[jax.ShapeDtypeStruct((s, LANES), BF16), jax.ShapeDtypeStruct((1, LANES), F32)],
        scratch_shapes=[pltpu.VMEM((1, LANES), F32)],
        compiler_params=_params(("arbitrary",)),
    )(dc, zf, bf)


def _sum_pieces(p_ref):
    g = p_ref[0].astype(F32)
    for k in range(1, N_DEV):
        g = g + p_ref[k].astype(F32)
    return g


def _adam_update(g, w, m, v):
    bc1 = 1.0 - ADAM_B1 ** ADAM_STEP
    bc2 = 1.0 - ADAM_B2 ** ADAM_STEP
    nm = ADAM_B1 * m + (1.0 - ADAM_B1) * g
    nv = ADAM_B2 * v + (1.0 - ADAM_B2) * (g * g)
    m_hat = nm / bc1
    v_hat = nv / bc2
    return -ADAM_LR * (m_hat / (jnp.sqrt(v_hat) + ADAM_EPS) + ADAM_WD * w), nm, nv


def _adamw(pieces, w, m, v, *, name):
    rows, cols = w.shape
    tr = _tile(rows, (RB1, RB0, SMALL_ROWS))

    def body(p_ref, w_ref, m_ref, v_ref, g_ref, d_ref, nm_ref, nv_ref):
        g = _sum_pieces(p_ref)
        g_ref[...] = g
        d_ref[...], nm_ref[...], nv_ref[...] = _adam_update(g, w_ref[...], m_ref[...], v_ref[...])

    spec = pl.BlockSpec((tr, cols), lambda i: (i, 0))
    shape = jax.ShapeDtypeStruct((rows, cols), F32)
    return pl.pallas_call(
        body, name=name, grid=(rows // tr,),
        in_specs=[pl.BlockSpec((N_DEV, tr, cols), lambda i: (0, i, 0)), spec, spec, spec],
        out_specs=[spec, spec, spec, spec], out_shape=[shape, shape, shape, shape],
        compiler_params=_params(("parallel",)),
    )(pieces, w, m, v)


def _sum8(pieces, rows, *, name):
    cols = pieces.shape[2]
    tr = _tile(rows, (176, 96))

    def body(p_ref, g_ref):
        g_ref[...] = _sum_pieces(p_ref)

    return pl.pallas_call(
        body, name=name, grid=(rows // tr,),
        in_specs=[pl.BlockSpec((N_DEV, tr, cols), lambda i: (0, i, 0))],
        out_specs=pl.BlockSpec((tr, cols), lambda i: (i, 0)),
        out_shape=jax.ShapeDtypeStruct((rows, cols), F32),
        compiler_params=_params(("parallel",)),
    )(pieces)


def _adamw_columns(g, w, m, v, *, name):
    n, _, k = w.shape
    tr = n // 2

    def body(g_ref, w_ref, m_ref, v_ref, d_ref, nm_ref, nv_ref):
        d_ref[...], nm_ref[...], nv_ref[...] = _adam_update(g_ref[...], w_ref[...], m_ref[...], v_ref[...])

    spec = pl.BlockSpec((tr, 1, k), lambda i: (i, 0, 0))
    shape = jax.ShapeDtypeStruct((n, 1, k), F32)
    return pl.pallas_call(
        body, name=name, grid=(n // tr,), in_specs=[spec, spec, spec, spec],
        out_specs=[spec, spec, spec], out_shape=[shape, shape, shape],
        compiler_params=_params(("parallel",)),
    )(g, w, m, v)


MESH = pl.DeviceIdType.MESH
ANY = pl.BlockSpec(memory_space=pl.ANY)


def _all_gather(shard, *, name):
    rows, lanes = shard.shape

    def body(x_ref, out_ref, send_sems, recv_sems, local_sem):
        x, y, c = lax.axis_index("x"), lax.axis_index("y"), lax.axis_index("c")
        me, sibling = (x, y, c), (x, y, 1 - c)
        chips = [(1 - x, y), (x, 1 - y), (1 - x, 1 - y)]

        def block(px, py, pc):
            return out_ref.at[4 * px + 2 * py + pc]

        def copy(k, blk, to, src=None):
            return pltpu.make_async_remote_copy(
                src_ref=block(*blk) if src is None else src, dst_ref=block(*blk),
                send_sem=send_sems.at[k], recv_sem=recv_sems.at[k], device_id=to, device_id_type=MESH)

        mine = pltpu.make_async_copy(x_ref, block(*me), local_sem)
        mine.start()
        first = [copy(0, me, sibling, src=x_ref)]
        first += [copy(1 + j, me, (*chip, c), src=x_ref) for j, chip in enumerate(chips)]
        for cp in first:
            cp.start()
        passed = [copy(4 + j, (*chip, c), sibling) for j, chip in enumerate(chips)]
        for j, chip in enumerate(chips):
            copy(1 + j, (*chip, c), me).wait_recv()
            passed[j].start()
        copy(0, sibling, me).wait_recv()
        for j, chip in enumerate(chips):
            copy(4 + j, (*chip, 1 - c), me).wait_recv()
        for cp in first + passed:
            cp.wait_send()
        mine.wait()

    return pl.pallas_call(
        body, name=name, out_shape=jax.ShapeDtypeStruct((N_DEV, rows, lanes), shard.dtype),
        in_specs=[ANY], out_specs=ANY,
        scratch_shapes=[pltpu.SemaphoreType.DMA((7,)), pltpu.SemaphoreType.DMA((7,)), pltpu.SemaphoreType.DMA(())],
    )(shard)


def _peer_copies(kind, src_ref, out_ref, send_sems, recv_sems, local_sem):
    x, y, c = lax.axis_index("x"), lax.axis_index("y"), lax.axis_index("c")
    me = 4 * x + 2 * y + c

    def src(idx):
        return src_ref.at[idx] if kind == "exchange" else src_ref

    mine = None if local_sem is None else pltpu.make_async_copy(src(me), out_ref.at[me], local_sem)
    copies = []
    for r in (2, 4, 6) if kind == "across" else range(1, N_DEV):
        px = 1 - x if r & 4 else x
        py = 1 - y if r & 2 else y
        pc = 1 - c if r & 1 else c
        copies.append(pltpu.make_async_remote_copy(
            src_ref=src(4 * px + 2 * py + pc), dst_ref=out_ref.at[me],
            send_sem=send_sems.at[r - 1], recv_sem=recv_sems.at[r - 1],
            device_id=(px, py, pc), device_id_type=MESH))
    return mine, copies


def _to_other_core(shard, land, *, name):
    def body(src_ref, land_ref, out_ref, send_sems, recv_sems):
        x, y, c = lax.axis_index("x"), lax.axis_index("y"), lax.axis_index("c")
        copies = []
        for k, r in enumerate((0, 2, 4, 6)):
            slot = 4 * (1 - x if r & 4 else x) + 2 * (1 - y if r & 2 else y) + c
            copies.append(pltpu.make_async_remote_copy(
                src_ref=src_ref if r == 0 else land_ref.at[slot], dst_ref=out_ref.at[slot],
                send_sem=send_sems.at[k], recv_sem=recv_sems.at[k], device_id=(x, y, 1 - c), device_id_type=MESH))
        for cp in copies:
            cp.start()
        for cp in copies:
            cp.wait()

    return pl.pallas_call(
        body, name=name, out_shape=jax.ShapeDtypeStruct(land.shape, land.dtype), in_specs=[ANY, ANY], out_specs=ANY,
        input_output_aliases={1: 0}, scratch_shapes=[pltpu.SemaphoreType.DMA((4,)), pltpu.SemaphoreType.DMA((4,))],
    )(shard, land)


PEER_SEMS = [pltpu.SemaphoreType.DMA((7,)), pltpu.SemaphoreType.DMA((7,)), pltpu.SemaphoreType.DMA(())]


HBM = pl.BlockSpec(memory_space=pltpu.HBM)
SEMAPHORES = pl.BlockSpec(memory_space=pltpu.SEMAPHORE)


def _peer_start(kind, arr, *, name):
    land = lax.empty((N_DEV,) + arr.shape[-2:], arr.dtype)

    def body(src_ref, land_ref, send_sems, recv_sems, src_thru, land_thru, token):
        _, copies = _peer_copies(kind, src_ref, land_ref, send_sems, recv_sems, None)
        for cp in copies:
            cp.start()
        token[...] = jnp.zeros_like(token)

    return pl.pallas_call(
        body, name=name,
        out_shape=(pltpu.SemaphoreType.DMA((N_DEV - 1,)), pltpu.SemaphoreType.DMA((N_DEV - 1,)),
                   pltpu.HBM(arr.shape, arr.dtype), pltpu.HBM(land.shape, land.dtype), jax.ShapeDtypeStruct((8, LANES), F32)),
        in_specs=(HBM, HBM), out_specs=(SEMAPHORES, SEMAPHORES, HBM, HBM, pl.BlockSpec(memory_space=pltpu.VMEM)),
        input_output_aliases={0: 2, 1: 3},
        compiler_params=pltpu.CompilerParams(has_side_effects=pltpu.SideEffectType.DATAFLOW_SIDE_EFFECTING),
    )(pltpu.with_memory_space_constraint(arr, pltpu.HBM), pltpu.with_memory_space_constraint(land, pltpu.HBM))


def _peer_wait(kind, send_sems, recv_sems, src_thru, land_thru, after, *, name):
    def body(src_ref, land_ref, send_sems, recv_sems, *_):
        _, copies = _peer_copies(kind, src_ref, land_ref, send_sems, recv_sems, None)
        for cp in copies:
            cp.wait_send()
            cp.wait_recv()

    return pl.pallas_call(
        body, name=name,
        out_shape=(pltpu.HBM(src_thru.shape, src_thru.dtype), pltpu.HBM(land_thru.shape, land_thru.dtype)),
        in_specs=(HBM, HBM, SEMAPHORES, SEMAPHORES) + (ANY,) * len(after), out_specs=(HBM, HBM),
        input_output_aliases={0: 0, 1: 1},
        compiler_params=pltpu.CompilerParams(has_side_effects=pltpu.SideEffectType.DATAFLOW_SIDE_EFFECTING),
    )(src_thru, land_thru, send_sems, recv_sems, *after)


def _add_rider(rider, in_specs, args, out_specs, out_shape):
    if rider is None:
        return []
    _, arr = rider
    in_specs.append(ANY)
    args.append(arr)
    out_specs.append(ANY)
    out_shape.append(jax.ShapeDtypeStruct((N_DEV,) + arr.shape[-2:], arr.dtype))
    return list(PEER_SEMS)


def _split_rider(refs, rider, n_in, n_out):
    if rider is None:
        return refs, None
    refs = list(refs)
    rin = refs.pop(n_in)
    rout = refs.pop(n_in + n_out)
    return refs[:-3], (rin, rout, *refs[-3:])


def _ride_start(rider, ride_refs, first):
    if rider is None:
        return

    @pl.when(first)
    def _():
        mine, copies = _peer_copies(rider[0], *ride_refs)
        mine.start()
        for cp in copies:
            cp.start()


def _ride_wait(rider, ride_refs, last):
    if rider is None:
        return

    @pl.when(last)
    def _():
        mine, copies = _peer_copies(rider[0], *ride_refs)
        for cp in copies:
            cp.wait()
        mine.wait()


def _gathered_cols(blocks, kdim):
    n = blocks.shape[1] * WIDE // kdim
    return blocks.reshape(N_DEV, kdim, n).transpose(1, 0, 2).reshape(kdim, N_DEV * n)


def _scatter_cols(dw):
    kdim, n8 = dw.shape
    n = n8 // N_DEV
    return dw.reshape(kdim, N_DEV, n).transpose(1, 0, 2).reshape(N_DEV, kdim * n // WIDE, WIDE)


def _pad_rows(a, rows):
    pad = [(0, 0)] * a.ndim
    pad[-2] = (0, rows - a.shape[-2])
    return jnp.pad(a, pad)


def _layer0_in_weight_t(wt):
    cq, ckv, kpe = wt[0:256], wt[256:384], wt[384:416]
    q_s, k_s, v_s, gate = wt[416:928], wt[928:1056], wt[1056:1184], wt[1184:2208]
    z = jnp.zeros((64, wt.shape[1]), wt.dtype)
    return jnp.concatenate([gate, cq, ckv, z, kpe, z[:32], q_s, k_s, v_s], axis=0)


def _layer0_in_grad_t(dwt):
    gate, cq, ckv, kpe = dwt[0:1024], dwt[1024:1280], dwt[1280:1408], dwt[1472:1504]
    q_s, k_s, v_s = dwt[1536:2048], dwt[2048:2176], dwt[2176:2304]
    return jnp.concatenate([cq, ckv, kpe, q_s, k_s, v_s, gate], axis=0)


def _layer1_in_weight_t(wt):
    main = jnp.concatenate([wt[:3 * D_MODEL], wt[3 * D_MODEL + FOX_HEADS:]], axis=0)
    return main, _pad_rows(wt[3 * D_MODEL:3 * D_MODEL + FOX_HEADS], LANES)


def _layer1_in_unpack(gath, *, name):
    n_main = 3 * D_MODEL

    def body(g_ref, w_ref, f_ref):
        f_ref[...] = jnp.zeros_like(f_ref)
        for p in range(N_DEV):
            lo, hi = p * N_O_IN, (p + 1) * N_O_IN
            for ref, first, start, stop in ((w_ref, 0, lo, min(hi, n_main)),
                                            (f_ref, -n_main, max(lo, n_main), min(hi, n_main + FOX_HEADS)),
                                            (w_ref, -FOX_HEADS, max(lo, n_main + FOX_HEADS), hi)):
                if start < stop:
                    ref[start + first:stop + first, :] = g_ref[p, start - lo:stop - lo, :]

    return pl.pallas_call(
        body, name=name, grid=(1,), in_specs=[_resident((N_DEV, RA1, WIDE), lambda i: (0, 0, 0))],
        out_specs=[_resident((n_main + D_MODEL, WIDE), lambda i: (0, 0)), _resident((LANES, WIDE), lambda i: (0, 0))],
        out_shape=[jax.ShapeDtypeStruct((n_main + D_MODEL, WIDE), gath.dtype), jax.ShapeDtypeStruct((LANES, WIDE), gath.dtype)],
        compiler_params=_params(("arbitrary",)),
    )(gath)


def _late_grads_pack(d_qkv, d_wft, d_gate, d_wo1, d_wo0, d_o_g, *, name):
    n_main = 3 * D_MODEL
    arrays = (d_qkv, d_wft, d_gate, d_wo1, d_wo0, d_o_g)

    def body(q_ref, f_ref, g_ref, o1_ref, o0_ref, og_ref, out_ref):
        for p in range(N_DEV):
            lo, hi = p * N_O_IN, (p + 1) * N_O_IN
            for ref, first, start, stop in ((q_ref, 0, lo, min(hi, n_main)),
                                            (f_ref, -n_main, max(lo, n_main), min(hi, n_main + FOX_HEADS)),
                                            (g_ref, -n_main - FOX_HEADS, max(lo, n_main + FOX_HEADS), hi)):
                if start < stop:
                    out_ref[p, start - lo:stop - lo, :] = ref[start + first:stop + first, :]
            out_ref[p, N_O_IN:RA1, :] = jnp.zeros((RA1 - N_O_IN, WIDE), out_ref.dtype)
            out_ref[p, RA1:RA1 + 128, :] = o1_ref[128 * p:128 * p + 128, :]
            out_ref[p, RA1 + 128:RA1 + 256, :] = o0_ref[128 * p:128 * p + 128, :]
            out_ref[p, RA1 + 256:, :] = og_ref[p]

    return pl.pallas_call(
        body, name=name, grid=(1,), in_specs=[_resident(a.shape, lambda i, n=a.ndim: (0,) * n) for a in arrays],
        out_specs=_resident((N_DEV, RA1 + RB1, WIDE), lambda i: (0, 0, 0)),
        out_shape=jax.ShapeDtypeStruct((N_DEV, RA1 + RB1, WIDE), BF16), compiler_params=_params(("arbitrary",)),
    )(*arrays)


def _q_up_weight(w):
    return jnp.pad(w.reshape(MLA_Q_RANK, MLA_HEADS, 96), ((0, 0), (0, 0), (0, 32))).reshape(MLA_Q_RANK, MLA_HEADS * LANES)


def _q_up_grad(dwp):
    return dwp.reshape(MLA_Q_RANK, MLA_HEADS, LANES)[:, :, :96].reshape(MLA_Q_RANK, MLA_HEADS * 96)


def _kv_up_weight(w):
    w4 = w.reshape(MLA_KV_RANK, MLA_HEADS, 2, 64)
    kp = jnp.pad(w4[:, :, 0, :], ((0, 0), (0, 0), (0, 64))).reshape(MLA_KV_RANK, MLA_HEADS * LANES)
    vp = w4[:, :, 1, :].reshape(MLA_KV_RANK, MLA_HEADS * 64)
    return jnp.concatenate([kp, vp], axis=1)


def _kv_up_grad(dwp):
    dk = dwp[:, :MLA_HEADS * LANES].reshape(MLA_KV_RANK, MLA_HEADS, LANES)[:, :, :64]
    dv = dwp[:, MLA_HEADS * LANES:].reshape(MLA_KV_RANK, MLA_HEADS, 64)
    return jnp.stack([dk, dv], axis=2).reshape(MLA_KV_RANK, MLA_HEADS * LANES)


def _pad_lanes(a):
    return jnp.pad(a, ((0, 0), (0, LANES - a.shape[1])))


def _small_pack(g_in, g_final, g_q_a, g_kv_a, sinks, b_f, loss):
    rows = [g_in.reshape(8, LANES), g_final.reshape(8, LANES), g_q_a.reshape(2, LANES), g_kv_a.reshape(1, LANES),
            _pad_lanes(sinks.reshape(1, -1)), _pad_lanes(b_f.reshape(1, -1)), _pad_lanes(loss.reshape(1, 1)),
            jnp.zeros((2, LANES), F32)]
    return jnp.concatenate(rows, axis=0)


def _small_unpack(a):
    return (a[0:8].reshape(1, D_MODEL), a[8:16].reshape(D_MODEL), a[16:18].reshape(1, MLA_Q_RANK),
            a[18:19].reshape(1, MLA_KV_RANK), a[19:20, :SWA_HEADS], a[20:21, :FOX_HEADS], a[21, 0])


def _local_step(x, positions, target, e_g_in, early, e_g_q_a, e_g_kv_a, e_sinks,
                late, o_b_f, g_final, scatter1=None, scatter0=None):
    s = x.shape[0]
    mla_scale = (MLA_NOPE + MLA_ROPE) ** -0.5
    fox_scale = FOX_DIM ** -0.5
    n0a = Z0A_UNITS * LANES

    inv_freq = 1.0 / (ROPE_THETA ** (jnp.arange(0, MLA_ROPE, 2, dtype=F32) / MLA_ROPE))
    ang = positions.astype(F32)[:, None] * inv_freq
    cos, sin = jnp.cos(ang), jnp.sin(ang)
    ones, zeros = jnp.ones((s, 64), F32), jnp.zeros((s, 64), F32)
    cos_t = jnp.concatenate([ones, cos, cos, ones[:, :32]], axis=1)
    sin_t = jnp.concatenate([zeros, -sin, sin, zeros[:, :32]], axis=1)
    cos_t, sin_t = lax.optimization_barrier((cos_t, sin_t))

    if len(early) == 3:
        h0 = _rmsnorm_fwd(x, e_g_in, width=D_MODEL, col_blk=0, name="l0_norm")
        w0t, wq, wkv = early
    else:
        pending, token, unpack, prep = early
        h0 = _rmsnorm_fwd(x, e_g_in, width=D_MODEL, col_blk=0, name="l0_norm", after=[token])
        sent, across = _peer_wait("across", *pending, after=[h0] + prep, name="weights0_wait")
        w0t, wq, wkv = unpack(sent, _to_other_core(sent, across, name="weights0_over"))
    z0a, z0b = _matmul_rows([(h0, w0t, True)], [], [], lambda r: (r[:, :n0a], r[:, n0a:]),
                            [("rows", n0a, F32), ("rows", Z0B_UNITS * LANES, BF16)], name="l0_in")
    cqn = _rmsnorm_fwd(z0a, e_g_q_a, width=MLA_Q_RANK, col_blk=4, name="l0_q_norm")
    ckvn = _rmsnorm_fwd(z0a, e_g_kv_a, width=MLA_KV_RANK, col_blk=10, name="l0_kv_norm")
    rope_rows = [(cos_t, LANES, 0), (sin_t, LANES, 0)]
    qm, = _matmul_rows([(cqn, wq, False)], rope_rows, [], _rope_q_epilogue, [("rows", MLA_HEADS * LANES, BF16)],
                       name="l0_q_up")
    kvm, = _matmul_rows([(ckvn, wkv, False)], [(z0a, LANES, 11)] + rope_rows, [], _rope_k_epilogue,
                        [("rows", MLA_HEADS * (LANES + MLA_V), BF16)], name="l0_kv_up")
    gathers = len(late) == 2
    res = _flash_fwd(qm, kvm, kvm, None, n_pairs=MLA_HEADS // 2, hw=LANES, q_off=0, k_off=0, v_off=MLA_HEADS,
                     scale=mla_scale, name="l0_mla_fwd", rider=("gather", late[0]) if gathers else None)
    o_mla, lse_mla = res[0], res[1]
    wo0, o_g_in, w1t, wft, wo1 = late[1](res[2]) if gathers else late
    o_swa, lse_swa = _swa_fwd(z0b, e_sinks, name="l0_swa_fwd")
    half = D_MODEL // 2

    x1, h1, og0 = _matmul_rows(
        [(None, wo0, False)], [(o_mla, half, 0), (o_swa, half, 0), (z0a, D_MODEL, 0), (x, D_MODEL, 0)], [o_g_in],
        lambda r, om, osw, gt, xt, g, made: (*_residual_norm_epilogue(r, xt, g), made),
        [("rows", D_MODEL, F32), ("rows", D_MODEL, BF16), ("rows", D_MODEL, BF16)], name="l0_out",
        prologue=lambda om, osw, gt, xt, g: _gated([om, osw], gt))
    z1, gate1, zf = _matmul_rows(
        [(None, w1t, True), (None, wft, True)], [(h1, D_MODEL, 0)], [],
        lambda r, h, made: (r[0][:, :3 * D_MODEL], r[0][:, 3 * D_MODEL:], r[1]),
        [("rows", 3 * D_MODEL, BF16), ("rows", D_MODEL, F32), ("rows", LANES, F32)], name="l1_in",
        prologue=lambda h: h, separate=True)
    bf = _pad_lanes(o_b_f)
    log_cum = _logf_fwd(zf, bf, name="l1_logf")
    bias2 = (-LOG2E * log_cum[:, :FOX_HEADS]).T
    t_bwd = min(ATT_T, s)
    bias = bias2.reshape(FOX_HEADS // 2, 2, s // t_bwd, 1, t_bwd)
    t_fwd = _fwd_tile(s)
    o_fox, lse_fox = _flash_fwd(z1, z1, z1, bias2.reshape(FOX_HEADS // 2, 2, s // t_fwd, 1, t_fwd),
                                n_pairs=FOX_HEADS // 2, hw=64, q_off=0, k_off=8, v_off=16, scale=fox_scale,
                                name="l1_fox_fwd")

    dx2, loss_part, d_g_final, og1, dx2_bf = _matmul_rows(
        [(None, wo1, False)], [(o_fox, D_MODEL, 0), (gate1, D_MODEL, 0), (x1, D_MODEL, 0), (target, D_MODEL, 0)],
        [g_final.reshape(1, D_MODEL)],
        lambda r, o, gt, xt, tg, g, made: _and_first(_loss_epilogue(r, xt, tg, g), made),
        [("rows", D_MODEL, F32), ("sum", (8, LANES)), ("sum", (1, D_MODEL)), ("rows", D_MODEL, BF16),
         ("rows", D_MODEL, BF16)], name="l1_out_loss", prologue=lambda o, gt, xt, tg, g: _gated([o], gt))

    d_wo1 = _matmul(og1, dx2_bf, ta=True, out_dtype=BF16, name="l1_out_dw")
    do_fox, d_gate1 = _matmul_rows([(dx2_bf, wo1, True)], [(o_fox, D_MODEL, 0), (gate1, D_MODEL, 0)], [],
                                   _gate_bwd_epilogue([D_MODEL]), [("rows", D_MODEL, F32), ("rows", D_MODEL, BF16)],
                                   name="l1_out_dx")
    dqkv1, dbias, drow = _flash_bwd(z1, z1, z1, do_fox, o_fox, lse_fox, bias, n_pairs=FOX_HEADS // 2, hw=64, q_off=0,
                                    k_off=8, v_off=16, scale=fox_scale, qk_dtype=BF16, stacked=True, name="l1_fox_bwd")
    d_log_cum = (drow.reshape(FOX_HEADS, s) - dbias.reshape(FOX_HEADS, s)).T
    d_log_cum = jnp.pad(d_log_cum, ((0, 0), (0, LANES - FOX_HEADS)))
    d_zf, d_bf = _logf_bwd(d_log_cum, zf, bf, name="l1_logf_bwd")
    d_w1t = (_matmul(dqkv1, h1, ta=True, out_dtype=BF16, name="l1_in_dw_qkv"),
             _matmul(d_gate1, h1, ta=True, out_dtype=BF16, name="l1_in_dw_gate"))
    d_wft = _matmul(d_zf, h1, ta=True, out_dtype=BF16, name="l1_in_f_dw")
    dx1, d_o_g_in, dx1_bf = _matmul_rows([(dqkv1, w1t, False, c * D_MODEL, c) for c in range(3)]
                                         + [(d_gate1, w1t, False, 3 * D_MODEL), (d_zf, wft, False)],
                                         [(x1, D_MODEL, 0), (dx2, D_MODEL, 0)], [o_g_in],
                                         lambda *a: _and_first(_rms_bwd_epilogue(*a)),
                                         [("rows", D_MODEL, F32), ("sum", (1, D_MODEL)), ("rows", D_MODEL, BF16)],
                                         name="l1_in_dx")

    d_wo0 = _matmul(og0, dx1_bf, ta=True, out_dtype=BF16, name="l0_out_dw")
    do_mla, do_swa, d_gate0 = _matmul_rows(
        [(dx1_bf, wo0, True)], [(o_mla, half, 0), (o_swa, half, 0), (z0a, D_MODEL, 0)], [], _gate_bwd_epilogue([half, half]),
        [("rows", half, F32), ("rows", half, F32), ("rows", D_MODEL, BF16)], name="l0_out_dx")
    dq_s, dkt_s, dvt_s, d_sinks = _swa_bwd(z0b, e_sinks, do_swa, o_swa, lse_swa, name="l0_swa_bwd")
    dk_s = dkt_s.transpose(0, 2, 1).reshape(s, LANES)
    dv_s = dvt_s.transpose(0, 2, 1).reshape(s, LANES)
    rider = None
    if scatter1 is not None:
        rider = ("exchange", scatter1(dict(w1t=d_w1t, wft=d_wft, wo1=d_wo1, o_g_in=d_o_g_in, wo0=d_wo0)))
    res = _flash_bwd(qm, kvm, kvm, do_mla, o_mla, lse_mla, None, n_pairs=MLA_HEADS // 2, hw=LANES, q_off=0, k_off=0,
                     v_off=MLA_HEADS, scale=mla_scale, qk_dtype=F32, name="l0_mla_bwd", rider=rider)
    dqm, dkm, dvm = res[0], res[1], res[2]
    recv1 = res[3] if rider is not None else None
    d_qp, d_kvp, d_kpe = _rope_bwd(dqm, dkm, dvm, cos_t, sin_t, name="l0_rope_bwd")
    d_wq = _matmul(cqn, d_qp, ta=True, out_dtype=BF16, name="l0_q_up_dw")
    d_cqn = _matmul(d_qp, wq, tb=True, name="l0_q_up_dx")
    d_wkv = _matmul(ckvn, d_kvp, ta=True, out_dtype=BF16, name="l0_kv_up_dw")
    d_ckvn = _matmul(d_kvp, wkv, tb=True, name="l0_kv_up_dx")
    d_cq, d_g_q_a = _rmsnorm_bwd(z0a, e_g_q_a, d_cqn, width=MLA_Q_RANK, col_blk=4, name="l0_q_norm_bwd")
    d_ckv, d_g_kv_a = _rmsnorm_bwd(z0a, e_g_kv_a, d_ckvn, width=MLA_KV_RANK, col_blk=10, name="l0_kv_norm_bwd")
    dz0 = jnp.concatenate([d_gate0, d_cq, d_ckv, d_kpe, dq_s.astype(BF16), dk_s.astype(BF16), dv_s.astype(BF16)], axis=1)
    d_w0t = _matmul(dz0, h0, ta=True, out_dtype=BF16, name="l0_in_dw")
    pending0, after_start = None, []
    if scatter0 is not None:
        *pending0, token = _peer_start("exchange", scatter0(dict(w0t=d_w0t, wq=d_wq, wkv=d_wkv)), name="grads0_start")
        after_start = [token]
    grad_x, d_e_g_in = _matmul_rows(
        [(dz0, w0t, False)], [(x, D_MODEL, 0), (dx1, D_MODEL, 0)], [e_g_in] + after_start,
        lambda dy, xt, add, g, *_: _rms_bwd_epilogue(dy, xt, add, g),
        [("rows", D_MODEL, F32), ("sum", (1, D_MODEL))], name="l0_in_dx")

    return dict(pending0=pending0, recv1=recv1, loss=loss_part[0, 0], grad_x=grad_x, e_g_in=d_e_g_in, w0t=d_w0t, e_g_q_a=d_g_q_a, wq=d_wq,
                e_g_kv_a=d_g_kv_a, wkv=d_wkv, e_sinks=d_sinks[:, 0].reshape(1, SWA_HEADS), wo0=d_wo0,
                o_g_in=d_o_g_in, w1t=d_w1t, wft=d_wft, o_b_f=d_bf[:, :FOX_HEADS], wo1=d_wo1, g_final=d_g_final.reshape(D_MODEL))


def _wide(a, rows):
    flat = a.reshape(-1)
    return jnp.pad(flat, (0, rows * WIDE - flat.shape[0])).reshape(rows, WIDE)


def _rows_b0(w_q, w_kv):
    return jnp.concatenate([_wide(w_q, 32), _wide(w_kv, 16)], axis=0)


def _unflat_b0(f):
    return f[0:24].reshape(1, MLA_Q_RANK, 96), f[32:48].reshape(1, MLA_KV_RANK, 128)


def _rows_b1(o_w_out, e_w_out, g_in):
    return jnp.concatenate([o_w_out, e_w_out, _wide(g_in, 16)], axis=0)


def _unflat_b1(f):
    return f[0:128][None], f[128:256][None], f[256:257, :LANES]


def kernel(x, positions, e_g_in, e_w_in, e_g_q_a, e_w_q_up, e_g_kv_a, e_w_kv_up, e_sinks, e_w_out, o_g_in, o_w_in, o_b_f, o_w_out, g_final, loss_target, m_e_g_in, m_e_w_in, m_e_g_q_a, m_e_w_q_up, m_e_g_kv_a, m_e_w_kv_up, m_e_sinks, m_e_w_out, m_o_g_in, m_o_w_in, m_o_b_f, m_o_w_out, m_g_final, v_e_g_in, v_e_w_in, v_e_g_q_a, v_e_w_q_up, v_e_g_kv_a, v_e_w_kv_up, v_e_sinks, v_e_w_out, v_o_g_in, v_o_w_in, v_o_b_f, v_o_w_out, v_g_final):
    def bf(a):
        return a.astype(BF16)

    me = 4 * lax.axis_index("x") + 2 * lax.axis_index("y") + lax.axis_index("c")
    shard0 = jnp.concatenate([_pad_rows(bf(e_w_in[0]).T, RA0), _rows_b0(bf(e_w_q_up[0]), bf(e_w_kv_up[0]))], axis=0)
    *pending_w0, token_w0 = _peer_start("across", shard0, name="weights0_start")

    def unpack0(sent, gath0):
        gath0 = lax.dynamic_update_slice_in_dim(gath0, sent[None], me, axis=0)
        w0t = _layer0_in_weight_t(gath0[:, :N_E_IN].reshape(N_DEV * N_E_IN, WIDE))
        wq = _q_up_weight(_gathered_cols(gath0[:, RA0:RA0 + 24], MLA_Q_RANK))
        wkv = _kv_up_weight(_gathered_cols(gath0[:, RA0 + 32:RA0 + 48], MLA_KV_RANK))
        return w0t, wq, wkv

    rows_b0 = [_rows_b0(q[0], kv[0]) for q, kv in ((e_w_q_up, e_w_kv_up), (m_e_w_q_up, m_e_w_kv_up), (v_e_w_q_up, v_e_w_kv_up))]
    rows_b1 = [_rows_b1(o[0], e[0], g) for o, e, g in ((o_w_out, e_w_out, o_g_in), (m_o_w_out, m_e_w_out, m_o_g_in),
                                                       (v_o_w_out, v_e_w_out, v_o_g_in))]

    g_bits = lax.bitcast_convert_type(o_g_in.reshape(LANES), BF16)
    shard1 = jnp.concatenate([_pad_rows(bf(o_w_in[0]).T, RA1), _rows_b1(bf(o_w_out[0]), bf(e_w_out[0]), g_bits)], axis=0)

    def unpack1(gath1):
        w1t, wft = _layer1_in_unpack(gath1, name="weights1_unpack")
        wo1 = gath1[:, RA1:RA1 + 128].reshape(D_MODEL, D_MODEL)
        wo0 = gath1[:, RA1 + 128:RA1 + 256].reshape(D_MODEL, D_MODEL)
        bits = gath1[:, RA1 + 256, :2 * LANES].reshape(N_DEV, LANES, 2)
        return wo0, lax.bitcast_convert_type(bits, F32).reshape(1, D_MODEL), w1t, wft, wo1

    def scatter1(g):
        d_o_g = jnp.pad(bf(g["o_g_in"]).reshape(N_DEV, 1, LANES), ((0, 0), (0, 15), (0, WIDE - LANES)))
        return _late_grads_pack(g["w1t"][0], g["wft"], g["w1t"][1], g["wo1"], g["wo0"], d_o_g, name="grads1_pack")

    def scatter0(g):
        return jnp.concatenate([
            _pad_rows(_layer0_in_grad_t(g["w0t"]).reshape(N_DEV, N_E_IN, WIDE), RA0),
            _pad_rows(_scatter_cols(_q_up_grad(g["wq"])), 32), _scatter_cols(_kv_up_grad(g["wkv"]))], axis=1)

    gr = _local_step(x[0], positions[0], loss_target[0], e_g_in,
                     (pending_w0, token_w0, unpack0, [shard1] + rows_b0 + rows_b1), e_g_q_a, e_g_kv_a, e_sinks,
                     (shard1, unpack1), o_b_f, g_final, scatter1=scatter1, scatter0=scatter0)

    def in_projection(recv, ra, n, w, m, v, name):
        g = _sum8(recv, ra, name=name + "_grad_sum")[:n].reshape(n, 1, D_MODEL)
        w, m, v = [jnp.transpose(a, (2, 0, 1)) for a in (w, m, v)]
        return (g, *_adamw_columns(g, w, m, v, name=name + "_adamw"))

    o_in = in_projection(gr["recv1"], RA1, N_O_IN, o_w_in, m_o_w_in, v_o_w_in, "o_w_in")
    b1 = _adamw(gr["recv1"][:, RA1:], *rows_b1, name="adamw_late")

    small = _small_pack(gr["e_g_in"], gr["g_final"], gr["e_g_q_a"], gr["e_g_kv_a"], gr["e_sinks"], gr["o_b_f"], gr["loss"])
    small_all = _all_gather(small, name="small_all_gather")
    zero = jnp.zeros((), F32)
    w_small = _small_pack(e_g_in, g_final, e_g_q_a, e_g_kv_a, e_sinks, o_b_f, zero)
    m_small = _small_pack(m_e_g_in, m_g_final, m_e_g_q_a, m_e_g_kv_a, m_e_sinks, m_o_b_f, zero)
    v_small = _small_pack(v_e_g_in, v_g_final, v_e_g_q_a, v_e_g_kv_a, v_e_sinks, v_o_b_f, zero)
    smalls = _adamw(small_all, w_small, m_small, v_small, name="adamw_replicated")
    g_sm, d_sm, m_sm, v_sm = [_small_unpack(a) for a in smalls]
    loss = g_sm[6]

    sent0, recv0 = _peer_wait("exchange", *gr["pending0"], after=[o_in[1], b1[1], smalls[1]], name="grads0_wait")
    own = lax.dynamic_slice_in_dim(sent0, me, 1, axis=0)
    recv0 = lax.dynamic_update_slice_in_dim(recv0, own, me, axis=0)
    e_in = in_projection(recv0, RA0, N_E_IN, e_w_in, m_e_w_in, v_e_w_in, "e_w_in")
    b0 = _adamw(recv0[:, RA0:], *rows_b0, name="adamw_early")

    def sharded(k):
        q_up, kv_up = _unflat_b0(b0[k])
        o_out, e_out, o_g = _unflat_b1(b1[k])
        return jnp.transpose(e_in[k], (1, 2, 0)), q_up, kv_up, e_out, jnp.transpose(o_in[k], (1, 2, 0)), o_out, o_g

    g_sh, d_sh, m_sh, v_sh = [sharded(k) for k in range(4)]

    def leaves(sh, sm):
        return (sm[0], sh[0], sm[2], sh[1], sm[3], sh[2], sm[4], sh[3], sh[6], sh[4], sm[5], sh[5], sm[1])

    return (loss, gr["grad_x"][None], *leaves(g_sh, g_sm), *leaves(d_sh, d_sm), *leaves(m_sh, m_sm), *leaves(v_sh, v_sm))
```

```python
import functools

import jax
import jax.numpy as jnp
from jax import lax
from jax.experimental import pallas as pl
from jax.experimental.pallas import tpu as pltpu

F32 = jnp.float32
BF16 = jnp.bfloat16
NEG_INF = float("-inf")

N_DEV = 8
LANES = 128
D_MODEL = 1024
EPS = 1e-6
ROPE_THETA = 10000.0
MLA_HEADS = 8
MLA_Q_RANK = 256
MLA_KV_RANK = 128
MLA_NOPE = 64
MLA_ROPE = 32
MLA_V = 64
SWA_HEADS = 8
SWA_KV_HEADS = 2
SWA_DIM = 64
WINDOW = 128
FOX_HEADS = 16
FOX_DIM = 64

ADAM_LR = 0.001
ADAM_B1 = 0.9
ADAM_B2 = 0.999
ADAM_EPS = 1e-08
ADAM_WD = 0.01
ADAM_STEP = 10

ATT_T = 512
ATT_T_FWD = 1024
VMEM_LIMIT = 56 * 1024 * 1024
MATMUL_B_BLOCK_BYTES = 8 * 1024 * 1024

Z0A_UNITS = 12
Z0B_UNITS = 6

WIDE = 1024
N_E_IN = 276
N_O_IN = 514
RA0 = 288
RB0 = 32 + 16
RA1 = 528
RB1 = 128 + 128 + 16
SMALL_ROWS = 24


def _tile(n, cands):
    for c in cands:
        if n % c == 0:
            return c
    raise ValueError(f"no tile for {n}")


ROW_TILES = (512, 256, 128)


def _params(sem, vmem=VMEM_LIMIT):
    return pltpu.CompilerParams(dimension_semantics=sem, vmem_limit_bytes=vmem)


def _matmul(a, b, *, name, ta=False, tb=False, out_dtype=F32, b_rows=None):
    if ta:
        kdim, m = a.shape[-2], a.shape[-1] * (a.shape[0] if a.ndim == 3 else 1)
    else:
        m, kdim = a.shape
    if tb:
        n, kb = b.shape
    else:
        kb, n = b.shape
    assert kdim == kb, (a.shape, b.shape)
    b_start = 0
    if b_rows is not None:
        assert tb
        b_start, n = b_rows
    tm = _tile(m, (512, 256, 128))
    tn = _tile(n, [c for c in (1024, 768, 512, 384, 256, 128)
                   if c * kdim * b.dtype.itemsize <= MATMUL_B_BLOCK_BYTES and b_start % c == 0])
    assert b_start % tn == 0, (b_start, tn)
    b_off = b_start // tn
    dims = (((0 if ta else 1,), (1 if tb else 0,)), ((), ()))

    def body(a_ref, b_ref, o_ref):
        r = lax.dot_general(a_ref[...].astype(BF16), b_ref[...].astype(BF16), dims, preferred_element_type=F32)
        o_ref[...] = r.astype(out_dtype)

    if a.ndim == 3:
        per = a.shape[2] // tm
        a_spec = pl.BlockSpec((None, kdim, tm), lambda i, j: (i // per, 0, i % per))
    else:
        a_spec = pl.BlockSpec((kdim, tm), lambda i, j: (0, i)) if ta else pl.BlockSpec((tm, kdim), lambda i, j: (i, 0))
    b_spec = pl.BlockSpec((tn, kdim), lambda i, j: (j + b_off, 0)) if tb else pl.BlockSpec((kdim, tn), lambda i, j: (0, j))
    return pl.pallas_call(
        body, name=name, grid=(m // tm, n // tn), in_specs=[a_spec, b_spec],
        out_specs=pl.BlockSpec((tm, tn), lambda i, j: (i, j)), out_shape=jax.ShapeDtypeStruct((m, n), out_dtype),
        compiler_params=_params(("parallel", "parallel")),
    )(a, b)


def _rmsnorm_fwd(x, g, *, width, col_blk, name, after=()):
    s = x.shape[0]
    tm = _tile(s, ROW_TILES)

    def body(x_ref, g_ref, *rest):
        y_ref = rest[-1]
        xf = x_ref[...].astype(F32)
        r = lax.rsqrt(jnp.mean(xf * xf, axis=-1, keepdims=True) + EPS)
        y_ref[...] = ((xf * r) * g_ref[...]).astype(BF16)

    return pl.pallas_call(
        body, name=name, grid=(s // tm,),
        in_specs=[pl.BlockSpec((tm, width), lambda i: (i, col_blk)), pl.BlockSpec((1, width), lambda i: (0, 0))]
        + [ANY] * len(after),
        out_specs=pl.BlockSpec((tm, width), lambda i: (i, 0)),
        out_shape=jax.ShapeDtypeStruct((s, width), BF16),
        compiler_params=_params(("parallel",)),
    )(x, g, *after)


def _rmsnorm_bwd(x, g, dy, *, width, col_blk, name):
    s = x.shape[0]
    tm = _tile(s, ROW_TILES)

    def body(x_ref, g_ref, dy_ref, dx_ref, dg_ref):
        @pl.when(pl.program_id(0) == 0)
        def _():
            dg_ref[...] = jnp.zeros_like(dg_ref)

        dx, dg = _rms_bwd_epilogue(dy_ref[...], x_ref[...], 0.0, g_ref[...])
        dg_ref[...] += dg
        dx_ref[...] = dx.astype(BF16)

    return pl.pallas_call(
        body, name=name, grid=(s // tm,),
        in_specs=[pl.BlockSpec((tm, width), lambda i: (i, col_blk)), pl.BlockSpec((1, width), lambda i: (0, 0)),
                  pl.BlockSpec((tm, width), lambda i: (i, 0))],
        out_specs=[pl.BlockSpec((tm, width), lambda i: (i, 0)), pl.BlockSpec((1, width), lambda i: (0, 0))],
        out_shape=[jax.ShapeDtypeStruct((s, width), BF16), jax.ShapeDtypeStruct((1, width), F32)],
        compiler_params=_params(("arbitrary",)),
    )(x, g, dy)


def _sigmoid(x):
    return 1.0 / (1.0 + jnp.exp(-x))


def _matmul_rows(terms, row_inputs, params, epilogue, outs, *, name, prologue=None, separate=False):
    s = row_inputs[0][0].shape[0] if row_inputs else terms[0][0].shape[-2]
    tm = _tile(s, ROW_TILES)
    steps = s // tm
    n_r, n_p, n_o = len(row_inputs), len(params), len(outs)
    n_t = sum(1 if term[0] is None else 2 for term in terms)

    def body(*refs):
        t_refs, r_refs = list(refs[:n_t]), refs[n_t:n_t + n_r]
        p_refs, o_refs = refs[n_t + n_r:n_t + n_r + n_p], refs[n_t + n_r + n_p:]
        i = pl.program_id(0)
        rows, small = [r[...] for r in r_refs], [p[...] for p in p_refs]
        made = None if prologue is None else prologue(*rows, *small)
        parts = []
        for term in terms:
            a = made if term[0] is None else t_refs.pop(0)[...].astype(BF16)
            dims = (((1,), (1 if term[2] else 0,)), ((), ()))
            parts.append(lax.dot_general(a, t_refs.pop(0)[...].astype(BF16), dims, preferred_element_type=F32))
        acc = parts if separate else sum(parts[1:], parts[0])
        vals = epilogue(acc, *rows, *small) if prologue is None else epilogue(acc, *rows, *small, made)
        for ref, val, out in zip(o_refs, vals, outs):
            if out[0] == "rows":
                ref[...] = val.astype(ref.dtype)
            else:
                @pl.when(i == 0)
                def _(ref=ref):
                    ref[...] = jnp.zeros_like(ref)

                ref[...] += val

    in_specs, args = [], []
    for term in terms:
        a, b = term[0], term[1]
        if a is None:
            in_specs.append(_resident(b.shape, lambda i: (0, 0)))
            args.append(b)
            continue
        b_rows = b.shape[0] if term[2] or len(term) < 4 else a.shape[-1]
        b_blk = 0 if len(term) < 4 else term[3] // b_rows
        if len(term) == 5:
            a_spec = pl.BlockSpec((None, tm, a.shape[2]), lambda i, c=term[4]: (c, i, 0))
        else:
            a_spec = pl.BlockSpec((tm, a.shape[1]), lambda i: (i, 0))
        in_specs += [a_spec, _resident((b_rows, b.shape[1]), lambda i, b_blk=b_blk: (b_blk, 0))]
        args += [a, b]
    for arr, width, col_blk in row_inputs:
        in_specs.append(pl.BlockSpec((tm, width), lambda i, col_blk=col_blk: (i, col_blk)))
        args.append(arr)
    for p in params:
        in_specs.append(pl.BlockSpec(p.shape, lambda i: (0, 0)))
        args.append(p)
    out_specs, out_shape = [], []
    for out in outs:
        if out[0] == "rows":
            out_specs.append(pl.BlockSpec((tm, out[1]), lambda i: (i, 0)))
            out_shape.append(jax.ShapeDtypeStruct((s, out[1]), out[2]))
        else:
            out_specs.append(pl.BlockSpec(out[1], lambda i: (0, 0)))
            out_shape.append(jax.ShapeDtypeStruct(out[1], F32))
    return pl.pallas_call(
        body, name=name, grid=(steps,), in_specs=in_specs, out_specs=out_specs, out_shape=out_shape,
        compiler_params=_params(("arbitrary",)),
    )(*args)


def _rms_stats(x):
    r = lax.rsqrt(jnp.mean(x * x, axis=-1, keepdims=True) + EPS)
    return r, x * r


def _gated(o_parts, gate):
    o = o_parts[0] if len(o_parts) == 1 else jnp.concatenate(o_parts, axis=1)
    return (o * (gate * _sigmoid(gate))).astype(BF16)


def _and_first(vals, *more):
    return (*vals, *more, vals[0])


def _residual_norm_epilogue(r, x, g):
    x1 = x + r
    _, xh = _rms_stats(x1)
    return x1, xh * g


def _rms_bwd_epilogue(dy, x, add, g):
    r, xh = _rms_stats(x)
    dxh = dy * g
    dx = r * (dxh - xh * jnp.mean(dxh * xh, axis=-1, keepdims=True)) + add
    return dx, jnp.sum(dy * xh, axis=0, keepdims=True)


def _loss_epilogue(r, x1, target, g):
    rs, xh = _rms_stats(x1 + r)
    err = xh * g - target
    loss = jnp.broadcast_to(0.5 * jnp.sum(jnp.mean(err * err, axis=-1, keepdims=True)), (8, LANES))
    dy = err * (1.0 / D_MODEL)
    dxh = dy * g
    dx = rs * (dxh - xh * jnp.mean(dxh * xh, axis=-1, keepdims=True))
    return dx, loss, jnp.sum(dy * xh, axis=0, keepdims=True)


def _gate_bwd_epilogue(widths):
    def epilogue(d, *rows):
        o_parts, gt = rows[:-1], rows[-1]
        o = o_parts[0] if len(o_parts) == 1 else jnp.concatenate(o_parts, axis=1)
        sg = _sigmoid(gt)
        do = d * (gt * sg)
        d_gate = d * o * (sg * (1.0 + gt * (1.0 - sg)))
        cuts = [sum(widths[:k]) for k in range(len(widths) + 1)]
        return tuple(do[:, cuts[k]:cuts[k + 1]] for k in range(len(widths))) + (d_gate,)

    return epilogue


def _rot_half(x):
    lane = lax.broadcasted_iota(jnp.int32, x.shape, 1)
    return jnp.where(lane < 80, pltpu.roll(x, LANES - 16, axis=1), pltpu.roll(x, 16, axis=1))


def _rot_half_t(g):
    lane = lax.broadcasted_iota(jnp.int32, g.shape, 1)
    lo = (lane >= MLA_NOPE) & (lane < MLA_NOPE + MLA_ROPE // 2)
    hi = (lane >= MLA_NOPE + MLA_ROPE // 2) & (lane < MLA_NOPE + MLA_ROPE)
    return jnp.where(lo, pltpu.roll(g, LANES - 16, axis=1), jnp.where(hi, pltpu.roll(g, 16, axis=1), 0.0))


def _rope_q_epilogue(q, c, sn):
    heads = [q[:, h * LANES:(h + 1) * LANES] for h in range(MLA_HEADS)]
    return (jnp.concatenate([qh * c + _rot_half(qh) * sn for qh in heads], axis=1),)


def _rope_k_epilogue(kv, kpe, c, sn):
    kpe_r = kpe * c + _rot_half(kpe) * sn
    lane = lax.broadcasted_iota(jnp.int32, kpe.shape, 1)
    heads = [jnp.where(lane < MLA_NOPE, kv[:, h * LANES:(h + 1) * LANES], kpe_r) for h in range(MLA_HEADS)]
    return (jnp.concatenate(heads + [kv[:, MLA_HEADS * LANES:]], axis=1),)


def _rope_bwd(dqm, dkm, dvm, cos_t, sin_t, *, name):
    s = dqm.shape[0]
    tm = _tile(s, ROW_TILES)
    hw = MLA_HEADS * LANES
    vw = MLA_HEADS * MLA_V

    def body(dq_ref, dk_ref, dv_ref, c_ref, s_ref, dqp_ref, dkv_ref, dkpe_ref):
        c = c_ref[...]
        sn = s_ref[...]
        ksum = jnp.zeros((tm, LANES), F32)
        for h in range(MLA_HEADS):
            sl = slice(h * LANES, (h + 1) * LANES)
            dq = dq_ref[:, sl]
            dqp_ref[:, sl] = (dq * c + _rot_half_t(dq * sn)).astype(BF16)
            dk = dk_ref[:, sl]
            dkv_ref[:, sl] = dk.astype(BF16)
            ksum = ksum + dk
        dkv_ref[:, hw:] = dv_ref[...]
        lane = lax.broadcasted_iota(jnp.int32, ksum.shape, 1)
        dkpe = ksum * c + _rot_half_t(ksum * sn)
        dkpe_ref[...] = jnp.where((lane >= MLA_NOPE) & (lane < MLA_NOPE + MLA_ROPE), dkpe, 0.0).astype(BF16)

    return pl.pallas_call(
        body, name=name, grid=(s // tm,),
        in_specs=[pl.BlockSpec((tm, hw), lambda i: (i, 0)), pl.BlockSpec((tm, hw), lambda i: (i, 0)),
                  pl.BlockSpec((tm, vw), lambda i: (i, 0)),
                  pl.BlockSpec((tm, LANES), lambda i: (i, 0)), pl.BlockSpec((tm, LANES), lambda i: (i, 0))],
        out_specs=[pl.BlockSpec((tm, hw), lambda i: (i, 0)), pl.BlockSpec((tm, hw + vw), lambda i: (i, 0)),
                   pl.BlockSpec((tm, LANES), lambda i: (i, 0))],
        out_shape=[jax.ShapeDtypeStruct((s, hw), BF16), jax.ShapeDtypeStruct((s, hw + vw), BF16),
                   jax.ShapeDtypeStruct((s, LANES), BF16)],
        compiler_params=_params(("parallel",)),
    )(dqm, dkm, dvm, cos_t, sin_t)


def _head_mask(shape, a):
    lane = lax.broadcasted_iota(jnp.int32, shape, 1)
    return (lane >= 64 * a) & (lane < 64 * (a + 1))


_NT = (((1,), (1,)), ((), ()))
LOG2E = 1.4426950408889634


def _stack_heads(tile, hw):
    lane = lax.broadcasted_iota(jnp.int32, tile.shape, 1)
    z = jnp.zeros_like(tile)
    return jnp.concatenate([jnp.where(lane < hw, tile, z), jnp.where(lane >= hw, tile, z)], axis=0)


def _stacked_rows(r0, r1, t):
    n = r0.shape[-1]
    return jnp.concatenate([jnp.broadcast_to(r0, (t, n)), jnp.broadcast_to(r1, (t, n))], axis=0)


def _resident(block, index_map):
    return pl.BlockSpec(block, index_map, pipeline_mode=pl.Buffered(1))


def _fwd_tile(s):
    return ATT_T_FWD if s % ATT_T_FWD == 0 else min(ATT_T, s)


def _flash_fwd(q, k, v, bias, *, n_pairs, hw, q_off, k_off, v_off, scale, name, rider=None):
    s = q.shape[0]
    t = _fwd_tile(s)
    nb = s // t
    qw = 2 * hw
    has_bias = bias is not None
    c1 = scale * LOG2E

    def body(*refs):
        refs, ride_refs = _split_rider(refs, rider, n_in=4 if has_bias else 3, n_out=2)
        if has_bias:
            q_ref, k_ref, v_ref, b_ref, o_ref, lse_ref, vt_ref, bcol_ref = refs
        else:
            q_ref, k_ref, v_ref, o_ref, lse_ref, vt_ref = refs
            b_ref = bcol_ref = None
        _ride_start(rider, ride_refs, pl.program_id(0) == 0)
        row = lax.broadcasted_iota(jnp.int32, (t, t), 0)
        col = lax.broadcasted_iota(jnp.int32, (t, t), 1)
        cmask_t = jnp.concatenate([row <= col, row <= col], axis=1)
        lane_lt64 = lax.broadcasted_iota(jnp.int32, (t, LANES), 1) < 64

        def as_column(r):
            return jnp.broadcast_to(r, (8, r.shape[1])).T[:, 0:1]

        def v_block(j, _):
            c0 = pl.multiple_of(j * t, t)
            vt_ref[j] = v_ref[pl.ds(c0, t), :].astype(F32).T.astype(BF16)
            if has_bias:
                for a in range(2):
                    bcol_ref[a, pl.ds(c0, t), :] = as_column(b_ref[0, a, j])
            return 0

        lax.fori_loop(0, nb, v_block, 0)

        def stacked_queries(i):
            return _stack_heads(q_ref[pl.ds(pl.multiple_of(i * t, t), t), :], hw).astype(F32).T.astype(BF16)

        def kv_step(j, carry, qs_t, masked):
            m, l, acc = carry
            rows = pl.ds(pl.multiple_of(j * t, t), t)
            sc = jnp.dot(k_ref[rows, :], qs_t, preferred_element_type=F32) * c1
            if has_bias:
                sc = sc + jnp.concatenate([jnp.broadcast_to(bcol_ref[0, rows, :], (t, t)),
                                           jnp.broadcast_to(bcol_ref[1, rows, :], (t, t))], axis=1)
            if masked:
                sc = jnp.where(cmask_t, sc, NEG_INF)
            m_new = jnp.maximum(m, jnp.max(sc, axis=0, keepdims=True))
            alpha = jnp.exp2(m - m_new)
            p = jnp.exp2(sc - m_new)
            l_new = alpha * l + jnp.sum(p, axis=0, keepdims=True)
            pv = jnp.dot(vt_ref[j], p.astype(BF16), preferred_element_type=F32)
            return m_new, l_new, alpha * acc + pv

        def finish(i, carry):
            m, l, acc = carry
            r0 = pl.multiple_of(i * t, t)
            out = (acc / l).T
            lse2 = as_column(m + jnp.log2(l))
            lse_ref[0, 0, pl.ds(r0, t), :] = lse2[:t]
            lse_ref[0, 1, pl.ds(r0, t), :] = lse2[t:]
            o_ref[pl.ds(r0, t), :] = jnp.where(lane_lt64, out[:t], out[t:])

        init = (jnp.full((1, 2 * t), NEG_INF, F32), jnp.zeros((1, 2 * t), F32), jnp.zeros((LANES, 2 * t), F32))

        def q_block(i, _):
            qs_t = stacked_queries(i)
            carry = lax.fori_loop(0, i, lambda j, c: kv_step(j, c, qs_t, False), init)
            finish(i, kv_step(i, carry, qs_t, True))
            return 0

        lax.fori_loop(0, nb, q_block, 0)
        _ride_wait(rider, ride_refs, pl.program_id(0) == n_pairs - 1)

    in_specs = [_resident((s, qw), lambda p: (0, q_off + p)), _resident((s, qw), lambda p: (0, k_off + p)),
                _resident((s, LANES), lambda p: (0, v_off + p))]
    args = [q, k, v]
    if has_bias:
        in_specs.append(_resident((1, 2, nb, 1, t), lambda p: (p, 0, 0, 0, 0)))
        args.append(bias)
    out_specs = [pl.BlockSpec((s, LANES), lambda p: (0, p)), pl.BlockSpec((1, 2, s, 1), lambda p: (p, 0, 0, 0))]
    out_shape = [jax.ShapeDtypeStruct((s, n_pairs * LANES), F32), jax.ShapeDtypeStruct((n_pairs, 2, s, 1), F32)]
    scratch = [pltpu.VMEM((nb, LANES, t), BF16)] + ([pltpu.VMEM((2, s, 1), F32)] if has_bias else [])
    scratch += _add_rider(rider, in_specs, args, out_specs, out_shape)
    return pl.pallas_call(
        body, name=name, grid=(n_pairs,), in_specs=in_specs, out_specs=out_specs, out_shape=out_shape,
        scratch_shapes=scratch,
        compiler_params=_params(("parallel",) if rider is None else ("arbitrary",)),
    )(*args)


def _flash_bwd(q, k, v, do, o, lse, bias, *, n_pairs, hw, q_off, k_off, v_off, scale, qk_dtype, name, rider=None,
               stacked=False):
    s = q.shape[0]
    t = min(ATT_T, s)
    nb = s // t
    qw = 2 * hw
    has_bias = bias is not None
    c1 = scale * LOG2E

    def body(*refs):
        n_grads = 1 if stacked else 3
        refs, ride_refs = _split_rider(refs, rider, n_in=7 if has_bias else 6, n_out=n_grads + (2 if has_bias else 0))
        if stacked:
            refs = list(refs)
            n_in = 7 if has_bias else 6
            refs[n_in:n_in + 1] = [refs[n_in].at[0], refs[n_in].at[1], refs[n_in].at[2]]
        if has_bias:
            (q_ref, k_ref, v_ref, do_ref, o_ref, lse_ref, b_ref, dq_ref, dk_ref, dv_ref, db_ref, dr_ref,
             dkt_ref, dvt_ref) = refs
            db_ref[...] = jnp.zeros_like(db_ref)
        else:
            q_ref, k_ref, v_ref, do_ref, o_ref, lse_ref, dq_ref, dk_ref, dv_ref, dkt_ref, dvt_ref = refs
            b_ref = db_ref = dr_ref = None
        _ride_start(rider, ride_refs, pl.program_id(0) == 0)
        dkt_ref[...] = jnp.zeros_like(dkt_ref)
        dvt_ref[...] = jnp.zeros_like(dvt_ref)
        causal = lax.broadcasted_iota(jnp.int32, (t, t), 1) <= lax.broadcasted_iota(jnp.int32, (t, t), 0)
        cmask = jnp.concatenate([causal, causal], axis=0)
        lane_lt_hw = lax.broadcasted_iota(jnp.int32, (t, qw), 1) < hw

        def q_block(i, _):
            r0 = pl.multiple_of(i * t, t)
            qs = _stack_heads(q_ref[pl.ds(r0, t), :], hw)
            dos = _stack_heads(do_ref[pl.ds(r0, t), :], 64)
            ot = o_ref[pl.ds(r0, t), :]
            delta = jnp.sum(dos * jnp.concatenate([ot, ot], axis=0), axis=-1, keepdims=True)
            lse2 = jnp.concatenate([lse_ref[0, 0, pl.ds(r0, t), :], lse_ref[0, 1, pl.ds(r0, t), :]], axis=0)
            dosb = dos.astype(BF16)
            dos_t = dos.T.astype(BF16)
            qs_t = qs.astype(F32).T.astype(BF16)

            def kv_step(j, carry, masked):
                dq, rsum = carry
                c0 = pl.multiple_of(j * t, t)
                kt = k_ref[pl.ds(c0, t), :]
                vt = v_ref[pl.ds(c0, t), :]
                sc = lax.dot_general(qs, kt, _NT, preferred_element_type=F32) * c1
                if has_bias:
                    sc = sc + _stacked_rows(b_ref[0, 0, j], b_ref[0, 1, j], t)
                if masked:
                    sc = jnp.where(cmask, sc, NEG_INF)
                p = jnp.exp2(sc - lse2)
                dp = lax.dot_general(dosb, vt, _NT, preferred_element_type=F32)
                ds = p * (dp - delta)
                dsb = ds.astype(BF16)
                pb = p.astype(BF16)
                if hw == LANES:
                    dvt_ref[j] += jnp.concatenate(
                        [jnp.dot(dos_t[:64, :t], pb[:t], preferred_element_type=F32),
                         jnp.dot(dos_t[64:, t:], pb[t:], preferred_element_type=F32)], axis=0)
                    dkt_ref[j] += jnp.concatenate(
                        [jnp.dot(qs_t[:hw, :t], dsb[:t], preferred_element_type=F32),
                         jnp.dot(qs_t[hw:, t:], dsb[t:], preferred_element_type=F32)], axis=0)
                else:
                    dvt_ref[j] += jnp.dot(dos_t, pb, preferred_element_type=F32)
                    dkt_ref[j] += jnp.dot(qs_t, dsb, preferred_element_type=F32)
                if has_bias:
                    db_ref[0, 0, j] += jnp.sum(ds[:t], axis=0, keepdims=True)
                    db_ref[0, 1, j] += jnp.sum(ds[t:], axis=0, keepdims=True)
                    rsum = rsum + jnp.sum(ds, axis=-1, keepdims=True)
                return dq + jnp.dot(dsb, kt, preferred_element_type=F32), rsum

            init = (jnp.zeros((2 * t, qw), F32), jnp.zeros((2 * t, 1), F32))
            carry = lax.fori_loop(0, i, functools.partial(kv_step, masked=False), init)
            dq, rsum = kv_step(i, carry, True)
            dq = dq * scale
            dq_ref[pl.ds(r0, t), :] = jnp.where(lane_lt_hw, dq[:t], dq[t:]).astype(qk_dtype)
            if has_bias:
                rsum_row = jnp.broadcast_to(rsum, (2 * t, LANES)).T[0:1]
                dr_ref[0, 0, i] = rsum_row[:, :t]
                dr_ref[0, 1, i] = rsum_row[:, t:]
            return 0

        lax.fori_loop(0, nb, q_block, 0)

        def k_block(j, _):
            c0 = pl.multiple_of(j * t, t)
            dk_ref[pl.ds(c0, t), :] = (dkt_ref[j].T * scale).astype(qk_dtype)
            dv_ref[pl.ds(c0, t), :] = dvt_ref[j].T.astype(BF16)
            return 0

        lax.fori_loop(0, nb, k_block, 0)
        _ride_wait(rider, ride_refs, pl.program_id(0) == n_pairs - 1)

    in_specs = [_resident((s, qw), lambda p: (0, q_off + p)), _resident((s, qw), lambda p: (0, k_off + p)),
                _resident((s, LANES), lambda p: (0, v_off + p)),
                _resident((s, LANES), lambda p: (0, p)), _resident((s, LANES), lambda p: (0, p)),
                _resident((1, 2, s, 1), lambda p: (p, 0, 0, 0))]
    args = [q, k, v, do, o, lse]
    if stacked:
        assert qw == LANES and qk_dtype == BF16
        out_specs = [pl.BlockSpec((3, s, LANES), lambda p: (0, 0, p))]
        out_shape = [jax.ShapeDtypeStruct((3, s, n_pairs * LANES), BF16)]
    else:
        out_specs = [pl.BlockSpec((s, qw), lambda p: (0, p)), pl.BlockSpec((s, qw), lambda p: (0, p)),
                     pl.BlockSpec((s, LANES), lambda p: (0, p))]
        out_shape = [jax.ShapeDtypeStruct((s, n_pairs * qw), qk_dtype), jax.ShapeDtypeStruct((s, n_pairs * qw), qk_dtype),
                     jax.ShapeDtypeStruct((s, n_pairs * LANES), BF16)]
    if has_bias:
        in_specs.append(_resident((1, 2, nb, 1, t), lambda p: (p, 0, 0, 0, 0)))
        args.append(bias)
        for _ in range(2):
            out_specs.append(pl.BlockSpec((1, 2, nb, 1, t), lambda p: (p, 0, 0, 0, 0)))
            out_shape.append(jax.ShapeDtypeStruct((n_pairs, 2, nb, 1, t), F32))
    scratch = [pltpu.VMEM((nb, qw, t), F32), pltpu.VMEM((nb, LANES, t), F32)]
    scratch += _add_rider(rider, in_specs, args, out_specs, out_shape)
    return pl.pallas_call(
        body, name=name, grid=(n_pairs,), in_specs=in_specs, out_specs=out_specs, out_shape=out_shape,
        scratch_shapes=scratch,
        compiler_params=_params(("parallel",) if rider is None else ("arbitrary",)),
    )(*args)


def _alibi_slope(h):
    return 2.0 ** (-8.0 * (h + 1.0) / SWA_HEADS)


SWA_ROWS = 512
SWA_SCALE = SWA_DIM ** -0.5


def _swa_geometry(i):
    w = WINDOW
    r0 = pl.multiple_of(i * w, w)
    b0 = pl.multiple_of(jnp.maximum(i - 1, 0) * w, w)
    row = lax.broadcasted_iota(jnp.int32, (w, 2 * w), 0)
    col = lax.broadcasted_iota(jnp.int32, (w, 2 * w), 1)
    dist = row - col + (r0 - b0)
    valid = (dist >= 0) & (dist < w)
    return r0, b0, dist.astype(F32), valid


def _swa_q_head(qblk, h):
    kv = h // (SWA_HEADS // SWA_KV_HEADS)
    if h % 2 != kv:
        qblk = pltpu.roll(qblk, 64, axis=1)
    return jnp.where(_head_mask(qblk.shape, kv), qblk, 0.0)


SWA_GROUP = SWA_HEADS // SWA_KV_HEADS


def _swa_stack(ref, rs, grp):
    parts = []
    for a in range(SWA_GROUP):
        h = SWA_GROUP * grp + a
        parts.append(_swa_q_head(ref[rs, (h // 2) * LANES:(h // 2 + 1) * LANES].astype(F32), h))
    return jnp.concatenate(parts, axis=0)


def _swa_unstack(x, grp):
    tiles = []
    for a in range(SWA_GROUP):
        h = SWA_GROUP * grp + a
        tile = x[a * WINDOW:(a + 1) * WINDOW]
        tiles.append(pltpu.roll(tile, 64, axis=1) if h % 2 != grp else tile)
    return tiles


def _swa_head_column(vals):
    return jnp.concatenate([jnp.full((WINDOW, 1), v, F32) for v in vals], axis=0)


def _swa_logits(qs, kb, dist, valid, grp):
    slopes = _swa_head_column([_alibi_slope(SWA_GROUP * grp + a) for a in range(SWA_GROUP)])
    dist4 = jnp.concatenate([dist] * SWA_GROUP, axis=0)
    valid4 = jnp.concatenate([valid] * SWA_GROUP, axis=0)
    sc = lax.dot_general(qs, kb, _NT, preferred_element_type=F32) * SWA_SCALE - slopes * dist4
    return jnp.where(valid4, sc, NEG_INF)


def _swa_merge_heads(tiles):
    lt64 = lax.broadcasted_iota(jnp.int32, (WINDOW, LANES), 1) < 64
    return jnp.concatenate([jnp.where(lt64, tiles[2 * b], tiles[2 * b + 1]) for b in range(SWA_HEADS // 2)], axis=1)


def _swa_fwd(z0b, sinks, *, name):
    s = z0b.shape[0]
    w = WINDOW
    rows = min(SWA_ROWS, s)
    per_step = rows // w
    qcols = SWA_HEADS * SWA_DIM

    def body(sink_ref, q_ref, k_ref, v_ref, o_ref, lse_ref):
        g = pl.program_id(0)
        for ii in range(per_step):
            rs = slice(ii * w, (ii + 1) * w)
            r0, b0, dist, valid = _swa_geometry(g * per_step + ii)
            kb = k_ref[pl.ds(b0, 2 * w), :]
            vb = v_ref[pl.ds(b0, 2 * w), :]
            o_tiles = []
            for h in range(SWA_HEADS):
                kv = h // SWA_GROUP
                qh = _swa_q_head(q_ref[rs, (h // 2) * LANES:(h // 2 + 1) * LANES].astype(F32), h).astype(BF16)
                sc = lax.dot_general(qh, kb, _NT, preferred_element_type=F32) * SWA_SCALE - _alibi_slope(h) * dist
                sc = jnp.where(valid, sc, NEG_INF)
                sink = sink_ref[0, h]
                m = jnp.maximum(jnp.max(sc, axis=-1, keepdims=True), sink)
                p = jnp.exp(sc - m)
                l = jnp.sum(p, axis=-1, keepdims=True) + jnp.exp(sink - m)
                oh = jnp.dot(p.astype(BF16), vb, preferred_element_type=F32) / l
                o_tiles.append(pltpu.roll(oh, 64, axis=1) if h % 2 != kv else oh)
                lse_ref[h, rs, :] = m + jnp.log(l)
            o_ref[rs, :] = _swa_merge_heads(o_tiles)

    return pl.pallas_call(
        body, name=name, grid=(s // rows,),
        in_specs=[pl.BlockSpec(memory_space=pltpu.SMEM),
                  pl.BlockSpec((rows, qcols), lambda g: (g, 0)),
                  pl.BlockSpec((s, LANES), lambda g: (0, 4)), pl.BlockSpec((s, LANES), lambda g: (0, 5))],
        out_specs=[pl.BlockSpec((rows, qcols), lambda g: (g, 0)), pl.BlockSpec((SWA_HEADS, rows, 1), lambda g: (0, g, 0))],
        out_shape=[jax.ShapeDtypeStruct((s, qcols), F32), jax.ShapeDtypeStruct((SWA_HEADS, s, 1), F32)],
        compiler_params=_params(("parallel",)),
    )(sinks, z0b, z0b, z0b)


def _swa_bwd(z0b, sinks, do, o, lse, *, name):
    s = z0b.shape[0]
    w = WINDOW
    rows = min(SWA_ROWS, s)
    per_step = rows // w
    qcols = SWA_HEADS * SWA_DIM
    nblk = s // w

    def body(sink_ref, q_ref, k_ref, v_ref, do_ref, o_ref, lse_ref, dq_ref, dkt_ref, dvt_ref, dsink_ref):
        g = pl.program_id(0)

        @pl.when(g == 0)
        def _():
            dkt_ref[...] = jnp.zeros_like(dkt_ref)
            dvt_ref[...] = jnp.zeros_like(dvt_ref)
            dsink_ref[...] = jnp.zeros_like(dsink_ref)

        for ii in range(per_step):
            i = g * per_step + ii
            rs = slice(ii * w, (ii + 1) * w)
            r0, b0, dist, valid = _swa_geometry(i)
            j0 = jnp.maximum(i - 1, 0)
            kb = k_ref[pl.ds(b0, 2 * w), :]
            vb = v_ref[pl.ds(b0, 2 * w), :]
            dq_tiles = []
            for grp in range(SWA_KV_HEADS):
                heads = [SWA_GROUP * grp + a for a in range(SWA_GROUP)]
                qs32 = _swa_stack(q_ref, rs, grp)
                dos32 = _swa_stack(do_ref, rs, grp)
                delta = jnp.sum(dos32 * _swa_stack(o_ref, rs, grp), axis=-1, keepdims=True)
                lse = jnp.concatenate([lse_ref[h, rs, :] for h in heads], axis=0)
                sink = _swa_head_column([sink_ref[0, h] for h in heads])
                p = jnp.exp(_swa_logits(qs32.astype(BF16), kb, dist, valid, grp) - lse)
                dp = lax.dot_general(dos32.astype(BF16), vb, _NT, preferred_element_type=F32)
                ds = p * (dp - delta)
                dsb = ds.astype(BF16)
                d_sink = jnp.exp(sink - lse) * delta
                for a, h in enumerate(heads):
                    dsink_ref[h:h + 1, :] += jnp.broadcast_to(-jnp.sum(d_sink[a * w:(a + 1) * w]), (1, LANES))
                dvt = jnp.dot(dos32.T.astype(BF16), p.astype(BF16), preferred_element_type=F32)
                dkt = jnp.dot(qs32.T.astype(BF16), dsb, preferred_element_type=F32) * SWA_SCALE
                dvt_ref[j0] += dvt[:, :w]
                dvt_ref[j0 + 1] += dvt[:, w:]
                dkt_ref[j0] += dkt[:, :w]
                dkt_ref[j0 + 1] += dkt[:, w:]
                dq_tiles += _swa_unstack(jnp.dot(dsb, kb, preferred_element_type=F32) * SWA_SCALE, grp)
            dq_ref[rs, :] = _swa_merge_heads(dq_tiles)

    return pl.pallas_call(
        body, name=name, grid=(s // rows,),
        in_specs=[pl.BlockSpec(memory_space=pltpu.SMEM),
                  pl.BlockSpec((rows, qcols), lambda g: (g, 0)),
                  pl.BlockSpec((s, LANES), lambda g: (0, 4)), pl.BlockSpec((s, LANES), lambda g: (0, 5)),
                  pl.BlockSpec((rows, qcols), lambda g: (g, 0)), pl.BlockSpec((rows, qcols), lambda g: (g, 0)),
                  pl.BlockSpec((SWA_HEADS, rows, 1), lambda g: (0, g, 0))],
        out_specs=[pl.BlockSpec((rows, qcols), lambda g: (g, 0)),
                   pl.BlockSpec((nblk, LANES, w), lambda g: (0, 0, 0)),
                   pl.BlockSpec((nblk, LANES, w), lambda g: (0, 0, 0)),
                   pl.BlockSpec((SWA_HEADS, LANES), lambda g: (0, 0))],
        out_shape=[jax.ShapeDtypeStruct((s, qcols), F32),
                   jax.ShapeDtypeStruct((nblk, LANES, w), F32), jax.ShapeDtypeStruct((nblk, LANES, w), F32),
                   jax.ShapeDtypeStruct((SWA_HEADS, LANES), F32)],
        compiler_params=_params(("arbitrary",)),
    )(sinks, z0b, z0b, z0b, do, o, lse)


CUM_T = 256


def _split3(x):
    hi = x.astype(BF16)
    r1 = x - hi.astype(F32)
    mid = r1.astype(BF16)
    lo = (r1 - mid.astype(F32)).astype(BF16)
    return hi, mid, lo


def _tri_dot(tri, x):
    hi, mid, lo = _split3(x)
    out = jnp.dot(tri, hi, preferred_element_type=F32)
    out = out + jnp.dot(tri, mid, preferred_element_type=F32)
    return out + jnp.dot(tri, lo, preferred_element_type=F32)


def _logf_fwd(zf, bf, *, name):
    s = zf.shape[0]
    t = CUM_T
    nb = s // t

    def body(z_ref, b_ref, c_ref, carry_ref):
        i = pl.program_id(0)

        @pl.when(i == 0)
        def _():
            carry_ref[...] = jnp.zeros_like(carry_ref)

        x = z_ref[...] + b_ref[...]
        lf = jnp.minimum(x, 0.0) - jnp.log(1.0 + jnp.exp(-jnp.abs(x)))
        row = lax.broadcasted_iota(jnp.int32, (t, t), 0)
        col = lax.broadcasted_iota(jnp.int32, (t, t), 1)
        tri = jnp.where(col <= row, 1.0, 0.0).astype(BF16)
        c = _tri_dot(tri, lf) + carry_ref[...]
        c_ref[...] = c
        carry_ref[...] = c[t - 1:t, :]

    return pl.pallas_call(
        body, name=name, grid=(nb,),
        in_specs=[pl.BlockSpec((t, LANES), lambda i: (i, 0)), pl.BlockSpec((1, LANES), lambda i: (0, 0))],
        out_specs=pl.BlockSpec((t, LANES), lambda i: (i, 0)),
        out_shape=jax.ShapeDtypeStruct((s, LANES), F32),
        scratch_shapes=[pltpu.VMEM((1, LANES), F32)],
        compiler_params=_params(("arbitrary",)),
    )(zf, bf)


def _logf_bwd(dc, zf, bf, *, name):
    s = zf.shape[0]
    t = CUM_T
    nb = s // t

    def body(dc_ref, z_ref, b_ref, dz_ref, db_ref, carry_ref):
        i = pl.program_id(0)

        @pl.when(i == 0)
        def _():
            carry_ref[...] = jnp.zeros_like(carry_ref)
            db_ref[...] = jnp.zeros_like(db_ref)

        row = lax.broadcasted_iota(jnp.int32, (t, t), 0)
        col = lax.broadcasted_iota(jnp.int32, (t, t), 1)
        tri = jnp.where(col >= row, 1.0, 0.0).astype(BF16)
        dlf = _tri_dot(tri, dc_ref[...]) + carry_ref[...]
        carry_ref[...] = dlf[0:1, :]
        x = z_ref[...] + b_ref[...]
        dz = dlf * _sigmoid(-x)
        dz_ref[...] = dz.astype(BF16)
        db_ref[...] += jnp.sum(dz, axis=0, keepdims=True)

    return pl.pallas_call(
        body, name=name, grid=(nb,),
        in_specs=[pl.BlockSpec((t, LANES), lambda i: (nb - 1 - i, 0)), pl.BlockSpec((t, LANES), lambda i: (nb - 1 - i, 0)),
                  pl.BlockSpec((1, LANES), lambda i: (0, 0))],
        out_specs=[pl.BlockSpec((t, LANES), lambda i: (nb - 1 - i, 0)), pl.BlockSpec((1, LANES), lambda i: (0, 0))],
        out_shape=[jax.ShapeDtypeStruct((s, LANES), BF16), jax.ShapeDtypeStruct((1, LANES), F32)],
        scratch_shapes=[pltpu.VMEM((1, LANES), F32)],
        compiler_params=_params(("arbitrary",)),
    )(dc, zf, bf)


def _sum_pieces(p_ref):
    g = p_ref[0].astype(F32)
    for k in range(1, N_DEV):
        g = g + p_ref[k].astype(F32)
    return g


def _adam_update(g, w, m, v):
    bc1 = 1.0 - ADAM_B1 ** ADAM_STEP
    bc2 = 1.0 - ADAM_B2 ** ADAM_STEP
    nm = ADAM_B1 * m + (1.0 - ADAM_B1) * g
    nv = ADAM_B2 * v + (1.0 - ADAM_B2) * (g * g)
    m_hat = nm / bc1
    v_hat = nv / bc2
    return -ADAM_LR * (m_hat / (jnp.sqrt(v_hat) + ADAM_EPS) + ADAM_WD * w), nm, nv


def _adamw(pieces, w, m, v, *, name):
    rows, cols = w.shape
    tr = _tile(rows, (RB1, RB0, SMALL_ROWS))

    def body(p_ref, w_ref, m_ref, v_ref, g_ref, d_ref, nm_ref, nv_ref):
        g = _sum_pieces(p_ref)
        g_ref[...] = g
        d_ref[...], nm_ref[...], nv_ref[...] = _adam_update(g, w_ref[...], m_ref[...], v_ref[...])

    spec = pl.BlockSpec((tr, cols), lambda i: (i, 0))
    shape = jax.ShapeDtypeStruct((rows, cols), F32)
    return pl.pallas_call(
        body, name=name, grid=(rows // tr,),
        in_specs=[pl.BlockSpec((N_DEV, tr, cols), lambda i: (0, i, 0)), spec, spec, spec],
        out_specs=[spec, spec, spec, spec], out_shape=[shape, shape, shape, shape],
        compiler_params=_params(("parallel",)),
    )(pieces, w, m, v)


def _sum8(pieces, rows, *, name):
    cols = pieces.shape[2]
    tr = _tile(rows, (176, 96))

    def body(p_ref, g_ref):
        g_ref[...] = _sum_pieces(p_ref)

    return pl.pallas_call(
        body, name=name, grid=(rows // tr,),
        in_specs=[pl.BlockSpec((N_DEV, tr, cols), lambda i: (0, i, 0))],
        out_specs=pl.BlockSpec((tr, cols), lambda i: (i, 0)),
        out_shape=jax.ShapeDtypeStruct((rows, cols), F32),
        compiler_params=_params(("parallel",)),
    )(pieces)


def _adamw_columns(g, w, m, v, *, name):
    n, _, k = w.shape
    tr = n // 2

    def body(g_ref, w_ref, m_ref, v_ref, d_ref, nm_ref, nv_ref):
        d_ref[...], nm_ref[...], nv_ref[...] = _adam_update(g_ref[...], w_ref[...], m_ref[...], v_ref[...])

    spec = pl.BlockSpec((tr, 1, k), lambda i: (i, 0, 0))
    shape = jax.ShapeDtypeStruct((n, 1, k), F32)
    return pl.pallas_call(
        body, name=name, grid=(n // tr,), in_specs=[spec, spec, spec, spec],
        out_specs=[spec, spec, spec], out_shape=[shape, shape, shape],
        compiler_params=_params(("parallel",)),
    )(g, w, m, v)


MESH = pl.DeviceIdType.MESH
ANY = pl.BlockSpec(memory_space=pl.ANY)


def _all_gather(shard, *, name):
    rows, lanes = shard.shape

    def body(x_ref, out_ref, send_sems, recv_sems, local_sem):
        x, y, c = lax.axis_index("x"), lax.axis_index("y"), lax.axis_index("c")
        me, sibling = (x, y, c), (x, y, 1 - c)
        chips = [(1 - x, y), (x, 1 - y), (1 - x, 1 - y)]

        def block(px, py, pc):
            return out_ref.at[4 * px + 2 * py + pc]

        def copy(k, blk, to, src=None):
            return pltpu.make_async_remote_copy(
                src_ref=block(*blk) if src is None else src, dst_ref=block(*blk),
                send_sem=send_sems.at[k], recv_sem=recv_sems.at[k], device_id=to, device_id_type=MESH)

        mine = pltpu.make_async_copy(x_ref, block(*me), local_sem)
        mine.start()
        first = [copy(0, me, sibling, src=x_ref)]
        first += [copy(1 + j, me, (*chip, c), src=x_ref) for j, chip in enumerate(chips)]
        for cp in first:
            cp.start()
        passed = [copy(4 + j, (*chip, c), sibling) for j, chip in enumerate(chips)]
        for j, chip in enumerate(chips):
            copy(1 + j, (*chip, c), me).wait_recv()
            passed[j].start()
        copy(0, sibling, me).wait_recv()
        for j, chip in enumerate(chips):
            copy(4 + j, (*chip, 1 - c), me).wait_recv()
        for cp in first + passed:
            cp.wait_send()
        mine.wait()

    return pl.pallas_call(
        body, name=name, out_shape=jax.ShapeDtypeStruct((N_DEV, rows, lanes), shard.dtype),
        in_specs=[ANY], out_specs=ANY,
        scratch_shapes=[pltpu.SemaphoreType.DMA((7,)), pltpu.SemaphoreType.DMA((7,)), pltpu.SemaphoreType.DMA(())],
    )(shard)


def _peer_copies(kind, src_ref, out_ref, send_sems, recv_sems, local_sem):
    x, y, c = lax.axis_index("x"), lax.axis_index("y"), lax.axis_index("c")
    me = 4 * x + 2 * y + c

    def src(idx):
        return src_ref.at[idx] if kind == "exchange" else src_ref

    mine = None if local_sem is None else pltpu.make_async_copy(src(me), out_ref.at[me], local_sem)
    copies = []
    for r in (2, 4, 6) if kind == "across" else range(1, N_DEV):
        px = 1 - x if r & 4 else x
        py = 1 - y if r & 2 else y
        pc = 1 - c if r & 1 else c
        copies.append(pltpu.make_async_remote_copy(
            src_ref=src(4 * px + 2 * py + pc), dst_ref=out_ref.at[me],
            send_sem=send_sems.at[r - 1], recv_sem=recv_sems.at[r - 1],
            device_id=(px, py, pc), device_id_type=MESH))
    return mine, copies


def _to_other_core(shard, land, *, name):
    def body(src_ref, land_ref, out_ref, send_sems, recv_sems):
        x, y, c = lax.axis_index("x"), lax.axis_index("y"), lax.axis_index("c")
        copies = []
        for k, r in enumerate((0, 2, 4, 6)):
            slot = 4 * (1 - x if r & 4 else x) + 2 * (1 - y if r & 2 else y) + c
            copies.append(pltpu.make_async_remote_copy(
                src_ref=src_ref if r == 0 else land_ref.at[slot], dst_ref=out_ref.at[slot],
                send_sem=send_sems.at[k], recv_sem=recv_sems.at[k], device_id=(x, y, 1 - c), device_id_type=MESH))
        for cp in copies:
            cp.start()
        for cp in copies:
            cp.wait()

    return pl.pallas_call(
        body, name=name, out_shape=jax.ShapeDtypeStruct(land.shape, land.dtype), in_specs=[ANY, ANY], out_specs=ANY,
        input_output_aliases={1: 0}, scratch_shapes=[pltpu.SemaphoreType.DMA((4,)), pltpu.SemaphoreType.DMA((4,))],
    )(shard, land)


PEER_SEMS = [pltpu.SemaphoreType.DMA((7,)), pltpu.SemaphoreType.DMA((7,)), pltpu.SemaphoreType.DMA(())]


HBM = pl.BlockSpec(memory_space=pltpu.HBM)
SEMAPHORES = pl.BlockSpec(memory_space=pltpu.SEMAPHORE)


def _peer_start(kind, arr, *, name):
    land = lax.empty((N_DEV,) + arr.shape[-2:], arr.dtype)

    def body(src_ref, land_ref, send_sems, recv_sems, src_thru, land_thru, token):
        _, copies = _peer_copies(kind, src_ref, land_ref, send_sems, recv_sems, None)
        for cp in copies:
            cp.start()
        token[...] = jnp.zeros_like(token)

    return pl.pallas_call(
        body, name=name,
        out_shape=(pltpu.SemaphoreType.DMA((N_DEV - 1,)), pltpu.SemaphoreType.DMA((N_DEV - 1,)),
                   pltpu.HBM(arr.shape, arr.dtype), pltpu.HBM(land.shape, land.dtype), jax.ShapeDtypeStruct((8, LANES), F32)),
        in_specs=(HBM, HBM), out_specs=(SEMAPHORES, SEMAPHORES, HBM, HBM, pl.BlockSpec(memory_space=pltpu.VMEM)),
        input_output_aliases={0: 2, 1: 3},
        compiler_params=pltpu.CompilerParams(has_side_effects=pltpu.SideEffectType.DATAFLOW_SIDE_EFFECTING),
    )(pltpu.with_memory_space_constraint(arr, pltpu.HBM), pltpu.with_memory_space_constraint(land, pltpu.HBM))


def _peer_wait(kind, send_sems, recv_sems, src_thru, land_thru, after, *, name):
    def body(src_ref, land_ref, send_sems, recv_sems, *_):
        _, copies = _peer_copies(kind, src_ref, land_ref, send_sems, recv_sems, None)
        for cp in copies:
            cp.wait_send()
            cp.wait_recv()

    return pl.pallas_call(
        body, name=name,
        out_shape=(pltpu.HBM(src_thru.shape, src_thru.dtype), pltpu.HBM(land_thru.shape, land_thru.dtype)),
        in_specs=(HBM, HBM, SEMAPHORES, SEMAPHORES) + (ANY,) * len(after), out_specs=(HBM, HBM),
        input_output_aliases={0: 0, 1: 1},
        compiler_params=pltpu.CompilerParams(has_side_effects=pltpu.SideEffectType.DATAFLOW_SIDE_EFFECTING),
    )(src_thru, land_thru, send_sems, recv_sems, *after)


def _add_rider(rider, in_specs, args, out_specs, out_shape):
    if rider is None:
        return []
    _, arr = rider
    in_specs.append(ANY)
    args.append(arr)
    out_specs.append(ANY)
    out_shape.append(jax.ShapeDtypeStruct((N_DEV,) + arr.shape[-2:], arr.dtype))
    return list(PEER_SEMS)


def _split_rider(refs, rider, n_in, n_out):
    if rider is None:
        return refs, None
    refs = list(refs)
    rin = refs.pop(n_in)
    rout = refs.pop(n_in + n_out)
    return refs[:-3], (rin, rout, *refs[-3:])


def _ride_start(rider, ride_refs, first):
    if rider is None:
        return

    @pl.when(first)
    def _():
        mine, copies = _peer_copies(rider[0], *ride_refs)
        mine.start()
        for cp in copies:
            cp.start()


def _ride_wait(rider, ride_refs, last):
    if rider is None:
        return

    @pl.when(last)
    def _():
        mine, copies = _peer_copies(rider[0], *ride_refs)
        for cp in copies:
            cp.wait()
        mine.wait()


def _gathered_cols(blocks, kdim):
    n = blocks.shape[1] * WIDE // kdim
    return blocks.reshape(N_DEV, kdim, n).transpose(1, 0, 2).reshape(kdim, N_DEV * n)


def _scatter_cols(dw):
    kdim, n8 = dw.shape
    n = n8 // N_DEV
    return dw.reshape(kdim, N_DEV, n).transpose(1, 0, 2).reshape(N_DEV, kdim * n // WIDE, WIDE)


def _pad_rows(a, rows):
    pad = [(0, 0)] * a.ndim
    pad[-2] = (0, rows - a.shape[-2])
    return jnp.pad(a, pad)


def _layer0_in_weight_t(wt):
    cq, ckv, kpe = wt[0:256], wt[256:384], wt[384:416]
    q_s, k_s, v_s, gate = wt[416:928], wt[928:1056], wt[1056:1184], wt[1184:2208]
    z = jnp.zeros((64, wt.shape[1]), wt.dtype)
    return jnp.concatenate([gate, cq, ckv, z, kpe, z[:32], q_s, k_s, v_s], axis=0)


def _layer0_in_grad_t(dwt):
    gate, cq, ckv, kpe = dwt[0:1024], dwt[1024:1280], dwt[1280:1408], dwt[1472:1504]
    q_s, k_s, v_s = dwt[1536:2048], dwt[2048:2176], dwt[2176:2304]
    return jnp.concatenate([cq, ckv, kpe, q_s, k_s, v_s, gate], axis=0)


def _layer1_in_weight_t(wt):
    main = jnp.concatenate([wt[:3 * D_MODEL], wt[3 * D_MODEL + FOX_HEADS:]], axis=0)
    return main, _pad_rows(wt[3 * D_MODEL:3 * D_MODEL + FOX_HEADS], LANES)


def _layer1_in_unpack(gath, *, name):
    n_main = 3 * D_MODEL

    def body(g_ref, w_ref, f_ref, o1_ref, o0_ref):
        f_ref[...] = jnp.zeros_like(f_ref)
        for p in range(N_DEV):
            o1_ref[128 * p:128 * p + 128, :] = g_ref[p, RA1:RA1 + 128, :]
            o0_ref[128 * p:128 * p + 128, :] = g_ref[p, RA1 + 128:RA1 + 256, :]
            lo, hi = p * N_O_IN, (p + 1) * N_O_IN
            for ref, first, start, stop in ((w_ref, 0, lo, min(hi, n_main)),
                                            (f_ref, -n_main, max(lo, n_main), min(hi, n_main + FOX_HEADS)),
                                            (w_ref, -FOX_HEADS, max(lo, n_main + FOX_HEADS), hi)):
                if start < stop:
                    ref[start + first:stop + first, :] = g_ref[p, start - lo:stop - lo, :]

    return pl.pallas_call(
        body, name=name, grid=(1,), in_specs=[_resident((N_DEV, RA1 + 256, WIDE), lambda i: (0, 0, 0))],
        out_specs=[_resident((rows, WIDE), lambda i: (0, 0)) for rows in (n_main + D_MODEL, LANES, D_MODEL, D_MODEL)],
        out_shape=[jax.ShapeDtypeStruct((rows, WIDE), gath.dtype) for rows in (n_main + D_MODEL, LANES, D_MODEL, D_MODEL)],
        compiler_params=_params(("arbitrary",)),
    )(gath)


def _late_grads_pack(d_qkv, d_wft, d_gate, d_wo1, d_wo0, d_o_g, *, name):
    n_main = 3 * D_MODEL
    arrays = (d_qkv, d_wft, d_gate, d_wo1, d_wo0, d_o_g)

    def body(q_ref, f_ref, g_ref, o1_ref, o0_ref, og_ref, out_ref):
        for p in range(N_DEV):
            lo, hi = p * N_O_IN, (p + 1) * N_O_IN
            for ref, first, start, stop in ((q_ref, 0, lo, min(hi, n_main)),
                                            (f_ref, -n_main, max(lo, n_main), min(hi, n_main + FOX_HEADS)),
                                            (g_ref, -n_main - FOX_HEADS, max(lo, n_main + FOX_HEADS), hi)):
                if start < stop:
                    out_ref[p, start - lo:stop - lo, :] = ref[start + first:stop + first, :]
            out_ref[p, N_O_IN:RA1, :] = jnp.zeros((RA1 - N_O_IN, WIDE), out_ref.dtype)
            out_ref[p, RA1:RA1 + 128, :] = o1_ref[128 * p:128 * p + 128, :]
            out_ref[p, RA1 + 128:RA1 + 256, :] = o0_ref[128 * p:128 * p + 128, :]
            out_ref[p, RA1 + 256:, :] = og_ref[p]

    return pl.pallas_call(
        body, name=name, grid=(1,), in_specs=[_resident(a.shape, lambda i, n=a.ndim: (0,) * n) for a in arrays],
        out_specs=_resident((N_DEV, RA1 + RB1, WIDE), lambda i: (0, 0, 0)),
        out_shape=jax.ShapeDtypeStruct((N_DEV, RA1 + RB1, WIDE), BF16), compiler_params=_params(("arbitrary",)),
    )(*arrays)


def _q_up_weight(w):
    return jnp.pad(w.reshape(MLA_Q_RANK, MLA_HEADS, 96), ((0, 0), (0, 0), (0, 32))).reshape(MLA_Q_RANK, MLA_HEADS * LANES)


def _q_up_grad(dwp):
    return dwp.reshape(MLA_Q_RANK, MLA_HEADS, LANES)[:, :, :96].reshape(MLA_Q_RANK, MLA_HEADS * 96)


def _kv_up_weight(w):
    w4 = w.reshape(MLA_KV_RANK, MLA_HEADS, 2, 64)
    kp = jnp.pad(w4[:, :, 0, :], ((0, 0), (0, 0), (0, 64))).reshape(MLA_KV_RANK, MLA_HEADS * LANES)
    vp = w4[:, :, 1, :].reshape(MLA_KV_RANK, MLA_HEADS * 64)
    return jnp.concatenate([kp, vp], axis=1)


def _kv_up_grad(dwp):
    dk = dwp[:, :MLA_HEADS * LANES].reshape(MLA_KV_RANK, MLA_HEADS, LANES)[:, :, :64]
    dv = dwp[:, MLA_HEADS * LANES:].reshape(MLA_KV_RANK, MLA_HEADS, 64)
    return jnp.stack([dk, dv], axis=2).reshape(MLA_KV_RANK, MLA_HEADS * LANES)


def _pad_lanes(a):
    return jnp.pad(a, ((0, 0), (0, LANES - a.shape[1])))


def _small_pack(g_in, g_final, g_q_a, g_kv_a, sinks, b_f, loss):
    rows = [g_in.reshape(8, LANES), g_final.reshape(8, LANES), g_q_a.reshape(2, LANES), g_kv_a.reshape(1, LANES),
            _pad_lanes(sinks.reshape(1, -1)), _pad_lanes(b_f.reshape(1, -1)), _pad_lanes(loss.reshape(1, 1)),
            jnp.zeros((2, LANES), F32)]
    return jnp.concatenate(rows, axis=0)


def _small_unpack(a):
    return (a[0:8].reshape(1, D_MODEL), a[8:16].reshape(D_MODEL), a[16:18].reshape(1, MLA_Q_RANK),
            a[18:19].reshape(1, MLA_KV_RANK), a[19:20, :SWA_HEADS], a[20:21, :FOX_HEADS], a[21, 0])


def _local_step(x, positions, target, e_g_in, early, e_g_q_a, e_g_kv_a, e_sinks,
                late, o_b_f, g_final, scatter1=None, scatter0=None):
    s = x.shape[0]
    mla_scale = (MLA_NOPE + MLA_ROPE) ** -0.5
    fox_scale = FOX_DIM ** -0.5
    n0a = Z0A_UNITS * LANES

    inv_freq = 1.0 / (ROPE_THETA ** (jnp.arange(0, MLA_ROPE, 2, dtype=F32) / MLA_ROPE))
    ang = positions.astype(F32)[:, None] * inv_freq
    cos, sin = jnp.cos(ang), jnp.sin(ang)
    ones, zeros = jnp.ones((s, 64), F32), jnp.zeros((s, 64), F32)
    cos_t = jnp.concatenate([ones, cos, cos, ones[:, :32]], axis=1)
    sin_t = jnp.concatenate([zeros, -sin, sin, zeros[:, :32]], axis=1)
    cos_t, sin_t = lax.optimization_barrier((cos_t, sin_t))

    if len(early) == 3:
        h0 = _rmsnorm_fwd(x, e_g_in, width=D_MODEL, col_blk=0, name="l0_norm")
        w0t, wq, wkv = early
    else:
        pending, token, unpack, prep = early
        h0 = _rmsnorm_fwd(x, e_g_in, width=D_MODEL, col_blk=0, name="l0_norm", after=[token])
        sent, across = _peer_wait("across", *pending, after=[h0] + prep, name="weights0_wait")
        w0t, wq, wkv = unpack(sent, _to_other_core(sent, across, name="weights0_over"))
    z0a, z0b = _matmul_rows([(h0, w0t, True)], [], [], lambda r: (r[:, :n0a], r[:, n0a:]),
                            [("rows", n0a, F32), ("rows", Z0B_UNITS * LANES, BF16)], name="l0_in")
    cqn = _rmsnorm_fwd(z0a, e_g_q_a, width=MLA_Q_RANK, col_blk=4, name="l0_q_norm")
    ckvn = _rmsnorm_fwd(z0a, e_g_kv_a, width=MLA_KV_RANK, col_blk=10, name="l0_kv_norm")
    rope_rows = [(cos_t, LANES, 0), (sin_t, LANES, 0)]
    qm, = _matmul_rows([(cqn, wq, False)], rope_rows, [], _rope_q_epilogue, [("rows", MLA_HEADS * LANES, BF16)],
                       name="l0_q_up")
    kvm, = _matmul_rows([(ckvn, wkv, False)], [(z0a, LANES, 11)] + rope_rows, [], _rope_k_epilogue,
                        [("rows", MLA_HEADS * (LANES + MLA_V), BF16)], name="l0_kv_up")
    gathers = len(late) == 2
    res = _flash_fwd(qm, kvm, kvm, None, n_pairs=MLA_HEADS // 2, hw=LANES, q_off=0, k_off=0, v_off=MLA_HEADS,
                     scale=mla_scale, name="l0_mla_fwd", rider=("gather", late[0]) if gathers else None)
    o_mla, lse_mla = res[0], res[1]
    wo0, o_g_in, w1t, wft, wo1 = late[1](res[2]) if gathers else late
    o_swa, lse_swa = _swa_fwd(z0b, e_sinks, name="l0_swa_fwd")
    half = D_MODEL // 2

    x1, h1, og0 = _matmul_rows(
        [(None, wo0, False)], [(o_mla, half, 0), (o_swa, half, 0), (z0a, D_MODEL, 0), (x, D_MODEL, 0)], [o_g_in],
        lambda r, om, osw, gt, xt, g, made: (*_residual_norm_epilogue(r, xt, g), made),
        [("rows", D_MODEL, F32), ("rows", D_MODEL, BF16), ("rows", D_MODEL, BF16)], name="l0_out",
        prologue=lambda om, osw, gt, xt, g: _gated([om, osw], gt))
    z1, gate1, zf = _matmul_rows(
        [(None, w1t, True), (None, wft, True)], [(h1, D_MODEL, 0)], [],
        lambda r, h, made: (r[0][:, :3 * D_MODEL], r[0][:, 3 * D_MODEL:], r[1]),
        [("rows", 3 * D_MODEL, BF16), ("rows", D_MODEL, F32), ("rows", LANES, F32)], name="l1_in",
        prologue=lambda h: h, separate=True)
    bf = _pad_lanes(o_b_f)
    log_cum = _logf_fwd(zf, bf, name="l1_logf")
    bias2 = (-LOG2E * log_cum[:, :FOX_HEADS]).T
    t_bwd = min(ATT_T, s)
    bias = bias2.reshape(FOX_HEADS // 2, 2, s // t_bwd, 1, t_bwd)
    t_fwd = _fwd_tile(s)
    o_fox, lse_fox = _flash_fwd(z1, z1, z1, bias2.reshape(FOX_HEADS // 2, 2, s // t_fwd, 1, t_fwd),
                                n_pairs=FOX_HEADS // 2, hw=64, q_off=0, k_off=8, v_off=16, scale=fox_scale,
                                name="l1_fox_fwd")

    dx2, loss_part, d_g_final, og1, dx2_bf = _matmul_rows(
        [(None, wo1, False)], [(o_fox, D_MODEL, 0), (gate1, D_MODEL, 0), (x1, D_MODEL, 0), (target, D_MODEL, 0)],
        [g_final.reshape(1, D_MODEL)],
        lambda r, o, gt, xt, tg, g, made: _and_first(_loss_epilogue(r, xt, tg, g), made),
        [("rows", D_MODEL, F32), ("sum", (8, LANES)), ("sum", (1, D_MODEL)), ("rows", D_MODEL, BF16),
         ("rows", D_MODEL, BF16)], name="l1_out_loss", prologue=lambda o, gt, xt, tg, g: _gated([o], gt))

    d_wo1 = _matmul(og1, dx2_bf, ta=True, out_dtype=BF16, name="l1_out_dw")
    do_fox, d_gate1 = _matmul_rows([(dx2_bf, wo1, True)], [(o_fox, D_MODEL, 0), (gate1, D_MODEL, 0)], [],
                                   _gate_bwd_epilogue([D_MODEL]), [("rows", D_MODEL, F32), ("rows", D_MODEL, BF16)],
                                   name="l1_out_dx")
    dqkv1, dbias, drow = _flash_bwd(z1, z1, z1, do_fox, o_fox, lse_fox, bias, n_pairs=FOX_HEADS // 2, hw=64, q_off=0,
                                    k_off=8, v_off=16, scale=fox_scale, qk_dtype=BF16, stacked=True, name="l1_fox_bwd")
    d_log_cum = (drow.reshape(FOX_HEADS, s) - dbias.reshape(FOX_HEADS, s)).T
    d_log_cum = jnp.pad(d_log_cum, ((0, 0), (0, LANES - FOX_HEADS)))
    d_zf, d_bf = _logf_bwd(d_log_cum, zf, bf, name="l1_logf_bwd")
    d_w1t = (_matmul(dqkv1, h1, ta=True, out_dtype=BF16, name="l1_in_dw_qkv"),
             _matmul(d_gate1, h1, ta=True, out_dtype=BF16, name="l1_in_dw_gate"))
    d_wft = _matmul(d_zf, h1, ta=True, out_dtype=BF16, name="l1_in_f_dw")
    dx1, d_o_g_in, dx1_bf = _matmul_rows([(dqkv1, w1t, False, c * D_MODEL, c) for c in range(3)]
                                         + [(d_gate1, w1t, False, 3 * D_MODEL), (d_zf, wft, False)],
                                         [(x1, D_MODEL, 0), (dx2, D_MODEL, 0)], [o_g_in],
                                         lambda *a: _and_first(_rms_bwd_epilogue(*a)),
                                         [("rows", D_MODEL, F32), ("sum", (1, D_MODEL)), ("rows", D_MODEL, BF16)],
                                         name="l1_in_dx")

    d_wo0 = _matmul(og0, dx1_bf, ta=True, out_dtype=BF16, name="l0_out_dw")
    do_mla, do_swa, d_gate0 = _matmul_rows(
        [(dx1_bf, wo0, True)], [(o_mla, half, 0), (o_swa, half, 0), (z0a, D_MODEL, 0)], [], _gate_bwd_epilogue([half, half]),
        [("rows", half, F32), ("rows", half, F32), ("rows", D_MODEL, BF16)], name="l0_out_dx")
    dq_s, dkt_s, dvt_s, d_sinks = _swa_bwd(z0b, e_sinks, do_swa, o_swa, lse_swa, name="l0_swa_bwd")
    dk_s = dkt_s.transpose(0, 2, 1).reshape(s, LANES)
    dv_s = dvt_s.transpose(0, 2, 1).reshape(s, LANES)
    rider = None
    if scatter1 is not None:
        rider = ("exchange", scatter1(dict(w1t=d_w1t, wft=d_wft, wo1=d_wo1, o_g_in=d_o_g_in, wo0=d_wo0)))
    res = _flash_bwd(qm, kvm, kvm, do_mla, o_mla, lse_mla, None, n_pairs=MLA_HEADS // 2, hw=LANES, q_off=0, k_off=0,
                     v_off=MLA_HEADS, scale=mla_scale, qk_dtype=F32, name="l0_mla_bwd", rider=rider)
    dqm, dkm, dvm = res[0], res[1], res[2]
    recv1 = res[3] if rider is not None else None
    d_qp, d_kvp, d_kpe = _rope_bwd(dqm, dkm, dvm, cos_t, sin_t, name="l0_rope_bwd")
    d_wq = _matmul(cqn, d_qp, ta=True, out_dtype=BF16, name="l0_q_up_dw")
    d_cqn = _matmul(d_qp, wq, tb=True, name="l0_q_up_dx")
    d_wkv = _matmul(ckvn, d_kvp, ta=True, out_dtype=BF16, name="l0_kv_up_dw")
    d_ckvn = _matmul(d_kvp, wkv, tb=True, name="l0_kv_up_dx")
    d_cq, d_g_q_a = _rmsnorm_bwd(z0a, e_g_q_a, d_cqn, width=MLA_Q_RANK, col_blk=4, name="l0_q_norm_bwd")
    d_ckv, d_g_kv_a = _rmsnorm_bwd(z0a, e_g_kv_a, d_ckvn, width=MLA_KV_RANK, col_blk=10, name="l0_kv_norm_bwd")
    dz0 = jnp.concatenate([d_gate0, d_cq, d_ckv, d_kpe, dq_s.astype(BF16), dk_s.astype(BF16), dv_s.astype(BF16)], axis=1)
    d_w0t = _matmul(dz0, h0, ta=True, out_dtype=BF16, name="l0_in_dw")
    pending0, after_start = None, []
    if scatter0 is not None:
        *pending0, token = _peer_start("exchange", scatter0(dict(w0t=d_w0t, wq=d_wq, wkv=d_wkv)), name="grads0_start")
        after_start = [token]
    grad_x, d_e_g_in = _matmul_rows(
        [(dz0, w0t, False)], [(x, D_MODEL, 0), (dx1, D_MODEL, 0)], [e_g_in] + after_start,
        lambda dy, xt, add, g, *_: _rms_bwd_epilogue(dy, xt, add, g),
        [("rows", D_MODEL, F32), ("sum", (1, D_MODEL))], name="l0_in_dx")

    return dict(pending0=pending0, recv1=recv1, loss=loss_part[0, 0], grad_x=grad_x, e_g_in=d_e_g_in, w0t=d_w0t, e_g_q_a=d_g_q_a, wq=d_wq,
                e_g_kv_a=d_g_kv_a, wkv=d_wkv, e_sinks=d_sinks[:, 0].reshape(1, SWA_HEADS), wo0=d_wo0,
                o_g_in=d_o_g_in, w1t=d_w1t, wft=d_wft, o_b_f=d_bf[:, :FOX_HEADS], wo1=d_wo1, g_final=d_g_final.reshape(D_MODEL))


def _wide(a, rows):
    flat = a.reshape(-1)
    return jnp.pad(flat, (0, rows * WIDE - flat.shape[0])).reshape(rows, WIDE)


def _rows_b0(w_q, w_kv):
    return jnp.concatenate([_wide(w_q, 32), _wide(w_kv, 16)], axis=0)


def _unflat_b0(f):
    return f[0:24].reshape(1, MLA_Q_RANK, 96), f[32:48].reshape(1, MLA_KV_RANK, 128)


def _rows_b1(o_w_out, e_w_out, g_in):
    return jnp.concatenate([o_w_out, e_w_out, _wide(g_in, 16)], axis=0)


def _unflat_b1(f):
    return f[0:128][None], f[128:256][None], f[256:257, :LANES]


def kernel(x, positions, e_g_in, e_w_in, e_g_q_a, e_w_q_up, e_g_kv_a, e_w_kv_up, e_sinks, e_w_out, o_g_in, o_w_in, o_b_f, o_w_out, g_final, loss_target, m_e_g_in, m_e_w_in, m_e_g_q_a, m_e_w_q_up, m_e_g_kv_a, m_e_w_kv_up, m_e_sinks, m_e_w_out, m_o_g_in, m_o_w_in, m_o_b_f, m_o_w_out, m_g_final, v_e_g_in, v_e_w_in, v_e_g_q_a, v_e_w_q_up, v_e_g_kv_a, v_e_w_kv_up, v_e_sinks, v_e_w_out, v_o_g_in, v_o_w_in, v_o_b_f, v_o_w_out, v_g_final):
    def bf(a):
        return a.astype(BF16)

    me = 4 * lax.axis_index("x") + 2 * lax.axis_index("y") + lax.axis_index("c")
    shard0 = jnp.concatenate([_pad_rows(bf(e_w_in[0]).T, RA0), _rows_b0(bf(e_w_q_up[0]), bf(e_w_kv_up[0]))], axis=0)
    *pending_w0, token_w0 = _peer_start("across", shard0, name="weights0_start")

    def unpack0(sent, gath0):
        gath0 = lax.dynamic_update_slice_in_dim(gath0, sent[None], me, axis=0)
        w0t = _layer0_in_weight_t(gath0[:, :N_E_IN].reshape(N_DEV * N_E_IN, WIDE))
        wq = _q_up_weight(_gathered_cols(gath0[:, RA0:RA0 + 24], MLA_Q_RANK))
        wkv = _kv_up_weight(_gathered_cols(gath0[:, RA0 + 32:RA0 + 48], MLA_KV_RANK))
        return w0t, wq, wkv

    rows_b0 = [_rows_b0(q[0], kv[0]) for q, kv in ((e_w_q_up, e_w_kv_up), (m_e_w_q_up, m_e_w_kv_up), (v_e_w_q_up, v_e_w_kv_up))]
    rows_b1 = [_rows_b1(o[0], e[0], g) for o, e, g in ((o_w_out, e_w_out, o_g_in), (m_o_w_out, m_e_w_out, m_o_g_in),
                                                       (v_o_w_out, v_e_w_out, v_o_g_in))]

    g_bits = lax.bitcast_convert_type(o_g_in.reshape(LANES), BF16)
    shard1 = jnp.concatenate([_pad_rows(bf(o_w_in[0]).T, RA1), _rows_b1(bf(o_w_out[0]), bf(e_w_out[0]), g_bits)], axis=0)

    def unpack1(gath1):
        w1t, wft, wo1, wo0 = _layer1_in_unpack(gath1, name="weights1_unpack")
        bits = gath1[:, RA1 + 256, :2 * LANES].reshape(N_DEV, LANES, 2)
        return wo0, lax.bitcast_convert_type(bits, F32).reshape(1, D_MODEL), w1t, wft, wo1

    def scatter1(g):
        d_o_g = jnp.pad(bf(g["o_g_in"]).reshape(N_DEV, 1, LANES), ((0, 0), (0, 15), (0, WIDE - LANES)))
        return _late_grads_pack(g["w1t"][0], g["wft"], g["w1t"][1], g["wo1"], g["wo0"], d_o_g, name="grads1_pack")

    def scatter0(g):
        return jnp.concatenate([
            _pad_rows(_layer0_in_grad_t(g["w0t"]).reshape(N_DEV, N_E_IN, WIDE), RA0),
            _pad_rows(_scatter_cols(_q_up_grad(g["wq"])), 32), _scatter_cols(_kv_up_grad(g["wkv"]))], axis=1)

    gr = _local_step(x[0], positions[0], loss_target[0], e_g_in,
                     (pending_w0, token_w0, unpack0, [shard1] + rows_b0 + rows_b1), e_g_q_a, e_g_kv_a, e_sinks,
                     (shard1, unpack1), o_b_f, g_final, scatter1=scatter1, scatter0=scatter0)

    def in_projection(recv, ra, n, w, m, v, name):
        g = _sum8(recv, ra, name=name + "_grad_sum")[:n].reshape(n, 1, D_MODEL)
        w, m, v = [jnp.transpose(a, (2, 0, 1)) for a in (w, m, v)]
        return (g, *_adamw_columns(g, w, m, v, name=name + "_adamw"))

    o_in = in_projection(gr["recv1"], RA1, N_O_IN, o_w_in, m_o_w_in, v_o_w_in, "o_w_in")
    b1 = _adamw(gr["recv1"][:, RA1:], *rows_b1, name="adamw_late")

    small = _small_pack(gr["e_g_in"], gr["g_final"], gr["e_g_q_a"], gr["e_g_kv_a"], gr["e_sinks"], gr["o_b_f"], gr["loss"])
    small_all = _all_gather(small, name="small_all_gather")
    zero = jnp.zeros((), F32)
    w_small = _small_pack(e_g_in, g_final, e_g_q_a, e_g_kv_a, e_sinks, o_b_f, zero)
    m_small = _small_pack(m_e_g_in, m_g_final, m_e_g_q_a, m_e_g_kv_a, m_e_sinks, m_o_b_f, zero)
    v_small = _small_pack(v_e_g_in, v_g_final, v_e_g_q_a, v_e_g_kv_a, v_e_sinks, v_o_b_f, zero)
    smalls = _adamw(small_all, w_small, m_small, v_small, name="adamw_replicated")
    g_sm, d_sm, m_sm, v_sm = [_small_unpack(a) for a in smalls]
    loss = g_sm[6]

    sent0, recv0 = _peer_wait("exchange", *gr["pending0"], after=[o_in[1], b1[1], smalls[1]], name="grads0_wait")
    own = lax.dynamic_slice_in_dim(sent0, me, 1, axis=0)
    recv0 = lax.dynamic_update_slice_in_dim(recv0, own, me, axis=0)
    e_in = in_projection(recv0, RA0, N_E_IN, e_w_in, m_e_w_in, v_e_w_in, "e_w_in")
    b0 = _adamw(recv0[:, RA0:], *rows_b0, name="adamw_early")

    def sharded(k):
        q_up, kv_up = _unflat_b0(b0[k])
        o_out, e_out, o_g = _unflat_b1(b1[k])
        return jnp.transpose(e_in[k], (1, 2, 0)), q_up, kv_up, e_out, jnp.transpose(o_in[k], (1, 2, 0)), o_out, o_g

    g_sh, d_sh, m_sh, v_sh = [sharded(k) for k in range(4)]

    def leaves(sh, sm):
        return (sm[0], sh[0], sm[2], sh[1], sm[3], sh[2], sm[4], sh[3], sh[6], sh[4], sm[5], sh[5], sm[1])

    return (loss, gr["grad_x"][None], *leaves(g_sh, g_sm), *leaves(d_sh, d_sm), *leaves(m_sh, m_sm), *leaves(v_sh, v_sm))
```

```python
import functools

import jax
import jax.numpy as jnp
from jax import lax
from jax.experimental import pallas as pl
from jax.experimental.pallas import tpu as pltpu

F32 = jnp.float32
BF16 = jnp.bfloat16
NEG_INF = float("-inf")

N_DEV = 8
LANES = 128
D_MODEL = 1024
EPS = 1e-6
ROPE_THETA = 10000.0
MLA_HEADS = 8
MLA_Q_RANK = 256
MLA_KV_RANK = 128
MLA_NOPE = 64
MLA_ROPE = 32
MLA_V = 64
SWA_HEADS = 8
SWA_KV_HEADS = 2
SWA_DIM = 64
WINDOW = 128
FOX_HEADS = 16
FOX_DIM = 64

ADAM_LR = 0.001
ADAM_B1 = 0.9
ADAM_B2 = 0.999
ADAM_EPS = 1e-08
ADAM_WD = 0.01
ADAM_STEP = 10

ATT_T = 512
ATT_T_FWD = 1024
VMEM_LIMIT = 56 * 1024 * 1024
MATMUL_B_BLOCK_BYTES = 8 * 1024 * 1024

Z0A_UNITS = 12
Z0B_UNITS = 6

WIDE = 1024
N_E_IN = 276
N_O_IN = 514
RA0 = 288
RB0 = 32 + 16
RA1 = 528
RB1 = 128 + 128 + 16
SMALL_ROWS = 24


def _tile(n, cands):
    for c in cands:
        if n % c == 0:
            return c
    raise ValueError(f"no tile for {n}")


ROW_TILES = (512, 256, 128)


def _params(sem, vmem=VMEM_LIMIT):
    return pltpu.CompilerParams(dimension_semantics=sem, vmem_limit_bytes=vmem)


def _matmul(a, b, *, name, ta=False, tb=False, out_dtype=F32, b_rows=None):
    if ta:
        kdim, m = a.shape[-2], a.shape[-1] * (a.shape[0] if a.ndim == 3 else 1)
    else:
        m, kdim = a.shape
    if tb:
        n, kb = b.shape
    else:
        kb, n = b.shape
    assert kdim == kb, (a.shape, b.shape)
    b_start = 0
    if b_rows is not None:
        assert tb
        b_start, n = b_rows
    tm = _tile(m, (512, 256, 128))
    tn = _tile(n, [c for c in (1024, 768, 512, 384, 256, 128)
                   if c * kdim * b.dtype.itemsize <= MATMUL_B_BLOCK_BYTES and b_start % c == 0])
    assert b_start % tn == 0, (b_start, tn)
    b_off = b_start // tn
    dims = (((0 if ta else 1,), (1 if tb else 0,)), ((), ()))

    def body(a_ref, b_ref, o_ref):
        r = lax.dot_general(a_ref[...].astype(BF16), b_ref[...].astype(BF16), dims, preferred_element_type=F32)
        o_ref[...] = r.astype(out_dtype)

    if a.ndim == 3:
        per = a.shape[2] // tm
        a_spec = pl.BlockSpec((None, kdim, tm), lambda i, j: (i // per, 0, i % per))
    else:
        a_spec = pl.BlockSpec((kdim, tm), lambda i, j: (0, i)) if ta else pl.BlockSpec((tm, kdim), lambda i, j: (i, 0))
    b_spec = pl.BlockSpec((tn, kdim), lambda i, j: (j + b_off, 0)) if tb else pl.BlockSpec((kdim, tn), lambda i, j: (0, j))
    return pl.pallas_call(
        body, name=name, grid=(m // tm, n // tn), in_specs=[a_spec, b_spec],
        out_specs=pl.BlockSpec((tm, tn), lambda i, j: (i, j)), out_shape=jax.ShapeDtypeStruct((m, n), out_dtype),
        compiler_params=_params(("parallel", "parallel")),
    )(a, b)


def _rmsnorm_fwd(x, g, *, width, col_blk, name, after=()):
    s = x.shape[0]
    tm = _tile(s, ROW_TILES)

    def body(x_ref, g_ref, *rest):
        y_ref = rest[-1]
        xf = x_ref[...].astype(F32)
        r = lax.rsqrt(jnp.mean(xf * xf, axis=-1, keepdims=True) + EPS)
        y_ref[...] = ((xf * r) * g_ref[...]).astype(BF16)

    return pl.pallas_call(
        body, name=name, grid=(s // tm,),
        in_specs=[pl.BlockSpec((tm, width), lambda i: (i, col_blk)), pl.BlockSpec((1, width), lambda i: (0, 0))]
        + [ANY] * len(after),
        out_specs=pl.BlockSpec((tm, width), lambda i: (i, 0)),
        out_shape=jax.ShapeDtypeStruct((s, width), BF16),
        compiler_params=_params(("parallel",)),
    )(x, g, *after)


def _rmsnorm_bwd(x, g, dy, *, width, col_blk, name):
    s = x.shape[0]
    tm = _tile(s, ROW_TILES)

    def body(x_ref, g_ref, dy_ref, dx_ref, dg_ref):
        @pl.when(pl.program_id(0) == 0)
        def _():
            dg_ref[...] = jnp.zeros_like(dg_ref)

        dx, dg = _rms_bwd_epilogue(dy_ref[...], x_ref[...], 0.0, g_ref[...])
        dg_ref[...] += dg
        dx_ref[...] = dx.astype(BF16)

    return pl.pallas_call(
        body, name=name, grid=(s // tm,),
        in_specs=[pl.BlockSpec((tm, width), lambda i: (i, col_blk)), pl.BlockSpec((1, width), lambda i: (0, 0)),
                  pl.BlockSpec((tm, width), lambda i: (i, 0))],
        out_specs=[pl.BlockSpec((tm, width), lambda i: (i, 0)), pl.BlockSpec((1, width), lambda i: (0, 0))],
        out_shape=[jax.ShapeDtypeStruct((s, width), BF16), jax.ShapeDtypeStruct((1, width), F32)],
        compiler_params=_params(("arbitrary",)),
    )(x, g, dy)


def _sigmoid(x):
    return 1.0 / (1.0 + jnp.exp(-x))


def _matmul_rows(terms, row_inputs, params, epilogue, outs, *, name, prologue=None, separate=False):
    s = row_inputs[0][0].shape[0] if row_inputs else terms[0][0].shape[-2]
    tm = _tile(s, ROW_TILES)
    steps = s // tm
    n_r, n_p, n_o = len(row_inputs), len(params), len(outs)
    n_t = sum(1 if term[0] is None else 2 for term in terms)

    def body(*refs):
        t_refs, r_refs = list(refs[:n_t]), refs[n_t:n_t + n_r]
        p_refs, o_refs = refs[n_t + n_r:n_t + n_r + n_p], refs[n_t + n_r + n_p:]
        i = pl.program_id(0)
        rows, small = [r[...] for r in r_refs], [p[...] for p in p_refs]
        made = None if prologue is None else prologue(*rows, *small)
        parts = []
        for term in terms:
            a = made if term[0] is None else t_refs.pop(0)[...].astype(BF16)
            dims = (((1,), (1 if term[2] else 0,)), ((), ()))
            parts.append(lax.dot_general(a, t_refs.pop(0)[...].astype(BF16), dims, preferred_element_type=F32))
        acc = parts if separate else sum(parts[1:], parts[0])
        vals = epilogue(acc, *rows, *small) if prologue is None else epilogue(acc, *rows, *small, made)
        for ref, val, out in zip(o_refs, vals, outs):
            if out[0] == "rows":
                ref[...] = val.astype(ref.dtype)
            else:
                @pl.when(i == 0)
                def _(ref=ref):
                    ref[...] = jnp.zeros_like(ref)

                ref[...] += val

    in_specs, args = [], []
    for term in terms:
        a, b = term[0], term[1]
        if a is None:
            in_specs.append(_resident(b.shape, lambda i: (0, 0)))
            args.append(b)
            continue
        b_rows = b.shape[0] if term[2] or len(term) < 4 else a.shape[-1]
        b_blk = 0 if len(term) < 4 else term[3] // b_rows
        if len(term) == 5:
            a_spec = pl.BlockSpec((None, tm, a.shape[2]), lambda i, c=term[4]: (c, i, 0))
        else:
            a_spec = pl.BlockSpec((tm, a.shape[1]), lambda i: (i, 0))
        in_specs += [a_spec, _resident((b_rows, b.shape[1]), lambda i, b_blk=b_blk: (b_blk, 0))]
        args += [a, b]
    for arr, width, col_blk in row_inputs:
        in_specs.append(pl.BlockSpec((tm, width), lambda i, col_blk=col_blk: (i, col_blk)))
        args.append(arr)
    for p in params:
        in_specs.append(pl.BlockSpec(p.shape, lambda i: (0, 0)))
        args.append(p)
    out_specs, out_shape = [], []
    for out in outs:
        if out[0] == "rows":
            out_specs.append(pl.BlockSpec((tm, out[1]), lambda i: (i, 0)))
            out_shape.append(jax.ShapeDtypeStruct((s, out[1]), out[2]))
        else:
            out_specs.append(pl.BlockSpec(out[1], lambda i: (0, 0)))
            out_shape.append(jax.ShapeDtypeStruct(out[1], F32))
    return pl.pallas_call(
        body, name=name, grid=(steps,), in_specs=in_specs, out_specs=out_specs, out_shape=out_shape,
        compiler_params=_params(("arbitrary",)),
    )(*args)


def _rms_stats(x):
    r = lax.rsqrt(jnp.mean(x * x, axis=-1, keepdims=True) + EPS)
    return r, x * r


def _gated(o_parts, gate):
    o = o_parts[0] if len(o_parts) == 1 else jnp.concatenate(o_parts, axis=1)
    return (o * (gate * _sigmoid(gate))).astype(BF16)


def _and_first(vals, *more):
    return (*vals, *more, vals[0])


def _residual_norm_epilogue(r, x, g):
    x1 = x + r
    _, xh = _rms_stats(x1)
    return x1, xh * g


def _rms_bwd_epilogue(dy, x, add, g):
    r, xh = _rms_stats(x)
    dxh = dy * g
    dx = r * (dxh - xh * jnp.mean(dxh * xh, axis=-1, keepdims=True)) + add
    return dx, jnp.sum(dy * xh, axis=0, keepdims=True)


def _loss_epilogue(r, x1, target, g):
    rs, xh = _rms_stats(x1 + r)
    err = xh * g - target
    loss = jnp.broadcast_to(0.5 * jnp.sum(jnp.mean(err * err, axis=-1, keepdims=True)), (8, LANES))
    dy = err * (1.0 / D_MODEL)
    dxh = dy * g
    dx = rs * (dxh - xh * jnp.mean(dxh * xh, axis=-1, keepdims=True))
    return dx, loss, jnp.sum(dy * xh, axis=0, keepdims=True)


def _gate_bwd_epilogue(widths):
    def epilogue(d, *rows):
        o_parts, gt = rows[:-1], rows[-1]
        o = o_parts[0] if len(o_parts) == 1 else jnp.concatenate(o_parts, axis=1)
        sg = _sigmoid(gt)
        do = d * (gt * sg)
        d_gate = d * o * (sg * (1.0 + gt * (1.0 - sg)))
        cuts = [sum(widths[:k]) for k in range(len(widths) + 1)]
        return tuple(do[:, cuts[k]:cuts[k + 1]] for k in range(len(widths))) + (d_gate,)

    return epilogue


def _rot_half(x):
    lane = lax.broadcasted_iota(jnp.int32, x.shape, 1)
    return jnp.where(lane < 80, pltpu.roll(x, LANES - 16, axis=1), pltpu.roll(x, 16, axis=1))


def _rot_half_t(g):
    lane = lax.broadcasted_iota(jnp.int32, g.shape, 1)
    lo = (lane >= MLA_NOPE) & (lane < MLA_NOPE + MLA_ROPE // 2)
    hi = (lane >= MLA_NOPE + MLA_ROPE // 2) & (lane < MLA_NOPE + MLA_ROPE)
    return jnp.where(lo, pltpu.roll(g, LANES - 16, axis=1), jnp.where(hi, pltpu.roll(g, 16, axis=1), 0.0))


def _rope_q_epilogue(q, c, sn):
    heads = [q[:, h * LANES:(h + 1) * LANES] for h in range(MLA_HEADS)]
    return (jnp.concatenate([qh * c + _rot_half(qh) * sn for qh in heads], axis=1),)


def _rope_k_epilogue(kv, kpe, c, sn):
    kpe_r = kpe * c + _rot_half(kpe) * sn
    lane = lax.broadcasted_iota(jnp.int32, kpe.shape, 1)
    heads = [jnp.where(lane < MLA_NOPE, kv[:, h * LANES:(h + 1) * LANES], kpe_r) for h in range(MLA_HEADS)]
    return (jnp.concatenate(heads + [kv[:, MLA_HEADS * LANES:]], axis=1),)


def _rope_bwd(dqm, dkm, dvm, cos_t, sin_t, *, name):
    s = dqm.shape[0]
    tm = _tile(s, ROW_TILES)
    hw = MLA_HEADS * LANES
    vw = MLA_HEADS * MLA_V

    def body(dq_ref, dk_ref, dv_ref, c_ref, s_ref, dqp_ref, dkv_ref, dkpe_ref):
        c = c_ref[...]
        sn = s_ref[...]
        ksum = jnp.zeros((tm, LANES), F32)
        for h in range(MLA_HEADS):
            sl = slice(h * LANES, (h + 1) * LANES)
            dq = dq_ref[:, sl]
            dqp_ref[:, sl] = (dq * c + _rot_half_t(dq * sn)).astype(BF16)
            dk = dk_ref[:, sl]
            dkv_ref[:, sl] = dk.astype(BF16)
            ksum = ksum + dk
        dkv_ref[:, hw:] = dv_ref[...]
        lane = lax.broadcasted_iota(jnp.int32, ksum.shape, 1)
        dkpe = ksum * c + _rot_half_t(ksum * sn)
        dkpe_ref[...] = jnp.where((lane >= MLA_NOPE) & (lane < MLA_NOPE + MLA_ROPE), dkpe, 0.0).astype(BF16)

    return pl.pallas_call(
        body, name=name, grid=(s // tm,),
        in_specs=[pl.BlockSpec((tm, hw), lambda i: (i, 0)), pl.BlockSpec((tm, hw), lambda i: (i, 0)),
                  pl.BlockSpec((tm, vw), lambda i: (i, 0)),
                  pl.BlockSpec((tm, LANES), lambda i: (i, 0)), pl.BlockSpec((tm, LANES), lambda i: (i, 0))],
        out_specs=[pl.BlockSpec((tm, hw), lambda i: (i, 0)), pl.BlockSpec((tm, hw + vw), lambda i: (i, 0)),
                   pl.BlockSpec((tm, LANES), lambda i: (i, 0))],
        out_shape=[jax.ShapeDtypeStruct((s, hw), BF16), jax.ShapeDtypeStruct((s, hw + vw), BF16),
                   jax.ShapeDtypeStruct((s, LANES), BF16)],
        compiler_params=_params(("parallel",)),
    )(dqm, dkm, dvm, cos_t, sin_t)


def _head_mask(shape, a):
    lane = lax.broadcasted_iota(jnp.int32, shape, 1)
    return (lane >= 64 * a) & (lane < 64 * (a + 1))


_NT = (((1,), (1,)), ((), ()))
LOG2E = 1.4426950408889634


def _stack_heads(tile, hw):
    lane = lax.broadcasted_iota(jnp.int32, tile.shape, 1)
    z = jnp.zeros_like(tile)
    return jnp.concatenate([jnp.where(lane < hw, tile, z), jnp.where(lane >= hw, tile, z)], axis=0)


def _stacked_rows(r0, r1, t):
    n = r0.shape[-1]
    return jnp.concatenate([jnp.broadcast_to(r0, (t, n)), jnp.broadcast_to(r1, (t, n))], axis=0)


def _resident(block, index_map):
    return pl.BlockSpec(block, index_map, pipeline_mode=pl.Buffered(1))


def _fwd_tile(s):
    return ATT_T_FWD if s % ATT_T_FWD == 0 else min(ATT_T, s)


def _flash_fwd(q, k, v, bias, *, n_pairs, hw, q_off, k_off, v_off, scale, name, rider=None):
    s = q.shape[0]
    t = _fwd_tile(s)
    nb = s // t
    qw = 2 * hw
    has_bias = bias is not None
    c1 = scale * LOG2E

    def body(*refs):
        refs, ride_refs = _split_rider(refs, rider, n_in=4 if has_bias else 3, n_out=2)
        if has_bias:
            q_ref, k_ref, v_ref, b_ref, o_ref, lse_ref, vt_ref, bcol_ref = refs
        else:
            q_ref, k_ref, v_ref, o_ref, lse_ref, vt_ref = refs
            b_ref = bcol_ref = None
        _ride_start(rider, ride_refs, pl.program_id(0) == 0)
        row = lax.broadcasted_iota(jnp.int32, (t, t), 0)
        col = lax.broadcasted_iota(jnp.int32, (t, t), 1)
        cmask_t = jnp.concatenate([row <= col, row <= col], axis=1)
        lane_lt64 = lax.broadcasted_iota(jnp.int32, (t, LANES), 1) < 64

        def as_column(r):
            return jnp.broadcast_to(r, (8, r.shape[1])).T[:, 0:1]

        def v_block(j, _):
            c0 = pl.multiple_of(j * t, t)
            vt_ref[j] = v_ref[pl.ds(c0, t), :].astype(F32).T.astype(BF16)
            if has_bias:
                for a in range(2):
                    bcol_ref[a, pl.ds(c0, t), :] = as_column(b_ref[0, a, j])
            return 0

        lax.fori_loop(0, nb, v_block, 0)

        def stacked_queries(i):
            return _stack_heads(q_ref[pl.ds(pl.multiple_of(i * t, t), t), :], hw).astype(F32).T.astype(BF16)

        def kv_step(j, carry, qs_t, masked):
            m, l, acc = carry
            rows = pl.ds(pl.multiple_of(j * t, t), t)
            sc = jnp.dot(k_ref[rows, :], qs_t, preferred_element_type=F32) * c1
            if has_bias:
                sc = sc + jnp.concatenate([jnp.broadcast_to(bcol_ref[0, rows, :], (t, t)),
                                           jnp.broadcast_to(bcol_ref[1, rows, :], (t, t))], axis=1)
            if masked:
                sc = jnp.where(cmask_t, sc, NEG_INF)
            m_new = jnp.maximum(m, jnp.max(sc, axis=0, keepdims=True))
            alpha = jnp.exp2(m - m_new)
            p = jnp.exp2(sc - m_new)
            l_new = alpha * l + jnp.sum(p, axis=0, keepdims=True)
            pv = jnp.dot(vt_ref[j], p.astype(BF16), preferred_element_type=F32)
            return m_new, l_new, alpha * acc + pv

        def finish(i, carry):
            m, l, acc = carry
            r0 = pl.multiple_of(i * t, t)
            out = (acc / l).T
            lse2 = as_column(m + jnp.log2(l))
            lse_ref[0, 0, pl.ds(r0, t), :] = lse2[:t]
            lse_ref[0, 1, pl.ds(r0, t), :] = lse2[t:]
            o_ref[pl.ds(r0, t), :] = jnp.where(lane_lt64, out[:t], out[t:])

        init = (jnp.full((1, 2 * t), NEG_INF, F32), jnp.zeros((1, 2 * t), F32), jnp.zeros((LANES, 2 * t), F32))

        def q_block(i, _):
            qs_t = stacked_queries(i)
            carry = lax.fori_loop(0, i, lambda j, c: kv_step(j, c, qs_t, False), init)
            finish(i, kv_step(i, carry, qs_t, True))
            return 0

        lax.fori_loop(0, nb, q_block, 0)
        _ride_wait(rider, ride_refs, pl.program_id(0) == n_pairs - 1)

    in_specs = [_resident((s, qw), lambda p: (0, q_off + p)), _resident((s, qw), lambda p: (0, k_off + p)),
                _resident((s, LANES), lambda p: (0, v_off + p))]
    args = [q, k, v]
    if has_bias:
        in_specs.append(_resident((1, 2, nb, 1, t), lambda p: (p, 0, 0, 0, 0)))
        args.append(bias)
    out_specs = [pl.BlockSpec((s, LANES), lambda p: (0, p)), pl.BlockSpec((1, 2, s, 1), lambda p: (p, 0, 0, 0))]
    out_shape = [jax.ShapeDtypeStruct((s, n_pairs * LANES), F32), jax.ShapeDtypeStruct((n_pairs, 2, s, 1), F32)]
    scratch = [pltpu.VMEM((nb, LANES, t), BF16)] + ([pltpu.VMEM((2, s, 1), F32)] if has_bias else [])
    scratch += _add_rider(rider, in_specs, args, out_specs, out_shape)
    return pl.pallas_call(
        body, name=name, grid=(n_pairs,), in_specs=in_specs, out_specs=out_specs, out_shape=out_shape,
        scratch_shapes=scratch,
        compiler_params=_params(("parallel",) if rider is None else ("arbitrary",)),
    )(*args)


def _flash_bwd(q, k, v, do, o, lse, bias, *, n_pairs, hw, q_off, k_off, v_off, scale, qk_dtype, name, rider=None,
               stacked=False):
    s = q.shape[0]
    t = min(ATT_T, s)
    nb = s // t
    qw = 2 * hw
    has_bias = bias is not None
    c1 = scale * LOG2E

    def body(*refs):
        n_grads = 1 if stacked else 3
        refs, ride_refs = _split_rider(refs, rider, n_in=7 if has_bias else 6, n_out=n_grads + (2 if has_bias else 0))
        if stacked:
            refs = list(refs)
            n_in = 7 if has_bias else 6
            refs[n_in:n_in + 1] = [refs[n_in].at[0], refs[n_in].at[1], refs[n_in].at[2]]
        if has_bias:
            (q_ref, k_ref, v_ref, do_ref, o_ref, lse_ref, b_ref, dq_ref, dk_ref, dv_ref, db_ref, dr_ref,
             dkt_ref, dvt_ref) = refs
            db_ref[...] = jnp.zeros_like(db_ref)
        else:
            q_ref, k_ref, v_ref, do_ref, o_ref, lse_ref, dq_ref, dk_ref, dv_ref, dkt_ref, dvt_ref = refs
            b_ref = db_ref = dr_ref = None
        _ride_start(rider, ride_refs, pl.program_id(0) == 0)
        dkt_ref[...] = jnp.zeros_like(dkt_ref)
        dvt_ref[...] = jnp.zeros_like(dvt_ref)
        causal = lax.broadcasted_iota(jnp.int32, (t, t), 1) <= lax.broadcasted_iota(jnp.int32, (t, t), 0)
        cmask = jnp.concatenate([causal, causal], axis=0)
        lane_lt_hw = lax.broadcasted_iota(jnp.int32, (t, qw), 1) < hw

        def q_block(i, _):
            r0 = pl.multiple_of(i * t, t)
            qs = _stack_heads(q_ref[pl.ds(r0, t), :], hw)
            dos = _stack_heads(do_ref[pl.ds(r0, t), :], 64)
            ot = o_ref[pl.ds(r0, t), :]
            delta = jnp.sum(dos * jnp.concatenate([ot, ot], axis=0), axis=-1, keepdims=True)
            lse2 = jnp.concatenate([lse_ref[0, 0, pl.ds(r0, t), :], lse_ref[0, 1, pl.ds(r0, t), :]], axis=0)
            dosb = dos.astype(BF16)
            dos_t = dos.T.astype(BF16)
            qs_t = qs.astype(F32).T.astype(BF16)

            def kv_step(j, carry, masked):
                dq, rsum = carry
                c0 = pl.multiple_of(j * t, t)
                kt = k_ref[pl.ds(c0, t), :]
                vt = v_ref[pl.ds(c0, t), :]
                sc = lax.dot_general(qs, kt, _NT, preferred_element_type=F32) * c1
                if has_bias:
                    sc = sc + _stacked_rows(b_ref[0, 0, j], b_ref[0, 1, j], t)
                if masked:
                    sc = jnp.where(cmask, sc, NEG_INF)
                p = jnp.exp2(sc - lse2)
                dp = lax.dot_general(dosb, vt, _NT, preferred_element_type=F32)
                ds = p * (dp - delta)
                dsb = ds.astype(BF16)
                pb = p.astype(BF16)
                if hw == LANES:
                    dvt_ref[j] += jnp.concatenate(
                        [jnp.dot(dos_t[:64, :t], pb[:t], preferred_element_type=F32),
                         jnp.dot(dos_t[64:, t:], pb[t:], preferred_element_type=F32)], axis=0)
                    dkt_ref[j] += jnp.concatenate(
                        [jnp.dot(qs_t[:hw, :t], dsb[:t], preferred_element_type=F32),
                         jnp.dot(qs_t[hw:, t:], dsb[t:], preferred_element_type=F32)], axis=0)
                else:
                    dvt_ref[j] += jnp.dot(dos_t, pb, preferred_element_type=F32)
                    dkt_ref[j] += jnp.dot(qs_t, dsb, preferred_element_type=F32)
                if has_bias:
                    db_ref[0, 0, j] += jnp.sum(ds[:t], axis=0, keepdims=True)
                    db_ref[0, 1, j] += jnp.sum(ds[t:], axis=0, keepdims=True)
                    rsum = rsum + jnp.sum(ds, axis=-1, keepdims=True)
                return dq + jnp.dot(dsb, kt, preferred_element_type=F32), rsum

            init = (jnp.zeros((2 * t, qw), F32), jnp.zeros((2 * t, 1), F32))
            carry = lax.fori_loop(0, i, functools.partial(kv_step, masked=False), init)
            dq, rsum = kv_step(i, carry, True)
            dq = dq * scale
            dq_ref[pl.ds(r0, t), :] = jnp.where(lane_lt_hw, dq[:t], dq[t:]).astype(qk_dtype)
            if has_bias:
                rsum_row = jnp.broadcast_to(rsum, (2 * t, LANES)).T[0:1]
                dr_ref[0, 0, i] = rsum_row[:, :t]
                dr_ref[0, 1, i] = rsum_row[:, t:]
            return 0

        lax.fori_loop(0, nb, q_block, 0)

        def k_block(j, _):
            c0 = pl.multiple_of(j * t, t)
            dk_ref[pl.ds(c0, t), :] = (dkt_ref[j].T * scale).astype(qk_dtype)
            dv_ref[pl.ds(c0, t), :] = dvt_ref[j].T.astype(BF16)
            return 0

        lax.fori_loop(0, nb, k_block, 0)
        _ride_wait(rider, ride_refs, pl.program_id(0) == n_pairs - 1)

    in_specs = [_resident((s, qw), lambda p: (0, q_off + p)), _resident((s, qw), lambda p: (0, k_off + p)),
                _resident((s, LANES), lambda p: (0, v_off + p)),
                _resident((s, LANES), lambda p: (0, p)), _resident((s, LANES), lambda p: (0, p)),
                _resident((1, 2, s, 1), lambda p: (p, 0, 0, 0))]
    args = [q, k, v, do, o, lse]
    if stacked:
        assert qw == LANES and qk_dtype == BF16
        out_specs = [pl.BlockSpec((3, s, LANES), lambda p: (0, 0, p))]
        out_shape = [jax.ShapeDtypeStruct((3, s, n_pairs * LANES), BF16)]
    else:
        out_specs = [pl.BlockSpec((s, qw), lambda p: (0, p)), pl.BlockSpec((s, qw), lambda p: (0, p)),
                     pl.BlockSpec((s, LANES), lambda p: (0, p))]
        out_shape = [jax.ShapeDtypeStruct((s, n_pairs * qw), qk_dtype), jax.ShapeDtypeStruct((s, n_pairs * qw), qk_dtype),
                     jax.ShapeDtypeStruct((s, n_pairs * LANES), BF16)]
    if has_bias:
        in_specs.append(_resident((1, 2, nb, 1, t), lambda p: (p, 0, 0, 0, 0)))
        args.append(bias)
        for _ in range(2):
            out_specs.append(pl.BlockSpec((1, 2, nb, 1, t), lambda p: (p, 0, 0, 0, 0)))
            out_shape.append(jax.ShapeDtypeStruct((n_pairs, 2, nb, 1, t), F32))
    scratch = [pltpu.VMEM((nb, qw, t), F32), pltpu.VMEM((nb, LANES, t), F32)]
    scratch += _add_rider(rider, in_specs, args, out_specs, out_shape)
    return pl.pallas_call(
        body, name=name, grid=(n_pairs,), in_specs=in_specs, out_specs=out_specs, out_shape=out_shape,
        scratch_shapes=scratch,
        compiler_params=_params(("parallel",) if rider is None else ("arbitrary",)),
    )(*args)


def _alibi_slope(h):
    return 2.0 ** (-8.0 * (h + 1.0) / SWA_HEADS)


SWA_ROWS = 512
SWA_SCALE = SWA_DIM ** -0.5


def _swa_geometry(i):
    w = WINDOW
    r0 = pl.multiple_of(i * w, w)
    b0 = pl.multiple_of(jnp.maximum(i - 1, 0) * w, w)
    row = lax.broadcasted_iota(jnp.int32, (w, 2 * w), 0)
    col = lax.broadcasted_iota(jnp.int32, (w, 2 * w), 1)
    dist = row - col + (r0 - b0)
    valid = (dist >= 0) & (dist < w)
    return r0, b0, dist.astype(F32), valid


def _swa_q_head(qblk, h):
    kv = h // (SWA_HEADS // SWA_KV_HEADS)
    if h % 2 != kv:
        qblk = pltpu.roll(qblk, 64, axis=1)
    return jnp.where(_head_mask(qblk.shape, kv), qblk, 0.0)


SWA_GROUP = SWA_HEADS // SWA_KV_HEADS


def _swa_stack(ref, rs, grp):
    parts = []
    for a in range(SWA_GROUP):
        h = SWA_GROUP * grp + a
        parts.append(_swa_q_head(ref[rs, (h // 2) * LANES:(h // 2 + 1) * LANES].astype(F32), h))
    return jnp.concatenate(parts, axis=0)


def _swa_unstack(x, grp):
    tiles = []
    for a in range(SWA_GROUP):
        h = SWA_GROUP * grp + a
        tile = x[a * WINDOW:(a + 1) * WINDOW]
        tiles.append(pltpu.roll(tile, 64, axis=1) if h % 2 != grp else tile)
    return tiles


def _swa_head_column(vals):
    return jnp.concatenate([jnp.full((WINDOW, 1), v, F32) for v in vals], axis=0)


def _swa_logits(qs, kb, dist, valid, grp):
    slopes = _swa_head_column([_alibi_slope(SWA_GROUP * grp + a) for a in range(SWA_GROUP)])
    dist4 = jnp.concatenate([dist] * SWA_GROUP, axis=0)
    valid4 = jnp.concatenate([valid] * SWA_GROUP, axis=0)
    sc = lax.dot_general(qs, kb, _NT, preferred_element_type=F32) * SWA_SCALE - slopes * dist4
    return jnp.where(valid4, sc, NEG_INF)


def _swa_merge_heads(tiles):
    lt64 = lax.broadcasted_iota(jnp.int32, (WINDOW, LANES), 1) < 64
    return jnp.concatenate([jnp.where(lt64, tiles[2 * b], tiles[2 * b + 1]) for b in range(SWA_HEADS // 2)], axis=1)


def _swa_fwd(z0b, sinks, *, name):
    s = z0b.shape[0]
    w = WINDOW
    rows = min(SWA_ROWS, s)
    per_step = rows // w
    qcols = SWA_HEADS * SWA_DIM

    def body(sink_ref, q_ref, k_ref, v_ref, o_ref, lse_ref):
        g = pl.program_id(0)
        for ii in range(per_step):
            rs = slice(ii * w, (ii + 1) * w)
            r0, b0, dist, valid = _swa_geometry(g * per_step + ii)
            kb = k_ref[pl.ds(b0, 2 * w), :]
            vb = v_ref[pl.ds(b0, 2 * w), :]
            o_tiles = []
            for h in range(SWA_HEADS):
                kv = h // SWA_GROUP
                qh = _swa_q_head(q_ref[rs, (h // 2) * LANES:(h // 2 + 1) * LANES].astype(F32), h).astype(BF16)
                sc = lax.dot_general(qh, kb, _NT, preferred_element_type=F32) * SWA_SCALE - _alibi_slope(h) * dist
                sc = jnp.where(valid, sc, NEG_INF)
                sink = sink_ref[0, h]
                m = jnp.maximum(jnp.max(sc, axis=-1, keepdims=True), sink)
                p = jnp.exp(sc - m)
                l = jnp.sum(p, axis=-1, keepdims=True) + jnp.exp(sink - m)
                oh = jnp.dot(p.astype(BF16), vb, preferred_element_type=F32) / l
                o_tiles.append(pltpu.roll(oh, 64, axis=1) if h % 2 != kv else oh)
                lse_ref[h, rs, :] = m + jnp.log(l)
            o_ref[rs, :] = _swa_merge_heads(o_tiles)

    return pl.pallas_call(
        body, name=name, grid=(s // rows,),
        in_specs=[pl.BlockSpec(memory_space=pltpu.SMEM),
                  pl.BlockSpec((rows, qcols), lambda g: (g, 0)),
                  pl.BlockSpec((s, LANES), lambda g: (0, 4)), pl.BlockSpec((s, LANES), lambda g: (0, 5))],
        out_specs=[pl.BlockSpec((rows, qcols), lambda g: (g, 0)), pl.BlockSpec((SWA_HEADS, rows, 1), lambda g: (0, g, 0))],
        out_shape=[jax.ShapeDtypeStruct((s, qcols), F32), jax.ShapeDtypeStruct((SWA_HEADS, s, 1), F32)],
        compiler_params=_params(("parallel",)),
    )(sinks, z0b, z0b, z0b)


def _swa_bwd(z0b, sinks, do, o, lse, *, name):
    s = z0b.shape[0]
    w = WINDOW
    rows = min(SWA_ROWS, s)
    per_step = rows // w
    qcols = SWA_HEADS * SWA_DIM
    nblk = s // w

    def body(sink_ref, q_ref, k_ref, v_ref, do_ref, o_ref, lse_ref, dq_ref, dkt_ref, dvt_ref, dsink_ref):
        g = pl.program_id(0)

        @pl.when(g == 0)
        def _():
            dkt_ref[...] = jnp.zeros_like(dkt_ref)
            dvt_ref[...] = jnp.zeros_like(dvt_ref)
            dsink_ref[...] = jnp.zeros_like(dsink_ref)

        for ii in range(per_step):
            i = g * per_step + ii
            rs = slice(ii * w, (ii + 1) * w)
            r0, b0, dist, valid = _swa_geometry(i)
            j0 = jnp.maximum(i - 1, 0)
            kb = k_ref[pl.ds(b0, 2 * w), :]
            vb = v_ref[pl.ds(b0, 2 * w), :]
            dq_tiles = []
            for grp in range(SWA_KV_HEADS):
                heads = [SWA_GROUP * grp + a for a in range(SWA_GROUP)]
                qs32 = _swa_stack(q_ref, rs, grp)
                dos32 = _swa_stack(do_ref, rs, grp)
                delta = jnp.sum(dos32 * _swa_stack(o_ref, rs, grp), axis=-1, keepdims=True)
                lse = jnp.concatenate([lse_ref[h, rs, :] for h in heads], axis=0)
                sink = _swa_head_column([sink_ref[0, h] for h in heads])
                p = jnp.exp(_swa_logits(qs32.astype(BF16), kb, dist, valid, grp) - lse)
                dp = lax.dot_general(dos32.astype(BF16), vb, _NT, preferred_element_type=F32)
                ds = p * (dp - delta)
                dsb = ds.astype(BF16)
                d_sink = jnp.exp(sink - lse) * delta
                for a, h in enumerate(heads):
                    dsink_ref[h:h + 1, :] += jnp.broadcast_to(-jnp.sum(d_sink[a * w:(a + 1) * w]), (1, LANES))
                dvt = jnp.dot(dos32.T.astype(BF16), p.astype(BF16), preferred_element_type=F32)
                dkt = jnp.dot(qs32.T.astype(BF16), dsb, preferred_element_type=F32) * SWA_SCALE
                dvt_ref[j0] += dvt[:, :w]
                dvt_ref[j0 + 1] += dvt[:, w:]
                dkt_ref[j0] += dkt[:, :w]
                dkt_ref[j0 + 1] += dkt[:, w:]
                dq_tiles += _swa_unstack(jnp.dot(dsb, kb, preferred_element_type=F32) * SWA_SCALE, grp)
            dq_ref[rs, :] = _swa_merge_heads(dq_tiles)

    return pl.pallas_call(
        body, name=name, grid=(s // rows,),
        in_specs=[pl.BlockSpec(memory_space=pltpu.SMEM),
                  pl.BlockSpec((rows, qcols), lambda g: (g, 0)),
                  pl.BlockSpec((s, LANES), lambda g: (0, 4)), pl.BlockSpec((s, LANES), lambda g: (0, 5)),
                  pl.BlockSpec((rows, qcols), lambda g: (g, 0)), pl.BlockSpec((rows, qcols), lambda g: (g, 0)),
                  pl.BlockSpec((SWA_HEADS, rows, 1), lambda g: (0, g, 0))],
        out_specs=[pl.BlockSpec((rows, qcols), lambda g: (g, 0)),
                   pl.BlockSpec((nblk, LANES, w), lambda g: (0, 0, 0)),
                   pl.BlockSpec((nblk, LANES, w), lambda g: (0, 0, 0)),
                   pl.BlockSpec((SWA_HEADS, LANES), lambda g: (0, 0))],
        out_shape=[jax.ShapeDtypeStruct((s, qcols), F32),
                   jax.ShapeDtypeStruct((nblk, LANES, w), F32), jax.ShapeDtypeStruct((nblk, LANES, w), F32),
                   jax.ShapeDtypeStruct((SWA_HEADS, LANES), F32)],
        compiler_params=_params(("arbitrary",)),
    )(sinks, z0b, z0b, z0b, do, o, lse)


CUM_T = 256


def _split3(x):
    hi = x.astype(BF16)
    r1 = x - hi.astype(F32)
    mid = r1.astype(BF16)
    lo = (r1 - mid.astype(F32)).astype(BF16)
    return hi, mid, lo


def _tri_dot(tri, x):
    hi, mid, lo = _split3(x)
    out = jnp.dot(tri, hi, preferred_element_type=F32)
    out = out + jnp.dot(tri, mid, preferred_element_type=F32)
    return out + jnp.dot(tri, lo, preferred_element_type=F32)


def _logf_fwd(zf, bf, *, name):
    s = zf.shape[0]
    t = CUM_T
    nb = s // t

    def body(z_ref, b_ref, c_ref, carry_ref):
        i = pl.program_id(0)

        @pl.when(i == 0)
        def _():
            carry_ref[...] = jnp.zeros_like(carry_ref)

        x = z_ref[...] + b_ref[...]
        lf = jnp.minimum(x, 0.0) - jnp.log(1.0 + jnp.exp(-jnp.abs(x)))
        row = lax.broadcasted_iota(jnp.int32, (t, t), 0)
        col = lax.broadcasted_iota(jnp.int32, (t, t), 1)
        tri = jnp.where(col <= row, 1.0, 0.0).astype(BF16)
        c = _tri_dot(tri, lf) + carry_ref[...]
        c_ref[...] = c
        carry_ref[...] = c[t - 1:t, :]

    return pl.pallas_call(
        body, name=name, grid=(nb,),
        in_specs=[pl.BlockSpec((t, LANES), lambda i: (i, 0)), pl.BlockSpec((1, LANES), lambda i: (0, 0))],
        out_specs=pl.BlockSpec((t, LANES), lambda i: (i, 0)),
        out_shape=jax.ShapeDtypeStruct((s, LANES), F32),
        scratch_shapes=[pltpu.VMEM((1, LANES), F32)],
        compiler_params=_params(("arbitrary",)),
    )(zf, bf)


def _logf_bwd(dc, zf, bf, *, name):
    s = zf.shape[0]
    t = CUM_T
    nb = s // t

    def body(dc_ref, z_ref, b_ref, dz_ref, db_ref, carry_ref):
        i = pl.program_id(0)

        @pl.when(i == 0)
        def _():
            carry_ref[...] = jnp.zeros_like(carry_ref)
            db_ref[...] = jnp.zeros_like(db_ref)

        row = lax.broadcasted_iota(jnp.int32, (t, t), 0)
        col = lax.broadcasted_iota(jnp.int32, (t, t), 1)
        tri = jnp.where(col >= row, 1.0, 0.0).astype(BF16)
        dlf = _tri_dot(tri, dc_ref[...]) + carry_ref[...]
        carry_ref[...] = dlf[0:1, :]
        x = z_ref[...] + b_ref[...]
        dz = dlf * _sigmoid(-x)
        dz_ref[...] = dz.astype(BF16)
        db_ref[...] += jnp.sum(dz, axis=0, keepdims=True)

    return pl.pallas_call(
        body, name=name, grid=(nb,),
        in_specs=[pl.BlockSpec((t, LANES), lambda i: (nb - 1 - i, 0)), pl.BlockSpec((t, LANES), lambda i: (nb - 1 - i, 0)),
                  pl.BlockSpec((1, LANES), lambda i: (0, 0))],
        out_specs=[pl.BlockSpec((t, LANES), lambda i: (nb - 1 - i, 0)), pl.BlockSpec((1, LANES), lambda i: (0, 0))],
        out_shape=[jax.ShapeDtypeStruct((s, LANES), BF16), jax.ShapeDtypeStruct((1, LANES), F32)],
        scratch_shapes=[pltpu.VMEM((1, LANES), F32)],
        compiler_params=_params(("arbitrary",)),
    )(dc, zf, bf)


def _sum_pieces(p_ref):
    g = p_ref[0].astype(F32)
    for k in range(1, N_DEV):
        g = g + p_ref[k].astype(F32)
    return g


def _adam_update(g, w, m, v):
    bc1 = 1.0 - ADAM_B1 ** ADAM_STEP
    bc2 = 1.0 - ADAM_B2 ** ADAM_STEP
    nm = ADAM_B1 * m + (1.0 - ADAM_B1) * g
    nv = ADAM_B2 * v + (1.0 - ADAM_B2) * (g * g)
    m_hat = nm / bc1
    v_hat = nv / bc2
    return -ADAM_LR * (m_hat / (jnp.sqrt(v_hat) + ADAM_EPS) + ADAM_WD * w), nm, nv


def _adamw(pieces, w, m, v, *, name):
    rows, cols = w.shape
    tr = _tile(rows, (RB1, RB0, SMALL_ROWS))

    def body(p_ref, w_ref, m_ref, v_ref, g_ref, d_ref, nm_ref, nv_ref):
        g = _sum_pieces(p_ref)
        g_ref[...] = g
        d_ref[...], nm_ref[...], nv_ref[...] = _adam_update(g, w_ref[...], m_ref[...], v_ref[...])

    spec = pl.BlockSpec((tr, cols), lambda i: (i, 0))
    shape = jax.ShapeDtypeStruct((rows, cols), F32)
    return pl.pallas_call(
        body, name=name, grid=(rows // tr,),
        in_specs=[pl.BlockSpec((N_DEV, tr, cols), lambda i: (0, i, 0)), spec, spec, spec],
        out_specs=[spec, spec, spec, spec], out_shape=[shape, shape, shape, shape],
        compiler_params=_params(("parallel",)),
    )(pieces, w, m, v)


def _sum8(pieces, rows, *, name):
    cols = pieces.shape[2]
    tr = _tile(rows, (176, 96))

    def body(p_ref, g_ref):
        g_ref[...] = _sum_pieces(p_ref)

    return pl.pallas_call(
        body, name=name, grid=(rows // tr,),
        in_specs=[pl.BlockSpec((N_DEV, tr, cols), lambda i: (0, i, 0))],
        out_specs=pl.BlockSpec((tr, cols), lambda i: (i, 0)),
        out_shape=jax.ShapeDtypeStruct((rows, cols), F32),
        compiler_params=_params(("parallel",)),
    )(pieces)


def _adamw_columns(g, w, m, v, *, name):
    n, _, k = w.shape
    tr = n // 2

    def body(g_ref, w_ref, m_ref, v_ref, d_ref, nm_ref, nv_ref):
        d_ref[...], nm_ref[...], nv_ref[...] = _adam_update(g_ref[...], w_ref[...], m_ref[...], v_ref[...])

    spec = pl.BlockSpec((tr, 1, k), lambda i: (i, 0, 0))
    shape = jax.ShapeDtypeStruct((n, 1, k), F32)
    return pl.pallas_call(
        body, name=name, grid=(n // tr,), in_specs=[spec, spec, spec, spec],
        out_specs=[spec, spec, spec], out_shape=[shape, shape, shape],
        compiler_params=_params(("parallel",)),
    )(g, w, m, v)


MESH = pl.DeviceIdType.MESH
ANY = pl.BlockSpec(memory_space=pl.ANY)


def _all_gather(shard, *, name):
    rows, lanes = shard.shape

    def body(x_ref, out_ref, send_sems, recv_sems, local_sem):
        x, y, c = lax.axis_index("x"), lax.axis_index("y"), lax.axis_index("c")
        me, sibling = (x, y, c), (x, y, 1 - c)
        chips = [(1 - x, y), (x, 1 - y), (1 - x, 1 - y)]

        def block(px, py, pc):
            return out_ref.at[4 * px + 2 * py + pc]

        def copy(k, blk, to, src=None):
            return pltpu.make_async_remote_copy(
                src_ref=block(*blk) if src is None else src, dst_ref=block(*blk),
                send_sem=send_sems.at[k], recv_sem=recv_sems.at[k], device_id=to, device_id_type=MESH)

        mine = pltpu.make_async_copy(x_ref, block(*me), local_sem)
        mine.start()
        first = [copy(0, me, sibling, src=x_ref)]
        first += [copy(1 + j, me, (*chip, c), src=x_ref) for j, chip in enumerate(chips)]
        for cp in first:
            cp.start()
        passed = [copy(4 + j, (*chip, c), sibling) for j, chip in enumerate(chips)]
        for j, chip in enumerate(chips):
            copy(1 + j, (*chip, c), me).wait_recv()
            passed[j].start()
        copy(0, sibling, me).wait_recv()
        for j, chip in enumerate(chips):
            copy(4 + j, (*chip, 1 - c), me).wait_recv()
        for cp in first + passed:
            cp.wait_send()
        mine.wait()

    return pl.pallas_call(
        body, name=name, out_shape=jax.ShapeDtypeStruct((N_DEV, rows, lanes), shard.dtype),
        in_specs=[ANY], out_specs=ANY,
        scratch_shapes=[pltpu.SemaphoreType.DMA((7,)), pltpu.SemaphoreType.DMA((7,)), pltpu.SemaphoreType.DMA(())],
    )(shard)


def _peer_copies(kind, src_ref, out_ref, send_sems, recv_sems, local_sem):
    x, y, c = lax.axis_index("x"), lax.axis_index("y"), lax.axis_index("c")
    me = 4 * x + 2 * y + c

    def src(idx):
        return src_ref.at[idx] if kind == "exchange" else src_ref

    mine = None if local_sem is None else pltpu.make_async_copy(src(me), out_ref.at[me], local_sem)
    copies = []
    for r in (2, 4, 6) if kind == "across" else range(1, N_DEV):
        px = 1 - x if r & 4 else x
        py = 1 - y if r & 2 else y
        pc = 1 - c if r & 1 else c
        copies.append(pltpu.make_async_remote_copy(
            src_ref=src(4 * px + 2 * py + pc), dst_ref=out_ref.at[me],
            send_sem=send_sems.at[r - 1], recv_sem=recv_sems.at[r - 1],
            device_id=(px, py, pc), device_id_type=MESH))
    return mine, copies


def _to_other_core(shard, land, *, name):
    def body(src_ref, land_ref, out_ref, send_sems, recv_sems):
        x, y, c = lax.axis_index("x"), lax.axis_index("y"), lax.axis_index("c")
        copies = []
        for k, r in enumerate((0, 2, 4, 6)):
            slot = 4 * (1 - x if r & 4 else x) + 2 * (1 - y if r & 2 else y) + c
            copies.append(pltpu.make_async_remote_copy(
                src_ref=src_ref if r == 0 else land_ref.at[slot], dst_ref=out_ref.at[slot],
                send_sem=send_sems.at[k], recv_sem=recv_sems.at[k], device_id=(x, y, 1 - c), device_id_type=MESH))
        for cp in copies:
            cp.start()
        for cp in copies:
            cp.wait()

    return pl.pallas_call(
        body, name=name, out_shape=jax.ShapeDtypeStruct(land.shape, land.dtype), in_specs=[ANY, ANY], out_specs=ANY,
        input_output_aliases={1: 0}, scratch_shapes=[pltpu.SemaphoreType.DMA((4,)), pltpu.SemaphoreType.DMA((4,))],
    )(shard, land)


PEER_SEMS = [pltpu.SemaphoreType.DMA((7,)), pltpu.SemaphoreType.DMA((7,)), pltpu.SemaphoreType.DMA(())]


HBM = pl.BlockSpec(memory_space=pltpu.HBM)
SEMAPHORES = pl.BlockSpec(memory_space=pltpu.SEMAPHORE)


def _peer_start(kind, arr, *, name):
    land = lax.empty((N_DEV,) + arr.shape[-2:], arr.dtype)

    def body(src_ref, land_ref, send_sems, recv_sems, src_thru, land_thru, token):
        _, copies = _peer_copies(kind, src_ref, land_ref, send_sems, recv_sems, None)
        for cp in copies:
            cp.start()
        token[...] = jnp.zeros_like(token)

    return pl.pallas_call(
        body, name=name,
        out_shape=(pltpu.SemaphoreType.DMA((N_DEV - 1,)), pltpu.SemaphoreType.DMA((N_DEV - 1,)),
                   pltpu.HBM(arr.shape, arr.dtype), pltpu.HBM(land.shape, land.dtype), jax.ShapeDtypeStruct((8, LANES), F32)),
        in_specs=(HBM, HBM), out_specs=(SEMAPHORES, SEMAPHORES, HBM, HBM, pl.BlockSpec(memory_space=pltpu.VMEM)),
        input_output_aliases={0: 2, 1: 3},
        compiler_params=pltpu.CompilerParams(has_side_effects=pltpu.SideEffectType.DATAFLOW_SIDE_EFFECTING),
    )(pltpu.with_memory_space_constraint(arr, pltpu.HBM), pltpu.with_memory_space_constraint(land, pltpu.HBM))


def _peer_wait(kind, send_sems, recv_sems, src_thru, land_thru, after, *, name):
    def body(src_ref, land_ref, send_sems, recv_sems, *_):
        _, copies = _peer_copies(kind, src_ref, land_ref, send_sems, recv_sems, None)
        for cp in copies:
            cp.wait_send()
            cp.wait_recv()

    return pl.pallas_call(
        body, name=name,
        out_shape=(pltpu.HBM(src_thru.shape, src_thru.dtype), pltpu.HBM(land_thru.shape, land_thru.dtype)),
        in_specs=(HBM, HBM, SEMAPHORES, SEMAPHORES) + (ANY,) * len(after), out_specs=(HBM, HBM),
        input_output_aliases={0: 0, 1: 1},
        compiler_params=pltpu.CompilerParams(has_side_effects=pltpu.SideEffectType.DATAFLOW_SIDE_EFFECTING),
    )(src_thru, land_thru, send_sems, recv_sems, *after)


def _add_rider(rider, in_specs, args, out_specs, out_shape):
    if rider is None:
        return []
    _, arr = rider
    in_specs.append(ANY)
    args.append(arr)
    out_specs.append(ANY)
    out_shape.append(jax.ShapeDtypeStruct((N_DEV,) + arr.shape[-2:], arr.dtype))
    return list(PEER_SEMS)


def _split_rider(refs, rider, n_in, n_out):
    if rider is None:
        return refs, None
    refs = list(refs)
    rin = refs.pop(n_in)
    rout = refs.pop(n_in + n_out)
    return refs[:-3], (rin, rout, *refs[-3:])


def _ride_start(rider, ride_refs, first):
    if rider is None:
        return

    @pl.when(first)
    def _():
        mine, copies = _peer_copies(rider[0], *ride_refs)
        mine.start()
        for cp in copies:
            cp.start()


def _ride_wait(rider, ride_refs, last):
    if rider is None:
        return

    @pl.when(last)
    def _():
        mine, copies = _peer_copies(rider[0], *ride_refs)
        for cp in copies:
            cp.wait()
        mine.wait()


def _gathered_cols(blocks, kdim):
    n = blocks.shape[1] * WIDE // kdim
    return blocks.reshape(N_DEV, kdim, n).transpose(1, 0, 2).reshape(kdim, N_DEV * n)


def _scatter_cols(dw):
    kdim, n8 = dw.shape
    n = n8 // N_DEV
    return dw.reshape(kdim, N_DEV, n).transpose(1, 0, 2).reshape(N_DEV, kdim * n // WIDE, WIDE)


def _pad_rows(a, rows):
    pad = [(0, 0)] * a.ndim
    pad[-2] = (0, rows - a.shape[-2])
    return jnp.pad(a, pad)


def _layer0_in_weight_t(wt):
    cq, ckv, kpe = wt[0:256], wt[256:384], wt[384:416]
    q_s, k_s, v_s, gate = wt[416:928], wt[928:1056], wt[1056:1184], wt[1184:2208]
    z = jnp.zeros((64, wt.shape[1]), wt.dtype)
    return jnp.concatenate([gate, cq, ckv, z, kpe, z[:32], q_s, k_s, v_s], axis=0)


def _layer0_in_grad_t(dwt):
    gate, cq, ckv, kpe = dwt[0:1024], dwt[1024:1280], dwt[1280:1408], dwt[1472:1504]
    q_s, k_s, v_s = dwt[1536:2048], dwt[2048:2176], dwt[2176:2304]
    return jnp.concatenate([cq, ckv, kpe, q_s, k_s, v_s, gate], axis=0)


def _early_grads_pack(d_w0t, d_q, d_kv, *, name):
    blocks = ((256, 1024), (128, 1280), (32, 1472), (512, 1536), (128, 2048), (128, 2176), (1024, 0))

    def body(w_ref, q_ref, kv_ref, out_ref):
        for p in range(N_DEV):
            lo, hi, at = p * N_E_IN, (p + 1) * N_E_IN, 0
            for rows, first in blocks:
                start, stop = max(lo, at), min(hi, at + rows)
                if start < stop:
                    out_ref[p, start - lo:stop - lo, :] = w_ref[first + start - at:first + stop - at, :]
                at += rows
            out_ref[p, N_E_IN:RA0, :] = jnp.zeros((RA0 - N_E_IN, WIDE), out_ref.dtype)
            out_ref[p, RA0:RA0 + 32, :] = q_ref[p]
            out_ref[p, RA0 + 32:, :] = kv_ref[p]

    arrays = (d_w0t, d_q, d_kv)
    return pl.pallas_call(
        body, name=name, grid=(1,), in_specs=[_resident(a.shape, lambda i, n=a.ndim: (0,) * n) for a in arrays],
        out_specs=_resident((N_DEV, RA0 + RB0, WIDE), lambda i: (0, 0, 0)),
        out_shape=jax.ShapeDtypeStruct((N_DEV, RA0 + RB0, WIDE), BF16), compiler_params=_params(("arbitrary",)),
    )(*arrays)


def _layer1_in_weight_t(wt):
    main = jnp.concatenate([wt[:3 * D_MODEL], wt[3 * D_MODEL + FOX_HEADS:]], axis=0)
    return main, _pad_rows(wt[3 * D_MODEL:3 * D_MODEL + FOX_HEADS], LANES)


def _layer1_in_unpack(gath, *, name):
    n_main = 3 * D_MODEL

    def body(g_ref, w_ref, f_ref, o1_ref, o0_ref):
        f_ref[...] = jnp.zeros_like(f_ref)
        for p in range(N_DEV):
            o1_ref[128 * p:128 * p + 128, :] = g_ref[p, RA1:RA1 + 128, :]
            o0_ref[128 * p:128 * p + 128, :] = g_ref[p, RA1 + 128:RA1 + 256, :]
            lo, hi = p * N_O_IN, (p + 1) * N_O_IN
            for ref, first, start, stop in ((w_ref, 0, lo, min(hi, n_main)),
                                            (f_ref, -n_main, max(lo, n_main), min(hi, n_main + FOX_HEADS)),
                                            (w_ref, -FOX_HEADS, max(lo, n_main + FOX_HEADS), hi)):
                if start < stop:
                    ref[start + first:stop + first, :] = g_ref[p, start - lo:stop - lo, :]

    return pl.pallas_call(
        body, name=name, grid=(1,), in_specs=[_resident((N_DEV, RA1 + 256, WIDE), lambda i: (0, 0, 0))],
        out_specs=[_resident((rows, WIDE), lambda i: (0, 0)) for rows in (n_main + D_MODEL, LANES, D_MODEL, D_MODEL)],
        out_shape=[jax.ShapeDtypeStruct((rows, WIDE), gath.dtype) for rows in (n_main + D_MODEL, LANES, D_MODEL, D_MODEL)],
        compiler_params=_params(("arbitrary",)),
    )(gath)


def _late_grads_pack(d_qkv, d_wft, d_gate, d_wo1, d_wo0, d_o_g, *, name):
    n_main = 3 * D_MODEL
    arrays = (d_qkv, d_wft, d_gate, d_wo1, d_wo0, d_o_g)

    def body(q_ref, f_ref, g_ref, o1_ref, o0_ref, og_ref, out_ref):
        for p in range(N_DEV):
            lo, hi = p * N_O_IN, (p + 1) * N_O_IN
            for ref, first, start, stop in ((q_ref, 0, lo, min(hi, n_main)),
                                            (f_ref, -n_main, max(lo, n_main), min(hi, n_main + FOX_HEADS)),
                                            (g_ref, -n_main - FOX_HEADS, max(lo, n_main + FOX_HEADS), hi)):
                if start < stop:
                    out_ref[p, start - lo:stop - lo, :] = ref[start + first:stop + first, :]
            out_ref[p, N_O_IN:RA1, :] = jnp.zeros((RA1 - N_O_IN, WIDE), out_ref.dtype)
            out_ref[p, RA1:RA1 + 128, :] = o1_ref[128 * p:128 * p + 128, :]
            out_ref[p, RA1 + 128:RA1 + 256, :] = o0_ref[128 * p:128 * p + 128, :]
            out_ref[p, RA1 + 256:, :] = og_ref[p]

    return pl.pallas_call(
        body, name=name, grid=(1,), in_specs=[_resident(a.shape, lambda i, n=a.ndim: (0,) * n) for a in arrays],
        out_specs=_resident((N_DEV, RA1 + RB1, WIDE), lambda i: (0, 0, 0)),
        out_shape=jax.ShapeDtypeStruct((N_DEV, RA1 + RB1, WIDE), BF16), compiler_params=_params(("arbitrary",)),
    )(*arrays)


def _q_up_weight(w):
    return jnp.pad(w.reshape(MLA_Q_RANK, MLA_HEADS, 96), ((0, 0), (0, 0), (0, 32))).reshape(MLA_Q_RANK, MLA_HEADS * LANES)


def _q_up_grad(dwp):
    return dwp.reshape(MLA_Q_RANK, MLA_HEADS, LANES)[:, :, :96].reshape(MLA_Q_RANK, MLA_HEADS * 96)


def _kv_up_weight(w):
    w4 = w.reshape(MLA_KV_RANK, MLA_HEADS, 2, 64)
    kp = jnp.pad(w4[:, :, 0, :], ((0, 0), (0, 0), (0, 64))).reshape(MLA_KV_RANK, MLA_HEADS * LANES)
    vp = w4[:, :, 1, :].reshape(MLA_KV_RANK, MLA_HEADS * 64)
    return jnp.concatenate([kp, vp], axis=1)


def _kv_up_grad(dwp):
    dk = dwp[:, :MLA_HEADS * LANES].reshape(MLA_KV_RANK, MLA_HEADS, LANES)[:, :, :64]
    dv = dwp[:, MLA_HEADS * LANES:].reshape(MLA_KV_RANK, MLA_HEADS, 64)
    return jnp.stack([dk, dv], axis=2).reshape(MLA_KV_RANK, MLA_HEADS * LANES)


def _pad_lanes(a):
    return jnp.pad(a, ((0, 0), (0, LANES - a.shape[1])))


def _small_pack(g_in, g_final, g_q_a, g_kv_a, sinks, b_f, loss):
    rows = [g_in.reshape(8, LANES), g_final.reshape(8, LANES), g_q_a.reshape(2, LANES), g_kv_a.reshape(1, LANES),
            _pad_lanes(sinks.reshape(1, -1)), _pad_lanes(b_f.reshape(1, -1)), _pad_lanes(loss.reshape(1, 1)),
            jnp.zeros((2, LANES), F32)]
    return jnp.concatenate(rows, axis=0)


def _small_unpack(a):
    return (a[0:8].reshape(1, D_MODEL), a[8:16].reshape(D_MODEL), a[16:18].reshape(1, MLA_Q_RANK),
            a[18:19].reshape(1, MLA_KV_RANK), a[19:20, :SWA_HEADS], a[20:21, :FOX_HEADS], a[21, 0])


def _local_step(x, positions, target, e_g_in, early, e_g_q_a, e_g_kv_a, e_sinks,
                late, o_b_f, g_final, scatter1=None, scatter0=None):
    s = x.shape[0]
    mla_scale = (MLA_NOPE + MLA_ROPE) ** -0.5
    fox_scale = FOX_DIM ** -0.5
    n0a = Z0A_UNITS * LANES

    inv_freq = 1.0 / (ROPE_THETA ** (jnp.arange(0, MLA_ROPE, 2, dtype=F32) / MLA_ROPE))
    ang = positions.astype(F32)[:, None] * inv_freq
    cos, sin = jnp.cos(ang), jnp.sin(ang)
    ones, zeros = jnp.ones((s, 64), F32), jnp.zeros((s, 64), F32)
    cos_t = jnp.concatenate([ones, cos, cos, ones[:, :32]], axis=1)
    sin_t = jnp.concatenate([zeros, -sin, sin, zeros[:, :32]], axis=1)
    cos_t, sin_t = lax.optimization_barrier((cos_t, sin_t))

    if len(early) == 3:
        h0 = _rmsnorm_fwd(x, e_g_in, width=D_MODEL, col_blk=0, name="l0_norm")
        w0t, wq, wkv = early
    else:
        pending, token, unpack, prep = early
        h0 = _rmsnorm_fwd(x, e_g_in, width=D_MODEL, col_blk=0, name="l0_norm", after=[token])
        sent, across = _peer_wait("across", *pending, after=[h0] + prep, name="weights0_wait")
        w0t, wq, wkv = unpack(sent, _to_other_core(sent, across, name="weights0_over"))
    z0a, z0b = _matmul_rows([(h0, w0t, True)], [], [], lambda r: (r[:, :n0a], r[:, n0a:]),
                            [("rows", n0a, F32), ("rows", Z0B_UNITS * LANES, BF16)], name="l0_in")
    cqn = _rmsnorm_fwd(z0a, e_g_q_a, width=MLA_Q_RANK, col_blk=4, name="l0_q_norm")
    ckvn = _rmsnorm_fwd(z0a, e_g_kv_a, width=MLA_KV_RANK, col_blk=10, name="l0_kv_norm")
    rope_rows = [(cos_t, LANES, 0), (sin_t, LANES, 0)]
    qm, = _matmul_rows([(cqn, wq, False)], rope_rows, [], _rope_q_epilogue, [("rows", MLA_HEADS * LANES, BF16)],
                       name="l0_q_up")
    kvm, = _matmul_rows([(ckvn, wkv, False)], [(z0a, LANES, 11)] + rope_rows, [], _rope_k_epilogue,
                        [("rows", MLA_HEADS * (LANES + MLA_V), BF16)], name="l0_kv_up")
    gathers = len(late) == 2
    res = _flash_fwd(qm, kvm, kvm, None, n_pairs=MLA_HEADS // 2, hw=LANES, q_off=0, k_off=0, v_off=MLA_HEADS,
                     scale=mla_scale, name="l0_mla_fwd", rider=("gather", late[0]) if gathers else None)
    o_mla, lse_mla = res[0], res[1]
    wo0, o_g_in, w1t, wft, wo1 = late[1](res[2]) if gathers else late
    o_swa, lse_swa = _swa_fwd(z0b, e_sinks, name="l0_swa_fwd")
    half = D_MODEL // 2

    x1, h1, og0 = _matmul_rows(
        [(None, wo0, False)], [(o_mla, half, 0), (o_swa, half, 0), (z0a, D_MODEL, 0), (x, D_MODEL, 0)], [o_g_in],
        lambda r, om, osw, gt, xt, g, made: (*_residual_norm_epilogue(r, xt, g), made),
        [("rows", D_MODEL, F32), ("rows", D_MODEL, BF16), ("rows", D_MODEL, BF16)], name="l0_out",
        prologue=lambda om, osw, gt, xt, g: _gated([om, osw], gt))
    z1, gate1, zf = _matmul_rows(
        [(None, w1t, True), (None, wft, True)], [(h1, D_MODEL, 0)], [],
        lambda r, h, made: (r[0][:, :3 * D_MODEL], r[0][:, 3 * D_MODEL:], r[1]),
        [("rows", 3 * D_MODEL, BF16), ("rows", D_MODEL, F32), ("rows", LANES, F32)], name="l1_in",
        prologue=lambda h: h, separate=True)
    bf = _pad_lanes(o_b_f)
    log_cum = _logf_fwd(zf, bf, name="l1_logf")
    bias2 = (-LOG2E * log_cum[:, :FOX_HEADS]).T
    t_bwd = min(ATT_T, s)
    bias = bias2.reshape(FOX_HEADS // 2, 2, s // t_bwd, 1, t_bwd)
    t_fwd = _fwd_tile(s)
    o_fox, lse_fox = _flash_fwd(z1, z1, z1, bias2.reshape(FOX_HEADS // 2, 2, s // t_fwd, 1, t_fwd),
                                n_pairs=FOX_HEADS // 2, hw=64, q_off=0, k_off=8, v_off=16, scale=fox_scale,
                                name="l1_fox_fwd")

    dx2, loss_part, d_g_final, og1, dx2_bf = _matmul_rows(
        [(None, wo1, False)], [(o_fox, D_MODEL, 0), (gate1, D_MODEL, 0), (x1, D_MODEL, 0), (target, D_MODEL, 0)],
        [g_final.reshape(1, D_MODEL)],
        lambda r, o, gt, xt, tg, g, made: _and_first(_loss_epilogue(r, xt, tg, g), made),
        [("rows", D_MODEL, F32), ("sum", (8, LANES)), ("sum", (1, D_MODEL)), ("rows", D_MODEL, BF16),
         ("rows", D_MODEL, BF16)], name="l1_out_loss", prologue=lambda o, gt, xt, tg, g: _gated([o], gt))

    d_wo1 = _matmul(og1, dx2_bf, ta=True, out_dtype=BF16, name="l1_out_dw")
    do_fox, d_gate1 = _matmul_rows([(dx2_bf, wo1, True)], [(o_fox, D_MODEL, 0), (gate1, D_MODEL, 0)], [],
                                   _gate_bwd_epilogue([D_MODEL]), [("rows", D_MODEL, F32), ("rows", D_MODEL, BF16)],
                                   name="l1_out_dx")
    dqkv1, dbias, drow = _flash_bwd(z1, z1, z1, do_fox, o_fox, lse_fox, bias, n_pairs=FOX_HEADS // 2, hw=64, q_off=0,
                                    k_off=8, v_off=16, scale=fox_scale, qk_dtype=BF16, stacked=True, name="l1_fox_bwd")
    d_log_cum = (drow.reshape(FOX_HEADS, s) - dbias.reshape(FOX_HEADS, s)).T
    d_log_cum = jnp.pad(d_log_cum, ((0, 0), (0, LANES - FOX_HEADS)))
    d_zf, d_bf = _logf_bwd(d_log_cum, zf, bf, name="l1_logf_bwd")
    d_w1t = (_matmul(dqkv1, h1, ta=True, out_dtype=BF16, name="l1_in_dw_qkv"),
             _matmul(d_gate1, h1, ta=True, out_dtype=BF16, name="l1_in_dw_gate"))
    d_wft = _matmul(d_zf, h1, ta=True, out_dtype=BF16, name="l1_in_f_dw")
    dx1, d_o_g_in, dx1_bf = _matmul_rows([(dqkv1, w1t, False, c * D_MODEL, c) for c in range(3)]
                                         + [(d_gate1, w1t, False, 3 * D_MODEL), (d_zf, wft, False)],
                                         [(x1, D_MODEL, 0), (dx2, D_MODEL, 0)], [o_g_in],
                                         lambda *a: _and_first(_rms_bwd_epilogue(*a)),
                                         [("rows", D_MODEL, F32), ("sum", (1, D_MODEL)), ("rows", D_MODEL, BF16)],
                                         name="l1_in_dx")

    d_wo0 = _matmul(og0, dx1_bf, ta=True, out_dtype=BF16, name="l0_out_dw")
    do_mla, do_swa, d_gate0 = _matmul_rows(
        [(dx1_bf, wo0, True)], [(o_mla, half, 0), (o_swa, half, 0), (z0a, D_MODEL, 0)], [], _gate_bwd_epilogue([half, half]),
        [("rows", half, F32), ("rows", half, F32), ("rows", D_MODEL, BF16)], name="l0_out_dx")
    dq_s, dkt_s, dvt_s, d_sinks = _swa_bwd(z0b, e_sinks, do_swa, o_swa, lse_swa, name="l0_swa_bwd")
    dk_s = dkt_s.transpose(0, 2, 1).reshape(s, LANES)
    dv_s = dvt_s.transpose(0, 2, 1).reshape(s, LANES)
    rider = None
    if scatter1 is not None:
        rider = ("exchange", scatter1(dict(w1t=d_w1t, wft=d_wft, wo1=d_wo1, o_g_in=d_o_g_in, wo0=d_wo0)))
    res = _flash_bwd(qm, kvm, kvm, do_mla, o_mla, lse_mla, None, n_pairs=MLA_HEADS // 2, hw=LANES, q_off=0, k_off=0,
                     v_off=MLA_HEADS, scale=mla_scale, qk_dtype=F32, name="l0_mla_bwd", rider=rider)
    dqm, dkm, dvm = res[0], res[1], res[2]
    recv1 = res[3] if rider is not None else None
    d_qp, d_kvp, d_kpe = _rope_bwd(dqm, dkm, dvm, cos_t, sin_t, name="l0_rope_bwd")
    d_wq = _matmul(cqn, d_qp, ta=True, out_dtype=BF16, name="l0_q_up_dw")
    d_cqn = _matmul(d_qp, wq, tb=True, name="l0_q_up_dx")
    d_wkv = _matmul(ckvn, d_kvp, ta=True, out_dtype=BF16, name="l0_kv_up_dw")
    d_ckvn = _matmul(d_kvp, wkv, tb=True, name="l0_kv_up_dx")
    d_cq, d_g_q_a = _rmsnorm_bwd(z0a, e_g_q_a, d_cqn, width=MLA_Q_RANK, col_blk=4, name="l0_q_norm_bwd")
    d_ckv, d_g_kv_a = _rmsnorm_bwd(z0a, e_g_kv_a, d_ckvn, width=MLA_KV_RANK, col_blk=10, name="l0_kv_norm_bwd")
    dz0 = jnp.concatenate([d_gate0, d_cq, d_ckv, d_kpe, dq_s.astype(BF16), dk_s.astype(BF16), dv_s.astype(BF16)], axis=1)
    d_w0t = _matmul(dz0, h0, ta=True, out_dtype=BF16, name="l0_in_dw")
    pending0, after_start = None, []
    if scatter0 is not None:
        *pending0, token = _peer_start("exchange", scatter0(dict(w0t=d_w0t, wq=d_wq, wkv=d_wkv)), name="grads0_start")
        after_start = [token]
    grad_x, d_e_g_in = _matmul_rows(
        [(dz0, w0t, False)], [(x, D_MODEL, 0), (dx1, D_MODEL, 0)], [e_g_in] + after_start,
        lambda dy, xt, add, g, *_: _rms_bwd_epilogue(dy, xt, add, g),
        [("rows", D_MODEL, F32), ("sum", (1, D_MODEL))], name="l0_in_dx")

    return dict(pending0=pending0, recv1=recv1, loss=loss_part[0, 0], grad_x=grad_x, e_g_in=d_e_g_in, w0t=d_w0t, e_g_q_a=d_g_q_a, wq=d_wq,
                e_g_kv_a=d_g_kv_a, wkv=d_wkv, e_sinks=d_sinks[:, 0].reshape(1, SWA_HEADS), wo0=d_wo0,
                o_g_in=d_o_g_in, w1t=d_w1t, wft=d_wft, o_b_f=d_bf[:, :FOX_HEADS], wo1=d_wo1, g_final=d_g_final.reshape(D_MODEL))


def _wide(a, rows):
    flat = a.reshape(-1)
    return jnp.pad(flat, (0, rows * WIDE - flat.shape[0])).reshape(rows, WIDE)


def _rows_b0(w_q, w_kv):
    return jnp.concatenate([_wide(w_q, 32), _wide(w_kv, 16)], axis=0)


def _unflat_b0(f):
    return f[0:24].reshape(1, MLA_Q_RANK, 96), f[32:48].reshape(1, MLA_KV_RANK, 128)


def _rows_b1(o_w_out, e_w_out, g_in):
    return jnp.concatenate([o_w_out, e_w_out, _wide(g_in, 16)], axis=0)


def _unflat_b1(f):
    return f[0:128][None], f[128:256][None], f[256:257, :LANES]


def kernel(x, positions, e_g_in, e_w_in, e_g_q_a, e_w_q_up, e_g_kv_a, e_w_kv_up, e_sinks, e_w_out, o_g_in, o_w_in, o_b_f, o_w_out, g_final, loss_target, m_e_g_in, m_e_w_in, m_e_g_q_a, m_e_w_q_up, m_e_g_kv_a, m_e_w_kv_up, m_e_sinks, m_e_w_out, m_o_g_in, m_o_w_in, m_o_b_f, m_o_w_out, m_g_final, v_e_g_in, v_e_w_in, v_e_g_q_a, v_e_w_q_up, v_e_g_kv_a, v_e_w_kv_up, v_e_sinks, v_e_w_out, v_o_g_in, v_o_w_in, v_o_b_f, v_o_w_out, v_g_final):
    def bf(a):
        return a.astype(BF16)

    me = 4 * lax.axis_index("x") + 2 * lax.axis_index("y") + lax.axis_index("c")
    shard0 = jnp.concatenate([_pad_rows(bf(e_w_in[0]).T, RA0), _rows_b0(bf(e_w_q_up[0]), bf(e_w_kv_up[0]))], axis=0)
    *pending_w0, token_w0 = _peer_start("across", shard0, name="weights0_start")

    def unpack0(sent, gath0):
        gath0 = lax.dynamic_update_slice_in_dim(gath0, sent[None], me, axis=0)
        w0t = _layer0_in_weight_t(gath0[:, :N_E_IN].reshape(N_DEV * N_E_IN, WIDE))
        wq = _q_up_weight(_gathered_cols(gath0[:, RA0:RA0 + 24], MLA_Q_RANK))
        wkv = _kv_up_weight(_gathered_cols(gath0[:, RA0 + 32:RA0 + 48], MLA_KV_RANK))
        return w0t, wq, wkv

    rows_b0 = [_rows_b0(q[0], kv[0]) for q, kv in ((e_w_q_up, e_w_kv_up), (m_e_w_q_up, m_e_w_kv_up), (v_e_w_q_up, v_e_w_kv_up))]
    rows_b1 = [_rows_b1(o[0], e[0], g) for o, e, g in ((o_w_out, e_w_out, o_g_in), (m_o_w_out, m_e_w_out, m_o_g_in),
                                                       (v_o_w_out, v_e_w_out, v_o_g_in))]

    g_bits = lax.bitcast_convert_type(o_g_in.reshape(LANES), BF16)
    shard1 = jnp.concatenate([_pad_rows(bf(o_w_in[0]).T, RA1), _rows_b1(bf(o_w_out[0]), bf(e_w_out[0]), g_bits)], axis=0)

    def unpack1(gath1):
        w1t, wft, wo1, wo0 = _layer1_in_unpack(gath1, name="weights1_unpack")
        bits = gath1[:, RA1 + 256, :2 * LANES].reshape(N_DEV, LANES, 2)
        return wo0, lax.bitcast_convert_type(bits, F32).reshape(1, D_MODEL), w1t, wft, wo1

    def scatter1(g):
        d_o_g = jnp.pad(bf(g["o_g_in"]).reshape(N_DEV, 1, LANES), ((0, 0), (0, 15), (0, WIDE - LANES)))
        return _late_grads_pack(g["w1t"][0], g["wft"], g["w1t"][1], g["wo1"], g["wo0"], d_o_g, name="grads1_pack")

    def scatter0(g):
        return _early_grads_pack(g["w0t"], _pad_rows(_scatter_cols(_q_up_grad(g["wq"])), 32),
                                 _scatter_cols(_kv_up_grad(g["wkv"])), name="grads0_pack")

    gr = _local_step(x[0], positions[0], loss_target[0], e_g_in,
                     (pending_w0, token_w0, unpack0, [shard1] + rows_b0 + rows_b1), e_g_q_a, e_g_kv_a, e_sinks,
                     (shard1, unpack1), o_b_f, g_final, scatter1=scatter1, scatter0=scatter0)

    def in_projection(recv, ra, n, w, m, v, name):
        g = _sum8(recv, ra, name=name + "_grad_sum")[:n].reshape(n, 1, D_MODEL)
        w, m, v = [jnp.transpose(a, (2, 0, 1)) for a in (w, m, v)]
        return (g, *_adamw_columns(g, w, m, v, name=name + "_adamw"))

    o_in = in_projection(gr["recv1"], RA1, N_O_IN, o_w_in, m_o_w_in, v_o_w_in, "o_w_in")
    b1 = _adamw(gr["recv1"][:, RA1:], *rows_b1, name="adamw_late")

    small = _small_pack(gr["e_g_in"], gr["g_final"], gr["e_g_q_a"], gr["e_g_kv_a"], gr["e_sinks"], gr["o_b_f"], gr["loss"])
    small_all = _all_gather(small, name="small_all_gather")
    zero = jnp.zeros((), F32)
    w_small = _small_pack(e_g_in, g_final, e_g_q_a, e_g_kv_a, e_sinks, o_b_f, zero)
    m_small = _small_pack(m_e_g_in, m_g_final, m_e_g_q_a, m_e_g_kv_a, m_e_sinks, m_o_b_f, zero)
    v_small = _small_pack(v_e_g_in, v_g_final, v_e_g_q_a, v_e_g_kv_a, v_e_sinks, v_o_b_f, zero)
    smalls = _adamw(small_all, w_small, m_small, v_small, name="adamw_replicated")
    g_sm, d_sm, m_sm, v_sm = [_small_unpack(a) for a in smalls]
    loss = g_sm[6]

    sent0, recv0 = _peer_wait("exchange", *gr["pending0"], after=[o_in[1], b1[1], smalls[1]], name="grads0_wait")
    own = lax.dynamic_slice_in_dim(sent0, me, 1, axis=0)
    recv0 = lax.dynamic_update_slice_in_dim(recv0, own, me, axis=0)
    e_in = in_projection(recv0, RA0, N_E_IN, e_w_in, m_e_w_in, v_e_w_in, "e_w_in")
    b0 = _adamw(recv0[:, RA0:], *rows_b0, name="adamw_early")

    def sharded(k):
        q_up, kv_up = _unflat_b0(b0[k])
        o_out, e_out, o_g = _unflat_b1(b1[k])
        return jnp.transpose(e_in[k], (1, 2, 0)), q_up, kv_up, e_out, jnp.transpose(o_in[k], (1, 2, 0)), o_out, o_g

    g_sh, d_sh, m_sh, v_sh = [sharded(k) for k in range(4)]

    def leaves(sh, sm):
        return (sm[0], sh[0], sm[2], sh[1], sm[3], sh[2], sm[4], sh[3], sh[6], sh[4], sm[5], sh[5], sm[1])

    return (loss, gr["grad_x"][None], *leaves(g_sh, g_sm), *leaves(d_sh, d_sm), *leaves(m_sh, m_sm), *leaves(v_sh, v_sm))
```

```python
import functools

import jax
import jax.numpy as jnp
from jax import lax
from jax.experimental import pallas as pl
from jax.experimental.pallas import tpu as pltpu

F32 = jnp.float32
BF16 = jnp.bfloat16
NEG_INF = float("-inf")

N_DEV = 8
LANES = 128
D_MODEL = 1024
EPS = 1e-6
ROPE_THETA = 10000.0
MLA_HEADS = 8
MLA_Q_RANK = 256
MLA_KV_RANK = 128
MLA_NOPE = 64
MLA_ROPE = 32
MLA_V = 64
SWA_HEADS = 8
SWA_KV_HEADS = 2
SWA_DIM = 64
WINDOW = 128
FOX_HEADS = 16
FOX_DIM = 64

ADAM_LR = 0.001
ADAM_B1 = 0.9
ADAM_B2 = 0.999
ADAM_EPS = 1e-08
ADAM_WD = 0.01
ADAM_STEP = 10

ATT_T = 512
ATT_T_FWD = 1024
VMEM_LIMIT = 56 * 1024 * 1024
MATMUL_B_BLOCK_BYTES = 8 * 1024 * 1024

Z0A_UNITS = 12
Z0B_UNITS = 6

WIDE = 1024
N_E_IN = 276
N_O_IN = 514
RA0 = 288
RB0 = 32 + 16
RA1 = 528
RB1 = 128 + 128 + 16
SMALL_ROWS = 24


def _tile(n, cands):
    for c in cands:
        if n % c == 0:
            return c
    raise ValueError(f"no tile for {n}")


ROW_TILES = (512, 256, 128)


def _params(sem, vmem=VMEM_LIMIT):
    return pltpu.CompilerParams(dimension_semantics=sem, vmem_limit_bytes=vmem)


def _matmul(a, b, *, name, ta=False, tb=False, out_dtype=F32, b_rows=None):
    if ta:
        kdim, m = a.shape[-2], a.shape[-1] * (a.shape[0] if a.ndim == 3 else 1)
    else:
        m, kdim = a.shape
    if tb:
        n, kb = b.shape
    else:
        kb, n = b.shape
    assert kdim == kb, (a.shape, b.shape)
    b_start = 0
    if b_rows is not None:
        assert tb
        b_start, n = b_rows
    tm = _tile(m, (512, 256, 128))
    tn = _tile(n, [c for c in (1024, 768, 512, 384, 256, 128)
                   if c * kdim * b.dtype.itemsize <= MATMUL_B_BLOCK_BYTES and b_start % c == 0])
    assert b_start % tn == 0, (b_start, tn)
    b_off = b_start // tn
    dims = (((0 if ta else 1,), (1 if tb else 0,)), ((), ()))

    def body(a_ref, b_ref, o_ref):
        r = lax.dot_general(a_ref[...].astype(BF16), b_ref[...].astype(BF16), dims, preferred_element_type=F32)
        o_ref[...] = r.astype(out_dtype)

    if a.ndim == 3:
        per = a.shape[2] // tm
        a_spec = pl.BlockSpec((None, kdim, tm), lambda i, j: (i // per, 0, i % per))
    else:
        a_spec = pl.BlockSpec((kdim, tm), lambda i, j: (0, i)) if ta else pl.BlockSpec((tm, kdim), lambda i, j: (i, 0))
    b_spec = pl.BlockSpec((tn, kdim), lambda i, j: (j + b_off, 0)) if tb else pl.BlockSpec((kdim, tn), lambda i, j: (0, j))
    return pl.pallas_call(
        body, name=name, grid=(m // tm, n // tn), in_specs=[a_spec, b_spec],
        out_specs=pl.BlockSpec((tm, tn), lambda i, j: (i, j)), out_shape=jax.ShapeDtypeStruct((m, n), out_dtype),
        compiler_params=_params(("parallel", "parallel")),
    )(a, b)


def _rmsnorm_fwd(x, g, *, width, col_blk, name, after=()):
    s = x.shape[0]
    tm = _tile(s, ROW_TILES)

    def body(x_ref, g_ref, *rest):
        y_ref = rest[-1]
        xf = x_ref[...].astype(F32)
        r = lax.rsqrt(jnp.mean(xf * xf, axis=-1, keepdims=True) + EPS)
        y_ref[...] = ((xf * r) * g_ref[...]).astype(BF16)

    return pl.pallas_call(
        body, name=name, grid=(s // tm,),
        in_specs=[pl.BlockSpec((tm, width), lambda i: (i, col_blk)), pl.BlockSpec((1, width), lambda i: (0, 0))]
        + [ANY] * len(after),
        out_specs=pl.BlockSpec((tm, width), lambda i: (i, 0)),
        out_shape=jax.ShapeDtypeStruct((s, width), BF16),
        compiler_params=_params(("parallel",)),
    )(x, g, *after)


def _rmsnorm_bwd(x, g, dy, *, width, col_blk, name):
    s = x.shape[0]
    tm = _tile(s, ROW_TILES)

    def body(x_ref, g_ref, dy_ref, dx_ref, dg_ref):
        @pl.when(pl.program_id(0) == 0)
        def _():
            dg_ref[...] = jnp.zeros_like(dg_ref)

        dx, dg = _rms_bwd_epilogue(dy_ref[...], x_ref[...], 0.0, g_ref[...])
        dg_ref[...] += dg
        dx_ref[...] = dx.astype(BF16)

    return pl.pallas_call(
        body, name=name, grid=(s // tm,),
        in_specs=[pl.BlockSpec((tm, width), lambda i: (i, col_blk)), pl.BlockSpec((1, width), lambda i: (0, 0)),
                  pl.BlockSpec((tm, width), lambda i: (i, 0))],
        out_specs=[pl.BlockSpec((tm, width), lambda i: (i, 0)), pl.BlockSpec((1, width), lambda i: (0, 0))],
        out_shape=[jax.ShapeDtypeStruct((s, width), BF16), jax.ShapeDtypeStruct((1, width), F32)],
        compiler_params=_params(("arbitrary",)),
    )(x, g, dy)


def _sigmoid(x):
    return 1.0 / (1.0 + jnp.exp(-x))


def _matmul_rows(terms, row_inputs, params, epilogue, outs, *, name, prologue=None, separate=False):
    s = row_inputs[0][0].shape[0] if row_inputs else terms[0][0].shape[-2]
    tm = _tile(s, ROW_TILES)
    steps = s // tm
    n_r, n_p, n_o = len(row_inputs), len(params), len(outs)
    n_t = sum(1 if term[0] is None else 2 for term in terms)

    def body(*refs):
        t_refs, r_refs = list(refs[:n_t]), refs[n_t:n_t + n_r]
        p_refs, o_refs = refs[n_t + n_r:n_t + n_r + n_p], refs[n_t + n_r + n_p:]
        i = pl.program_id(0)
        rows, small = [r[...] for r in r_refs], [p[...] for p in p_refs]
        made = None if prologue is None else prologue(*rows, *small)
        parts = []
        for term in terms:
            a = made if term[0] is None else t_refs.pop(0)[...].astype(BF16)
            dims = (((1,), (1 if term[2] else 0,)), ((), ()))
            parts.append(lax.dot_general(a, t_refs.pop(0)[...].astype(BF16), dims, preferred_element_type=F32))
        acc = parts if separate else sum(parts[1:], parts[0])
        vals = epilogue(acc, *rows, *small) if prologue is None else epilogue(acc, *rows, *small, made)
        for ref, val, out in zip(o_refs, vals, outs):
            if out[0] == "rows":
                ref[...] = val.astype(ref.dtype)
            else:
                @pl.when(i == 0)
                def _(ref=ref):
                    ref[...] = jnp.zeros_like(ref)

                ref[...] += val

    in_specs, args = [], []
    for term in terms:
        a, b = term[0], term[1]
        if a is None:
            in_specs.append(_resident(b.shape, lambda i: (0, 0)))
            args.append(b)
            continue
        b_rows = b.shape[0] if term[2] or len(term) < 4 else a.shape[-1]
        b_blk = 0 if len(term) < 4 else term[3] // b_rows
        if len(term) == 5:
            a_spec = pl.BlockSpec((None, tm, a.shape[2]), lambda i, c=term[4]: (c, i, 0))
        else:
            a_spec = pl.BlockSpec((tm, a.shape[1]), lambda i: (i, 0))
        in_specs += [a_spec, _resident((b_rows, b.shape[1]), lambda i, b_blk=b_blk: (b_blk, 0))]
        args += [a, b]
    for arr, width, col_blk in row_inputs:
        in_specs.append(pl.BlockSpec((tm, width), lambda i, col_blk=col_blk: (i, col_blk)))
        args.append(arr)
    for p in params:
        in_specs.append(pl.BlockSpec(p.shape, lambda i: (0, 0)))
        args.append(p)
    out_specs, out_shape = [], []
    for out in outs:
        if out[0] == "rows":
            out_specs.append(pl.BlockSpec((tm, out[1]), lambda i: (i, 0)))
            out_shape.append(jax.ShapeDtypeStruct((s, out[1]), out[2]))
        else:
            out_specs.append(pl.BlockSpec(out[1], lambda i: (0, 0)))
            out_shape.append(jax.ShapeDtypeStruct(out[1], F32))
    return pl.pallas_call(
        body, name=name, grid=(steps,), in_specs=in_specs, out_specs=out_specs, out_shape=out_shape,
        compiler_params=_params(("arbitrary",)),
    )(*args)


def _rms_stats(x):
    r = lax.rsqrt(jnp.mean(x * x, axis=-1, keepdims=True) + EPS)
    return r, x * r


def _gated(o_parts, gate):
    o = o_parts[0] if len(o_parts) == 1 else jnp.concatenate(o_parts, axis=1)
    return (o * (gate * _sigmoid(gate))).astype(BF16)


def _and_first(vals, *more):
    return (*vals, *more, vals[0])


def _residual_norm_epilogue(r, x, g):
    x1 = x + r
    _, xh = _rms_stats(x1)
    return x1, xh * g


def _rms_bwd_epilogue(dy, x, add, g):
    r, xh = _rms_stats(x)
    dxh = dy * g
    dx = r * (dxh - xh * jnp.mean(dxh * xh, axis=-1, keepdims=True)) + add
    return dx, jnp.sum(dy * xh, axis=0, keepdims=True)


def _loss_epilogue(r, x1, target, g):
    rs, xh = _rms_stats(x1 + r)
    err = xh * g - target
    loss = jnp.broadcast_to(0.5 * jnp.sum(jnp.mean(err * err, axis=-1, keepdims=True)), (8, LANES))
    dy = err * (1.0 / D_MODEL)
    dxh = dy * g
    dx = rs * (dxh - xh * jnp.mean(dxh * xh, axis=-1, keepdims=True))
    return dx, loss, jnp.sum(dy * xh, axis=0, keepdims=True)


def _gate_bwd_epilogue(widths):
    def epilogue(d, *rows):
        o_parts, gt = rows[:-1], rows[-1]
        o = o_parts[0] if len(o_parts) == 1 else jnp.concatenate(o_parts, axis=1)
        sg = _sigmoid(gt)
        do = d * (gt * sg)
        d_gate = d * o * (sg * (1.0 + gt * (1.0 - sg)))
        cuts = [sum(widths[:k]) for k in range(len(widths) + 1)]
        return tuple(do[:, cuts[k]:cuts[k + 1]] for k in range(len(widths))) + (d_gate,)

    return epilogue


def _rot_half(x):
    lane = lax.broadcasted_iota(jnp.int32, x.shape, 1)
    return jnp.where(lane < 80, pltpu.roll(x, LANES - 16, axis=1), pltpu.roll(x, 16, axis=1))


def _rot_half_t(g):
    lane = lax.broadcasted_iota(jnp.int32, g.shape, 1)
    lo = (lane >= MLA_NOPE) & (lane < MLA_NOPE + MLA_ROPE // 2)
    hi = (lane >= MLA_NOPE + MLA_ROPE // 2) & (lane < MLA_NOPE + MLA_ROPE)
    return jnp.where(lo, pltpu.roll(g, LANES - 16, axis=1), jnp.where(hi, pltpu.roll(g, 16, axis=1), 0.0))


def _rope_q_epilogue(q, c, sn):
    heads = [q[:, h * LANES:(h + 1) * LANES] for h in range(MLA_HEADS)]
    return (jnp.concatenate([qh * c + _rot_half(qh) * sn for qh in heads], axis=1),)


def _rope_k_epilogue(kv, kpe, c, sn):
    kpe_r = kpe * c + _rot_half(kpe) * sn
    lane = lax.broadcasted_iota(jnp.int32, kpe.shape, 1)
    heads = [jnp.where(lane < MLA_NOPE, kv[:, h * LANES:(h + 1) * LANES], kpe_r) for h in range(MLA_HEADS)]
    return (jnp.concatenate(heads + [kv[:, MLA_HEADS * LANES:]], axis=1),)


def _rope_bwd(dqm, dkm, dvm, cos_t, sin_t, *, name):
    s = dqm.shape[0]
    tm = _tile(s, ROW_TILES)
    hw = MLA_HEADS * LANES
    vw = MLA_HEADS * MLA_V

    def body(dq_ref, dk_ref, dv_ref, c_ref, s_ref, dqp_ref, dkv_ref, dkpe_ref):
        c = c_ref[...]
        sn = s_ref[...]
        ksum = jnp.zeros((tm, LANES), F32)
        for h in range(MLA_HEADS):
            sl = slice(h * LANES, (h + 1) * LANES)
            dq = dq_ref[:, sl]
            dqp_ref[:, sl] = (dq * c + _rot_half_t(dq * sn)).astype(BF16)
            dk = dk_ref[:, sl]
            dkv_ref[:, sl] = dk.astype(BF16)
            ksum = ksum + dk
        dkv_ref[:, hw:] = dv_ref[...]
        lane = lax.broadcasted_iota(jnp.int32, ksum.shape, 1)
        dkpe = ksum * c + _rot_half_t(ksum * sn)
        dkpe_ref[...] = jnp.where((lane >= MLA_NOPE) & (lane < MLA_NOPE + MLA_ROPE), dkpe, 0.0).astype(BF16)

    return pl.pallas_call(
        body, name=name, grid=(s // tm,),
        in_specs=[pl.BlockSpec((tm, hw), lambda i: (i, 0)), pl.BlockSpec((tm, hw), lambda i: (i, 0)),
                  pl.BlockSpec((tm, vw), lambda i: (i, 0)),
                  pl.BlockSpec((tm, LANES), lambda i: (i, 0)), pl.BlockSpec((tm, LANES), lambda i: (i, 0))],
        out_specs=[pl.BlockSpec((tm, hw), lambda i: (i, 0)), pl.BlockSpec((tm, hw + vw), lambda i: (i, 0)),
                   pl.BlockSpec((tm, LANES), lambda i: (i, 0))],
        out_shape=[jax.ShapeDtypeStruct((s, hw), BF16), jax.ShapeDtypeStruct((s, hw + vw), BF16),
                   jax.ShapeDtypeStruct((s, LANES), BF16)],
        compiler_params=_params(("parallel",)),
    )(dqm, dkm, dvm, cos_t, sin_t)


def _head_mask(shape, a):
    lane = lax.broadcasted_iota(jnp.int32, shape, 1)
    return (lane >= 64 * a) & (lane < 64 * (a + 1))


_NT = (((1,), (1,)), ((), ()))
LOG2E = 1.4426950408889634


def _stack_heads(tile, hw):
    lane = lax.broadcasted_iota(jnp.int32, tile.shape, 1)
    z = jnp.zeros_like(tile)
    return jnp.concatenate([jnp.where(lane < hw, tile, z), jnp.where(lane >= hw, tile, z)], axis=0)


def _stacked_rows(r0, r1, t):
    n = r0.shape[-1]
    return jnp.concatenate([jnp.broadcast_to(r0, (t, n)), jnp.broadcast_to(r1, (t, n))], axis=0)


def _resident(block, index_map):
    return pl.BlockSpec(block, index_map, pipeline_mode=pl.Buffered(1))


def _fwd_tile(s):
    return ATT_T_FWD if s % ATT_T_FWD == 0 else min(ATT_T, s)


def _flash_fwd(q, k, v, bias, *, n_pairs, hw, q_off, k_off, v_off, scale, name, rider=None):
    s = q.shape[0]
    t = _fwd_tile(s)
    nb = s // t
    qw = 2 * hw
    has_bias = bias is not None
    c1 = scale * LOG2E

    def body(*refs):
        refs, ride_refs = _split_rider(refs, rider, n_in=4 if has_bias else 3, n_out=2)
        if has_bias:
            q_ref, k_ref, v_ref, b_ref, o_ref, lse_ref, vt_ref, bcol_ref = refs
        else:
            q_ref, k_ref, v_ref, o_ref, lse_ref, vt_ref = refs
            b_ref = bcol_ref = None
        _ride_start(rider, ride_refs, pl.program_id(0) == 0)
        row = lax.broadcasted_iota(jnp.int32, (t, t), 0)
        col = lax.broadcasted_iota(jnp.int32, (t, t), 1)
        cmask_t = jnp.concatenate([row <= col, row <= col], axis=1)
        lane_lt64 = lax.broadcasted_iota(jnp.int32, (t, LANES), 1) < 64

        def as_column(r):
            return jnp.broadcast_to(r, (8, r.shape[1])).T[:, 0:1]

        def v_block(j, _):
            c0 = pl.multiple_of(j * t, t)
            vt_ref[j] = v_ref[pl.ds(c0, t), :].astype(F32).T.astype(BF16)
            if has_bias:
                for a in range(2):
                    bcol_ref[a, pl.ds(c0, t), :] = as_column(b_ref[0, a, j])
            return 0

        lax.fori_loop(0, nb, v_block, 0)

        def stacked_queries(i):
            return _stack_heads(q_ref[pl.ds(pl.multiple_of(i * t, t), t), :], hw).astype(F32).T.astype(BF16)

        def kv_step(j, carry, qs_t, masked):
            m, l, acc = carry
            rows = pl.ds(pl.multiple_of(j * t, t), t)
            sc = jnp.dot(k_ref[rows, :], qs_t, preferred_element_type=F32) * c1
            if has_bias:
                sc = sc + jnp.concatenate([jnp.broadcast_to(bcol_ref[0, rows, :], (t, t)),
                                           jnp.broadcast_to(bcol_ref[1, rows, :], (t, t))], axis=1)
            if masked:
                sc = jnp.where(cmask_t, sc, NEG_INF)
            m_new = jnp.maximum(m, jnp.max(sc, axis=0, keepdims=True))
            alpha = jnp.exp2(m - m_new)
            p = jnp.exp2(sc - m_new)
            l_new = alpha * l + jnp.sum(p, axis=0, keepdims=True)
            pv = jnp.dot(vt_ref[j], p.astype(BF16), preferred_element_type=F32)
            return m_new, l_new, alpha * acc + pv

        def finish(i, carry):
            m, l, acc = carry
            r0 = pl.multiple_of(i * t, t)
            out = (acc / l).T
            lse2 = as_column(m + jnp.log2(l))
            lse_ref[0, 0, pl.ds(r0, t), :] = lse2[:t]
            lse_ref[0, 1, pl.ds(r0, t), :] = lse2[t:]
            o_ref[pl.ds(r0, t), :] = jnp.where(lane_lt64, out[:t], out[t:])

        init = (jnp.full((1, 2 * t), NEG_INF, F32), jnp.zeros((1, 2 * t), F32), jnp.zeros((LANES, 2 * t), F32))

        def q_block(i, _):
            qs_t = stacked_queries(i)
            carry = lax.fori_loop(0, i, lambda j, c: kv_step(j, c, qs_t, False), init)
            finish(i, kv_step(i, carry, qs_t, True))
            return 0

        lax.fori_loop(0, nb, q_block, 0)
        _ride_wait(rider, ride_refs, pl.program_id(0) == n_pairs - 1)

    in_specs = [_resident((s, qw), lambda p: (0, q_off + p)), _resident((s, qw), lambda p: (0, k_off + p)),
                _resident((s, LANES), lambda p: (0, v_off + p))]
    args = [q, k, v]
    if has_bias:
        in_specs.append(_resident((1, 2, nb, 1, t), lambda p: (p, 0, 0, 0, 0)))
        args.append(bias)
    out_specs = [pl.BlockSpec((s, LANES), lambda p: (0, p)), pl.BlockSpec((1, 2, s, 1), lambda p: (p, 0, 0, 0))]
    out_shape = [jax.ShapeDtypeStruct((s, n_pairs * LANES), F32), jax.ShapeDtypeStruct((n_pairs, 2, s, 1), F32)]
    scratch = [pltpu.VMEM((nb, LANES, t), BF16)] + ([pltpu.VMEM((2, s, 1), F32)] if has_bias else [])
    scratch += _add_rider(rider, in_specs, args, out_specs, out_shape)
    return pl.pallas_call(
        body, name=name, grid=(n_pairs,), in_specs=in_specs, out_specs=out_specs, out_shape=out_shape,
        scratch_shapes=scratch,
        compiler_params=_params(("parallel",) if rider is None else ("arbitrary",)),
    )(*args)


def _flash_bwd(q, k, v, do, o, lse, bias, *, n_pairs, hw, q_off, k_off, v_off, scale, qk_dtype, name, rider=None,
               stacked=False):
    s = q.shape[0]
    t = min(ATT_T, s)
    nb = s // t
    qw = 2 * hw
    has_bias = bias is not None
    c1 = scale * LOG2E

    def body(*refs):
        n_grads = 1 if stacked else 3
        refs, ride_refs = _split_rider(refs, rider, n_in=7 if has_bias else 6, n_out=n_grads + (2 if has_bias else 0))
        if stacked:
            refs = list(refs)
            n_in = 7 if has_bias else 6
            refs[n_in:n_in + 1] = [refs[n_in].at[0], refs[n_in].at[1], refs[n_in].at[2]]
        if has_bias:
            (q_ref, k_ref, v_ref, do_ref, o_ref, lse_ref, b_ref, dq_ref, dk_ref, dv_ref, db_ref, dr_ref,
             dkt_ref, dvt_ref) = refs
            db_ref[...] = jnp.zeros_like(db_ref)
        else:
            q_ref, k_ref, v_ref, do_ref, o_ref, lse_ref, dq_ref, dk_ref, dv_ref, dkt_ref, dvt_ref = refs
            b_ref = db_ref = dr_ref = None
        _ride_start(rider, ride_refs, pl.program_id(0) == 0)
        dkt_ref[...] = jnp.zeros_like(dkt_ref)
        dvt_ref[...] = jnp.zeros_like(dvt_ref)
        causal = lax.broadcasted_iota(jnp.int32, (t, t), 1) <= lax.broadcasted_iota(jnp.int32, (t, t), 0)
        cmask = jnp.concatenate([causal, causal], axis=0)
        lane_lt_hw = lax.broadcasted_iota(jnp.int32, (t, qw), 1) < hw

        def q_block(i, _):
            r0 = pl.multiple_of(i * t, t)
            qs = _stack_heads(q_ref[pl.ds(r0, t), :], hw)
            dos = _stack_heads(do_ref[pl.ds(r0, t), :], 64)
            ot = o_ref[pl.ds(r0, t), :]
            delta = jnp.sum(dos * jnp.concatenate([ot, ot], axis=0), axis=-1, keepdims=True)
            lse2 = jnp.concatenate([lse_ref[0, 0, pl.ds(r0, t), :], lse_ref[0, 1, pl.ds(r0, t), :]], axis=0)
            dosb = dos.astype(BF16)
            dos_t = dos.T.astype(BF16)
            qs_t = qs.astype(F32).T.astype(BF16)

            def kv_step(j, carry, masked):
                dq, rsum = carry
                c0 = pl.multiple_of(j * t, t)
                kt = k_ref[pl.ds(c0, t), :]
                vt = v_ref[pl.ds(c0, t), :]
                sc = lax.dot_general(qs, kt, _NT, preferred_element_type=F32) * c1
                if has_bias:
                    sc = sc + _stacked_rows(b_ref[0, 0, j], b_ref[0, 1, j], t)
                if masked:
                    sc = jnp.where(cmask, sc, NEG_INF)
                p = jnp.exp2(sc - lse2)
                dp = lax.dot_general(dosb, vt, _NT, preferred_element_type=F32)
                ds = p * (dp - delta)
                dsb = ds.astype(BF16)
                pb = p.astype(BF16)
                if hw == LANES:
                    dvt_ref[j] += jnp.concatenate(
                        [jnp.dot(dos_t[:64, :t], pb[:t], preferred_element_type=F32),
                         jnp.dot(dos_t[64:, t:], pb[t:], preferred_element_type=F32)], axis=0)
                    dkt_ref[j] += jnp.concatenate(
                        [jnp.dot(qs_t[:hw, :t], dsb[:t], preferred_element_type=F32),
                         jnp.dot(qs_t[hw:, t:], dsb[t:], preferred_element_type=F32)], axis=0)
                else:
                    dvt_ref[j] += jnp.dot(dos_t, pb, preferred_element_type=F32)
                    dkt_ref[j] += jnp.dot(qs_t, dsb, preferred_element_type=F32)
                if has_bias:
                    db_ref[0, 0, j] += jnp.sum(ds[:t], axis=0, keepdims=True)
                    db_ref[0, 1, j] += jnp.sum(ds[t:], axis=0, keepdims=True)
                    rsum = rsum + jnp.sum(ds, axis=-1, keepdims=True)
                return dq + jnp.dot(dsb, kt, preferred_element_type=F32), rsum

            init = (jnp.zeros((2 * t, qw), F32), jnp.zeros((2 * t, 1), F32))
            carry = lax.fori_loop(0, i, functools.partial(kv_step, masked=False), init)
            dq, rsum = kv_step(i, carry, True)
            dq = dq * scale
            dq_ref[pl.ds(r0, t), :] = jnp.where(lane_lt_hw, dq[:t], dq[t:]).astype(qk_dtype)
            if has_bias:
                rsum_row = jnp.broadcast_to(rsum, (2 * t, LANES)).T[0:1]
                dr_ref[0, 0, i] = rsum_row[:, :t]
                dr_ref[0, 1, i] = rsum_row[:, t:]
            return 0

        lax.fori_loop(0, nb, q_block, 0)

        def k_block(j, _):
            c0 = pl.multiple_of(j * t, t)
            dk_ref[pl.ds(c0, t), :] = (dkt_ref[j].T * scale).astype(qk_dtype)
            dv_ref[pl.ds(c0, t), :] = dvt_ref[j].T.astype(BF16)
            return 0

        lax.fori_loop(0, nb, k_block, 0)
        _ride_wait(rider, ride_refs, pl.program_id(0) == n_pairs - 1)

    in_specs = [_resident((s, qw), lambda p: (0, q_off + p)), _resident((s, qw), lambda p: (0, k_off + p)),
                _resident((s, LANES), lambda p: (0, v_off + p)),
                _resident((s, LANES), lambda p: (0, p)), _resident((s, LANES), lambda p: (0, p)),
                _resident((1, 2, s, 1), lambda p: (p, 0, 0, 0))]
    args = [q, k, v, do, o, lse]
    if stacked:
        assert qw == LANES and qk_dtype == BF16
        out_specs = [pl.BlockSpec((3, s, LANES), lambda p: (0, 0, p))]
        out_shape = [jax.ShapeDtypeStruct((3, s, n_pairs * LANES), BF16)]
    else:
        out_specs = [pl.BlockSpec((s, qw), lambda p: (0, p)), pl.BlockSpec((s, qw), lambda p: (0, p)),
                     pl.BlockSpec((s, LANES), lambda p: (0, p))]
        out_shape = [jax.ShapeDtypeStruct((s, n_pairs * qw), qk_dtype), jax.ShapeDtypeStruct((s, n_pairs * qw), qk_dtype),
                     jax.ShapeDtypeStruct((s, n_pairs * LANES), BF16)]
    if has_bias:
        in_specs.append(_resident((1, 2, nb, 1, t), lambda p: (p, 0, 0, 0, 0)))
        args.append(bias)
        for _ in range(2):
            out_specs.append(pl.BlockSpec((1, 2, nb, 1, t), lambda p: (p, 0, 0, 0, 0)))
            out_shape.append(jax.ShapeDtypeStruct((n_pairs, 2, nb, 1, t), F32))
    scratch = [pltpu.VMEM((nb, qw, t), F32), pltpu.VMEM((nb, LANES, t), F32)]
    scratch += _add_rider(rider, in_specs, args, out_specs, out_shape)
    return pl.pallas_call(
        body, name=name, grid=(n_pairs,), in_specs=in_specs, out_specs=out_specs, out_shape=out_shape,
        scratch_shapes=scratch,
        compiler_params=_params(("parallel",) if rider is None else ("arbitrary",)),
    )(*args)


def _alibi_slope(h):
    return 2.0 ** (-8.0 * (h + 1.0) / SWA_HEADS)


SWA_ROWS = 512
SWA_SCALE = SWA_DIM ** -0.5


def _swa_geometry(i):
    w = WINDOW
    r0 = pl.multiple_of(i * w, w)
    b0 = pl.multiple_of(jnp.maximum(i - 1, 0) * w, w)
    row = lax.broadcasted_iota(jnp.int32, (w, 2 * w), 0)
    col = lax.broadcasted_iota(jnp.int32, (w, 2 * w), 1)
    dist = row - col + (r0 - b0)
    valid = (dist >= 0) & (dist < w)
    return r0, b0, dist.astype(F32), valid


def _swa_q_head(qblk, h):
    kv = h // (SWA_HEADS // SWA_KV_HEADS)
    if h % 2 != kv:
        qblk = pltpu.roll(qblk, 64, axis=1)
    return jnp.where(_head_mask(qblk.shape, kv), qblk, 0.0)


SWA_GROUP = SWA_HEADS // SWA_KV_HEADS


def _swa_stack(ref, rs, grp):
    parts = []
    for a in range(SWA_GROUP):
        h = SWA_GROUP * grp + a
        parts.append(_swa_q_head(ref[rs, (h // 2) * LANES:(h // 2 + 1) * LANES].astype(F32), h))
    return jnp.concatenate(parts, axis=0)


def _swa_unstack(x, grp):
    tiles = []
    for a in range(SWA_GROUP):
        h = SWA_GROUP * grp + a
        tile = x[a * WINDOW:(a + 1) * WINDOW]
        tiles.append(pltpu.roll(tile, 64, axis=1) if h % 2 != grp else tile)
    return tiles


def _swa_head_column(vals):
    return jnp.concatenate([jnp.full((WINDOW, 1), v, F32) for v in vals], axis=0)


def _swa_logits(qs, kb, dist, valid, grp):
    slopes = _swa_head_column([_alibi_slope(SWA_GROUP * grp + a) for a in range(SWA_GROUP)])
    dist4 = jnp.concatenate([dist] * SWA_GROUP, axis=0)
    valid4 = jnp.concatenate([valid] * SWA_GROUP, axis=0)
    sc = lax.dot_general(qs, kb, _NT, preferred_element_type=F32) * SWA_SCALE - slopes * dist4
    return jnp.where(valid4, sc, NEG_INF)


def _swa_merge_heads(tiles):
    lt64 = lax.broadcasted_iota(jnp.int32, (WINDOW, LANES), 1) < 64
    return jnp.concatenate([jnp.where(lt64, tiles[2 * b], tiles[2 * b + 1]) for b in range(SWA_HEADS // 2)], axis=1)


def _swa_fwd(z0b, sinks, *, name):
    s = z0b.shape[0]
    w = WINDOW
    rows = min(SWA_ROWS, s)
    per_step = rows // w
    qcols = SWA_HEADS * SWA_DIM

    def body(sink_ref, q_ref, k_ref, v_ref, o_ref, lse_ref):
        g = pl.program_id(0)
        for ii in range(per_step):
            rs = slice(ii * w, (ii + 1) * w)
            r0, b0, dist, valid = _swa_geometry(g * per_step + ii)
            kb = k_ref[pl.ds(b0, 2 * w), :]
            vb = v_ref[pl.ds(b0, 2 * w), :]
            o_tiles = []
            for h in range(SWA_HEADS):
                kv = h // SWA_GROUP
                qh = _swa_q_head(q_ref[rs, (h // 2) * LANES:(h // 2 + 1) * LANES].astype(F32), h).astype(BF16)
                sc = lax.dot_general(qh, kb, _NT, preferred_element_type=F32) * SWA_SCALE - _alibi_slope(h) * dist
                sc = jnp.where(valid, sc, NEG_INF)
                sink = sink_ref[0, h]
                m = jnp.maximum(jnp.max(sc, axis=-1, keepdims=True), sink)
                p = jnp.exp(sc - m)
                l = jnp.sum(p, axis=-1, keepdims=True) + jnp.exp(sink - m)
                oh = jnp.dot(p.astype(BF16), vb, preferred_element_type=F32) / l
                o_tiles.append(pltpu.roll(oh, 64, axis=1) if h % 2 != kv else oh)
                lse_ref[h, rs, :] = m + jnp.log(l)
            o_ref[rs, :] = _swa_merge_heads(o_tiles)

    return pl.pallas_call(
        body, name=name, grid=(s // rows,),
        in_specs=[pl.BlockSpec(memory_space=pltpu.SMEM),
                  pl.BlockSpec((rows, qcols), lambda g: (g, 0)),
                  pl.BlockSpec((s, LANES), lambda g: (0, 4)), pl.BlockSpec((s, LANES), lambda g: (0, 5))],
        out_specs=[pl.BlockSpec((rows, qcols), lambda g: (g, 0)), pl.BlockSpec((SWA_HEADS, rows, 1), lambda g: (0, g, 0))],
        out_shape=[jax.ShapeDtypeStruct((s, qcols), F32), jax.ShapeDtypeStruct((SWA_HEADS, s, 1), F32)],
        compiler_params=_params(("parallel",)),
    )(sinks, z0b, z0b, z0b)


def _swa_bwd(z0b, sinks, do, o, lse, *, name):
    s = z0b.shape[0]
    w = WINDOW
    rows = min(SWA_ROWS, s)
    per_step = rows // w
    qcols = SWA_HEADS * SWA_DIM
    nblk = s // w

    def body(sink_ref, q_ref, k_ref, v_ref, do_ref, o_ref, lse_ref, dq_ref, dkt_ref, dvt_ref, dsink_ref):
        g = pl.program_id(0)

        @pl.when(g == 0)
        def _():
            dkt_ref[...] = jnp.zeros_like(dkt_ref)
            dvt_ref[...] = jnp.zeros_like(dvt_ref)
            dsink_ref[...] = jnp.zeros_like(dsink_ref)

        for ii in range(per_step):
            i = g * per_step + ii
            rs = slice(ii * w, (ii + 1) * w)
            r0, b0, dist, valid = _swa_geometry(i)
            j0 = jnp.maximum(i - 1, 0)
            kb = k_ref[pl.ds(b0, 2 * w), :]
            vb = v_ref[pl.ds(b0, 2 * w), :]
            dq_tiles = []
            for grp in range(SWA_KV_HEADS):
                heads = [SWA_GROUP * grp + a for a in range(SWA_GROUP)]
                qs32 = _swa_stack(q_ref, rs, grp)
                dos32 = _swa_stack(do_ref, rs, grp)
                delta = jnp.sum(dos32 * _swa_stack(o_ref, rs, grp), axis=-1, keepdims=True)
                lse = jnp.concatenate([lse_ref[h, rs, :] for h in heads], axis=0)
                sink = _swa_head_column([sink_ref[0, h] for h in heads])
                p = jnp.exp(_swa_logits(qs32.astype(BF16), kb, dist, valid, grp) - lse)
                dp = lax.dot_general(dos32.astype(BF16), vb, _NT, preferred_element_type=F32)
                ds = p * (dp - delta)
                dsb = ds.astype(BF16)
                d_sink = jnp.exp(sink - lse) * delta
                for a, h in enumerate(heads):
                    dsink_ref[h:h + 1, :] += jnp.broadcast_to(-jnp.sum(d_sink[a * w:(a + 1) * w]), (1, LANES))
                dvt = jnp.dot(dos32.T.astype(BF16), p.astype(BF16), preferred_element_type=F32)
                dkt = jnp.dot(qs32.T.astype(BF16), dsb, preferred_element_type=F32) * SWA_SCALE
                dvt_ref[j0] += dvt[:, :w]
                dvt_ref[j0 + 1] += dvt[:, w:]
                dkt_ref[j0] += dkt[:, :w]
                dkt_ref[j0 + 1] += dkt[:, w:]
                dq_tiles += _swa_unstack(jnp.dot(dsb, kb, preferred_element_type=F32) * SWA_SCALE, grp)
            dq_ref[rs, :] = _swa_merge_heads(dq_tiles)

    return pl.pallas_call(
        body, name=name, grid=(s // rows,),
        in_specs=[pl.BlockSpec(memory_space=pltpu.SMEM),
                  pl.BlockSpec((rows, qcols), lambda g: (g, 0)),
                  pl.BlockSpec((s, LANES), lambda g: (0, 4)), pl.BlockSpec((s, LANES), lambda g: (0, 5)),
                  pl.BlockSpec((rows, qcols), lambda g: (g, 0)), pl.BlockSpec((rows, qcols), lambda g: (g, 0)),
                  pl.BlockSpec((SWA_HEADS, rows, 1), lambda g: (0, g, 0))],
        out_specs=[pl.BlockSpec((rows, qcols), lambda g: (g, 0)),
                   pl.BlockSpec((nblk, LANES, w), lambda g: (0, 0, 0)),
                   pl.BlockSpec((nblk, LANES, w), lambda g: (0, 0, 0)),
                   pl.BlockSpec((SWA_HEADS, LANES), lambda g: (0, 0))],
        out_shape=[jax.ShapeDtypeStruct((s, qcols), F32),
                   jax.ShapeDtypeStruct((nblk, LANES, w), F32), jax.ShapeDtypeStruct((nblk, LANES, w), F32),
                   jax.ShapeDtypeStruct((SWA_HEADS, LANES), F32)],
        compiler_params=_params(("arbitrary",)),
    )(sinks, z0b, z0b, z0b, do, o, lse)


CUM_T = 256


def _split3(x):
    hi = x.astype(BF16)
    r1 = x - hi.astype(F32)
    mid = r1.astype(BF16)
    lo = (r1 - mid.astype(F32)).astype(BF16)
    return hi, mid, lo


def _tri_dot(tri, x):
    hi, mid, lo = _split3(x)
    out = jnp.dot(tri, hi, preferred_element_type=F32)
    out = out + jnp.dot(tri, mid, preferred_element_type=F32)
    return out + jnp.dot(tri, lo, preferred_element_type=F32)


def _logf_fwd(zf, bf, *, name):
    s = zf.shape[0]
    t = CUM_T
    nb = s // t

    def body(z_ref, b_ref, c_ref, carry_ref):
        i = pl.program_id(0)

        @pl.when(i == 0)
        def _():
            carry_ref[...] = jnp.zeros_like(carry_ref)

        x = z_ref[...] + b_ref[...]
        lf = jnp.minimum(x, 0.0) - jnp.log(1.0 + jnp.exp(-jnp.abs(x)))
        row = lax.broadcasted_iota(jnp.int32, (t, t), 0)
        col = lax.broadcasted_iota(jnp.int32, (t, t), 1)
        tri = jnp.where(col <= row, 1.0, 0.0).astype(BF16)
        c = _tri_dot(tri, lf) + carry_ref[...]
        c_ref[...] = c
        carry_ref[...] = c[t - 1:t, :]

    return pl.pallas_call(
        body, name=name, grid=(nb,),
        in_specs=[pl.BlockSpec((t, LANES), lambda i: (i, 0)), pl.BlockSpec((1, LANES), lambda i: (0, 0))],
        out_specs=pl.BlockSpec((t, LANES), lambda i: (i, 0)),
        out_shape=jax.ShapeDtypeStruct((s, LANES), F32),
        scratch_shapes=[pltpu.VMEM((1, LANES), F32)],
        compiler_params=_params(("arbitrary",)),
    )(zf, bf)


def _logf_bwd(dc, zf, bf, *, name):
    s = zf.shape[0]
    t = CUM_T
    nb = s // t

    def body(dc_ref, z_ref, b_ref, dz_ref, db_ref, carry_ref):
        i = pl.program_id(0)

        @pl.when(i == 0)
        def _():
            carry_ref[...] = jnp.zeros_like(carry_ref)
            db_ref[...] = jnp.zeros_like(db_ref)

        row = lax.broadcasted_iota(jnp.int32, (t, t), 0)
        col = lax.broadcasted_iota(jnp.int32, (t, t), 1)
        tri = jnp.where(col >= row, 1.0, 0.0).astype(BF16)
        dlf = _tri_dot(tri, dc_ref[...]) + carry_ref[...]
        carry_ref[...] = dlf[0:1, :]
        x = z_ref[...] + b_ref[...]
        dz = dlf * _sigmoid(-x)
        dz_ref[...] = dz.astype(BF16)
        db_ref[...] += jnp.sum(dz, axis=0, keepdims=True)

    return pl.pallas_call(
        body, name=name, grid=(nb,),
        in_specs=[pl.BlockSpec((t, LANES), lambda i: (nb - 1 - i, 0)), pl.BlockSpec((t, LANES), lambda i: (nb - 1 - i, 0)),
                  pl.BlockSpec((1, LANES), lambda i: (0, 0))],
        out_specs=[pl.BlockSpec((t, LANES), lambda i: (nb - 1 - i, 0)), pl.BlockSpec((1, LANES), lambda i: (0, 0))],
        out_shape=[jax.ShapeDtypeStruct((s, LANES), BF16), jax.ShapeDtypeStruct((1, LANES), F32)],
        scratch_shapes=[pltpu.VMEM((1, LANES), F32)],
        compiler_params=_params(("arbitrary",)),
    )(dc, zf, bf)


def _sum_pieces(p_ref):
    g = p_ref[0].astype(F32)
    for k in range(1, N_DEV):
        g = g + p_ref[k].astype(F32)
    return g


def _adam_update(g, w, m, v):
    bc1 = 1.0 - ADAM_B1 ** ADAM_STEP
    bc2 = 1.0 - ADAM_B2 ** ADAM_STEP
    nm = ADAM_B1 * m + (1.0 - ADAM_B1) * g
    nv = ADAM_B2 * v + (1.0 - ADAM_B2) * (g * g)
    m_hat = nm / bc1
    v_hat = nv / bc2
    return -ADAM_LR * (m_hat / (jnp.sqrt(v_hat) + ADAM_EPS) + ADAM_WD * w), nm, nv


def _adamw(pieces, w, m, v, *, name):
    rows, cols = w.shape
    tr = _tile(rows, (RB1, RB0, SMALL_ROWS))

    def body(p_ref, w_ref, m_ref, v_ref, g_ref, d_ref, nm_ref, nv_ref):
        g = _sum_pieces(p_ref)
        g_ref[...] = g
        d_ref[...], nm_ref[...], nv_ref[...] = _adam_update(g, w_ref[...], m_ref[...], v_ref[...])

    spec = pl.BlockSpec((tr, cols), lambda i: (i, 0))
    shape = jax.ShapeDtypeStruct((rows, cols), F32)
    return pl.pallas_call(
        body, name=name, grid=(rows // tr,),
        in_specs=[pl.BlockSpec((N_DEV, tr, cols), lambda i: (0, i, 0)), spec, spec, spec],
        out_specs=[spec, spec, spec, spec], out_shape=[shape, shape, shape, shape],
        compiler_params=_params(("parallel",)),
    )(pieces, w, m, v)


def _sum8(pieces, rows, *, name):
    cols = pieces.shape[2]
    tr = _tile(rows, (176, 96))

    def body(p_ref, g_ref):
        g_ref[...] = _sum_pieces(p_ref)

    return pl.pallas_call(
        body, name=name, grid=(rows // tr,),
        in_specs=[pl.BlockSpec((N_DEV, tr, cols), lambda i: (0, i, 0))],
        out_specs=pl.BlockSpec((tr, cols), lambda i: (i, 0)),
        out_shape=jax.ShapeDtypeStruct((rows, cols), F32),
        compiler_params=_params(("parallel",)),
    )(pieces)


def _adamw_columns(g, w, m, v, *, name):
    n, _, k = w.shape
    tr = n // 2

    def body(g_ref, w_ref, m_ref, v_ref, d_ref, nm_ref, nv_ref):
        d_ref[...], nm_ref[...], nv_ref[...] = _adam_update(g_ref[...], w_ref[...], m_ref[...], v_ref[...])

    spec = pl.BlockSpec((tr, 1, k), lambda i: (i, 0, 0))
    shape = jax.ShapeDtypeStruct((n, 1, k), F32)
    return pl.pallas_call(
        body, name=name, grid=(n // tr,), in_specs=[spec, spec, spec, spec],
        out_specs=[spec, spec, spec], out_shape=[shape, shape, shape],
        compiler_params=_params(("parallel",)),
    )(g, w, m, v)


MESH = pl.DeviceIdType.MESH
ANY = pl.BlockSpec(memory_space=pl.ANY)


def _all_gather(shard, *, name):
    rows, lanes = shard.shape

    def body(x_ref, out_ref, send_sems, recv_sems, local_sem):
        x, y, c = lax.axis_index("x"), lax.axis_index("y"), lax.axis_index("c")
        me, sibling = (x, y, c), (x, y, 1 - c)
        chips = [(1 - x, y), (x, 1 - y), (1 - x, 1 - y)]

        def block(px, py, pc):
            return out_ref.at[4 * px + 2 * py + pc]

        def copy(k, blk, to, src=None):
            return pltpu.make_async_remote_copy(
                src_ref=block(*blk) if src is None else src, dst_ref=block(*blk),
                send_sem=send_sems.at[k], recv_sem=recv_sems.at[k], device_id=to, device_id_type=MESH)

        mine = pltpu.make_async_copy(x_ref, block(*me), local_sem)
        mine.start()
        first = [copy(0, me, sibling, src=x_ref)]
        first += [copy(1 + j, me, (*chip, c), src=x_ref) for j, chip in enumerate(chips)]
        for cp in first:
            cp.start()
        passed = [copy(4 + j, (*chip, c), sibling) for j, chip in enumerate(chips)]
        for j, chip in enumerate(chips):
            copy(1 + j, (*chip, c), me).wait_recv()
            passed[j].start()
        copy(0, sibling, me).wait_recv()
        for j, chip in enumerate(chips):
            copy(4 + j, (*chip, 1 - c), me).wait_recv()
        for cp in first + passed:
            cp.wait_send()
        mine.wait()

    return pl.pallas_call(
        body, name=name, out_shape=jax.ShapeDtypeStruct((N_DEV, rows, lanes), shard.dtype),
        in_specs=[ANY], out_specs=ANY,
        scratch_shapes=[pltpu.SemaphoreType.DMA((7,)), pltpu.SemaphoreType.DMA((7,)), pltpu.SemaphoreType.DMA(())],
    )(shard)


def _peer_copies(kind, src_ref, out_ref, send_sems, recv_sems, local_sem):
    x, y, c = lax.axis_index("x"), lax.axis_index("y"), lax.axis_index("c")
    me = 4 * x + 2 * y + c

    def src(idx):
        return src_ref.at[idx] if kind == "exchange" else src_ref

    mine = None if local_sem is None else pltpu.make_async_copy(src(me), out_ref.at[me], local_sem)
    copies = []
    for r in (2, 4, 6) if kind == "across" else range(1, N_DEV):
        px = 1 - x if r & 4 else x
        py = 1 - y if r & 2 else y
        pc = 1 - c if r & 1 else c
        copies.append(pltpu.make_async_remote_copy(
            src_ref=src(4 * px + 2 * py + pc), dst_ref=out_ref.at[me],
            send_sem=send_sems.at[r - 1], recv_sem=recv_sems.at[r - 1],
            device_id=(px, py, pc), device_id_type=MESH))
    return mine, copies


def _to_other_core(shard, land, *, name):
    def body(src_ref, land_ref, out_ref, send_sems, recv_sems):
        x, y, c = lax.axis_index("x"), lax.axis_index("y"), lax.axis_index("c")
        copies = []
        for k, r in enumerate((0, 2, 4, 6)):
            slot = 4 * (1 - x if r & 4 else x) + 2 * (1 - y if r & 2 else y) + c
            copies.append(pltpu.make_async_remote_copy(
                src_ref=src_ref if r == 0 else land_ref.at[slot], dst_ref=out_ref.at[slot],
                send_sem=send_sems.at[k], recv_sem=recv_sems.at[k], device_id=(x, y, 1 - c), device_id_type=MESH))
        for cp in copies:
            cp.start()
        for cp in copies:
            cp.wait()

    return pl.pallas_call(
        body, name=name, out_shape=jax.ShapeDtypeStruct(land.shape, land.dtype), in_specs=[ANY, ANY], out_specs=ANY,
        input_output_aliases={1: 0}, scratch_shapes=[pltpu.SemaphoreType.DMA((4,)), pltpu.SemaphoreType.DMA((4,))],
    )(shard, land)


PEER_SEMS = [pltpu.SemaphoreType.DMA((7,)), pltpu.SemaphoreType.DMA((7,)), pltpu.SemaphoreType.DMA(())]


HBM = pl.BlockSpec(memory_space=pltpu.HBM)
SEMAPHORES = pl.BlockSpec(memory_space=pltpu.SEMAPHORE)


def _peer_start(kind, arr, *, name):
    land = lax.empty((N_DEV,) + arr.shape[-2:], arr.dtype)

    def body(src_ref, land_ref, send_sems, recv_sems, src_thru, land_thru, token):
        _, copies = _peer_copies(kind, src_ref, land_ref, send_sems, recv_sems, None)
        for cp in copies:
            cp.start()
        token[...] = jnp.zeros_like(token)

    return pl.pallas_call(
        body, name=name,
        out_shape=(pltpu.SemaphoreType.DMA((N_DEV - 1,)), pltpu.SemaphoreType.DMA((N_DEV - 1,)),
                   pltpu.HBM(arr.shape, arr.dtype), pltpu.HBM(land.shape, land.dtype), jax.ShapeDtypeStruct((8, LANES), F32)),
        in_specs=(HBM, HBM), out_specs=(SEMAPHORES, SEMAPHORES, HBM, HBM, pl.BlockSpec(memory_space=pltpu.VMEM)),
        input_output_aliases={0: 2, 1: 3},
        compiler_params=pltpu.CompilerParams(has_side_effects=pltpu.SideEffectType.DATAFLOW_SIDE_EFFECTING),
    )(pltpu.with_memory_space_constraint(arr, pltpu.HBM), pltpu.with_memory_space_constraint(land, pltpu.HBM))


def _peer_wait(kind, send_sems, recv_sems, src_thru, land_thru, after, *, name):
    def body(src_ref, land_ref, send_sems, recv_sems, *_):
        _, copies = _peer_copies(kind, src_ref, land_ref, send_sems, recv_sems, None)
        for cp in copies:
            cp.wait_send()
            cp.wait_recv()

    return pl.pallas_call(
        body, name=name,
        out_shape=(pltpu.HBM(src_thru.shape, src_thru.dtype), pltpu.HBM(land_thru.shape, land_thru.dtype)),
        in_specs=(HBM, HBM, SEMAPHORES, SEMAPHORES) + (ANY,) * len(after), out_specs=(HBM, HBM),
        input_output_aliases={0: 0, 1: 1},
        compiler_params=pltpu.CompilerParams(has_side_effects=pltpu.SideEffectType.DATAFLOW_SIDE_EFFECTING),
    )(src_thru, land_thru, send_sems, recv_sems, *after)


def _add_rider(rider, in_specs, args, out_specs, out_shape):
    if rider is None:
        return []
    _, arr = rider
    in_specs.append(ANY)
    args.append(arr)
    out_specs.append(ANY)
    out_shape.append(jax.ShapeDtypeStruct((N_DEV,) + arr.shape[-2:], arr.dtype))
    return list(PEER_SEMS)


def _split_rider(refs, rider, n_in, n_out):
    if rider is None:
        return refs, None
    refs = list(refs)
    rin = refs.pop(n_in)
    rout = refs.pop(n_in + n_out)
    return refs[:-3], (rin, rout, *refs[-3:])


def _ride_start(rider, ride_refs, first):
    if rider is None:
        return

    @pl.when(first)
    def _():
        mine, copies = _peer_copies(rider[0], *ride_refs)
        mine.start()
        for cp in copies:
            cp.start()


def _ride_wait(rider, ride_refs, last):
    if rider is None:
        return

    @pl.when(last)
    def _():
        mine, copies = _peer_copies(rider[0], *ride_refs)
        for cp in copies:
            cp.wait()
        mine.wait()


def _gathered_cols(blocks, kdim):
    n = blocks.shape[1] * WIDE // kdim
    return blocks.reshape(N_DEV, kdim, n).transpose(1, 0, 2).reshape(kdim, N_DEV * n)


def _scatter_cols(dw):
    kdim, n8 = dw.shape
    n = n8 // N_DEV
    return dw.reshape(kdim, N_DEV, n).transpose(1, 0, 2).reshape(N_DEV, kdim * n // WIDE, WIDE)


def _pad_rows(a, rows):
    pad = [(0, 0)] * a.ndim
    pad[-2] = (0, rows - a.shape[-2])
    return jnp.pad(a, pad)


def _layer0_in_weight_t(wt):
    cq, ckv, kpe = wt[0:256], wt[256:384], wt[384:416]
    q_s, k_s, v_s, gate = wt[416:928], wt[928:1056], wt[1056:1184], wt[1184:2208]
    z = jnp.zeros((64, wt.shape[1]), wt.dtype)
    return jnp.concatenate([gate, cq, ckv, z, kpe, z[:32], q_s, k_s, v_s], axis=0)


def _layer0_in_grad_t(dwt):
    gate, cq, ckv, kpe = dwt[0:1024], dwt[1024:1280], dwt[1280:1408], dwt[1472:1504]
    q_s, k_s, v_s = dwt[1536:2048], dwt[2048:2176], dwt[2176:2304]
    return jnp.concatenate([cq, ckv, kpe, q_s, k_s, v_s, gate], axis=0)


L0_BLOCKS = ((256, 1024), (128, 1280), (32, 1472), (512, 1536), (128, 2048), (128, 2176), (1024, 0))


def _layer0_in_unpack(gath, *, name):
    total = (Z0A_UNITS + Z0B_UNITS) * LANES

    def body(g_ref, w_ref):
        w_ref[1408:1472, :] = jnp.zeros((64, WIDE), w_ref.dtype)
        w_ref[1504:1536, :] = jnp.zeros((32, WIDE), w_ref.dtype)
        for p in range(N_DEV):
            lo, hi, at = p * N_E_IN, (p + 1) * N_E_IN, 0
            for rows, first in L0_BLOCKS:
                start, stop = max(lo, at), min(hi, at + rows)
                if start < stop:
                    w_ref[first + start - at:first + stop - at, :] = g_ref[p, start - lo:stop - lo, :]
                at += rows

    return pl.pallas_call(
        body, name=name, grid=(1,), in_specs=[_resident((N_DEV, RA0, WIDE), lambda i: (0, 0, 0))],
        out_specs=_resident((total, WIDE), lambda i: (0, 0)), out_shape=jax.ShapeDtypeStruct((total, WIDE), gath.dtype),
        compiler_params=_params(("arbitrary",)),
    )(gath)


def _early_grads_pack(d_w0t, d_q, d_kv, *, name):
    def body(w_ref, q_ref, kv_ref, out_ref):
        for p in range(N_DEV):
            lo, hi, at = p * N_E_IN, (p + 1) * N_E_IN, 0
            for rows, first in L0_BLOCKS:
                start, stop = max(lo, at), min(hi, at + rows)
                if start < stop:
                    out_ref[p, start - lo:stop - lo, :] = w_ref[first + start - at:first + stop - at, :]
                at += rows
            out_ref[p, N_E_IN:RA0, :] = jnp.zeros((RA0 - N_E_IN, WIDE), out_ref.dtype)
            out_ref[p, RA0:RA0 + 32, :] = q_ref[p]
            out_ref[p, RA0 + 32:, :] = kv_ref[p]

    arrays = (d_w0t, d_q, d_kv)
    return pl.pallas_call(
        body, name=name, grid=(1,), in_specs=[_resident(a.shape, lambda i, n=a.ndim: (0,) * n) for a in arrays],
        out_specs=_resident((N_DEV, RA0 + RB0, WIDE), lambda i: (0, 0, 0)),
        out_shape=jax.ShapeDtypeStruct((N_DEV, RA0 + RB0, WIDE), BF16), compiler_params=_params(("arbitrary",)),
    )(*arrays)


def _layer1_in_weight_t(wt):
    main = jnp.concatenate([wt[:3 * D_MODEL], wt[3 * D_MODEL + FOX_HEADS:]], axis=0)
    return main, _pad_rows(wt[3 * D_MODEL:3 * D_MODEL + FOX_HEADS], LANES)


def _layer1_in_unpack(gath, *, name):
    n_main = 3 * D_MODEL

    def body(g_ref, w_ref, f_ref, o1_ref, o0_ref):
        f_ref[...] = jnp.zeros_like(f_ref)
        for p in range(N_DEV):
            o1_ref[128 * p:128 * p + 128, :] = g_ref[p, RA1:RA1 + 128, :]
            o0_ref[128 * p:128 * p + 128, :] = g_ref[p, RA1 + 128:RA1 + 256, :]
            lo, hi = p * N_O_IN, (p + 1) * N_O_IN
            for ref, first, start, stop in ((w_ref, 0, lo, min(hi, n_main)),
                                            (f_ref, -n_main, max(lo, n_main), min(hi, n_main + FOX_HEADS)),
                                            (w_ref, -FOX_HEADS, max(lo, n_main + FOX_HEADS), hi)):
                if start < stop:
                    ref[start + first:stop + first, :] = g_ref[p, start - lo:stop - lo, :]

    return pl.pallas_call(
        body, name=name, grid=(1,), in_specs=[_resident((N_DEV, RA1 + 256, WIDE), lambda i: (0, 0, 0))],
        out_specs=[_resident((rows, WIDE), lambda i: (0, 0)) for rows in (n_main + D_MODEL, LANES, D_MODEL, D_MODEL)],
        out_shape=[jax.ShapeDtypeStruct((rows, WIDE), gath.dtype) for rows in (n_main + D_MODEL, LANES, D_MODEL, D_MODEL)],
        compiler_params=_params(("arbitrary",)),
    )(gath)


def _late_grads_pack(d_qkv, d_wft, d_gate, d_wo1, d_wo0, d_o_g, *, name):
    n_main = 3 * D_MODEL
    arrays = (d_qkv, d_wft, d_gate, d_wo1, d_wo0, d_o_g)

    def body(q_ref, f_ref, g_ref, o1_ref, o0_ref, og_ref, out_ref):
        for p in range(N_DEV):
            lo, hi = p * N_O_IN, (p + 1) * N_O_IN
            for ref, first, start, stop in ((q_ref, 0, lo, min(hi, n_main)),
                                            (f_ref, -n_main, max(lo, n_main), min(hi, n_main + FOX_HEADS)),
                                            (g_ref, -n_main - FOX_HEADS, max(lo, n_main + FOX_HEADS), hi)):
                if start < stop:
                    out_ref[p, start - lo:stop - lo, :] = ref[start + first:stop + first, :]
            out_ref[p, N_O_IN:RA1, :] = jnp.zeros((RA1 - N_O_IN, WIDE), out_ref.dtype)
            out_ref[p, RA1:RA1 + 128, :] = o1_ref[128 * p:128 * p + 128, :]
            out_ref[p, RA1 + 128:RA1 + 256, :] = o0_ref[128 * p:128 * p + 128, :]
            out_ref[p, RA1 + 256:, :] = og_ref[p]

    return pl.pallas_call(
        body, name=name, grid=(1,), in_specs=[_resident(a.shape, lambda i, n=a.ndim: (0,) * n) for a in arrays],
        out_specs=_resident((N_DEV, RA1 + RB1, WIDE), lambda i: (0, 0, 0)),
        out_shape=jax.ShapeDtypeStruct((N_DEV, RA1 + RB1, WIDE), BF16), compiler_params=_params(("arbitrary",)),
    )(*arrays)


def _q_up_weight(w):
    return jnp.pad(w.reshape(MLA_Q_RANK, MLA_HEADS, 96), ((0, 0), (0, 0), (0, 32))).reshape(MLA_Q_RANK, MLA_HEADS * LANES)


def _q_up_grad(dwp):
    return dwp.reshape(MLA_Q_RANK, MLA_HEADS, LANES)[:, :, :96].reshape(MLA_Q_RANK, MLA_HEADS * 96)


def _kv_up_weight(w):
    w4 = w.reshape(MLA_KV_RANK, MLA_HEADS, 2, 64)
    kp = jnp.pad(w4[:, :, 0, :], ((0, 0), (0, 0), (0, 64))).reshape(MLA_KV_RANK, MLA_HEADS * LANES)
    vp = w4[:, :, 1, :].reshape(MLA_KV_RANK, MLA_HEADS * 64)
    return jnp.concatenate([kp, vp], axis=1)


def _kv_up_grad(dwp):
    dk = dwp[:, :MLA_HEADS * LANES].reshape(MLA_KV_RANK, MLA_HEADS, LANES)[:, :, :64]
    dv = dwp[:, MLA_HEADS * LANES:].reshape(MLA_KV_RANK, MLA_HEADS, 64)
    return jnp.stack([dk, dv], axis=2).reshape(MLA_KV_RANK, MLA_HEADS * LANES)


def _pad_lanes(a):
    return jnp.pad(a, ((0, 0), (0, LANES - a.shape[1])))


def _small_pack(g_in, g_final, g_q_a, g_kv_a, sinks, b_f, loss):
    rows = [g_in.reshape(8, LANES), g_final.reshape(8, LANES), g_q_a.reshape(2, LANES), g_kv_a.reshape(1, LANES),
            _pad_lanes(sinks.reshape(1, -1)), _pad_lanes(b_f.reshape(1, -1)), _pad_lanes(loss.reshape(1, 1)),
            jnp.zeros((2, LANES), F32)]
    return jnp.concatenate(rows, axis=0)


def _small_unpack(a):
    return (a[0:8].reshape(1, D_MODEL), a[8:16].reshape(D_MODEL), a[16:18].reshape(1, MLA_Q_RANK),
            a[18:19].reshape(1, MLA_KV_RANK), a[19:20, :SWA_HEADS], a[20:21, :FOX_HEADS], a[21, 0])


def _local_step(x, positions, target, e_g_in, early, e_g_q_a, e_g_kv_a, e_sinks,
                late, o_b_f, g_final, scatter1=None, scatter0=None):
    s = x.shape[0]
    mla_scale = (MLA_NOPE + MLA_ROPE) ** -0.5
    fox_scale = FOX_DIM ** -0.5
    n0a = Z0A_UNITS * LANES

    inv_freq = 1.0 / (ROPE_THETA ** (jnp.arange(0, MLA_ROPE, 2, dtype=F32) / MLA_ROPE))
    ang = positions.astype(F32)[:, None] * inv_freq
    cos, sin = jnp.cos(ang), jnp.sin(ang)
    ones, zeros = jnp.ones((s, 64), F32), jnp.zeros((s, 64), F32)
    cos_t = jnp.concatenate([ones, cos, cos, ones[:, :32]], axis=1)
    sin_t = jnp.concatenate([zeros, -sin, sin, zeros[:, :32]], axis=1)
    cos_t, sin_t = lax.optimization_barrier((cos_t, sin_t))

    if len(early) == 3:
        h0 = _rmsnorm_fwd(x, e_g_in, width=D_MODEL, col_blk=0, name="l0_norm")
        w0t, wq, wkv = early
    else:
        pending, token, unpack, prep = early
        h0 = _rmsnorm_fwd(x, e_g_in, width=D_MODEL, col_blk=0, name="l0_norm", after=[token])
        sent, across = _peer_wait("across", *pending, after=[h0] + prep, name="weights0_wait")
        w0t, wq, wkv = unpack(sent, _to_other_core(sent, across, name="weights0_over"))
    z0a, z0b = _matmul_rows([(h0, w0t, True)], [], [], lambda r: (r[:, :n0a], r[:, n0a:]),
                            [("rows", n0a, F32), ("rows", Z0B_UNITS * LANES, BF16)], name="l0_in")
    cqn = _rmsnorm_fwd(z0a, e_g_q_a, width=MLA_Q_RANK, col_blk=4, name="l0_q_norm")
    ckvn = _rmsnorm_fwd(z0a, e_g_kv_a, width=MLA_KV_RANK, col_blk=10, name="l0_kv_norm")
    rope_rows = [(cos_t, LANES, 0), (sin_t, LANES, 0)]
    qm, = _matmul_rows([(cqn, wq, False)], rope_rows, [], _rope_q_epilogue, [("rows", MLA_HEADS * LANES, BF16)],
                       name="l0_q_up")
    kvm, = _matmul_rows([(ckvn, wkv, False)], [(z0a, LANES, 11)] + rope_rows, [], _rope_k_epilogue,
                        [("rows", MLA_HEADS * (LANES + MLA_V), BF16)], name="l0_kv_up")
    gathers = len(late) == 2
    res = _flash_fwd(qm, kvm, kvm, None, n_pairs=MLA_HEADS // 2, hw=LANES, q_off=0, k_off=0, v_off=MLA_HEADS,
                     scale=mla_scale, name="l0_mla_fwd", rider=("gather", late[0]) if gathers else None)
    o_mla, lse_mla = res[0], res[1]
    wo0, o_g_in, w1t, wft, wo1 = late[1](res[2]) if gathers else late
    o_swa, lse_swa = _swa_fwd(z0b, e_sinks, name="l0_swa_fwd")
    half = D_MODEL // 2

    x1, h1, og0 = _matmul_rows(
        [(None, wo0, False)], [(o_mla, half, 0), (o_swa, half, 0), (z0a, D_MODEL, 0), (x, D_MODEL, 0)], [o_g_in],
        lambda r, om, osw, gt, xt, g, made: (*_residual_norm_epilogue(r, xt, g), made),
        [("rows", D_MODEL, F32), ("rows", D_MODEL, BF16), ("rows", D_MODEL, BF16)], name="l0_out",
        prologue=lambda om, osw, gt, xt, g: _gated([om, osw], gt))
    z1, gate1, zf = _matmul_rows(
        [(None, w1t, True), (None, wft, True)], [(h1, D_MODEL, 0)], [],
        lambda r, h, made: (r[0][:, :3 * D_MODEL], r[0][:, 3 * D_MODEL:], r[1]),
        [("rows", 3 * D_MODEL, BF16), ("rows", D_MODEL, F32), ("rows", LANES, F32)], name="l1_in",
        prologue=lambda h: h, separate=True)
    bf = _pad_lanes(o_b_f)
    log_cum = _logf_fwd(zf, bf, name="l1_logf")
    bias2 = (-LOG2E * log_cum[:, :FOX_HEADS]).T
    t_bwd = min(ATT_T, s)
    bias = bias2.reshape(FOX_HEADS // 2, 2, s // t_bwd, 1, t_bwd)
    t_fwd = _fwd_tile(s)
    o_fox, lse_fox = _flash_fwd(z1, z1, z1, bias2.reshape(FOX_HEADS // 2, 2, s // t_fwd, 1, t_fwd),
                                n_pairs=FOX_HEADS // 2, hw=64, q_off=0, k_off=8, v_off=16, scale=fox_scale,
                                name="l1_fox_fwd")

    dx2, loss_part, d_g_final, og1, dx2_bf = _matmul_rows(
        [(None, wo1, False)], [(o_fox, D_MODEL, 0), (gate1, D_MODEL, 0), (x1, D_MODEL, 0), (target, D_MODEL, 0)],
        [g_final.reshape(1, D_MODEL)],
        lambda r, o, gt, xt, tg, g, made: _and_first(_loss_epilogue(r, xt, tg, g), made),
        [("rows", D_MODEL, F32), ("sum", (8, LANES)), ("sum", (1, D_MODEL)), ("rows", D_MODEL, BF16),
         ("rows", D_MODEL, BF16)], name="l1_out_loss", prologue=lambda o, gt, xt, tg, g: _gated([o], gt))

    d_wo1 = _matmul(og1, dx2_bf, ta=True, out_dtype=BF16, name="l1_out_dw")
    do_fox, d_gate1 = _matmul_rows([(dx2_bf, wo1, True)], [(o_fox, D_MODEL, 0), (gate1, D_MODEL, 0)], [],
                                   _gate_bwd_epilogue([D_MODEL]), [("rows", D_MODEL, F32), ("rows", D_MODEL, BF16)],
                                   name="l1_out_dx")
    dqkv1, dbias, drow = _flash_bwd(z1, z1, z1, do_fox, o_fox, lse_fox, bias, n_pairs=FOX_HEADS // 2, hw=64, q_off=0,
                                    k_off=8, v_off=16, scale=fox_scale, qk_dtype=BF16, stacked=True, name="l1_fox_bwd")
    d_log_cum = (drow.reshape(FOX_HEADS, s) - dbias.reshape(FOX_HEADS, s)).T
    d_log_cum = jnp.pad(d_log_cum, ((0, 0), (0, LANES - FOX_HEADS)))
    d_zf, d_bf = _logf_bwd(d_log_cum, zf, bf, name="l1_logf_bwd")
    d_w1t = (_matmul(dqkv1, h1, ta=True, out_dtype=BF16, name="l1_in_dw_qkv"),
             _matmul(d_gate1, h1, ta=True, out_dtype=BF16, name="l1_in_dw_gate"))
    d_wft = _matmul(d_zf, h1, ta=True, out_dtype=BF16, name="l1_in_f_dw")
    dx1, d_o_g_in, dx1_bf = _matmul_rows([(dqkv1, w1t, False, c * D_MODEL, c) for c in range(3)]
                                         + [(d_gate1, w1t, False, 3 * D_MODEL), (d_zf, wft, False)],
                                         [(x1, D_MODEL, 0), (dx2, D_MODEL, 0)], [o_g_in],
                                         lambda *a: _and_first(_rms_bwd_epilogue(*a)),
                                         [("rows", D_MODEL, F32), ("sum", (1, D_MODEL)), ("rows", D_MODEL, BF16)],
                                         name="l1_in_dx")

    d_wo0 = _matmul(og0, dx1_bf, ta=True, out_dtype=BF16, name="l0_out_dw")
    do_mla, do_swa, d_gate0 = _matmul_rows(
        [(dx1_bf, wo0, True)], [(o_mla, half, 0), (o_swa, half, 0), (z0a, D_MODEL, 0)], [], _gate_bwd_epilogue([half, half]),
        [("rows", half, F32), ("rows", half, F32), ("rows", D_MODEL, BF16)], name="l0_out_dx")
    dq_s, dkt_s, dvt_s, d_sinks = _swa_bwd(z0b, e_sinks, do_swa, o_swa, lse_swa, name="l0_swa_bwd")
    dk_s = dkt_s.transpose(0, 2, 1).reshape(s, LANES)
    dv_s = dvt_s.transpose(0, 2, 1).reshape(s, LANES)
    rider = None
    if scatter1 is not None:
        rider = ("exchange", scatter1(dict(w1t=d_w1t, wft=d_wft, wo1=d_wo1, o_g_in=d_o_g_in, wo0=d_wo0)))
    res = _flash_bwd(qm, kvm, kvm, do_mla, o_mla, lse_mla, None, n_pairs=MLA_HEADS // 2, hw=LANES, q_off=0, k_off=0,
                     v_off=MLA_HEADS, scale=mla_scale, qk_dtype=F32, name="l0_mla_bwd", rider=rider)
    dqm, dkm, dvm = res[0], res[1], res[2]
    recv1 = res[3] if rider is not None else None
    d_qp, d_kvp, d_kpe = _rope_bwd(dqm, dkm, dvm, cos_t, sin_t, name="l0_rope_bwd")
    d_wq = _matmul(cqn, d_qp, ta=True, out_dtype=BF16, name="l0_q_up_dw")
    d_cqn = _matmul(d_qp, wq, tb=True, name="l0_q_up_dx")
    d_wkv = _matmul(ckvn, d_kvp, ta=True, out_dtype=BF16, name="l0_kv_up_dw")
    d_ckvn = _matmul(d_kvp, wkv, tb=True, name="l0_kv_up_dx")
    d_cq, d_g_q_a = _rmsnorm_bwd(z0a, e_g_q_a, d_cqn, width=MLA_Q_RANK, col_blk=4, name="l0_q_norm_bwd")
    d_ckv, d_g_kv_a = _rmsnorm_bwd(z0a, e_g_kv_a, d_ckvn, width=MLA_KV_RANK, col_blk=10, name="l0_kv_norm_bwd")
    dz0 = jnp.concatenate([d_gate0, d_cq, d_ckv, d_kpe, dq_s.astype(BF16), dk_s.astype(BF16), dv_s.astype(BF16)], axis=1)
    d_w0t = _matmul(dz0, h0, ta=True, out_dtype=BF16, name="l0_in_dw")
    pending0, after_start = None, []
    if scatter0 is not None:
        *pending0, token = _peer_start("exchange", scatter0(dict(w0t=d_w0t, wq=d_wq, wkv=d_wkv)), name="grads0_start")
        after_start = [token]
    grad_x, d_e_g_in = _matmul_rows(
        [(dz0, w0t, False)], [(x, D_MODEL, 0), (dx1, D_MODEL, 0)], [e_g_in] + after_start,
        lambda dy, xt, add, g, *_: _rms_bwd_epilogue(dy, xt, add, g),
        [("rows", D_MODEL, F32), ("sum", (1, D_MODEL))], name="l0_in_dx")

    return dict(pending0=pending0, recv1=recv1, loss=loss_part[0, 0], grad_x=grad_x, e_g_in=d_e_g_in, w0t=d_w0t, e_g_q_a=d_g_q_a, wq=d_wq,
                e_g_kv_a=d_g_kv_a, wkv=d_wkv, e_sinks=d_sinks[:, 0].reshape(1, SWA_HEADS), wo0=d_wo0,
                o_g_in=d_o_g_in, w1t=d_w1t, wft=d_wft, o_b_f=d_bf[:, :FOX_HEADS], wo1=d_wo1, g_final=d_g_final.reshape(D_MODEL))


def _wide(a, rows):
    flat = a.reshape(-1)
    return jnp.pad(flat, (0, rows * WIDE - flat.shape[0])).reshape(rows, WIDE)


def _rows_b0(w_q, w_kv):
    return jnp.concatenate([_wide(w_q, 32), _wide(w_kv, 16)], axis=0)


def _unflat_b0(f):
    return f[0:24].reshape(1, MLA_Q_RANK, 96), f[32:48].reshape(1, MLA_KV_RANK, 128)


def _rows_b1(o_w_out, e_w_out, g_in):
    return jnp.concatenate([o_w_out, e_w_out, _wide(g_in, 16)], axis=0)


def _unflat_b1(f):
    return f[0:128][None], f[128:256][None], f[256:257, :LANES]


def kernel(x, positions, e_g_in, e_w_in, e_g_q_a, e_w_q_up, e_g_kv_a, e_w_kv_up, e_sinks, e_w_out, o_g_in, o_w_in, o_b_f, o_w_out, g_final, loss_target, m_e_g_in, m_e_w_in, m_e_g_q_a, m_e_w_q_up, m_e_g_kv_a, m_e_w_kv_up, m_e_sinks, m_e_w_out, m_o_g_in, m_o_w_in, m_o_b_f, m_o_w_out, m_g_final, v_e_g_in, v_e_w_in, v_e_g_q_a, v_e_w_q_up, v_e_g_kv_a, v_e_w_kv_up, v_e_sinks, v_e_w_out, v_o_g_in, v_o_w_in, v_o_b_f, v_o_w_out, v_g_final):
    def bf(a):
        return a.astype(BF16)

    me = 4 * lax.axis_index("x") + 2 * lax.axis_index("y") + lax.axis_index("c")
    shard0 = jnp.concatenate([_pad_rows(bf(e_w_in[0]).T, RA0), _rows_b0(bf(e_w_q_up[0]), bf(e_w_kv_up[0]))], axis=0)
    *pending_w0, token_w0 = _peer_start("across", shard0, name="weights0_start")

    def unpack0(sent, gath0):
        gath0 = lax.dynamic_update_slice_in_dim(gath0, sent[None], me, axis=0)
        w0t = _layer0_in_unpack(gath0, name="weights0_unpack")
        wq = _q_up_weight(_gathered_cols(gath0[:, RA0:RA0 + 24], MLA_Q_RANK))
        wkv = _kv_up_weight(_gathered_cols(gath0[:, RA0 + 32:RA0 + 48], MLA_KV_RANK))
        return w0t, wq, wkv

    rows_b0 = [_rows_b0(q[0], kv[0]) for q, kv in ((e_w_q_up, e_w_kv_up), (m_e_w_q_up, m_e_w_kv_up), (v_e_w_q_up, v_e_w_kv_up))]
    rows_b1 = [_rows_b1(o[0], e[0], g) for o, e, g in ((o_w_out, e_w_out, o_g_in), (m_o_w_out, m_e_w_out, m_o_g_in),
                                                       (v_o_w_out, v_e_w_out, v_o_g_in))]

    g_bits = lax.bitcast_convert_type(o_g_in.reshape(LANES), BF16)
    shard1 = jnp.concatenate([_pad_rows(bf(o_w_in[0]).T, RA1), _rows_b1(bf(o_w_out[0]), bf(e_w_out[0]), g_bits)], axis=0)

    def unpack1(gath1):
        w1t, wft, wo1, wo0 = _layer1_in_unpack(gath1, name="weights1_unpack")
        bits = gath1[:, RA1 + 256, :2 * LANES].reshape(N_DEV, LANES, 2)
        return wo0, lax.bitcast_convert_type(bits, F32).reshape(1, D_MODEL), w1t, wft, wo1

    def scatter1(g):
        d_o_g = jnp.pad(bf(g["o_g_in"]).reshape(N_DEV, 1, LANES), ((0, 0), (0, 15), (0, WIDE - LANES)))
        return _late_grads_pack(g["w1t"][0], g["wft"], g["w1t"][1], g["wo1"], g["wo0"], d_o_g, name="grads1_pack")

    def scatter0(g):
        return _early_grads_pack(g["w0t"], _pad_rows(_scatter_cols(_q_up_grad(g["wq"])), 32),
                                 _scatter_cols(_kv_up_grad(g["wkv"])), name="grads0_pack")

    gr = _local_step(x[0], positions[0], loss_target[0], e_g_in,
                     (pending_w0, token_w0, unpack0, [shard1] + rows_b0 + rows_b1), e_g_q_a, e_g_kv_a, e_sinks,
                     (shard1, unpack1), o_b_f, g_final, scatter1=scatter1, scatter0=scatter0)

    def in_projection(recv, ra, n, w, m, v, name):
        g = _sum8(recv, ra, name=name + "_grad_sum")[:n].reshape(n, 1, D_MODEL)
        w, m, v = [jnp.transpose(a, (2, 0, 1)) for a in (w, m, v)]
        return (g, *_adamw_columns(g, w, m, v, name=name + "_adamw"))

    o_in = in_projection(gr["recv1"], RA1, N_O_IN, o_w_in, m_o_w_in, v_o_w_in, "o_w_in")
    b1 = _adamw(gr["recv1"][:, RA1:], *rows_b1, name="adamw_late")

    small = _small_pack(gr["e_g_in"], gr["g_final"], gr["e_g_q_a"], gr["e_g_kv_a"], gr["e_sinks"], gr["o_b_f"], gr["loss"])
    small_all = _all_gather(small, name="small_all_gather")
    zero = jnp.zeros((), F32)
    w_small = _small_pack(e_g_in, g_final, e_g_q_a, e_g_kv_a, e_sinks, o_b_f, zero)
    m_small = _small_pack(m_e_g_in, m_g_final, m_e_g_q_a, m_e_g_kv_a, m_e_sinks, m_o_b_f, zero)
    v_small = _small_pack(v_e_g_in, v_g_final, v_e_g_q_a, v_e_g_kv_a, v_e_sinks, v_o_b_f, zero)
    smalls = _adamw(small_all, w_small, m_small, v_small, name="adamw_replicated")
    g_sm, d_sm, m_sm, v_sm = [_small_unpack(a) for a in smalls]
    loss = g_sm[6]

    sent0, recv0 = _peer_wait("exchange", *gr["pending0"], after=[o_in[1], b1[1], smalls[1]], name="grads0_wait")
    own = lax.dynamic_slice_in_dim(sent0, me, 1, axis=0)
    recv0 = lax.dynamic_update_slice_in_dim(recv0, own, me, axis=0)
    e_in = in_projection(recv0, RA0, N_E_IN, e_w_in, m_e_w_in, v_e_w_in, "e_w_in")
    b0 = _adamw(recv0[:, RA0:], *rows_b0, name="adamw_early")

    def sharded(k):
        q_up, kv_up = _unflat_b0(b0[k])
        o_out, e_out, o_g = _unflat_b1(b1[k])
        return jnp.transpose(e_in[k], (1, 2, 0)), q_up, kv_up, e_out, jnp.transpose(o_in[k], (1, 2, 0)), o_out, o_g

    g_sh, d_sh, m_sh, v_sh = [sharded(k) for k in range(4)]

    def leaves(sh, sm):
        return (sm[0], sh[0], sm[2], sh[1], sm[3], sh[2], sm[4], sh[3], sh[6], sh[4], sm[5], sh[5], sm[1])

    return (loss, gr["grad_x"][None], *leaves(g_sh, g_sm), *leaves(d_sh, d_sm), *leaves(m_sh, m_sm), *leaves(v_sh, v_sm))
```

```python
import functools

import jax
import jax.numpy as jnp
from jax import lax
from jax.experimental import pallas as pl
from jax.experimental.pallas import tpu as pltpu

F32 = jnp.float32
BF16 = jnp.bfloat16
NEG_INF = float("-inf")

N_DEV = 8
LANES = 128
D_MODEL = 1024
EPS = 1e-6
ROPE_THETA = 10000.0
MLA_HEADS = 8
MLA_Q_RANK = 256
MLA_KV_RANK = 128
MLA_NOPE = 64
MLA_ROPE = 32
MLA_V = 64
SWA_HEADS = 8
SWA_KV_HEADS = 2
SWA_DIM = 64
WINDOW = 128
FOX_HEADS = 16
FOX_DIM = 64

ADAM_LR = 0.001
ADAM_B1 = 0.9
ADAM_B2 = 0.999
ADAM_EPS = 1e-08
ADAM_WD = 0.01
ADAM_STEP = 10

ATT_T = 512
ATT_T_FWD = 1024
VMEM_LIMIT = 56 * 1024 * 1024
MATMUL_B_BLOCK_BYTES = 8 * 1024 * 1024

Z0A_UNITS = 12
Z0B_UNITS = 6

WIDE = 1024
N_E_IN = 276
N_O_IN = 514
RA0 = 288
RB0 = 32 + 16
RA1 = 528
RB1 = 128 + 128 + 16
SMALL_ROWS = 24


def _tile(n, cands):
    for c in cands:
        if n % c == 0:
            return c
    raise ValueError(f"no tile for {n}")


ROW_TILES = (512, 256, 128)


def _params(sem, vmem=VMEM_LIMIT):
    return pltpu.CompilerParams(dimension_semantics=sem, vmem_limit_bytes=vmem)


def _matmul(a, b, *, name, ta=False, tb=False, out_dtype=F32):
    if ta:
        kdim, m = a.shape[-2], a.shape[-1] * (a.shape[0] if a.ndim == 3 else 1)
    else:
        m, kdim = a.shape
    if tb:
        n, kb = b.shape
    else:
        kb, n = b.shape
    assert kdim == kb, (a.shape, b.shape)
    tm = _tile(m, (512, 256, 128))
    tn = _tile(n, [c for c in (1024, 768, 512, 384, 256, 128) if c * kdim * b.dtype.itemsize <= MATMUL_B_BLOCK_BYTES])
    dims = (((0 if ta else 1,), (1 if tb else 0,)), ((), ()))

    def body(a_ref, b_ref, o_ref):
        r = lax.dot_general(a_ref[...].astype(BF16), b_ref[...].astype(BF16), dims, preferred_element_type=F32)
        o_ref[...] = r.astype(out_dtype)

    if a.ndim == 3:
        per = a.shape[2] // tm
        a_spec = pl.BlockSpec((None, kdim, tm), lambda i, j: (i // per, 0, i % per))
    else:
        a_spec = pl.BlockSpec((kdim, tm), lambda i, j: (0, i)) if ta else pl.BlockSpec((tm, kdim), lambda i, j: (i, 0))
    b_spec = pl.BlockSpec((tn, kdim), lambda i, j: (j, 0)) if tb else pl.BlockSpec((kdim, tn), lambda i, j: (0, j))
    return pl.pallas_call(
        body, name=name, grid=(m // tm, n // tn), in_specs=[a_spec, b_spec],
        out_specs=pl.BlockSpec((tm, tn), lambda i, j: (i, j)), out_shape=jax.ShapeDtypeStruct((m, n), out_dtype),
        compiler_params=_params(("parallel", "parallel")),
    )(a, b)


def _rmsnorm_fwd(x, g, *, width, col_blk, name, after=()):
    s = x.shape[0]
    tm = _tile(s, ROW_TILES)

    def body(x_ref, g_ref, *rest):
        y_ref = rest[-1]
        xf = x_ref[...].astype(F32)
        r = lax.rsqrt(jnp.mean(xf * xf, axis=-1, keepdims=True) + EPS)
        y_ref[...] = ((xf * r) * g_ref[...]).astype(BF16)

    return pl.pallas_call(
        body, name=name, grid=(s // tm,),
        in_specs=[pl.BlockSpec((tm, width), lambda i: (i, col_blk)), pl.BlockSpec((1, width), lambda i: (0, 0))]
        + [ANY] * len(after),
        out_specs=pl.BlockSpec((tm, width), lambda i: (i, 0)),
        out_shape=jax.ShapeDtypeStruct((s, width), BF16),
        compiler_params=_params(("parallel",)),
    )(x, g, *after)


def _rmsnorm_bwd(x, g, dy, *, width, col_blk, name):
    s = x.shape[0]
    tm = _tile(s, ROW_TILES)

    def body(x_ref, g_ref, dy_ref, dx_ref, dg_ref):
        @pl.when(pl.program_id(0) == 0)
        def _():
            dg_ref[...] = jnp.zeros_like(dg_ref)

        dx, dg = _rms_bwd_epilogue(dy_ref[...], x_ref[...], 0.0, g_ref[...])
        dg_ref[...] += dg
        dx_ref[...] = dx.astype(BF16)

    return pl.pallas_call(
        body, name=name, grid=(s // tm,),
        in_specs=[pl.BlockSpec((tm, width), lambda i: (i, col_blk)), pl.BlockSpec((1, width), lambda i: (0, 0)),
                  pl.BlockSpec((tm, width), lambda i: (i, 0))],
        out_specs=[pl.BlockSpec((tm, width), lambda i: (i, 0)), pl.BlockSpec((1, width), lambda i: (0, 0))],
        out_shape=[jax.ShapeDtypeStruct((s, width), BF16), jax.ShapeDtypeStruct((1, width), F32)],
        compiler_params=_params(("arbitrary",)),
    )(x, g, dy)


def _sigmoid(x):
    return 1.0 / (1.0 + jnp.exp(-x))


def _matmul_rows(terms, row_inputs, params, epilogue, outs, *, name, prologue=None, separate=False):
    s = row_inputs[0][0].shape[0] if row_inputs else terms[0][0].shape[-2]
    tm = _tile(s, ROW_TILES)
    steps = s // tm
    n_r, n_p, n_o = len(row_inputs), len(params), len(outs)
    n_t = sum(1 if term[0] is None else 2 for term in terms)

    def body(*refs):
        t_refs, r_refs = list(refs[:n_t]), refs[n_t:n_t + n_r]
        p_refs, o_refs = refs[n_t + n_r:n_t + n_r + n_p], refs[n_t + n_r + n_p:]
        i = pl.program_id(0)
        rows, small = [r[...] for r in r_refs], [p[...] for p in p_refs]
        made = None if prologue is None else prologue(*rows, *small)
        parts = []
        for term in terms:
            a = made if term[0] is None else t_refs.pop(0)[...].astype(BF16)
            dims = (((1,), (1 if term[2] else 0,)), ((), ()))
            parts.append(lax.dot_general(a, t_refs.pop(0)[...].astype(BF16), dims, preferred_element_type=F32))
        acc = parts if separate else sum(parts[1:], parts[0])
        vals = epilogue(acc, *rows, *small) if prologue is None else epilogue(acc, *rows, *small, made)
        for ref, val, out in zip(o_refs, vals, outs):
            if out[0] == "rows":
                ref[...] = val.astype(ref.dtype)
            else:
                @pl.when(i == 0)
                def _(ref=ref):
                    ref[...] = jnp.zeros_like(ref)

                ref[...] += val

    in_specs, args = [], []
    for term in terms:
        a, b = term[0], term[1]
        if a is None:
            in_specs.append(_resident(b.shape, lambda i: (0, 0)))
            args.append(b)
            continue
        b_rows = b.shape[0] if term[2] or len(term) < 4 else a.shape[-1]
        b_blk = 0 if len(term) < 4 else term[3] // b_rows
        if len(term) == 5:
            a_spec = pl.BlockSpec((None, tm, a.shape[2]), lambda i, c=term[4]: (c, i, 0))
        else:
            a_spec = pl.BlockSpec((tm, a.shape[1]), lambda i: (i, 0))
        in_specs += [a_spec, _resident((b_rows, b.shape[1]), lambda i, b_blk=b_blk: (b_blk, 0))]
        args += [a, b]
    for arr, width, col_blk in row_inputs:
        in_specs.append(pl.BlockSpec((tm, width), lambda i, col_blk=col_blk: (i, col_blk)))
        args.append(arr)
    for p in params:
        in_specs.append(pl.BlockSpec(p.shape, lambda i: (0, 0)))
        args.append(p)
    out_specs, out_shape = [], []
    for out in outs:
        if out[0] == "rows":
            out_specs.append(pl.BlockSpec((tm, out[1]), lambda i: (i, 0)))
            out_shape.append(jax.ShapeDtypeStruct((s, out[1]), out[2]))
        else:
            out_specs.append(pl.BlockSpec(out[1], lambda i: (0, 0)))
            out_shape.append(jax.ShapeDtypeStruct(out[1], F32))
    return pl.pallas_call(
        body, name=name, grid=(steps,), in_specs=in_specs, out_specs=out_specs, out_shape=out_shape,
        compiler_params=_params(("arbitrary",)),
    )(*args)


def _rms_stats(x):
    r = lax.rsqrt(jnp.mean(x * x, axis=-1, keepdims=True) + EPS)
    return r, x * r


def _gated(o_parts, gate):
    o = o_parts[0] if len(o_parts) == 1 else jnp.concatenate(o_parts, axis=1)
    return (o * (gate * _sigmoid(gate))).astype(BF16)


def _and_first(vals, *more):
    return (*vals, *more, vals[0])


def _residual_norm_epilogue(r, x, g):
    x1 = x + r
    _, xh = _rms_stats(x1)
    return x1, xh * g


def _rms_bwd_epilogue(dy, x, add, g):
    r, xh = _rms_stats(x)
    dxh = dy * g
    dx = r * (dxh - xh * jnp.mean(dxh * xh, axis=-1, keepdims=True)) + add
    return dx, jnp.sum(dy * xh, axis=0, keepdims=True)


def _loss_epilogue(r, x1, target, g):
    rs, xh = _rms_stats(x1 + r)
    err = xh * g - target
    loss = jnp.broadcast_to(0.5 * jnp.sum(jnp.mean(err * err, axis=-1, keepdims=True)), (8, LANES))
    dy = err * (1.0 / D_MODEL)
    dxh = dy * g
    dx = rs * (dxh - xh * jnp.mean(dxh * xh, axis=-1, keepdims=True))
    return dx, loss, jnp.sum(dy * xh, axis=0, keepdims=True)


def _gate_bwd_epilogue(widths):
    def epilogue(d, *rows):
        o_parts, gt = rows[:-1], rows[-1]
        o = o_parts[0] if len(o_parts) == 1 else jnp.concatenate(o_parts, axis=1)
        sg = _sigmoid(gt)
        do = d * (gt * sg)
        d_gate = d * o * (sg * (1.0 + gt * (1.0 - sg)))
        cuts = [sum(widths[:k]) for k in range(len(widths) + 1)]
        return tuple(do[:, cuts[k]:cuts[k + 1]] for k in range(len(widths))) + (d_gate,)

    return epilogue


def _rot_half(x):
    lane = lax.broadcasted_iota(jnp.int32, x.shape, 1)
    return jnp.where(lane < 80, pltpu.roll(x, LANES - 16, axis=1), pltpu.roll(x, 16, axis=1))


def _rot_half_t(g):
    lane = lax.broadcasted_iota(jnp.int32, g.shape, 1)
    lo = (lane >= MLA_NOPE) & (lane < MLA_NOPE + MLA_ROPE // 2)
    hi = (lane >= MLA_NOPE + MLA_ROPE // 2) & (lane < MLA_NOPE + MLA_ROPE)
    return jnp.where(lo, pltpu.roll(g, LANES - 16, axis=1), jnp.where(hi, pltpu.roll(g, 16, axis=1), 0.0))


def _rope_q_epilogue(q, c, sn):
    heads = [q[:, h * LANES:(h + 1) * LANES] for h in range(MLA_HEADS)]
    return (jnp.concatenate([qh * c + _rot_half(qh) * sn for qh in heads], axis=1),)


def _rope_k_epilogue(kv, kpe, c, sn):
    kpe_r = kpe * c + _rot_half(kpe) * sn
    lane = lax.broadcasted_iota(jnp.int32, kpe.shape, 1)
    heads = [jnp.where(lane < MLA_NOPE, kv[:, h * LANES:(h + 1) * LANES], kpe_r) for h in range(MLA_HEADS)]
    return (jnp.concatenate(heads + [kv[:, MLA_HEADS * LANES:]], axis=1),)


def _rope_bwd(dqm, dkm, dvm, cos_t, sin_t, *, name):
    s = dqm.shape[0]
    tm = _tile(s, ROW_TILES)
    hw = MLA_HEADS * LANES
    vw = MLA_HEADS * MLA_V

    def body(dq_ref, dk_ref, dv_ref, c_ref, s_ref, dqp_ref, dkv_ref, dkpe_ref):
        c = c_ref[...]
        sn = s_ref[...]
        ksum = jnp.zeros((tm, LANES), F32)
        for h in range(MLA_HEADS):
            sl = slice(h * LANES, (h + 1) * LANES)
            dq = dq_ref[:, sl]
            dqp_ref[:, sl] = (dq * c + _rot_half_t(dq * sn)).astype(BF16)
            dk = dk_ref[:, sl]
            dkv_ref[:, sl] = dk.astype(BF16)
            ksum = ksum + dk
        dkv_ref[:, hw:] = dv_ref[...]
        lane = lax.broadcasted_iota(jnp.int32, ksum.shape, 1)
        dkpe = ksum * c + _rot_half_t(ksum * sn)
        dkpe_ref[...] = jnp.where((lane >= MLA_NOPE) & (lane < MLA_NOPE + MLA_ROPE), dkpe, 0.0).astype(BF16)

    return pl.pallas_call(
        body, name=name, grid=(s // tm,),
        in_specs=[pl.BlockSpec((tm, hw), lambda i: (i, 0)), pl.BlockSpec((tm, hw), lambda i: (i, 0)),
                  pl.BlockSpec((tm, vw), lambda i: (i, 0)),
                  pl.BlockSpec((tm, LANES), lambda i: (i, 0)), pl.BlockSpec((tm, LANES), lambda i: (i, 0))],
        out_specs=[pl.BlockSpec((tm, hw), lambda i: (i, 0)), pl.BlockSpec((tm, hw + vw), lambda i: (i, 0)),
                   pl.BlockSpec((tm, LANES), lambda i: (i, 0))],
        out_shape=[jax.ShapeDtypeStruct((s, hw), BF16), jax.ShapeDtypeStruct((s, hw + vw), BF16),
                   jax.ShapeDtypeStruct((s, LANES), BF16)],
        compiler_params=_params(("parallel",)),
    )(dqm, dkm, dvm, cos_t, sin_t)


def _head_mask(shape, a):
    lane = lax.broadcasted_iota(jnp.int32, shape, 1)
    return (lane >= 64 * a) & (lane < 64 * (a + 1))


_NT = (((1,), (1,)), ((), ()))
LOG2E = 1.4426950408889634


def _stack_heads(tile, hw):
    lane = lax.broadcasted_iota(jnp.int32, tile.shape, 1)
    z = jnp.zeros_like(tile)
    return jnp.concatenate([jnp.where(lane < hw, tile, z), jnp.where(lane >= hw, tile, z)], axis=0)


def _stacked_rows(r0, r1, t):
    n = r0.shape[-1]
    return jnp.concatenate([jnp.broadcast_to(r0, (t, n)), jnp.broadcast_to(r1, (t, n))], axis=0)


def _resident(block, index_map):
    return pl.BlockSpec(block, index_map, pipeline_mode=pl.Buffered(1))


def _fwd_tile(s):
    return ATT_T_FWD if s % ATT_T_FWD == 0 else min(ATT_T, s)


def _flash_fwd(q, k, v, bias, *, n_pairs, hw, q_off, k_off, v_off, scale, name, rider=None):
    s = q.shape[0]
    t = _fwd_tile(s)
    nb = s // t
    qw = 2 * hw
    has_bias = bias is not None
    c1 = scale * LOG2E

    def body(*refs):
        refs, ride_refs = _split_rider(refs, rider, n_in=4 if has_bias else 3, n_out=2)
        if has_bias:
            q_ref, k_ref, v_ref, b_ref, o_ref, lse_ref, vt_ref, bcol_ref = refs
        else:
            q_ref, k_ref, v_ref, o_ref, lse_ref, vt_ref = refs
            b_ref = bcol_ref = None
        _ride_start(rider, ride_refs, pl.program_id(0) == 0)
        row = lax.broadcasted_iota(jnp.int32, (t, t), 0)
        col = lax.broadcasted_iota(jnp.int32, (t, t), 1)
        cmask_t = jnp.concatenate([row <= col, row <= col], axis=1)
        lane_lt64 = lax.broadcasted_iota(jnp.int32, (t, LANES), 1) < 64

        def as_column(r):
            return jnp.broadcast_to(r, (8, r.shape[1])).T[:, 0:1]

        def v_block(j, _):
            c0 = pl.multiple_of(j * t, t)
            vt_ref[j] = v_ref[pl.ds(c0, t), :].astype(F32).T.astype(BF16)
            if has_bias:
                for a in range(2):
                    bcol_ref[a, pl.ds(c0, t), :] = as_column(b_ref[0, a, j])
            return 0

        lax.fori_loop(0, nb, v_block, 0)

        def stacked_queries(i):
            return _stack_heads(q_ref[pl.ds(pl.multiple_of(i * t, t), t), :], hw).astype(F32).T.astype(BF16)

        def kv_step(j, carry, qs_t, masked):
            m, l, acc = carry
            rows = pl.ds(pl.multiple_of(j * t, t), t)
            sc = jnp.dot(k_ref[rows, :], qs_t, preferred_element_type=F32) * c1
            if has_bias:
                sc = sc + jnp.concatenate([jnp.broadcast_to(bcol_ref[0, rows, :], (t, t)),
                                           jnp.broadcast_to(bcol_ref[1, rows, :], (t, t))], axis=1)
            if masked:
                sc = jnp.where(cmask_t, sc, NEG_INF)
            m_new = jnp.maximum(m, jnp.max(sc, axis=0, keepdims=True))
            alpha = jnp.exp2(m - m_new)
            p = jnp.exp2(sc - m_new)
            l_new = alpha * l + jnp.sum(p, axis=0, keepdims=True)
            pv = jnp.dot(vt_ref[j], p.astype(BF16), preferred_element_type=F32)
            return m_new, l_new, alpha * acc + pv

        def finish(i, carry):
            m, l, acc = carry
            r0 = pl.multiple_of(i * t, t)
            out = (acc / l).T
            lse2 = as_column(m + jnp.log2(l))
            lse_ref[0, 0, pl.ds(r0, t), :] = lse2[:t]
            lse_ref[0, 1, pl.ds(r0, t), :] = lse2[t:]
            o_ref[pl.ds(r0, t), :] = jnp.where(lane_lt64, out[:t], out[t:])

        init = (jnp.full((1, 2 * t), NEG_INF, F32), jnp.zeros((1, 2 * t), F32), jnp.zeros((LANES, 2 * t), F32))

        def q_block(i, _):
            qs_t = stacked_queries(i)
            carry = lax.fori_loop(0, i, lambda j, c: kv_step(j, c, qs_t, False), init)
            finish(i, kv_step(i, carry, qs_t, True))
            return 0

        lax.fori_loop(0, nb, q_block, 0)
        _ride_wait(rider, ride_refs, pl.program_id(0) == n_pairs - 1)

    in_specs = [_resident((s, qw), lambda p: (0, q_off + p)), _resident((s, qw), lambda p: (0, k_off + p)),
                _resident((s, LANES), lambda p: (0, v_off + p))]
    args = [q, k, v]
    if has_bias:
        in_specs.append(_resident((1, 2, nb, 1, t), lambda p: (p, 0, 0, 0, 0)))
        args.append(bias)
    out_specs = [pl.BlockSpec((s, LANES), lambda p: (0, p)), pl.BlockSpec((1, 2, s, 1), lambda p: (p, 0, 0, 0))]
    out_shape = [jax.ShapeDtypeStruct((s, n_pairs * LANES), F32), jax.ShapeDtypeStruct((n_pairs, 2, s, 1), F32)]
    scratch = [pltpu.VMEM((nb, LANES, t), BF16)] + ([pltpu.VMEM((2, s, 1), F32)] if has_bias else [])
    scratch += _add_rider(rider, in_specs, args, out_specs, out_shape)
    return pl.pallas_call(
        body, name=name, grid=(n_pairs,), in_specs=in_specs, out_specs=out_specs, out_shape=out_shape,
        scratch_shapes=scratch,
        compiler_params=_params(("parallel",) if rider is None else ("arbitrary",)),
    )(*args)


def _flash_bwd(q, k, v, do, o, lse, bias, *, n_pairs, hw, q_off, k_off, v_off, scale, qk_dtype, name, rider=None,
               stacked=False):
    s = q.shape[0]
    t = min(ATT_T, s)
    nb = s // t
    qw = 2 * hw
    has_bias = bias is not None
    c1 = scale * LOG2E

    def body(*refs):
        n_grads = 1 if stacked else 3
        refs, ride_refs = _split_rider(refs, rider, n_in=7 if has_bias else 6, n_out=n_grads + (2 if has_bias else 0))
        if stacked:
            refs = list(refs)
            n_in = 7 if has_bias else 6
            refs[n_in:n_in + 1] = [refs[n_in].at[0], refs[n_in].at[1], refs[n_in].at[2]]
        if has_bias:
            (q_ref, k_ref, v_ref, do_ref, o_ref, lse_ref, b_ref, dq_ref, dk_ref, dv_ref, db_ref, dr_ref,
             dkt_ref, dvt_ref) = refs
            db_ref[...] = jnp.zeros_like(db_ref)
        else:
            q_ref, k_ref, v_ref, do_ref, o_ref, lse_ref, dq_ref, dk_ref, dv_ref, dkt_ref, dvt_ref = refs
            b_ref = db_ref = dr_ref = None
        _ride_start(rider, ride_refs, pl.program_id(0) == 0)
        dkt_ref[...] = jnp.zeros_like(dkt_ref)
        dvt_ref[...] = jnp.zeros_like(dvt_ref)
        causal = lax.broadcasted_iota(jnp.int32, (t, t), 1) <= lax.broadcasted_iota(jnp.int32, (t, t), 0)
        cmask = jnp.concatenate([causal, causal], axis=0)
        lane_lt_hw = lax.broadcasted_iota(jnp.int32, (t, qw), 1) < hw

        def q_block(i, _):
            r0 = pl.multiple_of(i * t, t)
            qs = _stack_heads(q_ref[pl.ds(r0, t), :], hw)
            dos = _stack_heads(do_ref[pl.ds(r0, t), :], 64)
            ot = o_ref[pl.ds(r0, t), :]
            delta = jnp.sum(dos * jnp.concatenate([ot, ot], axis=0), axis=-1, keepdims=True)
            lse2 = jnp.concatenate([lse_ref[0, 0, pl.ds(r0, t), :], lse_ref[0, 1, pl.ds(r0, t), :]], axis=0)
            dosb = dos.astype(BF16)
            dos_t = dos.T.astype(BF16)
            qs_t = qs.astype(F32).T.astype(BF16)

            def kv_step(j, carry, masked):
                dq, rsum = carry
                c0 = pl.multiple_of(j * t, t)
                kt = k_ref[pl.ds(c0, t), :]
                vt = v_ref[pl.ds(c0, t), :]
                sc = lax.dot_general(qs, kt, _NT, preferred_element_type=F32) * c1
                if has_bias:
                    sc = sc + _stacked_rows(b_ref[0, 0, j], b_ref[0, 1, j], t)
                if masked:
                    sc = jnp.where(cmask, sc, NEG_INF)
                p = jnp.exp2(sc - lse2)
                dp = lax.dot_general(dosb, vt, _NT, preferred_element_type=F32)
                ds = p * (dp - delta)
                dsb = ds.astype(BF16)
                pb = p.astype(BF16)
                if hw == LANES:
                    dvt_ref[j] += jnp.concatenate(
                        [jnp.dot(dos_t[:64, :t], pb[:t], preferred_element_type=F32),
                         jnp.dot(dos_t[64:, t:], pb[t:], preferred_element_type=F32)], axis=0)
                    dkt_ref[j] += jnp.concatenate(
                        [jnp.dot(qs_t[:hw, :t], dsb[:t], preferred_element_type=F32),
                         jnp.dot(qs_t[hw:, t:], dsb[t:], preferred_element_type=F32)], axis=0)
                else:
                    dvt_ref[j] += jnp.dot(dos_t, pb, preferred_element_type=F32)
                    dkt_ref[j] += jnp.dot(qs_t, dsb, preferred_element_type=F32)
                if has_bias:
                    db_ref[0, 0, j] += jnp.sum(ds[:t], axis=0, keepdims=True)
                    db_ref[0, 1, j] += jnp.sum(ds[t:], axis=0, keepdims=True)
                    rsum = rsum + jnp.sum(ds, axis=-1, keepdims=True)
                return dq + jnp.dot(dsb, kt, preferred_element_type=F32), rsum

            init = (jnp.zeros((2 * t, qw), F32), jnp.zeros((2 * t, 1), F32))
            carry = lax.fori_loop(0, i, functools.partial(kv_step, masked=False), init)
            dq, rsum = kv_step(i, carry, True)
            dq = dq * scale
            dq_ref[pl.ds(r0, t), :] = jnp.where(lane_lt_hw, dq[:t], dq[t:]).astype(qk_dtype)
            if has_bias:
                rsum_row = jnp.broadcast_to(rsum, (2 * t, LANES)).T[0:1]
                dr_ref[0, 0, i] = rsum_row[:, :t]
                dr_ref[0, 1, i] = rsum_row[:, t:]
            return 0

        lax.fori_loop(0, nb, q_block, 0)

        def k_block(j, _):
            c0 = pl.multiple_of(j * t, t)
            dk_ref[pl.ds(c0, t), :] = (dkt_ref[j].T * scale).astype(qk_dtype)
            dv_ref[pl.ds(c0, t), :] = dvt_ref[j].T.astype(BF16)
            return 0

        lax.fori_loop(0, nb, k_block, 0)
        _ride_wait(rider, ride_refs, pl.program_id(0) == n_pairs - 1)

    in_specs = [_resident((s, qw), lambda p: (0, q_off + p)), _resident((s, qw), lambda p: (0, k_off + p)),
                _resident((s, LANES), lambda p: (0, v_off + p)),
                _resident((s, LANES), lambda p: (0, p)), _resident((s, LANES), lambda p: (0, p)),
                _resident((1, 2, s, 1), lambda p: (p, 0, 0, 0))]
    args = [q, k, v, do, o, lse]
    if stacked:
        assert qw == LANES and qk_dtype == BF16
        out_specs = [pl.BlockSpec((3, s, LANES), lambda p: (0, 0, p))]
        out_shape = [jax.ShapeDtypeStruct((3, s, n_pairs * LANES), BF16)]
    else:
        out_specs = [pl.BlockSpec((s, qw), lambda p: (0, p)), pl.BlockSpec((s, qw), lambda p: (0, p)),
                     pl.BlockSpec((s, LANES), lambda p: (0, p))]
        out_shape = [jax.ShapeDtypeStruct((s, n_pairs * qw), qk_dtype), jax.ShapeDtypeStruct((s, n_pairs * qw), qk_dtype),
                     jax.ShapeDtypeStruct((s, n_pairs * LANES), BF16)]
    if has_bias:
        in_specs.append(_resident((1, 2, nb, 1, t), lambda p: (p, 0, 0, 0, 0)))
        args.append(bias)
        for _ in range(2):
            out_specs.append(pl.BlockSpec((1, 2, nb, 1, t), lambda p: (p, 0, 0, 0, 0)))
            out_shape.append(jax.ShapeDtypeStruct((n_pairs, 2, nb, 1, t), F32))
    scratch = [pltpu.VMEM((nb, qw, t), F32), pltpu.VMEM((nb, LANES, t), F32)]
    scratch += _add_rider(rider, in_specs, args, out_specs, out_shape)
    return pl.pallas_call(
        body, name=name, grid=(n_pairs,), in_specs=in_specs, out_specs=out_specs, out_shape=out_shape,
        scratch_shapes=scratch,
        compiler_params=_params(("parallel",) if rider is None else ("arbitrary",)),
    )(*args)


def _alibi_slope(h):
    return 2.0 ** (-8.0 * (h + 1.0) / SWA_HEADS)


SWA_ROWS = 512
SWA_SCALE = SWA_DIM ** -0.5


def _swa_geometry(i):
    w = WINDOW
    r0 = pl.multiple_of(i * w, w)
    b0 = pl.multiple_of(jnp.maximum(i - 1, 0) * w, w)
    row = lax.broadcasted_iota(jnp.int32, (w, 2 * w), 0)
    col = lax.broadcasted_iota(jnp.int32, (w, 2 * w), 1)
    dist = row - col + (r0 - b0)
    valid = (dist >= 0) & (dist < w)
    return r0, b0, dist.astype(F32), valid


def _swa_q_head(qblk, h):
    kv = h // (SWA_HEADS // SWA_KV_HEADS)
    if h % 2 != kv:
        qblk = pltpu.roll(qblk, 64, axis=1)
    return jnp.where(_head_mask(qblk.shape, kv), qblk, 0.0)


SWA_GROUP = SWA_HEADS // SWA_KV_HEADS


def _swa_stack(ref, rs, grp):
    parts = []
    for a in range(SWA_GROUP):
        h = SWA_GROUP * grp + a
        parts.append(_swa_q_head(ref[rs, (h // 2) * LANES:(h // 2 + 1) * LANES].astype(F32), h))
    return jnp.concatenate(parts, axis=0)


def _swa_unstack(x, grp):
    tiles = []
    for a in range(SWA_GROUP):
        h = SWA_GROUP * grp + a
        tile = x[a * WINDOW:(a + 1) * WINDOW]
        tiles.append(pltpu.roll(tile, 64, axis=1) if h % 2 != grp else tile)
    return tiles


def _swa_head_column(vals):
    return jnp.concatenate([jnp.full((WINDOW, 1), v, F32) for v in vals], axis=0)


def _swa_logits(qs, kb, dist, valid, grp):
    slopes = _swa_head_column([_alibi_slope(SWA_GROUP * grp + a) for a in range(SWA_GROUP)])
    dist4 = jnp.concatenate([dist] * SWA_GROUP, axis=0)
    valid4 = jnp.concatenate([valid] * SWA_GROUP, axis=0)
    sc = lax.dot_general(qs, kb, _NT, preferred_element_type=F32) * SWA_SCALE - slopes * dist4
    return jnp.where(valid4, sc, NEG_INF)


def _swa_merge_heads(tiles):
    lt64 = lax.broadcasted_iota(jnp.int32, (WINDOW, LANES), 1) < 64
    return jnp.concatenate([jnp.where(lt64, tiles[2 * b], tiles[2 * b + 1]) for b in range(SWA_HEADS // 2)], axis=1)


def _swa_fwd(z0b, sinks, *, name):
    s = z0b.shape[0]
    w = WINDOW
    rows = min(SWA_ROWS, s)
    per_step = rows // w
    qcols = SWA_HEADS * SWA_DIM

    def body(sink_ref, q_ref, k_ref, v_ref, o_ref, lse_ref):
        g = pl.program_id(0)
        for ii in range(per_step):
            rs = slice(ii * w, (ii + 1) * w)
            r0, b0, dist, valid = _swa_geometry(g * per_step + ii)
            kb = k_ref[pl.ds(b0, 2 * w), :]
            vb = v_ref[pl.ds(b0, 2 * w), :]
            o_tiles = []
            for h in range(SWA_HEADS):
                kv = h // SWA_GROUP
                qh = _swa_q_head(q_ref[rs, (h // 2) * LANES:(h // 2 + 1) * LANES].astype(F32), h).astype(BF16)
                sc = lax.dot_general(qh, kb, _NT, preferred_element_type=F32) * SWA_SCALE - _alibi_slope(h) * dist
                sc = jnp.where(valid, sc, NEG_INF)
                sink = sink_ref[0, h]
                m = jnp.maximum(jnp.max(sc, axis=-1, keepdims=True), sink)
                p = jnp.exp(sc - m)
                l = jnp.sum(p, axis=-1, keepdims=True) + jnp.exp(sink - m)
                oh = jnp.dot(p.astype(BF16), vb, preferred_element_type=F32) / l
                o_tiles.append(pltpu.roll(oh, 64, axis=1) if h % 2 != kv else oh)
                lse_ref[h, rs, :] = m + jnp.log(l)
            o_ref[rs, :] = _swa_merge_heads(o_tiles)

    return pl.pallas_call(
        body, name=name, grid=(s // rows,),
        in_specs=[pl.BlockSpec(memory_space=pltpu.SMEM),
                  pl.BlockSpec((rows, qcols), lambda g: (g, 0)),
                  pl.BlockSpec((s, LANES), lambda g: (0, 4)), pl.BlockSpec((s, LANES), lambda g: (0, 5))],
        out_specs=[pl.BlockSpec((rows, qcols), lambda g: (g, 0)), pl.BlockSpec((SWA_HEADS, rows, 1), lambda g: (0, g, 0))],
        out_shape=[jax.ShapeDtypeStruct((s, qcols), F32), jax.ShapeDtypeStruct((SWA_HEADS, s, 1), F32)],
        compiler_params=_params(("parallel",)),
    )(sinks, z0b, z0b, z0b)


def _swa_bwd(z0b, sinks, do, o, lse, *, name):
    s = z0b.shape[0]
    w = WINDOW
    rows = min(SWA_ROWS, s)
    per_step = rows // w
    qcols = SWA_HEADS * SWA_DIM
    nblk = s // w

    def body(sink_ref, q_ref, k_ref, v_ref, do_ref, o_ref, lse_ref, dq_ref, dkt_ref, dvt_ref, dsink_ref):
        g = pl.program_id(0)

        @pl.when(g == 0)
        def _():
            dkt_ref[...] = jnp.zeros_like(dkt_ref)
            dvt_ref[...] = jnp.zeros_like(dvt_ref)
            dsink_ref[...] = jnp.zeros_like(dsink_ref)

        for ii in range(per_step):
            i = g * per_step + ii
            rs = slice(ii * w, (ii + 1) * w)
            r0, b0, dist, valid = _swa_geometry(i)
            j0 = jnp.maximum(i - 1, 0)
            kb = k_ref[pl.ds(b0, 2 * w), :]
            vb = v_ref[pl.ds(b0, 2 * w), :]
            dq_tiles = []
            for grp in range(SWA_KV_HEADS):
                heads = [SWA_GROUP * grp + a for a in range(SWA_GROUP)]
                qs32 = _swa_stack(q_ref, rs, grp)
                dos32 = _swa_stack(do_ref, rs, grp)
                delta = jnp.sum(dos32 * _swa_stack(o_ref, rs, grp), axis=-1, keepdims=True)
                lse = jnp.concatenate([lse_ref[h, rs, :] for h in heads], axis=0)
                sink = _swa_head_column([sink_ref[0, h] for h in heads])
                p = jnp.exp(_swa_logits(qs32.astype(BF16), kb, dist, valid, grp) - lse)
                dp = lax.dot_general(dos32.astype(BF16), vb, _NT, preferred_element_type=F32)
                ds = p * (dp - delta)
                dsb = ds.astype(BF16)
                d_sink = jnp.exp(sink - lse) * delta
                for a, h in enumerate(heads):
                    dsink_ref[h:h + 1, :] += jnp.broadcast_to(-jnp.sum(d_sink[a * w:(a + 1) * w]), (1, LANES))
                dvt = jnp.dot(dos32.T.astype(BF16), p.astype(BF16), preferred_element_type=F32)
                dkt = jnp.dot(qs32.T.astype(BF16), dsb, preferred_element_type=F32) * SWA_SCALE
                dvt_ref[j0] += dvt[:, :w]
                dvt_ref[j0 + 1] += dvt[:, w:]
                dkt_ref[j0] += dkt[:, :w]
                dkt_ref[j0 + 1] += dkt[:, w:]
                dq_tiles += _swa_unstack(jnp.dot(dsb, kb, preferred_element_type=F32) * SWA_SCALE, grp)
            dq_ref[rs, :] = _swa_merge_heads(dq_tiles)

    return pl.pallas_call(
        body, name=name, grid=(s // rows,),
        in_specs=[pl.BlockSpec(memory_space=pltpu.SMEM),
                  pl.BlockSpec((rows, qcols), lambda g: (g, 0)),
                  pl.BlockSpec((s, LANES), lambda g: (0, 4)), pl.BlockSpec((s, LANES), lambda g: (0, 5)),
                  pl.BlockSpec((rows, qcols), lambda g: (g, 0)), pl.BlockSpec((rows, qcols), lambda g: (g, 0)),
                  pl.BlockSpec((SWA_HEADS, rows, 1), lambda g: (0, g, 0))],
        out_specs=[pl.BlockSpec((rows, qcols), lambda g: (g, 0)),
                   pl.BlockSpec((nblk, LANES, w), lambda g: (0, 0, 0)),
                   pl.BlockSpec((nblk, LANES, w), lambda g: (0, 0, 0)),
                   pl.BlockSpec((SWA_HEADS, LANES), lambda g: (0, 0))],
        out_shape=[jax.ShapeDtypeStruct((s, qcols), F32),
                   jax.ShapeDtypeStruct((nblk, LANES, w), F32), jax.ShapeDtypeStruct((nblk, LANES, w), F32),
                   jax.ShapeDtypeStruct((SWA_HEADS, LANES), F32)],
        compiler_params=_params(("arbitrary",)),
    )(sinks, z0b, z0b, z0b, do, o, lse)


CUM_T = 256


def _split3(x):
    hi = x.astype(BF16)
    r1 = x - hi.astype(F32)
    mid = r1.astype(BF16)
    lo = (r1 - mid.astype(F32)).astype(BF16)
    return hi, mid, lo


def _tri_dot(tri, x):
    hi, mid, lo = _split3(x)
    out = jnp.dot(tri, hi, preferred_element_type=F32)
    out = out + jnp.dot(tri, mid, preferred_element_type=F32)
    return out + jnp.dot(tri, lo, preferred_element_type=F32)


def _logf_fwd(zf, bf, *, name):
    s = zf.shape[0]
    t = CUM_T
    nb = s // t

    def body(z_ref, b_ref, c_ref, carry_ref):
        i = pl.program_id(0)

        @pl.when(i == 0)
        def _():
            carry_ref[...] = jnp.zeros_like(carry_ref)

        x = z_ref[...] + b_ref[...]
        lf = jnp.minimum(x, 0.0) - jnp.log(1.0 + jnp.exp(-jnp.abs(x)))
        row = lax.broadcasted_iota(jnp.int32, (t, t), 0)
        col = lax.broadcasted_iota(jnp.int32, (t, t), 1)
        tri = jnp.where(col <= row, 1.0, 0.0).astype(BF16)
        c = _tri_dot(tri, lf) + carry_ref[...]
        c_ref[...] = c
        carry_ref[...] = c[t - 1:t, :]

    return pl.pallas_call(
        body, name=name, grid=(nb,),
        in_specs=[pl.BlockSpec((t, LANES), lambda i: (i, 0)), pl.BlockSpec((1, LANES), lambda i: (0, 0))],
        out_specs=pl.BlockSpec((t, LANES), lambda i: (i, 0)),
        out_shape=jax.ShapeDtypeStruct((s, LANES), F32),
        scratch_shapes=[pltpu.VMEM((1, LANES), F32)],
        compiler_params=_params(("arbitrary",)),
    )(zf, bf)


def _logf_bwd(dc, zf, bf, *, name):
    s = zf.shape[0]
    t = CUM_T
    nb = s // t

    def body(dc_ref, z_ref, b_ref, dz_ref, db_ref, carry_ref):
        i = pl.program_id(0)

        @pl.when(i == 0)
        def _():
            carry_ref[...] = jnp.zeros_like(carry_ref)
            db_ref[...] = jnp.zeros_like(db_ref)

        row = lax.broadcasted_iota(jnp.int32, (t, t), 0)
        col = lax.broadcasted_iota(jnp.int32, (t, t), 1)
        tri = jnp.where(col >= row, 1.0, 0.0).astype(BF16)
        dlf = _tri_dot(tri, dc_ref[...]) + carry_ref[...]
        carry_ref[...] = dlf[0:1, :]
        x = z_ref[...] + b_ref[...]
        dz = dlf * _sigmoid(-x)
        dz_ref[...] = dz.astype(BF16)
        db_ref[...] += jnp.sum(dz, axis=0, keepdims=True)

    return pl.pallas_call(
        body, name=name, grid=(nb,),
        in_specs=[pl.BlockSpec((t, LANES), lambda i: (nb - 1 - i, 0)), pl.BlockSpec((t, LANES), lambda i: (nb - 1 - i, 0)),
                  pl.BlockSpec((1, LANES), lambda i: (0, 0))],
        out_specs=[pl.BlockSpec((t, LANES), lambda i: (nb - 1 - i, 0)), pl.BlockSpec((1, LANES), lambda i: (0, 0))],
        out_shape=[jax.ShapeDtypeStruct((s, LANES), BF16), jax.ShapeDtypeStruct((1, LANES), F32)],
        scratch_shapes=[pltpu.VMEM((1, LANES), F32)],
        compiler_params=_params(("arbitrary",)),
    )(dc, zf, bf)


def _sum_pieces(p_ref):
    g = p_ref[0].astype(F32)
    for k in range(1, N_DEV):
        g = g + p_ref[k].astype(F32)
    return g


def _adam_update(g, w, m, v):
    bc1 = 1.0 - ADAM_B1 ** ADAM_STEP
    bc2 = 1.0 - ADAM_B2 ** ADAM_STEP
    nm = ADAM_B1 * m + (1.0 - ADAM_B1) * g
    nv = ADAM_B2 * v + (1.0 - ADAM_B2) * (g * g)
    m_hat = nm / bc1
    v_hat = nv / bc2
    return -ADAM_LR * (m_hat / (jnp.sqrt(v_hat) + ADAM_EPS) + ADAM_WD * w), nm, nv


def _adamw(pieces, w, m, v, *, name):
    rows, cols = w.shape
    tr = _tile(rows, (RB1, RB0, SMALL_ROWS))

    def body(p_ref, w_ref, m_ref, v_ref, g_ref, d_ref, nm_ref, nv_ref):
        g = _sum_pieces(p_ref)
        g_ref[...] = g
        d_ref[...], nm_ref[...], nv_ref[...] = _adam_update(g, w_ref[...], m_ref[...], v_ref[...])

    spec = pl.BlockSpec((tr, cols), lambda i: (i, 0))
    shape = jax.ShapeDtypeStruct((rows, cols), F32)
    return pl.pallas_call(
        body, name=name, grid=(rows // tr,),
        in_specs=[pl.BlockSpec((N_DEV, tr, cols), lambda i: (0, i, 0)), spec, spec, spec],
        out_specs=[spec, spec, spec, spec], out_shape=[shape, shape, shape, shape],
        compiler_params=_params(("parallel",)),
    )(pieces, w, m, v)


def _sum8(pieces, rows, *, name):
    cols = pieces.shape[2]
    tr = _tile(rows, (176, 96))

    def body(p_ref, g_ref):
        g_ref[...] = _sum_pieces(p_ref)

    return pl.pallas_call(
        body, name=name, grid=(rows // tr,),
        in_specs=[pl.BlockSpec((N_DEV, tr, cols), lambda i: (0, i, 0))],
        out_specs=pl.BlockSpec((tr, cols), lambda i: (i, 0)),
        out_shape=jax.ShapeDtypeStruct((rows, cols), F32),
        compiler_params=_params(("parallel",)),
    )(pieces)


def _adamw_columns(g, w, m, v, *, name):
    n, _, k = w.shape
    tr = n // 2

    def body(g_ref, w_ref, m_ref, v_ref, d_ref, nm_ref, nv_ref):
        d_ref[...], nm_ref[...], nv_ref[...] = _adam_update(g_ref[...], w_ref[...], m_ref[...], v_ref[...])

    spec = pl.BlockSpec((tr, 1, k), lambda i: (i, 0, 0))
    shape = jax.ShapeDtypeStruct((n, 1, k), F32)
    return pl.pallas_call(
        body, name=name, grid=(n // tr,), in_specs=[spec, spec, spec, spec],
        out_specs=[spec, spec, spec], out_shape=[shape, shape, shape],
        compiler_params=_params(("parallel",)),
    )(g, w, m, v)


MESH = pl.DeviceIdType.MESH
ANY = pl.BlockSpec(memory_space=pl.ANY)


def _all_gather(shard, *, name):
    rows, lanes = shard.shape

    def body(x_ref, out_ref, send_sems, recv_sems, local_sem):
        x, y, c = lax.axis_index("x"), lax.axis_index("y"), lax.axis_index("c")
        me, sibling = (x, y, c), (x, y, 1 - c)
        chips = [(1 - x, y), (x, 1 - y), (1 - x, 1 - y)]

        def block(px, py, pc):
            return out_ref.at[4 * px + 2 * py + pc]

        def copy(k, blk, to, src=None):
            return pltpu.make_async_remote_copy(
                src_ref=block(*blk) if src is None else src, dst_ref=block(*blk),
                send_sem=send_sems.at[k], recv_sem=recv_sems.at[k], device_id=to, device_id_type=MESH)

        mine = pltpu.make_async_copy(x_ref, block(*me), local_sem)
        mine.start()
        first = [copy(0, me, sibling, src=x_ref)]
        first += [copy(1 + j, me, (*chip, c), src=x_ref) for j, chip in enumerate(chips)]
        for cp in first:
            cp.start()
        passed = [copy(4 + j, (*chip, c), sibling) for j, chip in enumerate(chips)]
        for j, chip in enumerate(chips):
            copy(1 + j, (*chip, c), me).wait_recv()
            passed[j].start()
        copy(0, sibling, me).wait_recv()
        for j, chip in enumerate(chips):
            copy(4 + j, (*chip, 1 - c), me).wait_recv()
        for cp in first + passed:
            cp.wait_send()
        mine.wait()

    return pl.pallas_call(
        body, name=name, out_shape=jax.ShapeDtypeStruct((N_DEV, rows, lanes), shard.dtype),
        in_specs=[ANY], out_specs=ANY,
        scratch_shapes=[pltpu.SemaphoreType.DMA((7,)), pltpu.SemaphoreType.DMA((7,)), pltpu.SemaphoreType.DMA(())],
    )(shard)


def _peer_copies(kind, src_ref, out_ref, send_sems, recv_sems, local_sem):
    x, y, c = lax.axis_index("x"), lax.axis_index("y"), lax.axis_index("c")
    me = 4 * x + 2 * y + c

    def src(idx):
        return src_ref.at[idx] if kind == "exchange" else src_ref

    mine = None if local_sem is None else pltpu.make_async_copy(src(me), out_ref.at[me], local_sem)
    copies = []
    for r in (2, 4, 6) if kind == "across" else range(1, N_DEV):
        px = 1 - x if r & 4 else x
        py = 1 - y if r & 2 else y
        pc = 1 - c if r & 1 else c
        copies.append(pltpu.make_async_remote_copy(
            src_ref=src(4 * px + 2 * py + pc), dst_ref=out_ref.at[me],
            send_sem=send_sems.at[r - 1], recv_sem=recv_sems.at[r - 1],
            device_id=(px, py, pc), device_id_type=MESH))
    return mine, copies


def _to_other_core(shard, land, *, name):
    def body(src_ref, land_ref, out_ref, send_sems, recv_sems):
        x, y, c = lax.axis_index("x"), lax.axis_index("y"), lax.axis_index("c")
        copies = []
        for k, r in enumerate((0, 2, 4, 6)):
            slot = 4 * (1 - x if r & 4 else x) + 2 * (1 - y if r & 2 else y) + c
            copies.append(pltpu.make_async_remote_copy(
                src_ref=src_ref if r == 0 else land_ref.at[slot], dst_ref=out_ref.at[slot],
                send_sem=send_sems.at[k], recv_sem=recv_sems.at[k], device_id=(x, y, 1 - c), device_id_type=MESH))
        for cp in copies:
            cp.start()
        for cp in copies:
            cp.wait()

    return pl.pallas_call(
        body, name=name, out_shape=jax.ShapeDtypeStruct(land.shape, land.dtype), in_specs=[ANY, ANY], out_specs=ANY,
        input_output_aliases={1: 0}, scratch_shapes=[pltpu.SemaphoreType.DMA((4,)), pltpu.SemaphoreType.DMA((4,))],
    )(shard, land)


PEER_SEMS = [pltpu.SemaphoreType.DMA((7,)), pltpu.SemaphoreType.DMA((7,)), pltpu.SemaphoreType.DMA(())]


HBM = pl.BlockSpec(memory_space=pltpu.HBM)
SEMAPHORES = pl.BlockSpec(memory_space=pltpu.SEMAPHORE)


def _peer_start(kind, arr, *, name):
    land = lax.empty((N_DEV,) + arr.shape[-2:], arr.dtype)

    def body(src_ref, land_ref, send_sems, recv_sems, src_thru, land_thru, token):
        _, copies = _peer_copies(kind, src_ref, land_ref, send_sems, recv_sems, None)
        for cp in copies:
            cp.start()
        token[...] = jnp.zeros_like(token)

    return pl.pallas_call(
        body, name=name,
        out_shape=(pltpu.SemaphoreType.DMA((N_DEV - 1,)), pltpu.SemaphoreType.DMA((N_DEV - 1,)),
                   pltpu.HBM(arr.shape, arr.dtype), pltpu.HBM(land.shape, land.dtype), jax.ShapeDtypeStruct((8, LANES), F32)),
        in_specs=(HBM, HBM), out_specs=(SEMAPHORES, SEMAPHORES, HBM, HBM, pl.BlockSpec(memory_space=pltpu.VMEM)),
        input_output_aliases={0: 2, 1: 3},
        compiler_params=pltpu.CompilerParams(has_side_effects=pltpu.SideEffectType.DATAFLOW_SIDE_EFFECTING),
    )(pltpu.with_memory_space_constraint(arr, pltpu.HBM), pltpu.with_memory_space_constraint(land, pltpu.HBM))


def _peer_wait(kind, send_sems, recv_sems, src_thru, land_thru, after, *, name):
    def body(src_ref, land_ref, send_sems, recv_sems, *_):
        _, copies = _peer_copies(kind, src_ref, land_ref, send_sems, recv_sems, None)
        for cp in copies:
            cp.wait_send()
            cp.wait_recv()

    return pl.pallas_call(
        body, name=name,
        out_shape=(pltpu.HBM(src_thru.shape, src_thru.dtype), pltpu.HBM(land_thru.shape, land_thru.dtype)),
        in_specs=(HBM, HBM, SEMAPHORES, SEMAPHORES) + (ANY,) * len(after), out_specs=(HBM, HBM),
        input_output_aliases={0: 0, 1: 1},
        compiler_params=pltpu.CompilerParams(has_side_effects=pltpu.SideEffectType.DATAFLOW_SIDE_EFFECTING),
    )(src_thru, land_thru, send_sems, recv_sems, *after)


def _add_rider(rider, in_specs, args, out_specs, out_shape):
    if rider is None:
        return []
    _, arr = rider
    in_specs.append(ANY)
    args.append(arr)
    out_specs.append(ANY)
    out_shape.append(jax.ShapeDtypeStruct((N_DEV,) + arr.shape[-2:], arr.dtype))
    return list(PEER_SEMS)


def _split_rider(refs, rider, n_in, n_out):
    if rider is None:
        return refs, None
    refs = list(refs)
    rin = refs.pop(n_in)
    rout = refs.pop(n_in + n_out)
    return refs[:-3], (rin, rout, *refs[-3:])


def _ride_start(rider, ride_refs, first):
    if rider is None:
        return

    @pl.when(first)
    def _():
        mine, copies = _peer_copies(rider[0], *ride_refs)
        mine.start()
        for cp in copies:
            cp.start()


def _ride_wait(rider, ride_refs, last):
    if rider is None:
        return

    @pl.when(last)
    def _():
        mine, copies = _peer_copies(rider[0], *ride_refs)
        for cp in copies:
            cp.wait()
        mine.wait()


def _gathered_cols(blocks, kdim):
    n = blocks.shape[1] * WIDE // kdim
    return blocks.reshape(N_DEV, kdim, n).transpose(1, 0, 2).reshape(kdim, N_DEV * n)


def _scatter_cols(dw):
    kdim, n8 = dw.shape
    n = n8 // N_DEV
    return dw.reshape(kdim, N_DEV, n).transpose(1, 0, 2).reshape(N_DEV, kdim * n // WIDE, WIDE)


def _pad_rows(a, rows):
    pad = [(0, 0)] * a.ndim
    pad[-2] = (0, rows - a.shape[-2])
    return jnp.pad(a, pad)


def _layer0_in_weight_t(wt):
    cq, ckv, kpe = wt[0:256], wt[256:384], wt[384:416]
    q_s, k_s, v_s, gate = wt[416:928], wt[928:1056], wt[1056:1184], wt[1184:2208]
    z = jnp.zeros((64, wt.shape[1]), wt.dtype)
    return jnp.concatenate([gate, cq, ckv, z, kpe, z[:32], q_s, k_s, v_s], axis=0)


def _layer0_in_grad_t(dwt):
    gate, cq, ckv, kpe = dwt[0:1024], dwt[1024:1280], dwt[1280:1408], dwt[1472:1504]
    q_s, k_s, v_s = dwt[1536:2048], dwt[2048:2176], dwt[2176:2304]
    return jnp.concatenate([cq, ckv, kpe, q_s, k_s, v_s, gate], axis=0)


L0_BLOCKS = ((256, 1024), (128, 1280), (32, 1472), (512, 1536), (128, 2048), (128, 2176), (1024, 0))


def _layer0_in_unpack(gath, *, name):
    total = (Z0A_UNITS + Z0B_UNITS) * LANES

    def body(g_ref, w_ref):
        w_ref[1408:1472, :] = jnp.zeros((64, WIDE), w_ref.dtype)
        w_ref[1504:1536, :] = jnp.zeros((32, WIDE), w_ref.dtype)
        for p in range(N_DEV):
            lo, hi, at = p * N_E_IN, (p + 1) * N_E_IN, 0
            for rows, first in L0_BLOCKS:
                start, stop = max(lo, at), min(hi, at + rows)
                if start < stop:
                    w_ref[first + start - at:first + stop - at, :] = g_ref[p, start - lo:stop - lo, :]
                at += rows

    return pl.pallas_call(
        body, name=name, grid=(1,), in_specs=[_resident((N_DEV, RA0, WIDE), lambda i: (0, 0, 0))],
        out_specs=_resident((total, WIDE), lambda i: (0, 0)), out_shape=jax.ShapeDtypeStruct((total, WIDE), gath.dtype),
        compiler_params=_params(("arbitrary",)),
    )(gath)


def _early_grads_pack(d_w0t, d_q, d_kv, *, name):
    def body(w_ref, q_ref, kv_ref, out_ref):
        for p in range(N_DEV):
            lo, hi, at = p * N_E_IN, (p + 1) * N_E_IN, 0
            for rows, first in L0_BLOCKS:
                start, stop = max(lo, at), min(hi, at + rows)
                if start < stop:
                    out_ref[p, start - lo:stop - lo, :] = w_ref[first + start - at:first + stop - at, :]
                at += rows
            out_ref[p, N_E_IN:RA0, :] = jnp.zeros((RA0 - N_E_IN, WIDE), out_ref.dtype)
            out_ref[p, RA0:RA0 + 32, :] = q_ref[p]
            out_ref[p, RA0 + 32:, :] = kv_ref[p]

    arrays = (d_w0t, d_q, d_kv)
    return pl.pallas_call(
        body, name=name, grid=(1,), in_specs=[_resident(a.shape, lambda i, n=a.ndim: (0,) * n) for a in arrays],
        out_specs=_resident((N_DEV, RA0 + RB0, WIDE), lambda i: (0, 0, 0)),
        out_shape=jax.ShapeDtypeStruct((N_DEV, RA0 + RB0, WIDE), BF16), compiler_params=_params(("arbitrary",)),
    )(*arrays)


def _layer1_in_weight_t(wt):
    main = jnp.concatenate([wt[:3 * D_MODEL], wt[3 * D_MODEL + FOX_HEADS:]], axis=0)
    return main, _pad_rows(wt[3 * D_MODEL:3 * D_MODEL + FOX_HEADS], LANES)


def _layer1_in_unpack(gath, *, name):
    n_main = 3 * D_MODEL

    def body(g_ref, w_ref, f_ref, o1_ref, o0_ref):
        f_ref[...] = jnp.zeros_like(f_ref)
        for p in range(N_DEV):
            o1_ref[128 * p:128 * p + 128, :] = g_ref[p, RA1:RA1 + 128, :]
            o0_ref[128 * p:128 * p + 128, :] = g_ref[p, RA1 + 128:RA1 + 256, :]
            lo, hi = p * N_O_IN, (p + 1) * N_O_IN
            for ref, first, start, stop in ((w_ref, 0, lo, min(hi, n_main)),
                                            (f_ref, -n_main, max(lo, n_main), min(hi, n_main + FOX_HEADS)),
                                            (w_ref, -FOX_HEADS, max(lo, n_main + FOX_HEADS), hi)):
                if start < stop:
                    ref[start + first:stop + first, :] = g_ref[p, start - lo:stop - lo, :]

    return pl.pallas_call(
        body, name=name, grid=(1,), in_specs=[_resident((N_DEV, RA1 + 256, WIDE), lambda i: (0, 0, 0))],
        out_specs=[_resident((rows, WIDE), lambda i: (0, 0)) for rows in (n_main + D_MODEL, LANES, D_MODEL, D_MODEL)],
        out_shape=[jax.ShapeDtypeStruct((rows, WIDE), gath.dtype) for rows in (n_main + D_MODEL, LANES, D_MODEL, D_MODEL)],
        compiler_params=_params(("arbitrary",)),
    )(gath)


def _late_grads_pack(d_qkv, d_wft, d_gate, d_wo1, d_wo0, d_o_g, *, name):
    n_main = 3 * D_MODEL
    arrays = (d_qkv, d_wft, d_gate, d_wo1, d_wo0, d_o_g)

    def body(q_ref, f_ref, g_ref, o1_ref, o0_ref, og_ref, out_ref):
        for p in range(N_DEV):
            lo, hi = p * N_O_IN, (p + 1) * N_O_IN
            for ref, first, start, stop in ((q_ref, 0, lo, min(hi, n_main)),
                                            (f_ref, -n_main, max(lo, n_main), min(hi, n_main + FOX_HEADS)),
                                            (g_ref, -n_main - FOX_HEADS, max(lo, n_main + FOX_HEADS), hi)):
                if start < stop:
                    out_ref[p, start - lo:stop - lo, :] = ref[start + first:stop + first, :]
            out_ref[p, N_O_IN:RA1, :] = jnp.zeros((RA1 - N_O_IN, WIDE), out_ref.dtype)
            out_ref[p, RA1:RA1 + 128, :] = o1_ref[128 * p:128 * p + 128, :]
            out_ref[p, RA1 + 128:RA1 + 256, :] = o0_ref[128 * p:128 * p + 128, :]
            out_ref[p, RA1 + 256:, :] = og_ref[p]

    return pl.pallas_call(
        body, name=name, grid=(1,), in_specs=[_resident(a.shape, lambda i, n=a.ndim: (0,) * n) for a in arrays],
        out_specs=_resident((N_DEV, RA1 + RB1, WIDE), lambda i: (0, 0, 0)),
        out_shape=jax.ShapeDtypeStruct((N_DEV, RA1 + RB1, WIDE), BF16), compiler_params=_params(("arbitrary",)),
    )(*arrays)


def _q_up_weight(w):
    return jnp.pad(w.reshape(MLA_Q_RANK, MLA_HEADS, 96), ((0, 0), (0, 0), (0, 32))).reshape(MLA_Q_RANK, MLA_HEADS * LANES)


def _q_up_grad(dwp):
    return dwp.reshape(MLA_Q_RANK, MLA_HEADS, LANES)[:, :, :96].reshape(MLA_Q_RANK, MLA_HEADS * 96)


def _kv_up_weight(w):
    w4 = w.reshape(MLA_KV_RANK, MLA_HEADS, 2, 64)
    kp = jnp.pad(w4[:, :, 0, :], ((0, 0), (0, 0), (0, 64))).reshape(MLA_KV_RANK, MLA_HEADS * LANES)
    vp = w4[:, :, 1, :].reshape(MLA_KV_RANK, MLA_HEADS * 64)
    return jnp.concatenate([kp, vp], axis=1)


def _kv_up_grad(dwp):
    dk = dwp[:, :MLA_HEADS * LANES].reshape(MLA_KV_RANK, MLA_HEADS, LANES)[:, :, :64]
    dv = dwp[:, MLA_HEADS * LANES:].reshape(MLA_KV_RANK, MLA_HEADS, 64)
    return jnp.stack([dk, dv], axis=2).reshape(MLA_KV_RANK, MLA_HEADS * LANES)


def _pad_lanes(a):
    return jnp.pad(a, ((0, 0), (0, LANES - a.shape[1])))


def _small_pack(g_in, g_final, g_q_a, g_kv_a, sinks, b_f, loss):
    rows = [g_in.reshape(8, LANES), g_final.reshape(8, LANES), g_q_a.reshape(2, LANES), g_kv_a.reshape(1, LANES),
            _pad_lanes(sinks.reshape(1, -1)), _pad_lanes(b_f.reshape(1, -1)), _pad_lanes(loss.reshape(1, 1)),
            jnp.zeros((2, LANES), F32)]
    return jnp.concatenate(rows, axis=0)


def _small_unpack(a):
    return (a[0:8].reshape(1, D_MODEL), a[8:16].reshape(D_MODEL), a[16:18].reshape(1, MLA_Q_RANK),
            a[18:19].reshape(1, MLA_KV_RANK), a[19:20, :SWA_HEADS], a[20:21, :FOX_HEADS], a[21, 0])


def _local_step(x, positions, target, e_g_in, early, e_g_q_a, e_g_kv_a, e_sinks,
                late, o_b_f, g_final, scatter1=None, scatter0=None):
    s = x.shape[0]
    mla_scale = (MLA_NOPE + MLA_ROPE) ** -0.5
    fox_scale = FOX_DIM ** -0.5
    n0a = Z0A_UNITS * LANES

    inv_freq = 1.0 / (ROPE_THETA ** (jnp.arange(0, MLA_ROPE, 2, dtype=F32) / MLA_ROPE))
    ang = positions.astype(F32)[:, None] * inv_freq
    cos, sin = jnp.cos(ang), jnp.sin(ang)
    ones, zeros = jnp.ones((s, 64), F32), jnp.zeros((s, 64), F32)
    cos_t = jnp.concatenate([ones, cos, cos, ones[:, :32]], axis=1)
    sin_t = jnp.concatenate([zeros, -sin, sin, zeros[:, :32]], axis=1)
    cos_t, sin_t = lax.optimization_barrier((cos_t, sin_t))

    if len(early) == 3:
        h0 = _rmsnorm_fwd(x, e_g_in, width=D_MODEL, col_blk=0, name="l0_norm")
        w0t, wq, wkv = early
    else:
        pending, token, unpack, prep = early
        h0 = _rmsnorm_fwd(x, e_g_in, width=D_MODEL, col_blk=0, name="l0_norm", after=[token])
        sent, across = _peer_wait("across", *pending, after=[h0] + prep, name="weights0_wait")
        w0t, wq, wkv = unpack(sent, _to_other_core(sent, across, name="weights0_over"))
    z0a, z0b = _matmul_rows([(h0, w0t, True)], [], [], lambda r: (r[:, :n0a], r[:, n0a:]),
                            [("rows", n0a, F32), ("rows", Z0B_UNITS * LANES, BF16)], name="l0_in")
    cqn = _rmsnorm_fwd(z0a, e_g_q_a, width=MLA_Q_RANK, col_blk=4, name="l0_q_norm")
    ckvn = _rmsnorm_fwd(z0a, e_g_kv_a, width=MLA_KV_RANK, col_blk=10, name="l0_kv_norm")
    rope_rows = [(cos_t, LANES, 0), (sin_t, LANES, 0)]
    qm, = _matmul_rows([(cqn, wq, False)], rope_rows, [], _rope_q_epilogue, [("rows", MLA_HEADS * LANES, BF16)],
                       name="l0_q_up")
    kvm, = _matmul_rows([(ckvn, wkv, False)], [(z0a, LANES, 11)] + rope_rows, [], _rope_k_epilogue,
                        [("rows", MLA_HEADS * (LANES + MLA_V), BF16)], name="l0_kv_up")
    gathers = len(late) == 2
    res = _flash_fwd(qm, kvm, kvm, None, n_pairs=MLA_HEADS // 2, hw=LANES, q_off=0, k_off=0, v_off=MLA_HEADS,
                     scale=mla_scale, name="l0_mla_fwd", rider=("gather", late[0]) if gathers else None)
    o_mla, lse_mla = res[0], res[1]
    wo0, o_g_in, w1t, wft, wo1 = late[1](res[2]) if gathers else late
    o_swa, lse_swa = _swa_fwd(z0b, e_sinks, name="l0_swa_fwd")
    half = D_MODEL // 2

    x1, h1, og0 = _matmul_rows(
        [(None, wo0, False)], [(o_mla, half, 0), (o_swa, half, 0), (z0a, D_MODEL, 0), (x, D_MODEL, 0)], [o_g_in],
        lambda r, om, osw, gt, xt, g, made: (*_residual_norm_epilogue(r, xt, g), made),
        [("rows", D_MODEL, F32), ("rows", D_MODEL, BF16), ("rows", D_MODEL, BF16)], name="l0_out",
        prologue=lambda om, osw, gt, xt, g: _gated([om, osw], gt))
    z1, gate1, zf = _matmul_rows(
        [(None, w1t, True), (None, wft, True)], [(h1, D_MODEL, 0)], [],
        lambda r, h, made: (r[0][:, :3 * D_MODEL], r[0][:, 3 * D_MODEL:], r[1]),
        [("rows", 3 * D_MODEL, BF16), ("rows", D_MODEL, F32), ("rows", LANES, F32)], name="l1_in",
        prologue=lambda h: h, separate=True)
    bf = _pad_lanes(o_b_f)
    log_cum = _logf_fwd(zf, bf, name="l1_logf")
    bias2 = (-LOG2E * log_cum[:, :FOX_HEADS]).T
    t_bwd = min(ATT_T, s)
    bias = bias2.reshape(FOX_HEADS // 2, 2, s // t_bwd, 1, t_bwd)
    t_fwd = _fwd_tile(s)
    o_fox, lse_fox = _flash_fwd(z1, z1, z1, bias2.reshape(FOX_HEADS // 2, 2, s // t_fwd, 1, t_fwd),
                                n_pairs=FOX_HEADS // 2, hw=64, q_off=0, k_off=8, v_off=16, scale=fox_scale,
                                name="l1_fox_fwd")

    dx2, loss_part, d_g_final, og1, dx2_bf = _matmul_rows(
        [(None, wo1, False)], [(o_fox, D_MODEL, 0), (gate1, D_MODEL, 0), (x1, D_MODEL, 0), (target, D_MODEL, 0)],
        [g_final.reshape(1, D_MODEL)],
        lambda r, o, gt, xt, tg, g, made: _and_first(_loss_epilogue(r, xt, tg, g), made),
        [("rows", D_MODEL, F32), ("sum", (8, LANES)), ("sum", (1, D_MODEL)), ("rows", D_MODEL, BF16),
         ("rows", D_MODEL, BF16)], name="l1_out_loss", prologue=lambda o, gt, xt, tg, g: _gated([o], gt))

    d_wo1 = _matmul(og1, dx2_bf, ta=True, out_dtype=BF16, name="l1_out_dw")
    do_fox, d_gate1 = _matmul_rows([(dx2_bf, wo1, True)], [(o_fox, D_MODEL, 0), (gate1, D_MODEL, 0)], [],
                                   _gate_bwd_epilogue([D_MODEL]), [("rows", D_MODEL, F32), ("rows", D_MODEL, BF16)],
                                   name="l1_out_dx")
    dqkv1, dbias, drow = _flash_bwd(z1, z1, z1, do_fox, o_fox, lse_fox, bias, n_pairs=FOX_HEADS // 2, hw=64, q_off=0,
                                    k_off=8, v_off=16, scale=fox_scale, qk_dtype=BF16, stacked=True, name="l1_fox_bwd")
    d_log_cum = (drow.reshape(FOX_HEADS, s) - dbias.reshape(FOX_HEADS, s)).T
    d_log_cum = jnp.pad(d_log_cum, ((0, 0), (0, LANES - FOX_HEADS)))
    d_zf, d_bf = _logf_bwd(d_log_cum, zf, bf, name="l1_logf_bwd")
    d_w1t = (_matmul(dqkv1, h1, ta=True, out_dtype=BF16, name="l1_in_dw_qkv"),
             _matmul(d_gate1, h1, ta=True, out_dtype=BF16, name="l1_in_dw_gate"))
    d_wft = _matmul(d_zf, h1, ta=True, out_dtype=BF16, name="l1_in_f_dw")
    dx1, d_o_g_in, dx1_bf = _matmul_rows([(dqkv1, w1t, False, c * D_MODEL, c) for c in range(3)]
                                         + [(d_gate1, w1t, False, 3 * D_MODEL), (d_zf, wft, False)],
                                         [(x1, D_MODEL, 0), (dx2, D_MODEL, 0)], [o_g_in],
                                         lambda *a: _and_first(_rms_bwd_epilogue(*a)),
                                         [("rows", D_MODEL, F32), ("sum", (1, D_MODEL)), ("rows", D_MODEL, BF16)],
                                         name="l1_in_dx")

    d_wo0 = _matmul(og0, dx1_bf, ta=True, out_dtype=BF16, name="l0_out_dw")
    do_mla, do_swa, d_gate0 = _matmul_rows(
        [(dx1_bf, wo0, True)], [(o_mla, half, 0), (o_swa, half, 0), (z0a, D_MODEL, 0)], [], _gate_bwd_epilogue([half, half]),
        [("rows", half, F32), ("rows", half, F32), ("rows", D_MODEL, BF16)], name="l0_out_dx")
    dq_s, dkt_s, dvt_s, d_sinks = _swa_bwd(z0b, e_sinks, do_swa, o_swa, lse_swa, name="l0_swa_bwd")
    dk_s = dkt_s.transpose(0, 2, 1).reshape(s, LANES)
    dv_s = dvt_s.transpose(0, 2, 1).reshape(s, LANES)
    rider = None
    if scatter1 is not None:
        rider = ("exchange", scatter1(dict(w1t=d_w1t, wft=d_wft, wo1=d_wo1, o_g_in=d_o_g_in, wo0=d_wo0)))
    res = _flash_bwd(qm, kvm, kvm, do_mla, o_mla, lse_mla, None, n_pairs=MLA_HEADS // 2, hw=LANES, q_off=0, k_off=0,
                     v_off=MLA_HEADS, scale=mla_scale, qk_dtype=F32, name="l0_mla_bwd", rider=rider)
    dqm, dkm, dvm = res[0], res[1], res[2]
    recv1 = res[3] if rider is not None else None
    d_qp, d_kvp, d_kpe = _rope_bwd(dqm, dkm, dvm, cos_t, sin_t, name="l0_rope_bwd")
    d_wq = _matmul(cqn, d_qp, ta=True, out_dtype=BF16, name="l0_q_up_dw")
    d_cqn = _matmul(d_qp, wq, tb=True, name="l0_q_up_dx")
    d_wkv = _matmul(ckvn, d_kvp, ta=True, out_dtype=BF16, name="l0_kv_up_dw")
    d_ckvn = _matmul(d_kvp, wkv, tb=True, name="l0_kv_up_dx")
    d_cq, d_g_q_a = _rmsnorm_bwd(z0a, e_g_q_a, d_cqn, width=MLA_Q_RANK, col_blk=4, name="l0_q_norm_bwd")
    d_ckv, d_g_kv_a = _rmsnorm_bwd(z0a, e_g_kv_a, d_ckvn, width=MLA_KV_RANK, col_blk=10, name="l0_kv_norm_bwd")
    dz0 = jnp.concatenate([d_gate0, d_cq, d_ckv, d_kpe, dq_s.astype(BF16), dk_s.astype(BF16), dv_s.astype(BF16)], axis=1)
    d_w0t = _matmul(dz0, h0, ta=True, out_dtype=BF16, name="l0_in_dw")
    pending0, after_start = None, []
    if scatter0 is not None:
        *pending0, token = _peer_start("exchange", scatter0(dict(w0t=d_w0t, wq=d_wq, wkv=d_wkv)), name="grads0_start")
        after_start = [token]
    grad_x, d_e_g_in = _matmul_rows(
        [(dz0, w0t, False)], [(x, D_MODEL, 0), (dx1, D_MODEL, 0)], [e_g_in] + after_start,
        lambda dy, xt, add, g, *_: _rms_bwd_epilogue(dy, xt, add, g),
        [("rows", D_MODEL, F32), ("sum", (1, D_MODEL))], name="l0_in_dx")

    return dict(pending0=pending0, recv1=recv1, loss=loss_part[0, 0], grad_x=grad_x, e_g_in=d_e_g_in, w0t=d_w0t, e_g_q_a=d_g_q_a, wq=d_wq,
                e_g_kv_a=d_g_kv_a, wkv=d_wkv, e_sinks=d_sinks[:, 0].reshape(1, SWA_HEADS), wo0=d_wo0,
                o_g_in=d_o_g_in, w1t=d_w1t, wft=d_wft, o_b_f=d_bf[:, :FOX_HEADS], wo1=d_wo1, g_final=d_g_final.reshape(D_MODEL))


def _wide(a, rows):
    flat = a.reshape(-1)
    return jnp.pad(flat, (0, rows * WIDE - flat.shape[0])).reshape(rows, WIDE)


def _rows_b0(w_q, w_kv):
    return jnp.concatenate([_wide(w_q, 32), _wide(w_kv, 16)], axis=0)


def _unflat_b0(f):
    return f[0:24].reshape(1, MLA_Q_RANK, 96), f[32:48].reshape(1, MLA_KV_RANK, 128)


def _rows_b1(o_w_out, e_w_out, g_in):
    return jnp.concatenate([o_w_out, e_w_out, _wide(g_in, 16)], axis=0)


def _unflat_b1(f):
    return f[0:128][None], f[128:256][None], f[256:257, :LANES]


def kernel(x, positions, e_g_in, e_w_in, e_g_q_a, e_w_q_up, e_g_kv_a, e_w_kv_up, e_sinks, e_w_out, o_g_in, o_w_in, o_b_f, o_w_out, g_final, loss_target, m_e_g_in, m_e_w_in, m_e_g_q_a, m_e_w_q_up, m_e_g_kv_a, m_e_w_kv_up, m_e_sinks, m_e_w_out, m_o_g_in, m_o_w_in, m_o_b_f, m_o_w_out, m_g_final, v_e_g_in, v_e_w_in, v_e_g_q_a, v_e_w_q_up, v_e_g_kv_a, v_e_w_kv_up, v_e_sinks, v_e_w_out, v_o_g_in, v_o_w_in, v_o_b_f, v_o_w_out, v_g_final):
    def bf(a):
        return a.astype(BF16)

    me = 4 * lax.axis_index("x") + 2 * lax.axis_index("y") + lax.axis_index("c")
    shard0 = jnp.concatenate([_pad_rows(bf(e_w_in[0]).T, RA0), _rows_b0(bf(e_w_q_up[0]), bf(e_w_kv_up[0]))], axis=0)
    *pending_w0, token_w0 = _peer_start("across", shard0, name="weights0_start")

    def unpack0(sent, gath0):
        gath0 = lax.dynamic_update_slice_in_dim(gath0, sent[None], me, axis=0)
        w0t = _layer0_in_unpack(gath0, name="weights0_unpack")
        wq = _q_up_weight(_gathered_cols(gath0[:, RA0:RA0 + 24], MLA_Q_RANK))
        wkv = _kv_up_weight(_gathered_cols(gath0[:, RA0 + 32:RA0 + 48], MLA_KV_RANK))
        return w0t, wq, wkv

    rows_b0 = [_rows_b0(q[0], kv[0]) for q, kv in ((e_w_q_up, e_w_kv_up), (m_e_w_q_up, m_e_w_kv_up), (v_e_w_q_up, v_e_w_kv_up))]
    rows_b1 = [_rows_b1(o[0], e[0], g) for o, e, g in ((o_w_out, e_w_out, o_g_in), (m_o_w_out, m_e_w_out, m_o_g_in),
                                                       (v_o_w_out, v_e_w_out, v_o_g_in))]

    g_bits = lax.bitcast_convert_type(o_g_in.reshape(LANES), BF16)
    shard1 = jnp.concatenate([_pad_rows(bf(o_w_in[0]).T, RA1), _rows_b1(bf(o_w_out[0]), bf(e_w_out[0]), g_bits)], axis=0)

    def unpack1(gath1):
        w1t, wft, wo1, wo0 = _layer1_in_unpack(gath1, name="weights1_unpack")
        bits = gath1[:, RA1 + 256, :2 * LANES].reshape(N_DEV, LANES, 2)
        return wo0, lax.bitcast_convert_type(bits, F32).reshape(1, D_MODEL), w1t, wft, wo1

    def scatter1(g):
        d_o_g = jnp.pad(bf(g["o_g_in"]).reshape(N_DEV, 1, LANES), ((0, 0), (0, 15), (0, WIDE - LANES)))
        return _late_grads_pack(g["w1t"][0], g["wft"], g["w1t"][1], g["wo1"], g["wo0"], d_o_g, name="grads1_pack")

    def scatter0(g):
        return _early_grads_pack(g["w0t"], _pad_rows(_scatter_cols(_q_up_grad(g["wq"])), 32),
                                 _scatter_cols(_kv_up_grad(g["wkv"])), name="grads0_pack")

    gr = _local_step(x[0], positions[0], loss_target[0], e_g_in,
                     (pending_w0, token_w0, unpack0, [shard1] + rows_b0 + rows_b1), e_g_q_a, e_g_kv_a, e_sinks,
                     (shard1, unpack1), o_b_f, g_final, scatter1=scatter1, scatter0=scatter0)

    def in_projection(recv, ra, n, w, m, v, name):
        g = _sum8(recv, ra, name=name + "_grad_sum")[:n].reshape(n, 1, D_MODEL)
        w, m, v = [jnp.transpose(a, (2, 0, 1)) for a in (w, m, v)]
        return (g, *_adamw_columns(g, w, m, v, name=name + "_adamw"))

    o_in = in_projection(gr["recv1"], RA1, N_O_IN, o_w_in, m_o_w_in, v_o_w_in, "o_w_in")
    b1 = _adamw(gr["recv1"][:, RA1:], *rows_b1, name="adamw_late")

    small = _small_pack(gr["e_g_in"], gr["g_final"], gr["e_g_q_a"], gr["e_g_kv_a"], gr["e_sinks"], gr["o_b_f"], gr["loss"])
    small_all = _all_gather(small, name="small_all_gather")
    zero = jnp.zeros((), F32)
    w_small = _small_pack(e_g_in, g_final, e_g_q_a, e_g_kv_a, e_sinks, o_b_f, zero)
    m_small = _small_pack(m_e_g_in, m_g_final, m_e_g_q_a, m_e_g_kv_a, m_e_sinks, m_o_b_f, zero)
    v_small = _small_pack(v_e_g_in, v_g_final, v_e_g_q_a, v_e_g_kv_a, v_e_sinks, v_o_b_f, zero)
    smalls = _adamw(small_all, w_small, m_small, v_small, name="adamw_replicated")
    g_sm, d_sm, m_sm, v_sm = [_small_unpack(a) for a in smalls]
    loss = g_sm[6]

    sent0, recv0 = _peer_wait("exchange", *gr["pending0"], after=[o_in[1], b1[1], smalls[1]], name="grads0_wait")
    own = lax.dynamic_slice_in_dim(sent0, me, 1, axis=0)
    recv0 = lax.dynamic_update_slice_in_dim(recv0, own, me, axis=0)
    e_in = in_projection(recv0, RA0, N_E_IN, e_w_in, m_e_w_in, v_e_w_in, "e_w_in")
    b0 = _adamw(recv0[:, RA0:], *rows_b0, name="adamw_early")

    def sharded(k):
        q_up, kv_up = _unflat_b0(b0[k])
        o_out, e_out, o_g = _unflat_b1(b1[k])
        return jnp.transpose(e_in[k], (1, 2, 0)), q_up, kv_up, e_out, jnp.transpose(o_in[k], (1, 2, 0)), o_out, o_g

    g_sh, d_sh, m_sh, v_sh = [sharded(k) for k in range(4)]

    def leaves(sh, sm):
        return (sm[0], sh[0], sm[2], sh[1], sm[3], sh[2], sm[4], sh[3], sh[6], sh[4], sm[5], sh[5], sm[1])

    return (loss, gr["grad_x"][None], *leaves(g_sh, g_sm), *leaves(d_sh, d_sm), *leaves(m_sh, m_sm), *leaves(v_sh, v_sm))
```

```python
import functools

import jax
import jax.numpy as jnp
from jax import lax
from jax.experimental import pallas as pl
from jax.experimental.pallas import tpu as pltpu

F32 = jnp.float32
BF16 = jnp.bfloat16
NEG_INF = float("-inf")

N_DEV = 8
LANES = 128
D_MODEL = 1024
EPS = 1e-6
ROPE_THETA = 10000.0
MLA_HEADS = 8
MLA_Q_RANK = 256
MLA_KV_RANK = 128
MLA_NOPE = 64
MLA_ROPE = 32
MLA_V = 64
SWA_HEADS = 8
SWA_KV_HEADS = 2
SWA_DIM = 64
WINDOW = 128
FOX_HEADS = 16
FOX_DIM = 64

ADAM_LR = 0.001
ADAM_B1 = 0.9
ADAM_B2 = 0.999
ADAM_EPS = 1e-08
ADAM_WD = 0.01
ADAM_STEP = 10

ATT_T = 512
ATT_T_FWD = 1024
VMEM_LIMIT = 56 * 1024 * 1024
MATMUL_B_BLOCK_BYTES = 8 * 1024 * 1024

Z0A_UNITS = 12
Z0B_UNITS = 6

WIDE = 1024
N_E_IN = 276
N_O_IN = 514
RA0 = 288
RB0 = 32 + 16
RA1 = 528
RB1 = 128 + 128 + 16
SMALL_ROWS = 24


def _tile(n, cands):
    for c in cands:
        if n % c == 0:
            return c
    raise ValueError(f"no tile for {n}")


ROW_TILES = (512, 256, 128)


def _params(sem, vmem=VMEM_LIMIT):
    return pltpu.CompilerParams(dimension_semantics=sem, vmem_limit_bytes=vmem)


def _matmul(a, b, *, name, ta=False, tb=False, out_dtype=F32):
    if ta:
        kdim, m = a.shape[-2], a.shape[-1] * (a.shape[0] if a.ndim == 3 else 1)
    else:
        m, kdim = a.shape
    if tb:
        n, kb = b.shape
    else:
        kb, n = b.shape
    assert kdim == kb, (a.shape, b.shape)
    tm = _tile(m, (512, 256, 128))
    tn = _tile(n, [c for c in (1024, 768, 512, 384, 256, 128) if c * kdim * b.dtype.itemsize <= MATMUL_B_BLOCK_BYTES])
    dims = (((0 if ta else 1,), (1 if tb else 0,)), ((), ()))

    def body(a_ref, b_ref, o_ref):
        r = lax.dot_general(a_ref[...].astype(BF16), b_ref[...].astype(BF16), dims, preferred_element_type=F32)
        o_ref[...] = r.astype(out_dtype)

    if a.ndim == 3:
        per = a.shape[2] // tm
        a_spec = pl.BlockSpec((None, kdim, tm), lambda i, j: (i // per, 0, i % per))
    else:
        a_spec = pl.BlockSpec((kdim, tm), lambda i, j: (0, i)) if ta else pl.BlockSpec((tm, kdim), lambda i, j: (i, 0))
    b_spec = pl.BlockSpec((tn, kdim), lambda i, j: (j, 0)) if tb else pl.BlockSpec((kdim, tn), lambda i, j: (0, j))
    return pl.pallas_call(
        body, name=name, grid=(m // tm, n // tn), in_specs=[a_spec, b_spec],
        out_specs=pl.BlockSpec((tm, tn), lambda i, j: (i, j)), out_shape=jax.ShapeDtypeStruct((m, n), out_dtype),
        compiler_params=_params(("parallel", "parallel")),
    )(a, b)


def _rmsnorm_fwd(x, g, *, width, col_blk, name, after=()):
    s = x.shape[0]
    tm = _tile(s, ROW_TILES)

    def body(x_ref, g_ref, *rest):
        y_ref = rest[-1]
        xf = x_ref[...].astype(F32)
        r = lax.rsqrt(jnp.mean(xf * xf, axis=-1, keepdims=True) + EPS)
        y_ref[...] = ((xf * r) * g_ref[...]).astype(BF16)

    return pl.pallas_call(
        body, name=name, grid=(s // tm,),
        in_specs=[pl.BlockSpec((tm, width), lambda i: (i, col_blk)), pl.BlockSpec((1, width), lambda i: (0, 0))]
        + [ANY] * len(after),
        out_specs=pl.BlockSpec((tm, width), lambda i: (i, 0)),
        out_shape=jax.ShapeDtypeStruct((s, width), BF16),
        compiler_params=_params(("parallel",)),
    )(x, g, *after)


def _rmsnorm_bwd(x, g, dy, *, width, col_blk, name):
    s = x.shape[0]
    tm = _tile(s, ROW_TILES)

    def body(x_ref, g_ref, dy_ref, dx_ref, dg_ref):
        @pl.when(pl.program_id(0) == 0)
        def _():
            dg_ref[...] = jnp.zeros_like(dg_ref)

        dx, dg = _rms_bwd_epilogue(dy_ref[...], x_ref[...], 0.0, g_ref[...])
        dg_ref[...] += dg
        dx_ref[...] = dx.astype(BF16)

    return pl.pallas_call(
        body, name=name, grid=(s // tm,),
        in_specs=[pl.BlockSpec((tm, width), lambda i: (i, col_blk)), pl.BlockSpec((1, width), lambda i: (0, 0)),
                  pl.BlockSpec((tm, width), lambda i: (i, 0))],
        out_specs=[pl.BlockSpec((tm, width), lambda i: (i, 0)), pl.BlockSpec((1, width), lambda i: (0, 0))],
        out_shape=[jax.ShapeDtypeStruct((s, width), BF16), jax.ShapeDtypeStruct((1, width), F32)],
        compiler_params=_params(("arbitrary",)),
    )(x, g, dy)


def _sigmoid(x):
    return 1.0 / (1.0 + jnp.exp(-x))


def _matmul_rows(terms, row_inputs, params, epilogue, outs, *, name, prologue=None, separate=False):
    s = row_inputs[0][0].shape[0] if row_inputs else terms[0][0].shape[-2]
    tm = _tile(s, ROW_TILES)
    steps = s // tm
    n_r, n_p, n_o = len(row_inputs), len(params), len(outs)
    n_t = sum(1 if term[0] is None else 2 for term in terms)

    def body(*refs):
        t_refs, r_refs = list(refs[:n_t]), refs[n_t:n_t + n_r]
        p_refs, o_refs = refs[n_t + n_r:n_t + n_r + n_p], refs[n_t + n_r + n_p:]
        i = pl.program_id(0)
        rows, small = [r[...] for r in r_refs], [p[...] for p in p_refs]
        made = None if prologue is None else prologue(*rows, *small)
        parts = []
        for term in terms:
            a = made if term[0] is None else t_refs.pop(0)[...].astype(BF16)
            dims = (((1,), (1 if term[2] else 0,)), ((), ()))
            parts.append(lax.dot_general(a, t_refs.pop(0)[...].astype(BF16), dims, preferred_element_type=F32))
        acc = parts if separate else sum(parts[1:], parts[0])
        vals = epilogue(acc, *rows, *small) if prologue is None else epilogue(acc, *rows, *small, made)
        for ref, val, out in zip(o_refs, vals, outs):
            if out[0] == "rows":
                ref[...] = val.astype(ref.dtype)
            else:
                @pl.when(i == 0)
                def _(ref=ref):
                    ref[...] = jnp.zeros_like(ref)

                ref[...] += val

    in_specs, args = [], []
    for term in terms:
        a, b = term[0], term[1]
        if a is None:
            in_specs.append(_resident(b.shape, lambda i: (0, 0)))
            args.append(b)
            continue
        b_rows = b.shape[0] if term[2] or len(term) < 4 else a.shape[-1]
        b_blk = 0 if len(term) < 4 else term[3] // b_rows
        if len(term) == 5:
            a_spec = pl.BlockSpec((None, tm, a.shape[2]), lambda i, c=term[4]: (c, i, 0))
        else:
            a_spec = pl.BlockSpec((tm, a.shape[1]), lambda i: (i, 0))
        in_specs += [a_spec, _resident((b_rows, b.shape[1]), lambda i, b_blk=b_blk: (b_blk, 0))]
        args += [a, b]
    for arr, width, col_blk in row_inputs:
        in_specs.append(pl.BlockSpec((tm, width), lambda i, col_blk=col_blk: (i, col_blk)))
        args.append(arr)
    for p in params:
        in_specs.append(pl.BlockSpec(p.shape, lambda i: (0, 0)))
        args.append(p)
    out_specs, out_shape = [], []
    for out in outs:
        if out[0] == "rows":
            out_specs.append(pl.BlockSpec((tm, out[1]), lambda i: (i, 0)))
            out_shape.append(jax.ShapeDtypeStruct((s, out[1]), out[2]))
        else:
            out_specs.append(pl.BlockSpec(out[1], lambda i: (0, 0)))
            out_shape.append(jax.ShapeDtypeStruct(out[1], F32))
    return pl.pallas_call(
        body, name=name, grid=(steps,), in_specs=in_specs, out_specs=out_specs, out_shape=out_shape,
        compiler_params=_params(("arbitrary",)),
    )(*args)


def _rms_stats(x):
    r = lax.rsqrt(jnp.mean(x * x, axis=-1, keepdims=True) + EPS)
    return r, x * r


def _gated(o_parts, gate):
    o = o_parts[0] if len(o_parts) == 1 else jnp.concatenate(o_parts, axis=1)
    return (o * (gate * _sigmoid(gate))).astype(BF16)


def _and_first(vals, *more):
    return (*vals, *more, vals[0])


def _residual_norm_epilogue(r, x, g):
    x1 = x + r
    _, xh = _rms_stats(x1)
    return x1, xh * g


def _rms_bwd_epilogue(dy, x, add, g):
    r, xh = _rms_stats(x)
    dxh = dy * g
    dx = r * (dxh - xh * jnp.mean(dxh * xh, axis=-1, keepdims=True)) + add
    return dx, jnp.sum(dy * xh, axis=0, keepdims=True)


def _loss_epilogue(r, x1, target, g):
    rs, xh = _rms_stats(x1 + r)
    err = xh * g - target
    loss = jnp.broadcast_to(0.5 * jnp.sum(jnp.mean(err * err, axis=-1, keepdims=True)), (8, LANES))
    dy = err * (1.0 / D_MODEL)
    dxh = dy * g
    dx = rs * (dxh - xh * jnp.mean(dxh * xh, axis=-1, keepdims=True))
    return dx, loss, jnp.sum(dy * xh, axis=0, keepdims=True)


def _gate_bwd_epilogue(widths):
    def epilogue(d, *rows):
        o_parts, gt = rows[:-1], rows[-1]
        o = o_parts[0] if len(o_parts) == 1 else jnp.concatenate(o_parts, axis=1)
        sg = _sigmoid(gt)
        do = d * (gt * sg)
        d_gate = d * o * (sg * (1.0 + gt * (1.0 - sg)))
        cuts = [sum(widths[:k]) for k in range(len(widths) + 1)]
        return tuple(do[:, cuts[k]:cuts[k + 1]] for k in range(len(widths))) + (d_gate,)

    return epilogue


def _rot_half(x):
    lane = lax.broadcasted_iota(jnp.int32, x.shape, 1)
    return jnp.where(lane < 80, pltpu.roll(x, LANES - 16, axis=1), pltpu.roll(x, 16, axis=1))


def _rot_half_t(g):
    lane = lax.broadcasted_iota(jnp.int32, g.shape, 1)
    lo = (lane >= MLA_NOPE) & (lane < MLA_NOPE + MLA_ROPE // 2)
    hi = (lane >= MLA_NOPE + MLA_ROPE // 2) & (lane < MLA_NOPE + MLA_ROPE)
    return jnp.where(lo, pltpu.roll(g, LANES - 16, axis=1), jnp.where(hi, pltpu.roll(g, 16, axis=1), 0.0))


def _rope_q_epilogue(q, c, sn):
    heads = [q[:, h * LANES:(h + 1) * LANES] for h in range(MLA_HEADS)]
    return (jnp.concatenate([qh * c + _rot_half(qh) * sn for qh in heads], axis=1),)


def _rope_k_epilogue(kv, kpe, c, sn):
    kpe_r = kpe * c + _rot_half(kpe) * sn
    lane = lax.broadcasted_iota(jnp.int32, kpe.shape, 1)
    heads = [jnp.where(lane < MLA_NOPE, kv[:, h * LANES:(h + 1) * LANES], kpe_r) for h in range(MLA_HEADS)]
    return (jnp.concatenate(heads + [kv[:, MLA_HEADS * LANES:]], axis=1),)


def _rope_bwd(dqm, dkm, dvm, cos_t, sin_t, *, name):
    s = dqm.shape[0]
    tm = _tile(s, ROW_TILES)
    hw = MLA_HEADS * LANES
    vw = MLA_HEADS * MLA_V

    def body(dq_ref, dk_ref, dv_ref, c_ref, s_ref, dqp_ref, dkv_ref, dkpe_ref):
        c = c_ref[...]
        sn = s_ref[...]
        ksum = jnp.zeros((tm, LANES), F32)
        for h in range(MLA_HEADS):
            sl = slice(h * LANES, (h + 1) * LANES)
            dq = dq_ref[:, sl]
            dqp_ref[:, sl] = (dq * c + _rot_half_t(dq * sn)).astype(BF16)
            dk = dk_ref[:, sl]
            dkv_ref[:, sl] = dk.astype(BF16)
            ksum = ksum + dk
        dkv_ref[:, hw:] = dv_ref[...]
        lane = lax.broadcasted_iota(jnp.int32, ksum.shape, 1)
        dkpe = ksum * c + _rot_half_t(ksum * sn)
        dkpe_ref[...] = jnp.where((lane >= MLA_NOPE) & (lane < MLA_NOPE + MLA_ROPE), dkpe, 0.0).astype(BF16)

    return pl.pallas_call(
        body, name=name, grid=(s // tm,),
        in_specs=[pl.BlockSpec((tm, hw), lambda i: (i, 0)), pl.BlockSpec((tm, hw), lambda i: (i, 0)),
                  pl.BlockSpec((tm, vw), lambda i: (i, 0)),
                  pl.BlockSpec((tm, LANES), lambda i: (i, 0)), pl.BlockSpec((tm, LANES), lambda i: (i, 0))],
        out_specs=[pl.BlockSpec((tm, hw), lambda i: (i, 0)), pl.BlockSpec((tm, hw + vw), lambda i: (i, 0)),
                   pl.BlockSpec((tm, LANES), lambda i: (i, 0))],
        out_shape=[jax.ShapeDtypeStruct((s, hw), BF16), jax.ShapeDtypeStruct((s, hw + vw), BF16),
                   jax.ShapeDtypeStruct((s, LANES), BF16)],
        compiler_params=_params(("parallel",)),
    )(dqm, dkm, dvm, cos_t, sin_t)


def _head_mask(shape, a):
    lane = lax.broadcasted_iota(jnp.int32, shape, 1)
    return (lane >= 64 * a) & (lane < 64 * (a + 1))


_NT = (((1,), (1,)), ((), ()))
LOG2E = 1.4426950408889634


def _stack_heads(tile, hw):
    lane = lax.broadcasted_iota(jnp.int32, tile.shape, 1)
    z = jnp.zeros_like(tile)
    return jnp.concatenate([jnp.where(lane < hw, tile, z), jnp.where(lane >= hw, tile, z)], axis=0)


def _stacked_rows(r0, r1, t):
    n = r0.shape[-1]
    return jnp.concatenate([jnp.broadcast_to(r0, (t, n)), jnp.broadcast_to(r1, (t, n))], axis=0)


def _resident(block, index_map):
    return pl.BlockSpec(block, index_map, pipeline_mode=pl.Buffered(1))


def _fwd_tile(s):
    return ATT_T_FWD if s % ATT_T_FWD == 0 else min(ATT_T, s)


def _flash_fwd(q, k, v, bias, *, n_pairs, hw, q_off, k_off, v_off, scale, name, rider=None):
    s = q.shape[0]
    t = _fwd_tile(s)
    nb = s // t
    qw = 2 * hw
    has_bias = bias is not None
    c1 = scale * LOG2E

    def body(*refs):
        refs, ride_refs = _split_rider(refs, rider, n_in=4 if has_bias else 3, n_out=2)
        if has_bias:
            q_ref, k_ref, v_ref, b_ref, o_ref, lse_ref, vt_ref, bcol_ref = refs
        else:
            q_ref, k_ref, v_ref, o_ref, lse_ref, vt_ref = refs
            b_ref = bcol_ref = None
        _ride_start(rider, ride_refs, pl.program_id(0) == 0)
        row = lax.broadcasted_iota(jnp.int32, (t, t), 0)
        col = lax.broadcasted_iota(jnp.int32, (t, t), 1)
        cmask_t = jnp.concatenate([row <= col, row <= col], axis=1)
        lane_lt64 = lax.broadcasted_iota(jnp.int32, (t, LANES), 1) < 64

        def as_column(r):
            return jnp.broadcast_to(r, (8, r.shape[1])).T[:, 0:1]

        def v_block(j, _):
            c0 = pl.multiple_of(j * t, t)
            vt_ref[j] = v_ref[pl.ds(c0, t), :].astype(F32).T.astype(BF16)
            if has_bias:
                for a in range(2):
                    bcol_ref[a, pl.ds(c0, t), :] = as_column(b_ref[0, a, j])
            return 0

        lax.fori_loop(0, nb, v_block, 0)

        def stacked_queries(i):
            return _stack_heads(q_ref[pl.ds(pl.multiple_of(i * t, t), t), :], hw).astype(F32).T.astype(BF16)

        def kv_step(j, carry, qs_t, masked):
            m, l, acc = carry
            rows = pl.ds(pl.multiple_of(j * t, t), t)
            sc = jnp.dot(k_ref[rows, :], qs_t, preferred_element_type=F32) * c1
            if has_bias:
                sc = sc + jnp.concatenate([jnp.broadcast_to(bcol_ref[0, rows, :], (t, t)),
                                           jnp.broadcast_to(bcol_ref[1, rows, :], (t, t))], axis=1)
            if masked:
                sc = jnp.where(cmask_t, sc, NEG_INF)
            m_new = jnp.maximum(m, jnp.max(sc, axis=0, keepdims=True))
            alpha = jnp.exp2(m - m_new)
            p = jnp.exp2(sc - m_new)
            l_new = alpha * l + jnp.sum(p, axis=0, keepdims=True)
            pv = jnp.dot(vt_ref[j], p.astype(BF16), preferred_element_type=F32)
            return m_new, l_new, alpha * acc + pv

        def finish(i, carry):
            m, l, acc = carry
            r0 = pl.multiple_of(i * t, t)
            out = (acc / l).T
            lse2 = as_column(m + jnp.log2(l))
            lse_ref[0, 0, pl.ds(r0, t), :] = lse2[:t]
            lse_ref[0, 1, pl.ds(r0, t), :] = lse2[t:]
            o_ref[pl.ds(r0, t), :] = jnp.where(lane_lt64, out[:t], out[t:])

        init = (jnp.full((1, 2 * t), NEG_INF, F32), jnp.zeros((1, 2 * t), F32), jnp.zeros((LANES, 2 * t), F32))

        def q_block(i, _):
            qs_t = stacked_queries(i)
            carry = lax.fori_loop(0, i, lambda j, c: kv_step(j, c, qs_t, False), init)
            finish(i, kv_step(i, carry, qs_t, True))
            return 0

        lax.fori_loop(0, nb, q_block, 0)
        _ride_wait(rider, ride_refs, pl.program_id(0) == n_pairs - 1)

    in_specs = [_resident((s, qw), lambda p: (0, q_off + p)), _resident((s, qw), lambda p: (0, k_off + p)),
                _resident((s, LANES), lambda p: (0, v_off + p))]
    args = [q, k, v]
    if has_bias:
        in_specs.append(_resident((1, 2, nb, 1, t), lambda p: (p, 0, 0, 0, 0)))
        args.append(bias)
    out_specs = [pl.BlockSpec((s, LANES), lambda p: (0, p)), pl.BlockSpec((1, 2, s, 1), lambda p: (p, 0, 0, 0))]
    out_shape = [jax.ShapeDtypeStruct((s, n_pairs * LANES), F32), jax.ShapeDtypeStruct((n_pairs, 2, s, 1), F32)]
    scratch = [pltpu.VMEM((nb, LANES, t), BF16)] + ([pltpu.VMEM((2, s, 1), F32)] if has_bias else [])
    scratch += _add_rider(rider, in_specs, args, out_specs, out_shape)
    return pl.pallas_call(
        body, name=name, grid=(n_pairs,), in_specs=in_specs, out_specs=out_specs, out_shape=out_shape,
        scratch_shapes=scratch,
        compiler_params=_params(("parallel",) if rider is None else ("arbitrary",)),
    )(*args)


def _flash_bwd(q, k, v, do, o, lse, bias, *, n_pairs, hw, q_off, k_off, v_off, scale, qk_dtype, name, rider=None,
               stacked=False):
    s = q.shape[0]
    t = min(ATT_T, s)
    nb = s // t
    qw = 2 * hw
    has_bias = bias is not None
    c1 = scale * LOG2E

    def body(*refs):
        n_grads = 1 if stacked else 3
        refs, ride_refs = _split_rider(refs, rider, n_in=7 if has_bias else 6, n_out=n_grads + (2 if has_bias else 0))
        if stacked:
            refs = list(refs)
            n_in = 7 if has_bias else 6
            refs[n_in:n_in + 1] = [refs[n_in].at[0], refs[n_in].at[1], refs[n_in].at[2]]
        if has_bias:
            (q_ref, k_ref, v_ref, do_ref, o_ref, lse_ref, b_ref, dq_ref, dk_ref, dv_ref, db_ref, dr_ref,
             dkt_ref, dvt_ref) = refs
            db_ref[...] = jnp.zeros_like(db_ref)
        else:
            q_ref, k_ref, v_ref, do_ref, o_ref, lse_ref, dq_ref, dk_ref, dv_ref, dkt_ref, dvt_ref = refs
            b_ref = db_ref = dr_ref = None
        _ride_start(rider, ride_refs, pl.program_id(0) == 0)
        dkt_ref[...] = jnp.zeros_like(dkt_ref)
        dvt_ref[...] = jnp.zeros_like(dvt_ref)
        causal = lax.broadcasted_iota(jnp.int32, (t, t), 1) <= lax.broadcasted_iota(jnp.int32, (t, t), 0)
        cmask = jnp.concatenate([causal, causal], axis=0)
        lane_lt_hw = lax.broadcasted_iota(jnp.int32, (t, qw), 1) < hw

        def q_block(i, _):
            r0 = pl.multiple_of(i * t, t)
            qs = _stack_heads(q_ref[pl.ds(r0, t), :], hw)
            dos = _stack_heads(do_ref[pl.ds(r0, t), :], 64)
            ot = o_ref[pl.ds(r0, t), :]
            delta = jnp.sum(dos * jnp.concatenate([ot, ot], axis=0), axis=-1, keepdims=True)
            lse2 = jnp.concatenate([lse_ref[0, 0, pl.ds(r0, t), :], lse_ref[0, 1, pl.ds(r0, t), :]], axis=0)
            dosb = dos.astype(BF16)
            dos_t = dos.T.astype(BF16)
            qs_t = qs.astype(F32).T.astype(BF16)

            def kv_step(j, carry, masked):
                dq, rsum = carry
                c0 = pl.multiple_of(j * t, t)
                kt = k_ref[pl.ds(c0, t), :]
                vt = v_ref[pl.ds(c0, t), :]
                sc = lax.dot_general(qs, kt, _NT, preferred_element_type=F32) * c1
                if has_bias:
                    sc = sc + _stacked_rows(b_ref[0, 0, j], b_ref[0, 1, j], t)
                if masked:
                    sc = jnp.where(cmask, sc, NEG_INF)
                p = jnp.exp2(sc - lse2)
                dp = lax.dot_general(dosb, vt, _NT, preferred_element_type=F32)
                ds = p * (dp - delta)
                dsb = ds.astype(BF16)
                pb = p.astype(BF16)
                if hw == LANES:
                    dvt_ref[j] += jnp.concatenate(
                        [jnp.dot(dos_t[:64, :t], pb[:t], preferred_element_type=F32),
                         jnp.dot(dos_t[64:, t:], pb[t:], preferred_element_type=F32)], axis=0)
                    dkt_ref[j] += jnp.concatenate(
                        [jnp.dot(qs_t[:hw, :t], dsb[:t], preferred_element_type=F32),
                         jnp.dot(qs_t[hw:, t:], dsb[t:], preferred_element_type=F32)], axis=0)
                else:
                    dvt_ref[j] += jnp.dot(dos_t, pb, preferred_element_type=F32)
                    dkt_ref[j] += jnp.dot(qs_t, dsb, preferred_element_type=F32)
                if has_bias:
                    db_ref[0, 0, j] += jnp.sum(ds[:t], axis=0, keepdims=True)
                    db_ref[0, 1, j] += jnp.sum(ds[t:], axis=0, keepdims=True)
                    rsum = rsum + jnp.sum(ds, axis=-1, keepdims=True)
                return dq + jnp.dot(dsb, kt, preferred_element_type=F32), rsum

            init = (jnp.zeros((2 * t, qw), F32), jnp.zeros((2 * t, 1), F32))
            carry = lax.fori_loop(0, i, functools.partial(kv_step, masked=False), init)
            dq, rsum = kv_step(i, carry, True)
            dq = dq * scale
            dq_ref[pl.ds(r0, t), :] = jnp.where(lane_lt_hw, dq[:t], dq[t:]).astype(qk_dtype)
            if has_bias:
                rsum_row = jnp.broadcast_to(rsum, (2 * t, LANES)).T[0:1]
                dr_ref[0, 0, i] = rsum_row[:, :t]
                dr_ref[0, 1, i] = rsum_row[:, t:]
            return 0

        lax.fori_loop(0, nb, q_block, 0)

        def k_block(j, _):
            c0 = pl.multiple_of(j * t, t)
            dk_ref[pl.ds(c0, t), :] = (dkt_ref[j].T * scale).astype(qk_dtype)
            dv_ref[pl.ds(c0, t), :] = dvt_ref[j].T.astype(BF16)
            return 0

        lax.fori_loop(0, nb, k_block, 0)
        _ride_wait(rider, ride_refs, pl.program_id(0) == n_pairs - 1)

    in_specs = [_resident((s, qw), lambda p: (0, q_off + p)), _resident((s, qw), lambda p: (0, k_off + p)),
                _resident((s, LANES), lambda p: (0, v_off + p)),
                _resident((s, LANES), lambda p: (0, p)), _resident((s, LANES), lambda p: (0, p)),
                _resident((1, 2, s, 1), lambda p: (p, 0, 0, 0))]
    args = [q, k, v, do, o, lse]
    if stacked:
        assert qw == LANES and qk_dtype == BF16
        out_specs = [pl.BlockSpec((3, s, LANES), lambda p: (0, 0, p))]
        out_shape = [jax.ShapeDtypeStruct((3, s, n_pairs * LANES), BF16)]
    else:
        out_specs = [pl.BlockSpec((s, qw), lambda p: (0, p)), pl.BlockSpec((s, qw), lambda p: (0, p)),
                     pl.BlockSpec((s, LANES), lambda p: (0, p))]
        out_shape = [jax.ShapeDtypeStruct((s, n_pairs * qw), qk_dtype), jax.ShapeDtypeStruct((s, n_pairs * qw), qk_dtype),
                     jax.ShapeDtypeStruct((s, n_pairs * LANES), BF16)]
    if has_bias:
        in_specs.append(_resident((1, 2, nb, 1, t), lambda p: (p, 0, 0, 0, 0)))
        args.append(bias)
        for _ in range(2):
            out_specs.append(pl.BlockSpec((1, 2, nb, 1, t), lambda p: (p, 0, 0, 0, 0)))
            out_shape.append(jax.ShapeDtypeStruct((n_pairs, 2, nb, 1, t), F32))
    scratch = [pltpu.VMEM((nb, qw, t), F32), pltpu.VMEM((nb, LANES, t), F32)]
    scratch += _add_rider(rider, in_specs, args, out_specs, out_shape)
    return pl.pallas_call(
        body, name=name, grid=(n_pairs,), in_specs=in_specs, out_specs=out_specs, out_shape=out_shape,
        scratch_shapes=scratch,
        compiler_params=_params(("parallel",) if rider is None else ("arbitrary",)),
    )(*args)


def _alibi_slope(h):
    return 2.0 ** (-8.0 * (h + 1.0) / SWA_HEADS)


SWA_ROWS = 512
SWA_SCALE = SWA_DIM ** -0.5


def _swa_geometry(i):
    w = WINDOW
    r0 = pl.multiple_of(i * w, w)
    b0 = pl.multiple_of(jnp.maximum(i - 1, 0) * w, w)
    row = lax.broadcasted_iota(jnp.int32, (w, 2 * w), 0)
    col = lax.broadcasted_iota(jnp.int32, (w, 2 * w), 1)
    dist = row - col + (r0 - b0)
    valid = (dist >= 0) & (dist < w)
    return r0, b0, dist.astype(F32), valid


def _swa_q_head(qblk, h):
    kv = h // (SWA_HEADS // SWA_KV_HEADS)
    if h % 2 != kv:
        qblk = pltpu.roll(qblk, 64, axis=1)
    return jnp.where(_head_mask(qblk.shape, kv), qblk, 0.0)


SWA_GROUP = SWA_HEADS // SWA_KV_HEADS


def _swa_stack(ref, rs, grp):
    parts = []
    for a in range(SWA_GROUP):
        h = SWA_GROUP * grp + a
        parts.append(_swa_q_head(ref[rs, (h // 2) * LANES:(h // 2 + 1) * LANES].astype(F32), h))
    return jnp.concatenate(parts, axis=0)


def _swa_unstack(x, grp):
    tiles = []
    for a in range(SWA_GROUP):
        h = SWA_GROUP * grp + a
        tile = x[a * WINDOW:(a + 1) * WINDOW]
        tiles.append(pltpu.roll(tile, 64, axis=1) if h % 2 != grp else tile)
    return tiles


def _swa_head_column(vals):
    return jnp.concatenate([jnp.full((WINDOW, 1), v, F32) for v in vals], axis=0)


def _swa_logits(qs, kb, dist, valid, grp):
    slopes = _swa_head_column([_alibi_slope(SWA_GROUP * grp + a) for a in range(SWA_GROUP)])
    dist4 = jnp.concatenate([dist] * SWA_GROUP, axis=0)
    valid4 = jnp.concatenate([valid] * SWA_GROUP, axis=0)
    sc = lax.dot_general(qs, kb, _NT, preferred_element_type=F32) * SWA_SCALE - slopes * dist4
    return jnp.where(valid4, sc, NEG_INF)


def _swa_merge_heads(tiles):
    lt64 = lax.broadcasted_iota(jnp.int32, (WINDOW, LANES), 1) < 64
    return jnp.concatenate([jnp.where(lt64, tiles[2 * b], tiles[2 * b + 1]) for b in range(SWA_HEADS // 2)], axis=1)


def _swa_fwd(z0b, sinks, *, name):
    s = z0b.shape[0]
    w = WINDOW
    rows = min(SWA_ROWS, s)
    per_step = rows // w
    qcols = SWA_HEADS * SWA_DIM

    def body(sink_ref, q_ref, k_ref, v_ref, o_ref, lse_ref):
        g = pl.program_id(0)
        for ii in range(per_step):
            rs = slice(ii * w, (ii + 1) * w)
            r0, b0, dist, valid = _swa_geometry(g * per_step + ii)
            kb = k_ref[pl.ds(b0, 2 * w), :]
            vb = v_ref[pl.ds(b0, 2 * w), :]
            o_tiles = []
            for h in range(SWA_HEADS):
                kv = h // SWA_GROUP
                qh = _swa_q_head(q_ref[rs, (h // 2) * LANES:(h // 2 + 1) * LANES].astype(F32), h).astype(BF16)
                sc = lax.dot_general(qh, kb, _NT, preferred_element_type=F32) * SWA_SCALE - _alibi_slope(h) * dist
                sc = jnp.where(valid, sc, NEG_INF)
                sink = sink_ref[0, h]
                m = jnp.maximum(jnp.max(sc, axis=-1, keepdims=True), sink)
                p = jnp.exp(sc - m)
                l = jnp.sum(p, axis=-1, keepdims=True) + jnp.exp(sink - m)
                oh = jnp.dot(p.astype(BF16), vb, preferred_element_type=F32) / l
                o_tiles.append(pltpu.roll(oh, 64, axis=1) if h % 2 != kv else oh)
                lse_ref[h, rs, :] = m + jnp.log(l)
            o_ref[rs, :] = _swa_merge_heads(o_tiles)

    return pl.pallas_call(
        body, name=name, grid=(s // rows,),
        in_specs=[pl.BlockSpec(memory_space=pltpu.SMEM),
                  pl.BlockSpec((rows, qcols), lambda g: (g, 0)),
                  pl.BlockSpec((s, LANES), lambda g: (0, 4)), pl.BlockSpec((s, LANES), lambda g: (0, 5))],
        out_specs=[pl.BlockSpec((rows, qcols), lambda g: (g, 0)), pl.BlockSpec((SWA_HEADS, rows, 1), lambda g: (0, g, 0))],
        out_shape=[jax.ShapeDtypeStruct((s, qcols), F32), jax.ShapeDtypeStruct((SWA_HEADS, s, 1), F32)],
        compiler_params=_params(("parallel",)),
    )(sinks, z0b, z0b, z0b)


def _swa_bwd(z0b, sinks, do, o, lse, *, name):
    s = z0b.shape[0]
    w = WINDOW
    rows = min(SWA_ROWS, s)
    per_step = rows // w
    qcols = SWA_HEADS * SWA_DIM
    nblk = s // w

    def body(sink_ref, q_ref, k_ref, v_ref, do_ref, o_ref, lse_ref, dq_ref, dkt_ref, dvt_ref, dsink_ref):
        g = pl.program_id(0)

        @pl.when(g == 0)
        def _():
            dkt_ref[...] = jnp.zeros_like(dkt_ref)
            dvt_ref[...] = jnp.zeros_like(dvt_ref)
            dsink_ref[...] = jnp.zeros_like(dsink_ref)

        for ii in range(per_step):
            i = g * per_step + ii
            rs = slice(ii * w, (ii + 1) * w)
            r0, b0, dist, valid = _swa_geometry(i)
            j0 = jnp.maximum(i - 1, 0)
            kb = k_ref[pl.ds(b0, 2 * w), :]
            vb = v_ref[pl.ds(b0, 2 * w), :]
            dq_tiles = []
            for grp in range(SWA_KV_HEADS):
                heads = [SWA_GROUP * grp + a for a in range(SWA_GROUP)]
                qs32 = _swa_stack(q_ref, rs, grp)
                dos32 = _swa_stack(do_ref, rs, grp)
                delta = jnp.sum(dos32 * _swa_stack(o_ref, rs, grp), axis=-1, keepdims=True)
                lse = jnp.concatenate([lse_ref[h, rs, :] for h in heads], axis=0)
                sink = _swa_head_column([sink_ref[0, h] for h in heads])
                p = jnp.exp(_swa_logits(qs32.astype(BF16), kb, dist, valid, grp) - lse)
                dp = lax.dot_general(dos32.astype(BF16), vb, _NT, preferred_element_type=F32)
                ds = p * (dp - delta)
                dsb = ds.astype(BF16)
                d_sink = jnp.exp(sink - lse) * delta
                for a, h in enumerate(heads):
                    dsink_ref[h:h + 1, :] += jnp.broadcast_to(-jnp.sum(d_sink[a * w:(a + 1) * w]), (1, LANES))
                dvt = jnp.dot(dos32.T.astype(BF16), p.astype(BF16), preferred_element_type=F32)
                dkt = jnp.dot(qs32.T.astype(BF16), dsb, preferred_element_type=F32) * SWA_SCALE
                dvt_ref[j0] += dvt[:, :w]
                dvt_ref[j0 + 1] += dvt[:, w:]
                dkt_ref[j0] += dkt[:, :w]
                dkt_ref[j0 + 1] += dkt[:, w:]
                dq_tiles += _swa_unstack(jnp.dot(dsb, kb, preferred_element_type=F32) * SWA_SCALE, grp)
            dq_ref[rs, :] = _swa_merge_heads(dq_tiles)

    return pl.pallas_call(
        body, name=name, grid=(s // rows,),
        in_specs=[pl.BlockSpec(memory_space=pltpu.SMEM),
                  pl.BlockSpec((rows, qcols), lambda g: (g, 0)),
                  pl.BlockSpec((s, LANES), lambda g: (0, 4)), pl.BlockSpec((s, LANES), lambda g: (0, 5)),
                  pl.BlockSpec((rows, qcols), lambda g: (g, 0)), pl.BlockSpec((rows, qcols), lambda g: (g, 0)),
                  pl.BlockSpec((SWA_HEADS, rows, 1), lambda g: (0, g, 0))],
        out_specs=[pl.BlockSpec((rows, qcols), lambda g: (g, 0)),
                   pl.BlockSpec((nblk, LANES, w), lambda g: (0, 0, 0)),
                   pl.BlockSpec((nblk, LANES, w), lambda g: (0, 0, 0)),
                   pl.BlockSpec((SWA_HEADS, LANES), lambda g: (0, 0))],
        out_shape=[jax.ShapeDtypeStruct((s, qcols), F32),
                   jax.ShapeDtypeStruct((nblk, LANES, w), F32), jax.ShapeDtypeStruct((nblk, LANES, w), F32),
                   jax.ShapeDtypeStruct((SWA_HEADS, LANES), F32)],
        compiler_params=_params(("arbitrary",)),
    )(sinks, z0b, z0b, z0b, do, o, lse)


CUM_T = 256


def _split3(x):
    hi = x.astype(BF16)
    r1 = x - hi.astype(F32)
    mid = r1.astype(BF16)
    lo = (r1 - mid.astype(F32)).astype(BF16)
    return hi, mid, lo


def _tri_dot(tri, x):
    hi, mid, lo = _split3(x)
    out = jnp.dot(tri, hi, preferred_element_type=F32)
    out = out + jnp.dot(tri, mid, preferred_element_type=F32)
    return out + jnp.dot(tri, lo, preferred_element_type=F32)


def _logf_fwd(zf, bf, *, name):
    s = zf.shape[0]
    t = CUM_T
    nb = s // t

    def body(z_ref, b_ref, c_ref, carry_ref):
        i = pl.program_id(0)

        @pl.when(i == 0)
        def _():
            carry_ref[...] = jnp.zeros_like(carry_ref)

        x = z_ref[...] + b_ref[...]
        lf = jnp.minimum(x, 0.0) - jnp.log(1.0 + jnp.exp(-jnp.abs(x)))
        row = lax.broadcasted_iota(jnp.int32, (t, t), 0)
        col = lax.broadcasted_iota(jnp.int32, (t, t), 1)
        tri = jnp.where(col <= row, 1.0, 0.0).astype(BF16)
        c = _tri_dot(tri, lf) + carry_ref[...]
        c_ref[...] = c
        carry_ref[...] = c[t - 1:t, :]

    return pl.pallas_call(
        body, name=name, grid=(nb,),
        in_specs=[pl.BlockSpec((t, LANES), lambda i: (i, 0)), pl.BlockSpec((1, LANES), lambda i: (0, 0))],
        out_specs=pl.BlockSpec((t, LANES), lambda i: (i, 0)),
        out_shape=jax.ShapeDtypeStruct((s, LANES), F32),
        scratch_shapes=[pltpu.VMEM((1, LANES), F32)],
        compiler_params=_params(("arbitrary",)),
    )(zf, bf)


def _logf_bwd(dc, zf, bf, *, name):
    s = zf.shape[0]
    t = CUM_T
    nb = s // t

    def body(dc_ref, z_ref, b_ref, dz_ref, db_ref, carry_ref):
        i = pl.program_id(0)

        @pl.when(i == 0)
        def _():
            carry_ref[...] = jnp.zeros_like(carry_ref)
            db_ref[...] = jnp.zeros_like(db_ref)

        row = lax.broadcasted_iota(jnp.int32, (t, t), 0)
        col = lax.broadcasted_iota(jnp.int32, (t, t), 1)
        tri = jnp.where(col >= row, 1.0, 0.0).astype(BF16)
        dlf = _tri_dot(tri, dc_ref[...]) + carry_ref[...]
        carry_ref[...] = dlf[0:1, :]
        x = z_ref[...] + b_ref[...]
        dz = dlf * _sigmoid(-x)
        dz_ref[...] = dz.astype(BF16)
        db_ref[...] += jnp.sum(dz, axis=0, keepdims=True)

    return pl.pallas_call(
        body, name=name, grid=(nb,),
        in_specs=[pl.BlockSpec((t, LANES), lambda i: (nb - 1 - i, 0)), pl.BlockSpec((t, LANES), lambda i: (nb - 1 - i, 0)),
                  pl.BlockSpec((1, LANES), lambda i: (0, 0))],
        out_specs=[pl.BlockSpec((t, LANES), lambda i: (nb - 1 - i, 0)), pl.BlockSpec((1, LANES), lambda i: (0, 0))],
        out_shape=[jax.ShapeDtypeStruct((s, LANES), BF16), jax.ShapeDtypeStruct((1, LANES), F32)],
        scratch_shapes=[pltpu.VMEM((1, LANES), F32)],
        compiler_params=_params(("arbitrary",)),
    )(dc, zf, bf)


def _sum_pieces(p_ref):
    g = p_ref[0].astype(F32)
    for k in range(1, N_DEV):
        g = g + p_ref[k].astype(F32)
    return g


def _adam_update(g, w, m, v):
    bc1 = 1.0 - ADAM_B1 ** ADAM_STEP
    bc2 = 1.0 - ADAM_B2 ** ADAM_STEP
    nm = ADAM_B1 * m + (1.0 - ADAM_B1) * g
    nv = ADAM_B2 * v + (1.0 - ADAM_B2) * (g * g)
    m_hat = nm / bc1
    v_hat = nv / bc2
    return -ADAM_LR * (m_hat / (jnp.sqrt(v_hat) + ADAM_EPS) + ADAM_WD * w), nm, nv


def _adamw(pieces, w, m, v, *, name):
    rows, cols = w.shape
    tr = _tile(rows, (RB1, RB0, SMALL_ROWS))

    def body(p_ref, w_ref, m_ref, v_ref, g_ref, d_ref, nm_ref, nv_ref):
        g = _sum_pieces(p_ref)
        g_ref[...] = g
        d_ref[...], nm_ref[...], nv_ref[...] = _adam_update(g, w_ref[...], m_ref[...], v_ref[...])

    spec = pl.BlockSpec((tr, cols), lambda i: (i, 0))
    shape = jax.ShapeDtypeStruct((rows, cols), F32)
    return pl.pallas_call(
        body, name=name, grid=(rows // tr,),
        in_specs=[pl.BlockSpec((N_DEV, tr, cols), lambda i: (0, i, 0)), spec, spec, spec],
        out_specs=[spec, spec, spec, spec], out_shape=[shape, shape, shape, shape],
        compiler_params=_params(("parallel",)),
    )(pieces, w, m, v)


def _sum8(pieces, rows, *, name):
    cols = pieces.shape[2]
    tr = _tile(rows, (176, 96))

    def body(p_ref, g_ref):
        g_ref[...] = _sum_pieces(p_ref)

    return pl.pallas_call(
        body, name=name, grid=(rows // tr,),
        in_specs=[pl.BlockSpec((N_DEV, tr, cols), lambda i: (0, i, 0))],
        out_specs=pl.BlockSpec((tr, cols), lambda i: (i, 0)),
        out_shape=jax.ShapeDtypeStruct((rows, cols), F32),
        compiler_params=_params(("parallel",)),
    )(pieces)


def _adamw_columns(g, w, m, v, *, name):
    n, _, k = w.shape
    tr = n // 2

    def body(g_ref, w_ref, m_ref, v_ref, d_ref, nm_ref, nv_ref):
        d_ref[...], nm_ref[...], nv_ref[...] = _adam_update(g_ref[...], w_ref[...], m_ref[...], v_ref[...])

    spec = pl.BlockSpec((tr, 1, k), lambda i: (i, 0, 0))
    shape = jax.ShapeDtypeStruct((n, 1, k), F32)
    return pl.pallas_call(
        body, name=name, grid=(n // tr,), in_specs=[spec, spec, spec, spec],
        out_specs=[spec, spec, spec], out_shape=[shape, shape, shape],
        compiler_params=_params(("parallel",)),
    )(g, w, m, v)


MESH = pl.DeviceIdType.MESH
ANY = pl.BlockSpec(memory_space=pl.ANY)


def _all_gather(shard, *, name):
    rows, lanes = shard.shape

    def body(x_ref, out_ref, send_sems, recv_sems, local_sem):
        x, y, c = lax.axis_index("x"), lax.axis_index("y"), lax.axis_index("c")
        me, sibling = (x, y, c), (x, y, 1 - c)
        chips = [(1 - x, y), (x, 1 - y), (1 - x, 1 - y)]

        def block(px, py, pc):
            return out_ref.at[4 * px + 2 * py + pc]

        def copy(k, blk, to, src=None):
            return pltpu.make_async_remote_copy(
                src_ref=block(*blk) if src is None else src, dst_ref=block(*blk),
                send_sem=send_sems.at[k], recv_sem=recv_sems.at[k], device_id=to, device_id_type=MESH)

        mine = pltpu.make_async_copy(x_ref, block(*me), local_sem)
        mine.start()
        first = [copy(0, me, sibling, src=x_ref)]
        first += [copy(1 + j, me, (*chip, c), src=x_ref) for j, chip in enumerate(chips)]
        for cp in first:
            cp.start()
        passed = [copy(4 + j, (*chip, c), sibling) for j, chip in enumerate(chips)]
        for j, chip in enumerate(chips):
            copy(1 + j, (*chip, c), me).wait_recv()
            passed[j].start()
        copy(0, sibling, me).wait_recv()
        for j, chip in enumerate(chips):
            copy(4 + j, (*chip, 1 - c), me).wait_recv()
        for cp in first + passed:
            cp.wait_send()
        mine.wait()

    return pl.pallas_call(
        body, name=name, out_shape=jax.ShapeDtypeStruct((N_DEV, rows, lanes), shard.dtype),
        in_specs=[ANY], out_specs=ANY,
        scratch_shapes=[pltpu.SemaphoreType.DMA((7,)), pltpu.SemaphoreType.DMA((7,)), pltpu.SemaphoreType.DMA(())],
    )(shard)


def _peer_copies(kind, src_ref, out_ref, send_sems, recv_sems, local_sem):
    x, y, c = lax.axis_index("x"), lax.axis_index("y"), lax.axis_index("c")
    me = 4 * x + 2 * y + c

    def src(idx):
        return src_ref.at[idx] if kind == "exchange" else src_ref

    mine = None if local_sem is None else pltpu.make_async_copy(src(me), out_ref.at[me], local_sem)
    copies = []
    for r in (2, 4, 6) if kind == "across" else range(1, N_DEV):
        px = 1 - x if r & 4 else x
        py = 1 - y if r & 2 else y
        pc = 1 - c if r & 1 else c
        copies.append(pltpu.make_async_remote_copy(
            src_ref=src(4 * px + 2 * py + pc), dst_ref=out_ref.at[me],
            send_sem=send_sems.at[r - 1], recv_sem=recv_sems.at[r - 1],
            device_id=(px, py, pc), device_id_type=MESH))
    return mine, copies


def _to_other_core(shard, land, *, name):
    def body(src_ref, land_ref, out_ref, send_sems, recv_sems):
        x, y, c = lax.axis_index("x"), lax.axis_index("y"), lax.axis_index("c")
        copies = []
        for k, r in enumerate((0, 2, 4, 6)):
            slot = 4 * (1 - x if r & 4 else x) + 2 * (1 - y if r & 2 else y) + c
            copies.append(pltpu.make_async_remote_copy(
                src_ref=src_ref if r == 0 else land_ref.at[slot], dst_ref=out_ref.at[slot],
                send_sem=send_sems.at[k], recv_sem=recv_sems.at[k], device_id=(x, y, 1 - c), device_id_type=MESH))
        for cp in copies:
            cp.start()
        for cp in copies:
            cp.wait()

    return pl.pallas_call(
        body, name=name, out_shape=jax.ShapeDtypeStruct(land.shape, land.dtype), in_specs=[ANY, ANY], out_specs=ANY,
        input_output_aliases={1: 0}, scratch_shapes=[pltpu.SemaphoreType.DMA((4,)), pltpu.SemaphoreType.DMA((4,))],
    )(shard, land)


PEER_SEMS = [pltpu.SemaphoreType.DMA((7,)), pltpu.SemaphoreType.DMA((7,)), pltpu.SemaphoreType.DMA(())]


HBM = pl.BlockSpec(memory_space=pltpu.HBM)
SEMAPHORES = pl.BlockSpec(memory_space=pltpu.SEMAPHORE)


def _peer_start(kind, arr, *, name):
    land = lax.empty((N_DEV,) + arr.shape[-2:], arr.dtype)

    def body(src_ref, land_ref, send_sems, recv_sems, src_thru, land_thru, token):
        _, copies = _peer_copies(kind, src_ref, land_ref, send_sems, recv_sems, None)
        for cp in copies:
            cp.start()
        token[...] = jnp.zeros_like(token)

    return pl.pallas_call(
        body, name=name,
        out_shape=(pltpu.SemaphoreType.DMA((N_DEV - 1,)), pltpu.SemaphoreType.DMA((N_DEV - 1,)),
                   pltpu.HBM(arr.shape, arr.dtype), pltpu.HBM(land.shape, land.dtype), jax.ShapeDtypeStruct((8, LANES), F32)),
        in_specs=(HBM, HBM), out_specs=(SEMAPHORES, SEMAPHORES, HBM, HBM, pl.BlockSpec(memory_space=pltpu.VMEM)),
        input_output_aliases={0: 2, 1: 3},
        compiler_params=pltpu.CompilerParams(has_side_effects=pltpu.SideEffectType.DATAFLOW_SIDE_EFFECTING),
    )(pltpu.with_memory_space_constraint(arr, pltpu.HBM), pltpu.with_memory_space_constraint(land, pltpu.HBM))


def _peer_wait(kind, send_sems, recv_sems, src_thru, land_thru, after, *, name):
    def body(src_ref, land_ref, send_sems, recv_sems, *_):
        _, copies = _peer_copies(kind, src_ref, land_ref, send_sems, recv_sems, None)
        for cp in copies:
            cp.wait_send()
            cp.wait_recv()

    return pl.pallas_call(
        body, name=name,
        out_shape=(pltpu.HBM(src_thru.shape, src_thru.dtype), pltpu.HBM(land_thru.shape, land_thru.dtype)),
        in_specs=(HBM, HBM, SEMAPHORES, SEMAPHORES) + (ANY,) * len(after), out_specs=(HBM, HBM),
        input_output_aliases={0: 0, 1: 1},
        compiler_params=pltpu.CompilerParams(has_side_effects=pltpu.SideEffectType.DATAFLOW_SIDE_EFFECTING),
    )(src_thru, land_thru, send_sems, recv_sems, *after)


def _add_rider(rider, in_specs, args, out_specs, out_shape):
    if rider is None:
        return []
    _, arr = rider
    in_specs.append(ANY)
    args.append(arr)
    out_specs.append(ANY)
    out_shape.append(jax.ShapeDtypeStruct((N_DEV,) + arr.shape[-2:], arr.dtype))
    return list(PEER_SEMS)


def _split_rider(refs, rider, n_in, n_out):
    if rider is None:
        return refs, None
    refs = list(refs)
    rin = refs.pop(n_in)
    rout = refs.pop(n_in + n_out)
    return refs[:-3], (rin, rout, *refs[-3:])


def _ride_start(rider, ride_refs, first):
    if rider is None:
        return

    @pl.when(first)
    def _():
        mine, copies = _peer_copies(rider[0], *ride_refs)
        mine.start()
        for cp in copies:
            cp.start()


def _ride_wait(rider, ride_refs, last):
    if rider is None:
        return

    @pl.when(last)
    def _():
        mine, copies = _peer_copies(rider[0], *ride_refs)
        for cp in copies:
            cp.wait()
        mine.wait()


def _gathered_cols(blocks, kdim):
    n = blocks.shape[1] * WIDE // kdim
    return blocks.reshape(N_DEV, kdim, n).transpose(1, 0, 2).reshape(kdim, N_DEV * n)


def _scatter_cols(dw):
    kdim, n8 = dw.shape
    n = n8 // N_DEV
    return dw.reshape(kdim, N_DEV, n).transpose(1, 0, 2).reshape(N_DEV, kdim * n // WIDE, WIDE)


def _pad_rows(a, rows):
    pad = [(0, 0)] * a.ndim
    pad[-2] = (0, rows - a.shape[-2])
    return jnp.pad(a, pad)


def _layer0_in_weight_t(wt):
    cq, ckv, kpe = wt[0:256], wt[256:384], wt[384:416]
    q_s, k_s, v_s, gate = wt[416:928], wt[928:1056], wt[1056:1184], wt[1184:2208]
    z = jnp.zeros((64, wt.shape[1]), wt.dtype)
    return jnp.concatenate([gate, cq, ckv, z, kpe, z[:32], q_s, k_s, v_s], axis=0)


def _layer0_in_grad_t(dwt):
    gate, cq, ckv, kpe = dwt[0:1024], dwt[1024:1280], dwt[1280:1408], dwt[1472:1504]
    q_s, k_s, v_s = dwt[1536:2048], dwt[2048:2176], dwt[2176:2304]
    return jnp.concatenate([cq, ckv, kpe, q_s, k_s, v_s, gate], axis=0)


L0_BLOCKS = ((256, 1024), (128, 1280), (32, 1472), (512, 1536), (128, 2048), (128, 2176), (1024, 0))


def _layer0_in_unpack(gath, *, name):
    total = (Z0A_UNITS + Z0B_UNITS) * LANES

    def body(g_ref, w_ref):
        w_ref[1408:1472, :] = jnp.zeros((64, WIDE), w_ref.dtype)
        w_ref[1504:1536, :] = jnp.zeros((32, WIDE), w_ref.dtype)
        for p in range(N_DEV):
            lo, hi, at = p * N_E_IN, (p + 1) * N_E_IN, 0
            for rows, first in L0_BLOCKS:
                start, stop = max(lo, at), min(hi, at + rows)
                if start < stop:
                    w_ref[first + start - at:first + stop - at, :] = g_ref[p, start - lo:stop - lo, :]
                at += rows

    return pl.pallas_call(
        body, name=name, grid=(1,), in_specs=[_resident((N_DEV, RA0, WIDE), lambda i: (0, 0, 0))],
        out_specs=_resident((total, WIDE), lambda i: (0, 0)), out_shape=jax.ShapeDtypeStruct((total, WIDE), gath.dtype),
        compiler_params=_params(("arbitrary",)),
    )(gath)


def _early_grads_pack(d_w0t, d_q, d_kv, *, name):
    def body(w_ref, q_ref, kv_ref, out_ref):
        for p in range(N_DEV):
            lo, hi, at = p * N_E_IN, (p + 1) * N_E_IN, 0
            for rows, first in L0_BLOCKS:
                start, stop = max(lo, at), min(hi, at + rows)
                if start < stop:
                    out_ref[p, start - lo:stop - lo, :] = w_ref[first + start - at:first + stop - at, :]
                at += rows
            out_ref[p, N_E_IN:RA0, :] = jnp.zeros((RA0 - N_E_IN, WIDE), out_ref.dtype)
            out_ref[p, RA0:RA0 + 32, :] = q_ref[p]
            out_ref[p, RA0 + 32:, :] = kv_ref[p]

    arrays = (d_w0t, d_q, d_kv)
    return pl.pallas_call(
        body, name=name, grid=(1,), in_specs=[_resident(a.shape, lambda i, n=a.ndim: (0,) * n) for a in arrays],
        out_specs=_resident((N_DEV, RA0 + RB0, WIDE), lambda i: (0, 0, 0)),
        out_shape=jax.ShapeDtypeStruct((N_DEV, RA0 + RB0, WIDE), BF16), compiler_params=_params(("arbitrary",)),
    )(*arrays)


def _layer1_in_weight_t(wt):
    main = jnp.concatenate([wt[:3 * D_MODEL], wt[3 * D_MODEL + FOX_HEADS:]], axis=0)
    return main, _pad_rows(wt[3 * D_MODEL:3 * D_MODEL + FOX_HEADS], LANES)


def _layer1_in_unpack(gath, *, name):
    n_main = 3 * D_MODEL

    def body(g_ref, w_ref, f_ref, o1_ref, o0_ref):
        f_ref[...] = jnp.zeros_like(f_ref)
        for p in range(N_DEV):
            o1_ref[128 * p:128 * p + 128, :] = g_ref[p, RA1:RA1 + 128, :]
            o0_ref[128 * p:128 * p + 128, :] = g_ref[p, RA1 + 128:RA1 + 256, :]
            lo, hi = p * N_O_IN, (p + 1) * N_O_IN
            for ref, first, start, stop in ((w_ref, 0, lo, min(hi, n_main)),
                                            (f_ref, -n_main, max(lo, n_main), min(hi, n_main + FOX_HEADS)),
                                            (w_ref, -FOX_HEADS, max(lo, n_main + FOX_HEADS), hi)):
                if start < stop:
                    ref[start + first:stop + first, :] = g_ref[p, start - lo:stop - lo, :]

    return pl.pallas_call(
        body, name=name, grid=(1,), in_specs=[_resident((N_DEV, RA1 + 256, WIDE), lambda i: (0, 0, 0))],
        out_specs=[_resident((rows, WIDE), lambda i: (0, 0)) for rows in (n_main + D_MODEL, LANES, D_MODEL, D_MODEL)],
        out_shape=[jax.ShapeDtypeStruct((rows, WIDE), gath.dtype) for rows in (n_main + D_MODEL, LANES, D_MODEL, D_MODEL)],
        compiler_params=_params(("arbitrary",)),
    )(gath)


def _late_grads_pack(d_qkv, d_wft, d_gate, d_wo1, d_wo0, d_o_g, *, name):
    n_main = 3 * D_MODEL
    arrays = (d_qkv, d_wft, d_gate, d_wo1, d_wo0, d_o_g)

    def body(q_ref, f_ref, g_ref, o1_ref, o0_ref, og_ref, out_ref):
        for p in range(N_DEV):
            lo, hi = p * N_O_IN, (p + 1) * N_O_IN
            for ref, first, start, stop in ((q_ref, 0, lo, min(hi, n_main)),
                                            (f_ref, -n_main, max(lo, n_main), min(hi, n_main + FOX_HEADS)),
                                            (g_ref, -n_main - FOX_HEADS, max(lo, n_main + FOX_HEADS), hi)):
                if start < stop:
                    out_ref[p, start - lo:stop - lo, :] = ref[start + first:stop + first, :]
            out_ref[p, N_O_IN:RA1, :] = jnp.zeros((RA1 - N_O_IN, WIDE), out_ref.dtype)
            out_ref[p, RA1:RA1 + 128, :] = o1_ref[128 * p:128 * p + 128, :]
            out_ref[p, RA1 + 128:RA1 + 256, :] = o0_ref[128 * p:128 * p + 128, :]
            out_ref[p, RA1 + 256:, :] = og_ref[p]

    return pl.pallas_call(
        body, name=name, grid=(1,), in_specs=[_resident(a.shape, lambda i, n=a.ndim: (0,) * n) for a in arrays],
        out_specs=_resident((N_DEV, RA1 + RB1, WIDE), lambda i: (0, 0, 0)),
        out_shape=jax.ShapeDtypeStruct((N_DEV, RA1 + RB1, WIDE), BF16), compiler_params=_params(("arbitrary",)),
    )(*arrays)


def _q_up_weight(w):
    return jnp.pad(w.reshape(MLA_Q_RANK, MLA_HEADS, 96), ((0, 0), (0, 0), (0, 32))).reshape(MLA_Q_RANK, MLA_HEADS * LANES)


def _q_up_grad(dwp):
    return dwp.reshape(MLA_Q_RANK, MLA_HEADS, LANES)[:, :, :96].reshape(MLA_Q_RANK, MLA_HEADS * 96)


def _kv_up_weight(w):
    w4 = w.reshape(MLA_KV_RANK, MLA_HEADS, 2, 64)
    kp = jnp.pad(w4[:, :, 0, :], ((0, 0), (0, 0), (0, 64))).reshape(MLA_KV_RANK, MLA_HEADS * LANES)
    vp = w4[:, :, 1, :].reshape(MLA_KV_RANK, MLA_HEADS * 64)
    return jnp.concatenate([kp, vp], axis=1)


def _kv_up_grad(dwp):
    dk = dwp[:, :MLA_HEADS * LANES].reshape(MLA_KV_RANK, MLA_HEADS, LANES)[:, :, :64]
    dv = dwp[:, MLA_HEADS * LANES:].reshape(MLA_KV_RANK, MLA_HEADS, 64)
    return jnp.stack([dk, dv], axis=2).reshape(MLA_KV_RANK, MLA_HEADS * LANES)


def _pad_lanes(a):
    return jnp.pad(a, ((0, 0), (0, LANES - a.shape[1])))


def _small_pack(g_in, g_final, g_q_a, g_kv_a, sinks, b_f, loss):
    rows = [g_in.reshape(8, LANES), g_final.reshape(8, LANES), g_q_a.reshape(2, LANES), g_kv_a.reshape(1, LANES),
            _pad_lanes(sinks.reshape(1, -1)), _pad_lanes(b_f.reshape(1, -1)), _pad_lanes(loss.reshape(1, 1)),
            jnp.zeros((2, LANES), F32)]
    return jnp.concatenate(rows, axis=0)


def _small_unpack(a):
    return (a[0:8].reshape(1, D_MODEL), a[8:16].reshape(D_MODEL), a[16:18].reshape(1, MLA_Q_RANK),
            a[18:19].reshape(1, MLA_KV_RANK), a[19:20, :SWA_HEADS], a[20:21, :FOX_HEADS], a[21, 0])


def _local_step(x, positions, target, e_g_in, early, e_g_q_a, e_g_kv_a, e_sinks,
                late, o_b_f, g_final, scatter1=None, scatter0=None):
    s = x.shape[0]
    mla_scale = (MLA_NOPE + MLA_ROPE) ** -0.5
    fox_scale = FOX_DIM ** -0.5
    n0a = Z0A_UNITS * LANES

    inv_freq = 1.0 / (ROPE_THETA ** (jnp.arange(0, MLA_ROPE, 2, dtype=F32) / MLA_ROPE))
    ang = positions.astype(F32)[:, None] * inv_freq
    cos, sin = jnp.cos(ang), jnp.sin(ang)
    ones, zeros = jnp.ones((s, 64), F32), jnp.zeros((s, 64), F32)
    cos_t = jnp.concatenate([ones, cos, cos, ones[:, :32]], axis=1)
    sin_t = jnp.concatenate([zeros, -sin, sin, zeros[:, :32]], axis=1)
    cos_t, sin_t = lax.optimization_barrier((cos_t, sin_t))

    if len(early) == 3:
        h0 = _rmsnorm_fwd(x, e_g_in, width=D_MODEL, col_blk=0, name="l0_norm")
        w0t, wq, wkv = early
    else:
        pending, token, unpack, prep = early
        h0 = _rmsnorm_fwd(x, e_g_in, width=D_MODEL, col_blk=0, name="l0_norm", after=[token])
        sent, across = _peer_wait("across", *pending, after=[h0, cos_t, sin_t] + prep, name="weights0_wait")
        w0t, wq, wkv = unpack(sent, _to_other_core(sent, across, name="weights0_over"))
    z0a, z0b = _matmul_rows([(h0, w0t, True)], [], [], lambda r: (r[:, :n0a], r[:, n0a:]),
                            [("rows", n0a, F32), ("rows", Z0B_UNITS * LANES, BF16)], name="l0_in")
    cqn = _rmsnorm_fwd(z0a, e_g_q_a, width=MLA_Q_RANK, col_blk=4, name="l0_q_norm")
    ckvn = _rmsnorm_fwd(z0a, e_g_kv_a, width=MLA_KV_RANK, col_blk=10, name="l0_kv_norm")
    rope_rows = [(cos_t, LANES, 0), (sin_t, LANES, 0)]
    qm, = _matmul_rows([(cqn, wq, False)], rope_rows, [], _rope_q_epilogue, [("rows", MLA_HEADS * LANES, BF16)],
                       name="l0_q_up")
    kvm, = _matmul_rows([(ckvn, wkv, False)], [(z0a, LANES, 11)] + rope_rows, [], _rope_k_epilogue,
                        [("rows", MLA_HEADS * (LANES + MLA_V), BF16)], name="l0_kv_up")
    gathers = len(late) == 2
    res = _flash_fwd(qm, kvm, kvm, None, n_pairs=MLA_HEADS // 2, hw=LANES, q_off=0, k_off=0, v_off=MLA_HEADS,
                     scale=mla_scale, name="l0_mla_fwd", rider=("gather", late[0]) if gathers else None)
    o_mla, lse_mla = res[0], res[1]
    wo0, o_g_in, w1t, wft, wo1 = late[1](res[2]) if gathers else late
    o_swa, lse_swa = _swa_fwd(z0b, e_sinks, name="l0_swa_fwd")
    half = D_MODEL // 2

    x1, h1, og0 = _matmul_rows(
        [(None, wo0, False)], [(o_mla, half, 0), (o_swa, half, 0), (z0a, D_MODEL, 0), (x, D_MODEL, 0)], [o_g_in],
        lambda r, om, osw, gt, xt, g, made: (*_residual_norm_epilogue(r, xt, g), made),
        [("rows", D_MODEL, F32), ("rows", D_MODEL, BF16), ("rows", D_MODEL, BF16)], name="l0_out",
        prologue=lambda om, osw, gt, xt, g: _gated([om, osw], gt))
    z1, gate1, zf = _matmul_rows(
        [(None, w1t, True), (None, wft, True)], [(h1, D_MODEL, 0)], [],
        lambda r, h, made: (r[0][:, :3 * D_MODEL], r[0][:, 3 * D_MODEL:], r[1]),
        [("rows", 3 * D_MODEL, BF16), ("rows", D_MODEL, F32), ("rows", LANES, F32)], name="l1_in",
        prologue=lambda h: h, separate=True)
    bf = _pad_lanes(o_b_f)
    log_cum = _logf_fwd(zf, bf, name="l1_logf")
    bias2 = (-LOG2E * log_cum[:, :FOX_HEADS]).T
    t_bwd = min(ATT_T, s)
    bias = bias2.reshape(FOX_HEADS // 2, 2, s // t_bwd, 1, t_bwd)
    t_fwd = _fwd_tile(s)
    o_fox, lse_fox = _flash_fwd(z1, z1, z1, bias2.reshape(FOX_HEADS // 2, 2, s // t_fwd, 1, t_fwd),
                                n_pairs=FOX_HEADS // 2, hw=64, q_off=0, k_off=8, v_off=16, scale=fox_scale,
                                name="l1_fox_fwd")

    dx2, loss_part, d_g_final, og1, dx2_bf = _matmul_rows(
        [(None, wo1, False)], [(o_fox, D_MODEL, 0), (gate1, D_MODEL, 0), (x1, D_MODEL, 0), (target, D_MODEL, 0)],
        [g_final.reshape(1, D_MODEL)],
        lambda r, o, gt, xt, tg, g, made: _and_first(_loss_epilogue(r, xt, tg, g), made),
        [("rows", D_MODEL, F32), ("sum", (8, LANES)), ("sum", (1, D_MODEL)), ("rows", D_MODEL, BF16),
         ("rows", D_MODEL, BF16)], name="l1_out_loss", prologue=lambda o, gt, xt, tg, g: _gated([o], gt))

    d_wo1 = _matmul(og1, dx2_bf, ta=True, out_dtype=BF16, name="l1_out_dw")
    do_fox, d_gate1 = _matmul_rows([(dx2_bf, wo1, True)], [(o_fox, D_MODEL, 0), (gate1, D_MODEL, 0)], [],
                                   _gate_bwd_epilogue([D_MODEL]), [("rows", D_MODEL, F32), ("rows", D_MODEL, BF16)],
                                   name="l1_out_dx")
    dqkv1, dbias, drow = _flash_bwd(z1, z1, z1, do_fox, o_fox, lse_fox, bias, n_pairs=FOX_HEADS // 2, hw=64, q_off=0,
                                    k_off=8, v_off=16, scale=fox_scale, qk_dtype=BF16, stacked=True, name="l1_fox_bwd")
    d_log_cum = (drow.reshape(FOX_HEADS, s) - dbias.reshape(FOX_HEADS, s)).T
    d_log_cum = jnp.pad(d_log_cum, ((0, 0), (0, LANES - FOX_HEADS)))
    d_zf, d_bf = _logf_bwd(d_log_cum, zf, bf, name="l1_logf_bwd")
    d_w1t = (_matmul(dqkv1, h1, ta=True, out_dtype=BF16, name="l1_in_dw_qkv"),
             _matmul(d_gate1, h1, ta=True, out_dtype=BF16, name="l1_in_dw_gate"))
    d_wft = _matmul(d_zf, h1, ta=True, out_dtype=BF16, name="l1_in_f_dw")
    dx1, d_o_g_in, dx1_bf = _matmul_rows([(dqkv1, w1t, False, c * D_MODEL, c) for c in range(3)]
                                         + [(d_gate1, w1t, False, 3 * D_MODEL), (d_zf, wft, False)],
                                         [(x1, D_MODEL, 0), (dx2, D_MODEL, 0)], [o_g_in],
                                         lambda *a: _and_first(_rms_bwd_epilogue(*a)),
                                         [("rows", D_MODEL, F32), ("sum", (1, D_MODEL)), ("rows", D_MODEL, BF16)],
                                         name="l1_in_dx")

    d_wo0 = _matmul(og0, dx1_bf, ta=True, out_dtype=BF16, name="l0_out_dw")
    do_mla, do_swa, d_gate0 = _matmul_rows(
        [(dx1_bf, wo0, True)], [(o_mla, half, 0), (o_swa, half, 0), (z0a, D_MODEL, 0)], [], _gate_bwd_epilogue([half, half]),
        [("rows", half, F32), ("rows", half, F32), ("rows", D_MODEL, BF16)], name="l0_out_dx")
    dq_s, dkt_s, dvt_s, d_sinks = _swa_bwd(z0b, e_sinks, do_swa, o_swa, lse_swa, name="l0_swa_bwd")
    dk_s = dkt_s.transpose(0, 2, 1).reshape(s, LANES)
    dv_s = dvt_s.transpose(0, 2, 1).reshape(s, LANES)
    rider = None
    if scatter1 is not None:
        rider = ("exchange", scatter1(dict(w1t=d_w1t, wft=d_wft, wo1=d_wo1, o_g_in=d_o_g_in, wo0=d_wo0)))
    res = _flash_bwd(qm, kvm, kvm, do_mla, o_mla, lse_mla, None, n_pairs=MLA_HEADS // 2, hw=LANES, q_off=0, k_off=0,
                     v_off=MLA_HEADS, scale=mla_scale, qk_dtype=F32, name="l0_mla_bwd", rider=rider)
    dqm, dkm, dvm = res[0], res[1], res[2]
    recv1 = res[3] if rider is not None else None
    d_qp, d_kvp, d_kpe = _rope_bwd(dqm, dkm, dvm, cos_t, sin_t, name="l0_rope_bwd")
    d_wq = _matmul(cqn, d_qp, ta=True, out_dtype=BF16, name="l0_q_up_dw")
    d_cqn = _matmul(d_qp, wq, tb=True, name="l0_q_up_dx")
    d_wkv = _matmul(ckvn, d_kvp, ta=True, out_dtype=BF16, name="l0_kv_up_dw")
    d_ckvn = _matmul(d_kvp, wkv, tb=True, name="l0_kv_up_dx")
    d_cq, d_g_q_a = _rmsnorm_bwd(z0a, e_g_q_a, d_cqn, width=MLA_Q_RANK, col_blk=4, name="l0_q_norm_bwd")
    d_ckv, d_g_kv_a = _rmsnorm_bwd(z0a, e_g_kv_a, d_ckvn, width=MLA_KV_RANK, col_blk=10, name="l0_kv_norm_bwd")
    dz0 = jnp.concatenate([d_gate0, d_cq, d_ckv, d_kpe, dq_s.astype(BF16), dk_s.astype(BF16), dv_s.astype(BF16)], axis=1)
    d_w0t = _matmul(dz0, h0, ta=True, out_dtype=BF16, name="l0_in_dw")
    pending0, after_start = None, []
    if scatter0 is not None:
        *pending0, token = _peer_start("exchange", scatter0(dict(w0t=d_w0t, wq=d_wq, wkv=d_wkv)), name="grads0_start")
        after_start = [token]
    grad_x, d_e_g_in = _matmul_rows(
        [(dz0, w0t, False)], [(x, D_MODEL, 0), (dx1, D_MODEL, 0)], [e_g_in] + after_start,
        lambda dy, xt, add, g, *_: _rms_bwd_epilogue(dy, xt, add, g),
        [("rows", D_MODEL, F32), ("sum", (1, D_MODEL))], name="l0_in_dx")

    return dict(pending0=pending0, recv1=recv1, loss=loss_part[0, 0], grad_x=grad_x, e_g_in=d_e_g_in, w0t=d_w0t, e_g_q_a=d_g_q_a, wq=d_wq,
                e_g_kv_a=d_g_kv_a, wkv=d_wkv, e_sinks=d_sinks[:, 0].reshape(1, SWA_HEADS), wo0=d_wo0,
                o_g_in=d_o_g_in, w1t=d_w1t, wft=d_wft, o_b_f=d_bf[:, :FOX_HEADS], wo1=d_wo1, g_final=d_g_final.reshape(D_MODEL))


def _wide(a, rows):
    flat = a.reshape(-1)
    return jnp.pad(flat, (0, rows * WIDE - flat.shape[0])).reshape(rows, WIDE)


def _rows_b0(w_q, w_kv):
    return jnp.concatenate([_wide(w_q, 32), _wide(w_kv, 16)], axis=0)


def _unflat_b0(f):
    return f[0:24].reshape(1, MLA_Q_RANK, 96), f[32:48].reshape(1, MLA_KV_RANK, 128)


def _rows_b1(o_w_out, e_w_out, g_in):
    return jnp.concatenate([o_w_out, e_w_out, _wide(g_in, 16)], axis=0)


def _unflat_b1(f):
    return f[0:128][None], f[128:256][None], f[256:257, :LANES]


def kernel(x, positions, e_g_in, e_w_in, e_g_q_a, e_w_q_up, e_g_kv_a, e_w_kv_up, e_sinks, e_w_out, o_g_in, o_w_in, o_b_f, o_w_out, g_final, loss_target, m_e_g_in, m_e_w_in, m_e_g_q_a, m_e_w_q_up, m_e_g_kv_a, m_e_w_kv_up, m_e_sinks, m_e_w_out, m_o_g_in, m_o_w_in, m_o_b_f, m_o_w_out, m_g_final, v_e_g_in, v_e_w_in, v_e_g_q_a, v_e_w_q_up, v_e_g_kv_a, v_e_w_kv_up, v_e_sinks, v_e_w_out, v_o_g_in, v_o_w_in, v_o_b_f, v_o_w_out, v_g_final):
    def bf(a):
        return a.astype(BF16)

    me = 4 * lax.axis_index("x") + 2 * lax.axis_index("y") + lax.axis_index("c")
    shard0 = jnp.concatenate([_pad_rows(bf(e_w_in[0]).T, RA0), _rows_b0(bf(e_w_q_up[0]), bf(e_w_kv_up[0]))], axis=0)
    *pending_w0, token_w0 = _peer_start("across", shard0, name="weights0_start")

    def unpack0(sent, gath0):
        gath0 = lax.dynamic_update_slice_in_dim(gath0, sent[None], me, axis=0)
        w0t = _layer0_in_unpack(gath0, name="weights0_unpack")
        wq = _q_up_weight(_gathered_cols(gath0[:, RA0:RA0 + 24], MLA_Q_RANK))
        wkv = _kv_up_weight(_gathered_cols(gath0[:, RA0 + 32:RA0 + 48], MLA_KV_RANK))
        return w0t, wq, wkv

    rows_b0 = [_rows_b0(q[0], kv[0]) for q, kv in ((e_w_q_up, e_w_kv_up), (m_e_w_q_up, m_e_w_kv_up), (v_e_w_q_up, v_e_w_kv_up))]
    rows_b1 = [_rows_b1(o[0], e[0], g) for o, e, g in ((o_w_out, e_w_out, o_g_in), (m_o_w_out, m_e_w_out, m_o_g_in),
                                                       (v_o_w_out, v_e_w_out, v_o_g_in))]

    g_bits = lax.bitcast_convert_type(o_g_in.reshape(LANES), BF16)
    shard1 = jnp.concatenate([_pad_rows(bf(o_w_in[0]).T, RA1), _rows_b1(bf(o_w_out[0]), bf(e_w_out[0]), g_bits)], axis=0)

    def unpack1(gath1):
        w1t, wft, wo1, wo0 = _layer1_in_unpack(gath1, name="weights1_unpack")
        bits = gath1[:, RA1 + 256, :2 * LANES].reshape(N_DEV, LANES, 2)
        return wo0, lax.bitcast_convert_type(bits, F32).reshape(1, D_MODEL), w1t, wft, wo1

    def scatter1(g):
        d_o_g = jnp.pad(bf(g["o_g_in"]).reshape(N_DEV, 1, LANES), ((0, 0), (0, 15), (0, WIDE - LANES)))
        return _late_grads_pack(g["w1t"][0], g["wft"], g["w1t"][1], g["wo1"], g["wo0"], d_o_g, name="grads1_pack")

    def scatter0(g):
        return _early_grads_pack(g["w0t"], _pad_rows(_scatter_cols(_q_up_grad(g["wq"])), 32),
                                 _scatter_cols(_kv_up_grad(g["wkv"])), name="grads0_pack")

    gr = _local_step(x[0], positions[0], loss_target[0], e_g_in,
                     (pending_w0, token_w0, unpack0, [shard1] + rows_b0 + rows_b1), e_g_q_a, e_g_kv_a, e_sinks,
                     (shard1, unpack1), o_b_f, g_final, scatter1=scatter1, scatter0=scatter0)

    def in_projection(recv, ra, n, w, m, v, name):
        g = _sum8(recv, ra, name=name + "_grad_sum")[:n].reshape(n, 1, D_MODEL)
        w, m, v = [jnp.transpose(a, (2, 0, 1)) for a in (w, m, v)]
        return (g, *_adamw_columns(g, w, m, v, name=name + "_adamw"))

    o_in = in_projection(gr["recv1"], RA1, N_O_IN, o_w_in, m_o_w_in, v_o_w_in, "o_w_in")
    b1 = _adamw(gr["recv1"][:, RA1:], *rows_b1, name="adamw_late")

    small = _small_pack(gr["e_g_in"], gr["g_final"], gr["e_g_q_a"], gr["e_g_kv_a"], gr["e_sinks"], gr["o_b_f"], gr["loss"])
    small_all = _all_gather(small, name="small_all_gather")
    zero = jnp.zeros((), F32)
    w_small = _small_pack(e_g_in, g_final, e_g_q_a, e_g_kv_a, e_sinks, o_b_f, zero)
    m_small = _small_pack(m_e_g_in, m_g_final, m_e_g_q_a, m_e_g_kv_a, m_e_sinks, m_o_b_f, zero)
    v_small = _small_pack(v_e_g_in, v_g_final, v_e_g_q_a, v_e_g_kv_a, v_e_sinks, v_o_b_f, zero)
    smalls = _adamw(small_all, w_small, m_small, v_small, name="adamw_replicated")
    g_sm, d_sm, m_sm, v_sm = [_small_unpack(a) for a in smalls]
    loss = g_sm[6]

    sent0, recv0 = _peer_wait("exchange", *gr["pending0"], after=[o_in[1], b1[1], smalls[1]], name="grads0_wait")
    own = lax.dynamic_slice_in_dim(sent0, me, 1, axis=0)
    recv0 = lax.dynamic_update_slice_in_dim(recv0, own, me, axis=0)
    e_in = in_projection(recv0, RA0, N_E_IN, e_w_in, m_e_w_in, v_e_w_in, "e_w_in")
    b0 = _adamw(recv0[:, RA0:], *rows_b0, name="adamw_early")

    def sharded(k):
        q_up, kv_up = _unflat_b0(b0[k])
        o_out, e_out, o_g = _unflat_b1(b1[k])
        return jnp.transpose(e_in[k], (1, 2, 0)), q_up, kv_up, e_out, jnp.transpose(o_in[k], (1, 2, 0)), o_out, o_g

    g_sh, d_sh, m_sh, v_sh = [sharded(k) for k in range(4)]

    def leaves(sh, sm):
        return (sm[0], sh[0], sm[2], sh[1], sm[3], sh[2], sm[4], sh[3], sh[6], sh[4], sm[5], sh[5], sm[1])

    return (loss, gr["grad_x"][None], *leaves(g_sh, g_sm), *leaves(d_sh, d_sm), *leaves(m_sh, m_sm), *leaves(v_sh, v_sm))
```

```python
import functools

import jax
import jax.numpy as jnp
from jax import lax
from jax.experimental import pallas as pl
from jax.experimental.pallas import tpu as pltpu

F32 = jnp.float32
BF16 = jnp.bfloat16
NEG_INF = float("-inf")

N_DEV = 8
LANES = 128
D_MODEL = 1024
EPS = 1e-6
ROPE_THETA = 10000.0
MLA_HEADS = 8
MLA_Q_RANK = 256
MLA_KV_RANK = 128
MLA_NOPE = 64
MLA_ROPE = 32
MLA_V = 64
SWA_HEADS = 8
SWA_KV_HEADS = 2
SWA_DIM = 64
WINDOW = 128
FOX_HEADS = 16
FOX_DIM = 64

ADAM_LR = 0.001
ADAM_B1 = 0.9
ADAM_B2 = 0.999
ADAM_EPS = 1e-08
ADAM_WD = 0.01
ADAM_STEP = 10

ATT_T = 512
ATT_T_FWD = 1024
VMEM_LIMIT = 56 * 1024 * 1024
MATMUL_B_BLOCK_BYTES = 8 * 1024 * 1024

Z0A_UNITS = 12
Z0B_UNITS = 6

WIDE = 1024
N_E_IN = 276
N_O_IN = 514
RA0 = 288
RB0 = 32 + 16
RA1 = 528
RB1 = 128 + 128 + 16
SMALL_ROWS = 24


def _tile(n, cands):
    for c in cands:
        if n % c == 0:
            return c
    raise ValueError(f"no tile for {n}")


ROW_TILES = (512, 256, 128)


def _params(sem, vmem=VMEM_LIMIT):
    return pltpu.CompilerParams(dimension_semantics=sem, vmem_limit_bytes=vmem)


def _matmul(a, b, *, name, ta=False, tb=False, out_dtype=F32):
    if ta:
        kdim, m = a.shape[-2], a.shape[-1] * (a.shape[0] if a.ndim == 3 else 1)
    else:
        m, kdim = a.shape
    if tb:
        n, kb = b.shape
    else:
        kb, n = b.shape
    assert kdim == kb, (a.shape, b.shape)
    tm = _tile(m, (512, 256, 128))
    tn = _tile(n, [c for c in (1024, 768, 512, 384, 256, 128) if c * kdim * b.dtype.itemsize <= MATMUL_B_BLOCK_BYTES])
    dims = (((0 if ta else 1,), (1 if tb else 0,)), ((), ()))

    def body(a_ref, b_ref, o_ref):
        r = lax.dot_general(a_ref[...].astype(BF16), b_ref[...].astype(BF16), dims, preferred_element_type=F32)
        o_ref[...] = r.astype(out_dtype)

    if a.ndim == 3:
        per = a.shape[2] // tm
        a_spec = pl.BlockSpec((None, kdim, tm), lambda i, j: (i // per, 0, i % per))
    else:
        a_spec = pl.BlockSpec((kdim, tm), lambda i, j: (0, i)) if ta else pl.BlockSpec((tm, kdim), lambda i, j: (i, 0))
    b_spec = pl.BlockSpec((tn, kdim), lambda i, j: (j, 0)) if tb else pl.BlockSpec((kdim, tn), lambda i, j: (0, j))
    return pl.pallas_call(
        body, name=name, grid=(m // tm, n // tn), in_specs=[a_spec, b_spec],
        out_specs=pl.BlockSpec((tm, tn), lambda i, j: (i, j)), out_shape=jax.ShapeDtypeStruct((m, n), out_dtype),
        compiler_params=_params(("parallel", "parallel")),
    )(a, b)


def _rmsnorm_fwd(x, g, *, width, col_blk, name, after=()):
    s = x.shape[0]
    tm = _tile(s, ROW_TILES)

    def body(x_ref, g_ref, *rest):
        y_ref = rest[-1]
        xf = x_ref[...].astype(F32)
        r = lax.rsqrt(jnp.mean(xf * xf, axis=-1, keepdims=True) + EPS)
        y_ref[...] = ((xf * r) * g_ref[...]).astype(BF16)

    return pl.pallas_call(
        body, name=name, grid=(s // tm,),
        in_specs=[pl.BlockSpec((tm, width), lambda i: (i, col_blk)), pl.BlockSpec((1, width), lambda i: (0, 0))]
        + [ANY] * len(after),
        out_specs=pl.BlockSpec((tm, width), lambda i: (i, 0)),
        out_shape=jax.ShapeDtypeStruct((s, width), BF16),
        compiler_params=_params(("parallel",)),
    )(x, g, *after)


def _rmsnorm_bwd(x, g, dy, *, width, col_blk, name):
    s = x.shape[0]
    tm = _tile(s, ROW_TILES)

    def body(x_ref, g_ref, dy_ref, dx_ref, dg_ref):
        @pl.when(pl.program_id(0) == 0)
        def _():
            dg_ref[...] = jnp.zeros_like(dg_ref)

        dx, dg = _rms_bwd_epilogue(dy_ref[...], x_ref[...], 0.0, g_ref[...])
        dg_ref[...] += dg
        dx_ref[...] = dx.astype(BF16)

    return pl.pallas_call(
        body, name=name, grid=(s // tm,),
        in_specs=[pl.BlockSpec((tm, width), lambda i: (i, col_blk)), pl.BlockSpec((1, width), lambda i: (0, 0)),
                  pl.BlockSpec((tm, width), lambda i: (i, 0))],
        out_specs=[pl.BlockSpec((tm, width), lambda i: (i, 0)), pl.BlockSpec((1, width), lambda i: (0, 0))],
        out_shape=[jax.ShapeDtypeStruct((s, width), BF16), jax.ShapeDtypeStruct((1, width), F32)],
        compiler_params=_params(("arbitrary",)),
    )(x, g, dy)


def _sigmoid(x):
    return 1.0 / (1.0 + jnp.exp(-x))


def _matmul_rows(terms, row_inputs, params, epilogue, outs, *, name, prologue=None, separate=False):
    s = row_inputs[0][0].shape[0] if row_inputs else terms[0][0].shape[-2]
    tm = _tile(s, ROW_TILES)
    steps = s // tm
    n_r, n_p, n_o = len(row_inputs), len(params), len(outs)
    n_t = sum(1 if term[0] is None else 2 for term in terms)

    def body(*refs):
        t_refs, r_refs = list(refs[:n_t]), refs[n_t:n_t + n_r]
        p_refs, o_refs = refs[n_t + n_r:n_t + n_r + n_p], refs[n_t + n_r + n_p:]
        i = pl.program_id(0)
        rows, small = [r[...] for r in r_refs], [p[...] for p in p_refs]
        made = None if prologue is None else prologue(*rows, *small)
        parts = []
        for term in terms:
            a = made if term[0] is None else t_refs.pop(0)[...].astype(BF16)
            dims = (((1,), (1 if term[2] else 0,)), ((), ()))
            parts.append(lax.dot_general(a, t_refs.pop(0)[...].astype(BF16), dims, preferred_element_type=F32))
        acc = parts if separate else sum(parts[1:], parts[0])
        vals = epilogue(acc, *rows, *small) if prologue is None else epilogue(acc, *rows, *small, made)
        for ref, val, out in zip(o_refs, vals, outs):
            if out[0] == "rows":
                ref[...] = val.astype(ref.dtype)
            else:
                @pl.when(i == 0)
                def _(ref=ref):
                    ref[...] = jnp.zeros_like(ref)

                ref[...] += val

    in_specs, args = [], []
    for term in terms:
        a, b = term[0], term[1]
        if a is None:
            in_specs.append(_resident(b.shape, lambda i: (0, 0)))
            args.append(b)
            continue
        b_rows = b.shape[0] if term[2] or len(term) < 4 else a.shape[-1]
        b_blk = 0 if len(term) < 4 else term[3] // b_rows
        if len(term) == 5:
            a_spec = pl.BlockSpec((None, tm, a.shape[2]), lambda i, c=term[4]: (c, i, 0))
        else:
            a_spec = pl.BlockSpec((tm, a.shape[1]), lambda i: (i, 0))
        in_specs += [a_spec, _resident((b_rows, b.shape[1]), lambda i, b_blk=b_blk: (b_blk, 0))]
        args += [a, b]
    for arr, width, col_blk in row_inputs:
        in_specs.append(pl.BlockSpec((tm, width), lambda i, col_blk=col_blk: (i, col_blk)))
        args.append(arr)
    for p in params:
        in_specs.append(pl.BlockSpec(p.shape, lambda i: (0, 0)))
        args.append(p)
    out_specs, out_shape = [], []
    for out in outs:
        if out[0] == "rows":
            out_specs.append(pl.BlockSpec((tm, out[1]), lambda i: (i, 0)))
            out_shape.append(jax.ShapeDtypeStruct((s, out[1]), out[2]))
        else:
            out_specs.append(pl.BlockSpec(out[1], lambda i: (0, 0)))
            out_shape.append(jax.ShapeDtypeStruct(out[1], F32))
    return pl.pallas_call(
        body, name=name, grid=(steps,), in_specs=in_specs, out_specs=out_specs, out_shape=out_shape,
        compiler_params=_params(("arbitrary",)),
    )(*args)


def _rms_stats(x):
    r = lax.rsqrt(jnp.mean(x * x, axis=-1, keepdims=True) + EPS)
    return r, x * r


def _gated(o_parts, gate):
    o = o_parts[0] if len(o_parts) == 1 else jnp.concatenate(o_parts, axis=1)
    return (o * (gate * _sigmoid(gate))).astype(BF16)


def _and_first(vals, *more):
    return (*vals, *more, vals[0])


def _residual_norm_epilogue(r, x, g):
    x1 = x + r
    _, xh = _rms_stats(x1)
    return x1, xh * g


def _rms_bwd_epilogue(dy, x, add, g):
    r, xh = _rms_stats(x)
    dxh = dy * g
    dx = r * (dxh - xh * jnp.mean(dxh * xh, axis=-1, keepdims=True)) + add
    return dx, jnp.sum(dy * xh, axis=0, keepdims=True)


def _loss_epilogue(r, x1, target, g):
    rs, xh = _rms_stats(x1 + r)
    err = xh * g - target
    loss = jnp.broadcast_to(0.5 * jnp.sum(jnp.mean(err * err, axis=-1, keepdims=True)), (8, LANES))
    dy = err * (1.0 / D_MODEL)
    dxh = dy * g
    dx = rs * (dxh - xh * jnp.mean(dxh * xh, axis=-1, keepdims=True))
    return dx, loss, jnp.sum(dy * xh, axis=0, keepdims=True)


def _gate_bwd_epilogue(widths):
    def epilogue(d, *rows):
        o_parts, gt = rows[:-1], rows[-1]
        o = o_parts[0] if len(o_parts) == 1 else jnp.concatenate(o_parts, axis=1)
        sg = _sigmoid(gt)
        do = d * (gt * sg)
        d_gate = d * o * (sg * (1.0 + gt * (1.0 - sg)))
        cuts = [sum(widths[:k]) for k in range(len(widths) + 1)]
        return tuple(do[:, cuts[k]:cuts[k + 1]] for k in range(len(widths))) + (d_gate,)

    return epilogue


def _rot_half(x):
    lane = lax.broadcasted_iota(jnp.int32, x.shape, 1)
    return jnp.where(lane < 80, pltpu.roll(x, LANES - 16, axis=1), pltpu.roll(x, 16, axis=1))


def _rot_half_t(g):
    lane = lax.broadcasted_iota(jnp.int32, g.shape, 1)
    lo = (lane >= MLA_NOPE) & (lane < MLA_NOPE + MLA_ROPE // 2)
    hi = (lane >= MLA_NOPE + MLA_ROPE // 2) & (lane < MLA_NOPE + MLA_ROPE)
    return jnp.where(lo, pltpu.roll(g, LANES - 16, axis=1), jnp.where(hi, pltpu.roll(g, 16, axis=1), 0.0))


def _rope_q_epilogue(q, c, sn):
    heads = [q[:, h * LANES:(h + 1) * LANES] for h in range(MLA_HEADS)]
    return (jnp.concatenate([qh * c + _rot_half(qh) * sn for qh in heads], axis=1),)


def _rope_k_epilogue(kv, kpe, c, sn):
    kpe_r = kpe * c + _rot_half(kpe) * sn
    lane = lax.broadcasted_iota(jnp.int32, kpe.shape, 1)
    heads = [jnp.where(lane < MLA_NOPE, kv[:, h * LANES:(h + 1) * LANES], kpe_r) for h in range(MLA_HEADS)]
    return (jnp.concatenate(heads + [kv[:, MLA_HEADS * LANES:]], axis=1),)


def _rope_bwd(dqm, dkm, dvm, cos_t, sin_t, *, name):
    s = dqm.shape[0]
    tm = _tile(s, ROW_TILES)
    hw = MLA_HEADS * LANES
    vw = MLA_HEADS * MLA_V

    def body(dq_ref, dk_ref, dv_ref, c_ref, s_ref, dqp_ref, dkv_ref, dkpe_ref):
        c = c_ref[...]
        sn = s_ref[...]
        ksum = jnp.zeros((tm, LANES), F32)
        for h in range(MLA_HEADS):
            sl = slice(h * LANES, (h + 1) * LANES)
            dq = dq_ref[:, sl]
            dqp_ref[:, sl] = (dq * c + _rot_half_t(dq * sn)).astype(BF16)
            dk = dk_ref[:, sl]
            dkv_ref[:, sl] = dk.astype(BF16)
            ksum = ksum + dk
        dkv_ref[:, hw:] = dv_ref[...]
        lane = lax.broadcasted_iota(jnp.int32, ksum.shape, 1)
        dkpe = ksum * c + _rot_half_t(ksum * sn)
        dkpe_ref[...] = jnp.where((lane >= MLA_NOPE) & (lane < MLA_NOPE + MLA_ROPE), dkpe, 0.0).astype(BF16)

    return pl.pallas_call(
        body, name=name, grid=(s // tm,),
        in_specs=[pl.BlockSpec((tm, hw), lambda i: (i, 0)), pl.BlockSpec((tm, hw), lambda i: (i, 0)),
                  pl.BlockSpec((tm, vw), lambda i: (i, 0)),
                  pl.BlockSpec((tm, LANES), lambda i: (i, 0)), pl.BlockSpec((tm, LANES), lambda i: (i, 0))],
        out_specs=[pl.BlockSpec((tm, hw), lambda i: (i, 0)), pl.BlockSpec((tm, hw + vw), lambda i: (i, 0)),
                   pl.BlockSpec((tm, LANES), lambda i: (i, 0))],
        out_shape=[jax.ShapeDtypeStruct((s, hw), BF16), jax.ShapeDtypeStruct((s, hw + vw), BF16),
                   jax.ShapeDtypeStruct((s, LANES), BF16)],
        compiler_params=_params(("parallel",)),
    )(dqm, dkm, dvm, cos_t, sin_t)


def _head_mask(shape, a):
    lane = lax.broadcasted_iota(jnp.int32, shape, 1)
    return (lane >= 64 * a) & (lane < 64 * (a + 1))


_NT = (((1,), (1,)), ((), ()))
LOG2E = 1.4426950408889634


def _stack_heads(tile, hw):
    lane = lax.broadcasted_iota(jnp.int32, tile.shape, 1)
    z = jnp.zeros_like(tile)
    return jnp.concatenate([jnp.where(lane < hw, tile, z), jnp.where(lane >= hw, tile, z)], axis=0)


def _stacked_rows(r0, r1, t):
    n = r0.shape[-1]
    return jnp.concatenate([jnp.broadcast_to(r0, (t, n)), jnp.broadcast_to(r1, (t, n))], axis=0)


def _resident(block, index_map):
    return pl.BlockSpec(block, index_map, pipeline_mode=pl.Buffered(1))


def _fwd_tile(s):
    return ATT_T_FWD if s % ATT_T_FWD == 0 else min(ATT_T, s)


def _flash_fwd(q, k, v, bias, *, n_pairs, hw, q_off, k_off, v_off, scale, name, rider=None):
    s = q.shape[0]
    t = _fwd_tile(s)
    nb = s // t
    qw = 2 * hw
    has_bias = bias is not None
    c1 = scale * LOG2E

    def body(*refs):
        refs, ride_refs = _split_rider(refs, rider, n_in=4 if has_bias else 3, n_out=2)
        if has_bias:
            q_ref, k_ref, v_ref, b_ref, o_ref, lse_ref, vt_ref, bcol_ref = refs
        else:
            q_ref, k_ref, v_ref, o_ref, lse_ref, vt_ref = refs
            b_ref = bcol_ref = None
        _ride_start(rider, ride_refs, pl.program_id(0) == 0)
        row = lax.broadcasted_iota(jnp.int32, (t, t), 0)
        col = lax.broadcasted_iota(jnp.int32, (t, t), 1)
        cmask_t = jnp.concatenate([row <= col, row <= col], axis=1)
        lane_lt64 = lax.broadcasted_iota(jnp.int32, (t, LANES), 1) < 64

        def as_column(r):
            return jnp.broadcast_to(r, (8, r.shape[1])).T[:, 0:1]

        def v_block(j, _):
            c0 = pl.multiple_of(j * t, t)
            vt_ref[j] = v_ref[pl.ds(c0, t), :].astype(F32).T.astype(BF16)
            if has_bias:
                for a in range(2):
                    bcol_ref[a, pl.ds(c0, t), :] = as_column(b_ref[0, a, j])
            return 0

        lax.fori_loop(0, nb, v_block, 0)

        def stacked_queries(i):
            return _stack_heads(q_ref[pl.ds(pl.multiple_of(i * t, t), t), :], hw).astype(F32).T.astype(BF16)

        def kv_step(j, carry, qs_t, masked):
            m, l, acc = carry
            rows = pl.ds(pl.multiple_of(j * t, t), t)
            sc = jnp.dot(k_ref[rows, :], qs_t, preferred_element_type=F32) * c1
            if has_bias:
                sc = sc + jnp.concatenate([jnp.broadcast_to(bcol_ref[0, rows, :], (t, t)),
                                           jnp.broadcast_to(bcol_ref[1, rows, :], (t, t))], axis=1)
            if masked:
                sc = jnp.where(cmask_t, sc, NEG_INF)
            m_new = jnp.maximum(m, jnp.max(sc, axis=0, keepdims=True))
            alpha = jnp.exp2(m - m_new)
            p = jnp.exp2(sc - m_new)
            l_new = alpha * l + jnp.sum(p, axis=0, keepdims=True)
            pv = jnp.dot(vt_ref[j], p.astype(BF16), preferred_element_type=F32)
            return m_new, l_new, alpha * acc + pv

        def finish(i, carry):
            m, l, acc = carry
            r0 = pl.multiple_of(i * t, t)
            out = (acc / l).T
            lse2 = as_column(m + jnp.log2(l))
            lse_ref[0, 0, pl.ds(r0, t), :] = lse2[:t]
            lse_ref[0, 1, pl.ds(r0, t), :] = lse2[t:]
            o_ref[pl.ds(r0, t), :] = jnp.where(lane_lt64, out[:t], out[t:])

        init = (jnp.full((1, 2 * t), NEG_INF, F32), jnp.zeros((1, 2 * t), F32), jnp.zeros((LANES, 2 * t), F32))

        def q_block(i, _):
            qs_t = stacked_queries(i)
            carry = lax.fori_loop(0, i, lambda j, c: kv_step(j, c, qs_t, False), init)
            finish(i, kv_step(i, carry, qs_t, True))
            return 0

        lax.fori_loop(0, nb, q_block, 0)
        _ride_wait(rider, ride_refs, pl.program_id(0) == n_pairs - 1)

    in_specs = [_resident((s, qw), lambda p: (0, q_off + p)), _resident((s, qw), lambda p: (0, k_off + p)),
                _resident((s, LANES), lambda p: (0, v_off + p))]
    args = [q, k, v]
    if has_bias:
        in_specs.append(_resident((1, 2, nb, 1, t), lambda p: (p, 0, 0, 0, 0)))
        args.append(bias)
    out_specs = [pl.BlockSpec((s, LANES), lambda p: (0, p)), pl.BlockSpec((1, 2, s, 1), lambda p: (p, 0, 0, 0))]
    out_shape = [jax.ShapeDtypeStruct((s, n_pairs * LANES), F32), jax.ShapeDtypeStruct((n_pairs, 2, s, 1), F32)]
    scratch = [pltpu.VMEM((nb, LANES, t), BF16)] + ([pltpu.VMEM((2, s, 1), F32)] if has_bias else [])
    scratch += _add_rider(rider, in_specs, args, out_specs, out_shape)
    return pl.pallas_call(
        body, name=name, grid=(n_pairs,), in_specs=in_specs, out_specs=out_specs, out_shape=out_shape,
        scratch_shapes=scratch,
        compiler_params=_params(("parallel",) if rider is None else ("arbitrary",)),
    )(*args)


def _flash_bwd(q, k, v, do, o, lse, bias, *, n_pairs, hw, q_off, k_off, v_off, scale, qk_dtype, name, rider=None,
               stacked=False):
    s = q.shape[0]
    t = min(ATT_T, s)
    nb = s // t
    qw = 2 * hw
    has_bias = bias is not None
    c1 = scale * LOG2E

    def body(*refs):
        n_grads = 1 if stacked else 3
        refs, ride_refs = _split_rider(refs, rider, n_in=7 if has_bias else 6, n_out=n_grads + (2 if has_bias else 0))
        if stacked:
            refs = list(refs)
            n_in = 7 if has_bias else 6
            refs[n_in:n_in + 1] = [refs[n_in].at[0], refs[n_in].at[1], refs[n_in].at[2]]
        if has_bias:
            (q_ref, k_ref, v_ref, do_ref, o_ref, lse_ref, b_ref, dq_ref, dk_ref, dv_ref, db_ref, dr_ref,
             dkt_ref, dvt_ref) = refs
            db_ref[...] = jnp.zeros_like(db_ref)
        else:
            q_ref, k_ref, v_ref, do_ref, o_ref, lse_ref, dq_ref, dk_ref, dv_ref, dkt_ref, dvt_ref = refs
            b_ref = db_ref = dr_ref = None
        _ride_start(rider, ride_refs, pl.program_id(0) == 0)
        dkt_ref[...] = jnp.zeros_like(dkt_ref)
        dvt_ref[...] = jnp.zeros_like(dvt_ref)
        causal = lax.broadcasted_iota(jnp.int32, (t, t), 1) <= lax.broadcasted_iota(jnp.int32, (t, t), 0)
        cmask = jnp.concatenate([causal, causal], axis=0)
        lane_lt_hw = lax.broadcasted_iota(jnp.int32, (t, qw), 1) < hw

        def q_block(i, _):
            r0 = pl.multiple_of(i * t, t)
            qs = _stack_heads(q_ref[pl.ds(r0, t), :], hw)
            dos = _stack_heads(do_ref[pl.ds(r0, t), :], 64)
            ot = o_ref[pl.ds(r0, t), :]
            delta = jnp.sum(dos * jnp.concatenate([ot, ot], axis=0), axis=-1, keepdims=True)
            lse2 = jnp.concatenate([lse_ref[0, 0, pl.ds(r0, t), :], lse_ref[0, 1, pl.ds(r0, t), :]], axis=0)
            dosb = dos.astype(BF16)
            dos_t = dos.T.astype(BF16)
            qs_t = qs.astype(F32).T.astype(BF16)

            def kv_step(j, carry, masked):
                dq, rsum = carry
                c0 = pl.multiple_of(j * t, t)
                kt = k_ref[pl.ds(c0, t), :]
                vt = v_ref[pl.ds(c0, t), :]
                sc = lax.dot_general(qs, kt, _NT, preferred_element_type=F32) * c1
                if has_bias:
                    sc = sc + _stacked_rows(b_ref[0, 0, j], b_ref[0, 1, j], t)
                if masked:
                    sc = jnp.where(cmask, sc, NEG_INF)
                p = jnp.exp2(sc - lse2)
                dp = lax.dot_general(dosb, vt, _NT, preferred_element_type=F32)
                ds = p * (dp - delta)
                dsb = ds.astype(BF16)
                pb = p.astype(BF16)
                if hw == LANES:
                    dvt_ref[j] += jnp.concatenate(
                        [jnp.dot(dos_t[:64, :t], pb[:t], preferred_element_type=F32),
                         jnp.dot(dos_t[64:, t:], pb[t:], preferred_element_type=F32)], axis=0)
                    dkt_ref[j] += jnp.concatenate(
                        [jnp.dot(qs_t[:hw, :t], dsb[:t], preferred_element_type=F32),
                         jnp.dot(qs_t[hw:, t:], dsb[t:], preferred_element_type=F32)], axis=0)
                else:
                    dvt_ref[j] += jnp.dot(dos_t, pb, preferred_element_type=F32)
                    dkt_ref[j] += jnp.dot(qs_t, dsb, preferred_element_type=F32)
                if has_bias:
                    db_ref[0, 0, j] += jnp.sum(ds[:t], axis=0, keepdims=True)
                    db_ref[0, 1, j] += jnp.sum(ds[t:], axis=0, keepdims=True)
                    rsum = rsum + jnp.sum(ds, axis=-1, keepdims=True)
                return dq + jnp.dot(dsb, kt, preferred_element_type=F32), rsum

            init = (jnp.zeros((2 * t, qw), F32), jnp.zeros((2 * t, 1), F32))
            carry = lax.fori_loop(0, i, functools.partial(kv_step, masked=False), init)
            dq, rsum = kv_step(i, carry, True)
            dq = dq * scale
            dq_ref[pl.ds(r0, t), :] = jnp.where(lane_lt_hw, dq[:t], dq[t:]).astype(qk_dtype)
            if has_bias:
                rsum_row = jnp.broadcast_to(rsum, (2 * t, LANES)).T[0:1]
                dr_ref[0, 0, i] = rsum_row[:, :t]
                dr_ref[0, 1, i] = rsum_row[:, t:]
            return 0

        lax.fori_loop(0, nb, q_block, 0)

        def k_block(j, _):
            c0 = pl.multiple_of(j * t, t)
            dk_ref[pl.ds(c0, t), :] = (dkt_ref[j].T * scale).astype(qk_dtype)
            dv_ref[pl.ds(c0, t), :] = dvt_ref[j].T.astype(BF16)
            return 0

        lax.fori_loop(0, nb, k_block, 0)
        _ride_wait(rider, ride_refs, pl.program_id(0) == n_pairs - 1)

    in_specs = [_resident((s, qw), lambda p: (0, q_off + p)), _resident((s, qw), lambda p: (0, k_off + p)),
                _resident((s, LANES), lambda p: (0, v_off + p)),
                _resident((s, LANES), lambda p: (0, p)), _resident((s, LANES), lambda p: (0, p)),
                _resident((1, 2, s, 1), lambda p: (p, 0, 0, 0))]
    args = [q, k, v, do, o, lse]
    if stacked:
        assert qw == LANES and qk_dtype == BF16
        out_specs = [pl.BlockSpec((3, s, LANES), lambda p: (0, 0, p))]
        out_shape = [jax.ShapeDtypeStruct((3, s, n_pairs * LANES), BF16)]
    else:
        out_specs = [pl.BlockSpec((s, qw), lambda p: (0, p)), pl.BlockSpec((s, qw), lambda p: (0, p)),
                     pl.BlockSpec((s, LANES), lambda p: (0, p))]
        out_shape = [jax.ShapeDtypeStruct((s, n_pairs * qw), qk_dtype), jax.ShapeDtypeStruct((s, n_pairs * qw), qk_dtype),
                     jax.ShapeDtypeStruct((s, n_pairs * LANES), BF16)]
    if has_bias:
        in_specs.append(_resident((1, 2, nb, 1, t), lambda p: (p, 0, 0, 0, 0)))
        args.append(bias)
        for _ in range(2):
            out_specs.append(pl.BlockSpec((1, 2, nb, 1, t), lambda p: (p, 0, 0, 0, 0)))
            out_shape.append(jax.ShapeDtypeStruct((n_pairs, 2, nb, 1, t), F32))
    scratch = [pltpu.VMEM((nb, qw, t), F32), pltpu.VMEM((nb, LANES, t), F32)]
    scratch += _add_rider(rider, in_specs, args, out_specs, out_shape)
    return pl.pallas_call(
        body, name=name, grid=(n_pairs,), in_specs=in_specs, out_specs=out_specs, out_shape=out_shape,
        scratch_shapes=scratch,
        compiler_params=_params(("parallel",) if rider is None else ("arbitrary",)),
    )(*args)


def _alibi_slope(h):
    return 2.0 ** (-8.0 * (h + 1.0) / SWA_HEADS)


SWA_ROWS = 512
SWA_SCALE = SWA_DIM ** -0.5


def _swa_geometry(i):
    w = WINDOW
    r0 = pl.multiple_of(i * w, w)
    b0 = pl.multiple_of(jnp.maximum(i - 1, 0) * w, w)
    row = lax.broadcasted_iota(jnp.int32, (w, 2 * w), 0)
    col = lax.broadcasted_iota(jnp.int32, (w, 2 * w), 1)
    dist = row - col + (r0 - b0)
    valid = (dist >= 0) & (dist < w)
    return r0, b0, dist.astype(F32), valid


def _swa_q_head(qblk, h):
    kv = h // (SWA_HEADS // SWA_KV_HEADS)
    if h % 2 != kv:
        qblk = pltpu.roll(qblk, 64, axis=1)
    return jnp.where(_head_mask(qblk.shape, kv), qblk, 0.0)


SWA_GROUP = SWA_HEADS // SWA_KV_HEADS


def _swa_stack(ref, rs, grp):
    parts = []
    for a in range(SWA_GROUP):
        h = SWA_GROUP * grp + a
        parts.append(_swa_q_head(ref[rs, (h // 2) * LANES:(h // 2 + 1) * LANES].astype(F32), h))
    return jnp.concatenate(parts, axis=0)


def _swa_unstack(x, grp):
    tiles = []
    for a in range(SWA_GROUP):
        h = SWA_GROUP * grp + a
        tile = x[a * WINDOW:(a + 1) * WINDOW]
        tiles.append(pltpu.roll(tile, 64, axis=1) if h % 2 != grp else tile)
    return tiles


def _swa_head_column(vals):
    return jnp.concatenate([jnp.full((WINDOW, 1), v, F32) for v in vals], axis=0)


def _swa_logits(qs, kb, dist, valid, grp):
    slopes = _swa_head_column([_alibi_slope(SWA_GROUP * grp + a) for a in range(SWA_GROUP)])
    dist4 = jnp.concatenate([dist] * SWA_GROUP, axis=0)
    valid4 = jnp.concatenate([valid] * SWA_GROUP, axis=0)
    sc = lax.dot_general(qs, kb, _NT, preferred_element_type=F32) * SWA_SCALE - slopes * dist4
    return jnp.where(valid4, sc, NEG_INF)


def _swa_merge_heads(tiles):
    lt64 = lax.broadcasted_iota(jnp.int32, (WINDOW, LANES), 1) < 64
    return jnp.concatenate([jnp.where(lt64, tiles[2 * b], tiles[2 * b + 1]) for b in range(SWA_HEADS // 2)], axis=1)


def _swa_fwd(z0b, sinks, *, name):
    s = z0b.shape[0]
    w = WINDOW
    rows = min(SWA_ROWS, s)
    per_step = rows // w
    qcols = SWA_HEADS * SWA_DIM

    def body(sink_ref, q_ref, k_ref, v_ref, o_ref, lse_ref):
        g = pl.program_id(0)
        for ii in range(per_step):
            rs = slice(ii * w, (ii + 1) * w)
            r0, b0, dist, valid = _swa_geometry(g * per_step + ii)
            kb = k_ref[pl.ds(b0, 2 * w), :]
            vb = v_ref[pl.ds(b0, 2 * w), :]
            o_tiles = []
            for h in range(SWA_HEADS):
                kv = h // SWA_GROUP
                qh = _swa_q_head(q_ref[rs, (h // 2) * LANES:(h // 2 + 1) * LANES].astype(F32), h).astype(BF16)
                sc = lax.dot_general(qh, kb, _NT, preferred_element_type=F32) * SWA_SCALE - _alibi_slope(h) * dist
                sc = jnp.where(valid, sc, NEG_INF)
                sink = sink_ref[0, h]
                m = jnp.maximum(jnp.max(sc, axis=-1, keepdims=True), sink)
                p = jnp.exp(sc - m)
                l = jnp.sum(p, axis=-1, keepdims=True) + jnp.exp(sink - m)
                oh = jnp.dot(p.astype(BF16), vb, preferred_element_type=F32) / l
                o_tiles.append(pltpu.roll(oh, 64, axis=1) if h % 2 != kv else oh)
                lse_ref[h, rs, :] = m + jnp.log(l)
            o_ref[rs, :] = _swa_merge_heads(o_tiles)

    return pl.pallas_call(
        body, name=name, grid=(s // rows,),
        in_specs=[pl.BlockSpec(memory_space=pltpu.SMEM),
                  pl.BlockSpec((rows, qcols), lambda g: (g, 0)),
                  pl.BlockSpec((s, LANES), lambda g: (0, 4)), pl.BlockSpec((s, LANES), lambda g: (0, 5))],
        out_specs=[pl.BlockSpec((rows, qcols), lambda g: (g, 0)), pl.BlockSpec((SWA_HEADS, rows, 1), lambda g: (0, g, 0))],
        out_shape=[jax.ShapeDtypeStruct((s, qcols), F32), jax.ShapeDtypeStruct((SWA_HEADS, s, 1), F32)],
        compiler_params=_params(("parallel",)),
    )(sinks, z0b, z0b, z0b)


def _swa_bwd(z0b, sinks, do, o, lse, *, name):
    s = z0b.shape[0]
    w = WINDOW
    rows = min(SWA_ROWS, s)
    per_step = rows // w
    qcols = SWA_HEADS * SWA_DIM
    nblk = s // w

    def body(sink_ref, q_ref, k_ref, v_ref, do_ref, o_ref, lse_ref, dq_ref, dkt_ref, dvt_ref, dsink_ref):
        g = pl.program_id(0)

        @pl.when(g == 0)
        def _():
            dkt_ref[...] = jnp.zeros_like(dkt_ref)
            dvt_ref[...] = jnp.zeros_like(dvt_ref)
            dsink_ref[...] = jnp.zeros_like(dsink_ref)

        for ii in range(per_step):
            i = g * per_step + ii
            rs = slice(ii * w, (ii + 1) * w)
            r0, b0, dist, valid = _swa_geometry(i)
            j0 = jnp.maximum(i - 1, 0)
            kb = k_ref[pl.ds(b0, 2 * w), :]
            vb = v_ref[pl.ds(b0, 2 * w), :]
            dq_tiles = []
            for grp in range(SWA_KV_HEADS):
                heads = [SWA_GROUP * grp + a for a in range(SWA_GROUP)]
                qs32 = _swa_stack(q_ref, rs, grp)
                dos32 = _swa_stack(do_ref, rs, grp)
                delta = jnp.sum(dos32 * _swa_stack(o_ref, rs, grp), axis=-1, keepdims=True)
                lse = jnp.concatenate([lse_ref[h, rs, :] for h in heads], axis=0)
                sink = _swa_head_column([sink_ref[0, h] for h in heads])
                p = jnp.exp(_swa_logits(qs32.astype(BF16), kb, dist, valid, grp) - lse)
                dp = lax.dot_general(dos32.astype(BF16), vb, _NT, preferred_element_type=F32)
                ds = p * (dp - delta)
                dsb = ds.astype(BF16)
                d_sink = jnp.exp(sink - lse) * delta
                for a, h in enumerate(heads):
                    dsink_ref[h:h + 1, :] += jnp.broadcast_to(-jnp.sum(d_sink[a * w:(a + 1) * w]), (1, LANES))
                dvt = jnp.dot(dos32.T.astype(BF16), p.astype(BF16), preferred_element_type=F32)
                dkt = jnp.dot(qs32.T.astype(BF16), dsb, preferred_element_type=F32) * SWA_SCALE
                dvt_ref[j0] += dvt[:, :w]
                dvt_ref[j0 + 1] += dvt[:, w:]
                dkt_ref[j0] += dkt[:, :w]
                dkt_ref[j0 + 1] += dkt[:, w:]
                dq_tiles += _swa_unstack(jnp.dot(dsb, kb, preferred_element_type=F32) * SWA_SCALE, grp)
            dq_ref[rs, :] = _swa_merge_heads(dq_tiles)

    return pl.pallas_call(
        body, name=name, grid=(s // rows,),
        in_specs=[pl.BlockSpec(memory_space=pltpu.SMEM),
                  pl.BlockSpec((rows, qcols), lambda g: (g, 0)),
                  pl.BlockSpec((s, LANES), lambda g: (0, 4)), pl.BlockSpec((s, LANES), lambda g: (0, 5)),
                  pl.BlockSpec((rows, qcols), lambda g: (g, 0)), pl.BlockSpec((rows, qcols), lambda g: (g, 0)),
                  pl.BlockSpec((SWA_HEADS, rows, 1), lambda g: (0, g, 0))],
        out_specs=[pl.BlockSpec((rows, qcols), lambda g: (g, 0)),
                   pl.BlockSpec((nblk, LANES, w), lambda g: (0, 0, 0)),
                   pl.BlockSpec((nblk, LANES, w), lambda g: (0, 0, 0)),
                   pl.BlockSpec((SWA_HEADS, LANES), lambda g: (0, 0))],
        out_shape=[jax.ShapeDtypeStruct((s, qcols), F32),
                   jax.ShapeDtypeStruct((nblk, LANES, w), F32), jax.ShapeDtypeStruct((nblk, LANES, w), F32),
                   jax.ShapeDtypeStruct((SWA_HEADS, LANES), F32)],
        compiler_params=_params(("arbitrary",)),
    )(sinks, z0b, z0b, z0b, do, o, lse)


CUM_T = 256


def _split3(x):
    hi = x.astype(BF16)
    r1 = x - hi.astype(F32)
    mid = r1.astype(BF16)
    lo = (r1 - mid.astype(F32)).astype(BF16)
    return hi, mid, lo


def _tri_dot(tri, x):
    hi, mid, lo = _split3(x)
    out = jnp.dot(tri, hi, preferred_element_type=F32)
    out = out + jnp.dot(tri, mid, preferred_element_type=F32)
    return out + jnp.dot(tri, lo, preferred_element_type=F32)


def _logf_fwd(zf, bf, *, name):
    s = zf.shape[0]
    t = CUM_T
    nb = s // t

    def body(z_ref, b_ref, c_ref, carry_ref):
        i = pl.program_id(0)

        @pl.when(i == 0)
        def _():
            carry_ref[...] = jnp.zeros_like(carry_ref)

        x = z_ref[...] + b_ref[...]
        lf = jnp.minimum(x, 0.0) - jnp.log(1.0 + jnp.exp(-jnp.abs(x)))
        row = lax.broadcasted_iota(jnp.int32, (t, t), 0)
        col = lax.broadcasted_iota(jnp.int32, (t, t), 1)
        tri = jnp.where(col <= row, 1.0, 0.0).astype(BF16)
        c = _tri_dot(tri, lf) + carry_ref[...]
        c_ref[...] = c
        carry_ref[...] = c[t - 1:t, :]

    return pl.pallas_call(
        body, name=name, grid=(nb,),
        in_specs=[pl.BlockSpec((t, LANES), lambda i: (i, 0)), pl.BlockSpec((1, LANES), lambda i: (0, 0))],
        out_specs=pl.BlockSpec((t, LANES), lambda i: (i, 0)),
        out_shape=jax.ShapeDtypeStruct((s, LANES), F32),
        scratch_shapes=[pltpu.VMEM((1, LANES), F32)],
        compiler_params=_params(("arbitrary",)),
    )(zf, bf)


def _logf_bwd(dc, zf, bf, *, name):
    s = zf.shape[0]
    t = CUM_T
    nb = s // t

    def body(dc_ref, z_ref, b_ref, dz_ref, db_ref, carry_ref):
        i = pl.program_id(0)

        @pl.when(i == 0)
        def _():
            carry_ref[...] = jnp.zeros_like(carry_ref)
            db_ref[...] = jnp.zeros_like(db_ref)

        row = lax.broadcasted_iota(jnp.int32, (t, t), 0)
        col = lax.broadcasted_iota(jnp.int32, (t, t), 1)
        tri = jnp.where(col >= row, 1.0, 0.0).astype(BF16)
        dlf = _tri_dot(tri, dc_ref[...]) + carry_ref[...]
        carry_ref[...] = dlf[0:1, :]
        x = z_ref[...] + b_ref[...]
        dz = dlf * _sigmoid(-x)
        dz_ref[...] = dz.astype(BF16)
        db_ref[...] += jnp.sum(dz, axis=0, keepdims=True)

    return pl.pallas_call(
        body, name=name, grid=(nb,),
        in_specs=[pl.BlockSpec((t, LANES), lambda i: (nb - 1 - i, 0)), pl.BlockSpec((t, LANES), lambda i: (nb - 1 - i, 0)),
                  pl.BlockSpec((1, LANES), lambda i: (0, 0))],
        out_specs=[pl.BlockSpec((t, LANES), lambda i: (nb - 1 - i, 0)), pl.BlockSpec((1, LANES), lambda i: (0, 0))],
        out_shape=[jax.ShapeDtypeStruct((s, LANES), BF16), jax.ShapeDtypeStruct((1, LANES), F32)],
        scratch_shapes=[pltpu.VMEM((1, LANES), F32)],
        compiler_params=_params(("arbitrary",)),
    )(dc, zf, bf)


def _sum_pieces(p_ref):
    g = p_ref[0].astype(F32)
    for k in range(1, N_DEV):
        g = g + p_ref[k].astype(F32)
    return g


def _adam_update(g, w, m, v):
    bc1 = 1.0 - ADAM_B1 ** ADAM_STEP
    bc2 = 1.0 - ADAM_B2 ** ADAM_STEP
    nm = ADAM_B1 * m + (1.0 - ADAM_B1) * g
    nv = ADAM_B2 * v + (1.0 - ADAM_B2) * (g * g)
    m_hat = nm / bc1
    v_hat = nv / bc2
    return -ADAM_LR * (m_hat / (jnp.sqrt(v_hat) + ADAM_EPS) + ADAM_WD * w), nm, nv


def _adamw(pieces, w, m, v, *, name):
    rows, cols = w.shape
    tr = _tile(rows, (RB1, RB0, SMALL_ROWS))

    def body(p_ref, w_ref, m_ref, v_ref, g_ref, d_ref, nm_ref, nv_ref):
        g = _sum_pieces(p_ref)
        g_ref[...] = g
        d_ref[...], nm_ref[...], nv_ref[...] = _adam_update(g, w_ref[...], m_ref[...], v_ref[...])

    spec = pl.BlockSpec((tr, cols), lambda i: (i, 0))
    shape = jax.ShapeDtypeStruct((rows, cols), F32)
    return pl.pallas_call(
        body, name=name, grid=(rows // tr,),
        in_specs=[pl.BlockSpec((N_DEV, tr, cols), lambda i: (0, i, 0)), spec, spec, spec],
        out_specs=[spec, spec, spec, spec], out_shape=[shape, shape, shape, shape],
        compiler_params=_params(("parallel",)),
    )(pieces, w, m, v)


def _sum8(pieces, rows, *, name):
    cols = pieces.shape[2]
    tr = _tile(rows, (176, 96))

    def body(p_ref, g_ref):
        g_ref[...] = _sum_pieces(p_ref)

    return pl.pallas_call(
        body, name=name, grid=(rows // tr,),
        in_specs=[pl.BlockSpec((N_DEV, tr, cols), lambda i: (0, i, 0))],
        out_specs=pl.BlockSpec((tr, cols), lambda i: (i, 0)),
        out_shape=jax.ShapeDtypeStruct((rows, cols), F32),
        compiler_params=_params(("parallel",)),
    )(pieces)


def _adamw_columns(g, w, m, v, *, name):
    n, _, k = w.shape
    tr = n // 2

    def body(g_ref, w_ref, m_ref, v_ref, d_ref, nm_ref, nv_ref):
        d_ref[...], nm_ref[...], nv_ref[...] = _adam_update(g_ref[...], w_ref[...], m_ref[...], v_ref[...])

    spec = pl.BlockSpec((tr, 1, k), lambda i: (i, 0, 0))
    shape = jax.ShapeDtypeStruct((n, 1, k), F32)
    return pl.pallas_call(
        body, name=name, grid=(n // tr,), in_specs=[spec, spec, spec, spec],
        out_specs=[spec, spec, spec], out_shape=[shape, shape, shape],
        compiler_params=_params(("parallel",)),
    )(g, w, m, v)


MESH = pl.DeviceIdType.MESH
ANY = pl.BlockSpec(memory_space=pl.ANY)


def _all_gather(shard, *, name):
    rows, lanes = shard.shape

    def body(x_ref, out_ref, send_sems, recv_sems, local_sem):
        x, y, c = lax.axis_index("x"), lax.axis_index("y"), lax.axis_index("c")
        me, sibling = (x, y, c), (x, y, 1 - c)
        chips = [(1 - x, y), (x, 1 - y), (1 - x, 1 - y)]

        def block(px, py, pc):
            return out_ref.at[4 * px + 2 * py + pc]

        def copy(k, blk, to, src=None):
            return pltpu.make_async_remote_copy(
                src_ref=block(*blk) if src is None else src, dst_ref=block(*blk),
                send_sem=send_sems.at[k], recv_sem=recv_sems.at[k], device_id=to, device_id_type=MESH)

        mine = pltpu.make_async_copy(x_ref, block(*me), local_sem)
        mine.start()
        first = [copy(0, me, sibling, src=x_ref)]
        first += [copy(1 + j, me, (*chip, c), src=x_ref) for j, chip in enumerate(chips)]
        for cp in first:
            cp.start()
        passed = [copy(4 + j, (*chip, c), sibling) for j, chip in enumerate(chips)]
        for j, chip in enumerate(chips):
            copy(1 + j, (*chip, c), me).wait_recv()
            passed[j].start()
        copy(0, sibling, me).wait_recv()
        for j, chip in enumerate(chips):
            copy(4 + j, (*chip, 1 - c), me).wait_recv()
        for cp in first + passed:
            cp.wait_send()
        mine.wait()

    return pl.pallas_call(
        body, name=name, out_shape=jax.ShapeDtypeStruct((N_DEV, rows, lanes), shard.dtype),
        in_specs=[ANY], out_specs=ANY,
        scratch_shapes=[pltpu.SemaphoreType.DMA((7,)), pltpu.SemaphoreType.DMA((7,)), pltpu.SemaphoreType.DMA(())],
    )(shard)


def _peer_copies(kind, src_ref, out_ref, send_sems, recv_sems, local_sem):
    x, y, c = lax.axis_index("x"), lax.axis_index("y"), lax.axis_index("c")
    me = 4 * x + 2 * y + c

    def src(idx):
        return src_ref.at[idx] if kind == "exchange" else src_ref

    mine = None if local_sem is None else pltpu.make_async_copy(src(me), out_ref.at[me], local_sem)
    copies = []
    for r in (2, 4, 6) if kind == "across" else range(1, N_DEV):
        px = 1 - x if r & 4 else x
        py = 1 - y if r & 2 else y
        pc = 1 - c if r & 1 else c
        copies.append(pltpu.make_async_remote_copy(
            src_ref=src(4 * px + 2 * py + pc), dst_ref=out_ref.at[me],
            send_sem=send_sems.at[r - 1], recv_sem=recv_sems.at[r - 1],
            device_id=(px, py, pc), device_id_type=MESH))
    return mine, copies


def _to_other_core(shard, land, *, name):
    def body(src_ref, land_ref, out_ref, send_sems, recv_sems):
        x, y, c = lax.axis_index("x"), lax.axis_index("y"), lax.axis_index("c")
        copies = []
        for k, r in enumerate((0, 2, 4, 6)):
            slot = 4 * (1 - x if r & 4 else x) + 2 * (1 - y if r & 2 else y) + c
            copies.append(pltpu.make_async_remote_copy(
                src_ref=src_ref if r == 0 else land_ref.at[slot], dst_ref=out_ref.at[slot],
                send_sem=send_sems.at[k], recv_sem=recv_sems.at[k], device_id=(x, y, 1 - c), device_id_type=MESH))
        for cp in copies:
            cp.start()
        for cp in copies:
            cp.wait()

    return pl.pallas_call(
        body, name=name, out_shape=jax.ShapeDtypeStruct(land.shape, land.dtype), in_specs=[ANY, ANY], out_specs=ANY,
        input_output_aliases={1: 0}, scratch_shapes=[pltpu.SemaphoreType.DMA((4,)), pltpu.SemaphoreType.DMA((4,))],
    )(shard, land)


PEER_SEMS = [pltpu.SemaphoreType.DMA((7,)), pltpu.SemaphoreType.DMA((7,)), pltpu.SemaphoreType.DMA(())]


def _all_gather_direct(shard, *, name):
    def body(x_ref, out_ref, send_sems, recv_sems, local_sem):
        mine, copies = _peer_copies("gather", x_ref, out_ref, send_sems, recv_sems, local_sem)
        mine.start()
        for cp in copies:
            cp.start()
        for cp in copies:
            cp.wait()
        mine.wait()

    return pl.pallas_call(
        body, name=name, out_shape=jax.ShapeDtypeStruct((N_DEV,) + shard.shape, shard.dtype),
        in_specs=[ANY], out_specs=ANY, scratch_shapes=list(PEER_SEMS),
    )(shard)


HBM = pl.BlockSpec(memory_space=pltpu.HBM)
SEMAPHORES = pl.BlockSpec(memory_space=pltpu.SEMAPHORE)


def _peer_start(kind, arr, *, name):
    land = lax.empty((N_DEV,) + arr.shape[-2:], arr.dtype)

    def body(src_ref, land_ref, send_sems, recv_sems, src_thru, land_thru, token):
        _, copies = _peer_copies(kind, src_ref, land_ref, send_sems, recv_sems, None)
        for cp in copies:
            cp.start()
        token[...] = jnp.zeros_like(token)

    return pl.pallas_call(
        body, name=name,
        out_shape=(pltpu.SemaphoreType.DMA((N_DEV - 1,)), pltpu.SemaphoreType.DMA((N_DEV - 1,)),
                   pltpu.HBM(arr.shape, arr.dtype), pltpu.HBM(land.shape, land.dtype), jax.ShapeDtypeStruct((8, LANES), F32)),
        in_specs=(HBM, HBM), out_specs=(SEMAPHORES, SEMAPHORES, HBM, HBM, pl.BlockSpec(memory_space=pltpu.VMEM)),
        input_output_aliases={0: 2, 1: 3},
        compiler_params=pltpu.CompilerParams(has_side_effects=pltpu.SideEffectType.DATAFLOW_SIDE_EFFECTING),
    )(pltpu.with_memory_space_constraint(arr, pltpu.HBM), pltpu.with_memory_space_constraint(land, pltpu.HBM))


def _peer_wait(kind, send_sems, recv_sems, src_thru, land_thru, after, *, name):
    def body(src_ref, land_ref, send_sems, recv_sems, *_):
        _, copies = _peer_copies(kind, src_ref, land_ref, send_sems, recv_sems, None)
        for cp in copies:
            cp.wait_send()
            cp.wait_recv()

    return pl.pallas_call(
        body, name=name,
        out_shape=(pltpu.HBM(src_thru.shape, src_thru.dtype), pltpu.HBM(land_thru.shape, land_thru.dtype)),
        in_specs=(HBM, HBM, SEMAPHORES, SEMAPHORES) + (ANY,) * len(after), out_specs=(HBM, HBM),
        input_output_aliases={0: 0, 1: 1},
        compiler_params=pltpu.CompilerParams(has_side_effects=pltpu.SideEffectType.DATAFLOW_SIDE_EFFECTING),
    )(src_thru, land_thru, send_sems, recv_sems, *after)


def _add_rider(rider, in_specs, args, out_specs, out_shape):
    if rider is None:
        return []
    _, arr = rider
    in_specs.append(ANY)
    args.append(arr)
    out_specs.append(ANY)
    out_shape.append(jax.ShapeDtypeStruct((N_DEV,) + arr.shape[-2:], arr.dtype))
    return list(PEER_SEMS)


def _split_rider(refs, rider, n_in, n_out):
    if rider is None:
        return refs, None
    refs = list(refs)
    rin = refs.pop(n_in)
    rout = refs.pop(n_in + n_out)
    return refs[:-3], (rin, rout, *refs[-3:])


def _ride_start(rider, ride_refs, first):
    if rider is None:
        return

    @pl.when(first)
    def _():
        mine, copies = _peer_copies(rider[0], *ride_refs)
        mine.start()
        for cp in copies:
            cp.start()


def _ride_wait(rider, ride_refs, last):
    if rider is None:
        return

    @pl.when(last)
    def _():
        mine, copies = _peer_copies(rider[0], *ride_refs)
        for cp in copies:
            cp.wait()
        mine.wait()


def _gathered_cols(blocks, kdim):
    n = blocks.shape[1] * WIDE // kdim
    return blocks.reshape(N_DEV, kdim, n).transpose(1, 0, 2).reshape(kdim, N_DEV * n)


def _scatter_cols(dw):
    kdim, n8 = dw.shape
    n = n8 // N_DEV
    return dw.reshape(kdim, N_DEV, n).transpose(1, 0, 2).reshape(N_DEV, kdim * n // WIDE, WIDE)


def _pad_rows(a, rows):
    pad = [(0, 0)] * a.ndim
    pad[-2] = (0, rows - a.shape[-2])
    return jnp.pad(a, pad)


def _layer0_in_weight_t(wt):
    cq, ckv, kpe = wt[0:256], wt[256:384], wt[384:416]
    q_s, k_s, v_s, gate = wt[416:928], wt[928:1056], wt[1056:1184], wt[1184:2208]
    z = jnp.zeros((64, wt.shape[1]), wt.dtype)
    return jnp.concatenate([gate, cq, ckv, z, kpe, z[:32], q_s, k_s, v_s], axis=0)


def _layer0_in_grad_t(dwt):
    gate, cq, ckv, kpe = dwt[0:1024], dwt[1024:1280], dwt[1280:1408], dwt[1472:1504]
    q_s, k_s, v_s = dwt[1536:2048], dwt[2048:2176], dwt[2176:2304]
    return jnp.concatenate([cq, ckv, kpe, q_s, k_s, v_s, gate], axis=0)


L0_BLOCKS = ((256, 1024), (128, 1280), (32, 1472), (512, 1536), (128, 2048), (128, 2176), (1024, 0))


def _layer0_in_unpack(gath, *, name):
    total = (Z0A_UNITS + Z0B_UNITS) * LANES

    def body(g_ref, w_ref):
        w_ref[1408:1472, :] = jnp.zeros((64, WIDE), w_ref.dtype)
        w_ref[1504:1536, :] = jnp.zeros((32, WIDE), w_ref.dtype)
        for p in range(N_DEV):
            lo, hi, at = p * N_E_IN, (p + 1) * N_E_IN, 0
            for rows, first in L0_BLOCKS:
                start, stop = max(lo, at), min(hi, at + rows)
                if start < stop:
                    w_ref[first + start - at:first + stop - at, :] = g_ref[p, start - lo:stop - lo, :]
                at += rows

    return pl.pallas_call(
        body, name=name, grid=(1,), in_specs=[_resident((N_DEV, RA0, WIDE), lambda i: (0, 0, 0))],
        out_specs=_resident((total, WIDE), lambda i: (0, 0)), out_shape=jax.ShapeDtypeStruct((total, WIDE), gath.dtype),
        compiler_params=_params(("arbitrary",)),
    )(gath)


def _early_grads_pack(d_w0t, d_q, d_kv, *, name):
    def body(w_ref, q_ref, kv_ref, out_ref):
        for p in range(N_DEV):
            lo, hi, at = p * N_E_IN, (p + 1) * N_E_IN, 0
            for rows, first in L0_BLOCKS:
                start, stop = max(lo, at), min(hi, at + rows)
                if start < stop:
                    out_ref[p, start - lo:stop - lo, :] = w_ref[first + start - at:first + stop - at, :]
                at += rows
            out_ref[p, N_E_IN:RA0, :] = jnp.zeros((RA0 - N_E_IN, WIDE), out_ref.dtype)
            out_ref[p, RA0:RA0 + 32, :] = q_ref[p]
            out_ref[p, RA0 + 32:, :] = kv_ref[p]

    arrays = (d_w0t, d_q, d_kv)
    return pl.pallas_call(
        body, name=name, grid=(1,), in_specs=[_resident(a.shape, lambda i, n=a.ndim: (0,) * n) for a in arrays],
        out_specs=_resident((N_DEV, RA0 + RB0, WIDE), lambda i: (0, 0, 0)),
        out_shape=jax.ShapeDtypeStruct((N_DEV, RA0 + RB0, WIDE), BF16), compiler_params=_params(("arbitrary",)),
    )(*arrays)


def _layer1_in_weight_t(wt):
    main = jnp.concatenate([wt[:3 * D_MODEL], wt[3 * D_MODEL + FOX_HEADS:]], axis=0)
    return main, _pad_rows(wt[3 * D_MODEL:3 * D_MODEL + FOX_HEADS], LANES)


def _layer1_in_unpack(gath, *, name):
    n_main = 3 * D_MODEL

    def body(g_ref, w_ref, f_ref, o1_ref, o0_ref):
        f_ref[...] = jnp.zeros_like(f_ref)
        for p in range(N_DEV):
            o1_ref[128 * p:128 * p + 128, :] = g_ref[p, RA1:RA1 + 128, :]
            o0_ref[128 * p:128 * p + 128, :] = g_ref[p, RA1 + 128:RA1 + 256, :]
            lo, hi = p * N_O_IN, (p + 1) * N_O_IN
            for ref, first, start, stop in ((w_ref, 0, lo, min(hi, n_main)),
                                            (f_ref, -n_main, max(lo, n_main), min(hi, n_main + FOX_HEADS)),
                                            (w_ref, -FOX_HEADS, max(lo, n_main + FOX_HEADS), hi)):
                if start < stop:
                    ref[start + first:stop + first, :] = g_ref[p, start - lo:stop - lo, :]

    return pl.pallas_call(
        body, name=name, grid=(1,), in_specs=[_resident((N_DEV, RA1 + 256, WIDE), lambda i: (0, 0, 0))],
        out_specs=[_resident((rows, WIDE), lambda i: (0, 0)) for rows in (n_main + D_MODEL, LANES, D_MODEL, D_MODEL)],
        out_shape=[jax.ShapeDtypeStruct((rows, WIDE), gath.dtype) for rows in (n_main + D_MODEL, LANES, D_MODEL, D_MODEL)],
        compiler_params=_params(("arbitrary",)),
    )(gath)


def _late_grads_pack(d_qkv, d_wft, d_gate, d_wo1, d_wo0, d_o_g, *, name):
    n_main = 3 * D_MODEL
    arrays = (d_qkv, d_wft, d_gate, d_wo1, d_wo0, d_o_g)

    def body(q_ref, f_ref, g_ref, o1_ref, o0_ref, og_ref, out_ref):
        for p in range(N_DEV):
            lo, hi = p * N_O_IN, (p + 1) * N_O_IN
            for ref, first, start, stop in ((q_ref, 0, lo, min(hi, n_main)),
                                            (f_ref, -n_main, max(lo, n_main), min(hi, n_main + FOX_HEADS)),
                                            (g_ref, -n_main - FOX_HEADS, max(lo, n_main + FOX_HEADS), hi)):
                if start < stop:
                    out_ref[p, start - lo:stop - lo, :] = ref[start + first:stop + first, :]
            out_ref[p, N_O_IN:RA1, :] = jnp.zeros((RA1 - N_O_IN, WIDE), out_ref.dtype)
            out_ref[p, RA1:RA1 + 128, :] = o1_ref[128 * p:128 * p + 128, :]
            out_ref[p, RA1 + 128:RA1 + 256, :] = o0_ref[128 * p:128 * p + 128, :]
            out_ref[p, RA1 + 256:, :] = og_ref[p]

    return pl.pallas_call(
        body, name=name, grid=(1,), in_specs=[_resident(a.shape, lambda i, n=a.ndim: (0,) * n) for a in arrays],
        out_specs=_resident((N_DEV, RA1 + RB1, WIDE), lambda i: (0, 0, 0)),
        out_shape=jax.ShapeDtypeStruct((N_DEV, RA1 + RB1, WIDE), BF16), compiler_params=_params(("arbitrary",)),
    )(*arrays)


def _q_up_weight(w):
    return jnp.pad(w.reshape(MLA_Q_RANK, MLA_HEADS, 96), ((0, 0), (0, 0), (0, 32))).reshape(MLA_Q_RANK, MLA_HEADS * LANES)


def _q_up_grad(dwp):
    return dwp.reshape(MLA_Q_RANK, MLA_HEADS, LANES)[:, :, :96].reshape(MLA_Q_RANK, MLA_HEADS * 96)


def _kv_up_weight(w):
    w4 = w.reshape(MLA_KV_RANK, MLA_HEADS, 2, 64)
    kp = jnp.pad(w4[:, :, 0, :], ((0, 0), (0, 0), (0, 64))).reshape(MLA_KV_RANK, MLA_HEADS * LANES)
    vp = w4[:, :, 1, :].reshape(MLA_KV_RANK, MLA_HEADS * 64)
    return jnp.concatenate([kp, vp], axis=1)


def _kv_up_grad(dwp):
    dk = dwp[:, :MLA_HEADS * LANES].reshape(MLA_KV_RANK, MLA_HEADS, LANES)[:, :, :64]
    dv = dwp[:, MLA_HEADS * LANES:].reshape(MLA_KV_RANK, MLA_HEADS, 64)
    return jnp.stack([dk, dv], axis=2).reshape(MLA_KV_RANK, MLA_HEADS * LANES)


def _pad_lanes(a):
    return jnp.pad(a, ((0, 0), (0, LANES - a.shape[1])))


def _small_pack(g_in, g_final, g_q_a, g_kv_a, sinks, b_f, loss):
    rows = [g_in.reshape(8, LANES), g_final.reshape(8, LANES), g_q_a.reshape(2, LANES), g_kv_a.reshape(1, LANES),
            _pad_lanes(sinks.reshape(1, -1)), _pad_lanes(b_f.reshape(1, -1)), _pad_lanes(loss.reshape(1, 1)),
            jnp.zeros((2, LANES), F32)]
    return jnp.concatenate(rows, axis=0)


def _small_unpack(a):
    return (a[0:8].reshape(1, D_MODEL), a[8:16].reshape(D_MODEL), a[16:18].reshape(1, MLA_Q_RANK),
            a[18:19].reshape(1, MLA_KV_RANK), a[19:20, :SWA_HEADS], a[20:21, :FOX_HEADS], a[21, 0])


def _local_step(x, positions, target, e_g_in, early, e_g_q_a, e_g_kv_a, e_sinks,
                late, o_b_f, g_final, scatter1=None, scatter0=None):
    s = x.shape[0]
    mla_scale = (MLA_NOPE + MLA_ROPE) ** -0.5
    fox_scale = FOX_DIM ** -0.5
    n0a = Z0A_UNITS * LANES

    inv_freq = 1.0 / (ROPE_THETA ** (jnp.arange(0, MLA_ROPE, 2, dtype=F32) / MLA_ROPE))
    ang = positions.astype(F32)[:, None] * inv_freq
    cos, sin = jnp.cos(ang), jnp.sin(ang)
    ones, zeros = jnp.ones((s, 64), F32), jnp.zeros((s, 64), F32)
    cos_t = jnp.concatenate([ones, cos, cos, ones[:, :32]], axis=1)
    sin_t = jnp.concatenate([zeros, -sin, sin, zeros[:, :32]], axis=1)
    cos_t, sin_t = lax.optimization_barrier((cos_t, sin_t))

    if len(early) == 3:
        h0 = _rmsnorm_fwd(x, e_g_in, width=D_MODEL, col_blk=0, name="l0_norm")
        w0t, wq, wkv = early
    else:
        pending, token, unpack, prep = early
        h0 = _rmsnorm_fwd(x, e_g_in, width=D_MODEL, col_blk=0, name="l0_norm", after=[token])
        sent, across = _peer_wait("across", *pending, after=[h0] + prep, name="weights0_wait")
        w0t, wq, wkv = unpack(sent, _to_other_core(sent, across, name="weights0_over"))
    z0a, z0b = _matmul_rows([(h0, w0t, True)], [], [], lambda r: (r[:, :n0a], r[:, n0a:]),
                            [("rows", n0a, F32), ("rows", Z0B_UNITS * LANES, BF16)], name="l0_in")
    cqn = _rmsnorm_fwd(z0a, e_g_q_a, width=MLA_Q_RANK, col_blk=4, name="l0_q_norm")
    ckvn = _rmsnorm_fwd(z0a, e_g_kv_a, width=MLA_KV_RANK, col_blk=10, name="l0_kv_norm")
    rope_rows = [(cos_t, LANES, 0), (sin_t, LANES, 0)]
    qm, = _matmul_rows([(cqn, wq, False)], rope_rows, [], _rope_q_epilogue, [("rows", MLA_HEADS * LANES, BF16)],
                       name="l0_q_up")
    kvm, = _matmul_rows([(ckvn, wkv, False)], [(z0a, LANES, 11)] + rope_rows, [], _rope_k_epilogue,
                        [("rows", MLA_HEADS * (LANES + MLA_V), BF16)], name="l0_kv_up")
    gathers = len(late) == 2
    res = _flash_fwd(qm, kvm, kvm, None, n_pairs=MLA_HEADS // 2, hw=LANES, q_off=0, k_off=0, v_off=MLA_HEADS,
                     scale=mla_scale, name="l0_mla_fwd", rider=("gather", late[0]) if gathers else None)
    o_mla, lse_mla = res[0], res[1]
    wo0, o_g_in, w1t, wft, wo1 = late[1](res[2]) if gathers else late
    o_swa, lse_swa = _swa_fwd(z0b, e_sinks, name="l0_swa_fwd")
    half = D_MODEL // 2

    x1, h1, og0 = _matmul_rows(
        [(None, wo0, False)], [(o_mla, half, 0), (o_swa, half, 0), (z0a, D_MODEL, 0), (x, D_MODEL, 0)], [o_g_in],
        lambda r, om, osw, gt, xt, g, made: (*_residual_norm_epilogue(r, xt, g), made),
        [("rows", D_MODEL, F32), ("rows", D_MODEL, BF16), ("rows", D_MODEL, BF16)], name="l0_out",
        prologue=lambda om, osw, gt, xt, g: _gated([om, osw], gt))
    z1, gate1, zf = _matmul_rows(
        [(None, w1t, True), (None, wft, True)], [(h1, D_MODEL, 0)], [],
        lambda r, h, made: (r[0][:, :3 * D_MODEL], r[0][:, 3 * D_MODEL:], r[1]),
        [("rows", 3 * D_MODEL, BF16), ("rows", D_MODEL, F32), ("rows", LANES, F32)], name="l1_in",
        prologue=lambda h: h, separate=True)
    bf = _pad_lanes(o_b_f)
    log_cum = _logf_fwd(zf, bf, name="l1_logf")
    bias2 = (-LOG2E * log_cum[:, :FOX_HEADS]).T
    t_bwd = min(ATT_T, s)
    bias = bias2.reshape(FOX_HEADS // 2, 2, s // t_bwd, 1, t_bwd)
    t_fwd = _fwd_tile(s)
    o_fox, lse_fox = _flash_fwd(z1, z1, z1, bias2.reshape(FOX_HEADS // 2, 2, s // t_fwd, 1, t_fwd),
                                n_pairs=FOX_HEADS // 2, hw=64, q_off=0, k_off=8, v_off=16, scale=fox_scale,
                                name="l1_fox_fwd")

    dx2, loss_part, d_g_final, og1, dx2_bf = _matmul_rows(
        [(None, wo1, False)], [(o_fox, D_MODEL, 0), (gate1, D_MODEL, 0), (x1, D_MODEL, 0), (target, D_MODEL, 0)],
        [g_final.reshape(1, D_MODEL)],
        lambda r, o, gt, xt, tg, g, made: _and_first(_loss_epilogue(r, xt, tg, g), made),
        [("rows", D_MODEL, F32), ("sum", (8, LANES)), ("sum", (1, D_MODEL)), ("rows", D_MODEL, BF16),
         ("rows", D_MODEL, BF16)], name="l1_out_loss", prologue=lambda o, gt, xt, tg, g: _gated([o], gt))

    d_wo1 = _matmul(og1, dx2_bf, ta=True, out_dtype=BF16, name="l1_out_dw")
    do_fox, d_gate1 = _matmul_rows([(dx2_bf, wo1, True)], [(o_fox, D_MODEL, 0), (gate1, D_MODEL, 0)], [],
                                   _gate_bwd_epilogue([D_MODEL]), [("rows", D_MODEL, F32), ("rows", D_MODEL, BF16)],
                                   name="l1_out_dx")
    dqkv1, dbias, drow = _flash_bwd(z1, z1, z1, do_fox, o_fox, lse_fox, bias, n_pairs=FOX_HEADS // 2, hw=64, q_off=0,
                                    k_off=8, v_off=16, scale=fox_scale, qk_dtype=BF16, stacked=True, name="l1_fox_bwd")
    d_log_cum = (drow.reshape(FOX_HEADS, s) - dbias.reshape(FOX_HEADS, s)).T
    d_log_cum = jnp.pad(d_log_cum, ((0, 0), (0, LANES - FOX_HEADS)))
    d_zf, d_bf = _logf_bwd(d_log_cum, zf, bf, name="l1_logf_bwd")
    d_w1t = (_matmul(dqkv1, h1, ta=True, out_dtype=BF16, name="l1_in_dw_qkv"),
             _matmul(d_gate1, h1, ta=True, out_dtype=BF16, name="l1_in_dw_gate"))
    d_wft = _matmul(d_zf, h1, ta=True, out_dtype=BF16, name="l1_in_f_dw")
    dx1, d_o_g_in, dx1_bf = _matmul_rows([(dqkv1, w1t, False, c * D_MODEL, c) for c in range(3)]
                                         + [(d_gate1, w1t, False, 3 * D_MODEL), (d_zf, wft, False)],
                                         [(x1, D_MODEL, 0), (dx2, D_MODEL, 0)], [o_g_in],
                                         lambda *a: _and_first(_rms_bwd_epilogue(*a)),
                                         [("rows", D_MODEL, F32), ("sum", (1, D_MODEL)), ("rows", D_MODEL, BF16)],
                                         name="l1_in_dx")

    d_wo0 = _matmul(og0, dx1_bf, ta=True, out_dtype=BF16, name="l0_out_dw")
    do_mla, do_swa, d_gate0 = _matmul_rows(
        [(dx1_bf, wo0, True)], [(o_mla, half, 0), (o_swa, half, 0), (z0a, D_MODEL, 0)], [], _gate_bwd_epilogue([half, half]),
        [("rows", half, F32), ("rows", half, F32), ("rows", D_MODEL, BF16)], name="l0_out_dx")
    dq_s, dkt_s, dvt_s, d_sinks = _swa_bwd(z0b, e_sinks, do_swa, o_swa, lse_swa, name="l0_swa_bwd")
    dk_s = dkt_s.transpose(0, 2, 1).reshape(s, LANES)
    dv_s = dvt_s.transpose(0, 2, 1).reshape(s, LANES)
    rider = None
    if scatter1 is not None:
        rider = ("exchange", scatter1(dict(w1t=d_w1t, wft=d_wft, wo1=d_wo1, o_g_in=d_o_g_in, wo0=d_wo0)))
    res = _flash_bwd(qm, kvm, kvm, do_mla, o_mla, lse_mla, None, n_pairs=MLA_HEADS // 2, hw=LANES, q_off=0, k_off=0,
                     v_off=MLA_HEADS, scale=mla_scale, qk_dtype=F32, name="l0_mla_bwd", rider=rider)
    dqm, dkm, dvm = res[0], res[1], res[2]
    recv1 = res[3] if rider is not None else None
    d_qp, d_kvp, d_kpe = _rope_bwd(dqm, dkm, dvm, cos_t, sin_t, name="l0_rope_bwd")
    d_wq = _matmul(cqn, d_qp, ta=True, out_dtype=BF16, name="l0_q_up_dw")
    d_cqn = _matmul(d_qp, wq, tb=True, name="l0_q_up_dx")
    d_wkv = _matmul(ckvn, d_kvp, ta=True, out_dtype=BF16, name="l0_kv_up_dw")
    d_ckvn = _matmul(d_kvp, wkv, tb=True, name="l0_kv_up_dx")
    d_cq, d_g_q_a = _rmsnorm_bwd(z0a, e_g_q_a, d_cqn, width=MLA_Q_RANK, col_blk=4, name="l0_q_norm_bwd")
    d_ckv, d_g_kv_a = _rmsnorm_bwd(z0a, e_g_kv_a, d_ckvn, width=MLA_KV_RANK, col_blk=10, name="l0_kv_norm_bwd")
    dz0 = jnp.concatenate([d_gate0, d_cq, d_ckv, d_kpe, dq_s.astype(BF16), dk_s.astype(BF16), dv_s.astype(BF16)], axis=1)
    d_w0t = _matmul(dz0, h0, ta=True, out_dtype=BF16, name="l0_in_dw")
    pending0, after_start = None, []
    if scatter0 is not None:
        *pending0, token = _peer_start("exchange", scatter0(dict(w0t=d_w0t, wq=d_wq, wkv=d_wkv)), name="grads0_start")
        after_start = [token]
    grad_x, d_e_g_in = _matmul_rows(
        [(dz0, w0t, False)], [(x, D_MODEL, 0), (dx1, D_MODEL, 0)], [e_g_in] + after_start,
        lambda dy, xt, add, g, *_: _rms_bwd_epilogue(dy, xt, add, g),
        [("rows", D_MODEL, F32), ("sum", (1, D_MODEL))], name="l0_in_dx")

    return dict(pending0=pending0, recv1=recv1, loss=loss_part[0, 0], grad_x=grad_x, e_g_in=d_e_g_in, w0t=d_w0t, e_g_q_a=d_g_q_a, wq=d_wq,
                e_g_kv_a=d_g_kv_a, wkv=d_wkv, e_sinks=d_sinks[:, 0].reshape(1, SWA_HEADS), wo0=d_wo0,
                o_g_in=d_o_g_in, w1t=d_w1t, wft=d_wft, o_b_f=d_bf[:, :FOX_HEADS], wo1=d_wo1, g_final=d_g_final.reshape(D_MODEL))


def _wide(a, rows):
    flat = a.reshape(-1)
    return jnp.pad(flat, (0, rows * WIDE - flat.shape[0])).reshape(rows, WIDE)


def _rows_b0(w_q, w_kv):
    return jnp.concatenate([_wide(w_q, 32), _wide(w_kv, 16)], axis=0)


def _unflat_b0(f):
    return f[0:24].reshape(1, MLA_Q_RANK, 96), f[32:48].reshape(1, MLA_KV_RANK, 128)


def _rows_b1(o_w_out, e_w_out, g_in):
    return jnp.concatenate([o_w_out, e_w_out, _wide(g_in, 16)], axis=0)


def _unflat_b1(f):
    return f[0:128][None], f[128:256][None], f[256:257, :LANES]


def kernel(x, positions, e_g_in, e_w_in, e_g_q_a, e_w_q_up, e_g_kv_a, e_w_kv_up, e_sinks, e_w_out, o_g_in, o_w_in, o_b_f, o_w_out, g_final, loss_target, m_e_g_in, m_e_w_in, m_e_g_q_a, m_e_w_q_up, m_e_g_kv_a, m_e_w_kv_up, m_e_sinks, m_e_w_out, m_o_g_in, m_o_w_in, m_o_b_f, m_o_w_out, m_g_final, v_e_g_in, v_e_w_in, v_e_g_q_a, v_e_w_q_up, v_e_g_kv_a, v_e_w_kv_up, v_e_sinks, v_e_w_out, v_o_g_in, v_o_w_in, v_o_b_f, v_o_w_out, v_g_final):
    def bf(a):
        return a.astype(BF16)

    me = 4 * lax.axis_index("x") + 2 * lax.axis_index("y") + lax.axis_index("c")
    shard0 = jnp.concatenate([_pad_rows(bf(e_w_in[0]).T, RA0), _rows_b0(bf(e_w_q_up[0]), bf(e_w_kv_up[0]))], axis=0)
    *pending_w0, token_w0 = _peer_start("across", shard0, name="weights0_start")

    def unpack0(sent, gath0):
        gath0 = lax.dynamic_update_slice_in_dim(gath0, sent[None], me, axis=0)
        w0t = _layer0_in_unpack(gath0, name="weights0_unpack")
        wq = _q_up_weight(_gathered_cols(gath0[:, RA0:RA0 + 24], MLA_Q_RANK))
        wkv = _kv_up_weight(_gathered_cols(gath0[:, RA0 + 32:RA0 + 48], MLA_KV_RANK))
        return w0t, wq, wkv

    rows_b0 = [_rows_b0(q[0], kv[0]) for q, kv in ((e_w_q_up, e_w_kv_up), (m_e_w_q_up, m_e_w_kv_up), (v_e_w_q_up, v_e_w_kv_up))]
    rows_b1 = [_rows_b1(o[0], e[0], g) for o, e, g in ((o_w_out, e_w_out, o_g_in), (m_o_w_out, m_e_w_out, m_o_g_in),
                                                       (v_o_w_out, v_e_w_out, v_o_g_in))]

    g_bits = lax.bitcast_convert_type(o_g_in.reshape(LANES), BF16)
    shard1 = jnp.concatenate([_pad_rows(bf(o_w_in[0]).T, RA1), _rows_b1(bf(o_w_out[0]), bf(e_w_out[0]), g_bits)], axis=0)

    def unpack1(gath1):
        w1t, wft, wo1, wo0 = _layer1_in_unpack(gath1, name="weights1_unpack")
        bits = gath1[:, RA1 + 256, :2 * LANES].reshape(N_DEV, LANES, 2)
        return wo0, lax.bitcast_convert_type(bits, F32).reshape(1, D_MODEL), w1t, wft, wo1

    def scatter1(g):
        d_o_g = jnp.pad(bf(g["o_g_in"]).reshape(N_DEV, 1, LANES), ((0, 0), (0, 15), (0, WIDE - LANES)))
        return _late_grads_pack(g["w1t"][0], g["wft"], g["w1t"][1], g["wo1"], g["wo0"], d_o_g, name="grads1_pack")

    def scatter0(g):
        return _early_grads_pack(g["w0t"], _pad_rows(_scatter_cols(_q_up_grad(g["wq"])), 32),
                                 _scatter_cols(_kv_up_grad(g["wkv"])), name="grads0_pack")

    gr = _local_step(x[0], positions[0], loss_target[0], e_g_in,
                     (pending_w0, token_w0, unpack0, [shard1] + rows_b0 + rows_b1), e_g_q_a, e_g_kv_a, e_sinks,
                     (shard1, unpack1), o_b_f, g_final, scatter1=scatter1, scatter0=scatter0)

    def in_projection(recv, ra, n, w, m, v, name):
        g = _sum8(recv, ra, name=name + "_grad_sum")[:n].reshape(n, 1, D_MODEL)
        w, m, v = [jnp.transpose(a, (2, 0, 1)) for a in (w, m, v)]
        return (g, *_adamw_columns(g, w, m, v, name=name + "_adamw"))

    o_in = in_projection(gr["recv1"], RA1, N_O_IN, o_w_in, m_o_w_in, v_o_w_in, "o_w_in")
    b1 = _adamw(gr["recv1"][:, RA1:], *rows_b1, name="adamw_late")

    small = _small_pack(gr["e_g_in"], gr["g_final"], gr["e_g_q_a"], gr["e_g_kv_a"], gr["e_sinks"], gr["o_b_f"], gr["loss"])
    small_all = _all_gather_direct(small, name="small_all_gather")
    zero = jnp.zeros((), F32)
    w_small = _small_pack(e_g_in, g_final, e_g_q_a, e_g_kv_a, e_sinks, o_b_f, zero)
    m_small = _small_pack(m_e_g_in, m_g_final, m_e_g_q_a, m_e_g_kv_a, m_e_sinks, m_o_b_f, zero)
    v_small = _small_pack(v_e_g_in, v_g_final, v_e_g_q_a, v_e_g_kv_a, v_e_sinks, v_o_b_f, zero)
    smalls = _adamw(small_all, w_small, m_small, v_small, name="adamw_replicated")
    g_sm, d_sm, m_sm, v_sm = [_small_unpack(a) for a in smalls]
    loss = g_sm[6]

    sent0, recv0 = _peer_wait("exchange", *gr["pending0"], after=[o_in[1], b1[1], smalls[1]], name="grads0_wait")
    own = lax.dynamic_slice_in_dim(sent0, me, 1, axis=0)
    recv0 = lax.dynamic_update_slice_in_dim(recv0, own, me, axis=0)
    e_in = in_projection(recv0, RA0, N_E_IN, e_w_in, m_e_w_in, v_e_w_in, "e_w_in")
    b0 = _adamw(recv0[:, RA0:], *rows_b0, name="adamw_early")

    def sharded(k):
        q_up, kv_up = _unflat_b0(b0[k])
        o_out, e_out, o_g = _unflat_b1(b1[k])
        return jnp.transpose(e_in[k], (1, 2, 0)), q_up, kv_up, e_out, jnp.transpose(o_in[k], (1, 2, 0)), o_out, o_g

    g_sh, d_sh, m_sh, v_sh = [sharded(k) for k in range(4)]

    def leaves(sh, sm):
        return (sm[0], sh[0], sm[2], sh[1], sm[3], sh[2], sm[4], sh[3], sh[6], sh[4], sm[5], sh[5], sm[1])

    return (loss, gr["grad_x"][None], *leaves(g_sh, g_sm), *leaves(d_sh, d_sm), *leaves(m_sh, m_sm), *leaves(v_sh, v_sm))
```

```python
import functools

import jax
import jax.numpy as jnp
from jax import lax
from jax.experimental import pallas as pl
from jax.experimental.pallas import tpu as pltpu

F32 = jnp.float32
BF16 = jnp.bfloat16
NEG_INF = float("-inf")

N_DEV = 8
LANES = 128
D_MODEL = 1024
EPS = 1e-6
ROPE_THETA = 10000.0
MLA_HEADS = 8
MLA_Q_RANK = 256
MLA_KV_RANK = 128
MLA_NOPE = 64
MLA_ROPE = 32
MLA_V = 64
SWA_HEADS = 8
SWA_KV_HEADS = 2
SWA_DIM = 64
WINDOW = 128
FOX_HEADS = 16
FOX_DIM = 64

ADAM_LR = 0.001
ADAM_B1 = 0.9
ADAM_B2 = 0.999
ADAM_EPS = 1e-08
ADAM_WD = 0.01
ADAM_STEP = 10

ATT_T = 512
ATT_T_FWD = 1024
VMEM_LIMIT = 56 * 1024 * 1024
MATMUL_B_BLOCK_BYTES = 8 * 1024 * 1024

Z0A_UNITS = 12
Z0B_UNITS = 6

WIDE = 1024
N_E_IN = 276
N_O_IN = 514
RA0 = 288
RB0 = 32 + 16
RA1 = 528
RB1 = 128 + 128 + 16
SMALL_ROWS = 24


def _tile(n, cands):
    for c in cands:
        if n % c == 0:
            return c
    raise ValueError(f"no tile for {n}")


ROW_TILES = (512, 256, 128)


def _params(sem, vmem=VMEM_LIMIT):
    return pltpu.CompilerParams(dimension_semantics=sem, vmem_limit_bytes=vmem)


def _matmul(a, b, *, name, ta=False, tb=False, out_dtype=F32):
    if ta:
        kdim, m = a.shape[-2], a.shape[-1] * (a.shape[0] if a.ndim == 3 else 1)
    else:
        m, kdim = a.shape
    if tb:
        n, kb = b.shape
    else:
        kb, n = b.shape
    assert kdim == kb, (a.shape, b.shape)
    tm = _tile(m, (512, 256, 128))
    tn = _tile(n, [c for c in (1024, 768, 512, 384, 256, 128) if c * kdim * b.dtype.itemsize <= MATMUL_B_BLOCK_BYTES])
    dims = (((0 if ta else 1,), (1 if tb else 0,)), ((), ()))

    def body(a_ref, b_ref, o_ref):
        r = lax.dot_general(a_ref[...].astype(BF16), b_ref[...].astype(BF16), dims, preferred_element_type=F32)
        o_ref[...] = r.astype(out_dtype)

    if a.ndim == 3:
        per = a.shape[2] // tm
        a_spec = pl.BlockSpec((None, kdim, tm), lambda i, j: (i // per, 0, i % per))
    else:
        a_spec = pl.BlockSpec((kdim, tm), lambda i, j: (0, i)) if ta else pl.BlockSpec((tm, kdim), lambda i, j: (i, 0))
    b_spec = pl.BlockSpec((tn, kdim), lambda i, j: (j, 0)) if tb else pl.BlockSpec((kdim, tn), lambda i, j: (0, j))
    return pl.pallas_call(
        body, name=name, grid=(m // tm, n // tn), in_specs=[a_spec, b_spec],
        out_specs=pl.BlockSpec((tm, tn), lambda i, j: (i, j)), out_shape=jax.ShapeDtypeStruct((m, n), out_dtype),
        compiler_params=_params(("parallel", "parallel")),
    )(a, b)


def _rmsnorm_fwd(x, g, *, width, col_blk, name, after=()):
    s = x.shape[0]
    tm = _tile(s, ROW_TILES)

    def body(x_ref, g_ref, *rest):
        y_ref = rest[-1]
        xf = x_ref[...].astype(F32)
        r = lax.rsqrt(jnp.mean(xf * xf, axis=-1, keepdims=True) + EPS)
        y_ref[...] = ((xf * r) * g_ref[...]).astype(BF16)

    return pl.pallas_call(
        body, name=name, grid=(s // tm,),
        in_specs=[pl.BlockSpec((tm, width), lambda i: (i, col_blk)), pl.BlockSpec((1, width), lambda i: (0, 0))]
        + [ANY] * len(after),
        out_specs=pl.BlockSpec((tm, width), lambda i: (i, 0)),
        out_shape=jax.ShapeDtypeStruct((s, width), BF16),
        compiler_params=_params(("parallel",)),
    )(x, g, *after)


def _rmsnorm_bwd(x, g, dy, *, width, col_blk, name):
    s = x.shape[0]
    tm = _tile(s, ROW_TILES)

    def body(x_ref, g_ref, dy_ref, dx_ref, dg_ref):
        @pl.when(pl.program_id(0) == 0)
        def _():
            dg_ref[...] = jnp.zeros_like(dg_ref)

        dx, dg = _rms_bwd_epilogue(dy_ref[...], x_ref[...], 0.0, g_ref[...])
        dg_ref[...] += dg
        dx_ref[...] = dx.astype(BF16)

    return pl.pallas_call(
        body, name=name, grid=(s // tm,),
        in_specs=[pl.BlockSpec((tm, width), lambda i: (i, col_blk)), pl.BlockSpec((1, width), lambda i: (0, 0)),
                  pl.BlockSpec((tm, width), lambda i: (i, 0))],
        out_specs=[pl.BlockSpec((tm, width), lambda i: (i, 0)), pl.BlockSpec((1, width), lambda i: (0, 0))],
        out_shape=[jax.ShapeDtypeStruct((s, width), BF16), jax.ShapeDtypeStruct((1, width), F32)],
        compiler_params=_params(("arbitrary",)),
    )(x, g, dy)


def _sigmoid(x):
    return 1.0 / (1.0 + jnp.exp(-x))


def _matmul_rows(terms, row_inputs, params, epilogue, outs, *, name, prologue=None, separate=False):
    s = row_inputs[0][0].shape[0] if row_inputs else terms[0][0].shape[-2]
    tm = _tile(s, ROW_TILES)
    steps = s // tm
    n_r, n_p, n_o = len(row_inputs), len(params), len(outs)
    n_t = sum(1 if term[0] is None else 2 for term in terms)

    def body(*refs):
        t_refs, r_refs = list(refs[:n_t]), refs[n_t:n_t + n_r]
        p_refs, o_refs = refs[n_t + n_r:n_t + n_r + n_p], refs[n_t + n_r + n_p:]
        i = pl.program_id(0)
        rows, small = [r[...] for r in r_refs], [p[...] for p in p_refs]
        made = None if prologue is None else prologue(*rows, *small)
        parts = []
        for term in terms:
            a = made if term[0] is None else t_refs.pop(0)[...].astype(BF16)
            dims = (((1,), (1 if term[2] else 0,)), ((), ()))
            parts.append(lax.dot_general(a, t_refs.pop(0)[...].astype(BF16), dims, preferred_element_type=F32))
        acc = parts if separate else sum(parts[1:], parts[0])
        vals = epilogue(acc, *rows, *small) if prologue is None else epilogue(acc, *rows, *small, made)
        for ref, val, out in zip(o_refs, vals, outs):
            if out[0] == "rows":
                ref[...] = val.astype(ref.dtype)
            else:
                @pl.when(i == 0)
                def _(ref=ref):
                    ref[...] = jnp.zeros_like(ref)

                ref[...] += val

    in_specs, args = [], []
    for term in terms:
        a, b = term[0], term[1]
        if a is None:
            in_specs.append(_resident(b.shape, lambda i: (0, 0)))
            args.append(b)
            continue
        b_rows = b.shape[0] if term[2] or len(term) < 4 else a.shape[-1]
        b_blk = 0 if len(term) < 4 else term[3] // b_rows
        if len(term) == 5:
            a_spec = pl.BlockSpec((None, tm, a.shape[2]), lambda i, c=term[4]: (c, i, 0))
        else:
            a_spec = pl.BlockSpec((tm, a.shape[1]), lambda i: (i, 0))
        in_specs += [a_spec, _resident((b_rows, b.shape[1]), lambda i, b_blk=b_blk: (b_blk, 0))]
        args += [a, b]
    for arr, width, col_blk in row_inputs:
        in_specs.append(pl.BlockSpec((tm, width), lambda i, col_blk=col_blk: (i, col_blk)))
        args.append(arr)
    for p in params:
        in_specs.append(pl.BlockSpec(p.shape, lambda i: (0, 0)))
        args.append(p)
    out_specs, out_shape = [], []
    for out in outs:
        if out[0] == "rows":
            out_specs.append(pl.BlockSpec((tm, out[1]), lambda i: (i, 0)))
            out_shape.append(jax.ShapeDtypeStruct((s, out[1]), out[2]))
        else:
            out_specs.append(pl.BlockSpec(out[1], lambda i: (0, 0)))
            out_shape.append(jax.ShapeDtypeStruct(out[1], F32))
    return pl.pallas_call(
        body, name=name, grid=(steps,), in_specs=in_specs, out_specs=out_specs, out_shape=out_shape,
        compiler_params=_params(("arbitrary",)),
    )(*args)


def _rms_stats(x):
    r = lax.rsqrt(jnp.mean(x * x, axis=-1, keepdims=True) + EPS)
    return r, x * r


def _gated(o_parts, gate):
    o = o_parts[0] if len(o_parts) == 1 else jnp.concatenate(o_parts, axis=1)
    return (o * (gate * _sigmoid(gate))).astype(BF16)


def _and_first(vals, *more):
    return (*vals, *more, vals[0])


def _residual_norm_epilogue(r, x, g):
    x1 = x + r
    _, xh = _rms_stats(x1)
    return x1, xh * g


def _rms_bwd_epilogue(dy, x, add, g):
    r, xh = _rms_stats(x)
    dxh = dy * g
    dx = r * (dxh - xh * jnp.mean(dxh * xh, axis=-1, keepdims=True)) + add
    return dx, jnp.sum(dy * xh, axis=0, keepdims=True)


def _loss_epilogue(r, x1, target, g):
    rs, xh = _rms_stats(x1 + r)
    err = xh * g - target
    loss = jnp.broadcast_to(0.5 * jnp.sum(jnp.mean(err * err, axis=-1, keepdims=True)), (8, LANES))
    dy = err * (1.0 / D_MODEL)
    dxh = dy * g
    dx = rs * (dxh - xh * jnp.mean(dxh * xh, axis=-1, keepdims=True))
    return dx, loss, jnp.sum(dy * xh, axis=0, keepdims=True)


def _gate_bwd_epilogue(widths):
    def epilogue(d, *rows):
        o_parts, gt = rows[:-1], rows[-1]
        o = o_parts[0] if len(o_parts) == 1 else jnp.concatenate(o_parts, axis=1)
        sg = _sigmoid(gt)
        do = d * (gt * sg)
        d_gate = d * o * (sg * (1.0 + gt * (1.0 - sg)))
        cuts = [sum(widths[:k]) for k in range(len(widths) + 1)]
        return tuple(do[:, cuts[k]:cuts[k + 1]] for k in range(len(widths))) + (d_gate,)

    return epilogue


def _rot_half(x):
    lane = lax.broadcasted_iota(jnp.int32, x.shape, 1)
    return jnp.where(lane < 80, pltpu.roll(x, LANES - 16, axis=1), pltpu.roll(x, 16, axis=1))


def _rot_half_t(g):
    lane = lax.broadcasted_iota(jnp.int32, g.shape, 1)
    lo = (lane >= MLA_NOPE) & (lane < MLA_NOPE + MLA_ROPE // 2)
    hi = (lane >= MLA_NOPE + MLA_ROPE // 2) & (lane < MLA_NOPE + MLA_ROPE)
    return jnp.where(lo, pltpu.roll(g, LANES - 16, axis=1), jnp.where(hi, pltpu.roll(g, 16, axis=1), 0.0))


def _rope_q_epilogue(q, c, sn):
    heads = [q[:, h * LANES:(h + 1) * LANES] for h in range(MLA_HEADS)]
    return (jnp.concatenate([qh * c + _rot_half(qh) * sn for qh in heads], axis=1),)


def _rope_k_epilogue(kv, kpe, c, sn):
    kpe_r = kpe * c + _rot_half(kpe) * sn
    lane = lax.broadcasted_iota(jnp.int32, kpe.shape, 1)
    heads = [jnp.where(lane < MLA_NOPE, kv[:, h * LANES:(h + 1) * LANES], kpe_r) for h in range(MLA_HEADS)]
    return (jnp.concatenate(heads + [kv[:, MLA_HEADS * LANES:]], axis=1),)


def _rope_bwd(dqm, dkm, dvm, cos_t, sin_t, *, name):
    s = dqm.shape[0]
    tm = _tile(s, ROW_TILES)
    hw = MLA_HEADS * LANES
    vw = MLA_HEADS * MLA_V

    def body(dq_ref, dk_ref, dv_ref, c_ref, s_ref, dqp_ref, dkv_ref, dkpe_ref):
        c = c_ref[...]
        sn = s_ref[...]
        ksum = jnp.zeros((tm, LANES), F32)
        for h in range(MLA_HEADS):
            sl = slice(h * LANES, (h + 1) * LANES)
            dq = dq_ref[:, sl]
            dqp_ref[:, sl] = (dq * c + _rot_half_t(dq * sn)).astype(BF16)
            dk = dk_ref[:, sl]
            dkv_ref[:, sl] = dk.astype(BF16)
            ksum = ksum + dk
        dkv_ref[:, hw:] = dv_ref[...]
        lane = lax.broadcasted_iota(jnp.int32, ksum.shape, 1)
        dkpe = ksum * c + _rot_half_t(ksum * sn)
        dkpe_ref[...] = jnp.where((lane >= MLA_NOPE) & (lane < MLA_NOPE + MLA_ROPE), dkpe, 0.0).astype(BF16)

    return pl.pallas_call(
        body, name=name, grid=(s // tm,),
        in_specs=[pl.BlockSpec((tm, hw), lambda i: (i, 0)), pl.BlockSpec((tm, hw), lambda i: (i, 0)),
                  pl.BlockSpec((tm, vw), lambda i: (i, 0)),
                  pl.BlockSpec((tm, LANES), lambda i: (i, 0)), pl.BlockSpec((tm, LANES), lambda i: (i, 0))],
        out_specs=[pl.BlockSpec((tm, hw), lambda i: (i, 0)), pl.BlockSpec((tm, hw + vw), lambda i: (i, 0)),
                   pl.BlockSpec((tm, LANES), lambda i: (i, 0))],
        out_shape=[jax.ShapeDtypeStruct((s, hw), BF16), jax.ShapeDtypeStruct((s, hw + vw), BF16),
                   jax.ShapeDtypeStruct((s, LANES), BF16)],
        compiler_params=_params(("parallel",)),
    )(dqm, dkm, dvm, cos_t, sin_t)


def _head_mask(shape, a):
    lane = lax.broadcasted_iota(jnp.int32, shape, 1)
    return (lane >= 64 * a) & (lane < 64 * (a + 1))


_NT = (((1,), (1,)), ((), ()))
LOG2E = 1.4426950408889634


def _stack_heads(tile, hw):
    lane = lax.broadcasted_iota(jnp.int32, tile.shape, 1)
    z = jnp.zeros_like(tile)
    return jnp.concatenate([jnp.where(lane < hw, tile, z), jnp.where(lane >= hw, tile, z)], axis=0)


def _stacked_rows(r0, r1, t):
    n = r0.shape[-1]
    return jnp.concatenate([jnp.broadcast_to(r0, (t, n)), jnp.broadcast_to(r1, (t, n))], axis=0)


def _resident(block, index_map):
    return pl.BlockSpec(block, index_map, pipeline_mode=pl.Buffered(1))


def _fwd_tile(s):
    return ATT_T_FWD if s % ATT_T_FWD == 0 else min(ATT_T, s)


def _flash_fwd(q, k, v, bias, *, n_pairs, hw, q_off, k_off, v_off, scale, name, rider=None):
    s = q.shape[0]
    t = _fwd_tile(s)
    nb = s // t
    qw = 2 * hw
    has_bias = bias is not None
    c1 = scale * LOG2E

    def body(*refs):
        refs, ride_refs = _split_rider(refs, rider, n_in=4 if has_bias else 3, n_out=2)
        if has_bias:
            q_ref, k_ref, v_ref, b_ref, o_ref, lse_ref, vt_ref, bcol_ref = refs
        else:
            q_ref, k_ref, v_ref, o_ref, lse_ref, vt_ref = refs
            b_ref = bcol_ref = None
        _ride_start(rider, ride_refs, pl.program_id(0) == 0)
        row = lax.broadcasted_iota(jnp.int32, (t, t), 0)
        col = lax.broadcasted_iota(jnp.int32, (t, t), 1)
        cmask_t = jnp.concatenate([row <= col, row <= col], axis=1)
        lane_lt64 = lax.broadcasted_iota(jnp.int32, (t, LANES), 1) < 64

        def as_column(r):
            return jnp.broadcast_to(r, (8, r.shape[1])).T[:, 0:1]

        def v_block(j, _):
            c0 = pl.multiple_of(j * t, t)
            vt_ref[j] = v_ref[pl.ds(c0, t), :].astype(F32).T.astype(BF16)
            if has_bias:
                for a in range(2):
                    bcol_ref[a, pl.ds(c0, t), :] = as_column(b_ref[0, a, j])
            return 0

        lax.fori_loop(0, nb, v_block, 0)

        def stacked_queries(i):
            return _stack_heads(q_ref[pl.ds(pl.multiple_of(i * t, t), t), :], hw).astype(F32).T.astype(BF16)

        def kv_step(j, carry, qs_t, masked):
            m, l, acc = carry
            rows = pl.ds(pl.multiple_of(j * t, t), t)
            sc = jnp.dot(k_ref[rows, :], qs_t, preferred_element_type=F32) * c1
            if has_bias:
                sc = sc + jnp.concatenate([jnp.broadcast_to(bcol_ref[0, rows, :], (t, t)),
                                           jnp.broadcast_to(bcol_ref[1, rows, :], (t, t))], axis=1)
            if masked:
                sc = jnp.where(cmask_t, sc, NEG_INF)
            m_new = jnp.maximum(m, jnp.max(sc, axis=0, keepdims=True))
            alpha = jnp.exp2(m - m_new)
            p = jnp.exp2(sc - m_new)
            l_new = alpha * l + jnp.sum(p, axis=0, keepdims=True)
            pv = jnp.dot(vt_ref[j], p.astype(BF16), preferred_element_type=F32)
            return m_new, l_new, alpha * acc + pv

        def finish(i, carry):
            m, l, acc = carry
            r0 = pl.multiple_of(i * t, t)
            out = (acc / l).T
            lse2 = as_column(m + jnp.log2(l))
            lse_ref[0, 0, pl.ds(r0, t), :] = lse2[:t]
            lse_ref[0, 1, pl.ds(r0, t), :] = lse2[t:]
            o_ref[pl.ds(r0, t), :] = jnp.where(lane_lt64, out[:t], out[t:])

        init = (jnp.full((1, 2 * t), NEG_INF, F32), jnp.zeros((1, 2 * t), F32), jnp.zeros((LANES, 2 * t), F32))

        def q_block(i, _):
            qs_t = stacked_queries(i)
            carry = lax.fori_loop(0, i, lambda j, c: kv_step(j, c, qs_t, False), init)
            finish(i, kv_step(i, carry, qs_t, True))
            return 0

        lax.fori_loop(0, nb, q_block, 0)
        _ride_wait(rider, ride_refs, pl.program_id(0) == n_pairs - 1)

    in_specs = [_resident((s, qw), lambda p: (0, q_off + p)), _resident((s, qw), lambda p: (0, k_off + p)),
                _resident((s, LANES), lambda p: (0, v_off + p))]
    args = [q, k, v]
    if has_bias:
        in_specs.append(_resident((1, 2, nb, 1, t), lambda p: (p, 0, 0, 0, 0)))
        args.append(bias)
    out_specs = [pl.BlockSpec((s, LANES), lambda p: (0, p)), pl.BlockSpec((1, 2, s, 1), lambda p: (p, 0, 0, 0))]
    out_shape = [jax.ShapeDtypeStruct((s, n_pairs * LANES), F32), jax.ShapeDtypeStruct((n_pairs, 2, s, 1), F32)]
    scratch = [pltpu.VMEM((nb, LANES, t), BF16)] + ([pltpu.VMEM((2, s, 1), F32)] if has_bias else [])
    scratch += _add_rider(rider, in_specs, args, out_specs, out_shape)
    return pl.pallas_call(
        body, name=name, grid=(n_pairs,), in_specs=in_specs, out_specs=out_specs, out_shape=out_shape,
        scratch_shapes=scratch,
        compiler_params=_params(("parallel",) if rider is None else ("arbitrary",)),
    )(*args)


def _flash_bwd(q, k, v, do, o, lse, bias, *, n_pairs, hw, q_off, k_off, v_off, scale, qk_dtype, name, rider=None,
               stacked=False):
    s = q.shape[0]
    t = min(ATT_T, s)
    nb = s // t
    qw = 2 * hw
    has_bias = bias is not None
    c1 = scale * LOG2E

    def body(*refs):
        n_grads = 1 if stacked else 3
        refs, ride_refs = _split_rider(refs, rider, n_in=7 if has_bias else 6, n_out=n_grads + (2 if has_bias else 0))
        if stacked:
            refs = list(refs)
            n_in = 7 if has_bias else 6
            refs[n_in:n_in + 1] = [refs[n_in].at[0], refs[n_in].at[1], refs[n_in].at[2]]
        if has_bias:
            (q_ref, k_ref, v_ref, do_ref, o_ref, lse_ref, b_ref, dq_ref, dk_ref, dv_ref, db_ref, dr_ref,
             dkt_ref, dvt_ref) = refs
            db_ref[...] = jnp.zeros_like(db_ref)
        else:
            q_ref, k_ref, v_ref, do_ref, o_ref, lse_ref, dq_ref, dk_ref, dv_ref, dkt_ref, dvt_ref = refs
            b_ref = db_ref = dr_ref = None
        _ride_start(rider, ride_refs, pl.program_id(0) == 0)
        dkt_ref[...] = jnp.zeros_like(dkt_ref)
        dvt_ref[...] = jnp.zeros_like(dvt_ref)
        causal = lax.broadcasted_iota(jnp.int32, (t, t), 1) <= lax.broadcasted_iota(jnp.int32, (t, t), 0)
        cmask = jnp.concatenate([causal, causal], axis=0)
        lane_lt_hw = lax.broadcasted_iota(jnp.int32, (t, qw), 1) < hw

        def q_block(i, _):
            r0 = pl.multiple_of(i * t, t)
            qs = _stack_heads(q_ref[pl.ds(r0, t), :], hw)
            dos = _stack_heads(do_ref[pl.ds(r0, t), :], 64)
            ot = o_ref[pl.ds(r0, t), :]
            delta = jnp.sum(dos * jnp.concatenate([ot, ot], axis=0), axis=-1, keepdims=True)
            lse2 = jnp.concatenate([lse_ref[0, 0, pl.ds(r0, t), :], lse_ref[0, 1, pl.ds(r0, t), :]], axis=0)
            dosb = dos.astype(BF16)
            dos_t = dos.T.astype(BF16)
            qs_t = qs.astype(F32).T.astype(BF16)

            def kv_step(j, carry, masked):
                dq, rsum = carry
                c0 = pl.multiple_of(j * t, t)
                kt = k_ref[pl.ds(c0, t), :]
                vt = v_ref[pl.ds(c0, t), :]
                sc = lax.dot_general(qs, kt, _NT, preferred_element_type=F32) * c1
                if has_bias:
                    sc = sc + _stacked_rows(b_ref[0, 0, j], b_ref[0, 1, j], t)
                if masked:
                    sc = jnp.where(cmask, sc, NEG_INF)
                p = jnp.exp2(sc - lse2)
                dp = lax.dot_general(dosb, vt, _NT, preferred_element_type=F32)
                ds = p * (dp - delta)
                dsb = ds.astype(BF16)
                pb = p.astype(BF16)
                if hw == LANES:
                    dvt_ref[j] += jnp.concatenate(
                        [jnp.dot(dos_t[:64, :t], pb[:t], preferred_element_type=F32),
                         jnp.dot(dos_t[64:, t:], pb[t:], preferred_element_type=F32)], axis=0)
                    dkt_ref[j] += jnp.concatenate(
                        [jnp.dot(qs_t[:hw, :t], dsb[:t], preferred_element_type=F32),
                         jnp.dot(qs_t[hw:, t:], dsb[t:], preferred_element_type=F32)], axis=0)
                else:
                    dvt_ref[j] += jnp.dot(dos_t, pb, preferred_element_type=F32)
                    dkt_ref[j] += jnp.dot(qs_t, dsb, preferred_element_type=F32)
                if has_bias:
                    db_ref[0, 0, j] += jnp.sum(ds[:t], axis=0, keepdims=True)
                    db_ref[0, 1, j] += jnp.sum(ds[t:], axis=0, keepdims=True)
                    rsum = rsum + jnp.sum(ds, axis=-1, keepdims=True)
                return dq + jnp.dot(dsb, kt, preferred_element_type=F32), rsum

            init = (jnp.zeros((2 * t, qw), F32), jnp.zeros((2 * t, 1), F32))
            carry = lax.fori_loop(0, i, functools.partial(kv_step, masked=False), init)
            dq, rsum = kv_step(i, carry, True)
            dq = dq * scale
            dq_ref[pl.ds(r0, t), :] = jnp.where(lane_lt_hw, dq[:t], dq[t:]).astype(qk_dtype)
            if has_bias:
                rsum_row = jnp.broadcast_to(rsum, (2 * t, LANES)).T[0:1]
                dr_ref[0, 0, i] = rsum_row[:, :t]
                dr_ref[0, 1, i] = rsum_row[:, t:]
            return 0

        lax.fori_loop(0, nb, q_block, 0)

        def k_block(j, _):
            c0 = pl.multiple_of(j * t, t)
            dk_ref[pl.ds(c0, t), :] = (dkt_ref[j].T * scale).astype(qk_dtype)
            dv_ref[pl.ds(c0, t), :] = dvt_ref[j].T.astype(BF16)
            return 0

        lax.fori_loop(0, nb, k_block, 0)
        _ride_wait(rider, ride_refs, pl.program_id(0) == n_pairs - 1)

    in_specs = [_resident((s, qw), lambda p: (0, q_off + p)), _resident((s, qw), lambda p: (0, k_off + p)),
                _resident((s, LANES), lambda p: (0, v_off + p)),
                _resident((s, LANES), lambda p: (0, p)), _resident((s, LANES), lambda p: (0, p)),
                _resident((1, 2, s, 1), lambda p: (p, 0, 0, 0))]
    args = [q, k, v, do, o, lse]
    if stacked:
        assert qw == LANES and qk_dtype == BF16
        out_specs = [pl.BlockSpec((3, s, LANES), lambda p: (0, 0, p))]
        out_shape = [jax.ShapeDtypeStruct((3, s, n_pairs * LANES), BF16)]
    else:
        out_specs = [pl.BlockSpec((s, qw), lambda p: (0, p)), pl.BlockSpec((s, qw), lambda p: (0, p)),
                     pl.BlockSpec((s, LANES), lambda p: (0, p))]
        out_shape = [jax.ShapeDtypeStruct((s, n_pairs * qw), qk_dtype), jax.ShapeDtypeStruct((s, n_pairs * qw), qk_dtype),
                     jax.ShapeDtypeStruct((s, n_pairs * LANES), BF16)]
    if has_bias:
        in_specs.append(_resident((1, 2, nb, 1, t), lambda p: (p, 0, 0, 0, 0)))
        args.append(bias)
        for _ in range(2):
            out_specs.append(pl.BlockSpec((1, 2, nb, 1, t), lambda p: (p, 0, 0, 0, 0)))
            out_shape.append(jax.ShapeDtypeStruct((n_pairs, 2, nb, 1, t), F32))
    scratch = [pltpu.VMEM((nb, qw, t), F32), pltpu.VMEM((nb, LANES, t), F32)]
    scratch += _add_rider(rider, in_specs, args, out_specs, out_shape)
    return pl.pallas_call(
        body, name=name, grid=(n_pairs,), in_specs=in_specs, out_specs=out_specs, out_shape=out_shape,
        scratch_shapes=scratch,
        compiler_params=_params(("parallel",) if rider is None else ("arbitrary",)),
    )(*args)


def _alibi_slope(h):
    return 2.0 ** (-8.0 * (h + 1.0) / SWA_HEADS)


SWA_ROWS = 512
SWA_SCALE = SWA_DIM ** -0.5


def _swa_geometry(i):
    w = WINDOW
    r0 = pl.multiple_of(i * w, w)
    b0 = pl.multiple_of(jnp.maximum(i - 1, 0) * w, w)
    row = lax.broadcasted_iota(jnp.int32, (w, 2 * w), 0)
    col = lax.broadcasted_iota(jnp.int32, (w, 2 * w), 1)
    dist = row - col + (r0 - b0)
    valid = (dist >= 0) & (dist < w)
    return r0, b0, dist.astype(F32), valid


def _swa_q_head(qblk, h):
    kv = h // (SWA_HEADS // SWA_KV_HEADS)
    if h % 2 != kv:
        qblk = pltpu.roll(qblk, 64, axis=1)
    return jnp.where(_head_mask(qblk.shape, kv), qblk, 0.0)


SWA_GROUP = SWA_HEADS // SWA_KV_HEADS


def _swa_stack(ref, rs, grp):
    parts = []
    for a in range(SWA_GROUP):
        h = SWA_GROUP * grp + a
        parts.append(_swa_q_head(ref[rs, (h // 2) * LANES:(h // 2 + 1) * LANES].astype(F32), h))
    return jnp.concatenate(parts, axis=0)


def _swa_unstack(x, grp):
    tiles = []
    for a in range(SWA_GROUP):
        h = SWA_GROUP * grp + a
        tile = x[a * WINDOW:(a + 1) * WINDOW]
        tiles.append(pltpu.roll(tile, 64, axis=1) if h % 2 != grp else tile)
    return tiles


def _swa_head_column(vals):
    return jnp.concatenate([jnp.full((WINDOW, 1), v, F32) for v in vals], axis=0)


def _swa_logits(qs, kb, dist, valid, grp):
    slopes = _swa_head_column([_alibi_slope(SWA_GROUP * grp + a) for a in range(SWA_GROUP)])
    dist4 = jnp.concatenate([dist] * SWA_GROUP, axis=0)
    valid4 = jnp.concatenate([valid] * SWA_GROUP, axis=0)
    sc = lax.dot_general(qs, kb, _NT, preferred_element_type=F32) * SWA_SCALE - slopes * dist4
    return jnp.where(valid4, sc, NEG_INF)


def _swa_merge_heads(tiles):
    lt64 = lax.broadcasted_iota(jnp.int32, (WINDOW, LANES), 1) < 64
    return jnp.concatenate([jnp.where(lt64, tiles[2 * b], tiles[2 * b + 1]) for b in range(SWA_HEADS // 2)], axis=1)


def _swa_fwd(z0b, sinks, *, name):
    s = z0b.shape[0]
    w = WINDOW
    rows = min(SWA_ROWS, s)
    per_step = rows // w
    qcols = SWA_HEADS * SWA_DIM

    def body(sink_ref, q_ref, k_ref, v_ref, o_ref, lse_ref):
        g = pl.program_id(0)
        for ii in range(per_step):
            rs = slice(ii * w, (ii + 1) * w)
            r0, b0, dist, valid = _swa_geometry(g * per_step + ii)
            kb = k_ref[pl.ds(b0, 2 * w), :]
            vb = v_ref[pl.ds(b0, 2 * w), :]
            o_tiles = []
            for h in range(SWA_HEADS):
                kv = h // SWA_GROUP
                qh = _swa_q_head(q_ref[rs, (h // 2) * LANES:(h // 2 + 1) * LANES].astype(F32), h).astype(BF16)
                sc = lax.dot_general(qh, kb, _NT, preferred_element_type=F32) * SWA_SCALE - _alibi_slope(h) * dist
                sc = jnp.where(valid, sc, NEG_INF)
                sink = sink_ref[0, h]
                m = jnp.maximum(jnp.max(sc, axis=-1, keepdims=True), sink)
                p = jnp.exp(sc - m)
                l = jnp.sum(p, axis=-1, keepdims=True) + jnp.exp(sink - m)
                oh = jnp.dot(p.astype(BF16), vb, preferred_element_type=F32) / l
                o_tiles.append(pltpu.roll(oh, 64, axis=1) if h % 2 != kv else oh)
                lse_ref[h, rs, :] = m + jnp.log(l)
            o_ref[rs, :] = _swa_merge_heads(o_tiles)

    return pl.pallas_call(
        body, name=name, grid=(s // rows,),
        in_specs=[pl.BlockSpec(memory_space=pltpu.SMEM),
                  pl.BlockSpec((rows, qcols), lambda g: (g, 0)),
                  pl.BlockSpec((s, LANES), lambda g: (0, 4)), pl.BlockSpec((s, LANES), lambda g: (0, 5))],
        out_specs=[pl.BlockSpec((rows, qcols), lambda g: (g, 0)), pl.BlockSpec((SWA_HEADS, rows, 1), lambda g: (0, g, 0))],
        out_shape=[jax.ShapeDtypeStruct((s, qcols), F32), jax.ShapeDtypeStruct((SWA_HEADS, s, 1), F32)],
        compiler_params=_params(("parallel",)),
    )(sinks, z0b, z0b, z0b)


def _swa_bwd(z0b, sinks, do, o, lse, *, name):
    s = z0b.shape[0]
    w = WINDOW
    rows = min(SWA_ROWS, s)
    per_step = rows // w
    qcols = SWA_HEADS * SWA_DIM
    nblk = s // w

    def body(sink_ref, q_ref, k_ref, v_ref, do_ref, o_ref, lse_ref, dq_ref, dkt_ref, dvt_ref, dsink_ref):
        g = pl.program_id(0)

        @pl.when(g == 0)
        def _():
            dkt_ref[...] = jnp.zeros_like(dkt_ref)
            dvt_ref[...] = jnp.zeros_like(dvt_ref)
            dsink_ref[...] = jnp.zeros_like(dsink_ref)

        for ii in range(per_step):
            i = g * per_step + ii
            rs = slice(ii * w, (ii + 1) * w)
            r0, b0, dist, valid = _swa_geometry(i)
            j0 = jnp.maximum(i - 1, 0)
            kb = k_ref[pl.ds(b0, 2 * w), :]
            vb = v_ref[pl.ds(b0, 2 * w), :]
            dq_tiles = []
            for grp in range(SWA_KV_HEADS):
                heads = [SWA_GROUP * grp + a for a in range(SWA_GROUP)]
                qs32 = _swa_stack(q_ref, rs, grp)
                dos32 = _swa_stack(do_ref, rs, grp)
                delta = jnp.sum(dos32 * _swa_stack(o_ref, rs, grp), axis=-1, keepdims=True)
                lse = jnp.concatenate([lse_ref[h, rs, :] for h in heads], axis=0)
                sink = _swa_head_column([sink_ref[0, h] for h in heads])
                p = jnp.exp(_swa_logits(qs32.astype(BF16), kb, dist, valid, grp) - lse)
                dp = lax.dot_general(dos32.astype(BF16), vb, _NT, preferred_element_type=F32)
                ds = p * (dp - delta)
                dsb = ds.astype(BF16)
                d_sink = jnp.exp(sink - lse) * delta
                for a, h in enumerate(heads):
                    dsink_ref[h:h + 1, :] += jnp.broadcast_to(-jnp.sum(d_sink[a * w:(a + 1) * w]), (1, LANES))
                dvt = jnp.dot(dos32.T.astype(BF16), p.astype(BF16), preferred_element_type=F32)
                dkt = jnp.dot(qs32.T.astype(BF16), dsb, preferred_element_type=F32) * SWA_SCALE
                dvt_ref[j0] += dvt[:, :w]
                dvt_ref[j0 + 1] += dvt[:, w:]
                dkt_ref[j0] += dkt[:, :w]
                dkt_ref[j0 + 1] += dkt[:, w:]
                dq_tiles += _swa_unstack(jnp.dot(dsb, kb, preferred_element_type=F32) * SWA_SCALE, grp)
            dq_ref[rs, :] = _swa_merge_heads(dq_tiles)

    return pl.pallas_call(
        body, name=name, grid=(s // rows,),
        in_specs=[pl.BlockSpec(memory_space=pltpu.SMEM),
                  pl.BlockSpec((rows, qcols), lambda g: (g, 0)),
                  pl.BlockSpec((s, LANES), lambda g: (0, 4)), pl.BlockSpec((s, LANES), lambda g: (0, 5)),
                  pl.BlockSpec((rows, qcols), lambda g: (g, 0)), pl.BlockSpec((rows, qcols), lambda g: (g, 0)),
                  pl.BlockSpec((SWA_HEADS, rows, 1), lambda g: (0, g, 0))],
        out_specs=[pl.BlockSpec((rows, qcols), lambda g: (g, 0)),
                   pl.BlockSpec((nblk, LANES, w), lambda g: (0, 0, 0)),
                   pl.BlockSpec((nblk, LANES, w), lambda g: (0, 0, 0)),
                   pl.BlockSpec((SWA_HEADS, LANES), lambda g: (0, 0))],
        out_shape=[jax.ShapeDtypeStruct((s, qcols), F32),
                   jax.ShapeDtypeStruct((nblk, LANES, w), F32), jax.ShapeDtypeStruct((nblk, LANES, w), F32),
                   jax.ShapeDtypeStruct((SWA_HEADS, LANES), F32)],
        compiler_params=_params(("arbitrary",)),
    )(sinks, z0b, z0b, z0b, do, o, lse)


CUM_T = 256


def _split3(x):
    hi = x.astype(BF16)
    r1 = x - hi.astype(F32)
    mid = r1.astype(BF16)
    lo = (r1 - mid.astype(F32)).astype(BF16)
    return hi, mid, lo


def _tri_dot(tri, x):
    hi, mid, lo = _split3(x)
    out = jnp.dot(tri, hi, preferred_element_type=F32)
    out = out + jnp.dot(tri, mid, preferred_element_type=F32)
    return out + jnp.dot(tri, lo, preferred_element_type=F32)


def _logf_fwd(zf, bf, *, name):
    s = zf.shape[0]
    t = CUM_T
    nb = s // t

    def body(z_ref, b_ref, c_ref, carry_ref):
        i = pl.program_id(0)

        @pl.when(i == 0)
        def _():
            carry_ref[...] = jnp.zeros_like(carry_ref)

        x = z_ref[...] + b_ref[...]
        lf = jnp.minimum(x, 0.0) - jnp.log(1.0 + jnp.exp(-jnp.abs(x)))
        row = lax.broadcasted_iota(jnp.int32, (t, t), 0)
        col = lax.broadcasted_iota(jnp.int32, (t, t), 1)
        tri = jnp.where(col <= row, 1.0, 0.0).astype(BF16)
        c = _tri_dot(tri, lf) + carry_ref[...]
        c_ref[...] = c
        carry_ref[...] = c[t - 1:t, :]

    return pl.pallas_call(
        body, name=name, grid=(nb,),
        in_specs=[pl.BlockSpec((t, LANES), lambda i: (i, 0)), pl.BlockSpec((1, LANES), lambda i: (0, 0))],
        out_specs=pl.BlockSpec((t, LANES), lambda i: (i, 0)),
        out_shape=jax.ShapeDtypeStruct((s, LANES), F32),
        scratch_shapes=[pltpu.VMEM((1, LANES), F32)],
        compiler_params=_params(("arbitrary",)),
    )(zf, bf)


def _logf_bwd(dc, zf, bf, *, name):
    s = zf.shape[0]
    t = CUM_T
    nb = s // t

    def body(dc_ref, z_ref, b_ref, dz_ref, db_ref, carry_ref):
        i = pl.program_id(0)

        @pl.when(i == 0)
        def _():
            carry_ref[...] = jnp.zeros_like(carry_ref)
            db_ref[...] = jnp.zeros_like(db_ref)

        row = lax.broadcasted_iota(jnp.int32, (t, t), 0)
        col = lax.broadcasted_iota(jnp.int32, (t, t), 1)
        tri = jnp.where(col >= row, 1.0, 0.0).astype(BF16)
        dlf = _tri_dot(tri, dc_ref[...]) + carry_ref[...]
        carry_ref[...] = dlf[0:1, :]
        x = z_ref[...] + b_ref[...]
        dz = dlf * _sigmoid(-x)
        dz_ref[...] = dz.astype(BF16)
        db_ref[...] += jnp.sum(dz, axis=0, keepdims=True)

    return pl.pallas_call(
        body, name=name, grid=(nb,),
        in_specs=[pl.BlockSpec((t, LANES), lambda i: (nb - 1 - i, 0)), pl.BlockSpec((t, LANES), lambda i: (nb - 1 - i, 0)),
                  pl.BlockSpec((1, LANES), lambda i: (0, 0))],
        out_specs=[pl.BlockSpec((t, LANES), lambda i: (nb - 1 - i, 0)), pl.BlockSpec((1, LANES), lambda i: (0, 0))],
        out_shape=[jax.ShapeDtypeStruct((s, LANES), BF16), jax.ShapeDtypeStruct((1, LANES), F32)],
        scratch_shapes=[pltpu.VMEM((1, LANES), F32)],
        compiler_params=_params(("arbitrary",)),
    )(dc, zf, bf)


def _sum_pieces(p_ref):
    g = p_ref[0].astype(F32)
    for k in range(1, N_DEV):
        g = g + p_ref[k].astype(F32)
    return g


def _adam_update(g, w, m, v):
    bc1 = 1.0 - ADAM_B1 ** ADAM_STEP
    bc2 = 1.0 - ADAM_B2 ** ADAM_STEP
    nm = ADAM_B1 * m + (1.0 - ADAM_B1) * g
    nv = ADAM_B2 * v + (1.0 - ADAM_B2) * (g * g)
    m_hat = nm / bc1
    v_hat = nv / bc2
    return -ADAM_LR * (m_hat / (jnp.sqrt(v_hat) + ADAM_EPS) + ADAM_WD * w), nm, nv


def _adamw(pieces, w, m, v, *, name):
    rows, cols = w.shape
    tr = _tile(rows, (RB1, RB0, SMALL_ROWS))

    def body(p_ref, w_ref, m_ref, v_ref, g_ref, d_ref, nm_ref, nv_ref):
        g = _sum_pieces(p_ref)
        g_ref[...] = g
        d_ref[...], nm_ref[...], nv_ref[...] = _adam_update(g, w_ref[...], m_ref[...], v_ref[...])

    spec = pl.BlockSpec((tr, cols), lambda i: (i, 0))
    shape = jax.ShapeDtypeStruct((rows, cols), F32)
    return pl.pallas_call(
        body, name=name, grid=(rows // tr,),
        in_specs=[pl.BlockSpec((N_DEV, tr, cols), lambda i: (0, i, 0)), spec, spec, spec],
        out_specs=[spec, spec, spec, spec], out_shape=[shape, shape, shape, shape],
        compiler_params=_params(("parallel",)),
    )(pieces, w, m, v)


def _sum8(pieces, rows, *, name):
    cols = pieces.shape[2]
    tr = _tile(rows, (176, 96))

    def body(p_ref, g_ref):
        g_ref[...] = _sum_pieces(p_ref)

    return pl.pallas_call(
        body, name=name, grid=(rows // tr,),
        in_specs=[pl.BlockSpec((N_DEV, tr, cols), lambda i: (0, i, 0))],
        out_specs=pl.BlockSpec((tr, cols), lambda i: (i, 0)),
        out_shape=jax.ShapeDtypeStruct((rows, cols), F32),
        compiler_params=_params(("parallel",)),
    )(pieces)


def _adamw_columns(g, w, m, v, *, name):
    n, _, k = w.shape
    tr = n // 2

    def body(g_ref, w_ref, m_ref, v_ref, d_ref, nm_ref, nv_ref):
        d_ref[...], nm_ref[...], nv_ref[...] = _adam_update(g_ref[...], w_ref[...], m_ref[...], v_ref[...])

    spec = pl.BlockSpec((tr, 1, k), lambda i: (i, 0, 0))
    shape = jax.ShapeDtypeStruct((n, 1, k), F32)
    return pl.pallas_call(
        body, name=name, grid=(n // tr,), in_specs=[spec, spec, spec, spec],
        out_specs=[spec, spec, spec], out_shape=[shape, shape, shape],
        compiler_params=_params(("parallel",)),
    )(g, w, m, v)


MESH = pl.DeviceIdType.MESH
ANY = pl.BlockSpec(memory_space=pl.ANY)


def _all_gather(shard, *, name):
    rows, lanes = shard.shape

    def body(x_ref, out_ref, send_sems, recv_sems, local_sem):
        x, y, c = lax.axis_index("x"), lax.axis_index("y"), lax.axis_index("c")
        me, sibling = (x, y, c), (x, y, 1 - c)
        chips = [(1 - x, y), (x, 1 - y), (1 - x, 1 - y)]

        def block(px, py, pc):
            return out_ref.at[4 * px + 2 * py + pc]

        def copy(k, blk, to, src=None):
            return pltpu.make_async_remote_copy(
                src_ref=block(*blk) if src is None else src, dst_ref=block(*blk),
                send_sem=send_sems.at[k], recv_sem=recv_sems.at[k], device_id=to, device_id_type=MESH)

        mine = pltpu.make_async_copy(x_ref, block(*me), local_sem)
        mine.start()
        first = [copy(0, me, sibling, src=x_ref)]
        first += [copy(1 + j, me, (*chip, c), src=x_ref) for j, chip in enumerate(chips)]
        for cp in first:
            cp.start()
        passed = [copy(4 + j, (*chip, c), sibling) for j, chip in enumerate(chips)]
        for j, chip in enumerate(chips):
            copy(1 + j, (*chip, c), me).wait_recv()
            passed[j].start()
        copy(0, sibling, me).wait_recv()
        for j, chip in enumerate(chips):
            copy(4 + j, (*chip, 1 - c), me).wait_recv()
        for cp in first + passed:
            cp.wait_send()
        mine.wait()

    return pl.pallas_call(
        body, name=name, out_shape=jax.ShapeDtypeStruct((N_DEV, rows, lanes), shard.dtype),
        in_specs=[ANY], out_specs=ANY,
        scratch_shapes=[pltpu.SemaphoreType.DMA((7,)), pltpu.SemaphoreType.DMA((7,)), pltpu.SemaphoreType.DMA(())],
    )(shard)


def _peer_copies(kind, src_ref, out_ref, send_sems, recv_sems, local_sem):
    x, y, c = lax.axis_index("x"), lax.axis_index("y"), lax.axis_index("c")
    me = 4 * x + 2 * y + c

    def src(idx):
        return src_ref.at[idx] if kind == "exchange" else src_ref

    mine = None if local_sem is None else pltpu.make_async_copy(src(me), out_ref.at[me], local_sem)
    copies = []
    for r in (2, 4, 6) if kind == "across" else range(1, N_DEV):
        px = 1 - x if r & 4 else x
        py = 1 - y if r & 2 else y
        pc = 1 - c if r & 1 else c
        copies.append(pltpu.make_async_remote_copy(
            src_ref=src(4 * px + 2 * py + pc), dst_ref=out_ref.at[me],
            send_sem=send_sems.at[r - 1], recv_sem=recv_sems.at[r - 1],
            device_id=(px, py, pc), device_id_type=MESH))
    return mine, copies


def _to_other_core(shard, land, *, name):
    def body(src_ref, land_ref, out_ref, send_sems, recv_sems):
        x, y, c = lax.axis_index("x"), lax.axis_index("y"), lax.axis_index("c")
        copies = []
        for k, r in enumerate((0, 2, 4, 6)):
            slot = 4 * (1 - x if r & 4 else x) + 2 * (1 - y if r & 2 else y) + c
            copies.append(pltpu.make_async_remote_copy(
                src_ref=src_ref if r == 0 else land_ref.at[slot], dst_ref=out_ref.at[slot],
                send_sem=send_sems.at[k], recv_sem=recv_sems.at[k], device_id=(x, y, 1 - c), device_id_type=MESH))
        for cp in copies:
            cp.start()
        for cp in copies:
            cp.wait()

    return pl.pallas_call(
        body, name=name, out_shape=jax.ShapeDtypeStruct(land.shape, land.dtype), in_specs=[ANY, ANY], out_specs=ANY,
        input_output_aliases={1: 0}, scratch_shapes=[pltpu.SemaphoreType.DMA((4,)), pltpu.SemaphoreType.DMA((4,))],
    )(shard, land)


PEER_SEMS = [pltpu.SemaphoreType.DMA((7,)), pltpu.SemaphoreType.DMA((7,)), pltpu.SemaphoreType.DMA(())]


HBM = pl.BlockSpec(memory_space=pltpu.HBM)
SEMAPHORES = pl.BlockSpec(memory_space=pltpu.SEMAPHORE)


def _peer_start(kind, arr, *, name):
    land = lax.empty((N_DEV,) + arr.shape[-2:], arr.dtype)

    def body(src_ref, land_ref, send_sems, recv_sems, src_thru, land_thru, token):
        _, copies = _peer_copies(kind, src_ref, land_ref, send_sems, recv_sems, None)
        for cp in copies:
            cp.start()
        token[...] = jnp.zeros_like(token)

    return pl.pallas_call(
        body, name=name,
        out_shape=(pltpu.SemaphoreType.DMA((N_DEV - 1,)), pltpu.SemaphoreType.DMA((N_DEV - 1,)),
                   pltpu.HBM(arr.shape, arr.dtype), pltpu.HBM(land.shape, land.dtype), jax.ShapeDtypeStruct((8, LANES), F32)),
        in_specs=(HBM, HBM), out_specs=(SEMAPHORES, SEMAPHORES, HBM, HBM, pl.BlockSpec(memory_space=pltpu.VMEM)),
        input_output_aliases={0: 2, 1: 3},
        compiler_params=pltpu.CompilerParams(has_side_effects=pltpu.SideEffectType.DATAFLOW_SIDE_EFFECTING),
    )(pltpu.with_memory_space_constraint(arr, pltpu.HBM), pltpu.with_memory_space_constraint(land, pltpu.HBM))


def _peer_wait(kind, send_sems, recv_sems, src_thru, land_thru, after, *, name):
    def body(src_ref, land_ref, send_sems, recv_sems, *_):
        _, copies = _peer_copies(kind, src_ref, land_ref, send_sems, recv_sems, None)
        for cp in copies:
            cp.wait_send()
            cp.wait_recv()

    return pl.pallas_call(
        body, name=name,
        out_shape=(pltpu.HBM(src_thru.shape, src_thru.dtype), pltpu.HBM(land_thru.shape, land_thru.dtype)),
        in_specs=(HBM, HBM, SEMAPHORES, SEMAPHORES) + (ANY,) * len(after), out_specs=(HBM, HBM),
        input_output_aliases={0: 0, 1: 1},
        compiler_params=pltpu.CompilerParams(has_side_effects=pltpu.SideEffectType.DATAFLOW_SIDE_EFFECTING),
    )(src_thru, land_thru, send_sems, recv_sems, *after)


def _add_rider(rider, in_specs, args, out_specs, out_shape):
    if rider is None:
        return []
    _, arr = rider
    in_specs.append(ANY)
    args.append(arr)
    out_specs.append(ANY)
    out_shape.append(jax.ShapeDtypeStruct((N_DEV,) + arr.shape[-2:], arr.dtype))
    return list(PEER_SEMS)


def _split_rider(refs, rider, n_in, n_out):
    if rider is None:
        return refs, None
    refs = list(refs)
    rin = refs.pop(n_in)
    rout = refs.pop(n_in + n_out)
    return refs[:-3], (rin, rout, *refs[-3:])


def _ride_start(rider, ride_refs, first):
    if rider is None:
        return

    @pl.when(first)
    def _():
        mine, copies = _peer_copies(rider[0], *ride_refs)
        mine.start()
        for cp in copies:
            cp.start()


def _ride_wait(rider, ride_refs, last):
    if rider is None:
        return

    @pl.when(last)
    def _():
        mine, copies = _peer_copies(rider[0], *ride_refs)
        for cp in copies:
            cp.wait()
        mine.wait()


def _gathered_cols(blocks, kdim):
    n = blocks.shape[1] * WIDE // kdim
    return blocks.reshape(N_DEV, kdim, n).transpose(1, 0, 2).reshape(kdim, N_DEV * n)


def _scatter_cols(dw):
    kdim, n8 = dw.shape
    n = n8 // N_DEV
    return dw.reshape(kdim, N_DEV, n).transpose(1, 0, 2).reshape(N_DEV, kdim * n // WIDE, WIDE)


def _pad_rows(a, rows):
    pad = [(0, 0)] * a.ndim
    pad[-2] = (0, rows - a.shape[-2])
    return jnp.pad(a, pad)


def _layer0_in_weight_t(wt):
    cq, ckv, kpe = wt[0:256], wt[256:384], wt[384:416]
    q_s, k_s, v_s, gate = wt[416:928], wt[928:1056], wt[1056:1184], wt[1184:2208]
    z = jnp.zeros((64, wt.shape[1]), wt.dtype)
    return jnp.concatenate([gate, cq, ckv, z, kpe, z[:32], q_s, k_s, v_s], axis=0)


def _layer0_in_grad_t(dwt):
    gate, cq, ckv, kpe = dwt[0:1024], dwt[1024:1280], dwt[1280:1408], dwt[1472:1504]
    q_s, k_s, v_s = dwt[1536:2048], dwt[2048:2176], dwt[2176:2304]
    return jnp.concatenate([cq, ckv, kpe, q_s, k_s, v_s, gate], axis=0)


L0_BLOCKS = ((256, 1024), (128, 1280), (32, 1472), (512, 1536), (128, 2048), (128, 2176), (1024, 0))


def _layer0_in_unpack(gath, *, name):
    total = (Z0A_UNITS + Z0B_UNITS) * LANES

    def body(g_ref, w_ref):
        w_ref[1408:1472, :] = jnp.zeros((64, WIDE), w_ref.dtype)
        w_ref[1504:1536, :] = jnp.zeros((32, WIDE), w_ref.dtype)
        for p in range(N_DEV):
            lo, hi, at = p * N_E_IN, (p + 1) * N_E_IN, 0
            for rows, first in L0_BLOCKS:
                start, stop = max(lo, at), min(hi, at + rows)
                if start < stop:
                    w_ref[first + start - at:first + stop - at, :] = g_ref[p, start - lo:stop - lo, :]
                at += rows

    return pl.pallas_call(
        body, name=name, grid=(1,), in_specs=[_resident((N_DEV, RA0, WIDE), lambda i: (0, 0, 0))],
        out_specs=_resident((total, WIDE), lambda i: (0, 0)), out_shape=jax.ShapeDtypeStruct((total, WIDE), gath.dtype),
        compiler_params=_params(("arbitrary",)),
    )(gath)


def _early_grads_pack(d_w0t, d_q, d_kv, *, name):
    def body(w_ref, q_ref, kv_ref, out_ref):
        for p in range(N_DEV):
            lo, hi, at = p * N_E_IN, (p + 1) * N_E_IN, 0
            for rows, first in L0_BLOCKS:
                start, stop = max(lo, at), min(hi, at + rows)
                if start < stop:
                    out_ref[p, start - lo:stop - lo, :] = w_ref[first + start - at:first + stop - at, :]
                at += rows
            out_ref[p, N_E_IN:RA0, :] = jnp.zeros((RA0 - N_E_IN, WIDE), out_ref.dtype)
            out_ref[p, RA0:RA0 + 32, :] = q_ref[p]
            out_ref[p, RA0 + 32:, :] = kv_ref[p]

    arrays = (d_w0t, d_q, d_kv)
    return pl.pallas_call(
        body, name=name, grid=(1,), in_specs=[_resident(a.shape, lambda i, n=a.ndim: (0,) * n) for a in arrays],
        out_specs=_resident((N_DEV, RA0 + RB0, WIDE), lambda i: (0, 0, 0)),
        out_shape=jax.ShapeDtypeStruct((N_DEV, RA0 + RB0, WIDE), BF16), compiler_params=_params(("arbitrary",)),
    )(*arrays)


def _layer1_in_weight_t(wt):
    main = jnp.concatenate([wt[:3 * D_MODEL], wt[3 * D_MODEL + FOX_HEADS:]], axis=0)
    return main, _pad_rows(wt[3 * D_MODEL:3 * D_MODEL + FOX_HEADS], LANES)


def _layer1_in_unpack(gath, *, name):
    n_main = 3 * D_MODEL

    def body(g_ref, w_ref, f_ref, o1_ref, o0_ref):
        f_ref[...] = jnp.zeros_like(f_ref)
        for p in range(N_DEV):
            o1_ref[128 * p:128 * p + 128, :] = g_ref[p, RA1:RA1 + 128, :]
            o0_ref[128 * p:128 * p + 128, :] = g_ref[p, RA1 + 128:RA1 + 256, :]
            lo, hi = p * N_O_IN, (p + 1) * N_O_IN
            for ref, first, start, stop in ((w_ref, 0, lo, min(hi, n_main)),
                                            (f_ref, -n_main, max(lo, n_main), min(hi, n_main + FOX_HEADS)),
                                            (w_ref, -FOX_HEADS, max(lo, n_main + FOX_HEADS), hi)):
                if start < stop:
                    ref[start + first:stop + first, :] = g_ref[p, start - lo:stop - lo, :]

    return pl.pallas_call(
        body, name=name, grid=(1,), in_specs=[_resident((N_DEV, RA1 + 256, WIDE), lambda i: (0, 0, 0))],
        out_specs=[_resident((rows, WIDE), lambda i: (0, 0)) for rows in (n_main + D_MODEL, LANES, D_MODEL, D_MODEL)],
        out_shape=[jax.ShapeDtypeStruct((rows, WIDE), gath.dtype) for rows in (n_main + D_MODEL, LANES, D_MODEL, D_MODEL)],
        compiler_params=_params(("arbitrary",)),
    )(gath)


def _late_grads_pack(d_qkv, d_wft, d_gate, d_wo1, d_wo0, d_o_g, *, name):
    n_main = 3 * D_MODEL
    arrays = (d_qkv, d_wft, d_gate, d_wo1, d_wo0, d_o_g)

    def body(q_ref, f_ref, g_ref, o1_ref, o0_ref, og_ref, out_ref):
        for p in range(N_DEV):
            lo, hi = p * N_O_IN, (p + 1) * N_O_IN
            for ref, first, start, stop in ((q_ref, 0, lo, min(hi, n_main)),
                                            (f_ref, -n_main, max(lo, n_main), min(hi, n_main + FOX_HEADS)),
                                            (g_ref, -n_main - FOX_HEADS, max(lo, n_main + FOX_HEADS), hi)):
                if start < stop:
                    out_ref[p, start - lo:stop - lo, :] = ref[start + first:stop + first, :]
            out_ref[p, N_O_IN:RA1, :] = jnp.zeros((RA1 - N_O_IN, WIDE), out_ref.dtype)
            out_ref[p, RA1:RA1 + 128, :] = o1_ref[128 * p:128 * p + 128, :]
            out_ref[p, RA1 + 128:RA1 + 256, :] = o0_ref[128 * p:128 * p + 128, :]
            out_ref[p, RA1 + 256:, :] = og_ref[p]

    return pl.pallas_call(
        body, name=name, grid=(1,), in_specs=[_resident(a.shape, lambda i, n=a.ndim: (0,) * n) for a in arrays],
        out_specs=_resident((N_DEV, RA1 + RB1, WIDE), lambda i: (0, 0, 0)),
        out_shape=jax.ShapeDtypeStruct((N_DEV, RA1 + RB1, WIDE), BF16), compiler_params=_params(("arbitrary",)),
    )(*arrays)


def _q_up_weight(w):
    return jnp.pad(w.reshape(MLA_Q_RANK, MLA_HEADS, 96), ((0, 0), (0, 0), (0, 32))).reshape(MLA_Q_RANK, MLA_HEADS * LANES)


def _q_up_grad(dwp):
    return dwp.reshape(MLA_Q_RANK, MLA_HEADS, LANES)[:, :, :96].reshape(MLA_Q_RANK, MLA_HEADS * 96)


def _kv_up_weight(w):
    w4 = w.reshape(MLA_KV_RANK, MLA_HEADS, 2, 64)
    kp = jnp.pad(w4[:, :, 0, :], ((0, 0), (0, 0), (0, 64))).reshape(MLA_KV_RANK, MLA_HEADS * LANES)
    vp = w4[:, :, 1, :].reshape(MLA_KV_RANK, MLA_HEADS * 64)
    return jnp.concatenate([kp, vp], axis=1)


def _kv_up_grad(dwp):
    dk = dwp[:, :MLA_HEADS * LANES].reshape(MLA_KV_RANK, MLA_HEADS, LANES)[:, :, :64]
    dv = dwp[:, MLA_HEADS * LANES:].reshape(MLA_KV_RANK, MLA_HEADS, 64)
    return jnp.stack([dk, dv], axis=2).reshape(MLA_KV_RANK, MLA_HEADS * LANES)


def _pad_lanes(a):
    return jnp.pad(a, ((0, 0), (0, LANES - a.shape[1])))


def _small_pack(g_in, g_final, g_q_a, g_kv_a, sinks, b_f, loss):
    rows = [g_in.reshape(8, LANES), g_final.reshape(8, LANES), g_q_a.reshape(2, LANES), g_kv_a.reshape(1, LANES),
            _pad_lanes(sinks.reshape(1, -1)), _pad_lanes(b_f.reshape(1, -1)), _pad_lanes(loss.reshape(1, 1)),
            jnp.zeros((2, LANES), F32)]
    return jnp.concatenate(rows, axis=0)


def _small_unpack(a):
    return (a[0:8].reshape(1, D_MODEL), a[8:16].reshape(D_MODEL), a[16:18].reshape(1, MLA_Q_RANK),
            a[18:19].reshape(1, MLA_KV_RANK), a[19:20, :SWA_HEADS], a[20:21, :FOX_HEADS], a[21, 0])


def _local_step(x, positions, target, e_g_in, early, e_g_q_a, e_g_kv_a, e_sinks,
                late, o_b_f, g_final, scatter1=None, scatter0=None):
    s = x.shape[0]
    mla_scale = (MLA_NOPE + MLA_ROPE) ** -0.5
    fox_scale = FOX_DIM ** -0.5
    n0a = Z0A_UNITS * LANES

    inv_freq = 1.0 / (ROPE_THETA ** (jnp.arange(0, MLA_ROPE, 2, dtype=F32) / MLA_ROPE))
    per_row = LANES // (MLA_ROPE // 2)
    ang = jnp.repeat(positions.astype(F32).reshape(s // per_row, per_row), MLA_ROPE // 2, axis=1) * jnp.tile(inv_freq, per_row)
    cos, sin = jnp.cos(ang).reshape(s, MLA_ROPE // 2), jnp.sin(ang).reshape(s, MLA_ROPE // 2)
    ones, zeros = jnp.ones((s, 64), F32), jnp.zeros((s, 64), F32)
    cos_t = jnp.concatenate([ones, cos, cos, ones[:, :32]], axis=1)
    sin_t = jnp.concatenate([zeros, -sin, sin, zeros[:, :32]], axis=1)
    cos_t, sin_t = lax.optimization_barrier((cos_t, sin_t))

    if len(early) == 3:
        h0 = _rmsnorm_fwd(x, e_g_in, width=D_MODEL, col_blk=0, name="l0_norm")
        w0t, wq, wkv = early
    else:
        pending, token, unpack, prep = early
        h0 = _rmsnorm_fwd(x, e_g_in, width=D_MODEL, col_blk=0, name="l0_norm", after=[token])
        sent, across = _peer_wait("across", *pending, after=[h0] + prep, name="weights0_wait")
        w0t, wq, wkv = unpack(sent, _to_other_core(sent, across, name="weights0_over"))
    z0a, z0b = _matmul_rows([(h0, w0t, True)], [], [], lambda r: (r[:, :n0a], r[:, n0a:]),
                            [("rows", n0a, F32), ("rows", Z0B_UNITS * LANES, BF16)], name="l0_in")
    cqn = _rmsnorm_fwd(z0a, e_g_q_a, width=MLA_Q_RANK, col_blk=4, name="l0_q_norm")
    ckvn = _rmsnorm_fwd(z0a, e_g_kv_a, width=MLA_KV_RANK, col_blk=10, name="l0_kv_norm")
    rope_rows = [(cos_t, LANES, 0), (sin_t, LANES, 0)]
    qm, = _matmul_rows([(cqn, wq, False)], rope_rows, [], _rope_q_epilogue, [("rows", MLA_HEADS * LANES, BF16)],
                       name="l0_q_up")
    kvm, = _matmul_rows([(ckvn, wkv, False)], [(z0a, LANES, 11)] + rope_rows, [], _rope_k_epilogue,
                        [("rows", MLA_HEADS * (LANES + MLA_V), BF16)], name="l0_kv_up")
    gathers = len(late) == 2
    res = _flash_fwd(qm, kvm, kvm, None, n_pairs=MLA_HEADS // 2, hw=LANES, q_off=0, k_off=0, v_off=MLA_HEADS,
                     scale=mla_scale, name="l0_mla_fwd", rider=("gather", late[0]) if gathers else None)
    o_mla, lse_mla = res[0], res[1]
    wo0, o_g_in, w1t, wft, wo1 = late[1](res[2]) if gathers else late
    o_swa, lse_swa = _swa_fwd(z0b, e_sinks, name="l0_swa_fwd")
    half = D_MODEL // 2

    x1, h1, og0 = _matmul_rows(
        [(None, wo0, False)], [(o_mla, half, 0), (o_swa, half, 0), (z0a, D_MODEL, 0), (x, D_MODEL, 0)], [o_g_in],
        lambda r, om, osw, gt, xt, g, made: (*_residual_norm_epilogue(r, xt, g), made),
        [("rows", D_MODEL, F32), ("rows", D_MODEL, BF16), ("rows", D_MODEL, BF16)], name="l0_out",
        prologue=lambda om, osw, gt, xt, g: _gated([om, osw], gt))
    z1, gate1, zf = _matmul_rows(
        [(None, w1t, True), (None, wft, True)], [(h1, D_MODEL, 0)], [],
        lambda r, h, made: (r[0][:, :3 * D_MODEL], r[0][:, 3 * D_MODEL:], r[1]),
        [("rows", 3 * D_MODEL, BF16), ("rows", D_MODEL, F32), ("rows", LANES, F32)], name="l1_in",
        prologue=lambda h: h, separate=True)
    bf = _pad_lanes(o_b_f)
    log_cum = _logf_fwd(zf, bf, name="l1_logf")
    bias2 = (-LOG2E * log_cum[:, :FOX_HEADS]).T
    t_bwd = min(ATT_T, s)
    bias = bias2.reshape(FOX_HEADS // 2, 2, s // t_bwd, 1, t_bwd)
    t_fwd = _fwd_tile(s)
    o_fox, lse_fox = _flash_fwd(z1, z1, z1, bias2.reshape(FOX_HEADS // 2, 2, s // t_fwd, 1, t_fwd),
                                n_pairs=FOX_HEADS // 2, hw=64, q_off=0, k_off=8, v_off=16, scale=fox_scale,
                                name="l1_fox_fwd")

    dx2, loss_part, d_g_final, og1, dx2_bf = _matmul_rows(
        [(None, wo1, False)], [(o_fox, D_MODEL, 0), (gate1, D_MODEL, 0), (x1, D_MODEL, 0), (target, D_MODEL, 0)],
        [g_final.reshape(1, D_MODEL)],
        lambda r, o, gt, xt, tg, g, made: _and_first(_loss_epilogue(r, xt, tg, g), made),
        [("rows", D_MODEL, F32), ("sum", (8, LANES)), ("sum", (1, D_MODEL)), ("rows", D_MODEL, BF16),
         ("rows", D_MODEL, BF16)], name="l1_out_loss", prologue=lambda o, gt, xt, tg, g: _gated([o], gt))

    d_wo1 = _matmul(og1, dx2_bf, ta=True, out_dtype=BF16, name="l1_out_dw")
    do_fox, d_gate1 = _matmul_rows([(dx2_bf, wo1, True)], [(o_fox, D_MODEL, 0), (gate1, D_MODEL, 0)], [],
                                   _gate_bwd_epilogue([D_MODEL]), [("rows", D_MODEL, F32), ("rows", D_MODEL, BF16)],
                                   name="l1_out_dx")
    dqkv1, dbias, drow = _flash_bwd(z1, z1, z1, do_fox, o_fox, lse_fox, bias, n_pairs=FOX_HEADS // 2, hw=64, q_off=0,
                                    k_off=8, v_off=16, scale=fox_scale, qk_dtype=BF16, stacked=True, name="l1_fox_bwd")
    d_log_cum = (drow.reshape(FOX_HEADS, s) - dbias.reshape(FOX_HEADS, s)).T
    d_log_cum = jnp.pad(d_log_cum, ((0, 0), (0, LANES - FOX_HEADS)))
    d_zf, d_bf = _logf_bwd(d_log_cum, zf, bf, name="l1_logf_bwd")
    d_w1t = (_matmul(dqkv1, h1, ta=True, out_dtype=BF16, name="l1_in_dw_qkv"),
             _matmul(d_gate1, h1, ta=True, out_dtype=BF16, name="l1_in_dw_gate"))
    d_wft = _matmul(d_zf, h1, ta=True, out_dtype=BF16, name="l1_in_f_dw")
    dx1, d_o_g_in, dx1_bf = _matmul_rows([(dqkv1, w1t, False, c * D_MODEL, c) for c in range(3)]
                                         + [(d_gate1, w1t, False, 3 * D_MODEL), (d_zf, wft, False)],
                                         [(x1, D_MODEL, 0), (dx2, D_MODEL, 0)], [o_g_in],
                                         lambda *a: _and_first(_rms_bwd_epilogue(*a)),
                                         [("rows", D_MODEL, F32), ("sum", (1, D_MODEL)), ("rows", D_MODEL, BF16)],
                                         name="l1_in_dx")

    d_wo0 = _matmul(og0, dx1_bf, ta=True, out_dtype=BF16, name="l0_out_dw")
    do_mla, do_swa, d_gate0 = _matmul_rows(
        [(dx1_bf, wo0, True)], [(o_mla, half, 0), (o_swa, half, 0), (z0a, D_MODEL, 0)], [], _gate_bwd_epilogue([half, half]),
        [("rows", half, F32), ("rows", half, F32), ("rows", D_MODEL, BF16)], name="l0_out_dx")
    dq_s, dkt_s, dvt_s, d_sinks = _swa_bwd(z0b, e_sinks, do_swa, o_swa, lse_swa, name="l0_swa_bwd")
    dk_s = dkt_s.transpose(0, 2, 1).reshape(s, LANES)
    dv_s = dvt_s.transpose(0, 2, 1).reshape(s, LANES)
    rider = None
    if scatter1 is not None:
        rider = ("exchange", scatter1(dict(w1t=d_w1t, wft=d_wft, wo1=d_wo1, o_g_in=d_o_g_in, wo0=d_wo0)))
    res = _flash_bwd(qm, kvm, kvm, do_mla, o_mla, lse_mla, None, n_pairs=MLA_HEADS // 2, hw=LANES, q_off=0, k_off=0,
                     v_off=MLA_HEADS, scale=mla_scale, qk_dtype=F32, name="l0_mla_bwd", rider=rider)
    dqm, dkm, dvm = res[0], res[1], res[2]
    recv1 = res[3] if rider is not None else None
    d_qp, d_kvp, d_kpe = _rope_bwd(dqm, dkm, dvm, cos_t, sin_t, name="l0_rope_bwd")
    d_wq = _matmul(cqn, d_qp, ta=True, out_dtype=BF16, name="l0_q_up_dw")
    d_cqn = _matmul(d_qp, wq, tb=True, name="l0_q_up_dx")
    d_wkv = _matmul(ckvn, d_kvp, ta=True, out_dtype=BF16, name="l0_kv_up_dw")
    d_ckvn = _matmul(d_kvp, wkv, tb=True, name="l0_kv_up_dx")
    d_cq, d_g_q_a = _rmsnorm_bwd(z0a, e_g_q_a, d_cqn, width=MLA_Q_RANK, col_blk=4, name="l0_q_norm_bwd")
    d_ckv, d_g_kv_a = _rmsnorm_bwd(z0a, e_g_kv_a, d_ckvn, width=MLA_KV_RANK, col_blk=10, name="l0_kv_norm_bwd")
    dz0 = jnp.concatenate([d_gate0, d_cq, d_ckv, d_kpe, dq_s.astype(BF16), dk_s.astype(BF16), dv_s.astype(BF16)], axis=1)
    d_w0t = _matmul(dz0, h0, ta=True, out_dtype=BF16, name="l0_in_dw")
    pending0, after_start = None, []
    if scatter0 is not None:
        *pending0, token = _peer_start("exchange", scatter0(dict(w0t=d_w0t, wq=d_wq, wkv=d_wkv)), name="grads0_start")
        after_start = [token]
    grad_x, d_e_g_in = _matmul_rows(
        [(dz0, w0t, False)], [(x, D_MODEL, 0), (dx1, D_MODEL, 0)], [e_g_in] + after_start,
        lambda dy, xt, add, g, *_: _rms_bwd_epilogue(dy, xt, add, g),
        [("rows", D_MODEL, F32), ("sum", (1, D_MODEL))], name="l0_in_dx")

    return dict(pending0=pending0, recv1=recv1, loss=loss_part[0, 0], grad_x=grad_x, e_g_in=d_e_g_in, w0t=d_w0t, e_g_q_a=d_g_q_a, wq=d_wq,
                e_g_kv_a=d_g_kv_a, wkv=d_wkv, e_sinks=d_sinks[:, 0].reshape(1, SWA_HEADS), wo0=d_wo0,
                o_g_in=d_o_g_in, w1t=d_w1t, wft=d_wft, o_b_f=d_bf[:, :FOX_HEADS], wo1=d_wo1, g_final=d_g_final.reshape(D_MODEL))


def _wide(a, rows):
    flat = a.reshape(-1)
    return jnp.pad(flat, (0, rows * WIDE - flat.shape[0])).reshape(rows, WIDE)


def _rows_b0(w_q, w_kv):
    return jnp.concatenate([_wide(w_q, 32), _wide(w_kv, 16)], axis=0)


def _unflat_b0(f):
    return f[0:24].reshape(1, MLA_Q_RANK, 96), f[32:48].reshape(1, MLA_KV_RANK, 128)


def _rows_b1(o_w_out, e_w_out, g_in):
    return jnp.concatenate([o_w_out, e_w_out, _wide(g_in, 16)], axis=0)


def _unflat_b1(f):
    return f[0:128][None], f[128:256][None], f[256:257, :LANES]


def kernel(x, positions, e_g_in, e_w_in, e_g_q_a, e_w_q_up, e_g_kv_a, e_w_kv_up, e_sinks, e_w_out, o_g_in, o_w_in, o_b_f, o_w_out, g_final, loss_target, m_e_g_in, m_e_w_in, m_e_g_q_a, m_e_w_q_up, m_e_g_kv_a, m_e_w_kv_up, m_e_sinks, m_e_w_out, m_o_g_in, m_o_w_in, m_o_b_f, m_o_w_out, m_g_final, v_e_g_in, v_e_w_in, v_e_g_q_a, v_e_w_q_up, v_e_g_kv_a, v_e_w_kv_up, v_e_sinks, v_e_w_out, v_o_g_in, v_o_w_in, v_o_b_f, v_o_w_out, v_g_final):
    def bf(a):
        return a.astype(BF16)

    me = 4 * lax.axis_index("x") + 2 * lax.axis_index("y") + lax.axis_index("c")
    shard0 = jnp.concatenate([_pad_rows(bf(e_w_in[0]).T, RA0), _rows_b0(bf(e_w_q_up[0]), bf(e_w_kv_up[0]))], axis=0)
    *pending_w0, token_w0 = _peer_start("across", shard0, name="weights0_start")

    def unpack0(sent, gath0):
        gath0 = lax.dynamic_update_slice_in_dim(gath0, sent[None], me, axis=0)
        w0t = _layer0_in_unpack(gath0, name="weights0_unpack")
        wq = _q_up_weight(_gathered_cols(gath0[:, RA0:RA0 + 24], MLA_Q_RANK))
        wkv = _kv_up_weight(_gathered_cols(gath0[:, RA0 + 32:RA0 + 48], MLA_KV_RANK))
        return w0t, wq, wkv

    rows_b0 = [_rows_b0(q[0], kv[0]) for q, kv in ((e_w_q_up, e_w_kv_up), (m_e_w_q_up, m_e_w_kv_up), (v_e_w_q_up, v_e_w_kv_up))]
    rows_b1 = [_rows_b1(o[0], e[0], g) for o, e, g in ((o_w_out, e_w_out, o_g_in), (m_o_w_out, m_e_w_out, m_o_g_in),
                                                       (v_o_w_out, v_e_w_out, v_o_g_in))]

    g_bits = lax.bitcast_convert_type(o_g_in.reshape(LANES), BF16)
    shard1 = jnp.concatenate([_pad_rows(bf(o_w_in[0]).T, RA1), _rows_b1(bf(o_w_out[0]), bf(e_w_out[0]), g_bits)], axis=0)

    def unpack1(gath1):
        w1t, wft, wo1, wo0 = _layer1_in_unpack(gath1, name="weights1_unpack")
        bits = gath1[:, RA1 + 256, :2 * LANES].reshape(N_DEV, LANES, 2)
        return wo0, lax.bitcast_convert_type(bits, F32).reshape(1, D_MODEL), w1t, wft, wo1

    def scatter1(g):
        d_o_g = jnp.pad(bf(g["o_g_in"]).reshape(N_DEV, 1, LANES), ((0, 0), (0, 15), (0, WIDE - LANES)))
        return _late_grads_pack(g["w1t"][0], g["wft"], g["w1t"][1], g["wo1"], g["wo0"], d_o_g, name="grads1_pack")

    def scatter0(g):
        return _early_grads_pack(g["w0t"], _pad_rows(_scatter_cols(_q_up_grad(g["wq"])), 32),
                                 _scatter_cols(_kv_up_grad(g["wkv"])), name="grads0_pack")

    gr = _local_step(x[0], positions[0], loss_target[0], e_g_in,
                     (pending_w0, token_w0, unpack0, [shard1] + rows_b0 + rows_b1), e_g_q_a, e_g_kv_a, e_sinks,
                     (shard1, unpack1), o_b_f, g_final, scatter1=scatter1, scatter0=scatter0)

    def in_projection(recv, ra, n, w, m, v, name):
        g = _sum8(recv, ra, name=name + "_grad_sum")[:n].reshape(n, 1, D_MODEL)
        w, m, v = [jnp.transpose(a, (2, 0, 1)) for a in (w, m, v)]
        return (g, *_adamw_columns(g, w, m, v, name=name + "_adamw"))

    o_in = in_projection(gr["recv1"], RA1, N_O_IN, o_w_in, m_o_w_in, v_o_w_in, "o_w_in")
    b1 = _adamw(gr["recv1"][:, RA1:], *rows_b1, name="adamw_late")

    small = _small_pack(gr["e_g_in"], gr["g_final"], gr["e_g_q_a"], gr["e_g_kv_a"], gr["e_sinks"], gr["o_b_f"], gr["loss"])
    small_all = _all_gather(small, name="small_all_gather")
    zero = jnp.zeros((), F32)
    w_small = _small_pack(e_g_in, g_final, e_g_q_a, e_g_kv_a, e_sinks, o_b_f, zero)
    m_small = _small_pack(m_e_g_in, m_g_final, m_e_g_q_a, m_e_g_kv_a, m_e_sinks, m_o_b_f, zero)
    v_small = _small_pack(v_e_g_in, v_g_final, v_e_g_q_a, v_e_g_kv_a, v_e_sinks, v_o_b_f, zero)
    smalls = _adamw(small_all, w_small, m_small, v_small, name="adamw_replicated")
    g_sm, d_sm, m_sm, v_sm = [_small_unpack(a) for a in smalls]
    loss = g_sm[6]

    sent0, recv0 = _peer_wait("exchange", *gr["pending0"], after=[o_in[1], b1[1], smalls[1]], name="grads0_wait")
    own = lax.dynamic_slice_in_dim(sent0, me, 1, axis=0)
    recv0 = lax.dynamic_update_slice_in_dim(recv0, own, me, axis=0)
    e_in = in_projection(recv0, RA0, N_E_IN, e_w_in, m_e_w_in, v_e_w_in, "e_w_in")
    b0 = _adamw(recv0[:, RA0:], *rows_b0, name="adamw_early")

    def sharded(k):
        q_up, kv_up = _unflat_b0(b0[k])
        o_out, e_out, o_g = _unflat_b1(b1[k])
        return jnp.transpose(e_in[k], (1, 2, 0)), q_up, kv_up, e_out, jnp.transpose(o_in[k], (1, 2, 0)), o_out, o_g

    g_sh, d_sh, m_sh, v_sh = [sharded(k) for k in range(4)]

    def leaves(sh, sm):
        return (sm[0], sh[0], sm[2], sh[1], sm[3], sh[2], sm[4], sh[3], sh[6], sh[4], sm[5], sh[5], sm[1])

    return (loss, gr["grad_x"][None], *leaves(g_sh, g_sm), *leaves(d_sh, d_sm), *leaves(m_sh, m_sm), *leaves(v_sh, v_sm))
```

```python
import functools

import jax
import jax.numpy as jnp
from jax import lax
from jax.experimental import pallas as pl
from jax.experimental.pallas import tpu as pltpu

F32 = jnp.float32
BF16 = jnp.bfloat16
NEG_INF = float("-inf")

N_DEV = 8
LANES = 128
D_MODEL = 1024
EPS = 1e-6
ROPE_THETA = 10000.0
MLA_HEADS = 8
MLA_Q_RANK = 256
MLA_KV_RANK = 128
MLA_NOPE = 64
MLA_ROPE = 32
MLA_V = 64
SWA_HEADS = 8
SWA_KV_HEADS = 2
SWA_DIM = 64
WINDOW = 128
FOX_HEADS = 16
FOX_DIM = 64

ADAM_LR = 0.001
ADAM_B1 = 0.9
ADAM_B2 = 0.999
ADAM_EPS = 1e-08
ADAM_WD = 0.01
ADAM_STEP = 10

ATT_T = 512
ATT_T_FWD = 1024
VMEM_LIMIT = 56 * 1024 * 1024
MATMUL_B_BLOCK_BYTES = 8 * 1024 * 1024

Z0A_UNITS = 12
Z0B_UNITS = 6

WIDE = 1024
N_E_IN = 276
N_O_IN = 514
RA0 = 288
RB0 = 32 + 16
RA1 = 528
RB1 = 128 + 128 + 16
SMALL_ROWS = 24


def _tile(n, cands):
    for c in cands:
        if n % c == 0:
            return c
    raise ValueError(f"no tile for {n}")


ROW_TILES = (512, 256, 128)


def _params(sem, vmem=VMEM_LIMIT):
    return pltpu.CompilerParams(dimension_semantics=sem, vmem_limit_bytes=vmem)


def _matmul(a, b, *, name, ta=False, tb=False, out_dtype=F32):
    if ta:
        kdim, m = a.shape[-2], a.shape[-1] * (a.shape[0] if a.ndim == 3 else 1)
    else:
        m, kdim = a.shape
    if tb:
        n, kb = b.shape
    else:
        kb, n = b.shape
    assert kdim == kb, (a.shape, b.shape)
    tm = _tile(m, (512, 256, 128))
    tn = _tile(n, [c for c in (1024, 768, 512, 384, 256, 128) if c * kdim * b.dtype.itemsize <= MATMUL_B_BLOCK_BYTES])
    dims = (((0 if ta else 1,), (1 if tb else 0,)), ((), ()))

    def body(a_ref, b_ref, o_ref):
        r = lax.dot_general(a_ref[...].astype(BF16), b_ref[...].astype(BF16), dims, preferred_element_type=F32)
        o_ref[...] = r.astype(out_dtype)

    if a.ndim == 3:
        per = a.shape[2] // tm
        a_spec = pl.BlockSpec((None, kdim, tm), lambda i, j: (i // per, 0, i % per))
    else:
        a_spec = pl.BlockSpec((kdim, tm), lambda i, j: (0, i)) if ta else pl.BlockSpec((tm, kdim), lambda i, j: (i, 0))
    b_spec = pl.BlockSpec((tn, kdim), lambda i, j: (j, 0)) if tb else pl.BlockSpec((kdim, tn), lambda i, j: (0, j))
    return pl.pallas_call(
        body, name=name, grid=(m // tm, n // tn), in_specs=[a_spec, b_spec],
        out_specs=pl.BlockSpec((tm, tn), lambda i, j: (i, j)), out_shape=jax.ShapeDtypeStruct((m, n), out_dtype),
        compiler_params=_params(("parallel", "parallel")),
    )(a, b)


def _rmsnorm_fwd(x, g, *, width, col_blk, name, after=()):
    s = x.shape[0]
    tm = _tile(s, ROW_TILES)

    def body(x_ref, g_ref, *rest):
        y_ref = rest[-1]
        xf = x_ref[...].astype(F32)
        r = lax.rsqrt(jnp.mean(xf * xf, axis=-1, keepdims=True) + EPS)
        y_ref[...] = ((xf * r) * g_ref[...]).astype(BF16)

    return pl.pallas_call(
        body, name=name, grid=(s // tm,),
        in_specs=[pl.BlockSpec((tm, width), lambda i: (i, col_blk)), pl.BlockSpec((1, width), lambda i: (0, 0))]
        + [ANY] * len(after),
        out_specs=pl.BlockSpec((tm, width), lambda i: (i, 0)),
        out_shape=jax.ShapeDtypeStruct((s, width), BF16),
        compiler_params=_params(("parallel",)),
    )(x, g, *after)


def _rmsnorm_bwd(x, g, dy, *, width, col_blk, name):
    s = x.shape[0]
    tm = _tile(s, ROW_TILES)

    def body(x_ref, g_ref, dy_ref, dx_ref, dg_ref):
        @pl.when(pl.program_id(0) == 0)
        def _():
            dg_ref[...] = jnp.zeros_like(dg_ref)

        dx, dg = _rms_bwd_epilogue(dy_ref[...], x_ref[...], 0.0, g_ref[...])
        dg_ref[...] += dg
        dx_ref[...] = dx.astype(BF16)

    return pl.pallas_call(
        body, name=name, grid=(s // tm,),
        in_specs=[pl.BlockSpec((tm, width), lambda i: (i, col_blk)), pl.BlockSpec((1, width), lambda i: (0, 0)),
                  pl.BlockSpec((tm, width), lambda i: (i, 0))],
        out_specs=[pl.BlockSpec((tm, width), lambda i: (i, 0)), pl.BlockSpec((1, width), lambda i: (0, 0))],
        out_shape=[jax.ShapeDtypeStruct((s, width), BF16), jax.ShapeDtypeStruct((1, width), F32)],
        compiler_params=_params(("arbitrary",)),
    )(x, g, dy)


def _sigmoid(x):
    return 1.0 / (1.0 + jnp.exp(-x))


def _matmul_rows(terms, row_inputs, params, epilogue, outs, *, name, prologue=None, separate=False):
    s = row_inputs[0][0].shape[0] if row_inputs else terms[0][0].shape[-2]
    tm = _tile(s, ROW_TILES)
    steps = s // tm
    n_r, n_p, n_o = len(row_inputs), len(params), len(outs)
    n_t = sum(1 if term[0] is None else 2 for term in terms)

    def body(*refs):
        t_refs, r_refs = list(refs[:n_t]), refs[n_t:n_t + n_r]
        p_refs, o_refs = refs[n_t + n_r:n_t + n_r + n_p], refs[n_t + n_r + n_p:]
        i = pl.program_id(0)
        rows, small = [r[...] for r in r_refs], [p[...] for p in p_refs]
        made = None if prologue is None else prologue(*rows, *small)
        parts = []
        for term in terms:
            a = made if term[0] is None else t_refs.pop(0)[...].astype(BF16)
            dims = (((1,), (1 if term[2] else 0,)), ((), ()))
            parts.append(lax.dot_general(a, t_refs.pop(0)[...].astype(BF16), dims, preferred_element_type=F32))
        acc = parts if separate else sum(parts[1:], parts[0])
        vals = epilogue(acc, *rows, *small) if prologue is None else epilogue(acc, *rows, *small, made)
        for ref, val, out in zip(o_refs, vals, outs):
            if out[0] == "rows":
                ref[...] = val.astype(ref.dtype)
            else:
                @pl.when(i == 0)
                def _(ref=ref):
                    ref[...] = jnp.zeros_like(ref)

                ref[...] += val

    in_specs, args = [], []
    for term in terms:
        a, b = term[0], term[1]
        if a is None:
            in_specs.append(_resident(b.shape, lambda i: (0, 0)))
            args.append(b)
            continue
        b_rows = b.shape[0] if term[2] or len(term) < 4 else a.shape[-1]
        b_blk = 0 if len(term) < 4 else term[3] // b_rows
        if len(term) == 5:
            a_spec = pl.BlockSpec((None, tm, a.shape[2]), lambda i, c=term[4]: (c, i, 0))
        else:
            a_spec = pl.BlockSpec((tm, a.shape[1]), lambda i: (i, 0))
        in_specs += [a_spec, _resident((b_rows, b.shape[1]), lambda i, b_blk=b_blk: (b_blk, 0))]
        args += [a, b]
    for arr, width, col_blk in row_inputs:
        in_specs.append(pl.BlockSpec((tm, width), lambda i, col_blk=col_blk: (i, col_blk)))
        args.append(arr)
    for p in params:
        in_specs.append(pl.BlockSpec(p.shape, lambda i: (0, 0)))
        args.append(p)
    out_specs, out_shape = [], []
    for out in outs:
        if out[0] == "rows":
            out_specs.append(pl.BlockSpec((tm, out[1]), lambda i: (i, 0)))
            out_shape.append(jax.ShapeDtypeStruct((s, out[1]), out[2]))
        else:
            out_specs.append(pl.BlockSpec(out[1], lambda i: (0, 0)))
            out_shape.append(jax.ShapeDtypeStruct(out[1], F32))
    return pl.pallas_call(
        body, name=name, grid=(steps,), in_specs=in_specs, out_specs=out_specs, out_shape=out_shape,
        compiler_params=_params(("arbitrary",)),
    )(*args)


RING = 3


def _in_projection_ring(h, w, wf, *, name):
    s, k = h.shape
    tm = _tile(s, ROW_TILES)
    steps = s // tm
    n_qkv = w.shape[0] - D_MODEL

    def body(h_hbm, w_ref, wf_ref, z_ref, g_ref, f_ref, buf, sems):
        i = pl.program_id(0)

        def fetch(step):
            slot = step % RING
            return pltpu.make_async_copy(h_hbm.at[pl.ds(step * tm, tm)], buf.at[slot], sems.at[slot])

        @pl.when(i == 0)
        def _():
            for ahead in range(min(RING - 1, steps)):
                fetch(ahead).start()

        @pl.when(i + RING - 1 < steps)
        def _():
            fetch(i + RING - 1).start()

        fetch(i).wait()
        a = buf[i % RING]
        r = lax.dot_general(a, w_ref[...], _NT, preferred_element_type=F32)
        z_ref[...] = r[:, :n_qkv].astype(BF16)
        g_ref[...] = r[:, n_qkv:]
        f_ref[...] = lax.dot_general(a, wf_ref[...], _NT, preferred_element_type=F32)

    return pl.pallas_call(
        body, name=name, grid=(steps,),
        in_specs=[ANY, _resident(w.shape, lambda i: (0, 0)), _resident(wf.shape, lambda i: (0, 0))],
        out_specs=[pl.BlockSpec((tm, n_qkv), lambda i: (i, 0)), pl.BlockSpec((tm, D_MODEL), lambda i: (i, 0)),
                   pl.BlockSpec((tm, wf.shape[0]), lambda i: (i, 0))],
        out_shape=[jax.ShapeDtypeStruct((s, n_qkv), BF16), jax.ShapeDtypeStruct((s, D_MODEL), F32),
                   jax.ShapeDtypeStruct((s, wf.shape[0]), F32)],
        scratch_shapes=[pltpu.VMEM((RING, tm, k), BF16), pltpu.SemaphoreType.DMA((RING,))],
        compiler_params=_params(("arbitrary",)),
    )(h, w, wf)


def _rms_stats(x):
    r = lax.rsqrt(jnp.mean(x * x, axis=-1, keepdims=True) + EPS)
    return r, x * r


def _gated(o_parts, gate):
    o = o_parts[0] if len(o_parts) == 1 else jnp.concatenate(o_parts, axis=1)
    return (o * (gate * _sigmoid(gate))).astype(BF16)


def _and_first(vals, *more):
    return (*vals, *more, vals[0])


def _residual_norm_epilogue(r, x, g):
    x1 = x + r
    _, xh = _rms_stats(x1)
    return x1, xh * g


def _rms_bwd_epilogue(dy, x, add, g):
    r, xh = _rms_stats(x)
    dxh = dy * g
    dx = r * (dxh - xh * jnp.mean(dxh * xh, axis=-1, keepdims=True)) + add
    return dx, jnp.sum(dy * xh, axis=0, keepdims=True)


def _loss_epilogue(r, x1, target, g):
    rs, xh = _rms_stats(x1 + r)
    err = xh * g - target
    loss = jnp.broadcast_to(0.5 * jnp.sum(jnp.mean(err * err, axis=-1, keepdims=True)), (8, LANES))
    dy = err * (1.0 / D_MODEL)
    dxh = dy * g
    dx = rs * (dxh - xh * jnp.mean(dxh * xh, axis=-1, keepdims=True))
    return dx, loss, jnp.sum(dy * xh, axis=0, keepdims=True)


def _gate_bwd_epilogue(widths):
    def epilogue(d, *rows):
        o_parts, gt = rows[:-1], rows[-1]
        o = o_parts[0] if len(o_parts) == 1 else jnp.concatenate(o_parts, axis=1)
        sg = _sigmoid(gt)
        do = d * (gt * sg)
        d_gate = d * o * (sg * (1.0 + gt * (1.0 - sg)))
        cuts = [sum(widths[:k]) for k in range(len(widths) + 1)]
        return tuple(do[:, cuts[k]:cuts[k + 1]] for k in range(len(widths))) + (d_gate,)

    return epilogue


def _rot_half(x):
    lane = lax.broadcasted_iota(jnp.int32, x.shape, 1)
    return jnp.where(lane < 80, pltpu.roll(x, LANES - 16, axis=1), pltpu.roll(x, 16, axis=1))


def _rot_half_t(g):
    lane = lax.broadcasted_iota(jnp.int32, g.shape, 1)
    lo = (lane >= MLA_NOPE) & (lane < MLA_NOPE + MLA_ROPE // 2)
    hi = (lane >= MLA_NOPE + MLA_ROPE // 2) & (lane < MLA_NOPE + MLA_ROPE)
    return jnp.where(lo, pltpu.roll(g, LANES - 16, axis=1), jnp.where(hi, pltpu.roll(g, 16, axis=1), 0.0))


def _rope_q_epilogue(q, c, sn):
    heads = [q[:, h * LANES:(h + 1) * LANES] for h in range(MLA_HEADS)]
    return (jnp.concatenate([qh * c + _rot_half(qh) * sn for qh in heads], axis=1),)


def _rope_k_epilogue(kv, kpe, c, sn):
    kpe_r = kpe * c + _rot_half(kpe) * sn
    lane = lax.broadcasted_iota(jnp.int32, kpe.shape, 1)
    heads = [jnp.where(lane < MLA_NOPE, kv[:, h * LANES:(h + 1) * LANES], kpe_r) for h in range(MLA_HEADS)]
    return (jnp.concatenate(heads + [kv[:, MLA_HEADS * LANES:]], axis=1),)


def _rope_bwd(dqm, dkm, dvm, cos_t, sin_t, *, name):
    s = dqm.shape[0]
    tm = _tile(s, ROW_TILES)
    hw = MLA_HEADS * LANES
    vw = MLA_HEADS * MLA_V

    def body(dq_ref, dk_ref, dv_ref, c_ref, s_ref, dqp_ref, dkv_ref, dkpe_ref):
        c = c_ref[...]
        sn = s_ref[...]
        ksum = jnp.zeros((tm, LANES), F32)
        for h in range(MLA_HEADS):
            sl = slice(h * LANES, (h + 1) * LANES)
            dq = dq_ref[:, sl]
            dqp_ref[:, sl] = (dq * c + _rot_half_t(dq * sn)).astype(BF16)
            dk = dk_ref[:, sl]
            dkv_ref[:, sl] = dk.astype(BF16)
            ksum = ksum + dk
        dkv_ref[:, hw:] = dv_ref[...]
        lane = lax.broadcasted_iota(jnp.int32, ksum.shape, 1)
        dkpe = ksum * c + _rot_half_t(ksum * sn)
        dkpe_ref[...] = jnp.where((lane >= MLA_NOPE) & (lane < MLA_NOPE + MLA_ROPE), dkpe, 0.0).astype(BF16)

    return pl.pallas_call(
        body, name=name, grid=(s // tm,),
        in_specs=[pl.BlockSpec((tm, hw), lambda i: (i, 0)), pl.BlockSpec((tm, hw), lambda i: (i, 0)),
                  pl.BlockSpec((tm, vw), lambda i: (i, 0)),
                  pl.BlockSpec((tm, LANES), lambda i: (i, 0)), pl.BlockSpec((tm, LANES), lambda i: (i, 0))],
        out_specs=[pl.BlockSpec((tm, hw), lambda i: (i, 0)), pl.BlockSpec((tm, hw + vw), lambda i: (i, 0)),
                   pl.BlockSpec((tm, LANES), lambda i: (i, 0))],
        out_shape=[jax.ShapeDtypeStruct((s, hw), BF16), jax.ShapeDtypeStruct((s, hw + vw), BF16),
                   jax.ShapeDtypeStruct((s, LANES), BF16)],
        compiler_params=_params(("parallel",)),
    )(dqm, dkm, dvm, cos_t, sin_t)


def _head_mask(shape, a):
    lane = lax.broadcasted_iota(jnp.int32, shape, 1)
    return (lane >= 64 * a) & (lane < 64 * (a + 1))


_NT = (((1,), (1,)), ((), ()))
LOG2E = 1.4426950408889634


def _stack_heads(tile, hw):
    lane = lax.broadcasted_iota(jnp.int32, tile.shape, 1)
    z = jnp.zeros_like(tile)
    return jnp.concatenate([jnp.where(lane < hw, tile, z), jnp.where(lane >= hw, tile, z)], axis=0)


def _stacked_rows(r0, r1, t):
    n = r0.shape[-1]
    return jnp.concatenate([jnp.broadcast_to(r0, (t, n)), jnp.broadcast_to(r1, (t, n))], axis=0)


def _resident(block, index_map):
    return pl.BlockSpec(block, index_map, pipeline_mode=pl.Buffered(1))


def _fwd_tile(s):
    return ATT_T_FWD if s % ATT_T_FWD == 0 else min(ATT_T, s)


def _flash_fwd(q, k, v, bias, *, n_pairs, hw, q_off, k_off, v_off, scale, name, rider=None):
    s = q.shape[0]
    t = _fwd_tile(s)
    nb = s // t
    qw = 2 * hw
    has_bias = bias is not None
    c1 = scale * LOG2E

    def body(*refs):
        refs, ride_refs = _split_rider(refs, rider, n_in=4 if has_bias else 3, n_out=2)
        if has_bias:
            q_ref, k_ref, v_ref, b_ref, o_ref, lse_ref, vt_ref, bcol_ref = refs
        else:
            q_ref, k_ref, v_ref, o_ref, lse_ref, vt_ref = refs
            b_ref = bcol_ref = None
        _ride_start(rider, ride_refs, pl.program_id(0) == 0)
        row = lax.broadcasted_iota(jnp.int32, (t, t), 0)
        col = lax.broadcasted_iota(jnp.int32, (t, t), 1)
        cmask_t = jnp.concatenate([row <= col, row <= col], axis=1)
        lane_lt64 = lax.broadcasted_iota(jnp.int32, (t, LANES), 1) < 64

        def as_column(r):
            return jnp.broadcast_to(r, (8, r.shape[1])).T[:, 0:1]

        def v_block(j, _):
            c0 = pl.multiple_of(j * t, t)
            vt_ref[j] = v_ref[pl.ds(c0, t), :].astype(F32).T.astype(BF16)
            if has_bias:
                for a in range(2):
                    bcol_ref[a, pl.ds(c0, t), :] = as_column(b_ref[0, a, j])
            return 0

        lax.fori_loop(0, nb, v_block, 0)

        def stacked_queries(i):
            return _stack_heads(q_ref[pl.ds(pl.multiple_of(i * t, t), t), :], hw).astype(F32).T.astype(BF16)

        def kv_step(j, carry, qs_t, masked):
            m, l, acc = carry
            rows = pl.ds(pl.multiple_of(j * t, t), t)
            sc = jnp.dot(k_ref[rows, :], qs_t, preferred_element_type=F32) * c1
            if has_bias:
                sc = sc + jnp.concatenate([jnp.broadcast_to(bcol_ref[0, rows, :], (t, t)),
                                           jnp.broadcast_to(bcol_ref[1, rows, :], (t, t))], axis=1)
            if masked:
                sc = jnp.where(cmask_t, sc, NEG_INF)
            m_new = jnp.maximum(m, jnp.max(sc, axis=0, keepdims=True))
            alpha = jnp.exp2(m - m_new)
            p = jnp.exp2(sc - m_new)
            l_new = alpha * l + jnp.sum(p, axis=0, keepdims=True)
            pv = jnp.dot(vt_ref[j], p.astype(BF16), preferred_element_type=F32)
            return m_new, l_new, alpha * acc + pv

        def finish(i, carry):
            m, l, acc = carry
            r0 = pl.multiple_of(i * t, t)
            out = (acc / l).T
            lse2 = as_column(m + jnp.log2(l))
            lse_ref[0, 0, pl.ds(r0, t), :] = lse2[:t]
            lse_ref[0, 1, pl.ds(r0, t), :] = lse2[t:]
            o_ref[pl.ds(r0, t), :] = jnp.where(lane_lt64, out[:t], out[t:])

        init = (jnp.full((1, 2 * t), NEG_INF, F32), jnp.zeros((1, 2 * t), F32), jnp.zeros((LANES, 2 * t), F32))

        def q_block(i, _):
            qs_t = stacked_queries(i)
            carry = lax.fori_loop(0, i, lambda j, c: kv_step(j, c, qs_t, False), init)
            finish(i, kv_step(i, carry, qs_t, True))
            return 0

        lax.fori_loop(0, nb, q_block, 0)
        _ride_wait(rider, ride_refs, pl.program_id(0) == n_pairs - 1)

    in_specs = [_resident((s, qw), lambda p: (0, q_off + p)), _resident((s, qw), lambda p: (0, k_off + p)),
                _resident((s, LANES), lambda p: (0, v_off + p))]
    args = [q, k, v]
    if has_bias:
        in_specs.append(_resident((1, 2, nb, 1, t), lambda p: (p, 0, 0, 0, 0)))
        args.append(bias)
    out_specs = [pl.BlockSpec((s, LANES), lambda p: (0, p)), pl.BlockSpec((1, 2, s, 1), lambda p: (p, 0, 0, 0))]
    out_shape = [jax.ShapeDtypeStruct((s, n_pairs * LANES), F32), jax.ShapeDtypeStruct((n_pairs, 2, s, 1), F32)]
    scratch = [pltpu.VMEM((nb, LANES, t), BF16)] + ([pltpu.VMEM((2, s, 1), F32)] if has_bias else [])
    scratch += _add_rider(rider, in_specs, args, out_specs, out_shape)
    return pl.pallas_call(
        body, name=name, grid=(n_pairs,), in_specs=in_specs, out_specs=out_specs, out_shape=out_shape,
        scratch_shapes=scratch,
        compiler_params=_params(("parallel",) if rider is None else ("arbitrary",)),
    )(*args)


def _flash_bwd(q, k, v, do, o, lse, bias, *, n_pairs, hw, q_off, k_off, v_off, scale, qk_dtype, name, rider=None,
               stacked=False):
    s = q.shape[0]
    t = min(ATT_T, s)
    nb = s // t
    qw = 2 * hw
    has_bias = bias is not None
    c1 = scale * LOG2E

    def body(*refs):
        n_grads = 1 if stacked else 3
        refs, ride_refs = _split_rider(refs, rider, n_in=7 if has_bias else 6, n_out=n_grads + (2 if has_bias else 0))
        if stacked:
            refs = list(refs)
            n_in = 7 if has_bias else 6
            refs[n_in:n_in + 1] = [refs[n_in].at[0], refs[n_in].at[1], refs[n_in].at[2]]
        if has_bias:
            (q_ref, k_ref, v_ref, do_ref, o_ref, lse_ref, b_ref, dq_ref, dk_ref, dv_ref, db_ref, dr_ref,
             dkt_ref, dvt_ref) = refs
            db_ref[...] = jnp.zeros_like(db_ref)
        else:
            q_ref, k_ref, v_ref, do_ref, o_ref, lse_ref, dq_ref, dk_ref, dv_ref, dkt_ref, dvt_ref = refs
            b_ref = db_ref = dr_ref = None
        _ride_start(rider, ride_refs, pl.program_id(0) == 0)
        dkt_ref[...] = jnp.zeros_like(dkt_ref)
        dvt_ref[...] = jnp.zeros_like(dvt_ref)
        causal = lax.broadcasted_iota(jnp.int32, (t, t), 1) <= lax.broadcasted_iota(jnp.int32, (t, t), 0)
        cmask = jnp.concatenate([causal, causal], axis=0)
        lane_lt_hw = lax.broadcasted_iota(jnp.int32, (t, qw), 1) < hw

        def q_block(i, _):
            r0 = pl.multiple_of(i * t, t)
            qs = _stack_heads(q_ref[pl.ds(r0, t), :], hw)
            dos = _stack_heads(do_ref[pl.ds(r0, t), :], 64)
            ot = o_ref[pl.ds(r0, t), :]
            delta = jnp.sum(dos * jnp.concatenate([ot, ot], axis=0), axis=-1, keepdims=True)
            lse2 = jnp.concatenate([lse_ref[0, 0, pl.ds(r0, t), :], lse_ref[0, 1, pl.ds(r0, t), :]], axis=0)
            dosb = dos.astype(BF16)
            dos_t = dos.T.astype(BF16)
            qs_t = qs.astype(F32).T.astype(BF16)

            def kv_step(j, carry, masked):
                dq, rsum = carry
                c0 = pl.multiple_of(j * t, t)
                kt = k_ref[pl.ds(c0, t), :]
                vt = v_ref[pl.ds(c0, t), :]
                sc = lax.dot_general(qs, kt, _NT, preferred_element_type=F32) * c1
                if has_bias:
                    sc = sc + _stacked_rows(b_ref[0, 0, j], b_ref[0, 1, j], t)
                if masked:
                    sc = jnp.where(cmask, sc, NEG_INF)
                p = jnp.exp2(sc - lse2)
                dp = lax.dot_general(dosb, vt, _NT, preferred_element_type=F32)
                ds = p * (dp - delta)
                dsb = ds.astype(BF16)
                pb = p.astype(BF16)
                if hw == LANES:
                    dvt_ref[j] += jnp.concatenate(
                        [jnp.dot(dos_t[:64, :t], pb[:t], preferred_element_type=F32),
                         jnp.dot(dos_t[64:, t:], pb[t:], preferred_element_type=F32)], axis=0)
                    dkt_ref[j] += jnp.concatenate(
                        [jnp.dot(qs_t[:hw, :t], dsb[:t], preferred_element_type=F32),
                         jnp.dot(qs_t[hw:, t:], dsb[t:], preferred_element_type=F32)], axis=0)
                else:
                    dvt_ref[j] += jnp.dot(dos_t, pb, preferred_element_type=F32)
                    dkt_ref[j] += jnp.dot(qs_t, dsb, preferred_element_type=F32)
                if has_bias:
                    db_ref[0, 0, j] += jnp.sum(ds[:t], axis=0, keepdims=True)
                    db_ref[0, 1, j] += jnp.sum(ds[t:], axis=0, keepdims=True)
                    rsum = rsum + jnp.sum(ds, axis=-1, keepdims=True)
                return dq + jnp.dot(dsb, kt, preferred_element_type=F32), rsum

            init = (jnp.zeros((2 * t, qw), F32), jnp.zeros((2 * t, 1), F32))
            carry = lax.fori_loop(0, i, functools.partial(kv_step, masked=False), init)
            dq, rsum = kv_step(i, carry, True)
            dq = dq * scale
            dq_ref[pl.ds(r0, t), :] = jnp.where(lane_lt_hw, dq[:t], dq[t:]).astype(qk_dtype)
            if has_bias:
                rsum_row = jnp.broadcast_to(rsum, (2 * t, LANES)).T[0:1]
                dr_ref[0, 0, i] = rsum_row[:, :t]
                dr_ref[0, 1, i] = rsum_row[:, t:]
            return 0

        lax.fori_loop(0, nb, q_block, 0)

        def k_block(j, _):
            c0 = pl.multiple_of(j * t, t)
            dk_ref[pl.ds(c0, t), :] = (dkt_ref[j].T * scale).astype(qk_dtype)
            dv_ref[pl.ds(c0, t), :] = dvt_ref[j].T.astype(BF16)
            return 0

        lax.fori_loop(0, nb, k_block, 0)
        _ride_wait(rider, ride_refs, pl.program_id(0) == n_pairs - 1)

    in_specs = [_resident((s, qw), lambda p: (0, q_off + p)), _resident((s, qw), lambda p: (0, k_off + p)),
                _resident((s, LANES), lambda p: (0, v_off + p)),
                _resident((s, LANES), lambda p: (0, p)), _resident((s, LANES), lambda p: (0, p)),
                _resident((1, 2, s, 1), lambda p: (p, 0, 0, 0))]
    args = [q, k, v, do, o, lse]
    if stacked:
        assert qw == LANES and qk_dtype == BF16
        out_specs = [pl.BlockSpec((3, s, LANES), lambda p: (0, 0, p))]
        out_shape = [jax.ShapeDtypeStruct((3, s, n_pairs * LANES), BF16)]
    else:
        out_specs = [pl.BlockSpec((s, qw), lambda p: (0, p)), pl.BlockSpec((s, qw), lambda p: (0, p)),
                     pl.BlockSpec((s, LANES), lambda p: (0, p))]
        out_shape = [jax.ShapeDtypeStruct((s, n_pairs * qw), qk_dtype), jax.ShapeDtypeStruct((s, n_pairs * qw), qk_dtype),
                     jax.ShapeDtypeStruct((s, n_pairs * LANES), BF16)]
    if has_bias:
        in_specs.append(_resident((1, 2, nb, 1, t), lambda p: (p, 0, 0, 0, 0)))
        args.append(bias)
        for _ in range(2):
            out_specs.append(pl.BlockSpec((1, 2, nb, 1, t), lambda p: (p, 0, 0, 0, 0)))
            out_shape.append(jax.ShapeDtypeStruct((n_pairs, 2, nb, 1, t), F32))
    scratch = [pltpu.VMEM((nb, qw, t), F32), pltpu.VMEM((nb, LANES, t), F32)]
    scratch += _add_rider(rider, in_specs, args, out_specs, out_shape)
    return pl.pallas_call(
        body, name=name, grid=(n_pairs,), in_specs=in_specs, out_specs=out_specs, out_shape=out_shape,
        scratch_shapes=scratch,
        compiler_params=_params(("parallel",) if rider is None else ("arbitrary",)),
    )(*args)


def _alibi_slope(h):
    return 2.0 ** (-8.0 * (h + 1.0) / SWA_HEADS)


SWA_ROWS = 512
SWA_SCALE = SWA_DIM ** -0.5


def _swa_geometry(i):
    w = WINDOW
    r0 = pl.multiple_of(i * w, w)
    b0 = pl.multiple_of(jnp.maximum(i - 1, 0) * w, w)
    row = lax.broadcasted_iota(jnp.int32, (w, 2 * w), 0)
    col = lax.broadcasted_iota(jnp.int32, (w, 2 * w), 1)
    dist = row - col + (r0 - b0)
    valid = (dist >= 0) & (dist < w)
    return r0, b0, dist.astype(F32), valid


def _swa_q_head(qblk, h):
    kv = h // (SWA_HEADS // SWA_KV_HEADS)
    if h % 2 != kv:
        qblk = pltpu.roll(qblk, 64, axis=1)
    return jnp.where(_head_mask(qblk.shape, kv), qblk, 0.0)


SWA_GROUP = SWA_HEADS // SWA_KV_HEADS


def _swa_stack(ref, rs, grp):
    parts = []
    for a in range(SWA_GROUP):
        h = SWA_GROUP * grp + a
        parts.append(_swa_q_head(ref[rs, (h // 2) * LANES:(h // 2 + 1) * LANES].astype(F32), h))
    return jnp.concatenate(parts, axis=0)


def _swa_unstack(x, grp):
    tiles = []
    for a in range(SWA_GROUP):
        h = SWA_GROUP * grp + a
        tile = x[a * WINDOW:(a + 1) * WINDOW]
        tiles.append(pltpu.roll(tile, 64, axis=1) if h % 2 != grp else tile)
    return tiles


def _swa_head_column(vals):
    return jnp.concatenate([jnp.full((WINDOW, 1), v, F32) for v in vals], axis=0)


def _swa_logits(qs, kb, dist, valid, grp):
    slopes = _swa_head_column([_alibi_slope(SWA_GROUP * grp + a) for a in range(SWA_GROUP)])
    dist4 = jnp.concatenate([dist] * SWA_GROUP, axis=0)
    valid4 = jnp.concatenate([valid] * SWA_GROUP, axis=0)
    sc = lax.dot_general(qs, kb, _NT, preferred_element_type=F32) * SWA_SCALE - slopes * dist4
    return jnp.where(valid4, sc, NEG_INF)


def _swa_merge_heads(tiles):
    lt64 = lax.broadcasted_iota(jnp.int32, (WINDOW, LANES), 1) < 64
    return jnp.concatenate([jnp.where(lt64, tiles[2 * b], tiles[2 * b + 1]) for b in range(SWA_HEADS // 2)], axis=1)


def _swa_fwd(z0b, sinks, *, name):
    s = z0b.shape[0]
    w = WINDOW
    rows = min(SWA_ROWS, s)
    per_step = rows // w
    qcols = SWA_HEADS * SWA_DIM

    def body(sink_ref, q_ref, k_ref, v_ref, o_ref, lse_ref):
        g = pl.program_id(0)
        for ii in range(per_step):
            rs = slice(ii * w, (ii + 1) * w)
            r0, b0, dist, valid = _swa_geometry(g * per_step + ii)
            kb = k_ref[pl.ds(b0, 2 * w), :]
            vb = v_ref[pl.ds(b0, 2 * w), :]
            o_tiles = []
            for h in range(SWA_HEADS):
                kv = h // SWA_GROUP
                qh = _swa_q_head(q_ref[rs, (h // 2) * LANES:(h // 2 + 1) * LANES].astype(F32), h).astype(BF16)
                sc = lax.dot_general(qh, kb, _NT, preferred_element_type=F32) * SWA_SCALE - _alibi_slope(h) * dist
                sc = jnp.where(valid, sc, NEG_INF)
                sink = sink_ref[0, h]
                m = jnp.maximum(jnp.max(sc, axis=-1, keepdims=True), sink)
                p = jnp.exp(sc - m)
                l = jnp.sum(p, axis=-1, keepdims=True) + jnp.exp(sink - m)
                oh = jnp.dot(p.astype(BF16), vb, preferred_element_type=F32) / l
                o_tiles.append(pltpu.roll(oh, 64, axis=1) if h % 2 != kv else oh)
                lse_ref[h, rs, :] = m + jnp.log(l)
            o_ref[rs, :] = _swa_merge_heads(o_tiles)

    return pl.pallas_call(
        body, name=name, grid=(s // rows,),
        in_specs=[pl.BlockSpec(memory_space=pltpu.SMEM),
                  pl.BlockSpec((rows, qcols), lambda g: (g, 0)),
                  pl.BlockSpec((s, LANES), lambda g: (0, 4)), pl.BlockSpec((s, LANES), lambda g: (0, 5))],
        out_specs=[pl.BlockSpec((rows, qcols), lambda g: (g, 0)), pl.BlockSpec((SWA_HEADS, rows, 1), lambda g: (0, g, 0))],
        out_shape=[jax.ShapeDtypeStruct((s, qcols), F32), jax.ShapeDtypeStruct((SWA_HEADS, s, 1), F32)],
        compiler_params=_params(("parallel",)),
    )(sinks, z0b, z0b, z0b)


def _swa_bwd(z0b, sinks, do, o, lse, *, name):
    s = z0b.shape[0]
    w = WINDOW
    rows = min(SWA_ROWS, s)
    per_step = rows // w
    qcols = SWA_HEADS * SWA_DIM
    nblk = s // w

    def body(sink_ref, q_ref, k_ref, v_ref, do_ref, o_ref, lse_ref, dq_ref, dkt_ref, dvt_ref, dsink_ref):
        g = pl.program_id(0)

        @pl.when(g == 0)
        def _():
            dkt_ref[...] = jnp.zeros_like(dkt_ref)
            dvt_ref[...] = jnp.zeros_like(dvt_ref)
            dsink_ref[...] = jnp.zeros_like(dsink_ref)

        for ii in range(per_step):
            i = g * per_step + ii
            rs = slice(ii * w, (ii + 1) * w)
            r0, b0, dist, valid = _swa_geometry(i)
            j0 = jnp.maximum(i - 1, 0)
            kb = k_ref[pl.ds(b0, 2 * w), :]
            vb = v_ref[pl.ds(b0, 2 * w), :]
            dq_tiles = []
            for grp in range(SWA_KV_HEADS):
                heads = [SWA_GROUP * grp + a for a in range(SWA_GROUP)]
                qs32 = _swa_stack(q_ref, rs, grp)
                dos32 = _swa_stack(do_ref, rs, grp)
                delta = jnp.sum(dos32 * _swa_stack(o_ref, rs, grp), axis=-1, keepdims=True)
                lse = jnp.concatenate([lse_ref[h, rs, :] for h in heads], axis=0)
                sink = _swa_head_column([sink_ref[0, h] for h in heads])
                p = jnp.exp(_swa_logits(qs32.astype(BF16), kb, dist, valid, grp) - lse)
                dp = lax.dot_general(dos32.astype(BF16), vb, _NT, preferred_element_type=F32)
                ds = p * (dp - delta)
                dsb = ds.astype(BF16)
                d_sink = jnp.exp(sink - lse) * delta
                for a, h in enumerate(heads):
                    dsink_ref[h:h + 1, :] += jnp.broadcast_to(-jnp.sum(d_sink[a * w:(a + 1) * w]), (1, LANES))
                dvt = jnp.dot(dos32.T.astype(BF16), p.astype(BF16), preferred_element_type=F32)
                dkt = jnp.dot(qs32.T.astype(BF16), dsb, preferred_element_type=F32) * SWA_SCALE
                dvt_ref[j0] += dvt[:, :w]
                dvt_ref[j0 + 1] += dvt[:, w:]
                dkt_ref[j0] += dkt[:, :w]
                dkt_ref[j0 + 1] += dkt[:, w:]
                dq_tiles += _swa_unstack(jnp.dot(dsb, kb, preferred_element_type=F32) * SWA_SCALE, grp)
            dq_ref[rs, :] = _swa_merge_heads(dq_tiles)

    return pl.pallas_call(
        body, name=name, grid=(s // rows,),
        in_specs=[pl.BlockSpec(memory_space=pltpu.SMEM),
                  pl.BlockSpec((rows, qcols), lambda g: (g, 0)),
                  pl.BlockSpec((s, LANES), lambda g: (0, 4)), pl.BlockSpec((s, LANES), lambda g: (0, 5)),
                  pl.BlockSpec((rows, qcols), lambda g: (g, 0)), pl.BlockSpec((rows, qcols), lambda g: (g, 0)),
                  pl.BlockSpec((SWA_HEADS, rows, 1), lambda g: (0, g, 0))],
        out_specs=[pl.BlockSpec((rows, qcols), lambda g: (g, 0)),
                   pl.BlockSpec((nblk, LANES, w), lambda g: (0, 0, 0)),
                   pl.BlockSpec((nblk, LANES, w), lambda g: (0, 0, 0)),
                   pl.BlockSpec((SWA_HEADS, LANES), lambda g: (0, 0))],
        out_shape=[jax.ShapeDtypeStruct((s, qcols), F32),
                   jax.ShapeDtypeStruct((nblk, LANES, w), F32), jax.ShapeDtypeStruct((nblk, LANES, w), F32),
                   jax.ShapeDtypeStruct((SWA_HEADS, LANES), F32)],
        compiler_params=_params(("arbitrary",)),
    )(sinks, z0b, z0b, z0b, do, o, lse)


CUM_T = 256


def _split3(x):
    hi = x.astype(BF16)
    r1 = x - hi.astype(F32)
    mid = r1.astype(BF16)
    lo = (r1 - mid.astype(F32)).astype(BF16)
    return hi, mid, lo


def _tri_dot(tri, x):
    hi, mid, lo = _split3(x)
    out = jnp.dot(tri, hi, preferred_element_type=F32)
    out = out + jnp.dot(tri, mid, preferred_element_type=F32)
    return out + jnp.dot(tri, lo, preferred_element_type=F32)


def _logf_fwd(zf, bf, *, name):
    s = zf.shape[0]
    t = CUM_T
    nb = s // t

    def body(z_ref, b_ref, c_ref, carry_ref):
        i = pl.program_id(0)

        @pl.when(i == 0)
        def _():
            carry_ref[...] = jnp.zeros_like(carry_ref)

        x = z_ref[...] + b_ref[...]
        lf = jnp.minimum(x, 0.0) - jnp.log(1.0 + jnp.exp(-jnp.abs(x)))
        row = lax.broadcasted_iota(jnp.int32, (t, t), 0)
        col = lax.broadcasted_iota(jnp.int32, (t, t), 1)
        tri = jnp.where(col <= row, 1.0, 0.0).astype(BF16)
        c = _tri_dot(tri, lf) + carry_ref[...]
        c_ref[...] = c
        carry_ref[...] = c[t - 1:t, :]

    return pl.pallas_call(
        body, name=name, grid=(nb,),
        in_specs=[pl.BlockSpec((t, LANES), lambda i: (i, 0)), pl.BlockSpec((1, LANES), lambda i: (0, 0))],
        out_specs=pl.BlockSpec((t, LANES), lambda i: (i, 0)),
        out_shape=jax.ShapeDtypeStruct((s, LANES), F32),
        scratch_shapes=[pltpu.VMEM((1, LANES), F32)],
        compiler_params=_params(("arbitrary",)),
    )(zf, bf)


def _logf_bwd(dc, zf, bf, *, name):
    s = zf.shape[0]
    t = CUM_T
    nb = s // t

    def body(dc_ref, z_ref, b_ref, dz_ref, db_ref, carry_ref):
        i = pl.program_id(0)

        @pl.when(i == 0)
        def _():
            carry_ref[...] = jnp.zeros_like(carry_ref)
            db_ref[...] = jnp.zeros_like(db_ref)

        row = lax.broadcasted_iota(jnp.int32, (t, t), 0)
        col = lax.broadcasted_iota(jnp.int32, (t, t), 1)
        tri = jnp.where(col >= row, 1.0, 0.0).astype(BF16)
        dlf = _tri_dot(tri, dc_ref[...]) + carry_ref[...]
        carry_ref[...] = dlf[0:1, :]
        x = z_ref[...] + b_ref[...]
        dz = dlf * _sigmoid(-x)
        dz_ref[...] = dz.astype(BF16)
        db_ref[...] += jnp.sum(dz, axis=0, keepdims=True)

    return pl.pallas_call(
        body, name=name, grid=(nb,),
        in_specs=[pl.BlockSpec((t, LANES), lambda i: (nb - 1 - i, 0)), pl.BlockSpec((t, LANES), lambda i: (nb - 1 - i, 0)),
                  pl.BlockSpec((1, LANES), lambda i: (0, 0))],
        out_specs=[pl.BlockSpec((t, LANES), lambda i: (nb - 1 - i, 0)), pl.BlockSpec((1, LANES), lambda i: (0, 0))],
        out_shape=[jax.ShapeDtypeStruct((s, LANES), BF16), jax.ShapeDtypeStruct((1, LANES), F32)],
        scratch_shapes=[pltpu.VMEM((1, LANES), F32)],
        compiler_params=_params(("arbitrary",)),
    )(dc, zf, bf)


def _sum_pieces(p_ref):
    g = p_ref[0].astype(F32)
    for k in range(1, N_DEV):
        g = g + p_ref[k].astype(F32)
    return g


def _adam_update(g, w, m, v):
    bc1 = 1.0 - ADAM_B1 ** ADAM_STEP
    bc2 = 1.0 - ADAM_B2 ** ADAM_STEP
    nm = ADAM_B1 * m + (1.0 - ADAM_B1) * g
    nv = ADAM_B2 * v + (1.0 - ADAM_B2) * (g * g)
    m_hat = nm / bc1
    v_hat = nv / bc2
    return -ADAM_LR * (m_hat / (jnp.sqrt(v_hat) + ADAM_EPS) + ADAM_WD * w), nm, nv


def _adamw(pieces, w, m, v, *, name):
    rows, cols = w.shape
    tr = _tile(rows, (RB1, RB0, SMALL_ROWS))

    def body(p_ref, w_ref, m_ref, v_ref, g_ref, d_ref, nm_ref, nv_ref):
        g = _sum_pieces(p_ref)
        g_ref[...] = g
        d_ref[...], nm_ref[...], nv_ref[...] = _adam_update(g, w_ref[...], m_ref[...], v_ref[...])

    spec = pl.BlockSpec((tr, cols), lambda i: (i, 0))
    shape = jax.ShapeDtypeStruct((rows, cols), F32)
    return pl.pallas_call(
        body, name=name, grid=(rows // tr,),
        in_specs=[pl.BlockSpec((N_DEV, tr, cols), lambda i: (0, i, 0)), spec, spec, spec],
        out_specs=[spec, spec, spec, spec], out_shape=[shape, shape, shape, shape],
        compiler_params=_params(("parallel",)),
    )(pieces, w, m, v)


def _sum8(pieces, rows, *, name):
    cols = pieces.shape[2]
    tr = _tile(rows, (176, 96))

    def body(p_ref, g_ref):
        g_ref[...] = _sum_pieces(p_ref)

    return pl.pallas_call(
        body, name=name, grid=(rows // tr,),
        in_specs=[pl.BlockSpec((N_DEV, tr, cols), lambda i: (0, i, 0))],
        out_specs=pl.BlockSpec((tr, cols), lambda i: (i, 0)),
        out_shape=jax.ShapeDtypeStruct((rows, cols), F32),
        compiler_params=_params(("parallel",)),
    )(pieces)


def _adamw_columns(g, w, m, v, *, name):
    n, _, k = w.shape
    tr = n // 2

    def body(g_ref, w_ref, m_ref, v_ref, d_ref, nm_ref, nv_ref):
        d_ref[...], nm_ref[...], nv_ref[...] = _adam_update(g_ref[...], w_ref[...], m_ref[...], v_ref[...])

    spec = pl.BlockSpec((tr, 1, k), lambda i: (i, 0, 0))
    shape = jax.ShapeDtypeStruct((n, 1, k), F32)
    return pl.pallas_call(
        body, name=name, grid=(n // tr,), in_specs=[spec, spec, spec, spec],
        out_specs=[spec, spec, spec], out_shape=[shape, shape, shape],
        compiler_params=_params(("parallel",)),
    )(g, w, m, v)


MESH = pl.DeviceIdType.MESH
ANY = pl.BlockSpec(memory_space=pl.ANY)


def _all_gather(shard, *, name):
    rows, lanes = shard.shape

    def body(x_ref, out_ref, send_sems, recv_sems, local_sem):
        x, y, c = lax.axis_index("x"), lax.axis_index("y"), lax.axis_index("c")
        me, sibling = (x, y, c), (x, y, 1 - c)
        chips = [(1 - x, y), (x, 1 - y), (1 - x, 1 - y)]

        def block(px, py, pc):
            return out_ref.at[4 * px + 2 * py + pc]

        def copy(k, blk, to, src=None):
            return pltpu.make_async_remote_copy(
                src_ref=block(*blk) if src is None else src, dst_ref=block(*blk),
                send_sem=send_sems.at[k], recv_sem=recv_sems.at[k], device_id=to, device_id_type=MESH)

        mine = pltpu.make_async_copy(x_ref, block(*me), local_sem)
        mine.start()
        first = [copy(0, me, sibling, src=x_ref)]
        first += [copy(1 + j, me, (*chip, c), src=x_ref) for j, chip in enumerate(chips)]
        for cp in first:
            cp.start()
        passed = [copy(4 + j, (*chip, c), sibling) for j, chip in enumerate(chips)]
        for j, chip in enumerate(chips):
            copy(1 + j, (*chip, c), me).wait_recv()
            passed[j].start()
        copy(0, sibling, me).wait_recv()
        for j, chip in enumerate(chips):
            copy(4 + j, (*chip, 1 - c), me).wait_recv()
        for cp in first + passed:
            cp.wait_send()
        mine.wait()

    return pl.pallas_call(
        body, name=name, out_shape=jax.ShapeDtypeStruct((N_DEV, rows, lanes), shard.dtype),
        in_specs=[ANY], out_specs=ANY,
        scratch_shapes=[pltpu.SemaphoreType.DMA((7,)), pltpu.SemaphoreType.DMA((7,)), pltpu.SemaphoreType.DMA(())],
    )(shard)


def _peer_copies(kind, src_ref, out_ref, send_sems, recv_sems, local_sem):
    x, y, c = lax.axis_index("x"), lax.axis_index("y"), lax.axis_index("c")
    me = 4 * x + 2 * y + c

    def src(idx):
        return src_ref.at[idx] if kind == "exchange" else src_ref

    mine = None if local_sem is None else pltpu.make_async_copy(src(me), out_ref.at[me], local_sem)
    copies = []
    for r in (2, 4, 6) if kind == "across" else range(1, N_DEV):
        px = 1 - x if r & 4 else x
        py = 1 - y if r & 2 else y
        pc = 1 - c if r & 1 else c
        copies.append(pltpu.make_async_remote_copy(
            src_ref=src(4 * px + 2 * py + pc), dst_ref=out_ref.at[me],
            send_sem=send_sems.at[r - 1], recv_sem=recv_sems.at[r - 1],
            device_id=(px, py, pc), device_id_type=MESH))
    return mine, copies


def _to_other_core(shard, land, *, name):
    def body(src_ref, land_ref, out_ref, send_sems, recv_sems):
        x, y, c = lax.axis_index("x"), lax.axis_index("y"), lax.axis_index("c")
        copies = []
        for k, r in enumerate((0, 2, 4, 6)):
            slot = 4 * (1 - x if r & 4 else x) + 2 * (1 - y if r & 2 else y) + c
            copies.append(pltpu.make_async_remote_copy(
                src_ref=src_ref if r == 0 else land_ref.at[slot], dst_ref=out_ref.at[slot],
                send_sem=send_sems.at[k], recv_sem=recv_sems.at[k], device_id=(x, y, 1 - c), device_id_type=MESH))
        for cp in copies:
            cp.start()
        for cp in copies:
            cp.wait()

    return pl.pallas_call(
        body, name=name, out_shape=jax.ShapeDtypeStruct(land.shape, land.dtype), in_specs=[ANY, ANY], out_specs=ANY,
        input_output_aliases={1: 0}, scratch_shapes=[pltpu.SemaphoreType.DMA((4,)), pltpu.SemaphoreType.DMA((4,))],
    )(shard, land)


PEER_SEMS = [pltpu.SemaphoreType.DMA((7,)), pltpu.SemaphoreType.DMA((7,)), pltpu.SemaphoreType.DMA(())]


HBM = pl.BlockSpec(memory_space=pltpu.HBM)
SEMAPHORES = pl.BlockSpec(memory_space=pltpu.SEMAPHORE)


def _peer_start(kind, arr, *, name):
    land = lax.empty((N_DEV,) + arr.shape[-2:], arr.dtype)

    def body(src_ref, land_ref, send_sems, recv_sems, src_thru, land_thru, token):
        _, copies = _peer_copies(kind, src_ref, land_ref, send_sems, recv_sems, None)
        for cp in copies:
            cp.start()
        token[...] = jnp.zeros_like(token)

    return pl.pallas_call(
        body, name=name,
        out_shape=(pltpu.SemaphoreType.DMA((N_DEV - 1,)), pltpu.SemaphoreType.DMA((N_DEV - 1,)),
                   pltpu.HBM(arr.shape, arr.dtype), pltpu.HBM(land.shape, land.dtype), jax.ShapeDtypeStruct((8, LANES), F32)),
        in_specs=(HBM, HBM), out_specs=(SEMAPHORES, SEMAPHORES, HBM, HBM, pl.BlockSpec(memory_space=pltpu.VMEM)),
        input_output_aliases={0: 2, 1: 3},
        compiler_params=pltpu.CompilerParams(has_side_effects=pltpu.SideEffectType.DATAFLOW_SIDE_EFFECTING),
    )(pltpu.with_memory_space_constraint(arr, pltpu.HBM), pltpu.with_memory_space_constraint(land, pltpu.HBM))


def _peer_wait(kind, send_sems, recv_sems, src_thru, land_thru, after, *, name):
    def body(src_ref, land_ref, send_sems, recv_sems, *_):
        _, copies = _peer_copies(kind, src_ref, land_ref, send_sems, recv_sems, None)
        for cp in copies:
            cp.wait_send()
            cp.wait_recv()

    return pl.pallas_call(
        body, name=name,
        out_shape=(pltpu.HBM(src_thru.shape, src_thru.dtype), pltpu.HBM(land_thru.shape, land_thru.dtype)),
        in_specs=(HBM, HBM, SEMAPHORES, SEMAPHORES) + (ANY,) * len(after), out_specs=(HBM, HBM),
        input_output_aliases={0: 0, 1: 1},
        compiler_params=pltpu.CompilerParams(has_side_effects=pltpu.SideEffectType.DATAFLOW_SIDE_EFFECTING),
    )(src_thru, land_thru, send_sems, recv_sems, *after)


def _add_rider(rider, in_specs, args, out_specs, out_shape):
    if rider is None:
        return []
    _, arr = rider
    in_specs.append(ANY)
    args.append(arr)
    out_specs.append(ANY)
    out_shape.append(jax.ShapeDtypeStruct((N_DEV,) + arr.shape[-2:], arr.dtype))
    return list(PEER_SEMS)


def _split_rider(refs, rider, n_in, n_out):
    if rider is None:
        return refs, None
    refs = list(refs)
    rin = refs.pop(n_in)
    rout = refs.pop(n_in + n_out)
    return refs[:-3], (rin, rout, *refs[-3:])


def _ride_start(rider, ride_refs, first):
    if rider is None:
        return

    @pl.when(first)
    def _():
        mine, copies = _peer_copies(rider[0], *ride_refs)
        mine.start()
        for cp in copies:
            cp.start()


def _ride_wait(rider, ride_refs, last):
    if rider is None:
        return

    @pl.when(last)
    def _():
        mine, copies = _peer_copies(rider[0], *ride_refs)
        for cp in copies:
            cp.wait()
        mine.wait()


def _gathered_cols(blocks, kdim):
    n = blocks.shape[1] * WIDE // kdim
    return blocks.reshape(N_DEV, kdim, n).transpose(1, 0, 2).reshape(kdim, N_DEV * n)


def _scatter_cols(dw):
    kdim, n8 = dw.shape
    n = n8 // N_DEV
    return dw.reshape(kdim, N_DEV, n).transpose(1, 0, 2).reshape(N_DEV, kdim * n // WIDE, WIDE)


def _pad_rows(a, rows):
    pad = [(0, 0)] * a.ndim
    pad[-2] = (0, rows - a.shape[-2])
    return jnp.pad(a, pad)


def _layer0_in_weight_t(wt):
    cq, ckv, kpe = wt[0:256], wt[256:384], wt[384:416]
    q_s, k_s, v_s, gate = wt[416:928], wt[928:1056], wt[1056:1184], wt[1184:2208]
    z = jnp.zeros((64, wt.shape[1]), wt.dtype)
    return jnp.concatenate([gate, cq, ckv, z, kpe, z[:32], q_s, k_s, v_s], axis=0)


def _layer0_in_grad_t(dwt):
    gate, cq, ckv, kpe = dwt[0:1024], dwt[1024:1280], dwt[1280:1408], dwt[1472:1504]
    q_s, k_s, v_s = dwt[1536:2048], dwt[2048:2176], dwt[2176:2304]
    return jnp.concatenate([cq, ckv, kpe, q_s, k_s, v_s, gate], axis=0)


L0_BLOCKS = ((256, 1024), (128, 1280), (32, 1472), (512, 1536), (128, 2048), (128, 2176), (1024, 0))


def _layer0_in_unpack(gath, *, name):
    total = (Z0A_UNITS + Z0B_UNITS) * LANES

    def body(g_ref, w_ref):
        w_ref[1408:1472, :] = jnp.zeros((64, WIDE), w_ref.dtype)
        w_ref[1504:1536, :] = jnp.zeros((32, WIDE), w_ref.dtype)
        for p in range(N_DEV):
            lo, hi, at = p * N_E_IN, (p + 1) * N_E_IN, 0
            for rows, first in L0_BLOCKS:
                start, stop = max(lo, at), min(hi, at + rows)
                if start < stop:
                    w_ref[first + start - at:first + stop - at, :] = g_ref[p, start - lo:stop - lo, :]
                at += rows

    return pl.pallas_call(
        body, name=name, grid=(1,), in_specs=[_resident((N_DEV, RA0, WIDE), lambda i: (0, 0, 0))],
        out_specs=_resident((total, WIDE), lambda i: (0, 0)), out_shape=jax.ShapeDtypeStruct((total, WIDE), gath.dtype),
        compiler_params=_params(("arbitrary",)),
    )(gath)


def _early_grads_pack(d_w0t, d_q, d_kv, *, name):
    def body(w_ref, q_ref, kv_ref, out_ref):
        for p in range(N_DEV):
            lo, hi, at = p * N_E_IN, (p + 1) * N_E_IN, 0
            for rows, first in L0_BLOCKS:
                start, stop = max(lo, at), min(hi, at + rows)
                if start < stop:
                    out_ref[p, start - lo:stop - lo, :] = w_ref[first + start - at:first + stop - at, :]
                at += rows
            out_ref[p, N_E_IN:RA0, :] = jnp.zeros((RA0 - N_E_IN, WIDE), out_ref.dtype)
            out_ref[p, RA0:RA0 + 32, :] = q_ref[p]
            out_ref[p, RA0 + 32:, :] = kv_ref[p]

    arrays = (d_w0t, d_q, d_kv)
    return pl.pallas_call(
        body, name=name, grid=(1,), in_specs=[_resident(a.shape, lambda i, n=a.ndim: (0,) * n) for a in arrays],
        out_specs=_resident((N_DEV, RA0 + RB0, WIDE), lambda i: (0, 0, 0)),
        out_shape=jax.ShapeDtypeStruct((N_DEV, RA0 + RB0, WIDE), BF16), compiler_params=_params(("arbitrary",)),
    )(*arrays)


def _layer1_in_weight_t(wt):
    main = jnp.concatenate([wt[:3 * D_MODEL], wt[3 * D_MODEL + FOX_HEADS:]], axis=0)
    return main, _pad_rows(wt[3 * D_MODEL:3 * D_MODEL + FOX_HEADS], LANES)


def _layer1_in_unpack(gath, *, name):
    n_main = 3 * D_MODEL

    def body(g_ref, w_ref, f_ref, o1_ref, o0_ref):
        f_ref[...] = jnp.zeros_like(f_ref)
        for p in range(N_DEV):
            o1_ref[128 * p:128 * p + 128, :] = g_ref[p, RA1:RA1 + 128, :]
            o0_ref[128 * p:128 * p + 128, :] = g_ref[p, RA1 + 128:RA1 + 256, :]
            lo, hi = p * N_O_IN, (p + 1) * N_O_IN
            for ref, first, start, stop in ((w_ref, 0, lo, min(hi, n_main)),
                                            (f_ref, -n_main, max(lo, n_main), min(hi, n_main + FOX_HEADS)),
                                            (w_ref, -FOX_HEADS, max(lo, n_main + FOX_HEADS), hi)):
                if start < stop:
                    ref[start + first:stop + first, :] = g_ref[p, start - lo:stop - lo, :]

    return pl.pallas_call(
        body, name=name, grid=(1,), in_specs=[_resident((N_DEV, RA1 + 256, WIDE), lambda i: (0, 0, 0))],
        out_specs=[_resident((rows, WIDE), lambda i: (0, 0)) for rows in (n_main + D_MODEL, LANES, D_MODEL, D_MODEL)],
        out_shape=[jax.ShapeDtypeStruct((rows, WIDE), gath.dtype) for rows in (n_main + D_MODEL, LANES, D_MODEL, D_MODEL)],
        compiler_params=_params(("arbitrary",)),
    )(gath)


def _late_grads_pack(d_qkv, d_wft, d_gate, d_wo1, d_wo0, d_o_g, *, name):
    n_main = 3 * D_MODEL
    arrays = (d_qkv, d_wft, d_gate, d_wo1, d_wo0, d_o_g)

    def body(q_ref, f_ref, g_ref, o1_ref, o0_ref, og_ref, out_ref):
        for p in range(N_DEV):
            lo, hi = p * N_O_IN, (p + 1) * N_O_IN
            for ref, first, start, stop in ((q_ref, 0, lo, min(hi, n_main)),
                                            (f_ref, -n_main, max(lo, n_main), min(hi, n_main + FOX_HEADS)),
                                            (g_ref, -n_main - FOX_HEADS, max(lo, n_main + FOX_HEADS), hi)):
                if start < stop:
                    out_ref[p, start - lo:stop - lo, :] = ref[start + first:stop + first, :]
            out_ref[p, N_O_IN:RA1, :] = jnp.zeros((RA1 - N_O_IN, WIDE), out_ref.dtype)
            out_ref[p, RA1:RA1 + 128, :] = o1_ref[128 * p:128 * p + 128, :]
            out_ref[p, RA1 + 128:RA1 + 256, :] = o0_ref[128 * p:128 * p + 128, :]
            out_ref[p, RA1 + 256:, :] = og_ref[p]

    return pl.pallas_call(
        body, name=name, grid=(1,), in_specs=[_resident(a.shape, lambda i, n=a.ndim: (0,) * n) for a in arrays],
        out_specs=_resident((N_DEV, RA1 + RB1, WIDE), lambda i: (0, 0, 0)),
        out_shape=jax.ShapeDtypeStruct((N_DEV, RA1 + RB1, WIDE), BF16), compiler_params=_params(("arbitrary",)),
    )(*arrays)


def _q_up_weight(w):
    return jnp.pad(w.reshape(MLA_Q_RANK, MLA_HEADS, 96), ((0, 0), (0, 0), (0, 32))).reshape(MLA_Q_RANK, MLA_HEADS * LANES)


def _q_up_grad(dwp):
    return dwp.reshape(MLA_Q_RANK, MLA_HEADS, LANES)[:, :, :96].reshape(MLA_Q_RANK, MLA_HEADS * 96)


def _kv_up_weight(w):
    w4 = w.reshape(MLA_KV_RANK, MLA_HEADS, 2, 64)
    kp = jnp.pad(w4[:, :, 0, :], ((0, 0), (0, 0), (0, 64))).reshape(MLA_KV_RANK, MLA_HEADS * LANES)
    vp = w4[:, :, 1, :].reshape(MLA_KV_RANK, MLA_HEADS * 64)
    return jnp.concatenate([kp, vp], axis=1)


def _kv_up_grad(dwp):
    dk = dwp[:, :MLA_HEADS * LANES].reshape(MLA_KV_RANK, MLA_HEADS, LANES)[:, :, :64]
    dv = dwp[:, MLA_HEADS * LANES:].reshape(MLA_KV_RANK, MLA_HEADS, 64)
    return jnp.stack([dk, dv], axis=2).reshape(MLA_KV_RANK, MLA_HEADS * LANES)


def _pad_lanes(a):
    return jnp.pad(a, ((0, 0), (0, LANES - a.shape[1])))


def _small_pack(g_in, g_final, g_q_a, g_kv_a, sinks, b_f, loss):
    rows = [g_in.reshape(8, LANES), g_final.reshape(8, LANES), g_q_a.reshape(2, LANES), g_kv_a.reshape(1, LANES),
            _pad_lanes(sinks.reshape(1, -1)), _pad_lanes(b_f.reshape(1, -1)), _pad_lanes(loss.reshape(1, 1)),
            jnp.zeros((2, LANES), F32)]
    return jnp.concatenate(rows, axis=0)


def _small_unpack(a):
    return (a[0:8].reshape(1, D_MODEL), a[8:16].reshape(D_MODEL), a[16:18].reshape(1, MLA_Q_RANK),
            a[18:19].reshape(1, MLA_KV_RANK), a[19:20, :SWA_HEADS], a[20:21, :FOX_HEADS], a[21, 0])


def _local_step(x, positions, target, e_g_in, early, e_g_q_a, e_g_kv_a, e_sinks,
                late, o_b_f, g_final, scatter1=None, scatter0=None):
    s = x.shape[0]
    mla_scale = (MLA_NOPE + MLA_ROPE) ** -0.5
    fox_scale = FOX_DIM ** -0.5
    n0a = Z0A_UNITS * LANES

    inv_freq = 1.0 / (ROPE_THETA ** (jnp.arange(0, MLA_ROPE, 2, dtype=F32) / MLA_ROPE))
    ang = positions.astype(F32)[:, None] * inv_freq
    cos, sin = jnp.cos(ang), jnp.sin(ang)
    ones, zeros = jnp.ones((s, 64), F32), jnp.zeros((s, 64), F32)
    cos_t = jnp.concatenate([ones, cos, cos, ones[:, :32]], axis=1)
    sin_t = jnp.concatenate([zeros, -sin, sin, zeros[:, :32]], axis=1)
    cos_t, sin_t = lax.optimization_barrier((cos_t, sin_t))

    if len(early) == 3:
        h0 = _rmsnorm_fwd(x, e_g_in, width=D_MODEL, col_blk=0, name="l0_norm")
        w0t, wq, wkv = early
    else:
        pending, token, unpack, prep = early
        h0 = _rmsnorm_fwd(x, e_g_in, width=D_MODEL, col_blk=0, name="l0_norm", after=[token])
        sent, across = _peer_wait("across", *pending, after=[h0] + prep, name="weights0_wait")
        w0t, wq, wkv = unpack(sent, _to_other_core(sent, across, name="weights0_over"))
    z0a, z0b = _matmul_rows([(h0, w0t, True)], [], [], lambda r: (r[:, :n0a], r[:, n0a:]),
                            [("rows", n0a, F32), ("rows", Z0B_UNITS * LANES, BF16)], name="l0_in")
    cqn = _rmsnorm_fwd(z0a, e_g_q_a, width=MLA_Q_RANK, col_blk=4, name="l0_q_norm")
    ckvn = _rmsnorm_fwd(z0a, e_g_kv_a, width=MLA_KV_RANK, col_blk=10, name="l0_kv_norm")
    rope_rows = [(cos_t, LANES, 0), (sin_t, LANES, 0)]
    qm, = _matmul_rows([(cqn, wq, False)], rope_rows, [], _rope_q_epilogue, [("rows", MLA_HEADS * LANES, BF16)],
                       name="l0_q_up")
    kvm, = _matmul_rows([(ckvn, wkv, False)], [(z0a, LANES, 11)] + rope_rows, [], _rope_k_epilogue,
                        [("rows", MLA_HEADS * (LANES + MLA_V), BF16)], name="l0_kv_up")
    gathers = len(late) == 2
    res = _flash_fwd(qm, kvm, kvm, None, n_pairs=MLA_HEADS // 2, hw=LANES, q_off=0, k_off=0, v_off=MLA_HEADS,
                     scale=mla_scale, name="l0_mla_fwd", rider=("gather", late[0]) if gathers else None)
    o_mla, lse_mla = res[0], res[1]
    wo0, o_g_in, w1t, wft, wo1 = late[1](res[2]) if gathers else late
    o_swa, lse_swa = _swa_fwd(z0b, e_sinks, name="l0_swa_fwd")
    half = D_MODEL // 2

    x1, h1, og0 = _matmul_rows(
        [(None, wo0, False)], [(o_mla, half, 0), (o_swa, half, 0), (z0a, D_MODEL, 0), (x, D_MODEL, 0)], [o_g_in],
        lambda r, om, osw, gt, xt, g, made: (*_residual_norm_epilogue(r, xt, g), made),
        [("rows", D_MODEL, F32), ("rows", D_MODEL, BF16), ("rows", D_MODEL, BF16)], name="l0_out",
        prologue=lambda om, osw, gt, xt, g: _gated([om, osw], gt))
    z1, gate1, zf = _in_projection_ring(h1, w1t, wft, name="l1_in")
    bf = _pad_lanes(o_b_f)
    log_cum = _logf_fwd(zf, bf, name="l1_logf")
    bias2 = (-LOG2E * log_cum[:, :FOX_HEADS]).T
    t_bwd = min(ATT_T, s)
    bias = bias2.reshape(FOX_HEADS // 2, 2, s // t_bwd, 1, t_bwd)
    t_fwd = _fwd_tile(s)
    o_fox, lse_fox = _flash_fwd(z1, z1, z1, bias2.reshape(FOX_HEADS // 2, 2, s // t_fwd, 1, t_fwd),
                                n_pairs=FOX_HEADS // 2, hw=64, q_off=0, k_off=8, v_off=16, scale=fox_scale,
                                name="l1_fox_fwd")

    dx2, loss_part, d_g_final, og1, dx2_bf = _matmul_rows(
        [(None, wo1, False)], [(o_fox, D_MODEL, 0), (gate1, D_MODEL, 0), (x1, D_MODEL, 0), (target, D_MODEL, 0)],
        [g_final.reshape(1, D_MODEL)],
        lambda r, o, gt, xt, tg, g, made: _and_first(_loss_epilogue(r, xt, tg, g), made),
        [("rows", D_MODEL, F32), ("sum", (8, LANES)), ("sum", (1, D_MODEL)), ("rows", D_MODEL, BF16),
         ("rows", D_MODEL, BF16)], name="l1_out_loss", prologue=lambda o, gt, xt, tg, g: _gated([o], gt))

    d_wo1 = _matmul(og1, dx2_bf, ta=True, out_dtype=BF16, name="l1_out_dw")
    do_fox, d_gate1 = _matmul_rows([(dx2_bf, wo1, True)], [(o_fox, D_MODEL, 0), (gate1, D_MODEL, 0)], [],
                                   _gate_bwd_epilogue([D_MODEL]), [("rows", D_MODEL, F32), ("rows", D_MODEL, BF16)],
                                   name="l1_out_dx")
    dqkv1, dbias, drow = _flash_bwd(z1, z1, z1, do_fox, o_fox, lse_fox, bias, n_pairs=FOX_HEADS // 2, hw=64, q_off=0,
                                    k_off=8, v_off=16, scale=fox_scale, qk_dtype=BF16, stacked=True, name="l1_fox_bwd")
    d_log_cum = (drow.reshape(FOX_HEADS, s) - dbias.reshape(FOX_HEADS, s)).T
    d_log_cum = jnp.pad(d_log_cum, ((0, 0), (0, LANES - FOX_HEADS)))
    d_zf, d_bf = _logf_bwd(d_log_cum, zf, bf, name="l1_logf_bwd")
    d_w1t = (_matmul(dqkv1, h1, ta=True, out_dtype=BF16, name="l1_in_dw_qkv"),
             _matmul(d_gate1, h1, ta=True, out_dtype=BF16, name="l1_in_dw_gate"))
    d_wft = _matmul(d_zf, h1, ta=True, out_dtype=BF16, name="l1_in_f_dw")
    dx1, d_o_g_in, dx1_bf = _matmul_rows([(dqkv1, w1t, False, c * D_MODEL, c) for c in range(3)]
                                         + [(d_gate1, w1t, False, 3 * D_MODEL), (d_zf, wft, False)],
                                         [(x1, D_MODEL, 0), (dx2, D_MODEL, 0)], [o_g_in],
                                         lambda *a: _and_first(_rms_bwd_epilogue(*a)),
                                         [("rows", D_MODEL, F32), ("sum", (1, D_MODEL)), ("rows", D_MODEL, BF16)],
                                         name="l1_in_dx")

    d_wo0 = _matmul(og0, dx1_bf, ta=True, out_dtype=BF16, name="l0_out_dw")
    do_mla, do_swa, d_gate0 = _matmul_rows(
        [(dx1_bf, wo0, True)], [(o_mla, half, 0), (o_swa, half, 0), (z0a, D_MODEL, 0)], [], _gate_bwd_epilogue([half, half]),
        [("rows", half, F32), ("rows", half, F32), ("rows", D_MODEL, BF16)], name="l0_out_dx")
    dq_s, dkt_s, dvt_s, d_sinks = _swa_bwd(z0b, e_sinks, do_swa, o_swa, lse_swa, name="l0_swa_bwd")
    dk_s = dkt_s.transpose(0, 2, 1).reshape(s, LANES)
    dv_s = dvt_s.transpose(0, 2, 1).reshape(s, LANES)
    rider = None
    if scatter1 is not None:
        rider = ("exchange", scatter1(dict(w1t=d_w1t, wft=d_wft, wo1=d_wo1, o_g_in=d_o_g_in, wo0=d_wo0)))
    res = _flash_bwd(qm, kvm, kvm, do_mla, o_mla, lse_mla, None, n_pairs=MLA_HEADS // 2, hw=LANES, q_off=0, k_off=0,
                     v_off=MLA_HEADS, scale=mla_scale, qk_dtype=F32, name="l0_mla_bwd", rider=rider)
    dqm, dkm, dvm = res[0], res[1], res[2]
    recv1 = res[3] if rider is not None else None
    d_qp, d_kvp, d_kpe = _rope_bwd(dqm, dkm, dvm, cos_t, sin_t, name="l0_rope_bwd")
    d_wq = _matmul(cqn, d_qp, ta=True, out_dtype=BF16, name="l0_q_up_dw")
    d_cqn = _matmul(d_qp, wq, tb=True, name="l0_q_up_dx")
    d_wkv = _matmul(ckvn, d_kvp, ta=True, out_dtype=BF16, name="l0_kv_up_dw")
    d_ckvn = _matmul(d_kvp, wkv, tb=True, name="l0_kv_up_dx")
    d_cq, d_g_q_a = _rmsnorm_bwd(z0a, e_g_q_a, d_cqn, width=MLA_Q_RANK, col_blk=4, name="l0_q_norm_bwd")
    d_ckv, d_g_kv_a = _rmsnorm_bwd(z0a, e_g_kv_a, d_ckvn, width=MLA_KV_RANK, col_blk=10, name="l0_kv_norm_bwd")
    dz0 = jnp.concatenate([d_gate0, d_cq, d_ckv, d_kpe, dq_s.astype(BF16), dk_s.astype(BF16), dv_s.astype(BF16)], axis=1)
    d_w0t = _matmul(dz0, h0, ta=True, out_dtype=BF16, name="l0_in_dw")
    pending0, after_start = None, []
    if scatter0 is not None:
        *pending0, token = _peer_start("exchange", scatter0(dict(w0t=d_w0t, wq=d_wq, wkv=d_wkv)), name="grads0_start")
        after_start = [token]
    grad_x, d_e_g_in = _matmul_rows(
        [(dz0, w0t, False)], [(x, D_MODEL, 0), (dx1, D_MODEL, 0)], [e_g_in] + after_start,
        lambda dy, xt, add, g, *_: _rms_bwd_epilogue(dy, xt, add, g),
        [("rows", D_MODEL, F32), ("sum", (1, D_MODEL))], name="l0_in_dx")

    return dict(pending0=pending0, recv1=recv1, loss=loss_part[0, 0], grad_x=grad_x, e_g_in=d_e_g_in, w0t=d_w0t, e_g_q_a=d_g_q_a, wq=d_wq,
                e_g_kv_a=d_g_kv_a, wkv=d_wkv, e_sinks=d_sinks[:, 0].reshape(1, SWA_HEADS), wo0=d_wo0,
                o_g_in=d_o_g_in, w1t=d_w1t, wft=d_wft, o_b_f=d_bf[:, :FOX_HEADS], wo1=d_wo1, g_final=d_g_final.reshape(D_MODEL))


def _wide(a, rows):
    flat = a.reshape(-1)
    return jnp.pad(flat, (0, rows * WIDE - flat.shape[0])).reshape(rows, WIDE)


def _rows_b0(w_q, w_kv):
    return jnp.concatenate([_wide(w_q, 32), _wide(w_kv, 16)], axis=0)


def _unflat_b0(f):
    return f[0:24].reshape(1, MLA_Q_RANK, 96), f[32:48].reshape(1, MLA_KV_RANK, 128)


def _rows_b1(o_w_out, e_w_out, g_in):
    return jnp.concatenate([o_w_out, e_w_out, _wide(g_in, 16)], axis=0)


def _unflat_b1(f):
    return f[0:128][None], f[128:256][None], f[256:257, :LANES]


def kernel(x, positions, e_g_in, e_w_in, e_g_q_a, e_w_q_up, e_g_kv_a, e_w_kv_up, e_sinks, e_w_out, o_g_in, o_w_in, o_b_f, o_w_out, g_final, loss_target, m_e_g_in, m_e_w_in, m_e_g_q_a, m_e_w_q_up, m_e_g_kv_a, m_e_w_kv_up, m_e_sinks, m_e_w_out, m_o_g_in, m_o_w_in, m_o_b_f, m_o_w_out, m_g_final, v_e_g_in, v_e_w_in, v_e_g_q_a, v_e_w_q_up, v_e_g_kv_a, v_e_w_kv_up, v_e_sinks, v_e_w_out, v_o_g_in, v_o_w_in, v_o_b_f, v_o_w_out, v_g_final):
    def bf(a):
        return a.astype(BF16)

    me = 4 * lax.axis_index("x") + 2 * lax.axis_index("y") + lax.axis_index("c")
    shard0 = jnp.concatenate([_pad_rows(bf(e_w_in[0]).T, RA0), _rows_b0(bf(e_w_q_up[0]), bf(e_w_kv_up[0]))], axis=0)
    *pending_w0, token_w0 = _peer_start("across", shard0, name="weights0_start")

    def unpack0(sent, gath0):
        gath0 = lax.dynamic_update_slice_in_dim(gath0, sent[None], me, axis=0)
        w0t = _layer0_in_unpack(gath0, name="weights0_unpack")
        wq = _q_up_weight(_gathered_cols(gath0[:, RA0:RA0 + 24], MLA_Q_RANK))
        wkv = _kv_up_weight(_gathered_cols(gath0[:, RA0 + 32:RA0 + 48], MLA_KV_RANK))
        return w0t, wq, wkv

    rows_b0 = [_rows_b0(q[0], kv[0]) for q, kv in ((e_w_q_up, e_w_kv_up), (m_e_w_q_up, m_e_w_kv_up), (v_e_w_q_up, v_e_w_kv_up))]
    rows_b1 = [_rows_b1(o[0], e[0], g) for o, e, g in ((o_w_out, e_w_out, o_g_in), (m_o_w_out, m_e_w_out, m_o_g_in),
                                                       (v_o_w_out, v_e_w_out, v_o_g_in))]

    g_bits = lax.bitcast_convert_type(o_g_in.reshape(LANES), BF16)
    shard1 = jnp.concatenate([_pad_rows(bf(o_w_in[0]).T, RA1), _rows_b1(bf(o_w_out[0]), bf(e_w_out[0]), g_bits)], axis=0)

    def unpack1(gath1):
        w1t, wft, wo1, wo0 = _layer1_in_unpack(gath1, name="weights1_unpack")
        bits = gath1[:, RA1 + 256, :2 * LANES].reshape(N_DEV, LANES, 2)
        return wo0, lax.bitcast_convert_type(bits, F32).reshape(1, D_MODEL), w1t, wft, wo1

    def scatter1(g):
        d_o_g = jnp.pad(bf(g["o_g_in"]).reshape(N_DEV, 1, LANES), ((0, 0), (0, 15), (0, WIDE - LANES)))
        return _late_grads_pack(g["w1t"][0], g["wft"], g["w1t"][1], g["wo1"], g["wo0"], d_o_g, name="grads1_pack")

    def scatter0(g):
        return _early_grads_pack(g["w0t"], _pad_rows(_scatter_cols(_q_up_grad(g["wq"])), 32),
                                 _scatter_cols(_kv_up_grad(g["wkv"])), name="grads0_pack")

    gr = _local_step(x[0], positions[0], loss_target[0], e_g_in,
                     (pending_w0, token_w0, unpack0, [shard1] + rows_b0 + rows_b1), e_g_q_a, e_g_kv_a, e_sinks,
                     (shard1, unpack1), o_b_f, g_final, scatter1=scatter1, scatter0=scatter0)

    def in_projection(recv, ra, n, w, m, v, name):
        g = _sum8(recv, ra, name=name + "_grad_sum")[:n].reshape(n, 1, D_MODEL)
        w, m, v = [jnp.transpose(a, (2, 0, 1)) for a in (w, m, v)]
        return (g, *_adamw_columns(g, w, m, v, name=name + "_adamw"))

    o_in = in_projection(gr["recv1"], RA1, N_O_IN, o_w_in, m_o_w_in, v_o_w_in, "o_w_in")
    b1 = _adamw(gr["recv1"][:, RA1:], *rows_b1, name="adamw_late")

    small = _small_pack(gr["e_g_in"], gr["g_final"], gr["e_g_q_a"], gr["e_g_kv_a"], gr["e_sinks"], gr["o_b_f"], gr["loss"])
    small_all = _all_gather(small, name="small_all_gather")
    zero = jnp.zeros((), F32)
    w_small = _small_pack(e_g_in, g_final, e_g_q_a, e_g_kv_a, e_sinks, o_b_f, zero)
    m_small = _small_pack(m_e_g_in, m_g_final, m_e_g_q_a, m_e_g_kv_a, m_e_sinks, m_o_b_f, zero)
    v_small = _small_pack(v_e_g_in, v_g_final, v_e_g_q_a, v_e_g_kv_a, v_e_sinks, v_o_b_f, zero)
    smalls = _adamw(small_all, w_small, m_small, v_small, name="adamw_replicated")
    g_sm, d_sm, m_sm, v_sm = [_small_unpack(a) for a in smalls]
    loss = g_sm[6]

    sent0, recv0 = _peer_wait("exchange", *gr["pending0"], after=[o_in[1], b1[1], smalls[1]], name="grads0_wait")
    own = lax.dynamic_slice_in_dim(sent0, me, 1, axis=0)
    recv0 = lax.dynamic_update_slice_in_dim(recv0, own, me, axis=0)
    e_in = in_projection(recv0, RA0, N_E_IN, e_w_in, m_e_w_in, v_e_w_in, "e_w_in")
    b0 = _adamw(recv0[:, RA0:], *rows_b0, name="adamw_early")

    def sharded(k):
        q_up, kv_up = _unflat_b0(b0[k])
        o_out, e_out, o_g = _unflat_b1(b1[k])
        return jnp.transpose(e_in[k], (1, 2, 0)), q_up, kv_up, e_out, jnp.transpose(o_in[k], (1, 2, 0)), o_out, o_g

    g_sh, d_sh, m_sh, v_sh = [sharded(k) for k in range(4)]

    def leaves(sh, sm):
        return (sm[0], sh[0], sm[2], sh[1], sm[3], sh[2], sm[4], sh[3], sh[6], sh[4], sm[5], sh[5], sm[1])

    return (loss, gr["grad_x"][None], *leaves(g_sh, g_sm), *leaves(d_sh, d_sm), *leaves(m_sh, m_sm), *leaves(v_sh, v_sm))
```

```python
import functools

import jax
import jax.numpy as jnp
from jax import lax
from jax.experimental import pallas as pl
from jax.experimental.pallas import tpu as pltpu

F32 = jnp.float32
BF16 = jnp.bfloat16
NEG_INF = float("-inf")

N_DEV = 8
LANES = 128
D_MODEL = 1024
EPS = 1e-6
ROPE_THETA = 10000.0
MLA_HEADS = 8
MLA_Q_RANK = 256
MLA_KV_RANK = 128
MLA_NOPE = 64
MLA_ROPE = 32
MLA_V = 64
SWA_HEADS = 8
SWA_KV_HEADS = 2
SWA_DIM = 64
WINDOW = 128
FOX_HEADS = 16
FOX_DIM = 64

ADAM_LR = 0.001
ADAM_B1 = 0.9
ADAM_B2 = 0.999
ADAM_EPS = 1e-08
ADAM_WD = 0.01
ADAM_STEP = 10

ATT_T = 512
ATT_T_FWD = 1024
VMEM_LIMIT = 56 * 1024 * 1024
MATMUL_B_BLOCK_BYTES = 8 * 1024 * 1024

Z0A_UNITS = 12
Z0B_UNITS = 6

WIDE = 1024
N_E_IN = 276
N_O_IN = 514
RA0 = 288
RB0 = 32 + 16
RA1 = 528
RB1 = 128 + 128 + 16
SMALL_ROWS = 24


def _tile(n, cands):
    for c in cands:
        if n % c == 0:
            return c
    raise ValueError(f"no tile for {n}")


ROW_TILES = (512, 256, 128)


def _params(sem, vmem=VMEM_LIMIT):
    return pltpu.CompilerParams(dimension_semantics=sem, vmem_limit_bytes=vmem)


def _matmul(a, b, *, name, ta=False, tb=False, out_dtype=F32):
    if ta:
        kdim, m = a.shape[-2], a.shape[-1] * (a.shape[0] if a.ndim == 3 else 1)
    else:
        m, kdim = a.shape
    if tb:
        n, kb = b.shape
    else:
        kb, n = b.shape
    assert kdim == kb, (a.shape, b.shape)
    tm = _tile(m, (512, 256, 128))
    tn = _tile(n, [c for c in (1024, 768, 512, 384, 256, 128) if c * kdim * b.dtype.itemsize <= MATMUL_B_BLOCK_BYTES])
    dims = (((0 if ta else 1,), (1 if tb else 0,)), ((), ()))

    def body(a_ref, b_ref, o_ref):
        r = lax.dot_general(a_ref[...].astype(BF16), b_ref[...].astype(BF16), dims, preferred_element_type=F32)
        o_ref[...] = r.astype(out_dtype)

    if a.ndim == 3:
        per = a.shape[2] // tm
        a_spec = pl.BlockSpec((None, kdim, tm), lambda i, j: (i // per, 0, i % per))
    else:
        a_spec = pl.BlockSpec((kdim, tm), lambda i, j: (0, i)) if ta else pl.BlockSpec((tm, kdim), lambda i, j: (i, 0))
    b_spec = pl.BlockSpec((tn, kdim), lambda i, j: (j, 0)) if tb else pl.BlockSpec((kdim, tn), lambda i, j: (0, j))
    return pl.pallas_call(
        body, name=name, grid=(m // tm, n // tn), in_specs=[a_spec, b_spec],
        out_specs=pl.BlockSpec((tm, tn), lambda i, j: (i, j)), out_shape=jax.ShapeDtypeStruct((m, n), out_dtype),
        compiler_params=_params(("parallel", "parallel")),
    )(a, b)


def _rmsnorm_fwd(x, g, *, width, col_blk, name, after=()):
    s = x.shape[0]
    tm = _tile(s, ROW_TILES)

    def body(x_ref, g_ref, *rest):
        y_ref = rest[-1]
        xf = x_ref[...].astype(F32)
        r = lax.rsqrt(jnp.mean(xf * xf, axis=-1, keepdims=True) + EPS)
        y_ref[...] = ((xf * r) * g_ref[...]).astype(BF16)

    return pl.pallas_call(
        body, name=name, grid=(s // tm,),
        in_specs=[pl.BlockSpec((tm, width), lambda i: (i, col_blk)), pl.BlockSpec((1, width), lambda i: (0, 0))]
        + [ANY] * len(after),
        out_specs=pl.BlockSpec((tm, width), lambda i: (i, 0)),
        out_shape=jax.ShapeDtypeStruct((s, width), BF16),
        compiler_params=_params(("parallel",)),
    )(x, g, *after)


def _rmsnorm_bwd(x, g, dy, *, width, col_blk, name):
    s = x.shape[0]
    tm = _tile(s, ROW_TILES)

    def body(x_ref, g_ref, dy_ref, dx_ref, dg_ref):
        @pl.when(pl.program_id(0) == 0)
        def _():
            dg_ref[...] = jnp.zeros_like(dg_ref)

        dx, dg = _rms_bwd_epilogue(dy_ref[...], x_ref[...], 0.0, g_ref[...])
        dg_ref[...] += dg
        dx_ref[...] = dx.astype(BF16)

    return pl.pallas_call(
        body, name=name, grid=(s // tm,),
        in_specs=[pl.BlockSpec((tm, width), lambda i: (i, col_blk)), pl.BlockSpec((1, width), lambda i: (0, 0)),
                  pl.BlockSpec((tm, width), lambda i: (i, 0))],
        out_specs=[pl.BlockSpec((tm, width), lambda i: (i, 0)), pl.BlockSpec((1, width), lambda i: (0, 0))],
        out_shape=[jax.ShapeDtypeStruct((s, width), BF16), jax.ShapeDtypeStruct((1, width), F32)],
        compiler_params=_params(("arbitrary",)),
    )(x, g, dy)


def _sigmoid(x):
    return 1.0 / (1.0 + jnp.exp(-x))


def _matmul_rows(terms, row_inputs, params, epilogue, outs, *, name, prologue=None, separate=False):
    s = row_inputs[0][0].shape[0] if row_inputs else terms[0][0].shape[-2]
    tm = _tile(s, ROW_TILES)
    steps = s // tm
    n_r, n_p, n_o = len(row_inputs), len(params), len(outs)
    n_t = sum(1 if term[0] is None else 2 for term in terms)

    def body(*refs):
        t_refs, r_refs = list(refs[:n_t]), refs[n_t:n_t + n_r]
        p_refs, o_refs = refs[n_t + n_r:n_t + n_r + n_p], refs[n_t + n_r + n_p:]
        i = pl.program_id(0)
        rows, small = [r[...] for r in r_refs], [p[...] for p in p_refs]
        made = None if prologue is None else prologue(*rows, *small)
        parts = []
        for term in terms:
            a = made if term[0] is None else t_refs.pop(0)[...].astype(BF16)
            dims = (((1,), (1 if term[2] else 0,)), ((), ()))
            parts.append(lax.dot_general(a, t_refs.pop(0)[...].astype(BF16), dims, preferred_element_type=F32))
        acc = parts if separate else sum(parts[1:], parts[0])
        vals = epilogue(acc, *rows, *small) if prologue is None else epilogue(acc, *rows, *small, made)
        for ref, val, out in zip(o_refs, vals, outs):
            if out[0] == "rows":
                ref[...] = val.astype(ref.dtype)
            else:
                @pl.when(i == 0)
                def _(ref=ref):
                    ref[...] = jnp.zeros_like(ref)

                ref[...] += val

    in_specs, args = [], []
    for term in terms:
        a, b = term[0], term[1]
        if a is None:
            in_specs.append(pl.BlockSpec(b.shape, lambda i: (0, 0)))
            args.append(b)
            continue
        b_rows = b.shape[0] if term[2] or len(term) < 4 else a.shape[-1]
        b_blk = 0 if len(term) < 4 else term[3] // b_rows
        if len(term) == 5:
            a_spec = pl.BlockSpec((None, tm, a.shape[2]), lambda i, c=term[4]: (c, i, 0))
        else:
            a_spec = pl.BlockSpec((tm, a.shape[1]), lambda i: (i, 0))
        in_specs += [a_spec, pl.BlockSpec((b_rows, b.shape[1]), lambda i, b_blk=b_blk: (b_blk, 0))]
        args += [a, b]
    for arr, width, col_blk in row_inputs:
        in_specs.append(pl.BlockSpec((tm, width), lambda i, col_blk=col_blk: (i, col_blk)))
        args.append(arr)
    for p in params:
        in_specs.append(pl.BlockSpec(p.shape, lambda i: (0, 0)))
        args.append(p)
    out_specs, out_shape = [], []
    for out in outs:
        if out[0] == "rows":
            out_specs.append(pl.BlockSpec((tm, out[1]), lambda i: (i, 0)))
            out_shape.append(jax.ShapeDtypeStruct((s, out[1]), out[2]))
        else:
            out_specs.append(pl.BlockSpec(out[1], lambda i: (0, 0)))
            out_shape.append(jax.ShapeDtypeStruct(out[1], F32))
    return pl.pallas_call(
        body, name=name, grid=(steps,), in_specs=in_specs, out_specs=out_specs, out_shape=out_shape,
        compiler_params=_params(("arbitrary",)),
    )(*args)


def _rms_stats(x):
    r = lax.rsqrt(jnp.mean(x * x, axis=-1, keepdims=True) + EPS)
    return r, x * r


def _gated(o_parts, gate):
    o = o_parts[0] if len(o_parts) == 1 else jnp.concatenate(o_parts, axis=1)
    return (o * (gate * _sigmoid(gate))).astype(BF16)


def _and_first(vals, *more):
    return (*vals, *more, vals[0])


def _residual_norm_epilogue(r, x, g):
    x1 = x + r
    _, xh = _rms_stats(x1)
    return x1, xh * g


def _rms_bwd_epilogue(dy, x, add, g):
    r, xh = _rms_stats(x)
    dxh = dy * g
    dx = r * (dxh - xh * jnp.mean(dxh * xh, axis=-1, keepdims=True)) + add
    return dx, jnp.sum(dy * xh, axis=0, keepdims=True)


def _loss_epilogue(r, x1, target, g):
    rs, xh = _rms_stats(x1 + r)
    err = xh * g - target
    loss = jnp.broadcast_to(0.5 * jnp.sum(jnp.mean(err * err, axis=-1, keepdims=True)), (8, LANES))
    dy = err * (1.0 / D_MODEL)
    dxh = dy * g
    dx = rs * (dxh - xh * jnp.mean(dxh * xh, axis=-1, keepdims=True))
    return dx, loss, jnp.sum(dy * xh, axis=0, keepdims=True)


def _gate_bwd_epilogue(widths):
    def epilogue(d, *rows):
        o_parts, gt = rows[:-1], rows[-1]
        o = o_parts[0] if len(o_parts) == 1 else jnp.concatenate(o_parts, axis=1)
        sg = _sigmoid(gt)
        do = d * (gt * sg)
        d_gate = d * o * (sg * (1.0 + gt * (1.0 - sg)))
        cuts = [sum(widths[:k]) for k in range(len(widths) + 1)]
        return tuple(do[:, cuts[k]:cuts[k + 1]] for k in range(len(widths))) + (d_gate,)

    return epilogue


def _rot_half(x):
    lane = lax.broadcasted_iota(jnp.int32, x.shape, 1)
    return jnp.where(lane < 80, pltpu.roll(x, LANES - 16, axis=1), pltpu.roll(x, 16, axis=1))


def _rot_half_t(g):
    lane = lax.broadcasted_iota(jnp.int32, g.shape, 1)
    lo = (lane >= MLA_NOPE) & (lane < MLA_NOPE + MLA_ROPE // 2)
    hi = (lane >= MLA_NOPE + MLA_ROPE // 2) & (lane < MLA_NOPE + MLA_ROPE)
    return jnp.where(lo, pltpu.roll(g, LANES - 16, axis=1), jnp.where(hi, pltpu.roll(g, 16, axis=1), 0.0))


def _rope_q_epilogue(q, c, sn):
    heads = [q[:, h * LANES:(h + 1) * LANES] for h in range(MLA_HEADS)]
    return (jnp.concatenate([qh * c + _rot_half(qh) * sn for qh in heads], axis=1),)


def _rope_k_epilogue(kv, kpe, c, sn):
    kpe_r = kpe * c + _rot_half(kpe) * sn
    lane = lax.broadcasted_iota(jnp.int32, kpe.shape, 1)
    heads = [jnp.where(lane < MLA_NOPE, kv[:, h * LANES:(h + 1) * LANES], kpe_r) for h in range(MLA_HEADS)]
    return (jnp.concatenate(heads + [kv[:, MLA_HEADS * LANES:]], axis=1),)


def _rope_bwd(dqm, dkm, dvm, cos_t, sin_t, *, name):
    s = dqm.shape[0]
    tm = _tile(s, ROW_TILES)
    hw = MLA_HEADS * LANES
    vw = MLA_HEADS * MLA_V

    def body(dq_ref, dk_ref, dv_ref, c_ref, s_ref, dqp_ref, dkv_ref, dkpe_ref):
        c = c_ref[...]
        sn = s_ref[...]
        ksum = jnp.zeros((tm, LANES), F32)
        for h in range(MLA_HEADS):
            sl = slice(h * LANES, (h + 1) * LANES)
            dq = dq_ref[:, sl]
            dqp_ref[:, sl] = (dq * c + _rot_half_t(dq * sn)).astype(BF16)
            dk = dk_ref[:, sl]
            dkv_ref[:, sl] = dk.astype(BF16)
            ksum = ksum + dk
        dkv_ref[:, hw:] = dv_ref[...]
        lane = lax.broadcasted_iota(jnp.int32, ksum.shape, 1)
        dkpe = ksum * c + _rot_half_t(ksum * sn)
        dkpe_ref[...] = jnp.where((lane >= MLA_NOPE) & (lane < MLA_NOPE + MLA_ROPE), dkpe, 0.0).astype(BF16)

    return pl.pallas_call(
        body, name=name, grid=(s // tm,),
        in_specs=[pl.BlockSpec((tm, hw), lambda i: (i, 0)), pl.BlockSpec((tm, hw), lambda i: (i, 0)),
                  pl.BlockSpec((tm, vw), lambda i: (i, 0)),
                  pl.BlockSpec((tm, LANES), lambda i: (i, 0)), pl.BlockSpec((tm, LANES), lambda i: (i, 0))],
        out_specs=[pl.BlockSpec((tm, hw), lambda i: (i, 0)), pl.BlockSpec((tm, hw + vw), lambda i: (i, 0)),
                   pl.BlockSpec((tm, LANES), lambda i: (i, 0))],
        out_shape=[jax.ShapeDtypeStruct((s, hw), BF16), jax.ShapeDtypeStruct((s, hw + vw), BF16),
                   jax.ShapeDtypeStruct((s, LANES), BF16)],
        compiler_params=_params(("parallel",)),
    )(dqm, dkm, dvm, cos_t, sin_t)


def _head_mask(shape, a):
    lane = lax.broadcasted_iota(jnp.int32, shape, 1)
    return (lane >= 64 * a) & (lane < 64 * (a + 1))


_NT = (((1,), (1,)), ((), ()))
LOG2E = 1.4426950408889634


def _stack_heads(tile, hw):
    lane = lax.broadcasted_iota(jnp.int32, tile.shape, 1)
    z = jnp.zeros_like(tile)
    return jnp.concatenate([jnp.where(lane < hw, tile, z), jnp.where(lane >= hw, tile, z)], axis=0)


def _stacked_rows(r0, r1, t):
    n = r0.shape[-1]
    return jnp.concatenate([jnp.broadcast_to(r0, (t, n)), jnp.broadcast_to(r1, (t, n))], axis=0)


def _resident(block, index_map):
    return pl.BlockSpec(block, index_map, pipeline_mode=pl.Buffered(1))


def _fwd_tile(s):
    return ATT_T_FWD if s % ATT_T_FWD == 0 else min(ATT_T, s)


def _flash_fwd(q, k, v, bias, *, n_pairs, hw, q_off, k_off, v_off, scale, name, rider=None):
    s = q.shape[0]
    t = _fwd_tile(s)
    nb = s // t
    qw = 2 * hw
    has_bias = bias is not None
    c1 = scale * LOG2E

    def body(*refs):
        refs, ride_refs = _split_rider(refs, rider, n_in=4 if has_bias else 3, n_out=2)
        if has_bias:
            q_ref, k_ref, v_ref, b_ref, o_ref, lse_ref, vt_ref, bcol_ref = refs
        else:
            q_ref, k_ref, v_ref, o_ref, lse_ref, vt_ref = refs
            b_ref = bcol_ref = None
        _ride_start(rider, ride_refs, pl.program_id(0) == 0)
        row = lax.broadcasted_iota(jnp.int32, (t, t), 0)
        col = lax.broadcasted_iota(jnp.int32, (t, t), 1)
        cmask_t = jnp.concatenate([row <= col, row <= col], axis=1)
        lane_lt64 = lax.broadcasted_iota(jnp.int32, (t, LANES), 1) < 64

        def as_column(r):
            return jnp.broadcast_to(r, (8, r.shape[1])).T[:, 0:1]

        def v_block(j, _):
            c0 = pl.multiple_of(j * t, t)
            vt_ref[j] = v_ref[pl.ds(c0, t), :].astype(F32).T.astype(BF16)
            if has_bias:
                for a in range(2):
                    bcol_ref[a, pl.ds(c0, t), :] = as_column(b_ref[0, a, j])
            return 0

        lax.fori_loop(0, nb, v_block, 0)

        def stacked_queries(i):
            return _stack_heads(q_ref[pl.ds(pl.multiple_of(i * t, t), t), :], hw).astype(F32).T.astype(BF16)

        def kv_step(j, carry, qs_t, masked):
            m, l, acc = carry
            rows = pl.ds(pl.multiple_of(j * t, t), t)
            sc = jnp.dot(k_ref[rows, :], qs_t, preferred_element_type=F32) * c1
            if has_bias:
                sc = sc + jnp.concatenate([jnp.broadcast_to(bcol_ref[0, rows, :], (t, t)),
                                           jnp.broadcast_to(bcol_ref[1, rows, :], (t, t))], axis=1)
            if masked:
                sc = jnp.where(cmask_t, sc, NEG_INF)
            m_new = jnp.maximum(m, jnp.max(sc, axis=0, keepdims=True))
            alpha = jnp.exp2(m - m_new)
            p = jnp.exp2(sc - m_new)
            l_new = alpha * l + jnp.sum(p, axis=0, keepdims=True)
            pv = jnp.dot(vt_ref[j], p.astype(BF16), preferred_element_type=F32)
            return m_new, l_new, alpha * acc + pv

        def finish(i, carry):
            m, l, acc = carry
            r0 = pl.multiple_of(i * t, t)
            out = (acc / l).T
            lse2 = as_column(m + jnp.log2(l))
            lse_ref[0, 0, pl.ds(r0, t), :] = lse2[:t]
            lse_ref[0, 1, pl.ds(r0, t), :] = lse2[t:]
            o_ref[pl.ds(r0, t), :] = jnp.where(lane_lt64, out[:t], out[t:])

        init = (jnp.full((1, 2 * t), NEG_INF, F32), jnp.zeros((1, 2 * t), F32), jnp.zeros((LANES, 2 * t), F32))

        def q_block(i, _):
            qs_t = stacked_queries(i)
            carry = lax.fori_loop(0, i, lambda j, c: kv_step(j, c, qs_t, False), init)
            finish(i, kv_step(i, carry, qs_t, True))
            return 0

        lax.fori_loop(0, nb, q_block, 0)
        _ride_wait(rider, ride_refs, pl.program_id(0) == n_pairs - 1)

    in_specs = [_resident((s, qw), lambda p: (0, q_off + p)), _resident((s, qw), lambda p: (0, k_off + p)),
                _resident((s, LANES), lambda p: (0, v_off + p))]
    args = [q, k, v]
    if has_bias:
        in_specs.append(_resident((1, 2, nb, 1, t), lambda p: (p, 0, 0, 0, 0)))
        args.append(bias)
    out_specs = [pl.BlockSpec((s, LANES), lambda p: (0, p)), pl.BlockSpec((1, 2, s, 1), lambda p: (p, 0, 0, 0))]
    out_shape = [jax.ShapeDtypeStruct((s, n_pairs * LANES), F32), jax.ShapeDtypeStruct((n_pairs, 2, s, 1), F32)]
    scratch = [pltpu.VMEM((nb, LANES, t), BF16)] + ([pltpu.VMEM((2, s, 1), F32)] if has_bias else [])
    scratch += _add_rider(rider, in_specs, args, out_specs, out_shape)
    return pl.pallas_call(
        body, name=name, grid=(n_pairs,), in_specs=in_specs, out_specs=out_specs, out_shape=out_shape,
        scratch_shapes=scratch,
        compiler_params=_params(("parallel",) if rider is None else ("arbitrary",)),
    )(*args)


def _flash_bwd(q, k, v, do, o, lse, bias, *, n_pairs, hw, q_off, k_off, v_off, scale, qk_dtype, name, rider=None,
               stacked=False):
    s = q.shape[0]
    t = min(ATT_T, s)
    nb = s // t
    qw = 2 * hw
    has_bias = bias is not None
    c1 = scale * LOG2E

    def body(*refs):
        n_grads = 1 if stacked else 3
        refs, ride_refs = _split_rider(refs, rider, n_in=7 if has_bias else 6, n_out=n_grads + (2 if has_bias else 0))
        if stacked:
            refs = list(refs)
            n_in = 7 if has_bias else 6
            refs[n_in:n_in + 1] = [refs[n_in].at[0], refs[n_in].at[1], refs[n_in].at[2]]
        if has_bias:
            (q_ref, k_ref, v_ref, do_ref, o_ref, lse_ref, b_ref, dq_ref, dk_ref, dv_ref, db_ref, dr_ref,
             dkt_ref, dvt_ref) = refs
            db_ref[...] = jnp.zeros_like(db_ref)
        else:
            q_ref, k_ref, v_ref, do_ref, o_ref, lse_ref, dq_ref, dk_ref, dv_ref, dkt_ref, dvt_ref = refs
            b_ref = db_ref = dr_ref = None
        _ride_start(rider, ride_refs, pl.program_id(0) == 0)
        dkt_ref[...] = jnp.zeros_like(dkt_ref)
        dvt_ref[...] = jnp.zeros_like(dvt_ref)
        causal = lax.broadcasted_iota(jnp.int32, (t, t), 1) <= lax.broadcasted_iota(jnp.int32, (t, t), 0)
        cmask = jnp.concatenate([causal, causal], axis=0)
        lane_lt_hw = lax.broadcasted_iota(jnp.int32, (t, qw), 1) < hw

        def q_block(i, _):
            r0 = pl.multiple_of(i * t, t)
            qs = _stack_heads(q_ref[pl.ds(r0, t), :], hw)
            dos = _stack_heads(do_ref[pl.ds(r0, t), :], 64)
            ot = o_ref[pl.ds(r0, t), :]
            delta = jnp.sum(dos * jnp.concatenate([ot, ot], axis=0), axis=-1, keepdims=True)
            lse2 = jnp.concatenate([lse_ref[0, 0, pl.ds(r0, t), :], lse_ref[0, 1, pl.ds(r0, t), :]], axis=0)
            dosb = dos.astype(BF16)
            dos_t = dos.T.astype(BF16)
            qs_t = qs.astype(F32).T.astype(BF16)

            def kv_step(j, carry, masked):
                dq, rsum = carry
                c0 = pl.multiple_of(j * t, t)
                kt = k_ref[pl.ds(c0, t), :]
                vt = v_ref[pl.ds(c0, t), :]
                sc = lax.dot_general(qs, kt, _NT, preferred_element_type=F32) * c1
                if has_bias:
                    sc = sc + _stacked_rows(b_ref[0, 0, j], b_ref[0, 1, j], t)
                if masked:
                    sc = jnp.where(cmask, sc, NEG_INF)
                p = jnp.exp2(sc - lse2)
                dp = lax.dot_general(dosb, vt, _NT, preferred_element_type=F32)
                ds = p * (dp - delta)
                dsb = ds.astype(BF16)
                pb = p.astype(BF16)
                if hw == LANES:
                    dvt_ref[j] += jnp.concatenate(
                        [jnp.dot(dos_t[:64, :t], pb[:t], preferred_element_type=F32),
                         jnp.dot(dos_t[64:, t:], pb[t:], preferred_element_type=F32)], axis=0)
                    dkt_ref[j] += jnp.concatenate(
                        [jnp.dot(qs_t[:hw, :t], dsb[:t], preferred_element_type=F32),
                         jnp.dot(qs_t[hw:, t:], dsb[t:], preferred_element_type=F32)], axis=0)
                else:
                    dvt_ref[j] += jnp.dot(dos_t, pb, preferred_element_type=F32)
                    dkt_ref[j] += jnp.dot(qs_t, dsb, preferred_element_type=F32)
                if has_bias:
                    db_ref[0, 0, j] += jnp.sum(ds[:t], axis=0, keepdims=True)
                    db_ref[0, 1, j] += jnp.sum(ds[t:], axis=0, keepdims=True)
                    rsum = rsum + jnp.sum(ds, axis=-1, keepdims=True)
                return dq + jnp.dot(dsb, kt, preferred_element_type=F32), rsum

            init = (jnp.zeros((2 * t, qw), F32), jnp.zeros((2 * t, 1), F32))
            carry = lax.fori_loop(0, i, functools.partial(kv_step, masked=False), init)
            dq, rsum = kv_step(i, carry, True)
            dq = dq * scale
            dq_ref[pl.ds(r0, t), :] = jnp.where(lane_lt_hw, dq[:t], dq[t:]).astype(qk_dtype)
            if has_bias:
                rsum_row = jnp.broadcast_to(rsum, (2 * t, LANES)).T[0:1]
                dr_ref[0, 0, i] = rsum_row[:, :t]
                dr_ref[0, 1, i] = rsum_row[:, t:]
            return 0

        lax.fori_loop(0, nb, q_block, 0)

        def k_block(j, _):
            c0 = pl.multiple_of(j * t, t)
            dk_ref[pl.ds(c0, t), :] = (dkt_ref[j].T * scale).astype(qk_dtype)
            dv_ref[pl.ds(c0, t), :] = dvt_ref[j].T.astype(BF16)
            return 0

        lax.fori_loop(0, nb, k_block, 0)
        _ride_wait(rider, ride_refs, pl.program_id(0) == n_pairs - 1)

    in_specs = [_resident((s, qw), lambda p: (0, q_off + p)), _resident((s, qw), lambda p: (0, k_off + p)),
                _resident((s, LANES), lambda p: (0, v_off + p)),
                _resident((s, LANES), lambda p: (0, p)), _resident((s, LANES), lambda p: (0, p)),
                _resident((1, 2, s, 1), lambda p: (p, 0, 0, 0))]
    args = [q, k, v, do, o, lse]
    if stacked:
        assert qw == LANES and qk_dtype == BF16
        out_specs = [pl.BlockSpec((3, s, LANES), lambda p: (0, 0, p))]
        out_shape = [jax.ShapeDtypeStruct((3, s, n_pairs * LANES), BF16)]
    else:
        out_specs = [pl.BlockSpec((s, qw), lambda p: (0, p)), pl.BlockSpec((s, qw), lambda p: (0, p)),
                     pl.BlockSpec((s, LANES), lambda p: (0, p))]
        out_shape = [jax.ShapeDtypeStruct((s, n_pairs * qw), qk_dtype), jax.ShapeDtypeStruct((s, n_pairs * qw), qk_dtype),
                     jax.ShapeDtypeStruct((s, n_pairs * LANES), BF16)]
    if has_bias:
        in_specs.append(_resident((1, 2, nb, 1, t), lambda p: (p, 0, 0, 0, 0)))
        args.append(bias)
        for _ in range(2):
            out_specs.append(pl.BlockSpec((1, 2, nb, 1, t), lambda p: (p, 0, 0, 0, 0)))
            out_shape.append(jax.ShapeDtypeStruct((n_pairs, 2, nb, 1, t), F32))
    scratch = [pltpu.VMEM((nb, qw, t), F32), pltpu.VMEM((nb, LANES, t), F32)]
    scratch += _add_rider(rider, in_specs, args, out_specs, out_shape)
    return pl.pallas_call(
        body, name=name, grid=(n_pairs,), in_specs=in_specs, out_specs=out_specs, out_shape=out_shape,
        scratch_shapes=scratch,
        compiler_params=_params(("parallel",) if rider is None else ("arbitrary",)),
    )(*args)


def _alibi_slope(h):
    return 2.0 ** (-8.0 * (h + 1.0) / SWA_HEADS)


SWA_ROWS = 512
SWA_SCALE = SWA_DIM ** -0.5


def _swa_geometry(i):
    w = WINDOW
    r0 = pl.multiple_of(i * w, w)
    b0 = pl.multiple_of(jnp.maximum(i - 1, 0) * w, w)
    row = lax.broadcasted_iota(jnp.int32, (w, 2 * w), 0)
    col = lax.broadcasted_iota(jnp.int32, (w, 2 * w), 1)
    dist = row - col + (r0 - b0)
    valid = (dist >= 0) & (dist < w)
    return r0, b0, dist.astype(F32), valid


def _swa_q_head(qblk, h):
    kv = h // (SWA_HEADS // SWA_KV_HEADS)
    if h % 2 != kv:
        qblk = pltpu.roll(qblk, 64, axis=1)
    return jnp.where(_head_mask(qblk.shape, kv), qblk, 0.0)


SWA_GROUP = SWA_HEADS // SWA_KV_HEADS


def _swa_stack(ref, rs, grp):
    parts = []
    for a in range(SWA_GROUP):
        h = SWA_GROUP * grp + a
        parts.append(_swa_q_head(ref[rs, (h // 2) * LANES:(h // 2 + 1) * LANES].astype(F32), h))
    return jnp.concatenate(parts, axis=0)


def _swa_unstack(x, grp):
    tiles = []
    for a in range(SWA_GROUP):
        h = SWA_GROUP * grp + a
        tile = x[a * WINDOW:(a + 1) * WINDOW]
        tiles.append(pltpu.roll(tile, 64, axis=1) if h % 2 != grp else tile)
    return tiles


def _swa_head_column(vals):
    return jnp.concatenate([jnp.full((WINDOW, 1), v, F32) for v in vals], axis=0)


def _swa_logits(qs, kb, dist, valid, grp):
    slopes = _swa_head_column([_alibi_slope(SWA_GROUP * grp + a) for a in range(SWA_GROUP)])
    dist4 = jnp.concatenate([dist] * SWA_GROUP, axis=0)
    valid4 = jnp.concatenate([valid] * SWA_GROUP, axis=0)
    sc = lax.dot_general(qs, kb, _NT, preferred_element_type=F32) * SWA_SCALE - slopes * dist4
    return jnp.where(valid4, sc, NEG_INF)


def _swa_merge_heads(tiles):
    lt64 = lax.broadcasted_iota(jnp.int32, (WINDOW, LANES), 1) < 64
    return jnp.concatenate([jnp.where(lt64, tiles[2 * b], tiles[2 * b + 1]) for b in range(SWA_HEADS // 2)], axis=1)


def _swa_fwd(z0b, sinks, *, name):
    s = z0b.shape[0]
    w = WINDOW
    rows = min(SWA_ROWS, s)
    per_step = rows // w
    qcols = SWA_HEADS * SWA_DIM

    def body(sink_ref, q_ref, k_ref, v_ref, o_ref, lse_ref):
        g = pl.program_id(0)
        for ii in range(per_step):
            rs = slice(ii * w, (ii + 1) * w)
            r0, b0, dist, valid = _swa_geometry(g * per_step + ii)
            kb = k_ref[pl.ds(b0, 2 * w), :]
            vb = v_ref[pl.ds(b0, 2 * w), :]
            o_tiles = []
            for h in range(SWA_HEADS):
                kv = h // SWA_GROUP
                qh = _swa_q_head(q_ref[rs, (h // 2) * LANES:(h // 2 + 1) * LANES].astype(F32), h).astype(BF16)
                sc = lax.dot_general(qh, kb, _NT, preferred_element_type=F32) * SWA_SCALE - _alibi_slope(h) * dist
                sc = jnp.where(valid, sc, NEG_INF)
                sink = sink_ref[0, h]
                m = jnp.maximum(jnp.max(sc, axis=-1, keepdims=True), sink)
                p = jnp.exp(sc - m)
                l = jnp.sum(p, axis=-1, keepdims=True) + jnp.exp(sink - m)
                oh = jnp.dot(p.astype(BF16), vb, preferred_element_type=F32) / l
                o_tiles.append(pltpu.roll(oh, 64, axis=1) if h % 2 != kv else oh)
                lse_ref[h, rs, :] = m + jnp.log(l)
            o_ref[rs, :] = _swa_merge_heads(o_tiles)

    return pl.pallas_call(
        body, name=name, grid=(s // rows,),
        in_specs=[pl.BlockSpec(memory_space=pltpu.SMEM),
                  pl.BlockSpec((rows, qcols), lambda g: (g, 0)),
                  pl.BlockSpec((s, LANES), lambda g: (0, 4)), pl.BlockSpec((s, LANES), lambda g: (0, 5))],
        out_specs=[pl.BlockSpec((rows, qcols), lambda g: (g, 0)), pl.BlockSpec((SWA_HEADS, rows, 1), lambda g: (0, g, 0))],
        out_shape=[jax.ShapeDtypeStruct((s, qcols), F32), jax.ShapeDtypeStruct((SWA_HEADS, s, 1), F32)],
        compiler_params=_params(("parallel",)),
    )(sinks, z0b, z0b, z0b)


def _swa_bwd(z0b, sinks, do, o, lse, *, name):
    s = z0b.shape[0]
    w = WINDOW
    rows = min(SWA_ROWS, s)
    per_step = rows // w
    qcols = SWA_HEADS * SWA_DIM
    nblk = s // w

    def body(sink_ref, q_ref, k_ref, v_ref, do_ref, o_ref, lse_ref, dq_ref, dkt_ref, dvt_ref, dsink_ref):
        g = pl.program_id(0)

        @pl.when(g == 0)
        def _():
            dkt_ref[...] = jnp.zeros_like(dkt_ref)
            dvt_ref[...] = jnp.zeros_like(dvt_ref)
            dsink_ref[...] = jnp.zeros_like(dsink_ref)

        for ii in range(per_step):
            i = g * per_step + ii
            rs = slice(ii * w, (ii + 1) * w)
            r0, b0, dist, valid = _swa_geometry(i)
            j0 = jnp.maximum(i - 1, 0)
            kb = k_ref[pl.ds(b0, 2 * w), :]
            vb = v_ref[pl.ds(b0, 2 * w), :]
            dq_tiles = []
            for grp in range(SWA_KV_HEADS):
                heads = [SWA_GROUP * grp + a for a in range(SWA_GROUP)]
                qs32 = _swa_stack(q_ref, rs, grp)
                dos32 = _swa_stack(do_ref, rs, grp)
                delta = jnp.sum(dos32 * _swa_stack(o_ref, rs, grp), axis=-1, keepdims=True)
                lse = jnp.concatenate([lse_ref[h, rs, :] for h in heads], axis=0)
                sink = _swa_head_column([sink_ref[0, h] for h in heads])
                p = jnp.exp(_swa_logits(qs32.astype(BF16), kb, dist, valid, grp) - lse)
                dp = lax.dot_general(dos32.astype(BF16), vb, _NT, preferred_element_type=F32)
                ds = p * (dp - delta)
                dsb = ds.astype(BF16)
                d_sink = jnp.exp(sink - lse) * delta
                for a, h in enumerate(heads):
                    dsink_ref[h:h + 1, :] += jnp.broadcast_to(-jnp.sum(d_sink[a * w:(a + 1) * w]), (1, LANES))
                dvt = jnp.dot(dos32.T.astype(BF16), p.astype(BF16), preferred_element_type=F32)
                dkt = jnp.dot(qs32.T.astype(BF16), dsb, preferred_element_type=F32) * SWA_SCALE
                dvt_ref[j0] += dvt[:, :w]
                dvt_ref[j0 + 1] += dvt[:, w:]
                dkt_ref[j0] += dkt[:, :w]
                dkt_ref[j0 + 1] += dkt[:, w:]
                dq_tiles += _swa_unstack(jnp.dot(dsb, kb, preferred_element_type=F32) * SWA_SCALE, grp)
            dq_ref[rs, :] = _swa_merge_heads(dq_tiles)

    return pl.pallas_call(
        body, name=name, grid=(s // rows,),
        in_specs=[pl.BlockSpec(memory_space=pltpu.SMEM),
                  pl.BlockSpec((rows, qcols), lambda g: (g, 0)),
                  pl.BlockSpec((s, LANES), lambda g: (0, 4)), pl.BlockSpec((s, LANES), lambda g: (0, 5)),
                  pl.BlockSpec((rows, qcols), lambda g: (g, 0)), pl.BlockSpec((rows, qcols), lambda g: (g, 0)),
                  pl.BlockSpec((SWA_HEADS, rows, 1), lambda g: (0, g, 0))],
        out_specs=[pl.BlockSpec((rows, qcols), lambda g: (g, 0)),
                   pl.BlockSpec((nblk, LANES, w), lambda g: (0, 0, 0)),
                   pl.BlockSpec((nblk, LANES, w), lambda g: (0, 0, 0)),
                   pl.BlockSpec((SWA_HEADS, LANES), lambda g: (0, 0))],
        out_shape=[jax.ShapeDtypeStruct((s, qcols), F32),
                   jax.ShapeDtypeStruct((nblk, LANES, w), F32), jax.ShapeDtypeStruct((nblk, LANES, w), F32),
                   jax.ShapeDtypeStruct((SWA_HEADS, LANES), F32)],
        compiler_params=_params(("arbitrary",)),
    )(sinks, z0b, z0b, z0b, do, o, lse)


CUM_T = 256


def _split3(x):
    hi = x.astype(BF16)
    r1 = x - hi.astype(F32)
    mid = r1.astype(BF16)
    lo = (r1 - mid.astype(F32)).astype(BF16)
    return hi, mid, lo


def _tri_dot(tri, x):
    hi, mid, lo = _split3(x)
    out = jnp.dot(tri, hi, preferred_element_type=F32)
    out = out + jnp.dot(tri, mid, preferred_element_type=F32)
    return out + jnp.dot(tri, lo, preferred_element_type=F32)


def _logf_fwd(zf, bf, *, name):
    s = zf.shape[0]
    t = CUM_T
    nb = s // t

    def body(z_ref, b_ref, c_ref, carry_ref):
        i = pl.program_id(0)

        @pl.when(i == 0)
        def _():
            carry_ref[...] = jnp.zeros_like(carry_ref)

        x = z_ref[...] + b_ref[...]
        lf = jnp.minimum(x, 0.0) - jnp.log(1.0 + jnp.exp(-jnp.abs(x)))
        row = lax.broadcasted_iota(jnp.int32, (t, t), 0)
        col = lax.broadcasted_iota(jnp.int32, (t, t), 1)
        tri = jnp.where(col <= row, 1.0, 0.0).astype(BF16)
        c = _tri_dot(tri, lf) + carry_ref[...]
        c_ref[...] = c
        carry_ref[...] = c[t - 1:t, :]

    return pl.pallas_call(
        body, name=name, grid=(nb,),
        in_specs=[pl.BlockSpec((t, LANES), lambda i: (i, 0)), pl.BlockSpec((1, LANES), lambda i: (0, 0))],
        out_specs=pl.BlockSpec((t, LANES), lambda i: (i, 0)),
        out_shape=jax.ShapeDtypeStruct((s, LANES), F32),
        scratch_shapes=[pltpu.VMEM((1, LANES), F32)],
        compiler_params=_params(("arbitrary",)),
    )(zf, bf)


def _logf_bwd(dc, zf, bf, *, name):
    s = zf.shape[0]
    t = CUM_T
    nb = s // t

    def body(dc_ref, z_ref, b_ref, dz_ref, db_ref, carry_ref):
        i = pl.program_id(0)

        @pl.when(i == 0)
        def _():
            carry_ref[...] = jnp.zeros_like(carry_ref)
            db_ref[...] = jnp.zeros_like(db_ref)

        row = lax.broadcasted_iota(jnp.int32, (t, t), 0)
        col = lax.broadcasted_iota(jnp.int32, (t, t), 1)
        tri = jnp.where(col >= row, 1.0, 0.0).astype(BF16)
        dlf = _tri_dot(tri, dc_ref[...]) + carry_ref[...]
        carry_ref[...] = dlf[0:1, :]
        x = z_ref[...] + b_ref[...]
        dz = dlf * _sigmoid(-x)
        dz_ref[...] = dz.astype(BF16)
        db_ref[...] += jnp.sum(dz, axis=0, keepdims=True)

    return pl.pallas_call(
        body, name=name, grid=(nb,),
        in_specs=[pl.BlockSpec((t, LANES), lambda i: (nb - 1 - i, 0)), pl.BlockSpec((t, LANES), lambda i: (nb - 1 - i, 0)),
                  pl.BlockSpec((1, LANES), lambda i: (0, 0))],
        out_specs=[pl.BlockSpec((t, LANES), lambda i: (nb - 1 - i, 0)), pl.BlockSpec((1, LANES), lambda i: (0, 0))],
        out_shape=[jax.ShapeDtypeStruct((s, LANES), BF16), jax.ShapeDtypeStruct((1, LANES), F32)],
        scratch_shapes=[pltpu.VMEM((1, LANES), F32)],
        compiler_params=_params(("arbitrary",)),
    )(dc, zf, bf)


def _sum_pieces(p_ref):
    g = p_ref[0].astype(F32)
    for k in range(1, N_DEV):
        g = g + p_ref[k].astype(F32)
    return g


def _adam_update(g, w, m, v):
    bc1 = 1.0 - ADAM_B1 ** ADAM_STEP
    bc2 = 1.0 - ADAM_B2 ** ADAM_STEP
    nm = ADAM_B1 * m + (1.0 - ADAM_B1) * g
    nv = ADAM_B2 * v + (1.0 - ADAM_B2) * (g * g)
    m_hat = nm / bc1
    v_hat = nv / bc2
    return -ADAM_LR * (m_hat / (jnp.sqrt(v_hat) + ADAM_EPS) + ADAM_WD * w), nm, nv


def _adamw(pieces, w, m, v, *, name):
    rows, cols = w.shape
    tr = _tile(rows, (RB1, RB0, SMALL_ROWS))

    def body(p_ref, w_ref, m_ref, v_ref, g_ref, d_ref, nm_ref, nv_ref):
        g = _sum_pieces(p_ref)
        g_ref[...] = g
        d_ref[...], nm_ref[...], nv_ref[...] = _adam_update(g, w_ref[...], m_ref[...], v_ref[...])

    spec = pl.BlockSpec((tr, cols), lambda i: (i, 0))
    shape = jax.ShapeDtypeStruct((rows, cols), F32)
    return pl.pallas_call(
        body, name=name, grid=(rows // tr,),
        in_specs=[pl.BlockSpec((N_DEV, tr, cols), lambda i: (0, i, 0)), spec, spec, spec],
        out_specs=[spec, spec, spec, spec], out_shape=[shape, shape, shape, shape],
        compiler_params=_params(("parallel",)),
    )(pieces, w, m, v)


def _sum8(pieces, rows, *, name):
    cols = pieces.shape[2]
    tr = _tile(rows, (176, 96))

    def body(p_ref, g_ref):
        g_ref[...] = _sum_pieces(p_ref)

    return pl.pallas_call(
        body, name=name, grid=(rows // tr,),
        in_specs=[pl.BlockSpec((N_DEV, tr, cols), lambda i: (0, i, 0))],
        out_specs=pl.BlockSpec((tr, cols), lambda i: (i, 0)),
        out_shape=jax.ShapeDtypeStruct((rows, cols), F32),
        compiler_params=_params(("parallel",)),
    )(pieces)


def _adamw_columns(g, w, m, v, *, name):
    n, _, k = w.shape
    tr = n // 2

    def body(g_ref, w_ref, m_ref, v_ref, d_ref, nm_ref, nv_ref):
        d_ref[...], nm_ref[...], nv_ref[...] = _adam_update(g_ref[...], w_ref[...], m_ref[...], v_ref[...])

    spec = pl.BlockSpec((tr, 1, k), lambda i: (i, 0, 0))
    shape = jax.ShapeDtypeStruct((n, 1, k), F32)
    return pl.pallas_call(
        body, name=name, grid=(n // tr,), in_specs=[spec, spec, spec, spec],
        out_specs=[spec, spec, spec], out_shape=[shape, shape, shape],
        compiler_params=_params(("parallel",)),
    )(g, w, m, v)


MESH = pl.DeviceIdType.MESH
ANY = pl.BlockSpec(memory_space=pl.ANY)


def _all_gather(shard, *, name):
    rows, lanes = shard.shape

    def body(x_ref, out_ref, send_sems, recv_sems, local_sem):
        x, y, c = lax.axis_index("x"), lax.axis_index("y"), lax.axis_index("c")
        me, sibling = (x, y, c), (x, y, 1 - c)
        chips = [(1 - x, y), (x, 1 - y), (1 - x, 1 - y)]

        def block(px, py, pc):
            return out_ref.at[4 * px + 2 * py + pc]

        def copy(k, blk, to, src=None):
            return pltpu.make_async_remote_copy(
                src_ref=block(*blk) if src is None else src, dst_ref=block(*blk),
                send_sem=send_sems.at[k], recv_sem=recv_sems.at[k], device_id=to, device_id_type=MESH)

        mine = pltpu.make_async_copy(x_ref, block(*me), local_sem)
        mine.start()
        first = [copy(0, me, sibling, src=x_ref)]
        first += [copy(1 + j, me, (*chip, c), src=x_ref) for j, chip in enumerate(chips)]
        for cp in first:
            cp.start()
        passed = [copy(4 + j, (*chip, c), sibling) for j, chip in enumerate(chips)]
        for j, chip in enumerate(chips):
            copy(1 + j, (*chip, c), me).wait_recv()
            passed[j].start()
        copy(0, sibling, me).wait_recv()
        for j, chip in enumerate(chips):
            copy(4 + j, (*chip, 1 - c), me).wait_recv()
        for cp in first + passed:
            cp.wait_send()
        mine.wait()

    return pl.pallas_call(
        body, name=name, out_shape=jax.ShapeDtypeStruct((N_DEV, rows, lanes), shard.dtype),
        in_specs=[ANY], out_specs=ANY,
        scratch_shapes=[pltpu.SemaphoreType.DMA((7,)), pltpu.SemaphoreType.DMA((7,)), pltpu.SemaphoreType.DMA(())],
    )(shard)


def _peer_copies(kind, src_ref, out_ref, send_sems, recv_sems, local_sem):
    x, y, c = lax.axis_index("x"), lax.axis_index("y"), lax.axis_index("c")
    me = 4 * x + 2 * y + c

    def src(idx):
        return src_ref.at[idx] if kind == "exchange" else src_ref

    mine = None if local_sem is None else pltpu.make_async_copy(src(me), out_ref.at[me], local_sem)
    copies = []
    for r in (2, 4, 6) if kind == "across" else range(1, N_DEV):
        px = 1 - x if r & 4 else x
        py = 1 - y if r & 2 else y
        pc = 1 - c if r & 1 else c
        copies.append(pltpu.make_async_remote_copy(
            src_ref=src(4 * px + 2 * py + pc), dst_ref=out_ref.at[me],
            send_sem=send_sems.at[r - 1], recv_sem=recv_sems.at[r - 1],
            device_id=(px, py, pc), device_id_type=MESH))
    return mine, copies


def _to_other_core(shard, land, *, name):
    def body(src_ref, land_ref, out_ref, send_sems, recv_sems):
        x, y, c = lax.axis_index("x"), lax.axis_index("y"), lax.axis_index("c")
        copies = []
        for k, r in enumerate((0, 2, 4, 6)):
            slot = 4 * (1 - x if r & 4 else x) + 2 * (1 - y if r & 2 else y) + c
            copies.append(pltpu.make_async_remote_copy(
                src_ref=src_ref if r == 0 else land_ref.at[slot], dst_ref=out_ref.at[slot],
                send_sem=send_sems.at[k], recv_sem=recv_sems.at[k], device_id=(x, y, 1 - c), device_id_type=MESH))
        for cp in copies:
            cp.start()
        for cp in copies:
            cp.wait()

    return pl.pallas_call(
        body, name=name, out_shape=jax.ShapeDtypeStruct(land.shape, land.dtype), in_specs=[ANY, ANY], out_specs=ANY,
        input_output_aliases={1: 0}, scratch_shapes=[pltpu.SemaphoreType.DMA((4,)), pltpu.SemaphoreType.DMA((4,))],
    )(shard, land)


PEER_SEMS = [pltpu.SemaphoreType.DMA((7,)), pltpu.SemaphoreType.DMA((7,)), pltpu.SemaphoreType.DMA(())]


HBM = pl.BlockSpec(memory_space=pltpu.HBM)
SEMAPHORES = pl.BlockSpec(memory_space=pltpu.SEMAPHORE)


def _peer_start(kind, arr, *, name):
    land = lax.empty((N_DEV,) + arr.shape[-2:], arr.dtype)

    def body(src_ref, land_ref, send_sems, recv_sems, src_thru, land_thru, token):
        _, copies = _peer_copies(kind, src_ref, land_ref, send_sems, recv_sems, None)
        for cp in copies:
            cp.start()
        token[...] = jnp.zeros_like(token)

    return pl.pallas_call(
        body, name=name,
        out_shape=(pltpu.SemaphoreType.DMA((N_DEV - 1,)), pltpu.SemaphoreType.DMA((N_DEV - 1,)),
                   pltpu.HBM(arr.shape, arr.dtype), pltpu.HBM(land.shape, land.dtype), jax.ShapeDtypeStruct((8, LANES), F32)),
        in_specs=(HBM, HBM), out_specs=(SEMAPHORES, SEMAPHORES, HBM, HBM, pl.BlockSpec(memory_space=pltpu.VMEM)),
        input_output_aliases={0: 2, 1: 3},
        compiler_params=pltpu.CompilerParams(has_side_effects=pltpu.SideEffectType.DATAFLOW_SIDE_EFFECTING),
    )(pltpu.with_memory_space_constraint(arr, pltpu.HBM), pltpu.with_memory_space_constraint(land, pltpu.HBM))


def _peer_wait(kind, send_sems, recv_sems, src_thru, land_thru, after, *, name):
    def body(src_ref, land_ref, send_sems, recv_sems, *_):
        _, copies = _peer_copies(kind, src_ref, land_ref, send_sems, recv_sems, None)
        for cp in copies:
            cp.wait_send()
            cp.wait_recv()

    return pl.pallas_call(
        body, name=name,
        out_shape=(pltpu.HBM(src_thru.shape, src_thru.dtype), pltpu.HBM(land_thru.shape, land_thru.dtype)),
        in_specs=(HBM, HBM, SEMAPHORES, SEMAPHORES) + (ANY,) * len(after), out_specs=(HBM, HBM),
        input_output_aliases={0: 0, 1: 1},
        compiler_params=pltpu.CompilerParams(has_side_effects=pltpu.SideEffectType.DATAFLOW_SIDE_EFFECTING),
    )(src_thru, land_thru, send_sems, recv_sems, *after)


def _add_rider(rider, in_specs, args, out_specs, out_shape):
    if rider is None:
        return []
    _, arr = rider
    in_specs.append(ANY)
    args.append(arr)
    out_specs.append(ANY)
    out_shape.append(jax.ShapeDtypeStruct((N_DEV,) + arr.shape[-2:], arr.dtype))
    return list(PEER_SEMS)


def _split_rider(refs, rider, n_in, n_out):
    if rider is None:
        return refs, None
    refs = list(refs)
    rin = refs.pop(n_in)
    rout = refs.pop(n_in + n_out)
    return refs[:-3], (rin, rout, *refs[-3:])


def _ride_start(rider, ride_refs, first):
    if rider is None:
        return

    @pl.when(first)
    def _():
        mine, copies = _peer_copies(rider[0], *ride_refs)
        mine.start()
        for cp in copies:
            cp.start()


def _ride_wait(rider, ride_refs, last):
    if rider is None:
        return

    @pl.when(last)
    def _():
        mine, copies = _peer_copies(rider[0], *ride_refs)
        for cp in copies:
            cp.wait()
        mine.wait()


def _gathered_cols(blocks, kdim):
    n = blocks.shape[1] * WIDE // kdim
    return blocks.reshape(N_DEV, kdim, n).transpose(1, 0, 2).reshape(kdim, N_DEV * n)


def _scatter_cols(dw):
    kdim, n8 = dw.shape
    n = n8 // N_DEV
    return dw.reshape(kdim, N_DEV, n).transpose(1, 0, 2).reshape(N_DEV, kdim * n // WIDE, WIDE)


def _pad_rows(a, rows):
    pad = [(0, 0)] * a.ndim
    pad[-2] = (0, rows - a.shape[-2])
    return jnp.pad(a, pad)


def _layer0_in_weight_t(wt):
    cq, ckv, kpe = wt[0:256], wt[256:384], wt[384:416]
    q_s, k_s, v_s, gate = wt[416:928], wt[928:1056], wt[1056:1184], wt[1184:2208]
    z = jnp.zeros((64, wt.shape[1]), wt.dtype)
    return jnp.concatenate([gate, cq, ckv, z, kpe, z[:32], q_s, k_s, v_s], axis=0)


def _layer0_in_grad_t(dwt):
    gate, cq, ckv, kpe = dwt[0:1024], dwt[1024:1280], dwt[1280:1408], dwt[1472:1504]
    q_s, k_s, v_s = dwt[1536:2048], dwt[2048:2176], dwt[2176:2304]
    return jnp.concatenate([cq, ckv, kpe, q_s, k_s, v_s, gate], axis=0)


L0_BLOCKS = ((256, 1024), (128, 1280), (32, 1472), (512, 1536), (128, 2048), (128, 2176), (1024, 0))


def _layer0_in_unpack(gath, *, name):
    total = (Z0A_UNITS + Z0B_UNITS) * LANES

    def body(g_ref, w_ref):
        w_ref[1408:1472, :] = jnp.zeros((64, WIDE), w_ref.dtype)
        w_ref[1504:1536, :] = jnp.zeros((32, WIDE), w_ref.dtype)
        for p in range(N_DEV):
            lo, hi, at = p * N_E_IN, (p + 1) * N_E_IN, 0
            for rows, first in L0_BLOCKS:
                start, stop = max(lo, at), min(hi, at + rows)
                if start < stop:
                    w_ref[first + start - at:first + stop - at, :] = g_ref[p, start - lo:stop - lo, :]
                at += rows

    return pl.pallas_call(
        body, name=name, grid=(1,), in_specs=[_resident((N_DEV, RA0, WIDE), lambda i: (0, 0, 0))],
        out_specs=_resident((total, WIDE), lambda i: (0, 0)), out_shape=jax.ShapeDtypeStruct((total, WIDE), gath.dtype),
        compiler_params=_params(("arbitrary",)),
    )(gath)


def _early_grads_pack(d_w0t, d_q, d_kv, *, name):
    def body(w_ref, q_ref, kv_ref, out_ref):
        for p in range(N_DEV):
            lo, hi, at = p * N_E_IN, (p + 1) * N_E_IN, 0
            for rows, first in L0_BLOCKS:
                start, stop = max(lo, at), min(hi, at + rows)
                if start < stop:
                    out_ref[p, start - lo:stop - lo, :] = w_ref[first + start - at:first + stop - at, :]
                at += rows
            out_ref[p, N_E_IN:RA0, :] = jnp.zeros((RA0 - N_E_IN, WIDE), out_ref.dtype)
            out_ref[p, RA0:RA0 + 32, :] = q_ref[p]
            out_ref[p, RA0 + 32:, :] = kv_ref[p]

    arrays = (d_w0t, d_q, d_kv)
    return pl.pallas_call(
        body, name=name, grid=(1,), in_specs=[_resident(a.shape, lambda i, n=a.ndim: (0,) * n) for a in arrays],
        out_specs=_resident((N_DEV, RA0 + RB0, WIDE), lambda i: (0, 0, 0)),
        out_shape=jax.ShapeDtypeStruct((N_DEV, RA0 + RB0, WIDE), BF16), compiler_params=_params(("arbitrary",)),
    )(*arrays)


def _layer1_in_weight_t(wt):
    main = jnp.concatenate([wt[:3 * D_MODEL], wt[3 * D_MODEL + FOX_HEADS:]], axis=0)
    return main, _pad_rows(wt[3 * D_MODEL:3 * D_MODEL + FOX_HEADS], LANES)


def _layer1_in_unpack(gath, *, name):
    n_main = 3 * D_MODEL

    def body(g_ref, w_ref, f_ref, o1_ref, o0_ref):
        f_ref[...] = jnp.zeros_like(f_ref)
        for p in range(N_DEV):
            o1_ref[128 * p:128 * p + 128, :] = g_ref[p, RA1:RA1 + 128, :]
            o0_ref[128 * p:128 * p + 128, :] = g_ref[p, RA1 + 128:RA1 + 256, :]
            lo, hi = p * N_O_IN, (p + 1) * N_O_IN
            for ref, first, start, stop in ((w_ref, 0, lo, min(hi, n_main)),
                                            (f_ref, -n_main, max(lo, n_main), min(hi, n_main + FOX_HEADS)),
                                            (w_ref, -FOX_HEADS, max(lo, n_main + FOX_HEADS), hi)):
                if start < stop:
                    ref[start + first:stop + first, :] = g_ref[p, start - lo:stop - lo, :]

    return pl.pallas_call(
        body, name=name, grid=(1,), in_specs=[_resident((N_DEV, RA1 + 256, WIDE), lambda i: (0, 0, 0))],
        out_specs=[_resident((rows, WIDE), lambda i: (0, 0)) for rows in (n_main + D_MODEL, LANES, D_MODEL, D_MODEL)],
        out_shape=[jax.ShapeDtypeStruct((rows, WIDE), gath.dtype) for rows in (n_main + D_MODEL, LANES, D_MODEL, D_MODEL)],
        compiler_params=_params(("arbitrary",)),
    )(gath)


def _late_grads_pack(d_qkv, d_wft, d_gate, d_wo1, d_wo0, d_o_g, *, name):
    n_main = 3 * D_MODEL
    arrays = (d_qkv, d_wft, d_gate, d_wo1, d_wo0, d_o_g)

    def body(q_ref, f_ref, g_ref, o1_ref, o0_ref, og_ref, out_ref):
        for p in range(N_DEV):
            lo, hi = p * N_O_IN, (p + 1) * N_O_IN
            for ref, first, start, stop in ((q_ref, 0, lo, min(hi, n_main)),
                                            (f_ref, -n_main, max(lo, n_main), min(hi, n_main + FOX_HEADS)),
                                            (g_ref, -n_main - FOX_HEADS, max(lo, n_main + FOX_HEADS), hi)):
                if start < stop:
                    out_ref[p, start - lo:stop - lo, :] = ref[start + first:stop + first, :]
            out_ref[p, N_O_IN:RA1, :] = jnp.zeros((RA1 - N_O_IN, WIDE), out_ref.dtype)
            out_ref[p, RA1:RA1 + 128, :] = o1_ref[128 * p:128 * p + 128, :]
            out_ref[p, RA1 + 128:RA1 + 256, :] = o0_ref[128 * p:128 * p + 128, :]
            out_ref[p, RA1 + 256:, :] = og_ref[p]

    return pl.pallas_call(
        body, name=name, grid=(1,), in_specs=[_resident(a.shape, lambda i, n=a.ndim: (0,) * n) for a in arrays],
        out_specs=_resident((N_DEV, RA1 + RB1, WIDE), lambda i: (0, 0, 0)),
        out_shape=jax.ShapeDtypeStruct((N_DEV, RA1 + RB1, WIDE), BF16), compiler_params=_params(("arbitrary",)),
    )(*arrays)


def _q_up_weight(w):
    return jnp.pad(w.reshape(MLA_Q_RANK, MLA_HEADS, 96), ((0, 0), (0, 0), (0, 32))).reshape(MLA_Q_RANK, MLA_HEADS * LANES)


def _q_up_grad(dwp):
    return dwp.reshape(MLA_Q_RANK, MLA_HEADS, LANES)[:, :, :96].reshape(MLA_Q_RANK, MLA_HEADS * 96)


def _kv_up_weight(w):
    w4 = w.reshape(MLA_KV_RANK, MLA_HEADS, 2, 64)
    kp = jnp.pad(w4[:, :, 0, :], ((0, 0), (0, 0), (0, 64))).reshape(MLA_KV_RANK, MLA_HEADS * LANES)
    vp = w4[:, :, 1, :].reshape(MLA_KV_RANK, MLA_HEADS * 64)
    return jnp.concatenate([kp, vp], axis=1)


def _kv_up_grad(dwp):
    dk = dwp[:, :MLA_HEADS * LANES].reshape(MLA_KV_RANK, MLA_HEADS, LANES)[:, :, :64]
    dv = dwp[:, MLA_HEADS * LANES:].reshape(MLA_KV_RANK, MLA_HEADS, 64)
    return jnp.stack([dk, dv], axis=2).reshape(MLA_KV_RANK, MLA_HEADS * LANES)


def _pad_lanes(a):
    return jnp.pad(a, ((0, 0), (0, LANES - a.shape[1])))


def _small_pack(g_in, g_final, g_q_a, g_kv_a, sinks, b_f, loss):
    rows = [g_in.reshape(8, LANES), g_final.reshape(8, LANES), g_q_a.reshape(2, LANES), g_kv_a.reshape(1, LANES),
            _pad_lanes(sinks.reshape(1, -1)), _pad_lanes(b_f.reshape(1, -1)), _pad_lanes(loss.reshape(1, 1)),
            jnp.zeros((2, LANES), F32)]
    return jnp.concatenate(rows, axis=0)


def _small_unpack(a):
    return (a[0:8].reshape(1, D_MODEL), a[8:16].reshape(D_MODEL), a[16:18].reshape(1, MLA_Q_RANK),
            a[18:19].reshape(1, MLA_KV_RANK), a[19:20, :SWA_HEADS], a[20:21, :FOX_HEADS], a[21, 0])


def _local_step(x, positions, target, e_g_in, early, e_g_q_a, e_g_kv_a, e_sinks,
                late, o_b_f, g_final, scatter1=None, scatter0=None):
    s = x.shape[0]
    mla_scale = (MLA_NOPE + MLA_ROPE) ** -0.5
    fox_scale = FOX_DIM ** -0.5
    n0a = Z0A_UNITS * LANES

    inv_freq = 1.0 / (ROPE_THETA ** (jnp.arange(0, MLA_ROPE, 2, dtype=F32) / MLA_ROPE))
    ang = positions.astype(F32)[:, None] * inv_freq
    cos, sin = jnp.cos(ang), jnp.sin(ang)
    ones, zeros = jnp.ones((s, 64), F32), jnp.zeros((s, 64), F32)
    cos_t = jnp.concatenate([ones, cos, cos, ones[:, :32]], axis=1)
    sin_t = jnp.concatenate([zeros, -sin, sin, zeros[:, :32]], axis=1)
    cos_t, sin_t = lax.optimization_barrier((cos_t, sin_t))

    if len(early) == 3:
        h0 = _rmsnorm_fwd(x, e_g_in, width=D_MODEL, col_blk=0, name="l0_norm")
        w0t, wq, wkv = early
    else:
        pending, token, unpack, prep = early
        h0 = _rmsnorm_fwd(x, e_g_in, width=D_MODEL, col_blk=0, name="l0_norm", after=[token])
        sent, across = _peer_wait("across", *pending, after=[h0] + prep, name="weights0_wait")
        w0t, wq, wkv = unpack(sent, _to_other_core(sent, across, name="weights0_over"))
    z0a, z0b = _matmul_rows([(h0, w0t, True)], [], [], lambda r: (r[:, :n0a], r[:, n0a:]),
                            [("rows", n0a, F32), ("rows", Z0B_UNITS * LANES, BF16)], name="l0_in")
    cqn = _rmsnorm_fwd(z0a, e_g_q_a, width=MLA_Q_RANK, col_blk=4, name="l0_q_norm")
    ckvn = _rmsnorm_fwd(z0a, e_g_kv_a, width=MLA_KV_RANK, col_blk=10, name="l0_kv_norm")
    rope_rows = [(cos_t, LANES, 0), (sin_t, LANES, 0)]
    qm, = _matmul_rows([(cqn, wq, False)], rope_rows, [], _rope_q_epilogue, [("rows", MLA_HEADS * LANES, BF16)],
                       name="l0_q_up")
    kvm, = _matmul_rows([(ckvn, wkv, False)], [(z0a, LANES, 11)] + rope_rows, [], _rope_k_epilogue,
                        [("rows", MLA_HEADS * (LANES + MLA_V), BF16)], name="l0_kv_up")
    gathers = len(late) == 2
    res = _flash_fwd(qm, kvm, kvm, None, n_pairs=MLA_HEADS // 2, hw=LANES, q_off=0, k_off=0, v_off=MLA_HEADS,
                     scale=mla_scale, name="l0_mla_fwd", rider=("gather", late[0]) if gathers else None)
    o_mla, lse_mla = res[0], res[1]
    wo0, o_g_in, w1t, wft, wo1 = late[1](res[2]) if gathers else late
    o_swa, lse_swa = _swa_fwd(z0b, e_sinks, name="l0_swa_fwd")
    half = D_MODEL // 2

    x1, h1, og0 = _matmul_rows(
        [(None, wo0, False)], [(o_mla, half, 0), (o_swa, half, 0), (z0a, D_MODEL, 0), (x, D_MODEL, 0)], [o_g_in],
        lambda r, om, osw, gt, xt, g, made: (*_residual_norm_epilogue(r, xt, g), made),
        [("rows", D_MODEL, F32), ("rows", D_MODEL, BF16), ("rows", D_MODEL, BF16)], name="l0_out",
        prologue=lambda om, osw, gt, xt, g: _gated([om, osw], gt))
    z1, gate1, zf = _matmul_rows(
        [(None, w1t, True), (None, wft, True)], [(h1, D_MODEL, 0)], [],
        lambda r, h, made: (r[0][:, :3 * D_MODEL], r[0][:, 3 * D_MODEL:], r[1]),
        [("rows", 3 * D_MODEL, BF16), ("rows", D_MODEL, F32), ("rows", LANES, F32)], name="l1_in",
        prologue=lambda h: h, separate=True)
    bf = _pad_lanes(o_b_f)
    log_cum = _logf_fwd(zf, bf, name="l1_logf")
    bias2 = (-LOG2E * log_cum[:, :FOX_HEADS]).T
    t_bwd = min(ATT_T, s)
    bias = bias2.reshape(FOX_HEADS // 2, 2, s // t_bwd, 1, t_bwd)
    t_fwd = _fwd_tile(s)
    o_fox, lse_fox = _flash_fwd(z1, z1, z1, bias2.reshape(FOX_HEADS // 2, 2, s // t_fwd, 1, t_fwd),
                                n_pairs=FOX_HEADS // 2, hw=64, q_off=0, k_off=8, v_off=16, scale=fox_scale,
                                name="l1_fox_fwd")

    dx2, loss_part, d_g_final, og1, dx2_bf = _matmul_rows(
        [(None, wo1, False)], [(o_fox, D_MODEL, 0), (gate1, D_MODEL, 0), (x1, D_MODEL, 0), (target, D_MODEL, 0)],
        [g_final.reshape(1, D_MODEL)],
        lambda r, o, gt, xt, tg, g, made: _and_first(_loss_epilogue(r, xt, tg, g), made),
        [("rows", D_MODEL, F32), ("sum", (8, LANES)), ("sum", (1, D_MODEL)), ("rows", D_MODEL, BF16),
         ("rows", D_MODEL, BF16)], name="l1_out_loss", prologue=lambda o, gt, xt, tg, g: _gated([o], gt))

    d_wo1 = _matmul(og1, dx2_bf, ta=True, out_dtype=BF16, name="l1_out_dw")
    do_fox, d_gate1 = _matmul_rows([(dx2_bf, wo1, True)], [(o_fox, D_MODEL, 0), (gate1, D_MODEL, 0)], [],
                                   _gate_bwd_epilogue([D_MODEL]), [("rows", D_MODEL, F32), ("rows", D_MODEL, BF16)],
                                   name="l1_out_dx")
    dqkv1, dbias, drow = _flash_bwd(z1, z1, z1, do_fox, o_fox, lse_fox, bias, n_pairs=FOX_HEADS // 2, hw=64, q_off=0,
                                    k_off=8, v_off=16, scale=fox_scale, qk_dtype=BF16, stacked=True, name="l1_fox_bwd")
    d_log_cum = (drow.reshape(FOX_HEADS, s) - dbias.reshape(FOX_HEADS, s)).T
    d_log_cum = jnp.pad(d_log_cum, ((0, 0), (0, LANES - FOX_HEADS)))
    d_zf, d_bf = _logf_bwd(d_log_cum, zf, bf, name="l1_logf_bwd")
    d_w1t = (_matmul(dqkv1, h1, ta=True, out_dtype=BF16, name="l1_in_dw_qkv"),
             _matmul(d_gate1, h1, ta=True, out_dtype=BF16, name="l1_in_dw_gate"))
    d_wft = _matmul(d_zf, h1, ta=True, out_dtype=BF16, name="l1_in_f_dw")
    dx1, d_o_g_in, dx1_bf = _matmul_rows([(dqkv1, w1t, False, c * D_MODEL, c) for c in range(3)]
                                         + [(d_gate1, w1t, False, 3 * D_MODEL), (d_zf, wft, False)],
                                         [(x1, D_MODEL, 0), (dx2, D_MODEL, 0)], [o_g_in],
                                         lambda *a: _and_first(_rms_bwd_epilogue(*a)),
                                         [("rows", D_MODEL, F32), ("sum", (1, D_MODEL)), ("rows", D_MODEL, BF16)],
                                         name="l1_in_dx")

    d_wo0 = _matmul(og0, dx1_bf, ta=True, out_dtype=BF16, name="l0_out_dw")
    do_mla, do_swa, d_gate0 = _matmul_rows(
        [(dx1_bf, wo0, True)], [(o_mla, half, 0), (o_swa, half, 0), (z0a, D_MODEL, 0)], [], _gate_bwd_epilogue([half, half]),
        [("rows", half, F32), ("rows", half, F32), ("rows", D_MODEL, BF16)], name="l0_out_dx")
    dq_s, dkt_s, dvt_s, d_sinks = _swa_bwd(z0b, e_sinks, do_swa, o_swa, lse_swa, name="l0_swa_bwd")
    dk_s = dkt_s.transpose(0, 2, 1).reshape(s, LANES)
    dv_s = dvt_s.transpose(0, 2, 1).reshape(s, LANES)
    rider = None
    if scatter1 is not None:
        rider = ("exchange", scatter1(dict(w1t=d_w1t, wft=d_wft, wo1=d_wo1, o_g_in=d_o_g_in, wo0=d_wo0)))
    res = _flash_bwd(qm, kvm, kvm, do_mla, o_mla, lse_mla, None, n_pairs=MLA_HEADS // 2, hw=LANES, q_off=0, k_off=0,
                     v_off=MLA_HEADS, scale=mla_scale, qk_dtype=F32, name="l0_mla_bwd", rider=rider)
    dqm, dkm, dvm = res[0], res[1], res[2]
    recv1 = res[3] if rider is not None else None
    d_qp, d_kvp, d_kpe = _rope_bwd(dqm, dkm, dvm, cos_t, sin_t, name="l0_rope_bwd")
    d_wq = _matmul(cqn, d_qp, ta=True, out_dtype=BF16, name="l0_q_up_dw")
    d_cqn = _matmul(d_qp, wq, tb=True, name="l0_q_up_dx")
    d_wkv = _matmul(ckvn, d_kvp, ta=True, out_dtype=BF16, name="l0_kv_up_dw")
    d_ckvn = _matmul(d_kvp, wkv, tb=True, name="l0_kv_up_dx")
    d_cq, d_g_q_a = _rmsnorm_bwd(z0a, e_g_q_a, d_cqn, width=MLA_Q_RANK, col_blk=4, name="l0_q_norm_bwd")
    d_ckv, d_g_kv_a = _rmsnorm_bwd(z0a, e_g_kv_a, d_ckvn, width=MLA_KV_RANK, col_blk=10, name="l0_kv_norm_bwd")
    dz0 = jnp.concatenate([d_gate0, d_cq, d_ckv, d_kpe, dq_s.astype(BF16), dk_s.astype(BF16), dv_s.astype(BF16)], axis=1)
    d_w0t = _matmul(dz0, h0, ta=True, out_dtype=BF16, name="l0_in_dw")
    pending0, after_start = None, []
    if scatter0 is not None:
        *pending0, token = _peer_start("exchange", scatter0(dict(w0t=d_w0t, wq=d_wq, wkv=d_wkv)), name="grads0_start")
        after_start = [token]
    grad_x, d_e_g_in = _matmul_rows(
        [(dz0, w0t, False)], [(x, D_MODEL, 0), (dx1, D_MODEL, 0)], [e_g_in] + after_start,
        lambda dy, xt, add, g, *_: _rms_bwd_epilogue(dy, xt, add, g),
        [("rows", D_MODEL, F32), ("sum", (1, D_MODEL))], name="l0_in_dx")

    return dict(pending0=pending0, recv1=recv1, loss=loss_part[0, 0], grad_x=grad_x, e_g_in=d_e_g_in, w0t=d_w0t, e_g_q_a=d_g_q_a, wq=d_wq,
                e_g_kv_a=d_g_kv_a, wkv=d_wkv, e_sinks=d_sinks[:, 0].reshape(1, SWA_HEADS), wo0=d_wo0,
                o_g_in=d_o_g_in, w1t=d_w1t, wft=d_wft, o_b_f=d_bf[:, :FOX_HEADS], wo1=d_wo1, g_final=d_g_final.reshape(D_MODEL))


def _wide(a, rows):
    flat = a.reshape(-1)
    return jnp.pad(flat, (0, rows * WIDE - flat.shape[0])).reshape(rows, WIDE)


def _rows_b0(w_q, w_kv):
    return jnp.concatenate([_wide(w_q, 32), _wide(w_kv, 16)], axis=0)


def _unflat_b0(f):
    return f[0:24].reshape(1, MLA_Q_RANK, 96), f[32:48].reshape(1, MLA_KV_RANK, 128)


def _rows_b1(o_w_out, e_w_out, g_in):
    return jnp.concatenate([o_w_out, e_w_out, _wide(g_in, 16)], axis=0)


def _unflat_b1(f):
    return f[0:128][None], f[128:256][None], f[256:257, :LANES]


def kernel(x, positions, e_g_in, e_w_in, e_g_q_a, e_w_q_up, e_g_kv_a, e_w_kv_up, e_sinks, e_w_out, o_g_in, o_w_in, o_b_f, o_w_out, g_final, loss_target, m_e_g_in, m_e_w_in, m_e_g_q_a, m_e_w_q_up, m_e_g_kv_a, m_e_w_kv_up, m_e_sinks, m_e_w_out, m_o_g_in, m_o_w_in, m_o_b_f, m_o_w_out, m_g_final, v_e_g_in, v_e_w_in, v_e_g_q_a, v_e_w_q_up, v_e_g_kv_a, v_e_w_kv_up, v_e_sinks, v_e_w_out, v_o_g_in, v_o_w_in, v_o_b_f, v_o_w_out, v_g_final):
    def bf(a):
        return a.astype(BF16)

    me = 4 * lax.axis_index("x") + 2 * lax.axis_index("y") + lax.axis_index("c")
    shard0 = jnp.concatenate([_pad_rows(bf(e_w_in[0]).T, RA0), _rows_b0(bf(e_w_q_up[0]), bf(e_w_kv_up[0]))], axis=0)
    *pending_w0, token_w0 = _peer_start("across", shard0, name="weights0_start")

    def unpack0(sent, gath0):
        gath0 = lax.dynamic_update_slice_in_dim(gath0, sent[None], me, axis=0)
        w0t = _layer0_in_unpack(gath0, name="weights0_unpack")
        wq = _q_up_weight(_gathered_cols(gath0[:, RA0:RA0 + 24], MLA_Q_RANK))
        wkv = _kv_up_weight(_gathered_cols(gath0[:, RA0 + 32:RA0 + 48], MLA_KV_RANK))
        return w0t, wq, wkv

    rows_b0 = [_rows_b0(q[0], kv[0]) for q, kv in ((e_w_q_up, e_w_kv_up), (m_e_w_q_up, m_e_w_kv_up), (v_e_w_q_up, v_e_w_kv_up))]
    rows_b1 = [_rows_b1(o[0], e[0], g) for o, e, g in ((o_w_out, e_w_out, o_g_in), (m_o_w_out, m_e_w_out, m_o_g_in),
                                                       (v_o_w_out, v_e_w_out, v_o_g_in))]

    g_bits = lax.bitcast_convert_type(o_g_in.reshape(LANES), BF16)
    shard1 = jnp.concatenate([_pad_rows(bf(o_w_in[0]).T, RA1), _rows_b1(bf(o_w_out[0]), bf(e_w_out[0]), g_bits)], axis=0)

    def unpack1(gath1):
        w1t, wft, wo1, wo0 = _layer1_in_unpack(gath1, name="weights1_unpack")
        bits = gath1[:, RA1 + 256, :2 * LANES].reshape(N_DEV, LANES, 2)
        return wo0, lax.bitcast_convert_type(bits, F32).reshape(1, D_MODEL), w1t, wft, wo1

    def scatter1(g):
        d_o_g = jnp.pad(bf(g["o_g_in"]).reshape(N_DEV, 1, LANES), ((0, 0), (0, 15), (0, WIDE - LANES)))
        return _late_grads_pack(g["w1t"][0], g["wft"], g["w1t"][1], g["wo1"], g["wo0"], d_o_g, name="grads1_pack")

    def scatter0(g):
        return _early_grads_pack(g["w0t"], _pad_rows(_scatter_cols(_q_up_grad(g["wq"])), 32),
                                 _scatter_cols(_kv_up_grad(g["wkv"])), name="grads0_pack")

    gr = _local_step(x[0], positions[0], loss_target[0], e_g_in,
                     (pending_w0, token_w0, unpack0, [shard1] + rows_b0 + rows_b1), e_g_q_a, e_g_kv_a, e_sinks,
                     (shard1, unpack1), o_b_f, g_final, scatter1=scatter1, scatter0=scatter0)

    def in_projection(recv, ra, n, w, m, v, name):
        g = _sum8(recv, ra, name=name + "_grad_sum")[:n].reshape(n, 1, D_MODEL)
        w, m, v = [jnp.transpose(a, (2, 0, 1)) for a in (w, m, v)]
        return (g, *_adamw_columns(g, w, m, v, name=name + "_adamw"))

    o_in = in_projection(gr["recv1"], RA1, N_O_IN, o_w_in, m_o_w_in, v_o_w_in, "o_w_in")
    b1 = _adamw(gr["recv1"][:, RA1:], *rows_b1, name="adamw_late")

    small = _small_pack(gr["e_g_in"], gr["g_final"], gr["e_g_q_a"], gr["e_g_kv_a"], gr["e_sinks"], gr["o_b_f"], gr["loss"])
    small_all = _all_gather(small, name="small_all_gather")
    zero = jnp.zeros((), F32)
    w_small = _small_pack(e_g_in, g_final, e_g_q_a, e_g_kv_a, e_sinks, o_b_f, zero)
    m_small = _small_pack(m_e_g_in, m_g_final, m_e_g_q_a, m_e_g_kv_a, m_e_sinks, m_o_b_f, zero)
    v_small = _small_pack(v_e_g_in, v_g_final, v_e_g_q_a, v_e_g_kv_a, v_e_sinks, v_o_b_f, zero)
    smalls = _adamw(small_all, w_small, m_small, v_small, name="adamw_replicated")
    g_sm, d_sm, m_sm, v_sm = [_small_unpack(a) for a in smalls]
    loss = g_sm[6]

    sent0, recv0 = _peer_wait("exchange", *gr["pending0"], after=[o_in[1], b1[1], smalls[1]], name="grads0_wait")
    own = lax.dynamic_slice_in_dim(sent0, me, 1, axis=0)
    recv0 = lax.dynamic_update_slice_in_dim(recv0, own, me, axis=0)
    e_in = in_projection(recv0, RA0, N_E_IN, e_w_in, m_e_w_in, v_e_w_in, "e_w_in")
    b0 = _adamw(recv0[:, RA0:], *rows_b0, name="adamw_early")

    def sharded(k):
        q_up, kv_up = _unflat_b0(b0[k])
        o_out, e_out, o_g = _unflat_b1(b1[k])
        return jnp.transpose(e_in[k], (1, 2, 0)), q_up, kv_up, e_out, jnp.transpose(o_in[k], (1, 2, 0)), o_out, o_g

    g_sh, d_sh, m_sh, v_sh = [sharded(k) for k in range(4)]

    def leaves(sh, sm):
        return (sm[0], sh[0], sm[2], sh[1], sm[3], sh[2], sm[4], sh[3], sh[6], sh[4], sm[5], sh[5], sm[1])

    return (loss, gr["grad_x"][None], *leaves(g_sh, g_sm), *leaves(d_sh, d_sm), *leaves(m_sh, m_sm), *leaves(v_sh, v_sm))
```
